```python
import jax, jax.numpy as jnp
from jax import lax
import numpy as np

D_MODEL = 1024
BATCH = 8
SEQ = 8192
DEPTH = 1

N_GROUPS = 3
HEADS_PER_GROUP = 4
HEAD_DIM = 128
DILATED_CONFIGS = ((128, 1), (512, 4), (2048, 16))
ATTN_WIDTH = N_GROUPS * HEADS_PER_GROUP * HEAD_DIM
ATTN_OUT = HEADS_PER_GROUP * HEAD_DIM
CONV_WIDTH = D_MODEL
CONV_K = 3
D_FF = 2816
EPS = 1e-6
N_MOD = 9
IN_SIZES = (ATTN_WIDTH, ATTN_WIDTH, ATTN_WIDTH, CONV_WIDTH, CONV_WIDTH, CONV_WIDTH, D_MODEL, D_MODEL)
IN_WIDTH = sum(IN_SIZES)
IN_SPLITS = tuple(int(s) for s in np.cumsum(IN_SIZES)[:-1])

kernel_name = "hybrid_dilated_attn_shortconv_macaron_adaln"


def rms_norm(x, g):
    x32 = x.astype(jnp.float32)
    y = x32 * lax.rsqrt(jnp.mean(x32 * x32, axis=-1, keepdims=True) + EPS)
    return (y * g.astype(jnp.float32)).astype(x.dtype)


def modulate(h, shift, scale):
    return h * (1.0 + scale[:, None, :]) + shift[:, None, :]


def swiglu(h, w_gate, w_up, w_down):
    return (jax.nn.silu(h @ w_gate) * (h @ w_up)) @ w_down


def dilated_band_attention(q, k, v, window, dilation):
    B, S, H, E = q.shape
    nw = window // dilation
    L = S // dilation
    nb = -(-L // nw)
    Lp = nb * nw

    def to_blocks(t):
        t = t.reshape(B, L, dilation, H, E).transpose(0, 2, 3, 1, 4)
        t = jnp.pad(t, ((0, 0), (0, 0), (0, 0), (0, Lp - L), (0, 0)))
        return t.reshape(B, dilation, H, nb, nw, E)

    def with_prev(t):
        prev = jnp.pad(t[:, :, :, :-1], ((0, 0), (0, 0), (0, 0), (1, 0), (0, 0), (0, 0)))
        return jnp.concatenate([prev, t], axis=4)

    qb = to_blocks(q)
    kc = with_prev(to_blocks(k))
    vc = with_prev(to_blocks(v))
    s = jnp.einsum('brhnqe,brhnke->brhnqk', qb, kc) * (E ** -0.5)
    qi = jnp.arange(nw)
    kj = jnp.arange(2 * nw)
    rel = nw + qi[:, None] - kj[None, :]
    band = (rel >= 0) & (rel <= nw)
    key_pos = jnp.arange(nb)[:, None] * nw - nw + kj[None, :]
    mask = band[None, :, :] & (key_pos >= 0)[:, None, :]
    s = jnp.where(mask, s, -jnp.inf)
    m = jnp.max(s, axis=-1, keepdims=True)
    p = jnp.exp(s - m)
    denom = jnp.sum(p, axis=-1, keepdims=True)
    o = jnp.einsum('brhnqk,brhnke->brhnqe', p, vc) / denom
    lse = (m + jnp.log(denom))[..., 0]
    o = o.reshape(B, dilation, H, Lp, E)[:, :, :, :L].transpose(0, 3, 1, 2, 4).reshape(B, S, H, E)
    lse = lse.reshape(B, dilation, H, Lp)[:, :, :, :L].transpose(0, 3, 1, 2).reshape(B, S, H)
    return o, lse


def hybrid_mixer(h, w_in, q_norm, k_norm, conv_w, w_attn_branch, w_conv_branch, w_out):
    B, S, _ = h.shape
    proj = h @ w_in
    q, k, v, u, b_gate, c_gate, g_attn, g_conv = jnp.split(proj, IN_SPLITS, axis=-1)
    q = rms_norm(q.reshape(B, S, N_GROUPS, HEADS_PER_GROUP, HEAD_DIM), q_norm).astype(jnp.float32)
    k = rms_norm(k.reshape(B, S, N_GROUPS, HEADS_PER_GROUP, HEAD_DIM), k_norm).astype(jnp.float32)
    v = v.reshape(B, S, N_GROUPS, HEADS_PER_GROUP, HEAD_DIM).astype(jnp.float32)
    outs, lses = [], []
    for g, (window, dilation) in enumerate(DILATED_CONFIGS):
        o_g, lse_g = dilated_band_attention(q[:, :, g], k[:, :, g], v[:, :, g], window, dilation)
        outs.append(o_g)
        lses.append(lse_g)
    weights = jax.nn.softmax(jnp.stack(lses, axis=0), axis=0)
    o = jnp.einsum('gbsh,gbshe->bshe', weights, jnp.stack(outs, axis=0))
    y_attn = o.reshape(B, S, ATTN_OUT).astype(h.dtype) @ w_attn_branch
    xc = c_gate * u
    xp = jnp.pad(xc, ((0, 0), (CONV_K - 1, 0), (0, 0)))
    conv = xp[:, 0:S] * conv_w[0]
    for j in range(1, CONV_K):
        conv = conv + xp[:, j:j + S] * conv_w[j]
    y_conv = (b_gate * conv) @ w_conv_branch
    merged = jax.nn.sigmoid(g_attn) * y_attn + jax.nn.sigmoid(g_conv) * y_conv
    return merged @ w_out


def _fwd_setup_inputs(seed: int = 0) -> dict:
    key = jax.random.key(seed)
    ks = jax.random.split(key, 24)
    f32 = jnp.float32
    L, D = DEPTH, D_MODEL

    def nrm(k, shape, scale):
        return jax.random.normal(k, shape, f32) * scale

    return {
        'x': nrm(ks[0], (BATCH, SEQ, D), 1.0),
        'c': nrm(ks[1], (BATCH, D), 1.0),
        'w_ada': nrm(ks[2], (L, D, N_MOD * D), 0.5 * D ** -0.5),
        'b_ada': nrm(ks[3], (L, N_MOD * D), 0.01),
        'norm_ffn1': 1.0 + nrm(ks[4], (L, D), 0.01),
        'ffn1_w_gate': nrm(ks[5], (L, D, D_FF), D ** -0.5),
        'ffn1_w_up': nrm(ks[6], (L, D, D_FF), D ** -0.5),
        'ffn1_w_down': nrm(ks[7], (L, D_FF, D), D_FF ** -0.5),
        'norm_mix': 1.0 + nrm(ks[8], (L, D), 0.01),
        'w_in': nrm(ks[9], (L, D, IN_WIDTH), D ** -0.5),
        'q_norm': 1.0 + nrm(ks[10], (L, HEAD_DIM), 0.01),
        'k_norm': 1.0 + nrm(ks[11], (L, HEAD_DIM), 0.01),
        'conv_w': nrm(ks[12], (L, CONV_K, CONV_WIDTH), CONV_K ** -0.5),
        'w_attn_branch': nrm(ks[13], (L, ATTN_OUT, D), ATTN_OUT ** -0.5),
        'w_conv_branch': nrm(ks[14], (L, CONV_WIDTH, D), CONV_WIDTH ** -0.5),
        'w_out': nrm(ks[15], (L, D, D), D ** -0.5),
        'norm_ffn2': 1.0 + nrm(ks[16], (L, D), 0.01),
        'ffn2_w_gate': nrm(ks[17], (L, D, D_FF), D ** -0.5),
        'ffn2_w_up': nrm(ks[18], (L, D, D_FF), D ** -0.5),
        'ffn2_w_down': nrm(ks[19], (L, D_FF, D), D_FF ** -0.5),
    }


def _fwd_reference(x, c, w_ada, b_ada, norm_ffn1, ffn1_w_gate, ffn1_w_up, ffn1_w_down,
              norm_mix, w_in, q_norm, k_norm, conv_w, w_attn_branch, w_conv_branch, w_out,
              norm_ffn2, ffn2_w_gate, ffn2_w_up, ffn2_w_down):
    c_act = jax.nn.silu(c)
    for l in range(DEPTH):
        mod = c_act @ w_ada[l] + b_ada[l]
        sh1, sc1, gt1, sh2, sc2, gt2, sh3, sc3, gt3 = jnp.split(mod, N_MOD, axis=-1)
        h = modulate(rms_norm(x, norm_ffn1[l]), sh1, sc1)
        x = x + 0.5 * gt1[:, None, :] * swiglu(h, ffn1_w_gate[l], ffn1_w_up[l], ffn1_w_down[l])
        h = modulate(rms_norm(x, norm_mix[l]), sh2, sc2)
        x = x + gt2[:, None, :] * hybrid_mixer(h, w_in[l], q_norm[l], k_norm[l], conv_w[l],
                                               w_attn_branch[l], w_conv_branch[l], w_out[l])
        h = modulate(rms_norm(x, norm_ffn2[l]), sh3, sc3)
        x = x + 0.5 * gt3[:, None, :] * swiglu(h, ffn2_w_gate[l], ffn2_w_up[l], ffn2_w_down[l])
    return x


import jax as _jax
import jax.numpy as _jnp

TWIN_FORMAT = 'train_step'
FWD_PARAMS = ['x', 'c', 'w_ada', 'b_ada', 'norm_ffn1', 'ffn1_w_gate', 'ffn1_w_up', 'ffn1_w_down', 'norm_mix', 'w_in', 'q_norm', 'k_norm', 'conv_w', 'w_attn_branch', 'w_conv_branch', 'w_out', 'norm_ffn2', 'ffn2_w_gate', 'ffn2_w_up', 'ffn2_w_down']
TWIN_WEIGHTS = ['w_ada', 'b_ada', 'norm_ffn1', 'ffn1_w_gate', 'ffn1_w_up', 'ffn1_w_down', 'norm_mix', 'w_in', 'q_norm', 'k_norm', 'conv_w', 'w_attn_branch', 'w_conv_branch', 'w_out', 'norm_ffn2', 'ffn2_w_gate', 'ffn2_w_up', 'ffn2_w_down']
TWIN_DIFF_INPUT = 'x'
TWIN_INPUTS = ['x', 'c', 'w_ada', 'b_ada', 'norm_ffn1', 'ffn1_w_gate', 'ffn1_w_up', 'ffn1_w_down', 'norm_mix', 'w_in', 'q_norm', 'k_norm', 'conv_w', 'w_attn_branch', 'w_conv_branch', 'w_out', 'norm_ffn2', 'ffn2_w_gate', 'ffn2_w_up', 'ffn2_w_down', 'loss_target', 'm_w_ada', 'm_b_ada', 'm_norm_ffn1', 'm_ffn1_w_gate', 'm_ffn1_w_up', 'm_ffn1_w_down', 'm_norm_mix', 'm_w_in', 'm_q_norm', 'm_k_norm', 'm_conv_w', 'm_w_attn_branch', 'm_w_conv_branch', 'm_w_out', 'm_norm_ffn2', 'm_ffn2_w_gate', 'm_ffn2_w_up', 'm_ffn2_w_down', 'v_w_ada', 'v_b_ada', 'v_norm_ffn1', 'v_ffn1_w_gate', 'v_ffn1_w_up', 'v_ffn1_w_down', 'v_norm_mix', 'v_w_in', 'v_q_norm', 'v_k_norm', 'v_conv_w', 'v_w_attn_branch', 'v_w_conv_branch', 'v_w_out', 'v_norm_ffn2', 'v_ffn2_w_gate', 'v_ffn2_w_up', 'v_ffn2_w_down']
TWIN_OUTPUTS = ['loss', 'grad_x', 'grad_w_ada', 'grad_b_ada', 'grad_norm_ffn1', 'grad_ffn1_w_gate', 'grad_ffn1_w_up', 'grad_ffn1_w_down', 'grad_norm_mix', 'grad_w_in', 'grad_q_norm', 'grad_k_norm', 'grad_conv_w', 'grad_w_attn_branch', 'grad_w_conv_branch', 'grad_w_out', 'grad_norm_ffn2', 'grad_ffn2_w_gate', 'grad_ffn2_w_up', 'grad_ffn2_w_down', 'delta_w_ada', 'delta_b_ada', 'delta_norm_ffn1', 'delta_ffn1_w_gate', 'delta_ffn1_w_up', 'delta_ffn1_w_down', 'delta_norm_mix', 'delta_w_in', 'delta_q_norm', 'delta_k_norm', 'delta_conv_w', 'delta_w_attn_branch', 'delta_w_conv_branch', 'delta_w_out', 'delta_norm_ffn2', 'delta_ffn2_w_gate', 'delta_ffn2_w_up', 'delta_ffn2_w_down', 'new_m_w_ada', 'new_m_b_ada', 'new_m_norm_ffn1', 'new_m_ffn1_w_gate', 'new_m_ffn1_w_up', 'new_m_ffn1_w_down', 'new_m_norm_mix', 'new_m_w_in', 'new_m_q_norm', 'new_m_k_norm', 'new_m_conv_w', 'new_m_w_attn_branch', 'new_m_w_conv_branch', 'new_m_w_out', 'new_m_norm_ffn2', 'new_m_ffn2_w_gate', 'new_m_ffn2_w_up', 'new_m_ffn2_w_down', 'new_v_w_ada', 'new_v_b_ada', 'new_v_norm_ffn1', 'new_v_ffn1_w_gate', 'new_v_ffn1_w_up', 'new_v_ffn1_w_down', 'new_v_norm_mix', 'new_v_w_in', 'new_v_q_norm', 'new_v_k_norm', 'new_v_conv_w', 'new_v_w_attn_branch', 'new_v_w_conv_branch', 'new_v_w_out', 'new_v_norm_ffn2', 'new_v_ffn2_w_gate', 'new_v_ffn2_w_up', 'new_v_ffn2_w_down']
TWIN_LEAF_KINDS = {'loss': 'loss', 'grad_x': 'grad_x', 'grad_w_ada': 'grad_w', 'grad_b_ada': 'grad_w', 'grad_norm_ffn1': 'grad_w', 'grad_ffn1_w_gate': 'grad_w', 'grad_ffn1_w_up': 'grad_w', 'grad_ffn1_w_down': 'grad_w', 'grad_norm_mix': 'grad_w', 'grad_w_in': 'grad_w', 'grad_q_norm': 'grad_w', 'grad_k_norm': 'grad_w', 'grad_conv_w': 'grad_w', 'grad_w_attn_branch': 'grad_w', 'grad_w_conv_branch': 'grad_w', 'grad_w_out': 'grad_w', 'grad_norm_ffn2': 'grad_w', 'grad_ffn2_w_gate': 'grad_w', 'grad_ffn2_w_up': 'grad_w', 'grad_ffn2_w_down': 'grad_w', 'delta_w_ada': 'delta_w', 'delta_b_ada': 'delta_w', 'delta_norm_ffn1': 'delta_w', 'delta_ffn1_w_gate': 'delta_w', 'delta_ffn1_w_up': 'delta_w', 'delta_ffn1_w_down': 'delta_w', 'delta_norm_mix': 'delta_w', 'delta_w_in': 'delta_w', 'delta_q_norm': 'delta_w', 'delta_k_norm': 'delta_w', 'delta_conv_w': 'delta_w', 'delta_w_attn_branch': 'delta_w', 'delta_w_conv_branch': 'delta_w', 'delta_w_out': 'delta_w', 'delta_norm_ffn2': 'delta_w', 'delta_ffn2_w_gate': 'delta_w', 'delta_ffn2_w_up': 'delta_w', 'delta_ffn2_w_down': 'delta_w', 'new_m_w_ada': 'new_m', 'new_m_b_ada': 'new_m', 'new_m_norm_ffn1': 'new_m', 'new_m_ffn1_w_gate': 'new_m', 'new_m_ffn1_w_up': 'new_m', 'new_m_ffn1_w_down': 'new_m', 'new_m_norm_mix': 'new_m', 'new_m_w_in': 'new_m', 'new_m_q_norm': 'new_m', 'new_m_k_norm': 'new_m', 'new_m_conv_w': 'new_m', 'new_m_w_attn_branch': 'new_m', 'new_m_w_conv_branch': 'new_m', 'new_m_w_out': 'new_m', 'new_m_norm_ffn2': 'new_m', 'new_m_ffn2_w_gate': 'new_m', 'new_m_ffn2_w_up': 'new_m', 'new_m_ffn2_w_down': 'new_m', 'new_v_w_ada': 'new_v', 'new_v_b_ada': 'new_v', 'new_v_norm_ffn1': 'new_v', 'new_v_ffn1_w_gate': 'new_v', 'new_v_ffn1_w_up': 'new_v', 'new_v_ffn1_w_down': 'new_v', 'new_v_norm_mix': 'new_v', 'new_v_w_in': 'new_v', 'new_v_q_norm': 'new_v', 'new_v_k_norm': 'new_v', 'new_v_conv_w': 'new_v', 'new_v_w_attn_branch': 'new_v', 'new_v_w_conv_branch': 'new_v', 'new_v_w_out': 'new_v', 'new_v_norm_ffn2': 'new_v', 'new_v_ffn2_w_gate': 'new_v', 'new_v_ffn2_w_up': 'new_v', 'new_v_ffn2_w_down': 'new_v'}


def _forward(args):
    return _fwd_reference(*[args[k] for k in FWD_PARAMS])


def _output_shape():
    def fwd():
        inp = _fwd_setup_inputs(0)
        return _fwd_reference(*[inp[k] for k in FWD_PARAMS])
    out = _jax.eval_shape(fwd)
    return out.shape, out.dtype

N_MICROBATCH = 1
ADAM_LR = 0.001
ADAM_B1 = 0.9
ADAM_B2 = 0.999
ADAM_EPS = 1e-08
ADAM_WD = 0.01
ADAM_STEP = 10
PER_EXAMPLE_BATCH_AXIS = {'x': 0, 'c': 0, 'loss_target': 0}
SHARED_INPUTS = []
_WEIGHT_DTYPES = {'w_ada': _jnp.float32, 'b_ada': _jnp.float32, 'norm_ffn1': _jnp.float32, 'ffn1_w_gate': _jnp.float32, 'ffn1_w_up': _jnp.float32, 'ffn1_w_down': _jnp.float32, 'norm_mix': _jnp.float32, 'w_in': _jnp.float32, 'q_norm': _jnp.float32, 'k_norm': _jnp.float32, 'conv_w': _jnp.float32, 'w_attn_branch': _jnp.float32, 'w_conv_branch': _jnp.float32, 'w_out': _jnp.float32, 'norm_ffn2': _jnp.float32, 'ffn2_w_gate': _jnp.float32, 'ffn2_w_up': _jnp.float32, 'ffn2_w_down': _jnp.float32}
MOMENT_SCALE = {'w_ada': 1.041883e+00, 'b_ada': 3.047246e+00, 'norm_ffn1': 1.420031e+00, 'ffn1_w_gate': 3.217094e-02, 'ffn1_w_up': 3.034791e-02, 'ffn1_w_down': 4.932654e-02, 'norm_mix': 8.407146e+00, 'w_in': 7.700193e-02, 'q_norm': 8.988246e-02, 'k_norm': 8.977467e-02, 'conv_w': 1.619439e+00, 'w_attn_branch': 2.613361e-02, 'w_conv_branch': 1.195214e-01, 'w_out': 1.173814e-01, 'norm_ffn2': 1.557061e+00, 'ffn2_w_gate': 3.201575e-02, 'ffn2_w_up': 2.923341e-02, 'ffn2_w_down': 4.770360e-02}


def _to_microbatches(a, axis):
    t = _jnp.moveaxis(a, axis, 0)
    t = t.reshape((N_MICROBATCH, t.shape[0] // N_MICROBATCH) + t.shape[1:])
    return _jnp.moveaxis(t, 1, axis + 1)


def setup_inputs(seed: int = 0) -> dict:
    inp = _fwd_setup_inputs(seed)
    key = _jax.random.fold_in(_jax.random.key(seed), 7919)
    shape, _ = _output_shape()
    out = dict(inp)
    out["loss_target"] = _jax.random.normal(_jax.random.fold_in(key, 0), shape, _jnp.float32)
    for i, name in enumerate(TWIN_WEIGHTS):
        w = inp[name].astype(_jnp.float32)
        if MOMENT_SCALE is None:
            s = _jnp.sqrt(_jnp.mean(_jnp.square(w)) + 1e-30)
        else:
            s = MOMENT_SCALE[name]
        km, kv = _jax.random.split(_jax.random.fold_in(key, i + 1))
        out[name] = w
        out["m_" + name] = s * _jax.random.normal(km, w.shape, _jnp.float32)
        out["v_" + name] = (s * s) * _jax.random.uniform(kv, w.shape, _jnp.float32, 0.5, 1.5)
    if N_MICROBATCH > 1:
        for name, axis in PER_EXAMPLE_BATCH_AXIS.items():
            out[name] = _to_microbatches(out[name], axis)
    return {'x': out['x'], 'c': out['c'], 'w_ada': out['w_ada'], 'b_ada': out['b_ada'], 'norm_ffn1': out['norm_ffn1'], 'ffn1_w_gate': out['ffn1_w_gate'], 'ffn1_w_up': out['ffn1_w_up'], 'ffn1_w_down': out['ffn1_w_down'], 'norm_mix': out['norm_mix'], 'w_in': out['w_in'], 'q_norm': out['q_norm'], 'k_norm': out['k_norm'], 'conv_w': out['conv_w'], 'w_attn_branch': out['w_attn_branch'], 'w_conv_branch': out['w_conv_branch'], 'w_out': out['w_out'], 'norm_ffn2': out['norm_ffn2'], 'ffn2_w_gate': out['ffn2_w_gate'], 'ffn2_w_up': out['ffn2_w_up'], 'ffn2_w_down': out['ffn2_w_down'], 'loss_target': out['loss_target'], 'm_w_ada': out['m_w_ada'], 'm_b_ada': out['m_b_ada'], 'm_norm_ffn1': out['m_norm_ffn1'], 'm_ffn1_w_gate': out['m_ffn1_w_gate'], 'm_ffn1_w_up': out['m_ffn1_w_up'], 'm_ffn1_w_down': out['m_ffn1_w_down'], 'm_norm_mix': out['m_norm_mix'], 'm_w_in': out['m_w_in'], 'm_q_norm': out['m_q_norm'], 'm_k_norm': out['m_k_norm'], 'm_conv_w': out['m_conv_w'], 'm_w_attn_branch': out['m_w_attn_branch'], 'm_w_conv_branch': out['m_w_conv_branch'], 'm_w_out': out['m_w_out'], 'm_norm_ffn2': out['m_norm_ffn2'], 'm_ffn2_w_gate': out['m_ffn2_w_gate'], 'm_ffn2_w_up': out['m_ffn2_w_up'], 'm_ffn2_w_down': out['m_ffn2_w_down'], 'v_w_ada': out['v_w_ada'], 'v_b_ada': out['v_b_ada'], 'v_norm_ffn1': out['v_norm_ffn1'], 'v_ffn1_w_gate': out['v_ffn1_w_gate'], 'v_ffn1_w_up': out['v_ffn1_w_up'], 'v_ffn1_w_down': out['v_ffn1_w_down'], 'v_norm_mix': out['v_norm_mix'], 'v_w_in': out['v_w_in'], 'v_q_norm': out['v_q_norm'], 'v_k_norm': out['v_k_norm'], 'v_conv_w': out['v_conv_w'], 'v_w_attn_branch': out['v_w_attn_branch'], 'v_w_conv_branch': out['v_w_conv_branch'], 'v_w_out': out['v_w_out'], 'v_norm_ffn2': out['v_norm_ffn2'], 'v_ffn2_w_gate': out['v_ffn2_w_gate'], 'v_ffn2_w_up': out['v_ffn2_w_up'], 'v_ffn2_w_down': out['v_ffn2_w_down']}


def _loss(weights, diff, rest, loss_target):
    with _jax.named_scope("forward"):
        args = {**rest, TWIN_DIFF_INPUT: diff, **{k: w.astype(_WEIGHT_DTYPES[k]) for k, w in weights.items()}}
        y = _forward(args)
    with _jax.named_scope("loss_head"):
        err = _jnp.square(y.astype(_jnp.float32) - loss_target)
        return 0.5 * _jnp.sum(_jnp.mean(err, axis=-1)) if err.ndim else 0.5 * err


def _adamw(w, g, m, v):
    m = ADAM_B1 * m + (1.0 - ADAM_B1) * g
    v = ADAM_B2 * v + (1.0 - ADAM_B2) * _jnp.square(g)
    m_hat = m / (1.0 - ADAM_B1 ** ADAM_STEP)
    v_hat = v / (1.0 - ADAM_B2 ** ADAM_STEP)
    delta = -ADAM_LR * (m_hat / (_jnp.sqrt(v_hat) + ADAM_EPS) + ADAM_WD * w)
    return delta, m, v


def reference(x, c, w_ada, b_ada, norm_ffn1, ffn1_w_gate, ffn1_w_up, ffn1_w_down, norm_mix, w_in, q_norm, k_norm, conv_w, w_attn_branch, w_conv_branch, w_out, norm_ffn2, ffn2_w_gate, ffn2_w_up, ffn2_w_down, loss_target, m_w_ada, m_b_ada, m_norm_ffn1, m_ffn1_w_gate, m_ffn1_w_up, m_ffn1_w_down, m_norm_mix, m_w_in, m_q_norm, m_k_norm, m_conv_w, m_w_attn_branch, m_w_conv_branch, m_w_out, m_norm_ffn2, m_ffn2_w_gate, m_ffn2_w_up, m_ffn2_w_down, v_w_ada, v_b_ada, v_norm_ffn1, v_ffn1_w_gate, v_ffn1_w_up, v_ffn1_w_down, v_norm_mix, v_w_in, v_q_norm, v_k_norm, v_conv_w, v_w_attn_branch, v_w_conv_branch, v_w_out, v_norm_ffn2, v_ffn2_w_gate, v_ffn2_w_up, v_ffn2_w_down):
    given = dict(x=x, c=c, w_ada=w_ada, b_ada=b_ada, norm_ffn1=norm_ffn1, ffn1_w_gate=ffn1_w_gate, ffn1_w_up=ffn1_w_up, ffn1_w_down=ffn1_w_down, norm_mix=norm_mix, w_in=w_in, q_norm=q_norm, k_norm=k_norm, conv_w=conv_w, w_attn_branch=w_attn_branch, w_conv_branch=w_conv_branch, w_out=w_out, norm_ffn2=norm_ffn2, ffn2_w_gate=ffn2_w_gate, ffn2_w_up=ffn2_w_up, ffn2_w_down=ffn2_w_down, loss_target=loss_target, m_w_ada=m_w_ada, m_b_ada=m_b_ada, m_norm_ffn1=m_norm_ffn1, m_ffn1_w_gate=m_ffn1_w_gate, m_ffn1_w_up=m_ffn1_w_up, m_ffn1_w_down=m_ffn1_w_down, m_norm_mix=m_norm_mix, m_w_in=m_w_in, m_q_norm=m_q_norm, m_k_norm=m_k_norm, m_conv_w=m_conv_w, m_w_attn_branch=m_w_attn_branch, m_w_conv_branch=m_w_conv_branch, m_w_out=m_w_out, m_norm_ffn2=m_norm_ffn2, m_ffn2_w_gate=m_ffn2_w_gate, m_ffn2_w_up=m_ffn2_w_up, m_ffn2_w_down=m_ffn2_w_down, v_w_ada=v_w_ada, v_b_ada=v_b_ada, v_norm_ffn1=v_norm_ffn1, v_ffn1_w_gate=v_ffn1_w_gate, v_ffn1_w_up=v_ffn1_w_up, v_ffn1_w_down=v_ffn1_w_down, v_norm_mix=v_norm_mix, v_w_in=v_w_in, v_q_norm=v_q_norm, v_k_norm=v_k_norm, v_conv_w=v_conv_w, v_w_attn_branch=v_w_attn_branch, v_w_conv_branch=v_w_conv_branch, v_w_out=v_w_out, v_norm_ffn2=v_norm_ffn2, v_ffn2_w_gate=v_ffn2_w_gate, v_ffn2_w_up=v_ffn2_w_up, v_ffn2_w_down=v_ffn2_w_down)
    weights = {n: given[n] for n in TWIN_WEIGHTS}
    shared = {n: given[n] for n in SHARED_INPUTS}
    per_example = {n: given[n] for n in ['x', 'c']}
    grad_fn = _jax.value_and_grad(_loss, argnums=(0, 1))

    def one_microbatch(ex, loss_target):
        ex = dict(ex)
        diff = ex.pop(TWIN_DIFF_INPUT)
        return grad_fn(weights, diff, {**shared, **ex}, loss_target)

    if N_MICROBATCH == 1:
        loss, (grad_w, grad_x) = one_microbatch(per_example, given["loss_target"])
    else:
        def body(carry, xs):
            loss_sum, grad_sum = carry
            l_k, (gw_k, gx_k) = one_microbatch(xs[0], xs[1])
            with _jax.named_scope("update"):
                return (loss_sum + l_k, _jax.tree.map(_jnp.add, grad_sum, gw_k)), gx_k

        init = (_jnp.zeros((), _jnp.float32), _jax.tree.map(_jnp.zeros_like, weights))
        (loss, grad_w), grad_x = _jax.lax.scan(body, init, (per_example, given["loss_target"]))
    with _jax.named_scope("update"):
        delta_w, new_m, new_v = {}, {}, {}
        for n in TWIN_WEIGHTS:
            delta_w[n], new_m[n], new_v[n] = _adamw(weights[n], grad_w[n], given["m_" + n], given["v_" + n])
    return (loss, grad_x, *[grad_w[n] for n in TWIN_WEIGHTS], *[delta_w[n] for n in TWIN_WEIGHTS],
            *[new_m[n] for n in TWIN_WEIGHTS], *[new_v[n] for n in TWIN_WEIGHTS])
```

```python
import jax
import jax.numpy as jnp
from jax import lax
from jax.experimental import pallas as pl
from jax.experimental.pallas import tpu as pltpu

F32 = jnp.float32
BF16 = jnp.bfloat16
MESH = pl.DeviceIdType.MESH
ANY = pl.BlockSpec(memory_space=pl.ANY)

NORM_EPS = 1e-6
HEAD_DIM = 128
N_GROUPS = 3
HEADS = 4
DILATIONS = (1, 4, 16)
ATTN_BLOCK = 128
SLAB = ATTN_BLOCK * max(DILATIONS)
QKV = N_GROUPS * HEADS * HEAD_DIM
ATTN_SCALE = HEAD_DIM ** -0.5
NEG = -1e30
N_CHIPS = 4

ADAM_LR = 0.001
ADAM_B1 = 0.9
ADAM_B2 = 0.999
ADAM_EPS = 1e-08
ADAM_WD = 0.01
ADAM_STEP = 10

VMEM_LIMIT_BYTES = 56 * 1024 * 1024
TOKEN_TILE = 512
MIX_TILE = 256


def _params(n_axes=0):
    return pltpu.CompilerParams(
        dimension_semantics=("arbitrary",) * n_axes if n_axes else None,
        vmem_limit_bytes=VMEM_LIMIT_BYTES)


def _dot(a, b):
    return jnp.dot(a, b, preferred_element_type=F32)


def _dot_nt(a, b):
    return lax.dot_general(a, b, (((1,), (1,)), ((), ())), preferred_element_type=F32)


def _dot_tn(a, b):
    return lax.dot_general(a, b, (((0,), (0,)), ((), ())), preferred_element_type=F32)


def _sigmoid(x):
    return 1.0 / (1.0 + jnp.exp(-x))


def _place():
    return lax.axis_index("x"), lax.axis_index("y"), lax.axis_index("c")


def _allgather8(block, name):
    m_per, n = block.shape

    def body(x_ref, out_ref, send_sems, recv_sems, local_sem):
        x, y, c = _place()
        me, sibling = (x, y, c), (x, y, 1 - c)
        chips = [(1 - x, y), (x, 1 - y), (1 - x, 1 - y)]

        def rows(px, py, pc):
            return out_ref.at[pl.ds((4 * px + 2 * py + pc) * m_per, m_per), :]

        def copy(k, blk, to, src=None):
            return pltpu.make_async_remote_copy(
                src_ref=rows(*blk) if src is None else src, dst_ref=rows(*blk),
                send_sem=send_sems.at[k], recv_sem=recv_sems.at[k],
                device_id=to, device_id_type=MESH)

        mine = pltpu.make_async_copy(x_ref, rows(*me), local_sem)
        mine.start()
        first = [copy(0, me, sibling, src=x_ref)]
        first += [copy(1 + j, me, (*chip, c), src=x_ref) for j, chip in enumerate(chips)]
        for cp in first:
            cp.start()
        passed = [copy(4 + j, (*chip, c), sibling) for j, chip in enumerate(chips)]
        for j, chip in enumerate(chips):
            copy(1 + j, (*chip, c), me).wait_recv()
            passed[j].start()
        copy(0, sibling, me).wait_recv()
        for j, chip in enumerate(chips):
            copy(4 + j, (*chip, 1 - c), me).wait_recv()
        for cp in first + passed:
            cp.wait_send()
        mine.wait()

    return pl.pallas_call(
        body, name=name,
        out_shape=jax.ShapeDtypeStruct((8 * m_per, n), block.dtype),
        in_specs=[pl.BlockSpec(memory_space=pltpu.VMEM)],
        out_specs=pl.BlockSpec(memory_space=pltpu.VMEM),
        scratch_shapes=[pltpu.SemaphoreType.DMA((7,)), pltpu.SemaphoreType.DMA((7,)),
                        pltpu.SemaphoreType.DMA],
        compiler_params=_params(),
    )(block)


def _gather_weights(shards, name):
    n_arr = len(shards)

    def body(*refs):
        srcs, outs = refs[:n_arr], refs[n_arr:2 * n_arr]
        send_sems, recv_sems, local_sems = refs[2 * n_arr:]
        x, y, c = _place()
        me_dev, sibling = (x, y, c), (x, y, 1 - c)
        chips = [(1 - x, y), (x, 1 - y), (1 - x, 1 - y)]
        me = 2 * x + y

        def copy(k, slot, chip_idx, half_sel, to, from_shard=False):
            half = srcs[k].shape[0] // 2
            rows = pl.ds(half_sel * half, half)
            dst = outs[k].at[chip_idx, rows]
            return pltpu.make_async_remote_copy(
                src_ref=srcs[k].at[rows] if from_shard else dst, dst_ref=dst,
                send_sem=send_sems.at[6 * k + slot], recv_sem=recv_sems.at[6 * k + slot],
                device_id=to, device_id_type=MESH)

        own = [pltpu.make_async_copy(srcs[k], outs[k].at[me], local_sems.at[k]) for k in range(n_arr)]
        for cp in own:
            cp.start()
        sent = []
        for k in range(n_arr):
            for j, chip in enumerate(chips):
                sent.append(copy(k, j, me, c, (*chip, c), from_shard=True))
                sent[-1].start()
        for k in range(n_arr):
            for j, chip in enumerate(chips):
                chip_idx = 2 * chip[0] + chip[1]
                copy(k, j, chip_idx, c, me_dev).wait_recv()
                sent.append(copy(k, 3 + j, chip_idx, c, sibling))
                sent[-1].start()
        for k in range(n_arr):
            for j, chip in enumerate(chips):
                copy(k, 3 + j, 2 * chip[0] + chip[1], 1 - c, me_dev).wait_recv()
        for cp in sent:
            cp.wait_send()
        for cp in own:
            cp.wait()

    return pl.pallas_call(
        body, name=name,
        out_shape=[jax.ShapeDtypeStruct((N_CHIPS,) + s.shape, s.dtype) for s in shards],
        in_specs=[ANY] * n_arr, out_specs=[ANY] * n_arr,
        scratch_shapes=[pltpu.SemaphoreType.DMA((6 * n_arr,)), pltpu.SemaphoreType.DMA((6 * n_arr,)),
                        pltpu.SemaphoreType.DMA((n_arr,))],
        compiler_params=_params(),
    )(*shards)


def _rs_pair_exchange(grads, name):
    n_arr = len(grads)

    def body(*refs):
        srcs, outs = refs[:n_arr], refs[n_arr:2 * n_arr]
        send_sems, recv_sems = refs[2 * n_arr:]
        x, y, c = _place()
        cps = []
        for k in range(n_arr):
            half = srcs[k].shape[1] // 2
            cps.append(pltpu.make_async_remote_copy(
                src_ref=srcs[k].at[:, pl.ds((1 - c) * half, half)], dst_ref=outs[k],
                send_sem=send_sems.at[k], recv_sem=recv_sems.at[k],
                device_id=(x, y, 1 - c), device_id_type=MESH))
            cps[-1].start()
        for cp in cps:
            cp.wait_recv()
        for cp in cps:
            cp.wait_send()

    return pl.pallas_call(
        body, name=name,
        out_shape=[jax.ShapeDtypeStruct((g.shape[0], g.shape[1] // 2, g.shape[2]), g.dtype) for g in grads],
        in_specs=[ANY] * n_arr, out_specs=[ANY] * n_arr,
        scratch_shapes=[pltpu.SemaphoreType.DMA((n_arr,)), pltpu.SemaphoreType.DMA((n_arr,))],
        compiler_params=_params(),
    )(*grads)


def _rs_chip_exchange(sums, name):
    n_arr = len(sums)

    def body(*refs):
        srcs, outs = refs[:n_arr], refs[n_arr:2 * n_arr]
        send_sems, recv_sems = refs[2 * n_arr:]
        x, y, c = _place()
        chips = [(1 - x, y), (x, 1 - y), (1 - x, 1 - y)]
        cps = []
        for k in range(n_arr):
            for j, chip in enumerate(chips):
                cps.append(pltpu.make_async_remote_copy(
                    src_ref=srcs[k].at[2 * chip[0] + chip[1]], dst_ref=outs[k].at[j],
                    send_sem=send_sems.at[3 * k + j], recv_sem=recv_sems.at[3 * k + j],
                    device_id=(*chip, c), device_id_type=MESH))
                cps[-1].start()
        for cp in cps:
            cp.wait_recv()
        for cp in cps:
            cp.wait_send()

    return pl.pallas_call(
        body, name=name,
        out_shape=[jax.ShapeDtypeStruct((3,) + s.shape[1:], s.dtype) for s in sums],
        in_specs=[ANY] * n_arr, out_specs=[ANY] * n_arr,
        scratch_shapes=[pltpu.SemaphoreType.DMA((3 * n_arr,)), pltpu.SemaphoreType.DMA((3 * n_arr,))],
        compiler_params=_params(),
    )(*sums)


def _rs_share(totals, name):
    n_arr = len(totals)

    def body(*refs):
        srcs, outs = refs[:n_arr], refs[n_arr:2 * n_arr]
        send_sems, recv_sems, local_sems = refs[2 * n_arr:]
        x, y, c = _place()
        own, cps = [], []
        for k in range(n_arr):
            half = srcs[k].shape[0]
            mine = outs[k].at[pl.ds(c * half, half)]
            own.append(pltpu.make_async_copy(srcs[k], mine, local_sems.at[k]))
            own[-1].start()
            cps.append(pltpu.make_async_remote_copy(
                src_ref=srcs[k], dst_ref=mine, send_sem=send_sems.at[k], recv_sem=recv_sems.at[k],
                device_id=(x, y, 1 - c), device_id_type=MESH))
            cps[-1].start()
        for k in range(n_arr):
            half = srcs[k].shape[0]
            theirs = outs[k].at[pl.ds((1 - c) * half, half)]
            pltpu.make_async_remote_copy(
                src_ref=srcs[k], dst_ref=theirs, send_sem=send_sems.at[k], recv_sem=recv_sems.at[k],
                device_id=(x, y, 1 - c), device_id_type=MESH).wait_recv()
        for cp in cps:
            cp.wait_send()
        for cp in own:
            cp.wait()

    return pl.pallas_call(
        body, name=name,
        out_shape=[jax.ShapeDtypeStruct((2 * t.shape[0], t.shape[1]), t.dtype) for t in totals],
        in_specs=[ANY] * n_arr, out_specs=[ANY] * n_arr,
        scratch_shapes=[pltpu.SemaphoreType.DMA((n_arr,)), pltpu.SemaphoreType.DMA((n_arr,)),
                        pltpu.SemaphoreType.DMA((n_arr,))],
        compiler_params=_params(),
    )(*totals)


def _pair_add(grad, recv, c_idx, name):
    n, r, cols = grad.shape
    half = r // 2
    rows = half // 2

    def body(_, g_ref, r_ref, o_ref):
        o_ref[...] = (g_ref[...].astype(F32) + r_ref[...].astype(F32)).astype(o_ref.dtype)

    return pl.pallas_call(
        body, name=name,
        grid_spec=pltpu.PrefetchScalarGridSpec(
            num_scalar_prefetch=1, grid=(n, 2),
            in_specs=[pl.BlockSpec((None, None, rows, cols), lambda s, i, ci: (s, ci[0], i, 0)),
                      pl.BlockSpec((None, rows, cols), lambda s, i, ci: (s, i, 0))],
            out_specs=pl.BlockSpec((None, rows, cols), lambda s, i, ci: (s, i, 0))),
        out_shape=jax.ShapeDtypeStruct((n, half, cols), BF16),
        compiler_params=_params(2),
    )(c_idx, grad.reshape(n, 2, half, cols), recv)


def _chip_add(sums, recv, chip_idx, name):
    _, half, cols = sums.shape
    rows = half // 2

    def body(_, s_ref, r0_ref, r1_ref, r2_ref, o_ref):
        o_ref[...] = ((s_ref[...].astype(F32) + r0_ref[...].astype(F32))
                      + r1_ref[...].astype(F32)) + r2_ref[...].astype(F32)

    def recv_spec(j):
        return pl.BlockSpec((None, rows, cols), lambda i, ci: (j, i, 0))

    return pl.pallas_call(
        body, name=name,
        grid_spec=pltpu.PrefetchScalarGridSpec(
            num_scalar_prefetch=1, grid=(2,),
            in_specs=[pl.BlockSpec((None, rows, cols), lambda i, ci: (ci[0], i, 0)),
                      recv_spec(0), recv_spec(1), recv_spec(2)],
            out_specs=pl.BlockSpec((rows, cols), lambda i, ci: (i, 0))),
        out_shape=jax.ShapeDtypeStruct((half, cols), F32),
        compiler_params=_params(1),
    )(chip_idx, sums, recv, recv, recv)


def _rms(x):
    return lax.rsqrt(jnp.mean(x * x, axis=-1, keepdims=True) + NORM_EPS)


def _norm_mod(x, mod, name):
    s_len, d = x.shape
    tm = TOKEN_TILE

    def body(x_ref, mod_ref, h_ref):
        xv = x_ref[...]
        n = (xv * _rms(xv)) * mod_ref[3:4, :]
        h_ref[...] = (n * (1.0 + mod_ref[1:2, :]) + mod_ref[0:1, :]).astype(BF16)

    return pl.pallas_call(
        body, name=name, grid=(s_len // tm,),
        in_specs=[pl.BlockSpec((tm, d), lambda i: (i, 0)), pl.BlockSpec((8, d), lambda i: (0, 0))],
        out_specs=pl.BlockSpec((tm, d), lambda i: (i, 0)),
        out_shape=jax.ShapeDtypeStruct((s_len, d), BF16),
        compiler_params=_params(1),
    )(x, mod)


def _norm_bwd(dh, x, mod, dxo, y_raw, coef, name):
    s_len, d = x.shape
    tm = TOKEN_TILE

    def body(dh_ref, x_ref, mod_ref, dxo_ref, y_ref, dx_ref, st_ref):
        @pl.when(pl.program_id(0) == 0)
        def _():
            st_ref[...] = jnp.zeros_like(st_ref)

        xv, dhv, dxov = x_ref[...], dh_ref[...], dxo_ref[...]
        r = _rms(xv)
        xh = xv * r
        gain, scale = mod_ref[3:4, :], mod_ref[1:2, :]
        dn = dhv * (1.0 + scale)
        dxh = dn * gain
        dx_ref[...] = dxov + r * (dxh - xh * jnp.mean(dxh * xh, axis=-1, keepdims=True))
        st_ref[0:1, :] += jnp.sum(dhv, axis=0, keepdims=True)
        st_ref[1:2, :] += jnp.sum(dhv * (xh * gain), axis=0, keepdims=True)
        st_ref[2:3, :] += coef * jnp.sum(y_ref[...] * dxov, axis=0, keepdims=True)
        st_ref[3:4, :] += jnp.sum(dn * xh, axis=0, keepdims=True)

    tile = pl.BlockSpec((tm, d), lambda i: (i, 0))
    small = pl.BlockSpec((8, d), lambda i: (0, 0))
    return pl.pallas_call(
        body, name=name, grid=(s_len // tm,),
        in_specs=[tile, tile, small, tile, tile],
        out_specs=[tile, small],
        out_shape=[jax.ShapeDtypeStruct((s_len, d), F32), jax.ShapeDtypeStruct((8, d), F32)],
        compiler_params=_params(1),
    )(dh, x, mod, dxo, y_raw)


def _loss_grad(y, target, name):
    s_len, d = y.shape
    tm = TOKEN_TILE

    def body(y_ref, t_ref, dy_ref, part_ref):
        @pl.when(pl.program_id(0) == 0)
        def _():
            part_ref[...] = jnp.zeros_like(part_ref)

        err = y_ref[...] - t_ref[...]
        dy_ref[...] = err * (1.0 / d)
        sq = err * err
        part_ref[...] += jnp.sum(sq.reshape(tm // 8, 8, d), axis=0)

    tile = pl.BlockSpec((tm, d), lambda i: (i, 0))
    return pl.pallas_call(
        body, name=name, grid=(s_len // tm,),
        in_specs=[tile, tile],
        out_specs=[tile, pl.BlockSpec((8, d), lambda i: (0, 0))],
        out_shape=[jax.ShapeDtypeStruct((s_len, d), F32), jax.ShapeDtypeStruct((8, d), F32)],
        compiler_params=_params(1),
    )(y, target)


def _adamw_math(w, g, m, v):
    m = ADAM_B1 * m + (1.0 - ADAM_B1) * g
    v = ADAM_B2 * v + (1.0 - ADAM_B2) * (g * g)
    m_hat = m / (1.0 - ADAM_B1 ** ADAM_STEP)
    v_hat = v / (1.0 - ADAM_B2 ** ADAM_STEP)
    delta = -ADAM_LR * (m_hat / (jnp.sqrt(v_hat) + ADAM_EPS) + ADAM_WD * w)
    return delta, m, v


def _adamw(w, g, m, v, name):
    r, cols = w.shape
    tr = r // 8 if r % 64 == 0 else r

    def body(w_ref, g_ref, m_ref, v_ref, d_ref, nm_ref, nv_ref):
        d_ref[...], nm_ref[...], nv_ref[...] = _adamw_math(w_ref[...], g_ref[...], m_ref[...], v_ref[...])

    tile = pl.BlockSpec((tr, cols), lambda i: (i, 0))
    shape = jax.ShapeDtypeStruct((r, cols), F32)
    return pl.pallas_call(
        body, name=name, grid=(r // tr,),
        in_specs=[tile] * 4, out_specs=[tile] * 3, out_shape=[shape] * 3,
        compiler_params=_params(1),
    )(w, g, m, v)


def _matmul_nn(a, w, tn, out_dtype, name):
    s_len, k = a.shape
    n = w.shape[1]
    tm = TOKEN_TILE

    def body(a_ref, w_ref, o_ref):
        o_ref[...] = _dot(a_ref[...], w_ref[...]).astype(o_ref.dtype)

    return pl.pallas_call(
        body, name=name, grid=(n // tn, s_len // tm),
        in_specs=[pl.BlockSpec((tm, k), lambda j, i: (i, 0)), pl.BlockSpec((k, tn), lambda j, i: (0, j))],
        out_specs=pl.BlockSpec((tm, tn), lambda j, i: (i, j)),
        out_shape=jax.ShapeDtypeStruct((s_len, n), out_dtype),
        compiler_params=_params(2),
    )(a, w)


def _matmul_nt(a, w, tn, add, name):
    s_len, n = a.shape
    k = w.shape[0]
    tm = TOKEN_TILE
    steps = n // tn

    def body(*refs):
        if add is None:
            a_ref, w_ref, o_ref, acc_ref = refs
        else:
            a_ref, w_ref, add_ref, o_ref, acc_ref = refs
        j = pl.program_id(1)

        @pl.when(j == 0)
        def _():
            acc_ref[...] = jnp.zeros_like(acc_ref) if add is None else add_ref[...]

        acc_ref[...] += _dot_nt(a_ref[...], w_ref[...])

        @pl.when(j == steps - 1)
        def _():
            o_ref[...] = acc_ref[...]

    tile = pl.BlockSpec((tm, k), lambda i, j: (i, 0))
    in_specs = [pl.BlockSpec((tm, tn), lambda i, j: (i, j)), pl.BlockSpec((k, tn), lambda i, j: (0, j))]
    operands = [a, w]
    if add is not None:
        in_specs.append(tile)
        operands.append(add)
    return pl.pallas_call(
        body, name=name, grid=(s_len // tm, steps),
        in_specs=in_specs, out_specs=tile,
        out_shape=jax.ShapeDtypeStruct((s_len, k), F32),
        scratch_shapes=[pltpu.VMEM((tm, k), F32)],
        compiler_params=_params(2),
    )(*operands)


def _wgrad(x, y, x_spec, y_spec, out_shape, out_spec, acc_shape, n_chunks, name):
    s_len = x.shape[-2]
    ts = TOKEN_TILE
    steps = s_len // ts

    def body(x_ref, y_ref, o_ref, acc_ref):
        s = pl.program_id(1)

        @pl.when(s == 0)
        def _():
            acc_ref[...] = jnp.zeros_like(acc_ref)

        acc_ref[...] += _dot_tn(x_ref[...], y_ref[...])

        @pl.when(s == steps - 1)
        def _():
            o_ref[...] = acc_ref[...].astype(o_ref.dtype)

    return pl.pallas_call(
        body, name=name, grid=(n_chunks, steps),
        in_specs=[x_spec(ts), y_spec(ts)], out_specs=out_spec,
        out_shape=jax.ShapeDtypeStruct(out_shape, BF16),
        scratch_shapes=[pltpu.VMEM(acc_shape, F32)],
        compiler_params=_params(2),
    )(x, y)


def _ffn_fwd(x, h, mod, w_gate, w_up, w_down, name):
    s_len, d = x.shape
    n_chunks, _, fs = w_gate.shape
    tm = TOKEN_TILE

    def body(x_ref, h_ref, mod_ref, wg_ref, wu_ref, wd_ref, xo_ref, g_ref, u_ref, y_ref, acc_ref):
        j = pl.program_id(1)

        @pl.when(j == 0)
        def _():
            acc_ref[...] = jnp.zeros_like(acc_ref)

        hv = h_ref[...]
        g = _dot(hv, wg_ref[...])
        u = _dot(hv, wu_ref[...])
        g_ref[...] = g.astype(BF16)
        u_ref[...] = u.astype(BF16)
        act = (g * _sigmoid(g)) * u
        acc_ref[...] += _dot(act.astype(BF16), wd_ref[...])

        @pl.when(j == n_chunks - 1)
        def _():
            yv = acc_ref[...]
            y_ref[...] = yv
            xo_ref[...] = x_ref[...] + 0.5 * mod_ref[2:3, :] * yv

    tile = pl.BlockSpec((tm, d), lambda i, j: (i, 0))
    hid = pl.BlockSpec((None, tm, fs), lambda i, j: (j, i, 0))
    w_in_spec = pl.BlockSpec((None, d, fs), lambda i, j: (j, 0, 0))
    return pl.pallas_call(
        body, name=name, grid=(s_len // tm, n_chunks),
        in_specs=[tile, tile, pl.BlockSpec((8, d), lambda i, j: (0, 0)), w_in_spec, w_in_spec,
                  pl.BlockSpec((None, fs, d), lambda i, j: (j, 0, 0))],
        out_specs=[tile, hid, hid, tile],
        out_shape=[jax.ShapeDtypeStruct((s_len, d), F32),
                   jax.ShapeDtypeStruct((n_chunks, s_len, fs), BF16),
                   jax.ShapeDtypeStruct((n_chunks, s_len, fs), BF16),
                   jax.ShapeDtypeStruct((s_len, d), F32)],
        scratch_shapes=[pltpu.VMEM((tm, d), F32)],
        compiler_params=_params(2),
    )(x, h, mod, w_gate, w_up, w_down)


def _ffn_bwd(dxo, mod, g_pre, u_pre, w_gate, w_up, w_down, name):
    s_len, d = dxo.shape
    n_chunks, _, fs = w_gate.shape
    tm = TOKEN_TILE

    def body(dxo_ref, mod_ref, g_ref, u_ref, wg_ref, wu_ref, wd_ref, dh_ref, dg_ref, du_ref, a_ref, dy_ref, acc_ref):
        j = pl.program_id(1)

        @pl.when(j == 0)
        def _():
            dy_ref[...] = (0.5 * mod_ref[2:3, :] * dxo_ref[...]).astype(BF16)
            acc_ref[...] = jnp.zeros_like(acc_ref)

        da = _dot_nt(dy_ref[...], wd_ref[...])
        g = g_ref[...].astype(F32)
        u = u_ref[...].astype(F32)
        sg = _sigmoid(g)
        silu = g * sg
        dg = (da * u * (sg * (1.0 + g * (1.0 - sg)))).astype(BF16)
        du = (da * silu).astype(BF16)
        dg_ref[...] = dg
        du_ref[...] = du
        a_ref[...] = (silu * u).astype(BF16)
        acc_ref[...] += _dot_nt(dg, wg_ref[...]) + _dot_nt(du, wu_ref[...])

        @pl.when(j == n_chunks - 1)
        def _():
            dh_ref[...] = acc_ref[...]

    tile = pl.BlockSpec((tm, d), lambda i, j: (i, 0))
    hid = pl.BlockSpec((None, tm, fs), lambda i, j: (j, i, 0))
    w_in_spec = pl.BlockSpec((None, d, fs), lambda i, j: (j, 0, 0))
    hid_shape = jax.ShapeDtypeStruct((n_chunks, s_len, fs), BF16)
    return pl.pallas_call(
        body, name=name, grid=(s_len // tm, n_chunks),
        in_specs=[tile, pl.BlockSpec((8, d), lambda i, j: (0, 0)), hid, hid, w_in_spec, w_in_spec,
                  pl.BlockSpec((None, fs, d), lambda i, j: (j, 0, 0))],
        out_specs=[tile, hid, hid, hid, tile],
        out_shape=[jax.ShapeDtypeStruct((s_len, d), F32), hid_shape, hid_shape, hid_shape,
                   jax.ShapeDtypeStruct((s_len, d), BF16)],
        scratch_shapes=[pltpu.VMEM((tm, d), F32)],
        compiler_params=_params(2),
    )(dxo, mod, g_pre, u_pre, w_gate, w_up, w_down)


def _ffn_wgrads(h, dg, du, act, dy, tag):
    n_chunks, s_len, fs = dg.shape
    d = h.shape[1]
    tok = lambda ts: pl.BlockSpec((ts, d), lambda c, s: (s, 0))
    hid = lambda ts: pl.BlockSpec((None, ts, fs), lambda c, s: (c, s, 0))
    d_up = pl.BlockSpec((None, d, fs), lambda c, s: (c, 0, 0))
    d_down = pl.BlockSpec((None, fs, d), lambda c, s: (c, 0, 0))
    dwg = _wgrad(h, dg, tok, hid, (n_chunks, d, fs), d_up, (d, fs), n_chunks, tag + "_dwg")
    dwu = _wgrad(h, du, tok, hid, (n_chunks, d, fs), d_up, (d, fs), n_chunks, tag + "_dwu")
    dwd = _wgrad(act, dy, hid, tok, (n_chunks, fs, d), d_down, (fs, d), n_chunks, tag + "_dwd")
    return dwg, dwu, dwd


def _band_mask(first):
    qi = lax.broadcasted_iota(jnp.int32, (ATTN_BLOCK, 2 * ATTN_BLOCK), 0)
    kj = lax.broadcasted_iota(jnp.int32, (ATTN_BLOCK, 2 * ATTN_BLOCK), 1)
    lo = jnp.where(first, jnp.maximum(qi, ATTN_BLOCK), qi)
    return (kj >= lo) & (kj <= qi + ATTN_BLOCK)


def _rows(base, count, stride):
    return pl.ds(base, count) if stride == 1 else pl.ds(base, count, stride=stride)


def _load_qkv(cur_ref, prev_ref, qn_ref, kn_ref, qs, kb, vb, n):
    e = HEAD_DIM
    q = cur_ref[:, 0:e]
    qs[...] = (q * _rms(q)) * qn_ref[...]
    k = cur_ref[:, e:2 * e]
    kb[SLAB:2 * SLAB, :] = (k * _rms(k)) * kn_ref[...]
    vb[SLAB:2 * SLAB, :] = cur_ref[:, 2 * e:3 * e]

    @pl.when(n > 0)
    def _():
        kp = prev_ref[:, e:2 * e]
        kb[0:SLAB, :] = (kp * _rms(kp)) * kn_ref[...]
        vb[0:SLAB, :] = prev_ref[:, 2 * e:3 * e]

    @pl.when(n == 0)
    def _():
        kb[0:SLAB, :] = jnp.zeros((SLAB, e), F32)
        vb[0:SLAB, :] = jnp.zeros((SLAB, e), F32)


def _for_each_tile(dil, n, tile_fn):
    span = ATTN_BLOCK * dil

    def sub(jj, carry):
        start = pl.multiple_of(jj * span, ATTN_BLOCK)
        first = jnp.logical_and(n == 0, jj == 0)
        for r in range(dil):
            tile_fn(_rows(start + r, ATTN_BLOCK, dil), _rows(SLAB - span + start + r, 2 * ATTN_BLOCK, dil), first)
        return carry

    lax.fori_loop(0, SLAB // span, sub, 0)


def _attn_fwd(qkv, q_norm, k_norm, name):
    s_len = qkv.shape[0]
    e = HEAD_DIM
    n_slabs = s_len // SLAB

    def body(cur_ref, prev_ref, qn_ref, kn_ref, o_ref, lse_ref, qs, kb, vb, m_s, l_s, acc_s):
        n, grp = pl.program_id(1), pl.program_id(2)
        _load_qkv(cur_ref, prev_ref, qn_ref, kn_ref, qs, kb, vb, n)

        def run(gi, dil):
            def tile(q_rows, kv_rows, first):
                q = qs[q_rows, :].astype(BF16)
                k = kb[kv_rows, :].astype(BF16)
                v = vb[kv_rows, :].astype(BF16)
                s = jnp.where(_band_mask(first), _dot_nt(q, k) * ATTN_SCALE, NEG)
                m = jnp.max(s, axis=-1, keepdims=True)
                p = jnp.exp(s - m)
                m_s.at[gi][q_rows, :] = jnp.broadcast_to(m, (ATTN_BLOCK, e))
                l_s.at[gi][q_rows, :] = jnp.broadcast_to(jnp.sum(p, axis=-1, keepdims=True), (ATTN_BLOCK, e))
                acc_s.at[gi][q_rows, :] = _dot(p.astype(BF16), v)

            _for_each_tile(dil, n, tile)

        for gi, dil in enumerate(DILATIONS):
            pl.when(grp == gi)(lambda gi=gi, dil=dil: run(gi, dil))

        @pl.when(grp == N_GROUPS - 1)
        def _():
            m_all = jnp.maximum(jnp.maximum(m_s[0], m_s[1]), m_s[2])
            den = jnp.zeros((SLAB, e), F32)
            num = jnp.zeros((SLAB, e), F32)
            for gi in range(N_GROUPS):
                w = jnp.exp(m_s[gi] - m_all)
                den += l_s[gi] * w
                num += acc_s[gi] * w
            o_ref[...] = num / den
            lse_ref[...] = m_all + jnp.log(den)

    col = lambda h, g: h * N_GROUPS + g
    small = pl.BlockSpec((1, e), lambda h, n, g: (0, 0))
    out = pl.BlockSpec((SLAB, e), lambda h, n, g: (n, h))
    return pl.pallas_call(
        body, name=name, grid=(HEADS, n_slabs, N_GROUPS),
        in_specs=[pl.BlockSpec((SLAB, 3 * e), lambda h, n, g: (n, col(h, g))),
                  pl.BlockSpec((SLAB, 3 * e), lambda h, n, g: (jnp.maximum(n - 1, 0), col(h, g))),
                  small, small],
        out_specs=[out, out],
        out_shape=[jax.ShapeDtypeStruct((s_len, HEADS * e), F32)] * 2,
        scratch_shapes=[pltpu.VMEM((SLAB, e), F32), pltpu.VMEM((2 * SLAB, e), F32), pltpu.VMEM((2 * SLAB, e), F32),
                        pltpu.VMEM((N_GROUPS, SLAB, e), F32), pltpu.VMEM((N_GROUPS, SLAB, e), F32),
                        pltpu.VMEM((N_GROUPS, SLAB, e), F32)],
        compiler_params=_params(3),
    )(qkv, qkv, q_norm, k_norm)


def _attn_bwd(qkv, d_out, out, lse, q_norm, k_norm, name):
    s_len = qkv.shape[0]
    e = HEAD_DIM
    n_slabs = s_len // SLAB

    def body(cur_ref, prev_ref, do_ref, o_ref, lse_ref, qn_ref, kn_ref, dqkv_ref, st_ref,
             qs, kb, vb, dqs, dkb, dvb, carry):
        head, step, grp = pl.program_id(0), pl.program_id(1), pl.program_id(2)
        n = n_slabs - 1 - step
        _load_qkv(cur_ref, prev_ref, qn_ref, kn_ref, qs, kb, vb, n)
        dkb[...] = jnp.zeros_like(dkb)
        dvb[...] = jnp.zeros_like(dvb)

        @pl.when((head == 0) & (step == 0) & (grp == 0))
        def _():
            st_ref[...] = jnp.zeros_like(st_ref)

        def run(gi, dil):
            @pl.when(step == 0)
            def _():
                carry[gi] = jnp.zeros((2, SLAB, e), F32)

            def tile(q_rows, kv_rows, first):
                q = qs[q_rows, :].astype(BF16)
                k = kb[kv_rows, :].astype(BF16)
                v = vb[kv_rows, :].astype(BF16)
                do = do_ref[q_rows, :]
                delta = jnp.sum(do * o_ref[q_rows, :], axis=-1, keepdims=True)
                s = jnp.where(_band_mask(first), _dot_nt(q, k) * ATTN_SCALE, NEG)
                p = jnp.exp(s - lse_ref[q_rows, :][:, 0:1])
                do16 = do.astype(BF16)
                ds = (p * (_dot_nt(do16, v) - delta) * ATTN_SCALE).astype(BF16)
                dqs[q_rows, :] = _dot(ds, k)
                dkb[kv_rows, :] += _dot_tn(ds, q)
                dvb[kv_rows, :] += _dot_tn(p.astype(BF16), do16)

            _for_each_tile(dil, n, tile)
            dk_hat = dkb[SLAB:2 * SLAB, :] + carry[gi, 0]
            dv = dvb[SLAB:2 * SLAB, :] + carry[gi, 1]
            carry[gi, 0] = dkb[0:SLAB, :]
            carry[gi, 1] = dvb[0:SLAB, :]

            def norm_bwd(raw, gain, d_hat):
                r = _rms(raw)
                y = raw * r
                dy = d_hat * gain
                return r * (dy - y * jnp.mean(dy * y, axis=-1, keepdims=True)), jnp.sum(d_hat * y, axis=0, keepdims=True)

            dq, dqn = norm_bwd(cur_ref[:, 0:e], qn_ref[...], dqs[...])
            dk, dkn = norm_bwd(cur_ref[:, e:2 * e], kn_ref[...], dk_hat)
            dqkv_ref[:, 0:e] = dq.astype(BF16)
            dqkv_ref[:, e:2 * e] = dk.astype(BF16)
            dqkv_ref[:, 2 * e:3 * e] = dv.astype(BF16)
            st_ref[0:1, :] += dqn
            st_ref[1:2, :] += dkn

        for gi, dil in enumerate(DILATIONS):
            pl.when(grp == gi)(lambda gi=gi, dil=dil: run(gi, dil))

    col = lambda h, g: h * N_GROUPS + g
    slab_of = lambda s: n_slabs - 1 - s
    small = pl.BlockSpec((1, e), lambda h, s, g: (0, 0))
    head_blk = pl.BlockSpec((SLAB, e), lambda h, s, g: (slab_of(s), h))
    return pl.pallas_call(
        body, name=name, grid=(HEADS, n_slabs, N_GROUPS),
        in_specs=[pl.BlockSpec((SLAB, 3 * e), lambda h, s, g: (slab_of(s), col(h, g))),
                  pl.BlockSpec((SLAB, 3 * e), lambda h, s, g: (jnp.maximum(slab_of(s) - 1, 0), col(h, g))),
                  head_blk, head_blk, head_blk, small, small],
        out_specs=[pl.BlockSpec((SLAB, 3 * e), lambda h, s, g: (slab_of(s), col(h, g))),
                   pl.BlockSpec((8, e), lambda h, s, g: (0, 0))],
        out_shape=[jax.ShapeDtypeStruct((s_len, 3 * QKV), BF16), jax.ShapeDtypeStruct((8, e), F32)],
        scratch_shapes=[pltpu.VMEM((SLAB, e), F32), pltpu.VMEM((2 * SLAB, e), F32), pltpu.VMEM((2 * SLAB, e), F32),
                        pltpu.VMEM((SLAB, e), F32), pltpu.VMEM((2 * SLAB, e), F32), pltpu.VMEM((2 * SLAB, e), F32),
                        pltpu.VMEM((N_GROUPS, 2, SLAB, e), F32)],
        compiler_params=_params(3),
    )(qkv, qkv, d_out, out, lse, q_norm, k_norm)


def _shift_rows(x, by, edge, forward):
    t_len = x.shape[0]
    row = lax.broadcasted_iota(jnp.int32, x.shape, 0)
    if forward:
        out = pltpu.roll(x, by, 0)
        for i in range(by):
            out = jnp.where(row == i, edge[8 - by + i:8 - by + i + 1, :], out)
    else:
        out = pltpu.roll(x, t_len - by, 0)
        for i in range(by):
            out = jnp.where(row == t_len - by + i, edge[i:i + 1, :], out)
    return out


def _mix_fwd(x, o, rest, mod, conv_w, w_attn, w_conv, w_out, name):
    s_len, d = x.shape
    tm = MIX_TILE
    a_w = o.shape[1]

    def body(x_ref, o_ref, u_ref, b_ref, c_ref, ga_ref, gc_ref, mod_ref, cw_ref, wa_ref, wc_ref, wo_ref,
             xo_ref, z_ref, ya_ref, yc_ref, conv_ref, yb_ref, m_ref, o16_ref, carry):
        @pl.when(pl.program_id(0) == 0)
        def _():
            carry[...] = jnp.zeros_like(carry)

        xc = c_ref[...] * u_ref[...]
        edge = carry[...]
        conv = (_shift_rows(xc, 2, edge, True) * cw_ref[0:1, :] + _shift_rows(xc, 1, edge, True) * cw_ref[1:2, :]
                + xc * cw_ref[2:3, :])
        carry[...] = xc[tm - 8:tm, :]
        yb = (b_ref[...] * conv).astype(BF16)
        o16 = o_ref[...].astype(BF16)
        ya = _dot(o16, wa_ref[...])
        yc = _dot(yb, wc_ref[...])
        merged = (_sigmoid(ga_ref[...]) * ya + _sigmoid(gc_ref[...]) * yc).astype(BF16)
        z = _dot(merged, wo_ref[...])
        xo_ref[...] = x_ref[...] + mod_ref[2:3, :] * z
        z_ref[...] = z
        ya_ref[...] = ya.astype(BF16)
        yc_ref[...] = yc.astype(BF16)
        conv_ref[...] = conv.astype(BF16)
        yb_ref[...] = yb
        m_ref[...] = merged
        o16_ref[...] = o16

    tile = pl.BlockSpec((tm, d), lambda i: (i, 0))
    sect = lambda k: pl.BlockSpec((tm, d), lambda i: (i, k))
    att = pl.BlockSpec((tm, a_w), lambda i: (i, 0))
    const = lambda shape: pl.BlockSpec(shape, lambda i: (0, 0))
    f32_out = jax.ShapeDtypeStruct((s_len, d), F32)
    b16_out = jax.ShapeDtypeStruct((s_len, d), BF16)
    return pl.pallas_call(
        body, name=name, grid=(s_len // tm,),
        in_specs=[tile, att, sect(0), sect(1), sect(2), sect(3), sect(4), const((8, d)), const((8, d)),
                  const((a_w, d)), const((d, d)), const((d, d))],
        out_specs=[tile] * 7 + [att],
        out_shape=[f32_out, f32_out] + [b16_out] * 5 + [jax.ShapeDtypeStruct((s_len, a_w), BF16)],
        scratch_shapes=[pltpu.VMEM((8, d), F32)],
        compiler_params=_params(1),
    )(x, o, rest, rest, rest, rest, rest, mod, conv_w, w_attn, w_conv, w_out)


def _mix_bwd(dxo, ya, yc, conv, rest, mod, conv_w, w_attn, w_conv, w_out, a_w, name):
    s_len, d = dxo.shape
    tm = MIX_TILE
    n_tiles = s_len // tm

    def body(dxo_ref, ya_ref, yc_ref, conv_ref, u_ref, b_ref, c_ref, ga_ref, gc_ref, mod_ref, cw_ref,
             wa_ref, wc_ref, wo_ref, do_ref, drest_ref, dz_ref, dya_ref, dyc_ref, st_ref, carry):
        @pl.when(pl.program_id(0) == 0)
        def _():
            carry[...] = jnp.zeros_like(carry)
            st_ref[...] = jnp.zeros_like(st_ref)

        dz = (mod_ref[2:3, :] * dxo_ref[...]).astype(BF16)
        dz_ref[...] = dz
        dm = _dot_nt(dz, wo_ref[...])
        sa, sc = _sigmoid(ga_ref[...]), _sigmoid(gc_ref[...])
        dya = (dm * sa).astype(BF16)
        dyc = (dm * sc).astype(BF16)
        dya_ref[...] = dya
        dyc_ref[...] = dyc
        drest_ref[:, 3 * d:4 * d] = (dm * ya_ref[...].astype(F32) * (sa * (1.0 - sa))).astype(BF16)
        drest_ref[:, 4 * d:5 * d] = (dm * yc_ref[...].astype(F32) * (sc * (1.0 - sc))).astype(BF16)
        do_ref[...] = _dot_nt(dya, wa_ref[...])
        dyb = _dot_nt(dyc, wc_ref[...])
        drest_ref[:, d:2 * d] = (dyb * conv_ref[...].astype(F32)).astype(BF16)
        dconv = dyb * b_ref[...]
        edge = carry[...]
        sh1 = _shift_rows(dconv, 1, edge, False)
        sh2 = _shift_rows(dconv, 2, edge, False)
        carry[...] = dconv[0:8, :]
        dxc = dconv * cw_ref[2:3, :] + sh1 * cw_ref[1:2, :] + sh2 * cw_ref[0:1, :]
        u, c = u_ref[...], c_ref[...]
        xc = c * u
        drest_ref[:, 0:d] = (dxc * c).astype(BF16)
        drest_ref[:, 2 * d:3 * d] = (dxc * u).astype(BF16)
        st_ref[0:1, :] += jnp.sum(xc * sh2, axis=0, keepdims=True)
        st_ref[1:2, :] += jnp.sum(xc * sh1, axis=0, keepdims=True)
        st_ref[2:3, :] += jnp.sum(xc * dconv, axis=0, keepdims=True)

    rev = lambda i: n_tiles - 1 - i
    tile = pl.BlockSpec((tm, d), lambda i: (rev(i), 0))
    sect = lambda k: pl.BlockSpec((tm, d), lambda i: (rev(i), k))
    const = lambda shape: pl.BlockSpec(shape, lambda i: (0, 0))
    b16_out = jax.ShapeDtypeStruct((s_len, d), BF16)
    return pl.pallas_call(
        body, name=name, grid=(n_tiles,),
        in_specs=[tile, tile, tile, tile, sect(0), sect(1), sect(2), sect(3), sect(4), const((8, d)), const((8, d)),
                  const((a_w, d)), const((d, d)), const((d, d))],
        out_specs=[pl.BlockSpec((tm, a_w), lambda i: (rev(i), 0)), pl.BlockSpec((tm, 5 * d), lambda i: (rev(i), 0)),
                   tile, tile, tile, const((8, d))],
        out_shape=[jax.ShapeDtypeStruct((s_len, a_w), F32), jax.ShapeDtypeStruct((s_len, 5 * d), BF16),
                   b16_out, b16_out, b16_out, jax.ShapeDtypeStruct((8, d), F32)],
        scratch_shapes=[pltpu.VMEM((8, d), F32)],
        compiler_params=_params(1),
    )(dxo, ya, yc, conv, rest, rest, rest, rest, rest, mod, conv_w, w_attn, w_conv, w_out)


ADA_COLS = 128


def _ada_fwd(c_all, w_shard, b_shard, name):
    d, cols = w_shard.shape

    def body(c_ref, w_ref, b_ref, o_ref):
        cv = c_ref[...]
        o_ref[...] = jnp.dot(cv * _sigmoid(cv), w_ref[...], preferred_element_type=F32,
                             precision=lax.Precision.HIGHEST) + b_ref[...]

    return pl.pallas_call(
        body, name=name, grid=(cols // ADA_COLS,),
        in_specs=[pl.BlockSpec((8, d), lambda j: (0, 0)), pl.BlockSpec((d, ADA_COLS), lambda j: (0, j)),
                  pl.BlockSpec((1, ADA_COLS), lambda j: (0, j))],
        out_specs=pl.BlockSpec((8, ADA_COLS), lambda j: (0, j)),
        out_shape=jax.ShapeDtypeStruct((8, cols), F32),
        compiler_params=_params(1),
    )(c_all, w_shard, b_shard)


def _ada_bwd(c_all, dmod_shard, w, m, v, name):
    d, cols = w.shape

    def body(c_ref, dm_ref, w_ref, m_ref, v_ref, g_ref, d_ref, nm_ref, nv_ref):
        cv = c_ref[...]
        g = lax.dot_general(cv * _sigmoid(cv), dm_ref[...], (((0,), (0,)), ((), ())),
                            preferred_element_type=F32, precision=lax.Precision.HIGHEST)
        g_ref[...] = g
        d_ref[...], nm_ref[...], nv_ref[...] = _adamw_math(w_ref[...], g, m_ref[...], v_ref[...])

    blk = pl.BlockSpec((d, ADA_COLS), lambda j: (0, j))
    shape = jax.ShapeDtypeStruct((d, cols), F32)
    return pl.pallas_call(
        body, name=name, grid=(cols // ADA_COLS,),
        in_specs=[pl.BlockSpec((8, d), lambda j: (0, 0)), pl.BlockSpec((8, ADA_COLS), lambda j: (0, j)), blk, blk, blk],
        out_specs=[blk] * 4, out_shape=[shape] * 4,
        compiler_params=_params(1),
    )(c_all, dmod_shard, w, m, v)


def _small_update(parts, w, m, v, name):
    n = w.shape[1]

    def body(p_ref, w_ref, m_ref, v_ref, g_ref, d_ref, nm_ref, nv_ref):
        g = p_ref[0:1, :]
        for i in range(1, 8):
            g = g + p_ref[i:i + 1, :]
        g_ref[...] = g
        d_ref[...], nm_ref[...], nv_ref[...] = _adamw_math(w_ref[...], g, m_ref[...], v_ref[...])

    shape = jax.ShapeDtypeStruct((1, n), F32)
    return pl.pallas_call(body, name=name, out_shape=[shape] * 4, compiler_params=_params())(parts, w, m, v)


def _qkv_perm(w):
    k = w.shape[0]
    return w.reshape(k, 3, N_GROUPS, HEADS, HEAD_DIM).transpose(0, 3, 2, 1, 4).reshape(k, 3 * QKV)


def _qkv_unperm(w):
    k = w.shape[0]
    return w.reshape(k, HEADS, N_GROUPS, 3, HEAD_DIM).transpose(0, 3, 2, 1, 4).reshape(k, 3 * QKV)


def _cols_from_shards(w):
    n, r, c = w.shape
    return w.transpose(1, 0, 2).reshape(r, n * c)


def _cols_to_shards(w, n):
    r, nc = w.shape
    return w.reshape(r, n, nc // n).transpose(1, 0, 2)


def kernel(x, c, w_ada, b_ada, norm_ffn1, ffn1_w_gate, ffn1_w_up, ffn1_w_down, norm_mix, w_in, q_norm, k_norm, conv_w, w_attn_branch, w_conv_branch, w_out, norm_ffn2, ffn2_w_gate, ffn2_w_up, ffn2_w_down, loss_target, m_w_ada, m_b_ada, m_norm_ffn1, m_ffn1_w_gate, m_ffn1_w_up, m_ffn1_w_down, m_norm_mix, m_w_in, m_q_norm, m_k_norm, m_conv_w, m_w_attn_branch, m_w_conv_branch, m_w_out, m_norm_ffn2, m_ffn2_w_gate, m_ffn2_w_up, m_ffn2_w_down, v_w_ada, v_b_ada, v_norm_ffn1, v_ffn1_w_gate, v_ffn1_w_up, v_ffn1_w_down, v_norm_mix, v_w_in, v_q_norm, v_k_norm, v_conv_w, v_w_attn_branch, v_w_conv_branch, v_w_out, v_norm_ffn2, v_ffn2_w_gate, v_ffn2_w_up, v_ffn2_w_down):
    ix, iy, ic = _place()
    chip = 2 * ix + iy
    me = 4 * ix + 2 * iy + ic
    xs = x[0]
    target = loss_target[0]
    s_len, d = xs.shape
    ada_cols = w_ada.shape[2]
    conv_cols = conv_w.shape[2]

    conv_rows = jnp.zeros((8, conv_cols), F32).at[0:3].set(conv_w[0])
    small_in = jnp.concatenate([jnp.broadcast_to(c, (8, d)), conv_rows], axis=1)
    small_all = _allgather8(small_in, "gather_c").reshape(8, 8, d + conv_cols)
    c_all = small_all[:, 0, :d]
    conv_full = small_all[0::2, 0:3, d:].transpose(1, 0, 2).reshape(3, N_CHIPS * conv_cols)
    conv_pad = jnp.zeros((8, N_CHIPS * conv_cols), F32).at[0:3].set(conv_full)
    b_shard = lax.dynamic_slice(b_ada, (0, chip * ada_cols), (1, ada_cols))
    mod_part = _ada_fwd(c_all, w_ada[0], b_shard, "ada_fwd")
    mod_all = _allgather8(mod_part, "gather_mod").reshape(N_CHIPS, 2, 8, ada_cols)[:, 0]
    mod_mine = lax.dynamic_slice(mod_all, (0, me, 0), (N_CHIPS, 1, ada_cols)).reshape(9, d)

    def mod_rows(i, gain):
        return jnp.zeros((8, d), F32).at[0:3].set(mod_mine[3 * i:3 * i + 3]).at[3:4].set(gain)

    mod1, mod2, mod3 = mod_rows(0, norm_ffn1), mod_rows(1, norm_mix), mod_rows(2, norm_ffn2)

    to16 = lambda w: w[0].astype(BF16)
    wg1, wu1, wd1 = _gather_weights([to16(ffn1_w_gate), to16(ffn1_w_up), to16(ffn1_w_down)], "gather_ffn1")
    w_in_g, w_ab_g, w_cb_g, w_o_g, wg2, wu2, wd2 = _gather_weights(
        [to16(w_in), to16(w_attn_branch), to16(w_conv_branch), to16(w_out),
         to16(ffn2_w_gate), to16(ffn2_w_up), to16(ffn2_w_down)], "gather_rest")
    w_in_full = _cols_from_shards(w_in_g)
    w_qkv = _qkv_perm(w_in_full[:, :3 * QKV])
    w_rest = w_in_full[:, 3 * QKV:]
    w_ab = _cols_from_shards(w_ab_g)
    a_w = w_ab.shape[0]
    w_cb = w_cb_g.reshape(d, d)
    w_o = w_o_g.reshape(d, d)

    h1 = _norm_mod(xs, mod1, "norm1")
    x1, g1, u1, y1 = _ffn_fwd(xs, h1, mod1, wg1, wu1, wd1, "ffn1_fwd")
    h2 = _norm_mod(x1, mod2, "norm2")
    qkv = _matmul_nn(h2, w_qkv, 3 * HEAD_DIM * N_GROUPS, F32, "in_proj_qkv")
    rest = _matmul_nn(h2, w_rest, d, F32, "in_proj_rest")
    o, lse = _attn_fwd(qkv, q_norm, k_norm, "attn_fwd")
    x2, z, ya, yc, conv, yb, merged, o16 = _mix_fwd(x1, o, rest, mod2, conv_pad, w_ab, w_cb, w_o, "mix_fwd")
    h3 = _norm_mod(x2, mod3, "norm3")
    x3, g3, u3, y3 = _ffn_fwd(x2, h3, mod3, wg2, wu2, wd2, "ffn2_fwd")
    dx3, loss_part = _loss_grad(x3, target, "loss")
    loss = lax.psum(0.5 * jnp.sum(loss_part) / d, ("x", "y", "c"))

    dh3, dg3, du3, a3, dy3 = _ffn_bwd(dx3, mod3, g3, u3, wg2, wu2, wd2, "ffn2_bwd")
    dx2, st3 = _norm_bwd(dh3, x2, mod3, dx3, y3, 0.5, "norm3_bwd")
    dwg2, dwu2, dwd2 = _ffn_wgrads(h3, dg3, du3, a3, dy3, "ffn2")

    do, drest, dz, dya, dyc, st_conv = _mix_bwd(dx2, ya, yc, conv, rest, mod2, conv_pad, w_ab, w_cb, w_o, a_w, "mix_bwd")
    dqkv, st_qk = _attn_bwd(qkv, do, o, lse, q_norm, k_norm, "attn_bwd")
    dh2 = _matmul_nt(dqkv, w_qkv, 3 * HEAD_DIM * N_GROUPS, None, "in_proj_bwd_qkv")
    dh2 = _matmul_nt(drest, w_rest, d, dh2, "in_proj_bwd_rest")
    dx1, st2 = _norm_bwd(dh2, x1, mod2, dx2, z, 1.0, "norm2_bwd")
    tok = lambda width: (lambda ts: pl.BlockSpec((ts, width), lambda cc, s: (s, 0)))
    colblk = lambda width: (lambda ts: pl.BlockSpec((ts, width), lambda cc, s: (s, cc)))
    qw = 3 * HEAD_DIM * N_GROUPS
    dw_qkv = _wgrad(h2, dqkv, tok(d), colblk(qw), (d, 3 * QKV), pl.BlockSpec((d, qw), lambda cc, s: (0, cc)),
                    (d, qw), 3 * QKV // qw, "dw_in_qkv")
    dw_rest = _wgrad(h2, drest, tok(d), colblk(d), (d, 5 * d), pl.BlockSpec((d, d), lambda cc, s: (0, cc)),
                     (d, d), 5, "dw_in_rest")
    dw_in = _cols_to_shards(jnp.concatenate([_qkv_unperm(dw_qkv), dw_rest], axis=1), N_CHIPS)
    shard_w = d // N_CHIPS
    dw_ab = _wgrad(o16, dya, tok(a_w), colblk(shard_w), (a_w, d), pl.BlockSpec((a_w, shard_w), lambda cc, s: (0, cc)),
                   (a_w, shard_w), N_CHIPS, "dw_attn_branch")
    dw_ab = _cols_to_shards(dw_ab, N_CHIPS)
    row_out = pl.BlockSpec((None, shard_w, d), lambda cc, s: (cc, 0, 0))
    dw_cb = _wgrad(yb, dyc, colblk(shard_w), tok(d), (N_CHIPS, shard_w, d), row_out, (shard_w, d), N_CHIPS, "dw_conv_branch")
    dw_o = _wgrad(merged, dz, colblk(shard_w), tok(d), (N_CHIPS, shard_w, d), row_out, (shard_w, d), N_CHIPS, "dw_out")

    dh1, dg1, du1, a1, dy1 = _ffn_bwd(dx1, mod1, g1, u1, wg1, wu1, wd1, "ffn1_bwd")
    dx0, st1 = _norm_bwd(dh1, xs, mod1, dx1, y1, 0.5, "norm1_bwd")
    dwg1, dwu1, dwd1 = _ffn_wgrads(h1, dg1, du1, a1, dy1, "ffn1")

    grads = [dwg1, dwu1, dwd1, dw_in, dw_ab, dw_cb, dw_o, dwg2, dwu2, dwd2]
    names = ["ffn1_w_gate", "ffn1_w_up", "ffn1_w_down", "w_in", "w_attn_branch", "w_conv_branch", "w_out",
             "ffn2_w_gate", "ffn2_w_up", "ffn2_w_down"]
    c_idx = jnp.reshape(ic, (1,)).astype(jnp.int32)
    chip_idx = jnp.reshape(chip, (1,)).astype(jnp.int32)
    from_sibling = _rs_pair_exchange(grads, "rs_pair")
    pair_sums = [_pair_add(g, r, c_idx, "pair_add_" + nm) for g, r, nm in zip(grads, from_sibling, names)]
    from_chips = _rs_chip_exchange(pair_sums, "rs_chips")
    totals = [_chip_add(p, r, chip_idx, "chip_add_" + nm) for p, r, nm in zip(pair_sums, from_chips, names)]
    shard_grads = dict(zip(names, _rs_share(totals, "rs_share")))

    dmod = jnp.concatenate([st1[0:3], st2[0:3], st3[0:3]], axis=0).reshape(1, 9 * d)
    small = jnp.concatenate([dmod, st1[3:4], st2[3:4], st3[3:4], st_qk[0:1], st_qk[1:2],
                             st_conv[0:3].reshape(1, 3 * d)], axis=1)
    small_all = _allgather8(jnp.broadcast_to(small, (8, small.shape[1])), "gather_small").reshape(8, 8, -1)[:, 0]
    dmod_all = small_all[:, :9 * d]
    dmod_shard = lax.dynamic_slice(dmod_all, (0, chip * ada_cols), (8, ada_cols))
    g_w_ada, d_w_ada, nm_w_ada, nv_w_ada = _ada_bwd(c_all, dmod_shard, w_ada[0], m_w_ada[0], v_w_ada[0], "ada_bwd")

    vec_names = ["b_ada", "norm_ffn1", "norm_mix", "norm_ffn2", "q_norm", "k_norm"]
    vec_w = [b_ada, norm_ffn1, norm_mix, norm_ffn2, q_norm, k_norm]
    vec_m = [m_b_ada, m_norm_ffn1, m_norm_mix, m_norm_ffn2, m_q_norm, m_k_norm]
    vec_v = [v_b_ada, v_norm_ffn1, v_norm_mix, v_norm_ffn2, v_q_norm, v_k_norm]
    n_vec = sum(w.shape[1] for w in vec_w)
    cat = lambda arrs: jnp.concatenate(arrs, axis=1)
    vec_out = _small_update(small_all[:, :n_vec], cat(vec_w), cat(vec_m), cat(vec_v), "small_update")
    conv_parts = small_all[:, n_vec:].reshape(8, 3, N_CHIPS * conv_cols)
    conv_parts = lax.dynamic_slice(conv_parts, (0, 0, chip * conv_cols), (8, 3, conv_cols)).reshape(8, 3 * conv_cols)
    flat3 = lambda w: w[0].reshape(1, 3 * conv_cols)
    conv_out = _small_update(conv_parts, flat3(conv_w), flat3(m_conv_w), flat3(v_conv_w), "conv_update")

    res = {"w_ada": [t[None] for t in (g_w_ada, d_w_ada, nm_w_ada, nv_w_ada)],
           "conv_w": [t.reshape(1, 3, conv_cols) for t in conv_out]}
    off = 0
    for nm, w in zip(vec_names, vec_w):
        width = w.shape[1]
        res[nm] = [t[:, off:off + width] for t in vec_out]
        off += width
    big = {"ffn1_w_gate": (ffn1_w_gate, m_ffn1_w_gate, v_ffn1_w_gate), "ffn1_w_up": (ffn1_w_up, m_ffn1_w_up, v_ffn1_w_up),
           "ffn1_w_down": (ffn1_w_down, m_ffn1_w_down, v_ffn1_w_down), "w_in": (w_in, m_w_in, v_w_in),
           "w_attn_branch": (w_attn_branch, m_w_attn_branch, v_w_attn_branch),
           "w_conv_branch": (w_conv_branch, m_w_conv_branch, v_w_conv_branch), "w_out": (w_out, m_w_out, v_w_out),
           "ffn2_w_gate": (ffn2_w_gate, m_ffn2_w_gate, v_ffn2_w_gate), "ffn2_w_up": (ffn2_w_up, m_ffn2_w_up, v_ffn2_w_up),
           "ffn2_w_down": (ffn2_w_down, m_ffn2_w_down, v_ffn2_w_down)}
    for nm, (w, m, v) in big.items():
        g = shard_grads[nm]
        delta, new_m, new_v = _adamw(w[0], g, m[0], v[0], "adamw_" + nm)
        res[nm] = [t[None] for t in (g, delta, new_m, new_v)]

    order = ["w_ada", "b_ada", "norm_ffn1", "ffn1_w_gate", "ffn1_w_up", "ffn1_w_down", "norm_mix", "w_in", "q_norm",
             "k_norm", "conv_w", "w_attn_branch", "w_conv_branch", "w_out", "norm_ffn2", "ffn2_w_gate", "ffn2_w_up",
             "ffn2_w_down"]
    return (loss, dx0[None], *[res[nm][0] for nm in order], *[res[nm][1] for nm in order],
            *[res[nm][2] for nm in order], *[res[nm][3] for nm in order])
```

```python
import jax
import jax.numpy as jnp
from jax import lax
from jax.experimental import pallas as pl
from jax.experimental.pallas import tpu as pltpu
from jax.experimental.pallas import tpu_sc as plsc

F32 = jnp.float32
BF16 = jnp.bfloat16
MESH = pl.DeviceIdType.MESH
ANY = pl.BlockSpec(memory_space=pl.ANY)

NORM_EPS = 1e-6
HEAD_DIM = 128
N_GROUPS = 3
HEADS = 4
DILATIONS = (1, 4, 16)
ATTN_BLOCK = 128
SLAB = ATTN_BLOCK * max(DILATIONS)
QKV = N_GROUPS * HEADS * HEAD_DIM
ATTN_SCALE = HEAD_DIM ** -0.5
NEG = -1e30
N_CHIPS = 4

ADAM_LR = 0.001
ADAM_B1 = 0.9
ADAM_B2 = 0.999
ADAM_EPS = 1e-08
ADAM_WD = 0.01
ADAM_STEP = 10

VMEM_LIMIT_BYTES = 56 * 1024 * 1024
TOKEN_TILE = 512
MIX_TILE = 256


def _params(n_axes=0):
    return pltpu.CompilerParams(
        dimension_semantics=("arbitrary",) * n_axes if n_axes else None,
        vmem_limit_bytes=VMEM_LIMIT_BYTES)


def _dot(a, b):
    return jnp.dot(a, b, preferred_element_type=F32)


def _dot_nt(a, b):
    return lax.dot_general(a, b, (((1,), (1,)), ((), ())), preferred_element_type=F32)


def _dot_tn(a, b):
    return lax.dot_general(a, b, (((0,), (0,)), ((), ())), preferred_element_type=F32)


def _sigmoid(x):
    return 1.0 / (1.0 + jnp.exp(-x))


def _place():
    return lax.axis_index("x"), lax.axis_index("y"), lax.axis_index("c")


def _allgather8(block, name):
    m_per, n = block.shape

    def body(x_ref, out_ref, send_sems, recv_sems, local_sem):
        x, y, c = _place()
        me, sibling = (x, y, c), (x, y, 1 - c)
        chips = [(1 - x, y), (x, 1 - y), (1 - x, 1 - y)]

        def rows(px, py, pc):
            return out_ref.at[pl.ds((4 * px + 2 * py + pc) * m_per, m_per), :]

        def copy(k, blk, to, src=None):
            return pltpu.make_async_remote_copy(
                src_ref=rows(*blk) if src is None else src, dst_ref=rows(*blk),
                send_sem=send_sems.at[k], recv_sem=recv_sems.at[k],
                device_id=to, device_id_type=MESH)

        mine = pltpu.make_async_copy(x_ref, rows(*me), local_sem)
        mine.start()
        first = [copy(0, me, sibling, src=x_ref)]
        first += [copy(1 + j, me, (*chip, c), src=x_ref) for j, chip in enumerate(chips)]
        for cp in first:
            cp.start()
        passed = [copy(4 + j, (*chip, c), sibling) for j, chip in enumerate(chips)]
        for j, chip in enumerate(chips):
            copy(1 + j, (*chip, c), me).wait_recv()
            passed[j].start()
        copy(0, sibling, me).wait_recv()
        for j, chip in enumerate(chips):
            copy(4 + j, (*chip, 1 - c), me).wait_recv()
        for cp in first + passed:
            cp.wait_send()
        mine.wait()

    return pl.pallas_call(
        body, name=name,
        out_shape=jax.ShapeDtypeStruct((8 * m_per, n), block.dtype),
        in_specs=[pl.BlockSpec(memory_space=pltpu.VMEM)],
        out_specs=pl.BlockSpec(memory_space=pltpu.VMEM),
        scratch_shapes=[pltpu.SemaphoreType.DMA((7,)), pltpu.SemaphoreType.DMA((7,)),
                        pltpu.SemaphoreType.DMA],
        compiler_params=_params(),
    )(block)


def _handshake(peers):
    barrier = pltpu.get_barrier_semaphore()
    for peer in peers:
        pl.semaphore_signal(barrier, inc=1, device_id=peer, device_id_type=MESH)
    pl.semaphore_wait(barrier, len(peers))


def _gather_weights(shards, name, collective_id):
    n_arr = len(shards)

    def body(*refs):
        srcs, outs = refs[:n_arr], refs[n_arr:2 * n_arr]
        send_sems, recv_sems, local_sems = refs[2 * n_arr:]
        x, y, c = _place()
        me_dev, sibling = (x, y, c), (x, y, 1 - c)
        chips = [(1 - x, y), (x, 1 - y), (1 - x, 1 - y)]
        me = 2 * x + y
        _handshake([sibling] + [(*chip, c) for chip in chips])

        def copy(k, slot, chip_idx, half_sel, to, from_shard=False):
            half = srcs[k].shape[0] // 2
            rows = pl.ds(half_sel * half, half)
            dst = outs[k].at[chip_idx, rows]
            return pltpu.make_async_remote_copy(
                src_ref=srcs[k].at[rows] if from_shard else dst, dst_ref=dst,
                send_sem=send_sems.at[6 * k + slot], recv_sem=recv_sems.at[6 * k + slot],
                device_id=to, device_id_type=MESH)

        own = [pltpu.make_async_copy(srcs[k], outs[k].at[me], local_sems.at[k]) for k in range(n_arr)]
        for cp in own:
            cp.start()
        sent = []
        for k in range(n_arr):
            for j, chip in enumerate(chips):
                sent.append(copy(k, j, me, c, (*chip, c), from_shard=True))
                sent[-1].start()
        for k in range(n_arr):
            for j, chip in enumerate(chips):
                chip_idx = 2 * chip[0] + chip[1]
                copy(k, j, chip_idx, c, me_dev).wait_recv()
                sent.append(copy(k, 3 + j, chip_idx, c, sibling))
                sent[-1].start()
        for k in range(n_arr):
            for j, chip in enumerate(chips):
                copy(k, 3 + j, 2 * chip[0] + chip[1], 1 - c, me_dev).wait_recv()
        for cp in sent:
            cp.wait_send()
        for cp in own:
            cp.wait()

    return pl.kernel(
        body, name=name,
        out_type=[jax.ShapeDtypeStruct((N_CHIPS,) + s.shape, s.dtype) for s in shards],
        mesh=plsc.ScalarSubcoreMesh(axis_name="sequencer", num_cores=1),
        scratch_types=[pltpu.SemaphoreType.DMA((6 * n_arr,)), pltpu.SemaphoreType.DMA((6 * n_arr,)),
                       pltpu.SemaphoreType.DMA((n_arr,))],
        compiler_params=pltpu.CompilerParams(collective_id=collective_id),
    )(*shards)


def _rs_pair_exchange(grads, name):
    n_arr = len(grads)

    def body(*refs):
        srcs, outs = refs[:n_arr], refs[n_arr:2 * n_arr]
        send_sems, recv_sems = refs[2 * n_arr:]
        x, y, c = _place()
        cps = []
        for k in range(n_arr):
            half = srcs[k].shape[1] // 2
            cps.append(pltpu.make_async_remote_copy(
                src_ref=srcs[k].at[:, pl.ds((1 - c) * half, half)], dst_ref=outs[k],
                send_sem=send_sems.at[k], recv_sem=recv_sems.at[k],
                device_id=(x, y, 1 - c), device_id_type=MESH))
            cps[-1].start()
        for cp in cps:
            cp.wait_recv()
        for cp in cps:
            cp.wait_send()

    return pl.pallas_call(
        body, name=name,
        out_shape=[jax.ShapeDtypeStruct((g.shape[0], g.shape[1] // 2, g.shape[2]), g.dtype) for g in grads],
        in_specs=[ANY] * n_arr, out_specs=[ANY] * n_arr,
        scratch_shapes=[pltpu.SemaphoreType.DMA((n_arr,)), pltpu.SemaphoreType.DMA((n_arr,))],
        compiler_params=_params(),
    )(*grads)


def _rs_chip_exchange(sums, name):
    n_arr = len(sums)

    def body(*refs):
        srcs, outs = refs[:n_arr], refs[n_arr:2 * n_arr]
        send_sems, recv_sems = refs[2 * n_arr:]
        x, y, c = _place()
        chips = [(1 - x, y), (x, 1 - y), (1 - x, 1 - y)]
        cps = []
        for k in range(n_arr):
            for j, chip in enumerate(chips):
                cps.append(pltpu.make_async_remote_copy(
                    src_ref=srcs[k].at[2 * chip[0] + chip[1]], dst_ref=outs[k].at[j],
                    send_sem=send_sems.at[3 * k + j], recv_sem=recv_sems.at[3 * k + j],
                    device_id=(*chip, c), device_id_type=MESH))
                cps[-1].start()
        for cp in cps:
            cp.wait_recv()
        for cp in cps:
            cp.wait_send()

    return pl.pallas_call(
        body, name=name,
        out_shape=[jax.ShapeDtypeStruct((3,) + s.shape[1:], s.dtype) for s in sums],
        in_specs=[ANY] * n_arr, out_specs=[ANY] * n_arr,
        scratch_shapes=[pltpu.SemaphoreType.DMA((3 * n_arr,)), pltpu.SemaphoreType.DMA((3 * n_arr,))],
        compiler_params=_params(),
    )(*sums)


def _rs_share(totals, name):
    n_arr = len(totals)

    def body(*refs):
        srcs, outs = refs[:n_arr], refs[n_arr:2 * n_arr]
        send_sems, recv_sems, local_sems = refs[2 * n_arr:]
        x, y, c = _place()
        own, cps = [], []
        for k in range(n_arr):
            half = srcs[k].shape[0]
            mine = outs[k].at[pl.ds(c * half, half)]
            own.append(pltpu.make_async_copy(srcs[k], mine, local_sems.at[k]))
            own[-1].start()
            cps.append(pltpu.make_async_remote_copy(
                src_ref=srcs[k], dst_ref=mine, send_sem=send_sems.at[k], recv_sem=recv_sems.at[k],
                device_id=(x, y, 1 - c), device_id_type=MESH))
            cps[-1].start()
        for k in range(n_arr):
            half = srcs[k].shape[0]
            theirs = outs[k].at[pl.ds((1 - c) * half, half)]
            pltpu.make_async_remote_copy(
                src_ref=srcs[k], dst_ref=theirs, send_sem=send_sems.at[k], recv_sem=recv_sems.at[k],
                device_id=(x, y, 1 - c), device_id_type=MESH).wait_recv()
        for cp in cps:
            cp.wait_send()
        for cp in own:
            cp.wait()

    return pl.pallas_call(
        body, name=name,
        out_shape=[jax.ShapeDtypeStruct((2 * t.shape[0], t.shape[1]), t.dtype) for t in totals],
        in_specs=[ANY] * n_arr, out_specs=[ANY] * n_arr,
        scratch_shapes=[pltpu.SemaphoreType.DMA((n_arr,)), pltpu.SemaphoreType.DMA((n_arr,)),
                        pltpu.SemaphoreType.DMA((n_arr,))],
        compiler_params=_params(),
    )(*totals)


def _pair_add(grad, recv, c_idx, name):
    n, r, cols = grad.shape
    half = r // 2
    rows = half // 2

    def body(_, g_ref, r_ref, o_ref):
        o_ref[...] = (g_ref[...].astype(F32) + r_ref[...].astype(F32)).astype(o_ref.dtype)

    return pl.pallas_call(
        body, name=name,
        grid_spec=pltpu.PrefetchScalarGridSpec(
            num_scalar_prefetch=1, grid=(n, 2),
            in_specs=[pl.BlockSpec((None, None, rows, cols), lambda s, i, ci: (s, ci[0], i, 0)),
                      pl.BlockSpec((None, rows, cols), lambda s, i, ci: (s, i, 0))],
            out_specs=pl.BlockSpec((None, rows, cols), lambda s, i, ci: (s, i, 0))),
        out_shape=jax.ShapeDtypeStruct((n, half, cols), BF16),
        compiler_params=_params(2),
    )(c_idx, grad.reshape(n, 2, half, cols), recv)


def _chip_add(sums, recv, chip_idx, name):
    _, half, cols = sums.shape
    rows = half // 2

    def body(_, s_ref, r0_ref, r1_ref, r2_ref, o_ref):
        o_ref[...] = ((s_ref[...].astype(F32) + r0_ref[...].astype(F32))
                      + r1_ref[...].astype(F32)) + r2_ref[...].astype(F32)

    def recv_spec(j):
        return pl.BlockSpec((None, rows, cols), lambda i, ci: (j, i, 0))

    return pl.pallas_call(
        body, name=name,
        grid_spec=pltpu.PrefetchScalarGridSpec(
            num_scalar_prefetch=1, grid=(2,),
            in_specs=[pl.BlockSpec((None, rows, cols), lambda i, ci: (ci[0], i, 0)),
                      recv_spec(0), recv_spec(1), recv_spec(2)],
            out_specs=pl.BlockSpec((rows, cols), lambda i, ci: (i, 0))),
        out_shape=jax.ShapeDtypeStruct((half, cols), F32),
        compiler_params=_params(1),
    )(chip_idx, sums, recv, recv, recv)


def _rms(x):
    return lax.rsqrt(jnp.mean(x * x, axis=-1, keepdims=True) + NORM_EPS)


def _norm_mod(x, mod, name):
    s_len, d = x.shape
    tm = TOKEN_TILE

    def body(x_ref, mod_ref, h_ref):
        xv = x_ref[...]
        n = (xv * _rms(xv)) * mod_ref[3:4, :]
        h_ref[...] = (n * (1.0 + mod_ref[1:2, :]) + mod_ref[0:1, :]).astype(BF16)

    return pl.pallas_call(
        body, name=name, grid=(s_len // tm,),
        in_specs=[pl.BlockSpec((tm, d), lambda i: (i, 0)), pl.BlockSpec((8, d), lambda i: (0, 0))],
        out_specs=pl.BlockSpec((tm, d), lambda i: (i, 0)),
        out_shape=jax.ShapeDtypeStruct((s_len, d), BF16),
        compiler_params=_params(1),
    )(x, mod)


def _norm_bwd(dh, x, mod, dxo, y_raw, coef, name):
    s_len, d = x.shape
    tm = TOKEN_TILE

    def body(dh_ref, x_ref, mod_ref, dxo_ref, y_ref, dx_ref, st_ref):
        @pl.when(pl.program_id(0) == 0)
        def _():
            st_ref[...] = jnp.zeros_like(st_ref)

        xv, dhv, dxov = x_ref[...], dh_ref[...], dxo_ref[...]
        r = _rms(xv)
        xh = xv * r
        gain, scale = mod_ref[3:4, :], mod_ref[1:2, :]
        dn = dhv * (1.0 + scale)
        dxh = dn * gain
        dx_ref[...] = dxov + r * (dxh - xh * jnp.mean(dxh * xh, axis=-1, keepdims=True))
        st_ref[0:1, :] += jnp.sum(dhv, axis=0, keepdims=True)
        st_ref[1:2, :] += jnp.sum(dhv * (xh * gain), axis=0, keepdims=True)
        st_ref[2:3, :] += coef * jnp.sum(y_ref[...] * dxov, axis=0, keepdims=True)
        st_ref[3:4, :] += jnp.sum(dn * xh, axis=0, keepdims=True)

    tile = pl.BlockSpec((tm, d), lambda i: (i, 0))
    small = pl.BlockSpec((8, d), lambda i: (0, 0))
    return pl.pallas_call(
        body, name=name, grid=(s_len // tm,),
        in_specs=[tile, tile, small, tile, tile],
        out_specs=[tile, small],
        out_shape=[jax.ShapeDtypeStruct((s_len, d), F32), jax.ShapeDtypeStruct((8, d), F32)],
        compiler_params=_params(1),
    )(dh, x, mod, dxo, y_raw)


def _loss_grad(y, target, name):
    s_len, d = y.shape
    tm = TOKEN_TILE

    def body(y_ref, t_ref, dy_ref, part_ref):
        @pl.when(pl.program_id(0) == 0)
        def _():
            part_ref[...] = jnp.zeros_like(part_ref)

        err = y_ref[...] - t_ref[...]
        dy_ref[...] = err * (1.0 / d)
        sq = err * err
        part_ref[...] += jnp.sum(sq.reshape(tm // 8, 8, d), axis=0)

    tile = pl.BlockSpec((tm, d), lambda i: (i, 0))
    return pl.pallas_call(
        body, name=name, grid=(s_len // tm,),
        in_specs=[tile, tile],
        out_specs=[tile, pl.BlockSpec((8, d), lambda i: (0, 0))],
        out_shape=[jax.ShapeDtypeStruct((s_len, d), F32), jax.ShapeDtypeStruct((8, d), F32)],
        compiler_params=_params(1),
    )(y, target)


def _adamw_math(w, g, m, v):
    m = ADAM_B1 * m + (1.0 - ADAM_B1) * g
    v = ADAM_B2 * v + (1.0 - ADAM_B2) * (g * g)
    m_hat = m / (1.0 - ADAM_B1 ** ADAM_STEP)
    v_hat = v / (1.0 - ADAM_B2 ** ADAM_STEP)
    delta = -ADAM_LR * (m_hat / (jnp.sqrt(v_hat) + ADAM_EPS) + ADAM_WD * w)
    return delta, m, v


def _adamw(w, g, m, v, name):
    r, cols = w.shape
    tr = r // 8 if r % 64 == 0 else r

    def body(w_ref, g_ref, m_ref, v_ref, d_ref, nm_ref, nv_ref):
        d_ref[...], nm_ref[...], nv_ref[...] = _adamw_math(w_ref[...], g_ref[...], m_ref[...], v_ref[...])

    tile = pl.BlockSpec((tr, cols), lambda i: (i, 0))
    shape = jax.ShapeDtypeStruct((r, cols), F32)
    return pl.pallas_call(
        body, name=name, grid=(r // tr,),
        in_specs=[tile] * 4, out_specs=[tile] * 3, out_shape=[shape] * 3,
        compiler_params=_params(1),
    )(w, g, m, v)


def _matmul_nn(a, w, tn, out_dtype, name):
    s_len, k = a.shape
    n = w.shape[1]
    tm = TOKEN_TILE

    def body(a_ref, w_ref, o_ref):
        o_ref[...] = _dot(a_ref[...], w_ref[...]).astype(o_ref.dtype)

    return pl.pallas_call(
        body, name=name, grid=(n // tn, s_len // tm),
        in_specs=[pl.BlockSpec((tm, k), lambda j, i: (i, 0)), pl.BlockSpec((k, tn), lambda j, i: (0, j))],
        out_specs=pl.BlockSpec((tm, tn), lambda j, i: (i, j)),
        out_shape=jax.ShapeDtypeStruct((s_len, n), out_dtype),
        compiler_params=_params(2),
    )(a, w)


def _matmul_nt(a, w, tn, add, name):
    s_len, n = a.shape
    k = w.shape[0]
    tm = TOKEN_TILE
    steps = n // tn

    def body(*refs):
        if add is None:
            a_ref, w_ref, o_ref, acc_ref = refs
        else:
            a_ref, w_ref, add_ref, o_ref, acc_ref = refs
        j = pl.program_id(1)

        @pl.when(j == 0)
        def _():
            acc_ref[...] = jnp.zeros_like(acc_ref) if add is None else add_ref[...]

        acc_ref[...] += _dot_nt(a_ref[...], w_ref[...])

        @pl.when(j == steps - 1)
        def _():
            o_ref[...] = acc_ref[...]

    tile = pl.BlockSpec((tm, k), lambda i, j: (i, 0))
    in_specs = [pl.BlockSpec((tm, tn), lambda i, j: (i, j)), pl.BlockSpec((k, tn), lambda i, j: (0, j))]
    operands = [a, w]
    if add is not None:
        in_specs.append(tile)
        operands.append(add)
    return pl.pallas_call(
        body, name=name, grid=(s_len // tm, steps),
        in_specs=in_specs, out_specs=tile,
        out_shape=jax.ShapeDtypeStruct((s_len, k), F32),
        scratch_shapes=[pltpu.VMEM((tm, k), F32)],
        compiler_params=_params(2),
    )(*operands)


def _wgrad(x, y, x_spec, y_spec, out_shape, out_spec, acc_shape, n_chunks, name):
    s_len = x.shape[-2]
    ts = TOKEN_TILE
    steps = s_len // ts

    def body(x_ref, y_ref, o_ref, acc_ref):
        s = pl.program_id(1)

        @pl.when(s == 0)
        def _():
            acc_ref[...] = jnp.zeros_like(acc_ref)

        acc_ref[...] += _dot_tn(x_ref[...], y_ref[...])

        @pl.when(s == steps - 1)
        def _():
            o_ref[...] = acc_ref[...].astype(o_ref.dtype)

    return pl.pallas_call(
        body, name=name, grid=(n_chunks, steps),
        in_specs=[x_spec(ts), y_spec(ts)], out_specs=out_spec,
        out_shape=jax.ShapeDtypeStruct(out_shape, BF16),
        scratch_shapes=[pltpu.VMEM(acc_shape, F32)],
        compiler_params=_params(2),
    )(x, y)


def _ffn_fwd(x, h, mod, w_gate, w_up, w_down, name):
    s_len, d = x.shape
    n_chunks, _, fs = w_gate.shape
    tm = TOKEN_TILE

    def body(x_ref, h_ref, mod_ref, wg_ref, wu_ref, wd_ref, xo_ref, g_ref, u_ref, y_ref, acc_ref):
        j = pl.program_id(1)

        @pl.when(j == 0)
        def _():
            acc_ref[...] = jnp.zeros_like(acc_ref)

        hv = h_ref[...]
        g = _dot(hv, wg_ref[...])
        u = _dot(hv, wu_ref[...])
        g_ref[...] = g.astype(BF16)
        u_ref[...] = u.astype(BF16)
        act = (g * _sigmoid(g)) * u
        acc_ref[...] += _dot(act.astype(BF16), wd_ref[...])

        @pl.when(j == n_chunks - 1)
        def _():
            yv = acc_ref[...]
            y_ref[...] = yv
            xo_ref[...] = x_ref[...] + 0.5 * mod_ref[2:3, :] * yv

    tile = pl.BlockSpec((tm, d), lambda i, j: (i, 0))
    hid = pl.BlockSpec((None, tm, fs), lambda i, j: (j, i, 0))
    w_in_spec = pl.BlockSpec((None, d, fs), lambda i, j: (j, 0, 0))
    return pl.pallas_call(
        body, name=name, grid=(s_len // tm, n_chunks),
        in_specs=[tile, tile, pl.BlockSpec((8, d), lambda i, j: (0, 0)), w_in_spec, w_in_spec,
                  pl.BlockSpec((None, fs, d), lambda i, j: (j, 0, 0))],
        out_specs=[tile, hid, hid, tile],
        out_shape=[jax.ShapeDtypeStruct((s_len, d), F32),
                   jax.ShapeDtypeStruct((n_chunks, s_len, fs), BF16),
                   jax.ShapeDtypeStruct((n_chunks, s_len, fs), BF16),
                   jax.ShapeDtypeStruct((s_len, d), F32)],
        scratch_shapes=[pltpu.VMEM((tm, d), F32)],
        compiler_params=_params(2),
    )(x, h, mod, w_gate, w_up, w_down)


def _ffn_bwd(dxo, mod, g_pre, u_pre, w_gate, w_up, w_down, name):
    s_len, d = dxo.shape
    n_chunks, _, fs = w_gate.shape
    tm = TOKEN_TILE

    def body(dxo_ref, mod_ref, g_ref, u_ref, wg_ref, wu_ref, wd_ref, dh_ref, dg_ref, du_ref, a_ref, dy_ref, acc_ref):
        j = pl.program_id(1)

        @pl.when(j == 0)
        def _():
            dy_ref[...] = (0.5 * mod_ref[2:3, :] * dxo_ref[...]).astype(BF16)
            acc_ref[...] = jnp.zeros_like(acc_ref)

        da = _dot_nt(dy_ref[...], wd_ref[...])
        g = g_ref[...].astype(F32)
        u = u_ref[...].astype(F32)
        sg = _sigmoid(g)
        silu = g * sg
        dg = (da * u * (sg * (1.0 + g * (1.0 - sg)))).astype(BF16)
        du = (da * silu).astype(BF16)
        dg_ref[...] = dg
        du_ref[...] = du
        a_ref[...] = (silu * u).astype(BF16)
        acc_ref[...] += _dot_nt(dg, wg_ref[...]) + _dot_nt(du, wu_ref[...])

        @pl.when(j == n_chunks - 1)
        def _():
            dh_ref[...] = acc_ref[...]

    tile = pl.BlockSpec((tm, d), lambda i, j: (i, 0))
    hid = pl.BlockSpec((None, tm, fs), lambda i, j: (j, i, 0))
    w_in_spec = pl.BlockSpec((None, d, fs), lambda i, j: (j, 0, 0))
    hid_shape = jax.ShapeDtypeStruct((n_chunks, s_len, fs), BF16)
    return pl.pallas_call(
        body, name=name, grid=(s_len // tm, n_chunks),
        in_specs=[tile, pl.BlockSpec((8, d), lambda i, j: (0, 0)), hid, hid, w_in_spec, w_in_spec,
                  pl.BlockSpec((None, fs, d), lambda i, j: (j, 0, 0))],
        out_specs=[tile, hid, hid, hid, tile],
        out_shape=[jax.ShapeDtypeStruct((s_len, d), F32), hid_shape, hid_shape, hid_shape,
                   jax.ShapeDtypeStruct((s_len, d), BF16)],
        scratch_shapes=[pltpu.VMEM((tm, d), F32)],
        compiler_params=_params(2),
    )(dxo, mod, g_pre, u_pre, w_gate, w_up, w_down)


def _ffn_wgrads(h, dg, du, act, dy, tag):
    n_chunks, s_len, fs = dg.shape
    d = h.shape[1]
    tok = lambda ts: pl.BlockSpec((ts, d), lambda c, s: (s, 0))
    hid = lambda ts: pl.BlockSpec((None, ts, fs), lambda c, s: (c, s, 0))
    d_up = pl.BlockSpec((None, d, fs), lambda c, s: (c, 0, 0))
    d_down = pl.BlockSpec((None, fs, d), lambda c, s: (c, 0, 0))
    dwg = _wgrad(h, dg, tok, hid, (n_chunks, d, fs), d_up, (d, fs), n_chunks, tag + "_dwg")
    dwu = _wgrad(h, du, tok, hid, (n_chunks, d, fs), d_up, (d, fs), n_chunks, tag + "_dwu")
    dwd = _wgrad(act, dy, hid, tok, (n_chunks, fs, d), d_down, (fs, d), n_chunks, tag + "_dwd")
    return dwg, dwu, dwd


def _band_mask(first):
    qi = lax.broadcasted_iota(jnp.int32, (ATTN_BLOCK, 2 * ATTN_BLOCK), 0)
    kj = lax.broadcasted_iota(jnp.int32, (ATTN_BLOCK, 2 * ATTN_BLOCK), 1)
    lo = jnp.where(first, jnp.maximum(qi, ATTN_BLOCK), qi)
    return (kj >= lo) & (kj <= qi + ATTN_BLOCK)


def _rows(base, count, stride):
    return pl.ds(base, count) if stride == 1 else pl.ds(base, count, stride=stride)


def _load_qkv(cur_ref, prev_ref, qn_ref, kn_ref, qs, kb, vb, n):
    e = HEAD_DIM
    q = cur_ref[:, 0:e]
    qs[...] = (q * _rms(q)) * qn_ref[...]
    k = cur_ref[:, e:2 * e]
    kb[SLAB:2 * SLAB, :] = (k * _rms(k)) * kn_ref[...]
    vb[SLAB:2 * SLAB, :] = cur_ref[:, 2 * e:3 * e]

    @pl.when(n > 0)
    def _():
        kp = prev_ref[:, e:2 * e]
        kb[0:SLAB, :] = (kp * _rms(kp)) * kn_ref[...]
        vb[0:SLAB, :] = prev_ref[:, 2 * e:3 * e]

    @pl.when(n == 0)
    def _():
        kb[0:SLAB, :] = jnp.zeros((SLAB, e), F32)
        vb[0:SLAB, :] = jnp.zeros((SLAB, e), F32)


def _for_each_tile(dil, n, tile_fn):
    span = ATTN_BLOCK * dil

    def sub(jj, carry):
        start = pl.multiple_of(jj * span, ATTN_BLOCK)
        first = jnp.logical_and(n == 0, jj == 0)
        for r in range(dil):
            tile_fn(_rows(start + r, ATTN_BLOCK, dil), _rows(SLAB - span + start + r, 2 * ATTN_BLOCK, dil), first)
        return carry

    lax.fori_loop(0, SLAB // span, sub, 0)


def _attn_fwd(qkv, q_norm, k_norm, name):
    s_len = qkv.shape[0]
    e = HEAD_DIM
    n_slabs = s_len // SLAB

    def body(cur_ref, prev_ref, qn_ref, kn_ref, o_ref, lse_ref, qs, kb, vb, m_s, l_s, acc_s):
        n, grp = pl.program_id(1), pl.program_id(2)
        _load_qkv(cur_ref, prev_ref, qn_ref, kn_ref, qs, kb, vb, n)

        def run(gi, dil):
            def tile(q_rows, kv_rows, first):
                q = qs[q_rows, :].astype(BF16)
                k = kb[kv_rows, :].astype(BF16)
                v = vb[kv_rows, :].astype(BF16)
                s = jnp.where(_band_mask(first), _dot_nt(q, k) * ATTN_SCALE, NEG)
                m = jnp.max(s, axis=-1, keepdims=True)
                p = jnp.exp(s - m)
                m_s.at[gi][q_rows, :] = jnp.broadcast_to(m, (ATTN_BLOCK, e))
                l_s.at[gi][q_rows, :] = jnp.broadcast_to(jnp.sum(p, axis=-1, keepdims=True), (ATTN_BLOCK, e))
                acc_s.at[gi][q_rows, :] = _dot(p.astype(BF16), v)

            _for_each_tile(dil, n, tile)

        for gi, dil in enumerate(DILATIONS):
            pl.when(grp == gi)(lambda gi=gi, dil=dil: run(gi, dil))

        @pl.when(grp == N_GROUPS - 1)
        def _():
            m_all = jnp.maximum(jnp.maximum(m_s[0], m_s[1]), m_s[2])
            den = jnp.zeros((SLAB, e), F32)
            num = jnp.zeros((SLAB, e), F32)
            for gi in range(N_GROUPS):
                w = jnp.exp(m_s[gi] - m_all)
                den += l_s[gi] * w
                num += acc_s[gi] * w
            o_ref[...] = num / den
            lse_ref[...] = m_all + jnp.log(den)

    col = lambda h, g: h * N_GROUPS + g
    small = pl.BlockSpec((1, e), lambda h, n, g: (0, 0))
    out = pl.BlockSpec((SLAB, e), lambda h, n, g: (n, h))
    return pl.pallas_call(
        body, name=name, grid=(HEADS, n_slabs, N_GROUPS),
        in_specs=[pl.BlockSpec((SLAB, 3 * e), lambda h, n, g: (n, col(h, g))),
                  pl.BlockSpec((SLAB, 3 * e), lambda h, n, g: (jnp.maximum(n - 1, 0), col(h, g))),
                  small, small],
        out_specs=[out, out],
        out_shape=[jax.ShapeDtypeStruct((s_len, HEADS * e), F32)] * 2,
        scratch_shapes=[pltpu.VMEM((SLAB, e), F32), pltpu.VMEM((2 * SLAB, e), F32), pltpu.VMEM((2 * SLAB, e), F32),
                        pltpu.VMEM((N_GROUPS, SLAB, e), F32), pltpu.VMEM((N_GROUPS, SLAB, e), F32),
                        pltpu.VMEM((N_GROUPS, SLAB, e), F32)],
        compiler_params=_params(3),
    )(qkv, qkv, q_norm, k_norm)


def _attn_bwd(qkv, d_out, out, lse, q_norm, k_norm, name):
    s_len = qkv.shape[0]
    e = HEAD_DIM
    n_slabs = s_len // SLAB

    def body(cur_ref, prev_ref, do_ref, o_ref, lse_ref, qn_ref, kn_ref, dqkv_ref, st_ref,
             qs, kb, vb, dqs, dkb, dvb, carry):
        head, step, grp = pl.program_id(0), pl.program_id(1), pl.program_id(2)
        n = n_slabs - 1 - step
        _load_qkv(cur_ref, prev_ref, qn_ref, kn_ref, qs, kb, vb, n)
        dkb[...] = jnp.zeros_like(dkb)
        dvb[...] = jnp.zeros_like(dvb)

        @pl.when((head == 0) & (step == 0) & (grp == 0))
        def _():
            st_ref[...] = jnp.zeros_like(st_ref)

        def run(gi, dil):
            @pl.when(step == 0)
            def _():
                carry[gi] = jnp.zeros((2, SLAB, e), F32)

            def tile(q_rows, kv_rows, first):
                q = qs[q_rows, :].astype(BF16)
                k = kb[kv_rows, :].astype(BF16)
                v = vb[kv_rows, :].astype(BF16)
                do = do_ref[q_rows, :]
                delta = jnp.sum(do * o_ref[q_rows, :], axis=-1, keepdims=True)
                s = jnp.where(_band_mask(first), _dot_nt(q, k) * ATTN_SCALE, NEG)
                p = jnp.exp(s - lse_ref[q_rows, :][:, 0:1])
                do16 = do.astype(BF16)
                ds = (p * (_dot_nt(do16, v) - delta) * ATTN_SCALE).astype(BF16)
                dqs[q_rows, :] = _dot(ds, k)
                dkb[kv_rows, :] += _dot_tn(ds, q)
                dvb[kv_rows, :] += _dot_tn(p.astype(BF16), do16)

            _for_each_tile(dil, n, tile)
            dk_hat = dkb[SLAB:2 * SLAB, :] + carry[gi, 0]
            dv = dvb[SLAB:2 * SLAB, :] + carry[gi, 1]
            carry[gi, 0] = dkb[0:SLAB, :]
            carry[gi, 1] = dvb[0:SLAB, :]

            def norm_bwd(raw, gain, d_hat):
                r = _rms(raw)
                y = raw * r
                dy = d_hat * gain
                return r * (dy - y * jnp.mean(dy * y, axis=-1, keepdims=True)), jnp.sum(d_hat * y, axis=0, keepdims=True)

            dq, dqn = norm_bwd(cur_ref[:, 0:e], qn_ref[...], dqs[...])
            dk, dkn = norm_bwd(cur_ref[:, e:2 * e], kn_ref[...], dk_hat)
            dqkv_ref[:, 0:e] = dq.astype(BF16)
            dqkv_ref[:, e:2 * e] = dk.astype(BF16)
            dqkv_ref[:, 2 * e:3 * e] = dv.astype(BF16)
            st_ref[0:1, :] += dqn
            st_ref[1:2, :] += dkn

        for gi, dil in enumerate(DILATIONS):
            pl.when(grp == gi)(lambda gi=gi, dil=dil: run(gi, dil))

    col = lambda h, g: h * N_GROUPS + g
    slab_of = lambda s: n_slabs - 1 - s
    small = pl.BlockSpec((1, e), lambda h, s, g: (0, 0))
    head_blk = pl.BlockSpec((SLAB, e), lambda h, s, g: (slab_of(s), h))
    return pl.pallas_call(
        body, name=name, grid=(HEADS, n_slabs, N_GROUPS),
        in_specs=[pl.BlockSpec((SLAB, 3 * e), lambda h, s, g: (slab_of(s), col(h, g))),
                  pl.BlockSpec((SLAB, 3 * e), lambda h, s, g: (jnp.maximum(slab_of(s) - 1, 0), col(h, g))),
                  head_blk, head_blk, head_blk, small, small],
        out_specs=[pl.BlockSpec((SLAB, 3 * e), lambda h, s, g: (slab_of(s), col(h, g))),
                   pl.BlockSpec((8, e), lambda h, s, g: (0, 0))],
        out_shape=[jax.ShapeDtypeStruct((s_len, 3 * QKV), BF16), jax.ShapeDtypeStruct((8, e), F32)],
        scratch_shapes=[pltpu.VMEM((SLAB, e), F32), pltpu.VMEM((2 * SLAB, e), F32), pltpu.VMEM((2 * SLAB, e), F32),
                        pltpu.VMEM((SLAB, e), F32), pltpu.VMEM((2 * SLAB, e), F32), pltpu.VMEM((2 * SLAB, e), F32),
                        pltpu.VMEM((N_GROUPS, 2, SLAB, e), F32)],
        compiler_params=_params(3),
    )(qkv, qkv, d_out, out, lse, q_norm, k_norm)


def _shift_rows(x, by, edge, forward):
    t_len = x.shape[0]
    row = lax.broadcasted_iota(jnp.int32, x.shape, 0)
    if forward:
        out = pltpu.roll(x, by, 0)
        for i in range(by):
            out = jnp.where(row == i, edge[8 - by + i:8 - by + i + 1, :], out)
    else:
        out = pltpu.roll(x, t_len - by, 0)
        for i in range(by):
            out = jnp.where(row == t_len - by + i, edge[i:i + 1, :], out)
    return out


def _mix_fwd(x, o, rest, mod, conv_w, w_attn, w_conv, w_out, name):
    s_len, d = x.shape
    tm = MIX_TILE
    a_w = o.shape[1]

    def body(x_ref, o_ref, u_ref, b_ref, c_ref, ga_ref, gc_ref, mod_ref, cw_ref, wa_ref, wc_ref, wo_ref,
             xo_ref, z_ref, ya_ref, yc_ref, conv_ref, yb_ref, m_ref, o16_ref, carry):
        @pl.when(pl.program_id(0) == 0)
        def _():
            carry[...] = jnp.zeros_like(carry)

        xc = c_ref[...] * u_ref[...]
        edge = carry[...]
        conv = (_shift_rows(xc, 2, edge, True) * cw_ref[0:1, :] + _shift_rows(xc, 1, edge, True) * cw_ref[1:2, :]
                + xc * cw_ref[2:3, :])
        carry[...] = xc[tm - 8:tm, :]
        yb = (b_ref[...] * conv).astype(BF16)
        o16 = o_ref[...].astype(BF16)
        ya = _dot(o16, wa_ref[...])
        yc = _dot(yb, wc_ref[...])
        merged = (_sigmoid(ga_ref[...]) * ya + _sigmoid(gc_ref[...]) * yc).astype(BF16)
        z = _dot(merged, wo_ref[...])
        xo_ref[...] = x_ref[...] + mod_ref[2:3, :] * z
        z_ref[...] = z
        ya_ref[...] = ya.astype(BF16)
        yc_ref[...] = yc.astype(BF16)
        conv_ref[...] = conv.astype(BF16)
        yb_ref[...] = yb
        m_ref[...] = merged
        o16_ref[...] = o16

    tile = pl.BlockSpec((tm, d), lambda i: (i, 0))
    sect = lambda k: pl.BlockSpec((tm, d), lambda i: (i, k))
    att = pl.BlockSpec((tm, a_w), lambda i: (i, 0))
    const = lambda shape: pl.BlockSpec(shape, lambda i: (0, 0))
    f32_out = jax.ShapeDtypeStruct((s_len, d), F32)
    b16_out = jax.ShapeDtypeStruct((s_len, d), BF16)
    return pl.pallas_call(
        body, name=name, grid=(s_len // tm,),
        in_specs=[tile, att, sect(0), sect(1), sect(2), sect(3), sect(4), const((8, d)), const((8, d)),
                  const((a_w, d)), const((d, d)), const((d, d))],
        out_specs=[tile] * 7 + [att],
        out_shape=[f32_out, f32_out] + [b16_out] * 5 + [jax.ShapeDtypeStruct((s_len, a_w), BF16)],
        scratch_shapes=[pltpu.VMEM((8, d), F32)],
        compiler_params=_params(1),
    )(x, o, rest, rest, rest, rest, rest, mod, conv_w, w_attn, w_conv, w_out)


def _mix_bwd(dxo, ya, yc, conv, rest, mod, conv_w, w_attn, w_conv, w_out, a_w, name):
    s_len, d = dxo.shape
    tm = MIX_TILE
    n_tiles = s_len // tm

    def body(dxo_ref, ya_ref, yc_ref, conv_ref, u_ref, b_ref, c_ref, ga_ref, gc_ref, mod_ref, cw_ref,
             wa_ref, wc_ref, wo_ref, do_ref, drest_ref, dz_ref, dya_ref, dyc_ref, st_ref, carry):
        @pl.when(pl.program_id(0) == 0)
        def _():
            carry[...] = jnp.zeros_like(carry)
            st_ref[...] = jnp.zeros_like(st_ref)

        dz = (mod_ref[2:3, :] * dxo_ref[...]).astype(BF16)
        dz_ref[...] = dz
        dm = _dot_nt(dz, wo_ref[...])
        sa, sc = _sigmoid(ga_ref[...]), _sigmoid(gc_ref[...])
        dya = (dm * sa).astype(BF16)
        dyc = (dm * sc).astype(BF16)
        dya_ref[...] = dya
        dyc_ref[...] = dyc
        drest_ref[:, 3 * d:4 * d] = (dm * ya_ref[...].astype(F32) * (sa * (1.0 - sa))).astype(BF16)
        drest_ref[:, 4 * d:5 * d] = (dm * yc_ref[...].astype(F32) * (sc * (1.0 - sc))).astype(BF16)
        do_ref[...] = _dot_nt(dya, wa_ref[...])
        dyb = _dot_nt(dyc, wc_ref[...])
        drest_ref[:, d:2 * d] = (dyb * conv_ref[...].astype(F32)).astype(BF16)
        dconv = dyb * b_ref[...]
        edge = carry[...]
        sh1 = _shift_rows(dconv, 1, edge, False)
        sh2 = _shift_rows(dconv, 2, edge, False)
        carry[...] = dconv[0:8, :]
        dxc = dconv * cw_ref[2:3, :] + sh1 * cw_ref[1:2, :] + sh2 * cw_ref[0:1, :]
        u, c = u_ref[...], c_ref[...]
        xc = c * u
        drest_ref[:, 0:d] = (dxc * c).astype(BF16)
        drest_ref[:, 2 * d:3 * d] = (dxc * u).astype(BF16)
        st_ref[0:1, :] += jnp.sum(xc * sh2, axis=0, keepdims=True)
        st_ref[1:2, :] += jnp.sum(xc * sh1, axis=0, keepdims=True)
        st_ref[2:3, :] += jnp.sum(xc * dconv, axis=0, keepdims=True)

    rev = lambda i: n_tiles - 1 - i
    tile = pl.BlockSpec((tm, d), lambda i: (rev(i), 0))
    sect = lambda k: pl.BlockSpec((tm, d), lambda i: (rev(i), k))
    const = lambda shape: pl.BlockSpec(shape, lambda i: (0, 0))
    b16_out = jax.ShapeDtypeStruct((s_len, d), BF16)
    return pl.pallas_call(
        body, name=name, grid=(n_tiles,),
        in_specs=[tile, tile, tile, tile, sect(0), sect(1), sect(2), sect(3), sect(4), const((8, d)), const((8, d)),
                  const((a_w, d)), const((d, d)), const((d, d))],
        out_specs=[pl.BlockSpec((tm, a_w), lambda i: (rev(i), 0)), pl.BlockSpec((tm, 5 * d), lambda i: (rev(i), 0)),
                   tile, tile, tile, const((8, d))],
        out_shape=[jax.ShapeDtypeStruct((s_len, a_w), F32), jax.ShapeDtypeStruct((s_len, 5 * d), BF16),
                   b16_out, b16_out, b16_out, jax.ShapeDtypeStruct((8, d), F32)],
        scratch_shapes=[pltpu.VMEM((8, d), F32)],
        compiler_params=_params(1),
    )(dxo, ya, yc, conv, rest, rest, rest, rest, rest, mod, conv_w, w_attn, w_conv, w_out)


ADA_COLS = 128


def _ada_fwd(c_all, w_shard, b_shard, name):
    d, cols = w_shard.shape

    def body(c_ref, w_ref, b_ref, o_ref):
        cv = c_ref[...]
        o_ref[...] = jnp.dot(cv * _sigmoid(cv), w_ref[...], preferred_element_type=F32,
                             precision=lax.Precision.HIGHEST) + b_ref[...]

    return pl.pallas_call(
        body, name=name, grid=(cols // ADA_COLS,),
        in_specs=[pl.BlockSpec((8, d), lambda j: (0, 0)), pl.BlockSpec((d, ADA_COLS), lambda j: (0, j)),
                  pl.BlockSpec((1, ADA_COLS), lambda j: (0, j))],
        out_specs=pl.BlockSpec((8, ADA_COLS), lambda j: (0, j)),
        out_shape=jax.ShapeDtypeStruct((8, cols), F32),
        compiler_params=_params(1),
    )(c_all, w_shard, b_shard)


def _ada_bwd(c_all, dmod_shard, w, m, v, name):
    d, cols = w.shape

    def body(c_ref, dm_ref, w_ref, m_ref, v_ref, g_ref, d_ref, nm_ref, nv_ref):
        cv = c_ref[...]
        g = lax.dot_general(cv * _sigmoid(cv), dm_ref[...], (((0,), (0,)), ((), ())),
                            preferred_element_type=F32, precision=lax.Precision.HIGHEST)
        g_ref[...] = g
        d_ref[...], nm_ref[...], nv_ref[...] = _adamw_math(w_ref[...], g, m_ref[...], v_ref[...])

    blk = pl.BlockSpec((d, ADA_COLS), lambda j: (0, j))
    shape = jax.ShapeDtypeStruct((d, cols), F32)
    return pl.pallas_call(
        body, name=name, grid=(cols // ADA_COLS,),
        in_specs=[pl.BlockSpec((8, d), lambda j: (0, 0)), pl.BlockSpec((8, ADA_COLS), lambda j: (0, j)), blk, blk, blk],
        out_specs=[blk] * 4, out_shape=[shape] * 4,
        compiler_params=_params(1),
    )(c_all, dmod_shard, w, m, v)


def _small_update(parts, w, m, v, name):
    n = w.shape[1]

    def body(p_ref, w_ref, m_ref, v_ref, g_ref, d_ref, nm_ref, nv_ref):
        g = p_ref[0:1, :]
        for i in range(1, 8):
            g = g + p_ref[i:i + 1, :]
        g_ref[...] = g
        d_ref[...], nm_ref[...], nv_ref[...] = _adamw_math(w_ref[...], g, m_ref[...], v_ref[...])

    shape = jax.ShapeDtypeStruct((1, n), F32)
    return pl.pallas_call(body, name=name, out_shape=[shape] * 4, compiler_params=_params())(parts, w, m, v)


def _qkv_perm(w):
    k = w.shape[0]
    return w.reshape(k, 3, N_GROUPS, HEADS, HEAD_DIM).transpose(0, 3, 2, 1, 4).reshape(k, 3 * QKV)


def _qkv_unperm(w):
    k = w.shape[0]
    return w.reshape(k, HEADS, N_GROUPS, 3, HEAD_DIM).transpose(0, 3, 2, 1, 4).reshape(k, 3 * QKV)


def _cols_from_shards(w):
    n, r, c = w.shape
    return w.transpose(1, 0, 2).reshape(r, n * c)


def _cols_to_shards(w, n):
    r, nc = w.shape
    return w.reshape(r, n, nc // n).transpose(1, 0, 2)


def kernel(x, c, w_ada, b_ada, norm_ffn1, ffn1_w_gate, ffn1_w_up, ffn1_w_down, norm_mix, w_in, q_norm, k_norm, conv_w, w_attn_branch, w_conv_branch, w_out, norm_ffn2, ffn2_w_gate, ffn2_w_up, ffn2_w_down, loss_target, m_w_ada, m_b_ada, m_norm_ffn1, m_ffn1_w_gate, m_ffn1_w_up, m_ffn1_w_down, m_norm_mix, m_w_in, m_q_norm, m_k_norm, m_conv_w, m_w_attn_branch, m_w_conv_branch, m_w_out, m_norm_ffn2, m_ffn2_w_gate, m_ffn2_w_up, m_ffn2_w_down, v_w_ada, v_b_ada, v_norm_ffn1, v_ffn1_w_gate, v_ffn1_w_up, v_ffn1_w_down, v_norm_mix, v_w_in, v_q_norm, v_k_norm, v_conv_w, v_w_attn_branch, v_w_conv_branch, v_w_out, v_norm_ffn2, v_ffn2_w_gate, v_ffn2_w_up, v_ffn2_w_down):
    ix, iy, ic = _place()
    chip = 2 * ix + iy
    me = 4 * ix + 2 * iy + ic
    xs = x[0]
    target = loss_target[0]
    s_len, d = xs.shape
    ada_cols = w_ada.shape[2]
    conv_cols = conv_w.shape[2]

    conv_rows = jnp.zeros((8, conv_cols), F32).at[0:3].set(conv_w[0])
    small_in = jnp.concatenate([jnp.broadcast_to(c, (8, d)), conv_rows], axis=1)
    small_all = _allgather8(small_in, "gather_c").reshape(8, 8, d + conv_cols)
    c_all = small_all[:, 0, :d]
    conv_full = small_all[0::2, 0:3, d:].transpose(1, 0, 2).reshape(3, N_CHIPS * conv_cols)
    conv_pad = jnp.zeros((8, N_CHIPS * conv_cols), F32).at[0:3].set(conv_full)
    b_shard = lax.dynamic_slice(b_ada, (0, chip * ada_cols), (1, ada_cols))
    mod_part = _ada_fwd(c_all, w_ada[0], b_shard, "ada_fwd")
    mod_all = _allgather8(mod_part, "gather_mod").reshape(N_CHIPS, 2, 8, ada_cols)[:, 0]
    mod_mine = lax.dynamic_slice(mod_all, (0, me, 0), (N_CHIPS, 1, ada_cols)).reshape(9, d)

    def mod_rows(i, gain):
        return jnp.zeros((8, d), F32).at[0:3].set(mod_mine[3 * i:3 * i + 3]).at[3:4].set(gain)

    mod1, mod2, mod3 = mod_rows(0, norm_ffn1), mod_rows(1, norm_mix), mod_rows(2, norm_ffn2)

    to16 = lambda w: w[0].astype(BF16)
    wg1, wu1, wd1 = _gather_weights([to16(ffn1_w_gate), to16(ffn1_w_up), to16(ffn1_w_down)], "gather_ffn1", 1)
    (w_in_g,) = _gather_weights([to16(w_in)], "gather_w_in", 2)
    w_ab_g, w_cb_g, w_o_g, wg2, wu2, wd2 = _gather_weights(
        [to16(w_attn_branch), to16(w_conv_branch), to16(w_out),
         to16(ffn2_w_gate), to16(ffn2_w_up), to16(ffn2_w_down)], "gather_rest", 3)
    w_in_full = _cols_from_shards(w_in_g)
    w_qkv = _qkv_perm(w_in_full[:, :3 * QKV])
    w_rest = w_in_full[:, 3 * QKV:]
    w_ab = _cols_from_shards(w_ab_g)
    a_w = w_ab.shape[0]
    w_cb = w_cb_g.reshape(d, d)
    w_o = w_o_g.reshape(d, d)

    h1 = _norm_mod(xs, mod1, "norm1")
    x1, g1, u1, y1 = _ffn_fwd(xs, h1, mod1, wg1, wu1, wd1, "ffn1_fwd")
    h2 = _norm_mod(x1, mod2, "norm2")
    qkv = _matmul_nn(h2, w_qkv, 3 * HEAD_DIM * N_GROUPS, F32, "in_proj_qkv")
    rest = _matmul_nn(h2, w_rest, d, F32, "in_proj_rest")
    o, lse = _attn_fwd(qkv, q_norm, k_norm, "attn_fwd")
    x2, z, ya, yc, conv, yb, merged, o16 = _mix_fwd(x1, o, rest, mod2, conv_pad, w_ab, w_cb, w_o, "mix_fwd")
    h3 = _norm_mod(x2, mod3, "norm3")
    x3, g3, u3, y3 = _ffn_fwd(x2, h3, mod3, wg2, wu2, wd2, "ffn2_fwd")
    dx3, loss_part = _loss_grad(x3, target, "loss")
    loss = lax.psum(0.5 * jnp.sum(loss_part) / d, ("x", "y", "c"))

    dh3, dg3, du3, a3, dy3 = _ffn_bwd(dx3, mod3, g3, u3, wg2, wu2, wd2, "ffn2_bwd")
    dx2, st3 = _norm_bwd(dh3, x2, mod3, dx3, y3, 0.5, "norm3_bwd")
    dwg2, dwu2, dwd2 = _ffn_wgrads(h3, dg3, du3, a3, dy3, "ffn2")

    do, drest, dz, dya, dyc, st_conv = _mix_bwd(dx2, ya, yc, conv, rest, mod2, conv_pad, w_ab, w_cb, w_o, a_w, "mix_bwd")
    dqkv, st_qk = _attn_bwd(qkv, do, o, lse, q_norm, k_norm, "attn_bwd")
    dh2 = _matmul_nt(dqkv, w_qkv, 3 * HEAD_DIM * N_GROUPS, None, "in_proj_bwd_qkv")
    dh2 = _matmul_nt(drest, w_rest, d, dh2, "in_proj_bwd_rest")
    dx1, st2 = _norm_bwd(dh2, x1, mod2, dx2, z, 1.0, "norm2_bwd")
    tok = lambda width: (lambda ts: pl.BlockSpec((ts, width), lambda cc, s: (s, 0)))
    colblk = lambda width: (lambda ts: pl.BlockSpec((ts, width), lambda cc, s: (s, cc)))
    qw = 3 * HEAD_DIM * N_GROUPS
    dw_qkv = _wgrad(h2, dqkv, tok(d), colblk(qw), (d, 3 * QKV), pl.BlockSpec((d, qw), lambda cc, s: (0, cc)),
                    (d, qw), 3 * QKV // qw, "dw_in_qkv")
    dw_rest = _wgrad(h2, drest, tok(d), colblk(d), (d, 5 * d), pl.BlockSpec((d, d), lambda cc, s: (0, cc)),
                     (d, d), 5, "dw_in_rest")
    dw_in = _cols_to_shards(jnp.concatenate([_qkv_unperm(dw_qkv), dw_rest], axis=1), N_CHIPS)
    shard_w = d // N_CHIPS
    dw_ab = _wgrad(o16, dya, tok(a_w), colblk(shard_w), (a_w, d), pl.BlockSpec((a_w, shard_w), lambda cc, s: (0, cc)),
                   (a_w, shard_w), N_CHIPS, "dw_attn_branch")
    dw_ab = _cols_to_shards(dw_ab, N_CHIPS)
    row_out = pl.BlockSpec((None, shard_w, d), lambda cc, s: (cc, 0, 0))
    dw_cb = _wgrad(yb, dyc, colblk(shard_w), tok(d), (N_CHIPS, shard_w, d), row_out, (shard_w, d), N_CHIPS, "dw_conv_branch")
    dw_o = _wgrad(merged, dz, colblk(shard_w), tok(d), (N_CHIPS, shard_w, d), row_out, (shard_w, d), N_CHIPS, "dw_out")

    dh1, dg1, du1, a1, dy1 = _ffn_bwd(dx1, mod1, g1, u1, wg1, wu1, wd1, "ffn1_bwd")
    dx0, st1 = _norm_bwd(dh1, xs, mod1, dx1, y1, 0.5, "norm1_bwd")
    dwg1, dwu1, dwd1 = _ffn_wgrads(h1, dg1, du1, a1, dy1, "ffn1")

    grads = [dwg1, dwu1, dwd1, dw_in, dw_ab, dw_cb, dw_o, dwg2, dwu2, dwd2]
    names = ["ffn1_w_gate", "ffn1_w_up", "ffn1_w_down", "w_in", "w_attn_branch", "w_conv_branch", "w_out",
             "ffn2_w_gate", "ffn2_w_up", "ffn2_w_down"]
    c_idx = jnp.reshape(ic, (1,)).astype(jnp.int32)
    chip_idx = jnp.reshape(chip, (1,)).astype(jnp.int32)
    from_sibling = _rs_pair_exchange(grads, "rs_pair")
    pair_sums = [_pair_add(g, r, c_idx, "pair_add_" + nm) for g, r, nm in zip(grads, from_sibling, names)]
    from_chips = _rs_chip_exchange(pair_sums, "rs_chips")
    totals = [_chip_add(p, r, chip_idx, "chip_add_" + nm) for p, r, nm in zip(pair_sums, from_chips, names)]
    shard_grads = dict(zip(names, _rs_share(totals, "rs_share")))

    dmod = jnp.concatenate([st1[0:3], st2[0:3], st3[0:3]], axis=0).reshape(1, 9 * d)
    small = jnp.concatenate([dmod, st1[3:4], st2[3:4], st3[3:4], st_qk[0:1], st_qk[1:2],
                             st_conv[0:3].reshape(1, 3 * d)], axis=1)
    small_all = _allgather8(jnp.broadcast_to(small, (8, small.shape[1])), "gather_small").reshape(8, 8, -1)[:, 0]
    dmod_all = small_all[:, :9 * d]
    dmod_shard = lax.dynamic_slice(dmod_all, (0, chip * ada_cols), (8, ada_cols))
    g_w_ada, d_w_ada, nm_w_ada, nv_w_ada = _ada_bwd(c_all, dmod_shard, w_ada[0], m_w_ada[0], v_w_ada[0], "ada_bwd")

    vec_names = ["b_ada", "norm_ffn1", "norm_mix", "norm_ffn2", "q_norm", "k_norm"]
    vec_w = [b_ada, norm_ffn1, norm_mix, norm_ffn2, q_norm, k_norm]
    vec_m = [m_b_ada, m_norm_ffn1, m_norm_mix, m_norm_ffn2, m_q_norm, m_k_norm]
    vec_v = [v_b_ada, v_norm_ffn1, v_norm_mix, v_norm_ffn2, v_q_norm, v_k_norm]
    n_vec = sum(w.shape[1] for w in vec_w)
    cat = lambda arrs: jnp.concatenate(arrs, axis=1)
    vec_out = _small_update(small_all[:, :n_vec], cat(vec_w), cat(vec_m), cat(vec_v), "small_update")
    conv_parts = small_all[:, n_vec:].reshape(8, 3, N_CHIPS * conv_cols)
    conv_parts = lax.dynamic_slice(conv_parts, (0, 0, chip * conv_cols), (8, 3, conv_cols)).reshape(8, 3 * conv_cols)
    flat3 = lambda w: w[0].reshape(1, 3 * conv_cols)
    conv_out = _small_update(conv_parts, flat3(conv_w), flat3(m_conv_w), flat3(v_conv_w), "conv_update")

    res = {"w_ada": [t[None] for t in (g_w_ada, d_w_ada, nm_w_ada, nv_w_ada)],
           "conv_w": [t.reshape(1, 3, conv_cols) for t in conv_out]}
    off = 0
    for nm, w in zip(vec_names, vec_w):
        width = w.shape[1]
        res[nm] = [t[:, off:off + width] for t in vec_out]
        off += width
    big = {"ffn1_w_gate": (ffn1_w_gate, m_ffn1_w_gate, v_ffn1_w_gate), "ffn1_w_up": (ffn1_w_up, m_ffn1_w_up, v_ffn1_w_up),
           "ffn1_w_down": (ffn1_w_down, m_ffn1_w_down, v_ffn1_w_down), "w_in": (w_in, m_w_in, v_w_in),
           "w_attn_branch": (w_attn_branch, m_w_attn_branch, v_w_attn_branch),
           "w_conv_branch": (w_conv_branch, m_w_conv_branch, v_w_conv_branch), "w_out": (w_out, m_w_out, v_w_out),
           "ffn2_w_gate": (ffn2_w_gate, m_ffn2_w_gate, v_ffn2_w_gate), "ffn2_w_up": (ffn2_w_up, m_ffn2_w_up, v_ffn2_w_up),
           "ffn2_w_down": (ffn2_w_down, m_ffn2_w_down, v_ffn2_w_down)}
    for nm, (w, m, v) in big.items():
        g = shard_grads[nm]
        delta, new_m, new_v = _adamw(w[0], g, m[0], v[0], "adamw_" + nm)
        res[nm] = [t[None] for t in (g, delta, new_m, new_v)]

    order = ["w_ada", "b_ada", "norm_ffn1", "ffn1_w_gate", "ffn1_w_up", "ffn1_w_down", "norm_mix", "w_in", "q_norm",
             "k_norm", "conv_w", "w_attn_branch", "w_conv_branch", "w_out", "norm_ffn2", "ffn2_w_gate", "ffn2_w_up",
             "ffn2_w_down"]
    return (loss, dx0[None], *[res[nm][0] for nm in order], *[res[nm][1] for nm in order],
            *[res[nm][2] for nm in order], *[res[nm][3] for nm in order])
```

```python
import jax
import jax.numpy as jnp
from jax import lax
from jax.experimental import pallas as pl
from jax.experimental.pallas import tpu as pltpu
from jax.experimental.pallas import tpu_sc as plsc

F32 = jnp.float32
BF16 = jnp.bfloat16
MESH = pl.DeviceIdType.MESH
ANY = pl.BlockSpec(memory_space=pl.ANY)

NORM_EPS = 1e-6
HEAD_DIM = 128
N_GROUPS = 3
HEADS = 4
DILATIONS = (1, 4, 16)
ATTN_BLOCK = 128
SLAB = ATTN_BLOCK * max(DILATIONS)
QKV = N_GROUPS * HEADS * HEAD_DIM
ATTN_SCALE = HEAD_DIM ** -0.5
NEG = -1e30
N_CHIPS = 4

ADAM_LR = 0.001
ADAM_B1 = 0.9
ADAM_B2 = 0.999
ADAM_EPS = 1e-08
ADAM_WD = 0.01
ADAM_STEP = 10

VMEM_LIMIT_BYTES = 56 * 1024 * 1024
TOKEN_TILE = 512
PROJ_TILE = 2048
WGRAD_TILE = 2048
IN_BLOCK = 512
MIX_TILE = 256


def _params(n_axes=0):
    return pltpu.CompilerParams(
        dimension_semantics=("arbitrary",) * n_axes if n_axes else None,
        vmem_limit_bytes=VMEM_LIMIT_BYTES)


def _dot(a, b):
    return jnp.dot(a, b, preferred_element_type=F32)


def _dot_nt(a, b):
    return lax.dot_general(a, b, (((1,), (1,)), ((), ())), preferred_element_type=F32)


def _dot_tn(a, b):
    return lax.dot_general(a, b, (((0,), (0,)), ((), ())), preferred_element_type=F32)


def _sigmoid(x):
    return 1.0 / (1.0 + jnp.exp(-x))


def _place():
    return lax.axis_index("x"), lax.axis_index("y"), lax.axis_index("c")


def _allgather8(block, name):
    m_per, n = block.shape

    def body(x_ref, out_ref, send_sems, recv_sems, local_sem):
        x, y, c = _place()
        me, sibling = (x, y, c), (x, y, 1 - c)
        chips = [(1 - x, y), (x, 1 - y), (1 - x, 1 - y)]

        def rows(px, py, pc):
            return out_ref.at[pl.ds((4 * px + 2 * py + pc) * m_per, m_per), :]

        def copy(k, blk, to, src=None):
            return pltpu.make_async_remote_copy(
                src_ref=rows(*blk) if src is None else src, dst_ref=rows(*blk),
                send_sem=send_sems.at[k], recv_sem=recv_sems.at[k],
                device_id=to, device_id_type=MESH)

        mine = pltpu.make_async_copy(x_ref, rows(*me), local_sem)
        mine.start()
        first = [copy(0, me, sibling, src=x_ref)]
        first += [copy(1 + j, me, (*chip, c), src=x_ref) for j, chip in enumerate(chips)]
        for cp in first:
            cp.start()
        passed = [copy(4 + j, (*chip, c), sibling) for j, chip in enumerate(chips)]
        for j, chip in enumerate(chips):
            copy(1 + j, (*chip, c), me).wait_recv()
            passed[j].start()
        copy(0, sibling, me).wait_recv()
        for j, chip in enumerate(chips):
            copy(4 + j, (*chip, 1 - c), me).wait_recv()
        for cp in first + passed:
            cp.wait_send()
        mine.wait()

    return pl.pallas_call(
        body, name=name,
        out_shape=jax.ShapeDtypeStruct((8 * m_per, n), block.dtype),
        in_specs=[pl.BlockSpec(memory_space=pltpu.VMEM)],
        out_specs=pl.BlockSpec(memory_space=pltpu.VMEM),
        scratch_shapes=[pltpu.SemaphoreType.DMA((7,)), pltpu.SemaphoreType.DMA((7,)),
                        pltpu.SemaphoreType.DMA],
        compiler_params=_params(),
    )(block)


def _handshake(peers):
    barrier = pltpu.get_barrier_semaphore()
    for peer in peers:
        pl.semaphore_signal(barrier, inc=1, device_id=peer, device_id_type=MESH)
    pl.semaphore_wait(barrier, len(peers))


def _gather_weights(shards, by_cols, name, collective_id, after=()):
    n_arr = len(shards)

    def body(*refs):
        srcs, outs = refs[:n_arr], refs[n_arr + len(after):2 * n_arr + len(after)]
        send_sems, recv_sems, local_sems = refs[2 * n_arr + len(after):]
        x, y, c = _place()
        me_dev, sibling = (x, y, c), (x, y, 1 - c)
        chips = [(1 - x, y), (x, 1 - y), (1 - x, 1 - y)]
        me = 2 * x + y
        _handshake([sibling] + [(*chip, c) for chip in chips])

        def place(k, chip_idx, rows):
            if by_cols[k]:
                width = srcs[k].shape[1]
                return outs[k].at[rows, pl.ds(pl.multiple_of(chip_idx * width, 128), width)]
            return outs[k].at[chip_idx, rows]

        def copy(k, slot, chip_idx, half_sel, to, from_shard=False):
            half = srcs[k].shape[0] // 2
            rows = pl.ds(half_sel * half, half)
            dst = place(k, chip_idx, rows)
            return pltpu.make_async_remote_copy(
                src_ref=srcs[k].at[rows] if from_shard else dst, dst_ref=dst,
                send_sem=send_sems.at[6 * k + slot], recv_sem=recv_sems.at[6 * k + slot],
                device_id=to, device_id_type=MESH)

        own = [pltpu.make_async_copy(srcs[k], place(k, me, pl.ds(0, srcs[k].shape[0])), local_sems.at[k])
               for k in range(n_arr)]
        for cp in own:
            cp.start()
        sent = []
        for k in range(n_arr):
            for j, chip in enumerate(chips):
                sent.append(copy(k, j, me, c, (*chip, c), from_shard=True))
                sent[-1].start()
        for k in range(n_arr):
            for j, chip in enumerate(chips):
                chip_idx = 2 * chip[0] + chip[1]
                copy(k, j, chip_idx, c, me_dev).wait_recv()
                sent.append(copy(k, 3 + j, chip_idx, c, sibling))
                sent[-1].start()
        for k in range(n_arr):
            for j, chip in enumerate(chips):
                copy(k, 3 + j, 2 * chip[0] + chip[1], 1 - c, me_dev).wait_recv()
        for cp in sent:
            cp.wait_send()
        for cp in own:
            cp.wait()

    def gathered(k):
        r, cols = shards[k].shape
        return (r, N_CHIPS * cols) if by_cols[k] else (N_CHIPS, r, cols)

    return pl.kernel(
        body, name=name,
        out_type=[jax.ShapeDtypeStruct(gathered(k), shards[k].dtype) for k in range(n_arr)],
        mesh=plsc.ScalarSubcoreMesh(axis_name="sequencer", num_cores=1),
        scratch_types=[pltpu.SemaphoreType.DMA((6 * n_arr,)), pltpu.SemaphoreType.DMA((6 * n_arr,)),
                       pltpu.SemaphoreType.DMA((n_arr,))],
        compiler_params=pltpu.CompilerParams(collective_id=collective_id),
    )(*shards, *after)


def _rs_pair_exchange(grads, name):
    n_arr = len(grads)

    def body(*refs):
        srcs, outs = refs[:n_arr], refs[n_arr:2 * n_arr]
        send_sems, recv_sems = refs[2 * n_arr:]
        x, y, c = _place()
        cps = []
        for k in range(n_arr):
            half = srcs[k].shape[1] // 2
            cps.append(pltpu.make_async_remote_copy(
                src_ref=srcs[k].at[:, pl.ds((1 - c) * half, half)], dst_ref=outs[k],
                send_sem=send_sems.at[k], recv_sem=recv_sems.at[k],
                device_id=(x, y, 1 - c), device_id_type=MESH))
            cps[-1].start()
        for cp in cps:
            cp.wait_recv()
        for cp in cps:
            cp.wait_send()

    return pl.pallas_call(
        body, name=name,
        out_shape=[jax.ShapeDtypeStruct((g.shape[0], g.shape[1] // 2, g.shape[2]), g.dtype) for g in grads],
        in_specs=[ANY] * n_arr, out_specs=[ANY] * n_arr,
        scratch_shapes=[pltpu.SemaphoreType.DMA((n_arr,)), pltpu.SemaphoreType.DMA((n_arr,))],
        compiler_params=_params(),
    )(*grads)


def _rs_chip_exchange(sums, name):
    n_arr = len(sums)

    def body(*refs):
        srcs, outs = refs[:n_arr], refs[n_arr:2 * n_arr]
        send_sems, recv_sems = refs[2 * n_arr:]
        x, y, c = _place()
        chips = [(1 - x, y), (x, 1 - y), (1 - x, 1 - y)]
        cps = []
        for k in range(n_arr):
            for j, chip in enumerate(chips):
                cps.append(pltpu.make_async_remote_copy(
                    src_ref=srcs[k].at[2 * chip[0] + chip[1]], dst_ref=outs[k].at[j],
                    send_sem=send_sems.at[3 * k + j], recv_sem=recv_sems.at[3 * k + j],
                    device_id=(*chip, c), device_id_type=MESH))
                cps[-1].start()
        for cp in cps:
            cp.wait_recv()
        for cp in cps:
            cp.wait_send()

    return pl.pallas_call(
        body, name=name,
        out_shape=[jax.ShapeDtypeStruct((3,) + s.shape[1:], s.dtype) for s in sums],
        in_specs=[ANY] * n_arr, out_specs=[ANY] * n_arr,
        scratch_shapes=[pltpu.SemaphoreType.DMA((3 * n_arr,)), pltpu.SemaphoreType.DMA((3 * n_arr,))],
        compiler_params=_params(),
    )(*sums)


def _rs_share(totals, name):
    n_arr = len(totals)

    def body(*refs):
        outs = refs[n_arr:2 * n_arr]
        send_sems, recv_sems = refs[2 * n_arr:]
        x, y, c = _place()

        def half_rows(k, sel):
            return outs[k].at[sel]

        cps = []
        for k in range(n_arr):
            cps.append(pltpu.make_async_remote_copy(
                src_ref=half_rows(k, c), dst_ref=half_rows(k, c), send_sem=send_sems.at[k], recv_sem=recv_sems.at[k],
                device_id=(x, y, 1 - c), device_id_type=MESH))
            cps[-1].start()
        for k in range(n_arr):
            pltpu.make_async_remote_copy(
                src_ref=half_rows(k, c), dst_ref=half_rows(k, 1 - c), send_sem=send_sems.at[k],
                recv_sem=recv_sems.at[k], device_id=(x, y, 1 - c), device_id_type=MESH).wait_recv()
        for cp in cps:
            cp.wait_send()

    shared = pl.pallas_call(
        body, name=name,
        out_shape=[jax.ShapeDtypeStruct(t.shape, t.dtype) for t in totals],
        in_specs=[ANY] * n_arr, out_specs=[ANY] * n_arr,
        input_output_aliases={k: k for k in range(n_arr)},
        scratch_shapes=[pltpu.SemaphoreType.DMA((n_arr,)), pltpu.SemaphoreType.DMA((n_arr,))],
        compiler_params=_params(),
    )(*totals)
    return [t.reshape(2 * t.shape[1], t.shape[2]) for t in shared]


def _pair_add(grad, recv, c_idx, name):
    n, r, cols = grad.shape
    half = r // 2
    rows = half // 2

    def body(_, g_ref, r_ref, o_ref):
        o_ref[...] = (g_ref[...].astype(F32) + r_ref[...].astype(F32)).astype(o_ref.dtype)

    return pl.pallas_call(
        body, name=name,
        grid_spec=pltpu.PrefetchScalarGridSpec(
            num_scalar_prefetch=1, grid=(n, 2),
            in_specs=[pl.BlockSpec((None, None, rows, cols), lambda s, i, ci: (s, ci[0], i, 0)),
                      pl.BlockSpec((None, rows, cols), lambda s, i, ci: (s, i, 0))],
            out_specs=pl.BlockSpec((None, rows, cols), lambda s, i, ci: (s, i, 0))),
        out_shape=jax.ShapeDtypeStruct((n, half, cols), BF16),
        compiler_params=_params(2),
    )(c_idx, grad.reshape(n, 2, half, cols), recv)


def _chip_add(sums, recv, chip_and_core, name):
    _, half, cols = sums.shape
    rows = half // 2

    def body(_, s_ref, r0_ref, r1_ref, r2_ref, o_ref):
        o_ref[...] = ((s_ref[...].astype(F32) + r0_ref[...].astype(F32))
                      + r1_ref[...].astype(F32)) + r2_ref[...].astype(F32)

    def recv_spec(j):
        return pl.BlockSpec((None, rows, cols), lambda i, ci: (j, i, 0))

    return pl.pallas_call(
        body, name=name,
        grid_spec=pltpu.PrefetchScalarGridSpec(
            num_scalar_prefetch=1, grid=(2,),
            in_specs=[pl.BlockSpec((None, rows, cols), lambda i, ci: (ci[0], i, 0)),
                      recv_spec(0), recv_spec(1), recv_spec(2)],
            out_specs=pl.BlockSpec((None, rows, cols), lambda i, ci: (ci[1], i, 0))),
        out_shape=jax.ShapeDtypeStruct((2, half, cols), F32),
        compiler_params=_params(1),
    )(chip_and_core, sums, recv, recv, recv)


def _rms(x):
    return lax.rsqrt(jnp.mean(x * x, axis=-1, keepdims=True) + NORM_EPS)


def _norm_mod(x, mod, name):
    s_len, d = x.shape
    tm = TOKEN_TILE

    def body(x_ref, mod_ref, h_ref):
        xv = x_ref[...]
        n = (xv * _rms(xv)) * mod_ref[3:4, :]
        h_ref[...] = (n * (1.0 + mod_ref[1:2, :]) + mod_ref[0:1, :]).astype(BF16)

    return pl.pallas_call(
        body, name=name, grid=(s_len // tm,),
        in_specs=[pl.BlockSpec((tm, d), lambda i: (i, 0)), pl.BlockSpec((8, d), lambda i: (0, 0))],
        out_specs=pl.BlockSpec((tm, d), lambda i: (i, 0)),
        out_shape=jax.ShapeDtypeStruct((s_len, d), BF16),
        compiler_params=_params(1),
    )(x, mod)


def _norm_bwd(dh, x, mod, dxo, y_raw, coef, name):
    s_len, d = x.shape
    tm = TOKEN_TILE

    def body(dh_ref, x_ref, mod_ref, dxo_ref, y_ref, dx_ref, st_ref):
        @pl.when(pl.program_id(0) == 0)
        def _():
            st_ref[...] = jnp.zeros_like(st_ref)

        xv, dhv, dxov = x_ref[...], dh_ref[...], dxo_ref[...]
        r = _rms(xv)
        xh = xv * r
        gain, scale = mod_ref[3:4, :], mod_ref[1:2, :]
        dn = dhv * (1.0 + scale)
        dxh = dn * gain
        dx_ref[...] = dxov + r * (dxh - xh * jnp.mean(dxh * xh, axis=-1, keepdims=True))
        st_ref[0:1, :] += jnp.sum(dhv, axis=0, keepdims=True)
        st_ref[1:2, :] += jnp.sum(dhv * (xh * gain), axis=0, keepdims=True)
        st_ref[2:3, :] += coef * jnp.sum(y_ref[...] * dxov, axis=0, keepdims=True)
        st_ref[3:4, :] += jnp.sum(dn * xh, axis=0, keepdims=True)

    tile = pl.BlockSpec((tm, d), lambda i: (i, 0))
    small = pl.BlockSpec((8, d), lambda i: (0, 0))
    return pl.pallas_call(
        body, name=name, grid=(s_len // tm,),
        in_specs=[tile, tile, small, tile, tile],
        out_specs=[tile, small],
        out_shape=[jax.ShapeDtypeStruct((s_len, d), F32), jax.ShapeDtypeStruct((8, d), F32)],
        compiler_params=_params(1),
    )(dh, x, mod, dxo, y_raw)


def _loss_grad(y, target, name):
    s_len, d = y.shape
    tm = TOKEN_TILE

    def body(y_ref, t_ref, dy_ref, part_ref):
        @pl.when(pl.program_id(0) == 0)
        def _():
            part_ref[...] = jnp.zeros_like(part_ref)

        err = y_ref[...] - t_ref[...]
        dy_ref[...] = err * (1.0 / d)
        sq = err * err
        part_ref[...] += jnp.sum(sq.reshape(tm // 8, 8, d), axis=0)

    tile = pl.BlockSpec((tm, d), lambda i: (i, 0))
    return pl.pallas_call(
        body, name=name, grid=(s_len // tm,),
        in_specs=[tile, tile],
        out_specs=[tile, pl.BlockSpec((8, d), lambda i: (0, 0))],
        out_shape=[jax.ShapeDtypeStruct((s_len, d), F32), jax.ShapeDtypeStruct((8, d), F32)],
        compiler_params=_params(1),
    )(y, target)


def _adamw_math(w, g, m, v):
    m = ADAM_B1 * m + (1.0 - ADAM_B1) * g
    v = ADAM_B2 * v + (1.0 - ADAM_B2) * (g * g)
    m_hat = m / (1.0 - ADAM_B1 ** ADAM_STEP)
    v_hat = v / (1.0 - ADAM_B2 ** ADAM_STEP)
    delta = -ADAM_LR * (m_hat / (jnp.sqrt(v_hat) + ADAM_EPS) + ADAM_WD * w)
    return delta, m, v


def _adamw(w, g, m, v, name):
    r, cols = w.shape
    tr = r // 8 if r % 64 == 0 else r

    def body(w_ref, g_ref, m_ref, v_ref, d_ref, nm_ref, nv_ref):
        d_ref[...], nm_ref[...], nv_ref[...] = _adamw_math(w_ref[...], g_ref[...], m_ref[...], v_ref[...])

    tile = pl.BlockSpec((tr, cols), lambda i: (i, 0))
    shape = jax.ShapeDtypeStruct((r, cols), F32)
    return pl.pallas_call(
        body, name=name, grid=(r // tr,),
        in_specs=[tile] * 4, out_specs=[tile] * 3, out_shape=[shape] * 3,
        compiler_params=_params(1),
    )(w, g, m, v)


def _in_parts(tm, n_qkv, n_rest):
    def part(lo, n_blk):
        return pl.BlockSpec((tm, IN_BLOCK), lambda i, j: (i, jnp.clip(j - lo, 0, n_blk - 1)))
    return [part(0, n_qkv), part(n_qkv, n_qkv), part(2 * n_qkv, n_qkv), part(3 * n_qkv, n_rest)]


def _pick_part(j, n_qkv, refs, fn):
    bounds = [0, n_qkv, 2 * n_qkv, 3 * n_qkv]
    for p, ref in enumerate(refs):
        inside = j >= bounds[p]
        if p + 1 < len(refs):
            inside = inside & (j < bounds[p + 1])
        pl.when(inside)(lambda ref=ref: fn(ref))


def _in_proj(h, w, name):
    s_len, d = h.shape
    tm = PROJ_TILE
    steps = w.shape[1] // IN_BLOCK
    n_qkv = 3 * QKV // IN_BLOCK

    def body(h_ref, w_ref, qkv_ref, rest_ref):
        j = pl.program_id(1)
        res = _dot(h_ref[...], w_ref[...])

        @pl.when(j < n_qkv)
        def _():
            qkv_ref[...] = res

        @pl.when(j >= n_qkv)
        def _():
            rest_ref[...] = res

    return pl.pallas_call(
        body, name=name, grid=(s_len // tm, steps),
        in_specs=[pl.BlockSpec((tm, d), lambda i, j: (i, 0)), pl.BlockSpec((d, IN_BLOCK), lambda i, j: (0, j))],
        out_specs=[pl.BlockSpec((tm, IN_BLOCK), lambda i, j: (i, jnp.minimum(j, n_qkv - 1))),
                   pl.BlockSpec((tm, IN_BLOCK), lambda i, j: (i, jnp.maximum(j - n_qkv, 0)))],
        out_shape=[jax.ShapeDtypeStruct((s_len, 3 * QKV), F32),
                   jax.ShapeDtypeStruct((s_len, w.shape[1] - 3 * QKV), F32)],
        compiler_params=_params(2),
    )(h, w)


def _in_proj_bwd(dq, dk, dv, drest, w, name):
    s_len = dq.shape[0]
    d = w.shape[0]
    tm = PROJ_TILE
    steps = w.shape[1] // IN_BLOCK
    n_qkv = QKV // IN_BLOCK

    def body(dq_ref, dk_ref, dv_ref, dr_ref, w_ref, o_ref, acc_ref):
        j = pl.program_id(1)

        @pl.when(j == 0)
        def _():
            acc_ref[...] = jnp.zeros_like(acc_ref)

        def add(a_ref):
            acc_ref[...] += _dot_nt(a_ref[...], w_ref[...])

        _pick_part(j, n_qkv, [dq_ref, dk_ref, dv_ref, dr_ref], add)

        @pl.when(j == steps - 1)
        def _():
            o_ref[...] = acc_ref[...]

    return pl.pallas_call(
        body, name=name, grid=(s_len // tm, steps),
        in_specs=_in_parts(tm, n_qkv, steps - 3 * n_qkv) + [pl.BlockSpec((d, IN_BLOCK), lambda i, j: (0, j))],
        out_specs=pl.BlockSpec((tm, d), lambda i, j: (i, 0)),
        out_shape=jax.ShapeDtypeStruct((s_len, d), F32),
        scratch_shapes=[pltpu.VMEM((tm, d), F32)],
        compiler_params=_params(2),
    )(dq, dk, dv, drest, w)


def _in_proj_wgrad(h, dq, dk, dv, drest, name):
    s_len, d = h.shape
    ts = PROJ_TILE
    t_steps = s_len // ts
    n_qkv = QKV // IN_BLOCK
    n_cols = 3 * QKV + drest.shape[1]
    steps = n_cols // IN_BLOCK

    def body(h_ref, dq_ref, dk_ref, dv_ref, dr_ref, o_ref, acc_ref):
        j, s = pl.program_id(0), pl.program_id(1)

        @pl.when(s == 0)
        def _():
            acc_ref[...] = jnp.zeros_like(acc_ref)

        def add(a_ref):
            acc_ref[...] += _dot_tn(h_ref[...], a_ref[...])

        _pick_part(j, n_qkv, [dq_ref, dk_ref, dv_ref, dr_ref], add)

        @pl.when(s == t_steps - 1)
        def _():
            o_ref[...] = acc_ref[...].astype(BF16)

    parts = [pl.BlockSpec(p.block_shape, lambda j, s, f=p.index_map: f(s, j))
             for p in _in_parts(ts, n_qkv, steps - 3 * n_qkv)]
    return pl.pallas_call(
        body, name=name, grid=(steps, t_steps),
        in_specs=[pl.BlockSpec((ts, d), lambda j, s: (s, 0))] + parts,
        out_specs=pl.BlockSpec((d, IN_BLOCK), lambda j, s: (0, j)),
        out_shape=jax.ShapeDtypeStruct((d, n_cols), BF16),
        scratch_shapes=[pltpu.VMEM((d, IN_BLOCK), F32)],
        compiler_params=_params(2),
    )(h, dq, dk, dv, drest)


def _wgrad(x, y, x_spec, y_spec, out_shape, out_spec, acc_shape, n_chunks, name):
    s_len = x.shape[-2]
    ts = WGRAD_TILE
    steps = s_len // ts

    def body(x_ref, y_ref, o_ref, acc_ref):
        s = pl.program_id(1)

        @pl.when(s == 0)
        def _():
            acc_ref[...] = jnp.zeros_like(acc_ref)

        acc_ref[...] += _dot_tn(x_ref[...], y_ref[...])

        @pl.when(s == steps - 1)
        def _():
            o_ref[...] = acc_ref[...].astype(o_ref.dtype)

    return pl.pallas_call(
        body, name=name, grid=(n_chunks, steps),
        in_specs=[x_spec(ts), y_spec(ts)], out_specs=out_spec,
        out_shape=jax.ShapeDtypeStruct(out_shape, BF16),
        scratch_shapes=[pltpu.VMEM(acc_shape, F32)],
        compiler_params=_params(2),
    )(x, y)


def _ffn_fwd(x, h, mod, w_gate, w_up, w_down, name):
    s_len, d = x.shape
    n_chunks, _, fs = w_gate.shape
    tm = TOKEN_TILE

    def body(x_ref, h_ref, mod_ref, wg_ref, wu_ref, wd_ref, xo_ref, g_ref, u_ref, y_ref, acc_ref):
        j = pl.program_id(1)

        @pl.when(j == 0)
        def _():
            acc_ref[...] = jnp.zeros_like(acc_ref)

        hv = h_ref[...]
        g = _dot(hv, wg_ref[...])
        u = _dot(hv, wu_ref[...])
        g_ref[...] = g.astype(BF16)
        u_ref[...] = u.astype(BF16)
        act = (g * _sigmoid(g)) * u
        acc_ref[...] += _dot(act.astype(BF16), wd_ref[...])

        @pl.when(j == n_chunks - 1)
        def _():
            yv = acc_ref[...]
            y_ref[...] = yv
            xo_ref[...] = x_ref[...] + 0.5 * mod_ref[2:3, :] * yv

    tile = pl.BlockSpec((tm, d), lambda i, j: (i, 0))
    hid = pl.BlockSpec((None, tm, fs), lambda i, j: (j, i, 0))
    w_in_spec = pl.BlockSpec((None, d, fs), lambda i, j: (j, 0, 0))
    return pl.pallas_call(
        body, name=name, grid=(s_len // tm, n_chunks),
        in_specs=[tile, tile, pl.BlockSpec((8, d), lambda i, j: (0, 0)), w_in_spec, w_in_spec,
                  pl.BlockSpec((None, fs, d), lambda i, j: (j, 0, 0))],
        out_specs=[tile, hid, hid, tile],
        out_shape=[jax.ShapeDtypeStruct((s_len, d), F32),
                   jax.ShapeDtypeStruct((n_chunks, s_len, fs), BF16),
                   jax.ShapeDtypeStruct((n_chunks, s_len, fs), BF16),
                   jax.ShapeDtypeStruct((s_len, d), F32)],
        scratch_shapes=[pltpu.VMEM((tm, d), F32)],
        compiler_params=_params(2),
    )(x, h, mod, w_gate, w_up, w_down)


def _ffn_bwd(dxo, mod, g_pre, u_pre, w_gate, w_up, w_down, name):
    s_len, d = dxo.shape
    n_chunks, _, fs = w_gate.shape
    tm = TOKEN_TILE

    def body(dxo_ref, mod_ref, g_ref, u_ref, wg_ref, wu_ref, wd_ref, dh_ref, dg_ref, du_ref, a_ref, dy_ref, acc_ref):
        j = pl.program_id(1)

        @pl.when(j == 0)
        def _():
            dy_ref[...] = (0.5 * mod_ref[2:3, :] * dxo_ref[...]).astype(BF16)
            acc_ref[...] = jnp.zeros_like(acc_ref)

        da = _dot_nt(dy_ref[...], wd_ref[...])
        g = g_ref[...].astype(F32)
        u = u_ref[...].astype(F32)
        sg = _sigmoid(g)
        silu = g * sg
        dg = (da * u * (sg * (1.0 + g * (1.0 - sg)))).astype(BF16)
        du = (da * silu).astype(BF16)
        dg_ref[...] = dg
        du_ref[...] = du
        a_ref[...] = (silu * u).astype(BF16)
        acc_ref[...] += _dot_nt(dg, wg_ref[...]) + _dot_nt(du, wu_ref[...])

        @pl.when(j == n_chunks - 1)
        def _():
            dh_ref[...] = acc_ref[...]

    tile = pl.BlockSpec((tm, d), lambda i, j: (i, 0))
    hid = pl.BlockSpec((None, tm, fs), lambda i, j: (j, i, 0))
    w_in_spec = pl.BlockSpec((None, d, fs), lambda i, j: (j, 0, 0))
    hid_shape = jax.ShapeDtypeStruct((n_chunks, s_len, fs), BF16)
    return pl.pallas_call(
        body, name=name, grid=(s_len // tm, n_chunks),
        in_specs=[tile, pl.BlockSpec((8, d), lambda i, j: (0, 0)), hid, hid, w_in_spec, w_in_spec,
                  pl.BlockSpec((None, fs, d), lambda i, j: (j, 0, 0))],
        out_specs=[tile, hid, hid, hid, tile],
        out_shape=[jax.ShapeDtypeStruct((s_len, d), F32), hid_shape, hid_shape, hid_shape,
                   jax.ShapeDtypeStruct((s_len, d), BF16)],
        scratch_shapes=[pltpu.VMEM((tm, d), F32)],
        compiler_params=_params(2),
    )(dxo, mod, g_pre, u_pre, w_gate, w_up, w_down)


def _ffn_wgrads(h, dg, du, act, dy, tag):
    n_chunks, s_len, fs = dg.shape
    d = h.shape[1]
    tok = lambda ts: pl.BlockSpec((ts, d), lambda c, s: (s, 0))
    hid = lambda ts: pl.BlockSpec((None, ts, fs), lambda c, s: (c, s, 0))
    d_up = pl.BlockSpec((None, d, fs), lambda c, s: (c, 0, 0))
    d_down = pl.BlockSpec((None, fs, d), lambda c, s: (c, 0, 0))
    dwg = _wgrad(h, dg, tok, hid, (n_chunks, d, fs), d_up, (d, fs), n_chunks, tag + "_dwg")
    dwu = _wgrad(h, du, tok, hid, (n_chunks, d, fs), d_up, (d, fs), n_chunks, tag + "_dwu")
    dwd = _wgrad(act, dy, hid, tok, (n_chunks, fs, d), d_down, (fs, d), n_chunks, tag + "_dwd")
    return dwg, dwu, dwd


def _band_bias():
    qi = lax.broadcasted_iota(jnp.int32, (ATTN_BLOCK, 2 * ATTN_BLOCK), 0)
    kj = lax.broadcasted_iota(jnp.int32, (ATTN_BLOCK, 2 * ATTN_BLOCK), 1)
    band = (kj >= qi) & (kj <= qi + ATTN_BLOCK)
    return jnp.where(band, 0.0, NEG), jnp.where(band & (kj >= ATTN_BLOCK), 0.0, NEG)


def _rows(base, count, stride):
    return pl.ds(base, count) if stride == 1 else pl.ds(base, count, stride=stride)


def _qkv_specs(slab_of):
    def spec(sect, back):
        return pl.BlockSpec((SLAB, HEAD_DIM),
                            lambda h, s, g: (jnp.maximum(slab_of(s) - back, 0), (sect * N_GROUPS + g) * HEADS + h))
    return [spec(0, 0), spec(1, 0), spec(2, 0), spec(1, 1), spec(2, 1)]


def _load_qkv(q_ref, k_ref, v_ref, kp_ref, vp_ref, qn_ref, kn_ref, qs, kb, vb, n):
    e = HEAD_DIM
    q = q_ref[...]
    qs[...] = (q * _rms(q)) * qn_ref[...]
    k = k_ref[...]
    kb[SLAB:2 * SLAB, :] = (k * _rms(k)) * kn_ref[...]
    vb[SLAB:2 * SLAB, :] = v_ref[...]

    @pl.when(n > 0)
    def _():
        kp = kp_ref[...]
        kb[0:SLAB, :] = (kp * _rms(kp)) * kn_ref[...]
        vb[0:SLAB, :] = vp_ref[...]

    @pl.when(n == 0)
    def _():
        kb[0:SLAB, :] = jnp.zeros((SLAB, e), F32)
        vb[0:SLAB, :] = jnp.zeros((SLAB, e), F32)


def _for_each_tile(dil, n, tile_fn):
    span = ATTN_BLOCK * dil
    bias, first_bias = _band_bias()

    def sub(jj, carry):
        start = pl.multiple_of(jj * span, ATTN_BLOCK)
        tile_bias = jnp.where(jnp.logical_and(n == 0, jj == 0), first_bias, bias)
        for r in range(dil):
            tile_fn(_rows(start + r, ATTN_BLOCK, dil), _rows(SLAB - span + start + r, 2 * ATTN_BLOCK, dil), tile_bias)
        return carry

    lax.fori_loop(0, SLAB // span, sub, 0)


def _attn_fwd(qkv, q_norm, k_norm, name):
    s_len = qkv.shape[0]
    e = HEAD_DIM
    n_slabs = s_len // SLAB

    def body(q_ref, k_ref, v_ref, kp_ref, vp_ref, qn_ref, kn_ref, o_ref, lse_ref, qs, kb, vb, m_s, l_s, acc_s):
        n, grp = pl.program_id(1), pl.program_id(2)
        _load_qkv(q_ref, k_ref, v_ref, kp_ref, vp_ref, qn_ref, kn_ref, qs, kb, vb, n)

        def run(gi, dil):
            def tile(q_rows, kv_rows, bias):
                q = qs[q_rows, :].astype(BF16)
                k = kb[kv_rows, :].astype(BF16)
                v = vb[kv_rows, :].astype(BF16)
                s = _dot_nt(q, k) * ATTN_SCALE + bias
                m = jnp.max(s, axis=-1, keepdims=True)
                p = jnp.exp(s - m)
                m_s.at[gi][q_rows, :] = jnp.broadcast_to(m, (ATTN_BLOCK, e))
                l_s.at[gi][q_rows, :] = jnp.broadcast_to(jnp.sum(p, axis=-1, keepdims=True), (ATTN_BLOCK, e))
                acc_s.at[gi][q_rows, :] = _dot(p.astype(BF16), v)

            _for_each_tile(dil, n, tile)

        for gi, dil in enumerate(DILATIONS):
            pl.when(grp == gi)(lambda gi=gi, dil=dil: run(gi, dil))

        @pl.when(grp == N_GROUPS - 1)
        def _():
            m_all = jnp.maximum(jnp.maximum(m_s[0], m_s[1]), m_s[2])
            den = jnp.zeros((SLAB, e), F32)
            num = jnp.zeros((SLAB, e), F32)
            for gi in range(N_GROUPS):
                w = jnp.exp(m_s[gi] - m_all)
                den += l_s[gi] * w
                num += acc_s[gi] * w
            o_ref[...] = num / den
            lse_ref[...] = m_all + jnp.log(den)

    small = pl.BlockSpec((1, e), lambda h, n, g: (0, 0))
    out = pl.BlockSpec((SLAB, e), lambda h, n, g: (n, h))
    return pl.pallas_call(
        body, name=name, grid=(HEADS, n_slabs, N_GROUPS),
        in_specs=_qkv_specs(lambda n: n) + [small, small],
        out_specs=[out, out],
        out_shape=[jax.ShapeDtypeStruct((s_len, HEADS * e), F32)] * 2,
        scratch_shapes=[pltpu.VMEM((SLAB, e), F32), pltpu.VMEM((2 * SLAB, e), F32), pltpu.VMEM((2 * SLAB, e), F32),
                        pltpu.VMEM((N_GROUPS, SLAB, e), F32), pltpu.VMEM((N_GROUPS, SLAB, e), F32),
                        pltpu.VMEM((N_GROUPS, SLAB, e), F32)],
        compiler_params=_params(3),
    )(qkv, qkv, qkv, qkv, qkv, q_norm, k_norm)


def _attn_bwd(qkv, d_out, out, lse, q_norm, k_norm, name):
    s_len = qkv.shape[0]
    e = HEAD_DIM
    n_slabs = s_len // SLAB

    def body(q_ref, k_ref, v_ref, kp_ref, vp_ref, do_ref, o_ref, lse_ref, qn_ref, kn_ref, dq_ref, dk_ref, dv_ref,
             st_ref, qs, kb, vb, dqs, dkb, dvb, carry):
        head, step, grp = pl.program_id(0), pl.program_id(1), pl.program_id(2)
        n = n_slabs - 1 - step
        _load_qkv(q_ref, k_ref, v_ref, kp_ref, vp_ref, qn_ref, kn_ref, qs, kb, vb, n)
        dkb[...] = jnp.zeros_like(dkb)
        dvb[...] = jnp.zeros_like(dvb)

        @pl.when((head == 0) & (step == 0) & (grp == 0))
        def _():
            st_ref[...] = jnp.zeros_like(st_ref)

        def run(gi, dil):
            @pl.when(step == 0)
            def _():
                carry[gi] = jnp.zeros((2, SLAB, e), F32)

            def tile(q_rows, kv_rows, bias):
                q = qs[q_rows, :].astype(BF16)
                k = kb[kv_rows, :].astype(BF16)
                v = vb[kv_rows, :].astype(BF16)
                do = do_ref[q_rows, :]
                delta = jnp.sum(do * o_ref[q_rows, :], axis=-1, keepdims=True)
                s = _dot_nt(q, k) * ATTN_SCALE + bias
                p = jnp.exp(s - lse_ref[q_rows, :][:, 0:1])
                do16 = do.astype(BF16)
                ds = (p * (_dot_nt(do16, v) - delta) * ATTN_SCALE).astype(BF16)
                dqs[q_rows, :] = _dot(ds, k)
                dkb[kv_rows, :] += _dot_tn(ds, q)
                dvb[kv_rows, :] += _dot_tn(p.astype(BF16), do16)

            _for_each_tile(dil, n, tile)
            dk_hat = dkb[SLAB:2 * SLAB, :] + carry[gi, 0]
            dv = dvb[SLAB:2 * SLAB, :] + carry[gi, 1]
            carry[gi, 0] = dkb[0:SLAB, :]
            carry[gi, 1] = dvb[0:SLAB, :]

            def norm_bwd(raw, gain, d_hat):
                r = _rms(raw)
                y = raw * r
                dy = d_hat * gain
                return r * (dy - y * jnp.mean(dy * y, axis=-1, keepdims=True)), jnp.sum(d_hat * y, axis=0, keepdims=True)

            dq, dqn = norm_bwd(q_ref[...], qn_ref[...], dqs[...])
            dk, dkn = norm_bwd(k_ref[...], kn_ref[...], dk_hat)
            dq_ref[...] = dq.astype(BF16)
            dk_ref[...] = dk.astype(BF16)
            dv_ref[...] = dv.astype(BF16)
            st_ref[0:1, :] += dqn
            st_ref[1:2, :] += dkn

        for gi, dil in enumerate(DILATIONS):
            pl.when(grp == gi)(lambda gi=gi, dil=dil: run(gi, dil))

    slab_of = lambda s: n_slabs - 1 - s
    small = pl.BlockSpec((1, e), lambda h, s, g: (0, 0))
    head_blk = pl.BlockSpec((SLAB, e), lambda h, s, g: (slab_of(s), h))
    grad_blk = pl.BlockSpec((SLAB, e), lambda h, s, g: (slab_of(s), g * HEADS + h))
    grad_shape = jax.ShapeDtypeStruct((s_len, QKV), BF16)
    return pl.pallas_call(
        body, name=name, grid=(HEADS, n_slabs, N_GROUPS),
        in_specs=_qkv_specs(slab_of) + [head_blk, head_blk, head_blk, small, small],
        out_specs=[grad_blk, grad_blk, grad_blk, pl.BlockSpec((8, e), lambda h, s, g: (0, 0))],
        out_shape=[grad_shape, grad_shape, grad_shape, jax.ShapeDtypeStruct((8, e), F32)],
        scratch_shapes=[pltpu.VMEM((SLAB, e), F32), pltpu.VMEM((2 * SLAB, e), F32), pltpu.VMEM((2 * SLAB, e), F32),
                        pltpu.VMEM((SLAB, e), F32), pltpu.VMEM((2 * SLAB, e), F32), pltpu.VMEM((2 * SLAB, e), F32),
                        pltpu.VMEM((N_GROUPS, 2, SLAB, e), F32)],
        compiler_params=_params(3),
    )(qkv, qkv, qkv, qkv, qkv, d_out, out, lse, q_norm, k_norm)


def _shift_rows(x, by, edge, forward):
    t_len = x.shape[0]
    row = lax.broadcasted_iota(jnp.int32, x.shape, 0)
    if forward:
        out = pltpu.roll(x, by, 0)
        for i in range(by):
            out = jnp.where(row == i, edge[8 - by + i:8 - by + i + 1, :], out)
    else:
        out = pltpu.roll(x, t_len - by, 0)
        for i in range(by):
            out = jnp.where(row == t_len - by + i, edge[i:i + 1, :], out)
    return out


def _mix_fwd(x, o, rest, mod, conv_w, w_attn, w_conv, w_out, name):
    s_len, d = x.shape
    tm = MIX_TILE
    a_w = o.shape[1]

    def body(x_ref, o_ref, u_ref, b_ref, c_ref, ga_ref, gc_ref, mod_ref, cw_ref, wa_ref, wc_ref, wo_ref,
             xo_ref, z_ref, ya_ref, yc_ref, conv_ref, yb_ref, m_ref, o16_ref, carry):
        @pl.when(pl.program_id(0) == 0)
        def _():
            carry[...] = jnp.zeros_like(carry)

        xc = c_ref[...] * u_ref[...]
        edge = carry[...]
        conv = (_shift_rows(xc, 2, edge, True) * cw_ref[0:1, :] + _shift_rows(xc, 1, edge, True) * cw_ref[1:2, :]
                + xc * cw_ref[2:3, :])
        carry[...] = xc[tm - 8:tm, :]
        yb = (b_ref[...] * conv).astype(BF16)
        o16 = o_ref[...].astype(BF16)
        ya = _dot(o16, wa_ref[...])
        yc = _dot(yb, wc_ref[...])
        merged = (_sigmoid(ga_ref[...]) * ya + _sigmoid(gc_ref[...]) * yc).astype(BF16)
        z = _dot(merged, wo_ref[...])
        xo_ref[...] = x_ref[...] + mod_ref[2:3, :] * z
        z_ref[...] = z
        ya_ref[...] = ya.astype(BF16)
        yc_ref[...] = yc.astype(BF16)
        conv_ref[...] = conv.astype(BF16)
        yb_ref[...] = yb
        m_ref[...] = merged
        o16_ref[...] = o16

    tile = pl.BlockSpec((tm, d), lambda i: (i, 0))
    sect = lambda k: pl.BlockSpec((tm, d), lambda i: (i, k))
    att = pl.BlockSpec((tm, a_w), lambda i: (i, 0))
    const = lambda shape: pl.BlockSpec(shape, lambda i: (0, 0))
    f32_out = jax.ShapeDtypeStruct((s_len, d), F32)
    b16_out = jax.ShapeDtypeStruct((s_len, d), BF16)
    return pl.pallas_call(
        body, name=name, grid=(s_len // tm,),
        in_specs=[tile, att, sect(0), sect(1), sect(2), sect(3), sect(4), const((8, d)), const((8, d)),
                  const((a_w, d)), const((d, d)), const((d, d))],
        out_specs=[tile] * 7 + [att],
        out_shape=[f32_out, f32_out] + [b16_out] * 5 + [jax.ShapeDtypeStruct((s_len, a_w), BF16)],
        scratch_shapes=[pltpu.VMEM((8, d), F32)],
        compiler_params=_params(1),
    )(x, o, rest, rest, rest, rest, rest, mod, conv_w, w_attn, w_conv, w_out)


def _mix_bwd(dxo, ya, yc, conv, rest, mod, conv_w, w_attn, w_conv, w_out, a_w, name):
    s_len, d = dxo.shape
    tm = MIX_TILE
    n_tiles = s_len // tm

    def body(dxo_ref, ya_ref, yc_ref, conv_ref, u_ref, b_ref, c_ref, ga_ref, gc_ref, mod_ref, cw_ref,
             wa_ref, wc_ref, wo_ref, do_ref, drest_ref, dz_ref, dya_ref, dyc_ref, st_ref, carry):
        @pl.when(pl.program_id(0) == 0)
        def _():
            carry[...] = jnp.zeros_like(carry)
            st_ref[...] = jnp.zeros_like(st_ref)

        dz = (mod_ref[2:3, :] * dxo_ref[...]).astype(BF16)
        dz_ref[...] = dz
        dm = _dot_nt(dz, wo_ref[...])
        sa, sc = _sigmoid(ga_ref[...]), _sigmoid(gc_ref[...])
        dya = (dm * sa).astype(BF16)
        dyc = (dm * sc).astype(BF16)
        dya_ref[...] = dya
        dyc_ref[...] = dyc
        drest_ref[:, 3 * d:4 * d] = (dm * ya_ref[...].astype(F32) * (sa * (1.0 - sa))).astype(BF16)
        drest_ref[:, 4 * d:5 * d] = (dm * yc_ref[...].astype(F32) * (sc * (1.0 - sc))).astype(BF16)
        do_ref[...] = _dot_nt(dya, wa_ref[...])
        dyb = _dot_nt(dyc, wc_ref[...])
        drest_ref[:, d:2 * d] = (dyb * conv_ref[...].astype(F32)).astype(BF16)
        dconv = dyb * b_ref[...]
        edge = carry[...]
        sh1 = _shift_rows(dconv, 1, edge, False)
        sh2 = _shift_rows(dconv, 2, edge, False)
        carry[...] = dconv[0:8, :]
        dxc = dconv * cw_ref[2:3, :] + sh1 * cw_ref[1:2, :] + sh2 * cw_ref[0:1, :]
        u, c = u_ref[...], c_ref[...]
        xc = c * u
        drest_ref[:, 0:d] = (dxc * c).astype(BF16)
        drest_ref[:, 2 * d:3 * d] = (dxc * u).astype(BF16)
        st_ref[0:1, :] += jnp.sum(xc * sh2, axis=0, keepdims=True)
        st_ref[1:2, :] += jnp.sum(xc * sh1, axis=0, keepdims=True)
        st_ref[2:3, :] += jnp.sum(xc * dconv, axis=0, keepdims=True)

    rev = lambda i: n_tiles - 1 - i
    tile = pl.BlockSpec((tm, d), lambda i: (rev(i), 0))
    sect = lambda k: pl.BlockSpec((tm, d), lambda i: (rev(i), k))
    const = lambda shape: pl.BlockSpec(shape, lambda i: (0, 0))
    b16_out = jax.ShapeDtypeStruct((s_len, d), BF16)
    return pl.pallas_call(
        body, name=name, grid=(n_tiles,),
        in_specs=[tile, tile, tile, tile, sect(0), sect(1), sect(2), sect(3), sect(4), const((8, d)), const((8, d)),
                  const((a_w, d)), const((d, d)), const((d, d))],
        out_specs=[pl.BlockSpec((tm, a_w), lambda i: (rev(i), 0)), pl.BlockSpec((tm, 5 * d), lambda i: (rev(i), 0)),
                   tile, tile, tile, const((8, d))],
        out_shape=[jax.ShapeDtypeStruct((s_len, a_w), F32), jax.ShapeDtypeStruct((s_len, 5 * d), BF16),
                   b16_out, b16_out, b16_out, jax.ShapeDtypeStruct((8, d), F32)],
        scratch_shapes=[pltpu.VMEM((8, d), F32)],
        compiler_params=_params(1),
    )(dxo, ya, yc, conv, rest, rest, rest, rest, rest, mod, conv_w, w_attn, w_conv, w_out)


ADA_COLS = 128


def _ada_fwd(c_all, w_shard, b_shard, name):
    d, cols = w_shard.shape

    def body(c_ref, w_ref, b_ref, o_ref):
        cv = c_ref[...]
        o_ref[...] = jnp.dot(cv * _sigmoid(cv), w_ref[...], preferred_element_type=F32,
                             precision=lax.Precision.HIGHEST) + b_ref[...]

    return pl.pallas_call(
        body, name=name, grid=(cols // ADA_COLS,),
        in_specs=[pl.BlockSpec((8, d), lambda j: (0, 0)), pl.BlockSpec((d, ADA_COLS), lambda j: (0, j)),
                  pl.BlockSpec((1, ADA_COLS), lambda j: (0, j))],
        out_specs=pl.BlockSpec((8, ADA_COLS), lambda j: (0, j)),
        out_shape=jax.ShapeDtypeStruct((8, cols), F32),
        compiler_params=_params(1),
    )(c_all, w_shard, b_shard)


def _ada_bwd(c_all, dmod_shard, w, m, v, name):
    d, cols = w.shape

    def body(c_ref, dm_ref, w_ref, m_ref, v_ref, g_ref, d_ref, nm_ref, nv_ref):
        cv = c_ref[...]
        g = lax.dot_general(cv * _sigmoid(cv), dm_ref[...], (((0,), (0,)), ((), ())),
                            preferred_element_type=F32, precision=lax.Precision.HIGHEST)
        g_ref[...] = g
        d_ref[...], nm_ref[...], nv_ref[...] = _adamw_math(w_ref[...], g, m_ref[...], v_ref[...])

    blk = pl.BlockSpec((d, ADA_COLS), lambda j: (0, j))
    shape = jax.ShapeDtypeStruct((d, cols), F32)
    return pl.pallas_call(
        body, name=name, grid=(cols // ADA_COLS,),
        in_specs=[pl.BlockSpec((8, d), lambda j: (0, 0)), pl.BlockSpec((8, ADA_COLS), lambda j: (0, j)), blk, blk, blk],
        out_specs=[blk] * 4, out_shape=[shape] * 4,
        compiler_params=_params(1),
    )(c_all, dmod_shard, w, m, v)


def _small_update(parts, w, m, v, name):
    n = w.shape[1]

    def body(p_ref, w_ref, m_ref, v_ref, g_ref, d_ref, nm_ref, nv_ref):
        g = p_ref[0:1, :]
        for i in range(1, 8):
            g = g + p_ref[i:i + 1, :]
        g_ref[...] = g
        d_ref[...], nm_ref[...], nv_ref[...] = _adamw_math(w_ref[...], g, m_ref[...], v_ref[...])

    shape = jax.ShapeDtypeStruct((1, n), F32)
    return pl.pallas_call(body, name=name, out_shape=[shape] * 4, compiler_params=_params())(parts, w, m, v)


def _cols_to_shards(w, n):
    r, nc = w.shape
    return w.reshape(r, n, nc // n).transpose(1, 0, 2)


def kernel(x, c, w_ada, b_ada, norm_ffn1, ffn1_w_gate, ffn1_w_up, ffn1_w_down, norm_mix, w_in, q_norm, k_norm, conv_w, w_attn_branch, w_conv_branch, w_out, norm_ffn2, ffn2_w_gate, ffn2_w_up, ffn2_w_down, loss_target, m_w_ada, m_b_ada, m_norm_ffn1, m_ffn1_w_gate, m_ffn1_w_up, m_ffn1_w_down, m_norm_mix, m_w_in, m_q_norm, m_k_norm, m_conv_w, m_w_attn_branch, m_w_conv_branch, m_w_out, m_norm_ffn2, m_ffn2_w_gate, m_ffn2_w_up, m_ffn2_w_down, v_w_ada, v_b_ada, v_norm_ffn1, v_ffn1_w_gate, v_ffn1_w_up, v_ffn1_w_down, v_norm_mix, v_w_in, v_q_norm, v_k_norm, v_conv_w, v_w_attn_branch, v_w_conv_branch, v_w_out, v_norm_ffn2, v_ffn2_w_gate, v_ffn2_w_up, v_ffn2_w_down):
    ix, iy, ic = _place()
    chip = 2 * ix + iy
    me = 4 * ix + 2 * iy + ic
    xs = x[0]
    target = loss_target[0]
    s_len, d = xs.shape
    ada_cols = w_ada.shape[2]
    conv_cols = conv_w.shape[2]

    conv_rows = jnp.zeros((8, conv_cols), F32).at[0:3].set(conv_w[0])
    small_in = jnp.concatenate([jnp.broadcast_to(c, (8, d)), conv_rows], axis=1)
    small_all = _allgather8(small_in, "gather_c").reshape(8, 8, d + conv_cols)
    c_all = small_all[:, 0, :d]
    conv_full = small_all[0::2, 0:3, d:].transpose(1, 0, 2).reshape(3, N_CHIPS * conv_cols)
    conv_pad = jnp.zeros((8, N_CHIPS * conv_cols), F32).at[0:3].set(conv_full)
    b_shard = lax.dynamic_slice(b_ada, (0, chip * ada_cols), (1, ada_cols))
    mod_part = _ada_fwd(c_all, w_ada[0], b_shard, "ada_fwd")
    mod_all = _allgather8(mod_part, "gather_mod").reshape(N_CHIPS, 2, 8, ada_cols)[:, 0]
    mod_mine = lax.dynamic_slice(mod_all, (0, me, 0), (N_CHIPS, 1, ada_cols)).reshape(9, d)

    def mod_rows(i, gain):
        return jnp.zeros((8, d), F32).at[0:3].set(mod_mine[3 * i:3 * i + 3]).at[3:4].set(gain)

    mod1, mod2, mod3 = mod_rows(0, norm_ffn1), mod_rows(1, norm_mix), mod_rows(2, norm_ffn2)

    to16 = lambda w: w[0].astype(BF16)
    wg1, wu1, wd1 = _gather_weights([to16(ffn1_w_gate), to16(ffn1_w_up), to16(ffn1_w_down)], [False] * 3,
                                    "gather_ffn1", 1)
    (w_in_full,) = _gather_weights([to16(w_in)], [True], "gather_w_in", 2, after=(wd1,))
    w_ab, w_cb_g, w_o_g, wg2, wu2, wd2 = _gather_weights(
        [to16(w_attn_branch), to16(w_conv_branch), to16(w_out),
         to16(ffn2_w_gate), to16(ffn2_w_up), to16(ffn2_w_down)], [True] + [False] * 5,
        "gather_rest", 3, after=(w_in_full,))
    a_w = w_ab.shape[0]
    w_cb = w_cb_g.reshape(d, d)
    w_o = w_o_g.reshape(d, d)

    h1 = _norm_mod(xs, mod1, "norm1")
    x1, g1, u1, y1 = _ffn_fwd(xs, h1, mod1, wg1, wu1, wd1, "ffn1_fwd")
    h2 = _norm_mod(x1, mod2, "norm2")
    qkv, rest = _in_proj(h2, w_in_full, "in_proj")
    o, lse = _attn_fwd(qkv, q_norm, k_norm, "attn_fwd")
    x2, z, ya, yc, conv, yb, merged, o16 = _mix_fwd(x1, o, rest, mod2, conv_pad, w_ab, w_cb, w_o, "mix_fwd")
    h3 = _norm_mod(x2, mod3, "norm3")
    x3, g3, u3, y3 = _ffn_fwd(x2, h3, mod3, wg2, wu2, wd2, "ffn2_fwd")
    dx3, loss_part = _loss_grad(x3, target, "loss")
    loss = lax.psum(0.5 * jnp.sum(loss_part) / d, ("x", "y", "c"))

    dh3, dg3, du3, a3, dy3 = _ffn_bwd(dx3, mod3, g3, u3, wg2, wu2, wd2, "ffn2_bwd")
    dx2, st3 = _norm_bwd(dh3, x2, mod3, dx3, y3, 0.5, "norm3_bwd")
    dwg2, dwu2, dwd2 = _ffn_wgrads(h3, dg3, du3, a3, dy3, "ffn2")

    do, drest, dz, dya, dyc, st_conv = _mix_bwd(dx2, ya, yc, conv, rest, mod2, conv_pad, w_ab, w_cb, w_o, a_w, "mix_bwd")
    dq, dk, dv, st_qk = _attn_bwd(qkv, do, o, lse, q_norm, k_norm, "attn_bwd")
    dh2 = _in_proj_bwd(dq, dk, dv, drest, w_in_full, "in_proj_bwd")
    dx1, st2 = _norm_bwd(dh2, x1, mod2, dx2, z, 1.0, "norm2_bwd")
    tok = lambda width: (lambda ts: pl.BlockSpec((ts, width), lambda cc, s: (s, 0)))
    colblk = lambda width: (lambda ts: pl.BlockSpec((ts, width), lambda cc, s: (s, cc)))
    dw_in = _cols_to_shards(_in_proj_wgrad(h2, dq, dk, dv, drest, "dw_in"), N_CHIPS)
    shard_w = d // N_CHIPS
    dw_ab = _wgrad(o16, dya, tok(a_w), colblk(shard_w), (a_w, d), pl.BlockSpec((a_w, shard_w), lambda cc, s: (0, cc)),
                   (a_w, shard_w), N_CHIPS, "dw_attn_branch")
    dw_ab = _cols_to_shards(dw_ab, N_CHIPS)
    row_out = pl.BlockSpec((None, shard_w, d), lambda cc, s: (cc, 0, 0))
    dw_cb = _wgrad(yb, dyc, colblk(shard_w), tok(d), (N_CHIPS, shard_w, d), row_out, (shard_w, d), N_CHIPS, "dw_conv_branch")
    dw_o = _wgrad(merged, dz, colblk(shard_w), tok(d), (N_CHIPS, shard_w, d), row_out, (shard_w, d), N_CHIPS, "dw_out")

    dh1, dg1, du1, a1, dy1 = _ffn_bwd(dx1, mod1, g1, u1, wg1, wu1, wd1, "ffn1_bwd")
    dx0, st1 = _norm_bwd(dh1, xs, mod1, dx1, y1, 0.5, "norm1_bwd")
    dwg1, dwu1, dwd1 = _ffn_wgrads(h1, dg1, du1, a1, dy1, "ffn1")

    grads = [dwg1, dwu1, dwd1, dw_in, dw_ab, dw_cb, dw_o, dwg2, dwu2, dwd2]
    names = ["ffn1_w_gate", "ffn1_w_up", "ffn1_w_down", "w_in", "w_attn_branch", "w_conv_branch", "w_out",
             "ffn2_w_gate", "ffn2_w_up", "ffn2_w_down"]
    c_idx = jnp.reshape(ic, (1,)).astype(jnp.int32)
    chip_idx = jnp.stack([chip, ic]).astype(jnp.int32)
    from_sibling = _rs_pair_exchange(grads, "rs_pair")
    pair_sums = [_pair_add(g, r, c_idx, "pair_add_" + nm) for g, r, nm in zip(grads, from_sibling, names)]
    from_chips = _rs_chip_exchange(pair_sums, "rs_chips")
    totals = [_chip_add(p, r, chip_idx, "chip_add_" + nm) for p, r, nm in zip(pair_sums, from_chips, names)]
    shard_grads = dict(zip(names, _rs_share(totals, "rs_share")))

    dmod = jnp.concatenate([st1[0:3], st2[0:3], st3[0:3]], axis=0).reshape(1, 9 * d)
    small = jnp.concatenate([dmod, st1[3:4], st2[3:4], st3[3:4], st_qk[0:1], st_qk[1:2],
                             st_conv[0:3].reshape(1, 3 * d)], axis=1)
    small_all = _allgather8(jnp.broadcast_to(small, (8, small.shape[1])), "gather_small").reshape(8, 8, -1)[:, 0]
    dmod_all = small_all[:, :9 * d]
    dmod_shard = lax.dynamic_slice(dmod_all, (0, chip * ada_cols), (8, ada_cols))
    g_w_ada, d_w_ada, nm_w_ada, nv_w_ada = _ada_bwd(c_all, dmod_shard, w_ada[0], m_w_ada[0], v_w_ada[0], "ada_bwd")

    vec_names = ["b_ada", "norm_ffn1", "norm_mix", "norm_ffn2", "q_norm", "k_norm"]
    vec_w = [b_ada, norm_ffn1, norm_mix, norm_ffn2, q_norm, k_norm]
    vec_m = [m_b_ada, m_norm_ffn1, m_norm_mix, m_norm_ffn2, m_q_norm, m_k_norm]
    vec_v = [v_b_ada, v_norm_ffn1, v_norm_mix, v_norm_ffn2, v_q_norm, v_k_norm]
    n_vec = sum(w.shape[1] for w in vec_w)
    cat = lambda arrs: jnp.concatenate(arrs, axis=1)
    vec_out = _small_update(small_all[:, :n_vec], cat(vec_w), cat(vec_m), cat(vec_v), "small_update")
    conv_parts = small_all[:, n_vec:].reshape(8, 3, N_CHIPS * conv_cols)
    conv_parts = lax.dynamic_slice(conv_parts, (0, 0, chip * conv_cols), (8, 3, conv_cols)).reshape(8, 3 * conv_cols)
    flat3 = lambda w: w[0].reshape(1, 3 * conv_cols)
    conv_out = _small_update(conv_parts, flat3(conv_w), flat3(m_conv_w), flat3(v_conv_w), "conv_update")

    res = {"w_ada": [t[None] for t in (g_w_ada, d_w_ada, nm_w_ada, nv_w_ada)],
           "conv_w": [t.reshape(1, 3, conv_cols) for t in conv_out]}
    off = 0
    for nm, w in zip(vec_names, vec_w):
        width = w.shape[1]
        res[nm] = [t[:, off:off + width] for t in vec_out]
        off += width
    big = {"ffn1_w_gate": (ffn1_w_gate, m_ffn1_w_gate, v_ffn1_w_gate), "ffn1_w_up": (ffn1_w_up, m_ffn1_w_up, v_ffn1_w_up),
           "ffn1_w_down": (ffn1_w_down, m_ffn1_w_down, v_ffn1_w_down), "w_in": (w_in, m_w_in, v_w_in),
           "w_attn_branch": (w_attn_branch, m_w_attn_branch, v_w_attn_branch),
           "w_conv_branch": (w_conv_branch, m_w_conv_branch, v_w_conv_branch), "w_out": (w_out, m_w_out, v_w_out),
           "ffn2_w_gate": (ffn2_w_gate, m_ffn2_w_gate, v_ffn2_w_gate), "ffn2_w_up": (ffn2_w_up, m_ffn2_w_up, v_ffn2_w_up),
           "ffn2_w_down": (ffn2_w_down, m_ffn2_w_down, v_ffn2_w_down)}
    for nm, (w, m, v) in big.items():
        g = shard_grads[nm]
        delta, new_m, new_v = _adamw(w[0], g, m[0], v[0], "adamw_" + nm)
        res[nm] = [t[None] for t in (g, delta, new_m, new_v)]

    order = ["w_ada", "b_ada", "norm_ffn1", "ffn1_w_gate", "ffn1_w_up", "ffn1_w_down", "norm_mix", "w_in", "q_norm",
             "k_norm", "conv_w", "w_attn_branch", "w_conv_branch", "w_out", "norm_ffn2", "ffn2_w_gate", "ffn2_w_up",
             "ffn2_w_down"]
    return (loss, dx0[None], *[res[nm][0] for nm in order], *[res[nm][1] for nm in order],
            *[res[nm][2] for nm in order], *[res[nm][3] for nm in order])
```

```python
import jax
import jax.numpy as jnp
from jax import lax
from jax.experimental import pallas as pl
from jax.experimental.pallas import tpu as pltpu
from jax.experimental.pallas import tpu_sc as plsc

F32 = jnp.float32
BF16 = jnp.bfloat16
MESH = pl.DeviceIdType.MESH
ANY = pl.BlockSpec(memory_space=pl.ANY)

NORM_EPS = 1e-6
HEAD_DIM = 128
N_GROUPS = 3
HEADS = 4
DILATIONS = (1, 4, 16)
ATTN_BLOCK = 128
SLAB = ATTN_BLOCK * max(DILATIONS)
QKV = N_GROUPS * HEADS * HEAD_DIM
ATTN_SCALE = HEAD_DIM ** -0.5
NEG = -1e30
N_CHIPS = 4

ADAM_LR = 0.001
ADAM_B1 = 0.9
ADAM_B2 = 0.999
ADAM_EPS = 1e-08
ADAM_WD = 0.01
ADAM_STEP = 10

VMEM_LIMIT_BYTES = 56 * 1024 * 1024
TOKEN_TILE = 512
PROJ_TILE = 2048
WGRAD_TILE = 2048
IN_BLOCK = 512
MIX_TILE = 256


def _params(n_axes=0):
    return pltpu.CompilerParams(
        dimension_semantics=("arbitrary",) * n_axes if n_axes else None,
        vmem_limit_bytes=VMEM_LIMIT_BYTES)


def _dot(a, b):
    return jnp.dot(a, b, preferred_element_type=F32)


def _dot_nt(a, b):
    return lax.dot_general(a, b, (((1,), (1,)), ((), ())), preferred_element_type=F32)


def _dot_tn(a, b):
    return lax.dot_general(a, b, (((0,), (0,)), ((), ())), preferred_element_type=F32)


def _sigmoid(x):
    return 1.0 / (1.0 + jnp.exp(-x))


def _place():
    return lax.axis_index("x"), lax.axis_index("y"), lax.axis_index("c")


def _ordered(body, n_in, after):
    if not after:
        return body
    return lambda *refs: body(*refs[:n_in], *refs[n_in + len(after):])


def _allgather8(block, name):
    m_per, n = block.shape

    def body(x_ref, out_ref, send_sems, recv_sems, local_sem):
        x, y, c = _place()
        me, sibling = (x, y, c), (x, y, 1 - c)
        chips = [(1 - x, y), (x, 1 - y), (1 - x, 1 - y)]

        def rows(px, py, pc):
            return out_ref.at[pl.ds((4 * px + 2 * py + pc) * m_per, m_per), :]

        def copy(k, blk, to, src=None):
            return pltpu.make_async_remote_copy(
                src_ref=rows(*blk) if src is None else src, dst_ref=rows(*blk),
                send_sem=send_sems.at[k], recv_sem=recv_sems.at[k],
                device_id=to, device_id_type=MESH)

        mine = pltpu.make_async_copy(x_ref, rows(*me), local_sem)
        mine.start()
        first = [copy(0, me, sibling, src=x_ref)]
        first += [copy(1 + j, me, (*chip, c), src=x_ref) for j, chip in enumerate(chips)]
        for cp in first:
            cp.start()
        passed = [copy(4 + j, (*chip, c), sibling) for j, chip in enumerate(chips)]
        for j, chip in enumerate(chips):
            copy(1 + j, (*chip, c), me).wait_recv()
            passed[j].start()
        copy(0, sibling, me).wait_recv()
        for j, chip in enumerate(chips):
            copy(4 + j, (*chip, 1 - c), me).wait_recv()
        for cp in first + passed:
            cp.wait_send()
        mine.wait()

    return pl.pallas_call(
        body, name=name,
        out_shape=jax.ShapeDtypeStruct((8 * m_per, n), block.dtype),
        in_specs=[pl.BlockSpec(memory_space=pltpu.VMEM)],
        out_specs=pl.BlockSpec(memory_space=pltpu.VMEM),
        scratch_shapes=[pltpu.SemaphoreType.DMA((7,)), pltpu.SemaphoreType.DMA((7,)),
                        pltpu.SemaphoreType.DMA],
        compiler_params=_params(),
    )(block)


def _handshake(peers):
    barrier = pltpu.get_barrier_semaphore()
    for peer in peers:
        pl.semaphore_signal(barrier, inc=1, device_id=peer, device_id_type=MESH)
    pl.semaphore_wait(barrier, len(peers))


def _gather_weights(shards, by_cols, name, collective_id, after=()):
    n_arr = len(shards)

    def body(*refs):
        srcs, outs = refs[:n_arr], refs[n_arr + len(after):2 * n_arr + len(after)]
        send_sems, recv_sems, local_sems = refs[2 * n_arr + len(after):]
        x, y, c = _place()
        me_dev, sibling = (x, y, c), (x, y, 1 - c)
        chips = [(1 - x, y), (x, 1 - y), (1 - x, 1 - y)]
        me = 2 * x + y
        _handshake([sibling] + [(*chip, c) for chip in chips])

        def place(k, chip_idx, rows):
            if by_cols[k]:
                width = srcs[k].shape[1]
                return outs[k].at[rows, pl.ds(pl.multiple_of(chip_idx * width, 128), width)]
            return outs[k].at[chip_idx, rows]

        def copy(k, slot, chip_idx, half_sel, to, from_shard=False):
            half = srcs[k].shape[0] // 2
            rows = pl.ds(half_sel * half, half)
            dst = place(k, chip_idx, rows)
            return pltpu.make_async_remote_copy(
                src_ref=srcs[k].at[rows] if from_shard else dst, dst_ref=dst,
                send_sem=send_sems.at[6 * k + slot], recv_sem=recv_sems.at[6 * k + slot],
                device_id=to, device_id_type=MESH)

        own = [pltpu.make_async_copy(srcs[k], place(k, me, pl.ds(0, srcs[k].shape[0])), local_sems.at[k])
               for k in range(n_arr)]
        for cp in own:
            cp.start()
        sent = []
        for k in range(n_arr):
            for j, chip in enumerate(chips):
                sent.append(copy(k, j, me, c, (*chip, c), from_shard=True))
                sent[-1].start()
        for k in range(n_arr):
            for j, chip in enumerate(chips):
                chip_idx = 2 * chip[0] + chip[1]
                copy(k, j, chip_idx, c, me_dev).wait_recv()
                sent.append(copy(k, 3 + j, chip_idx, c, sibling))
                sent[-1].start()
        for k in range(n_arr):
            for j, chip in enumerate(chips):
                copy(k, 3 + j, 2 * chip[0] + chip[1], 1 - c, me_dev).wait_recv()
        for cp in sent:
            cp.wait_send()
        for cp in own:
            cp.wait()

    def gathered(k):
        r, cols = shards[k].shape
        return (r, N_CHIPS * cols) if by_cols[k] else (N_CHIPS, r, cols)

    return pl.kernel(
        body, name=name,
        out_type=[jax.ShapeDtypeStruct(gathered(k), shards[k].dtype) for k in range(n_arr)],
        mesh=plsc.ScalarSubcoreMesh(axis_name="sequencer", num_cores=1),
        scratch_types=[pltpu.SemaphoreType.DMA((6 * n_arr,)), pltpu.SemaphoreType.DMA((6 * n_arr,)),
                       pltpu.SemaphoreType.DMA((n_arr,))],
        compiler_params=pltpu.CompilerParams(collective_id=collective_id),
    )(*shards, *after)


def _rs_pair_exchange(grads, name):
    n_arr = len(grads)

    def body(*refs):
        srcs, outs = refs[:n_arr], refs[n_arr:2 * n_arr]
        send_sems, recv_sems = refs[2 * n_arr:]
        x, y, c = _place()
        cps = []
        for k in range(n_arr):
            half = srcs[k].shape[1] // 2
            cps.append(pltpu.make_async_remote_copy(
                src_ref=srcs[k].at[:, pl.ds((1 - c) * half, half)], dst_ref=outs[k],
                send_sem=send_sems.at[k], recv_sem=recv_sems.at[k],
                device_id=(x, y, 1 - c), device_id_type=MESH))
            cps[-1].start()
        for cp in cps:
            cp.wait_recv()
        for cp in cps:
            cp.wait_send()

    return pl.pallas_call(
        body, name=name,
        out_shape=[jax.ShapeDtypeStruct((g.shape[0], g.shape[1] // 2, g.shape[2]), g.dtype) for g in grads],
        in_specs=[ANY] * n_arr, out_specs=[ANY] * n_arr,
        scratch_shapes=[pltpu.SemaphoreType.DMA((n_arr,)), pltpu.SemaphoreType.DMA((n_arr,))],
        compiler_params=_params(),
    )(*grads)


def _rs_chip_exchange(sums, name, collective_id):
    n_arr = len(sums)

    def body(*refs):
        srcs, outs = refs[:n_arr], refs[n_arr:2 * n_arr]
        send_sems, recv_sems = refs[2 * n_arr:]
        x, y, c = _place()
        chips = [(1 - x, y), (x, 1 - y), (1 - x, 1 - y)]
        _handshake([(*chip, c) for chip in chips])
        cps = []
        for k in range(n_arr):
            for j, chip in enumerate(chips):
                cps.append(pltpu.make_async_remote_copy(
                    src_ref=srcs[k].at[2 * chip[0] + chip[1]], dst_ref=outs[k].at[j],
                    send_sem=send_sems.at[3 * k + j], recv_sem=recv_sems.at[3 * k + j],
                    device_id=(*chip, c), device_id_type=MESH))
                cps[-1].start()
        for cp in cps:
            cp.wait_recv()
        for cp in cps:
            cp.wait_send()

    return pl.kernel(
        body, name=name,
        out_type=[jax.ShapeDtypeStruct((3,) + s.shape[1:], s.dtype) for s in sums],
        mesh=plsc.ScalarSubcoreMesh(axis_name="sequencer", num_cores=1),
        scratch_types=[pltpu.SemaphoreType.DMA((3 * n_arr,)), pltpu.SemaphoreType.DMA((3 * n_arr,))],
        compiler_params=pltpu.CompilerParams(collective_id=collective_id),
    )(*sums)


def _rs_share(totals, name):
    n_arr = len(totals)

    def body(*refs):
        outs = refs[n_arr:2 * n_arr]
        send_sems, recv_sems = refs[2 * n_arr:]
        x, y, c = _place()

        def half_rows(k, sel):
            return outs[k].at[sel]

        cps = []
        for k in range(n_arr):
            cps.append(pltpu.make_async_remote_copy(
                src_ref=half_rows(k, c), dst_ref=half_rows(k, c), send_sem=send_sems.at[k], recv_sem=recv_sems.at[k],
                device_id=(x, y, 1 - c), device_id_type=MESH))
            cps[-1].start()
        for k in range(n_arr):
            pltpu.make_async_remote_copy(
                src_ref=half_rows(k, c), dst_ref=half_rows(k, 1 - c), send_sem=send_sems.at[k],
                recv_sem=recv_sems.at[k], device_id=(x, y, 1 - c), device_id_type=MESH).wait_recv()
        for cp in cps:
            cp.wait_send()

    shared = pl.pallas_call(
        body, name=name,
        out_shape=[jax.ShapeDtypeStruct(t.shape, t.dtype) for t in totals],
        in_specs=[ANY] * n_arr, out_specs=[ANY] * n_arr,
        input_output_aliases={k: k for k in range(n_arr)},
        scratch_shapes=[pltpu.SemaphoreType.DMA((n_arr,)), pltpu.SemaphoreType.DMA((n_arr,))],
        compiler_params=_params(),
    )(*totals)
    return [t.reshape(2 * t.shape[1], t.shape[2]) for t in shared]


def _pair_add(grad, recv, c_idx, name):
    n, r, cols = grad.shape
    half = r // 2
    rows = half // 2

    def body(_, g_ref, r_ref, o_ref):
        o_ref[...] = (g_ref[...].astype(F32) + r_ref[...].astype(F32)).astype(o_ref.dtype)

    return pl.pallas_call(
        body, name=name,
        grid_spec=pltpu.PrefetchScalarGridSpec(
            num_scalar_prefetch=1, grid=(n, 2),
            in_specs=[pl.BlockSpec((None, None, rows, cols), lambda s, i, ci: (s, ci[0], i, 0)),
                      pl.BlockSpec((None, rows, cols), lambda s, i, ci: (s, i, 0))],
            out_specs=pl.BlockSpec((None, rows, cols), lambda s, i, ci: (s, i, 0))),
        out_shape=jax.ShapeDtypeStruct((n, half, cols), BF16),
        compiler_params=_params(2),
    )(c_idx, grad.reshape(n, 2, half, cols), recv)


def _chip_add(sums, recv, chip_and_core, name, after=()):
    _, half, cols = sums.shape
    rows = half // 2

    def body(_, s_ref, r0_ref, r1_ref, r2_ref, o_ref):
        o_ref[...] = ((s_ref[...].astype(F32) + r0_ref[...].astype(F32))
                      + r1_ref[...].astype(F32)) + r2_ref[...].astype(F32)

    def recv_spec(j):
        return pl.BlockSpec((None, rows, cols), lambda i, ci: (j, i, 0))

    return pl.pallas_call(
        _ordered(body, 5, after), name=name,
        grid_spec=pltpu.PrefetchScalarGridSpec(
            num_scalar_prefetch=1, grid=(2,),
            in_specs=[pl.BlockSpec((None, rows, cols), lambda i, ci: (ci[0], i, 0)),
                      recv_spec(0), recv_spec(1), recv_spec(2)] + [ANY] * len(after),
            out_specs=pl.BlockSpec((None, rows, cols), lambda i, ci: (ci[1], i, 0))),
        out_shape=jax.ShapeDtypeStruct((2, half, cols), F32),
        compiler_params=_params(1),
    )(chip_and_core, sums, recv, recv, recv, *after)


def _rms(x):
    return lax.rsqrt(jnp.mean(x * x, axis=-1, keepdims=True) + NORM_EPS)


def _norm_mod(x, mod, name):
    s_len, d = x.shape
    tm = TOKEN_TILE

    def body(x_ref, mod_ref, h_ref, ht_ref):
        xv = x_ref[...]
        n = (xv * _rms(xv)) * mod_ref[3:4, :]
        h = n * (1.0 + mod_ref[1:2, :]) + mod_ref[0:1, :]
        h_ref[...] = h.astype(BF16)
        ht_ref[...] = h.T.astype(BF16)

    return pl.pallas_call(
        body, name=name, grid=(s_len // tm,),
        in_specs=[pl.BlockSpec((tm, d), lambda i: (i, 0)), pl.BlockSpec((8, d), lambda i: (0, 0))],
        out_specs=[pl.BlockSpec((tm, d), lambda i: (i, 0)), pl.BlockSpec((d, tm), lambda i: (0, i))],
        out_shape=[jax.ShapeDtypeStruct((s_len, d), BF16), jax.ShapeDtypeStruct((d, s_len), BF16)],
        compiler_params=_params(1),
    )(x, mod)


def _norm_bwd(dh, x, mod, dxo, y_raw, coef, name, after=()):
    s_len, d = x.shape
    tm = TOKEN_TILE

    def body(dh_ref, x_ref, mod_ref, dxo_ref, y_ref, dx_ref, st_ref):
        @pl.when(pl.program_id(0) == 0)
        def _():
            st_ref[...] = jnp.zeros_like(st_ref)

        xv, dhv, dxov = x_ref[...], dh_ref[...], dxo_ref[...]
        r = _rms(xv)
        xh = xv * r
        gain, scale = mod_ref[3:4, :], mod_ref[1:2, :]
        dn = dhv * (1.0 + scale)
        dxh = dn * gain
        dx_ref[...] = dxov + r * (dxh - xh * jnp.mean(dxh * xh, axis=-1, keepdims=True))
        st_ref[0:1, :] += jnp.sum(dhv, axis=0, keepdims=True)
        st_ref[1:2, :] += jnp.sum(dhv * (xh * gain), axis=0, keepdims=True)
        st_ref[2:3, :] += coef * jnp.sum(y_ref[...] * dxov, axis=0, keepdims=True)
        st_ref[3:4, :] += jnp.sum(dn * xh, axis=0, keepdims=True)

    tile = pl.BlockSpec((tm, d), lambda i: (i, 0))
    small = pl.BlockSpec((8, d), lambda i: (0, 0))
    return pl.pallas_call(
        _ordered(body, 5, after), name=name, grid=(s_len // tm,),
        in_specs=[tile, tile, small, tile, tile] + [ANY] * len(after),
        out_specs=[tile, small],
        out_shape=[jax.ShapeDtypeStruct((s_len, d), F32), jax.ShapeDtypeStruct((8, d), F32)],
        compiler_params=_params(1),
    )(dh, x, mod, dxo, y_raw, *after)


def _loss_grad(y, target, name):
    s_len, d = y.shape
    tm = TOKEN_TILE

    def body(y_ref, t_ref, dy_ref, part_ref):
        @pl.when(pl.program_id(0) == 0)
        def _():
            part_ref[...] = jnp.zeros_like(part_ref)

        err = y_ref[...] - t_ref[...]
        dy_ref[...] = err * (1.0 / d)
        sq = err * err
        part_ref[...] += jnp.sum(sq.reshape(tm // 8, 8, d), axis=0)

    tile = pl.BlockSpec((tm, d), lambda i: (i, 0))
    return pl.pallas_call(
        body, name=name, grid=(s_len // tm,),
        in_specs=[tile, tile],
        out_specs=[tile, pl.BlockSpec((8, d), lambda i: (0, 0))],
        out_shape=[jax.ShapeDtypeStruct((s_len, d), F32), jax.ShapeDtypeStruct((8, d), F32)],
        compiler_params=_params(1),
    )(y, target)


def _adamw_math(w, g, m, v):
    m = ADAM_B1 * m + (1.0 - ADAM_B1) * g
    v = ADAM_B2 * v + (1.0 - ADAM_B2) * (g * g)
    m_hat = m / (1.0 - ADAM_B1 ** ADAM_STEP)
    v_hat = v / (1.0 - ADAM_B2 ** ADAM_STEP)
    delta = -ADAM_LR * (m_hat / (jnp.sqrt(v_hat) + ADAM_EPS) + ADAM_WD * w)
    return delta, m, v


def _adamw(w, g, m, v, name, after=()):
    r, cols = w.shape
    tr = r // 8 if r % 64 == 0 else r

    def body(w_ref, g_ref, m_ref, v_ref, d_ref, nm_ref, nv_ref):
        d_ref[...], nm_ref[...], nv_ref[...] = _adamw_math(w_ref[...], g_ref[...], m_ref[...], v_ref[...])

    tile = pl.BlockSpec((tr, cols), lambda i: (i, 0))
    shape = jax.ShapeDtypeStruct((r, cols), F32)
    return pl.pallas_call(
        _ordered(body, 4, after), name=name, grid=(r // tr,),
        in_specs=[tile] * 4 + [ANY] * len(after), out_specs=[tile] * 3, out_shape=[shape] * 3,
        compiler_params=_params(1),
    )(w, g, m, v, *after)


def _in_parts(tm, n_qkv, n_rest):
    def part(lo, n_blk):
        return pl.BlockSpec((tm, IN_BLOCK), lambda i, j: (i, jnp.clip(j - lo, 0, n_blk - 1)))
    return [part(0, n_qkv), part(n_qkv, n_qkv), part(2 * n_qkv, n_qkv), part(3 * n_qkv, n_rest)]


def _pick_part(j, n_qkv, refs, fn):
    bounds = [0, n_qkv, 2 * n_qkv, 3 * n_qkv]
    for p, ref in enumerate(refs):
        inside = j >= bounds[p]
        if p + 1 < len(refs):
            inside = inside & (j < bounds[p + 1])
        pl.when(inside)(lambda ref=ref: fn(ref))


def _in_proj(h, w, name):
    s_len, d = h.shape
    tm = PROJ_TILE
    steps = w.shape[1] // IN_BLOCK
    n_qkv = 3 * QKV // IN_BLOCK

    def body(h_ref, w_ref, qkv_ref, rest_ref):
        j = pl.program_id(1)
        res = _dot(h_ref[...], w_ref[...])

        @pl.when(j < n_qkv)
        def _():
            qkv_ref[...] = res

        @pl.when(j >= n_qkv)
        def _():
            rest_ref[...] = res

    return pl.pallas_call(
        body, name=name, grid=(s_len // tm, steps),
        in_specs=[pl.BlockSpec((tm, d), lambda i, j: (i, 0)), pl.BlockSpec((d, IN_BLOCK), lambda i, j: (0, j))],
        out_specs=[pl.BlockSpec((tm, IN_BLOCK), lambda i, j: (i, jnp.minimum(j, n_qkv - 1))),
                   pl.BlockSpec((tm, IN_BLOCK), lambda i, j: (i, jnp.maximum(j - n_qkv, 0)))],
        out_shape=[jax.ShapeDtypeStruct((s_len, 3 * QKV), F32),
                   jax.ShapeDtypeStruct((s_len, w.shape[1] - 3 * QKV), F32)],
        compiler_params=_params(2),
    )(h, w)


def _in_proj_bwd(dq, dk, dv, drest, w, name, after=()):
    s_len = dq.shape[0]
    d = w.shape[0]
    tm = PROJ_TILE
    steps = w.shape[1] // IN_BLOCK
    n_qkv = QKV // IN_BLOCK

    def body(dq_ref, dk_ref, dv_ref, dr_ref, w_ref, o_ref, acc_ref):
        j = pl.program_id(1)

        @pl.when(j == 0)
        def _():
            acc_ref[...] = jnp.zeros_like(acc_ref)

        def add(a_ref):
            acc_ref[...] += _dot_nt(a_ref[...], w_ref[...])

        _pick_part(j, n_qkv, [dq_ref, dk_ref, dv_ref, dr_ref], add)

        @pl.when(j == steps - 1)
        def _():
            o_ref[...] = acc_ref[...]

    return pl.pallas_call(
        _ordered(body, 5, after), name=name, grid=(s_len // tm, steps),
        in_specs=(_in_parts(tm, n_qkv, steps - 3 * n_qkv) + [pl.BlockSpec((d, IN_BLOCK), lambda i, j: (0, j))]
                  + [ANY] * len(after)),
        out_specs=pl.BlockSpec((tm, d), lambda i, j: (i, 0)),
        out_shape=jax.ShapeDtypeStruct((s_len, d), F32),
        scratch_shapes=[pltpu.VMEM((tm, d), F32)],
        compiler_params=_params(2),
    )(dq, dk, dv, drest, w, *after)


def _in_proj_wgrad(ht, dq, dk, dv, drest, name):
    d, s_len = ht.shape
    ts = PROJ_TILE
    t_steps = s_len // ts
    n_qkv = QKV // IN_BLOCK
    n_cols = 3 * QKV + drest.shape[1]
    steps = n_cols // IN_BLOCK

    def body(h_ref, dq_ref, dk_ref, dv_ref, dr_ref, o_ref, acc_ref):
        j, s = pl.program_id(0), pl.program_id(1)

        @pl.when(s == 0)
        def _():
            acc_ref[...] = jnp.zeros_like(acc_ref)

        def add(a_ref):
            acc_ref[...] += _dot(h_ref[...], a_ref[...])

        _pick_part(j, n_qkv, [dq_ref, dk_ref, dv_ref, dr_ref], add)

        @pl.when(s == t_steps - 1)
        def _():
            o_ref[...] = acc_ref[...].astype(BF16)

    parts = [pl.BlockSpec(p.block_shape, lambda j, s, f=p.index_map: f(s, j))
             for p in _in_parts(ts, n_qkv, steps - 3 * n_qkv)]
    return pl.pallas_call(
        body, name=name, grid=(steps, t_steps),
        in_specs=[pl.BlockSpec((d, ts), lambda j, s: (0, s))] + parts,
        out_specs=pl.BlockSpec((d, IN_BLOCK), lambda j, s: (0, j)),
        out_shape=jax.ShapeDtypeStruct((d, n_cols), BF16),
        scratch_shapes=[pltpu.VMEM((d, IN_BLOCK), F32)],
        compiler_params=_params(2),
    )(ht, dq, dk, dv, drest)


def _wgrad(x, y, x_spec, y_spec, out_shape, out_spec, acc_shape, n_chunks, name, x_transposed=False, after=()):
    s_len = y.shape[-2]
    ts = WGRAD_TILE
    steps = s_len // ts

    def body(x_ref, y_ref, o_ref, acc_ref):
        s = pl.program_id(1)

        @pl.when(s == 0)
        def _():
            acc_ref[...] = jnp.zeros_like(acc_ref)

        acc_ref[...] += (_dot if x_transposed else _dot_tn)(x_ref[...], y_ref[...])

        @pl.when(s == steps - 1)
        def _():
            o_ref[...] = acc_ref[...].astype(o_ref.dtype)

    return pl.pallas_call(
        _ordered(body, 2, after), name=name, grid=(n_chunks, steps),
        in_specs=[x_spec(ts), y_spec(ts)] + [ANY] * len(after), out_specs=out_spec,
        out_shape=jax.ShapeDtypeStruct(out_shape, BF16),
        scratch_shapes=[pltpu.VMEM(acc_shape, F32)],
        compiler_params=_params(2),
    )(x, y, *after)


def _ffn_fwd(x, h, mod, w_gate, w_up, w_down, name):
    s_len, d = x.shape
    n_chunks, _, fs = w_gate.shape
    tm = TOKEN_TILE

    def body(x_ref, h_ref, mod_ref, wg_ref, wu_ref, wd_ref, xo_ref, g_ref, u_ref, y_ref, acc_ref):
        j = pl.program_id(1)

        @pl.when(j == 0)
        def _():
            acc_ref[...] = jnp.zeros_like(acc_ref)

        hv = h_ref[...]
        g = _dot(hv, wg_ref[...])
        u = _dot(hv, wu_ref[...])
        g_ref[...] = g.astype(BF16)
        u_ref[...] = u.astype(BF16)
        act = (g * _sigmoid(g)) * u
        acc_ref[...] += _dot(act.astype(BF16), wd_ref[...])

        @pl.when(j == n_chunks - 1)
        def _():
            yv = acc_ref[...]
            y_ref[...] = yv
            xo_ref[...] = x_ref[...] + 0.5 * mod_ref[2:3, :] * yv

    tile = pl.BlockSpec((tm, d), lambda i, j: (i, 0))
    hid = pl.BlockSpec((None, tm, fs), lambda i, j: (j, i, 0))
    w_in_spec = pl.BlockSpec((None, d, fs), lambda i, j: (j, 0, 0))
    return pl.pallas_call(
        body, name=name, grid=(s_len // tm, n_chunks),
        in_specs=[tile, tile, pl.BlockSpec((8, d), lambda i, j: (0, 0)), w_in_spec, w_in_spec,
                  pl.BlockSpec((None, fs, d), lambda i, j: (j, 0, 0))],
        out_specs=[tile, hid, hid, tile],
        out_shape=[jax.ShapeDtypeStruct((s_len, d), F32),
                   jax.ShapeDtypeStruct((n_chunks, s_len, fs), BF16),
                   jax.ShapeDtypeStruct((n_chunks, s_len, fs), BF16),
                   jax.ShapeDtypeStruct((s_len, d), F32)],
        scratch_shapes=[pltpu.VMEM((tm, d), F32)],
        compiler_params=_params(2),
    )(x, h, mod, w_gate, w_up, w_down)


def _ffn_bwd(dxo, mod, g_pre, u_pre, w_gate, w_up, w_down, name):
    s_len, d = dxo.shape
    n_chunks, _, fs = w_gate.shape
    tm = TOKEN_TILE

    def body(dxo_ref, mod_ref, g_ref, u_ref, wg_ref, wu_ref, wd_ref, dh_ref, dg_ref, du_ref, a_ref, dy_ref, acc_ref):
        j = pl.program_id(1)

        @pl.when(j == 0)
        def _():
            dy_ref[...] = (0.5 * mod_ref[2:3, :] * dxo_ref[...]).astype(BF16)
            acc_ref[...] = jnp.zeros_like(acc_ref)

        da = _dot_nt(dy_ref[...], wd_ref[...])
        g = g_ref[...].astype(F32)
        u = u_ref[...].astype(F32)
        sg = _sigmoid(g)
        silu = g * sg
        dg = (da * u * (sg * (1.0 + g * (1.0 - sg)))).astype(BF16)
        du = (da * silu).astype(BF16)
        dg_ref[...] = dg
        du_ref[...] = du
        a_ref[...] = (silu * u).astype(BF16)
        acc_ref[...] += _dot_nt(dg, wg_ref[...]) + _dot_nt(du, wu_ref[...])

        @pl.when(j == n_chunks - 1)
        def _():
            dh_ref[...] = acc_ref[...]

    tile = pl.BlockSpec((tm, d), lambda i, j: (i, 0))
    hid = pl.BlockSpec((None, tm, fs), lambda i, j: (j, i, 0))
    w_in_spec = pl.BlockSpec((None, d, fs), lambda i, j: (j, 0, 0))
    hid_shape = jax.ShapeDtypeStruct((n_chunks, s_len, fs), BF16)
    return pl.pallas_call(
        body, name=name, grid=(s_len // tm, n_chunks),
        in_specs=[tile, pl.BlockSpec((8, d), lambda i, j: (0, 0)), hid, hid, w_in_spec, w_in_spec,
                  pl.BlockSpec((None, fs, d), lambda i, j: (j, 0, 0))],
        out_specs=[tile, hid, hid, hid, tile],
        out_shape=[jax.ShapeDtypeStruct((s_len, d), F32), hid_shape, hid_shape, hid_shape,
                   jax.ShapeDtypeStruct((s_len, d), BF16)],
        scratch_shapes=[pltpu.VMEM((tm, d), F32)],
        compiler_params=_params(2),
    )(dxo, mod, g_pre, u_pre, w_gate, w_up, w_down)


def _ffn_wgrads(ht, dg, du, act, dy, tag, after=()):
    n_chunks, s_len, fs = dg.shape
    d = ht.shape[0]
    tok = lambda ts: pl.BlockSpec((ts, d), lambda c, s: (s, 0))
    tok_t = lambda ts: pl.BlockSpec((d, ts), lambda c, s: (0, s))
    hid = lambda ts: pl.BlockSpec((None, ts, fs), lambda c, s: (c, s, 0))
    d_up = pl.BlockSpec((None, d, fs), lambda c, s: (c, 0, 0))
    d_down = pl.BlockSpec((None, fs, d), lambda c, s: (c, 0, 0))
    dwg = _wgrad(ht, dg, tok_t, hid, (n_chunks, d, fs), d_up, (d, fs), n_chunks, tag + "_dwg", True, after)
    dwu = _wgrad(ht, du, tok_t, hid, (n_chunks, d, fs), d_up, (d, fs), n_chunks, tag + "_dwu", True, after)
    dwd = _wgrad(act, dy, hid, tok, (n_chunks, fs, d), d_down, (fs, d), n_chunks, tag + "_dwd", False, after)
    return dwg, dwu, dwd


def _band_bias():
    qi = lax.broadcasted_iota(jnp.int32, (ATTN_BLOCK, 2 * ATTN_BLOCK), 0)
    kj = lax.broadcasted_iota(jnp.int32, (ATTN_BLOCK, 2 * ATTN_BLOCK), 1)
    band = (kj >= qi) & (kj <= qi + ATTN_BLOCK)
    return jnp.where(band, 0.0, NEG), jnp.where(band & (kj >= ATTN_BLOCK), 0.0, NEG)


def _rows(base, count, stride):
    return pl.ds(base, count) if stride == 1 else pl.ds(base, count, stride=stride)


def _qkv_specs(slab_of):
    def spec(sect, back):
        return pl.BlockSpec((SLAB, HEAD_DIM),
                            lambda h, s, g: (jnp.maximum(slab_of(s) - back, 0), (sect * N_GROUPS + g) * HEADS + h))
    return [spec(0, 0), spec(1, 0), spec(2, 0), spec(1, 1), spec(2, 1)]


def _load_qkv(q_ref, k_ref, v_ref, kp_ref, vp_ref, qn_ref, kn_ref, qs, kb, vb, n):
    e = HEAD_DIM
    q = q_ref[...]
    qs[...] = (q * _rms(q)) * qn_ref[...]
    k = k_ref[...]
    kb[SLAB:2 * SLAB, :] = (k * _rms(k)) * kn_ref[...]
    vb[SLAB:2 * SLAB, :] = v_ref[...]

    @pl.when(n > 0)
    def _():
        kp = kp_ref[...]
        kb[0:SLAB, :] = (kp * _rms(kp)) * kn_ref[...]
        vb[0:SLAB, :] = vp_ref[...]

    @pl.when(n == 0)
    def _():
        kb[0:SLAB, :] = jnp.zeros((SLAB, e), F32)
        vb[0:SLAB, :] = jnp.zeros((SLAB, e), F32)


def _for_each_tile(dil, n, tile_fn):
    span = ATTN_BLOCK * dil
    bias, first_bias = _band_bias()

    def sub(jj, carry):
        start = pl.multiple_of(jj * span, ATTN_BLOCK)
        tile_bias = jnp.where(jnp.logical_and(n == 0, jj == 0), first_bias, bias)
        for r in range(dil):
            tile_fn(_rows(start + r, ATTN_BLOCK, dil), _rows(SLAB - span + start + r, 2 * ATTN_BLOCK, dil), tile_bias)
        return carry

    lax.fori_loop(0, SLAB // span, sub, 0)


def _attn_fwd(qkv, q_norm, k_norm, name):
    s_len = qkv.shape[0]
    e = HEAD_DIM
    n_slabs = s_len // SLAB

    def body(q_ref, k_ref, v_ref, kp_ref, vp_ref, qn_ref, kn_ref, o_ref, lse_ref, qs, kb, vb, m_s, l_s, acc_s):
        n, grp = pl.program_id(1), pl.program_id(2)
        _load_qkv(q_ref, k_ref, v_ref, kp_ref, vp_ref, qn_ref, kn_ref, qs, kb, vb, n)

        def run(gi, dil):
            def tile(q_rows, kv_rows, bias):
                q = qs[q_rows, :].astype(BF16)
                k = kb[kv_rows, :].astype(BF16)
                v = vb[kv_rows, :].astype(BF16)
                s = _dot_nt(q, k) * ATTN_SCALE + bias
                m = jnp.max(s, axis=-1, keepdims=True)
                p = jnp.exp(s - m)
                m_s.at[gi][q_rows, :] = jnp.broadcast_to(m, (ATTN_BLOCK, e))
                l_s.at[gi][q_rows, :] = jnp.broadcast_to(jnp.sum(p, axis=-1, keepdims=True), (ATTN_BLOCK, e))
                acc_s.at[gi][q_rows, :] = _dot(p.astype(BF16), v)

            _for_each_tile(dil, n, tile)

        for gi, dil in enumerate(DILATIONS):
            pl.when(grp == gi)(lambda gi=gi, dil=dil: run(gi, dil))

        @pl.when(grp == N_GROUPS - 1)
        def _():
            m_all = jnp.maximum(jnp.maximum(m_s[0], m_s[1]), m_s[2])
            den = jnp.zeros((SLAB, e), F32)
            num = jnp.zeros((SLAB, e), F32)
            for gi in range(N_GROUPS):
                w = jnp.exp(m_s[gi] - m_all)
                den += l_s[gi] * w
                num += acc_s[gi] * w
            o_ref[...] = num / den
            lse_ref[...] = m_all + jnp.log(den)

    small = pl.BlockSpec((1, e), lambda h, n, g: (0, 0))
    out = pl.BlockSpec((SLAB, e), lambda h, n, g: (n, h))
    return pl.pallas_call(
        body, name=name, grid=(HEADS, n_slabs, N_GROUPS),
        in_specs=_qkv_specs(lambda n: n) + [small, small],
        out_specs=[out, out],
        out_shape=[jax.ShapeDtypeStruct((s_len, HEADS * e), F32)] * 2,
        scratch_shapes=[pltpu.VMEM((SLAB, e), F32), pltpu.VMEM((2 * SLAB, e), F32), pltpu.VMEM((2 * SLAB, e), F32),
                        pltpu.VMEM((N_GROUPS, SLAB, e), F32), pltpu.VMEM((N_GROUPS, SLAB, e), F32),
                        pltpu.VMEM((N_GROUPS, SLAB, e), F32)],
        compiler_params=_params(3),
    )(qkv, qkv, qkv, qkv, qkv, q_norm, k_norm)


def _attn_bwd(qkv, d_out, out, lse, q_norm, k_norm, name):
    s_len = qkv.shape[0]
    e = HEAD_DIM
    n_slabs = s_len // SLAB

    def body(q_ref, k_ref, v_ref, kp_ref, vp_ref, do_ref, o_ref, lse_ref, qn_ref, kn_ref, dq_ref, dk_ref, dv_ref,
             st_ref, qs, kb, vb, dqs, dkb, dvb, carry):
        head, step, grp = pl.program_id(0), pl.program_id(1), pl.program_id(2)
        n = n_slabs - 1 - step
        _load_qkv(q_ref, k_ref, v_ref, kp_ref, vp_ref, qn_ref, kn_ref, qs, kb, vb, n)
        dkb[...] = jnp.zeros_like(dkb)
        dvb[...] = jnp.zeros_like(dvb)

        @pl.when((head == 0) & (step == 0) & (grp == 0))
        def _():
            st_ref[...] = jnp.zeros_like(st_ref)

        def run(gi, dil):
            @pl.when(step == 0)
            def _():
                carry[gi] = jnp.zeros((2, SLAB, e), F32)

            def tile(q_rows, kv_rows, bias):
                q = qs[q_rows, :].astype(BF16)
                k = kb[kv_rows, :].astype(BF16)
                v = vb[kv_rows, :].astype(BF16)
                do = do_ref[q_rows, :]
                delta = jnp.sum(do * o_ref[q_rows, :], axis=-1, keepdims=True)
                s = _dot_nt(q, k) * ATTN_SCALE + bias
                p = jnp.exp(s - lse_ref[q_rows, :][:, 0:1])
                do16 = do.astype(BF16)
                ds = (p * (_dot_nt(do16, v) - delta) * ATTN_SCALE).astype(BF16)
                dqs[q_rows, :] = _dot(ds, k)
                dkb[kv_rows, :] += _dot_tn(ds, q)
                dvb[kv_rows, :] += _dot_tn(p.astype(BF16), do16)

            _for_each_tile(dil, n, tile)
            dk_hat = dkb[SLAB:2 * SLAB, :] + carry[gi, 0]
            dv = dvb[SLAB:2 * SLAB, :] + carry[gi, 1]
            carry[gi, 0] = dkb[0:SLAB, :]
            carry[gi, 1] = dvb[0:SLAB, :]

            def norm_bwd(raw, gain, d_hat):
                r = _rms(raw)
                y = raw * r
                dy = d_hat * gain
                return r * (dy - y * jnp.mean(dy * y, axis=-1, keepdims=True)), jnp.sum(d_hat * y, axis=0, keepdims=True)

            dq, dqn = norm_bwd(q_ref[...], qn_ref[...], dqs[...])
            dk, dkn = norm_bwd(k_ref[...], kn_ref[...], dk_hat)
            dq_ref[...] = dq.astype(BF16)
            dk_ref[...] = dk.astype(BF16)
            dv_ref[...] = dv.astype(BF16)
            st_ref[0:1, :] += dqn
            st_ref[1:2, :] += dkn

        for gi, dil in enumerate(DILATIONS):
            pl.when(grp == gi)(lambda gi=gi, dil=dil: run(gi, dil))

    slab_of = lambda s: n_slabs - 1 - s
    small = pl.BlockSpec((1, e), lambda h, s, g: (0, 0))
    head_blk = pl.BlockSpec((SLAB, e), lambda h, s, g: (slab_of(s), h))
    grad_blk = pl.BlockSpec((SLAB, e), lambda h, s, g: (slab_of(s), g * HEADS + h))
    grad_shape = jax.ShapeDtypeStruct((s_len, QKV), BF16)
    return pl.pallas_call(
        body, name=name, grid=(HEADS, n_slabs, N_GROUPS),
        in_specs=_qkv_specs(slab_of) + [head_blk, head_blk, head_blk, small, small],
        out_specs=[grad_blk, grad_blk, grad_blk, pl.BlockSpec((8, e), lambda h, s, g: (0, 0))],
        out_shape=[grad_shape, grad_shape, grad_shape, jax.ShapeDtypeStruct((8, e), F32)],
        scratch_shapes=[pltpu.VMEM((SLAB, e), F32), pltpu.VMEM((2 * SLAB, e), F32), pltpu.VMEM((2 * SLAB, e), F32),
                        pltpu.VMEM((SLAB, e), F32), pltpu.VMEM((2 * SLAB, e), F32), pltpu.VMEM((2 * SLAB, e), F32),
                        pltpu.VMEM((N_GROUPS, 2, SLAB, e), F32)],
        compiler_params=_params(3),
    )(qkv, qkv, qkv, qkv, qkv, d_out, out, lse, q_norm, k_norm)


def _shift_rows(x, by, edge, forward):
    t_len = x.shape[0]
    row = lax.broadcasted_iota(jnp.int32, x.shape, 0)
    if forward:
        out = pltpu.roll(x, by, 0)
        for i in range(by):
            out = jnp.where(row == i, edge[8 - by + i:8 - by + i + 1, :], out)
    else:
        out = pltpu.roll(x, t_len - by, 0)
        for i in range(by):
            out = jnp.where(row == t_len - by + i, edge[i:i + 1, :], out)
    return out


def _mix_fwd(x, o, rest, mod, conv_w, w_attn, w_conv, w_out, name):
    s_len, d = x.shape
    tm = MIX_TILE
    a_w = o.shape[1]

    def body(x_ref, o_ref, u_ref, b_ref, c_ref, ga_ref, gc_ref, mod_ref, cw_ref, wa_ref, wc_ref, wo_ref,
             xo_ref, z_ref, ya_ref, yc_ref, conv_ref, yb_ref, m_ref, o16_ref, carry):
        @pl.when(pl.program_id(0) == 0)
        def _():
            carry[...] = jnp.zeros_like(carry)

        xc = c_ref[...] * u_ref[...]
        edge = carry[...]
        conv = (_shift_rows(xc, 2, edge, True) * cw_ref[0:1, :] + _shift_rows(xc, 1, edge, True) * cw_ref[1:2, :]
                + xc * cw_ref[2:3, :])
        carry[...] = xc[tm - 8:tm, :]
        yb = (b_ref[...] * conv).astype(BF16)
        o16 = o_ref[...].astype(BF16)
        ya = _dot(o16, wa_ref[...])
        yc = _dot(yb, wc_ref[...])
        merged = (_sigmoid(ga_ref[...]) * ya + _sigmoid(gc_ref[...]) * yc).astype(BF16)
        z = _dot(merged, wo_ref[...])
        xo_ref[...] = x_ref[...] + mod_ref[2:3, :] * z
        z_ref[...] = z
        ya_ref[...] = ya.astype(BF16)
        yc_ref[...] = yc.astype(BF16)
        conv_ref[...] = conv.astype(BF16)
        yb_ref[...] = yb
        m_ref[...] = merged
        o16_ref[...] = o16

    tile = pl.BlockSpec((tm, d), lambda i: (i, 0))
    sect = lambda k: pl.BlockSpec((tm, d), lambda i: (i, k))
    att = pl.BlockSpec((tm, a_w), lambda i: (i, 0))
    const = lambda shape: pl.BlockSpec(shape, lambda i: (0, 0))
    f32_out = jax.ShapeDtypeStruct((s_len, d), F32)
    b16_out = jax.ShapeDtypeStruct((s_len, d), BF16)
    return pl.pallas_call(
        body, name=name, grid=(s_len // tm,),
        in_specs=[tile, att, sect(0), sect(1), sect(2), sect(3), sect(4), const((8, d)), const((8, d)),
                  const((a_w, d)), const((d, d)), const((d, d))],
        out_specs=[tile] * 7 + [att],
        out_shape=[f32_out, f32_out] + [b16_out] * 5 + [jax.ShapeDtypeStruct((s_len, a_w), BF16)],
        scratch_shapes=[pltpu.VMEM((8, d), F32)],
        compiler_params=_params(1),
    )(x, o, rest, rest, rest, rest, rest, mod, conv_w, w_attn, w_conv, w_out)


def _mix_bwd(dxo, ya, yc, conv, rest, mod, conv_w, w_attn, w_conv, w_out, a_w, name):
    s_len, d = dxo.shape
    tm = MIX_TILE
    n_tiles = s_len // tm

    def body(dxo_ref, ya_ref, yc_ref, conv_ref, u_ref, b_ref, c_ref, ga_ref, gc_ref, mod_ref, cw_ref,
             wa_ref, wc_ref, wo_ref, do_ref, drest_ref, dz_ref, dya_ref, dyc_ref, st_ref, carry):
        @pl.when(pl.program_id(0) == 0)
        def _():
            carry[...] = jnp.zeros_like(carry)
            st_ref[...] = jnp.zeros_like(st_ref)

        dz = (mod_ref[2:3, :] * dxo_ref[...]).astype(BF16)
        dz_ref[...] = dz
        dm = _dot_nt(dz, wo_ref[...])
        sa, sc = _sigmoid(ga_ref[...]), _sigmoid(gc_ref[...])
        dya = (dm * sa).astype(BF16)
        dyc = (dm * sc).astype(BF16)
        dya_ref[...] = dya
        dyc_ref[...] = dyc
        drest_ref[:, 3 * d:4 * d] = (dm * ya_ref[...].astype(F32) * (sa * (1.0 - sa))).astype(BF16)
        drest_ref[:, 4 * d:5 * d] = (dm * yc_ref[...].astype(F32) * (sc * (1.0 - sc))).astype(BF16)
        do_ref[...] = _dot_nt(dya, wa_ref[...])
        dyb = _dot_nt(dyc, wc_ref[...])
        drest_ref[:, d:2 * d] = (dyb * conv_ref[...].astype(F32)).astype(BF16)
        dconv = dyb * b_ref[...]
        edge = carry[...]
        sh1 = _shift_rows(dconv, 1, edge, False)
        sh2 = _shift_rows(dconv, 2, edge, False)
        carry[...] = dconv[0:8, :]
        dxc = dconv * cw_ref[2:3, :] + sh1 * cw_ref[1:2, :] + sh2 * cw_ref[0:1, :]
        u, c = u_ref[...], c_ref[...]
        xc = c * u
        drest_ref[:, 0:d] = (dxc * c).astype(BF16)
        drest_ref[:, 2 * d:3 * d] = (dxc * u).astype(BF16)
        st_ref[0:1, :] += jnp.sum(xc * sh2, axis=0, keepdims=True)
        st_ref[1:2, :] += jnp.sum(xc * sh1, axis=0, keepdims=True)
        st_ref[2:3, :] += jnp.sum(xc * dconv, axis=0, keepdims=True)

    rev = lambda i: n_tiles - 1 - i
    tile = pl.BlockSpec((tm, d), lambda i: (rev(i), 0))
    sect = lambda k: pl.BlockSpec((tm, d), lambda i: (rev(i), k))
    const = lambda shape: pl.BlockSpec(shape, lambda i: (0, 0))
    b16_out = jax.ShapeDtypeStruct((s_len, d), BF16)
    return pl.pallas_call(
        body, name=name, grid=(n_tiles,),
        in_specs=[tile, tile, tile, tile, sect(0), sect(1), sect(2), sect(3), sect(4), const((8, d)), const((8, d)),
                  const((a_w, d)), const((d, d)), const((d, d))],
        out_specs=[pl.BlockSpec((tm, a_w), lambda i: (rev(i), 0)), pl.BlockSpec((tm, 5 * d), lambda i: (rev(i), 0)),
                   tile, tile, tile, const((8, d))],
        out_shape=[jax.ShapeDtypeStruct((s_len, a_w), F32), jax.ShapeDtypeStruct((s_len, 5 * d), BF16),
                   b16_out, b16_out, b16_out, jax.ShapeDtypeStruct((8, d), F32)],
        scratch_shapes=[pltpu.VMEM((8, d), F32)],
        compiler_params=_params(1),
    )(dxo, ya, yc, conv, rest, rest, rest, rest, rest, mod, conv_w, w_attn, w_conv, w_out)


ADA_COLS = 128


def _ada_fwd(c_all, w_shard, b_shard, name):
    d, cols = w_shard.shape

    def body(c_ref, w_ref, b_ref, o_ref):
        cv = c_ref[...]
        o_ref[...] = jnp.dot(cv * _sigmoid(cv), w_ref[...], preferred_element_type=F32,
                             precision=lax.Precision.HIGHEST) + b_ref[...]

    return pl.pallas_call(
        body, name=name, grid=(cols // ADA_COLS,),
        in_specs=[pl.BlockSpec((8, d), lambda j: (0, 0)), pl.BlockSpec((d, ADA_COLS), lambda j: (0, j)),
                  pl.BlockSpec((1, ADA_COLS), lambda j: (0, j))],
        out_specs=pl.BlockSpec((8, ADA_COLS), lambda j: (0, j)),
        out_shape=jax.ShapeDtypeStruct((8, cols), F32),
        compiler_params=_params(1),
    )(c_all, w_shard, b_shard)


def _ada_bwd(c_all, dmod_shard, w, m, v, name):
    d, cols = w.shape

    def body(c_ref, dm_ref, w_ref, m_ref, v_ref, g_ref, d_ref, nm_ref, nv_ref):
        cv = c_ref[...]
        g = lax.dot_general(cv * _sigmoid(cv), dm_ref[...], (((0,), (0,)), ((), ())),
                            preferred_element_type=F32, precision=lax.Precision.HIGHEST)
        g_ref[...] = g
        d_ref[...], nm_ref[...], nv_ref[...] = _adamw_math(w_ref[...], g, m_ref[...], v_ref[...])

    blk = pl.BlockSpec((d, ADA_COLS), lambda j: (0, j))
    shape = jax.ShapeDtypeStruct((d, cols), F32)
    return pl.pallas_call(
        body, name=name, grid=(cols // ADA_COLS,),
        in_specs=[pl.BlockSpec((8, d), lambda j: (0, 0)), pl.BlockSpec((8, ADA_COLS), lambda j: (0, j)), blk, blk, blk],
        out_specs=[blk] * 4, out_shape=[shape] * 4,
        compiler_params=_params(1),
    )(c_all, dmod_shard, w, m, v)


def _small_update(parts, w, m, v, name):
    n = w.shape[1]

    def body(p_ref, w_ref, m_ref, v_ref, g_ref, d_ref, nm_ref, nv_ref):
        g = p_ref[0:1, :]
        for i in range(1, 8):
            g = g + p_ref[i:i + 1, :]
        g_ref[...] = g
        d_ref[...], nm_ref[...], nv_ref[...] = _adamw_math(w_ref[...], g, m_ref[...], v_ref[...])

    shape = jax.ShapeDtypeStruct((1, n), F32)
    return pl.pallas_call(body, name=name, out_shape=[shape] * 4, compiler_params=_params())(parts, w, m, v)


def _cols_to_shards(w, n):
    r, nc = w.shape
    return w.reshape(r, n, nc // n).transpose(1, 0, 2)


def kernel(x, c, w_ada, b_ada, norm_ffn1, ffn1_w_gate, ffn1_w_up, ffn1_w_down, norm_mix, w_in, q_norm, k_norm, conv_w, w_attn_branch, w_conv_branch, w_out, norm_ffn2, ffn2_w_gate, ffn2_w_up, ffn2_w_down, loss_target, m_w_ada, m_b_ada, m_norm_ffn1, m_ffn1_w_gate, m_ffn1_w_up, m_ffn1_w_down, m_norm_mix, m_w_in, m_q_norm, m_k_norm, m_conv_w, m_w_attn_branch, m_w_conv_branch, m_w_out, m_norm_ffn2, m_ffn2_w_gate, m_ffn2_w_up, m_ffn2_w_down, v_w_ada, v_b_ada, v_norm_ffn1, v_ffn1_w_gate, v_ffn1_w_up, v_ffn1_w_down, v_norm_mix, v_w_in, v_q_norm, v_k_norm, v_conv_w, v_w_attn_branch, v_w_conv_branch, v_w_out, v_norm_ffn2, v_ffn2_w_gate, v_ffn2_w_up, v_ffn2_w_down):
    ix, iy, ic = _place()
    chip = 2 * ix + iy
    me = 4 * ix + 2 * iy + ic
    xs = x[0]
    target = loss_target[0]
    s_len, d = xs.shape
    ada_cols = w_ada.shape[2]
    conv_cols = conv_w.shape[2]

    conv_rows = jnp.zeros((8, conv_cols), F32).at[0:3].set(conv_w[0])
    small_in = jnp.concatenate([jnp.broadcast_to(c, (8, d)), conv_rows], axis=1)
    small_all = _allgather8(small_in, "gather_c").reshape(8, 8, d + conv_cols)
    c_all = small_all[:, 0, :d]
    conv_full = small_all[0::2, 0:3, d:].transpose(1, 0, 2).reshape(3, N_CHIPS * conv_cols)
    conv_pad = jnp.zeros((8, N_CHIPS * conv_cols), F32).at[0:3].set(conv_full)
    b_shard = lax.dynamic_slice(b_ada, (0, chip * ada_cols), (1, ada_cols))
    mod_part = _ada_fwd(c_all, w_ada[0], b_shard, "ada_fwd")
    mod_all = _allgather8(mod_part, "gather_mod").reshape(N_CHIPS, 2, 8, ada_cols)[:, 0]
    mod_mine = lax.dynamic_slice(mod_all, (0, me, 0), (N_CHIPS, 1, ada_cols)).reshape(9, d)

    def mod_rows(i, gain):
        return jnp.zeros((8, d), F32).at[0:3].set(mod_mine[3 * i:3 * i + 3]).at[3:4].set(gain)

    mod1, mod2, mod3 = mod_rows(0, norm_ffn1), mod_rows(1, norm_mix), mod_rows(2, norm_ffn2)

    to16 = lambda w: w[0].astype(BF16)
    wg1, wu1, wd1 = _gather_weights([to16(ffn1_w_gate), to16(ffn1_w_up), to16(ffn1_w_down)], [False] * 3,
                                    "gather_ffn1", 1)
    (w_in_full,) = _gather_weights([to16(w_in)], [True], "gather_w_in", 2, after=(wd1,))

    h1, h1t = _norm_mod(xs, mod1, "norm1")
    x1, g1, u1, y1 = _ffn_fwd(xs, h1, mod1, wg1, wu1, wd1, "ffn1_fwd")
    h2, h2t = _norm_mod(x1, mod2, "norm2")
    qkv, rest = _in_proj(h2, w_in_full, "in_proj")
    w_ab, w_cb_g, w_o_g, wg2, wu2, wd2 = _gather_weights(
        [to16(w_attn_branch), to16(w_conv_branch), to16(w_out),
         to16(ffn2_w_gate), to16(ffn2_w_up), to16(ffn2_w_down)], [True] + [False] * 5,
        "gather_rest", 3, after=(qkv,))
    a_w = w_ab.shape[0]
    w_cb = w_cb_g.reshape(d, d)
    w_o = w_o_g.reshape(d, d)
    o, lse = _attn_fwd(qkv, q_norm, k_norm, "attn_fwd")
    x2, z, ya, yc, conv, yb, merged, o16 = _mix_fwd(x1, o, rest, mod2, conv_pad, w_ab, w_cb, w_o, "mix_fwd")
    h3, h3t = _norm_mod(x2, mod3, "norm3")
    x3, g3, u3, y3 = _ffn_fwd(x2, h3, mod3, wg2, wu2, wd2, "ffn2_fwd")
    dx3, loss_part = _loss_grad(x3, target, "loss")
    loss = lax.psum(0.5 * jnp.sum(loss_part) / d, ("x", "y", "c"))

    c_idx = jnp.reshape(ic, (1,)).astype(jnp.int32)
    chip_idx = jnp.stack([chip, ic]).astype(jnp.int32)

    def reduce_start(grads, names, tag, collective_id):
        from_sibling = _rs_pair_exchange(grads, "rs_pair_" + tag)
        pair_sums = [_pair_add(g, r, c_idx, "pair_add_" + nm) for g, r, nm in zip(grads, from_sibling, names)]
        return pair_sums, _rs_chip_exchange(pair_sums, "rs_chips_" + tag, collective_id)

    def reduce_finish(pair_sums, from_chips, names, tag, after):
        totals = [_chip_add(p, r, chip_idx, "chip_add_" + nm, after)
                  for p, r, nm in zip(pair_sums, from_chips, names)]
        return dict(zip(names, _rs_share(totals, "rs_share_" + tag)))

    names_a = ["ffn2_w_gate", "ffn2_w_up", "ffn2_w_down"]
    names_b = ["w_in", "w_attn_branch", "w_conv_branch", "w_out"]
    names_c = ["ffn1_w_gate", "ffn1_w_up", "ffn1_w_down"]

    dh3, dg3, du3, a3, dy3 = _ffn_bwd(dx3, mod3, g3, u3, wg2, wu2, wd2, "ffn2_bwd")
    sums_a, chips_a = reduce_start(list(_ffn_wgrads(h3t, dg3, du3, a3, dy3, "ffn2")), names_a, "a", 4)
    dx2, st3 = _norm_bwd(dh3, x2, mod3, dx3, y3, 0.5, "norm3_bwd", after=tuple(sums_a))

    do, drest, dz, dya, dyc, st_conv = _mix_bwd(dx2, ya, yc, conv, rest, mod2, conv_pad, w_ab, w_cb, w_o, a_w, "mix_bwd")
    dq, dk, dv, st_qk = _attn_bwd(qkv, do, o, lse, q_norm, k_norm, "attn_bwd")
    tok = lambda width: (lambda ts: pl.BlockSpec((ts, width), lambda cc, s: (s, 0)))
    colblk = lambda width: (lambda ts: pl.BlockSpec((ts, width), lambda cc, s: (s, cc)))
    dw_in = _cols_to_shards(_in_proj_wgrad(h2t, dq, dk, dv, drest, "dw_in"), N_CHIPS)
    shard_w = d // N_CHIPS
    dw_ab = _wgrad(o16, dya, tok(a_w), colblk(shard_w), (a_w, d), pl.BlockSpec((a_w, shard_w), lambda cc, s: (0, cc)),
                   (a_w, shard_w), N_CHIPS, "dw_attn_branch")
    dw_ab = _cols_to_shards(dw_ab, N_CHIPS)
    row_out = pl.BlockSpec((None, shard_w, d), lambda cc, s: (cc, 0, 0))
    dw_cb = _wgrad(yb, dyc, colblk(shard_w), tok(d), (N_CHIPS, shard_w, d), row_out, (shard_w, d), N_CHIPS, "dw_conv_branch")
    dw_o = _wgrad(merged, dz, colblk(shard_w), tok(d), (N_CHIPS, shard_w, d), row_out, (shard_w, d), N_CHIPS, "dw_out")
    shard_grads = reduce_finish(sums_a, chips_a, names_a, "a", after=(dw_in, dw_o))
    sums_b, chips_b = reduce_start([dw_in, dw_ab, dw_cb, dw_o], names_b, "b", 5)

    dh2 = _in_proj_bwd(dq, dk, dv, drest, w_in_full, "in_proj_bwd", after=tuple(sums_b))
    dx1, st2 = _norm_bwd(dh2, x1, mod2, dx2, z, 1.0, "norm2_bwd")
    dh1, dg1, du1, a1, dy1 = _ffn_bwd(dx1, mod1, g1, u1, wg1, wu1, wd1, "ffn1_bwd")
    dx0, st1 = _norm_bwd(dh1, xs, mod1, dx1, y1, 0.5, "norm1_bwd")
    grads_c = list(_ffn_wgrads(h1t, dg1, du1, a1, dy1, "ffn1"))
    shard_grads.update(reduce_finish(sums_b, chips_b, names_b, "b", after=tuple(grads_c)))
    sums_c, chips_c = reduce_start(grads_c, names_c, "c", 6)

    dmod = jnp.concatenate([st1[0:3], st2[0:3], st3[0:3]], axis=0).reshape(1, 9 * d)
    small = jnp.concatenate([dmod, st1[3:4], st2[3:4], st3[3:4], st_qk[0:1], st_qk[1:2],
                             st_conv[0:3].reshape(1, 3 * d)], axis=1)
    small_all = _allgather8(jnp.broadcast_to(small, (8, small.shape[1])), "gather_small").reshape(8, 8, -1)[:, 0]
    dmod_all = small_all[:, :9 * d]
    dmod_shard = lax.dynamic_slice(dmod_all, (0, chip * ada_cols), (8, ada_cols))
    g_w_ada, d_w_ada, nm_w_ada, nv_w_ada = _ada_bwd(c_all, dmod_shard, w_ada[0], m_w_ada[0], v_w_ada[0], "ada_bwd")

    vec_names = ["b_ada", "norm_ffn1", "norm_mix", "norm_ffn2", "q_norm", "k_norm"]
    vec_w = [b_ada, norm_ffn1, norm_mix, norm_ffn2, q_norm, k_norm]
    vec_m = [m_b_ada, m_norm_ffn1, m_norm_mix, m_norm_ffn2, m_q_norm, m_k_norm]
    vec_v = [v_b_ada, v_norm_ffn1, v_norm_mix, v_norm_ffn2, v_q_norm, v_k_norm]
    n_vec = sum(w.shape[1] for w in vec_w)
    cat = lambda arrs: jnp.concatenate(arrs, axis=1)
    vec_out = _small_update(small_all[:, :n_vec], cat(vec_w), cat(vec_m), cat(vec_v), "small_update")
    conv_parts = small_all[:, n_vec:].reshape(8, 3, N_CHIPS * conv_cols)
    conv_parts = lax.dynamic_slice(conv_parts, (0, 0, chip * conv_cols), (8, 3, conv_cols)).reshape(8, 3 * conv_cols)
    flat3 = lambda w: w[0].reshape(1, 3 * conv_cols)
    conv_out = _small_update(conv_parts, flat3(conv_w), flat3(m_conv_w), flat3(v_conv_w), "conv_update")

    res = {"w_ada": [t[None] for t in (g_w_ada, d_w_ada, nm_w_ada, nv_w_ada)],
           "conv_w": [t.reshape(1, 3, conv_cols) for t in conv_out]}
    off = 0
    for nm, w in zip(vec_names, vec_w):
        width = w.shape[1]
        res[nm] = [t[:, off:off + width] for t in vec_out]
        off += width
    big = {"ffn1_w_gate": (ffn1_w_gate, m_ffn1_w_gate, v_ffn1_w_gate), "ffn1_w_up": (ffn1_w_up, m_ffn1_w_up, v_ffn1_w_up),
           "ffn1_w_down": (ffn1_w_down, m_ffn1_w_down, v_ffn1_w_down), "w_in": (w_in, m_w_in, v_w_in),
           "w_attn_branch": (w_attn_branch, m_w_attn_branch, v_w_attn_branch),
           "w_conv_branch": (w_conv_branch, m_w_conv_branch, v_w_conv_branch), "w_out": (w_out, m_w_out, v_w_out),
           "ffn2_w_gate": (ffn2_w_gate, m_ffn2_w_gate, v_ffn2_w_gate), "ffn2_w_up": (ffn2_w_up, m_ffn2_w_up, v_ffn2_w_up),
           "ffn2_w_down": (ffn2_w_down, m_ffn2_w_down, v_ffn2_w_down)}
    def update(nm, after=()):
        w, m, v = big[nm]
        g = shard_grads[nm]
        delta, new_m, new_v = _adamw(w[0], g, m[0], v[0], "adamw_" + nm, after)
        res[nm] = [t[None] for t in (g, delta, new_m, new_v)]
        return new_v

    last = tuple(sums_c)
    for nm in names_a + names_b:
        last = (update(nm, last),)
    shard_grads.update(reduce_finish(sums_c, chips_c, names_c, "c", after=last))
    for nm in names_c:
        update(nm)

    order = ["w_ada", "b_ada", "norm_ffn1", "ffn1_w_gate", "ffn1_w_up", "ffn1_w_down", "norm_mix", "w_in", "q_norm",
             "k_norm", "conv_w", "w_attn_branch", "w_conv_branch", "w_out", "norm_ffn2", "ffn2_w_gate", "ffn2_w_up",
             "ffn2_w_down"]
    return (loss, dx0[None], *[res[nm][0] for nm in order], *[res[nm][1] for nm in order],
            *[res[nm][2] for nm in order], *[res[nm][3] for nm in order])
```

```python
import jax
import jax.numpy as jnp
from jax import lax
from jax.experimental import pallas as pl
from jax.experimental.pallas import tpu as pltpu
from jax.experimental.pallas import tpu_sc as plsc

F32 = jnp.float32
BF16 = jnp.bfloat16
MESH = pl.DeviceIdType.MESH
ANY = pl.BlockSpec(memory_space=pl.ANY)

NORM_EPS = 1e-6
HEAD_DIM = 128
N_GROUPS = 3
HEADS = 4
DILATIONS = (1, 4, 16)
ATTN_BLOCK = 128
SLAB = ATTN_BLOCK * max(DILATIONS)
QKV = N_GROUPS * HEADS * HEAD_DIM
ATTN_SCALE = HEAD_DIM ** -0.5
NEG = -1e30
N_CHIPS = 4

ADAM_LR = 0.001
ADAM_B1 = 0.9
ADAM_B2 = 0.999
ADAM_EPS = 1e-08
ADAM_WD = 0.01
ADAM_STEP = 10

VMEM_LIMIT_BYTES = 56 * 1024 * 1024
TOKEN_TILE = 512
FFN_TILE = 1024
PROJ_TILE = 2048
WGRAD_TILE = 2048
IN_BLOCK = 512
MIX_TILE = 256


def _params(n_axes=0):
    return pltpu.CompilerParams(
        dimension_semantics=("arbitrary",) * n_axes if n_axes else None,
        vmem_limit_bytes=VMEM_LIMIT_BYTES)


def _dot(a, b):
    return jnp.dot(a, b, preferred_element_type=F32)


def _dot_nt(a, b):
    return lax.dot_general(a, b, (((1,), (1,)), ((), ())), preferred_element_type=F32)


def _dot_tn(a, b):
    return lax.dot_general(a, b, (((0,), (0,)), ((), ())), preferred_element_type=F32)


def _sigmoid(x):
    return 1.0 / (1.0 + jnp.exp(-x))


def _place():
    return lax.axis_index("x"), lax.axis_index("y"), lax.axis_index("c")


def _ordered(body, n_in, after):
    if not after:
        return body
    return lambda *refs: body(*refs[:n_in], *refs[n_in + len(after):])


def _allgather8(block, name):
    m_per, n = block.shape

    def body(x_ref, out_ref, send_sems, recv_sems, local_sem):
        x, y, c = _place()
        me, sibling = (x, y, c), (x, y, 1 - c)
        chips = [(1 - x, y), (x, 1 - y), (1 - x, 1 - y)]

        def rows(px, py, pc):
            return out_ref.at[pl.ds((4 * px + 2 * py + pc) * m_per, m_per), :]

        def copy(k, blk, to, src=None):
            return pltpu.make_async_remote_copy(
                src_ref=rows(*blk) if src is None else src, dst_ref=rows(*blk),
                send_sem=send_sems.at[k], recv_sem=recv_sems.at[k],
                device_id=to, device_id_type=MESH)

        mine = pltpu.make_async_copy(x_ref, rows(*me), local_sem)
        mine.start()
        first = [copy(0, me, sibling, src=x_ref)]
        first += [copy(1 + j, me, (*chip, c), src=x_ref) for j, chip in enumerate(chips)]
        for cp in first:
            cp.start()
        passed = [copy(4 + j, (*chip, c), sibling) for j, chip in enumerate(chips)]
        for j, chip in enumerate(chips):
            copy(1 + j, (*chip, c), me).wait_recv()
            passed[j].start()
        copy(0, sibling, me).wait_recv()
        for j, chip in enumerate(chips):
            copy(4 + j, (*chip, 1 - c), me).wait_recv()
        for cp in first + passed:
            cp.wait_send()
        mine.wait()

    return pl.pallas_call(
        body, name=name,
        out_shape=jax.ShapeDtypeStruct((8 * m_per, n), block.dtype),
        in_specs=[pl.BlockSpec(memory_space=pltpu.VMEM)],
        out_specs=pl.BlockSpec(memory_space=pltpu.VMEM),
        scratch_shapes=[pltpu.SemaphoreType.DMA((7,)), pltpu.SemaphoreType.DMA((7,)),
                        pltpu.SemaphoreType.DMA],
        compiler_params=_params(),
    )(block)


def _handshake(peers):
    barrier = pltpu.get_barrier_semaphore()
    for peer in peers:
        pl.semaphore_signal(barrier, inc=1, device_id=peer, device_id_type=MESH)
    pl.semaphore_wait(barrier, len(peers))


def _gather_weights(shards, by_cols, name, collective_id, after=()):
    n_arr = len(shards)

    def body(*refs):
        srcs, outs = refs[:n_arr], refs[n_arr + len(after):2 * n_arr + len(after)]
        send_sems, recv_sems, local_sems = refs[2 * n_arr + len(after):]
        x, y, c = _place()
        me_dev, sibling = (x, y, c), (x, y, 1 - c)
        chips = [(1 - x, y), (x, 1 - y), (1 - x, 1 - y)]
        me = 2 * x + y
        _handshake([sibling] + [(*chip, c) for chip in chips])

        def place(k, chip_idx, rows):
            if by_cols[k]:
                width = srcs[k].shape[1]
                return outs[k].at[rows, pl.ds(pl.multiple_of(chip_idx * width, 128), width)]
            return outs[k].at[chip_idx, rows]

        def copy(k, slot, chip_idx, half_sel, to, from_shard=False):
            half = srcs[k].shape[0] // 2
            rows = pl.ds(half_sel * half, half)
            dst = place(k, chip_idx, rows)
            return pltpu.make_async_remote_copy(
                src_ref=srcs[k].at[rows] if from_shard else dst, dst_ref=dst,
                send_sem=send_sems.at[6 * k + slot], recv_sem=recv_sems.at[6 * k + slot],
                device_id=to, device_id_type=MESH)

        own = [pltpu.make_async_copy(srcs[k], place(k, me, pl.ds(0, srcs[k].shape[0])), local_sems.at[k])
               for k in range(n_arr)]
        for cp in own:
            cp.start()
        sent = []
        for k in range(n_arr):
            for j, chip in enumerate(chips):
                sent.append(copy(k, j, me, c, (*chip, c), from_shard=True))
                sent[-1].start()
        for k in range(n_arr):
            for j, chip in enumerate(chips):
                chip_idx = 2 * chip[0] + chip[1]
                copy(k, j, chip_idx, c, me_dev).wait_recv()
                sent.append(copy(k, 3 + j, chip_idx, c, sibling))
                sent[-1].start()
        for k in range(n_arr):
            for j, chip in enumerate(chips):
                copy(k, 3 + j, 2 * chip[0] + chip[1], 1 - c, me_dev).wait_recv()
        for cp in sent:
            cp.wait_send()
        for cp in own:
            cp.wait()

    def gathered(k):
        r, cols = shards[k].shape
        return (r, N_CHIPS * cols) if by_cols[k] else (N_CHIPS, r, cols)

    return pl.kernel(
        body, name=name,
        out_type=[jax.ShapeDtypeStruct(gathered(k), shards[k].dtype) for k in range(n_arr)],
        mesh=plsc.ScalarSubcoreMesh(axis_name="sequencer", num_cores=1),
        scratch_types=[pltpu.SemaphoreType.DMA((6 * n_arr,)), pltpu.SemaphoreType.DMA((6 * n_arr,)),
                       pltpu.SemaphoreType.DMA((n_arr,))],
        compiler_params=pltpu.CompilerParams(collective_id=collective_id),
    )(*shards, *after)


def _rs_pair_exchange(grads, name):
    n_arr = len(grads)

    def body(*refs):
        srcs, outs = refs[:n_arr], refs[n_arr:2 * n_arr]
        send_sems, recv_sems = refs[2 * n_arr:]
        x, y, c = _place()
        cps = []
        for k in range(n_arr):
            half = srcs[k].shape[1] // 2
            cps.append(pltpu.make_async_remote_copy(
                src_ref=srcs[k].at[:, pl.ds((1 - c) * half, half)], dst_ref=outs[k],
                send_sem=send_sems.at[k], recv_sem=recv_sems.at[k],
                device_id=(x, y, 1 - c), device_id_type=MESH))
            cps[-1].start()
        for cp in cps:
            cp.wait_recv()
        for cp in cps:
            cp.wait_send()

    return pl.pallas_call(
        body, name=name,
        out_shape=[jax.ShapeDtypeStruct((g.shape[0], g.shape[1] // 2, g.shape[2]), g.dtype) for g in grads],
        in_specs=[ANY] * n_arr, out_specs=[ANY] * n_arr,
        scratch_shapes=[pltpu.SemaphoreType.DMA((n_arr,)), pltpu.SemaphoreType.DMA((n_arr,))],
        compiler_params=_params(),
    )(*grads)


def _rs_chip_exchange(sums, name, collective_id):
    n_arr = len(sums)

    def body(*refs):
        srcs, outs = refs[:n_arr], refs[n_arr:2 * n_arr]
        send_sems, recv_sems = refs[2 * n_arr:]
        x, y, c = _place()
        chips = [(1 - x, y), (x, 1 - y), (1 - x, 1 - y)]
        _handshake([(*chip, c) for chip in chips])
        cps = []
        for k in range(n_arr):
            for j, chip in enumerate(chips):
                cps.append(pltpu.make_async_remote_copy(
                    src_ref=srcs[k].at[2 * chip[0] + chip[1]], dst_ref=outs[k].at[j],
                    send_sem=send_sems.at[3 * k + j], recv_sem=recv_sems.at[3 * k + j],
                    device_id=(*chip, c), device_id_type=MESH))
                cps[-1].start()
        for cp in cps:
            cp.wait_recv()
        for cp in cps:
            cp.wait_send()

    return pl.kernel(
        body, name=name,
        out_type=[jax.ShapeDtypeStruct((3,) + s.shape[1:], s.dtype) for s in sums],
        mesh=plsc.ScalarSubcoreMesh(axis_name="sequencer", num_cores=1),
        scratch_types=[pltpu.SemaphoreType.DMA((3 * n_arr,)), pltpu.SemaphoreType.DMA((3 * n_arr,))],
        compiler_params=pltpu.CompilerParams(collective_id=collective_id),
    )(*sums)


def _rs_share(totals, name):
    n_arr = len(totals)

    def body(*refs):
        outs = refs[n_arr:2 * n_arr]
        send_sems, recv_sems = refs[2 * n_arr:]
        x, y, c = _place()

        def half_rows(k, sel):
            return outs[k].at[sel]

        cps = []
        for k in range(n_arr):
            cps.append(pltpu.make_async_remote_copy(
                src_ref=half_rows(k, c), dst_ref=half_rows(k, c), send_sem=send_sems.at[k], recv_sem=recv_sems.at[k],
                device_id=(x, y, 1 - c), device_id_type=MESH))
            cps[-1].start()
        for k in range(n_arr):
            pltpu.make_async_remote_copy(
                src_ref=half_rows(k, c), dst_ref=half_rows(k, 1 - c), send_sem=send_sems.at[k],
                recv_sem=recv_sems.at[k], device_id=(x, y, 1 - c), device_id_type=MESH).wait_recv()
        for cp in cps:
            cp.wait_send()

    shared = pl.pallas_call(
        body, name=name,
        out_shape=[jax.ShapeDtypeStruct(t.shape, t.dtype) for t in totals],
        in_specs=[ANY] * n_arr, out_specs=[ANY] * n_arr,
        input_output_aliases={k: k for k in range(n_arr)},
        scratch_shapes=[pltpu.SemaphoreType.DMA((n_arr,)), pltpu.SemaphoreType.DMA((n_arr,))],
        compiler_params=_params(),
    )(*totals)
    return [t.reshape(2 * t.shape[1], t.shape[2]) for t in shared]


def _pair_add(grad, recv, c_idx, name):
    n, r, cols = grad.shape
    half = r // 2
    rows = half // 2

    def body(_, g_ref, r_ref, o_ref):
        o_ref[...] = (g_ref[...].astype(F32) + r_ref[...].astype(F32)).astype(o_ref.dtype)

    return pl.pallas_call(
        body, name=name,
        grid_spec=pltpu.PrefetchScalarGridSpec(
            num_scalar_prefetch=1, grid=(n, 2),
            in_specs=[pl.BlockSpec((None, None, rows, cols), lambda s, i, ci: (s, ci[0], i, 0)),
                      pl.BlockSpec((None, rows, cols), lambda s, i, ci: (s, i, 0))],
            out_specs=pl.BlockSpec((None, rows, cols), lambda s, i, ci: (s, i, 0))),
        out_shape=jax.ShapeDtypeStruct((n, half, cols), BF16),
        compiler_params=_params(2),
    )(c_idx, grad.reshape(n, 2, half, cols), recv)


def _chip_add(sums, recv, chip_and_core, name, after=()):
    _, half, cols = sums.shape
    rows = half // 2

    def body(_, s_ref, r0_ref, r1_ref, r2_ref, o_ref):
        o_ref[...] = ((s_ref[...].astype(F32) + r0_ref[...].astype(F32))
                      + r1_ref[...].astype(F32)) + r2_ref[...].astype(F32)

    def recv_spec(j):
        return pl.BlockSpec((None, rows, cols), lambda i, ci: (j, i, 0))

    return pl.pallas_call(
        _ordered(body, 5, after), name=name,
        grid_spec=pltpu.PrefetchScalarGridSpec(
            num_scalar_prefetch=1, grid=(2,),
            in_specs=[pl.BlockSpec((None, rows, cols), lambda i, ci: (ci[0], i, 0)),
                      recv_spec(0), recv_spec(1), recv_spec(2)] + [ANY] * len(after),
            out_specs=pl.BlockSpec((None, rows, cols), lambda i, ci: (ci[1], i, 0))),
        out_shape=jax.ShapeDtypeStruct((2, half, cols), F32),
        compiler_params=_params(1),
    )(chip_and_core, sums, recv, recv, recv, *after)


def _rms(x):
    return lax.rsqrt(jnp.mean(x * x, axis=-1, keepdims=True) + NORM_EPS)


def _norm_mod(x, mod, name, prev=None):
    s_len, d = x.shape
    tm = TOKEN_TILE

    def body(*refs):
        if prev is None:
            x_ref, mod_ref, h_ref, ht_ref = refs
            xv = x_ref[...]
        else:
            x_ref, y_ref, modp_ref, mod_ref, xo_ref, h_ref, ht_ref = refs
            xv = x_ref[...] + prev[2] * modp_ref[2:3, :] * y_ref[...]
            xo_ref[...] = xv
        n = (xv * _rms(xv)) * mod_ref[3:4, :]
        h = n * (1.0 + mod_ref[1:2, :]) + mod_ref[0:1, :]
        h_ref[...] = h.astype(BF16)
        ht_ref[...] = h.T.astype(BF16)

    tile = pl.BlockSpec((tm, d), lambda i: (i, 0))
    small = pl.BlockSpec((8, d), lambda i: (0, 0))
    h_specs = [tile, pl.BlockSpec((d, tm), lambda i: (0, i))]
    h_shapes = [jax.ShapeDtypeStruct((s_len, d), BF16), jax.ShapeDtypeStruct((d, s_len), BF16)]
    if prev is None:
        return pl.pallas_call(
            body, name=name, grid=(s_len // tm,), in_specs=[tile, small], out_specs=h_specs, out_shape=h_shapes,
            compiler_params=_params(1))(x, mod)
    return pl.pallas_call(
        body, name=name, grid=(s_len // tm,), in_specs=[tile, tile, small, small],
        out_specs=[tile] + h_specs, out_shape=[jax.ShapeDtypeStruct((s_len, d), F32)] + h_shapes,
        compiler_params=_params(1))(x, prev[0], prev[1], mod)


def _norm_bwd(dh, x, mod, dxo, y_raw, coef, name, after=(), prev=None):
    s_len, d = x.shape
    tm = TOKEN_TILE

    def body(*refs):
        if prev is None:
            dh_ref, x_ref, mod_ref, dxo_ref, y_ref, dx_ref, st_ref = refs
        else:
            dh_ref, x_ref, mod_ref, dxo_ref, y_ref, modp_ref, dx_ref, st_ref, dyp_ref = refs

        @pl.when(pl.program_id(0) == 0)
        def _():
            st_ref[...] = jnp.zeros_like(st_ref)

        xv, dhv, dxov = x_ref[...], dh_ref[...], dxo_ref[...]
        r = _rms(xv)
        xh = xv * r
        gain, scale = mod_ref[3:4, :], mod_ref[1:2, :]
        dn = dhv * (1.0 + scale)
        dxh = dn * gain
        dx = dxov + r * (dxh - xh * jnp.mean(dxh * xh, axis=-1, keepdims=True))
        dx_ref[...] = dx
        if prev is not None:
            dyp_ref[...] = (prev[1] * modp_ref[2:3, :] * dx).astype(BF16)
        st_ref[0:1, :] += jnp.sum(dhv, axis=0, keepdims=True)
        st_ref[1:2, :] += jnp.sum(dhv * (xh * gain), axis=0, keepdims=True)
        st_ref[2:3, :] += coef * jnp.sum(y_ref[...] * dxov, axis=0, keepdims=True)
        st_ref[3:4, :] += jnp.sum(dn * xh, axis=0, keepdims=True)

    tile = pl.BlockSpec((tm, d), lambda i: (i, 0))
    small = pl.BlockSpec((8, d), lambda i: (0, 0))
    operands = [dh, x, mod, dxo, y_raw] + ([] if prev is None else [prev[0]])
    in_specs = [tile, tile, small, tile, tile] + ([] if prev is None else [small])
    out_specs = [tile, small] + ([] if prev is None else [tile])
    out_shape = [jax.ShapeDtypeStruct((s_len, d), F32), jax.ShapeDtypeStruct((8, d), F32)]
    if prev is not None:
        out_shape.append(jax.ShapeDtypeStruct((s_len, d), BF16))
    return pl.pallas_call(
        _ordered(body, len(operands), after), name=name, grid=(s_len // tm,),
        in_specs=in_specs + [ANY] * len(after), out_specs=out_specs, out_shape=out_shape,
        compiler_params=_params(1),
    )(*operands, *after)


def _loss_grad(x, y, mod, target, name):
    s_len, d = x.shape
    tm = TOKEN_TILE

    def body(x_ref, y_ref, mod_ref, t_ref, do_ref, dy_ref, part_ref):
        @pl.when(pl.program_id(0) == 0)
        def _():
            part_ref[...] = jnp.zeros_like(part_ref)

        half_gate = 0.5 * mod_ref[2:3, :]
        err = (x_ref[...] + half_gate * y_ref[...]) - t_ref[...]
        do = err * (1.0 / d)
        do_ref[...] = do
        dy_ref[...] = (half_gate * do).astype(BF16)
        sq = err * err
        part_ref[...] += jnp.sum(sq.reshape(tm // 8, 8, d), axis=0)

    tile = pl.BlockSpec((tm, d), lambda i: (i, 0))
    small = pl.BlockSpec((8, d), lambda i: (0, 0))
    return pl.pallas_call(
        body, name=name, grid=(s_len // tm,),
        in_specs=[tile, tile, small, tile],
        out_specs=[tile, tile, small],
        out_shape=[jax.ShapeDtypeStruct((s_len, d), F32), jax.ShapeDtypeStruct((s_len, d), BF16),
                   jax.ShapeDtypeStruct((8, d), F32)],
        compiler_params=_params(1),
    )(x, y, mod, target)


def _adamw_math(w, g, m, v):
    m = ADAM_B1 * m + (1.0 - ADAM_B1) * g
    v = ADAM_B2 * v + (1.0 - ADAM_B2) * (g * g)
    m_hat = m / (1.0 - ADAM_B1 ** ADAM_STEP)
    v_hat = v / (1.0 - ADAM_B2 ** ADAM_STEP)
    delta = -ADAM_LR * (m_hat / (jnp.sqrt(v_hat) + ADAM_EPS) + ADAM_WD * w)
    return delta, m, v


def _adamw(w, g, m, v, name, after=()):
    r, cols = w.shape
    tr = r // 8 if r % 64 == 0 else r

    def body(w_ref, g_ref, m_ref, v_ref, d_ref, nm_ref, nv_ref):
        d_ref[...], nm_ref[...], nv_ref[...] = _adamw_math(w_ref[...], g_ref[...], m_ref[...], v_ref[...])

    tile = pl.BlockSpec((tr, cols), lambda i: (i, 0))
    shape = jax.ShapeDtypeStruct((r, cols), F32)
    return pl.pallas_call(
        _ordered(body, 4, after), name=name, grid=(r // tr,),
        in_specs=[tile] * 4 + [ANY] * len(after), out_specs=[tile] * 3, out_shape=[shape] * 3,
        compiler_params=_params(1),
    )(w, g, m, v, *after)


def _in_parts(tm, n_qkv, n_rest):
    def part(lo, n_blk):
        return pl.BlockSpec((tm, IN_BLOCK), lambda i, j: (i, jnp.clip(j - lo, 0, n_blk - 1)))
    return [part(0, n_qkv), part(n_qkv, n_qkv), part(2 * n_qkv, n_qkv), part(3 * n_qkv, n_rest)]


def _pick_part(j, n_qkv, refs, fn):
    bounds = [0, n_qkv, 2 * n_qkv, 3 * n_qkv]
    for p, ref in enumerate(refs):
        inside = j >= bounds[p]
        if p + 1 < len(refs):
            inside = inside & (j < bounds[p + 1])
        pl.when(inside)(lambda ref=ref: fn(ref))


def _in_proj(h, w, name):
    s_len, d = h.shape
    tm = PROJ_TILE
    steps = w.shape[1] // IN_BLOCK
    n_qkv = 3 * QKV // IN_BLOCK

    def body(h_ref, w_ref, qkv_ref, rest_ref):
        j = pl.program_id(1)
        res = _dot(h_ref[...], w_ref[...])

        @pl.when(j < n_qkv)
        def _():
            qkv_ref[...] = res

        @pl.when(j >= n_qkv)
        def _():
            rest_ref[...] = res

    return pl.pallas_call(
        body, name=name, grid=(s_len // tm, steps),
        in_specs=[pl.BlockSpec((tm, d), lambda i, j: (i, 0)), pl.BlockSpec((d, IN_BLOCK), lambda i, j: (0, j))],
        out_specs=[pl.BlockSpec((tm, IN_BLOCK), lambda i, j: (i, jnp.minimum(j, n_qkv - 1))),
                   pl.BlockSpec((tm, IN_BLOCK), lambda i, j: (i, jnp.maximum(j - n_qkv, 0)))],
        out_shape=[jax.ShapeDtypeStruct((s_len, 3 * QKV), F32),
                   jax.ShapeDtypeStruct((s_len, w.shape[1] - 3 * QKV), F32)],
        compiler_params=_params(2),
    )(h, w)


def _in_proj_bwd(dq, dk, dv, drest, w, name, after=()):
    s_len = dq.shape[0]
    d = w.shape[0]
    tm = PROJ_TILE
    steps = w.shape[1] // IN_BLOCK
    n_qkv = QKV // IN_BLOCK

    def body(dq_ref, dk_ref, dv_ref, dr_ref, w_ref, o_ref, acc_ref):
        j = pl.program_id(1)

        @pl.when(j == 0)
        def _():
            acc_ref[...] = jnp.zeros_like(acc_ref)

        def add(a_ref):
            acc_ref[...] += _dot_nt(a_ref[...], w_ref[...])

        _pick_part(j, n_qkv, [dq_ref, dk_ref, dv_ref, dr_ref], add)

        @pl.when(j == steps - 1)
        def _():
            o_ref[...] = acc_ref[...]

    return pl.pallas_call(
        _ordered(body, 5, after), name=name, grid=(s_len // tm, steps),
        in_specs=(_in_parts(tm, n_qkv, steps - 3 * n_qkv) + [pl.BlockSpec((d, IN_BLOCK), lambda i, j: (0, j))]
                  + [ANY] * len(after)),
        out_specs=pl.BlockSpec((tm, d), lambda i, j: (i, 0)),
        out_shape=jax.ShapeDtypeStruct((s_len, d), F32),
        scratch_shapes=[pltpu.VMEM((tm, d), F32)],
        compiler_params=_params(2),
    )(dq, dk, dv, drest, w, *after)


def _wgrad(x, y, x_spec, y_spec, out_shape, out_spec, acc_shape, n_chunks, name, x_transposed=False, after=()):
    s_len = y.shape[-2]
    ts = WGRAD_TILE
    steps = s_len // ts

    def body(x_ref, y_ref, o_ref, acc_ref):
        s = pl.program_id(1)

        @pl.when(s == 0)
        def _():
            acc_ref[...] = jnp.zeros_like(acc_ref)

        acc_ref[...] += (_dot if x_transposed else _dot_tn)(x_ref[...], y_ref[...])

        @pl.when(s == steps - 1)
        def _():
            o_ref[...] = acc_ref[...].astype(o_ref.dtype)

    return pl.pallas_call(
        _ordered(body, 2, after), name=name, grid=(n_chunks, steps),
        in_specs=[x_spec(ts), y_spec(ts)] + [ANY] * len(after), out_specs=out_spec,
        out_shape=jax.ShapeDtypeStruct(out_shape, BF16),
        scratch_shapes=[pltpu.VMEM(acc_shape, F32)],
        compiler_params=_params(2),
    )(x, y, *after)


def _pieces(width, piece=256):
    return [slice(a, min(a + piece, width)) for a in range(0, width, piece)]


def _ffn_fwd(h, w_gate, w_up, w_down, name):
    s_len, d = h.shape
    n_chunks, _, fs = w_gate.shape
    tm = FFN_TILE

    def body(h_ref, wg_ref, wu_ref, wd_ref, g_ref, u_ref, y_ref):
        j = pl.program_id(1)
        hv = h_ref[...]
        total = None
        for cols in _pieces(fs):
            g = _dot(hv, wg_ref[:, cols])
            u = _dot(hv, wu_ref[:, cols])
            g_ref[:, cols] = g.astype(BF16)
            u_ref[:, cols] = u.astype(BF16)
            act = (g * _sigmoid(g)) * u
            part = _dot(act.astype(BF16), wd_ref[cols, :])
            total = part if total is None else total + part

        @pl.when(j == 0)
        def _():
            y_ref[...] = total

        @pl.when(j > 0)
        def _():
            y_ref[...] += total

    tile = pl.BlockSpec((tm, d), lambda i, j: (i, 0))
    hid = pl.BlockSpec((None, tm, fs), lambda i, j: (j, i, 0))
    w_in_spec = pl.BlockSpec((None, d, fs), lambda i, j: (j, 0, 0))
    hid_shape = jax.ShapeDtypeStruct((n_chunks, s_len, fs), BF16)
    return pl.pallas_call(
        body, name=name, grid=(s_len // tm, n_chunks),
        in_specs=[tile, w_in_spec, w_in_spec, pl.BlockSpec((None, fs, d), lambda i, j: (j, 0, 0))],
        out_specs=[hid, hid, tile],
        out_shape=[hid_shape, hid_shape, jax.ShapeDtypeStruct((s_len, d), F32)],
        compiler_params=_params(2),
    )(h, w_gate, w_up, w_down)


def _ffn_bwd(dy, g_pre, u_pre, w_gate, w_up, w_down, name):
    s_len, d = dy.shape
    n_chunks, _, fs = w_gate.shape
    tm = FFN_TILE

    def body(dy_ref, g_ref, u_ref, wg_ref, wu_ref, wd_ref, dh_ref, dg_ref, du_ref, a_ref):
        j = pl.program_id(1)
        dyv = dy_ref[...]
        total = None
        for cols in _pieces(fs):
            da = _dot_nt(dyv, wd_ref[cols, :])
            g = g_ref[:, cols].astype(F32)
            u = u_ref[:, cols].astype(F32)
            sg = _sigmoid(g)
            silu = g * sg
            dg = (da * u * (sg * (1.0 + g * (1.0 - sg)))).astype(BF16)
            du = (da * silu).astype(BF16)
            dg_ref[:, cols] = dg
            du_ref[:, cols] = du
            a_ref[:, cols] = (silu * u).astype(BF16)
            part = _dot_nt(dg, wg_ref[:, cols]) + _dot_nt(du, wu_ref[:, cols])
            total = part if total is None else total + part

        @pl.when(j == 0)
        def _():
            dh_ref[...] = total

        @pl.when(j > 0)
        def _():
            dh_ref[...] += total

    tile = pl.BlockSpec((tm, d), lambda i, j: (i, 0))
    hid = pl.BlockSpec((None, tm, fs), lambda i, j: (j, i, 0))
    w_in_spec = pl.BlockSpec((None, d, fs), lambda i, j: (j, 0, 0))
    hid_shape = jax.ShapeDtypeStruct((n_chunks, s_len, fs), BF16)
    return pl.pallas_call(
        body, name=name, grid=(s_len // tm, n_chunks),
        in_specs=[tile, hid, hid, w_in_spec, w_in_spec, pl.BlockSpec((None, fs, d), lambda i, j: (j, 0, 0))],
        out_specs=[tile, hid, hid, hid],
        out_shape=[jax.ShapeDtypeStruct((s_len, d), F32), hid_shape, hid_shape, hid_shape],
        compiler_params=_params(2),
    )(dy, g_pre, u_pre, w_gate, w_up, w_down)


def _ffn_wgrads(ht, dg, du, act, dy, tag, after=()):
    n_chunks, s_len, fs = dg.shape
    d = ht.shape[0]
    tok = lambda ts: pl.BlockSpec((ts, d), lambda c, s: (s, 0))
    tok_t = lambda ts: pl.BlockSpec((d, ts), lambda c, s: (0, s))
    hid = lambda ts: pl.BlockSpec((None, ts, fs), lambda c, s: (c, s, 0))
    d_up = pl.BlockSpec((None, d, fs), lambda c, s: (c, 0, 0))
    d_down = pl.BlockSpec((None, fs, d), lambda c, s: (c, 0, 0))
    dwg = _wgrad(ht, dg, tok_t, hid, (n_chunks, d, fs), d_up, (d, fs), n_chunks, tag + "_dwg", True, after)
    dwu = _wgrad(ht, du, tok_t, hid, (n_chunks, d, fs), d_up, (d, fs), n_chunks, tag + "_dwu", True, after)
    dwd = _wgrad(act, dy, hid, tok, (n_chunks, fs, d), d_down, (fs, d), n_chunks, tag + "_dwd", False, after)
    return dwg, dwu, dwd


def _band_bias():
    qi = lax.broadcasted_iota(jnp.int32, (ATTN_BLOCK, 2 * ATTN_BLOCK), 0)
    kj = lax.broadcasted_iota(jnp.int32, (ATTN_BLOCK, 2 * ATTN_BLOCK), 1)
    band = (kj >= qi) & (kj <= qi + ATTN_BLOCK)
    return jnp.where(band, 0.0, NEG), jnp.where(band & (kj >= ATTN_BLOCK), 0.0, NEG)


def _rows(base, count, stride):
    return pl.ds(base, count) if stride == 1 else pl.ds(base, count, stride=stride)


def _qkv_specs(slab_of):
    def spec(sect, back):
        return pl.BlockSpec((SLAB, HEAD_DIM),
                            lambda h, s, g: (jnp.maximum(slab_of(s) - back, 0), (sect * N_GROUPS + g) * HEADS + h))
    return [spec(0, 0), spec(1, 0), spec(2, 0), spec(1, 1), spec(2, 1)]


def _load_qkv(q_ref, k_ref, v_ref, kp_ref, vp_ref, qn_ref, kn_ref, qs, kb, vb, n):
    e = HEAD_DIM
    q = q_ref[...]
    qs[...] = (q * _rms(q)) * qn_ref[...]
    k = k_ref[...]
    kb[SLAB:2 * SLAB, :] = (k * _rms(k)) * kn_ref[...]
    vb[SLAB:2 * SLAB, :] = v_ref[...]

    @pl.when(n > 0)
    def _():
        kp = kp_ref[...]
        kb[0:SLAB, :] = (kp * _rms(kp)) * kn_ref[...]
        vb[0:SLAB, :] = vp_ref[...]

    @pl.when(n == 0)
    def _():
        kb[0:SLAB, :] = jnp.zeros((SLAB, e), F32)
        vb[0:SLAB, :] = jnp.zeros((SLAB, e), F32)


def _for_each_tile(dil, n, tile_fn):
    span = ATTN_BLOCK * dil
    bias, first_bias = _band_bias()

    def sub(jj, carry):
        start = pl.multiple_of(jj * span, ATTN_BLOCK)
        tile_bias = jnp.where(jnp.logical_and(n == 0, jj == 0), first_bias, bias)
        for r in range(dil):
            tile_fn(_rows(start + r, ATTN_BLOCK, dil), _rows(SLAB - span + start + r, 2 * ATTN_BLOCK, dil), tile_bias)
        return carry

    lax.fori_loop(0, SLAB // span, sub, 0)


def _attn_fwd(qkv, q_norm, k_norm, name):
    s_len = qkv.shape[0]
    e = HEAD_DIM
    n_slabs = s_len // SLAB

    def body(q_ref, k_ref, v_ref, kp_ref, vp_ref, qn_ref, kn_ref, o_ref, lse_ref, qs, kb, vb, m_s, l_s, acc_s):
        n, grp = pl.program_id(1), pl.program_id(2)
        _load_qkv(q_ref, k_ref, v_ref, kp_ref, vp_ref, qn_ref, kn_ref, qs, kb, vb, n)

        def run(gi, dil):
            def tile(q_rows, kv_rows, bias):
                q = qs[q_rows, :].astype(BF16)
                k = kb[kv_rows, :].astype(BF16)
                v = vb[kv_rows, :].astype(BF16)
                s = _dot_nt(q, k) * ATTN_SCALE + bias
                m = jnp.max(s, axis=-1, keepdims=True)
                p = jnp.exp(s - m)
                m_s.at[gi][q_rows, :] = jnp.broadcast_to(m, (ATTN_BLOCK, e))
                l_s.at[gi][q_rows, :] = jnp.broadcast_to(jnp.sum(p, axis=-1, keepdims=True), (ATTN_BLOCK, e))
                acc_s.at[gi][q_rows, :] = _dot(p.astype(BF16), v)

            _for_each_tile(dil, n, tile)

        for gi, dil in enumerate(DILATIONS):
            pl.when(grp == gi)(lambda gi=gi, dil=dil: run(gi, dil))

        @pl.when(grp == N_GROUPS - 1)
        def _():
            m_all = jnp.maximum(jnp.maximum(m_s[0], m_s[1]), m_s[2])
            den = jnp.zeros((SLAB, e), F32)
            num = jnp.zeros((SLAB, e), F32)
            for gi in range(N_GROUPS):
                w = jnp.exp(m_s[gi] - m_all)
                den += l_s[gi] * w
                num += acc_s[gi] * w
            o_ref[...] = num / den
            lse_ref[...] = m_all + jnp.log(den)

    small = pl.BlockSpec((1, e), lambda h, n, g: (0, 0))
    out = pl.BlockSpec((SLAB, e), lambda h, n, g: (n, h))
    return pl.pallas_call(
        body, name=name, grid=(HEADS, n_slabs, N_GROUPS),
        in_specs=_qkv_specs(lambda n: n) + [small, small],
        out_specs=[out, out],
        out_shape=[jax.ShapeDtypeStruct((s_len, HEADS * e), F32)] * 2,
        scratch_shapes=[pltpu.VMEM((SLAB, e), F32), pltpu.VMEM((2 * SLAB, e), F32), pltpu.VMEM((2 * SLAB, e), F32),
                        pltpu.VMEM((N_GROUPS, SLAB, e), F32), pltpu.VMEM((N_GROUPS, SLAB, e), F32),
                        pltpu.VMEM((N_GROUPS, SLAB, e), F32)],
        compiler_params=_params(3),
    )(qkv, qkv, qkv, qkv, qkv, q_norm, k_norm)


def _attn_bwd(qkv, d_out, out, lse, q_norm, k_norm, name):
    s_len = qkv.shape[0]
    e = HEAD_DIM
    n_slabs = s_len // SLAB

    def body(q_ref, k_ref, v_ref, kp_ref, vp_ref, do_ref, o_ref, lse_ref, qn_ref, kn_ref, dq_ref, dk_ref, dv_ref,
             st_ref, qs, kb, vb, dqs, dkb, dvb, carry):
        head, step, grp = pl.program_id(0), pl.program_id(1), pl.program_id(2)
        n = n_slabs - 1 - step
        _load_qkv(q_ref, k_ref, v_ref, kp_ref, vp_ref, qn_ref, kn_ref, qs, kb, vb, n)
        dkb[...] = jnp.zeros_like(dkb)
        dvb[...] = jnp.zeros_like(dvb)

        @pl.when((head == 0) & (step == 0) & (grp == 0))
        def _():
            st_ref[...] = jnp.zeros_like(st_ref)

        def run(gi, dil):
            @pl.when(step == 0)
            def _():
                carry[gi] = jnp.zeros((2, SLAB, e), F32)

            def tile(q_rows, kv_rows, bias):
                q = qs[q_rows, :].astype(BF16)
                k = kb[kv_rows, :].astype(BF16)
                v = vb[kv_rows, :].astype(BF16)
                do = do_ref[q_rows, :]
                delta = jnp.sum(do * o_ref[q_rows, :], axis=-1, keepdims=True)
                s = _dot_nt(q, k) * ATTN_SCALE + bias
                p = jnp.exp(s - lse_ref[q_rows, :][:, 0:1])
                do16 = do.astype(BF16)
                ds = (p * (_dot_nt(do16, v) - delta) * ATTN_SCALE).astype(BF16)
                dqs[q_rows, :] = _dot(ds, k)
                dkb[kv_rows, :] += _dot_tn(ds, q)
                dvb[kv_rows, :] += _dot_tn(p.astype(BF16), do16)

            _for_each_tile(dil, n, tile)
            dk_hat = dkb[SLAB:2 * SLAB, :] + carry[gi, 0]
            dv = dvb[SLAB:2 * SLAB, :] + carry[gi, 1]
            carry[gi, 0] = dkb[0:SLAB, :]
            carry[gi, 1] = dvb[0:SLAB, :]

            def norm_bwd(raw, gain, d_hat):
                r = _rms(raw)
                y = raw * r
                dy = d_hat * gain
                return r * (dy - y * jnp.mean(dy * y, axis=-1, keepdims=True)), jnp.sum(d_hat * y, axis=0, keepdims=True)

            dq, dqn = norm_bwd(q_ref[...], qn_ref[...], dqs[...])
            dk, dkn = norm_bwd(k_ref[...], kn_ref[...], dk_hat)
            dq_ref[...] = dq.astype(BF16)
            dk_ref[...] = dk.astype(BF16)
            dv_ref[...] = dv.astype(BF16)
            st_ref[0:1, :] += dqn
            st_ref[1:2, :] += dkn

        for gi, dil in enumerate(DILATIONS):
            pl.when(grp == gi)(lambda gi=gi, dil=dil: run(gi, dil))

    slab_of = lambda s: n_slabs - 1 - s
    small = pl.BlockSpec((1, e), lambda h, s, g: (0, 0))
    head_blk = pl.BlockSpec((SLAB, e), lambda h, s, g: (slab_of(s), h))
    grad_blk = pl.BlockSpec((SLAB, e), lambda h, s, g: (slab_of(s), g * HEADS + h))
    grad_shape = jax.ShapeDtypeStruct((s_len, QKV), BF16)
    return pl.pallas_call(
        body, name=name, grid=(HEADS, n_slabs, N_GROUPS),
        in_specs=_qkv_specs(slab_of) + [head_blk, head_blk, head_blk, small, small],
        out_specs=[grad_blk, grad_blk, grad_blk, pl.BlockSpec((8, e), lambda h, s, g: (0, 0))],
        out_shape=[grad_shape, grad_shape, grad_shape, jax.ShapeDtypeStruct((8, e), F32)],
        scratch_shapes=[pltpu.VMEM((SLAB, e), F32), pltpu.VMEM((2 * SLAB, e), F32), pltpu.VMEM((2 * SLAB, e), F32),
                        pltpu.VMEM((SLAB, e), F32), pltpu.VMEM((2 * SLAB, e), F32), pltpu.VMEM((2 * SLAB, e), F32),
                        pltpu.VMEM((N_GROUPS, 2, SLAB, e), F32)],
        compiler_params=_params(3),
    )(qkv, qkv, qkv, qkv, qkv, d_out, out, lse, q_norm, k_norm)


def _shift_rows(x, by, edge, forward):
    t_len = x.shape[0]
    row = lax.broadcasted_iota(jnp.int32, x.shape, 0)
    if forward:
        out = pltpu.roll(x, by, 0)
        for i in range(by):
            out = jnp.where(row == i, edge[8 - by + i:8 - by + i + 1, :], out)
    else:
        out = pltpu.roll(x, t_len - by, 0)
        for i in range(by):
            out = jnp.where(row == t_len - by + i, edge[i:i + 1, :], out)
    return out


def _mix_fwd(x, o, rest, mod, conv_w, w_attn, w_conv, w_out, name):
    s_len, d = x.shape
    tm = MIX_TILE
    a_w = o.shape[1]

    def body(x_ref, o_ref, u_ref, b_ref, c_ref, ga_ref, gc_ref, mod_ref, cw_ref, wa_ref, wc_ref, wo_ref,
             xo_ref, z_ref, ya_ref, yc_ref, conv_ref, yb_ref, m_ref, o16_ref, carry):
        @pl.when(pl.program_id(0) == 0)
        def _():
            carry[...] = jnp.zeros_like(carry)

        xc = c_ref[...] * u_ref[...]
        edge = carry[...]
        conv = (_shift_rows(xc, 2, edge, True) * cw_ref[0:1, :] + _shift_rows(xc, 1, edge, True) * cw_ref[1:2, :]
                + xc * cw_ref[2:3, :])
        carry[...] = xc[tm - 8:tm, :]
        yb = (b_ref[...] * conv).astype(BF16)
        o16 = o_ref[...].astype(BF16)
        ya = _dot(o16, wa_ref[...])
        yc = _dot(yb, wc_ref[...])
        merged = (_sigmoid(ga_ref[...]) * ya + _sigmoid(gc_ref[...]) * yc).astype(BF16)
        z = _dot(merged, wo_ref[...])
        xo_ref[...] = x_ref[...] + mod_ref[2:3, :] * z
        z_ref[...] = z
        ya_ref[...] = ya.astype(BF16)
        yc_ref[...] = yc.astype(BF16)
        conv_ref[...] = conv.astype(BF16)
        yb_ref[...] = yb
        m_ref[...] = merged
        o16_ref[...] = o16

    tile = pl.BlockSpec((tm, d), lambda i: (i, 0))
    sect = lambda k: pl.BlockSpec((tm, d), lambda i: (i, k))
    att = pl.BlockSpec((tm, a_w), lambda i: (i, 0))
    const = lambda shape: pl.BlockSpec(shape, lambda i: (0, 0))
    f32_out = jax.ShapeDtypeStruct((s_len, d), F32)
    b16_out = jax.ShapeDtypeStruct((s_len, d), BF16)
    return pl.pallas_call(
        body, name=name, grid=(s_len // tm,),
        in_specs=[tile, att, sect(0), sect(1), sect(2), sect(3), sect(4), const((8, d)), const((8, d)),
                  const((a_w, d)), const((d, d)), const((d, d))],
        out_specs=[tile] * 7 + [att],
        out_shape=[f32_out, f32_out] + [b16_out] * 5 + [jax.ShapeDtypeStruct((s_len, a_w), BF16)],
        scratch_shapes=[pltpu.VMEM((8, d), F32)],
        compiler_params=_params(1),
    )(x, o, rest, rest, rest, rest, rest, mod, conv_w, w_attn, w_conv, w_out)


def _mix_bwd(dxo, ya, yc, conv, rest, mod, conv_w, w_attn, w_conv, w_out, a_w, name):
    s_len, d = dxo.shape
    tm = MIX_TILE
    n_tiles = s_len // tm

    def body(dxo_ref, ya_ref, yc_ref, conv_ref, u_ref, b_ref, c_ref, ga_ref, gc_ref, mod_ref, cw_ref,
             wa_ref, wc_ref, wo_ref, do_ref, drest_ref, dz_ref, dya_ref, dyc_ref, st_ref, carry):
        @pl.when(pl.program_id(0) == 0)
        def _():
            carry[...] = jnp.zeros_like(carry)
            st_ref[...] = jnp.zeros_like(st_ref)

        dz = (mod_ref[2:3, :] * dxo_ref[...]).astype(BF16)
        dz_ref[...] = dz
        dm = _dot_nt(dz, wo_ref[...])
        sa, sc = _sigmoid(ga_ref[...]), _sigmoid(gc_ref[...])
        dya = (dm * sa).astype(BF16)
        dyc = (dm * sc).astype(BF16)
        dya_ref[...] = dya
        dyc_ref[...] = dyc
        drest_ref[:, 3 * d:4 * d] = (dm * ya_ref[...].astype(F32) * (sa * (1.0 - sa))).astype(BF16)
        drest_ref[:, 4 * d:5 * d] = (dm * yc_ref[...].astype(F32) * (sc * (1.0 - sc))).astype(BF16)
        do_ref[...] = _dot_nt(dya, wa_ref[...])
        dyb = _dot_nt(dyc, wc_ref[...])
        drest_ref[:, d:2 * d] = (dyb * conv_ref[...].astype(F32)).astype(BF16)
        dconv = dyb * b_ref[...]
        edge = carry[...]
        sh1 = _shift_rows(dconv, 1, edge, False)
        sh2 = _shift_rows(dconv, 2, edge, False)
        carry[...] = dconv[0:8, :]
        dxc = dconv * cw_ref[2:3, :] + sh1 * cw_ref[1:2, :] + sh2 * cw_ref[0:1, :]
        u, c = u_ref[...], c_ref[...]
        xc = c * u
        drest_ref[:, 0:d] = (dxc * c).astype(BF16)
        drest_ref[:, 2 * d:3 * d] = (dxc * u).astype(BF16)
        st_ref[0:1, :] += jnp.sum(xc * sh2, axis=0, keepdims=True)
        st_ref[1:2, :] += jnp.sum(xc * sh1, axis=0, keepdims=True)
        st_ref[2:3, :] += jnp.sum(xc * dconv, axis=0, keepdims=True)

    rev = lambda i: n_tiles - 1 - i
    tile = pl.BlockSpec((tm, d), lambda i: (rev(i), 0))
    sect = lambda k: pl.BlockSpec((tm, d), lambda i: (rev(i), k))
    const = lambda shape: pl.BlockSpec(shape, lambda i: (0, 0))
    b16_out = jax.ShapeDtypeStruct((s_len, d), BF16)
    return pl.pallas_call(
        body, name=name, grid=(n_tiles,),
        in_specs=[tile, tile, tile, tile, sect(0), sect(1), sect(2), sect(3), sect(4), const((8, d)), const((8, d)),
                  const((a_w, d)), const((d, d)), const((d, d))],
        out_specs=[pl.BlockSpec((tm, a_w), lambda i: (rev(i), 0)), pl.BlockSpec((tm, 5 * d), lambda i: (rev(i), 0)),
                   tile, tile, tile, const((8, d))],
        out_shape=[jax.ShapeDtypeStruct((s_len, a_w), F32), jax.ShapeDtypeStruct((s_len, 5 * d), BF16),
                   b16_out, b16_out, b16_out, jax.ShapeDtypeStruct((8, d), F32)],
        scratch_shapes=[pltpu.VMEM((8, d), F32)],
        compiler_params=_params(1),
    )(dxo, ya, yc, conv, rest, rest, rest, rest, rest, mod, conv_w, w_attn, w_conv, w_out)


ADA_COLS = 128


def _ada_fwd(c_all, w_shard, b_shard, name):
    d, cols = w_shard.shape

    def body(c_ref, w_ref, b_ref, o_ref):
        cv = c_ref[...]
        o_ref[...] = jnp.dot(cv * _sigmoid(cv), w_ref[...], preferred_element_type=F32,
                             precision=lax.Precision.HIGHEST) + b_ref[...]

    return pl.pallas_call(
        body, name=name, grid=(cols // ADA_COLS,),
        in_specs=[pl.BlockSpec((8, d), lambda j: (0, 0)), pl.BlockSpec((d, ADA_COLS), lambda j: (0, j)),
                  pl.BlockSpec((1, ADA_COLS), lambda j: (0, j))],
        out_specs=pl.BlockSpec((8, ADA_COLS), lambda j: (0, j)),
        out_shape=jax.ShapeDtypeStruct((8, cols), F32),
        compiler_params=_params(1),
    )(c_all, w_shard, b_shard)


def _ada_bwd(c_all, dmod_shard, w, m, v, name):
    d, cols = w.shape

    def body(c_ref, dm_ref, w_ref, m_ref, v_ref, g_ref, d_ref, nm_ref, nv_ref):
        cv = c_ref[...]
        g = lax.dot_general(cv * _sigmoid(cv), dm_ref[...], (((0,), (0,)), ((), ())),
                            preferred_element_type=F32, precision=lax.Precision.HIGHEST)
        g_ref[...] = g
        d_ref[...], nm_ref[...], nv_ref[...] = _adamw_math(w_ref[...], g, m_ref[...], v_ref[...])

    blk = pl.BlockSpec((d, ADA_COLS), lambda j: (0, j))
    shape = jax.ShapeDtypeStruct((d, cols), F32)
    return pl.pallas_call(
        body, name=name, grid=(cols // ADA_COLS,),
        in_specs=[pl.BlockSpec((8, d), lambda j: (0, 0)), pl.BlockSpec((8, ADA_COLS), lambda j: (0, j)), blk, blk, blk],
        out_specs=[blk] * 4, out_shape=[shape] * 4,
        compiler_params=_params(1),
    )(c_all, dmod_shard, w, m, v)


def _small_update(parts, w, m, v, name):
    n = w.shape[1]

    def body(p_ref, w_ref, m_ref, v_ref, g_ref, d_ref, nm_ref, nv_ref):
        g = p_ref[0:1, :]
        for i in range(1, 8):
            g = g + p_ref[i:i + 1, :]
        g_ref[...] = g
        d_ref[...], nm_ref[...], nv_ref[...] = _adamw_math(w_ref[...], g, m_ref[...], v_ref[...])

    shape = jax.ShapeDtypeStruct((1, n), F32)
    return pl.pallas_call(body, name=name, out_shape=[shape] * 4, compiler_params=_params())(parts, w, m, v)


def _cols_to_shards(w, n):
    r, nc = w.shape
    return w.reshape(r, n, nc // n).transpose(1, 0, 2)


def kernel(x, c, w_ada, b_ada, norm_ffn1, ffn1_w_gate, ffn1_w_up, ffn1_w_down, norm_mix, w_in, q_norm, k_norm, conv_w, w_attn_branch, w_conv_branch, w_out, norm_ffn2, ffn2_w_gate, ffn2_w_up, ffn2_w_down, loss_target, m_w_ada, m_b_ada, m_norm_ffn1, m_ffn1_w_gate, m_ffn1_w_up, m_ffn1_w_down, m_norm_mix, m_w_in, m_q_norm, m_k_norm, m_conv_w, m_w_attn_branch, m_w_conv_branch, m_w_out, m_norm_ffn2, m_ffn2_w_gate, m_ffn2_w_up, m_ffn2_w_down, v_w_ada, v_b_ada, v_norm_ffn1, v_ffn1_w_gate, v_ffn1_w_up, v_ffn1_w_down, v_norm_mix, v_w_in, v_q_norm, v_k_norm, v_conv_w, v_w_attn_branch, v_w_conv_branch, v_w_out, v_norm_ffn2, v_ffn2_w_gate, v_ffn2_w_up, v_ffn2_w_down):
    ix, iy, ic = _place()
    chip = 2 * ix + iy
    me = 4 * ix + 2 * iy + ic
    xs = x[0]
    target = loss_target[0]
    s_len, d = xs.shape
    ada_cols = w_ada.shape[2]
    conv_cols = conv_w.shape[2]

    conv_rows = jnp.zeros((8, conv_cols), F32).at[0:3].set(conv_w[0])
    small_in = jnp.concatenate([jnp.broadcast_to(c, (8, d)), conv_rows], axis=1)
    small_all = _allgather8(small_in, "gather_c").reshape(8, 8, d + conv_cols)
    c_all = small_all[:, 0, :d]
    conv_full = small_all[0::2, 0:3, d:].transpose(1, 0, 2).reshape(3, N_CHIPS * conv_cols)
    conv_pad = jnp.zeros((8, N_CHIPS * conv_cols), F32).at[0:3].set(conv_full)
    b_shard = lax.dynamic_slice(b_ada, (0, chip * ada_cols), (1, ada_cols))
    mod_part = _ada_fwd(c_all, w_ada[0], b_shard, "ada_fwd")
    mod_all = _allgather8(mod_part, "gather_mod").reshape(N_CHIPS, 2, 8, ada_cols)[:, 0]
    mod_mine = lax.dynamic_slice(mod_all, (0, me, 0), (N_CHIPS, 1, ada_cols)).reshape(9, d)

    def mod_rows(i, gain):
        return jnp.zeros((8, d), F32).at[0:3].set(mod_mine[3 * i:3 * i + 3]).at[3:4].set(gain)

    mod1, mod2, mod3 = mod_rows(0, norm_ffn1), mod_rows(1, norm_mix), mod_rows(2, norm_ffn2)

    to16 = lambda w: w[0].astype(BF16)
    wg1, wu1, wd1 = _gather_weights([to16(ffn1_w_gate), to16(ffn1_w_up), to16(ffn1_w_down)], [False] * 3,
                                    "gather_ffn1", 1)
    h1, h1t = _norm_mod(xs, mod1, "norm1")
    (w_in_full,) = _gather_weights([to16(w_in)], [True], "gather_w_in", 2, after=(wd1, h1))

    g1, u1, y1 = _ffn_fwd(h1, wg1, wu1, wd1, "ffn1_fwd")
    x1, h2, h2t = _norm_mod(xs, mod2, "norm2", prev=(y1, mod1, 0.5))
    qkv, rest = _in_proj(h2, w_in_full, "in_proj")
    w_ab, w_cb_g, w_o_g, wg2, wu2, wd2 = _gather_weights(
        [to16(w_attn_branch), to16(w_conv_branch), to16(w_out),
         to16(ffn2_w_gate), to16(ffn2_w_up), to16(ffn2_w_down)], [True] + [False] * 5,
        "gather_rest", 3, after=(qkv,))
    a_w = w_ab.shape[0]
    w_cb = w_cb_g.reshape(d, d)
    w_o = w_o_g.reshape(d, d)
    o, lse = _attn_fwd(qkv, q_norm, k_norm, "attn_fwd")
    x2, z, ya, yc, conv, yb, merged, o16 = _mix_fwd(x1, o, rest, mod2, conv_pad, w_ab, w_cb, w_o, "mix_fwd")
    h3, h3t = _norm_mod(x2, mod3, "norm3")
    g3, u3, y3 = _ffn_fwd(h3, wg2, wu2, wd2, "ffn2_fwd")
    dx3, dy3, loss_part = _loss_grad(x2, y3, mod3, target, "loss")
    loss = lax.psum(0.5 * jnp.sum(loss_part) / d, ("x", "y", "c"))

    c_idx = jnp.reshape(ic, (1,)).astype(jnp.int32)
    chip_idx = jnp.stack([chip, ic]).astype(jnp.int32)

    def reduce_start(grads, names, tag, collective_id):
        from_sibling = _rs_pair_exchange(grads, "rs_pair_" + tag)
        pair_sums = [_pair_add(g, r, c_idx, "pair_add_" + nm) for g, r, nm in zip(grads, from_sibling, names)]
        return pair_sums, _rs_chip_exchange(pair_sums, "rs_chips_" + tag, collective_id)

    def reduce_finish(pair_sums, from_chips, names, tag, after):
        totals = [_chip_add(p, r, chip_idx, "chip_add_" + nm, after)
                  for p, r, nm in zip(pair_sums, from_chips, names)]
        return dict(zip(names, _rs_share(totals, "rs_share_" + tag)))

    names_a = ["ffn2_w_gate", "ffn2_w_up", "ffn2_w_down"]
    names_b = ["w_in", "w_attn_branch", "w_conv_branch", "w_out"]
    names_c = ["ffn1_w_gate", "ffn1_w_up", "ffn1_w_down"]

    dh3, dg3, du3, a3 = _ffn_bwd(dy3, g3, u3, wg2, wu2, wd2, "ffn2_bwd")
    sums_a, chips_a = reduce_start(list(_ffn_wgrads(h3t, dg3, du3, a3, dy3, "ffn2")), names_a, "a", 4)
    dx2, st3 = _norm_bwd(dh3, x2, mod3, dx3, y3, 0.5, "norm3_bwd", after=tuple(sums_a))

    do, drest, dz, dya, dyc, st_conv = _mix_bwd(dx2, ya, yc, conv, rest, mod2, conv_pad, w_ab, w_cb, w_o, a_w, "mix_bwd")
    dq, dk, dv, st_qk = _attn_bwd(qkv, do, o, lse, q_norm, k_norm, "attn_bwd")
    tok = lambda width: (lambda ts: pl.BlockSpec((ts, width), lambda cc, s: (s, 0)))
    colblk = lambda width: (lambda ts: pl.BlockSpec((ts, width), lambda cc, s: (s, cc)))
    tok_t = lambda ts: pl.BlockSpec((d, ts), lambda cc, s: (0, s))
    whole = pl.BlockSpec((d, QKV), lambda cc, s: (0, 0))
    dw_in = [_wgrad(h2t, part, tok_t, tok(QKV), (d, QKV), whole, (d, QKV), 1, "dw_in_" + nm, True)
             for part, nm in ((dq, "q"), (dk, "k"), (dv, "v"))]
    dw_in.append(_wgrad(h2t, drest, tok_t, colblk(d), (d, 5 * d), pl.BlockSpec((d, d), lambda cc, s: (0, cc)),
                        (d, d), 5, "dw_in_rest", True))
    dw_in = _cols_to_shards(jnp.concatenate(dw_in, axis=1), N_CHIPS)
    shard_w = d // N_CHIPS
    dw_ab = _wgrad(o16, dya, tok(a_w), colblk(shard_w), (a_w, d), pl.BlockSpec((a_w, shard_w), lambda cc, s: (0, cc)),
                   (a_w, shard_w), N_CHIPS, "dw_attn_branch")
    dw_ab = _cols_to_shards(dw_ab, N_CHIPS)
    row_out = pl.BlockSpec((None, shard_w, d), lambda cc, s: (cc, 0, 0))
    dw_cb = _wgrad(yb, dyc, colblk(shard_w), tok(d), (N_CHIPS, shard_w, d), row_out, (shard_w, d), N_CHIPS, "dw_conv_branch")
    dw_o = _wgrad(merged, dz, colblk(shard_w), tok(d), (N_CHIPS, shard_w, d), row_out, (shard_w, d), N_CHIPS, "dw_out")
    shard_grads = reduce_finish(sums_a, chips_a, names_a, "a", after=(dw_in, dw_o))
    sums_b, chips_b = reduce_start([dw_in, dw_ab, dw_cb, dw_o], names_b, "b", 5)

    dh2 = _in_proj_bwd(dq, dk, dv, drest, w_in_full, "in_proj_bwd", after=tuple(sums_b))
    dx1, st2, dy1 = _norm_bwd(dh2, x1, mod2, dx2, z, 1.0, "norm2_bwd", prev=(mod1, 0.5))
    dh1, dg1, du1, a1 = _ffn_bwd(dy1, g1, u1, wg1, wu1, wd1, "ffn1_bwd")
    dx0, st1 = _norm_bwd(dh1, xs, mod1, dx1, y1, 0.5, "norm1_bwd")
    grads_c = list(_ffn_wgrads(h1t, dg1, du1, a1, dy1, "ffn1"))
    shard_grads.update(reduce_finish(sums_b, chips_b, names_b, "b", after=tuple(grads_c)))
    sums_c, chips_c = reduce_start(grads_c, names_c, "c", 6)

    dmod = jnp.concatenate([st1[0:3], st2[0:3], st3[0:3]], axis=0).reshape(1, 9 * d)
    small = jnp.concatenate([dmod, st1[3:4], st2[3:4], st3[3:4], st_qk[0:1], st_qk[1:2],
                             st_conv[0:3].reshape(1, 3 * d)], axis=1)
    small_all = _allgather8(jnp.broadcast_to(small, (8, small.shape[1])), "gather_small").reshape(8, 8, -1)[:, 0]
    dmod_all = small_all[:, :9 * d]
    dmod_shard = lax.dynamic_slice(dmod_all, (0, chip * ada_cols), (8, ada_cols))
    g_w_ada, d_w_ada, nm_w_ada, nv_w_ada = _ada_bwd(c_all, dmod_shard, w_ada[0], m_w_ada[0], v_w_ada[0], "ada_bwd")

    vec_names = ["b_ada", "norm_ffn1", "norm_mix", "norm_ffn2", "q_norm", "k_norm"]
    vec_w = [b_ada, norm_ffn1, norm_mix, norm_ffn2, q_norm, k_norm]
    vec_m = [m_b_ada, m_norm_ffn1, m_norm_mix, m_norm_ffn2, m_q_norm, m_k_norm]
    vec_v = [v_b_ada, v_norm_ffn1, v_norm_mix, v_norm_ffn2, v_q_norm, v_k_norm]
    n_vec = sum(w.shape[1] for w in vec_w)
    cat = lambda arrs: jnp.concatenate(arrs, axis=1)
    vec_out = _small_update(small_all[:, :n_vec], cat(vec_w), cat(vec_m), cat(vec_v), "small_update")
    conv_parts = small_all[:, n_vec:].reshape(8, 3, N_CHIPS * conv_cols)
    conv_parts = lax.dynamic_slice(conv_parts, (0, 0, chip * conv_cols), (8, 3, conv_cols)).reshape(8, 3 * conv_cols)
    flat3 = lambda w: w[0].reshape(1, 3 * conv_cols)
    conv_out = _small_update(conv_parts, flat3(conv_w), flat3(m_conv_w), flat3(v_conv_w), "conv_update")

    res = {"w_ada": [t[None] for t in (g_w_ada, d_w_ada, nm_w_ada, nv_w_ada)],
           "conv_w": [t.reshape(1, 3, conv_cols) for t in conv_out]}
    off = 0
    for nm, w in zip(vec_names, vec_w):
        width = w.shape[1]
        res[nm] = [t[:, off:off + width] for t in vec_out]
        off += width
    big = {"ffn1_w_gate": (ffn1_w_gate, m_ffn1_w_gate, v_ffn1_w_gate), "ffn1_w_up": (ffn1_w_up, m_ffn1_w_up, v_ffn1_w_up),
           "ffn1_w_down": (ffn1_w_down, m_ffn1_w_down, v_ffn1_w_down), "w_in": (w_in, m_w_in, v_w_in),
           "w_attn_branch": (w_attn_branch, m_w_attn_branch, v_w_attn_branch),
           "w_conv_branch": (w_conv_branch, m_w_conv_branch, v_w_conv_branch), "w_out": (w_out, m_w_out, v_w_out),
           "ffn2_w_gate": (ffn2_w_gate, m_ffn2_w_gate, v_ffn2_w_gate), "ffn2_w_up": (ffn2_w_up, m_ffn2_w_up, v_ffn2_w_up),
           "ffn2_w_down": (ffn2_w_down, m_ffn2_w_down, v_ffn2_w_down)}
    def update(nm, after=()):
        w, m, v = big[nm]
        g = shard_grads[nm]
        delta, new_m, new_v = _adamw(w[0], g, m[0], v[0], "adamw_" + nm, after)
        res[nm] = [t[None] for t in (g, delta, new_m, new_v)]
        return new_v

    last = tuple(sums_c)
    for nm in names_a + names_b:
        last = (update(nm, last),)
    shard_grads.update(reduce_finish(sums_c, chips_c, names_c, "c", after=last))
    for nm in names_c:
        update(nm)

    order = ["w_ada", "b_ada", "norm_ffn1", "ffn1_w_gate", "ffn1_w_up", "ffn1_w_down", "norm_mix", "w_in", "q_norm",
             "k_norm", "conv_w", "w_attn_branch", "w_conv_branch", "w_out", "norm_ffn2", "ffn2_w_gate", "ffn2_w_up",
             "ffn2_w_down"]
    return (loss, dx0[None], *[res[nm][0] for nm in order], *[res[nm][1] for nm in order],
            *[res[nm][2] for nm in order], *[res[nm][3] for nm in order])
```

```python
import jax
import jax.numpy as jnp
from jax import lax
from jax.experimental import pallas as pl
from jax.experimental.pallas import tpu as pltpu
from jax.experimental.pallas import tpu_sc as plsc

F32 = jnp.float32
BF16 = jnp.bfloat16
MESH = pl.DeviceIdType.MESH
ANY = pl.BlockSpec(memory_space=pl.ANY)

NORM_EPS = 1e-6
HEAD_DIM = 128
N_GROUPS = 3
HEADS = 4
DILATIONS = (1, 4, 16)
ATTN_BLOCK = 128
SLAB = ATTN_BLOCK * max(DILATIONS)
QKV = N_GROUPS * HEADS * HEAD_DIM
ATTN_SCALE = HEAD_DIM ** -0.5
NEG = -1e30
N_CHIPS = 4

ADAM_LR = 0.001
ADAM_B1 = 0.9
ADAM_B2 = 0.999
ADAM_EPS = 1e-08
ADAM_WD = 0.01
ADAM_STEP = 10

VMEM_LIMIT_BYTES = 56 * 1024 * 1024
TOKEN_TILE = 512
FFN_TILE = 1024
PROJ_TILE = 2048
WGRAD_TILE = 2048
IN_BLOCK = 512
MIX_TILE = 256


def _params(n_axes=0):
    return pltpu.CompilerParams(
        dimension_semantics=("arbitrary",) * n_axes if n_axes else None,
        vmem_limit_bytes=VMEM_LIMIT_BYTES)


def _dot(a, b):
    return jnp.dot(a, b, preferred_element_type=F32)


def _dot_nt(a, b):
    return lax.dot_general(a, b, (((1,), (1,)), ((), ())), preferred_element_type=F32)


def _dot_tn(a, b):
    return lax.dot_general(a, b, (((0,), (0,)), ((), ())), preferred_element_type=F32)


def _sigmoid(x):
    return 1.0 / (1.0 + jnp.exp(-x))


def _place():
    return lax.axis_index("x"), lax.axis_index("y"), lax.axis_index("c")


def _ordered(body, n_in, after):
    if not after:
        return body
    return lambda *refs: body(*refs[:n_in], *refs[n_in + len(after):])


def _allgather8(block, name):
    m_per, n = block.shape

    def body(x_ref, out_ref, send_sems, recv_sems, local_sem):
        x, y, c = _place()
        me, sibling = (x, y, c), (x, y, 1 - c)
        chips = [(1 - x, y), (x, 1 - y), (1 - x, 1 - y)]

        def rows(px, py, pc):
            return out_ref.at[pl.ds((4 * px + 2 * py + pc) * m_per, m_per), :]

        def copy(k, blk, to, src=None):
            return pltpu.make_async_remote_copy(
                src_ref=rows(*blk) if src is None else src, dst_ref=rows(*blk),
                send_sem=send_sems.at[k], recv_sem=recv_sems.at[k],
                device_id=to, device_id_type=MESH)

        mine = pltpu.make_async_copy(x_ref, rows(*me), local_sem)
        mine.start()
        first = [copy(0, me, sibling, src=x_ref)]
        first += [copy(1 + j, me, (*chip, c), src=x_ref) for j, chip in enumerate(chips)]
        for cp in first:
            cp.start()
        passed = [copy(4 + j, (*chip, c), sibling) for j, chip in enumerate(chips)]
        for j, chip in enumerate(chips):
            copy(1 + j, (*chip, c), me).wait_recv()
            passed[j].start()
        copy(0, sibling, me).wait_recv()
        for j, chip in enumerate(chips):
            copy(4 + j, (*chip, 1 - c), me).wait_recv()
        for cp in first + passed:
            cp.wait_send()
        mine.wait()

    return pl.pallas_call(
        body, name=name,
        out_shape=jax.ShapeDtypeStruct((8 * m_per, n), block.dtype),
        in_specs=[pl.BlockSpec(memory_space=pltpu.VMEM)],
        out_specs=pl.BlockSpec(memory_space=pltpu.VMEM),
        scratch_shapes=[pltpu.SemaphoreType.DMA((7,)), pltpu.SemaphoreType.DMA((7,)),
                        pltpu.SemaphoreType.DMA],
        compiler_params=_params(),
    )(block)


def _handshake(peers):
    barrier = pltpu.get_barrier_semaphore()
    for peer in peers:
        pl.semaphore_signal(barrier, inc=1, device_id=peer, device_id_type=MESH)
    pl.semaphore_wait(barrier, len(peers))


def _gather_weights(shards, by_cols, name, collective_id, after=()):
    n_arr = len(shards)

    def body(*refs):
        srcs, outs = refs[:n_arr], refs[n_arr + len(after):2 * n_arr + len(after)]
        send_sems, recv_sems, local_sems = refs[2 * n_arr + len(after):]
        x, y, c = _place()
        me_dev, sibling = (x, y, c), (x, y, 1 - c)
        chips = [(1 - x, y), (x, 1 - y), (1 - x, 1 - y)]
        me = 2 * x + y
        _handshake([sibling] + [(*chip, c) for chip in chips])

        def place(k, chip_idx, rows):
            if by_cols[k]:
                width = srcs[k].shape[1]
                return outs[k].at[rows, pl.ds(pl.multiple_of(chip_idx * width, 128), width)]
            return outs[k].at[chip_idx, rows]

        def copy(k, slot, chip_idx, half_sel, to, from_shard=False):
            half = srcs[k].shape[0] // 2
            rows = pl.ds(half_sel * half, half)
            dst = place(k, chip_idx, rows)
            return pltpu.make_async_remote_copy(
                src_ref=srcs[k].at[rows] if from_shard else dst, dst_ref=dst,
                send_sem=send_sems.at[6 * k + slot], recv_sem=recv_sems.at[6 * k + slot],
                device_id=to, device_id_type=MESH)

        own = [pltpu.make_async_copy(srcs[k], place(k, me, pl.ds(0, srcs[k].shape[0])), local_sems.at[k])
               for k in range(n_arr)]
        for cp in own:
            cp.start()
        sent = []
        for k in range(n_arr):
            for j, chip in enumerate(chips):
                sent.append(copy(k, j, me, c, (*chip, c), from_shard=True))
                sent[-1].start()
        for k in range(n_arr):
            for j, chip in enumerate(chips):
                chip_idx = 2 * chip[0] + chip[1]
                copy(k, j, chip_idx, c, me_dev).wait_recv()
                sent.append(copy(k, 3 + j, chip_idx, c, sibling))
                sent[-1].start()
        for k in range(n_arr):
            for j, chip in enumerate(chips):
                copy(k, 3 + j, 2 * chip[0] + chip[1], 1 - c, me_dev).wait_recv()
        for cp in sent:
            cp.wait_send()
        for cp in own:
            cp.wait()

    def gathered(k):
        r, cols = shards[k].shape
        return (r, N_CHIPS * cols) if by_cols[k] else (N_CHIPS, r, cols)

    return pl.kernel(
        body, name=name,
        out_type=[jax.ShapeDtypeStruct(gathered(k), shards[k].dtype) for k in range(n_arr)],
        mesh=plsc.ScalarSubcoreMesh(axis_name="sequencer", num_cores=1),
        scratch_types=[pltpu.SemaphoreType.DMA((6 * n_arr,)), pltpu.SemaphoreType.DMA((6 * n_arr,)),
                       pltpu.SemaphoreType.DMA((n_arr,))],
        compiler_params=pltpu.CompilerParams(collective_id=collective_id),
    )(*shards, *after)


def _rs_pair_exchange(grads, name):
    n_arr = len(grads)

    def body(*refs):
        srcs, outs = refs[:n_arr], refs[n_arr:2 * n_arr]
        send_sems, recv_sems = refs[2 * n_arr:]
        x, y, c = _place()
        cps = []
        for k in range(n_arr):
            half = srcs[k].shape[1] // 2
            cps.append(pltpu.make_async_remote_copy(
                src_ref=srcs[k].at[:, pl.ds((1 - c) * half, half)], dst_ref=outs[k],
                send_sem=send_sems.at[k], recv_sem=recv_sems.at[k],
                device_id=(x, y, 1 - c), device_id_type=MESH))
            cps[-1].start()
        for cp in cps:
            cp.wait_recv()
        for cp in cps:
            cp.wait_send()

    return pl.pallas_call(
        body, name=name,
        out_shape=[jax.ShapeDtypeStruct((g.shape[0], g.shape[1] // 2, g.shape[2]), g.dtype) for g in grads],
        in_specs=[ANY] * n_arr, out_specs=[ANY] * n_arr,
        scratch_shapes=[pltpu.SemaphoreType.DMA((n_arr,)), pltpu.SemaphoreType.DMA((n_arr,))],
        compiler_params=_params(),
    )(*grads)


def _rs_chip_exchange(sums, name, collective_id):
    n_arr = len(sums)

    def body(*refs):
        srcs, outs = refs[:n_arr], refs[n_arr:2 * n_arr]
        send_sems, recv_sems = refs[2 * n_arr:]
        x, y, c = _place()
        chips = [(1 - x, y), (x, 1 - y), (1 - x, 1 - y)]
        _handshake([(*chip, c) for chip in chips])
        cps = []
        for k in range(n_arr):
            for j, chip in enumerate(chips):
                cps.append(pltpu.make_async_remote_copy(
                    src_ref=srcs[k].at[2 * chip[0] + chip[1]], dst_ref=outs[k].at[j],
                    send_sem=send_sems.at[3 * k + j], recv_sem=recv_sems.at[3 * k + j],
                    device_id=(*chip, c), device_id_type=MESH))
                cps[-1].start()
        for cp in cps:
            cp.wait_recv()
        for cp in cps:
            cp.wait_send()

    return pl.kernel(
        body, name=name,
        out_type=[jax.ShapeDtypeStruct((3,) + s.shape[1:], s.dtype) for s in sums],
        mesh=plsc.ScalarSubcoreMesh(axis_name="sequencer", num_cores=1),
        scratch_types=[pltpu.SemaphoreType.DMA((3 * n_arr,)), pltpu.SemaphoreType.DMA((3 * n_arr,))],
        compiler_params=pltpu.CompilerParams(collective_id=collective_id),
    )(*sums)


def _rs_share(totals, name):
    n_arr = len(totals)

    def body(*refs):
        outs = refs[n_arr:2 * n_arr]
        send_sems, recv_sems = refs[2 * n_arr:]
        x, y, c = _place()

        def half_rows(k, sel):
            return outs[k].at[sel]

        cps = []
        for k in range(n_arr):
            cps.append(pltpu.make_async_remote_copy(
                src_ref=half_rows(k, c), dst_ref=half_rows(k, c), send_sem=send_sems.at[k], recv_sem=recv_sems.at[k],
                device_id=(x, y, 1 - c), device_id_type=MESH))
            cps[-1].start()
        for k in range(n_arr):
            pltpu.make_async_remote_copy(
                src_ref=half_rows(k, c), dst_ref=half_rows(k, 1 - c), send_sem=send_sems.at[k],
                recv_sem=recv_sems.at[k], device_id=(x, y, 1 - c), device_id_type=MESH).wait_recv()
        for cp in cps:
            cp.wait_send()

    shared = pl.pallas_call(
        body, name=name,
        out_shape=[jax.ShapeDtypeStruct(t.shape, t.dtype) for t in totals],
        in_specs=[ANY] * n_arr, out_specs=[ANY] * n_arr,
        input_output_aliases={k: k for k in range(n_arr)},
        scratch_shapes=[pltpu.SemaphoreType.DMA((n_arr,)), pltpu.SemaphoreType.DMA((n_arr,))],
        compiler_params=_params(),
    )(*totals)
    return [t.reshape(2 * t.shape[1], t.shape[2]) for t in shared]


def _pair_add(grad, recv, c_idx, name):
    n, r, cols = grad.shape
    half = r // 2
    rows = half // 2

    def body(_, g_ref, r_ref, o_ref):
        o_ref[...] = (g_ref[...].astype(F32) + r_ref[...].astype(F32)).astype(o_ref.dtype)

    return pl.pallas_call(
        body, name=name,
        grid_spec=pltpu.PrefetchScalarGridSpec(
            num_scalar_prefetch=1, grid=(n, 2),
            in_specs=[pl.BlockSpec((None, None, rows, cols), lambda s, i, ci: (s, ci[0], i, 0)),
                      pl.BlockSpec((None, rows, cols), lambda s, i, ci: (s, i, 0))],
            out_specs=pl.BlockSpec((None, rows, cols), lambda s, i, ci: (s, i, 0))),
        out_shape=jax.ShapeDtypeStruct((n, half, cols), BF16),
        compiler_params=_params(2),
    )(c_idx, grad.reshape(n, 2, half, cols), recv)


def _chip_add(sums, recv, chip_and_core, name, after=()):
    _, half, cols = sums.shape
    rows = half // 2

    def body(_, s_ref, r0_ref, r1_ref, r2_ref, o_ref):
        o_ref[...] = ((s_ref[...].astype(F32) + r0_ref[...].astype(F32))
                      + r1_ref[...].astype(F32)) + r2_ref[...].astype(F32)

    def recv_spec(j):
        return pl.BlockSpec((None, rows, cols), lambda i, ci: (j, i, 0))

    return pl.pallas_call(
        _ordered(body, 5, after), name=name,
        grid_spec=pltpu.PrefetchScalarGridSpec(
            num_scalar_prefetch=1, grid=(2,),
            in_specs=[pl.BlockSpec((None, rows, cols), lambda i, ci: (ci[0], i, 0)),
                      recv_spec(0), recv_spec(1), recv_spec(2)] + [ANY] * len(after),
            out_specs=pl.BlockSpec((None, rows, cols), lambda i, ci: (ci[1], i, 0))),
        out_shape=jax.ShapeDtypeStruct((2, half, cols), F32),
        compiler_params=_params(1),
    )(chip_and_core, sums, recv, recv, recv, *after)


def _rms(x):
    return lax.rsqrt(jnp.mean(x * x, axis=-1, keepdims=True) + NORM_EPS)


def _norm_mod(x, mod, name, prev=None):
    s_len, d = x.shape
    tm = TOKEN_TILE

    def body(*refs):
        if prev is None:
            x_ref, mod_ref, h_ref, ht_ref = refs
            xv = x_ref[...]
        else:
            x_ref, y_ref, modp_ref, mod_ref, xo_ref, h_ref, ht_ref = refs
            xv = x_ref[...] + prev[2] * modp_ref[2:3, :] * y_ref[...]
            xo_ref[...] = xv
        n = (xv * _rms(xv)) * mod_ref[3:4, :]
        h = n * (1.0 + mod_ref[1:2, :]) + mod_ref[0:1, :]
        h_ref[...] = h.astype(BF16)
        ht_ref[...] = h.T.astype(BF16)

    tile = pl.BlockSpec((tm, d), lambda i: (i, 0))
    small = pl.BlockSpec((8, d), lambda i: (0, 0))
    h_specs = [tile, pl.BlockSpec((d, tm), lambda i: (0, i))]
    h_shapes = [jax.ShapeDtypeStruct((s_len, d), BF16), jax.ShapeDtypeStruct((d, s_len), BF16)]
    if prev is None:
        return pl.pallas_call(
            body, name=name, grid=(s_len // tm,), in_specs=[tile, small], out_specs=h_specs, out_shape=h_shapes,
            compiler_params=_params(1))(x, mod)
    return pl.pallas_call(
        body, name=name, grid=(s_len // tm,), in_specs=[tile, tile, small, small],
        out_specs=[tile] + h_specs, out_shape=[jax.ShapeDtypeStruct((s_len, d), F32)] + h_shapes,
        compiler_params=_params(1))(x, prev[0], prev[1], mod)


def _norm_bwd(dh, x, mod, dxo, y_raw, coef, name, after=(), prev=None):
    s_len, d = x.shape
    tm = TOKEN_TILE

    def body(*refs):
        if prev is None:
            dh_ref, x_ref, mod_ref, dxo_ref, y_ref, dx_ref, st_ref = refs
        else:
            dh_ref, x_ref, mod_ref, dxo_ref, y_ref, modp_ref, dx_ref, st_ref, dyp_ref = refs

        @pl.when(pl.program_id(0) == 0)
        def _():
            st_ref[...] = jnp.zeros_like(st_ref)

        xv, dhv, dxov = x_ref[...], dh_ref[...], dxo_ref[...]
        r = _rms(xv)
        xh = xv * r
        gain, scale = mod_ref[3:4, :], mod_ref[1:2, :]
        dn = dhv * (1.0 + scale)
        dxh = dn * gain
        dx = dxov + r * (dxh - xh * jnp.mean(dxh * xh, axis=-1, keepdims=True))
        dx_ref[...] = dx
        if prev is not None:
            dyp_ref[...] = (prev[1] * modp_ref[2:3, :] * dx).astype(BF16)
        st_ref[0:1, :] += jnp.sum(dhv, axis=0, keepdims=True)
        st_ref[1:2, :] += jnp.sum(dhv * (xh * gain), axis=0, keepdims=True)
        st_ref[2:3, :] += coef * jnp.sum(y_ref[...] * dxov, axis=0, keepdims=True)
        st_ref[3:4, :] += jnp.sum(dn * xh, axis=0, keepdims=True)

    tile = pl.BlockSpec((tm, d), lambda i: (i, 0))
    small = pl.BlockSpec((8, d), lambda i: (0, 0))
    operands = [dh, x, mod, dxo, y_raw] + ([] if prev is None else [prev[0]])
    in_specs = [tile, tile, small, tile, tile] + ([] if prev is None else [small])
    out_specs = [tile, small] + ([] if prev is None else [tile])
    out_shape = [jax.ShapeDtypeStruct((s_len, d), F32), jax.ShapeDtypeStruct((8, d), F32)]
    if prev is not None:
        out_shape.append(jax.ShapeDtypeStruct((s_len, d), BF16))
    return pl.pallas_call(
        _ordered(body, len(operands), after), name=name, grid=(s_len // tm,),
        in_specs=in_specs + [ANY] * len(after), out_specs=out_specs, out_shape=out_shape,
        compiler_params=_params(1),
    )(*operands, *after)


def _loss_grad(x, y, mod, target, name):
    s_len, d = x.shape
    tm = TOKEN_TILE

    def body(x_ref, y_ref, mod_ref, t_ref, do_ref, dy_ref, part_ref):
        @pl.when(pl.program_id(0) == 0)
        def _():
            part_ref[...] = jnp.zeros_like(part_ref)

        half_gate = 0.5 * mod_ref[2:3, :]
        err = (x_ref[...] + half_gate * y_ref[...]) - t_ref[...]
        do = err * (1.0 / d)
        do_ref[...] = do
        dy_ref[...] = (half_gate * do).astype(BF16)
        sq = err * err
        part_ref[...] += jnp.sum(sq.reshape(tm // 8, 8, d), axis=0)

    tile = pl.BlockSpec((tm, d), lambda i: (i, 0))
    small = pl.BlockSpec((8, d), lambda i: (0, 0))
    return pl.pallas_call(
        body, name=name, grid=(s_len // tm,),
        in_specs=[tile, tile, small, tile],
        out_specs=[tile, tile, small],
        out_shape=[jax.ShapeDtypeStruct((s_len, d), F32), jax.ShapeDtypeStruct((s_len, d), BF16),
                   jax.ShapeDtypeStruct((8, d), F32)],
        compiler_params=_params(1),
    )(x, y, mod, target)


def _adamw_math(w, g, m, v):
    m = ADAM_B1 * m + (1.0 - ADAM_B1) * g
    v = ADAM_B2 * v + (1.0 - ADAM_B2) * (g * g)
    m_hat = m / (1.0 - ADAM_B1 ** ADAM_STEP)
    v_hat = v / (1.0 - ADAM_B2 ** ADAM_STEP)
    delta = -ADAM_LR * (m_hat / (jnp.sqrt(v_hat) + ADAM_EPS) + ADAM_WD * w)
    return delta, m, v


def _adamw(w, g, m, v, name, after=()):
    r, cols = w.shape
    tr = r // 8 if r % 64 == 0 else r

    def body(w_ref, g_ref, m_ref, v_ref, d_ref, nm_ref, nv_ref):
        d_ref[...], nm_ref[...], nv_ref[...] = _adamw_math(w_ref[...], g_ref[...], m_ref[...], v_ref[...])

    tile = pl.BlockSpec((tr, cols), lambda i: (i, 0))
    shape = jax.ShapeDtypeStruct((r, cols), F32)
    return pl.pallas_call(
        _ordered(body, 4, after), name=name, grid=(r // tr,),
        in_specs=[tile] * 4 + [ANY] * len(after), out_specs=[tile] * 3, out_shape=[shape] * 3,
        compiler_params=_params(1),
    )(w, g, m, v, *after)


def _in_parts(tm, n_qkv, n_rest):
    def part(lo, n_blk):
        return pl.BlockSpec((tm, IN_BLOCK), lambda i, j: (i, jnp.clip(j - lo, 0, n_blk - 1)))
    return [part(0, n_qkv), part(n_qkv, n_qkv), part(2 * n_qkv, n_qkv), part(3 * n_qkv, n_rest)]


def _pick_part(j, n_qkv, refs, fn):
    bounds = [0, n_qkv, 2 * n_qkv, 3 * n_qkv]
    for p, ref in enumerate(refs):
        inside = j >= bounds[p]
        if p + 1 < len(refs):
            inside = inside & (j < bounds[p + 1])
        pl.when(inside)(lambda ref=ref: fn(ref))


def _in_proj(h, w, name):
    s_len, d = h.shape
    tm = PROJ_TILE
    steps = w.shape[1] // IN_BLOCK
    n_qkv = 3 * QKV // IN_BLOCK

    def body(h_ref, w_ref, qkv_ref, rest_ref):
        j = pl.program_id(1)
        res = _dot(h_ref[...], w_ref[...])

        @pl.when(j < n_qkv)
        def _():
            qkv_ref[...] = res

        @pl.when(j >= n_qkv)
        def _():
            rest_ref[...] = res

    return pl.pallas_call(
        body, name=name, grid=(s_len // tm, steps),
        in_specs=[pl.BlockSpec((tm, d), lambda i, j: (i, 0)), pl.BlockSpec((d, IN_BLOCK), lambda i, j: (0, j))],
        out_specs=[pl.BlockSpec((tm, IN_BLOCK), lambda i, j: (i, jnp.minimum(j, n_qkv - 1))),
                   pl.BlockSpec((tm, IN_BLOCK), lambda i, j: (i, jnp.maximum(j - n_qkv, 0)))],
        out_shape=[jax.ShapeDtypeStruct((s_len, 3 * QKV), F32),
                   jax.ShapeDtypeStruct((s_len, w.shape[1] - 3 * QKV), F32)],
        compiler_params=_params(2),
    )(h, w)


def _in_proj_bwd(dq, dk, dv, drest, w_t, name, after=()):
    s_len = dq.shape[0]
    d = w_t.shape[1]
    tm = PROJ_TILE
    steps = w_t.shape[0] // IN_BLOCK
    n_qkv = QKV // IN_BLOCK

    def body(dq_ref, dk_ref, dv_ref, dr_ref, w_ref, o_ref, acc_ref):
        j = pl.program_id(1)

        @pl.when(j == 0)
        def _():
            acc_ref[...] = jnp.zeros_like(acc_ref)

        def add(a_ref):
            acc_ref[...] += _dot(a_ref[...], w_ref[...])

        _pick_part(j, n_qkv, [dq_ref, dk_ref, dv_ref, dr_ref], add)

        @pl.when(j == steps - 1)
        def _():
            o_ref[...] = acc_ref[...]

    return pl.pallas_call(
        _ordered(body, 5, after), name=name, grid=(s_len // tm, steps),
        in_specs=(_in_parts(tm, n_qkv, steps - 3 * n_qkv) + [pl.BlockSpec((IN_BLOCK, d), lambda i, j: (j, 0))]
                  + [ANY] * len(after)),
        out_specs=pl.BlockSpec((tm, d), lambda i, j: (i, 0)),
        out_shape=jax.ShapeDtypeStruct((s_len, d), F32),
        scratch_shapes=[pltpu.VMEM((tm, d), F32)],
        compiler_params=_params(2),
    )(dq, dk, dv, drest, w_t, *after)


def _wgrad(x, y, x_spec, y_spec, out_shape, out_spec, acc_shape, n_chunks, name, x_transposed=False, after=()):
    s_len = y.shape[-2]
    ts = WGRAD_TILE
    steps = s_len // ts

    def body(x_ref, y_ref, o_ref, acc_ref):
        s = pl.program_id(1)

        @pl.when(s == 0)
        def _():
            acc_ref[...] = jnp.zeros_like(acc_ref)

        acc_ref[...] += (_dot if x_transposed else _dot_tn)(x_ref[...], y_ref[...])

        @pl.when(s == steps - 1)
        def _():
            o_ref[...] = acc_ref[...].astype(o_ref.dtype)

    return pl.pallas_call(
        _ordered(body, 2, after), name=name, grid=(n_chunks, steps),
        in_specs=[x_spec(ts), y_spec(ts)] + [ANY] * len(after), out_specs=out_spec,
        out_shape=jax.ShapeDtypeStruct(out_shape, BF16),
        scratch_shapes=[pltpu.VMEM(acc_shape, F32)],
        compiler_params=_params(2),
    )(x, y, *after)


def _pieces(width, piece=256):
    return [slice(a, min(a + piece, width)) for a in range(0, width, piece)]


def _ffn_fwd(h, w_gate, w_up, w_down, name):
    s_len, d = h.shape
    n_chunks, _, fs = w_gate.shape
    tm = FFN_TILE

    def body(h_ref, wg_ref, wu_ref, wd_ref, dg_ref, du_ref, a_ref, y_ref):
        j = pl.program_id(1)
        hv = h_ref[...]
        total = None
        for cols in _pieces(fs):
            g = _dot(hv, wg_ref[:, cols])
            u = _dot(hv, wu_ref[:, cols])
            sg = _sigmoid(g)
            silu = g * sg
            act = (silu * u).astype(BF16)
            dg_ref[:, cols] = (u * (sg * (1.0 + g * (1.0 - sg)))).astype(BF16)
            du_ref[:, cols] = silu.astype(BF16)
            a_ref[:, cols] = act
            part = _dot(act, wd_ref[cols, :])
            total = part if total is None else total + part

        @pl.when(j == 0)
        def _():
            y_ref[...] = total

        @pl.when(j > 0)
        def _():
            y_ref[...] += total

    tile = pl.BlockSpec((tm, d), lambda i, j: (i, 0))
    hid = pl.BlockSpec((None, tm, fs), lambda i, j: (j, i, 0))
    w_in_spec = pl.BlockSpec((None, d, fs), lambda i, j: (j, 0, 0))
    hid_shape = jax.ShapeDtypeStruct((n_chunks, s_len, fs), BF16)
    return pl.pallas_call(
        body, name=name, grid=(s_len // tm, n_chunks),
        in_specs=[tile, w_in_spec, w_in_spec, pl.BlockSpec((None, fs, d), lambda i, j: (j, 0, 0))],
        out_specs=[hid, hid, hid, tile],
        out_shape=[hid_shape, hid_shape, hid_shape, jax.ShapeDtypeStruct((s_len, d), F32)],
        compiler_params=_params(2),
    )(h, w_gate, w_up, w_down)


def _ffn_bwd(dy, act_dg, act_du, w_gate_t, w_up_t, w_down_t, name):
    s_len, d = dy.shape
    n_chunks, fs, _ = w_gate_t.shape
    tm = FFN_TILE

    def body(dy_ref, pg_ref, pu_ref, wgt_ref, wut_ref, wdt_ref, dh_ref, dg_ref, du_ref):
        j = pl.program_id(1)
        dyv = dy_ref[...]
        total = None
        for cols in _pieces(fs):
            da = _dot(dyv, wdt_ref[:, cols])
            dg = (da * pg_ref[:, cols].astype(F32)).astype(BF16)
            du = (da * pu_ref[:, cols].astype(F32)).astype(BF16)
            dg_ref[:, cols] = dg
            du_ref[:, cols] = du
            part = _dot(dg, wgt_ref[cols, :]) + _dot(du, wut_ref[cols, :])
            total = part if total is None else total + part

        @pl.when(j == 0)
        def _():
            dh_ref[...] = total

        @pl.when(j > 0)
        def _():
            dh_ref[...] += total

    tile = pl.BlockSpec((tm, d), lambda i, j: (i, 0))
    hid = pl.BlockSpec((None, tm, fs), lambda i, j: (j, i, 0))
    wide = pl.BlockSpec((None, fs, d), lambda i, j: (j, 0, 0))
    hid_shape = jax.ShapeDtypeStruct((n_chunks, s_len, fs), BF16)
    return pl.pallas_call(
        body, name=name, grid=(s_len // tm, n_chunks),
        in_specs=[tile, hid, hid, wide, wide, pl.BlockSpec((None, d, fs), lambda i, j: (j, 0, 0))],
        out_specs=[tile, hid, hid],
        out_shape=[jax.ShapeDtypeStruct((s_len, d), F32), hid_shape, hid_shape],
        compiler_params=_params(2),
    )(dy, act_dg, act_du, w_gate_t, w_up_t, w_down_t)


def _ffn_wgrads(ht, dg, du, act, dy, tag, after=()):
    n_chunks, s_len, fs = dg.shape
    d = ht.shape[0]
    tok = lambda ts: pl.BlockSpec((ts, d), lambda c, s: (s, 0))
    tok_t = lambda ts: pl.BlockSpec((d, ts), lambda c, s: (0, s))
    hid = lambda ts: pl.BlockSpec((None, ts, fs), lambda c, s: (c, s, 0))
    d_up = pl.BlockSpec((None, d, fs), lambda c, s: (c, 0, 0))
    d_down = pl.BlockSpec((None, fs, d), lambda c, s: (c, 0, 0))
    dwg = _wgrad(ht, dg, tok_t, hid, (n_chunks, d, fs), d_up, (d, fs), n_chunks, tag + "_dwg", True, after)
    dwu = _wgrad(ht, du, tok_t, hid, (n_chunks, d, fs), d_up, (d, fs), n_chunks, tag + "_dwu", True, after)
    dwd = _wgrad(act, dy, hid, tok, (n_chunks, fs, d), d_down, (fs, d), n_chunks, tag + "_dwd", False, after)
    return dwg, dwu, dwd


def _band_bias():
    qi = lax.broadcasted_iota(jnp.int32, (ATTN_BLOCK, 2 * ATTN_BLOCK), 0)
    kj = lax.broadcasted_iota(jnp.int32, (ATTN_BLOCK, 2 * ATTN_BLOCK), 1)
    band = (kj >= qi) & (kj <= qi + ATTN_BLOCK)
    return jnp.where(band, 0.0, NEG), jnp.where(band & (kj >= ATTN_BLOCK), 0.0, NEG)


def _rows(base, count, stride):
    return pl.ds(base, count) if stride == 1 else pl.ds(base, count, stride=stride)


def _qkv_specs(slab_of):
    def spec(sect, back):
        return pl.BlockSpec((SLAB, HEAD_DIM),
                            lambda h, s, g: (jnp.maximum(slab_of(s) - back, 0), (sect * N_GROUPS + g) * HEADS + h))
    return [spec(0, 0), spec(1, 0), spec(2, 0), spec(1, 1), spec(2, 1)]


def _load_qkv(q_ref, k_ref, v_ref, kp_ref, vp_ref, qn_ref, kn_ref, qs, kb, vb, n):
    e = HEAD_DIM
    q = q_ref[...]
    qs[...] = (q * _rms(q)) * qn_ref[...]
    k = k_ref[...]
    kb[SLAB:2 * SLAB, :] = (k * _rms(k)) * kn_ref[...]
    vb[SLAB:2 * SLAB, :] = v_ref[...]

    @pl.when(n > 0)
    def _():
        kp = kp_ref[...]
        kb[0:SLAB, :] = (kp * _rms(kp)) * kn_ref[...]
        vb[0:SLAB, :] = vp_ref[...]

    @pl.when(n == 0)
    def _():
        kb[0:SLAB, :] = jnp.zeros((SLAB, e), F32)
        vb[0:SLAB, :] = jnp.zeros((SLAB, e), F32)


def _for_each_tile(dil, n, tile_fn):
    span = ATTN_BLOCK * dil
    bias, first_bias = _band_bias()

    def sub(jj, carry):
        start = pl.multiple_of(jj * span, ATTN_BLOCK)
        tile_bias = jnp.where(jnp.logical_and(n == 0, jj == 0), first_bias, bias)
        for r in range(dil):
            tile_fn(_rows(start + r, ATTN_BLOCK, dil), _rows(SLAB - span + start + r, 2 * ATTN_BLOCK, dil), tile_bias)
        return carry

    lax.fori_loop(0, SLAB // span, sub, 0)


def _attn_fwd(qkv, q_norm, k_norm, name):
    s_len = qkv.shape[0]
    e = HEAD_DIM
    n_slabs = s_len // SLAB

    def body(q_ref, k_ref, v_ref, kp_ref, vp_ref, qn_ref, kn_ref, o_ref, lse_ref, qs, kb, vb, m_s, l_s, acc_s):
        n, grp = pl.program_id(1), pl.program_id(2)
        _load_qkv(q_ref, k_ref, v_ref, kp_ref, vp_ref, qn_ref, kn_ref, qs, kb, vb, n)

        def run(gi, dil):
            def tile(q_rows, kv_rows, bias):
                q = qs[q_rows, :].astype(BF16)
                k = kb[kv_rows, :].astype(BF16)
                v = vb[kv_rows, :].astype(BF16)
                s = _dot_nt(q, k) * ATTN_SCALE + bias
                m = jnp.max(s, axis=-1, keepdims=True)
                p = jnp.exp(s - m)
                m_s.at[gi][q_rows, :] = jnp.broadcast_to(m, (ATTN_BLOCK, e))
                l_s.at[gi][q_rows, :] = jnp.broadcast_to(jnp.sum(p, axis=-1, keepdims=True), (ATTN_BLOCK, e))
                acc_s.at[gi][q_rows, :] = _dot(p.astype(BF16), v)

            _for_each_tile(dil, n, tile)

        for gi, dil in enumerate(DILATIONS):
            pl.when(grp == gi)(lambda gi=gi, dil=dil: run(gi, dil))

        @pl.when(grp == N_GROUPS - 1)
        def _():
            m_all = jnp.maximum(jnp.maximum(m_s[0], m_s[1]), m_s[2])
            den = jnp.zeros((SLAB, e), F32)
            num = jnp.zeros((SLAB, e), F32)
            for gi in range(N_GROUPS):
                w = jnp.exp(m_s[gi] - m_all)
                den += l_s[gi] * w
                num += acc_s[gi] * w
            o_ref[...] = num / den
            lse_ref[...] = m_all + jnp.log(den)

    small = pl.BlockSpec((1, e), lambda h, n, g: (0, 0))
    out = pl.BlockSpec((SLAB, e), lambda h, n, g: (n, h))
    return pl.pallas_call(
        body, name=name, grid=(HEADS, n_slabs, N_GROUPS),
        in_specs=_qkv_specs(lambda n: n) + [small, small],
        out_specs=[out, out],
        out_shape=[jax.ShapeDtypeStruct((s_len, HEADS * e), F32)] * 2,
        scratch_shapes=[pltpu.VMEM((SLAB, e), F32), pltpu.VMEM((2 * SLAB, e), F32), pltpu.VMEM((2 * SLAB, e), F32),
                        pltpu.VMEM((N_GROUPS, SLAB, e), F32), pltpu.VMEM((N_GROUPS, SLAB, e), F32),
                        pltpu.VMEM((N_GROUPS, SLAB, e), F32)],
        compiler_params=_params(3),
    )(qkv, qkv, qkv, qkv, qkv, q_norm, k_norm)


def _attn_bwd(qkv, d_out, out, lse, q_norm, k_norm, name):
    s_len = qkv.shape[0]
    e = HEAD_DIM
    n_slabs = s_len // SLAB

    def body(q_ref, k_ref, v_ref, kp_ref, vp_ref, do_ref, o_ref, lse_ref, qn_ref, kn_ref, dq_ref, dk_ref, dv_ref,
             st_ref, qs, kb, vb, dqs, dkb, dvb, carry):
        head, step, grp = pl.program_id(0), pl.program_id(1), pl.program_id(2)
        n = n_slabs - 1 - step
        _load_qkv(q_ref, k_ref, v_ref, kp_ref, vp_ref, qn_ref, kn_ref, qs, kb, vb, n)
        dkb[...] = jnp.zeros_like(dkb)
        dvb[...] = jnp.zeros_like(dvb)

        @pl.when((head == 0) & (step == 0) & (grp == 0))
        def _():
            st_ref[...] = jnp.zeros_like(st_ref)

        def run(gi, dil):
            @pl.when(step == 0)
            def _():
                carry[gi] = jnp.zeros((2, SLAB, e), F32)

            def tile(q_rows, kv_rows, bias):
                q = qs[q_rows, :].astype(BF16)
                k = kb[kv_rows, :].astype(BF16)
                v = vb[kv_rows, :].astype(BF16)
                do = do_ref[q_rows, :]
                delta = jnp.sum(do * o_ref[q_rows, :], axis=-1, keepdims=True)
                s = _dot_nt(q, k) * ATTN_SCALE + bias
                p = jnp.exp(s - lse_ref[q_rows, :][:, 0:1])
                do16 = do.astype(BF16)
                ds = (p * (_dot_nt(do16, v) - delta) * ATTN_SCALE).astype(BF16)
                dqs[q_rows, :] = _dot(ds, k)
                dkb[kv_rows, :] += _dot_tn(ds, q)
                dvb[kv_rows, :] += _dot_tn(p.astype(BF16), do16)

            _for_each_tile(dil, n, tile)
            dk_hat = dkb[SLAB:2 * SLAB, :] + carry[gi, 0]
            dv = dvb[SLAB:2 * SLAB, :] + carry[gi, 1]
            carry[gi, 0] = dkb[0:SLAB, :]
            carry[gi, 1] = dvb[0:SLAB, :]

            def norm_bwd(raw, gain, d_hat):
                r = _rms(raw)
                y = raw * r
                dy = d_hat * gain
                return r * (dy - y * jnp.mean(dy * y, axis=-1, keepdims=True)), jnp.sum(d_hat * y, axis=0, keepdims=True)

            dq, dqn = norm_bwd(q_ref[...], qn_ref[...], dqs[...])
            dk, dkn = norm_bwd(k_ref[...], kn_ref[...], dk_hat)
            dq_ref[...] = dq.astype(BF16)
            dk_ref[...] = dk.astype(BF16)
            dv_ref[...] = dv.astype(BF16)
            st_ref[0:1, :] += dqn
            st_ref[1:2, :] += dkn

        for gi, dil in enumerate(DILATIONS):
            pl.when(grp == gi)(lambda gi=gi, dil=dil: run(gi, dil))

    slab_of = lambda s: n_slabs - 1 - s
    small = pl.BlockSpec((1, e), lambda h, s, g: (0, 0))
    head_blk = pl.BlockSpec((SLAB, e), lambda h, s, g: (slab_of(s), h))
    grad_blk = pl.BlockSpec((SLAB, e), lambda h, s, g: (slab_of(s), g * HEADS + h))
    grad_shape = jax.ShapeDtypeStruct((s_len, QKV), BF16)
    return pl.pallas_call(
        body, name=name, grid=(HEADS, n_slabs, N_GROUPS),
        in_specs=_qkv_specs(slab_of) + [head_blk, head_blk, head_blk, small, small],
        out_specs=[grad_blk, grad_blk, grad_blk, pl.BlockSpec((8, e), lambda h, s, g: (0, 0))],
        out_shape=[grad_shape, grad_shape, grad_shape, jax.ShapeDtypeStruct((8, e), F32)],
        scratch_shapes=[pltpu.VMEM((SLAB, e), F32), pltpu.VMEM((2 * SLAB, e), F32), pltpu.VMEM((2 * SLAB, e), F32),
                        pltpu.VMEM((SLAB, e), F32), pltpu.VMEM((2 * SLAB, e), F32), pltpu.VMEM((2 * SLAB, e), F32),
                        pltpu.VMEM((N_GROUPS, 2, SLAB, e), F32)],
        compiler_params=_params(3),
    )(qkv, qkv, qkv, qkv, qkv, d_out, out, lse, q_norm, k_norm)


def _shift_rows(x, by, edge, forward):
    t_len = x.shape[0]
    row = lax.broadcasted_iota(jnp.int32, x.shape, 0)
    if forward:
        out = pltpu.roll(x, by, 0)
        for i in range(by):
            out = jnp.where(row == i, edge[8 - by + i:8 - by + i + 1, :], out)
    else:
        out = pltpu.roll(x, t_len - by, 0)
        for i in range(by):
            out = jnp.where(row == t_len - by + i, edge[i:i + 1, :], out)
    return out


def _mix_fwd(x, o, rest, mod, conv_w, w_attn, w_conv, w_out, name):
    s_len, d = x.shape
    tm = MIX_TILE
    a_w = o.shape[1]

    def body(x_ref, o_ref, u_ref, b_ref, c_ref, ga_ref, gc_ref, mod_ref, cw_ref, wa_ref, wc_ref, wo_ref,
             xo_ref, z_ref, ya_ref, yc_ref, conv_ref, yb_ref, m_ref, o16_ref, carry):
        @pl.when(pl.program_id(0) == 0)
        def _():
            carry[...] = jnp.zeros_like(carry)

        xc = c_ref[...] * u_ref[...]
        edge = carry[...]
        conv = (_shift_rows(xc, 2, edge, True) * cw_ref[0:1, :] + _shift_rows(xc, 1, edge, True) * cw_ref[1:2, :]
                + xc * cw_ref[2:3, :])
        carry[...] = xc[tm - 8:tm, :]
        yb = (b_ref[...] * conv).astype(BF16)
        o16 = o_ref[...].astype(BF16)
        ya = _dot(o16, wa_ref[...])
        yc = _dot(yb, wc_ref[...])
        merged = (_sigmoid(ga_ref[...]) * ya + _sigmoid(gc_ref[...]) * yc).astype(BF16)
        z = _dot(merged, wo_ref[...])
        xo_ref[...] = x_ref[...] + mod_ref[2:3, :] * z
        z_ref[...] = z
        ya_ref[...] = ya.astype(BF16)
        yc_ref[...] = yc.astype(BF16)
        conv_ref[...] = conv.astype(BF16)
        yb_ref[...] = yb
        m_ref[...] = merged
        o16_ref[...] = o16

    tile = pl.BlockSpec((tm, d), lambda i: (i, 0))
    sect = lambda k: pl.BlockSpec((tm, d), lambda i: (i, k))
    att = pl.BlockSpec((tm, a_w), lambda i: (i, 0))
    const = lambda shape: pl.BlockSpec(shape, lambda i: (0, 0))
    f32_out = jax.ShapeDtypeStruct((s_len, d), F32)
    b16_out = jax.ShapeDtypeStruct((s_len, d), BF16)
    return pl.pallas_call(
        body, name=name, grid=(s_len // tm,),
        in_specs=[tile, att, sect(0), sect(1), sect(2), sect(3), sect(4), const((8, d)), const((8, d)),
                  const((a_w, d)), const((d, d)), const((d, d))],
        out_specs=[tile] * 7 + [att],
        out_shape=[f32_out, f32_out] + [b16_out] * 5 + [jax.ShapeDtypeStruct((s_len, a_w), BF16)],
        scratch_shapes=[pltpu.VMEM((8, d), F32)],
        compiler_params=_params(1),
    )(x, o, rest, rest, rest, rest, rest, mod, conv_w, w_attn, w_conv, w_out)


def _mix_bwd(dxo, ya, yc, conv, rest, mod, conv_w, w_attn, w_conv, w_out, a_w, name):
    s_len, d = dxo.shape
    tm = MIX_TILE
    n_tiles = s_len // tm

    def body(dxo_ref, ya_ref, yc_ref, conv_ref, u_ref, b_ref, c_ref, ga_ref, gc_ref, mod_ref, cw_ref,
             wa_ref, wc_ref, wo_ref, do_ref, drest_ref, dz_ref, dya_ref, dyc_ref, st_ref, carry):
        @pl.when(pl.program_id(0) == 0)
        def _():
            carry[...] = jnp.zeros_like(carry)
            st_ref[...] = jnp.zeros_like(st_ref)

        dz = (mod_ref[2:3, :] * dxo_ref[...]).astype(BF16)
        dz_ref[...] = dz
        dm = _dot_nt(dz, wo_ref[...])
        sa, sc = _sigmoid(ga_ref[...]), _sigmoid(gc_ref[...])
        dya = (dm * sa).astype(BF16)
        dyc = (dm * sc).astype(BF16)
        dya_ref[...] = dya
        dyc_ref[...] = dyc
        drest_ref[:, 3 * d:4 * d] = (dm * ya_ref[...].astype(F32) * (sa * (1.0 - sa))).astype(BF16)
        drest_ref[:, 4 * d:5 * d] = (dm * yc_ref[...].astype(F32) * (sc * (1.0 - sc))).astype(BF16)
        do_ref[...] = _dot_nt(dya, wa_ref[...])
        dyb = _dot_nt(dyc, wc_ref[...])
        drest_ref[:, d:2 * d] = (dyb * conv_ref[...].astype(F32)).astype(BF16)
        dconv = dyb * b_ref[...]
        edge = carry[...]
        sh1 = _shift_rows(dconv, 1, edge, False)
        sh2 = _shift_rows(dconv, 2, edge, False)
        carry[...] = dconv[0:8, :]
        dxc = dconv * cw_ref[2:3, :] + sh1 * cw_ref[1:2, :] + sh2 * cw_ref[0:1, :]
        u, c = u_ref[...], c_ref[...]
        xc = c * u
        drest_ref[:, 0:d] = (dxc * c).astype(BF16)
        drest_ref[:, 2 * d:3 * d] = (dxc * u).astype(BF16)
        st_ref[0:1, :] += jnp.sum(xc * sh2, axis=0, keepdims=True)
        st_ref[1:2, :] += jnp.sum(xc * sh1, axis=0, keepdims=True)
        st_ref[2:3, :] += jnp.sum(xc * dconv, axis=0, keepdims=True)

    rev = lambda i: n_tiles - 1 - i
    tile = pl.BlockSpec((tm, d), lambda i: (rev(i), 0))
    sect = lambda k: pl.BlockSpec((tm, d), lambda i: (rev(i), k))
    const = lambda shape: pl.BlockSpec(shape, lambda i: (0, 0))
    b16_out = jax.ShapeDtypeStruct((s_len, d), BF16)
    return pl.pallas_call(
        body, name=name, grid=(n_tiles,),
        in_specs=[tile, tile, tile, tile, sect(0), sect(1), sect(2), sect(3), sect(4), const((8, d)), const((8, d)),
                  const((a_w, d)), const((d, d)), const((d, d))],
        out_specs=[pl.BlockSpec((tm, a_w), lambda i: (rev(i), 0)), pl.BlockSpec((tm, 5 * d), lambda i: (rev(i), 0)),
                   tile, tile, tile, const((8, d))],
        out_shape=[jax.ShapeDtypeStruct((s_len, a_w), F32), jax.ShapeDtypeStruct((s_len, 5 * d), BF16),
                   b16_out, b16_out, b16_out, jax.ShapeDtypeStruct((8, d), F32)],
        scratch_shapes=[pltpu.VMEM((8, d), F32)],
        compiler_params=_params(1),
    )(dxo, ya, yc, conv, rest, rest, rest, rest, rest, mod, conv_w, w_attn, w_conv, w_out)


ADA_COLS = 128


def _ada_fwd(c_all, w_shard, b_shard, name):
    d, cols = w_shard.shape

    def body(c_ref, w_ref, b_ref, o_ref):
        cv = c_ref[...]
        o_ref[...] = jnp.dot(cv * _sigmoid(cv), w_ref[...], preferred_element_type=F32,
                             precision=lax.Precision.HIGHEST) + b_ref[...]

    return pl.pallas_call(
        body, name=name, grid=(cols // ADA_COLS,),
        in_specs=[pl.BlockSpec((8, d), lambda j: (0, 0)), pl.BlockSpec((d, ADA_COLS), lambda j: (0, j)),
                  pl.BlockSpec((1, ADA_COLS), lambda j: (0, j))],
        out_specs=pl.BlockSpec((8, ADA_COLS), lambda j: (0, j)),
        out_shape=jax.ShapeDtypeStruct((8, cols), F32),
        compiler_params=_params(1),
    )(c_all, w_shard, b_shard)


def _ada_bwd(c_all, dmod_shard, w, m, v, name):
    d, cols = w.shape

    def body(c_ref, dm_ref, w_ref, m_ref, v_ref, g_ref, d_ref, nm_ref, nv_ref):
        cv = c_ref[...]
        g = lax.dot_general(cv * _sigmoid(cv), dm_ref[...], (((0,), (0,)), ((), ())),
                            preferred_element_type=F32, precision=lax.Precision.HIGHEST)
        g_ref[...] = g
        d_ref[...], nm_ref[...], nv_ref[...] = _adamw_math(w_ref[...], g, m_ref[...], v_ref[...])

    blk = pl.BlockSpec((d, ADA_COLS), lambda j: (0, j))
    shape = jax.ShapeDtypeStruct((d, cols), F32)
    return pl.pallas_call(
        body, name=name, grid=(cols // ADA_COLS,),
        in_specs=[pl.BlockSpec((8, d), lambda j: (0, 0)), pl.BlockSpec((8, ADA_COLS), lambda j: (0, j)), blk, blk, blk],
        out_specs=[blk] * 4, out_shape=[shape] * 4,
        compiler_params=_params(1),
    )(c_all, dmod_shard, w, m, v)


def _small_update(parts, w, m, v, name):
    n = w.shape[1]

    def body(p_ref, w_ref, m_ref, v_ref, g_ref, d_ref, nm_ref, nv_ref):
        g = p_ref[0:1, :]
        for i in range(1, 8):
            g = g + p_ref[i:i + 1, :]
        g_ref[...] = g
        d_ref[...], nm_ref[...], nv_ref[...] = _adamw_math(w_ref[...], g, m_ref[...], v_ref[...])

    shape = jax.ShapeDtypeStruct((1, n), F32)
    return pl.pallas_call(body, name=name, out_shape=[shape] * 4, compiler_params=_params())(parts, w, m, v)


def _cols_to_shards(w, n):
    r, nc = w.shape
    return w.reshape(r, n, nc // n).transpose(1, 0, 2)


def kernel(x, c, w_ada, b_ada, norm_ffn1, ffn1_w_gate, ffn1_w_up, ffn1_w_down, norm_mix, w_in, q_norm, k_norm, conv_w, w_attn_branch, w_conv_branch, w_out, norm_ffn2, ffn2_w_gate, ffn2_w_up, ffn2_w_down, loss_target, m_w_ada, m_b_ada, m_norm_ffn1, m_ffn1_w_gate, m_ffn1_w_up, m_ffn1_w_down, m_norm_mix, m_w_in, m_q_norm, m_k_norm, m_conv_w, m_w_attn_branch, m_w_conv_branch, m_w_out, m_norm_ffn2, m_ffn2_w_gate, m_ffn2_w_up, m_ffn2_w_down, v_w_ada, v_b_ada, v_norm_ffn1, v_ffn1_w_gate, v_ffn1_w_up, v_ffn1_w_down, v_norm_mix, v_w_in, v_q_norm, v_k_norm, v_conv_w, v_w_attn_branch, v_w_conv_branch, v_w_out, v_norm_ffn2, v_ffn2_w_gate, v_ffn2_w_up, v_ffn2_w_down):
    ix, iy, ic = _place()
    chip = 2 * ix + iy
    me = 4 * ix + 2 * iy + ic
    xs = x[0]
    target = loss_target[0]
    s_len, d = xs.shape
    ada_cols = w_ada.shape[2]
    conv_cols = conv_w.shape[2]

    conv_rows = jnp.zeros((8, conv_cols), F32).at[0:3].set(conv_w[0])
    small_in = jnp.concatenate([jnp.broadcast_to(c, (8, d)), conv_rows], axis=1)
    small_all = _allgather8(small_in, "gather_c").reshape(8, 8, d + conv_cols)
    c_all = small_all[:, 0, :d]
    conv_full = small_all[0::2, 0:3, d:].transpose(1, 0, 2).reshape(3, N_CHIPS * conv_cols)
    conv_pad = jnp.zeros((8, N_CHIPS * conv_cols), F32).at[0:3].set(conv_full)
    b_shard = lax.dynamic_slice(b_ada, (0, chip * ada_cols), (1, ada_cols))
    mod_part = _ada_fwd(c_all, w_ada[0], b_shard, "ada_fwd")
    mod_all = _allgather8(mod_part, "gather_mod").reshape(N_CHIPS, 2, 8, ada_cols)[:, 0]
    mod_mine = lax.dynamic_slice(mod_all, (0, me, 0), (N_CHIPS, 1, ada_cols)).reshape(9, d)

    def mod_rows(i, gain):
        return jnp.zeros((8, d), F32).at[0:3].set(mod_mine[3 * i:3 * i + 3]).at[3:4].set(gain)

    mod1, mod2, mod3 = mod_rows(0, norm_ffn1), mod_rows(1, norm_mix), mod_rows(2, norm_ffn2)

    to16 = lambda w: w[0].astype(BF16)
    wg1, wu1, wd1 = _gather_weights([to16(ffn1_w_gate), to16(ffn1_w_up), to16(ffn1_w_down)], [False] * 3,
                                    "gather_ffn1", 1)
    h1, h1t = _norm_mod(xs, mod1, "norm1")
    (w_in_full,) = _gather_weights([to16(w_in)], [True], "gather_w_in", 2, after=(wd1, h1))

    pg1, pu1, a1, y1 = _ffn_fwd(h1, wg1, wu1, wd1, "ffn1_fwd")
    x1, h2, h2t = _norm_mod(xs, mod2, "norm2", prev=(y1, mod1, 0.5))
    qkv, rest = _in_proj(h2, w_in_full, "in_proj")
    w_ab, w_cb_g, w_o_g, wg2, wu2, wd2 = _gather_weights(
        [to16(w_attn_branch), to16(w_conv_branch), to16(w_out),
         to16(ffn2_w_gate), to16(ffn2_w_up), to16(ffn2_w_down)], [True] + [False] * 5,
        "gather_rest", 3, after=(qkv,))
    a_w = w_ab.shape[0]
    w_cb = w_cb_g.reshape(d, d)
    w_o = w_o_g.reshape(d, d)
    o, lse = _attn_fwd(qkv, q_norm, k_norm, "attn_fwd")
    to16_t = lambda w: w[0].T.astype(BF16)
    wg2_t, wu2_t, wd2_t = _gather_weights([to16_t(ffn2_w_gate), to16_t(ffn2_w_up), to16_t(ffn2_w_down)], [False] * 3,
                                          "gather_ffn2_t", 7, after=(o,))
    x2, z, ya, yc, conv, yb, merged, o16 = _mix_fwd(x1, o, rest, mod2, conv_pad, w_ab, w_cb, w_o, "mix_fwd")
    h3, h3t = _norm_mod(x2, mod3, "norm3")
    pg3, pu3, a3, y3 = _ffn_fwd(h3, wg2, wu2, wd2, "ffn2_fwd")
    dx3, dy3, loss_part = _loss_grad(x2, y3, mod3, target, "loss")
    wg1_t, wu1_t, wd1_t, w_in_t = _gather_weights(
        [to16_t(ffn1_w_gate), to16_t(ffn1_w_up), to16_t(ffn1_w_down), to16_t(w_in)], [False] * 4,
        "gather_bwd_t", 8, after=(wd2_t, dy3))
    w_in_t = w_in_t.reshape(N_CHIPS * w_in_t.shape[1], d)
    loss = lax.psum(0.5 * jnp.sum(loss_part) / d, ("x", "y", "c"))

    c_idx = jnp.reshape(ic, (1,)).astype(jnp.int32)
    chip_idx = jnp.stack([chip, ic]).astype(jnp.int32)

    def reduce_start(grads, names, tag, collective_id):
        from_sibling = _rs_pair_exchange(grads, "rs_pair_" + tag)
        pair_sums = [_pair_add(g, r, c_idx, "pair_add_" + nm) for g, r, nm in zip(grads, from_sibling, names)]
        return pair_sums, _rs_chip_exchange(pair_sums, "rs_chips_" + tag, collective_id)

    def reduce_finish(pair_sums, from_chips, names, tag, after):
        totals = [_chip_add(p, r, chip_idx, "chip_add_" + nm, after)
                  for p, r, nm in zip(pair_sums, from_chips, names)]
        return dict(zip(names, _rs_share(totals, "rs_share_" + tag)))

    names_a = ["ffn2_w_gate", "ffn2_w_up", "ffn2_w_down"]
    names_b = ["w_in", "w_attn_branch", "w_conv_branch", "w_out"]
    names_c = ["ffn1_w_gate", "ffn1_w_up", "ffn1_w_down"]

    dh3, dg3, du3 = _ffn_bwd(dy3, pg3, pu3, wg2_t, wu2_t, wd2_t, "ffn2_bwd")
    sums_a, chips_a = reduce_start(list(_ffn_wgrads(h3t, dg3, du3, a3, dy3, "ffn2")), names_a, "a", 4)
    dx2, st3 = _norm_bwd(dh3, x2, mod3, dx3, y3, 0.5, "norm3_bwd", after=tuple(sums_a))

    do, drest, dz, dya, dyc, st_conv = _mix_bwd(dx2, ya, yc, conv, rest, mod2, conv_pad, w_ab, w_cb, w_o, a_w, "mix_bwd")
    dq, dk, dv, st_qk = _attn_bwd(qkv, do, o, lse, q_norm, k_norm, "attn_bwd")
    tok = lambda width: (lambda ts: pl.BlockSpec((ts, width), lambda cc, s: (s, 0)))
    colblk = lambda width: (lambda ts: pl.BlockSpec((ts, width), lambda cc, s: (s, cc)))
    tok_t = lambda ts: pl.BlockSpec((d, ts), lambda cc, s: (0, s))
    whole = pl.BlockSpec((d, QKV), lambda cc, s: (0, 0))
    dw_in = [_wgrad(h2t, part, tok_t, tok(QKV), (d, QKV), whole, (d, QKV), 1, "dw_in_" + nm, True)
             for part, nm in ((dq, "q"), (dk, "k"), (dv, "v"))]
    dw_in.append(_wgrad(h2t, drest, tok_t, colblk(d), (d, 5 * d), pl.BlockSpec((d, d), lambda cc, s: (0, cc)),
                        (d, d), 5, "dw_in_rest", True))
    dw_in = _cols_to_shards(jnp.concatenate(dw_in, axis=1), N_CHIPS)
    shard_w = d // N_CHIPS
    dw_ab = _wgrad(o16, dya, tok(a_w), colblk(shard_w), (a_w, d), pl.BlockSpec((a_w, shard_w), lambda cc, s: (0, cc)),
                   (a_w, shard_w), N_CHIPS, "dw_attn_branch")
    dw_ab = _cols_to_shards(dw_ab, N_CHIPS)
    row_out = pl.BlockSpec((None, shard_w, d), lambda cc, s: (cc, 0, 0))
    dw_cb = _wgrad(yb, dyc, colblk(shard_w), tok(d), (N_CHIPS, shard_w, d), row_out, (shard_w, d), N_CHIPS, "dw_conv_branch")
    dw_o = _wgrad(merged, dz, colblk(shard_w), tok(d), (N_CHIPS, shard_w, d), row_out, (shard_w, d), N_CHIPS, "dw_out")
    shard_grads = reduce_finish(sums_a, chips_a, names_a, "a", after=(dw_in, dw_o))
    sums_b, chips_b = reduce_start([dw_in, dw_ab, dw_cb, dw_o], names_b, "b", 5)

    dh2 = _in_proj_bwd(dq, dk, dv, drest, w_in_t, "in_proj_bwd", after=tuple(sums_b))
    dx1, st2, dy1 = _norm_bwd(dh2, x1, mod2, dx2, z, 1.0, "norm2_bwd", prev=(mod1, 0.5))
    dh1, dg1, du1 = _ffn_bwd(dy1, pg1, pu1, wg1_t, wu1_t, wd1_t, "ffn1_bwd")
    dx0, st1 = _norm_bwd(dh1, xs, mod1, dx1, y1, 0.5, "norm1_bwd")
    grads_c = list(_ffn_wgrads(h1t, dg1, du1, a1, dy1, "ffn1"))
    shard_grads.update(reduce_finish(sums_b, chips_b, names_b, "b", after=tuple(grads_c)))
    sums_c, chips_c = reduce_start(grads_c, names_c, "c", 6)

    dmod = jnp.concatenate([st1[0:3], st2[0:3], st3[0:3]], axis=0).reshape(1, 9 * d)
    small = jnp.concatenate([dmod, st1[3:4], st2[3:4], st3[3:4], st_qk[0:1], st_qk[1:2],
                             st_conv[0:3].reshape(1, 3 * d)], axis=1)
    small_all = _allgather8(jnp.broadcast_to(small, (8, small.shape[1])), "gather_small").reshape(8, 8, -1)[:, 0]
    dmod_all = small_all[:, :9 * d]
    dmod_shard = lax.dynamic_slice(dmod_all, (0, chip * ada_cols), (8, ada_cols))
    g_w_ada, d_w_ada, nm_w_ada, nv_w_ada = _ada_bwd(c_all, dmod_shard, w_ada[0], m_w_ada[0], v_w_ada[0], "ada_bwd")

    vec_names = ["b_ada", "norm_ffn1", "norm_mix", "norm_ffn2", "q_norm", "k_norm"]
    vec_w = [b_ada, norm_ffn1, norm_mix, norm_ffn2, q_norm, k_norm]
    vec_m = [m_b_ada, m_norm_ffn1, m_norm_mix, m_norm_ffn2, m_q_norm, m_k_norm]
    vec_v = [v_b_ada, v_norm_ffn1, v_norm_mix, v_norm_ffn2, v_q_norm, v_k_norm]
    n_vec = sum(w.shape[1] for w in vec_w)
    cat = lambda arrs: jnp.concatenate(arrs, axis=1)
    vec_out = _small_update(small_all[:, :n_vec], cat(vec_w), cat(vec_m), cat(vec_v), "small_update")
    conv_parts = small_all[:, n_vec:].reshape(8, 3, N_CHIPS * conv_cols)
    conv_parts = lax.dynamic_slice(conv_parts, (0, 0, chip * conv_cols), (8, 3, conv_cols)).reshape(8, 3 * conv_cols)
    flat3 = lambda w: w[0].reshape(1, 3 * conv_cols)
    conv_out = _small_update(conv_parts, flat3(conv_w), flat3(m_conv_w), flat3(v_conv_w), "conv_update")

    res = {"w_ada": [t[None] for t in (g_w_ada, d_w_ada, nm_w_ada, nv_w_ada)],
           "conv_w": [t.reshape(1, 3, conv_cols) for t in conv_out]}
    off = 0
    for nm, w in zip(vec_names, vec_w):
        width = w.shape[1]
        res[nm] = [t[:, off:off + width] for t in vec_out]
        off += width
    big = {"ffn1_w_gate": (ffn1_w_gate, m_ffn1_w_gate, v_ffn1_w_gate), "ffn1_w_up": (ffn1_w_up, m_ffn1_w_up, v_ffn1_w_up),
           "ffn1_w_down": (ffn1_w_down, m_ffn1_w_down, v_ffn1_w_down), "w_in": (w_in, m_w_in, v_w_in),
           "w_attn_branch": (w_attn_branch, m_w_attn_branch, v_w_attn_branch),
           "w_conv_branch": (w_conv_branch, m_w_conv_branch, v_w_conv_branch), "w_out": (w_out, m_w_out, v_w_out),
           "ffn2_w_gate": (ffn2_w_gate, m_ffn2_w_gate, v_ffn2_w_gate), "ffn2_w_up": (ffn2_w_up, m_ffn2_w_up, v_ffn2_w_up),
           "ffn2_w_down": (ffn2_w_down, m_ffn2_w_down, v_ffn2_w_down)}
    def update(nm, after=()):
        w, m, v = big[nm]
        g = shard_grads[nm]
        delta, new_m, new_v = _adamw(w[0], g, m[0], v[0], "adamw_" + nm, after)
        res[nm] = [t[None] for t in (g, delta, new_m, new_v)]
        return new_v

    last = tuple(sums_c)
    for nm in names_a + names_b:
        last = (update(nm, last),)
    shard_grads.update(reduce_finish(sums_c, chips_c, names_c, "c", after=last))
    for nm in names_c:
        update(nm)

    order = ["w_ada", "b_ada", "norm_ffn1", "ffn1_w_gate", "ffn1_w_up", "ffn1_w_down", "norm_mix", "w_in", "q_norm",
             "k_norm", "conv_w", "w_attn_branch", "w_conv_branch", "w_out", "norm_ffn2", "ffn2_w_gate", "ffn2_w_up",
             "ffn2_w_down"]
    return (loss, dx0[None], *[res[nm][0] for nm in order], *[res[nm][1] for nm in order],
            *[res[nm][2] for nm in order], *[res[nm][3] for nm in order])
```

```python
import jax
import jax.numpy as jnp
from jax import lax
from jax.experimental import pallas as pl
from jax.experimental.pallas import tpu as pltpu
from jax.experimental.pallas import tpu_sc as plsc

F32 = jnp.float32
BF16 = jnp.bfloat16
MESH = pl.DeviceIdType.MESH
ANY = pl.BlockSpec(memory_space=pl.ANY)

NORM_EPS = 1e-6
HEAD_DIM = 128
N_GROUPS = 3
HEADS = 4
DILATIONS = (1, 4, 16)
ATTN_BLOCK = 128
SLAB = ATTN_BLOCK * max(DILATIONS)
QKV = N_GROUPS * HEADS * HEAD_DIM
ATTN_SCALE = HEAD_DIM ** -0.5
NEG = -1e30
N_CHIPS = 4

ADAM_LR = 0.001
ADAM_B1 = 0.9
ADAM_B2 = 0.999
ADAM_EPS = 1e-08
ADAM_WD = 0.01
ADAM_STEP = 10

VMEM_LIMIT_BYTES = 56 * 1024 * 1024
TOKEN_TILE = 512
FFN_TILE = 1024
PROJ_TILE = 2048
WGRAD_TILE = 2048
IN_BLOCK = 512
MIX_TILE = 256


def _params(n_axes=0):
    return pltpu.CompilerParams(
        dimension_semantics=("arbitrary",) * n_axes if n_axes else None,
        vmem_limit_bytes=VMEM_LIMIT_BYTES)


def _dot(a, b):
    return jnp.dot(a, b, preferred_element_type=F32)


def _dot_nt(a, b):
    return lax.dot_general(a, b, (((1,), (1,)), ((), ())), preferred_element_type=F32)


def _dot_tn(a, b):
    return lax.dot_general(a, b, (((0,), (0,)), ((), ())), preferred_element_type=F32)


def _sigmoid(x):
    return 1.0 / (1.0 + jnp.exp(-x))


def _place():
    return lax.axis_index("x"), lax.axis_index("y"), lax.axis_index("c")


def _ordered(body, n_in, after):
    if not after:
        return body
    return lambda *refs: body(*refs[:n_in], *refs[n_in + len(after):])


def _allgather8(block, name):
    m_per, n = block.shape

    def body(x_ref, out_ref, send_sems, recv_sems, local_sem):
        x, y, c = _place()
        me, sibling = (x, y, c), (x, y, 1 - c)
        chips = [(1 - x, y), (x, 1 - y), (1 - x, 1 - y)]

        def rows(px, py, pc):
            return out_ref.at[pl.ds((4 * px + 2 * py + pc) * m_per, m_per), :]

        def copy(k, blk, to, src=None):
            return pltpu.make_async_remote_copy(
                src_ref=rows(*blk) if src is None else src, dst_ref=rows(*blk),
                send_sem=send_sems.at[k], recv_sem=recv_sems.at[k],
                device_id=to, device_id_type=MESH)

        mine = pltpu.make_async_copy(x_ref, rows(*me), local_sem)
        mine.start()
        first = [copy(0, me, sibling, src=x_ref)]
        first += [copy(1 + j, me, (*chip, c), src=x_ref) for j, chip in enumerate(chips)]
        for cp in first:
            cp.start()
        passed = [copy(4 + j, (*chip, c), sibling) for j, chip in enumerate(chips)]
        for j, chip in enumerate(chips):
            copy(1 + j, (*chip, c), me).wait_recv()
            passed[j].start()
        copy(0, sibling, me).wait_recv()
        for j, chip in enumerate(chips):
            copy(4 + j, (*chip, 1 - c), me).wait_recv()
        for cp in first + passed:
            cp.wait_send()
        mine.wait()

    return pl.pallas_call(
        body, name=name,
        out_shape=jax.ShapeDtypeStruct((8 * m_per, n), block.dtype),
        in_specs=[pl.BlockSpec(memory_space=pltpu.VMEM)],
        out_specs=pl.BlockSpec(memory_space=pltpu.VMEM),
        scratch_shapes=[pltpu.SemaphoreType.DMA((7,)), pltpu.SemaphoreType.DMA((7,)),
                        pltpu.SemaphoreType.DMA],
        compiler_params=_params(),
    )(block)


def _handshake(peers):
    barrier = pltpu.get_barrier_semaphore()
    for peer in peers:
        pl.semaphore_signal(barrier, inc=1, device_id=peer, device_id_type=MESH)
    pl.semaphore_wait(barrier, len(peers))


def _gather_weights(shards, by_cols, name, collective_id, after=()):
    n_arr = len(shards)

    def body(*refs):
        srcs, outs = refs[:n_arr], refs[n_arr + len(after):2 * n_arr + len(after)]
        send_sems, recv_sems, local_sems = refs[2 * n_arr + len(after):]
        x, y, c = _place()
        me_dev, sibling = (x, y, c), (x, y, 1 - c)
        chips = [(1 - x, y), (x, 1 - y), (1 - x, 1 - y)]
        me = 2 * x + y
        _handshake([sibling] + [(*chip, c) for chip in chips])

        def place(k, chip_idx, rows):
            if by_cols[k]:
                width = srcs[k].shape[1]
                return outs[k].at[rows, pl.ds(pl.multiple_of(chip_idx * width, 128), width)]
            return outs[k].at[chip_idx, rows]

        def copy(k, slot, chip_idx, half_sel, to, from_shard=False):
            half = srcs[k].shape[0] // 2
            rows = pl.ds(half_sel * half, half)
            dst = place(k, chip_idx, rows)
            return pltpu.make_async_remote_copy(
                src_ref=srcs[k].at[rows] if from_shard else dst, dst_ref=dst,
                send_sem=send_sems.at[6 * k + slot], recv_sem=recv_sems.at[6 * k + slot],
                device_id=to, device_id_type=MESH)

        own = [pltpu.make_async_copy(srcs[k], place(k, me, pl.ds(0, srcs[k].shape[0])), local_sems.at[k])
               for k in range(n_arr)]
        for cp in own:
            cp.start()
        sent = []
        for k in range(n_arr):
            for j, chip in enumerate(chips):
                sent.append(copy(k, j, me, c, (*chip, c), from_shard=True))
                sent[-1].start()
        for k in range(n_arr):
            for j, chip in enumerate(chips):
                chip_idx = 2 * chip[0] + chip[1]
                copy(k, j, chip_idx, c, me_dev).wait_recv()
                sent.append(copy(k, 3 + j, chip_idx, c, sibling))
                sent[-1].start()
        for k in range(n_arr):
            for j, chip in enumerate(chips):
                copy(k, 3 + j, 2 * chip[0] + chip[1], 1 - c, me_dev).wait_recv()
        for cp in sent:
            cp.wait_send()
        for cp in own:
            cp.wait()

    def gathered(k):
        r, cols = shards[k].shape
        return (r, N_CHIPS * cols) if by_cols[k] else (N_CHIPS, r, cols)

    return pl.kernel(
        body, name=name,
        out_type=[jax.ShapeDtypeStruct(gathered(k), shards[k].dtype) for k in range(n_arr)],
        mesh=plsc.ScalarSubcoreMesh(axis_name="sequencer", num_cores=1),
        scratch_types=[pltpu.SemaphoreType.DMA((6 * n_arr,)), pltpu.SemaphoreType.DMA((6 * n_arr,)),
                       pltpu.SemaphoreType.DMA((n_arr,))],
        compiler_params=pltpu.CompilerParams(collective_id=collective_id),
    )(*shards, *after)


def _rs_pair_exchange(grads, name):
    n_arr = len(grads)

    def body(*refs):
        srcs, outs = refs[:n_arr], refs[n_arr:2 * n_arr]
        send_sems, recv_sems = refs[2 * n_arr:]
        x, y, c = _place()
        cps = []
        for k in range(n_arr):
            half = srcs[k].shape[1] // 2
            cps.append(pltpu.make_async_remote_copy(
                src_ref=srcs[k].at[:, pl.ds((1 - c) * half, half)], dst_ref=outs[k],
                send_sem=send_sems.at[k], recv_sem=recv_sems.at[k],
                device_id=(x, y, 1 - c), device_id_type=MESH))
            cps[-1].start()
        for cp in cps:
            cp.wait_recv()
        for cp in cps:
            cp.wait_send()

    return pl.pallas_call(
        body, name=name,
        out_shape=[jax.ShapeDtypeStruct((g.shape[0], g.shape[1] // 2, g.shape[2]), g.dtype) for g in grads],
        in_specs=[ANY] * n_arr, out_specs=[ANY] * n_arr,
        scratch_shapes=[pltpu.SemaphoreType.DMA((n_arr,)), pltpu.SemaphoreType.DMA((n_arr,))],
        compiler_params=_params(),
    )(*grads)


def _rs_chip_exchange(sums, name, collective_id):
    n_arr = len(sums)

    def body(*refs):
        srcs, outs = refs[:n_arr], refs[n_arr:2 * n_arr]
        send_sems, recv_sems = refs[2 * n_arr:]
        x, y, c = _place()
        chips = [(1 - x, y), (x, 1 - y), (1 - x, 1 - y)]
        _handshake([(*chip, c) for chip in chips])
        cps = []
        for k in range(n_arr):
            for j, chip in enumerate(chips):
                cps.append(pltpu.make_async_remote_copy(
                    src_ref=srcs[k].at[2 * chip[0] + chip[1]], dst_ref=outs[k].at[j],
                    send_sem=send_sems.at[3 * k + j], recv_sem=recv_sems.at[3 * k + j],
                    device_id=(*chip, c), device_id_type=MESH))
                cps[-1].start()
        for cp in cps:
            cp.wait_recv()
        for cp in cps:
            cp.wait_send()

    return pl.kernel(
        body, name=name,
        out_type=[jax.ShapeDtypeStruct((3,) + s.shape[1:], s.dtype) for s in sums],
        mesh=plsc.ScalarSubcoreMesh(axis_name="sequencer", num_cores=1),
        scratch_types=[pltpu.SemaphoreType.DMA((3 * n_arr,)), pltpu.SemaphoreType.DMA((3 * n_arr,))],
        compiler_params=pltpu.CompilerParams(collective_id=collective_id),
    )(*sums)


def _rs_share(totals, name):
    n_arr = len(totals)

    def body(*refs):
        outs = refs[n_arr:2 * n_arr]
        send_sems, recv_sems = refs[2 * n_arr:]
        x, y, c = _place()

        def half_rows(k, sel):
            return outs[k].at[sel]

        cps = []
        for k in range(n_arr):
            cps.append(pltpu.make_async_remote_copy(
                src_ref=half_rows(k, c), dst_ref=half_rows(k, c), send_sem=send_sems.at[k], recv_sem=recv_sems.at[k],
                device_id=(x, y, 1 - c), device_id_type=MESH))
            cps[-1].start()
        for k in range(n_arr):
            pltpu.make_async_remote_copy(
                src_ref=half_rows(k, c), dst_ref=half_rows(k, 1 - c), send_sem=send_sems.at[k],
                recv_sem=recv_sems.at[k], device_id=(x, y, 1 - c), device_id_type=MESH).wait_recv()
        for cp in cps:
            cp.wait_send()

    shared = pl.pallas_call(
        body, name=name,
        out_shape=[jax.ShapeDtypeStruct(t.shape, t.dtype) for t in totals],
        in_specs=[ANY] * n_arr, out_specs=[ANY] * n_arr,
        input_output_aliases={k: k for k in range(n_arr)},
        scratch_shapes=[pltpu.SemaphoreType.DMA((n_arr,)), pltpu.SemaphoreType.DMA((n_arr,))],
        compiler_params=_params(),
    )(*totals)
    return [t.reshape(2 * t.shape[1], t.shape[2]) for t in shared]


def _pair_add(grad, recv, c_idx, name):
    n, r, cols = grad.shape
    half = r // 2
    rows = half // 2

    def body(_, g_ref, r_ref, o_ref):
        o_ref[...] = (g_ref[...].astype(F32) + r_ref[...].astype(F32)).astype(o_ref.dtype)

    return pl.pallas_call(
        body, name=name,
        grid_spec=pltpu.PrefetchScalarGridSpec(
            num_scalar_prefetch=1, grid=(n, 2),
            in_specs=[pl.BlockSpec((None, None, rows, cols), lambda s, i, ci: (s, ci[0], i, 0)),
                      pl.BlockSpec((None, rows, cols), lambda s, i, ci: (s, i, 0))],
            out_specs=pl.BlockSpec((None, rows, cols), lambda s, i, ci: (s, i, 0))),
        out_shape=jax.ShapeDtypeStruct((n, half, cols), BF16),
        compiler_params=_params(2),
    )(c_idx, grad.reshape(n, 2, half, cols), recv)


def _chip_add(sums, recv, chip_and_core, name, after=()):
    _, half, cols = sums.shape
    rows = half // 2

    def body(_, s_ref, r0_ref, r1_ref, r2_ref, o_ref):
        o_ref[...] = ((s_ref[...].astype(F32) + r0_ref[...].astype(F32))
                      + r1_ref[...].astype(F32)) + r2_ref[...].astype(F32)

    def recv_spec(j):
        return pl.BlockSpec((None, rows, cols), lambda i, ci: (j, i, 0))

    return pl.pallas_call(
        _ordered(body, 5, after), name=name,
        grid_spec=pltpu.PrefetchScalarGridSpec(
            num_scalar_prefetch=1, grid=(2,),
            in_specs=[pl.BlockSpec((None, rows, cols), lambda i, ci: (ci[0], i, 0)),
                      recv_spec(0), recv_spec(1), recv_spec(2)] + [ANY] * len(after),
            out_specs=pl.BlockSpec((None, rows, cols), lambda i, ci: (ci[1], i, 0))),
        out_shape=jax.ShapeDtypeStruct((2, half, cols), F32),
        compiler_params=_params(1),
    )(chip_and_core, sums, recv, recv, recv, *after)


def _rms(x):
    return lax.rsqrt(jnp.mean(x * x, axis=-1, keepdims=True) + NORM_EPS)


def _norm_mod(x, mod, name, prev=None):
    s_len, d = x.shape
    tm = TOKEN_TILE

    def body(*refs):
        if prev is None:
            x_ref, mod_ref, h_ref, ht_ref = refs
            xv = x_ref[...]
        else:
            x_ref, y_ref, modp_ref, mod_ref, xo_ref, h_ref, ht_ref = refs
            xv = x_ref[...] + prev[2] * modp_ref[2:3, :] * y_ref[...]
            xo_ref[...] = xv
        n = (xv * _rms(xv)) * mod_ref[3:4, :]
        h = n * (1.0 + mod_ref[1:2, :]) + mod_ref[0:1, :]
        h_ref[...] = h.astype(BF16)
        ht_ref[...] = h.T.astype(BF16)

    tile = pl.BlockSpec((tm, d), lambda i: (i, 0))
    small = pl.BlockSpec((8, d), lambda i: (0, 0))
    h_specs = [tile, pl.BlockSpec((d, tm), lambda i: (0, i))]
    h_shapes = [jax.ShapeDtypeStruct((s_len, d), BF16), jax.ShapeDtypeStruct((d, s_len), BF16)]
    if prev is None:
        return pl.pallas_call(
            body, name=name, grid=(s_len // tm,), in_specs=[tile, small], out_specs=h_specs, out_shape=h_shapes,
            compiler_params=_params(1))(x, mod)
    return pl.pallas_call(
        body, name=name, grid=(s_len // tm,), in_specs=[tile, tile, small, small],
        out_specs=[tile] + h_specs, out_shape=[jax.ShapeDtypeStruct((s_len, d), F32)] + h_shapes,
        compiler_params=_params(1))(x, prev[0], prev[1], mod)


def _norm_bwd(dh, x, mod, dxo, y_raw, coef, name, after=(), prev=None):
    s_len, d = x.shape
    tm = TOKEN_TILE

    def body(*refs):
        if prev is None:
            dh_ref, x_ref, mod_ref, dxo_ref, y_ref, dx_ref, st_ref = refs
        else:
            dh_ref, x_ref, mod_ref, dxo_ref, y_ref, modp_ref, dx_ref, st_ref, dyp_ref = refs

        @pl.when(pl.program_id(0) == 0)
        def _():
            st_ref[...] = jnp.zeros_like(st_ref)

        xv, dhv, dxov = x_ref[...], dh_ref[...], dxo_ref[...]
        r = _rms(xv)
        xh = xv * r
        gain, scale = mod_ref[3:4, :], mod_ref[1:2, :]
        dn = dhv * (1.0 + scale)
        dxh = dn * gain
        dx = dxov + r * (dxh - xh * jnp.mean(dxh * xh, axis=-1, keepdims=True))
        dx_ref[...] = dx
        if prev is not None:
            dyp_ref[...] = (prev[1] * modp_ref[2:3, :] * dx).astype(BF16)
        st_ref[0:1, :] += jnp.sum(dhv, axis=0, keepdims=True)
        st_ref[1:2, :] += jnp.sum(dhv * (xh * gain), axis=0, keepdims=True)
        st_ref[2:3, :] += coef * jnp.sum(y_ref[...] * dxov, axis=0, keepdims=True)
        st_ref[3:4, :] += jnp.sum(dn * xh, axis=0, keepdims=True)

    tile = pl.BlockSpec((tm, d), lambda i: (i, 0))
    small = pl.BlockSpec((8, d), lambda i: (0, 0))
    operands = [dh, x, mod, dxo, y_raw] + ([] if prev is None else [prev[0]])
    in_specs = [tile, tile, small, tile, tile] + ([] if prev is None else [small])
    out_specs = [tile, small] + ([] if prev is None else [tile])
    out_shape = [jax.ShapeDtypeStruct((s_len, d), F32), jax.ShapeDtypeStruct((8, d), F32)]
    if prev is not None:
        out_shape.append(jax.ShapeDtypeStruct((s_len, d), BF16))
    return pl.pallas_call(
        _ordered(body, len(operands), after), name=name, grid=(s_len // tm,),
        in_specs=in_specs + [ANY] * len(after), out_specs=out_specs, out_shape=out_shape,
        compiler_params=_params(1),
    )(*operands, *after)


def _loss_grad(x, y, mod, target, name):
    s_len, d = x.shape
    tm = TOKEN_TILE

    def body(x_ref, y_ref, mod_ref, t_ref, do_ref, dy_ref, part_ref):
        @pl.when(pl.program_id(0) == 0)
        def _():
            part_ref[...] = jnp.zeros_like(part_ref)

        half_gate = 0.5 * mod_ref[2:3, :]
        err = (x_ref[...] + half_gate * y_ref[...]) - t_ref[...]
        do = err * (1.0 / d)
        do_ref[...] = do
        dy_ref[...] = (half_gate * do).astype(BF16)
        sq = err * err
        part_ref[...] += jnp.sum(sq.reshape(tm // 8, 8, d), axis=0)

    tile = pl.BlockSpec((tm, d), lambda i: (i, 0))
    small = pl.BlockSpec((8, d), lambda i: (0, 0))
    return pl.pallas_call(
        body, name=name, grid=(s_len // tm,),
        in_specs=[tile, tile, small, tile],
        out_specs=[tile, tile, small],
        out_shape=[jax.ShapeDtypeStruct((s_len, d), F32), jax.ShapeDtypeStruct((s_len, d), BF16),
                   jax.ShapeDtypeStruct((8, d), F32)],
        compiler_params=_params(1),
    )(x, y, mod, target)


def _adamw_math(w, g, m, v):
    m = ADAM_B1 * m + (1.0 - ADAM_B1) * g
    v = ADAM_B2 * v + (1.0 - ADAM_B2) * (g * g)
    m_hat = m / (1.0 - ADAM_B1 ** ADAM_STEP)
    v_hat = v / (1.0 - ADAM_B2 ** ADAM_STEP)
    delta = -ADAM_LR * (m_hat / (jnp.sqrt(v_hat) + ADAM_EPS) + ADAM_WD * w)
    return delta, m, v


def _adamw(w, g, m, v, name, after=()):
    r, cols = w.shape
    tr = r // 8 if r % 64 == 0 else r

    def body(w_ref, g_ref, m_ref, v_ref, d_ref, nm_ref, nv_ref):
        d_ref[...], nm_ref[...], nv_ref[...] = _adamw_math(w_ref[...], g_ref[...], m_ref[...], v_ref[...])

    tile = pl.BlockSpec((tr, cols), lambda i: (i, 0))
    shape = jax.ShapeDtypeStruct((r, cols), F32)
    return pl.pallas_call(
        _ordered(body, 4, after), name=name, grid=(r // tr,),
        in_specs=[tile] * 4 + [ANY] * len(after), out_specs=[tile] * 3, out_shape=[shape] * 3,
        compiler_params=_params(1),
    )(w, g, m, v, *after)


def _in_parts(tm, n_qkv, n_rest):
    def part(lo, n_blk):
        return pl.BlockSpec((tm, IN_BLOCK), lambda i, j: (i, jnp.clip(j - lo, 0, n_blk - 1)))
    return [part(0, n_qkv), part(n_qkv, n_qkv), part(2 * n_qkv, n_qkv), part(3 * n_qkv, n_rest)]


def _pick_part(j, n_qkv, refs, fn):
    bounds = [0, n_qkv, 2 * n_qkv, 3 * n_qkv]
    for p, ref in enumerate(refs):
        inside = j >= bounds[p]
        if p + 1 < len(refs):
            inside = inside & (j < bounds[p + 1])
        pl.when(inside)(lambda ref=ref: fn(ref))


def _in_proj(h, w, q_norm, k_norm, name):
    s_len, d = h.shape
    tm = PROJ_TILE
    assert tm == SLAB and IN_BLOCK == HEADS * HEAD_DIM
    steps = w.shape[1] // IN_BLOCK
    n_qkv = 3 * QKV // IN_BLOCK

    def body(h_ref, w_ref, qn_ref, kn_ref, qkv_ref, rest_ref, hat_ref, tmp_ref):
        j = pl.program_id(1)
        res = _dot(h_ref[...], w_ref[...])

        def emit(sect, gi):
            dil = DILATIONS[gi]
            run = SLAB // dil
            for hh in range(HEADS):
                cols = slice(hh * HEAD_DIM, (hh + 1) * HEAD_DIM)
                x = res[:, cols]
                if sect < 2:
                    x = (x * _rms(x)) * (qn_ref if sect == 0 else kn_ref)[...]
                tmp_ref[...] = x
                for r in range(dil):
                    hat_ref[r * run:(r + 1) * run, cols] = tmp_ref[_rows(r, run, dil), :].astype(BF16)

        @pl.when(j < n_qkv)
        def _():
            qkv_ref[...] = res

        for sect in range(3):
            for gi in range(N_GROUPS):
                pl.when(j == sect * N_GROUPS + gi)(lambda sect=sect, gi=gi: emit(sect, gi))

        @pl.when(j >= n_qkv)
        def _():
            rest_ref[...] = res

    qkv_blk = pl.BlockSpec((tm, IN_BLOCK), lambda i, j: (i, jnp.minimum(j, n_qkv - 1)))
    small = pl.BlockSpec((1, HEAD_DIM), lambda i, j: (0, 0))
    return pl.pallas_call(
        body, name=name, grid=(s_len // tm, steps),
        in_specs=[pl.BlockSpec((tm, d), lambda i, j: (i, 0)), pl.BlockSpec((d, IN_BLOCK), lambda i, j: (0, j)),
                  small, small],
        out_specs=[qkv_blk, pl.BlockSpec((tm, IN_BLOCK), lambda i, j: (i, jnp.maximum(j - n_qkv, 0))), qkv_blk],
        out_shape=[jax.ShapeDtypeStruct((s_len, 3 * QKV), F32),
                   jax.ShapeDtypeStruct((s_len, w.shape[1] - 3 * QKV), F32),
                   jax.ShapeDtypeStruct((s_len, 3 * QKV), BF16)],
        scratch_shapes=[pltpu.VMEM((tm, HEAD_DIM), F32)],
        compiler_params=_params(2),
    )(h, w, q_norm, k_norm)


def _in_proj_bwd(dq, dk, dv, drest, w, name, after=()):
    s_len = dq.shape[0]
    d = w.shape[0]
    tm = PROJ_TILE
    steps = w.shape[1] // IN_BLOCK
    n_qkv = QKV // IN_BLOCK

    def body(dq_ref, dk_ref, dv_ref, dr_ref, w_ref, o_ref, acc_ref):
        j = pl.program_id(1)

        @pl.when(j == 0)
        def _():
            acc_ref[...] = jnp.zeros_like(acc_ref)

        def add(a_ref):
            acc_ref[...] += _dot_nt(a_ref[...], w_ref[...])

        _pick_part(j, n_qkv, [dq_ref, dk_ref, dv_ref, dr_ref], add)

        @pl.when(j == steps - 1)
        def _():
            o_ref[...] = acc_ref[...]

    return pl.pallas_call(
        _ordered(body, 5, after), name=name, grid=(s_len // tm, steps),
        in_specs=(_in_parts(tm, n_qkv, steps - 3 * n_qkv) + [pl.BlockSpec((d, IN_BLOCK), lambda i, j: (0, j))]
                  + [ANY] * len(after)),
        out_specs=pl.BlockSpec((tm, d), lambda i, j: (i, 0)),
        out_shape=jax.ShapeDtypeStruct((s_len, d), F32),
        scratch_shapes=[pltpu.VMEM((tm, d), F32)],
        compiler_params=_params(2),
    )(dq, dk, dv, drest, w, *after)


def _wgrad(x, y, x_spec, y_spec, out_shape, out_spec, acc_shape, n_chunks, name, x_transposed=False, after=()):
    s_len = y.shape[-2]
    ts = WGRAD_TILE
    steps = s_len // ts

    def body(x_ref, y_ref, o_ref, acc_ref):
        s = pl.program_id(1)

        @pl.when(s == 0)
        def _():
            acc_ref[...] = jnp.zeros_like(acc_ref)

        acc_ref[...] += (_dot if x_transposed else _dot_tn)(x_ref[...], y_ref[...])

        @pl.when(s == steps - 1)
        def _():
            o_ref[...] = acc_ref[...].astype(o_ref.dtype)

    return pl.pallas_call(
        _ordered(body, 2, after), name=name, grid=(n_chunks, steps),
        in_specs=[x_spec(ts), y_spec(ts)] + [ANY] * len(after), out_specs=out_spec,
        out_shape=jax.ShapeDtypeStruct(out_shape, BF16),
        scratch_shapes=[pltpu.VMEM(acc_shape, F32)],
        compiler_params=_params(2),
    )(x, y, *after)


def _pieces(width, piece=256):
    return [slice(a, min(a + piece, width)) for a in range(0, width, piece)]


def _ffn_fwd(h, w_gate, w_up, w_down, name):
    s_len, d = h.shape
    n_chunks, _, fs = w_gate.shape
    tm = FFN_TILE

    def body(h_ref, wg_ref, wu_ref, wd_ref, g_ref, u_ref, y_ref):
        j = pl.program_id(1)
        hv = h_ref[...]
        total = None
        for cols in _pieces(fs):
            g = _dot(hv, wg_ref[:, cols])
            u = _dot(hv, wu_ref[:, cols])
            g_ref[:, cols] = g.astype(BF16)
            u_ref[:, cols] = u.astype(BF16)
            act = (g * _sigmoid(g)) * u
            part = _dot(act.astype(BF16), wd_ref[cols, :])
            total = part if total is None else total + part

        @pl.when(j == 0)
        def _():
            y_ref[...] = total

        @pl.when(j > 0)
        def _():
            y_ref[...] += total

    tile = pl.BlockSpec((tm, d), lambda i, j: (i, 0))
    hid = pl.BlockSpec((None, tm, fs), lambda i, j: (j, i, 0))
    w_in_spec = pl.BlockSpec((None, d, fs), lambda i, j: (j, 0, 0))
    hid_shape = jax.ShapeDtypeStruct((n_chunks, s_len, fs), BF16)
    return pl.pallas_call(
        body, name=name, grid=(s_len // tm, n_chunks),
        in_specs=[tile, w_in_spec, w_in_spec, pl.BlockSpec((None, fs, d), lambda i, j: (j, 0, 0))],
        out_specs=[hid, hid, tile],
        out_shape=[hid_shape, hid_shape, jax.ShapeDtypeStruct((s_len, d), F32)],
        compiler_params=_params(2),
    )(h, w_gate, w_up, w_down)


def _ffn_bwd(dy, g_pre, u_pre, w_gate, w_up, w_down, name):
    s_len, d = dy.shape
    n_chunks, _, fs = w_gate.shape
    tm = FFN_TILE

    def body(dy_ref, g_ref, u_ref, wg_ref, wu_ref, wd_ref, dh_ref, dg_ref, du_ref, a_ref):
        j = pl.program_id(1)
        dyv = dy_ref[...]
        total = None
        for cols in _pieces(fs):
            da = _dot_nt(dyv, wd_ref[cols, :])
            g = g_ref[:, cols].astype(F32)
            u = u_ref[:, cols].astype(F32)
            sg = _sigmoid(g)
            silu = g * sg
            dg = (da * u * (sg * (1.0 + g * (1.0 - sg)))).astype(BF16)
            du = (da * silu).astype(BF16)
            dg_ref[:, cols] = dg
            du_ref[:, cols] = du
            a_ref[:, cols] = (silu * u).astype(BF16)
            part = _dot_nt(dg, wg_ref[:, cols]) + _dot_nt(du, wu_ref[:, cols])
            total = part if total is None else total + part

        @pl.when(j == 0)
        def _():
            dh_ref[...] = total

        @pl.when(j > 0)
        def _():
            dh_ref[...] += total

    tile = pl.BlockSpec((tm, d), lambda i, j: (i, 0))
    hid = pl.BlockSpec((None, tm, fs), lambda i, j: (j, i, 0))
    w_in_spec = pl.BlockSpec((None, d, fs), lambda i, j: (j, 0, 0))
    hid_shape = jax.ShapeDtypeStruct((n_chunks, s_len, fs), BF16)
    return pl.pallas_call(
        body, name=name, grid=(s_len // tm, n_chunks),
        in_specs=[tile, hid, hid, w_in_spec, w_in_spec, pl.BlockSpec((None, fs, d), lambda i, j: (j, 0, 0))],
        out_specs=[tile, hid, hid, hid],
        out_shape=[jax.ShapeDtypeStruct((s_len, d), F32), hid_shape, hid_shape, hid_shape],
        compiler_params=_params(2),
    )(dy, g_pre, u_pre, w_gate, w_up, w_down)


def _ffn_wgrads(ht, dg, du, act, dy, tag, after=()):
    n_chunks, s_len, fs = dg.shape
    d = ht.shape[0]
    tok = lambda ts: pl.BlockSpec((ts, d), lambda c, s: (s, 0))
    tok_t = lambda ts: pl.BlockSpec((d, ts), lambda c, s: (0, s))
    hid = lambda ts: pl.BlockSpec((None, ts, fs), lambda c, s: (c, s, 0))
    d_up = pl.BlockSpec((None, d, fs), lambda c, s: (c, 0, 0))
    d_down = pl.BlockSpec((None, fs, d), lambda c, s: (c, 0, 0))
    dwg = _wgrad(ht, dg, tok_t, hid, (n_chunks, d, fs), d_up, (d, fs), n_chunks, tag + "_dwg", True, after)
    dwu = _wgrad(ht, du, tok_t, hid, (n_chunks, d, fs), d_up, (d, fs), n_chunks, tag + "_dwu", True, after)
    dwd = _wgrad(act, dy, hid, tok, (n_chunks, fs, d), d_down, (fs, d), n_chunks, tag + "_dwd", False, after)
    return dwg, dwu, dwd


def _band_bias():
    qi = lax.broadcasted_iota(jnp.int32, (ATTN_BLOCK, 2 * ATTN_BLOCK), 0)
    kj = lax.broadcasted_iota(jnp.int32, (ATTN_BLOCK, 2 * ATTN_BLOCK), 1)
    band = (kj >= qi) & (kj <= qi + ATTN_BLOCK)
    return jnp.where(band, 0.0, NEG), jnp.where(band & (kj >= ATTN_BLOCK), 0.0, NEG)


def _rows(base, count, stride):
    return pl.ds(base, count) if stride == 1 else pl.ds(base, count, stride=stride)


def _qkv_specs(slab_of, sections):
    def spec(sect, back):
        return pl.BlockSpec((SLAB, HEAD_DIM),
                            lambda h, s, g: (jnp.maximum(slab_of(s) - back, 0), (sect * N_GROUPS + g) * HEADS + h))
    return [spec(sect, back) for sect, back in sections]


HAT_BLOCKS = [(0, 0), (1, 0), (2, 0), (1, 1), (2, 1)]


def _stage_keys(k_ref, v_ref, kp_ref, vp_ref, kbuf, vbuf, dil, n):
    run = SLAB // dil
    for r in range(dil):
        own, before = pl.ds(r * run, run), pl.ds(2 * r * run, run)
        kbuf[pl.ds((2 * r + 1) * run, run), :] = k_ref[own, :]
        vbuf[pl.ds((2 * r + 1) * run, run), :] = v_ref[own, :]

        @pl.when(n > 0)
        def _():
            kbuf[before, :] = kp_ref[own, :]
            vbuf[before, :] = vp_ref[own, :]

        @pl.when(n == 0)
        def _():
            kbuf[before, :] = jnp.zeros((run, HEAD_DIM), BF16)
            vbuf[before, :] = jnp.zeros((run, HEAD_DIM), BF16)


def _for_each_tile(dil, n, tile_fn):
    run = SLAB // dil
    bias, first_bias = _band_bias()
    for jj in range(run // ATTN_BLOCK):
        start = jj * ATTN_BLOCK
        tile_bias = jnp.where(n == 0, first_bias, bias) if jj == 0 else bias
        for r in range(dil):
            tile_fn(pl.ds(r * run + start, ATTN_BLOCK),
                    pl.ds((2 * r + 1) * run - ATTN_BLOCK + start, 2 * ATTN_BLOCK),
                    _rows(start * dil + r, ATTN_BLOCK, dil), tile_bias)


def _attn_fwd(hat, name):
    s_len = hat.shape[0]
    e = HEAD_DIM
    n_slabs = s_len // SLAB

    def body(q_ref, k_ref, v_ref, kp_ref, vp_ref, o_ref, lse_ref, kbuf, vbuf, ml_s, acc_s):
        n, grp = pl.program_id(1), pl.program_id(2)
        lane = lax.broadcasted_iota(jnp.int32, (ATTN_BLOCK, e), 1)

        def run(gi, dil):
            _stage_keys(k_ref, v_ref, kp_ref, vp_ref, kbuf, vbuf, dil, n)

            def tile(q_rows, kv_rows, token_rows, bias):
                s = _dot_nt(q_ref[q_rows, :], kbuf[kv_rows, :]) * ATTN_SCALE + bias
                m = jnp.max(s, axis=-1, keepdims=True)
                p = jnp.exp(s - m)
                ml_s.at[gi][token_rows, :] = jnp.where(lane < e // 2, m, jnp.sum(p, axis=-1, keepdims=True))
                acc_s.at[gi][token_rows, :] = _dot(p.astype(BF16), vbuf[kv_rows, :])

            _for_each_tile(dil, n, tile)

        for gi, dil in enumerate(DILATIONS):
            pl.when(grp == gi)(lambda gi=gi, dil=dil: run(gi, dil))

        @pl.when(grp == N_GROUPS - 1)
        def _():
            ms = [jnp.broadcast_to(ml_s[gi][:, 0:1], (SLAB, e)) for gi in range(N_GROUPS)]
            m_all = jnp.maximum(jnp.maximum(ms[0], ms[1]), ms[2])
            den = jnp.zeros((SLAB, e), F32)
            num = jnp.zeros((SLAB, e), F32)
            for gi in range(N_GROUPS):
                w = jnp.exp(ms[gi] - m_all)
                den += jnp.broadcast_to(ml_s[gi][:, e // 2:e // 2 + 1], (SLAB, e)) * w
                num += acc_s[gi] * w
            o_ref[...] = num / den
            lse_ref[...] = m_all + jnp.log(den)

    out = pl.BlockSpec((SLAB, e), lambda h, n, g: (n, h))
    return pl.pallas_call(
        body, name=name, grid=(HEADS, n_slabs, N_GROUPS),
        in_specs=_qkv_specs(lambda n: n, HAT_BLOCKS),
        out_specs=[out, out],
        out_shape=[jax.ShapeDtypeStruct((s_len, HEADS * e), F32)] * 2,
        scratch_shapes=[pltpu.VMEM((2 * SLAB, e), BF16), pltpu.VMEM((2 * SLAB, e), BF16),
                        pltpu.VMEM((N_GROUPS, SLAB, e), F32), pltpu.VMEM((N_GROUPS, SLAB, e), F32)],
        compiler_params=_params(3),
    )(hat, hat, hat, hat, hat)


def _attn_bwd(qkv, hat, d_out, out, lse, q_norm, k_norm, name):
    s_len = qkv.shape[0]
    e = HEAD_DIM
    n_slabs = s_len // SLAB

    def body(q_ref, k_ref, v_ref, kp_ref, vp_ref, qraw_ref, kraw_ref, do_ref, o_ref, lse_ref, qn_ref, kn_ref,
             dq_ref, dk_ref, dv_ref, st_ref, kbuf, vbuf, stat_s, dqs, dkb, dvb, dk_tok, dv_tok, carry):
        head, step, grp = pl.program_id(0), pl.program_id(1), pl.program_id(2)
        n = n_slabs - 1 - step
        dkb[...] = jnp.zeros_like(dkb)
        dvb[...] = jnp.zeros_like(dvb)
        lane = lax.broadcasted_iota(jnp.int32, (SLAB, e), 1)
        do_all = do_ref[...]
        stat_s[...] = jnp.where(lane < e // 2, lse_ref[...], jnp.sum(do_all * o_ref[...], axis=-1, keepdims=True))

        @pl.when((head == 0) & (step == 0) & (grp == 0))
        def _():
            st_ref[...] = jnp.zeros_like(st_ref)

        def run(gi, dil):
            seg = SLAB // dil
            _stage_keys(k_ref, v_ref, kp_ref, vp_ref, kbuf, vbuf, dil, n)

            @pl.when(step == 0)
            def _():
                carry[gi] = jnp.zeros((2, SLAB, e), F32)

            def tile(q_rows, kv_rows, token_rows, bias):
                q = q_ref[q_rows, :]
                k = kbuf[kv_rows, :]
                v = vbuf[kv_rows, :]
                stat = stat_s[token_rows, :]
                do16 = do_ref[token_rows, :].astype(BF16)
                s = _dot_nt(q, k) * ATTN_SCALE + bias
                p = jnp.exp(s - stat[:, 0:1])
                ds = (p * (_dot_nt(do16, v) - stat[:, e // 2:e // 2 + 1]) * ATTN_SCALE).astype(BF16)
                dqs[token_rows, :] = _dot(ds, k)
                dkb[kv_rows, :] += _dot_tn(ds, q)
                dvb[kv_rows, :] += _dot_tn(p.astype(BF16), do16)

            _for_each_tile(dil, n, tile)
            for r in range(dil):
                own, before = pl.ds((2 * r + 1) * seg, seg), pl.ds(2 * r * seg, seg)
                kept = pl.ds(r * seg, seg)
                dk_tok[_rows(r, seg, dil), :] = dkb[own, :] + carry.at[gi, 0][kept, :]
                dv_tok[_rows(r, seg, dil), :] = dvb[own, :] + carry.at[gi, 1][kept, :]
                carry.at[gi, 0][kept, :] = dkb[before, :]
                carry.at[gi, 1][kept, :] = dvb[before, :]

            def norm_bwd(raw, gain, d_hat):
                r = _rms(raw)
                y = raw * r
                dy = d_hat * gain
                return r * (dy - y * jnp.mean(dy * y, axis=-1, keepdims=True)), jnp.sum(d_hat * y, axis=0, keepdims=True)

            dq, dqn = norm_bwd(qraw_ref[...], qn_ref[...], dqs[...])
            dk, dkn = norm_bwd(kraw_ref[...], kn_ref[...], dk_tok[...])
            dq_ref[...] = dq.astype(BF16)
            dk_ref[...] = dk.astype(BF16)
            dv_ref[...] = dv_tok[...].astype(BF16)
            st_ref[0:1, :] += dqn
            st_ref[1:2, :] += dkn

        for gi, dil in enumerate(DILATIONS):
            pl.when(grp == gi)(lambda gi=gi, dil=dil: run(gi, dil))

    slab_of = lambda s: n_slabs - 1 - s
    small = pl.BlockSpec((1, e), lambda h, s, g: (0, 0))
    head_blk = pl.BlockSpec((SLAB, e), lambda h, s, g: (slab_of(s), h))
    grad_blk = pl.BlockSpec((SLAB, e), lambda h, s, g: (slab_of(s), g * HEADS + h))
    grad_shape = jax.ShapeDtypeStruct((s_len, QKV), BF16)
    return pl.pallas_call(
        body, name=name, grid=(HEADS, n_slabs, N_GROUPS),
        in_specs=(_qkv_specs(slab_of, HAT_BLOCKS) + _qkv_specs(slab_of, [(0, 0), (1, 0)])
                  + [head_blk, head_blk, head_blk, small, small]),
        out_specs=[grad_blk, grad_blk, grad_blk, pl.BlockSpec((8, e), lambda h, s, g: (0, 0))],
        out_shape=[grad_shape, grad_shape, grad_shape, jax.ShapeDtypeStruct((8, e), F32)],
        scratch_shapes=[pltpu.VMEM((2 * SLAB, e), BF16), pltpu.VMEM((2 * SLAB, e), BF16), pltpu.VMEM((SLAB, e), F32),
                        pltpu.VMEM((SLAB, e), F32), pltpu.VMEM((2 * SLAB, e), F32), pltpu.VMEM((2 * SLAB, e), F32),
                        pltpu.VMEM((SLAB, e), F32), pltpu.VMEM((SLAB, e), F32),
                        pltpu.VMEM((N_GROUPS, 2, SLAB, e), F32)],
        compiler_params=_params(3),
    )(hat, hat, hat, hat, hat, qkv, qkv, d_out, out, lse, q_norm, k_norm)


def _shift_rows(x, by, edge, forward):
    t_len = x.shape[0]
    row = lax.broadcasted_iota(jnp.int32, x.shape, 0)
    if forward:
        out = pltpu.roll(x, by, 0)
        for i in range(by):
            out = jnp.where(row == i, edge[8 - by + i:8 - by + i + 1, :], out)
    else:
        out = pltpu.roll(x, t_len - by, 0)
        for i in range(by):
            out = jnp.where(row == t_len - by + i, edge[i:i + 1, :], out)
    return out


def _mix_fwd(x, o, rest, mod, conv_w, w_attn, w_conv, w_out, name):
    s_len, d = x.shape
    tm = MIX_TILE
    a_w = o.shape[1]

    def body(x_ref, o_ref, u_ref, b_ref, c_ref, ga_ref, gc_ref, mod_ref, cw_ref, wa_ref, wc_ref, wo_ref,
             xo_ref, z_ref, ya_ref, yc_ref, conv_ref, yb_ref, m_ref, o16_ref, carry):
        @pl.when(pl.program_id(0) == 0)
        def _():
            carry[...] = jnp.zeros_like(carry)

        xc = c_ref[...] * u_ref[...]
        edge = carry[...]
        conv = (_shift_rows(xc, 2, edge, True) * cw_ref[0:1, :] + _shift_rows(xc, 1, edge, True) * cw_ref[1:2, :]
                + xc * cw_ref[2:3, :])
        carry[...] = xc[tm - 8:tm, :]
        yb = (b_ref[...] * conv).astype(BF16)
        o16 = o_ref[...].astype(BF16)
        ya = _dot(o16, wa_ref[...])
        yc = _dot(yb, wc_ref[...])
        merged = (_sigmoid(ga_ref[...]) * ya + _sigmoid(gc_ref[...]) * yc).astype(BF16)
        z = _dot(merged, wo_ref[...])
        xo_ref[...] = x_ref[...] + mod_ref[2:3, :] * z
        z_ref[...] = z
        ya_ref[...] = ya.astype(BF16)
        yc_ref[...] = yc.astype(BF16)
        conv_ref[...] = conv.astype(BF16)
        yb_ref[...] = yb
        m_ref[...] = merged
        o16_ref[...] = o16

    tile = pl.BlockSpec((tm, d), lambda i: (i, 0))
    sect = lambda k: pl.BlockSpec((tm, d), lambda i: (i, k))
    att = pl.BlockSpec((tm, a_w), lambda i: (i, 0))
    const = lambda shape: pl.BlockSpec(shape, lambda i: (0, 0))
    f32_out = jax.ShapeDtypeStruct((s_len, d), F32)
    b16_out = jax.ShapeDtypeStruct((s_len, d), BF16)
    return pl.pallas_call(
        body, name=name, grid=(s_len // tm,),
        in_specs=[tile, att, sect(0), sect(1), sect(2), sect(3), sect(4), const((8, d)), const((8, d)),
                  const((a_w, d)), const((d, d)), const((d, d))],
        out_specs=[tile] * 7 + [att],
        out_shape=[f32_out, f32_out] + [b16_out] * 5 + [jax.ShapeDtypeStruct((s_len, a_w), BF16)],
        scratch_shapes=[pltpu.VMEM((8, d), F32)],
        compiler_params=_params(1),
    )(x, o, rest, rest, rest, rest, rest, mod, conv_w, w_attn, w_conv, w_out)


def _mix_bwd(dxo, ya, yc, conv, rest, mod, conv_w, w_attn, w_conv, w_out, a_w, name):
    s_len, d = dxo.shape
    tm = MIX_TILE
    n_tiles = s_len // tm

    def body(dxo_ref, ya_ref, yc_ref, conv_ref, u_ref, b_ref, c_ref, ga_ref, gc_ref, mod_ref, cw_ref,
             wa_ref, wc_ref, wo_ref, do_ref, drest_ref, dz_ref, dya_ref, dyc_ref, st_ref, carry):
        @pl.when(pl.program_id(0) == 0)
        def _():
            carry[...] = jnp.zeros_like(carry)
            st_ref[...] = jnp.zeros_like(st_ref)

        dz = (mod_ref[2:3, :] * dxo_ref[...]).astype(BF16)
        dz_ref[...] = dz
        dm = _dot_nt(dz, wo_ref[...])
        sa, sc = _sigmoid(ga_ref[...]), _sigmoid(gc_ref[...])
        dya = (dm * sa).astype(BF16)
        dyc = (dm * sc).astype(BF16)
        dya_ref[...] = dya
        dyc_ref[...] = dyc
        drest_ref[:, 3 * d:4 * d] = (dm * ya_ref[...].astype(F32) * (sa * (1.0 - sa))).astype(BF16)
        drest_ref[:, 4 * d:5 * d] = (dm * yc_ref[...].astype(F32) * (sc * (1.0 - sc))).astype(BF16)
        do_ref[...] = _dot_nt(dya, wa_ref[...])
        dyb = _dot_nt(dyc, wc_ref[...])
        drest_ref[:, d:2 * d] = (dyb * conv_ref[...].astype(F32)).astype(BF16)
        dconv = dyb * b_ref[...]
        edge = carry[...]
        sh1 = _shift_rows(dconv, 1, edge, False)
        sh2 = _shift_rows(dconv, 2, edge, False)
        carry[...] = dconv[0:8, :]
        dxc = dconv * cw_ref[2:3, :] + sh1 * cw_ref[1:2, :] + sh2 * cw_ref[0:1, :]
        u, c = u_ref[...], c_ref[...]
        xc = c * u
        drest_ref[:, 0:d] = (dxc * c).astype(BF16)
        drest_ref[:, 2 * d:3 * d] = (dxc * u).astype(BF16)
        st_ref[0:1, :] += jnp.sum(xc * sh2, axis=0, keepdims=True)
        st_ref[1:2, :] += jnp.sum(xc * sh1, axis=0, keepdims=True)
        st_ref[2:3, :] += jnp.sum(xc * dconv, axis=0, keepdims=True)

    rev = lambda i: n_tiles - 1 - i
    tile = pl.BlockSpec((tm, d), lambda i: (rev(i), 0))
    sect = lambda k: pl.BlockSpec((tm, d), lambda i: (rev(i), k))
    const = lambda shape: pl.BlockSpec(shape, lambda i: (0, 0))
    b16_out = jax.ShapeDtypeStruct((s_len, d), BF16)
    return pl.pallas_call(
        body, name=name, grid=(n_tiles,),
        in_specs=[tile, tile, tile, tile, sect(0), sect(1), sect(2), sect(3), sect(4), const((8, d)), const((8, d)),
                  const((a_w, d)), const((d, d)), const((d, d))],
        out_specs=[pl.BlockSpec((tm, a_w), lambda i: (rev(i), 0)), pl.BlockSpec((tm, 5 * d), lambda i: (rev(i), 0)),
                   tile, tile, tile, const((8, d))],
        out_shape=[jax.ShapeDtypeStruct((s_len, a_w), F32), jax.ShapeDtypeStruct((s_len, 5 * d), BF16),
                   b16_out, b16_out, b16_out, jax.ShapeDtypeStruct((8, d), F32)],
        scratch_shapes=[pltpu.VMEM((8, d), F32)],
        compiler_params=_params(1),
    )(dxo, ya, yc, conv, rest, rest, rest, rest, rest, mod, conv_w, w_attn, w_conv, w_out)


ADA_COLS = 128


def _ada_fwd(c_all, w_shard, b_shard, name):
    d, cols = w_shard.shape

    def body(c_ref, w_ref, b_ref, o_ref):
        cv = c_ref[...]
        o_ref[...] = jnp.dot(cv * _sigmoid(cv), w_ref[...], preferred_element_type=F32,
                             precision=lax.Precision.HIGHEST) + b_ref[...]

    return pl.pallas_call(
        body, name=name, grid=(cols // ADA_COLS,),
        in_specs=[pl.BlockSpec((8, d), lambda j: (0, 0)), pl.BlockSpec((d, ADA_COLS), lambda j: (0, j)),
                  pl.BlockSpec((1, ADA_COLS), lambda j: (0, j))],
        out_specs=pl.BlockSpec((8, ADA_COLS), lambda j: (0, j)),
        out_shape=jax.ShapeDtypeStruct((8, cols), F32),
        compiler_params=_params(1),
    )(c_all, w_shard, b_shard)


def _ada_bwd(c_all, dmod_shard, w, m, v, name):
    d, cols = w.shape

    def body(c_ref, dm_ref, w_ref, m_ref, v_ref, g_ref, d_ref, nm_ref, nv_ref):
        cv = c_ref[...]
        g = lax.dot_general(cv * _sigmoid(cv), dm_ref[...], (((0,), (0,)), ((), ())),
                            preferred_element_type=F32, precision=lax.Precision.HIGHEST)
        g_ref[...] = g
        d_ref[...], nm_ref[...], nv_ref[...] = _adamw_math(w_ref[...], g, m_ref[...], v_ref[...])

    blk = pl.BlockSpec((d, ADA_COLS), lambda j: (0, j))
    shape = jax.ShapeDtypeStruct((d, cols), F32)
    return pl.pallas_call(
        body, name=name, grid=(cols // ADA_COLS,),
        in_specs=[pl.BlockSpec((8, d), lambda j: (0, 0)), pl.BlockSpec((8, ADA_COLS), lambda j: (0, j)), blk, blk, blk],
        out_specs=[blk] * 4, out_shape=[shape] * 4,
        compiler_params=_params(1),
    )(c_all, dmod_shard, w, m, v)


def _small_update(parts, w, m, v, name):
    n = w.shape[1]

    def body(p_ref, w_ref, m_ref, v_ref, g_ref, d_ref, nm_ref, nv_ref):
        g = p_ref[0:1, :]
        for i in range(1, 8):
            g = g + p_ref[i:i + 1, :]
        g_ref[...] = g
        d_ref[...], nm_ref[...], nv_ref[...] = _adamw_math(w_ref[...], g, m_ref[...], v_ref[...])

    shape = jax.ShapeDtypeStruct((1, n), F32)
    return pl.pallas_call(body, name=name, out_shape=[shape] * 4, compiler_params=_params())(parts, w, m, v)


def _cols_to_shards(w, n):
    r, nc = w.shape
    return w.reshape(r, n, nc // n).transpose(1, 0, 2)


def kernel(x, c, w_ada, b_ada, norm_ffn1, ffn1_w_gate, ffn1_w_up, ffn1_w_down, norm_mix, w_in, q_norm, k_norm, conv_w, w_attn_branch, w_conv_branch, w_out, norm_ffn2, ffn2_w_gate, ffn2_w_up, ffn2_w_down, loss_target, m_w_ada, m_b_ada, m_norm_ffn1, m_ffn1_w_gate, m_ffn1_w_up, m_ffn1_w_down, m_norm_mix, m_w_in, m_q_norm, m_k_norm, m_conv_w, m_w_attn_branch, m_w_conv_branch, m_w_out, m_norm_ffn2, m_ffn2_w_gate, m_ffn2_w_up, m_ffn2_w_down, v_w_ada, v_b_ada, v_norm_ffn1, v_ffn1_w_gate, v_ffn1_w_up, v_ffn1_w_down, v_norm_mix, v_w_in, v_q_norm, v_k_norm, v_conv_w, v_w_attn_branch, v_w_conv_branch, v_w_out, v_norm_ffn2, v_ffn2_w_gate, v_ffn2_w_up, v_ffn2_w_down):
    ix, iy, ic = _place()
    chip = 2 * ix + iy
    me = 4 * ix + 2 * iy + ic
    xs = x[0]
    target = loss_target[0]
    s_len, d = xs.shape
    ada_cols = w_ada.shape[2]
    conv_cols = conv_w.shape[2]

    conv_rows = jnp.zeros((8, conv_cols), F32).at[0:3].set(conv_w[0])
    small_in = jnp.concatenate([jnp.broadcast_to(c, (8, d)), conv_rows], axis=1)
    small_all = _allgather8(small_in, "gather_c").reshape(8, 8, d + conv_cols)
    c_all = small_all[:, 0, :d]
    conv_full = small_all[0::2, 0:3, d:].transpose(1, 0, 2).reshape(3, N_CHIPS * conv_cols)
    conv_pad = jnp.zeros((8, N_CHIPS * conv_cols), F32).at[0:3].set(conv_full)
    b_shard = lax.dynamic_slice(b_ada, (0, chip * ada_cols), (1, ada_cols))
    mod_part = _ada_fwd(c_all, w_ada[0], b_shard, "ada_fwd")
    mod_all = _allgather8(mod_part, "gather_mod").reshape(N_CHIPS, 2, 8, ada_cols)[:, 0]
    mod_mine = lax.dynamic_slice(mod_all, (0, me, 0), (N_CHIPS, 1, ada_cols)).reshape(9, d)

    def mod_rows(i, gain):
        return jnp.zeros((8, d), F32).at[0:3].set(mod_mine[3 * i:3 * i + 3]).at[3:4].set(gain)

    mod1, mod2, mod3 = mod_rows(0, norm_ffn1), mod_rows(1, norm_mix), mod_rows(2, norm_ffn2)

    to16 = lambda w: w[0].astype(BF16)
    wg1, wu1, wd1 = _gather_weights([to16(ffn1_w_gate), to16(ffn1_w_up), to16(ffn1_w_down)], [False] * 3,
                                    "gather_ffn1", 1)
    h1, h1t = _norm_mod(xs, mod1, "norm1")
    (w_in_full,) = _gather_weights([to16(w_in)], [True], "gather_w_in", 2, after=(wd1, h1))

    g1, u1, y1 = _ffn_fwd(h1, wg1, wu1, wd1, "ffn1_fwd")
    x1, h2, h2t = _norm_mod(xs, mod2, "norm2", prev=(y1, mod1, 0.5))
    qkv, rest, qkv_hat = _in_proj(h2, w_in_full, q_norm, k_norm, "in_proj")
    w_ab, w_cb_g, w_o_g, wg2, wu2, wd2 = _gather_weights(
        [to16(w_attn_branch), to16(w_conv_branch), to16(w_out),
         to16(ffn2_w_gate), to16(ffn2_w_up), to16(ffn2_w_down)], [True] + [False] * 5,
        "gather_rest", 3, after=(qkv,))
    a_w = w_ab.shape[0]
    w_cb = w_cb_g.reshape(d, d)
    w_o = w_o_g.reshape(d, d)
    o, lse = _attn_fwd(qkv_hat, "attn_fwd")
    x2, z, ya, yc, conv, yb, merged, o16 = _mix_fwd(x1, o, rest, mod2, conv_pad, w_ab, w_cb, w_o, "mix_fwd")
    h3, h3t = _norm_mod(x2, mod3, "norm3")
    g3, u3, y3 = _ffn_fwd(h3, wg2, wu2, wd2, "ffn2_fwd")
    dx3, dy3, loss_part = _loss_grad(x2, y3, mod3, target, "loss")
    loss = lax.psum(0.5 * jnp.sum(loss_part) / d, ("x", "y", "c"))

    c_idx = jnp.reshape(ic, (1,)).astype(jnp.int32)
    chip_idx = jnp.stack([chip, ic]).astype(jnp.int32)

    def reduce_start(grads, names, tag, collective_id):
        from_sibling = _rs_pair_exchange(grads, "rs_pair_" + tag)
        pair_sums = [_pair_add(g, r, c_idx, "pair_add_" + nm) for g, r, nm in zip(grads, from_sibling, names)]
        return pair_sums, _rs_chip_exchange(pair_sums, "rs_chips_" + tag, collective_id)

    def reduce_finish(pair_sums, from_chips, names, tag, after):
        totals = [_chip_add(p, r, chip_idx, "chip_add_" + nm, after)
                  for p, r, nm in zip(pair_sums, from_chips, names)]
        return dict(zip(names, _rs_share(totals, "rs_share_" + tag)))

    names_a = ["ffn2_w_gate", "ffn2_w_up", "ffn2_w_down"]
    names_b = ["w_in", "w_attn_branch", "w_conv_branch", "w_out"]
    names_c = ["ffn1_w_gate", "ffn1_w_up", "ffn1_w_down"]

    dh3, dg3, du3, a3 = _ffn_bwd(dy3, g3, u3, wg2, wu2, wd2, "ffn2_bwd")
    sums_a, chips_a = reduce_start(list(_ffn_wgrads(h3t, dg3, du3, a3, dy3, "ffn2")), names_a, "a", 4)
    dx2, st3 = _norm_bwd(dh3, x2, mod3, dx3, y3, 0.5, "norm3_bwd", after=tuple(sums_a))

    do, drest, dz, dya, dyc, st_conv = _mix_bwd(dx2, ya, yc, conv, rest, mod2, conv_pad, w_ab, w_cb, w_o, a_w, "mix_bwd")
    dq, dk, dv, st_qk = _attn_bwd(qkv, qkv_hat, do, o, lse, q_norm, k_norm, "attn_bwd")
    tok = lambda width: (lambda ts: pl.BlockSpec((ts, width), lambda cc, s: (s, 0)))
    colblk = lambda width: (lambda ts: pl.BlockSpec((ts, width), lambda cc, s: (s, cc)))
    tok_t = lambda ts: pl.BlockSpec((d, ts), lambda cc, s: (0, s))
    whole = pl.BlockSpec((d, QKV), lambda cc, s: (0, 0))
    dw_in = [_wgrad(h2t, part, tok_t, tok(QKV), (d, QKV), whole, (d, QKV), 1, "dw_in_" + nm, True)
             for part, nm in ((dq, "q"), (dk, "k"), (dv, "v"))]
    dw_in.append(_wgrad(h2t, drest, tok_t, colblk(d), (d, 5 * d), pl.BlockSpec((d, d), lambda cc, s: (0, cc)),
                        (d, d), 5, "dw_in_rest", True))
    dw_in = _cols_to_shards(jnp.concatenate(dw_in, axis=1), N_CHIPS)
    shard_w = d // N_CHIPS
    dw_ab = _wgrad(o16, dya, tok(a_w), colblk(shard_w), (a_w, d), pl.BlockSpec((a_w, shard_w), lambda cc, s: (0, cc)),
                   (a_w, shard_w), N_CHIPS, "dw_attn_branch")
    dw_ab = _cols_to_shards(dw_ab, N_CHIPS)
    row_out = pl.BlockSpec((None, shard_w, d), lambda cc, s: (cc, 0, 0))
    dw_cb = _wgrad(yb, dyc, colblk(shard_w), tok(d), (N_CHIPS, shard_w, d), row_out, (shard_w, d), N_CHIPS, "dw_conv_branch")
    dw_o = _wgrad(merged, dz, colblk(shard_w), tok(d), (N_CHIPS, shard_w, d), row_out, (shard_w, d), N_CHIPS, "dw_out")
    shard_grads = reduce_finish(sums_a, chips_a, names_a, "a", after=(dw_in, dw_o))
    sums_b, chips_b = reduce_start([dw_in, dw_ab, dw_cb, dw_o], names_b, "b", 5)

    dh2 = _in_proj_bwd(dq, dk, dv, drest, w_in_full, "in_proj_bwd", after=tuple(sums_b))
    dx1, st2, dy1 = _norm_bwd(dh2, x1, mod2, dx2, z, 1.0, "norm2_bwd", prev=(mod1, 0.5))
    dh1, dg1, du1, a1 = _ffn_bwd(dy1, g1, u1, wg1, wu1, wd1, "ffn1_bwd")
    dx0, st1 = _norm_bwd(dh1, xs, mod1, dx1, y1, 0.5, "norm1_bwd")
    grads_c = list(_ffn_wgrads(h1t, dg1, du1, a1, dy1, "ffn1"))
    shard_grads.update(reduce_finish(sums_b, chips_b, names_b, "b", after=tuple(grads_c)))
    sums_c, chips_c = reduce_start(grads_c, names_c, "c", 6)

    dmod = jnp.concatenate([st1[0:3], st2[0:3], st3[0:3]], axis=0).reshape(1, 9 * d)
    small = jnp.concatenate([dmod, st1[3:4], st2[3:4], st3[3:4], st_qk[0:1], st_qk[1:2],
                             st_conv[0:3].reshape(1, 3 * d)], axis=1)
    small_all = _allgather8(jnp.broadcast_to(small, (8, small.shape[1])), "gather_small").reshape(8, 8, -1)[:, 0]
    dmod_all = small_all[:, :9 * d]
    dmod_shard = lax.dynamic_slice(dmod_all, (0, chip * ada_cols), (8, ada_cols))
    g_w_ada, d_w_ada, nm_w_ada, nv_w_ada = _ada_bwd(c_all, dmod_shard, w_ada[0], m_w_ada[0], v_w_ada[0], "ada_bwd")

    vec_names = ["b_ada", "norm_ffn1", "norm_mix", "norm_ffn2", "q_norm", "k_norm"]
    vec_w = [b_ada, norm_ffn1, norm_mix, norm_ffn2, q_norm, k_norm]
    vec_m = [m_b_ada, m_norm_ffn1, m_norm_mix, m_norm_ffn2, m_q_norm, m_k_norm]
    vec_v = [v_b_ada, v_norm_ffn1, v_norm_mix, v_norm_ffn2, v_q_norm, v_k_norm]
    n_vec = sum(w.shape[1] for w in vec_w)
    cat = lambda arrs: jnp.concatenate(arrs, axis=1)
    vec_out = _small_update(small_all[:, :n_vec], cat(vec_w), cat(vec_m), cat(vec_v), "small_update")
    conv_parts = small_all[:, n_vec:].reshape(8, 3, N_CHIPS * conv_cols)
    conv_parts = lax.dynamic_slice(conv_parts, (0, 0, chip * conv_cols), (8, 3, conv_cols)).reshape(8, 3 * conv_cols)
    flat3 = lambda w: w[0].reshape(1, 3 * conv_cols)
    conv_out = _small_update(conv_parts, flat3(conv_w), flat3(m_conv_w), flat3(v_conv_w), "conv_update")

    res = {"w_ada": [t[None] for t in (g_w_ada, d_w_ada, nm_w_ada, nv_w_ada)],
           "conv_w": [t.reshape(1, 3, conv_cols) for t in conv_out]}
    off = 0
    for nm, w in zip(vec_names, vec_w):
        width = w.shape[1]
        res[nm] = [t[:, off:off + width] for t in vec_out]
        off += width
    big = {"ffn1_w_gate": (ffn1_w_gate, m_ffn1_w_gate, v_ffn1_w_gate), "ffn1_w_up": (ffn1_w_up, m_ffn1_w_up, v_ffn1_w_up),
           "ffn1_w_down": (ffn1_w_down, m_ffn1_w_down, v_ffn1_w_down), "w_in": (w_in, m_w_in, v_w_in),
           "w_attn_branch": (w_attn_branch, m_w_attn_branch, v_w_attn_branch),
           "w_conv_branch": (w_conv_branch, m_w_conv_branch, v_w_conv_branch), "w_out": (w_out, m_w_out, v_w_out),
           "ffn2_w_gate": (ffn2_w_gate, m_ffn2_w_gate, v_ffn2_w_gate), "ffn2_w_up": (ffn2_w_up, m_ffn2_w_up, v_ffn2_w_up),
           "ffn2_w_down": (ffn2_w_down, m_ffn2_w_down, v_ffn2_w_down)}
    def update(nm, after=()):
        w, m, v = big[nm]
        g = shard_grads[nm]
        delta, new_m, new_v = _adamw(w[0], g, m[0], v[0], "adamw_" + nm, after)
        res[nm] = [t[None] for t in (g, delta, new_m, new_v)]
        return new_v

    last = tuple(sums_c)
    for nm in names_a + names_b:
        last = (update(nm, last),)
    shard_grads.update(reduce_finish(sums_c, chips_c, names_c, "c", after=last))
    for nm in names_c:
        update(nm)

    order = ["w_ada", "b_ada", "norm_ffn1", "ffn1_w_gate", "ffn1_w_up", "ffn1_w_down", "norm_mix", "w_in", "q_norm",
             "k_norm", "conv_w", "w_attn_branch", "w_conv_branch", "w_out", "norm_ffn2", "ffn2_w_gate", "ffn2_w_up",
             "ffn2_w_down"]
    return (loss, dx0[None], *[res[nm][0] for nm in order], *[res[nm][1] for nm in order],
            *[res[nm][2] for nm in order], *[res[nm][3] for nm in order])
```

```python
import jax
import jax.numpy as jnp
from jax import lax
from jax.experimental import pallas as pl
from jax.experimental.pallas import tpu as pltpu
from jax.experimental.pallas import tpu_sc as plsc

F32 = jnp.float32
BF16 = jnp.bfloat16
MESH = pl.DeviceIdType.MESH
ANY = pl.BlockSpec(memory_space=pl.ANY)

NORM_EPS = 1e-6
HEAD_DIM = 128
N_GROUPS = 3
HEADS = 4
DILATIONS = (1, 4, 16)
ATTN_BLOCK = 128
SLAB = ATTN_BLOCK * max(DILATIONS)
QKV = N_GROUPS * HEADS * HEAD_DIM
ATTN_SCALE = HEAD_DIM ** -0.5
NEG = -1e30
N_CHIPS = 4

ADAM_LR = 0.001
ADAM_B1 = 0.9
ADAM_B2 = 0.999
ADAM_EPS = 1e-08
ADAM_WD = 0.01
ADAM_STEP = 10

VMEM_LIMIT_BYTES = 56 * 1024 * 1024
TOKEN_TILE = 512
FFN_TILE = 1024
PROJ_TILE = 2048
WGRAD_TILE = 2048
IN_BLOCK = 512
MIX_TILE = 256


def _params(n_axes=0):
    return pltpu.CompilerParams(
        dimension_semantics=("arbitrary",) * n_axes if n_axes else None,
        vmem_limit_bytes=VMEM_LIMIT_BYTES)


def _dot(a, b):
    return jnp.dot(a, b, preferred_element_type=F32)


def _dot_nt(a, b):
    return lax.dot_general(a, b, (((1,), (1,)), ((), ())), preferred_element_type=F32)


def _dot_tn(a, b):
    return lax.dot_general(a, b, (((0,), (0,)), ((), ())), preferred_element_type=F32)


def _sigmoid(x):
    return 1.0 / (1.0 + jnp.exp(-x))


def _place():
    return lax.axis_index("x"), lax.axis_index("y"), lax.axis_index("c")


def _ordered(body, n_in, after):
    if not after:
        return body
    return lambda *refs: body(*refs[:n_in], *refs[n_in + len(after):])


def _allgather8(block, name):
    m_per, n = block.shape

    def body(x_ref, out_ref, send_sems, recv_sems, local_sem):
        x, y, c = _place()
        me, sibling = (x, y, c), (x, y, 1 - c)
        chips = [(1 - x, y), (x, 1 - y), (1 - x, 1 - y)]

        def rows(px, py, pc):
            return out_ref.at[pl.ds((4 * px + 2 * py + pc) * m_per, m_per), :]

        def copy(k, blk, to, src=None):
            return pltpu.make_async_remote_copy(
                src_ref=rows(*blk) if src is None else src, dst_ref=rows(*blk),
                send_sem=send_sems.at[k], recv_sem=recv_sems.at[k],
                device_id=to, device_id_type=MESH)

        mine = pltpu.make_async_copy(x_ref, rows(*me), local_sem)
        mine.start()
        first = [copy(0, me, sibling, src=x_ref)]
        first += [copy(1 + j, me, (*chip, c), src=x_ref) for j, chip in enumerate(chips)]
        for cp in first:
            cp.start()
        passed = [copy(4 + j, (*chip, c), sibling) for j, chip in enumerate(chips)]
        for j, chip in enumerate(chips):
            copy(1 + j, (*chip, c), me).wait_recv()
            passed[j].start()
        copy(0, sibling, me).wait_recv()
        for j, chip in enumerate(chips):
            copy(4 + j, (*chip, 1 - c), me).wait_recv()
        for cp in first + passed:
            cp.wait_send()
        mine.wait()

    return pl.pallas_call(
        body, name=name,
        out_shape=jax.ShapeDtypeStruct((8 * m_per, n), block.dtype),
        in_specs=[pl.BlockSpec(memory_space=pltpu.VMEM)],
        out_specs=pl.BlockSpec(memory_space=pltpu.VMEM),
        scratch_shapes=[pltpu.SemaphoreType.DMA((7,)), pltpu.SemaphoreType.DMA((7,)),
                        pltpu.SemaphoreType.DMA],
        compiler_params=_params(),
    )(block)


def _handshake(peers):
    barrier = pltpu.get_barrier_semaphore()
    for peer in peers:
        pl.semaphore_signal(barrier, inc=1, device_id=peer, device_id_type=MESH)
    pl.semaphore_wait(barrier, len(peers))


def _gather_weights(shards, by_cols, name, collective_id, after=()):
    n_arr = len(shards)

    def body(*refs):
        srcs, outs = refs[:n_arr], refs[n_arr + len(after):2 * n_arr + len(after)]
        send_sems, recv_sems, local_sems = refs[2 * n_arr + len(after):]
        x, y, c = _place()
        me_dev, sibling = (x, y, c), (x, y, 1 - c)
        chips = [(1 - x, y), (x, 1 - y), (1 - x, 1 - y)]
        me = 2 * x + y
        _handshake([sibling] + [(*chip, c) for chip in chips])

        def place(k, chip_idx, rows):
            if by_cols[k]:
                width = srcs[k].shape[1]
                return outs[k].at[rows, pl.ds(pl.multiple_of(chip_idx * width, 128), width)]
            return outs[k].at[chip_idx, rows]

        def copy(k, slot, chip_idx, half_sel, to, from_shard=False):
            half = srcs[k].shape[0] // 2
            rows = pl.ds(half_sel * half, half)
            dst = place(k, chip_idx, rows)
            return pltpu.make_async_remote_copy(
                src_ref=srcs[k].at[rows] if from_shard else dst, dst_ref=dst,
                send_sem=send_sems.at[6 * k + slot], recv_sem=recv_sems.at[6 * k + slot],
                device_id=to, device_id_type=MESH)

        own = [pltpu.make_async_copy(srcs[k], place(k, me, pl.ds(0, srcs[k].shape[0])), local_sems.at[k])
               for k in range(n_arr)]
        for cp in own:
            cp.start()
        sent = []
        for k in range(n_arr):
            for j, chip in enumerate(chips):
                sent.append(copy(k, j, me, c, (*chip, c), from_shard=True))
                sent[-1].start()
        for k in range(n_arr):
            for j, chip in enumerate(chips):
                chip_idx = 2 * chip[0] + chip[1]
                copy(k, j, chip_idx, c, me_dev).wait_recv()
                sent.append(copy(k, 3 + j, chip_idx, c, sibling))
                sent[-1].start()
        for k in range(n_arr):
            for j, chip in enumerate(chips):
                copy(k, 3 + j, 2 * chip[0] + chip[1], 1 - c, me_dev).wait_recv()
        for cp in sent:
            cp.wait_send()
        for cp in own:
            cp.wait()

    def gathered(k):
        r, cols = shards[k].shape
        return (r, N_CHIPS * cols) if by_cols[k] else (N_CHIPS, r, cols)

    return pl.kernel(
        body, name=name,
        out_type=[jax.ShapeDtypeStruct(gathered(k), shards[k].dtype) for k in range(n_arr)],
        mesh=plsc.ScalarSubcoreMesh(axis_name="sequencer", num_cores=1),
        scratch_types=[pltpu.SemaphoreType.DMA((6 * n_arr,)), pltpu.SemaphoreType.DMA((6 * n_arr,)),
                       pltpu.SemaphoreType.DMA((n_arr,))],
        compiler_params=pltpu.CompilerParams(collective_id=collective_id),
    )(*shards, *after)


def _rs_pair_exchange(grads, name):
    n_arr = len(grads)

    def body(*refs):
        srcs, outs = refs[:n_arr], refs[n_arr:2 * n_arr]
        send_sems, recv_sems = refs[2 * n_arr:]
        x, y, c = _place()
        cps = []
        for k in range(n_arr):
            half = srcs[k].shape[1] // 2
            cps.append(pltpu.make_async_remote_copy(
                src_ref=srcs[k].at[:, pl.ds((1 - c) * half, half)], dst_ref=outs[k],
                send_sem=send_sems.at[k], recv_sem=recv_sems.at[k],
                device_id=(x, y, 1 - c), device_id_type=MESH))
            cps[-1].start()
        for cp in cps:
            cp.wait_recv()
        for cp in cps:
            cp.wait_send()

    return pl.pallas_call(
        body, name=name,
        out_shape=[jax.ShapeDtypeStruct((g.shape[0], g.shape[1] // 2, g.shape[2]), g.dtype) for g in grads],
        in_specs=[ANY] * n_arr, out_specs=[ANY] * n_arr,
        scratch_shapes=[pltpu.SemaphoreType.DMA((n_arr,)), pltpu.SemaphoreType.DMA((n_arr,))],
        compiler_params=_params(),
    )(*grads)


def _rs_chip_exchange(sums, name, collective_id):
    n_arr = len(sums)

    def body(*refs):
        srcs, outs = refs[:n_arr], refs[n_arr:2 * n_arr]
        send_sems, recv_sems = refs[2 * n_arr:]
        x, y, c = _place()
        chips = [(1 - x, y), (x, 1 - y), (1 - x, 1 - y)]
        _handshake([(*chip, c) for chip in chips])
        cps = []
        for k in range(n_arr):
            for j, chip in enumerate(chips):
                cps.append(pltpu.make_async_remote_copy(
                    src_ref=srcs[k].at[2 * chip[0] + chip[1]], dst_ref=outs[k].at[j],
                    send_sem=send_sems.at[3 * k + j], recv_sem=recv_sems.at[3 * k + j],
                    device_id=(*chip, c), device_id_type=MESH))
                cps[-1].start()
        for cp in cps:
            cp.wait_recv()
        for cp in cps:
            cp.wait_send()

    return pl.kernel(
        body, name=name,
        out_type=[jax.ShapeDtypeStruct((3,) + s.shape[1:], s.dtype) for s in sums],
        mesh=plsc.ScalarSubcoreMesh(axis_name="sequencer", num_cores=1),
        scratch_types=[pltpu.SemaphoreType.DMA((3 * n_arr,)), pltpu.SemaphoreType.DMA((3 * n_arr,))],
        compiler_params=pltpu.CompilerParams(collective_id=collective_id),
    )(*sums)


def _rs_share(totals, name):
    n_arr = len(totals)

    def body(*refs):
        outs = refs[n_arr:2 * n_arr]
        send_sems, recv_sems = refs[2 * n_arr:]
        x, y, c = _place()

        def half_rows(k, sel):
            return outs[k].at[sel]

        cps = []
        for k in range(n_arr):
            cps.append(pltpu.make_async_remote_copy(
                src_ref=half_rows(k, c), dst_ref=half_rows(k, c), send_sem=send_sems.at[k], recv_sem=recv_sems.at[k],
                device_id=(x, y, 1 - c), device_id_type=MESH))
            cps[-1].start()
        for k in range(n_arr):
            pltpu.make_async_remote_copy(
                src_ref=half_rows(k, c), dst_ref=half_rows(k, 1 - c), send_sem=send_sems.at[k],
                recv_sem=recv_sems.at[k], device_id=(x, y, 1 - c), device_id_type=MESH).wait_recv()
        for cp in cps:
            cp.wait_send()

    shared = pl.pallas_call(
        body, name=name,
        out_shape=[jax.ShapeDtypeStruct(t.shape, t.dtype) for t in totals],
        in_specs=[ANY] * n_arr, out_specs=[ANY] * n_arr,
        input_output_aliases={k: k for k in range(n_arr)},
        scratch_shapes=[pltpu.SemaphoreType.DMA((n_arr,)), pltpu.SemaphoreType.DMA((n_arr,))],
        compiler_params=_params(),
    )(*totals)
    return [t.reshape(2 * t.shape[1], t.shape[2]) for t in shared]


def _pair_add(grad, recv, c_idx, name):
    n, r, cols = grad.shape
    half = r // 2
    rows = half // 2

    def body(_, g_ref, r_ref, o_ref):
        o_ref[...] = (g_ref[...].astype(F32) + r_ref[...].astype(F32)).astype(o_ref.dtype)

    return pl.pallas_call(
        body, name=name,
        grid_spec=pltpu.PrefetchScalarGridSpec(
            num_scalar_prefetch=1, grid=(n, 2),
            in_specs=[pl.BlockSpec((None, None, rows, cols), lambda s, i, ci: (s, ci[0], i, 0)),
                      pl.BlockSpec((None, rows, cols), lambda s, i, ci: (s, i, 0))],
            out_specs=pl.BlockSpec((None, rows, cols), lambda s, i, ci: (s, i, 0))),
        out_shape=jax.ShapeDtypeStruct((n, half, cols), BF16),
        compiler_params=_params(2),
    )(c_idx, grad.reshape(n, 2, half, cols), recv)


def _chip_add(sums, recv, chip_and_core, name, after=()):
    _, half, cols = sums.shape
    rows = half // 2

    def body(_, s_ref, r0_ref, r1_ref, r2_ref, o_ref):
        o_ref[...] = ((s_ref[...].astype(F32) + r0_ref[...].astype(F32))
                      + r1_ref[...].astype(F32)) + r2_ref[...].astype(F32)

    def recv_spec(j):
        return pl.BlockSpec((None, rows, cols), lambda i, ci: (j, i, 0))

    return pl.pallas_call(
        _ordered(body, 5, after), name=name,
        grid_spec=pltpu.PrefetchScalarGridSpec(
            num_scalar_prefetch=1, grid=(2,),
            in_specs=[pl.BlockSpec((None, rows, cols), lambda i, ci: (ci[0], i, 0)),
                      recv_spec(0), recv_spec(1), recv_spec(2)] + [ANY] * len(after),
            out_specs=pl.BlockSpec((None, rows, cols), lambda i, ci: (ci[1], i, 0))),
        out_shape=jax.ShapeDtypeStruct((2, half, cols), F32),
        compiler_params=_params(1),
    )(chip_and_core, sums, recv, recv, recv, *after)


def _rms(x):
    return lax.rsqrt(jnp.mean(x * x, axis=-1, keepdims=True) + NORM_EPS)


def _norm_mod(x, mod, name, prev=None):
    s_len, d = x.shape
    tm = TOKEN_TILE

    def body(*refs):
        if prev is None:
            x_ref, mod_ref, h_ref, ht_ref = refs
            xv = x_ref[...]
        else:
            x_ref, y_ref, modp_ref, mod_ref, xo_ref, h_ref, ht_ref = refs
            xv = x_ref[...] + prev[2] * modp_ref[2:3, :] * y_ref[...]
            xo_ref[...] = xv
        n = (xv * _rms(xv)) * mod_ref[3:4, :]
        h = n * (1.0 + mod_ref[1:2, :]) + mod_ref[0:1, :]
        h_ref[...] = h.astype(BF16)
        ht_ref[...] = h.T.astype(BF16)

    tile = pl.BlockSpec((tm, d), lambda i: (i, 0))
    small = pl.BlockSpec((8, d), lambda i: (0, 0))
    h_specs = [tile, pl.BlockSpec((d, tm), lambda i: (0, i))]
    h_shapes = [jax.ShapeDtypeStruct((s_len, d), BF16), jax.ShapeDtypeStruct((d, s_len), BF16)]
    if prev is None:
        return pl.pallas_call(
            body, name=name, grid=(s_len // tm,), in_specs=[tile, small], out_specs=h_specs, out_shape=h_shapes,
            compiler_params=_params(1))(x, mod)
    return pl.pallas_call(
        body, name=name, grid=(s_len // tm,), in_specs=[tile, tile, small, small],
        out_specs=[tile] + h_specs, out_shape=[jax.ShapeDtypeStruct((s_len, d), F32)] + h_shapes,
        compiler_params=_params(1))(x, prev[0], prev[1], mod)


def _norm_bwd(dh, x, mod, dxo, y_raw, coef, name, after=(), prev=None):
    s_len, d = x.shape
    tm = TOKEN_TILE

    def body(*refs):
        if prev is None:
            dh_ref, x_ref, mod_ref, dxo_ref, y_ref, dx_ref, st_ref = refs
        else:
            dh_ref, x_ref, mod_ref, dxo_ref, y_ref, modp_ref, dx_ref, st_ref, dyp_ref = refs

        @pl.when(pl.program_id(0) == 0)
        def _():
            st_ref[...] = jnp.zeros_like(st_ref)

        xv, dhv, dxov = x_ref[...], dh_ref[...], dxo_ref[...]
        r = _rms(xv)
        xh = xv * r
        gain, scale = mod_ref[3:4, :], mod_ref[1:2, :]
        dn = dhv * (1.0 + scale)
        dxh = dn * gain
        dx = dxov + r * (dxh - xh * jnp.mean(dxh * xh, axis=-1, keepdims=True))
        dx_ref[...] = dx
        if prev is not None:
            dyp_ref[...] = (prev[1] * modp_ref[2:3, :] * dx).astype(BF16)
        st_ref[0:1, :] += jnp.sum(dhv, axis=0, keepdims=True)
        st_ref[1:2, :] += jnp.sum(dhv * (xh * gain), axis=0, keepdims=True)
        st_ref[2:3, :] += coef * jnp.sum(y_ref[...] * dxov, axis=0, keepdims=True)
        st_ref[3:4, :] += jnp.sum(dn * xh, axis=0, keepdims=True)

    tile = pl.BlockSpec((tm, d), lambda i: (i, 0))
    small = pl.BlockSpec((8, d), lambda i: (0, 0))
    operands = [dh, x, mod, dxo, y_raw] + ([] if prev is None else [prev[0]])
    in_specs = [tile, tile, small, tile, tile] + ([] if prev is None else [small])
    out_specs = [tile, small] + ([] if prev is None else [tile])
    out_shape = [jax.ShapeDtypeStruct((s_len, d), F32), jax.ShapeDtypeStruct((8, d), F32)]
    if prev is not None:
        out_shape.append(jax.ShapeDtypeStruct((s_len, d), BF16))
    return pl.pallas_call(
        _ordered(body, len(operands), after), name=name, grid=(s_len // tm,),
        in_specs=in_specs + [ANY] * len(after), out_specs=out_specs, out_shape=out_shape,
        compiler_params=_params(1),
    )(*operands, *after)


def _loss_grad(x, y, mod, target, name):
    s_len, d = x.shape
    tm = TOKEN_TILE

    def body(x_ref, y_ref, mod_ref, t_ref, do_ref, dy_ref, part_ref):
        @pl.when(pl.program_id(0) == 0)
        def _():
            part_ref[...] = jnp.zeros_like(part_ref)

        half_gate = 0.5 * mod_ref[2:3, :]
        err = (x_ref[...] + half_gate * y_ref[...]) - t_ref[...]
        do = err * (1.0 / d)
        do_ref[...] = do
        dy_ref[...] = (half_gate * do).astype(BF16)
        sq = err * err
        part_ref[...] += jnp.sum(sq.reshape(tm // 8, 8, d), axis=0)

    tile = pl.BlockSpec((tm, d), lambda i: (i, 0))
    small = pl.BlockSpec((8, d), lambda i: (0, 0))
    return pl.pallas_call(
        body, name=name, grid=(s_len // tm,),
        in_specs=[tile, tile, small, tile],
        out_specs=[tile, tile, small],
        out_shape=[jax.ShapeDtypeStruct((s_len, d), F32), jax.ShapeDtypeStruct((s_len, d), BF16),
                   jax.ShapeDtypeStruct((8, d), F32)],
        compiler_params=_params(1),
    )(x, y, mod, target)


def _adamw_math(w, g, m, v):
    m = ADAM_B1 * m + (1.0 - ADAM_B1) * g
    v = ADAM_B2 * v + (1.0 - ADAM_B2) * (g * g)
    m_hat = m / (1.0 - ADAM_B1 ** ADAM_STEP)
    v_hat = v / (1.0 - ADAM_B2 ** ADAM_STEP)
    delta = -ADAM_LR * (m_hat / (jnp.sqrt(v_hat) + ADAM_EPS) + ADAM_WD * w)
    return delta, m, v


def _adamw(w, g, m, v, name, after=()):
    r, cols = w.shape
    tr = r // 8 if r % 64 == 0 else r

    def body(w_ref, g_ref, m_ref, v_ref, d_ref, nm_ref, nv_ref):
        d_ref[...], nm_ref[...], nv_ref[...] = _adamw_math(w_ref[...], g_ref[...], m_ref[...], v_ref[...])

    tile = pl.BlockSpec((tr, cols), lambda i: (i, 0))
    shape = jax.ShapeDtypeStruct((r, cols), F32)
    return pl.pallas_call(
        _ordered(body, 4, after), name=name, grid=(r // tr,),
        in_specs=[tile] * 4 + [ANY] * len(after), out_specs=[tile] * 3, out_shape=[shape] * 3,
        compiler_params=_params(1),
    )(w, g, m, v, *after)


def _in_parts(tm, n_qkv, n_rest):
    def part(lo, n_blk):
        return pl.BlockSpec((tm, IN_BLOCK), lambda i, j: (i, jnp.clip(j - lo, 0, n_blk - 1)))
    return [part(0, n_qkv), part(n_qkv, n_qkv), part(2 * n_qkv, n_qkv), part(3 * n_qkv, n_rest)]


def _pick_part(j, n_qkv, refs, fn):
    bounds = [0, n_qkv, 2 * n_qkv, 3 * n_qkv]
    for p, ref in enumerate(refs):
        inside = j >= bounds[p]
        if p + 1 < len(refs):
            inside = inside & (j < bounds[p + 1])
        pl.when(inside)(lambda ref=ref: fn(ref))


def _rows(base, count, stride):
    return pl.ds(base, count) if stride == 1 else pl.ds(base, count, stride=stride)


REORDER_STRIDE = 4


def _reorder_plan(dil):
    inner = min(dil, REORDER_STRIDE)
    return inner, dil // inner, SLAB // inner, SLAB // dil


def _to_residue_order(dst, src, dil, tmp):
    inner, outer, big, seg = _reorder_plan(dil)
    if outer == 1:
        for r in range(dil):
            dst[pl.ds(r * seg, seg), :] = src[_rows(r, seg, dil), :]
        return
    for b in range(inner):
        tmp[pl.ds(b * big, big), :] = src[_rows(b, big, inner), :]
    for a in range(outer):
        for b in range(inner):
            dst[pl.ds((inner * a + b) * seg, seg), :] = tmp[_rows(b * big + a, seg, outer), :]


def _to_token_order(dst, src, dil, tmp):
    inner, outer, big, seg = _reorder_plan(dil)
    if outer == 1:
        for r in range(dil):
            dst[_rows(r, seg, dil), :] = src[pl.ds(r * seg, seg), :]
        return
    for a in range(outer):
        for b in range(inner):
            tmp[_rows(b * big + a, seg, outer), :] = src[pl.ds((inner * a + b) * seg, seg), :]
    for b in range(inner):
        dst[_rows(b, big, inner), :] = tmp[pl.ds(b * big, big), :]


def _in_proj(h, w, q_norm, k_norm, name):
    s_len, d = h.shape
    tm = PROJ_TILE
    assert tm == SLAB and IN_BLOCK == HEADS * HEAD_DIM
    steps = w.shape[1] // IN_BLOCK
    n_qkv = 3 * QKV // IN_BLOCK

    def body(h_ref, w_ref, qn_ref, kn_ref, qkv_ref, rest_ref, hat_ref, tok_s, res_s, tmp_s):
        j = pl.program_id(1)
        res = _dot(h_ref[...], w_ref[...])

        def emit(sect, gi):
            dil = DILATIONS[gi]
            for hh in range(HEADS):
                cols = slice(hh * HEAD_DIM, (hh + 1) * HEAD_DIM)
                x = res[:, cols]
                if sect < 2:
                    x = (x * _rms(x)) * (qn_ref if sect == 0 else kn_ref)[...]
                tok_s[...] = x
                _to_residue_order(res_s, tok_s, dil, tmp_s)
                hat_ref[:, cols] = res_s[...].astype(BF16)

        @pl.when(j < n_qkv)
        def _():
            qkv_ref[...] = res

        for sect in range(3):
            for gi in range(N_GROUPS):
                pl.when(j == sect * N_GROUPS + gi)(lambda sect=sect, gi=gi: emit(sect, gi))

        @pl.when(j >= n_qkv)
        def _():
            rest_ref[...] = res

    qkv_blk = pl.BlockSpec((tm, IN_BLOCK), lambda i, j: (i, jnp.minimum(j, n_qkv - 1)))
    small = pl.BlockSpec((1, HEAD_DIM), lambda i, j: (0, 0))
    return pl.pallas_call(
        body, name=name, grid=(s_len // tm, steps),
        in_specs=[pl.BlockSpec((tm, d), lambda i, j: (i, 0)), pl.BlockSpec((d, IN_BLOCK), lambda i, j: (0, j)),
                  small, small],
        out_specs=[qkv_blk, pl.BlockSpec((tm, IN_BLOCK), lambda i, j: (i, jnp.maximum(j - n_qkv, 0))), qkv_blk],
        out_shape=[jax.ShapeDtypeStruct((s_len, 3 * QKV), F32),
                   jax.ShapeDtypeStruct((s_len, w.shape[1] - 3 * QKV), F32),
                   jax.ShapeDtypeStruct((s_len, 3 * QKV), BF16)],
        scratch_shapes=[pltpu.VMEM((tm, HEAD_DIM), F32)] * 3,
        compiler_params=_params(2),
    )(h, w, q_norm, k_norm)


def _in_proj_bwd(dq, dk, dv, drest, w, name, after=()):
    s_len = dq.shape[0]
    d = w.shape[0]
    tm = PROJ_TILE
    steps = w.shape[1] // IN_BLOCK
    n_qkv = QKV // IN_BLOCK

    def body(dq_ref, dk_ref, dv_ref, dr_ref, w_ref, o_ref, acc_ref):
        j = pl.program_id(1)

        @pl.when(j == 0)
        def _():
            acc_ref[...] = jnp.zeros_like(acc_ref)

        def add(a_ref):
            acc_ref[...] += _dot_nt(a_ref[...], w_ref[...])

        _pick_part(j, n_qkv, [dq_ref, dk_ref, dv_ref, dr_ref], add)

        @pl.when(j == steps - 1)
        def _():
            o_ref[...] = acc_ref[...]

    return pl.pallas_call(
        _ordered(body, 5, after), name=name, grid=(s_len // tm, steps),
        in_specs=(_in_parts(tm, n_qkv, steps - 3 * n_qkv) + [pl.BlockSpec((d, IN_BLOCK), lambda i, j: (0, j))]
                  + [ANY] * len(after)),
        out_specs=pl.BlockSpec((tm, d), lambda i, j: (i, 0)),
        out_shape=jax.ShapeDtypeStruct((s_len, d), F32),
        scratch_shapes=[pltpu.VMEM((tm, d), F32)],
        compiler_params=_params(2),
    )(dq, dk, dv, drest, w, *after)


def _wgrad(x, y, x_spec, y_spec, out_shape, out_spec, acc_shape, n_chunks, name, x_transposed=False, after=()):
    s_len = y.shape[-2]
    ts = WGRAD_TILE
    steps = s_len // ts

    def body(x_ref, y_ref, o_ref, acc_ref):
        s = pl.program_id(1)

        @pl.when(s == 0)
        def _():
            acc_ref[...] = jnp.zeros_like(acc_ref)

        acc_ref[...] += (_dot if x_transposed else _dot_tn)(x_ref[...], y_ref[...])

        @pl.when(s == steps - 1)
        def _():
            o_ref[...] = acc_ref[...].astype(o_ref.dtype)

    return pl.pallas_call(
        _ordered(body, 2, after), name=name, grid=(n_chunks, steps),
        in_specs=[x_spec(ts), y_spec(ts)] + [ANY] * len(after), out_specs=out_spec,
        out_shape=jax.ShapeDtypeStruct(out_shape, BF16),
        scratch_shapes=[pltpu.VMEM(acc_shape, F32)],
        compiler_params=_params(2),
    )(x, y, *after)


def _pieces(width, piece=256):
    return [slice(a, min(a + piece, width)) for a in range(0, width, piece)]


def _ffn_fwd(h, w_gate, w_up, w_down, name):
    s_len, d = h.shape
    n_chunks, _, fs = w_gate.shape
    tm = FFN_TILE

    def body(h_ref, wg_ref, wu_ref, wd_ref, g_ref, u_ref, y_ref):
        j = pl.program_id(1)
        hv = h_ref[...]
        total = None
        for cols in _pieces(fs):
            g = _dot(hv, wg_ref[:, cols])
            u = _dot(hv, wu_ref[:, cols])
            g_ref[:, cols] = g.astype(BF16)
            u_ref[:, cols] = u.astype(BF16)
            act = (g * _sigmoid(g)) * u
            part = _dot(act.astype(BF16), wd_ref[cols, :])
            total = part if total is None else total + part

        @pl.when(j == 0)
        def _():
            y_ref[...] = total

        @pl.when(j > 0)
        def _():
            y_ref[...] += total

    tile = pl.BlockSpec((tm, d), lambda i, j: (i, 0))
    hid = pl.BlockSpec((None, tm, fs), lambda i, j: (j, i, 0))
    w_in_spec = pl.BlockSpec((None, d, fs), lambda i, j: (j, 0, 0))
    hid_shape = jax.ShapeDtypeStruct((n_chunks, s_len, fs), BF16)
    return pl.pallas_call(
        body, name=name, grid=(s_len // tm, n_chunks),
        in_specs=[tile, w_in_spec, w_in_spec, pl.BlockSpec((None, fs, d), lambda i, j: (j, 0, 0))],
        out_specs=[hid, hid, tile],
        out_shape=[hid_shape, hid_shape, jax.ShapeDtypeStruct((s_len, d), F32)],
        compiler_params=_params(2),
    )(h, w_gate, w_up, w_down)


def _ffn_bwd(dy, g_pre, u_pre, w_gate, w_up, w_down, name):
    s_len, d = dy.shape
    n_chunks, _, fs = w_gate.shape
    tm = FFN_TILE

    def body(dy_ref, g_ref, u_ref, wg_ref, wu_ref, wd_ref, dh_ref, dg_ref, du_ref, a_ref):
        j = pl.program_id(1)
        dyv = dy_ref[...]
        total = None
        for cols in _pieces(fs):
            da = _dot_nt(dyv, wd_ref[cols, :])
            g = g_ref[:, cols].astype(F32)
            u = u_ref[:, cols].astype(F32)
            sg = _sigmoid(g)
            silu = g * sg
            dg = (da * u * (sg * (1.0 + g * (1.0 - sg)))).astype(BF16)
            du = (da * silu).astype(BF16)
            dg_ref[:, cols] = dg
            du_ref[:, cols] = du
            a_ref[:, cols] = (silu * u).astype(BF16)
            part = _dot_nt(dg, wg_ref[:, cols]) + _dot_nt(du, wu_ref[:, cols])
            total = part if total is None else total + part

        @pl.when(j == 0)
        def _():
            dh_ref[...] = total

        @pl.when(j > 0)
        def _():
            dh_ref[...] += total

    tile = pl.BlockSpec((tm, d), lambda i, j: (i, 0))
    hid = pl.BlockSpec((None, tm, fs), lambda i, j: (j, i, 0))
    w_in_spec = pl.BlockSpec((None, d, fs), lambda i, j: (j, 0, 0))
    hid_shape = jax.ShapeDtypeStruct((n_chunks, s_len, fs), BF16)
    return pl.pallas_call(
        body, name=name, grid=(s_len // tm, n_chunks),
        in_specs=[tile, hid, hid, w_in_spec, w_in_spec, pl.BlockSpec((None, fs, d), lambda i, j: (j, 0, 0))],
        out_specs=[tile, hid, hid, hid],
        out_shape=[jax.ShapeDtypeStruct((s_len, d), F32), hid_shape, hid_shape, hid_shape],
        compiler_params=_params(2),
    )(dy, g_pre, u_pre, w_gate, w_up, w_down)


def _ffn_wgrads(ht, dg, du, act, dy, tag, after=()):
    n_chunks, s_len, fs = dg.shape
    d = ht.shape[0]
    tok = lambda ts: pl.BlockSpec((ts, d), lambda c, s: (s, 0))
    tok_t = lambda ts: pl.BlockSpec((d, ts), lambda c, s: (0, s))
    hid = lambda ts: pl.BlockSpec((None, ts, fs), lambda c, s: (c, s, 0))
    d_up = pl.BlockSpec((None, d, fs), lambda c, s: (c, 0, 0))
    d_down = pl.BlockSpec((None, fs, d), lambda c, s: (c, 0, 0))
    dwg = _wgrad(ht, dg, tok_t, hid, (n_chunks, d, fs), d_up, (d, fs), n_chunks, tag + "_dwg", True, after)
    dwu = _wgrad(ht, du, tok_t, hid, (n_chunks, d, fs), d_up, (d, fs), n_chunks, tag + "_dwu", True, after)
    dwd = _wgrad(act, dy, hid, tok, (n_chunks, fs, d), d_down, (fs, d), n_chunks, tag + "_dwd", False, after)
    return dwg, dwu, dwd


def _band_bias():
    qi = lax.broadcasted_iota(jnp.int32, (ATTN_BLOCK, 2 * ATTN_BLOCK), 0)
    kj = lax.broadcasted_iota(jnp.int32, (ATTN_BLOCK, 2 * ATTN_BLOCK), 1)
    band = (kj >= qi) & (kj <= qi + ATTN_BLOCK)
    return jnp.where(band, 0.0, NEG), jnp.where(band & (kj >= ATTN_BLOCK), 0.0, NEG)


def _qkv_specs(slab_of, sections):
    def spec(sect, back):
        return pl.BlockSpec((SLAB, HEAD_DIM),
                            lambda h, s, g: (jnp.maximum(slab_of(s) - back, 0), (sect * N_GROUPS + g) * HEADS + h))
    return [spec(sect, back) for sect, back in sections]


HAT_BLOCKS = [(0, 0), (1, 0), (2, 0), (1, 1), (2, 1)]


def _stage_keys(k_ref, v_ref, kp_ref, vp_ref, kbuf, vbuf, dil, n):
    run = SLAB // dil
    for r in range(dil):
        own, before = pl.ds(r * run, run), pl.ds(2 * r * run, run)
        kbuf[pl.ds((2 * r + 1) * run, run), :] = k_ref[own, :]
        vbuf[pl.ds((2 * r + 1) * run, run), :] = v_ref[own, :]

        @pl.when(n > 0)
        def _():
            kbuf[before, :] = kp_ref[own, :]
            vbuf[before, :] = vp_ref[own, :]

        @pl.when(n == 0)
        def _():
            kbuf[before, :] = jnp.zeros((run, HEAD_DIM), BF16)
            vbuf[before, :] = jnp.zeros((run, HEAD_DIM), BF16)


def _for_each_tile(dil, n, tile_fn):
    run = SLAB // dil
    bias, first_bias = _band_bias()
    for jj in range(run // ATTN_BLOCK):
        start = jj * ATTN_BLOCK
        tile_bias = jnp.where(n == 0, first_bias, bias) if jj == 0 else bias
        for r in range(dil):
            tile_fn(pl.ds(r * run + start, ATTN_BLOCK),
                    pl.ds((2 * r + 1) * run - ATTN_BLOCK + start, 2 * ATTN_BLOCK), tile_bias)


def _attn_fwd(hat, name):
    s_len = hat.shape[0]
    e = HEAD_DIM
    n_slabs = s_len // SLAB

    def body(q_ref, k_ref, v_ref, kp_ref, vp_ref, o_ref, lse_ref, kbuf, vbuf, m_s, l_s, acc_s, m_p, l_p, acc_p, tmp_s):
        n, grp = pl.program_id(1), pl.program_id(2)

        def run(gi, dil):
            _stage_keys(k_ref, v_ref, kp_ref, vp_ref, kbuf, vbuf, dil, n)

            def tile(q_rows, kv_rows, bias):
                s = _dot_nt(q_ref[q_rows, :], kbuf[kv_rows, :]) * ATTN_SCALE + bias
                m = jnp.max(s, axis=-1, keepdims=True)
                p = jnp.exp(s - m)
                m_p[q_rows, :] = jnp.broadcast_to(m, (ATTN_BLOCK, e))
                l_p[q_rows, :] = jnp.broadcast_to(jnp.sum(p, axis=-1, keepdims=True), (ATTN_BLOCK, e))
                acc_p[q_rows, :] = _dot(p.astype(BF16), vbuf[kv_rows, :])

            _for_each_tile(dil, n, tile)
            _to_token_order(m_s.at[gi], m_p, dil, tmp_s)
            _to_token_order(l_s.at[gi], l_p, dil, tmp_s)
            _to_token_order(acc_s.at[gi], acc_p, dil, tmp_s)

        for gi, dil in enumerate(DILATIONS):
            pl.when(grp == gi)(lambda gi=gi, dil=dil: run(gi, dil))

        @pl.when(grp == N_GROUPS - 1)
        def _():
            m_all = jnp.maximum(jnp.maximum(m_s[0], m_s[1]), m_s[2])
            den = jnp.zeros((SLAB, e), F32)
            num = jnp.zeros((SLAB, e), F32)
            for gi in range(N_GROUPS):
                w = jnp.exp(m_s[gi] - m_all)
                den += l_s[gi] * w
                num += acc_s[gi] * w
            o_ref[...] = num / den
            lse_ref[...] = m_all + jnp.log(den)

    out = pl.BlockSpec((SLAB, e), lambda h, n, g: (n, h))
    return pl.pallas_call(
        body, name=name, grid=(HEADS, n_slabs, N_GROUPS),
        in_specs=_qkv_specs(lambda n: n, HAT_BLOCKS),
        out_specs=[out, out],
        out_shape=[jax.ShapeDtypeStruct((s_len, HEADS * e), F32)] * 2,
        scratch_shapes=[pltpu.VMEM((2 * SLAB, e), BF16), pltpu.VMEM((2 * SLAB, e), BF16),
                        pltpu.VMEM((N_GROUPS, SLAB, e), F32), pltpu.VMEM((N_GROUPS, SLAB, e), F32),
                        pltpu.VMEM((N_GROUPS, SLAB, e), F32)]
        + [pltpu.VMEM((SLAB, e), F32)] * 4,
        compiler_params=_params(3),
    )(hat, hat, hat, hat, hat)


def _attn_bwd(qkv, hat, d_out, out, lse, q_norm, k_norm, name):
    s_len = qkv.shape[0]
    e = HEAD_DIM
    n_slabs = s_len // SLAB

    def body(q_ref, k_ref, v_ref, kp_ref, vp_ref, qraw_ref, kraw_ref, do_ref, o_ref, lse_ref, qn_ref, kn_ref,
             dq_ref, dk_ref, dv_ref, st_ref, kbuf, vbuf, stat_s, dqs, dkb, dvb, dk_tok, dv_tok, carry,
             do_p, stat_p, dq_p, dk_p, dv_p, tmp_s, do16_p):
        head, step, grp = pl.program_id(0), pl.program_id(1), pl.program_id(2)
        n = n_slabs - 1 - step
        dkb[...] = jnp.zeros_like(dkb)
        dvb[...] = jnp.zeros_like(dvb)
        @pl.when(grp == 0)
        def _():
            lane = lax.broadcasted_iota(jnp.int32, (SLAB, e), 1)
            stat_s[...] = jnp.where(lane < e // 2, lse_ref[...],
                                    jnp.sum(do_ref[...] * o_ref[...], axis=-1, keepdims=True))

        @pl.when((head == 0) & (step == 0) & (grp == 0))
        def _():
            st_ref[...] = jnp.zeros_like(st_ref)

        def run(gi, dil):
            seg = SLAB // dil
            _stage_keys(k_ref, v_ref, kp_ref, vp_ref, kbuf, vbuf, dil, n)

            @pl.when(step == 0)
            def _():
                carry[gi] = jnp.zeros((2, SLAB, e), F32)

            _to_residue_order(do_p, do_ref, dil, tmp_s)
            do16_p[...] = do_p[...].astype(BF16)
            _to_residue_order(stat_p, stat_s, dil, tmp_s)

            def tile(q_rows, kv_rows, bias):
                q = q_ref[q_rows, :]
                k = kbuf[kv_rows, :]
                v = vbuf[kv_rows, :]
                stat = stat_p[q_rows, :]
                do16 = do16_p[q_rows, :]
                s = _dot_nt(q, k) * ATTN_SCALE + bias
                p = jnp.exp(s - stat[:, 0:1])
                ds = (p * (_dot_nt(do16, v) - stat[:, e // 2:e // 2 + 1]) * ATTN_SCALE).astype(BF16)
                dq_p[q_rows, :] = _dot(ds, k)
                dkb[kv_rows, :] += _dot_tn(ds, q)
                dvb[kv_rows, :] += _dot_tn(p.astype(BF16), do16)

            _for_each_tile(dil, n, tile)
            for r in range(dil):
                own, before = pl.ds((2 * r + 1) * seg, seg), pl.ds(2 * r * seg, seg)
                kept = pl.ds(r * seg, seg)
                dk_p[kept, :] = dkb[own, :] + carry.at[gi, 0][kept, :]
                dv_p[kept, :] = dvb[own, :] + carry.at[gi, 1][kept, :]
                carry.at[gi, 0][kept, :] = dkb[before, :]
                carry.at[gi, 1][kept, :] = dvb[before, :]
            _to_token_order(dqs, dq_p, dil, tmp_s)
            _to_token_order(dk_tok, dk_p, dil, tmp_s)
            _to_token_order(dv_tok, dv_p, dil, tmp_s)

            def norm_bwd(raw, gain, d_hat):
                r = _rms(raw)
                y = raw * r
                dy = d_hat * gain
                return r * (dy - y * jnp.mean(dy * y, axis=-1, keepdims=True)), jnp.sum(d_hat * y, axis=0, keepdims=True)

            dq, dqn = norm_bwd(qraw_ref[...], qn_ref[...], dqs[...])
            dk, dkn = norm_bwd(kraw_ref[...], kn_ref[...], dk_tok[...])
            dq_ref[...] = dq.astype(BF16)
            dk_ref[...] = dk.astype(BF16)
            dv_ref[...] = dv_tok[...].astype(BF16)
            st_ref[0:1, :] += dqn
            st_ref[1:2, :] += dkn

        for gi, dil in enumerate(DILATIONS):
            pl.when(grp == gi)(lambda gi=gi, dil=dil: run(gi, dil))

    slab_of = lambda s: n_slabs - 1 - s
    small = pl.BlockSpec((1, e), lambda h, s, g: (0, 0))
    head_blk = pl.BlockSpec((SLAB, e), lambda h, s, g: (slab_of(s), h))
    grad_blk = pl.BlockSpec((SLAB, e), lambda h, s, g: (slab_of(s), g * HEADS + h))
    grad_shape = jax.ShapeDtypeStruct((s_len, QKV), BF16)
    return pl.pallas_call(
        body, name=name, grid=(HEADS, n_slabs, N_GROUPS),
        in_specs=(_qkv_specs(slab_of, HAT_BLOCKS) + _qkv_specs(slab_of, [(0, 0), (1, 0)])
                  + [head_blk, head_blk, head_blk, small, small]),
        out_specs=[grad_blk, grad_blk, grad_blk, pl.BlockSpec((8, e), lambda h, s, g: (0, 0))],
        out_shape=[grad_shape, grad_shape, grad_shape, jax.ShapeDtypeStruct((8, e), F32)],
        scratch_shapes=[pltpu.VMEM((2 * SLAB, e), BF16), pltpu.VMEM((2 * SLAB, e), BF16), pltpu.VMEM((SLAB, e), F32),
                        pltpu.VMEM((SLAB, e), F32), pltpu.VMEM((2 * SLAB, e), F32), pltpu.VMEM((2 * SLAB, e), F32),
                        pltpu.VMEM((SLAB, e), F32), pltpu.VMEM((SLAB, e), F32),
                        pltpu.VMEM((N_GROUPS, 2, SLAB, e), F32)]
        + [pltpu.VMEM((SLAB, e), F32)] * 6 + [pltpu.VMEM((SLAB, e), BF16)],
        compiler_params=_params(3),
    )(hat, hat, hat, hat, hat, qkv, qkv, d_out, out, lse, q_norm, k_norm)


def _shift_rows(x, by, edge, forward):
    t_len = x.shape[0]
    row = lax.broadcasted_iota(jnp.int32, x.shape, 0)
    if forward:
        out = pltpu.roll(x, by, 0)
        for i in range(by):
            out = jnp.where(row == i, edge[8 - by + i:8 - by + i + 1, :], out)
    else:
        out = pltpu.roll(x, t_len - by, 0)
        for i in range(by):
            out = jnp.where(row == t_len - by + i, edge[i:i + 1, :], out)
    return out


def _mix_fwd(x, o, rest, mod, conv_w, w_attn, w_conv, w_out, name):
    s_len, d = x.shape
    tm = MIX_TILE
    a_w = o.shape[1]

    def body(x_ref, o_ref, u_ref, b_ref, c_ref, ga_ref, gc_ref, mod_ref, cw_ref, wa_ref, wc_ref, wo_ref,
             xo_ref, z_ref, ya_ref, yc_ref, conv_ref, yb_ref, m_ref, o16_ref, carry):
        @pl.when(pl.program_id(0) == 0)
        def _():
            carry[...] = jnp.zeros_like(carry)

        xc = c_ref[...] * u_ref[...]
        edge = carry[...]
        conv = (_shift_rows(xc, 2, edge, True) * cw_ref[0:1, :] + _shift_rows(xc, 1, edge, True) * cw_ref[1:2, :]
                + xc * cw_ref[2:3, :])
        carry[...] = xc[tm - 8:tm, :]
        yb = (b_ref[...] * conv).astype(BF16)
        o16 = o_ref[...].astype(BF16)
        ya = _dot(o16, wa_ref[...])
        yc = _dot(yb, wc_ref[...])
        merged = (_sigmoid(ga_ref[...]) * ya + _sigmoid(gc_ref[...]) * yc).astype(BF16)
        z = _dot(merged, wo_ref[...])
        xo_ref[...] = x_ref[...] + mod_ref[2:3, :] * z
        z_ref[...] = z
        ya_ref[...] = ya.astype(BF16)
        yc_ref[...] = yc.astype(BF16)
        conv_ref[...] = conv.astype(BF16)
        yb_ref[...] = yb
        m_ref[...] = merged
        o16_ref[...] = o16

    tile = pl.BlockSpec((tm, d), lambda i: (i, 0))
    sect = lambda k: pl.BlockSpec((tm, d), lambda i: (i, k))
    att = pl.BlockSpec((tm, a_w), lambda i: (i, 0))
    const = lambda shape: pl.BlockSpec(shape, lambda i: (0, 0))
    f32_out = jax.ShapeDtypeStruct((s_len, d), F32)
    b16_out = jax.ShapeDtypeStruct((s_len, d), BF16)
    return pl.pallas_call(
        body, name=name, grid=(s_len // tm,),
        in_specs=[tile, att, sect(0), sect(1), sect(2), sect(3), sect(4), const((8, d)), const((8, d)),
                  const((a_w, d)), const((d, d)), const((d, d))],
        out_specs=[tile] * 7 + [att],
        out_shape=[f32_out, f32_out] + [b16_out] * 5 + [jax.ShapeDtypeStruct((s_len, a_w), BF16)],
        scratch_shapes=[pltpu.VMEM((8, d), F32)],
        compiler_params=_params(1),
    )(x, o, rest, rest, rest, rest, rest, mod, conv_w, w_attn, w_conv, w_out)


def _mix_bwd(dxo, ya, yc, conv, rest, mod, conv_w, w_attn, w_conv, w_out, a_w, name):
    s_len, d = dxo.shape
    tm = MIX_TILE
    n_tiles = s_len // tm

    def body(dxo_ref, ya_ref, yc_ref, conv_ref, u_ref, b_ref, c_ref, ga_ref, gc_ref, mod_ref, cw_ref,
             wa_ref, wc_ref, wo_ref, do_ref, drest_ref, dz_ref, dya_ref, dyc_ref, st_ref, carry):
        @pl.when(pl.program_id(0) == 0)
        def _():
            carry[...] = jnp.zeros_like(carry)
            st_ref[...] = jnp.zeros_like(st_ref)

        dz = (mod_ref[2:3, :] * dxo_ref[...]).astype(BF16)
        dz_ref[...] = dz
        dm = _dot_nt(dz, wo_ref[...])
        sa, sc = _sigmoid(ga_ref[...]), _sigmoid(gc_ref[...])
        dya = (dm * sa).astype(BF16)
        dyc = (dm * sc).astype(BF16)
        dya_ref[...] = dya
        dyc_ref[...] = dyc
        drest_ref[:, 3 * d:4 * d] = (dm * ya_ref[...].astype(F32) * (sa * (1.0 - sa))).astype(BF16)
        drest_ref[:, 4 * d:5 * d] = (dm * yc_ref[...].astype(F32) * (sc * (1.0 - sc))).astype(BF16)
        do_ref[...] = _dot_nt(dya, wa_ref[...])
        dyb = _dot_nt(dyc, wc_ref[...])
        drest_ref[:, d:2 * d] = (dyb * conv_ref[...].astype(F32)).astype(BF16)
        dconv = dyb * b_ref[...]
        edge = carry[...]
        sh1 = _shift_rows(dconv, 1, edge, False)
        sh2 = _shift_rows(dconv, 2, edge, False)
        carry[...] = dconv[0:8, :]
        dxc = dconv * cw_ref[2:3, :] + sh1 * cw_ref[1:2, :] + sh2 * cw_ref[0:1, :]
        u, c = u_ref[...], c_ref[...]
        xc = c * u
        drest_ref[:, 0:d] = (dxc * c).astype(BF16)
        drest_ref[:, 2 * d:3 * d] = (dxc * u).astype(BF16)
        st_ref[0:1, :] += jnp.sum(xc * sh2, axis=0, keepdims=True)
        st_ref[1:2, :] += jnp.sum(xc * sh1, axis=0, keepdims=True)
        st_ref[2:3, :] += jnp.sum(xc * dconv, axis=0, keepdims=True)

    rev = lambda i: n_tiles - 1 - i
    tile = pl.BlockSpec((tm, d), lambda i: (rev(i), 0))
    sect = lambda k: pl.BlockSpec((tm, d), lambda i: (rev(i), k))
    const = lambda shape: pl.BlockSpec(shape, lambda i: (0, 0))
    b16_out = jax.ShapeDtypeStruct((s_len, d), BF16)
    return pl.pallas_call(
        body, name=name, grid=(n_tiles,),
        in_specs=[tile, tile, tile, tile, sect(0), sect(1), sect(2), sect(3), sect(4), const((8, d)), const((8, d)),
                  const((a_w, d)), const((d, d)), const((d, d))],
        out_specs=[pl.BlockSpec((tm, a_w), lambda i: (rev(i), 0)), pl.BlockSpec((tm, 5 * d), lambda i: (rev(i), 0)),
                   tile, tile, tile, const((8, d))],
        out_shape=[jax.ShapeDtypeStruct((s_len, a_w), F32), jax.ShapeDtypeStruct((s_len, 5 * d), BF16),
                   b16_out, b16_out, b16_out, jax.ShapeDtypeStruct((8, d), F32)],
        scratch_shapes=[pltpu.VMEM((8, d), F32)],
        compiler_params=_params(1),
    )(dxo, ya, yc, conv, rest, rest, rest, rest, rest, mod, conv_w, w_attn, w_conv, w_out)


ADA_COLS = 128


def _ada_fwd(c_all, w_shard, b_shard, name):
    d, cols = w_shard.shape

    def body(c_ref, w_ref, b_ref, o_ref):
        cv = c_ref[...]
        o_ref[...] = jnp.dot(cv * _sigmoid(cv), w_ref[...], preferred_element_type=F32,
                             precision=lax.Precision.HIGHEST) + b_ref[...]

    return pl.pallas_call(
        body, name=name, grid=(cols // ADA_COLS,),
        in_specs=[pl.BlockSpec((8, d), lambda j: (0, 0)), pl.BlockSpec((d, ADA_COLS), lambda j: (0, j)),
                  pl.BlockSpec((1, ADA_COLS), lambda j: (0, j))],
        out_specs=pl.BlockSpec((8, ADA_COLS), lambda j: (0, j)),
        out_shape=jax.ShapeDtypeStruct((8, cols), F32),
        compiler_params=_params(1),
    )(c_all, w_shard, b_shard)


def _ada_bwd(c_all, dmod_shard, w, m, v, name):
    d, cols = w.shape

    def body(c_ref, dm_ref, w_ref, m_ref, v_ref, g_ref, d_ref, nm_ref, nv_ref):
        cv = c_ref[...]
        g = lax.dot_general(cv * _sigmoid(cv), dm_ref[...], (((0,), (0,)), ((), ())),
                            preferred_element_type=F32, precision=lax.Precision.HIGHEST)
        g_ref[...] = g
        d_ref[...], nm_ref[...], nv_ref[...] = _adamw_math(w_ref[...], g, m_ref[...], v_ref[...])

    blk = pl.BlockSpec((d, ADA_COLS), lambda j: (0, j))
    shape = jax.ShapeDtypeStruct((d, cols), F32)
    return pl.pallas_call(
        body, name=name, grid=(cols // ADA_COLS,),
        in_specs=[pl.BlockSpec((8, d), lambda j: (0, 0)), pl.BlockSpec((8, ADA_COLS), lambda j: (0, j)), blk, blk, blk],
        out_specs=[blk] * 4, out_shape=[shape] * 4,
        compiler_params=_params(1),
    )(c_all, dmod_shard, w, m, v)


def _small_update(parts, w, m, v, name):
    n = w.shape[1]

    def body(p_ref, w_ref, m_ref, v_ref, g_ref, d_ref, nm_ref, nv_ref):
        g = p_ref[0:1, :]
        for i in range(1, 8):
            g = g + p_ref[i:i + 1, :]
        g_ref[...] = g
        d_ref[...], nm_ref[...], nv_ref[...] = _adamw_math(w_ref[...], g, m_ref[...], v_ref[...])

    shape = jax.ShapeDtypeStruct((1, n), F32)
    return pl.pallas_call(body, name=name, out_shape=[shape] * 4, compiler_params=_params())(parts, w, m, v)


def _cols_to_shards(w, n):
    r, nc = w.shape
    return w.reshape(r, n, nc // n).transpose(1, 0, 2)


def kernel(x, c, w_ada, b_ada, norm_ffn1, ffn1_w_gate, ffn1_w_up, ffn1_w_down, norm_mix, w_in, q_norm, k_norm, conv_w, w_attn_branch, w_conv_branch, w_out, norm_ffn2, ffn2_w_gate, ffn2_w_up, ffn2_w_down, loss_target, m_w_ada, m_b_ada, m_norm_ffn1, m_ffn1_w_gate, m_ffn1_w_up, m_ffn1_w_down, m_norm_mix, m_w_in, m_q_norm, m_k_norm, m_conv_w, m_w_attn_branch, m_w_conv_branch, m_w_out, m_norm_ffn2, m_ffn2_w_gate, m_ffn2_w_up, m_ffn2_w_down, v_w_ada, v_b_ada, v_norm_ffn1, v_ffn1_w_gate, v_ffn1_w_up, v_ffn1_w_down, v_norm_mix, v_w_in, v_q_norm, v_k_norm, v_conv_w, v_w_attn_branch, v_w_conv_branch, v_w_out, v_norm_ffn2, v_ffn2_w_gate, v_ffn2_w_up, v_ffn2_w_down):
    ix, iy, ic = _place()
    chip = 2 * ix + iy
    me = 4 * ix + 2 * iy + ic
    xs = x[0]
    target = loss_target[0]
    s_len, d = xs.shape
    ada_cols = w_ada.shape[2]
    conv_cols = conv_w.shape[2]

    conv_rows = jnp.zeros((8, conv_cols), F32).at[0:3].set(conv_w[0])
    small_in = jnp.concatenate([jnp.broadcast_to(c, (8, d)), conv_rows], axis=1)
    small_all = _allgather8(small_in, "gather_c").reshape(8, 8, d + conv_cols)
    c_all = small_all[:, 0, :d]
    conv_full = small_all[0::2, 0:3, d:].transpose(1, 0, 2).reshape(3, N_CHIPS * conv_cols)
    conv_pad = jnp.zeros((8, N_CHIPS * conv_cols), F32).at[0:3].set(conv_full)
    b_shard = lax.dynamic_slice(b_ada, (0, chip * ada_cols), (1, ada_cols))
    mod_part = _ada_fwd(c_all, w_ada[0], b_shard, "ada_fwd")
    mod_all = _allgather8(mod_part, "gather_mod").reshape(N_CHIPS, 2, 8, ada_cols)[:, 0]
    mod_mine = lax.dynamic_slice(mod_all, (0, me, 0), (N_CHIPS, 1, ada_cols)).reshape(9, d)

    def mod_rows(i, gain):
        return jnp.zeros((8, d), F32).at[0:3].set(mod_mine[3 * i:3 * i + 3]).at[3:4].set(gain)

    mod1, mod2, mod3 = mod_rows(0, norm_ffn1), mod_rows(1, norm_mix), mod_rows(2, norm_ffn2)

    to16 = lambda w: w[0].astype(BF16)
    wg1, wu1, wd1 = _gather_weights([to16(ffn1_w_gate), to16(ffn1_w_up), to16(ffn1_w_down)], [False] * 3,
                                    "gather_ffn1", 1)
    h1, h1t = _norm_mod(xs, mod1, "norm1")
    (w_in_full,) = _gather_weights([to16(w_in)], [True], "gather_w_in", 2, after=(wd1, h1))

    g1, u1, y1 = _ffn_fwd(h1, wg1, wu1, wd1, "ffn1_fwd")
    x1, h2, h2t = _norm_mod(xs, mod2, "norm2", prev=(y1, mod1, 0.5))
    qkv, rest, qkv_hat = _in_proj(h2, w_in_full, q_norm, k_norm, "in_proj")
    w_ab, w_cb_g, w_o_g, wg2, wu2, wd2 = _gather_weights(
        [to16(w_attn_branch), to16(w_conv_branch), to16(w_out),
         to16(ffn2_w_gate), to16(ffn2_w_up), to16(ffn2_w_down)], [True] + [False] * 5,
        "gather_rest", 3, after=(qkv,))
    a_w = w_ab.shape[0]
    w_cb = w_cb_g.reshape(d, d)
    w_o = w_o_g.reshape(d, d)
    o, lse = _attn_fwd(qkv_hat, "attn_fwd")
    x2, z, ya, yc, conv, yb, merged, o16 = _mix_fwd(x1, o, rest, mod2, conv_pad, w_ab, w_cb, w_o, "mix_fwd")
    h3, h3t = _norm_mod(x2, mod3, "norm3")
    g3, u3, y3 = _ffn_fwd(h3, wg2, wu2, wd2, "ffn2_fwd")
    dx3, dy3, loss_part = _loss_grad(x2, y3, mod3, target, "loss")
    loss = lax.psum(0.5 * jnp.sum(loss_part) / d, ("x", "y", "c"))

    c_idx = jnp.reshape(ic, (1,)).astype(jnp.int32)
    chip_idx = jnp.stack([chip, ic]).astype(jnp.int32)

    def reduce_start(grads, names, tag, collective_id):
        from_sibling = _rs_pair_exchange(grads, "rs_pair_" + tag)
        pair_sums = [_pair_add(g, r, c_idx, "pair_add_" + nm) for g, r, nm in zip(grads, from_sibling, names)]
        return pair_sums, _rs_chip_exchange(pair_sums, "rs_chips_" + tag, collective_id)

    def reduce_finish(pair_sums, from_chips, names, tag, after):
        totals = [_chip_add(p, r, chip_idx, "chip_add_" + nm, after)
                  for p, r, nm in zip(pair_sums, from_chips, names)]
        return dict(zip(names, _rs_share(totals, "rs_share_" + tag)))

    names_a = ["ffn2_w_gate", "ffn2_w_up", "ffn2_w_down"]
    names_b = ["w_in", "w_attn_branch", "w_conv_branch", "w_out"]
    names_c = ["ffn1_w_gate", "ffn1_w_up", "ffn1_w_down"]

    dh3, dg3, du3, a3 = _ffn_bwd(dy3, g3, u3, wg2, wu2, wd2, "ffn2_bwd")
    sums_a, chips_a = reduce_start(list(_ffn_wgrads(h3t, dg3, du3, a3, dy3, "ffn2")), names_a, "a", 4)
    dx2, st3 = _norm_bwd(dh3, x2, mod3, dx3, y3, 0.5, "norm3_bwd", after=tuple(sums_a))

    do, drest, dz, dya, dyc, st_conv = _mix_bwd(dx2, ya, yc, conv, rest, mod2, conv_pad, w_ab, w_cb, w_o, a_w, "mix_bwd")
    dq, dk, dv, st_qk = _attn_bwd(qkv, qkv_hat, do, o, lse, q_norm, k_norm, "attn_bwd")
    tok = lambda width: (lambda ts: pl.BlockSpec((ts, width), lambda cc, s: (s, 0)))
    colblk = lambda width: (lambda ts: pl.BlockSpec((ts, width), lambda cc, s: (s, cc)))
    tok_t = lambda ts: pl.BlockSpec((d, ts), lambda cc, s: (0, s))
    whole = pl.BlockSpec((d, QKV), lambda cc, s: (0, 0))
    dw_in = [_wgrad(h2t, part, tok_t, tok(QKV), (d, QKV), whole, (d, QKV), 1, "dw_in_" + nm, True)
             for part, nm in ((dq, "q"), (dk, "k"), (dv, "v"))]
    dw_in.append(_wgrad(h2t, drest, tok_t, colblk(d), (d, 5 * d), pl.BlockSpec((d, d), lambda cc, s: (0, cc)),
                        (d, d), 5, "dw_in_rest", True))
    dw_in = _cols_to_shards(jnp.concatenate(dw_in, axis=1), N_CHIPS)
    shard_w = d // N_CHIPS
    dw_ab = _wgrad(o16, dya, tok(a_w), colblk(shard_w), (a_w, d), pl.BlockSpec((a_w, shard_w), lambda cc, s: (0, cc)),
                   (a_w, shard_w), N_CHIPS, "dw_attn_branch")
    dw_ab = _cols_to_shards(dw_ab, N_CHIPS)
    row_out = pl.BlockSpec((None, shard_w, d), lambda cc, s: (cc, 0, 0))
    dw_cb = _wgrad(yb, dyc, colblk(shard_w), tok(d), (N_CHIPS, shard_w, d), row_out, (shard_w, d), N_CHIPS, "dw_conv_branch")
    dw_o = _wgrad(merged, dz, colblk(shard_w), tok(d), (N_CHIPS, shard_w, d), row_out, (shard_w, d), N_CHIPS, "dw_out")
    shard_grads = reduce_finish(sums_a, chips_a, names_a, "a", after=(dw_in, dw_o))
    sums_b, chips_b = reduce_start([dw_in, dw_ab, dw_cb, dw_o], names_b, "b", 5)

    dh2 = _in_proj_bwd(dq, dk, dv, drest, w_in_full, "in_proj_bwd", after=tuple(sums_b))
    dx1, st2, dy1 = _norm_bwd(dh2, x1, mod2, dx2, z, 1.0, "norm2_bwd", prev=(mod1, 0.5))
    dh1, dg1, du1, a1 = _ffn_bwd(dy1, g1, u1, wg1, wu1, wd1, "ffn1_bwd")
    dx0, st1 = _norm_bwd(dh1, xs, mod1, dx1, y1, 0.5, "norm1_bwd")
    grads_c = list(_ffn_wgrads(h1t, dg1, du1, a1, dy1, "ffn1"))
    shard_grads.update(reduce_finish(sums_b, chips_b, names_b, "b", after=tuple(grads_c)))
    sums_c, chips_c = reduce_start(grads_c, names_c, "c", 6)

    dmod = jnp.concatenate([st1[0:3], st2[0:3], st3[0:3]], axis=0).reshape(1, 9 * d)
    small = jnp.concatenate([dmod, st1[3:4], st2[3:4], st3[3:4], st_qk[0:1], st_qk[1:2],
                             st_conv[0:3].reshape(1, 3 * d)], axis=1)
    small_all = _allgather8(jnp.broadcast_to(small, (8, small.shape[1])), "gather_small").reshape(8, 8, -1)[:, 0]
    dmod_all = small_all[:, :9 * d]
    dmod_shard = lax.dynamic_slice(dmod_all, (0, chip * ada_cols), (8, ada_cols))
    g_w_ada, d_w_ada, nm_w_ada, nv_w_ada = _ada_bwd(c_all, dmod_shard, w_ada[0], m_w_ada[0], v_w_ada[0], "ada_bwd")

    vec_names = ["b_ada", "norm_ffn1", "norm_mix", "norm_ffn2", "q_norm", "k_norm"]
    vec_w = [b_ada, norm_ffn1, norm_mix, norm_ffn2, q_norm, k_norm]
    vec_m = [m_b_ada, m_norm_ffn1, m_norm_mix, m_norm_ffn2, m_q_norm, m_k_norm]
    vec_v = [v_b_ada, v_norm_ffn1, v_norm_mix, v_norm_ffn2, v_q_norm, v_k_norm]
    n_vec = sum(w.shape[1] for w in vec_w)
    cat = lambda arrs: jnp.concatenate(arrs, axis=1)
    vec_out = _small_update(small_all[:, :n_vec], cat(vec_w), cat(vec_m), cat(vec_v), "small_update")
    conv_parts = small_all[:, n_vec:].reshape(8, 3, N_CHIPS * conv_cols)
    conv_parts = lax.dynamic_slice(conv_parts, (0, 0, chip * conv_cols), (8, 3, conv_cols)).reshape(8, 3 * conv_cols)
    flat3 = lambda w: w[0].reshape(1, 3 * conv_cols)
    conv_out = _small_update(conv_parts, flat3(conv_w), flat3(m_conv_w), flat3(v_conv_w), "conv_update")

    res = {"w_ada": [t[None] for t in (g_w_ada, d_w_ada, nm_w_ada, nv_w_ada)],
           "conv_w": [t.reshape(1, 3, conv_cols) for t in conv_out]}
    off = 0
    for nm, w in zip(vec_names, vec_w):
        width = w.shape[1]
        res[nm] = [t[:, off:off + width] for t in vec_out]
        off += width
    big = {"ffn1_w_gate": (ffn1_w_gate, m_ffn1_w_gate, v_ffn1_w_gate), "ffn1_w_up": (ffn1_w_up, m_ffn1_w_up, v_ffn1_w_up),
           "ffn1_w_down": (ffn1_w_down, m_ffn1_w_down, v_ffn1_w_down), "w_in": (w_in, m_w_in, v_w_in),
           "w_attn_branch": (w_attn_branch, m_w_attn_branch, v_w_attn_branch),
           "w_conv_branch": (w_conv_branch, m_w_conv_branch, v_w_conv_branch), "w_out": (w_out, m_w_out, v_w_out),
           "ffn2_w_gate": (ffn2_w_gate, m_ffn2_w_gate, v_ffn2_w_gate), "ffn2_w_up": (ffn2_w_up, m_ffn2_w_up, v_ffn2_w_up),
           "ffn2_w_down": (ffn2_w_down, m_ffn2_w_down, v_ffn2_w_down)}
    def update(nm, after=()):
        w, m, v = big[nm]
        g = shard_grads[nm]
        delta, new_m, new_v = _adamw(w[0], g, m[0], v[0], "adamw_" + nm, after)
        res[nm] = [t[None] for t in (g, delta, new_m, new_v)]
        return new_v

    last = tuple(sums_c)
    for nm in names_a + names_b:
        last = (update(nm, last),)
    shard_grads.update(reduce_finish(sums_c, chips_c, names_c, "c", after=last))
    for nm in names_c:
        update(nm)

    order = ["w_ada", "b_ada", "norm_ffn1", "ffn1_w_gate", "ffn1_w_up", "ffn1_w_down", "norm_mix", "w_in", "q_norm",
             "k_norm", "conv_w", "w_attn_branch", "w_conv_branch", "w_out", "norm_ffn2", "ffn2_w_gate", "ffn2_w_up",
             "ffn2_w_down"]
    return (loss, dx0[None], *[res[nm][0] for nm in order], *[res[nm][1] for nm in order],
            *[res[nm][2] for nm in order], *[res[nm][3] for nm in order])
```

```python
import jax
import jax.numpy as jnp
from jax import lax
from jax.experimental import pallas as pl
from jax.experimental.pallas import tpu as pltpu
from jax.experimental.pallas import tpu_sc as plsc

F32 = jnp.float32
BF16 = jnp.bfloat16
MESH = pl.DeviceIdType.MESH
ANY = pl.BlockSpec(memory_space=pl.ANY)

NORM_EPS = 1e-6
HEAD_DIM = 128
N_GROUPS = 3
HEADS = 4
DILATIONS = (1, 4, 16)
ATTN_BLOCK = 128
SLAB = ATTN_BLOCK * max(DILATIONS)
QKV = N_GROUPS * HEADS * HEAD_DIM
ATTN_SCALE = HEAD_DIM ** -0.5
NEG = -1e30
N_CHIPS = 4

ADAM_LR = 0.001
ADAM_B1 = 0.9
ADAM_B2 = 0.999
ADAM_EPS = 1e-08
ADAM_WD = 0.01
ADAM_STEP = 10

VMEM_LIMIT_BYTES = 56 * 1024 * 1024
TOKEN_TILE = 512
FFN_TILE = 1024
PROJ_TILE = 2048
WGRAD_TILE = 2048
IN_BLOCK = 512
MIX_TILE = 256


def _params(n_axes=0):
    return pltpu.CompilerParams(
        dimension_semantics=("arbitrary",) * n_axes if n_axes else None,
        vmem_limit_bytes=VMEM_LIMIT_BYTES)


def _dot(a, b):
    return jnp.dot(a, b, preferred_element_type=F32)


def _dot_nt(a, b):
    return lax.dot_general(a, b, (((1,), (1,)), ((), ())), preferred_element_type=F32)


def _dot_tn(a, b):
    return lax.dot_general(a, b, (((0,), (0,)), ((), ())), preferred_element_type=F32)


def _sigmoid(x):
    return 1.0 / (1.0 + jnp.exp(-x))


def _place():
    return lax.axis_index("x"), lax.axis_index("y"), lax.axis_index("c")


def _ordered(body, n_in, after):
    if not after:
        return body
    return lambda *refs: body(*refs[:n_in], *refs[n_in + len(after):])


def _allgather8(block, name):
    m_per, n = block.shape

    def body(x_ref, out_ref, send_sems, recv_sems, local_sem):
        x, y, c = _place()
        me, sibling = (x, y, c), (x, y, 1 - c)
        chips = [(1 - x, y), (x, 1 - y), (1 - x, 1 - y)]

        def rows(px, py, pc):
            return out_ref.at[pl.ds((4 * px + 2 * py + pc) * m_per, m_per), :]

        def copy(k, blk, to, src=None):
            return pltpu.make_async_remote_copy(
                src_ref=rows(*blk) if src is None else src, dst_ref=rows(*blk),
                send_sem=send_sems.at[k], recv_sem=recv_sems.at[k],
                device_id=to, device_id_type=MESH)

        mine = pltpu.make_async_copy(x_ref, rows(*me), local_sem)
        mine.start()
        first = [copy(0, me, sibling, src=x_ref)]
        first += [copy(1 + j, me, (*chip, c), src=x_ref) for j, chip in enumerate(chips)]
        for cp in first:
            cp.start()
        passed = [copy(4 + j, (*chip, c), sibling) for j, chip in enumerate(chips)]
        for j, chip in enumerate(chips):
            copy(1 + j, (*chip, c), me).wait_recv()
            passed[j].start()
        copy(0, sibling, me).wait_recv()
        for j, chip in enumerate(chips):
            copy(4 + j, (*chip, 1 - c), me).wait_recv()
        for cp in first + passed:
            cp.wait_send()
        mine.wait()

    return pl.pallas_call(
        body, name=name,
        out_shape=jax.ShapeDtypeStruct((8 * m_per, n), block.dtype),
        in_specs=[pl.BlockSpec(memory_space=pltpu.VMEM)],
        out_specs=pl.BlockSpec(memory_space=pltpu.VMEM),
        scratch_shapes=[pltpu.SemaphoreType.DMA((7,)), pltpu.SemaphoreType.DMA((7,)),
                        pltpu.SemaphoreType.DMA],
        compiler_params=_params(),
    )(block)


def _handshake(peers):
    barrier = pltpu.get_barrier_semaphore()
    for peer in peers:
        pl.semaphore_signal(barrier, inc=1, device_id=peer, device_id_type=MESH)
    pl.semaphore_wait(barrier, len(peers))


def _gather_weights(shards, by_cols, name, collective_id, after=()):
    n_arr = len(shards)

    def body(*refs):
        srcs, outs = refs[:n_arr], refs[n_arr + len(after):2 * n_arr + len(after)]
        send_sems, recv_sems, local_sems = refs[2 * n_arr + len(after):]
        x, y, c = _place()
        me_dev, sibling = (x, y, c), (x, y, 1 - c)
        chips = [(1 - x, y), (x, 1 - y), (1 - x, 1 - y)]
        me = 2 * x + y
        _handshake([sibling] + [(*chip, c) for chip in chips])

        def place(k, chip_idx, rows):
            if by_cols[k]:
                width = srcs[k].shape[1]
                return outs[k].at[rows, pl.ds(pl.multiple_of(chip_idx * width, 128), width)]
            return outs[k].at[chip_idx, rows]

        def copy(k, slot, chip_idx, half_sel, to, from_shard=False):
            half = srcs[k].shape[0] // 2
            rows = pl.ds(half_sel * half, half)
            dst = place(k, chip_idx, rows)
            return pltpu.make_async_remote_copy(
                src_ref=srcs[k].at[rows] if from_shard else dst, dst_ref=dst,
                send_sem=send_sems.at[6 * k + slot], recv_sem=recv_sems.at[6 * k + slot],
                device_id=to, device_id_type=MESH)

        own = [pltpu.make_async_copy(srcs[k], place(k, me, pl.ds(0, srcs[k].shape[0])), local_sems.at[k])
               for k in range(n_arr)]
        for cp in own:
            cp.start()
        sent = []
        for k in range(n_arr):
            for j, chip in enumerate(chips):
                sent.append(copy(k, j, me, c, (*chip, c), from_shard=True))
                sent[-1].start()
        for k in range(n_arr):
            for j, chip in enumerate(chips):
                chip_idx = 2 * chip[0] + chip[1]
                copy(k, j, chip_idx, c, me_dev).wait_recv()
                sent.append(copy(k, 3 + j, chip_idx, c, sibling))
                sent[-1].start()
        for k in range(n_arr):
            for j, chip in enumerate(chips):
                copy(k, 3 + j, 2 * chip[0] + chip[1], 1 - c, me_dev).wait_recv()
        for cp in sent:
            cp.wait_send()
        for cp in own:
            cp.wait()

    def gathered(k):
        r, cols = shards[k].shape
        return (r, N_CHIPS * cols) if by_cols[k] else (N_CHIPS, r, cols)

    return pl.kernel(
        body, name=name,
        out_type=[jax.ShapeDtypeStruct(gathered(k), shards[k].dtype) for k in range(n_arr)],
        mesh=plsc.ScalarSubcoreMesh(axis_name="sequencer", num_cores=1),
        scratch_types=[pltpu.SemaphoreType.DMA((6 * n_arr,)), pltpu.SemaphoreType.DMA((6 * n_arr,)),
                       pltpu.SemaphoreType.DMA((n_arr,))],
        compiler_params=pltpu.CompilerParams(collective_id=collective_id),
    )(*shards, *after)


def _rs_pair_exchange(grads, name):
    n_arr = len(grads)

    def body(*refs):
        srcs, outs = refs[:n_arr], refs[n_arr:2 * n_arr]
        send_sems, recv_sems = refs[2 * n_arr:]
        x, y, c = _place()
        cps = []
        for k in range(n_arr):
            half = srcs[k].shape[1] // 2
            cps.append(pltpu.make_async_remote_copy(
                src_ref=srcs[k].at[:, pl.ds((1 - c) * half, half)], dst_ref=outs[k],
                send_sem=send_sems.at[k], recv_sem=recv_sems.at[k],
                device_id=(x, y, 1 - c), device_id_type=MESH))
            cps[-1].start()
        for cp in cps:
            cp.wait_recv()
        for cp in cps:
            cp.wait_send()

    return pl.pallas_call(
        body, name=name,
        out_shape=[jax.ShapeDtypeStruct((g.shape[0], g.shape[1] // 2, g.shape[2]), g.dtype) for g in grads],
        in_specs=[ANY] * n_arr, out_specs=[ANY] * n_arr,
        scratch_shapes=[pltpu.SemaphoreType.DMA((n_arr,)), pltpu.SemaphoreType.DMA((n_arr,))],
        compiler_params=_params(),
    )(*grads)


def _rs_chip_exchange(sums, name, collective_id):
    n_arr = len(sums)

    def body(*refs):
        srcs, outs = refs[:n_arr], refs[n_arr:2 * n_arr]
        send_sems, recv_sems = refs[2 * n_arr:]
        x, y, c = _place()
        chips = [(1 - x, y), (x, 1 - y), (1 - x, 1 - y)]
        _handshake([(*chip, c) for chip in chips])
        cps = []
        for k in range(n_arr):
            for j, chip in enumerate(chips):
                cps.append(pltpu.make_async_remote_copy(
                    src_ref=srcs[k].at[2 * chip[0] + chip[1]], dst_ref=outs[k].at[j],
                    send_sem=send_sems.at[3 * k + j], recv_sem=recv_sems.at[3 * k + j],
                    device_id=(*chip, c), device_id_type=MESH))
                cps[-1].start()
        for cp in cps:
            cp.wait_recv()
        for cp in cps:
            cp.wait_send()

    return pl.kernel(
        body, name=name,
        out_type=[jax.ShapeDtypeStruct((3,) + s.shape[1:], s.dtype) for s in sums],
        mesh=plsc.ScalarSubcoreMesh(axis_name="sequencer", num_cores=1),
        scratch_types=[pltpu.SemaphoreType.DMA((3 * n_arr,)), pltpu.SemaphoreType.DMA((3 * n_arr,))],
        compiler_params=pltpu.CompilerParams(collective_id=collective_id),
    )(*sums)


def _rs_share(totals, name):
    n_arr = len(totals)

    def body(*refs):
        outs = refs[n_arr:2 * n_arr]
        send_sems, recv_sems = refs[2 * n_arr:]
        x, y, c = _place()

        def half_rows(k, sel):
            return outs[k].at[sel]

        cps = []
        for k in range(n_arr):
            cps.append(pltpu.make_async_remote_copy(
                src_ref=half_rows(k, c), dst_ref=half_rows(k, c), send_sem=send_sems.at[k], recv_sem=recv_sems.at[k],
                device_id=(x, y, 1 - c), device_id_type=MESH))
            cps[-1].start()
        for k in range(n_arr):
            pltpu.make_async_remote_copy(
                src_ref=half_rows(k, c), dst_ref=half_rows(k, 1 - c), send_sem=send_sems.at[k],
                recv_sem=recv_sems.at[k], device_id=(x, y, 1 - c), device_id_type=MESH).wait_recv()
        for cp in cps:
            cp.wait_send()

    shared = pl.pallas_call(
        body, name=name,
        out_shape=[jax.ShapeDtypeStruct(t.shape, t.dtype) for t in totals],
        in_specs=[ANY] * n_arr, out_specs=[ANY] * n_arr,
        input_output_aliases={k: k for k in range(n_arr)},
        scratch_shapes=[pltpu.SemaphoreType.DMA((n_arr,)), pltpu.SemaphoreType.DMA((n_arr,))],
        compiler_params=_params(),
    )(*totals)
    return [t.reshape(2 * t.shape[1], t.shape[2]) for t in shared]


def _pair_add(grad, recv, c_idx, name):
    n, r, cols = grad.shape
    half = r // 2
    rows = half // 2

    def body(_, g_ref, r_ref, o_ref):
        o_ref[...] = (g_ref[...].astype(F32) + r_ref[...].astype(F32)).astype(o_ref.dtype)

    return pl.pallas_call(
        body, name=name,
        grid_spec=pltpu.PrefetchScalarGridSpec(
            num_scalar_prefetch=1, grid=(n, 2),
            in_specs=[pl.BlockSpec((None, None, rows, cols), lambda s, i, ci: (s, ci[0], i, 0)),
                      pl.BlockSpec((None, rows, cols), lambda s, i, ci: (s, i, 0))],
            out_specs=pl.BlockSpec((None, rows, cols), lambda s, i, ci: (s, i, 0))),
        out_shape=jax.ShapeDtypeStruct((n, half, cols), BF16),
        compiler_params=_params(2),
    )(c_idx, grad.reshape(n, 2, half, cols), recv)


def _chip_add(sums, recv, chip_and_core, name, after=()):
    _, half, cols = sums.shape
    rows = half // 2

    def body(_, s_ref, r0_ref, r1_ref, r2_ref, o_ref):
        o_ref[...] = ((s_ref[...].astype(F32) + r0_ref[...].astype(F32))
                      + r1_ref[...].astype(F32)) + r2_ref[...].astype(F32)

    def recv_spec(j):
        return pl.BlockSpec((None, rows, cols), lambda i, ci: (j, i, 0))

    return pl.pallas_call(
        _ordered(body, 5, after), name=name,
        grid_spec=pltpu.PrefetchScalarGridSpec(
            num_scalar_prefetch=1, grid=(2,),
            in_specs=[pl.BlockSpec((None, rows, cols), lambda i, ci: (ci[0], i, 0)),
                      recv_spec(0), recv_spec(1), recv_spec(2)] + [ANY] * len(after),
            out_specs=pl.BlockSpec((None, rows, cols), lambda i, ci: (ci[1], i, 0))),
        out_shape=jax.ShapeDtypeStruct((2, half, cols), F32),
        compiler_params=_params(1),
    )(chip_and_core, sums, recv, recv, recv, *after)


def _rms(x):
    return lax.rsqrt(jnp.mean(x * x, axis=-1, keepdims=True) + NORM_EPS)


def _norm_mod(x, mod, name, prev=None):
    s_len, d = x.shape
    tm = TOKEN_TILE

    def body(*refs):
        if prev is None:
            x_ref, mod_ref, h_ref, ht_ref = refs
            xv = x_ref[...]
        else:
            x_ref, y_ref, modp_ref, mod_ref, xo_ref, h_ref, ht_ref = refs
            xv = x_ref[...] + prev[2] * modp_ref[2:3, :] * y_ref[...]
            xo_ref[...] = xv
        n = (xv * _rms(xv)) * mod_ref[3:4, :]
        h = n * (1.0 + mod_ref[1:2, :]) + mod_ref[0:1, :]
        h_ref[...] = h.astype(BF16)
        ht_ref[...] = h.T.astype(BF16)

    tile = pl.BlockSpec((tm, d), lambda i: (i, 0))
    small = pl.BlockSpec((8, d), lambda i: (0, 0))
    h_specs = [tile, pl.BlockSpec((d, tm), lambda i: (0, i))]
    h_shapes = [jax.ShapeDtypeStruct((s_len, d), BF16), jax.ShapeDtypeStruct((d, s_len), BF16)]
    if prev is None:
        return pl.pallas_call(
            body, name=name, grid=(s_len // tm,), in_specs=[tile, small], out_specs=h_specs, out_shape=h_shapes,
            compiler_params=_params(1))(x, mod)
    return pl.pallas_call(
        body, name=name, grid=(s_len // tm,), in_specs=[tile, tile, small, small],
        out_specs=[tile] + h_specs, out_shape=[jax.ShapeDtypeStruct((s_len, d), F32)] + h_shapes,
        compiler_params=_params(1))(x, prev[0], prev[1], mod)


def _norm_bwd(dh, x, mod, dxo, y_raw, coef, name, after=(), prev=None):
    s_len, d = x.shape
    tm = TOKEN_TILE

    def body(*refs):
        if prev is None:
            dh_ref, x_ref, mod_ref, dxo_ref, y_ref, dx_ref, st_ref = refs
        else:
            dh_ref, x_ref, mod_ref, dxo_ref, y_ref, modp_ref, dx_ref, st_ref, dyp_ref = refs

        @pl.when(pl.program_id(0) == 0)
        def _():
            st_ref[...] = jnp.zeros_like(st_ref)

        xv, dhv, dxov = x_ref[...], dh_ref[...], dxo_ref[...]
        r = _rms(xv)
        xh = xv * r
        gain, scale = mod_ref[3:4, :], mod_ref[1:2, :]
        dn = dhv * (1.0 + scale)
        dxh = dn * gain
        dx = dxov + r * (dxh - xh * jnp.mean(dxh * xh, axis=-1, keepdims=True))
        dx_ref[...] = dx
        if prev is not None:
            dyp_ref[...] = (prev[1] * modp_ref[2:3, :] * dx).astype(BF16)
        st_ref[0:1, :] += jnp.sum(dhv, axis=0, keepdims=True)
        st_ref[1:2, :] += jnp.sum(dhv * (xh * gain), axis=0, keepdims=True)
        st_ref[2:3, :] += coef * jnp.sum(y_ref[...] * dxov, axis=0, keepdims=True)
        st_ref[3:4, :] += jnp.sum(dn * xh, axis=0, keepdims=True)

    tile = pl.BlockSpec((tm, d), lambda i: (i, 0))
    small = pl.BlockSpec((8, d), lambda i: (0, 0))
    operands = [dh, x, mod, dxo, y_raw] + ([] if prev is None else [prev[0]])
    in_specs = [tile, tile, small, tile, tile] + ([] if prev is None else [small])
    out_specs = [tile, small] + ([] if prev is None else [tile])
    out_shape = [jax.ShapeDtypeStruct((s_len, d), F32), jax.ShapeDtypeStruct((8, d), F32)]
    if prev is not None:
        out_shape.append(jax.ShapeDtypeStruct((s_len, d), BF16))
    return pl.pallas_call(
        _ordered(body, len(operands), after), name=name, grid=(s_len // tm,),
        in_specs=in_specs + [ANY] * len(after), out_specs=out_specs, out_shape=out_shape,
        compiler_params=_params(1),
    )(*operands, *after)


def _loss_grad(x, y, mod, target, name):
    s_len, d = x.shape
    tm = TOKEN_TILE

    def body(x_ref, y_ref, mod_ref, t_ref, do_ref, dy_ref, part_ref):
        @pl.when(pl.program_id(0) == 0)
        def _():
            part_ref[...] = jnp.zeros_like(part_ref)

        half_gate = 0.5 * mod_ref[2:3, :]
        err = (x_ref[...] + half_gate * y_ref[...]) - t_ref[...]
        do = err * (1.0 / d)
        do_ref[...] = do
        dy_ref[...] = (half_gate * do).astype(BF16)
        sq = err * err
        part_ref[...] += jnp.sum(sq.reshape(tm // 8, 8, d), axis=0)

    tile = pl.BlockSpec((tm, d), lambda i: (i, 0))
    small = pl.BlockSpec((8, d), lambda i: (0, 0))
    return pl.pallas_call(
        body, name=name, grid=(s_len // tm,),
        in_specs=[tile, tile, small, tile],
        out_specs=[tile, tile, small],
        out_shape=[jax.ShapeDtypeStruct((s_len, d), F32), jax.ShapeDtypeStruct((s_len, d), BF16),
                   jax.ShapeDtypeStruct((8, d), F32)],
        compiler_params=_params(1),
    )(x, y, mod, target)


def _adamw_math(w, g, m, v):
    m = ADAM_B1 * m + (1.0 - ADAM_B1) * g
    v = ADAM_B2 * v + (1.0 - ADAM_B2) * (g * g)
    m_hat = m / (1.0 - ADAM_B1 ** ADAM_STEP)
    v_hat = v / (1.0 - ADAM_B2 ** ADAM_STEP)
    delta = -ADAM_LR * (m_hat / (jnp.sqrt(v_hat) + ADAM_EPS) + ADAM_WD * w)
    return delta, m, v


def _adamw(w, g, m, v, name, after=()):
    r, cols = w.shape
    tr = r // 8 if r % 64 == 0 else r

    def body(w_ref, g_ref, m_ref, v_ref, go_ref, d_ref, nm_ref, nv_ref):
        gv = g_ref[...]
        go_ref[...] = gv
        d_ref[...], nm_ref[...], nv_ref[...] = _adamw_math(w_ref[...], gv, m_ref[...], v_ref[...])

    tile = pl.BlockSpec((tr, cols), lambda i: (i, 0))
    shape = jax.ShapeDtypeStruct((r, cols), F32)
    return pl.pallas_call(
        _ordered(body, 4, after), name=name, grid=(r // tr,),
        in_specs=[tile] * 4 + [ANY] * len(after), out_specs=[tile] * 4, out_shape=[shape] * 4,
        compiler_params=_params(1),
    )(w, g, m, v, *after)


def _in_parts(tm, n_qkv, n_rest):
    def part(lo, n_blk):
        return pl.BlockSpec((tm, IN_BLOCK), lambda i, j: (i, jnp.clip(j - lo, 0, n_blk - 1)))
    return [part(0, n_qkv), part(n_qkv, n_qkv), part(2 * n_qkv, n_qkv), part(3 * n_qkv, n_rest)]


def _pick_part(j, n_qkv, refs, fn):
    bounds = [0, n_qkv, 2 * n_qkv, 3 * n_qkv]
    for p, ref in enumerate(refs):
        inside = j >= bounds[p]
        if p + 1 < len(refs):
            inside = inside & (j < bounds[p + 1])
        pl.when(inside)(lambda ref=ref: fn(ref))


def _rows(base, count, stride):
    return pl.ds(base, count) if stride == 1 else pl.ds(base, count, stride=stride)


REORDER_STRIDE = 4


def _reorder_plan(dil):
    inner = min(dil, REORDER_STRIDE)
    return inner, dil // inner, SLAB // inner, SLAB // dil


def _to_residue_order(dst, src, dil, tmp):
    inner, outer, big, seg = _reorder_plan(dil)
    if outer == 1:
        for r in range(dil):
            dst[pl.ds(r * seg, seg), :] = src[_rows(r, seg, dil), :].astype(dst.dtype)
        return
    for b in range(inner):
        tmp[pl.ds(b * big, big), :] = src[_rows(b, big, inner), :]
    for a in range(outer):
        for b in range(inner):
            dst[pl.ds((inner * a + b) * seg, seg), :] = tmp[_rows(b * big + a, seg, outer), :].astype(dst.dtype)


def _to_token_order(dst, src, dil, tmp):
    inner, outer, big, seg = _reorder_plan(dil)
    if outer == 1:
        for r in range(dil):
            dst[_rows(r, seg, dil), :] = src[pl.ds(r * seg, seg), :]
        return
    for a in range(outer):
        for b in range(inner):
            tmp[_rows(b * big + a, seg, outer), :] = src[pl.ds((inner * a + b) * seg, seg), :]
    for b in range(inner):
        dst[_rows(b, big, inner), :] = tmp[pl.ds(b * big, big), :]


def _in_proj(h, w, q_norm, k_norm, name):
    s_len, d = h.shape
    tm = PROJ_TILE
    assert tm == SLAB and IN_BLOCK == HEADS * HEAD_DIM
    steps = w.shape[1] // IN_BLOCK
    n_qkv = 3 * QKV // IN_BLOCK

    def body(h_ref, w_ref, qn_ref, kn_ref, qkv_ref, rest_ref, hat_ref, tok_s, tmp_s):
        j = pl.program_id(1)
        res = _dot(h_ref[...], w_ref[...])

        def emit(sect, gi):
            dil = DILATIONS[gi]
            for hh in range(HEADS):
                cols = slice(hh * HEAD_DIM, (hh + 1) * HEAD_DIM)
                x = res[:, cols]
                if sect < 2:
                    x = (x * _rms(x)) * (qn_ref if sect == 0 else kn_ref)[...]
                tok_s[...] = x
                _to_residue_order(hat_ref.at[:, cols], tok_s, dil, tmp_s)

        @pl.when(j < n_qkv)
        def _():
            qkv_ref[...] = res

        for sect in range(3):
            for gi in range(N_GROUPS):
                pl.when(j == sect * N_GROUPS + gi)(lambda sect=sect, gi=gi: emit(sect, gi))

        @pl.when(j >= n_qkv)
        def _():
            rest_ref[...] = res.astype(BF16)

    qkv_blk = pl.BlockSpec((tm, IN_BLOCK), lambda i, j: (i, jnp.minimum(j, n_qkv - 1)))
    small = pl.BlockSpec((1, HEAD_DIM), lambda i, j: (0, 0))
    return pl.pallas_call(
        body, name=name, grid=(s_len // tm, steps),
        in_specs=[pl.BlockSpec((tm, d), lambda i, j: (i, 0)), pl.BlockSpec((d, IN_BLOCK), lambda i, j: (0, j)),
                  small, small],
        out_specs=[qkv_blk, pl.BlockSpec((tm, IN_BLOCK), lambda i, j: (i, jnp.maximum(j - n_qkv, 0))), qkv_blk],
        out_shape=[jax.ShapeDtypeStruct((s_len, 3 * QKV), F32),
                   jax.ShapeDtypeStruct((s_len, w.shape[1] - 3 * QKV), BF16),
                   jax.ShapeDtypeStruct((s_len, 3 * QKV), BF16)],
        scratch_shapes=[pltpu.VMEM((tm, HEAD_DIM), F32)] * 2,
        compiler_params=_params(2),
    )(h, w, q_norm, k_norm)


def _in_proj_bwd(dq, dk, dv, drest, w, name, after=()):
    s_len = dq.shape[0]
    d = w.shape[0]
    tm = PROJ_TILE
    steps = w.shape[1] // IN_BLOCK
    n_qkv = QKV // IN_BLOCK

    def body(dq_ref, dk_ref, dv_ref, dr_ref, w_ref, o_ref, acc_ref):
        j = pl.program_id(1)

        @pl.when(j == 0)
        def _():
            acc_ref[...] = jnp.zeros_like(acc_ref)

        def add(a_ref):
            acc_ref[...] += _dot_nt(a_ref[...], w_ref[...])

        _pick_part(j, n_qkv, [dq_ref, dk_ref, dv_ref, dr_ref], add)

        @pl.when(j == steps - 1)
        def _():
            o_ref[...] = acc_ref[...]

    return pl.pallas_call(
        _ordered(body, 5, after), name=name, grid=(s_len // tm, steps),
        in_specs=(_in_parts(tm, n_qkv, steps - 3 * n_qkv) + [pl.BlockSpec((d, IN_BLOCK), lambda i, j: (0, j))]
                  + [ANY] * len(after)),
        out_specs=pl.BlockSpec((tm, d), lambda i, j: (i, 0)),
        out_shape=jax.ShapeDtypeStruct((s_len, d), F32),
        scratch_shapes=[pltpu.VMEM((tm, d), F32)],
        compiler_params=_params(2),
    )(dq, dk, dv, drest, w, *after)


def _wgrad(x, y, x_spec, y_spec, out_shape, out_spec, acc_shape, n_chunks, name, x_transposed=False, after=()):
    s_len = y.shape[-2]
    ts = WGRAD_TILE
    steps = s_len // ts

    def body(x_ref, y_ref, o_ref, acc_ref):
        s = pl.program_id(1)

        @pl.when(s == 0)
        def _():
            acc_ref[...] = jnp.zeros_like(acc_ref)

        acc_ref[...] += (_dot if x_transposed else _dot_tn)(x_ref[...], y_ref[...])

        @pl.when(s == steps - 1)
        def _():
            o_ref[...] = acc_ref[...].astype(o_ref.dtype)

    return pl.pallas_call(
        _ordered(body, 2, after), name=name, grid=(n_chunks, steps),
        in_specs=[x_spec(ts), y_spec(ts)] + [ANY] * len(after), out_specs=out_spec,
        out_shape=jax.ShapeDtypeStruct(out_shape, BF16),
        scratch_shapes=[pltpu.VMEM(acc_shape, F32)],
        compiler_params=_params(2),
    )(x, y, *after)


def _pieces(width, piece=256):
    return [slice(a, min(a + piece, width)) for a in range(0, width, piece)]


def _ffn_fwd(h, w_gate, w_up, w_down, name):
    s_len, d = h.shape
    n_chunks, _, fs = w_gate.shape
    tm = FFN_TILE

    def body(h_ref, wg_ref, wu_ref, wd_ref, g_ref, u_ref, y_ref):
        j = pl.program_id(1)
        hv = h_ref[...]
        total = None
        for cols in _pieces(fs):
            g = _dot(hv, wg_ref[:, cols])
            u = _dot(hv, wu_ref[:, cols])
            g_ref[:, cols] = g.astype(BF16)
            u_ref[:, cols] = u.astype(BF16)
            act = (g * _sigmoid(g)) * u
            part = _dot(act.astype(BF16), wd_ref[cols, :])
            total = part if total is None else total + part

        @pl.when(j == 0)
        def _():
            y_ref[...] = total

        @pl.when(j > 0)
        def _():
            y_ref[...] += total

    tile = pl.BlockSpec((tm, d), lambda i, j: (i, 0))
    hid = pl.BlockSpec((None, tm, fs), lambda i, j: (j, i, 0))
    w_in_spec = pl.BlockSpec((None, d, fs), lambda i, j: (j, 0, 0))
    hid_shape = jax.ShapeDtypeStruct((n_chunks, s_len, fs), BF16)
    return pl.pallas_call(
        body, name=name, grid=(s_len // tm, n_chunks),
        in_specs=[tile, w_in_spec, w_in_spec, pl.BlockSpec((None, fs, d), lambda i, j: (j, 0, 0))],
        out_specs=[hid, hid, tile],
        out_shape=[hid_shape, hid_shape, jax.ShapeDtypeStruct((s_len, d), F32)],
        compiler_params=_params(2),
    )(h, w_gate, w_up, w_down)


def _ffn_bwd(dy, g_pre, u_pre, w_gate, w_up, w_down, name):
    s_len, d = dy.shape
    n_chunks, _, fs = w_gate.shape
    tm = FFN_TILE

    def body(dy_ref, g_ref, u_ref, wg_ref, wu_ref, wd_ref, dh_ref, dg_ref, du_ref, a_ref):
        j = pl.program_id(1)
        dyv = dy_ref[...]
        total = None
        for cols in _pieces(fs):
            da = _dot_nt(dyv, wd_ref[cols, :])
            g = g_ref[:, cols].astype(F32)
            u = u_ref[:, cols].astype(F32)
            sg = _sigmoid(g)
            silu = g * sg
            dg = (da * u * (sg * (1.0 + g * (1.0 - sg)))).astype(BF16)
            du = (da * silu).astype(BF16)
            dg_ref[:, cols] = dg
            du_ref[:, cols] = du
            a_ref[:, cols] = (silu * u).astype(BF16)
            part = _dot_nt(dg, wg_ref[:, cols]) + _dot_nt(du, wu_ref[:, cols])
            total = part if total is None else total + part

        @pl.when(j == 0)
        def _():
            dh_ref[...] = total

        @pl.when(j > 0)
        def _():
            dh_ref[...] += total

    tile = pl.BlockSpec((tm, d), lambda i, j: (i, 0))
    hid = pl.BlockSpec((None, tm, fs), lambda i, j: (j, i, 0))
    w_in_spec = pl.BlockSpec((None, d, fs), lambda i, j: (j, 0, 0))
    hid_shape = jax.ShapeDtypeStruct((n_chunks, s_len, fs), BF16)
    return pl.pallas_call(
        body, name=name, grid=(s_len // tm, n_chunks),
        in_specs=[tile, hid, hid, w_in_spec, w_in_spec, pl.BlockSpec((None, fs, d), lambda i, j: (j, 0, 0))],
        out_specs=[tile, hid, hid, hid],
        out_shape=[jax.ShapeDtypeStruct((s_len, d), F32), hid_shape, hid_shape, hid_shape],
        compiler_params=_params(2),
    )(dy, g_pre, u_pre, w_gate, w_up, w_down)


def _ffn_wgrads(ht, dg, du, act, dy, tag, after=()):
    n_chunks, s_len, fs = dg.shape
    d = ht.shape[0]
    tok = lambda ts: pl.BlockSpec((ts, d), lambda c, s: (s, 0))
    tok_t = lambda ts: pl.BlockSpec((d, ts), lambda c, s: (0, s))
    hid = lambda ts: pl.BlockSpec((None, ts, fs), lambda c, s: (c, s, 0))
    d_up = pl.BlockSpec((None, d, fs), lambda c, s: (c, 0, 0))
    d_down = pl.BlockSpec((None, fs, d), lambda c, s: (c, 0, 0))
    dwg = _wgrad(ht, dg, tok_t, hid, (n_chunks, d, fs), d_up, (d, fs), n_chunks, tag + "_dwg", True, after)
    dwu = _wgrad(ht, du, tok_t, hid, (n_chunks, d, fs), d_up, (d, fs), n_chunks, tag + "_dwu", True, after)
    dwd = _wgrad(act, dy, hid, tok, (n_chunks, fs, d), d_down, (fs, d), n_chunks, tag + "_dwd", False, after)
    return dwg, dwu, dwd


def _band_bias():
    qi = lax.broadcasted_iota(jnp.int32, (ATTN_BLOCK, 2 * ATTN_BLOCK), 0)
    kj = lax.broadcasted_iota(jnp.int32, (ATTN_BLOCK, 2 * ATTN_BLOCK), 1)
    band = (kj >= qi) & (kj <= qi + ATTN_BLOCK)
    return jnp.where(band, 0.0, NEG), jnp.where(band & (kj >= ATTN_BLOCK), 0.0, NEG)


def _qkv_specs(slab_of, sections):
    def spec(sect, back):
        return pl.BlockSpec((SLAB, HEAD_DIM),
                            lambda h, s, g: (jnp.maximum(slab_of(s) - back, 0), (sect * N_GROUPS + g) * HEADS + h))
    return [spec(sect, back) for sect, back in sections]


HAT_BLOCKS = [(0, 0), (1, 0), (2, 0), (1, 1), (2, 1)]


def _stage_keys(k_ref, v_ref, kp_ref, vp_ref, kbuf, vbuf, dil, n):
    run = SLAB // dil
    for r in range(dil):
        own, before = pl.ds(r * run, run), pl.ds(2 * r * run, run)
        kbuf[pl.ds((2 * r + 1) * run, run), :] = k_ref[own, :]
        vbuf[pl.ds((2 * r + 1) * run, run), :] = v_ref[own, :]

        @pl.when(n > 0)
        def _():
            kbuf[before, :] = kp_ref[own, :]
            vbuf[before, :] = vp_ref[own, :]

        @pl.when(n == 0)
        def _():
            kbuf[before, :] = jnp.zeros((run, HEAD_DIM), BF16)
            vbuf[before, :] = jnp.zeros((run, HEAD_DIM), BF16)


def _for_each_tile(dil, n, tile_fn):
    run = SLAB // dil
    bias, first_bias = _band_bias()
    for jj in range(run // ATTN_BLOCK):
        start = jj * ATTN_BLOCK
        tile_bias = jnp.where(n == 0, first_bias, bias) if jj == 0 else bias
        for r in range(dil):
            tile_fn(pl.ds(r * run + start, ATTN_BLOCK),
                    pl.ds((2 * r + 1) * run - ATTN_BLOCK + start, 2 * ATTN_BLOCK), tile_bias)


def _attn_fwd(hat, name):
    s_len = hat.shape[0]
    e = HEAD_DIM
    n_slabs = s_len // SLAB

    def body(q_ref, k_ref, v_ref, kp_ref, vp_ref, o_ref, lse_ref, kbuf, vbuf, m_s, l_s, acc_s, m_p, l_p, acc_p, tmp_s):
        n, grp = pl.program_id(1), pl.program_id(2)

        def run(gi, dil):
            _stage_keys(k_ref, v_ref, kp_ref, vp_ref, kbuf, vbuf, dil, n)

            def tile(q_rows, kv_rows, bias):
                s = _dot_nt(q_ref[q_rows, :], kbuf[kv_rows, :]) * ATTN_SCALE + bias
                m = jnp.max(s, axis=-1, keepdims=True)
                p = jnp.exp(s - m)
                m_p[q_rows, :] = jnp.broadcast_to(m, (ATTN_BLOCK, e))
                l_p[q_rows, :] = jnp.broadcast_to(jnp.sum(p, axis=-1, keepdims=True), (ATTN_BLOCK, e))
                acc_p[q_rows, :] = _dot(p.astype(BF16), vbuf[kv_rows, :])

            _for_each_tile(dil, n, tile)
            _to_token_order(m_s.at[gi], m_p, dil, tmp_s)
            _to_token_order(l_s.at[gi], l_p, dil, tmp_s)
            _to_token_order(acc_s.at[gi], acc_p, dil, tmp_s)

        for gi, dil in enumerate(DILATIONS):
            pl.when(grp == gi)(lambda gi=gi, dil=dil: run(gi, dil))

        @pl.when(grp == N_GROUPS - 1)
        def _():
            m_all = jnp.maximum(jnp.maximum(m_s[0], m_s[1]), m_s[2])
            den = jnp.zeros((SLAB, e), F32)
            num = jnp.zeros((SLAB, e), F32)
            for gi in range(N_GROUPS):
                w = jnp.exp(m_s[gi] - m_all)
                den += l_s[gi] * w
                num += acc_s[gi] * w
            o_ref[...] = num / den
            lse_ref[...] = m_all + jnp.log(den)

    out = pl.BlockSpec((SLAB, e), lambda h, n, g: (n, h))
    return pl.pallas_call(
        body, name=name, grid=(HEADS, n_slabs, N_GROUPS),
        in_specs=_qkv_specs(lambda n: n, HAT_BLOCKS),
        out_specs=[out, out],
        out_shape=[jax.ShapeDtypeStruct((s_len, HEADS * e), F32)] * 2,
        scratch_shapes=[pltpu.VMEM((2 * SLAB, e), BF16), pltpu.VMEM((2 * SLAB, e), BF16),
                        pltpu.VMEM((N_GROUPS, SLAB, e), F32), pltpu.VMEM((N_GROUPS, SLAB, e), F32),
                        pltpu.VMEM((N_GROUPS, SLAB, e), F32)]
        + [pltpu.VMEM((SLAB, e), F32)] * 4,
        compiler_params=_params(3),
    )(hat, hat, hat, hat, hat)


def _attn_bwd(qkv, hat, d_out, out, lse, q_norm, k_norm, name):
    s_len = qkv.shape[0]
    e = HEAD_DIM
    n_slabs = s_len // SLAB

    def body(q_ref, k_ref, v_ref, kp_ref, vp_ref, qraw_ref, kraw_ref, do_ref, o_ref, lse_ref, qn_ref, kn_ref,
             dq_ref, dk_ref, dv_ref, st_ref, kbuf, vbuf, stat_s, dqs, dkb, dvb, dk_tok, dv_tok, carry,
             do_p, stat_p, dq_p, dk_p, dv_p, tmp_s, do16_p):
        head, step, grp = pl.program_id(0), pl.program_id(1), pl.program_id(2)
        n = n_slabs - 1 - step
        dkb[...] = jnp.zeros_like(dkb)
        dvb[...] = jnp.zeros_like(dvb)
        @pl.when(grp == 0)
        def _():
            lane = lax.broadcasted_iota(jnp.int32, (SLAB, e), 1)
            stat_s[...] = jnp.where(lane < e // 2, lse_ref[...],
                                    jnp.sum(do_ref[...] * o_ref[...], axis=-1, keepdims=True))

        @pl.when((head == 0) & (step == 0) & (grp == 0))
        def _():
            st_ref[...] = jnp.zeros_like(st_ref)

        def run(gi, dil):
            seg = SLAB // dil
            _stage_keys(k_ref, v_ref, kp_ref, vp_ref, kbuf, vbuf, dil, n)

            @pl.when(step == 0)
            def _():
                carry[gi] = jnp.zeros((2, SLAB, e), F32)

            _to_residue_order(do_p, do_ref, dil, tmp_s)
            do16_p[...] = do_p[...].astype(BF16)
            _to_residue_order(stat_p, stat_s, dil, tmp_s)

            def tile(q_rows, kv_rows, bias):
                q = q_ref[q_rows, :]
                k = kbuf[kv_rows, :]
                v = vbuf[kv_rows, :]
                stat = stat_p[q_rows, :]
                do16 = do16_p[q_rows, :]
                s = _dot_nt(q, k) * ATTN_SCALE + bias
                p = jnp.exp(s - stat[:, 0:1])
                ds = (p * (_dot_nt(do16, v) - stat[:, e // 2:e // 2 + 1]) * ATTN_SCALE).astype(BF16)
                dq_p[q_rows, :] = _dot(ds, k)
                dkb[kv_rows, :] += _dot_tn(ds, q)
                dvb[kv_rows, :] += _dot_tn(p.astype(BF16), do16)

            _for_each_tile(dil, n, tile)
            for r in range(dil):
                own, before = pl.ds((2 * r + 1) * seg, seg), pl.ds(2 * r * seg, seg)
                kept = pl.ds(r * seg, seg)
                dk_p[kept, :] = dkb[own, :] + carry.at[gi, 0][kept, :]
                dv_p[kept, :] = dvb[own, :] + carry.at[gi, 1][kept, :]
                carry.at[gi, 0][kept, :] = dkb[before, :]
                carry.at[gi, 1][kept, :] = dvb[before, :]
            _to_token_order(dqs, dq_p, dil, tmp_s)
            _to_token_order(dk_tok, dk_p, dil, tmp_s)
            _to_token_order(dv_tok, dv_p, dil, tmp_s)

            def norm_bwd(raw, gain, d_hat):
                r = _rms(raw)
                y = raw * r
                dy = d_hat * gain
                return r * (dy - y * jnp.mean(dy * y, axis=-1, keepdims=True)), jnp.sum(d_hat * y, axis=0, keepdims=True)

            dq, dqn = norm_bwd(qraw_ref[...], qn_ref[...], dqs[...])
            dk, dkn = norm_bwd(kraw_ref[...], kn_ref[...], dk_tok[...])
            dq_ref[...] = dq.astype(BF16)
            dk_ref[...] = dk.astype(BF16)
            dv_ref[...] = dv_tok[...].astype(BF16)
            st_ref[0:1, :] += dqn
            st_ref[1:2, :] += dkn

        for gi, dil in enumerate(DILATIONS):
            pl.when(grp == gi)(lambda gi=gi, dil=dil: run(gi, dil))

    slab_of = lambda s: n_slabs - 1 - s
    small = pl.BlockSpec((1, e), lambda h, s, g: (0, 0))
    head_blk = pl.BlockSpec((SLAB, e), lambda h, s, g: (slab_of(s), h))
    grad_blk = pl.BlockSpec((SLAB, e), lambda h, s, g: (slab_of(s), g * HEADS + h))
    grad_shape = jax.ShapeDtypeStruct((s_len, QKV), BF16)
    return pl.pallas_call(
        body, name=name, grid=(HEADS, n_slabs, N_GROUPS),
        in_specs=(_qkv_specs(slab_of, HAT_BLOCKS) + _qkv_specs(slab_of, [(0, 0), (1, 0)])
                  + [head_blk, head_blk, head_blk, small, small]),
        out_specs=[grad_blk, grad_blk, grad_blk, pl.BlockSpec((8, e), lambda h, s, g: (0, 0))],
        out_shape=[grad_shape, grad_shape, grad_shape, jax.ShapeDtypeStruct((8, e), F32)],
        scratch_shapes=[pltpu.VMEM((2 * SLAB, e), BF16), pltpu.VMEM((2 * SLAB, e), BF16), pltpu.VMEM((SLAB, e), F32),
                        pltpu.VMEM((SLAB, e), F32), pltpu.VMEM((2 * SLAB, e), F32), pltpu.VMEM((2 * SLAB, e), F32),
                        pltpu.VMEM((SLAB, e), F32), pltpu.VMEM((SLAB, e), F32),
                        pltpu.VMEM((N_GROUPS, 2, SLAB, e), F32)]
        + [pltpu.VMEM((SLAB, e), F32)] * 6 + [pltpu.VMEM((SLAB, e), BF16)],
        compiler_params=_params(3),
    )(hat, hat, hat, hat, hat, qkv, qkv, d_out, out, lse, q_norm, k_norm)


def _shift_rows(x, by, edge, forward):
    t_len = x.shape[0]
    row = lax.broadcasted_iota(jnp.int32, x.shape, 0)
    if forward:
        out = pltpu.roll(x, by, 0)
        for i in range(by):
            out = jnp.where(row == i, edge[8 - by + i:8 - by + i + 1, :], out)
    else:
        out = pltpu.roll(x, t_len - by, 0)
        for i in range(by):
            out = jnp.where(row == t_len - by + i, edge[i:i + 1, :], out)
    return out


def _mix_fwd(x, o, rest, mod, conv_w, w_attn, w_conv, w_out, name):
    s_len, d = x.shape
    tm = MIX_TILE
    a_w = o.shape[1]

    def body(x_ref, o_ref, u_ref, b_ref, c_ref, ga_ref, gc_ref, mod_ref, cw_ref, wa_ref, wc_ref, wo_ref,
             xo_ref, z_ref, ya_ref, yc_ref, conv_ref, yb_ref, m_ref, o16_ref, carry):
        @pl.when(pl.program_id(0) == 0)
        def _():
            carry[...] = jnp.zeros_like(carry)

        xc = c_ref[...].astype(F32) * u_ref[...].astype(F32)
        edge = carry[...]
        conv = (_shift_rows(xc, 2, edge, True) * cw_ref[0:1, :] + _shift_rows(xc, 1, edge, True) * cw_ref[1:2, :]
                + xc * cw_ref[2:3, :])
        carry[...] = xc[tm - 8:tm, :]
        yb = (b_ref[...].astype(F32) * conv).astype(BF16)
        o16 = o_ref[...].astype(BF16)
        ya = _dot(o16, wa_ref[...])
        yc = _dot(yb, wc_ref[...])
        merged = (_sigmoid(ga_ref[...].astype(F32)) * ya + _sigmoid(gc_ref[...].astype(F32)) * yc).astype(BF16)
        z = _dot(merged, wo_ref[...])
        xo_ref[...] = x_ref[...] + mod_ref[2:3, :] * z
        z_ref[...] = z
        ya_ref[...] = ya.astype(BF16)
        yc_ref[...] = yc.astype(BF16)
        conv_ref[...] = conv.astype(BF16)
        yb_ref[...] = yb
        m_ref[...] = merged
        o16_ref[...] = o16

    tile = pl.BlockSpec((tm, d), lambda i: (i, 0))
    sect = lambda k: pl.BlockSpec((tm, d), lambda i: (i, k))
    att = pl.BlockSpec((tm, a_w), lambda i: (i, 0))
    const = lambda shape: pl.BlockSpec(shape, lambda i: (0, 0))
    f32_out = jax.ShapeDtypeStruct((s_len, d), F32)
    b16_out = jax.ShapeDtypeStruct((s_len, d), BF16)
    return pl.pallas_call(
        body, name=name, grid=(s_len // tm,),
        in_specs=[tile, att, sect(0), sect(1), sect(2), sect(3), sect(4), const((8, d)), const((8, d)),
                  const((a_w, d)), const((d, d)), const((d, d))],
        out_specs=[tile] * 7 + [att],
        out_shape=[f32_out, f32_out] + [b16_out] * 5 + [jax.ShapeDtypeStruct((s_len, a_w), BF16)],
        scratch_shapes=[pltpu.VMEM((8, d), F32)],
        compiler_params=_params(1),
    )(x, o, rest, rest, rest, rest, rest, mod, conv_w, w_attn, w_conv, w_out)


def _mix_bwd(dxo, ya, yc, conv, rest, mod, conv_w, w_attn, w_conv, w_out, a_w, name):
    s_len, d = dxo.shape
    tm = MIX_TILE
    n_tiles = s_len // tm

    def body(dxo_ref, ya_ref, yc_ref, conv_ref, u_ref, b_ref, c_ref, ga_ref, gc_ref, mod_ref, cw_ref,
             wa_ref, wc_ref, wo_ref, do_ref, drest_ref, dz_ref, dya_ref, dyc_ref, st_ref, carry):
        @pl.when(pl.program_id(0) == 0)
        def _():
            carry[...] = jnp.zeros_like(carry)
            st_ref[...] = jnp.zeros_like(st_ref)

        dz = (mod_ref[2:3, :] * dxo_ref[...]).astype(BF16)
        dz_ref[...] = dz
        dm = _dot_nt(dz, wo_ref[...])
        sa, sc = _sigmoid(ga_ref[...].astype(F32)), _sigmoid(gc_ref[...].astype(F32))
        dya = (dm * sa).astype(BF16)
        dyc = (dm * sc).astype(BF16)
        dya_ref[...] = dya
        dyc_ref[...] = dyc
        drest_ref[:, 3 * d:4 * d] = (dm * ya_ref[...].astype(F32) * (sa * (1.0 - sa))).astype(BF16)
        drest_ref[:, 4 * d:5 * d] = (dm * yc_ref[...].astype(F32) * (sc * (1.0 - sc))).astype(BF16)
        do_ref[...] = _dot_nt(dya, wa_ref[...])
        dyb = _dot_nt(dyc, wc_ref[...])
        drest_ref[:, d:2 * d] = (dyb * conv_ref[...].astype(F32)).astype(BF16)
        dconv = dyb * b_ref[...].astype(F32)
        edge = carry[...]
        sh1 = _shift_rows(dconv, 1, edge, False)
        sh2 = _shift_rows(dconv, 2, edge, False)
        carry[...] = dconv[0:8, :]
        dxc = dconv * cw_ref[2:3, :] + sh1 * cw_ref[1:2, :] + sh2 * cw_ref[0:1, :]
        u, c = u_ref[...].astype(F32), c_ref[...].astype(F32)
        xc = c * u
        drest_ref[:, 0:d] = (dxc * c).astype(BF16)
        drest_ref[:, 2 * d:3 * d] = (dxc * u).astype(BF16)
        st_ref[0:1, :] += jnp.sum(xc * sh2, axis=0, keepdims=True)
        st_ref[1:2, :] += jnp.sum(xc * sh1, axis=0, keepdims=True)
        st_ref[2:3, :] += jnp.sum(xc * dconv, axis=0, keepdims=True)

    rev = lambda i: n_tiles - 1 - i
    tile = pl.BlockSpec((tm, d), lambda i: (rev(i), 0))
    sect = lambda k: pl.BlockSpec((tm, d), lambda i: (rev(i), k))
    const = lambda shape: pl.BlockSpec(shape, lambda i: (0, 0))
    b16_out = jax.ShapeDtypeStruct((s_len, d), BF16)
    return pl.pallas_call(
        body, name=name, grid=(n_tiles,),
        in_specs=[tile, tile, tile, tile, sect(0), sect(1), sect(2), sect(3), sect(4), const((8, d)), const((8, d)),
                  const((a_w, d)), const((d, d)), const((d, d))],
        out_specs=[pl.BlockSpec((tm, a_w), lambda i: (rev(i), 0)), pl.BlockSpec((tm, 5 * d), lambda i: (rev(i), 0)),
                   tile, tile, tile, const((8, d))],
        out_shape=[jax.ShapeDtypeStruct((s_len, a_w), F32), jax.ShapeDtypeStruct((s_len, 5 * d), BF16),
                   b16_out, b16_out, b16_out, jax.ShapeDtypeStruct((8, d), F32)],
        scratch_shapes=[pltpu.VMEM((8, d), F32)],
        compiler_params=_params(1),
    )(dxo, ya, yc, conv, rest, rest, rest, rest, rest, mod, conv_w, w_attn, w_conv, w_out)


ADA_COLS = 128


def _ada_fwd(c_all, w_shard, b_shard, name):
    d, cols = w_shard.shape

    def body(c_ref, w_ref, b_ref, o_ref):
        cv = c_ref[...]
        o_ref[...] = jnp.dot(cv * _sigmoid(cv), w_ref[...], preferred_element_type=F32,
                             precision=lax.Precision.HIGHEST) + b_ref[...]

    return pl.pallas_call(
        body, name=name, grid=(cols // ADA_COLS,),
        in_specs=[pl.BlockSpec((8, d), lambda j: (0, 0)), pl.BlockSpec((d, ADA_COLS), lambda j: (0, j)),
                  pl.BlockSpec((1, ADA_COLS), lambda j: (0, j))],
        out_specs=pl.BlockSpec((8, ADA_COLS), lambda j: (0, j)),
        out_shape=jax.ShapeDtypeStruct((8, cols), F32),
        compiler_params=_params(1),
    )(c_all, w_shard, b_shard)


def _ada_bwd(c_all, dmod_shard, w, m, v, name):
    d, cols = w.shape

    def body(c_ref, dm_ref, w_ref, m_ref, v_ref, g_ref, d_ref, nm_ref, nv_ref):
        cv = c_ref[...]
        g = lax.dot_general(cv * _sigmoid(cv), dm_ref[...], (((0,), (0,)), ((), ())),
                            preferred_element_type=F32, precision=lax.Precision.HIGHEST)
        g_ref[...] = g
        d_ref[...], nm_ref[...], nv_ref[...] = _adamw_math(w_ref[...], g, m_ref[...], v_ref[...])

    blk = pl.BlockSpec((d, ADA_COLS), lambda j: (0, j))
    shape = jax.ShapeDtypeStruct((d, cols), F32)
    return pl.pallas_call(
        body, name=name, grid=(cols // ADA_COLS,),
        in_specs=[pl.BlockSpec((8, d), lambda j: (0, 0)), pl.BlockSpec((8, ADA_COLS), lambda j: (0, j)), blk, blk, blk],
        out_specs=[blk] * 4, out_shape=[shape] * 4,
        compiler_params=_params(1),
    )(c_all, dmod_shard, w, m, v)


def _small_update(parts, w, m, v, name):
    n = w.shape[1]

    def body(p_ref, w_ref, m_ref, v_ref, g_ref, d_ref, nm_ref, nv_ref):
        g = p_ref[0:1, :]
        for i in range(1, 8):
            g = g + p_ref[i:i + 1, :]
        g_ref[...] = g
        d_ref[...], nm_ref[...], nv_ref[...] = _adamw_math(w_ref[...], g, m_ref[...], v_ref[...])

    shape = jax.ShapeDtypeStruct((1, n), F32)
    return pl.pallas_call(body, name=name, out_shape=[shape] * 4, compiler_params=_params())(parts, w, m, v)


def _cols_to_shards(w, n):
    r, nc = w.shape
    return w.reshape(r, n, nc // n).transpose(1, 0, 2)


def kernel(x, c, w_ada, b_ada, norm_ffn1, ffn1_w_gate, ffn1_w_up, ffn1_w_down, norm_mix, w_in, q_norm, k_norm, conv_w, w_attn_branch, w_conv_branch, w_out, norm_ffn2, ffn2_w_gate, ffn2_w_up, ffn2_w_down, loss_target, m_w_ada, m_b_ada, m_norm_ffn1, m_ffn1_w_gate, m_ffn1_w_up, m_ffn1_w_down, m_norm_mix, m_w_in, m_q_norm, m_k_norm, m_conv_w, m_w_attn_branch, m_w_conv_branch, m_w_out, m_norm_ffn2, m_ffn2_w_gate, m_ffn2_w_up, m_ffn2_w_down, v_w_ada, v_b_ada, v_norm_ffn1, v_ffn1_w_gate, v_ffn1_w_up, v_ffn1_w_down, v_norm_mix, v_w_in, v_q_norm, v_k_norm, v_conv_w, v_w_attn_branch, v_w_conv_branch, v_w_out, v_norm_ffn2, v_ffn2_w_gate, v_ffn2_w_up, v_ffn2_w_down):
    ix, iy, ic = _place()
    chip = 2 * ix + iy
    me = 4 * ix + 2 * iy + ic
    xs = x[0]
    target = loss_target[0]
    s_len, d = xs.shape
    ada_cols = w_ada.shape[2]
    conv_cols = conv_w.shape[2]

    conv_rows = jnp.zeros((8, conv_cols), F32).at[0:3].set(conv_w[0])
    small_in = jnp.concatenate([jnp.broadcast_to(c, (8, d)), conv_rows], axis=1)
    small_all = _allgather8(small_in, "gather_c").reshape(8, 8, d + conv_cols)
    c_all = small_all[:, 0, :d]
    conv_full = small_all[0::2, 0:3, d:].transpose(1, 0, 2).reshape(3, N_CHIPS * conv_cols)
    conv_pad = jnp.zeros((8, N_CHIPS * conv_cols), F32).at[0:3].set(conv_full)
    b_shard = lax.dynamic_slice(b_ada, (0, chip * ada_cols), (1, ada_cols))
    mod_part = _ada_fwd(c_all, w_ada[0], b_shard, "ada_fwd")
    mod_all = _allgather8(mod_part, "gather_mod").reshape(N_CHIPS, 2, 8, ada_cols)[:, 0]
    mod_mine = lax.dynamic_slice(mod_all, (0, me, 0), (N_CHIPS, 1, ada_cols)).reshape(9, d)

    def mod_rows(i, gain):
        return jnp.zeros((8, d), F32).at[0:3].set(mod_mine[3 * i:3 * i + 3]).at[3:4].set(gain)

    mod1, mod2, mod3 = mod_rows(0, norm_ffn1), mod_rows(1, norm_mix), mod_rows(2, norm_ffn2)

    to16 = lambda w: w[0].astype(BF16)
    wg1, wu1, wd1 = _gather_weights([to16(ffn1_w_gate), to16(ffn1_w_up), to16(ffn1_w_down)], [False] * 3,
                                    "gather_ffn1", 1)
    h1, h1t = _norm_mod(xs, mod1, "norm1")
    (w_in_full,) = _gather_weights([to16(w_in)], [True], "gather_w_in", 2, after=(wd1, h1))

    g1, u1, y1 = _ffn_fwd(h1, wg1, wu1, wd1, "ffn1_fwd")
    x1, h2, h2t = _norm_mod(xs, mod2, "norm2", prev=(y1, mod1, 0.5))
    qkv, rest, qkv_hat = _in_proj(h2, w_in_full, q_norm, k_norm, "in_proj")
    w_ab, w_cb_g, w_o_g, wg2, wu2, wd2 = _gather_weights(
        [to16(w_attn_branch), to16(w_conv_branch), to16(w_out),
         to16(ffn2_w_gate), to16(ffn2_w_up), to16(ffn2_w_down)], [True] + [False] * 5,
        "gather_rest", 3, after=(h2,))
    a_w = w_ab.shape[0]
    w_cb = w_cb_g.reshape(d, d)
    w_o = w_o_g.reshape(d, d)
    o, lse = _attn_fwd(qkv_hat, "attn_fwd")
    x2, z, ya, yc, conv, yb, merged, o16 = _mix_fwd(x1, o, rest, mod2, conv_pad, w_ab, w_cb, w_o, "mix_fwd")
    h3, h3t = _norm_mod(x2, mod3, "norm3")
    g3, u3, y3 = _ffn_fwd(h3, wg2, wu2, wd2, "ffn2_fwd")
    dx3, dy3, loss_part = _loss_grad(x2, y3, mod3, target, "loss")
    loss = lax.psum(0.5 * jnp.sum(loss_part) / d, ("x", "y", "c"))

    c_idx = jnp.reshape(ic, (1,)).astype(jnp.int32)
    chip_idx = jnp.stack([chip, ic]).astype(jnp.int32)

    def reduce_start(grads, names, tag, collective_id):
        from_sibling = _rs_pair_exchange(grads, "rs_pair_" + tag)
        pair_sums = [_pair_add(g, r, c_idx, "pair_add_" + nm) for g, r, nm in zip(grads, from_sibling, names)]
        return pair_sums, _rs_chip_exchange(pair_sums, "rs_chips_" + tag, collective_id)

    def reduce_finish(pair_sums, from_chips, names, tag, after):
        totals = [_chip_add(p, r, chip_idx, "chip_add_" + nm, after)
                  for p, r, nm in zip(pair_sums, from_chips, names)]
        return dict(zip(names, _rs_share(totals, "rs_share_" + tag)))

    names_a = ["ffn2_w_gate", "ffn2_w_up", "ffn2_w_down"]
    names_b = ["w_in", "w_attn_branch", "w_conv_branch", "w_out"]
    names_c = ["ffn1_w_gate", "ffn1_w_up", "ffn1_w_down"]

    dh3, dg3, du3, a3 = _ffn_bwd(dy3, g3, u3, wg2, wu2, wd2, "ffn2_bwd")
    sums_a, chips_a = reduce_start(list(_ffn_wgrads(h3t, dg3, du3, a3, dy3, "ffn2")), names_a, "a", 4)
    dx2, st3 = _norm_bwd(dh3, x2, mod3, dx3, y3, 0.5, "norm3_bwd", after=tuple(sums_a))

    do, drest, dz, dya, dyc, st_conv = _mix_bwd(dx2, ya, yc, conv, rest, mod2, conv_pad, w_ab, w_cb, w_o, a_w, "mix_bwd")
    dq, dk, dv, st_qk = _attn_bwd(qkv, qkv_hat, do, o, lse, q_norm, k_norm, "attn_bwd")
    tok = lambda width: (lambda ts: pl.BlockSpec((ts, width), lambda cc, s: (s, 0)))
    colblk = lambda width: (lambda ts: pl.BlockSpec((ts, width), lambda cc, s: (s, cc)))
    tok_t = lambda ts: pl.BlockSpec((d, ts), lambda cc, s: (0, s))
    whole = pl.BlockSpec((d, QKV), lambda cc, s: (0, 0))
    dw_in = [_wgrad(h2t, part, tok_t, tok(QKV), (d, QKV), whole, (d, QKV), 1, "dw_in_" + nm, True)
             for part, nm in ((dq, "q"), (dk, "k"), (dv, "v"))]
    dw_in.append(_wgrad(h2t, drest, tok_t, colblk(d), (d, 5 * d), pl.BlockSpec((d, d), lambda cc, s: (0, cc)),
                        (d, d), 5, "dw_in_rest", True))
    dw_in = _cols_to_shards(jnp.concatenate(dw_in, axis=1), N_CHIPS)
    shard_w = d // N_CHIPS
    dw_ab = _wgrad(o16, dya, tok(a_w), colblk(shard_w), (a_w, d), pl.BlockSpec((a_w, shard_w), lambda cc, s: (0, cc)),
                   (a_w, shard_w), N_CHIPS, "dw_attn_branch")
    dw_ab = _cols_to_shards(dw_ab, N_CHIPS)
    row_out = pl.BlockSpec((None, shard_w, d), lambda cc, s: (cc, 0, 0))
    dw_cb = _wgrad(yb, dyc, colblk(shard_w), tok(d), (N_CHIPS, shard_w, d), row_out, (shard_w, d), N_CHIPS, "dw_conv_branch")
    dw_o = _wgrad(merged, dz, colblk(shard_w), tok(d), (N_CHIPS, shard_w, d), row_out, (shard_w, d), N_CHIPS, "dw_out")
    shard_grads = reduce_finish(sums_a, chips_a, names_a, "a", after=(dw_in, dw_o))
    sums_b, chips_b = reduce_start([dw_in, dw_ab, dw_cb, dw_o], names_b, "b", 5)

    dh2 = _in_proj_bwd(dq, dk, dv, drest, w_in_full, "in_proj_bwd", after=tuple(sums_b))
    dx1, st2, dy1 = _norm_bwd(dh2, x1, mod2, dx2, z, 1.0, "norm2_bwd", prev=(mod1, 0.5))
    dh1, dg1, du1, a1 = _ffn_bwd(dy1, g1, u1, wg1, wu1, wd1, "ffn1_bwd")
    dx0, st1 = _norm_bwd(dh1, xs, mod1, dx1, y1, 0.5, "norm1_bwd")
    grads_c = list(_ffn_wgrads(h1t, dg1, du1, a1, dy1, "ffn1"))
    shard_grads.update(reduce_finish(sums_b, chips_b, names_b, "b", after=tuple(grads_c)))
    sums_c, chips_c = reduce_start(grads_c, names_c, "c", 6)

    dmod = jnp.concatenate([st1[0:3], st2[0:3], st3[0:3]], axis=0).reshape(1, 9 * d)
    small = jnp.concatenate([dmod, st1[3:4], st2[3:4], st3[3:4], st_qk[0:1], st_qk[1:2],
                             st_conv[0:3].reshape(1, 3 * d)], axis=1)
    small_all = _allgather8(jnp.broadcast_to(small, (8, small.shape[1])), "gather_small").reshape(8, 8, -1)[:, 0]
    dmod_all = small_all[:, :9 * d]
    dmod_shard = lax.dynamic_slice(dmod_all, (0, chip * ada_cols), (8, ada_cols))
    g_w_ada, d_w_ada, nm_w_ada, nv_w_ada = _ada_bwd(c_all, dmod_shard, w_ada[0], m_w_ada[0], v_w_ada[0], "ada_bwd")

    vec_names = ["b_ada", "norm_ffn1", "norm_mix", "norm_ffn2", "q_norm", "k_norm"]
    vec_w = [b_ada, norm_ffn1, norm_mix, norm_ffn2, q_norm, k_norm]
    vec_m = [m_b_ada, m_norm_ffn1, m_norm_mix, m_norm_ffn2, m_q_norm, m_k_norm]
    vec_v = [v_b_ada, v_norm_ffn1, v_norm_mix, v_norm_ffn2, v_q_norm, v_k_norm]
    n_vec = sum(w.shape[1] for w in vec_w)
    cat = lambda arrs: jnp.concatenate(arrs, axis=1)
    vec_out = _small_update(small_all[:, :n_vec], cat(vec_w), cat(vec_m), cat(vec_v), "small_update")
    conv_parts = small_all[:, n_vec:].reshape(8, 3, N_CHIPS * conv_cols)
    conv_parts = lax.dynamic_slice(conv_parts, (0, 0, chip * conv_cols), (8, 3, conv_cols)).reshape(8, 3 * conv_cols)
    flat3 = lambda w: w[0].reshape(1, 3 * conv_cols)
    conv_out = _small_update(conv_parts, flat3(conv_w), flat3(m_conv_w), flat3(v_conv_w), "conv_update")

    res = {"w_ada": [t[None] for t in (g_w_ada, d_w_ada, nm_w_ada, nv_w_ada)],
           "conv_w": [t.reshape(1, 3, conv_cols) for t in conv_out]}
    off = 0
    for nm, w in zip(vec_names, vec_w):
        width = w.shape[1]
        res[nm] = [t[:, off:off + width] for t in vec_out]
        off += width
    big = {"ffn1_w_gate": (ffn1_w_gate, m_ffn1_w_gate, v_ffn1_w_gate), "ffn1_w_up": (ffn1_w_up, m_ffn1_w_up, v_ffn1_w_up),
           "ffn1_w_down": (ffn1_w_down, m_ffn1_w_down, v_ffn1_w_down), "w_in": (w_in, m_w_in, v_w_in),
           "w_attn_branch": (w_attn_branch, m_w_attn_branch, v_w_attn_branch),
           "w_conv_branch": (w_conv_branch, m_w_conv_branch, v_w_conv_branch), "w_out": (w_out, m_w_out, v_w_out),
           "ffn2_w_gate": (ffn2_w_gate, m_ffn2_w_gate, v_ffn2_w_gate), "ffn2_w_up": (ffn2_w_up, m_ffn2_w_up, v_ffn2_w_up),
           "ffn2_w_down": (ffn2_w_down, m_ffn2_w_down, v_ffn2_w_down)}
    def update(nm, after=()):
        w, m, v = big[nm]
        g, delta, new_m, new_v = _adamw(w[0], shard_grads[nm], m[0], v[0], "adamw_" + nm, after)
        res[nm] = [t[None] for t in (g, delta, new_m, new_v)]
        return new_v

    last = tuple(sums_c)
    for nm in names_a + names_b:
        last = (update(nm, last),)
    shard_grads.update(reduce_finish(sums_c, chips_c, names_c, "c", after=last))
    for nm in names_c:
        update(nm)

    order = ["w_ada", "b_ada", "norm_ffn1", "ffn1_w_gate", "ffn1_w_up", "ffn1_w_down", "norm_mix", "w_in", "q_norm",
             "k_norm", "conv_w", "w_attn_branch", "w_conv_branch", "w_out", "norm_ffn2", "ffn2_w_gate", "ffn2_w_up",
             "ffn2_w_down"]
    return (loss, dx0[None], *[res[nm][0] for nm in order], *[res[nm][1] for nm in order],
            *[res[nm][2] for nm in order], *[res[nm][3] for nm in order])
```

```python
import jax
import jax.numpy as jnp
from jax import lax
from jax.experimental import pallas as pl
from jax.experimental.pallas import tpu as pltpu
from jax.experimental.pallas import tpu_sc as plsc

F32 = jnp.float32
BF16 = jnp.bfloat16
MESH = pl.DeviceIdType.MESH
ANY = pl.BlockSpec(memory_space=pl.ANY)

NORM_EPS = 1e-6
HEAD_DIM = 128
N_GROUPS = 3
HEADS = 4
DILATIONS = (1, 4, 16)
ATTN_BLOCK = 128
SLAB = ATTN_BLOCK * max(DILATIONS)
QKV = N_GROUPS * HEADS * HEAD_DIM
ATTN_SCALE = HEAD_DIM ** -0.5
NEG = -1e30
N_CHIPS = 4

ADAM_LR = 0.001
ADAM_B1 = 0.9
ADAM_B2 = 0.999
ADAM_EPS = 1e-08
ADAM_WD = 0.01
ADAM_STEP = 10

VMEM_LIMIT_BYTES = 56 * 1024 * 1024
TOKEN_TILE = 512
FFN_TILE = 1024
PROJ_TILE = 2048
WGRAD_TILE = 2048
IN_BLOCK = 512
MIX_TILE = 256


def _params(n_axes=0):
    return pltpu.CompilerParams(
        dimension_semantics=("arbitrary",) * n_axes if n_axes else None,
        vmem_limit_bytes=VMEM_LIMIT_BYTES)


def _dot(a, b):
    return jnp.dot(a, b, preferred_element_type=F32)


def _dot_nt(a, b):
    return lax.dot_general(a, b, (((1,), (1,)), ((), ())), preferred_element_type=F32)


def _dot_tn(a, b):
    return lax.dot_general(a, b, (((0,), (0,)), ((), ())), preferred_element_type=F32)


def _sigmoid(x):
    return 1.0 / (1.0 + jnp.exp(-x))


def _place():
    return lax.axis_index("x"), lax.axis_index("y"), lax.axis_index("c")


def _ordered(body, n_in, after):
    if not after:
        return body
    return lambda *refs: body(*refs[:n_in], *refs[n_in + len(after):])


def _allgather8(block, name):
    m_per, n = block.shape

    def body(x_ref, out_ref, send_sems, recv_sems, local_sem):
        x, y, c = _place()
        me, sibling = (x, y, c), (x, y, 1 - c)
        chips = [(1 - x, y), (x, 1 - y), (1 - x, 1 - y)]

        def rows(px, py, pc):
            return out_ref.at[pl.ds((4 * px + 2 * py + pc) * m_per, m_per), :]

        def copy(k, blk, to, src=None):
            return pltpu.make_async_remote_copy(
                src_ref=rows(*blk) if src is None else src, dst_ref=rows(*blk),
                send_sem=send_sems.at[k], recv_sem=recv_sems.at[k],
                device_id=to, device_id_type=MESH)

        mine = pltpu.make_async_copy(x_ref, rows(*me), local_sem)
        mine.start()
        first = [copy(0, me, sibling, src=x_ref)]
        first += [copy(1 + j, me, (*chip, c), src=x_ref) for j, chip in enumerate(chips)]
        for cp in first:
            cp.start()
        passed = [copy(4 + j, (*chip, c), sibling) for j, chip in enumerate(chips)]
        for j, chip in enumerate(chips):
            copy(1 + j, (*chip, c), me).wait_recv()
            passed[j].start()
        copy(0, sibling, me).wait_recv()
        for j, chip in enumerate(chips):
            copy(4 + j, (*chip, 1 - c), me).wait_recv()
        for cp in first + passed:
            cp.wait_send()
        mine.wait()

    return pl.pallas_call(
        body, name=name,
        out_shape=jax.ShapeDtypeStruct((8 * m_per, n), block.dtype),
        in_specs=[pl.BlockSpec(memory_space=pltpu.VMEM)],
        out_specs=pl.BlockSpec(memory_space=pltpu.VMEM),
        scratch_shapes=[pltpu.SemaphoreType.DMA((7,)), pltpu.SemaphoreType.DMA((7,)),
                        pltpu.SemaphoreType.DMA],
        compiler_params=_params(),
    )(block)


def _handshake(peers):
    barrier = pltpu.get_barrier_semaphore()
    for peer in peers:
        pl.semaphore_signal(barrier, inc=1, device_id=peer, device_id_type=MESH)
    pl.semaphore_wait(barrier, len(peers))


def _gather_weights(shards, by_cols, name, collective_id, after=()):
    n_arr = len(shards)

    def body(*refs):
        srcs, outs = refs[:n_arr], refs[n_arr + len(after):2 * n_arr + len(after)]
        send_sems, recv_sems, local_sems = refs[2 * n_arr + len(after):]
        x, y, c = _place()
        me_dev, sibling = (x, y, c), (x, y, 1 - c)
        chips = [(1 - x, y), (x, 1 - y), (1 - x, 1 - y)]
        me = 2 * x + y
        _handshake([sibling] + [(*chip, c) for chip in chips])

        def place(k, chip_idx, rows):
            if by_cols[k]:
                width = srcs[k].shape[1]
                return outs[k].at[rows, pl.ds(pl.multiple_of(chip_idx * width, 128), width)]
            return outs[k].at[chip_idx, rows]

        def copy(k, slot, chip_idx, half_sel, to, from_shard=False):
            half = srcs[k].shape[0] // 2
            rows = pl.ds(half_sel * half, half)
            dst = place(k, chip_idx, rows)
            return pltpu.make_async_remote_copy(
                src_ref=srcs[k].at[rows] if from_shard else dst, dst_ref=dst,
                send_sem=send_sems.at[6 * k + slot], recv_sem=recv_sems.at[6 * k + slot],
                device_id=to, device_id_type=MESH)

        own = [pltpu.make_async_copy(srcs[k], place(k, me, pl.ds(0, srcs[k].shape[0])), local_sems.at[k])
               for k in range(n_arr)]
        for cp in own:
            cp.start()
        sent = []
        for k in range(n_arr):
            for j, chip in enumerate(chips):
                sent.append(copy(k, j, me, c, (*chip, c), from_shard=True))
                sent[-1].start()
        for k in range(n_arr):
            for j, chip in enumerate(chips):
                chip_idx = 2 * chip[0] + chip[1]
                copy(k, j, chip_idx, c, me_dev).wait_recv()
                sent.append(copy(k, 3 + j, chip_idx, c, sibling))
                sent[-1].start()
        for k in range(n_arr):
            for j, chip in enumerate(chips):
                copy(k, 3 + j, 2 * chip[0] + chip[1], 1 - c, me_dev).wait_recv()
        for cp in sent:
            cp.wait_send()
        for cp in own:
            cp.wait()

    def gathered(k):
        r, cols = shards[k].shape
        return (r, N_CHIPS * cols) if by_cols[k] else (N_CHIPS, r, cols)

    return pl.kernel(
        body, name=name,
        out_type=[jax.ShapeDtypeStruct(gathered(k), shards[k].dtype) for k in range(n_arr)],
        mesh=plsc.ScalarSubcoreMesh(axis_name="sequencer", num_cores=1),
        scratch_types=[pltpu.SemaphoreType.DMA((6 * n_arr,)), pltpu.SemaphoreType.DMA((6 * n_arr,)),
                       pltpu.SemaphoreType.DMA((n_arr,))],
        compiler_params=pltpu.CompilerParams(collective_id=collective_id),
    )(*shards, *after)


def _rs_pair_exchange(grads, name):
    n_arr = len(grads)

    def body(*refs):
        srcs, outs = refs[:n_arr], refs[n_arr:2 * n_arr]
        send_sems, recv_sems = refs[2 * n_arr:]
        x, y, c = _place()
        cps = []
        for k in range(n_arr):
            half = srcs[k].shape[1] // 2
            cps.append(pltpu.make_async_remote_copy(
                src_ref=srcs[k].at[:, pl.ds((1 - c) * half, half)], dst_ref=outs[k],
                send_sem=send_sems.at[k], recv_sem=recv_sems.at[k],
                device_id=(x, y, 1 - c), device_id_type=MESH))
            cps[-1].start()
        for cp in cps:
            cp.wait_recv()
        for cp in cps:
            cp.wait_send()

    return pl.pallas_call(
        body, name=name,
        out_shape=[jax.ShapeDtypeStruct((g.shape[0], g.shape[1] // 2, g.shape[2]), g.dtype) for g in grads],
        in_specs=[ANY] * n_arr, out_specs=[ANY] * n_arr,
        scratch_shapes=[pltpu.SemaphoreType.DMA((n_arr,)), pltpu.SemaphoreType.DMA((n_arr,))],
        compiler_params=_params(),
    )(*grads)


def _rs_chip_exchange(sums, name, collective_id):
    n_arr = len(sums)

    def body(*refs):
        srcs, outs = refs[:n_arr], refs[n_arr:2 * n_arr]
        send_sems, recv_sems = refs[2 * n_arr:]
        x, y, c = _place()
        chips = [(1 - x, y), (x, 1 - y), (1 - x, 1 - y)]
        _handshake([(*chip, c) for chip in chips])
        cps = []
        for k in range(n_arr):
            for j, chip in enumerate(chips):
                cps.append(pltpu.make_async_remote_copy(
                    src_ref=srcs[k].at[2 * chip[0] + chip[1]], dst_ref=outs[k].at[j],
                    send_sem=send_sems.at[3 * k + j], recv_sem=recv_sems.at[3 * k + j],
                    device_id=(*chip, c), device_id_type=MESH))
                cps[-1].start()
        for cp in cps:
            cp.wait_recv()
        for cp in cps:
            cp.wait_send()

    return pl.kernel(
        body, name=name,
        out_type=[jax.ShapeDtypeStruct((3,) + s.shape[1:], s.dtype) for s in sums],
        mesh=plsc.ScalarSubcoreMesh(axis_name="sequencer", num_cores=1),
        scratch_types=[pltpu.SemaphoreType.DMA((3 * n_arr,)), pltpu.SemaphoreType.DMA((3 * n_arr,))],
        compiler_params=pltpu.CompilerParams(collective_id=collective_id),
    )(*sums)


def _rs_share(totals, name):
    n_arr = len(totals)

    def body(*refs):
        outs = refs[n_arr:2 * n_arr]
        send_sems, recv_sems = refs[2 * n_arr:]
        x, y, c = _place()

        def half_rows(k, sel):
            return outs[k].at[sel]

        cps = []
        for k in range(n_arr):
            cps.append(pltpu.make_async_remote_copy(
                src_ref=half_rows(k, c), dst_ref=half_rows(k, c), send_sem=send_sems.at[k], recv_sem=recv_sems.at[k],
                device_id=(x, y, 1 - c), device_id_type=MESH))
            cps[-1].start()
        for k in range(n_arr):
            pltpu.make_async_remote_copy(
                src_ref=half_rows(k, c), dst_ref=half_rows(k, 1 - c), send_sem=send_sems.at[k],
                recv_sem=recv_sems.at[k], device_id=(x, y, 1 - c), device_id_type=MESH).wait_recv()
        for cp in cps:
            cp.wait_send()

    shared = pl.pallas_call(
        body, name=name,
        out_shape=[jax.ShapeDtypeStruct(t.shape, t.dtype) for t in totals],
        in_specs=[ANY] * n_arr, out_specs=[ANY] * n_arr,
        input_output_aliases={k: k for k in range(n_arr)},
        scratch_shapes=[pltpu.SemaphoreType.DMA((n_arr,)), pltpu.SemaphoreType.DMA((n_arr,))],
        compiler_params=_params(),
    )(*totals)
    return [t.reshape(2 * t.shape[1], t.shape[2]) for t in shared]


def _pair_add(grad, recv, c_idx, name):
    n, r, cols = grad.shape
    half = r // 2
    rows = half // 2

    def body(_, g_ref, r_ref, o_ref):
        o_ref[...] = (g_ref[...].astype(F32) + r_ref[...].astype(F32)).astype(o_ref.dtype)

    return pl.pallas_call(
        body, name=name,
        grid_spec=pltpu.PrefetchScalarGridSpec(
            num_scalar_prefetch=1, grid=(n, 2),
            in_specs=[pl.BlockSpec((None, None, rows, cols), lambda s, i, ci: (s, ci[0], i, 0)),
                      pl.BlockSpec((None, rows, cols), lambda s, i, ci: (s, i, 0))],
            out_specs=pl.BlockSpec((None, rows, cols), lambda s, i, ci: (s, i, 0))),
        out_shape=jax.ShapeDtypeStruct((n, half, cols), BF16),
        compiler_params=_params(2),
    )(c_idx, grad.reshape(n, 2, half, cols), recv)


def _chip_add(sums, recv, chip_and_core, name, after=()):
    _, half, cols = sums.shape
    rows = half // 2

    def body(_, s_ref, r0_ref, r1_ref, r2_ref, o_ref):
        o_ref[...] = ((s_ref[...].astype(F32) + r0_ref[...].astype(F32))
                      + r1_ref[...].astype(F32)) + r2_ref[...].astype(F32)

    def recv_spec(j):
        return pl.BlockSpec((None, rows, cols), lambda i, ci: (j, i, 0))

    return pl.pallas_call(
        _ordered(body, 5, after), name=name,
        grid_spec=pltpu.PrefetchScalarGridSpec(
            num_scalar_prefetch=1, grid=(2,),
            in_specs=[pl.BlockSpec((None, rows, cols), lambda i, ci: (ci[0], i, 0)),
                      recv_spec(0), recv_spec(1), recv_spec(2)] + [ANY] * len(after),
            out_specs=pl.BlockSpec((None, rows, cols), lambda i, ci: (ci[1], i, 0))),
        out_shape=jax.ShapeDtypeStruct((2, half, cols), F32),
        compiler_params=_params(1),
    )(chip_and_core, sums, recv, recv, recv, *after)


def _rms(x):
    return lax.rsqrt(jnp.mean(x * x, axis=-1, keepdims=True) + NORM_EPS)


def _norm_mod(x, mod, name, prev=None):
    s_len, d = x.shape
    tm = TOKEN_TILE

    def body(*refs):
        if prev is None:
            x_ref, mod_ref, h_ref, ht_ref = refs
            xv = x_ref[...]
        else:
            x_ref, y_ref, modp_ref, mod_ref, xo_ref, h_ref, ht_ref = refs
            xv = x_ref[...] + prev[2] * modp_ref[2:3, :] * y_ref[...]
            xo_ref[...] = xv
        n = (xv * _rms(xv)) * mod_ref[3:4, :]
        h = n * (1.0 + mod_ref[1:2, :]) + mod_ref[0:1, :]
        h_ref[...] = h.astype(BF16)
        ht_ref[...] = h.T.astype(BF16)

    tile = pl.BlockSpec((tm, d), lambda i: (i, 0))
    small = pl.BlockSpec((8, d), lambda i: (0, 0))
    h_specs = [tile, pl.BlockSpec((d, tm), lambda i: (0, i))]
    h_shapes = [jax.ShapeDtypeStruct((s_len, d), BF16), jax.ShapeDtypeStruct((d, s_len), BF16)]
    if prev is None:
        return pl.pallas_call(
            body, name=name, grid=(s_len // tm,), in_specs=[tile, small], out_specs=h_specs, out_shape=h_shapes,
            compiler_params=_params(1))(x, mod)
    return pl.pallas_call(
        body, name=name, grid=(s_len // tm,), in_specs=[tile, tile, small, small],
        out_specs=[tile] + h_specs, out_shape=[jax.ShapeDtypeStruct((s_len, d), F32)] + h_shapes,
        compiler_params=_params(1))(x, prev[0], prev[1], mod)


def _norm_bwd(dh, x, mod, dxo, y_raw, coef, name, after=(), prev=None):
    s_len, d = x.shape
    tm = TOKEN_TILE

    def body(*refs):
        if prev is None:
            dh_ref, x_ref, mod_ref, dxo_ref, y_ref, dx_ref, st_ref = refs
        else:
            dh_ref, x_ref, mod_ref, dxo_ref, y_ref, modp_ref, dx_ref, st_ref, dyp_ref = refs

        @pl.when(pl.program_id(0) == 0)
        def _():
            st_ref[...] = jnp.zeros_like(st_ref)

        xv, dhv, dxov = x_ref[...], dh_ref[...], dxo_ref[...]
        r = _rms(xv)
        xh = xv * r
        gain, scale = mod_ref[3:4, :], mod_ref[1:2, :]
        dn = dhv * (1.0 + scale)
        dxh = dn * gain
        dx = dxov + r * (dxh - xh * jnp.mean(dxh * xh, axis=-1, keepdims=True))
        dx_ref[...] = dx
        if prev is not None:
            dyp_ref[...] = (prev[1] * modp_ref[2:3, :] * dx).astype(BF16)
        st_ref[0:1, :] += jnp.sum(dhv, axis=0, keepdims=True)
        st_ref[1:2, :] += jnp.sum(dhv * (xh * gain), axis=0, keepdims=True)
        st_ref[2:3, :] += coef * jnp.sum(y_ref[...] * dxov, axis=0, keepdims=True)
        st_ref[3:4, :] += jnp.sum(dn * xh, axis=0, keepdims=True)

    tile = pl.BlockSpec((tm, d), lambda i: (i, 0))
    small = pl.BlockSpec((8, d), lambda i: (0, 0))
    operands = [dh, x, mod, dxo, y_raw] + ([] if prev is None else [prev[0]])
    in_specs = [tile, tile, small, tile, tile] + ([] if prev is None else [small])
    out_specs = [tile, small] + ([] if prev is None else [tile])
    out_shape = [jax.ShapeDtypeStruct((s_len, d), F32), jax.ShapeDtypeStruct((8, d), F32)]
    if prev is not None:
        out_shape.append(jax.ShapeDtypeStruct((s_len, d), BF16))
    return pl.pallas_call(
        _ordered(body, len(operands), after), name=name, grid=(s_len // tm,),
        in_specs=in_specs + [ANY] * len(after), out_specs=out_specs, out_shape=out_shape,
        compiler_params=_params(1),
    )(*operands, *after)


def _loss_grad(x, y, mod, target, name):
    s_len, d = x.shape
    tm = TOKEN_TILE

    def body(x_ref, y_ref, mod_ref, t_ref, do_ref, dy_ref, part_ref):
        @pl.when(pl.program_id(0) == 0)
        def _():
            part_ref[...] = jnp.zeros_like(part_ref)

        half_gate = 0.5 * mod_ref[2:3, :]
        err = (x_ref[...] + half_gate * y_ref[...]) - t_ref[...]
        do = err * (1.0 / d)
        do_ref[...] = do
        dy_ref[...] = (half_gate * do).astype(BF16)
        sq = err * err
        part_ref[...] += jnp.sum(sq.reshape(tm // 8, 8, d), axis=0)

    tile = pl.BlockSpec((tm, d), lambda i: (i, 0))
    small = pl.BlockSpec((8, d), lambda i: (0, 0))
    return pl.pallas_call(
        body, name=name, grid=(s_len // tm,),
        in_specs=[tile, tile, small, tile],
        out_specs=[tile, tile, small],
        out_shape=[jax.ShapeDtypeStruct((s_len, d), F32), jax.ShapeDtypeStruct((s_len, d), BF16),
                   jax.ShapeDtypeStruct((8, d), F32)],
        compiler_params=_params(1),
    )(x, y, mod, target)


def _adamw_math(w, g, m, v):
    m = ADAM_B1 * m + (1.0 - ADAM_B1) * g
    v = ADAM_B2 * v + (1.0 - ADAM_B2) * (g * g)
    m_hat = m / (1.0 - ADAM_B1 ** ADAM_STEP)
    v_hat = v / (1.0 - ADAM_B2 ** ADAM_STEP)
    delta = -ADAM_LR * (m_hat / (jnp.sqrt(v_hat) + ADAM_EPS) + ADAM_WD * w)
    return delta, m, v


def _adamw(w, g, m, v, name, after=()):
    r, cols = w.shape
    tr = r // 8 if r % 64 == 0 else r

    def body(w_ref, g_ref, m_ref, v_ref, go_ref, d_ref, nm_ref, nv_ref):
        gv = g_ref[...]
        go_ref[...] = gv
        d_ref[...], nm_ref[...], nv_ref[...] = _adamw_math(w_ref[...], gv, m_ref[...], v_ref[...])

    tile = pl.BlockSpec((tr, cols), lambda i: (i, 0))
    shape = jax.ShapeDtypeStruct((r, cols), F32)
    return pl.pallas_call(
        _ordered(body, 4, after), name=name, grid=(r // tr,),
        in_specs=[tile] * 4 + [ANY] * len(after), out_specs=[tile] * 4, out_shape=[shape] * 4,
        compiler_params=_params(1),
    )(w, g, m, v, *after)


def _in_parts(tm, n_qkv, n_rest):
    def part(lo, n_blk):
        return pl.BlockSpec((tm, IN_BLOCK), lambda i, j: (i, jnp.clip(j - lo, 0, n_blk - 1)))
    return [part(0, n_qkv), part(n_qkv, n_qkv), part(2 * n_qkv, n_qkv), part(3 * n_qkv, n_rest)]


def _pick_part(j, n_qkv, refs, fn):
    bounds = [0, n_qkv, 2 * n_qkv, 3 * n_qkv]
    for p, ref in enumerate(refs):
        inside = j >= bounds[p]
        if p + 1 < len(refs):
            inside = inside & (j < bounds[p + 1])
        pl.when(inside)(lambda ref=ref: fn(ref))


def _rows(base, count, stride):
    return pl.ds(base, count) if stride == 1 else pl.ds(base, count, stride=stride)


REORDER_STRIDE = 4


def _reorder_plan(dil):
    inner = min(dil, REORDER_STRIDE)
    return inner, dil // inner, SLAB // inner, SLAB // dil


def _to_residue_order(dst, src, dil, tmp):
    inner, outer, big, seg = _reorder_plan(dil)
    if outer == 1:
        for r in range(dil):
            dst[pl.ds(r * seg, seg), :] = src[_rows(r, seg, dil), :].astype(dst.dtype)
        return
    for b in range(inner):
        tmp[pl.ds(b * big, big), :] = src[_rows(b, big, inner), :]
    for a in range(outer):
        for b in range(inner):
            dst[pl.ds((inner * a + b) * seg, seg), :] = tmp[_rows(b * big + a, seg, outer), :].astype(dst.dtype)


def _to_token_order(dst, src, dil, tmp):
    inner, outer, big, seg = _reorder_plan(dil)
    if outer == 1:
        for r in range(dil):
            dst[_rows(r, seg, dil), :] = src[pl.ds(r * seg, seg), :]
        return
    for a in range(outer):
        for b in range(inner):
            tmp[_rows(b * big + a, seg, outer), :] = src[pl.ds((inner * a + b) * seg, seg), :]
    for b in range(inner):
        dst[_rows(b, big, inner), :] = tmp[pl.ds(b * big, big), :]


def _in_proj(h, w, q_norm, k_norm, name):
    s_len, d = h.shape
    tm = PROJ_TILE
    assert tm == SLAB and IN_BLOCK == HEADS * HEAD_DIM
    steps = w.shape[1] // IN_BLOCK
    n_qkv = 3 * QKV // IN_BLOCK

    def body(h_ref, w_ref, qn_ref, kn_ref, qkv_ref, rest_ref, hat_ref, tok_s, tmp_s):
        j = pl.program_id(1)
        res = _dot(h_ref[...], w_ref[...])

        def emit(sect, gi):
            dil = DILATIONS[gi]
            for hh in range(HEADS):
                cols = slice(hh * HEAD_DIM, (hh + 1) * HEAD_DIM)
                x = res[:, cols]
                if sect < 2:
                    x = (x * _rms(x)) * (qn_ref if sect == 0 else kn_ref)[...]
                tok_s[...] = x
                _to_residue_order(hat_ref.at[:, cols], tok_s, dil, tmp_s)

        @pl.when(j < n_qkv)
        def _():
            qkv_ref[...] = res

        for sect in range(3):
            for gi in range(N_GROUPS):
                pl.when(j == sect * N_GROUPS + gi)(lambda sect=sect, gi=gi: emit(sect, gi))

        @pl.when(j >= n_qkv)
        def _():
            rest_ref[...] = res.astype(BF16)

    qkv_blk = pl.BlockSpec((tm, IN_BLOCK), lambda i, j: (i, jnp.minimum(j, n_qkv - 1)))
    small = pl.BlockSpec((1, HEAD_DIM), lambda i, j: (0, 0))
    return pl.pallas_call(
        body, name=name, grid=(s_len // tm, steps),
        in_specs=[pl.BlockSpec((tm, d), lambda i, j: (i, 0)), pl.BlockSpec((d, IN_BLOCK), lambda i, j: (0, j)),
                  small, small],
        out_specs=[qkv_blk, pl.BlockSpec((tm, IN_BLOCK), lambda i, j: (i, jnp.maximum(j - n_qkv, 0))), qkv_blk],
        out_shape=[jax.ShapeDtypeStruct((s_len, 3 * QKV), F32),
                   jax.ShapeDtypeStruct((s_len, w.shape[1] - 3 * QKV), BF16),
                   jax.ShapeDtypeStruct((s_len, 3 * QKV), BF16)],
        scratch_shapes=[pltpu.VMEM((tm, HEAD_DIM), F32)] * 2,
        compiler_params=_params(2),
    )(h, w, q_norm, k_norm)


def _in_proj_bwd(dq, dk, dv, drest, w, name, after=()):
    s_len = dq.shape[0]
    d = w.shape[0]
    tm = PROJ_TILE
    steps = w.shape[1] // IN_BLOCK
    n_qkv = QKV // IN_BLOCK

    def body(dq_ref, dk_ref, dv_ref, dr_ref, w_ref, o_ref, acc_ref):
        j = pl.program_id(1)

        @pl.when(j == 0)
        def _():
            acc_ref[...] = jnp.zeros_like(acc_ref)

        def add(a_ref):
            acc_ref[...] += _dot_nt(a_ref[...], w_ref[...])

        _pick_part(j, n_qkv, [dq_ref, dk_ref, dv_ref, dr_ref], add)

        @pl.when(j == steps - 1)
        def _():
            o_ref[...] = acc_ref[...]

    return pl.pallas_call(
        _ordered(body, 5, after), name=name, grid=(s_len // tm, steps),
        in_specs=(_in_parts(tm, n_qkv, steps - 3 * n_qkv) + [pl.BlockSpec((d, IN_BLOCK), lambda i, j: (0, j))]
                  + [ANY] * len(after)),
        out_specs=pl.BlockSpec((tm, d), lambda i, j: (i, 0)),
        out_shape=jax.ShapeDtypeStruct((s_len, d), F32),
        scratch_shapes=[pltpu.VMEM((tm, d), F32)],
        compiler_params=_params(2),
    )(dq, dk, dv, drest, w, *after)


def _wgrad(x, y, x_spec, y_spec, out_shape, out_spec, acc_shape, n_chunks, name, x_transposed=False, after=()):
    s_len = y.shape[-2]
    ts = WGRAD_TILE
    steps = s_len // ts

    def body(x_ref, y_ref, o_ref, acc_ref):
        s = pl.program_id(1)

        @pl.when(s == 0)
        def _():
            acc_ref[...] = jnp.zeros_like(acc_ref)

        acc_ref[...] += (_dot if x_transposed else _dot_tn)(x_ref[...], y_ref[...])

        @pl.when(s == steps - 1)
        def _():
            o_ref[...] = acc_ref[...].astype(o_ref.dtype)

    return pl.pallas_call(
        _ordered(body, 2, after), name=name, grid=(n_chunks, steps),
        in_specs=[x_spec(ts), y_spec(ts)] + [ANY] * len(after), out_specs=out_spec,
        out_shape=jax.ShapeDtypeStruct(out_shape, BF16),
        scratch_shapes=[pltpu.VMEM(acc_shape, F32)],
        compiler_params=_params(2),
    )(x, y, *after)


def _pieces(width, piece=256):
    return [slice(a, min(a + piece, width)) for a in range(0, width, piece)]


def _ffn_fwd(h, w_gate, w_up, w_down, name):
    s_len, d = h.shape
    n_chunks, _, fs = w_gate.shape
    tm = FFN_TILE

    def body(h_ref, wg_ref, wu_ref, wd_ref, g_ref, u_ref, y_ref):
        j = pl.program_id(1)
        hv = h_ref[...]
        pieces = _pieces(fs)
        first = lambda cols: (_dot(hv, wg_ref[:, cols]), _dot(hv, wu_ref[:, cols]))
        total = None
        ahead = first(pieces[0])
        for k, cols in enumerate(pieces):
            g, u = ahead
            if k + 1 < len(pieces):
                ahead = first(pieces[k + 1])
            g_ref[:, cols] = g.astype(BF16)
            u_ref[:, cols] = u.astype(BF16)
            act = (g * _sigmoid(g)) * u
            part = _dot(act.astype(BF16), wd_ref[cols, :])
            total = part if total is None else total + part

        @pl.when(j == 0)
        def _():
            y_ref[...] = total

        @pl.when(j > 0)
        def _():
            y_ref[...] += total

    tile = pl.BlockSpec((tm, d), lambda i, j: (i, 0))
    hid = pl.BlockSpec((None, tm, fs), lambda i, j: (j, i, 0))
    w_in_spec = pl.BlockSpec((None, d, fs), lambda i, j: (j, 0, 0))
    hid_shape = jax.ShapeDtypeStruct((n_chunks, s_len, fs), BF16)
    return pl.pallas_call(
        body, name=name, grid=(s_len // tm, n_chunks),
        in_specs=[tile, w_in_spec, w_in_spec, pl.BlockSpec((None, fs, d), lambda i, j: (j, 0, 0))],
        out_specs=[hid, hid, tile],
        out_shape=[hid_shape, hid_shape, jax.ShapeDtypeStruct((s_len, d), F32)],
        compiler_params=_params(2),
    )(h, w_gate, w_up, w_down)


def _ffn_bwd(dy, g_pre, u_pre, w_gate, w_up, w_down, name):
    s_len, d = dy.shape
    n_chunks, _, fs = w_gate.shape
    tm = FFN_TILE

    def body(dy_ref, g_ref, u_ref, wg_ref, wu_ref, wd_ref, dh_ref, dg_ref, du_ref, a_ref):
        j = pl.program_id(1)
        dyv = dy_ref[...]
        pieces = _pieces(fs)
        first = lambda cols: _dot_nt(dyv, wd_ref[cols, :])
        total = None
        ahead = first(pieces[0])
        for k, cols in enumerate(pieces):
            da = ahead
            if k + 1 < len(pieces):
                ahead = first(pieces[k + 1])
            g = g_ref[:, cols].astype(F32)
            u = u_ref[:, cols].astype(F32)
            sg = _sigmoid(g)
            silu = g * sg
            dg = (da * u * (sg * (1.0 + g * (1.0 - sg)))).astype(BF16)
            du = (da * silu).astype(BF16)
            dg_ref[:, cols] = dg
            du_ref[:, cols] = du
            a_ref[:, cols] = (silu * u).astype(BF16)
            part = _dot_nt(dg, wg_ref[:, cols]) + _dot_nt(du, wu_ref[:, cols])
            total = part if total is None else total + part

        @pl.when(j == 0)
        def _():
            dh_ref[...] = total

        @pl.when(j > 0)
        def _():
            dh_ref[...] += total

    tile = pl.BlockSpec((tm, d), lambda i, j: (i, 0))
    hid = pl.BlockSpec((None, tm, fs), lambda i, j: (j, i, 0))
    w_in_spec = pl.BlockSpec((None, d, fs), lambda i, j: (j, 0, 0))
    hid_shape = jax.ShapeDtypeStruct((n_chunks, s_len, fs), BF16)
    return pl.pallas_call(
        body, name=name, grid=(s_len // tm, n_chunks),
        in_specs=[tile, hid, hid, w_in_spec, w_in_spec, pl.BlockSpec((None, fs, d), lambda i, j: (j, 0, 0))],
        out_specs=[tile, hid, hid, hid],
        out_shape=[jax.ShapeDtypeStruct((s_len, d), F32), hid_shape, hid_shape, hid_shape],
        compiler_params=_params(2),
    )(dy, g_pre, u_pre, w_gate, w_up, w_down)


def _ffn_wgrads(ht, dg, du, act, dy, tag, after=()):
    n_chunks, s_len, fs = dg.shape
    d = ht.shape[0]
    tok = lambda ts: pl.BlockSpec((ts, d), lambda c, s: (s, 0))
    tok_t = lambda ts: pl.BlockSpec((d, ts), lambda c, s: (0, s))
    hid = lambda ts: pl.BlockSpec((None, ts, fs), lambda c, s: (c, s, 0))
    d_up = pl.BlockSpec((None, d, fs), lambda c, s: (c, 0, 0))
    d_down = pl.BlockSpec((None, fs, d), lambda c, s: (c, 0, 0))
    dwg = _wgrad(ht, dg, tok_t, hid, (n_chunks, d, fs), d_up, (d, fs), n_chunks, tag + "_dwg", True, after)
    dwu = _wgrad(ht, du, tok_t, hid, (n_chunks, d, fs), d_up, (d, fs), n_chunks, tag + "_dwu", True, after)
    dwd = _wgrad(act, dy, hid, tok, (n_chunks, fs, d), d_down, (fs, d), n_chunks, tag + "_dwd", False, after)
    return dwg, dwu, dwd


def _band_bias():
    qi = lax.broadcasted_iota(jnp.int32, (ATTN_BLOCK, 2 * ATTN_BLOCK), 0)
    kj = lax.broadcasted_iota(jnp.int32, (ATTN_BLOCK, 2 * ATTN_BLOCK), 1)
    band = (kj >= qi) & (kj <= qi + ATTN_BLOCK)
    return jnp.where(band, 0.0, NEG), jnp.where(band & (kj >= ATTN_BLOCK), 0.0, NEG)


def _qkv_specs(slab_of, sections):
    def spec(sect, back):
        return pl.BlockSpec((SLAB, HEAD_DIM),
                            lambda h, s, g: (jnp.maximum(slab_of(s) - back, 0), (sect * N_GROUPS + g) * HEADS + h))
    return [spec(sect, back) for sect, back in sections]


HAT_BLOCKS = [(0, 0), (1, 0), (2, 0), (1, 1), (2, 1)]


def _stage_keys(k_ref, v_ref, kp_ref, vp_ref, kbuf, vbuf, dil, n):
    run = SLAB // dil
    for r in range(dil):
        own, before = pl.ds(r * run, run), pl.ds(2 * r * run, run)
        kbuf[pl.ds((2 * r + 1) * run, run), :] = k_ref[own, :]
        vbuf[pl.ds((2 * r + 1) * run, run), :] = v_ref[own, :]

        @pl.when(n > 0)
        def _():
            kbuf[before, :] = kp_ref[own, :]
            vbuf[before, :] = vp_ref[own, :]

        @pl.when(n == 0)
        def _():
            kbuf[before, :] = jnp.zeros((run, HEAD_DIM), BF16)
            vbuf[before, :] = jnp.zeros((run, HEAD_DIM), BF16)


def _for_each_tile(dil, n, tile_fn):
    run = SLAB // dil
    bias, first_bias = _band_bias()
    for jj in range(run // ATTN_BLOCK):
        start = jj * ATTN_BLOCK
        tile_bias = jnp.where(n == 0, first_bias, bias) if jj == 0 else bias
        for r in range(dil):
            tile_fn(pl.ds(r * run + start, ATTN_BLOCK),
                    pl.ds((2 * r + 1) * run - ATTN_BLOCK + start, 2 * ATTN_BLOCK), tile_bias)


def _attn_fwd(hat, name):
    s_len = hat.shape[0]
    e = HEAD_DIM
    n_slabs = s_len // SLAB

    def body(q_ref, k_ref, v_ref, kp_ref, vp_ref, o_ref, lse_ref, kbuf, vbuf, m_s, l_s, acc_s, m_p, l_p, acc_p, tmp_s):
        n, grp = pl.program_id(1), pl.program_id(2)

        def run(gi, dil):
            _stage_keys(k_ref, v_ref, kp_ref, vp_ref, kbuf, vbuf, dil, n)

            def tile(q_rows, kv_rows, bias):
                s = _dot_nt(q_ref[q_rows, :], kbuf[kv_rows, :]) * ATTN_SCALE + bias
                m = jnp.max(s, axis=-1, keepdims=True)
                p = jnp.exp(s - m)
                m_p[q_rows, :] = jnp.broadcast_to(m, (ATTN_BLOCK, e))
                l_p[q_rows, :] = jnp.broadcast_to(jnp.sum(p, axis=-1, keepdims=True), (ATTN_BLOCK, e))
                acc_p[q_rows, :] = _dot(p.astype(BF16), vbuf[kv_rows, :])

            _for_each_tile(dil, n, tile)
            _to_token_order(m_s.at[gi], m_p, dil, tmp_s)
            _to_token_order(l_s.at[gi], l_p, dil, tmp_s)
            _to_token_order(acc_s.at[gi], acc_p, dil, tmp_s)

        for gi, dil in enumerate(DILATIONS):
            pl.when(grp == gi)(lambda gi=gi, dil=dil: run(gi, dil))

        @pl.when(grp == N_GROUPS - 1)
        def _():
            m_all = jnp.maximum(jnp.maximum(m_s[0], m_s[1]), m_s[2])
            den = jnp.zeros((SLAB, e), F32)
            num = jnp.zeros((SLAB, e), F32)
            for gi in range(N_GROUPS):
                w = jnp.exp(m_s[gi] - m_all)
                den += l_s[gi] * w
                num += acc_s[gi] * w
            o_ref[...] = num / den
            lse_ref[...] = m_all + jnp.log(den)

    out = pl.BlockSpec((SLAB, e), lambda h, n, g: (n, h))
    return pl.pallas_call(
        body, name=name, grid=(HEADS, n_slabs, N_GROUPS),
        in_specs=_qkv_specs(lambda n: n, HAT_BLOCKS),
        out_specs=[out, out],
        out_shape=[jax.ShapeDtypeStruct((s_len, HEADS * e), F32)] * 2,
        scratch_shapes=[pltpu.VMEM((2 * SLAB, e), BF16), pltpu.VMEM((2 * SLAB, e), BF16),
                        pltpu.VMEM((N_GROUPS, SLAB, e), F32), pltpu.VMEM((N_GROUPS, SLAB, e), F32),
                        pltpu.VMEM((N_GROUPS, SLAB, e), F32)]
        + [pltpu.VMEM((SLAB, e), F32)] * 4,
        compiler_params=_params(3),
    )(hat, hat, hat, hat, hat)


def _attn_bwd(qkv, hat, d_out, out, lse, q_norm, k_norm, name):
    s_len = qkv.shape[0]
    e = HEAD_DIM
    n_slabs = s_len // SLAB

    def body(q_ref, k_ref, v_ref, kp_ref, vp_ref, qraw_ref, kraw_ref, do_ref, o_ref, lse_ref, qn_ref, kn_ref,
             dq_ref, dk_ref, dv_ref, st_ref, kbuf, vbuf, stat_s, dqs, dkb, dvb, dk_tok, dv_tok, carry,
             do_p, stat_p, dq_p, dk_p, dv_p, tmp_s, do16_p):
        head, step, grp = pl.program_id(0), pl.program_id(1), pl.program_id(2)
        n = n_slabs - 1 - step
        dkb[...] = jnp.zeros_like(dkb)
        dvb[...] = jnp.zeros_like(dvb)
        @pl.when(grp == 0)
        def _():
            lane = lax.broadcasted_iota(jnp.int32, (SLAB, e), 1)
            stat_s[...] = jnp.where(lane < e // 2, lse_ref[...],
                                    jnp.sum(do_ref[...] * o_ref[...], axis=-1, keepdims=True))

        @pl.when((head == 0) & (step == 0) & (grp == 0))
        def _():
            st_ref[...] = jnp.zeros_like(st_ref)

        def run(gi, dil):
            seg = SLAB // dil
            _stage_keys(k_ref, v_ref, kp_ref, vp_ref, kbuf, vbuf, dil, n)

            @pl.when(step == 0)
            def _():
                carry[gi] = jnp.zeros((2, SLAB, e), F32)

            _to_residue_order(do_p, do_ref, dil, tmp_s)
            do16_p[...] = do_p[...].astype(BF16)
            _to_residue_order(stat_p, stat_s, dil, tmp_s)

            def tile(q_rows, kv_rows, bias):
                q = q_ref[q_rows, :]
                k = kbuf[kv_rows, :]
                v = vbuf[kv_rows, :]
                stat = stat_p[q_rows, :]
                do16 = do16_p[q_rows, :]
                s = _dot_nt(q, k) * ATTN_SCALE + bias
                p = jnp.exp(s - stat[:, 0:1])
                ds = (p * (_dot_nt(do16, v) - stat[:, e // 2:e // 2 + 1]) * ATTN_SCALE).astype(BF16)
                dq_p[q_rows, :] = _dot(ds, k)
                dkb[kv_rows, :] += _dot_tn(ds, q)
                dvb[kv_rows, :] += _dot_tn(p.astype(BF16), do16)

            _for_each_tile(dil, n, tile)
            for r in range(dil):
                own, before = pl.ds((2 * r + 1) * seg, seg), pl.ds(2 * r * seg, seg)
                kept = pl.ds(r * seg, seg)
                dk_p[kept, :] = dkb[own, :] + carry.at[gi, 0][kept, :]
                dv_p[kept, :] = dvb[own, :] + carry.at[gi, 1][kept, :]
                carry.at[gi, 0][kept, :] = dkb[before, :]
                carry.at[gi, 1][kept, :] = dvb[before, :]
            _to_token_order(dqs, dq_p, dil, tmp_s)
            _to_token_order(dk_tok, dk_p, dil, tmp_s)
            _to_token_order(dv_tok, dv_p, dil, tmp_s)

            def norm_bwd(raw, gain, d_hat):
                r = _rms(raw)
                y = raw * r
                dy = d_hat * gain
                return r * (dy - y * jnp.mean(dy * y, axis=-1, keepdims=True)), jnp.sum(d_hat * y, axis=0, keepdims=True)

            dq, dqn = norm_bwd(qraw_ref[...], qn_ref[...], dqs[...])
            dk, dkn = norm_bwd(kraw_ref[...], kn_ref[...], dk_tok[...])
            dq_ref[...] = dq.astype(BF16)
            dk_ref[...] = dk.astype(BF16)
            dv_ref[...] = dv_tok[...].astype(BF16)
            st_ref[0:1, :] += dqn
            st_ref[1:2, :] += dkn

        for gi, dil in enumerate(DILATIONS):
            pl.when(grp == gi)(lambda gi=gi, dil=dil: run(gi, dil))

    slab_of = lambda s: n_slabs - 1 - s
    small = pl.BlockSpec((1, e), lambda h, s, g: (0, 0))
    head_blk = pl.BlockSpec((SLAB, e), lambda h, s, g: (slab_of(s), h))
    grad_blk = pl.BlockSpec((SLAB, e), lambda h, s, g: (slab_of(s), g * HEADS + h))
    grad_shape = jax.ShapeDtypeStruct((s_len, QKV), BF16)
    return pl.pallas_call(
        body, name=name, grid=(HEADS, n_slabs, N_GROUPS),
        in_specs=(_qkv_specs(slab_of, HAT_BLOCKS) + _qkv_specs(slab_of, [(0, 0), (1, 0)])
                  + [head_blk, head_blk, head_blk, small, small]),
        out_specs=[grad_blk, grad_blk, grad_blk, pl.BlockSpec((8, e), lambda h, s, g: (0, 0))],
        out_shape=[grad_shape, grad_shape, grad_shape, jax.ShapeDtypeStruct((8, e), F32)],
        scratch_shapes=[pltpu.VMEM((2 * SLAB, e), BF16), pltpu.VMEM((2 * SLAB, e), BF16), pltpu.VMEM((SLAB, e), F32),
                        pltpu.VMEM((SLAB, e), F32), pltpu.VMEM((2 * SLAB, e), F32), pltpu.VMEM((2 * SLAB, e), F32),
                        pltpu.VMEM((SLAB, e), F32), pltpu.VMEM((SLAB, e), F32),
                        pltpu.VMEM((N_GROUPS, 2, SLAB, e), F32)]
        + [pltpu.VMEM((SLAB, e), F32)] * 6 + [pltpu.VMEM((SLAB, e), BF16)],
        compiler_params=_params(3),
    )(hat, hat, hat, hat, hat, qkv, qkv, d_out, out, lse, q_norm, k_norm)


def _shift_rows(x, by, edge, forward):
    t_len = x.shape[0]
    row = lax.broadcasted_iota(jnp.int32, x.shape, 0)
    if forward:
        out = pltpu.roll(x, by, 0)
        for i in range(by):
            out = jnp.where(row == i, edge[8 - by + i:8 - by + i + 1, :], out)
    else:
        out = pltpu.roll(x, t_len - by, 0)
        for i in range(by):
            out = jnp.where(row == t_len - by + i, edge[i:i + 1, :], out)
    return out


def _mix_fwd(x, o, rest, mod, conv_w, w_attn, w_conv, w_out, name):
    s_len, d = x.shape
    tm = MIX_TILE
    a_w = o.shape[1]

    def body(x_ref, o_ref, u_ref, b_ref, c_ref, ga_ref, gc_ref, mod_ref, cw_ref, wa_ref, wc_ref, wo_ref,
             xo_ref, z_ref, ya_ref, yc_ref, conv_ref, yb_ref, m_ref, o16_ref, carry):
        @pl.when(pl.program_id(0) == 0)
        def _():
            carry[...] = jnp.zeros_like(carry)

        xc = c_ref[...].astype(F32) * u_ref[...].astype(F32)
        edge = carry[...]
        conv = (_shift_rows(xc, 2, edge, True) * cw_ref[0:1, :] + _shift_rows(xc, 1, edge, True) * cw_ref[1:2, :]
                + xc * cw_ref[2:3, :])
        carry[...] = xc[tm - 8:tm, :]
        yb = (b_ref[...].astype(F32) * conv).astype(BF16)
        o16 = o_ref[...].astype(BF16)
        ya = _dot(o16, wa_ref[...])
        yc = _dot(yb, wc_ref[...])
        merged = (_sigmoid(ga_ref[...].astype(F32)) * ya + _sigmoid(gc_ref[...].astype(F32)) * yc).astype(BF16)
        z = _dot(merged, wo_ref[...])
        xo_ref[...] = x_ref[...] + mod_ref[2:3, :] * z
        z_ref[...] = z
        ya_ref[...] = ya.astype(BF16)
        yc_ref[...] = yc.astype(BF16)
        conv_ref[...] = conv.astype(BF16)
        yb_ref[...] = yb
        m_ref[...] = merged
        o16_ref[...] = o16

    tile = pl.BlockSpec((tm, d), lambda i: (i, 0))
    sect = lambda k: pl.BlockSpec((tm, d), lambda i: (i, k))
    att = pl.BlockSpec((tm, a_w), lambda i: (i, 0))
    const = lambda shape: pl.BlockSpec(shape, lambda i: (0, 0))
    f32_out = jax.ShapeDtypeStruct((s_len, d), F32)
    b16_out = jax.ShapeDtypeStruct((s_len, d), BF16)
    return pl.pallas_call(
        body, name=name, grid=(s_len // tm,),
        in_specs=[tile, att, sect(0), sect(1), sect(2), sect(3), sect(4), const((8, d)), const((8, d)),
                  const((a_w, d)), const((d, d)), const((d, d))],
        out_specs=[tile] * 7 + [att],
        out_shape=[f32_out, f32_out] + [b16_out] * 5 + [jax.ShapeDtypeStruct((s_len, a_w), BF16)],
        scratch_shapes=[pltpu.VMEM((8, d), F32)],
        compiler_params=_params(1),
    )(x, o, rest, rest, rest, rest, rest, mod, conv_w, w_attn, w_conv, w_out)


def _mix_bwd(dxo, ya, yc, conv, rest, mod, conv_w, w_attn, w_conv, w_out, a_w, name):
    s_len, d = dxo.shape
    tm = MIX_TILE
    n_tiles = s_len // tm

    def body(dxo_ref, ya_ref, yc_ref, conv_ref, u_ref, b_ref, c_ref, ga_ref, gc_ref, mod_ref, cw_ref,
             wa_ref, wc_ref, wo_ref, do_ref, drest_ref, dz_ref, dya_ref, dyc_ref, st_ref, carry):
        @pl.when(pl.program_id(0) == 0)
        def _():
            carry[...] = jnp.zeros_like(carry)
            st_ref[...] = jnp.zeros_like(st_ref)

        dz = (mod_ref[2:3, :] * dxo_ref[...]).astype(BF16)
        dz_ref[...] = dz
        dm = _dot_nt(dz, wo_ref[...])
        sa, sc = _sigmoid(ga_ref[...].astype(F32)), _sigmoid(gc_ref[...].astype(F32))
        dya = (dm * sa).astype(BF16)
        dyc = (dm * sc).astype(BF16)
        dya_ref[...] = dya
        dyc_ref[...] = dyc
        drest_ref[:, 3 * d:4 * d] = (dm * ya_ref[...].astype(F32) * (sa * (1.0 - sa))).astype(BF16)
        drest_ref[:, 4 * d:5 * d] = (dm * yc_ref[...].astype(F32) * (sc * (1.0 - sc))).astype(BF16)
        do_ref[...] = _dot_nt(dya, wa_ref[...])
        dyb = _dot_nt(dyc, wc_ref[...])
        drest_ref[:, d:2 * d] = (dyb * conv_ref[...].astype(F32)).astype(BF16)
        dconv = dyb * b_ref[...].astype(F32)
        edge = carry[...]
        sh1 = _shift_rows(dconv, 1, edge, False)
        sh2 = _shift_rows(dconv, 2, edge, False)
        carry[...] = dconv[0:8, :]
        dxc = dconv * cw_ref[2:3, :] + sh1 * cw_ref[1:2, :] + sh2 * cw_ref[0:1, :]
        u, c = u_ref[...].astype(F32), c_ref[...].astype(F32)
        xc = c * u
        drest_ref[:, 0:d] = (dxc * c).astype(BF16)
        drest_ref[:, 2 * d:3 * d] = (dxc * u).astype(BF16)
        st_ref[0:1, :] += jnp.sum(xc * sh2, axis=0, keepdims=True)
        st_ref[1:2, :] += jnp.sum(xc * sh1, axis=0, keepdims=True)
        st_ref[2:3, :] += jnp.sum(xc * dconv, axis=0, keepdims=True)

    rev = lambda i: n_tiles - 1 - i
    tile = pl.BlockSpec((tm, d), lambda i: (rev(i), 0))
    sect = lambda k: pl.BlockSpec((tm, d), lambda i: (rev(i), k))
    const = lambda shape: pl.BlockSpec(shape, lambda i: (0, 0))
    b16_out = jax.ShapeDtypeStruct((s_len, d), BF16)
    return pl.pallas_call(
        body, name=name, grid=(n_tiles,),
        in_specs=[tile, tile, tile, tile, sect(0), sect(1), sect(2), sect(3), sect(4), const((8, d)), const((8, d)),
                  const((a_w, d)), const((d, d)), const((d, d))],
        out_specs=[pl.BlockSpec((tm, a_w), lambda i: (rev(i), 0)), pl.BlockSpec((tm, 5 * d), lambda i: (rev(i), 0)),
                   tile, tile, tile, const((8, d))],
        out_shape=[jax.ShapeDtypeStruct((s_len, a_w), F32), jax.ShapeDtypeStruct((s_len, 5 * d), BF16),
                   b16_out, b16_out, b16_out, jax.ShapeDtypeStruct((8, d), F32)],
        scratch_shapes=[pltpu.VMEM((8, d), F32)],
        compiler_params=_params(1),
    )(dxo, ya, yc, conv, rest, rest, rest, rest, rest, mod, conv_w, w_attn, w_conv, w_out)


ADA_COLS = 128


def _ada_fwd(c_all, w_shard, b_shard, name):
    d, cols = w_shard.shape

    def body(c_ref, w_ref, b_ref, o_ref):
        cv = c_ref[...]
        o_ref[...] = jnp.dot(cv * _sigmoid(cv), w_ref[...], preferred_element_type=F32,
                             precision=lax.Precision.HIGHEST) + b_ref[...]

    return pl.pallas_call(
        body, name=name, grid=(cols // ADA_COLS,),
        in_specs=[pl.BlockSpec((8, d), lambda j: (0, 0)), pl.BlockSpec((d, ADA_COLS), lambda j: (0, j)),
                  pl.BlockSpec((1, ADA_COLS), lambda j: (0, j))],
        out_specs=pl.BlockSpec((8, ADA_COLS), lambda j: (0, j)),
        out_shape=jax.ShapeDtypeStruct((8, cols), F32),
        compiler_params=_params(1),
    )(c_all, w_shard, b_shard)


def _ada_bwd(c_all, dmod_shard, w, m, v, name):
    d, cols = w.shape

    def body(c_ref, dm_ref, w_ref, m_ref, v_ref, g_ref, d_ref, nm_ref, nv_ref):
        cv = c_ref[...]
        g = lax.dot_general(cv * _sigmoid(cv), dm_ref[...], (((0,), (0,)), ((), ())),
                            preferred_element_type=F32, precision=lax.Precision.HIGHEST)
        g_ref[...] = g
        d_ref[...], nm_ref[...], nv_ref[...] = _adamw_math(w_ref[...], g, m_ref[...], v_ref[...])

    blk = pl.BlockSpec((d, ADA_COLS), lambda j: (0, j))
    shape = jax.ShapeDtypeStruct((d, cols), F32)
    return pl.pallas_call(
        body, name=name, grid=(cols // ADA_COLS,),
        in_specs=[pl.BlockSpec((8, d), lambda j: (0, 0)), pl.BlockSpec((8, ADA_COLS), lambda j: (0, j)), blk, blk, blk],
        out_specs=[blk] * 4, out_shape=[shape] * 4,
        compiler_params=_params(1),
    )(c_all, dmod_shard, w, m, v)


def _small_update(parts, w, m, v, name):
    n = w.shape[1]

    def body(p_ref, w_ref, m_ref, v_ref, g_ref, d_ref, nm_ref, nv_ref):
        g = p_ref[0:1, :]
        for i in range(1, 8):
            g = g + p_ref[i:i + 1, :]
        g_ref[...] = g
        d_ref[...], nm_ref[...], nv_ref[...] = _adamw_math(w_ref[...], g, m_ref[...], v_ref[...])

    shape = jax.ShapeDtypeStruct((1, n), F32)
    return pl.pallas_call(body, name=name, out_shape=[shape] * 4, compiler_params=_params())(parts, w, m, v)


def _cols_to_shards(w, n):
    r, nc = w.shape
    return w.reshape(r, n, nc // n).transpose(1, 0, 2)


def kernel(x, c, w_ada, b_ada, norm_ffn1, ffn1_w_gate, ffn1_w_up, ffn1_w_down, norm_mix, w_in, q_norm, k_norm, conv_w, w_attn_branch, w_conv_branch, w_out, norm_ffn2, ffn2_w_gate, ffn2_w_up, ffn2_w_down, loss_target, m_w_ada, m_b_ada, m_norm_ffn1, m_ffn1_w_gate, m_ffn1_w_up, m_ffn1_w_down, m_norm_mix, m_w_in, m_q_norm, m_k_norm, m_conv_w, m_w_attn_branch, m_w_conv_branch, m_w_out, m_norm_ffn2, m_ffn2_w_gate, m_ffn2_w_up, m_ffn2_w_down, v_w_ada, v_b_ada, v_norm_ffn1, v_ffn1_w_gate, v_ffn1_w_up, v_ffn1_w_down, v_norm_mix, v_w_in, v_q_norm, v_k_norm, v_conv_w, v_w_attn_branch, v_w_conv_branch, v_w_out, v_norm_ffn2, v_ffn2_w_gate, v_ffn2_w_up, v_ffn2_w_down):
    ix, iy, ic = _place()
    chip = 2 * ix + iy
    me = 4 * ix + 2 * iy + ic
    xs = x[0]
    target = loss_target[0]
    s_len, d = xs.shape
    ada_cols = w_ada.shape[2]
    conv_cols = conv_w.shape[2]

    conv_rows = jnp.zeros((8, conv_cols), F32).at[0:3].set(conv_w[0])
    small_in = jnp.concatenate([jnp.broadcast_to(c, (8, d)), conv_rows], axis=1)
    small_all = _allgather8(small_in, "gather_c").reshape(8, 8, d + conv_cols)
    c_all = small_all[:, 0, :d]
    conv_full = small_all[0::2, 0:3, d:].transpose(1, 0, 2).reshape(3, N_CHIPS * conv_cols)
    conv_pad = jnp.zeros((8, N_CHIPS * conv_cols), F32).at[0:3].set(conv_full)
    b_shard = lax.dynamic_slice(b_ada, (0, chip * ada_cols), (1, ada_cols))
    mod_part = _ada_fwd(c_all, w_ada[0], b_shard, "ada_fwd")
    mod_all = _allgather8(mod_part, "gather_mod").reshape(N_CHIPS, 2, 8, ada_cols)[:, 0]
    mod_mine = lax.dynamic_slice(mod_all, (0, me, 0), (N_CHIPS, 1, ada_cols)).reshape(9, d)

    def mod_rows(i, gain):
        return jnp.zeros((8, d), F32).at[0:3].set(mod_mine[3 * i:3 * i + 3]).at[3:4].set(gain)

    mod1, mod2, mod3 = mod_rows(0, norm_ffn1), mod_rows(1, norm_mix), mod_rows(2, norm_ffn2)

    to16 = lambda w: w[0].astype(BF16)
    wg1, wu1, wd1 = _gather_weights([to16(ffn1_w_gate), to16(ffn1_w_up), to16(ffn1_w_down)], [False] * 3,
                                    "gather_ffn1", 1)
    h1, h1t = _norm_mod(xs, mod1, "norm1")
    (w_in_full,) = _gather_weights([to16(w_in)], [True], "gather_w_in", 2, after=(wd1, h1))

    g1, u1, y1 = _ffn_fwd(h1, wg1, wu1, wd1, "ffn1_fwd")
    x1, h2, h2t = _norm_mod(xs, mod2, "norm2", prev=(y1, mod1, 0.5))
    qkv, rest, qkv_hat = _in_proj(h2, w_in_full, q_norm, k_norm, "in_proj")
    w_ab, w_cb_g, w_o_g, wg2, wu2, wd2 = _gather_weights(
        [to16(w_attn_branch), to16(w_conv_branch), to16(w_out),
         to16(ffn2_w_gate), to16(ffn2_w_up), to16(ffn2_w_down)], [True] + [False] * 5,
        "gather_rest", 3, after=(h2,))
    a_w = w_ab.shape[0]
    w_cb = w_cb_g.reshape(d, d)
    w_o = w_o_g.reshape(d, d)
    o, lse = _attn_fwd(qkv_hat, "attn_fwd")
    x2, z, ya, yc, conv, yb, merged, o16 = _mix_fwd(x1, o, rest, mod2, conv_pad, w_ab, w_cb, w_o, "mix_fwd")
    h3, h3t = _norm_mod(x2, mod3, "norm3")
    g3, u3, y3 = _ffn_fwd(h3, wg2, wu2, wd2, "ffn2_fwd")
    dx3, dy3, loss_part = _loss_grad(x2, y3, mod3, target, "loss")
    loss = lax.psum(0.5 * jnp.sum(loss_part) / d, ("x", "y", "c"))

    c_idx = jnp.reshape(ic, (1,)).astype(jnp.int32)
    chip_idx = jnp.stack([chip, ic]).astype(jnp.int32)

    def reduce_start(grads, names, tag, collective_id):
        from_sibling = _rs_pair_exchange(grads, "rs_pair_" + tag)
        pair_sums = [_pair_add(g, r, c_idx, "pair_add_" + nm) for g, r, nm in zip(grads, from_sibling, names)]
        return pair_sums, _rs_chip_exchange(pair_sums, "rs_chips_" + tag, collective_id)

    def reduce_finish(pair_sums, from_chips, names, tag, after):
        totals = [_chip_add(p, r, chip_idx, "chip_add_" + nm, after)
                  for p, r, nm in zip(pair_sums, from_chips, names)]
        return dict(zip(names, _rs_share(totals, "rs_share_" + tag)))

    names_a = ["ffn2_w_gate", "ffn2_w_up", "ffn2_w_down"]
    names_b = ["w_in", "w_attn_branch", "w_conv_branch", "w_out"]
    names_c = ["ffn1_w_gate", "ffn1_w_up", "ffn1_w_down"]

    dh3, dg3, du3, a3 = _ffn_bwd(dy3, g3, u3, wg2, wu2, wd2, "ffn2_bwd")
    sums_a, chips_a = reduce_start(list(_ffn_wgrads(h3t, dg3, du3, a3, dy3, "ffn2")), names_a, "a", 4)
    dx2, st3 = _norm_bwd(dh3, x2, mod3, dx3, y3, 0.5, "norm3_bwd", after=tuple(sums_a))

    do, drest, dz, dya, dyc, st_conv = _mix_bwd(dx2, ya, yc, conv, rest, mod2, conv_pad, w_ab, w_cb, w_o, a_w, "mix_bwd")
    dq, dk, dv, st_qk = _attn_bwd(qkv, qkv_hat, do, o, lse, q_norm, k_norm, "attn_bwd")
    tok = lambda width: (lambda ts: pl.BlockSpec((ts, width), lambda cc, s: (s, 0)))
    colblk = lambda width: (lambda ts: pl.BlockSpec((ts, width), lambda cc, s: (s, cc)))
    tok_t = lambda ts: pl.BlockSpec((d, ts), lambda cc, s: (0, s))
    whole = pl.BlockSpec((d, QKV), lambda cc, s: (0, 0))
    dw_in = [_wgrad(h2t, part, tok_t, tok(QKV), (d, QKV), whole, (d, QKV), 1, "dw_in_" + nm, True)
             for part, nm in ((dq, "q"), (dk, "k"), (dv, "v"))]
    dw_in.append(_wgrad(h2t, drest, tok_t, colblk(d), (d, 5 * d), pl.BlockSpec((d, d), lambda cc, s: (0, cc)),
                        (d, d), 5, "dw_in_rest", True))
    dw_in = _cols_to_shards(jnp.concatenate(dw_in, axis=1), N_CHIPS)
    shard_w = d // N_CHIPS
    dw_ab = _wgrad(o16, dya, tok(a_w), colblk(shard_w), (a_w, d), pl.BlockSpec((a_w, shard_w), lambda cc, s: (0, cc)),
                   (a_w, shard_w), N_CHIPS, "dw_attn_branch")
    dw_ab = _cols_to_shards(dw_ab, N_CHIPS)
    row_out = pl.BlockSpec((None, shard_w, d), lambda cc, s: (cc, 0, 0))
    dw_cb = _wgrad(yb, dyc, colblk(shard_w), tok(d), (N_CHIPS, shard_w, d), row_out, (shard_w, d), N_CHIPS, "dw_conv_branch")
    dw_o = _wgrad(merged, dz, colblk(shard_w), tok(d), (N_CHIPS, shard_w, d), row_out, (shard_w, d), N_CHIPS, "dw_out")
    shard_grads = reduce_finish(sums_a, chips_a, names_a, "a", after=(dw_in, dw_o))
    sums_b, chips_b = reduce_start([dw_in, dw_ab, dw_cb, dw_o], names_b, "b", 5)

    dh2 = _in_proj_bwd(dq, dk, dv, drest, w_in_full, "in_proj_bwd", after=tuple(sums_b))
    dx1, st2, dy1 = _norm_bwd(dh2, x1, mod2, dx2, z, 1.0, "norm2_bwd", prev=(mod1, 0.5))
    dh1, dg1, du1, a1 = _ffn_bwd(dy1, g1, u1, wg1, wu1, wd1, "ffn1_bwd")
    dx0, st1 = _norm_bwd(dh1, xs, mod1, dx1, y1, 0.5, "norm1_bwd")
    grads_c = list(_ffn_wgrads(h1t, dg1, du1, a1, dy1, "ffn1"))
    shard_grads.update(reduce_finish(sums_b, chips_b, names_b, "b", after=tuple(grads_c)))
    sums_c, chips_c = reduce_start(grads_c, names_c, "c", 6)

    dmod = jnp.concatenate([st1[0:3], st2[0:3], st3[0:3]], axis=0).reshape(1, 9 * d)
    small = jnp.concatenate([dmod, st1[3:4], st2[3:4], st3[3:4], st_qk[0:1], st_qk[1:2],
                             st_conv[0:3].reshape(1, 3 * d)], axis=1)
    small_all = _allgather8(jnp.broadcast_to(small, (8, small.shape[1])), "gather_small").reshape(8, 8, -1)[:, 0]
    dmod_all = small_all[:, :9 * d]
    dmod_shard = lax.dynamic_slice(dmod_all, (0, chip * ada_cols), (8, ada_cols))
    g_w_ada, d_w_ada, nm_w_ada, nv_w_ada = _ada_bwd(c_all, dmod_shard, w_ada[0], m_w_ada[0], v_w_ada[0], "ada_bwd")

    vec_names = ["b_ada", "norm_ffn1", "norm_mix", "norm_ffn2", "q_norm", "k_norm"]
    vec_w = [b_ada, norm_ffn1, norm_mix, norm_ffn2, q_norm, k_norm]
    vec_m = [m_b_ada, m_norm_ffn1, m_norm_mix, m_norm_ffn2, m_q_norm, m_k_norm]
    vec_v = [v_b_ada, v_norm_ffn1, v_norm_mix, v_norm_ffn2, v_q_norm, v_k_norm]
    n_vec = sum(w.shape[1] for w in vec_w)
    cat = lambda arrs: jnp.concatenate(arrs, axis=1)
    vec_out = _small_update(small_all[:, :n_vec], cat(vec_w), cat(vec_m), cat(vec_v), "small_update")
    conv_parts = small_all[:, n_vec:].reshape(8, 3, N_CHIPS * conv_cols)
    conv_parts = lax.dynamic_slice(conv_parts, (0, 0, chip * conv_cols), (8, 3, conv_cols)).reshape(8, 3 * conv_cols)
    flat3 = lambda w: w[0].reshape(1, 3 * conv_cols)
    conv_out = _small_update(conv_parts, flat3(conv_w), flat3(m_conv_w), flat3(v_conv_w), "conv_update")

    res = {"w_ada": [t[None] for t in (g_w_ada, d_w_ada, nm_w_ada, nv_w_ada)],
           "conv_w": [t.reshape(1, 3, conv_cols) for t in conv_out]}
    off = 0
    for nm, w in zip(vec_names, vec_w):
        width = w.shape[1]
        res[nm] = [t[:, off:off + width] for t in vec_out]
        off += width
    big = {"ffn1_w_gate": (ffn1_w_gate, m_ffn1_w_gate, v_ffn1_w_gate), "ffn1_w_up": (ffn1_w_up, m_ffn1_w_up, v_ffn1_w_up),
           "ffn1_w_down": (ffn1_w_down, m_ffn1_w_down, v_ffn1_w_down), "w_in": (w_in, m_w_in, v_w_in),
           "w_attn_branch": (w_attn_branch, m_w_attn_branch, v_w_attn_branch),
           "w_conv_branch": (w_conv_branch, m_w_conv_branch, v_w_conv_branch), "w_out": (w_out, m_w_out, v_w_out),
           "ffn2_w_gate": (ffn2_w_gate, m_ffn2_w_gate, v_ffn2_w_gate), "ffn2_w_up": (ffn2_w_up, m_ffn2_w_up, v_ffn2_w_up),
           "ffn2_w_down": (ffn2_w_down, m_ffn2_w_down, v_ffn2_w_down)}
    def update(nm, after=()):
        w, m, v = big[nm]
        g, delta, new_m, new_v = _adamw(w[0], shard_grads[nm], m[0], v[0], "adamw_" + nm, after)
        res[nm] = [t[None] for t in (g, delta, new_m, new_v)]
        return new_v

    last = tuple(sums_c)
    for nm in names_a + names_b:
        last = (update(nm, last),)
    shard_grads.update(reduce_finish(sums_c, chips_c, names_c, "c", after=last))
    for nm in names_c:
        update(nm)

    order = ["w_ada", "b_ada", "norm_ffn1", "ffn1_w_gate", "ffn1_w_up", "ffn1_w_down", "norm_mix", "w_in", "q_norm",
             "k_norm", "conv_w", "w_attn_branch", "w_conv_branch", "w_out", "norm_ffn2", "ffn2_w_gate", "ffn2_w_up",
             "ffn2_w_down"]
    return (loss, dx0[None], *[res[nm][0] for nm in order], *[res[nm][1] for nm in order],
            *[res[nm][2] for nm in order], *[res[nm][3] for nm in order])
```

```python
import jax
import jax.numpy as jnp
from jax import lax
from jax.experimental import pallas as pl
from jax.experimental.pallas import tpu as pltpu
from jax.experimental.pallas import tpu_sc as plsc

F32 = jnp.float32
BF16 = jnp.bfloat16
MESH = pl.DeviceIdType.MESH
ANY = pl.BlockSpec(memory_space=pl.ANY)

NORM_EPS = 1e-6
HEAD_DIM = 128
N_GROUPS = 3
HEADS = 4
DILATIONS = (1, 4, 16)
ATTN_BLOCK = 128
SLAB = ATTN_BLOCK * max(DILATIONS)
QKV = N_GROUPS * HEADS * HEAD_DIM
ATTN_SCALE = HEAD_DIM ** -0.5
NEG = -1e30
N_CHIPS = 4

ADAM_LR = 0.001
ADAM_B1 = 0.9
ADAM_B2 = 0.999
ADAM_EPS = 1e-08
ADAM_WD = 0.01
ADAM_STEP = 10

VMEM_LIMIT_BYTES = 56 * 1024 * 1024
TOKEN_TILE = 512
FFN_TILE = 1024
PROJ_TILE = 2048
WGRAD_TILE = 2048
IN_BLOCK = 512
MIX_TILE = 256


def _params(n_axes=0):
    return pltpu.CompilerParams(
        dimension_semantics=("arbitrary",) * n_axes if n_axes else None,
        vmem_limit_bytes=VMEM_LIMIT_BYTES)


def _dot(a, b):
    return jnp.dot(a, b, preferred_element_type=F32)


def _dot_nt(a, b):
    return lax.dot_general(a, b, (((1,), (1,)), ((), ())), preferred_element_type=F32)


def _dot_tn(a, b):
    return lax.dot_general(a, b, (((0,), (0,)), ((), ())), preferred_element_type=F32)


def _sigmoid(x):
    return 1.0 / (1.0 + jnp.exp(-x))


def _place():
    return lax.axis_index("x"), lax.axis_index("y"), lax.axis_index("c")


def _ordered(body, n_in, after):
    if not after:
        return body
    return lambda *refs: body(*refs[:n_in], *refs[n_in + len(after):])


def _allgather8(block, name):
    m_per, n = block.shape

    def body(x_ref, out_ref, send_sems, recv_sems, local_sem):
        x, y, c = _place()
        me, sibling = (x, y, c), (x, y, 1 - c)
        chips = [(1 - x, y), (x, 1 - y), (1 - x, 1 - y)]

        def rows(px, py, pc):
            return out_ref.at[pl.ds((4 * px + 2 * py + pc) * m_per, m_per), :]

        def copy(k, blk, to, src=None):
            return pltpu.make_async_remote_copy(
                src_ref=rows(*blk) if src is None else src, dst_ref=rows(*blk),
                send_sem=send_sems.at[k], recv_sem=recv_sems.at[k],
                device_id=to, device_id_type=MESH)

        mine = pltpu.make_async_copy(x_ref, rows(*me), local_sem)
        mine.start()
        first = [copy(0, me, sibling, src=x_ref)]
        first += [copy(1 + j, me, (*chip, c), src=x_ref) for j, chip in enumerate(chips)]
        for cp in first:
            cp.start()
        passed = [copy(4 + j, (*chip, c), sibling) for j, chip in enumerate(chips)]
        for j, chip in enumerate(chips):
            copy(1 + j, (*chip, c), me).wait_recv()
            passed[j].start()
        copy(0, sibling, me).wait_recv()
        for j, chip in enumerate(chips):
            copy(4 + j, (*chip, 1 - c), me).wait_recv()
        for cp in first + passed:
            cp.wait_send()
        mine.wait()

    return pl.pallas_call(
        body, name=name,
        out_shape=jax.ShapeDtypeStruct((8 * m_per, n), block.dtype),
        in_specs=[pl.BlockSpec(memory_space=pltpu.VMEM)],
        out_specs=pl.BlockSpec(memory_space=pltpu.VMEM),
        scratch_shapes=[pltpu.SemaphoreType.DMA((7,)), pltpu.SemaphoreType.DMA((7,)),
                        pltpu.SemaphoreType.DMA],
        compiler_params=_params(),
    )(block)


def _handshake(peers):
    barrier = pltpu.get_barrier_semaphore()
    for peer in peers:
        pl.semaphore_signal(barrier, inc=1, device_id=peer, device_id_type=MESH)
    pl.semaphore_wait(barrier, len(peers))


def _gather_weights(shards, by_cols, name, collective_id, after=()):
    n_arr = len(shards)

    def body(*refs):
        srcs, outs = refs[:n_arr], refs[n_arr + len(after):2 * n_arr + len(after)]
        send_sems, recv_sems, local_sems = refs[2 * n_arr + len(after):]
        x, y, c = _place()
        me_dev, sibling = (x, y, c), (x, y, 1 - c)
        chips = [(1 - x, y), (x, 1 - y), (1 - x, 1 - y)]
        me = 2 * x + y
        _handshake([sibling] + [(*chip, c) for chip in chips])

        def place(k, chip_idx, rows):
            if by_cols[k]:
                width = srcs[k].shape[1]
                return outs[k].at[rows, pl.ds(pl.multiple_of(chip_idx * width, 128), width)]
            return outs[k].at[chip_idx, rows]

        def copy(k, slot, chip_idx, half_sel, to, from_shard=False):
            half = srcs[k].shape[0] // 2
            rows = pl.ds(half_sel * half, half)
            dst = place(k, chip_idx, rows)
            return pltpu.make_async_remote_copy(
                src_ref=srcs[k].at[rows] if from_shard else dst, dst_ref=dst,
                send_sem=send_sems.at[6 * k + slot], recv_sem=recv_sems.at[6 * k + slot],
                device_id=to, device_id_type=MESH)

        own = [pltpu.make_async_copy(srcs[k], place(k, me, pl.ds(0, srcs[k].shape[0])), local_sems.at[k])
               for k in range(n_arr)]
        for cp in own:
            cp.start()
        sent = []
        for k in range(n_arr):
            for j, chip in enumerate(chips):
                sent.append(copy(k, j, me, c, (*chip, c), from_shard=True))
                sent[-1].start()
        for k in range(n_arr):
            for j, chip in enumerate(chips):
                chip_idx = 2 * chip[0] + chip[1]
                copy(k, j, chip_idx, c, me_dev).wait_recv()
                sent.append(copy(k, 3 + j, chip_idx, c, sibling))
                sent[-1].start()
        for k in range(n_arr):
            for j, chip in enumerate(chips):
                copy(k, 3 + j, 2 * chip[0] + chip[1], 1 - c, me_dev).wait_recv()
        for cp in sent:
            cp.wait_send()
        for cp in own:
            cp.wait()

    def gathered(k):
        r, cols = shards[k].shape
        return (r, N_CHIPS * cols) if by_cols[k] else (N_CHIPS, r, cols)

    return pl.kernel(
        body, name=name,
        out_type=[jax.ShapeDtypeStruct(gathered(k), shards[k].dtype) for k in range(n_arr)],
        mesh=plsc.ScalarSubcoreMesh(axis_name="sequencer", num_cores=1),
        scratch_types=[pltpu.SemaphoreType.DMA((6 * n_arr,)), pltpu.SemaphoreType.DMA((6 * n_arr,)),
                       pltpu.SemaphoreType.DMA((n_arr,))],
        compiler_params=pltpu.CompilerParams(collective_id=collective_id),
    )(*shards, *after)


def _rs_pair_exchange(grads, name):
    n_arr = len(grads)

    def body(*refs):
        srcs, outs = refs[:n_arr], refs[n_arr:2 * n_arr]
        send_sems, recv_sems = refs[2 * n_arr:]
        x, y, c = _place()
        cps = []
        for k in range(n_arr):
            half = srcs[k].shape[1] // 2
            cps.append(pltpu.make_async_remote_copy(
                src_ref=srcs[k].at[:, pl.ds((1 - c) * half, half)], dst_ref=outs[k],
                send_sem=send_sems.at[k], recv_sem=recv_sems.at[k],
                device_id=(x, y, 1 - c), device_id_type=MESH))
            cps[-1].start()
        for cp in cps:
            cp.wait_recv()
        for cp in cps:
            cp.wait_send()

    return pl.pallas_call(
        body, name=name,
        out_shape=[jax.ShapeDtypeStruct((g.shape[0], g.shape[1] // 2, g.shape[2]), g.dtype) for g in grads],
        in_specs=[ANY] * n_arr, out_specs=[ANY] * n_arr,
        scratch_shapes=[pltpu.SemaphoreType.DMA((n_arr,)), pltpu.SemaphoreType.DMA((n_arr,))],
        compiler_params=_params(),
    )(*grads)


def _rs_chip_exchange(sums, name, collective_id):
    n_arr = len(sums)

    def body(*refs):
        srcs, outs = refs[:n_arr], refs[n_arr:2 * n_arr]
        send_sems, recv_sems = refs[2 * n_arr:]
        x, y, c = _place()
        chips = [(1 - x, y), (x, 1 - y), (1 - x, 1 - y)]
        _handshake([(*chip, c) for chip in chips])
        cps = []
        for k in range(n_arr):
            for j, chip in enumerate(chips):
                cps.append(pltpu.make_async_remote_copy(
                    src_ref=srcs[k].at[2 * chip[0] + chip[1]], dst_ref=outs[k].at[j],
                    send_sem=send_sems.at[3 * k + j], recv_sem=recv_sems.at[3 * k + j],
                    device_id=(*chip, c), device_id_type=MESH))
                cps[-1].start()
        for cp in cps:
            cp.wait_recv()
        for cp in cps:
            cp.wait_send()

    return pl.kernel(
        body, name=name,
        out_type=[jax.ShapeDtypeStruct((3,) + s.shape[1:], s.dtype) for s in sums],
        mesh=plsc.ScalarSubcoreMesh(axis_name="sequencer", num_cores=1),
        scratch_types=[pltpu.SemaphoreType.DMA((3 * n_arr,)), pltpu.SemaphoreType.DMA((3 * n_arr,))],
        compiler_params=pltpu.CompilerParams(collective_id=collective_id),
    )(*sums)


def _rs_share(totals, name):
    n_arr = len(totals)

    def body(*refs):
        outs = refs[n_arr:2 * n_arr]
        send_sems, recv_sems = refs[2 * n_arr:]
        x, y, c = _place()

        def half_rows(k, sel):
            return outs[k].at[sel]

        cps = []
        for k in range(n_arr):
            cps.append(pltpu.make_async_remote_copy(
                src_ref=half_rows(k, c), dst_ref=half_rows(k, c), send_sem=send_sems.at[k], recv_sem=recv_sems.at[k],
                device_id=(x, y, 1 - c), device_id_type=MESH))
            cps[-1].start()
        for k in range(n_arr):
            pltpu.make_async_remote_copy(
                src_ref=half_rows(k, c), dst_ref=half_rows(k, 1 - c), send_sem=send_sems.at[k],
                recv_sem=recv_sems.at[k], device_id=(x, y, 1 - c), device_id_type=MESH).wait_recv()
        for cp in cps:
            cp.wait_send()

    shared = pl.pallas_call(
        body, name=name,
        out_shape=[jax.ShapeDtypeStruct(t.shape, t.dtype) for t in totals],
        in_specs=[ANY] * n_arr, out_specs=[ANY] * n_arr,
        input_output_aliases={k: k for k in range(n_arr)},
        scratch_shapes=[pltpu.SemaphoreType.DMA((n_arr,)), pltpu.SemaphoreType.DMA((n_arr,))],
        compiler_params=_params(),
    )(*totals)
    return [t.reshape(2 * t.shape[1], t.shape[2]) for t in shared]


def _pair_add(grad, recv, c_idx, name):
    n, r, cols = grad.shape
    half = r // 2
    rows = half // 2

    def body(_, g_ref, r_ref, o_ref):
        o_ref[...] = (g_ref[...].astype(F32) + r_ref[...].astype(F32)).astype(o_ref.dtype)

    return pl.pallas_call(
        body, name=name,
        grid_spec=pltpu.PrefetchScalarGridSpec(
            num_scalar_prefetch=1, grid=(n, 2),
            in_specs=[pl.BlockSpec((None, None, rows, cols), lambda s, i, ci: (s, ci[0], i, 0)),
                      pl.BlockSpec((None, rows, cols), lambda s, i, ci: (s, i, 0))],
            out_specs=pl.BlockSpec((None, rows, cols), lambda s, i, ci: (s, i, 0))),
        out_shape=jax.ShapeDtypeStruct((n, half, cols), BF16),
        compiler_params=_params(2),
    )(c_idx, grad.reshape(n, 2, half, cols), recv)


def _chip_add(sums, recv, chip_and_core, name, after=()):
    _, half, cols = sums.shape
    rows = half // 2

    def body(_, s_ref, r0_ref, r1_ref, r2_ref, o_ref):
        o_ref[...] = ((s_ref[...].astype(F32) + r0_ref[...].astype(F32))
                      + r1_ref[...].astype(F32)) + r2_ref[...].astype(F32)

    def recv_spec(j):
        return pl.BlockSpec((None, rows, cols), lambda i, ci: (j, i, 0))

    return pl.pallas_call(
        _ordered(body, 5, after), name=name,
        grid_spec=pltpu.PrefetchScalarGridSpec(
            num_scalar_prefetch=1, grid=(2,),
            in_specs=[pl.BlockSpec((None, rows, cols), lambda i, ci: (ci[0], i, 0)),
                      recv_spec(0), recv_spec(1), recv_spec(2)] + [ANY] * len(after),
            out_specs=pl.BlockSpec((None, rows, cols), lambda i, ci: (ci[1], i, 0))),
        out_shape=jax.ShapeDtypeStruct((2, half, cols), F32),
        compiler_params=_params(1),
    )(chip_and_core, sums, recv, recv, recv, *after)


def _rms(x):
    return lax.rsqrt(jnp.mean(x * x, axis=-1, keepdims=True) + NORM_EPS)


def _norm_mod(x, mod, name, prev=None):
    s_len, d = x.shape
    tm = TOKEN_TILE

    def body(*refs):
        if prev is None:
            x_ref, mod_ref, h_ref, ht_ref = refs
            xv = x_ref[...]
        else:
            x_ref, y_ref, modp_ref, mod_ref, xo_ref, h_ref, ht_ref = refs
            xv = x_ref[...] + prev[2] * modp_ref[2:3, :] * y_ref[...]
            xo_ref[...] = xv
        n = (xv * _rms(xv)) * mod_ref[3:4, :]
        h = n * (1.0 + mod_ref[1:2, :]) + mod_ref[0:1, :]
        h_ref[...] = h.astype(BF16)
        ht_ref[...] = h.T.astype(BF16)

    tile = pl.BlockSpec((tm, d), lambda i: (i, 0))
    small = pl.BlockSpec((8, d), lambda i: (0, 0))
    h_specs = [tile, pl.BlockSpec((d, tm), lambda i: (0, i))]
    h_shapes = [jax.ShapeDtypeStruct((s_len, d), BF16), jax.ShapeDtypeStruct((d, s_len), BF16)]
    if prev is None:
        return pl.pallas_call(
            body, name=name, grid=(s_len // tm,), in_specs=[tile, small], out_specs=h_specs, out_shape=h_shapes,
            compiler_params=_params(1))(x, mod)
    return pl.pallas_call(
        body, name=name, grid=(s_len // tm,), in_specs=[tile, tile, small, small],
        out_specs=[tile] + h_specs, out_shape=[jax.ShapeDtypeStruct((s_len, d), F32)] + h_shapes,
        compiler_params=_params(1))(x, prev[0], prev[1], mod)


def _norm_bwd(dh, x, mod, dxo, y_raw, coef, name, after=(), prev=None):
    s_len, d = x.shape
    tm = TOKEN_TILE

    def body(*refs):
        if prev is None:
            dh_ref, x_ref, mod_ref, dxo_ref, y_ref, dx_ref, st_ref = refs
        else:
            dh_ref, x_ref, mod_ref, dxo_ref, y_ref, modp_ref, dx_ref, st_ref, dyp_ref = refs

        @pl.when(pl.program_id(0) == 0)
        def _():
            st_ref[...] = jnp.zeros_like(st_ref)

        xv, dhv, dxov = x_ref[...], dh_ref[...], dxo_ref[...]
        r = _rms(xv)
        xh = xv * r
        gain, scale = mod_ref[3:4, :], mod_ref[1:2, :]
        dn = dhv * (1.0 + scale)
        dxh = dn * gain
        dx = dxov + r * (dxh - xh * jnp.mean(dxh * xh, axis=-1, keepdims=True))
        dx_ref[...] = dx
        if prev is not None:
            dyp_ref[...] = (prev[1] * modp_ref[2:3, :] * dx).astype(BF16)
        st_ref[0:1, :] += jnp.sum(dhv, axis=0, keepdims=True)
        st_ref[1:2, :] += jnp.sum(dhv * (xh * gain), axis=0, keepdims=True)
        st_ref[2:3, :] += coef * jnp.sum(y_ref[...] * dxov, axis=0, keepdims=True)
        st_ref[3:4, :] += jnp.sum(dn * xh, axis=0, keepdims=True)

    tile = pl.BlockSpec((tm, d), lambda i: (i, 0))
    small = pl.BlockSpec((8, d), lambda i: (0, 0))
    operands = [dh, x, mod, dxo, y_raw] + ([] if prev is None else [prev[0]])
    in_specs = [tile, tile, small, tile, tile] + ([] if prev is None else [small])
    out_specs = [tile, small] + ([] if prev is None else [tile])
    out_shape = [jax.ShapeDtypeStruct((s_len, d), F32), jax.ShapeDtypeStruct((8, d), F32)]
    if prev is not None:
        out_shape.append(jax.ShapeDtypeStruct((s_len, d), BF16))
    return pl.pallas_call(
        _ordered(body, len(operands), after), name=name, grid=(s_len // tm,),
        in_specs=in_specs + [ANY] * len(after), out_specs=out_specs, out_shape=out_shape,
        compiler_params=_params(1),
    )(*operands, *after)


def _loss_grad(x, y, mod, target, name):
    s_len, d = x.shape
    tm = TOKEN_TILE

    def body(x_ref, y_ref, mod_ref, t_ref, do_ref, dy_ref, part_ref):
        @pl.when(pl.program_id(0) == 0)
        def _():
            part_ref[...] = jnp.zeros_like(part_ref)

        half_gate = 0.5 * mod_ref[2:3, :]
        err = (x_ref[...] + half_gate * y_ref[...]) - t_ref[...]
        do = err * (1.0 / d)
        do_ref[...] = do
        dy_ref[...] = (half_gate * do).astype(BF16)
        sq = err * err
        part_ref[...] += jnp.sum(sq.reshape(tm // 8, 8, d), axis=0)

    tile = pl.BlockSpec((tm, d), lambda i: (i, 0))
    small = pl.BlockSpec((8, d), lambda i: (0, 0))
    return pl.pallas_call(
        body, name=name, grid=(s_len // tm,),
        in_specs=[tile, tile, small, tile],
        out_specs=[tile, tile, small],
        out_shape=[jax.ShapeDtypeStruct((s_len, d), F32), jax.ShapeDtypeStruct((s_len, d), BF16),
                   jax.ShapeDtypeStruct((8, d), F32)],
        compiler_params=_params(1),
    )(x, y, mod, target)


def _adamw_math(w, g, m, v):
    m = ADAM_B1 * m + (1.0 - ADAM_B1) * g
    v = ADAM_B2 * v + (1.0 - ADAM_B2) * (g * g)
    m_hat = m / (1.0 - ADAM_B1 ** ADAM_STEP)
    v_hat = v / (1.0 - ADAM_B2 ** ADAM_STEP)
    delta = -ADAM_LR * (m_hat / (jnp.sqrt(v_hat) + ADAM_EPS) + ADAM_WD * w)
    return delta, m, v


def _adamw(w, g, m, v, name, after=()):
    r, cols = w.shape
    tr = r // 8 if r % 64 == 0 else r

    def body(w_ref, g_ref, m_ref, v_ref, go_ref, d_ref, nm_ref, nv_ref):
        gv = g_ref[...]
        go_ref[...] = gv
        d_ref[...], nm_ref[...], nv_ref[...] = _adamw_math(w_ref[...], gv, m_ref[...], v_ref[...])

    tile = pl.BlockSpec((tr, cols), lambda i: (i, 0))
    shape = jax.ShapeDtypeStruct((r, cols), F32)
    return pl.pallas_call(
        _ordered(body, 4, after), name=name, grid=(r // tr,),
        in_specs=[tile] * 4 + [ANY] * len(after), out_specs=[tile] * 4, out_shape=[shape] * 4,
        compiler_params=_params(1),
    )(w, g, m, v, *after)


def _in_parts(tm, n_qkv, n_rest):
    def part(lo, n_blk):
        return pl.BlockSpec((tm, IN_BLOCK), lambda i, j: (i, jnp.clip(j - lo, 0, n_blk - 1)))
    return [part(0, n_qkv), part(n_qkv, n_qkv), part(2 * n_qkv, n_qkv), part(3 * n_qkv, n_rest)]


def _pick_part(j, n_qkv, refs, fn):
    bounds = [0, n_qkv, 2 * n_qkv, 3 * n_qkv]
    for p, ref in enumerate(refs):
        inside = j >= bounds[p]
        if p + 1 < len(refs):
            inside = inside & (j < bounds[p + 1])
        pl.when(inside)(lambda ref=ref: fn(ref))


def _rows(base, count, stride):
    return pl.ds(base, count) if stride == 1 else pl.ds(base, count, stride=stride)


REORDER_STRIDE = 4


def _reorder_plan(dil):
    inner = min(dil, REORDER_STRIDE)
    return inner, dil // inner, SLAB // inner, SLAB // dil


def _to_residue_order(dst, src, dil, tmp):
    inner, outer, big, seg = _reorder_plan(dil)
    if outer == 1:
        for r in range(dil):
            dst[pl.ds(r * seg, seg), :] = src[_rows(r, seg, dil), :].astype(dst.dtype)
        return
    for b in range(inner):
        tmp[pl.ds(b * big, big), :] = src[_rows(b, big, inner), :]
    for a in range(outer):
        for b in range(inner):
            dst[pl.ds((inner * a + b) * seg, seg), :] = tmp[_rows(b * big + a, seg, outer), :].astype(dst.dtype)


def _to_token_order(dst, src, dil, tmp):
    inner, outer, big, seg = _reorder_plan(dil)
    if outer == 1:
        for r in range(dil):
            dst[_rows(r, seg, dil), :] = src[pl.ds(r * seg, seg), :]
        return
    for a in range(outer):
        for b in range(inner):
            tmp[_rows(b * big + a, seg, outer), :] = src[pl.ds((inner * a + b) * seg, seg), :]
    for b in range(inner):
        dst[_rows(b, big, inner), :] = tmp[pl.ds(b * big, big), :]


def _in_proj(h, w, q_norm, k_norm, name):
    s_len, d = h.shape
    tm = PROJ_TILE
    assert tm == SLAB and IN_BLOCK == HEADS * HEAD_DIM
    steps = w.shape[1] // IN_BLOCK
    n_qkv = 3 * QKV // IN_BLOCK
    n_rest = steps - n_qkv

    def body(h_ref, w_ref, qn_ref, kn_ref, qkv_ref, rest_ref, hat_ref, res_s, tok_s, tmp_s):
        j = pl.program_id(1)
        done = j - 1
        kept = res_s.at[(j + 1) % 2]

        def multiply():
            res_s[j % 2] = _dot(h_ref[...], w_ref[...])

        def emit(sect, gi):
            multiply()
            qkv_ref[...] = kept[...]
            dil = DILATIONS[gi]
            for hh in range(HEADS):
                cols = slice(hh * HEAD_DIM, (hh + 1) * HEAD_DIM)
                x = kept[:, cols]
                if sect < 2:
                    x = (x * _rms(x)) * (qn_ref if sect == 0 else kn_ref)[...]
                tok_s[...] = x
                _to_residue_order(hat_ref.at[:, cols], tok_s, dil, tmp_s)

        pl.when(done < 0)(multiply)
        for sect in range(3):
            for gi in range(N_GROUPS):
                pl.when(done == sect * N_GROUPS + gi)(lambda sect=sect, gi=gi: emit(sect, gi))

        @pl.when((done >= n_qkv) & (done < steps - 1))
        def _():
            multiply()
            rest_ref[...] = kept[...].astype(BF16)

        @pl.when(done == steps - 1)
        def _():
            rest_ref[...] = kept[...].astype(BF16)

    qkv_blk = pl.BlockSpec((tm, IN_BLOCK), lambda i, j: (i, jnp.clip(j - 1, 0, n_qkv - 1)))
    small = pl.BlockSpec((1, HEAD_DIM), lambda i, j: (0, 0))
    return pl.pallas_call(
        body, name=name, grid=(s_len // tm, steps + 1),
        in_specs=[pl.BlockSpec((tm, d), lambda i, j: (i, 0)),
                  pl.BlockSpec((d, IN_BLOCK), lambda i, j: (0, jnp.minimum(j, steps - 1))), small, small],
        out_specs=[qkv_blk, pl.BlockSpec((tm, IN_BLOCK), lambda i, j: (i, jnp.clip(j - 1 - n_qkv, 0, n_rest - 1))),
                   qkv_blk],
        out_shape=[jax.ShapeDtypeStruct((s_len, 3 * QKV), F32),
                   jax.ShapeDtypeStruct((s_len, n_rest * IN_BLOCK), BF16),
                   jax.ShapeDtypeStruct((s_len, 3 * QKV), BF16)],
        scratch_shapes=[pltpu.VMEM((2, tm, IN_BLOCK), F32), pltpu.VMEM((tm, HEAD_DIM), F32),
                        pltpu.VMEM((tm, HEAD_DIM), F32)],
        compiler_params=_params(2),
    )(h, w, q_norm, k_norm)


def _in_proj_bwd(dq, dk, dv, drest, w, name, after=()):
    s_len = dq.shape[0]
    d = w.shape[0]
    tm = PROJ_TILE
    steps = w.shape[1] // IN_BLOCK
    n_qkv = QKV // IN_BLOCK

    def body(dq_ref, dk_ref, dv_ref, dr_ref, w_ref, o_ref, acc_ref):
        j = pl.program_id(1)

        @pl.when(j == 0)
        def _():
            acc_ref[...] = jnp.zeros_like(acc_ref)

        def add(a_ref):
            acc_ref[...] += _dot_nt(a_ref[...], w_ref[...])

        _pick_part(j, n_qkv, [dq_ref, dk_ref, dv_ref, dr_ref], add)

        @pl.when(j == steps - 1)
        def _():
            o_ref[...] = acc_ref[...]

    return pl.pallas_call(
        _ordered(body, 5, after), name=name, grid=(s_len // tm, steps),
        in_specs=(_in_parts(tm, n_qkv, steps - 3 * n_qkv) + [pl.BlockSpec((d, IN_BLOCK), lambda i, j: (0, j))]
                  + [ANY] * len(after)),
        out_specs=pl.BlockSpec((tm, d), lambda i, j: (i, 0)),
        out_shape=jax.ShapeDtypeStruct((s_len, d), F32),
        scratch_shapes=[pltpu.VMEM((tm, d), F32)],
        compiler_params=_params(2),
    )(dq, dk, dv, drest, w, *after)


def _wgrad(x, y, x_spec, y_spec, out_shape, out_spec, acc_shape, n_chunks, name, x_transposed=False, after=()):
    s_len = y.shape[-2]
    ts = WGRAD_TILE
    steps = s_len // ts

    def body(x_ref, y_ref, o_ref, acc_ref):
        s = pl.program_id(1)

        @pl.when(s == 0)
        def _():
            acc_ref[...] = jnp.zeros_like(acc_ref)

        acc_ref[...] += (_dot if x_transposed else _dot_tn)(x_ref[...], y_ref[...])

        @pl.when(s == steps - 1)
        def _():
            o_ref[...] = acc_ref[...].astype(o_ref.dtype)

    return pl.pallas_call(
        _ordered(body, 2, after), name=name, grid=(n_chunks, steps),
        in_specs=[x_spec(ts), y_spec(ts)] + [ANY] * len(after), out_specs=out_spec,
        out_shape=jax.ShapeDtypeStruct(out_shape, BF16),
        scratch_shapes=[pltpu.VMEM(acc_shape, F32)],
        compiler_params=_params(2),
    )(x, y, *after)


def _pieces(width, piece=256):
    return [slice(a, min(a + piece, width)) for a in range(0, width, piece)]


def _ffn_fwd(h, w_gate, w_up, w_down, name):
    s_len, d = h.shape
    n_chunks, _, fs = w_gate.shape
    tm = FFN_TILE

    def body(h_ref, wg_ref, wu_ref, wd_ref, g_ref, u_ref, y_ref):
        j = pl.program_id(1)
        hv = h_ref[...]
        pieces = _pieces(fs)
        first = lambda cols: (_dot(hv, wg_ref[:, cols]), _dot(hv, wu_ref[:, cols]))
        total = None
        ahead = first(pieces[0])
        for k, cols in enumerate(pieces):
            g, u = ahead
            if k + 1 < len(pieces):
                ahead = first(pieces[k + 1])
            g_ref[:, cols] = g.astype(BF16)
            u_ref[:, cols] = u.astype(BF16)
            act = (g * _sigmoid(g)) * u
            part = _dot(act.astype(BF16), wd_ref[cols, :])
            total = part if total is None else total + part

        @pl.when(j == 0)
        def _():
            y_ref[...] = total

        @pl.when(j > 0)
        def _():
            y_ref[...] += total

    tile = pl.BlockSpec((tm, d), lambda i, j: (i, 0))
    hid = pl.BlockSpec((None, tm, fs), lambda i, j: (j, i, 0))
    w_in_spec = pl.BlockSpec((None, d, fs), lambda i, j: (j, 0, 0))
    hid_shape = jax.ShapeDtypeStruct((n_chunks, s_len, fs), BF16)
    return pl.pallas_call(
        body, name=name, grid=(s_len // tm, n_chunks),
        in_specs=[tile, w_in_spec, w_in_spec, pl.BlockSpec((None, fs, d), lambda i, j: (j, 0, 0))],
        out_specs=[hid, hid, tile],
        out_shape=[hid_shape, hid_shape, jax.ShapeDtypeStruct((s_len, d), F32)],
        compiler_params=_params(2),
    )(h, w_gate, w_up, w_down)


def _ffn_bwd(dy, g_pre, u_pre, w_gate, w_up, w_down, name):
    s_len, d = dy.shape
    n_chunks, _, fs = w_gate.shape
    tm = FFN_TILE

    def body(dy_ref, g_ref, u_ref, wg_ref, wu_ref, wd_ref, dh_ref, dg_ref, du_ref, a_ref):
        j = pl.program_id(1)
        dyv = dy_ref[...]
        pieces = _pieces(fs)
        first = lambda cols: _dot_nt(dyv, wd_ref[cols, :])
        total = None
        ahead = first(pieces[0])
        for k, cols in enumerate(pieces):
            da = ahead
            if k + 1 < len(pieces):
                ahead = first(pieces[k + 1])
            g = g_ref[:, cols].astype(F32)
            u = u_ref[:, cols].astype(F32)
            sg = _sigmoid(g)
            silu = g * sg
            dg = (da * u * (sg * (1.0 + g * (1.0 - sg)))).astype(BF16)
            du = (da * silu).astype(BF16)
            dg_ref[:, cols] = dg
            du_ref[:, cols] = du
            a_ref[:, cols] = (silu * u).astype(BF16)
            part = _dot_nt(dg, wg_ref[:, cols]) + _dot_nt(du, wu_ref[:, cols])
            total = part if total is None else total + part

        @pl.when(j == 0)
        def _():
            dh_ref[...] = total

        @pl.when(j > 0)
        def _():
            dh_ref[...] += total

    tile = pl.BlockSpec((tm, d), lambda i, j: (i, 0))
    hid = pl.BlockSpec((None, tm, fs), lambda i, j: (j, i, 0))
    w_in_spec = pl.BlockSpec((None, d, fs), lambda i, j: (j, 0, 0))
    hid_shape = jax.ShapeDtypeStruct((n_chunks, s_len, fs), BF16)
    return pl.pallas_call(
        body, name=name, grid=(s_len // tm, n_chunks),
        in_specs=[tile, hid, hid, w_in_spec, w_in_spec, pl.BlockSpec((None, fs, d), lambda i, j: (j, 0, 0))],
        out_specs=[tile, hid, hid, hid],
        out_shape=[jax.ShapeDtypeStruct((s_len, d), F32), hid_shape, hid_shape, hid_shape],
        compiler_params=_params(2),
    )(dy, g_pre, u_pre, w_gate, w_up, w_down)


def _ffn_wgrads(ht, dg, du, act, dy, tag, after=()):
    n_chunks, s_len, fs = dg.shape
    d = ht.shape[0]
    tok = lambda ts: pl.BlockSpec((ts, d), lambda c, s: (s, 0))
    tok_t = lambda ts: pl.BlockSpec((d, ts), lambda c, s: (0, s))
    hid = lambda ts: pl.BlockSpec((None, ts, fs), lambda c, s: (c, s, 0))
    d_up = pl.BlockSpec((None, d, fs), lambda c, s: (c, 0, 0))
    d_down = pl.BlockSpec((None, fs, d), lambda c, s: (c, 0, 0))
    dwg = _wgrad(ht, dg, tok_t, hid, (n_chunks, d, fs), d_up, (d, fs), n_chunks, tag + "_dwg", True, after)
    dwu = _wgrad(ht, du, tok_t, hid, (n_chunks, d, fs), d_up, (d, fs), n_chunks, tag + "_dwu", True, after)
    dwd = _wgrad(act, dy, hid, tok, (n_chunks, fs, d), d_down, (fs, d), n_chunks, tag + "_dwd", False, after)
    return dwg, dwu, dwd


def _band_bias():
    qi = lax.broadcasted_iota(jnp.int32, (ATTN_BLOCK, 2 * ATTN_BLOCK), 0)
    kj = lax.broadcasted_iota(jnp.int32, (ATTN_BLOCK, 2 * ATTN_BLOCK), 1)
    band = (kj >= qi) & (kj <= qi + ATTN_BLOCK)
    return jnp.where(band, 0.0, NEG), jnp.where(band & (kj >= ATTN_BLOCK), 0.0, NEG)


def _qkv_specs(slab_of, sections):
    def spec(sect, back):
        return pl.BlockSpec((SLAB, HEAD_DIM),
                            lambda h, s, g: (jnp.maximum(slab_of(s) - back, 0), (sect * N_GROUPS + g) * HEADS + h))
    return [spec(sect, back) for sect, back in sections]


HAT_BLOCKS = [(0, 0), (1, 0), (2, 0), (1, 1), (2, 1)]


def _stage_keys(k_ref, v_ref, kp_ref, vp_ref, kbuf, vbuf, dil, n):
    run = SLAB // dil
    for r in range(dil):
        own, before = pl.ds(r * run, run), pl.ds(2 * r * run, run)
        kbuf[pl.ds((2 * r + 1) * run, run), :] = k_ref[own, :]
        vbuf[pl.ds((2 * r + 1) * run, run), :] = v_ref[own, :]

        @pl.when(n > 0)
        def _():
            kbuf[before, :] = kp_ref[own, :]
            vbuf[before, :] = vp_ref[own, :]

        @pl.when(n == 0)
        def _():
            kbuf[before, :] = jnp.zeros((run, HEAD_DIM), BF16)
            vbuf[before, :] = jnp.zeros((run, HEAD_DIM), BF16)


def _for_each_tile(dil, n, tile_fn):
    run = SLAB // dil
    bias, first_bias = _band_bias()
    for jj in range(run // ATTN_BLOCK):
        start = jj * ATTN_BLOCK
        tile_bias = jnp.where(n == 0, first_bias, bias) if jj == 0 else bias
        for r in range(dil):
            tile_fn(pl.ds(r * run + start, ATTN_BLOCK),
                    pl.ds((2 * r + 1) * run - ATTN_BLOCK + start, 2 * ATTN_BLOCK), tile_bias)


def _attn_fwd(hat, name):
    s_len = hat.shape[0]
    e = HEAD_DIM
    n_slabs = s_len // SLAB

    def body(q_ref, k_ref, v_ref, kp_ref, vp_ref, o_ref, lse_ref, kbuf, vbuf, m_s, l_s, acc_s, m_p, l_p, acc_p, tmp_s):
        n, grp = pl.program_id(1), pl.program_id(2)

        def run(gi, dil):
            _stage_keys(k_ref, v_ref, kp_ref, vp_ref, kbuf, vbuf, dil, n)

            def tile(q_rows, kv_rows, bias):
                s = _dot_nt(q_ref[q_rows, :], kbuf[kv_rows, :]) * ATTN_SCALE + bias
                m = jnp.max(s, axis=-1, keepdims=True)
                p = jnp.exp(s - m)
                m_p[q_rows, :] = jnp.broadcast_to(m, (ATTN_BLOCK, e))
                l_p[q_rows, :] = jnp.broadcast_to(jnp.sum(p, axis=-1, keepdims=True), (ATTN_BLOCK, e))
                acc_p[q_rows, :] = _dot(p.astype(BF16), vbuf[kv_rows, :])

            _for_each_tile(dil, n, tile)
            _to_token_order(m_s.at[gi], m_p, dil, tmp_s)
            _to_token_order(l_s.at[gi], l_p, dil, tmp_s)
            _to_token_order(acc_s.at[gi], acc_p, dil, tmp_s)

        for gi, dil in enumerate(DILATIONS):
            pl.when(grp == gi)(lambda gi=gi, dil=dil: run(gi, dil))

        @pl.when(grp == N_GROUPS - 1)
        def _():
            m_all = jnp.maximum(jnp.maximum(m_s[0], m_s[1]), m_s[2])
            den = jnp.zeros((SLAB, e), F32)
            num = jnp.zeros((SLAB, e), F32)
            for gi in range(N_GROUPS):
                w = jnp.exp(m_s[gi] - m_all)
                den += l_s[gi] * w
                num += acc_s[gi] * w
            o_ref[...] = num / den
            lse_ref[...] = m_all + jnp.log(den)

    out = pl.BlockSpec((SLAB, e), lambda h, n, g: (n, h))
    return pl.pallas_call(
        body, name=name, grid=(HEADS, n_slabs, N_GROUPS),
        in_specs=_qkv_specs(lambda n: n, HAT_BLOCKS),
        out_specs=[out, out],
        out_shape=[jax.ShapeDtypeStruct((s_len, HEADS * e), F32)] * 2,
        scratch_shapes=[pltpu.VMEM((2 * SLAB, e), BF16), pltpu.VMEM((2 * SLAB, e), BF16),
                        pltpu.VMEM((N_GROUPS, SLAB, e), F32), pltpu.VMEM((N_GROUPS, SLAB, e), F32),
                        pltpu.VMEM((N_GROUPS, SLAB, e), F32)]
        + [pltpu.VMEM((SLAB, e), F32)] * 4,
        compiler_params=_params(3),
    )(hat, hat, hat, hat, hat)


def _attn_bwd(qkv, hat, d_out, out, lse, q_norm, k_norm, name):
    s_len = qkv.shape[0]
    e = HEAD_DIM
    n_slabs = s_len // SLAB

    def body(q_ref, k_ref, v_ref, kp_ref, vp_ref, qraw_ref, kraw_ref, do_ref, o_ref, lse_ref, qn_ref, kn_ref,
             dq_ref, dk_ref, dv_ref, st_ref, kbuf, vbuf, stat_s, dqs, dkb, dvb, dk_tok, dv_tok, carry,
             do_p, stat_p, dq_p, dk_p, dv_p, tmp_s, do16_p):
        head, step, grp = pl.program_id(0), pl.program_id(1), pl.program_id(2)
        n = n_slabs - 1 - step
        dkb[...] = jnp.zeros_like(dkb)
        dvb[...] = jnp.zeros_like(dvb)
        @pl.when(grp == 0)
        def _():
            lane = lax.broadcasted_iota(jnp.int32, (SLAB, e), 1)
            stat_s[...] = jnp.where(lane < e // 2, lse_ref[...],
                                    jnp.sum(do_ref[...] * o_ref[...], axis=-1, keepdims=True))

        @pl.when((head == 0) & (step == 0) & (grp == 0))
        def _():
            st_ref[...] = jnp.zeros_like(st_ref)

        def run(gi, dil):
            seg = SLAB // dil
            _stage_keys(k_ref, v_ref, kp_ref, vp_ref, kbuf, vbuf, dil, n)

            @pl.when(step == 0)
            def _():
                carry[gi] = jnp.zeros((2, SLAB, e), F32)

            _to_residue_order(do_p, do_ref, dil, tmp_s)
            do16_p[...] = do_p[...].astype(BF16)
            _to_residue_order(stat_p, stat_s, dil, tmp_s)

            def tile(q_rows, kv_rows, bias):
                q = q_ref[q_rows, :]
                k = kbuf[kv_rows, :]
                v = vbuf[kv_rows, :]
                stat = stat_p[q_rows, :]
                do16 = do16_p[q_rows, :]
                s = _dot_nt(q, k) * ATTN_SCALE + bias
                p = jnp.exp(s - stat[:, 0:1])
                ds = (p * (_dot_nt(do16, v) - stat[:, e // 2:e // 2 + 1]) * ATTN_SCALE).astype(BF16)
                dq_p[q_rows, :] = _dot(ds, k)
                dkb[kv_rows, :] += _dot_tn(ds, q)
                dvb[kv_rows, :] += _dot_tn(p.astype(BF16), do16)

            _for_each_tile(dil, n, tile)
            for r in range(dil):
                own, before = pl.ds((2 * r + 1) * seg, seg), pl.ds(2 * r * seg, seg)
                kept = pl.ds(r * seg, seg)
                dk_p[kept, :] = dkb[own, :] + carry.at[gi, 0][kept, :]
                dv_p[kept, :] = dvb[own, :] + carry.at[gi, 1][kept, :]
                carry.at[gi, 0][kept, :] = dkb[before, :]
                carry.at[gi, 1][kept, :] = dvb[before, :]
            _to_token_order(dqs, dq_p, dil, tmp_s)
            _to_token_order(dk_tok, dk_p, dil, tmp_s)
            _to_token_order(dv_tok, dv_p, dil, tmp_s)

            def norm_bwd(raw, gain, d_hat):
                r = _rms(raw)
                y = raw * r
                dy = d_hat * gain
                return r * (dy - y * jnp.mean(dy * y, axis=-1, keepdims=True)), jnp.sum(d_hat * y, axis=0, keepdims=True)

            dq, dqn = norm_bwd(qraw_ref[...], qn_ref[...], dqs[...])
            dk, dkn = norm_bwd(kraw_ref[...], kn_ref[...], dk_tok[...])
            dq_ref[...] = dq.astype(BF16)
            dk_ref[...] = dk.astype(BF16)
            dv_ref[...] = dv_tok[...].astype(BF16)
            st_ref[0:1, :] += dqn
            st_ref[1:2, :] += dkn

        for gi, dil in enumerate(DILATIONS):
            pl.when(grp == gi)(lambda gi=gi, dil=dil: run(gi, dil))

    slab_of = lambda s: n_slabs - 1 - s
    small = pl.BlockSpec((1, e), lambda h, s, g: (0, 0))
    head_blk = pl.BlockSpec((SLAB, e), lambda h, s, g: (slab_of(s), h))
    grad_blk = pl.BlockSpec((SLAB, e), lambda h, s, g: (slab_of(s), g * HEADS + h))
    grad_shape = jax.ShapeDtypeStruct((s_len, QKV), BF16)
    return pl.pallas_call(
        body, name=name, grid=(HEADS, n_slabs, N_GROUPS),
        in_specs=(_qkv_specs(slab_of, HAT_BLOCKS) + _qkv_specs(slab_of, [(0, 0), (1, 0)])
                  + [head_blk, head_blk, head_blk, small, small]),
        out_specs=[grad_blk, grad_blk, grad_blk, pl.BlockSpec((8, e), lambda h, s, g: (0, 0))],
        out_shape=[grad_shape, grad_shape, grad_shape, jax.ShapeDtypeStruct((8, e), F32)],
        scratch_shapes=[pltpu.VMEM((2 * SLAB, e), BF16), pltpu.VMEM((2 * SLAB, e), BF16), pltpu.VMEM((SLAB, e), F32),
                        pltpu.VMEM((SLAB, e), F32), pltpu.VMEM((2 * SLAB, e), F32), pltpu.VMEM((2 * SLAB, e), F32),
                        pltpu.VMEM((SLAB, e), F32), pltpu.VMEM((SLAB, e), F32),
                        pltpu.VMEM((N_GROUPS, 2, SLAB, e), F32)]
        + [pltpu.VMEM((SLAB, e), F32)] * 6 + [pltpu.VMEM((SLAB, e), BF16)],
        compiler_params=_params(3),
    )(hat, hat, hat, hat, hat, qkv, qkv, d_out, out, lse, q_norm, k_norm)


def _shift_rows(x, by, edge, forward):
    t_len = x.shape[0]
    row = lax.broadcasted_iota(jnp.int32, x.shape, 0)
    if forward:
        out = pltpu.roll(x, by, 0)
        for i in range(by):
            out = jnp.where(row == i, edge[8 - by + i:8 - by + i + 1, :], out)
    else:
        out = pltpu.roll(x, t_len - by, 0)
        for i in range(by):
            out = jnp.where(row == t_len - by + i, edge[i:i + 1, :], out)
    return out


def _mix_fwd(x, o, rest, mod, conv_w, w_attn, w_conv, w_out, name):
    s_len, d = x.shape
    tm = MIX_TILE
    a_w = o.shape[1]

    def body(x_ref, o_ref, u_ref, b_ref, c_ref, ga_ref, gc_ref, mod_ref, cw_ref, wa_ref, wc_ref, wo_ref,
             xo_ref, z_ref, ya_ref, yc_ref, conv_ref, yb_ref, m_ref, o16_ref, carry):
        @pl.when(pl.program_id(0) == 0)
        def _():
            carry[...] = jnp.zeros_like(carry)

        xc = c_ref[...].astype(F32) * u_ref[...].astype(F32)
        edge = carry[...]
        conv = (_shift_rows(xc, 2, edge, True) * cw_ref[0:1, :] + _shift_rows(xc, 1, edge, True) * cw_ref[1:2, :]
                + xc * cw_ref[2:3, :])
        carry[...] = xc[tm - 8:tm, :]
        yb = (b_ref[...].astype(F32) * conv).astype(BF16)
        o16 = o_ref[...].astype(BF16)
        ya = _dot(o16, wa_ref[...])
        yc = _dot(yb, wc_ref[...])
        merged = (_sigmoid(ga_ref[...].astype(F32)) * ya + _sigmoid(gc_ref[...].astype(F32)) * yc).astype(BF16)
        z = _dot(merged, wo_ref[...])
        xo_ref[...] = x_ref[...] + mod_ref[2:3, :] * z
        z_ref[...] = z
        ya_ref[...] = ya.astype(BF16)
        yc_ref[...] = yc.astype(BF16)
        conv_ref[...] = conv.astype(BF16)
        yb_ref[...] = yb
        m_ref[...] = merged
        o16_ref[...] = o16

    tile = pl.BlockSpec((tm, d), lambda i: (i, 0))
    sect = lambda k: pl.BlockSpec((tm, d), lambda i: (i, k))
    att = pl.BlockSpec((tm, a_w), lambda i: (i, 0))
    const = lambda shape: pl.BlockSpec(shape, lambda i: (0, 0))
    f32_out = jax.ShapeDtypeStruct((s_len, d), F32)
    b16_out = jax.ShapeDtypeStruct((s_len, d), BF16)
    return pl.pallas_call(
        body, name=name, grid=(s_len // tm,),
        in_specs=[tile, att, sect(0), sect(1), sect(2), sect(3), sect(4), const((8, d)), const((8, d)),
                  const((a_w, d)), const((d, d)), const((d, d))],
        out_specs=[tile] * 7 + [att],
        out_shape=[f32_out, f32_out] + [b16_out] * 5 + [jax.ShapeDtypeStruct((s_len, a_w), BF16)],
        scratch_shapes=[pltpu.VMEM((8, d), F32)],
        compiler_params=_params(1),
    )(x, o, rest, rest, rest, rest, rest, mod, conv_w, w_attn, w_conv, w_out)


def _mix_bwd(dxo, ya, yc, conv, rest, mod, conv_w, w_attn, w_conv, w_out, a_w, name):
    s_len, d = dxo.shape
    tm = MIX_TILE
    n_tiles = s_len // tm

    def body(dxo_ref, ya_ref, yc_ref, conv_ref, u_ref, b_ref, c_ref, ga_ref, gc_ref, mod_ref, cw_ref,
             wa_ref, wc_ref, wo_ref, do_ref, drest_ref, dz_ref, dya_ref, dyc_ref, st_ref, carry):
        @pl.when(pl.program_id(0) == 0)
        def _():
            carry[...] = jnp.zeros_like(carry)
            st_ref[...] = jnp.zeros_like(st_ref)

        dz = (mod_ref[2:3, :] * dxo_ref[...]).astype(BF16)
        dz_ref[...] = dz
        dm = _dot_nt(dz, wo_ref[...])
        sa, sc = _sigmoid(ga_ref[...].astype(F32)), _sigmoid(gc_ref[...].astype(F32))
        dya = (dm * sa).astype(BF16)
        dyc = (dm * sc).astype(BF16)
        dya_ref[...] = dya
        dyc_ref[...] = dyc
        drest_ref[:, 3 * d:4 * d] = (dm * ya_ref[...].astype(F32) * (sa * (1.0 - sa))).astype(BF16)
        drest_ref[:, 4 * d:5 * d] = (dm * yc_ref[...].astype(F32) * (sc * (1.0 - sc))).astype(BF16)
        do_ref[...] = _dot_nt(dya, wa_ref[...])
        dyb = _dot_nt(dyc, wc_ref[...])
        drest_ref[:, d:2 * d] = (dyb * conv_ref[...].astype(F32)).astype(BF16)
        dconv = dyb * b_ref[...].astype(F32)
        edge = carry[...]
        sh1 = _shift_rows(dconv, 1, edge, False)
        sh2 = _shift_rows(dconv, 2, edge, False)
        carry[...] = dconv[0:8, :]
        dxc = dconv * cw_ref[2:3, :] + sh1 * cw_ref[1:2, :] + sh2 * cw_ref[0:1, :]
        u, c = u_ref[...].astype(F32), c_ref[...].astype(F32)
        xc = c * u
        drest_ref[:, 0:d] = (dxc * c).astype(BF16)
        drest_ref[:, 2 * d:3 * d] = (dxc * u).astype(BF16)
        st_ref[0:1, :] += jnp.sum(xc * sh2, axis=0, keepdims=True)
        st_ref[1:2, :] += jnp.sum(xc * sh1, axis=0, keepdims=True)
        st_ref[2:3, :] += jnp.sum(xc * dconv, axis=0, keepdims=True)

    rev = lambda i: n_tiles - 1 - i
    tile = pl.BlockSpec((tm, d), lambda i: (rev(i), 0))
    sect = lambda k: pl.BlockSpec((tm, d), lambda i: (rev(i), k))
    const = lambda shape: pl.BlockSpec(shape, lambda i: (0, 0))
    b16_out = jax.ShapeDtypeStruct((s_len, d), BF16)
    return pl.pallas_call(
        body, name=name, grid=(n_tiles,),
        in_specs=[tile, tile, tile, tile, sect(0), sect(1), sect(2), sect(3), sect(4), const((8, d)), const((8, d)),
                  const((a_w, d)), const((d, d)), const((d, d))],
        out_specs=[pl.BlockSpec((tm, a_w), lambda i: (rev(i), 0)), pl.BlockSpec((tm, 5 * d), lambda i: (rev(i), 0)),
                   tile, tile, tile, const((8, d))],
        out_shape=[jax.ShapeDtypeStruct((s_len, a_w), F32), jax.ShapeDtypeStruct((s_len, 5 * d), BF16),
                   b16_out, b16_out, b16_out, jax.ShapeDtypeStruct((8, d), F32)],
        scratch_shapes=[pltpu.VMEM((8, d), F32)],
        compiler_params=_params(1),
    )(dxo, ya, yc, conv, rest, rest, rest, rest, rest, mod, conv_w, w_attn, w_conv, w_out)


ADA_COLS = 128


def _ada_fwd(c_all, w_shard, b_shard, name):
    d, cols = w_shard.shape

    def body(c_ref, w_ref, b_ref, o_ref):
        cv = c_ref[...]
        o_ref[...] = jnp.dot(cv * _sigmoid(cv), w_ref[...], preferred_element_type=F32,
                             precision=lax.Precision.HIGHEST) + b_ref[...]

    return pl.pallas_call(
        body, name=name, grid=(cols // ADA_COLS,),
        in_specs=[pl.BlockSpec((8, d), lambda j: (0, 0)), pl.BlockSpec((d, ADA_COLS), lambda j: (0, j)),
                  pl.BlockSpec((1, ADA_COLS), lambda j: (0, j))],
        out_specs=pl.BlockSpec((8, ADA_COLS), lambda j: (0, j)),
        out_shape=jax.ShapeDtypeStruct((8, cols), F32),
        compiler_params=_params(1),
    )(c_all, w_shard, b_shard)


def _ada_bwd(c_all, dmod_shard, w, m, v, name):
    d, cols = w.shape

    def body(c_ref, dm_ref, w_ref, m_ref, v_ref, g_ref, d_ref, nm_ref, nv_ref):
        cv = c_ref[...]
        g = lax.dot_general(cv * _sigmoid(cv), dm_ref[...], (((0,), (0,)), ((), ())),
                            preferred_element_type=F32, precision=lax.Precision.HIGHEST)
        g_ref[...] = g
        d_ref[...], nm_ref[...], nv_ref[...] = _adamw_math(w_ref[...], g, m_ref[...], v_ref[...])

    blk = pl.BlockSpec((d, ADA_COLS), lambda j: (0, j))
    shape = jax.ShapeDtypeStruct((d, cols), F32)
    return pl.pallas_call(
        body, name=name, grid=(cols // ADA_COLS,),
        in_specs=[pl.BlockSpec((8, d), lambda j: (0, 0)), pl.BlockSpec((8, ADA_COLS), lambda j: (0, j)), blk, blk, blk],
        out_specs=[blk] * 4, out_shape=[shape] * 4,
        compiler_params=_params(1),
    )(c_all, dmod_shard, w, m, v)


def _small_update(parts, w, m, v, name):
    n = w.shape[1]

    def body(p_ref, w_ref, m_ref, v_ref, g_ref, d_ref, nm_ref, nv_ref):
        g = p_ref[0:1, :]
        for i in range(1, 8):
            g = g + p_ref[i:i + 1, :]
        g_ref[...] = g
        d_ref[...], nm_ref[...], nv_ref[...] = _adamw_math(w_ref[...], g, m_ref[...], v_ref[...])

    shape = jax.ShapeDtypeStruct((1, n), F32)
    return pl.pallas_call(body, name=name, out_shape=[shape] * 4, compiler_params=_params())(parts, w, m, v)


def _cols_to_shards(w, n):
    r, nc = w.shape
    return w.reshape(r, n, nc // n).transpose(1, 0, 2)


def kernel(x, c, w_ada, b_ada, norm_ffn1, ffn1_w_gate, ffn1_w_up, ffn1_w_down, norm_mix, w_in, q_norm, k_norm, conv_w, w_attn_branch, w_conv_branch, w_out, norm_ffn2, ffn2_w_gate, ffn2_w_up, ffn2_w_down, loss_target, m_w_ada, m_b_ada, m_norm_ffn1, m_ffn1_w_gate, m_ffn1_w_up, m_ffn1_w_down, m_norm_mix, m_w_in, m_q_norm, m_k_norm, m_conv_w, m_w_attn_branch, m_w_conv_branch, m_w_out, m_norm_ffn2, m_ffn2_w_gate, m_ffn2_w_up, m_ffn2_w_down, v_w_ada, v_b_ada, v_norm_ffn1, v_ffn1_w_gate, v_ffn1_w_up, v_ffn1_w_down, v_norm_mix, v_w_in, v_q_norm, v_k_norm, v_conv_w, v_w_attn_branch, v_w_conv_branch, v_w_out, v_norm_ffn2, v_ffn2_w_gate, v_ffn2_w_up, v_ffn2_w_down):
    ix, iy, ic = _place()
    chip = 2 * ix + iy
    me = 4 * ix + 2 * iy + ic
    xs = x[0]
    target = loss_target[0]
    s_len, d = xs.shape
    ada_cols = w_ada.shape[2]
    conv_cols = conv_w.shape[2]

    conv_rows = jnp.zeros((8, conv_cols), F32).at[0:3].set(conv_w[0])
    small_in = jnp.concatenate([jnp.broadcast_to(c, (8, d)), conv_rows], axis=1)
    small_all = _allgather8(small_in, "gather_c").reshape(8, 8, d + conv_cols)
    c_all = small_all[:, 0, :d]
    conv_full = small_all[0::2, 0:3, d:].transpose(1, 0, 2).reshape(3, N_CHIPS * conv_cols)
    conv_pad = jnp.zeros((8, N_CHIPS * conv_cols), F32).at[0:3].set(conv_full)
    b_shard = lax.dynamic_slice(b_ada, (0, chip * ada_cols), (1, ada_cols))
    mod_part = _ada_fwd(c_all, w_ada[0], b_shard, "ada_fwd")
    mod_all = _allgather8(mod_part, "gather_mod").reshape(N_CHIPS, 2, 8, ada_cols)[:, 0]
    mod_mine = lax.dynamic_slice(mod_all, (0, me, 0), (N_CHIPS, 1, ada_cols)).reshape(9, d)

    def mod_rows(i, gain):
        return jnp.zeros((8, d), F32).at[0:3].set(mod_mine[3 * i:3 * i + 3]).at[3:4].set(gain)

    mod1, mod2, mod3 = mod_rows(0, norm_ffn1), mod_rows(1, norm_mix), mod_rows(2, norm_ffn2)

    to16 = lambda w: w[0].astype(BF16)
    wg1, wu1, wd1 = _gather_weights([to16(ffn1_w_gate), to16(ffn1_w_up), to16(ffn1_w_down)], [False] * 3,
                                    "gather_ffn1", 1)
    h1, h1t = _norm_mod(xs, mod1, "norm1")
    (w_in_full,) = _gather_weights([to16(w_in)], [True], "gather_w_in", 2, after=(wd1, h1))

    g1, u1, y1 = _ffn_fwd(h1, wg1, wu1, wd1, "ffn1_fwd")
    x1, h2, h2t = _norm_mod(xs, mod2, "norm2", prev=(y1, mod1, 0.5))
    qkv, rest, qkv_hat = _in_proj(h2, w_in_full, q_norm, k_norm, "in_proj")
    w_ab, w_cb_g, w_o_g, wg2, wu2, wd2 = _gather_weights(
        [to16(w_attn_branch), to16(w_conv_branch), to16(w_out),
         to16(ffn2_w_gate), to16(ffn2_w_up), to16(ffn2_w_down)], [True] + [False] * 5,
        "gather_rest", 3, after=(h2,))
    a_w = w_ab.shape[0]
    w_cb = w_cb_g.reshape(d, d)
    w_o = w_o_g.reshape(d, d)
    o, lse = _attn_fwd(qkv_hat, "attn_fwd")
    x2, z, ya, yc, conv, yb, merged, o16 = _mix_fwd(x1, o, rest, mod2, conv_pad, w_ab, w_cb, w_o, "mix_fwd")
    h3, h3t = _norm_mod(x2, mod3, "norm3")
    g3, u3, y3 = _ffn_fwd(h3, wg2, wu2, wd2, "ffn2_fwd")
    dx3, dy3, loss_part = _loss_grad(x2, y3, mod3, target, "loss")
    loss = lax.psum(0.5 * jnp.sum(loss_part) / d, ("x", "y", "c"))

    c_idx = jnp.reshape(ic, (1,)).astype(jnp.int32)
    chip_idx = jnp.stack([chip, ic]).astype(jnp.int32)

    def reduce_start(grads, names, tag, collective_id):
        from_sibling = _rs_pair_exchange(grads, "rs_pair_" + tag)
        pair_sums = [_pair_add(g, r, c_idx, "pair_add_" + nm) for g, r, nm in zip(grads, from_sibling, names)]
        return pair_sums, _rs_chip_exchange(pair_sums, "rs_chips_" + tag, collective_id)

    def reduce_finish(pair_sums, from_chips, names, tag, after):
        totals = [_chip_add(p, r, chip_idx, "chip_add_" + nm, after)
                  for p, r, nm in zip(pair_sums, from_chips, names)]
        return dict(zip(names, _rs_share(totals, "rs_share_" + tag)))

    names_a = ["ffn2_w_gate", "ffn2_w_up", "ffn2_w_down"]
    names_b = ["w_in", "w_attn_branch", "w_conv_branch", "w_out"]
    names_c = ["ffn1_w_gate", "ffn1_w_up", "ffn1_w_down"]

    dh3, dg3, du3, a3 = _ffn_bwd(dy3, g3, u3, wg2, wu2, wd2, "ffn2_bwd")
    sums_a, chips_a = reduce_start(list(_ffn_wgrads(h3t, dg3, du3, a3, dy3, "ffn2")), names_a, "a", 4)
    dx2, st3 = _norm_bwd(dh3, x2, mod3, dx3, y3, 0.5, "norm3_bwd", after=tuple(sums_a))

    do, drest, dz, dya, dyc, st_conv = _mix_bwd(dx2, ya, yc, conv, rest, mod2, conv_pad, w_ab, w_cb, w_o, a_w, "mix_bwd")
    dq, dk, dv, st_qk = _attn_bwd(qkv, qkv_hat, do, o, lse, q_norm, k_norm, "attn_bwd")
    tok = lambda width: (lambda ts: pl.BlockSpec((ts, width), lambda cc, s: (s, 0)))
    colblk = lambda width: (lambda ts: pl.BlockSpec((ts, width), lambda cc, s: (s, cc)))
    tok_t = lambda ts: pl.BlockSpec((d, ts), lambda cc, s: (0, s))
    whole = pl.BlockSpec((d, QKV), lambda cc, s: (0, 0))
    dw_in = [_wgrad(h2t, part, tok_t, tok(QKV), (d, QKV), whole, (d, QKV), 1, "dw_in_" + nm, True)
             for part, nm in ((dq, "q"), (dk, "k"), (dv, "v"))]
    dw_in.append(_wgrad(h2t, drest, tok_t, colblk(d), (d, 5 * d), pl.BlockSpec((d, d), lambda cc, s: (0, cc)),
                        (d, d), 5, "dw_in_rest", True))
    dw_in = _cols_to_shards(jnp.concatenate(dw_in, axis=1), N_CHIPS)
    shard_w = d // N_CHIPS
    dw_ab = _wgrad(o16, dya, tok(a_w), colblk(shard_w), (a_w, d), pl.BlockSpec((a_w, shard_w), lambda cc, s: (0, cc)),
                   (a_w, shard_w), N_CHIPS, "dw_attn_branch")
    dw_ab = _cols_to_shards(dw_ab, N_CHIPS)
    row_out = pl.BlockSpec((None, shard_w, d), lambda cc, s: (cc, 0, 0))
    dw_cb = _wgrad(yb, dyc, colblk(shard_w), tok(d), (N_CHIPS, shard_w, d), row_out, (shard_w, d), N_CHIPS, "dw_conv_branch")
    dw_o = _wgrad(merged, dz, colblk(shard_w), tok(d), (N_CHIPS, shard_w, d), row_out, (shard_w, d), N_CHIPS, "dw_out")
    shard_grads = reduce_finish(sums_a, chips_a, names_a, "a", after=(dw_in, dw_o))
    sums_b, chips_b = reduce_start([dw_in, dw_ab, dw_cb, dw_o], names_b, "b", 5)

    dh2 = _in_proj_bwd(dq, dk, dv, drest, w_in_full, "in_proj_bwd", after=tuple(sums_b))
    dx1, st2, dy1 = _norm_bwd(dh2, x1, mod2, dx2, z, 1.0, "norm2_bwd", prev=(mod1, 0.5))
    dh1, dg1, du1, a1 = _ffn_bwd(dy1, g1, u1, wg1, wu1, wd1, "ffn1_bwd")
    dx0, st1 = _norm_bwd(dh1, xs, mod1, dx1, y1, 0.5, "norm1_bwd")
    grads_c = list(_ffn_wgrads(h1t, dg1, du1, a1, dy1, "ffn1"))
    shard_grads.update(reduce_finish(sums_b, chips_b, names_b, "b", after=tuple(grads_c)))
    sums_c, chips_c = reduce_start(grads_c, names_c, "c", 6)

    dmod = jnp.concatenate([st1[0:3], st2[0:3], st3[0:3]], axis=0).reshape(1, 9 * d)
    small = jnp.concatenate([dmod, st1[3:4], st2[3:4], st3[3:4], st_qk[0:1], st_qk[1:2],
                             st_conv[0:3].reshape(1, 3 * d)], axis=1)
    small_all = _allgather8(jnp.broadcast_to(small, (8, small.shape[1])), "gather_small").reshape(8, 8, -1)[:, 0]
    dmod_all = small_all[:, :9 * d]
    dmod_shard = lax.dynamic_slice(dmod_all, (0, chip * ada_cols), (8, ada_cols))
    g_w_ada, d_w_ada, nm_w_ada, nv_w_ada = _ada_bwd(c_all, dmod_shard, w_ada[0], m_w_ada[0], v_w_ada[0], "ada_bwd")

    vec_names = ["b_ada", "norm_ffn1", "norm_mix", "norm_ffn2", "q_norm", "k_norm"]
    vec_w = [b_ada, norm_ffn1, norm_mix, norm_ffn2, q_norm, k_norm]
    vec_m = [m_b_ada, m_norm_ffn1, m_norm_mix, m_norm_ffn2, m_q_norm, m_k_norm]
    vec_v = [v_b_ada, v_norm_ffn1, v_norm_mix, v_norm_ffn2, v_q_norm, v_k_norm]
    n_vec = sum(w.shape[1] for w in vec_w)
    cat = lambda arrs: jnp.concatenate(arrs, axis=1)
    vec_out = _small_update(small_all[:, :n_vec], cat(vec_w), cat(vec_m), cat(vec_v), "small_update")
    conv_parts = small_all[:, n_vec:].reshape(8, 3, N_CHIPS * conv_cols)
    conv_parts = lax.dynamic_slice(conv_parts, (0, 0, chip * conv_cols), (8, 3, conv_cols)).reshape(8, 3 * conv_cols)
    flat3 = lambda w: w[0].reshape(1, 3 * conv_cols)
    conv_out = _small_update(conv_parts, flat3(conv_w), flat3(m_conv_w), flat3(v_conv_w), "conv_update")

    res = {"w_ada": [t[None] for t in (g_w_ada, d_w_ada, nm_w_ada, nv_w_ada)],
           "conv_w": [t.reshape(1, 3, conv_cols) for t in conv_out]}
    off = 0
    for nm, w in zip(vec_names, vec_w):
        width = w.shape[1]
        res[nm] = [t[:, off:off + width] for t in vec_out]
        off += width
    big = {"ffn1_w_gate": (ffn1_w_gate, m_ffn1_w_gate, v_ffn1_w_gate), "ffn1_w_up": (ffn1_w_up, m_ffn1_w_up, v_ffn1_w_up),
           "ffn1_w_down": (ffn1_w_down, m_ffn1_w_down, v_ffn1_w_down), "w_in": (w_in, m_w_in, v_w_in),
           "w_attn_branch": (w_attn_branch, m_w_attn_branch, v_w_attn_branch),
           "w_conv_branch": (w_conv_branch, m_w_conv_branch, v_w_conv_branch), "w_out": (w_out, m_w_out, v_w_out),
           "ffn2_w_gate": (ffn2_w_gate, m_ffn2_w_gate, v_ffn2_w_gate), "ffn2_w_up": (ffn2_w_up, m_ffn2_w_up, v_ffn2_w_up),
           "ffn2_w_down": (ffn2_w_down, m_ffn2_w_down, v_ffn2_w_down)}
    def update(nm, after=()):
        w, m, v = big[nm]
        g, delta, new_m, new_v = _adamw(w[0], shard_grads[nm], m[0], v[0], "adamw_" + nm, after)
        res[nm] = [t[None] for t in (g, delta, new_m, new_v)]
        return new_v

    last = tuple(sums_c)
    for nm in names_a + names_b:
        last = (update(nm, last),)
    shard_grads.update(reduce_finish(sums_c, chips_c, names_c, "c", after=last))
    for nm in names_c:
        update(nm)

    order = ["w_ada", "b_ada", "norm_ffn1", "ffn1_w_gate", "ffn1_w_up", "ffn1_w_down", "norm_mix", "w_in", "q_norm",
             "k_norm", "conv_w", "w_attn_branch", "w_conv_branch", "w_out", "norm_ffn2", "ffn2_w_gate", "ffn2_w_up",
             "ffn2_w_down"]
    return (loss, dx0[None], *[res[nm][0] for nm in order], *[res[nm][1] for nm in order],
            *[res[nm][2] for nm in order], *[res[nm][3] for nm in order])
```

```python
import jax
import jax.numpy as jnp
from jax import lax
from jax.experimental import pallas as pl
from jax.experimental.pallas import tpu as pltpu
from jax.experimental.pallas import tpu_sc as plsc

F32 = jnp.float32
BF16 = jnp.bfloat16
MESH = pl.DeviceIdType.MESH
ANY = pl.BlockSpec(memory_space=pl.ANY)

NORM_EPS = 1e-6
HEAD_DIM = 128
N_GROUPS = 3
HEADS = 4
DILATIONS = (1, 4, 16)
ATTN_BLOCK = 128
SLAB = ATTN_BLOCK * max(DILATIONS)
QKV = N_GROUPS * HEADS * HEAD_DIM
ATTN_SCALE = HEAD_DIM ** -0.5
NEG = -1e30
N_CHIPS = 4

ADAM_LR = 0.001
ADAM_B1 = 0.9
ADAM_B2 = 0.999
ADAM_EPS = 1e-08
ADAM_WD = 0.01
ADAM_STEP = 10

VMEM_LIMIT_BYTES = 56 * 1024 * 1024
TOKEN_TILE = 512
FFN_TILE = 1024
PROJ_TILE = 2048
WGRAD_TILE = 2048
IN_BLOCK = 512
MIX_TILE = 256


def _params(n_axes=0):
    return pltpu.CompilerParams(
        dimension_semantics=("arbitrary",) * n_axes if n_axes else None,
        vmem_limit_bytes=VMEM_LIMIT_BYTES)


def _dot(a, b):
    return jnp.dot(a, b, preferred_element_type=F32)


def _dot_nt(a, b):
    return lax.dot_general(a, b, (((1,), (1,)), ((), ())), preferred_element_type=F32)


def _dot_tn(a, b):
    return lax.dot_general(a, b, (((0,), (0,)), ((), ())), preferred_element_type=F32)


def _sigmoid(x):
    return 1.0 / (1.0 + jnp.exp(-x))


def _place():
    return lax.axis_index("x"), lax.axis_index("y"), lax.axis_index("c")


def _ordered(body, n_in, after):
    if not after:
        return body
    return lambda *refs: body(*refs[:n_in], *refs[n_in + len(after):])


def _allgather8(block, name):
    m_per, n = block.shape

    def body(x_ref, out_ref, send_sems, recv_sems, local_sem):
        x, y, c = _place()
        me, sibling = (x, y, c), (x, y, 1 - c)
        chips = [(1 - x, y), (x, 1 - y), (1 - x, 1 - y)]

        def rows(px, py, pc):
            return out_ref.at[pl.ds((4 * px + 2 * py + pc) * m_per, m_per), :]

        def copy(k, blk, to, src=None):
            return pltpu.make_async_remote_copy(
                src_ref=rows(*blk) if src is None else src, dst_ref=rows(*blk),
                send_sem=send_sems.at[k], recv_sem=recv_sems.at[k],
                device_id=to, device_id_type=MESH)

        mine = pltpu.make_async_copy(x_ref, rows(*me), local_sem)
        mine.start()
        first = [copy(0, me, sibling, src=x_ref)]
        first += [copy(1 + j, me, (*chip, c), src=x_ref) for j, chip in enumerate(chips)]
        for cp in first:
            cp.start()
        passed = [copy(4 + j, (*chip, c), sibling) for j, chip in enumerate(chips)]
        for j, chip in enumerate(chips):
            copy(1 + j, (*chip, c), me).wait_recv()
            passed[j].start()
        copy(0, sibling, me).wait_recv()
        for j, chip in enumerate(chips):
            copy(4 + j, (*chip, 1 - c), me).wait_recv()
        for cp in first + passed:
            cp.wait_send()
        mine.wait()

    return pl.pallas_call(
        body, name=name,
        out_shape=jax.ShapeDtypeStruct((8 * m_per, n), block.dtype),
        in_specs=[pl.BlockSpec(memory_space=pltpu.VMEM)],
        out_specs=pl.BlockSpec(memory_space=pltpu.VMEM),
        scratch_shapes=[pltpu.SemaphoreType.DMA((7,)), pltpu.SemaphoreType.DMA((7,)),
                        pltpu.SemaphoreType.DMA],
        compiler_params=_params(),
    )(block)


def _handshake(peers):
    barrier = pltpu.get_barrier_semaphore()
    for peer in peers:
        pl.semaphore_signal(barrier, inc=1, device_id=peer, device_id_type=MESH)
    pl.semaphore_wait(barrier, len(peers))


def _gather_weights(shards, by_cols, name, collective_id, after=()):
    n_arr = len(shards)

    def body(*refs):
        srcs, outs = refs[:n_arr], refs[n_arr + len(after):2 * n_arr + len(after)]
        send_sems, recv_sems, local_sems = refs[2 * n_arr + len(after):]
        x, y, c = _place()
        me_dev, sibling = (x, y, c), (x, y, 1 - c)
        chips = [(1 - x, y), (x, 1 - y), (1 - x, 1 - y)]
        me = 2 * x + y
        _handshake([sibling] + [(*chip, c) for chip in chips])

        def place(k, chip_idx, rows):
            if by_cols[k]:
                width = srcs[k].shape[1]
                return outs[k].at[rows, pl.ds(pl.multiple_of(chip_idx * width, 128), width)]
            return outs[k].at[chip_idx, rows]

        def copy(k, slot, chip_idx, half_sel, to, from_shard=False):
            half = srcs[k].shape[0] // 2
            rows = pl.ds(half_sel * half, half)
            dst = place(k, chip_idx, rows)
            return pltpu.make_async_remote_copy(
                src_ref=srcs[k].at[rows] if from_shard else dst, dst_ref=dst,
                send_sem=send_sems.at[6 * k + slot], recv_sem=recv_sems.at[6 * k + slot],
                device_id=to, device_id_type=MESH)

        own = [pltpu.make_async_copy(srcs[k], place(k, me, pl.ds(0, srcs[k].shape[0])), local_sems.at[k])
               for k in range(n_arr)]
        for cp in own:
            cp.start()
        sent = []
        for k in range(n_arr):
            for j, chip in enumerate(chips):
                sent.append(copy(k, j, me, c, (*chip, c), from_shard=True))
                sent[-1].start()
        for k in range(n_arr):
            for j, chip in enumerate(chips):
                chip_idx = 2 * chip[0] + chip[1]
                copy(k, j, chip_idx, c, me_dev).wait_recv()
                sent.append(copy(k, 3 + j, chip_idx, c, sibling))
                sent[-1].start()
        for k in range(n_arr):
            for j, chip in enumerate(chips):
                copy(k, 3 + j, 2 * chip[0] + chip[1], 1 - c, me_dev).wait_recv()
        for cp in sent:
            cp.wait_send()
        for cp in own:
            cp.wait()

    def gathered(k):
        r, cols = shards[k].shape
        return (r, N_CHIPS * cols) if by_cols[k] else (N_CHIPS, r, cols)

    return pl.kernel(
        body, name=name,
        out_type=[jax.ShapeDtypeStruct(gathered(k), shards[k].dtype) for k in range(n_arr)],
        mesh=plsc.ScalarSubcoreMesh(axis_name="sequencer", num_cores=1),
        scratch_types=[pltpu.SemaphoreType.DMA((6 * n_arr,)), pltpu.SemaphoreType.DMA((6 * n_arr,)),
                       pltpu.SemaphoreType.DMA((n_arr,))],
        compiler_params=pltpu.CompilerParams(collective_id=collective_id),
    )(*shards, *after)


def _rs_pair_exchange(grads, name):
    n_arr = len(grads)

    def body(*refs):
        srcs, outs = refs[:n_arr], refs[n_arr:2 * n_arr]
        send_sems, recv_sems = refs[2 * n_arr:]
        x, y, c = _place()
        cps = []
        for k in range(n_arr):
            half = srcs[k].shape[1] // 2
            cps.append(pltpu.make_async_remote_copy(
                src_ref=srcs[k].at[:, pl.ds((1 - c) * half, half)], dst_ref=outs[k],
                send_sem=send_sems.at[k], recv_sem=recv_sems.at[k],
                device_id=(x, y, 1 - c), device_id_type=MESH))
            cps[-1].start()
        for cp in cps:
            cp.wait_recv()
        for cp in cps:
            cp.wait_send()

    return pl.pallas_call(
        body, name=name,
        out_shape=[jax.ShapeDtypeStruct((g.shape[0], g.shape[1] // 2, g.shape[2]), g.dtype) for g in grads],
        in_specs=[ANY] * n_arr, out_specs=[ANY] * n_arr,
        scratch_shapes=[pltpu.SemaphoreType.DMA((n_arr,)), pltpu.SemaphoreType.DMA((n_arr,))],
        compiler_params=_params(),
    )(*grads)


def _rs_chip_exchange(sums, name, collective_id):
    n_arr = len(sums)

    def body(*refs):
        srcs, outs = refs[:n_arr], refs[n_arr:2 * n_arr]
        send_sems, recv_sems = refs[2 * n_arr:]
        x, y, c = _place()
        chips = [(1 - x, y), (x, 1 - y), (1 - x, 1 - y)]
        _handshake([(*chip, c) for chip in chips])
        cps = []
        for k in range(n_arr):
            for j, chip in enumerate(chips):
                cps.append(pltpu.make_async_remote_copy(
                    src_ref=srcs[k].at[2 * chip[0] + chip[1]], dst_ref=outs[k].at[j],
                    send_sem=send_sems.at[3 * k + j], recv_sem=recv_sems.at[3 * k + j],
                    device_id=(*chip, c), device_id_type=MESH))
                cps[-1].start()
        for cp in cps:
            cp.wait_recv()
        for cp in cps:
            cp.wait_send()

    return pl.kernel(
        body, name=name,
        out_type=[jax.ShapeDtypeStruct((3,) + s.shape[1:], s.dtype) for s in sums],
        mesh=plsc.ScalarSubcoreMesh(axis_name="sequencer", num_cores=1),
        scratch_types=[pltpu.SemaphoreType.DMA((3 * n_arr,)), pltpu.SemaphoreType.DMA((3 * n_arr,))],
        compiler_params=pltpu.CompilerParams(collective_id=collective_id),
    )(*sums)


def _rs_share(totals, name):
    n_arr = len(totals)

    def body(*refs):
        outs = refs[n_arr:2 * n_arr]
        send_sems, recv_sems = refs[2 * n_arr:]
        x, y, c = _place()

        def half_rows(k, sel):
            return outs[k].at[sel]

        cps = []
        for k in range(n_arr):
            cps.append(pltpu.make_async_remote_copy(
                src_ref=half_rows(k, c), dst_ref=half_rows(k, c), send_sem=send_sems.at[k], recv_sem=recv_sems.at[k],
                device_id=(x, y, 1 - c), device_id_type=MESH))
            cps[-1].start()
        for k in range(n_arr):
            pltpu.make_async_remote_copy(
                src_ref=half_rows(k, c), dst_ref=half_rows(k, 1 - c), send_sem=send_sems.at[k],
                recv_sem=recv_sems.at[k], device_id=(x, y, 1 - c), device_id_type=MESH).wait_recv()
        for cp in cps:
            cp.wait_send()

    shared = pl.pallas_call(
        body, name=name,
        out_shape=[jax.ShapeDtypeStruct(t.shape, t.dtype) for t in totals],
        in_specs=[ANY] * n_arr, out_specs=[ANY] * n_arr,
        input_output_aliases={k: k for k in range(n_arr)},
        scratch_shapes=[pltpu.SemaphoreType.DMA((n_arr,)), pltpu.SemaphoreType.DMA((n_arr,))],
        compiler_params=_params(),
    )(*totals)
    return [t.reshape(2 * t.shape[1], t.shape[2]) for t in shared]


def _pair_add(grad, recv, c_idx, name):
    n, r, cols = grad.shape
    half = r // 2
    rows = half // 2

    def body(_, g_ref, r_ref, o_ref):
        o_ref[...] = (g_ref[...].astype(F32) + r_ref[...].astype(F32)).astype(o_ref.dtype)

    return pl.pallas_call(
        body, name=name,
        grid_spec=pltpu.PrefetchScalarGridSpec(
            num_scalar_prefetch=1, grid=(n, 2),
            in_specs=[pl.BlockSpec((None, None, rows, cols), lambda s, i, ci: (s, ci[0], i, 0)),
                      pl.BlockSpec((None, rows, cols), lambda s, i, ci: (s, i, 0))],
            out_specs=pl.BlockSpec((None, rows, cols), lambda s, i, ci: (s, i, 0))),
        out_shape=jax.ShapeDtypeStruct((n, half, cols), BF16),
        compiler_params=_params(2),
    )(c_idx, grad.reshape(n, 2, half, cols), recv)


def _chip_add(sums, recv, chip_and_core, name, after=()):
    _, half, cols = sums.shape
    rows = half // 2

    def body(_, s_ref, r0_ref, r1_ref, r2_ref, o_ref):
        o_ref[...] = ((s_ref[...].astype(F32) + r0_ref[...].astype(F32))
                      + r1_ref[...].astype(F32)) + r2_ref[...].astype(F32)

    def recv_spec(j):
        return pl.BlockSpec((None, rows, cols), lambda i, ci: (j, i, 0))

    return pl.pallas_call(
        _ordered(body, 5, after), name=name,
        grid_spec=pltpu.PrefetchScalarGridSpec(
            num_scalar_prefetch=1, grid=(2,),
            in_specs=[pl.BlockSpec((None, rows, cols), lambda i, ci: (ci[0], i, 0)),
                      recv_spec(0), recv_spec(1), recv_spec(2)] + [ANY] * len(after),
            out_specs=pl.BlockSpec((None, rows, cols), lambda i, ci: (ci[1], i, 0))),
        out_shape=jax.ShapeDtypeStruct((2, half, cols), F32),
        compiler_params=_params(1),
    )(chip_and_core, sums, recv, recv, recv, *after)


def _rms(x):
    return lax.rsqrt(jnp.mean(x * x, axis=-1, keepdims=True) + NORM_EPS)


def _norm_mod(x, mod, name, prev=None):
    s_len, d = x.shape
    tm = TOKEN_TILE

    def body(*refs):
        if prev is None:
            x_ref, mod_ref, h_ref, ht_ref = refs
            xv = x_ref[...]
        else:
            x_ref, y_ref, modp_ref, mod_ref, xo_ref, h_ref, ht_ref = refs
            xv = x_ref[...] + prev[2] * modp_ref[2:3, :] * y_ref[...]
            xo_ref[...] = xv
        n = (xv * _rms(xv)) * mod_ref[3:4, :]
        h = n * (1.0 + mod_ref[1:2, :]) + mod_ref[0:1, :]
        h_ref[...] = h.astype(BF16)
        ht_ref[...] = h.T.astype(BF16)

    tile = pl.BlockSpec((tm, d), lambda i: (i, 0))
    small = pl.BlockSpec((8, d), lambda i: (0, 0))
    h_specs = [tile, pl.BlockSpec((d, tm), lambda i: (0, i))]
    h_shapes = [jax.ShapeDtypeStruct((s_len, d), BF16), jax.ShapeDtypeStruct((d, s_len), BF16)]
    if prev is None:
        return pl.pallas_call(
            body, name=name, grid=(s_len // tm,), in_specs=[tile, small], out_specs=h_specs, out_shape=h_shapes,
            compiler_params=_params(1))(x, mod)
    return pl.pallas_call(
        body, name=name, grid=(s_len // tm,), in_specs=[tile, tile, small, small],
        out_specs=[tile] + h_specs, out_shape=[jax.ShapeDtypeStruct((s_len, d), F32)] + h_shapes,
        compiler_params=_params(1))(x, prev[0], prev[1], mod)


def _norm_bwd(dh, x, mod, dxo, y_raw, coef, name, after=(), prev=None):
    s_len, d = x.shape
    tm = TOKEN_TILE

    def body(*refs):
        if prev is None:
            dh_ref, x_ref, mod_ref, dxo_ref, y_ref, dx_ref, st_ref = refs
        else:
            dh_ref, x_ref, mod_ref, dxo_ref, y_ref, modp_ref, dx_ref, st_ref, dyp_ref = refs

        @pl.when(pl.program_id(0) == 0)
        def _():
            st_ref[...] = jnp.zeros_like(st_ref)

        xv, dhv, dxov = x_ref[...], dh_ref[...], dxo_ref[...]
        r = _rms(xv)
        xh = xv * r
        gain, scale = mod_ref[3:4, :], mod_ref[1:2, :]
        dn = dhv * (1.0 + scale)
        dxh = dn * gain
        dx = dxov + r * (dxh - xh * jnp.mean(dxh * xh, axis=-1, keepdims=True))
        dx_ref[...] = dx
        if prev is not None:
            dyp_ref[...] = (prev[1] * modp_ref[2:3, :] * dx).astype(BF16)
        st_ref[0:1, :] += jnp.sum(dhv, axis=0, keepdims=True)
        st_ref[1:2, :] += jnp.sum(dhv * (xh * gain), axis=0, keepdims=True)
        st_ref[2:3, :] += coef * jnp.sum(y_ref[...] * dxov, axis=0, keepdims=True)
        st_ref[3:4, :] += jnp.sum(dn * xh, axis=0, keepdims=True)

    tile = pl.BlockSpec((tm, d), lambda i: (i, 0))
    small = pl.BlockSpec((8, d), lambda i: (0, 0))
    operands = [dh, x, mod, dxo, y_raw] + ([] if prev is None else [prev[0]])
    in_specs = [tile, tile, small, tile, tile] + ([] if prev is None else [small])
    out_specs = [tile, small] + ([] if prev is None else [tile])
    out_shape = [jax.ShapeDtypeStruct((s_len, d), F32), jax.ShapeDtypeStruct((8, d), F32)]
    if prev is not None:
        out_shape.append(jax.ShapeDtypeStruct((s_len, d), BF16))
    return pl.pallas_call(
        _ordered(body, len(operands), after), name=name, grid=(s_len // tm,),
        in_specs=in_specs + [ANY] * len(after), out_specs=out_specs, out_shape=out_shape,
        compiler_params=_params(1),
    )(*operands, *after)


def _loss_grad(x, y, mod, target, name):
    s_len, d = x.shape
    tm = TOKEN_TILE

    def body(x_ref, y_ref, mod_ref, t_ref, do_ref, dy_ref, part_ref):
        @pl.when(pl.program_id(0) == 0)
        def _():
            part_ref[...] = jnp.zeros_like(part_ref)

        half_gate = 0.5 * mod_ref[2:3, :]
        err = (x_ref[...] + half_gate * y_ref[...]) - t_ref[...]
        do = err * (1.0 / d)
        do_ref[...] = do
        dy_ref[...] = (half_gate * do).astype(BF16)
        sq = err * err
        part_ref[...] += jnp.sum(sq.reshape(tm // 8, 8, d), axis=0)

    tile = pl.BlockSpec((tm, d), lambda i: (i, 0))
    small = pl.BlockSpec((8, d), lambda i: (0, 0))
    return pl.pallas_call(
        body, name=name, grid=(s_len // tm,),
        in_specs=[tile, tile, small, tile],
        out_specs=[tile, tile, small],
        out_shape=[jax.ShapeDtypeStruct((s_len, d), F32), jax.ShapeDtypeStruct((s_len, d), BF16),
                   jax.ShapeDtypeStruct((8, d), F32)],
        compiler_params=_params(1),
    )(x, y, mod, target)


def _adamw_math(w, g, m, v):
    m = ADAM_B1 * m + (1.0 - ADAM_B1) * g
    v = ADAM_B2 * v + (1.0 - ADAM_B2) * (g * g)
    m_hat = m / (1.0 - ADAM_B1 ** ADAM_STEP)
    v_hat = v / (1.0 - ADAM_B2 ** ADAM_STEP)
    delta = -ADAM_LR * (m_hat / (jnp.sqrt(v_hat) + ADAM_EPS) + ADAM_WD * w)
    return delta, m, v


def _adamw(w, g, m, v, name, after=()):
    r, cols = w.shape
    tr = r // 8 if r % 64 == 0 else r

    def body(w_ref, g_ref, m_ref, v_ref, go_ref, d_ref, nm_ref, nv_ref):
        gv = g_ref[...]
        go_ref[...] = gv
        d_ref[...], nm_ref[...], nv_ref[...] = _adamw_math(w_ref[...], gv, m_ref[...], v_ref[...])

    tile = pl.BlockSpec((tr, cols), lambda i: (i, 0))
    shape = jax.ShapeDtypeStruct((r, cols), F32)
    return pl.pallas_call(
        _ordered(body, 4, after), name=name, grid=(r // tr,),
        in_specs=[tile] * 4 + [ANY] * len(after), out_specs=[tile] * 4, out_shape=[shape] * 4,
        compiler_params=_params(1),
    )(w, g, m, v, *after)


def _in_parts(tm, n_qkv, n_rest):
    def part(lo, n_blk):
        return pl.BlockSpec((tm, IN_BLOCK), lambda i, j: (i, jnp.clip(j - lo, 0, n_blk - 1)))
    return [part(0, n_qkv), part(n_qkv, n_qkv), part(2 * n_qkv, n_qkv), part(3 * n_qkv, n_rest)]


def _pick_part(j, n_qkv, refs, fn):
    bounds = [0, n_qkv, 2 * n_qkv, 3 * n_qkv]
    for p, ref in enumerate(refs):
        inside = j >= bounds[p]
        if p + 1 < len(refs):
            inside = inside & (j < bounds[p + 1])
        pl.when(inside)(lambda ref=ref: fn(ref))


def _rows(base, count, stride):
    return pl.ds(base, count) if stride == 1 else pl.ds(base, count, stride=stride)


REORDER_STRIDE = 4


def _reorder_plan(dil):
    inner = min(dil, REORDER_STRIDE)
    return inner, dil // inner, SLAB // inner, SLAB // dil


def _to_residue_order(dst, src, dil, tmp):
    inner, outer, big, seg = _reorder_plan(dil)
    if outer == 1:
        for r in range(dil):
            dst[pl.ds(r * seg, seg), :] = src[_rows(r, seg, dil), :].astype(dst.dtype)
        return
    for b in range(inner):
        tmp[pl.ds(b * big, big), :] = src[_rows(b, big, inner), :]
    for a in range(outer):
        for b in range(inner):
            dst[pl.ds((inner * a + b) * seg, seg), :] = tmp[_rows(b * big + a, seg, outer), :].astype(dst.dtype)


def _to_token_order(dst, src, dil, tmp):
    inner, outer, big, seg = _reorder_plan(dil)
    if outer == 1:
        for r in range(dil):
            dst[_rows(r, seg, dil), :] = src[pl.ds(r * seg, seg), :]
        return
    for a in range(outer):
        for b in range(inner):
            tmp[_rows(b * big + a, seg, outer), :] = src[pl.ds((inner * a + b) * seg, seg), :]
    for b in range(inner):
        dst[_rows(b, big, inner), :] = tmp[pl.ds(b * big, big), :]


def _in_proj(h, w, q_norm, k_norm, name):
    s_len, d = h.shape
    tm = PROJ_TILE
    assert tm == SLAB and IN_BLOCK == HEADS * HEAD_DIM
    steps = w.shape[1] // IN_BLOCK
    n_qkv = 3 * QKV // IN_BLOCK

    def body(h_ref, w_ref, qn_ref, kn_ref, qkv_ref, rest_ref, hat_ref, tok_s, tmp_s):
        j = pl.program_id(1)
        res = _dot(h_ref[...], w_ref[...])

        def emit(sect, gi):
            dil = DILATIONS[gi]
            for hh in range(HEADS):
                cols = slice(hh * HEAD_DIM, (hh + 1) * HEAD_DIM)
                x = res[:, cols]
                if sect < 2:
                    x = (x * _rms(x)) * (qn_ref if sect == 0 else kn_ref)[...]
                tok_s[...] = x
                _to_residue_order(hat_ref.at[:, cols], tok_s, dil, tmp_s)

        @pl.when(j < n_qkv)
        def _():
            qkv_ref[...] = res

        for sect in range(3):
            for gi in range(N_GROUPS):
                pl.when(j == sect * N_GROUPS + gi)(lambda sect=sect, gi=gi: emit(sect, gi))

        @pl.when(j >= n_qkv)
        def _():
            rest_ref[...] = res.astype(BF16)

    qkv_blk = pl.BlockSpec((tm, IN_BLOCK), lambda i, j: (i, jnp.minimum(j, n_qkv - 1)))
    small = pl.BlockSpec((1, HEAD_DIM), lambda i, j: (0, 0))
    return pl.pallas_call(
        body, name=name, grid=(s_len // tm, steps),
        in_specs=[pl.BlockSpec((tm, d), lambda i, j: (i, 0)), pl.BlockSpec((d, IN_BLOCK), lambda i, j: (0, j)),
                  small, small],
        out_specs=[qkv_blk, pl.BlockSpec((tm, IN_BLOCK), lambda i, j: (i, jnp.maximum(j - n_qkv, 0))), qkv_blk],
        out_shape=[jax.ShapeDtypeStruct((s_len, 3 * QKV), F32),
                   jax.ShapeDtypeStruct((s_len, w.shape[1] - 3 * QKV), BF16),
                   jax.ShapeDtypeStruct((s_len, 3 * QKV), BF16)],
        scratch_shapes=[pltpu.VMEM((tm, HEAD_DIM), F32)] * 2,
        compiler_params=_params(2),
    )(h, w, q_norm, k_norm)


def _in_proj_bwd(dq, dk, dv, drest, w, name, after=()):
    s_len = dq.shape[0]
    d = w.shape[0]
    tm = PROJ_TILE
    steps = w.shape[1] // IN_BLOCK
    n_qkv = QKV // IN_BLOCK

    def body(dq_ref, dk_ref, dv_ref, dr_ref, w_ref, o_ref, acc_ref):
        j = pl.program_id(1)

        @pl.when(j == 0)
        def _():
            acc_ref[...] = jnp.zeros_like(acc_ref)

        def add(a_ref):
            acc_ref[...] += _dot_nt(a_ref[...], w_ref[...])

        _pick_part(j, n_qkv, [dq_ref, dk_ref, dv_ref, dr_ref], add)

        @pl.when(j == steps - 1)
        def _():
            o_ref[...] = acc_ref[...]

    return pl.pallas_call(
        _ordered(body, 5, after), name=name, grid=(s_len // tm, steps),
        in_specs=(_in_parts(tm, n_qkv, steps - 3 * n_qkv) + [pl.BlockSpec((d, IN_BLOCK), lambda i, j: (0, j))]
                  + [ANY] * len(after)),
        out_specs=pl.BlockSpec((tm, d), lambda i, j: (i, 0)),
        out_shape=jax.ShapeDtypeStruct((s_len, d), F32),
        scratch_shapes=[pltpu.VMEM((tm, d), F32)],
        compiler_params=_params(2),
    )(dq, dk, dv, drest, w, *after)


def _wgrad(x, y, x_spec, y_spec, out_shape, out_spec, acc_shape, n_chunks, name, x_transposed=False, after=()):
    s_len = y.shape[-2]
    ts = WGRAD_TILE
    steps = s_len // ts

    def body(x_ref, y_ref, o_ref, acc_ref):
        s = pl.program_id(1)

        @pl.when(s == 0)
        def _():
            acc_ref[...] = jnp.zeros_like(acc_ref)

        acc_ref[...] += (_dot if x_transposed else _dot_tn)(x_ref[...], y_ref[...])

        @pl.when(s == steps - 1)
        def _():
            o_ref[...] = acc_ref[...].astype(o_ref.dtype)

    return pl.pallas_call(
        _ordered(body, 2, after), name=name, grid=(n_chunks, steps),
        in_specs=[x_spec(ts), y_spec(ts)] + [ANY] * len(after), out_specs=out_spec,
        out_shape=jax.ShapeDtypeStruct(out_shape, BF16),
        scratch_shapes=[pltpu.VMEM(acc_shape, F32)],
        compiler_params=_params(2),
    )(x, y, *after)


def _pieces(width, piece=256):
    return [slice(a, min(a + piece, width)) for a in range(0, width, piece)]


def _ffn_fwd(h, w_gate, w_up, w_down, name):
    s_len, d = h.shape
    n_chunks, _, fs = w_gate.shape
    tm = FFN_TILE

    def body(h_ref, wg_ref, wu_ref, wd_ref, g_ref, u_ref, y_ref):
        j = pl.program_id(1)
        hv = h_ref[...]
        pieces = _pieces(fs)
        first = lambda cols: (_dot(hv, wg_ref[:, cols]), _dot(hv, wu_ref[:, cols]))
        total = None
        ahead = first(pieces[0])
        for k, cols in enumerate(pieces):
            g, u = ahead
            if k + 1 < len(pieces):
                ahead = first(pieces[k + 1])
            g_ref[:, cols] = g.astype(BF16)
            u_ref[:, cols] = u.astype(BF16)
            act = (g * _sigmoid(g)) * u
            part = _dot(act.astype(BF16), wd_ref[cols, :])
            total = part if total is None else total + part

        @pl.when(j == 0)
        def _():
            y_ref[...] = total

        @pl.when(j > 0)
        def _():
            y_ref[...] += total

    tile = pl.BlockSpec((tm, d), lambda i, j: (i, 0))
    hid = pl.BlockSpec((None, tm, fs), lambda i, j: (j, i, 0))
    w_in_spec = pl.BlockSpec((None, d, fs), lambda i, j: (j, 0, 0))
    hid_shape = jax.ShapeDtypeStruct((n_chunks, s_len, fs), BF16)
    return pl.pallas_call(
        body, name=name, grid=(s_len // tm, n_chunks),
        in_specs=[tile, w_in_spec, w_in_spec, pl.BlockSpec((None, fs, d), lambda i, j: (j, 0, 0))],
        out_specs=[hid, hid, tile],
        out_shape=[hid_shape, hid_shape, jax.ShapeDtypeStruct((s_len, d), F32)],
        compiler_params=_params(2),
    )(h, w_gate, w_up, w_down)


def _ffn_bwd(dy, g_pre, u_pre, w_gate, w_up, w_down, name):
    s_len, d = dy.shape
    n_chunks, _, fs = w_gate.shape
    tm = FFN_TILE

    def body(dy_ref, g_ref, u_ref, wg_ref, wu_ref, wd_ref, dh_ref, dg_ref, du_ref, a_ref):
        j = pl.program_id(1)
        dyv = dy_ref[...]
        pieces = _pieces(fs)
        first = lambda cols: _dot_nt(dyv, wd_ref[cols, :])
        total = None
        ahead = first(pieces[0])
        for k, cols in enumerate(pieces):
            da = ahead
            if k + 1 < len(pieces):
                ahead = first(pieces[k + 1])
            g = g_ref[:, cols].astype(F32)
            u = u_ref[:, cols].astype(F32)
            sg = _sigmoid(g)
            silu = g * sg
            dg = (da * u * (sg * (1.0 + g * (1.0 - sg)))).astype(BF16)
            du = (da * silu).astype(BF16)
            dg_ref[:, cols] = dg
            du_ref[:, cols] = du
            a_ref[:, cols] = (silu * u).astype(BF16)
            part = _dot_nt(dg, wg_ref[:, cols]) + _dot_nt(du, wu_ref[:, cols])
            total = part if total is None else total + part

        @pl.when(j == 0)
        def _():
            dh_ref[...] = total

        @pl.when(j > 0)
        def _():
            dh_ref[...] += total

    tile = pl.BlockSpec((tm, d), lambda i, j: (i, 0))
    hid = pl.BlockSpec((None, tm, fs), lambda i, j: (j, i, 0))
    w_in_spec = pl.BlockSpec((None, d, fs), lambda i, j: (j, 0, 0))
    hid_shape = jax.ShapeDtypeStruct((n_chunks, s_len, fs), BF16)
    return pl.pallas_call(
        body, name=name, grid=(s_len // tm, n_chunks),
        in_specs=[tile, hid, hid, w_in_spec, w_in_spec, pl.BlockSpec((None, fs, d), lambda i, j: (j, 0, 0))],
        out_specs=[tile, hid, hid, hid],
        out_shape=[jax.ShapeDtypeStruct((s_len, d), F32), hid_shape, hid_shape, hid_shape],
        compiler_params=_params(2),
    )(dy, g_pre, u_pre, w_gate, w_up, w_down)


def _ffn_wgrads(ht, dg, du, act, dy, tag, after=()):
    n_chunks, s_len, fs = dg.shape
    d = ht.shape[0]
    tok = lambda ts: pl.BlockSpec((ts, d), lambda c, s: (s, 0))
    tok_t = lambda ts: pl.BlockSpec((d, ts), lambda c, s: (0, s))
    hid = lambda ts: pl.BlockSpec((None, ts, fs), lambda c, s: (c, s, 0))
    d_up = pl.BlockSpec((None, d, fs), lambda c, s: (c, 0, 0))
    d_down = pl.BlockSpec((None, fs, d), lambda c, s: (c, 0, 0))
    dwg = _wgrad(ht, dg, tok_t, hid, (n_chunks, d, fs), d_up, (d, fs), n_chunks, tag + "_dwg", True, after)
    dwu = _wgrad(ht, du, tok_t, hid, (n_chunks, d, fs), d_up, (d, fs), n_chunks, tag + "_dwu", True, after)
    dwd = _wgrad(act, dy, hid, tok, (n_chunks, fs, d), d_down, (fs, d), n_chunks, tag + "_dwd", False, after)
    return dwg, dwu, dwd


def _band_bias():
    qi = lax.broadcasted_iota(jnp.int32, (ATTN_BLOCK, 2 * ATTN_BLOCK), 0)
    kj = lax.broadcasted_iota(jnp.int32, (ATTN_BLOCK, 2 * ATTN_BLOCK), 1)
    band = (kj >= qi) & (kj <= qi + ATTN_BLOCK)
    return jnp.where(band, 0.0, NEG), jnp.where(band & (kj >= ATTN_BLOCK), 0.0, NEG)


def _qkv_specs(slab_of, sections):
    def spec(sect, back):
        return pl.BlockSpec((SLAB, HEAD_DIM),
                            lambda h, s, g: (jnp.maximum(slab_of(s) - back, 0), (sect * N_GROUPS + g) * HEADS + h))
    return [spec(sect, back) for sect, back in sections]


HAT_BLOCKS = [(0, 0), (1, 0), (2, 0), (1, 1), (2, 1)]


def _stage_keys(k_ref, v_ref, kp_ref, vp_ref, kbuf, vbuf, dil, n):
    run = SLAB // dil
    for r in range(dil):
        own, before = pl.ds(r * run, run), pl.ds(2 * r * run, run)
        kbuf[pl.ds((2 * r + 1) * run, run), :] = k_ref[own, :]
        vbuf[pl.ds((2 * r + 1) * run, run), :] = v_ref[own, :]

        @pl.when(n > 0)
        def _():
            kbuf[before, :] = kp_ref[own, :]
            vbuf[before, :] = vp_ref[own, :]

        @pl.when(n == 0)
        def _():
            kbuf[before, :] = jnp.zeros((run, HEAD_DIM), BF16)
            vbuf[before, :] = jnp.zeros((run, HEAD_DIM), BF16)


def _for_each_tile(dil, n, first_fn, rest_fn):
    run = SLAB // dil
    bias, first_bias = _band_bias()
    tiles = []
    for jj in range(run // ATTN_BLOCK):
        start = jj * ATTN_BLOCK
        tile_bias = jnp.where(n == 0, first_bias, bias) if jj == 0 else bias
        for r in range(dil):
            tiles.append((pl.ds(r * run + start, ATTN_BLOCK),
                          pl.ds((2 * r + 1) * run - ATTN_BLOCK + start, 2 * ATTN_BLOCK), tile_bias))
    ahead = first_fn(*tiles[0])
    for t, tile in enumerate(tiles):
        begun = ahead
        if t + 1 < len(tiles):
            ahead = first_fn(*tiles[t + 1])
        rest_fn(*tile, begun)


def _attn_fwd(hat, name):
    s_len = hat.shape[0]
    e = HEAD_DIM
    n_slabs = s_len // SLAB

    def body(q_ref, k_ref, v_ref, kp_ref, vp_ref, o_ref, lse_ref, kbuf, vbuf, m_s, l_s, acc_s, m_p, l_p, acc_p, tmp_s):
        n, grp = pl.program_id(1), pl.program_id(2)

        def run(gi, dil):
            _stage_keys(k_ref, v_ref, kp_ref, vp_ref, kbuf, vbuf, dil, n)

            def scores(q_rows, kv_rows, bias):
                return _dot_nt(q_ref[q_rows, :], kbuf[kv_rows, :])

            def rest(q_rows, kv_rows, bias, qk):
                s = qk * ATTN_SCALE + bias
                m = jnp.max(s, axis=-1, keepdims=True)
                p = jnp.exp(s - m)
                m_p[q_rows, :] = jnp.broadcast_to(m, (ATTN_BLOCK, e))
                l_p[q_rows, :] = jnp.broadcast_to(jnp.sum(p, axis=-1, keepdims=True), (ATTN_BLOCK, e))
                acc_p[q_rows, :] = _dot(p.astype(BF16), vbuf[kv_rows, :])

            _for_each_tile(dil, n, scores, rest)
            _to_token_order(m_s.at[gi], m_p, dil, tmp_s)
            _to_token_order(l_s.at[gi], l_p, dil, tmp_s)
            _to_token_order(acc_s.at[gi], acc_p, dil, tmp_s)

        for gi, dil in enumerate(DILATIONS):
            pl.when(grp == gi)(lambda gi=gi, dil=dil: run(gi, dil))

        @pl.when(grp == N_GROUPS - 1)
        def _():
            m_all = jnp.maximum(jnp.maximum(m_s[0], m_s[1]), m_s[2])
            den = jnp.zeros((SLAB, e), F32)
            num = jnp.zeros((SLAB, e), F32)
            for gi in range(N_GROUPS):
                w = jnp.exp(m_s[gi] - m_all)
                den += l_s[gi] * w
                num += acc_s[gi] * w
            o_ref[...] = num / den
            lse_ref[...] = m_all + jnp.log(den)

    out = pl.BlockSpec((SLAB, e), lambda h, n, g: (n, h))
    return pl.pallas_call(
        body, name=name, grid=(HEADS, n_slabs, N_GROUPS),
        in_specs=_qkv_specs(lambda n: n, HAT_BLOCKS),
        out_specs=[out, out],
        out_shape=[jax.ShapeDtypeStruct((s_len, HEADS * e), F32)] * 2,
        scratch_shapes=[pltpu.VMEM((2 * SLAB, e), BF16), pltpu.VMEM((2 * SLAB, e), BF16),
                        pltpu.VMEM((N_GROUPS, SLAB, e), F32), pltpu.VMEM((N_GROUPS, SLAB, e), F32),
                        pltpu.VMEM((N_GROUPS, SLAB, e), F32)]
        + [pltpu.VMEM((SLAB, e), F32)] * 4,
        compiler_params=_params(3),
    )(hat, hat, hat, hat, hat)


def _attn_bwd(qkv, hat, d_out, out, lse, q_norm, k_norm, name):
    s_len = qkv.shape[0]
    e = HEAD_DIM
    n_slabs = s_len // SLAB

    def body(q_ref, k_ref, v_ref, kp_ref, vp_ref, qraw_ref, kraw_ref, do_ref, o_ref, lse_ref, qn_ref, kn_ref,
             dq_ref, dk_ref, dv_ref, st_ref, kbuf, vbuf, stat_s, dqs, dkb, dvb, dk_tok, dv_tok, carry,
             do_p, stat_p, dq_p, dk_p, dv_p, tmp_s, do16_p):
        head, step, grp = pl.program_id(0), pl.program_id(1), pl.program_id(2)
        n = n_slabs - 1 - step
        dkb[...] = jnp.zeros_like(dkb)
        dvb[...] = jnp.zeros_like(dvb)
        @pl.when(grp == 0)
        def _():
            lane = lax.broadcasted_iota(jnp.int32, (SLAB, e), 1)
            stat_s[...] = jnp.where(lane < e // 2, lse_ref[...],
                                    jnp.sum(do_ref[...] * o_ref[...], axis=-1, keepdims=True))

        @pl.when((head == 0) & (step == 0) & (grp == 0))
        def _():
            st_ref[...] = jnp.zeros_like(st_ref)

        def run(gi, dil):
            seg = SLAB // dil
            _stage_keys(k_ref, v_ref, kp_ref, vp_ref, kbuf, vbuf, dil, n)

            @pl.when(step == 0)
            def _():
                carry[gi] = jnp.zeros((2, SLAB, e), F32)

            _to_residue_order(do_p, do_ref, dil, tmp_s)
            do16_p[...] = do_p[...].astype(BF16)
            _to_residue_order(stat_p, stat_s, dil, tmp_s)

            def scores(q_rows, kv_rows, bias):
                return _dot_nt(q_ref[q_rows, :], kbuf[kv_rows, :]), _dot_nt(do16_p[q_rows, :], vbuf[kv_rows, :])

            def rest(q_rows, kv_rows, bias, begun):
                qk, dp = begun
                q = q_ref[q_rows, :]
                k = kbuf[kv_rows, :]
                stat = stat_p[q_rows, :]
                p = jnp.exp(qk * ATTN_SCALE + bias - stat[:, 0:1])
                ds = (p * (dp - stat[:, e // 2:e // 2 + 1]) * ATTN_SCALE).astype(BF16)
                dq_p[q_rows, :] = _dot(ds, k)
                dkb[kv_rows, :] += _dot_tn(ds, q)
                dvb[kv_rows, :] += _dot_tn(p.astype(BF16), do16_p[q_rows, :])

            _for_each_tile(dil, n, scores, rest)
            for r in range(dil):
                own, before = pl.ds((2 * r + 1) * seg, seg), pl.ds(2 * r * seg, seg)
                kept = pl.ds(r * seg, seg)
                dk_p[kept, :] = dkb[own, :] + carry.at[gi, 0][kept, :]
                dv_p[kept, :] = dvb[own, :] + carry.at[gi, 1][kept, :]
                carry.at[gi, 0][kept, :] = dkb[before, :]
                carry.at[gi, 1][kept, :] = dvb[before, :]
            _to_token_order(dqs, dq_p, dil, tmp_s)
            _to_token_order(dk_tok, dk_p, dil, tmp_s)
            _to_token_order(dv_tok, dv_p, dil, tmp_s)

            def norm_bwd(raw, gain, d_hat):
                r = _rms(raw)
                y = raw * r
                dy = d_hat * gain
                return r * (dy - y * jnp.mean(dy * y, axis=-1, keepdims=True)), jnp.sum(d_hat * y, axis=0, keepdims=True)

            dq, dqn = norm_bwd(qraw_ref[...], qn_ref[...], dqs[...])
            dk, dkn = norm_bwd(kraw_ref[...], kn_ref[...], dk_tok[...])
            dq_ref[...] = dq.astype(BF16)
            dk_ref[...] = dk.astype(BF16)
            dv_ref[...] = dv_tok[...].astype(BF16)
            st_ref[0:1, :] += dqn
            st_ref[1:2, :] += dkn

        for gi, dil in enumerate(DILATIONS):
            pl.when(grp == gi)(lambda gi=gi, dil=dil: run(gi, dil))

    slab_of = lambda s: n_slabs - 1 - s
    small = pl.BlockSpec((1, e), lambda h, s, g: (0, 0))
    head_blk = pl.BlockSpec((SLAB, e), lambda h, s, g: (slab_of(s), h))
    grad_blk = pl.BlockSpec((SLAB, e), lambda h, s, g: (slab_of(s), g * HEADS + h))
    grad_shape = jax.ShapeDtypeStruct((s_len, QKV), BF16)
    return pl.pallas_call(
        body, name=name, grid=(HEADS, n_slabs, N_GROUPS),
        in_specs=(_qkv_specs(slab_of, HAT_BLOCKS) + _qkv_specs(slab_of, [(0, 0), (1, 0)])
                  + [head_blk, head_blk, head_blk, small, small]),
        out_specs=[grad_blk, grad_blk, grad_blk, pl.BlockSpec((8, e), lambda h, s, g: (0, 0))],
        out_shape=[grad_shape, grad_shape, grad_shape, jax.ShapeDtypeStruct((8, e), F32)],
        scratch_shapes=[pltpu.VMEM((2 * SLAB, e), BF16), pltpu.VMEM((2 * SLAB, e), BF16), pltpu.VMEM((SLAB, e), F32),
                        pltpu.VMEM((SLAB, e), F32), pltpu.VMEM((2 * SLAB, e), F32), pltpu.VMEM((2 * SLAB, e), F32),
                        pltpu.VMEM((SLAB, e), F32), pltpu.VMEM((SLAB, e), F32),
                        pltpu.VMEM((N_GROUPS, 2, SLAB, e), F32)]
        + [pltpu.VMEM((SLAB, e), F32)] * 6 + [pltpu.VMEM((SLAB, e), BF16)],
        compiler_params=_params(3),
    )(hat, hat, hat, hat, hat, qkv, qkv, d_out, out, lse, q_norm, k_norm)


def _shift_rows(x, by, edge, forward):
    t_len = x.shape[0]
    row = lax.broadcasted_iota(jnp.int32, x.shape, 0)
    if forward:
        out = pltpu.roll(x, by, 0)
        for i in range(by):
            out = jnp.where(row == i, edge[8 - by + i:8 - by + i + 1, :], out)
    else:
        out = pltpu.roll(x, t_len - by, 0)
        for i in range(by):
            out = jnp.where(row == t_len - by + i, edge[i:i + 1, :], out)
    return out


def _mix_fwd(x, o, rest, mod, conv_w, w_attn, w_conv, w_out, name):
    s_len, d = x.shape
    tm = MIX_TILE
    a_w = o.shape[1]

    def body(x_ref, o_ref, u_ref, b_ref, c_ref, ga_ref, gc_ref, mod_ref, cw_ref, wa_ref, wc_ref, wo_ref,
             xo_ref, z_ref, ya_ref, yc_ref, conv_ref, yb_ref, m_ref, o16_ref, carry):
        @pl.when(pl.program_id(0) == 0)
        def _():
            carry[...] = jnp.zeros_like(carry)

        xc = c_ref[...].astype(F32) * u_ref[...].astype(F32)
        edge = carry[...]
        conv = (_shift_rows(xc, 2, edge, True) * cw_ref[0:1, :] + _shift_rows(xc, 1, edge, True) * cw_ref[1:2, :]
                + xc * cw_ref[2:3, :])
        carry[...] = xc[tm - 8:tm, :]
        yb = (b_ref[...].astype(F32) * conv).astype(BF16)
        o16 = o_ref[...].astype(BF16)
        ya = _dot(o16, wa_ref[...])
        yc = _dot(yb, wc_ref[...])
        merged = (_sigmoid(ga_ref[...].astype(F32)) * ya + _sigmoid(gc_ref[...].astype(F32)) * yc).astype(BF16)
        z = _dot(merged, wo_ref[...])
        xo_ref[...] = x_ref[...] + mod_ref[2:3, :] * z
        z_ref[...] = z
        ya_ref[...] = ya.astype(BF16)
        yc_ref[...] = yc.astype(BF16)
        conv_ref[...] = conv.astype(BF16)
        yb_ref[...] = yb
        m_ref[...] = merged
        o16_ref[...] = o16

    tile = pl.BlockSpec((tm, d), lambda i: (i, 0))
    sect = lambda k: pl.BlockSpec((tm, d), lambda i: (i, k))
    att = pl.BlockSpec((tm, a_w), lambda i: (i, 0))
    const = lambda shape: pl.BlockSpec(shape, lambda i: (0, 0))
    f32_out = jax.ShapeDtypeStruct((s_len, d), F32)
    b16_out = jax.ShapeDtypeStruct((s_len, d), BF16)
    return pl.pallas_call(
        body, name=name, grid=(s_len // tm,),
        in_specs=[tile, att, sect(0), sect(1), sect(2), sect(3), sect(4), const((8, d)), const((8, d)),
                  const((a_w, d)), const((d, d)), const((d, d))],
        out_specs=[tile] * 7 + [att],
        out_shape=[f32_out, f32_out] + [b16_out] * 5 + [jax.ShapeDtypeStruct((s_len, a_w), BF16)],
        scratch_shapes=[pltpu.VMEM((8, d), F32)],
        compiler_params=_params(1),
    )(x, o, rest, rest, rest, rest, rest, mod, conv_w, w_attn, w_conv, w_out)


def _mix_bwd(dxo, ya, yc, conv, rest, mod, conv_w, w_attn, w_conv, w_out, a_w, name):
    s_len, d = dxo.shape
    tm = MIX_TILE
    n_tiles = s_len // tm

    def body(dxo_ref, ya_ref, yc_ref, conv_ref, u_ref, b_ref, c_ref, ga_ref, gc_ref, mod_ref, cw_ref,
             wa_ref, wc_ref, wo_ref, do_ref, drest_ref, dz_ref, dya_ref, dyc_ref, st_ref, carry):
        @pl.when(pl.program_id(0) == 0)
        def _():
            carry[...] = jnp.zeros_like(carry)
            st_ref[...] = jnp.zeros_like(st_ref)

        dz = (mod_ref[2:3, :] * dxo_ref[...]).astype(BF16)
        dz_ref[...] = dz
        dm = _dot_nt(dz, wo_ref[...])
        sa, sc = _sigmoid(ga_ref[...].astype(F32)), _sigmoid(gc_ref[...].astype(F32))
        dya = (dm * sa).astype(BF16)
        dyc = (dm * sc).astype(BF16)
        dya_ref[...] = dya
        dyc_ref[...] = dyc
        drest_ref[:, 3 * d:4 * d] = (dm * ya_ref[...].astype(F32) * (sa * (1.0 - sa))).astype(BF16)
        drest_ref[:, 4 * d:5 * d] = (dm * yc_ref[...].astype(F32) * (sc * (1.0 - sc))).astype(BF16)
        do_ref[...] = _dot_nt(dya, wa_ref[...])
        dyb = _dot_nt(dyc, wc_ref[...])
        drest_ref[:, d:2 * d] = (dyb * conv_ref[...].astype(F32)).astype(BF16)
        dconv = dyb * b_ref[...].astype(F32)
        edge = carry[...]
        sh1 = _shift_rows(dconv, 1, edge, False)
        sh2 = _shift_rows(dconv, 2, edge, False)
        carry[...] = dconv[0:8, :]
        dxc = dconv * cw_ref[2:3, :] + sh1 * cw_ref[1:2, :] + sh2 * cw_ref[0:1, :]
        u, c = u_ref[...].astype(F32), c_ref[...].astype(F32)
        xc = c * u
        drest_ref[:, 0:d] = (dxc * c).astype(BF16)
        drest_ref[:, 2 * d:3 * d] = (dxc * u).astype(BF16)
        st_ref[0:1, :] += jnp.sum(xc * sh2, axis=0, keepdims=True)
        st_ref[1:2, :] += jnp.sum(xc * sh1, axis=0, keepdims=True)
        st_ref[2:3, :] += jnp.sum(xc * dconv, axis=0, keepdims=True)

    rev = lambda i: n_tiles - 1 - i
    tile = pl.BlockSpec((tm, d), lambda i: (rev(i), 0))
    sect = lambda k: pl.BlockSpec((tm, d), lambda i: (rev(i), k))
    const = lambda shape: pl.BlockSpec(shape, lambda i: (0, 0))
    b16_out = jax.ShapeDtypeStruct((s_len, d), BF16)
    return pl.pallas_call(
        body, name=name, grid=(n_tiles,),
        in_specs=[tile, tile, tile, tile, sect(0), sect(1), sect(2), sect(3), sect(4), const((8, d)), const((8, d)),
                  const((a_w, d)), const((d, d)), const((d, d))],
        out_specs=[pl.BlockSpec((tm, a_w), lambda i: (rev(i), 0)), pl.BlockSpec((tm, 5 * d), lambda i: (rev(i), 0)),
                   tile, tile, tile, const((8, d))],
        out_shape=[jax.ShapeDtypeStruct((s_len, a_w), F32), jax.ShapeDtypeStruct((s_len, 5 * d), BF16),
                   b16_out, b16_out, b16_out, jax.ShapeDtypeStruct((8, d), F32)],
        scratch_shapes=[pltpu.VMEM((8, d), F32)],
        compiler_params=_params(1),
    )(dxo, ya, yc, conv, rest, rest, rest, rest, rest, mod, conv_w, w_attn, w_conv, w_out)


ADA_COLS = 128


def _ada_fwd(c_all, w_shard, b_shard, name):
    d, cols = w_shard.shape

    def body(c_ref, w_ref, b_ref, o_ref):
        cv = c_ref[...]
        o_ref[...] = jnp.dot(cv * _sigmoid(cv), w_ref[...], preferred_element_type=F32,
                             precision=lax.Precision.HIGHEST) + b_ref[...]

    return pl.pallas_call(
        body, name=name, grid=(cols // ADA_COLS,),
        in_specs=[pl.BlockSpec((8, d), lambda j: (0, 0)), pl.BlockSpec((d, ADA_COLS), lambda j: (0, j)),
                  pl.BlockSpec((1, ADA_COLS), lambda j: (0, j))],
        out_specs=pl.BlockSpec((8, ADA_COLS), lambda j: (0, j)),
        out_shape=jax.ShapeDtypeStruct((8, cols), F32),
        compiler_params=_params(1),
    )(c_all, w_shard, b_shard)


def _ada_bwd(c_all, dmod_shard, w, m, v, name):
    d, cols = w.shape

    def body(c_ref, dm_ref, w_ref, m_ref, v_ref, g_ref, d_ref, nm_ref, nv_ref):
        cv = c_ref[...]
        g = lax.dot_general(cv * _sigmoid(cv), dm_ref[...], (((0,), (0,)), ((), ())),
                            preferred_element_type=F32, precision=lax.Precision.HIGHEST)
        g_ref[...] = g
        d_ref[...], nm_ref[...], nv_ref[...] = _adamw_math(w_ref[...], g, m_ref[...], v_ref[...])

    blk = pl.BlockSpec((d, ADA_COLS), lambda j: (0, j))
    shape = jax.ShapeDtypeStruct((d, cols), F32)
    return pl.pallas_call(
        body, name=name, grid=(cols // ADA_COLS,),
        in_specs=[pl.BlockSpec((8, d), lambda j: (0, 0)), pl.BlockSpec((8, ADA_COLS), lambda j: (0, j)), blk, blk, blk],
        out_specs=[blk] * 4, out_shape=[shape] * 4,
        compiler_params=_params(1),
    )(c_all, dmod_shard, w, m, v)


def _small_update(parts, w, m, v, name):
    n = w.shape[1]

    def body(p_ref, w_ref, m_ref, v_ref, g_ref, d_ref, nm_ref, nv_ref):
        g = p_ref[0:1, :]
        for i in range(1, 8):
            g = g + p_ref[i:i + 1, :]
        g_ref[...] = g
        d_ref[...], nm_ref[...], nv_ref[...] = _adamw_math(w_ref[...], g, m_ref[...], v_ref[...])

    shape = jax.ShapeDtypeStruct((1, n), F32)
    return pl.pallas_call(body, name=name, out_shape=[shape] * 4, compiler_params=_params())(parts, w, m, v)


def _cols_to_shards(w, n):
    r, nc = w.shape
    return w.reshape(r, n, nc // n).transpose(1, 0, 2)


def kernel(x, c, w_ada, b_ada, norm_ffn1, ffn1_w_gate, ffn1_w_up, ffn1_w_down, norm_mix, w_in, q_norm, k_norm, conv_w, w_attn_branch, w_conv_branch, w_out, norm_ffn2, ffn2_w_gate, ffn2_w_up, ffn2_w_down, loss_target, m_w_ada, m_b_ada, m_norm_ffn1, m_ffn1_w_gate, m_ffn1_w_up, m_ffn1_w_down, m_norm_mix, m_w_in, m_q_norm, m_k_norm, m_conv_w, m_w_attn_branch, m_w_conv_branch, m_w_out, m_norm_ffn2, m_ffn2_w_gate, m_ffn2_w_up, m_ffn2_w_down, v_w_ada, v_b_ada, v_norm_ffn1, v_ffn1_w_gate, v_ffn1_w_up, v_ffn1_w_down, v_norm_mix, v_w_in, v_q_norm, v_k_norm, v_conv_w, v_w_attn_branch, v_w_conv_branch, v_w_out, v_norm_ffn2, v_ffn2_w_gate, v_ffn2_w_up, v_ffn2_w_down):
    ix, iy, ic = _place()
    chip = 2 * ix + iy
    me = 4 * ix + 2 * iy + ic
    xs = x[0]
    target = loss_target[0]
    s_len, d = xs.shape
    ada_cols = w_ada.shape[2]
    conv_cols = conv_w.shape[2]

    conv_rows = jnp.zeros((8, conv_cols), F32).at[0:3].set(conv_w[0])
    small_in = jnp.concatenate([jnp.broadcast_to(c, (8, d)), conv_rows], axis=1)
    small_all = _allgather8(small_in, "gather_c").reshape(8, 8, d + conv_cols)
    c_all = small_all[:, 0, :d]
    conv_full = small_all[0::2, 0:3, d:].transpose(1, 0, 2).reshape(3, N_CHIPS * conv_cols)
    conv_pad = jnp.zeros((8, N_CHIPS * conv_cols), F32).at[0:3].set(conv_full)
    b_shard = lax.dynamic_slice(b_ada, (0, chip * ada_cols), (1, ada_cols))
    mod_part = _ada_fwd(c_all, w_ada[0], b_shard, "ada_fwd")
    mod_all = _allgather8(mod_part, "gather_mod").reshape(N_CHIPS, 2, 8, ada_cols)[:, 0]
    mod_mine = lax.dynamic_slice(mod_all, (0, me, 0), (N_CHIPS, 1, ada_cols)).reshape(9, d)

    def mod_rows(i, gain):
        return jnp.zeros((8, d), F32).at[0:3].set(mod_mine[3 * i:3 * i + 3]).at[3:4].set(gain)

    mod1, mod2, mod3 = mod_rows(0, norm_ffn1), mod_rows(1, norm_mix), mod_rows(2, norm_ffn2)

    to16 = lambda w: w[0].astype(BF16)
    wg1, wu1, wd1 = _gather_weights([to16(ffn1_w_gate), to16(ffn1_w_up), to16(ffn1_w_down)], [False] * 3,
                                    "gather_ffn1", 1)
    h1, h1t = _norm_mod(xs, mod1, "norm1")
    (w_in_full,) = _gather_weights([to16(w_in)], [True], "gather_w_in", 2, after=(wd1, h1))

    g1, u1, y1 = _ffn_fwd(h1, wg1, wu1, wd1, "ffn1_fwd")
    x1, h2, h2t = _norm_mod(xs, mod2, "norm2", prev=(y1, mod1, 0.5))
    qkv, rest, qkv_hat = _in_proj(h2, w_in_full, q_norm, k_norm, "in_proj")
    w_ab, w_cb_g, w_o_g, wg2, wu2, wd2 = _gather_weights(
        [to16(w_attn_branch), to16(w_conv_branch), to16(w_out),
         to16(ffn2_w_gate), to16(ffn2_w_up), to16(ffn2_w_down)], [True] + [False] * 5,
        "gather_rest", 3, after=(h2,))
    a_w = w_ab.shape[0]
    w_cb = w_cb_g.reshape(d, d)
    w_o = w_o_g.reshape(d, d)
    o, lse = _attn_fwd(qkv_hat, "attn_fwd")
    x2, z, ya, yc, conv, yb, merged, o16 = _mix_fwd(x1, o, rest, mod2, conv_pad, w_ab, w_cb, w_o, "mix_fwd")
    h3, h3t = _norm_mod(x2, mod3, "norm3")
    g3, u3, y3 = _ffn_fwd(h3, wg2, wu2, wd2, "ffn2_fwd")
    dx3, dy3, loss_part = _loss_grad(x2, y3, mod3, target, "loss")
    loss = lax.psum(0.5 * jnp.sum(loss_part) / d, ("x", "y", "c"))

    c_idx = jnp.reshape(ic, (1,)).astype(jnp.int32)
    chip_idx = jnp.stack([chip, ic]).astype(jnp.int32)

    def reduce_start(grads, names, tag, collective_id):
        from_sibling = _rs_pair_exchange(grads, "rs_pair_" + tag)
        pair_sums = [_pair_add(g, r, c_idx, "pair_add_" + nm) for g, r, nm in zip(grads, from_sibling, names)]
        return pair_sums, _rs_chip_exchange(pair_sums, "rs_chips_" + tag, collective_id)

    def reduce_finish(pair_sums, from_chips, names, tag, after):
        totals = [_chip_add(p, r, chip_idx, "chip_add_" + nm, after)
                  for p, r, nm in zip(pair_sums, from_chips, names)]
        return dict(zip(names, _rs_share(totals, "rs_share_" + tag)))

    names_a = ["ffn2_w_gate", "ffn2_w_up", "ffn2_w_down"]
    names_b = ["w_in", "w_attn_branch", "w_conv_branch", "w_out"]
    names_c = ["ffn1_w_gate", "ffn1_w_up", "ffn1_w_down"]

    dh3, dg3, du3, a3 = _ffn_bwd(dy3, g3, u3, wg2, wu2, wd2, "ffn2_bwd")
    sums_a, chips_a = reduce_start(list(_ffn_wgrads(h3t, dg3, du3, a3, dy3, "ffn2")), names_a, "a", 4)
    dx2, st3 = _norm_bwd(dh3, x2, mod3, dx3, y3, 0.5, "norm3_bwd", after=tuple(sums_a))

    do, drest, dz, dya, dyc, st_conv = _mix_bwd(dx2, ya, yc, conv, rest, mod2, conv_pad, w_ab, w_cb, w_o, a_w, "mix_bwd")
    dq, dk, dv, st_qk = _attn_bwd(qkv, qkv_hat, do, o, lse, q_norm, k_norm, "attn_bwd")
    tok = lambda width: (lambda ts: pl.BlockSpec((ts, width), lambda cc, s: (s, 0)))
    colblk = lambda width: (lambda ts: pl.BlockSpec((ts, width), lambda cc, s: (s, cc)))
    tok_t = lambda ts: pl.BlockSpec((d, ts), lambda cc, s: (0, s))
    whole = pl.BlockSpec((d, QKV), lambda cc, s: (0, 0))
    dw_in = [_wgrad(h2t, part, tok_t, tok(QKV), (d, QKV), whole, (d, QKV), 1, "dw_in_" + nm, True)
             for part, nm in ((dq, "q"), (dk, "k"), (dv, "v"))]
    dw_in.append(_wgrad(h2t, drest, tok_t, colblk(d), (d, 5 * d), pl.BlockSpec((d, d), lambda cc, s: (0, cc)),
                        (d, d), 5, "dw_in_rest", True))
    dw_in = _cols_to_shards(jnp.concatenate(dw_in, axis=1), N_CHIPS)
    shard_w = d // N_CHIPS
    dw_ab = _wgrad(o16, dya, tok(a_w), colblk(shard_w), (a_w, d), pl.BlockSpec((a_w, shard_w), lambda cc, s: (0, cc)),
                   (a_w, shard_w), N_CHIPS, "dw_attn_branch")
    dw_ab = _cols_to_shards(dw_ab, N_CHIPS)
    row_out = pl.BlockSpec((None, shard_w, d), lambda cc, s: (cc, 0, 0))
    dw_cb = _wgrad(yb, dyc, colblk(shard_w), tok(d), (N_CHIPS, shard_w, d), row_out, (shard_w, d), N_CHIPS, "dw_conv_branch")
    dw_o = _wgrad(merged, dz, colblk(shard_w), tok(d), (N_CHIPS, shard_w, d), row_out, (shard_w, d), N_CHIPS, "dw_out")
    shard_grads = reduce_finish(sums_a, chips_a, names_a, "a", after=(dw_in, dw_o))
    sums_b, chips_b = reduce_start([dw_in, dw_ab, dw_cb, dw_o], names_b, "b", 5)

    dh2 = _in_proj_bwd(dq, dk, dv, drest, w_in_full, "in_proj_bwd", after=tuple(sums_b))
    dx1, st2, dy1 = _norm_bwd(dh2, x1, mod2, dx2, z, 1.0, "norm2_bwd", prev=(mod1, 0.5))
    dh1, dg1, du1, a1 = _ffn_bwd(dy1, g1, u1, wg1, wu1, wd1, "ffn1_bwd")
    dx0, st1 = _norm_bwd(dh1, xs, mod1, dx1, y1, 0.5, "norm1_bwd")
    grads_c = list(_ffn_wgrads(h1t, dg1, du1, a1, dy1, "ffn1"))
    shard_grads.update(reduce_finish(sums_b, chips_b, names_b, "b", after=tuple(grads_c)))
    sums_c, chips_c = reduce_start(grads_c, names_c, "c", 6)

    dmod = jnp.concatenate([st1[0:3], st2[0:3], st3[0:3]], axis=0).reshape(1, 9 * d)
    small = jnp.concatenate([dmod, st1[3:4], st2[3:4], st3[3:4], st_qk[0:1], st_qk[1:2],
                             st_conv[0:3].reshape(1, 3 * d)], axis=1)
    small_all = _allgather8(jnp.broadcast_to(small, (8, small.shape[1])), "gather_small").reshape(8, 8, -1)[:, 0]
    dmod_all = small_all[:, :9 * d]
    dmod_shard = lax.dynamic_slice(dmod_all, (0, chip * ada_cols), (8, ada_cols))
    g_w_ada, d_w_ada, nm_w_ada, nv_w_ada = _ada_bwd(c_all, dmod_shard, w_ada[0], m_w_ada[0], v_w_ada[0], "ada_bwd")

    vec_names = ["b_ada", "norm_ffn1", "norm_mix", "norm_ffn2", "q_norm", "k_norm"]
    vec_w = [b_ada, norm_ffn1, norm_mix, norm_ffn2, q_norm, k_norm]
    vec_m = [m_b_ada, m_norm_ffn1, m_norm_mix, m_norm_ffn2, m_q_norm, m_k_norm]
    vec_v = [v_b_ada, v_norm_ffn1, v_norm_mix, v_norm_ffn2, v_q_norm, v_k_norm]
    n_vec = sum(w.shape[1] for w in vec_w)
    cat = lambda arrs: jnp.concatenate(arrs, axis=1)
    vec_out = _small_update(small_all[:, :n_vec], cat(vec_w), cat(vec_m), cat(vec_v), "small_update")
    conv_parts = small_all[:, n_vec:].reshape(8, 3, N_CHIPS * conv_cols)
    conv_parts = lax.dynamic_slice(conv_parts, (0, 0, chip * conv_cols), (8, 3, conv_cols)).reshape(8, 3 * conv_cols)
    flat3 = lambda w: w[0].reshape(1, 3 * conv_cols)
    conv_out = _small_update(conv_parts, flat3(conv_w), flat3(m_conv_w), flat3(v_conv_w), "conv_update")

    res = {"w_ada": [t[None] for t in (g_w_ada, d_w_ada, nm_w_ada, nv_w_ada)],
           "conv_w": [t.reshape(1, 3, conv_cols) for t in conv_out]}
    off = 0
    for nm, w in zip(vec_names, vec_w):
        width = w.shape[1]
        res[nm] = [t[:, off:off + width] for t in vec_out]
        off += width
    big = {"ffn1_w_gate": (ffn1_w_gate, m_ffn1_w_gate, v_ffn1_w_gate), "ffn1_w_up": (ffn1_w_up, m_ffn1_w_up, v_ffn1_w_up),
           "ffn1_w_down": (ffn1_w_down, m_ffn1_w_down, v_ffn1_w_down), "w_in": (w_in, m_w_in, v_w_in),
           "w_attn_branch": (w_attn_branch, m_w_attn_branch, v_w_attn_branch),
           "w_conv_branch": (w_conv_branch, m_w_conv_branch, v_w_conv_branch), "w_out": (w_out, m_w_out, v_w_out),
           "ffn2_w_gate": (ffn2_w_gate, m_ffn2_w_gate, v_ffn2_w_gate), "ffn2_w_up": (ffn2_w_up, m_ffn2_w_up, v_ffn2_w_up),
           "ffn2_w_down": (ffn2_w_down, m_ffn2_w_down, v_ffn2_w_down)}
    def update(nm, after=()):
        w, m, v = big[nm]
        g, delta, new_m, new_v = _adamw(w[0], shard_grads[nm], m[0], v[0], "adamw_" + nm, after)
        res[nm] = [t[None] for t in (g, delta, new_m, new_v)]
        return new_v

    last = tuple(sums_c)
    for nm in names_a + names_b:
        last = (update(nm, last),)
    shard_grads.update(reduce_finish(sums_c, chips_c, names_c, "c", after=last))
    for nm in names_c:
        update(nm)

    order = ["w_ada", "b_ada", "norm_ffn1", "ffn1_w_gate", "ffn1_w_up", "ffn1_w_down", "norm_mix", "w_in", "q_norm",
             "k_norm", "conv_w", "w_attn_branch", "w_conv_branch", "w_out", "norm_ffn2", "ffn2_w_gate", "ffn2_w_up",
             "ffn2_w_down"]
    return (loss, dx0[None], *[res[nm][0] for nm in order], *[res[nm][1] for nm in order],
            *[res[nm][2] for nm in order], *[res[nm][3] for nm in order])
```

```python
import jax
import jax.numpy as jnp
from jax import lax
from jax.experimental import pallas as pl
from jax.experimental.pallas import tpu as pltpu
from jax.experimental.pallas import tpu_sc as plsc

F32 = jnp.float32
BF16 = jnp.bfloat16
MESH = pl.DeviceIdType.MESH
ANY = pl.BlockSpec(memory_space=pl.ANY)

NORM_EPS = 1e-6
HEAD_DIM = 128
N_GROUPS = 3
HEADS = 4
DILATIONS = (1, 4, 16)
ATTN_BLOCK = 128
SLAB = ATTN_BLOCK * max(DILATIONS)
QKV = N_GROUPS * HEADS * HEAD_DIM
ATTN_SCALE = HEAD_DIM ** -0.5
NEG = -1e30
N_CHIPS = 4

ADAM_LR = 0.001
ADAM_B1 = 0.9
ADAM_B2 = 0.999
ADAM_EPS = 1e-08
ADAM_WD = 0.01
ADAM_STEP = 10

VMEM_LIMIT_BYTES = 56 * 1024 * 1024
TOKEN_TILE = 512
FFN_TILE = 1024
PROJ_TILE = 2048
WGRAD_TILE = 2048
IN_BLOCK = 512
MIX_TILE = 256


def _params(n_axes=0):
    return pltpu.CompilerParams(
        dimension_semantics=("arbitrary",) * n_axes if n_axes else None,
        vmem_limit_bytes=VMEM_LIMIT_BYTES)


def _dot(a, b):
    return jnp.dot(a, b, preferred_element_type=F32)


def _dot_nt(a, b):
    return lax.dot_general(a, b, (((1,), (1,)), ((), ())), preferred_element_type=F32)


def _dot_tn(a, b):
    return lax.dot_general(a, b, (((0,), (0,)), ((), ())), preferred_element_type=F32)


def _sigmoid(x):
    return 1.0 / (1.0 + jnp.exp(-x))


def _place():
    return lax.axis_index("x"), lax.axis_index("y"), lax.axis_index("c")


def _ordered(body, n_in, after):
    if not after:
        return body
    return lambda *refs: body(*refs[:n_in], *refs[n_in + len(after):])


def _allgather8(block, name):
    m_per, n = block.shape

    def body(x_ref, out_ref, send_sems, recv_sems, local_sem):
        x, y, c = _place()
        me, sibling = (x, y, c), (x, y, 1 - c)
        chips = [(1 - x, y), (x, 1 - y), (1 - x, 1 - y)]

        def rows(px, py, pc):
            return out_ref.at[pl.ds((4 * px + 2 * py + pc) * m_per, m_per), :]

        def copy(k, blk, to, src=None):
            return pltpu.make_async_remote_copy(
                src_ref=rows(*blk) if src is None else src, dst_ref=rows(*blk),
                send_sem=send_sems.at[k], recv_sem=recv_sems.at[k],
                device_id=to, device_id_type=MESH)

        mine = pltpu.make_async_copy(x_ref, rows(*me), local_sem)
        mine.start()
        first = [copy(0, me, sibling, src=x_ref)]
        first += [copy(1 + j, me, (*chip, c), src=x_ref) for j, chip in enumerate(chips)]
        for cp in first:
            cp.start()
        passed = [copy(4 + j, (*chip, c), sibling) for j, chip in enumerate(chips)]
        for j, chip in enumerate(chips):
            copy(1 + j, (*chip, c), me).wait_recv()
            passed[j].start()
        copy(0, sibling, me).wait_recv()
        for j, chip in enumerate(chips):
            copy(4 + j, (*chip, 1 - c), me).wait_recv()
        for cp in first + passed:
            cp.wait_send()
        mine.wait()

    return pl.pallas_call(
        body, name=name,
        out_shape=jax.ShapeDtypeStruct((8 * m_per, n), block.dtype),
        in_specs=[pl.BlockSpec(memory_space=pltpu.VMEM)],
        out_specs=pl.BlockSpec(memory_space=pltpu.VMEM),
        scratch_shapes=[pltpu.SemaphoreType.DMA((7,)), pltpu.SemaphoreType.DMA((7,)),
                        pltpu.SemaphoreType.DMA],
        compiler_params=_params(),
    )(block)


def _handshake(peers):
    barrier = pltpu.get_barrier_semaphore()
    for peer in peers:
        pl.semaphore_signal(barrier, inc=1, device_id=peer, device_id_type=MESH)
    pl.semaphore_wait(barrier, len(peers))


def _gather_weights(shards, by_cols, name, collective_id, after=()):
    n_arr = len(shards)

    def body(*refs):
        srcs, outs = refs[:n_arr], refs[n_arr + len(after):2 * n_arr + len(after)]
        send_sems, recv_sems, local_sems = refs[2 * n_arr + len(after):]
        x, y, c = _place()
        me_dev, sibling = (x, y, c), (x, y, 1 - c)
        chips = [(1 - x, y), (x, 1 - y), (1 - x, 1 - y)]
        me = 2 * x + y
        _handshake([sibling] + [(*chip, c) for chip in chips])

        def place(k, chip_idx, rows):
            if by_cols[k]:
                width = srcs[k].shape[1]
                return outs[k].at[rows, pl.ds(pl.multiple_of(chip_idx * width, 128), width)]
            return outs[k].at[chip_idx, rows]

        def copy(k, slot, chip_idx, half_sel, to, from_shard=False):
            half = srcs[k].shape[0] // 2
            rows = pl.ds(half_sel * half, half)
            dst = place(k, chip_idx, rows)
            return pltpu.make_async_remote_copy(
                src_ref=srcs[k].at[rows] if from_shard else dst, dst_ref=dst,
                send_sem=send_sems.at[6 * k + slot], recv_sem=recv_sems.at[6 * k + slot],
                device_id=to, device_id_type=MESH)

        own = [pltpu.make_async_copy(srcs[k], place(k, me, pl.ds(0, srcs[k].shape[0])), local_sems.at[k])
               for k in range(n_arr)]
        for cp in own:
            cp.start()
        sent = []
        for k in range(n_arr):
            for j, chip in enumerate(chips):
                sent.append(copy(k, j, me, c, (*chip, c), from_shard=True))
                sent[-1].start()
        for k in range(n_arr):
            for j, chip in enumerate(chips):
                chip_idx = 2 * chip[0] + chip[1]
                copy(k, j, chip_idx, c, me_dev).wait_recv()
                sent.append(copy(k, 3 + j, chip_idx, c, sibling))
                sent[-1].start()
        for k in range(n_arr):
            for j, chip in enumerate(chips):
                copy(k, 3 + j, 2 * chip[0] + chip[1], 1 - c, me_dev).wait_recv()
        for cp in sent:
            cp.wait_send()
        for cp in own:
            cp.wait()

    def gathered(k):
        r, cols = shards[k].shape
        return (r, N_CHIPS * cols) if by_cols[k] else (N_CHIPS, r, cols)

    return pl.kernel(
        body, name=name,
        out_type=[jax.ShapeDtypeStruct(gathered(k), shards[k].dtype) for k in range(n_arr)],
        mesh=plsc.ScalarSubcoreMesh(axis_name="sequencer", num_cores=1),
        scratch_types=[pltpu.SemaphoreType.DMA((6 * n_arr,)), pltpu.SemaphoreType.DMA((6 * n_arr,)),
                       pltpu.SemaphoreType.DMA((n_arr,))],
        compiler_params=pltpu.CompilerParams(collective_id=collective_id),
    )(*shards, *after)


def _rs_pair_exchange(grads, name):
    n_arr = len(grads)

    def body(*refs):
        srcs, outs = refs[:n_arr], refs[n_arr:2 * n_arr]
        send_sems, recv_sems = refs[2 * n_arr:]
        x, y, c = _place()
        cps = []
        for k in range(n_arr):
            half = srcs[k].shape[1] // 2
            cps.append(pltpu.make_async_remote_copy(
                src_ref=srcs[k].at[:, pl.ds((1 - c) * half, half)], dst_ref=outs[k],
                send_sem=send_sems.at[k], recv_sem=recv_sems.at[k],
                device_id=(x, y, 1 - c), device_id_type=MESH))
            cps[-1].start()
        for cp in cps:
            cp.wait_recv()
        for cp in cps:
            cp.wait_send()

    return pl.pallas_call(
        body, name=name,
        out_shape=[jax.ShapeDtypeStruct((g.shape[0], g.shape[1] // 2, g.shape[2]), g.dtype) for g in grads],
        in_specs=[ANY] * n_arr, out_specs=[ANY] * n_arr,
        scratch_shapes=[pltpu.SemaphoreType.DMA((n_arr,)), pltpu.SemaphoreType.DMA((n_arr,))],
        compiler_params=_params(),
    )(*grads)


def _rs_chip_exchange(sums, name, collective_id):
    n_arr = len(sums)

    def body(*refs):
        srcs, outs = refs[:n_arr], refs[n_arr:2 * n_arr]
        send_sems, recv_sems = refs[2 * n_arr:]
        x, y, c = _place()
        chips = [(1 - x, y), (x, 1 - y), (1 - x, 1 - y)]
        _handshake([(*chip, c) for chip in chips])
        cps = []
        for k in range(n_arr):
            for j, chip in enumerate(chips):
                cps.append(pltpu.make_async_remote_copy(
                    src_ref=srcs[k].at[2 * chip[0] + chip[1]], dst_ref=outs[k].at[j],
                    send_sem=send_sems.at[3 * k + j], recv_sem=recv_sems.at[3 * k + j],
                    device_id=(*chip, c), device_id_type=MESH))
                cps[-1].start()
        for cp in cps:
            cp.wait_recv()
        for cp in cps:
            cp.wait_send()

    return pl.kernel(
        body, name=name,
        out_type=[jax.ShapeDtypeStruct((3,) + s.shape[1:], s.dtype) for s in sums],
        mesh=plsc.ScalarSubcoreMesh(axis_name="sequencer", num_cores=1),
        scratch_types=[pltpu.SemaphoreType.DMA((3 * n_arr,)), pltpu.SemaphoreType.DMA((3 * n_arr,))],
        compiler_params=pltpu.CompilerParams(collective_id=collective_id),
    )(*sums)


def _rs_share(totals, name):
    n_arr = len(totals)

    def body(*refs):
        outs = refs[n_arr:2 * n_arr]
        send_sems, recv_sems = refs[2 * n_arr:]
        x, y, c = _place()

        def half_rows(k, sel):
            return outs[k].at[sel]

        cps = []
        for k in range(n_arr):
            cps.append(pltpu.make_async_remote_copy(
                src_ref=half_rows(k, c), dst_ref=half_rows(k, c), send_sem=send_sems.at[k], recv_sem=recv_sems.at[k],
                device_id=(x, y, 1 - c), device_id_type=MESH))
            cps[-1].start()
        for k in range(n_arr):
            pltpu.make_async_remote_copy(
                src_ref=half_rows(k, c), dst_ref=half_rows(k, 1 - c), send_sem=send_sems.at[k],
                recv_sem=recv_sems.at[k], device_id=(x, y, 1 - c), device_id_type=MESH).wait_recv()
        for cp in cps:
            cp.wait_send()

    shared = pl.pallas_call(
        body, name=name,
        out_shape=[jax.ShapeDtypeStruct(t.shape, t.dtype) for t in totals],
        in_specs=[ANY] * n_arr, out_specs=[ANY] * n_arr,
        input_output_aliases={k: k for k in range(n_arr)},
        scratch_shapes=[pltpu.SemaphoreType.DMA((n_arr,)), pltpu.SemaphoreType.DMA((n_arr,))],
        compiler_params=_params(),
    )(*totals)
    return [t.reshape(2 * t.shape[1], t.shape[2]) for t in shared]


def _pair_add(grad, recv, c_idx, name):
    n, r, cols = grad.shape
    half = r // 2
    rows = half // 2

    def body(_, g_ref, r_ref, o_ref):
        o_ref[...] = (g_ref[...].astype(F32) + r_ref[...].astype(F32)).astype(o_ref.dtype)

    return pl.pallas_call(
        body, name=name,
        grid_spec=pltpu.PrefetchScalarGridSpec(
            num_scalar_prefetch=1, grid=(n, 2),
            in_specs=[pl.BlockSpec((None, None, rows, cols), lambda s, i, ci: (s, ci[0], i, 0)),
                      pl.BlockSpec((None, rows, cols), lambda s, i, ci: (s, i, 0))],
            out_specs=pl.BlockSpec((None, rows, cols), lambda s, i, ci: (s, i, 0))),
        out_shape=jax.ShapeDtypeStruct((n, half, cols), BF16),
        compiler_params=_params(2),
    )(c_idx, grad.reshape(n, 2, half, cols), recv)


def _chip_add(sums, recv, chip_and_core, name, after=()):
    _, half, cols = sums.shape
    rows = half // 2

    def body(_, s_ref, r0_ref, r1_ref, r2_ref, o_ref):
        o_ref[...] = ((s_ref[...].astype(F32) + r0_ref[...].astype(F32))
                      + r1_ref[...].astype(F32)) + r2_ref[...].astype(F32)

    def recv_spec(j):
        return pl.BlockSpec((None, rows, cols), lambda i, ci: (j, i, 0))

    return pl.pallas_call(
        _ordered(body, 5, after), name=name,
        grid_spec=pltpu.PrefetchScalarGridSpec(
            num_scalar_prefetch=1, grid=(2,),
            in_specs=[pl.BlockSpec((None, rows, cols), lambda i, ci: (ci[0], i, 0)),
                      recv_spec(0), recv_spec(1), recv_spec(2)] + [ANY] * len(after),
            out_specs=pl.BlockSpec((None, rows, cols), lambda i, ci: (ci[1], i, 0))),
        out_shape=jax.ShapeDtypeStruct((2, half, cols), F32),
        compiler_params=_params(1),
    )(chip_and_core, sums, recv, recv, recv, *after)


def _rms(x):
    return lax.rsqrt(jnp.mean(x * x, axis=-1, keepdims=True) + NORM_EPS)


def _norm_mod(x, mod, name, prev=None):
    s_len, d = x.shape
    tm = TOKEN_TILE

    def body(*refs):
        if prev is None:
            x_ref, mod_ref, h_ref, ht_ref = refs
            xv = x_ref[...]
        else:
            x_ref, y_ref, modp_ref, mod_ref, xo_ref, h_ref, ht_ref = refs
            xv = x_ref[...] + prev[2] * modp_ref[2:3, :] * y_ref[...]
            xo_ref[...] = xv
        n = (xv * _rms(xv)) * mod_ref[3:4, :]
        h = n * (1.0 + mod_ref[1:2, :]) + mod_ref[0:1, :]
        h_ref[...] = h.astype(BF16)
        ht_ref[...] = h.T.astype(BF16)

    tile = pl.BlockSpec((tm, d), lambda i: (i, 0))
    small = pl.BlockSpec((8, d), lambda i: (0, 0))
    h_specs = [tile, pl.BlockSpec((d, tm), lambda i: (0, i))]
    h_shapes = [jax.ShapeDtypeStruct((s_len, d), BF16), jax.ShapeDtypeStruct((d, s_len), BF16)]
    if prev is None:
        return pl.pallas_call(
            body, name=name, grid=(s_len // tm,), in_specs=[tile, small], out_specs=h_specs, out_shape=h_shapes,
            compiler_params=_params(1))(x, mod)
    return pl.pallas_call(
        body, name=name, grid=(s_len // tm,), in_specs=[tile, tile, small, small],
        out_specs=[tile] + h_specs, out_shape=[jax.ShapeDtypeStruct((s_len, d), F32)] + h_shapes,
        compiler_params=_params(1))(x, prev[0], prev[1], mod)


def _norm_bwd(dh, x, mod, dxo, y_raw, coef, name, after=(), prev=None):
    s_len, d = x.shape
    tm = TOKEN_TILE

    def body(*refs):
        if prev is None:
            dh_ref, x_ref, mod_ref, dxo_ref, y_ref, dx_ref, st_ref = refs
        else:
            dh_ref, x_ref, mod_ref, dxo_ref, y_ref, modp_ref, dx_ref, st_ref, dyp_ref = refs

        @pl.when(pl.program_id(0) == 0)
        def _():
            st_ref[...] = jnp.zeros_like(st_ref)

        xv, dhv, dxov = x_ref[...], dh_ref[...], dxo_ref[...]
        r = _rms(xv)
        xh = xv * r
        gain, scale = mod_ref[3:4, :], mod_ref[1:2, :]
        dn = dhv * (1.0 + scale)
        dxh = dn * gain
        dx = dxov + r * (dxh - xh * jnp.mean(dxh * xh, axis=-1, keepdims=True))
        dx_ref[...] = dx
        if prev is not None:
            dyp_ref[...] = (prev[1] * modp_ref[2:3, :] * dx).astype(BF16)
        st_ref[0:1, :] += jnp.sum(dhv, axis=0, keepdims=True)
        st_ref[1:2, :] += jnp.sum(dhv * (xh * gain), axis=0, keepdims=True)
        st_ref[2:3, :] += coef * jnp.sum(y_ref[...] * dxov, axis=0, keepdims=True)
        st_ref[3:4, :] += jnp.sum(dn * xh, axis=0, keepdims=True)

    tile = pl.BlockSpec((tm, d), lambda i: (i, 0))
    small = pl.BlockSpec((8, d), lambda i: (0, 0))
    operands = [dh, x, mod, dxo, y_raw] + ([] if prev is None else [prev[0]])
    in_specs = [tile, tile, small, tile, tile] + ([] if prev is None else [small])
    out_specs = [tile, small] + ([] if prev is None else [tile])
    out_shape = [jax.ShapeDtypeStruct((s_len, d), F32), jax.ShapeDtypeStruct((8, d), F32)]
    if prev is not None:
        out_shape.append(jax.ShapeDtypeStruct((s_len, d), BF16))
    return pl.pallas_call(
        _ordered(body, len(operands), after), name=name, grid=(s_len // tm,),
        in_specs=in_specs + [ANY] * len(after), out_specs=out_specs, out_shape=out_shape,
        compiler_params=_params(1),
    )(*operands, *after)


def _loss_grad(x, y, mod, target, name):
    s_len, d = x.shape
    tm = TOKEN_TILE

    def body(x_ref, y_ref, mod_ref, t_ref, do_ref, dy_ref, part_ref):
        @pl.when(pl.program_id(0) == 0)
        def _():
            part_ref[...] = jnp.zeros_like(part_ref)

        half_gate = 0.5 * mod_ref[2:3, :]
        err = (x_ref[...] + half_gate * y_ref[...]) - t_ref[...]
        do = err * (1.0 / d)
        do_ref[...] = do
        dy_ref[...] = (half_gate * do).astype(BF16)
        sq = err * err
        part_ref[...] += jnp.sum(sq.reshape(tm // 8, 8, d), axis=0)

    tile = pl.BlockSpec((tm, d), lambda i: (i, 0))
    small = pl.BlockSpec((8, d), lambda i: (0, 0))
    return pl.pallas_call(
        body, name=name, grid=(s_len // tm,),
        in_specs=[tile, tile, small, tile],
        out_specs=[tile, tile, small],
        out_shape=[jax.ShapeDtypeStruct((s_len, d), F32), jax.ShapeDtypeStruct((s_len, d), BF16),
                   jax.ShapeDtypeStruct((8, d), F32)],
        compiler_params=_params(1),
    )(x, y, mod, target)


def _adamw_math(w, g, m, v):
    m = ADAM_B1 * m + (1.0 - ADAM_B1) * g
    v = ADAM_B2 * v + (1.0 - ADAM_B2) * (g * g)
    m_hat = m / (1.0 - ADAM_B1 ** ADAM_STEP)
    v_hat = v / (1.0 - ADAM_B2 ** ADAM_STEP)
    delta = -ADAM_LR * (m_hat / (jnp.sqrt(v_hat) + ADAM_EPS) + ADAM_WD * w)
    return delta, m, v


def _adamw(w, g, m, v, name, after=()):
    r, cols = w.shape
    tr = r // 8 if r % 64 == 0 else r

    def body(w_ref, g_ref, m_ref, v_ref, go_ref, d_ref, nm_ref, nv_ref):
        gv = g_ref[...]
        go_ref[...] = gv
        d_ref[...], nm_ref[...], nv_ref[...] = _adamw_math(w_ref[...], gv, m_ref[...], v_ref[...])

    tile = pl.BlockSpec((tr, cols), lambda i: (i, 0))
    shape = jax.ShapeDtypeStruct((r, cols), F32)
    return pl.pallas_call(
        _ordered(body, 4, after), name=name, grid=(r // tr,),
        in_specs=[tile] * 4 + [ANY] * len(after), out_specs=[tile] * 4, out_shape=[shape] * 4,
        compiler_params=_params(1),
    )(w, g, m, v, *after)


def _in_parts(tm, n_qkv, n_rest):
    def part(lo, n_blk):
        return pl.BlockSpec((tm, IN_BLOCK), lambda i, j: (i, jnp.clip(j - lo, 0, n_blk - 1)))
    return [part(0, n_qkv), part(n_qkv, n_qkv), part(2 * n_qkv, n_qkv), part(3 * n_qkv, n_rest)]


def _pick_part(j, n_qkv, refs, fn):
    bounds = [0, n_qkv, 2 * n_qkv, 3 * n_qkv]
    for p, ref in enumerate(refs):
        inside = j >= bounds[p]
        if p + 1 < len(refs):
            inside = inside & (j < bounds[p + 1])
        pl.when(inside)(lambda ref=ref: fn(ref))


def _rows(base, count, stride):
    return pl.ds(base, count) if stride == 1 else pl.ds(base, count, stride=stride)


REORDER_STRIDE = 4


def _reorder_plan(dil, parts=1):
    inner = min(dil, REORDER_STRIDE)
    return inner, dil // inner, SLAB // parts // inner, SLAB // dil


def _to_residue_order(dst, src, dil, tmp, part=0, parts=1):
    inner, outer, big, seg = _reorder_plan(dil, parts)
    piece = seg // parts
    if outer == 1:
        for r in range(dil):
            dst[pl.ds(r * seg + part * piece, piece), :] = src[_rows(r, piece, dil), :].astype(dst.dtype)
        return
    for b in range(inner):
        tmp[pl.ds(b * big, big), :] = src[_rows(b, big, inner), :]
    for a in range(outer):
        for b in range(inner):
            dst[pl.ds((inner * a + b) * seg + part * piece, piece), :] = (
                tmp[_rows(b * big + a, piece, outer), :].astype(dst.dtype))


def _to_token_order(dst, src, dil, tmp):
    inner, outer, big, seg = _reorder_plan(dil)
    if outer == 1:
        for r in range(dil):
            dst[_rows(r, seg, dil), :] = src[pl.ds(r * seg, seg), :]
        return
    for a in range(outer):
        for b in range(inner):
            tmp[_rows(b * big + a, seg, outer), :] = src[pl.ds((inner * a + b) * seg, seg), :]
    for b in range(inner):
        dst[_rows(b, big, inner), :] = tmp[pl.ds(b * big, big), :]


def _in_proj(h, w, q_norm, k_norm, name):
    s_len, d = h.shape
    tm = PROJ_TILE
    assert tm == SLAB and IN_BLOCK == HEADS * HEAD_DIM
    steps = w.shape[1] // IN_BLOCK
    n_qkv = 3 * QKV // IN_BLOCK
    halves = 2
    rows = [pl.ds(p * (tm // halves), tm // halves) for p in range(halves)]

    def body(h_ref, w_ref, qn_ref, kn_ref, qkv_ref, rest_ref, hat_ref, tok_s, tmp_s):
        j = pl.program_id(1)
        res = [_dot(h_ref[rows[p], :], w_ref[...]) for p in range(halves)]

        def emit(sect, gi):
            dil = DILATIONS[gi]
            for p in range(halves):
                qkv_ref[rows[p], :] = res[p]
                for hh in range(HEADS):
                    cols = slice(hh * HEAD_DIM, (hh + 1) * HEAD_DIM)
                    x = res[p][:, cols]
                    if sect < 2:
                        x = (x * _rms(x)) * (qn_ref if sect == 0 else kn_ref)[...]
                    tok_s[...] = x
                    _to_residue_order(hat_ref.at[:, cols], tok_s, dil, tmp_s, p, halves)

        for sect in range(3):
            for gi in range(N_GROUPS):
                pl.when(j == sect * N_GROUPS + gi)(lambda sect=sect, gi=gi: emit(sect, gi))

        @pl.when(j >= n_qkv)
        def _():
            for p in range(halves):
                rest_ref[rows[p], :] = res[p].astype(BF16)

    qkv_blk = pl.BlockSpec((tm, IN_BLOCK), lambda i, j: (i, jnp.minimum(j, n_qkv - 1)))
    small = pl.BlockSpec((1, HEAD_DIM), lambda i, j: (0, 0))
    return pl.pallas_call(
        body, name=name, grid=(s_len // tm, steps),
        in_specs=[pl.BlockSpec((tm, d), lambda i, j: (i, 0)), pl.BlockSpec((d, IN_BLOCK), lambda i, j: (0, j)),
                  small, small],
        out_specs=[qkv_blk, pl.BlockSpec((tm, IN_BLOCK), lambda i, j: (i, jnp.maximum(j - n_qkv, 0))), qkv_blk],
        out_shape=[jax.ShapeDtypeStruct((s_len, 3 * QKV), F32),
                   jax.ShapeDtypeStruct((s_len, w.shape[1] - 3 * QKV), BF16),
                   jax.ShapeDtypeStruct((s_len, 3 * QKV), BF16)],
        scratch_shapes=[pltpu.VMEM((tm // halves, HEAD_DIM), F32)] * 2,
        compiler_params=_params(2),
    )(h, w, q_norm, k_norm)


def _in_proj_bwd(dq, dk, dv, drest, w, name, after=()):
    s_len = dq.shape[0]
    d = w.shape[0]
    tm = PROJ_TILE
    steps = w.shape[1] // IN_BLOCK
    n_qkv = QKV // IN_BLOCK

    def body(dq_ref, dk_ref, dv_ref, dr_ref, w_ref, o_ref, acc_ref):
        j = pl.program_id(1)

        @pl.when(j == 0)
        def _():
            acc_ref[...] = jnp.zeros_like(acc_ref)

        def add(a_ref):
            acc_ref[...] += _dot_nt(a_ref[...], w_ref[...])

        _pick_part(j, n_qkv, [dq_ref, dk_ref, dv_ref, dr_ref], add)

        @pl.when(j == steps - 1)
        def _():
            o_ref[...] = acc_ref[...]

    return pl.pallas_call(
        _ordered(body, 5, after), name=name, grid=(s_len // tm, steps),
        in_specs=(_in_parts(tm, n_qkv, steps - 3 * n_qkv) + [pl.BlockSpec((d, IN_BLOCK), lambda i, j: (0, j))]
                  + [ANY] * len(after)),
        out_specs=pl.BlockSpec((tm, d), lambda i, j: (i, 0)),
        out_shape=jax.ShapeDtypeStruct((s_len, d), F32),
        scratch_shapes=[pltpu.VMEM((tm, d), F32)],
        compiler_params=_params(2),
    )(dq, dk, dv, drest, w, *after)


def _wgrad(x, y, x_spec, y_spec, out_shape, out_spec, acc_shape, n_chunks, name, x_transposed=False, after=()):
    s_len = y.shape[-2]
    ts = WGRAD_TILE
    steps = s_len // ts

    def body(x_ref, y_ref, o_ref, acc_ref):
        s = pl.program_id(1)

        @pl.when(s == 0)
        def _():
            acc_ref[...] = jnp.zeros_like(acc_ref)

        acc_ref[...] += (_dot if x_transposed else _dot_tn)(x_ref[...], y_ref[...])

        @pl.when(s == steps - 1)
        def _():
            o_ref[...] = acc_ref[...].astype(o_ref.dtype)

    return pl.pallas_call(
        _ordered(body, 2, after), name=name, grid=(n_chunks, steps),
        in_specs=[x_spec(ts), y_spec(ts)] + [ANY] * len(after), out_specs=out_spec,
        out_shape=jax.ShapeDtypeStruct(out_shape, BF16),
        scratch_shapes=[pltpu.VMEM(acc_shape, F32)],
        compiler_params=_params(2),
    )(x, y, *after)


def _pieces(width, piece=256):
    return [slice(a, min(a + piece, width)) for a in range(0, width, piece)]


def _ffn_fwd(h, w_gate, w_up, w_down, name):
    s_len, d = h.shape
    n_chunks, _, fs = w_gate.shape
    tm = FFN_TILE

    def body(h_ref, wg_ref, wu_ref, wd_ref, g_ref, u_ref, y_ref):
        j = pl.program_id(1)
        hv = h_ref[...]
        pieces = _pieces(fs)
        first = lambda cols: (_dot(hv, wg_ref[:, cols]), _dot(hv, wu_ref[:, cols]))
        total = None
        ahead = first(pieces[0])
        for k, cols in enumerate(pieces):
            g, u = ahead
            if k + 1 < len(pieces):
                ahead = first(pieces[k + 1])
            g_ref[:, cols] = g.astype(BF16)
            u_ref[:, cols] = u.astype(BF16)
            act = (g * _sigmoid(g)) * u
            part = _dot(act.astype(BF16), wd_ref[cols, :])
            total = part if total is None else total + part

        @pl.when(j == 0)
        def _():
            y_ref[...] = total

        @pl.when(j > 0)
        def _():
            y_ref[...] += total

    tile = pl.BlockSpec((tm, d), lambda i, j: (i, 0))
    hid = pl.BlockSpec((None, tm, fs), lambda i, j: (j, i, 0))
    w_in_spec = pl.BlockSpec((None, d, fs), lambda i, j: (j, 0, 0))
    hid_shape = jax.ShapeDtypeStruct((n_chunks, s_len, fs), BF16)
    return pl.pallas_call(
        body, name=name, grid=(s_len // tm, n_chunks),
        in_specs=[tile, w_in_spec, w_in_spec, pl.BlockSpec((None, fs, d), lambda i, j: (j, 0, 0))],
        out_specs=[hid, hid, tile],
        out_shape=[hid_shape, hid_shape, jax.ShapeDtypeStruct((s_len, d), F32)],
        compiler_params=_params(2),
    )(h, w_gate, w_up, w_down)


def _ffn_bwd(dy, g_pre, u_pre, w_gate, w_up, w_down, name):
    s_len, d = dy.shape
    n_chunks, _, fs = w_gate.shape
    tm = FFN_TILE

    def body(dy_ref, g_ref, u_ref, wg_ref, wu_ref, wd_ref, dh_ref, dg_ref, du_ref, a_ref):
        j = pl.program_id(1)
        dyv = dy_ref[...]
        pieces = _pieces(fs)
        first = lambda cols: _dot_nt(dyv, wd_ref[cols, :])
        total = None
        ahead = first(pieces[0])
        for k, cols in enumerate(pieces):
            da = ahead
            if k + 1 < len(pieces):
                ahead = first(pieces[k + 1])
            g = g_ref[:, cols].astype(F32)
            u = u_ref[:, cols].astype(F32)
            sg = _sigmoid(g)
            silu = g * sg
            dg = (da * u * (sg * (1.0 + g * (1.0 - sg)))).astype(BF16)
            du = (da * silu).astype(BF16)
            dg_ref[:, cols] = dg
            du_ref[:, cols] = du
            a_ref[:, cols] = (silu * u).astype(BF16)
            part = _dot_nt(dg, wg_ref[:, cols]) + _dot_nt(du, wu_ref[:, cols])
            total = part if total is None else total + part

        @pl.when(j == 0)
        def _():
            dh_ref[...] = total

        @pl.when(j > 0)
        def _():
            dh_ref[...] += total

    tile = pl.BlockSpec((tm, d), lambda i, j: (i, 0))
    hid = pl.BlockSpec((None, tm, fs), lambda i, j: (j, i, 0))
    w_in_spec = pl.BlockSpec((None, d, fs), lambda i, j: (j, 0, 0))
    hid_shape = jax.ShapeDtypeStruct((n_chunks, s_len, fs), BF16)
    return pl.pallas_call(
        body, name=name, grid=(s_len // tm, n_chunks),
        in_specs=[tile, hid, hid, w_in_spec, w_in_spec, pl.BlockSpec((None, fs, d), lambda i, j: (j, 0, 0))],
        out_specs=[tile, hid, hid, hid],
        out_shape=[jax.ShapeDtypeStruct((s_len, d), F32), hid_shape, hid_shape, hid_shape],
        compiler_params=_params(2),
    )(dy, g_pre, u_pre, w_gate, w_up, w_down)


def _ffn_wgrads(ht, dg, du, act, dy, tag, after=()):
    n_chunks, s_len, fs = dg.shape
    d = ht.shape[0]
    tok = lambda ts: pl.BlockSpec((ts, d), lambda c, s: (s, 0))
    tok_t = lambda ts: pl.BlockSpec((d, ts), lambda c, s: (0, s))
    hid = lambda ts: pl.BlockSpec((None, ts, fs), lambda c, s: (c, s, 0))
    d_up = pl.BlockSpec((None, d, fs), lambda c, s: (c, 0, 0))
    d_down = pl.BlockSpec((None, fs, d), lambda c, s: (c, 0, 0))
    dwg = _wgrad(ht, dg, tok_t, hid, (n_chunks, d, fs), d_up, (d, fs), n_chunks, tag + "_dwg", True, after)
    dwu = _wgrad(ht, du, tok_t, hid, (n_chunks, d, fs), d_up, (d, fs), n_chunks, tag + "_dwu", True, after)
    dwd = _wgrad(act, dy, hid, tok, (n_chunks, fs, d), d_down, (fs, d), n_chunks, tag + "_dwd", False, after)
    return dwg, dwu, dwd


def _band_bias():
    qi = lax.broadcasted_iota(jnp.int32, (ATTN_BLOCK, 2 * ATTN_BLOCK), 0)
    kj = lax.broadcasted_iota(jnp.int32, (ATTN_BLOCK, 2 * ATTN_BLOCK), 1)
    band = (kj >= qi) & (kj <= qi + ATTN_BLOCK)
    return jnp.where(band, 0.0, NEG), jnp.where(band & (kj >= ATTN_BLOCK), 0.0, NEG)


def _qkv_specs(slab_of, sections):
    def spec(sect, back):
        return pl.BlockSpec((SLAB, HEAD_DIM),
                            lambda h, s, g: (jnp.maximum(slab_of(s) - back, 0), (sect * N_GROUPS + g) * HEADS + h))
    return [spec(sect, back) for sect, back in sections]


HAT_BLOCKS = [(0, 0), (1, 0), (2, 0), (1, 1), (2, 1)]


def _stage_keys(k_ref, v_ref, kp_ref, vp_ref, kbuf, vbuf, dil, n):
    run = SLAB // dil
    for r in range(dil):
        own, before = pl.ds(r * run, run), pl.ds(2 * r * run, run)
        kbuf[pl.ds((2 * r + 1) * run, run), :] = k_ref[own, :]
        vbuf[pl.ds((2 * r + 1) * run, run), :] = v_ref[own, :]

        @pl.when(n > 0)
        def _():
            kbuf[before, :] = kp_ref[own, :]
            vbuf[before, :] = vp_ref[own, :]

        @pl.when(n == 0)
        def _():
            kbuf[before, :] = jnp.zeros((run, HEAD_DIM), BF16)
            vbuf[before, :] = jnp.zeros((run, HEAD_DIM), BF16)


def _for_each_tile(dil, n, first_fn, rest_fn):
    run = SLAB // dil
    bias, first_bias = _band_bias()
    tiles = []
    for jj in range(run // ATTN_BLOCK):
        start = jj * ATTN_BLOCK
        tile_bias = jnp.where(n == 0, first_bias, bias) if jj == 0 else bias
        for r in range(dil):
            tiles.append((pl.ds(r * run + start, ATTN_BLOCK),
                          pl.ds((2 * r + 1) * run - ATTN_BLOCK + start, 2 * ATTN_BLOCK), tile_bias))
    ahead = first_fn(*tiles[0])
    for t, tile in enumerate(tiles):
        begun = ahead
        if t + 1 < len(tiles):
            ahead = first_fn(*tiles[t + 1])
        rest_fn(*tile, begun)


def _attn_fwd(hat, name):
    s_len = hat.shape[0]
    e = HEAD_DIM
    n_slabs = s_len // SLAB

    def body(q_ref, k_ref, v_ref, kp_ref, vp_ref, o_ref, lse_ref, kbuf, vbuf, m_s, l_s, acc_s, m_p, l_p, acc_p, tmp_s):
        n, grp = pl.program_id(1), pl.program_id(2)

        def run(gi, dil):
            _stage_keys(k_ref, v_ref, kp_ref, vp_ref, kbuf, vbuf, dil, n)

            def scores(q_rows, kv_rows, bias):
                return _dot_nt(q_ref[q_rows, :], kbuf[kv_rows, :])

            def rest(q_rows, kv_rows, bias, qk):
                s = qk * ATTN_SCALE + bias
                m = jnp.max(s, axis=-1, keepdims=True)
                p = jnp.exp(s - m)
                m_p[q_rows, :] = jnp.broadcast_to(m, (ATTN_BLOCK, e))
                l_p[q_rows, :] = jnp.broadcast_to(jnp.sum(p, axis=-1, keepdims=True), (ATTN_BLOCK, e))
                acc_p[q_rows, :] = _dot(p.astype(BF16), vbuf[kv_rows, :])

            _for_each_tile(dil, n, scores, rest)
            _to_token_order(m_s.at[gi], m_p, dil, tmp_s)
            _to_token_order(l_s.at[gi], l_p, dil, tmp_s)
            _to_token_order(acc_s.at[gi], acc_p, dil, tmp_s)

        for gi, dil in enumerate(DILATIONS):
            pl.when(grp == gi)(lambda gi=gi, dil=dil: run(gi, dil))

        @pl.when(grp == N_GROUPS - 1)
        def _():
            m_all = jnp.maximum(jnp.maximum(m_s[0], m_s[1]), m_s[2])
            den = jnp.zeros((SLAB, e), F32)
            num = jnp.zeros((SLAB, e), F32)
            for gi in range(N_GROUPS):
                w = jnp.exp(m_s[gi] - m_all)
                den += l_s[gi] * w
                num += acc_s[gi] * w
            o_ref[...] = num / den
            lse_ref[...] = m_all + jnp.log(den)

    out = pl.BlockSpec((SLAB, e), lambda h, n, g: (n, h))
    return pl.pallas_call(
        body, name=name, grid=(HEADS, n_slabs, N_GROUPS),
        in_specs=_qkv_specs(lambda n: n, HAT_BLOCKS),
        out_specs=[out, out],
        out_shape=[jax.ShapeDtypeStruct((s_len, HEADS * e), F32)] * 2,
        scratch_shapes=[pltpu.VMEM((2 * SLAB, e), BF16), pltpu.VMEM((2 * SLAB, e), BF16),
                        pltpu.VMEM((N_GROUPS, SLAB, e), F32), pltpu.VMEM((N_GROUPS, SLAB, e), F32),
                        pltpu.VMEM((N_GROUPS, SLAB, e), F32)]
        + [pltpu.VMEM((SLAB, e), F32)] * 4,
        compiler_params=_params(3),
    )(hat, hat, hat, hat, hat)


def _attn_bwd(qkv, hat, d_out, out, lse, q_norm, k_norm, name):
    s_len = qkv.shape[0]
    e = HEAD_DIM
    n_slabs = s_len // SLAB

    def body(q_ref, k_ref, v_ref, kp_ref, vp_ref, qraw_ref, kraw_ref, do_ref, o_ref, lse_ref, qn_ref, kn_ref,
             dq_ref, dk_ref, dv_ref, st_ref, kbuf, vbuf, stat_s, dqs, dkb, dvb, dk_tok, dv_tok, carry,
             do_p, stat_p, dq_p, dk_p, dv_p, tmp_s, do16_p):
        head, step, grp = pl.program_id(0), pl.program_id(1), pl.program_id(2)
        n = n_slabs - 1 - step
        dkb[...] = jnp.zeros_like(dkb)
        dvb[...] = jnp.zeros_like(dvb)
        @pl.when(grp == 0)
        def _():
            lane = lax.broadcasted_iota(jnp.int32, (SLAB, e), 1)
            stat_s[...] = jnp.where(lane < e // 2, lse_ref[...],
                                    jnp.sum(do_ref[...] * o_ref[...], axis=-1, keepdims=True))

        @pl.when((head == 0) & (step == 0) & (grp == 0))
        def _():
            st_ref[...] = jnp.zeros_like(st_ref)

        def run(gi, dil):
            seg = SLAB // dil
            _stage_keys(k_ref, v_ref, kp_ref, vp_ref, kbuf, vbuf, dil, n)

            @pl.when(step == 0)
            def _():
                carry[gi] = jnp.zeros((2, SLAB, e), F32)

            _to_residue_order(do_p, do_ref, dil, tmp_s)
            do16_p[...] = do_p[...].astype(BF16)
            _to_residue_order(stat_p, stat_s, dil, tmp_s)

            def scores(q_rows, kv_rows, bias):
                return _dot_nt(q_ref[q_rows, :], kbuf[kv_rows, :]), _dot_nt(do16_p[q_rows, :], vbuf[kv_rows, :])

            def rest(q_rows, kv_rows, bias, begun):
                qk, dp = begun
                q = q_ref[q_rows, :]
                k = kbuf[kv_rows, :]
                stat = stat_p[q_rows, :]
                p = jnp.exp(qk * ATTN_SCALE + bias - stat[:, 0:1])
                ds = (p * (dp - stat[:, e // 2:e // 2 + 1]) * ATTN_SCALE).astype(BF16)
                dq_p[q_rows, :] = _dot(ds, k)
                dkb[kv_rows, :] += _dot_tn(ds, q)
                dvb[kv_rows, :] += _dot_tn(p.astype(BF16), do16_p[q_rows, :])

            _for_each_tile(dil, n, scores, rest)
            for r in range(dil):
                own, before = pl.ds((2 * r + 1) * seg, seg), pl.ds(2 * r * seg, seg)
                kept = pl.ds(r * seg, seg)
                dk_p[kept, :] = dkb[own, :] + carry.at[gi, 0][kept, :]
                dv_p[kept, :] = dvb[own, :] + carry.at[gi, 1][kept, :]
                carry.at[gi, 0][kept, :] = dkb[before, :]
                carry.at[gi, 1][kept, :] = dvb[before, :]
            _to_token_order(dqs, dq_p, dil, tmp_s)
            _to_token_order(dk_tok, dk_p, dil, tmp_s)
            _to_token_order(dv_tok, dv_p, dil, tmp_s)

            def norm_bwd(raw, gain, d_hat):
                r = _rms(raw)
                y = raw * r
                dy = d_hat * gain
                return r * (dy - y * jnp.mean(dy * y, axis=-1, keepdims=True)), jnp.sum(d_hat * y, axis=0, keepdims=True)

            dq, dqn = norm_bwd(qraw_ref[...], qn_ref[...], dqs[...])
            dk, dkn = norm_bwd(kraw_ref[...], kn_ref[...], dk_tok[...])
            dq_ref[...] = dq.astype(BF16)
            dk_ref[...] = dk.astype(BF16)
            dv_ref[...] = dv_tok[...].astype(BF16)
            st_ref[0:1, :] += dqn
            st_ref[1:2, :] += dkn

        for gi, dil in enumerate(DILATIONS):
            pl.when(grp == gi)(lambda gi=gi, dil=dil: run(gi, dil))

    slab_of = lambda s: n_slabs - 1 - s
    small = pl.BlockSpec((1, e), lambda h, s, g: (0, 0))
    head_blk = pl.BlockSpec((SLAB, e), lambda h, s, g: (slab_of(s), h))
    grad_blk = pl.BlockSpec((SLAB, e), lambda h, s, g: (slab_of(s), g * HEADS + h))
    grad_shape = jax.ShapeDtypeStruct((s_len, QKV), BF16)
    return pl.pallas_call(
        body, name=name, grid=(HEADS, n_slabs, N_GROUPS),
        in_specs=(_qkv_specs(slab_of, HAT_BLOCKS) + _qkv_specs(slab_of, [(0, 0), (1, 0)])
                  + [head_blk, head_blk, head_blk, small, small]),
        out_specs=[grad_blk, grad_blk, grad_blk, pl.BlockSpec((8, e), lambda h, s, g: (0, 0))],
        out_shape=[grad_shape, grad_shape, grad_shape, jax.ShapeDtypeStruct((8, e), F32)],
        scratch_shapes=[pltpu.VMEM((2 * SLAB, e), BF16), pltpu.VMEM((2 * SLAB, e), BF16), pltpu.VMEM((SLAB, e), F32),
                        pltpu.VMEM((SLAB, e), F32), pltpu.VMEM((2 * SLAB, e), F32), pltpu.VMEM((2 * SLAB, e), F32),
                        pltpu.VMEM((SLAB, e), F32), pltpu.VMEM((SLAB, e), F32),
                        pltpu.VMEM((N_GROUPS, 2, SLAB, e), F32)]
        + [pltpu.VMEM((SLAB, e), F32)] * 6 + [pltpu.VMEM((SLAB, e), BF16)],
        compiler_params=_params(3),
    )(hat, hat, hat, hat, hat, qkv, qkv, d_out, out, lse, q_norm, k_norm)


def _shift_rows(x, by, edge, forward):
    t_len = x.shape[0]
    row = lax.broadcasted_iota(jnp.int32, x.shape, 0)
    if forward:
        out = pltpu.roll(x, by, 0)
        for i in range(by):
            out = jnp.where(row == i, edge[8 - by + i:8 - by + i + 1, :], out)
    else:
        out = pltpu.roll(x, t_len - by, 0)
        for i in range(by):
            out = jnp.where(row == t_len - by + i, edge[i:i + 1, :], out)
    return out


def _mix_fwd(x, o, rest, mod, conv_w, w_attn, w_conv, w_out, name):
    s_len, d = x.shape
    tm = MIX_TILE
    a_w = o.shape[1]

    def body(x_ref, o_ref, u_ref, b_ref, c_ref, ga_ref, gc_ref, mod_ref, cw_ref, wa_ref, wc_ref, wo_ref,
             xo_ref, z_ref, ya_ref, yc_ref, conv_ref, yb_ref, m_ref, o16_ref, carry):
        @pl.when(pl.program_id(0) == 0)
        def _():
            carry[...] = jnp.zeros_like(carry)

        xc = c_ref[...].astype(F32) * u_ref[...].astype(F32)
        edge = carry[...]
        conv = (_shift_rows(xc, 2, edge, True) * cw_ref[0:1, :] + _shift_rows(xc, 1, edge, True) * cw_ref[1:2, :]
                + xc * cw_ref[2:3, :])
        carry[...] = xc[tm - 8:tm, :]
        yb = (b_ref[...].astype(F32) * conv).astype(BF16)
        o16 = o_ref[...].astype(BF16)
        ya = _dot(o16, wa_ref[...])
        yc = _dot(yb, wc_ref[...])
        merged = (_sigmoid(ga_ref[...].astype(F32)) * ya + _sigmoid(gc_ref[...].astype(F32)) * yc).astype(BF16)
        z = _dot(merged, wo_ref[...])
        xo_ref[...] = x_ref[...] + mod_ref[2:3, :] * z
        z_ref[...] = z
        ya_ref[...] = ya.astype(BF16)
        yc_ref[...] = yc.astype(BF16)
        conv_ref[...] = conv.astype(BF16)
        yb_ref[...] = yb
        m_ref[...] = merged
        o16_ref[...] = o16

    tile = pl.BlockSpec((tm, d), lambda i: (i, 0))
    sect = lambda k: pl.BlockSpec((tm, d), lambda i: (i, k))
    att = pl.BlockSpec((tm, a_w), lambda i: (i, 0))
    const = lambda shape: pl.BlockSpec(shape, lambda i: (0, 0))
    f32_out = jax.ShapeDtypeStruct((s_len, d), F32)
    b16_out = jax.ShapeDtypeStruct((s_len, d), BF16)
    return pl.pallas_call(
        body, name=name, grid=(s_len // tm,),
        in_specs=[tile, att, sect(0), sect(1), sect(2), sect(3), sect(4), const((8, d)), const((8, d)),
                  const((a_w, d)), const((d, d)), const((d, d))],
        out_specs=[tile] * 7 + [att],
        out_shape=[f32_out, f32_out] + [b16_out] * 5 + [jax.ShapeDtypeStruct((s_len, a_w), BF16)],
        scratch_shapes=[pltpu.VMEM((8, d), F32)],
        compiler_params=_params(1),
    )(x, o, rest, rest, rest, rest, rest, mod, conv_w, w_attn, w_conv, w_out)


def _mix_bwd(dxo, ya, yc, conv, rest, mod, conv_w, w_attn, w_conv, w_out, a_w, name):
    s_len, d = dxo.shape
    tm = MIX_TILE
    n_tiles = s_len // tm

    def body(dxo_ref, ya_ref, yc_ref, conv_ref, u_ref, b_ref, c_ref, ga_ref, gc_ref, mod_ref, cw_ref,
             wa_ref, wc_ref, wo_ref, do_ref, drest_ref, dz_ref, dya_ref, dyc_ref, st_ref, carry):
        @pl.when(pl.program_id(0) == 0)
        def _():
            carry[...] = jnp.zeros_like(carry)
            st_ref[...] = jnp.zeros_like(st_ref)

        dz = (mod_ref[2:3, :] * dxo_ref[...]).astype(BF16)
        dz_ref[...] = dz
        dm = _dot_nt(dz, wo_ref[...])
        sa, sc = _sigmoid(ga_ref[...].astype(F32)), _sigmoid(gc_ref[...].astype(F32))
        dya = (dm * sa).astype(BF16)
        dyc = (dm * sc).astype(BF16)
        dya_ref[...] = dya
        dyc_ref[...] = dyc
        drest_ref[:, 3 * d:4 * d] = (dm * ya_ref[...].astype(F32) * (sa * (1.0 - sa))).astype(BF16)
        drest_ref[:, 4 * d:5 * d] = (dm * yc_ref[...].astype(F32) * (sc * (1.0 - sc))).astype(BF16)
        do_ref[...] = _dot_nt(dya, wa_ref[...])
        dyb = _dot_nt(dyc, wc_ref[...])
        drest_ref[:, d:2 * d] = (dyb * conv_ref[...].astype(F32)).astype(BF16)
        dconv = dyb * b_ref[...].astype(F32)
        edge = carry[...]
        sh1 = _shift_rows(dconv, 1, edge, False)
        sh2 = _shift_rows(dconv, 2, edge, False)
        carry[...] = dconv[0:8, :]
        dxc = dconv * cw_ref[2:3, :] + sh1 * cw_ref[1:2, :] + sh2 * cw_ref[0:1, :]
        u, c = u_ref[...].astype(F32), c_ref[...].astype(F32)
        xc = c * u
        drest_ref[:, 0:d] = (dxc * c).astype(BF16)
        drest_ref[:, 2 * d:3 * d] = (dxc * u).astype(BF16)
        st_ref[0:1, :] += jnp.sum(xc * sh2, axis=0, keepdims=True)
        st_ref[1:2, :] += jnp.sum(xc * sh1, axis=0, keepdims=True)
        st_ref[2:3, :] += jnp.sum(xc * dconv, axis=0, keepdims=True)

    rev = lambda i: n_tiles - 1 - i
    tile = pl.BlockSpec((tm, d), lambda i: (rev(i), 0))
    sect = lambda k: pl.BlockSpec((tm, d), lambda i: (rev(i), k))
    const = lambda shape: pl.BlockSpec(shape, lambda i: (0, 0))
    b16_out = jax.ShapeDtypeStruct((s_len, d), BF16)
    return pl.pallas_call(
        body, name=name, grid=(n_tiles,),
        in_specs=[tile, tile, tile, tile, sect(0), sect(1), sect(2), sect(3), sect(4), const((8, d)), const((8, d)),
                  const((a_w, d)), const((d, d)), const((d, d))],
        out_specs=[pl.BlockSpec((tm, a_w), lambda i: (rev(i), 0)), pl.BlockSpec((tm, 5 * d), lambda i: (rev(i), 0)),
                   tile, tile, tile, const((8, d))],
        out_shape=[jax.ShapeDtypeStruct((s_len, a_w), F32), jax.ShapeDtypeStruct((s_len, 5 * d), BF16),
                   b16_out, b16_out, b16_out, jax.ShapeDtypeStruct((8, d), F32)],
        scratch_shapes=[pltpu.VMEM((8, d), F32)],
        compiler_params=_params(1),
    )(dxo, ya, yc, conv, rest, rest, rest, rest, rest, mod, conv_w, w_attn, w_conv, w_out)


ADA_COLS = 128


def _ada_fwd(c_all, w_shard, b_shard, name):
    d, cols = w_shard.shape

    def body(c_ref, w_ref, b_ref, o_ref):
        cv = c_ref[...]
        o_ref[...] = jnp.dot(cv * _sigmoid(cv), w_ref[...], preferred_element_type=F32,
                             precision=lax.Precision.HIGHEST) + b_ref[...]

    return pl.pallas_call(
        body, name=name, grid=(cols // ADA_COLS,),
        in_specs=[pl.BlockSpec((8, d), lambda j: (0, 0)), pl.BlockSpec((d, ADA_COLS), lambda j: (0, j)),
                  pl.BlockSpec((1, ADA_COLS), lambda j: (0, j))],
        out_specs=pl.BlockSpec((8, ADA_COLS), lambda j: (0, j)),
        out_shape=jax.ShapeDtypeStruct((8, cols), F32),
        compiler_params=_params(1),
    )(c_all, w_shard, b_shard)


def _ada_bwd(c_all, dmod_shard, w, m, v, name):
    d, cols = w.shape

    def body(c_ref, dm_ref, w_ref, m_ref, v_ref, g_ref, d_ref, nm_ref, nv_ref):
        cv = c_ref[...]
        g = lax.dot_general(cv * _sigmoid(cv), dm_ref[...], (((0,), (0,)), ((), ())),
                            preferred_element_type=F32, precision=lax.Precision.HIGHEST)
        g_ref[...] = g
        d_ref[...], nm_ref[...], nv_ref[...] = _adamw_math(w_ref[...], g, m_ref[...], v_ref[...])

    blk = pl.BlockSpec((d, ADA_COLS), lambda j: (0, j))
    shape = jax.ShapeDtypeStruct((d, cols), F32)
    return pl.pallas_call(
        body, name=name, grid=(cols // ADA_COLS,),
        in_specs=[pl.BlockSpec((8, d), lambda j: (0, 0)), pl.BlockSpec((8, ADA_COLS), lambda j: (0, j)), blk, blk, blk],
        out_specs=[blk] * 4, out_shape=[shape] * 4,
        compiler_params=_params(1),
    )(c_all, dmod_shard, w, m, v)


def _small_update(parts, w, m, v, name):
    n = w.shape[1]

    def body(p_ref, w_ref, m_ref, v_ref, g_ref, d_ref, nm_ref, nv_ref):
        g = p_ref[0:1, :]
        for i in range(1, 8):
            g = g + p_ref[i:i + 1, :]
        g_ref[...] = g
        d_ref[...], nm_ref[...], nv_ref[...] = _adamw_math(w_ref[...], g, m_ref[...], v_ref[...])

    shape = jax.ShapeDtypeStruct((1, n), F32)
    return pl.pallas_call(body, name=name, out_shape=[shape] * 4, compiler_params=_params())(parts, w, m, v)


def _cols_to_shards(w, n):
    r, nc = w.shape
    return w.reshape(r, n, nc // n).transpose(1, 0, 2)


def kernel(x, c, w_ada, b_ada, norm_ffn1, ffn1_w_gate, ffn1_w_up, ffn1_w_down, norm_mix, w_in, q_norm, k_norm, conv_w, w_attn_branch, w_conv_branch, w_out, norm_ffn2, ffn2_w_gate, ffn2_w_up, ffn2_w_down, loss_target, m_w_ada, m_b_ada, m_norm_ffn1, m_ffn1_w_gate, m_ffn1_w_up, m_ffn1_w_down, m_norm_mix, m_w_in, m_q_norm, m_k_norm, m_conv_w, m_w_attn_branch, m_w_conv_branch, m_w_out, m_norm_ffn2, m_ffn2_w_gate, m_ffn2_w_up, m_ffn2_w_down, v_w_ada, v_b_ada, v_norm_ffn1, v_ffn1_w_gate, v_ffn1_w_up, v_ffn1_w_down, v_norm_mix, v_w_in, v_q_norm, v_k_norm, v_conv_w, v_w_attn_branch, v_w_conv_branch, v_w_out, v_norm_ffn2, v_ffn2_w_gate, v_ffn2_w_up, v_ffn2_w_down):
    ix, iy, ic = _place()
    chip = 2 * ix + iy
    me = 4 * ix + 2 * iy + ic
    xs = x[0]
    target = loss_target[0]
    s_len, d = xs.shape
    ada_cols = w_ada.shape[2]
    conv_cols = conv_w.shape[2]

    conv_rows = jnp.zeros((8, conv_cols), F32).at[0:3].set(conv_w[0])
    small_in = jnp.concatenate([jnp.broadcast_to(c, (8, d)), conv_rows], axis=1)
    small_all = _allgather8(small_in, "gather_c").reshape(8, 8, d + conv_cols)
    c_all = small_all[:, 0, :d]
    conv_full = small_all[0::2, 0:3, d:].transpose(1, 0, 2).reshape(3, N_CHIPS * conv_cols)
    conv_pad = jnp.zeros((8, N_CHIPS * conv_cols), F32).at[0:3].set(conv_full)
    b_shard = lax.dynamic_slice(b_ada, (0, chip * ada_cols), (1, ada_cols))
    mod_part = _ada_fwd(c_all, w_ada[0], b_shard, "ada_fwd")
    mod_all = _allgather8(mod_part, "gather_mod").reshape(N_CHIPS, 2, 8, ada_cols)[:, 0]
    mod_mine = lax.dynamic_slice(mod_all, (0, me, 0), (N_CHIPS, 1, ada_cols)).reshape(9, d)

    def mod_rows(i, gain):
        return jnp.zeros((8, d), F32).at[0:3].set(mod_mine[3 * i:3 * i + 3]).at[3:4].set(gain)

    mod1, mod2, mod3 = mod_rows(0, norm_ffn1), mod_rows(1, norm_mix), mod_rows(2, norm_ffn2)

    to16 = lambda w: w[0].astype(BF16)
    wg1, wu1, wd1 = _gather_weights([to16(ffn1_w_gate), to16(ffn1_w_up), to16(ffn1_w_down)], [False] * 3,
                                    "gather_ffn1", 1)
    h1, h1t = _norm_mod(xs, mod1, "norm1")
    (w_in_full,) = _gather_weights([to16(w_in)], [True], "gather_w_in", 2, after=(wd1, h1))

    g1, u1, y1 = _ffn_fwd(h1, wg1, wu1, wd1, "ffn1_fwd")
    x1, h2, h2t = _norm_mod(xs, mod2, "norm2", prev=(y1, mod1, 0.5))
    qkv, rest, qkv_hat = _in_proj(h2, w_in_full, q_norm, k_norm, "in_proj")
    w_ab, w_cb_g, w_o_g, wg2, wu2, wd2 = _gather_weights(
        [to16(w_attn_branch), to16(w_conv_branch), to16(w_out),
         to16(ffn2_w_gate), to16(ffn2_w_up), to16(ffn2_w_down)], [True] + [False] * 5,
        "gather_rest", 3, after=(h2,))
    a_w = w_ab.shape[0]
    w_cb = w_cb_g.reshape(d, d)
    w_o = w_o_g.reshape(d, d)
    o, lse = _attn_fwd(qkv_hat, "attn_fwd")
    x2, z, ya, yc, conv, yb, merged, o16 = _mix_fwd(x1, o, rest, mod2, conv_pad, w_ab, w_cb, w_o, "mix_fwd")
    h3, h3t = _norm_mod(x2, mod3, "norm3")
    g3, u3, y3 = _ffn_fwd(h3, wg2, wu2, wd2, "ffn2_fwd")
    dx3, dy3, loss_part = _loss_grad(x2, y3, mod3, target, "loss")
    loss = lax.psum(0.5 * jnp.sum(loss_part) / d, ("x", "y", "c"))

    c_idx = jnp.reshape(ic, (1,)).astype(jnp.int32)
    chip_idx = jnp.stack([chip, ic]).astype(jnp.int32)

    def reduce_start(grads, names, tag, collective_id):
        from_sibling = _rs_pair_exchange(grads, "rs_pair_" + tag)
        pair_sums = [_pair_add(g, r, c_idx, "pair_add_" + nm) for g, r, nm in zip(grads, from_sibling, names)]
        return pair_sums, _rs_chip_exchange(pair_sums, "rs_chips_" + tag, collective_id)

    def reduce_finish(pair_sums, from_chips, names, tag, after):
        totals = [_chip_add(p, r, chip_idx, "chip_add_" + nm, after)
                  for p, r, nm in zip(pair_sums, from_chips, names)]
        return dict(zip(names, _rs_share(totals, "rs_share_" + tag)))

    names_a = ["ffn2_w_gate", "ffn2_w_up", "ffn2_w_down"]
    names_b = ["w_in", "w_attn_branch", "w_conv_branch", "w_out"]
    names_c = ["ffn1_w_gate", "ffn1_w_up", "ffn1_w_down"]

    dh3, dg3, du3, a3 = _ffn_bwd(dy3, g3, u3, wg2, wu2, wd2, "ffn2_bwd")
    sums_a, chips_a = reduce_start(list(_ffn_wgrads(h3t, dg3, du3, a3, dy3, "ffn2")), names_a, "a", 4)
    dx2, st3 = _norm_bwd(dh3, x2, mod3, dx3, y3, 0.5, "norm3_bwd", after=tuple(sums_a))

    do, drest, dz, dya, dyc, st_conv = _mix_bwd(dx2, ya, yc, conv, rest, mod2, conv_pad, w_ab, w_cb, w_o, a_w, "mix_bwd")
    dq, dk, dv, st_qk = _attn_bwd(qkv, qkv_hat, do, o, lse, q_norm, k_norm, "attn_bwd")
    tok = lambda width: (lambda ts: pl.BlockSpec((ts, width), lambda cc, s: (s, 0)))
    colblk = lambda width: (lambda ts: pl.BlockSpec((ts, width), lambda cc, s: (s, cc)))
    tok_t = lambda ts: pl.BlockSpec((d, ts), lambda cc, s: (0, s))
    whole = pl.BlockSpec((d, QKV), lambda cc, s: (0, 0))
    dw_in = [_wgrad(h2t, part, tok_t, tok(QKV), (d, QKV), whole, (d, QKV), 1, "dw_in_" + nm, True)
             for part, nm in ((dq, "q"), (dk, "k"), (dv, "v"))]
    dw_in.append(_wgrad(h2t, drest, tok_t, colblk(d), (d, 5 * d), pl.BlockSpec((d, d), lambda cc, s: (0, cc)),
                        (d, d), 5, "dw_in_rest", True))
    dw_in = _cols_to_shards(jnp.concatenate(dw_in, axis=1), N_CHIPS)
    shard_w = d // N_CHIPS
    dw_ab = _wgrad(o16, dya, tok(a_w), colblk(shard_w), (a_w, d), pl.BlockSpec((a_w, shard_w), lambda cc, s: (0, cc)),
                   (a_w, shard_w), N_CHIPS, "dw_attn_branch")
    dw_ab = _cols_to_shards(dw_ab, N_CHIPS)
    row_out = pl.BlockSpec((None, shard_w, d), lambda cc, s: (cc, 0, 0))
    dw_cb = _wgrad(yb, dyc, colblk(shard_w), tok(d), (N_CHIPS, shard_w, d), row_out, (shard_w, d), N_CHIPS, "dw_conv_branch")
    dw_o = _wgrad(merged, dz, colblk(shard_w), tok(d), (N_CHIPS, shard_w, d), row_out, (shard_w, d), N_CHIPS, "dw_out")
    shard_grads = reduce_finish(sums_a, chips_a, names_a, "a", after=(dw_in, dw_o))
    sums_b, chips_b = reduce_start([dw_in, dw_ab, dw_cb, dw_o], names_b, "b", 5)

    dh2 = _in_proj_bwd(dq, dk, dv, drest, w_in_full, "in_proj_bwd", after=tuple(sums_b))
    dx1, st2, dy1 = _norm_bwd(dh2, x1, mod2, dx2, z, 1.0, "norm2_bwd", prev=(mod1, 0.5))
    dh1, dg1, du1, a1 = _ffn_bwd(dy1, g1, u1, wg1, wu1, wd1, "ffn1_bwd")
    dx0, st1 = _norm_bwd(dh1, xs, mod1, dx1, y1, 0.5, "norm1_bwd")
    grads_c = list(_ffn_wgrads(h1t, dg1, du1, a1, dy1, "ffn1"))
    shard_grads.update(reduce_finish(sums_b, chips_b, names_b, "b", after=tuple(grads_c)))
    sums_c, chips_c = reduce_start(grads_c, names_c, "c", 6)

    dmod = jnp.concatenate([st1[0:3], st2[0:3], st3[0:3]], axis=0).reshape(1, 9 * d)
    small = jnp.concatenate([dmod, st1[3:4], st2[3:4], st3[3:4], st_qk[0:1], st_qk[1:2],
                             st_conv[0:3].reshape(1, 3 * d)], axis=1)
    small_all = _allgather8(jnp.broadcast_to(small, (8, small.shape[1])), "gather_small").reshape(8, 8, -1)[:, 0]
    dmod_all = small_all[:, :9 * d]
    dmod_shard = lax.dynamic_slice(dmod_all, (0, chip * ada_cols), (8, ada_cols))
    g_w_ada, d_w_ada, nm_w_ada, nv_w_ada = _ada_bwd(c_all, dmod_shard, w_ada[0], m_w_ada[0], v_w_ada[0], "ada_bwd")

    vec_names = ["b_ada", "norm_ffn1", "norm_mix", "norm_ffn2", "q_norm", "k_norm"]
    vec_w = [b_ada, norm_ffn1, norm_mix, norm_ffn2, q_norm, k_norm]
    vec_m = [m_b_ada, m_norm_ffn1, m_norm_mix, m_norm_ffn2, m_q_norm, m_k_norm]
    vec_v = [v_b_ada, v_norm_ffn1, v_norm_mix, v_norm_ffn2, v_q_norm, v_k_norm]
    n_vec = sum(w.shape[1] for w in vec_w)
    cat = lambda arrs: jnp.concatenate(arrs, axis=1)
    vec_out = _small_update(small_all[:, :n_vec], cat(vec_w), cat(vec_m), cat(vec_v), "small_update")
    conv_parts = small_all[:, n_vec:].reshape(8, 3, N_CHIPS * conv_cols)
    conv_parts = lax.dynamic_slice(conv_parts, (0, 0, chip * conv_cols), (8, 3, conv_cols)).reshape(8, 3 * conv_cols)
    flat3 = lambda w: w[0].reshape(1, 3 * conv_cols)
    conv_out = _small_update(conv_parts, flat3(conv_w), flat3(m_conv_w), flat3(v_conv_w), "conv_update")

    res = {"w_ada": [t[None] for t in (g_w_ada, d_w_ada, nm_w_ada, nv_w_ada)],
           "conv_w": [t.reshape(1, 3, conv_cols) for t in conv_out]}
    off = 0
    for nm, w in zip(vec_names, vec_w):
        width = w.shape[1]
        res[nm] = [t[:, off:off + width] for t in vec_out]
        off += width
    big = {"ffn1_w_gate": (ffn1_w_gate, m_ffn1_w_gate, v_ffn1_w_gate), "ffn1_w_up": (ffn1_w_up, m_ffn1_w_up, v_ffn1_w_up),
           "ffn1_w_down": (ffn1_w_down, m_ffn1_w_down, v_ffn1_w_down), "w_in": (w_in, m_w_in, v_w_in),
           "w_attn_branch": (w_attn_branch, m_w_attn_branch, v_w_attn_branch),
           "w_conv_branch": (w_conv_branch, m_w_conv_branch, v_w_conv_branch), "w_out": (w_out, m_w_out, v_w_out),
           "ffn2_w_gate": (ffn2_w_gate, m_ffn2_w_gate, v_ffn2_w_gate), "ffn2_w_up": (ffn2_w_up, m_ffn2_w_up, v_ffn2_w_up),
           "ffn2_w_down": (ffn2_w_down, m_ffn2_w_down, v_ffn2_w_down)}
    def update(nm, after=()):
        w, m, v = big[nm]
        g, delta, new_m, new_v = _adamw(w[0], shard_grads[nm], m[0], v[0], "adamw_" + nm, after)
        res[nm] = [t[None] for t in (g, delta, new_m, new_v)]
        return new_v

    last = tuple(sums_c)
    for nm in names_a + names_b:
        last = (update(nm, last),)
    shard_grads.update(reduce_finish(sums_c, chips_c, names_c, "c", after=last))
    for nm in names_c:
        update(nm)

    order = ["w_ada", "b_ada", "norm_ffn1", "ffn1_w_gate", "ffn1_w_up", "ffn1_w_down", "norm_mix", "w_in", "q_norm",
             "k_norm", "conv_w", "w_attn_branch", "w_conv_branch", "w_out", "norm_ffn2", "ffn2_w_gate", "ffn2_w_up",
             "ffn2_w_down"]
    return (loss, dx0[None], *[res[nm][0] for nm in order], *[res[nm][1] for nm in order],
            *[res[nm][2] for nm in order], *[res[nm][3] for nm in order])
```

```python
import jax
import jax.numpy as jnp
from jax import lax
from jax.experimental import pallas as pl
from jax.experimental.pallas import tpu as pltpu
from jax.experimental.pallas import tpu_sc as plsc

F32 = jnp.float32
BF16 = jnp.bfloat16
MESH = pl.DeviceIdType.MESH
ANY = pl.BlockSpec(memory_space=pl.ANY)

NORM_EPS = 1e-6
HEAD_DIM = 128
N_GROUPS = 3
HEADS = 4
DILATIONS = (1, 4, 16)
ATTN_BLOCK = 128
SLAB = ATTN_BLOCK * max(DILATIONS)
QKV = N_GROUPS * HEADS * HEAD_DIM
ATTN_SCALE = HEAD_DIM ** -0.5
NEG = -1e30
N_CHIPS = 4

ADAM_LR = 0.001
ADAM_B1 = 0.9
ADAM_B2 = 0.999
ADAM_EPS = 1e-08
ADAM_WD = 0.01
ADAM_STEP = 10

VMEM_LIMIT_BYTES = 56 * 1024 * 1024
TOKEN_TILE = 512
FFN_TILE = 1024
PROJ_TILE = 2048
WGRAD_TILE = 2048
IN_BLOCK = 512
MIX_TILE = 256


def _params(n_axes=0):
    return pltpu.CompilerParams(
        dimension_semantics=("arbitrary",) * n_axes if n_axes else None,
        vmem_limit_bytes=VMEM_LIMIT_BYTES)


def _dot(a, b):
    return jnp.dot(a, b, preferred_element_type=F32)


def _dot_nt(a, b):
    return lax.dot_general(a, b, (((1,), (1,)), ((), ())), preferred_element_type=F32)


def _dot_tn(a, b):
    return lax.dot_general(a, b, (((0,), (0,)), ((), ())), preferred_element_type=F32)


def _sigmoid(x):
    return 1.0 / (1.0 + jnp.exp(-x))


def _place():
    return lax.axis_index("x"), lax.axis_index("y"), lax.axis_index("c")


def _ordered(body, n_in, after):
    if not after:
        return body
    return lambda *refs: body(*refs[:n_in], *refs[n_in + len(after):])


def _allgather8(block, name):
    m_per, n = block.shape

    def body(x_ref, out_ref, send_sems, recv_sems, local_sem):
        x, y, c = _place()
        me, sibling = (x, y, c), (x, y, 1 - c)
        chips = [(1 - x, y), (x, 1 - y), (1 - x, 1 - y)]

        def rows(px, py, pc):
            return out_ref.at[pl.ds((4 * px + 2 * py + pc) * m_per, m_per), :]

        def copy(k, blk, to, src=None):
            return pltpu.make_async_remote_copy(
                src_ref=rows(*blk) if src is None else src, dst_ref=rows(*blk),
                send_sem=send_sems.at[k], recv_sem=recv_sems.at[k],
                device_id=to, device_id_type=MESH)

        mine = pltpu.make_async_copy(x_ref, rows(*me), local_sem)
        mine.start()
        first = [copy(0, me, sibling, src=x_ref)]
        first += [copy(1 + j, me, (*chip, c), src=x_ref) for j, chip in enumerate(chips)]
        for cp in first:
            cp.start()
        passed = [copy(4 + j, (*chip, c), sibling) for j, chip in enumerate(chips)]
        for j, chip in enumerate(chips):
            copy(1 + j, (*chip, c), me).wait_recv()
            passed[j].start()
        copy(0, sibling, me).wait_recv()
        for j, chip in enumerate(chips):
            copy(4 + j, (*chip, 1 - c), me).wait_recv()
        for cp in first + passed:
            cp.wait_send()
        mine.wait()

    return pl.pallas_call(
        body, name=name,
        out_shape=jax.ShapeDtypeStruct((8 * m_per, n), block.dtype),
        in_specs=[pl.BlockSpec(memory_space=pltpu.VMEM)],
        out_specs=pl.BlockSpec(memory_space=pltpu.VMEM),
        scratch_shapes=[pltpu.SemaphoreType.DMA((7,)), pltpu.SemaphoreType.DMA((7,)),
                        pltpu.SemaphoreType.DMA],
        compiler_params=_params(),
    )(block)


def _handshake(peers):
    barrier = pltpu.get_barrier_semaphore()
    for peer in peers:
        pl.semaphore_signal(barrier, inc=1, device_id=peer, device_id_type=MESH)
    pl.semaphore_wait(barrier, len(peers))


def _gather_weights(shards, by_cols, name, collective_id, after=()):
    n_arr = len(shards)

    def body(*refs):
        srcs, outs = refs[:n_arr], refs[n_arr + len(after):2 * n_arr + len(after)]
        send_sems, recv_sems, local_sems = refs[2 * n_arr + len(after):]
        x, y, c = _place()
        me_dev, sibling = (x, y, c), (x, y, 1 - c)
        chips = [(1 - x, y), (x, 1 - y), (1 - x, 1 - y)]
        me = 2 * x + y
        _handshake([sibling] + [(*chip, c) for chip in chips])

        def place(k, chip_idx, rows):
            if by_cols[k]:
                width = srcs[k].shape[1]
                return outs[k].at[rows, pl.ds(pl.multiple_of(chip_idx * width, 128), width)]
            return outs[k].at[chip_idx, rows]

        def copy(k, slot, chip_idx, half_sel, to, from_shard=False):
            half = srcs[k].shape[0] // 2
            rows = pl.ds(half_sel * half, half)
            dst = place(k, chip_idx, rows)
            return pltpu.make_async_remote_copy(
                src_ref=srcs[k].at[rows] if from_shard else dst, dst_ref=dst,
                send_sem=send_sems.at[6 * k + slot], recv_sem=recv_sems.at[6 * k + slot],
                device_id=to, device_id_type=MESH)

        own = [pltpu.make_async_copy(srcs[k], place(k, me, pl.ds(0, srcs[k].shape[0])), local_sems.at[k])
               for k in range(n_arr)]
        for cp in own:
            cp.start()
        sent = []
        for k in range(n_arr):
            for j, chip in enumerate(chips):
                sent.append(copy(k, j, me, c, (*chip, c), from_shard=True))
                sent[-1].start()
        for k in range(n_arr):
            for j, chip in enumerate(chips):
                chip_idx = 2 * chip[0] + chip[1]
                copy(k, j, chip_idx, c, me_dev).wait_recv()
                sent.append(copy(k, 3 + j, chip_idx, c, sibling))
                sent[-1].start()
        for k in range(n_arr):
            for j, chip in enumerate(chips):
                copy(k, 3 + j, 2 * chip[0] + chip[1], 1 - c, me_dev).wait_recv()
        for cp in sent:
            cp.wait_send()
        for cp in own:
            cp.wait()

    def gathered(k):
        r, cols = shards[k].shape
        return (r, N_CHIPS * cols) if by_cols[k] else (N_CHIPS, r, cols)

    return pl.kernel(
        body, name=name,
        out_type=[jax.ShapeDtypeStruct(gathered(k), shards[k].dtype) for k in range(n_arr)],
        mesh=plsc.ScalarSubcoreMesh(axis_name="sequencer", num_cores=1),
        scratch_types=[pltpu.SemaphoreType.DMA((6 * n_arr,)), pltpu.SemaphoreType.DMA((6 * n_arr,)),
                       pltpu.SemaphoreType.DMA((n_arr,))],
        compiler_params=pltpu.CompilerParams(collective_id=collective_id),
    )(*shards, *after)


def _rs_pair_exchange(grads, name):
    n_arr = len(grads)

    def body(*refs):
        srcs, outs = refs[:n_arr], refs[n_arr:2 * n_arr]
        send_sems, recv_sems = refs[2 * n_arr:]
        x, y, c = _place()
        cps = []
        for k in range(n_arr):
            half = srcs[k].shape[1] // 2
            cps.append(pltpu.make_async_remote_copy(
                src_ref=srcs[k].at[:, pl.ds((1 - c) * half, half)], dst_ref=outs[k],
                send_sem=send_sems.at[k], recv_sem=recv_sems.at[k],
                device_id=(x, y, 1 - c), device_id_type=MESH))
            cps[-1].start()
        for cp in cps:
            cp.wait_recv()
        for cp in cps:
            cp.wait_send()

    return pl.pallas_call(
        body, name=name,
        out_shape=[jax.ShapeDtypeStruct((g.shape[0], g.shape[1] // 2, g.shape[2]), g.dtype) for g in grads],
        in_specs=[ANY] * n_arr, out_specs=[ANY] * n_arr,
        scratch_shapes=[pltpu.SemaphoreType.DMA((n_arr,)), pltpu.SemaphoreType.DMA((n_arr,))],
        compiler_params=_params(),
    )(*grads)


def _rs_chip_exchange(sums, name, collective_id):
    n_arr = len(sums)

    def body(*refs):
        srcs, outs = refs[:n_arr], refs[n_arr:2 * n_arr]
        send_sems, recv_sems = refs[2 * n_arr:]
        x, y, c = _place()
        chips = [(1 - x, y), (x, 1 - y), (1 - x, 1 - y)]
        _handshake([(*chip, c) for chip in chips])
        cps = []
        for k in range(n_arr):
            for j, chip in enumerate(chips):
                cps.append(pltpu.make_async_remote_copy(
                    src_ref=srcs[k].at[2 * chip[0] + chip[1]], dst_ref=outs[k].at[j],
                    send_sem=send_sems.at[3 * k + j], recv_sem=recv_sems.at[3 * k + j],
                    device_id=(*chip, c), device_id_type=MESH))
                cps[-1].start()
        for cp in cps:
            cp.wait_recv()
        for cp in cps:
            cp.wait_send()

    return pl.kernel(
        body, name=name,
        out_type=[jax.ShapeDtypeStruct((3,) + s.shape[1:], s.dtype) for s in sums],
        mesh=plsc.ScalarSubcoreMesh(axis_name="sequencer", num_cores=1),
        scratch_types=[pltpu.SemaphoreType.DMA((3 * n_arr,)), pltpu.SemaphoreType.DMA((3 * n_arr,))],
        compiler_params=pltpu.CompilerParams(collective_id=collective_id),
    )(*sums)


def _rs_share(totals, name):
    n_arr = len(totals)

    def body(*refs):
        outs = refs[n_arr:2 * n_arr]
        send_sems, recv_sems = refs[2 * n_arr:]
        x, y, c = _place()

        def half_rows(k, sel):
            return outs[k].at[sel]

        cps = []
        for k in range(n_arr):
            cps.append(pltpu.make_async_remote_copy(
                src_ref=half_rows(k, c), dst_ref=half_rows(k, c), send_sem=send_sems.at[k], recv_sem=recv_sems.at[k],
                device_id=(x, y, 1 - c), device_id_type=MESH))
            cps[-1].start()
        for k in range(n_arr):
            pltpu.make_async_remote_copy(
                src_ref=half_rows(k, c), dst_ref=half_rows(k, 1 - c), send_sem=send_sems.at[k],
                recv_sem=recv_sems.at[k], device_id=(x, y, 1 - c), device_id_type=MESH).wait_recv()
        for cp in cps:
            cp.wait_send()

    shared = pl.pallas_call(
        body, name=name,
        out_shape=[jax.ShapeDtypeStruct(t.shape, t.dtype) for t in totals],
        in_specs=[ANY] * n_arr, out_specs=[ANY] * n_arr,
        input_output_aliases={k: k for k in range(n_arr)},
        scratch_shapes=[pltpu.SemaphoreType.DMA((n_arr,)), pltpu.SemaphoreType.DMA((n_arr,))],
        compiler_params=_params(),
    )(*totals)
    return [t.reshape(2 * t.shape[1], t.shape[2]) for t in shared]


def _pair_add(grad, recv, c_idx, name):
    n, r, cols = grad.shape
    half = r // 2
    rows = half // 2

    def body(_, g_ref, r_ref, o_ref):
        o_ref[...] = (g_ref[...].astype(F32) + r_ref[...].astype(F32)).astype(o_ref.dtype)

    return pl.pallas_call(
        body, name=name,
        grid_spec=pltpu.PrefetchScalarGridSpec(
            num_scalar_prefetch=1, grid=(n, 2),
            in_specs=[pl.BlockSpec((None, None, rows, cols), lambda s, i, ci: (s, ci[0], i, 0)),
                      pl.BlockSpec((None, rows, cols), lambda s, i, ci: (s, i, 0))],
            out_specs=pl.BlockSpec((None, rows, cols), lambda s, i, ci: (s, i, 0))),
        out_shape=jax.ShapeDtypeStruct((n, half, cols), BF16),
        compiler_params=_params(2),
    )(c_idx, grad.reshape(n, 2, half, cols), recv)


def _chip_add(sums, recv, chip_and_core, name, after=()):
    _, half, cols = sums.shape
    rows = half // 2

    def body(_, s_ref, r0_ref, r1_ref, r2_ref, o_ref):
        o_ref[...] = ((s_ref[...].astype(F32) + r0_ref[...].astype(F32))
                      + r1_ref[...].astype(F32)) + r2_ref[...].astype(F32)

    def recv_spec(j):
        return pl.BlockSpec((None, rows, cols), lambda i, ci: (j, i, 0))

    return pl.pallas_call(
        _ordered(body, 5, after), name=name,
        grid_spec=pltpu.PrefetchScalarGridSpec(
            num_scalar_prefetch=1, grid=(2,),
            in_specs=[pl.BlockSpec((None, rows, cols), lambda i, ci: (ci[0], i, 0)),
                      recv_spec(0), recv_spec(1), recv_spec(2)] + [ANY] * len(after),
            out_specs=pl.BlockSpec((None, rows, cols), lambda i, ci: (ci[1], i, 0))),
        out_shape=jax.ShapeDtypeStruct((2, half, cols), F32),
        compiler_params=_params(1),
    )(chip_and_core, sums, recv, recv, recv, *after)


def _rms(x):
    return lax.rsqrt(jnp.mean(x * x, axis=-1, keepdims=True) + NORM_EPS)


def _norm_mod(x, mod, name, prev=None):
    s_len, d = x.shape
    tm = TOKEN_TILE

    def body(*refs):
        if prev is None:
            x_ref, mod_ref, h_ref, ht_ref = refs
            xv = x_ref[...]
        else:
            x_ref, y_ref, modp_ref, mod_ref, xo_ref, h_ref, ht_ref = refs
            xv = x_ref[...] + prev[2] * modp_ref[2:3, :] * y_ref[...]
            xo_ref[...] = xv
        n = (xv * _rms(xv)) * mod_ref[3:4, :]
        h = n * (1.0 + mod_ref[1:2, :]) + mod_ref[0:1, :]
        h_ref[...] = h.astype(BF16)
        ht_ref[...] = h.T.astype(BF16)

    tile = pl.BlockSpec((tm, d), lambda i: (i, 0))
    small = pl.BlockSpec((8, d), lambda i: (0, 0))
    h_specs = [tile, pl.BlockSpec((d, tm), lambda i: (0, i))]
    h_shapes = [jax.ShapeDtypeStruct((s_len, d), BF16), jax.ShapeDtypeStruct((d, s_len), BF16)]
    if prev is None:
        return pl.pallas_call(
            body, name=name, grid=(s_len // tm,), in_specs=[tile, small], out_specs=h_specs, out_shape=h_shapes,
            compiler_params=_params(1))(x, mod)
    return pl.pallas_call(
        body, name=name, grid=(s_len // tm,), in_specs=[tile, tile, small, small],
        out_specs=[tile] + h_specs, out_shape=[jax.ShapeDtypeStruct((s_len, d), F32)] + h_shapes,
        compiler_params=_params(1))(x, prev[0], prev[1], mod)


def _norm_bwd(dh, x, mod, dxo, y_raw, coef, name, after=(), prev=None):
    s_len, d = x.shape
    tm = TOKEN_TILE

    def body(*refs):
        if prev is None:
            dh_ref, x_ref, mod_ref, dxo_ref, y_ref, dx_ref, st_ref = refs
        else:
            dh_ref, x_ref, mod_ref, dxo_ref, y_ref, modp_ref, dx_ref, st_ref, dyp_ref = refs

        @pl.when(pl.program_id(0) == 0)
        def _():
            st_ref[...] = jnp.zeros_like(st_ref)

        xv, dhv, dxov = x_ref[...], dh_ref[...], dxo_ref[...]
        r = _rms(xv)
        xh = xv * r
        gain, scale = mod_ref[3:4, :], mod_ref[1:2, :]
        dn = dhv * (1.0 + scale)
        dxh = dn * gain
        dx = dxov + r * (dxh - xh * jnp.mean(dxh * xh, axis=-1, keepdims=True))
        dx_ref[...] = dx
        if prev is not None:
            dyp_ref[...] = (prev[1] * modp_ref[2:3, :] * dx).astype(BF16)
        st_ref[0:1, :] += jnp.sum(dhv, axis=0, keepdims=True)
        st_ref[1:2, :] += jnp.sum(dhv * (xh * gain), axis=0, keepdims=True)
        st_ref[2:3, :] += coef * jnp.sum(y_ref[...].astype(F32) * dxov, axis=0, keepdims=True)
        st_ref[3:4, :] += jnp.sum(dn * xh, axis=0, keepdims=True)

    tile = pl.BlockSpec((tm, d), lambda i: (i, 0))
    small = pl.BlockSpec((8, d), lambda i: (0, 0))
    operands = [dh, x, mod, dxo, y_raw] + ([] if prev is None else [prev[0]])
    in_specs = [tile, tile, small, tile, tile] + ([] if prev is None else [small])
    out_specs = [tile, small] + ([] if prev is None else [tile])
    out_shape = [jax.ShapeDtypeStruct((s_len, d), F32), jax.ShapeDtypeStruct((8, d), F32)]
    if prev is not None:
        out_shape.append(jax.ShapeDtypeStruct((s_len, d), BF16))
    return pl.pallas_call(
        _ordered(body, len(operands), after), name=name, grid=(s_len // tm,),
        in_specs=in_specs + [ANY] * len(after), out_specs=out_specs, out_shape=out_shape,
        compiler_params=_params(1),
    )(*operands, *after)


def _loss_grad(x, y, mod, target, name):
    s_len, d = x.shape
    tm = TOKEN_TILE

    def body(x_ref, y_ref, mod_ref, t_ref, do_ref, dy_ref, part_ref):
        @pl.when(pl.program_id(0) == 0)
        def _():
            part_ref[...] = jnp.zeros_like(part_ref)

        half_gate = 0.5 * mod_ref[2:3, :]
        err = (x_ref[...] + half_gate * y_ref[...]) - t_ref[...]
        do = err * (1.0 / d)
        do_ref[...] = do
        dy_ref[...] = (half_gate * do).astype(BF16)
        sq = err * err
        part_ref[...] += jnp.sum(sq.reshape(tm // 8, 8, d), axis=0)

    tile = pl.BlockSpec((tm, d), lambda i: (i, 0))
    small = pl.BlockSpec((8, d), lambda i: (0, 0))
    return pl.pallas_call(
        body, name=name, grid=(s_len // tm,),
        in_specs=[tile, tile, small, tile],
        out_specs=[tile, tile, small],
        out_shape=[jax.ShapeDtypeStruct((s_len, d), F32), jax.ShapeDtypeStruct((s_len, d), BF16),
                   jax.ShapeDtypeStruct((8, d), F32)],
        compiler_params=_params(1),
    )(x, y, mod, target)


def _adamw_math(w, g, m, v):
    m = ADAM_B1 * m + (1.0 - ADAM_B1) * g
    v = ADAM_B2 * v + (1.0 - ADAM_B2) * (g * g)
    m_hat = m / (1.0 - ADAM_B1 ** ADAM_STEP)
    v_hat = v / (1.0 - ADAM_B2 ** ADAM_STEP)
    delta = -ADAM_LR * (m_hat / (jnp.sqrt(v_hat) + ADAM_EPS) + ADAM_WD * w)
    return delta, m, v


def _adamw(w, g, m, v, name, after=()):
    r, cols = w.shape
    tr = r // 8 if r % 64 == 0 else r

    def body(w_ref, g_ref, m_ref, v_ref, go_ref, d_ref, nm_ref, nv_ref):
        gv = g_ref[...]
        go_ref[...] = gv
        d_ref[...], nm_ref[...], nv_ref[...] = _adamw_math(w_ref[...], gv, m_ref[...], v_ref[...])

    tile = pl.BlockSpec((tr, cols), lambda i: (i, 0))
    shape = jax.ShapeDtypeStruct((r, cols), F32)
    return pl.pallas_call(
        _ordered(body, 4, after), name=name, grid=(r // tr,),
        in_specs=[tile] * 4 + [ANY] * len(after), out_specs=[tile] * 4, out_shape=[shape] * 4,
        compiler_params=_params(1),
    )(w, g, m, v, *after)


def _in_parts(tm, n_qkv, n_rest):
    def part(lo, n_blk):
        return pl.BlockSpec((tm, IN_BLOCK), lambda i, j: (i, jnp.clip(j - lo, 0, n_blk - 1)))
    return [part(0, n_qkv), part(n_qkv, n_qkv), part(2 * n_qkv, n_qkv), part(3 * n_qkv, n_rest)]


def _pick_part(j, n_qkv, refs, fn):
    bounds = [0, n_qkv, 2 * n_qkv, 3 * n_qkv]
    for p, ref in enumerate(refs):
        inside = j >= bounds[p]
        if p + 1 < len(refs):
            inside = inside & (j < bounds[p + 1])
        pl.when(inside)(lambda ref=ref: fn(ref))


def _rows(base, count, stride):
    return pl.ds(base, count) if stride == 1 else pl.ds(base, count, stride=stride)


REORDER_STRIDE = 4


def _reorder_plan(dil, parts=1):
    inner = min(dil, REORDER_STRIDE)
    return inner, dil // inner, SLAB // parts // inner, SLAB // dil


def _to_residue_order(dst, src, dil, tmp, part=0, parts=1):
    inner, outer, big, seg = _reorder_plan(dil, parts)
    piece = seg // parts
    if outer == 1:
        for r in range(dil):
            dst[pl.ds(r * seg + part * piece, piece), :] = src[_rows(r, piece, dil), :].astype(dst.dtype)
        return
    for b in range(inner):
        tmp[pl.ds(b * big, big), :] = src[_rows(b, big, inner), :]
    for a in range(outer):
        for b in range(inner):
            dst[pl.ds((inner * a + b) * seg + part * piece, piece), :] = (
                tmp[_rows(b * big + a, piece, outer), :].astype(dst.dtype))


def _to_token_order(dst, src, dil, tmp):
    inner, outer, big, seg = _reorder_plan(dil)
    if outer == 1:
        for r in range(dil):
            dst[_rows(r, seg, dil), :] = src[pl.ds(r * seg, seg), :]
        return
    for a in range(outer):
        for b in range(inner):
            tmp[_rows(b * big + a, seg, outer), :] = src[pl.ds((inner * a + b) * seg, seg), :]
    for b in range(inner):
        dst[_rows(b, big, inner), :] = tmp[pl.ds(b * big, big), :]


def _in_proj(h, w, q_norm, k_norm, name):
    s_len, d = h.shape
    tm = PROJ_TILE
    assert tm == SLAB and IN_BLOCK == HEADS * HEAD_DIM
    steps = w.shape[1] // IN_BLOCK
    n_qkv = 3 * QKV // IN_BLOCK
    halves = 2
    rows = [pl.ds(p * (tm // halves), tm // halves) for p in range(halves)]

    def body(h_ref, w_ref, qn_ref, kn_ref, qkv_ref, rest_ref, hat_ref, tok_s, tmp_s):
        j = pl.program_id(1)
        res = [_dot(h_ref[rows[p], :], w_ref[...]) for p in range(halves)]

        def emit(sect, gi):
            dil = DILATIONS[gi]
            for p in range(halves):
                qkv_ref[rows[p], :] = res[p]
                for hh in range(HEADS):
                    cols = slice(hh * HEAD_DIM, (hh + 1) * HEAD_DIM)
                    x = res[p][:, cols]
                    if sect < 2:
                        x = (x * _rms(x)) * (qn_ref if sect == 0 else kn_ref)[...]
                    tok_s[...] = x
                    _to_residue_order(hat_ref.at[:, cols], tok_s, dil, tmp_s, p, halves)

        for sect in range(3):
            for gi in range(N_GROUPS):
                pl.when(j == sect * N_GROUPS + gi)(lambda sect=sect, gi=gi: emit(sect, gi))

        @pl.when(j >= n_qkv)
        def _():
            for p in range(halves):
                rest_ref[rows[p], :] = res[p].astype(BF16)

    qkv_blk = pl.BlockSpec((tm, IN_BLOCK), lambda i, j: (i, jnp.minimum(j, n_qkv - 1)))
    small = pl.BlockSpec((1, HEAD_DIM), lambda i, j: (0, 0))
    return pl.pallas_call(
        body, name=name, grid=(s_len // tm, steps),
        in_specs=[pl.BlockSpec((tm, d), lambda i, j: (i, 0)), pl.BlockSpec((d, IN_BLOCK), lambda i, j: (0, j)),
                  small, small],
        out_specs=[qkv_blk, pl.BlockSpec((tm, IN_BLOCK), lambda i, j: (i, jnp.maximum(j - n_qkv, 0))), qkv_blk],
        out_shape=[jax.ShapeDtypeStruct((s_len, 3 * QKV), F32),
                   jax.ShapeDtypeStruct((s_len, w.shape[1] - 3 * QKV), BF16),
                   jax.ShapeDtypeStruct((s_len, 3 * QKV), BF16)],
        scratch_shapes=[pltpu.VMEM((tm // halves, HEAD_DIM), F32)] * 2,
        compiler_params=_params(2),
    )(h, w, q_norm, k_norm)


def _in_proj_bwd(dq, dk, dv, drest, w, name, after=()):
    s_len = dq.shape[0]
    d = w.shape[0]
    tm = PROJ_TILE
    steps = w.shape[1] // IN_BLOCK
    n_qkv = QKV // IN_BLOCK

    def body(dq_ref, dk_ref, dv_ref, dr_ref, w_ref, o_ref, acc_ref):
        j = pl.program_id(1)

        @pl.when(j == 0)
        def _():
            acc_ref[...] = jnp.zeros_like(acc_ref)

        def add(a_ref):
            acc_ref[...] += _dot_nt(a_ref[...], w_ref[...])

        _pick_part(j, n_qkv, [dq_ref, dk_ref, dv_ref, dr_ref], add)

        @pl.when(j == steps - 1)
        def _():
            o_ref[...] = acc_ref[...]

    return pl.pallas_call(
        _ordered(body, 5, after), name=name, grid=(s_len // tm, steps),
        in_specs=(_in_parts(tm, n_qkv, steps - 3 * n_qkv) + [pl.BlockSpec((d, IN_BLOCK), lambda i, j: (0, j))]
                  + [ANY] * len(after)),
        out_specs=pl.BlockSpec((tm, d), lambda i, j: (i, 0)),
        out_shape=jax.ShapeDtypeStruct((s_len, d), F32),
        scratch_shapes=[pltpu.VMEM((tm, d), F32)],
        compiler_params=_params(2),
    )(dq, dk, dv, drest, w, *after)


def _wgrad(x, y, x_spec, y_spec, out_shape, out_spec, acc_shape, n_chunks, name, x_transposed=False, after=()):
    s_len = y.shape[-2]
    ts = WGRAD_TILE
    steps = s_len // ts

    def body(x_ref, y_ref, o_ref, acc_ref):
        s = pl.program_id(1)

        @pl.when(s == 0)
        def _():
            acc_ref[...] = jnp.zeros_like(acc_ref)

        acc_ref[...] += (_dot if x_transposed else _dot_tn)(x_ref[...], y_ref[...])

        @pl.when(s == steps - 1)
        def _():
            o_ref[...] = acc_ref[...].astype(o_ref.dtype)

    return pl.pallas_call(
        _ordered(body, 2, after), name=name, grid=(n_chunks, steps),
        in_specs=[x_spec(ts), y_spec(ts)] + [ANY] * len(after), out_specs=out_spec,
        out_shape=jax.ShapeDtypeStruct(out_shape, BF16),
        scratch_shapes=[pltpu.VMEM(acc_shape, F32)],
        compiler_params=_params(2),
    )(x, y, *after)


def _pieces(width, piece=256):
    return [slice(a, min(a + piece, width)) for a in range(0, width, piece)]


def _ffn_fwd(h, w_gate, w_up, w_down, name):
    s_len, d = h.shape
    n_chunks, _, fs = w_gate.shape
    tm = FFN_TILE

    def body(h_ref, wg_ref, wu_ref, wd_ref, g_ref, u_ref, y_ref):
        j = pl.program_id(1)
        hv = h_ref[...]
        pieces = _pieces(fs)
        first = lambda cols: (_dot(hv, wg_ref[:, cols]), _dot(hv, wu_ref[:, cols]))
        total = None
        ahead = first(pieces[0])
        for k, cols in enumerate(pieces):
            g, u = ahead
            if k + 1 < len(pieces):
                ahead = first(pieces[k + 1])
            g_ref[:, cols] = g.astype(BF16)
            u_ref[:, cols] = u.astype(BF16)
            act = (g * _sigmoid(g)) * u
            part = _dot(act.astype(BF16), wd_ref[cols, :])
            total = part if total is None else total + part

        @pl.when(j == 0)
        def _():
            y_ref[...] = total

        @pl.when(j > 0)
        def _():
            y_ref[...] += total

    tile = pl.BlockSpec((tm, d), lambda i, j: (i, 0))
    hid = pl.BlockSpec((None, tm, fs), lambda i, j: (j, i, 0))
    w_in_spec = pl.BlockSpec((None, d, fs), lambda i, j: (j, 0, 0))
    hid_shape = jax.ShapeDtypeStruct((n_chunks, s_len, fs), BF16)
    return pl.pallas_call(
        body, name=name, grid=(s_len // tm, n_chunks),
        in_specs=[tile, w_in_spec, w_in_spec, pl.BlockSpec((None, fs, d), lambda i, j: (j, 0, 0))],
        out_specs=[hid, hid, tile],
        out_shape=[hid_shape, hid_shape, jax.ShapeDtypeStruct((s_len, d), F32)],
        compiler_params=_params(2),
    )(h, w_gate, w_up, w_down)


def _ffn_bwd(dy, g_pre, u_pre, w_gate, w_up, w_down, name):
    s_len, d = dy.shape
    n_chunks, _, fs = w_gate.shape
    tm = FFN_TILE

    def body(dy_ref, g_ref, u_ref, wg_ref, wu_ref, wd_ref, dh_ref, dg_ref, du_ref, a_ref):
        j = pl.program_id(1)
        dyv = dy_ref[...]
        pieces = _pieces(fs)
        first = lambda cols: _dot_nt(dyv, wd_ref[cols, :])
        total = None
        ahead = first(pieces[0])
        for k, cols in enumerate(pieces):
            da = ahead
            if k + 1 < len(pieces):
                ahead = first(pieces[k + 1])
            g = g_ref[:, cols].astype(F32)
            u = u_ref[:, cols].astype(F32)
            sg = _sigmoid(g)
            silu = g * sg
            dg = (da * u * (sg * (1.0 + g * (1.0 - sg)))).astype(BF16)
            du = (da * silu).astype(BF16)
            dg_ref[:, cols] = dg
            du_ref[:, cols] = du
            a_ref[:, cols] = (silu * u).astype(BF16)
            part = _dot_nt(dg, wg_ref[:, cols]) + _dot_nt(du, wu_ref[:, cols])
            total = part if total is None else total + part

        @pl.when(j == 0)
        def _():
            dh_ref[...] = total

        @pl.when(j > 0)
        def _():
            dh_ref[...] += total

    tile = pl.BlockSpec((tm, d), lambda i, j: (i, 0))
    hid = pl.BlockSpec((None, tm, fs), lambda i, j: (j, i, 0))
    w_in_spec = pl.BlockSpec((None, d, fs), lambda i, j: (j, 0, 0))
    hid_shape = jax.ShapeDtypeStruct((n_chunks, s_len, fs), BF16)
    return pl.pallas_call(
        body, name=name, grid=(s_len // tm, n_chunks),
        in_specs=[tile, hid, hid, w_in_spec, w_in_spec, pl.BlockSpec((None, fs, d), lambda i, j: (j, 0, 0))],
        out_specs=[tile, hid, hid, hid],
        out_shape=[jax.ShapeDtypeStruct((s_len, d), F32), hid_shape, hid_shape, hid_shape],
        compiler_params=_params(2),
    )(dy, g_pre, u_pre, w_gate, w_up, w_down)


def _ffn_wgrads(ht, dg, du, act, dy, tag, after=()):
    n_chunks, s_len, fs = dg.shape
    d = ht.shape[0]
    tok = lambda ts: pl.BlockSpec((ts, d), lambda c, s: (s, 0))
    tok_t = lambda ts: pl.BlockSpec((d, ts), lambda c, s: (0, s))
    hid = lambda ts: pl.BlockSpec((None, ts, fs), lambda c, s: (c, s, 0))
    d_up = pl.BlockSpec((None, d, fs), lambda c, s: (c, 0, 0))
    d_down = pl.BlockSpec((None, fs, d), lambda c, s: (c, 0, 0))
    dwg = _wgrad(ht, dg, tok_t, hid, (n_chunks, d, fs), d_up, (d, fs), n_chunks, tag + "_dwg", True, after)
    dwu = _wgrad(ht, du, tok_t, hid, (n_chunks, d, fs), d_up, (d, fs), n_chunks, tag + "_dwu", True, after)
    dwd = _wgrad(act, dy, hid, tok, (n_chunks, fs, d), d_down, (fs, d), n_chunks, tag + "_dwd", False, after)
    return dwg, dwu, dwd


def _band_bias():
    qi = lax.broadcasted_iota(jnp.int32, (ATTN_BLOCK, 2 * ATTN_BLOCK), 0)
    kj = lax.broadcasted_iota(jnp.int32, (ATTN_BLOCK, 2 * ATTN_BLOCK), 1)
    band = (kj >= qi) & (kj <= qi + ATTN_BLOCK)
    return jnp.where(band, 0.0, NEG), jnp.where(band & (kj >= ATTN_BLOCK), 0.0, NEG)


def _qkv_specs(slab_of, sections):
    def spec(sect, back):
        return pl.BlockSpec((SLAB, HEAD_DIM),
                            lambda h, s, g: (jnp.maximum(slab_of(s) - back, 0), (sect * N_GROUPS + g) * HEADS + h))
    return [spec(sect, back) for sect, back in sections]


HAT_BLOCKS = [(0, 0), (1, 0), (2, 0), (1, 1), (2, 1)]


def _stage_keys(k_ref, v_ref, kp_ref, vp_ref, kbuf, vbuf, dil, n):
    run = SLAB // dil
    for r in range(dil):
        own, before = pl.ds(r * run, run), pl.ds(2 * r * run, run)
        kbuf[pl.ds((2 * r + 1) * run, run), :] = k_ref[own, :]
        vbuf[pl.ds((2 * r + 1) * run, run), :] = v_ref[own, :]

        @pl.when(n > 0)
        def _():
            kbuf[before, :] = kp_ref[own, :]
            vbuf[before, :] = vp_ref[own, :]

        @pl.when(n == 0)
        def _():
            kbuf[before, :] = jnp.zeros((run, HEAD_DIM), BF16)
            vbuf[before, :] = jnp.zeros((run, HEAD_DIM), BF16)


def _for_each_tile(dil, n, first_fn, rest_fn):
    run = SLAB // dil
    bias, first_bias = _band_bias()
    tiles = []
    for jj in range(run // ATTN_BLOCK):
        start = jj * ATTN_BLOCK
        tile_bias = jnp.where(n == 0, first_bias, bias) if jj == 0 else bias
        for r in range(dil):
            tiles.append((pl.ds(r * run + start, ATTN_BLOCK),
                          pl.ds((2 * r + 1) * run - ATTN_BLOCK + start, 2 * ATTN_BLOCK), tile_bias))
    ahead = first_fn(*tiles[0])
    for t, tile in enumerate(tiles):
        begun = ahead
        if t + 1 < len(tiles):
            ahead = first_fn(*tiles[t + 1])
        rest_fn(*tile, begun)


def _attn_fwd(hat, name):
    s_len = hat.shape[0]
    e = HEAD_DIM
    n_slabs = s_len // SLAB

    def body(q_ref, k_ref, v_ref, kp_ref, vp_ref, o_ref, lse_ref, kbuf, vbuf, m_s, l_s, acc_s, m_p, l_p, acc_p, tmp_s):
        n, grp = pl.program_id(1), pl.program_id(2)

        def run(gi, dil):
            _stage_keys(k_ref, v_ref, kp_ref, vp_ref, kbuf, vbuf, dil, n)

            def scores(q_rows, kv_rows, bias):
                return _dot_nt(q_ref[q_rows, :], kbuf[kv_rows, :])

            def rest(q_rows, kv_rows, bias, qk):
                s = qk * ATTN_SCALE + bias
                m = jnp.max(s, axis=-1, keepdims=True)
                p = jnp.exp(s - m)
                m_p[q_rows, :] = jnp.broadcast_to(m, (ATTN_BLOCK, e))
                l_p[q_rows, :] = jnp.broadcast_to(jnp.sum(p, axis=-1, keepdims=True), (ATTN_BLOCK, e))
                acc_p[q_rows, :] = _dot(p.astype(BF16), vbuf[kv_rows, :])

            _for_each_tile(dil, n, scores, rest)
            _to_token_order(m_s.at[gi], m_p, dil, tmp_s)
            _to_token_order(l_s.at[gi], l_p, dil, tmp_s)
            _to_token_order(acc_s.at[gi], acc_p, dil, tmp_s)

        for gi, dil in enumerate(DILATIONS):
            pl.when(grp == gi)(lambda gi=gi, dil=dil: run(gi, dil))

        @pl.when(grp == N_GROUPS - 1)
        def _():
            m_all = jnp.maximum(jnp.maximum(m_s[0], m_s[1]), m_s[2])
            den = jnp.zeros((SLAB, e), F32)
            num = jnp.zeros((SLAB, e), F32)
            for gi in range(N_GROUPS):
                w = jnp.exp(m_s[gi] - m_all)
                den += l_s[gi] * w
                num += acc_s[gi] * w
            o_ref[...] = num / den
            lse_ref[...] = m_all + jnp.log(den)

    out = pl.BlockSpec((SLAB, e), lambda h, n, g: (n, h))
    return pl.pallas_call(
        body, name=name, grid=(HEADS, n_slabs, N_GROUPS),
        in_specs=_qkv_specs(lambda n: n, HAT_BLOCKS),
        out_specs=[out, out],
        out_shape=[jax.ShapeDtypeStruct((s_len, HEADS * e), F32)] * 2,
        scratch_shapes=[pltpu.VMEM((2 * SLAB, e), BF16), pltpu.VMEM((2 * SLAB, e), BF16),
                        pltpu.VMEM((N_GROUPS, SLAB, e), F32), pltpu.VMEM((N_GROUPS, SLAB, e), F32),
                        pltpu.VMEM((N_GROUPS, SLAB, e), F32)]
        + [pltpu.VMEM((SLAB, e), F32)] * 4,
        compiler_params=_params(3),
    )(hat, hat, hat, hat, hat)


def _attn_bwd(qkv, hat, d_out, out, lse, q_norm, k_norm, name):
    s_len = qkv.shape[0]
    e = HEAD_DIM
    n_slabs = s_len // SLAB

    def body(q_ref, k_ref, v_ref, kp_ref, vp_ref, qraw_ref, kraw_ref, do_ref, o_ref, lse_ref, qn_ref, kn_ref,
             dq_ref, dk_ref, dv_ref, st_ref, kbuf, vbuf, stat_s, dqs, dkb, dvb, dk_tok, dv_tok, carry,
             do_p, stat_p, dq_p, dk_p, dv_p, tmp_s, do16_p):
        head, step, grp = pl.program_id(0), pl.program_id(1), pl.program_id(2)
        n = n_slabs - 1 - step
        dkb[...] = jnp.zeros_like(dkb)
        dvb[...] = jnp.zeros_like(dvb)
        @pl.when(grp == 0)
        def _():
            lane = lax.broadcasted_iota(jnp.int32, (SLAB, e), 1)
            stat_s[...] = jnp.where(lane < e // 2, lse_ref[...],
                                    jnp.sum(do_ref[...] * o_ref[...], axis=-1, keepdims=True))

        @pl.when((head == 0) & (step == 0) & (grp == 0))
        def _():
            st_ref[...] = jnp.zeros_like(st_ref)

        def run(gi, dil):
            seg = SLAB // dil
            _stage_keys(k_ref, v_ref, kp_ref, vp_ref, kbuf, vbuf, dil, n)

            @pl.when(step == 0)
            def _():
                carry[gi] = jnp.zeros((2, SLAB, e), F32)

            _to_residue_order(do_p, do_ref, dil, tmp_s)
            do16_p[...] = do_p[...].astype(BF16)
            _to_residue_order(stat_p, stat_s, dil, tmp_s)

            def scores(q_rows, kv_rows, bias):
                return _dot_nt(q_ref[q_rows, :], kbuf[kv_rows, :]), _dot_nt(do16_p[q_rows, :], vbuf[kv_rows, :])

            def rest(q_rows, kv_rows, bias, begun):
                qk, dp = begun
                q = q_ref[q_rows, :]
                k = kbuf[kv_rows, :]
                stat = stat_p[q_rows, :]
                p = jnp.exp(qk * ATTN_SCALE + bias - stat[:, 0:1])
                ds = (p * (dp - stat[:, e // 2:e // 2 + 1]) * ATTN_SCALE).astype(BF16)
                dq_p[q_rows, :] = _dot(ds, k)
                dkb[kv_rows, :] += _dot_tn(ds, q)
                dvb[kv_rows, :] += _dot_tn(p.astype(BF16), do16_p[q_rows, :])

            _for_each_tile(dil, n, scores, rest)
            for r in range(dil):
                own, before = pl.ds((2 * r + 1) * seg, seg), pl.ds(2 * r * seg, seg)
                kept = pl.ds(r * seg, seg)
                dk_p[kept, :] = dkb[own, :] + carry.at[gi, 0][kept, :]
                dv_p[kept, :] = dvb[own, :] + carry.at[gi, 1][kept, :]
                carry.at[gi, 0][kept, :] = dkb[before, :]
                carry.at[gi, 1][kept, :] = dvb[before, :]
            _to_token_order(dqs, dq_p, dil, tmp_s)
            _to_token_order(dk_tok, dk_p, dil, tmp_s)
            _to_token_order(dv_tok, dv_p, dil, tmp_s)

            def norm_bwd(raw, gain, d_hat):
                r = _rms(raw)
                y = raw * r
                dy = d_hat * gain
                return r * (dy - y * jnp.mean(dy * y, axis=-1, keepdims=True)), jnp.sum(d_hat * y, axis=0, keepdims=True)

            dq, dqn = norm_bwd(qraw_ref[...], qn_ref[...], dqs[...])
            dk, dkn = norm_bwd(kraw_ref[...], kn_ref[...], dk_tok[...])
            dq_ref[...] = dq.astype(BF16)
            dk_ref[...] = dk.astype(BF16)
            dv_ref[...] = dv_tok[...].astype(BF16)
            st_ref[0:1, :] += dqn
            st_ref[1:2, :] += dkn

        for gi, dil in enumerate(DILATIONS):
            pl.when(grp == gi)(lambda gi=gi, dil=dil: run(gi, dil))

    slab_of = lambda s: n_slabs - 1 - s
    small = pl.BlockSpec((1, e), lambda h, s, g: (0, 0))
    head_blk = pl.BlockSpec((SLAB, e), lambda h, s, g: (slab_of(s), h))
    grad_blk = pl.BlockSpec((SLAB, e), lambda h, s, g: (slab_of(s), g * HEADS + h))
    grad_shape = jax.ShapeDtypeStruct((s_len, QKV), BF16)
    return pl.pallas_call(
        body, name=name, grid=(HEADS, n_slabs, N_GROUPS),
        in_specs=(_qkv_specs(slab_of, HAT_BLOCKS) + _qkv_specs(slab_of, [(0, 0), (1, 0)])
                  + [head_blk, head_blk, head_blk, small, small]),
        out_specs=[grad_blk, grad_blk, grad_blk, pl.BlockSpec((8, e), lambda h, s, g: (0, 0))],
        out_shape=[grad_shape, grad_shape, grad_shape, jax.ShapeDtypeStruct((8, e), F32)],
        scratch_shapes=[pltpu.VMEM((2 * SLAB, e), BF16), pltpu.VMEM((2 * SLAB, e), BF16), pltpu.VMEM((SLAB, e), F32),
                        pltpu.VMEM((SLAB, e), F32), pltpu.VMEM((2 * SLAB, e), F32), pltpu.VMEM((2 * SLAB, e), F32),
                        pltpu.VMEM((SLAB, e), F32), pltpu.VMEM((SLAB, e), F32),
                        pltpu.VMEM((N_GROUPS, 2, SLAB, e), F32)]
        + [pltpu.VMEM((SLAB, e), F32)] * 6 + [pltpu.VMEM((SLAB, e), BF16)],
        compiler_params=_params(3),
    )(hat, hat, hat, hat, hat, qkv, qkv, d_out, out, lse, q_norm, k_norm)


def _shift_rows(x, by, edge, forward):
    t_len = x.shape[0]
    row = lax.broadcasted_iota(jnp.int32, x.shape, 0)
    if forward:
        out = pltpu.roll(x, by, 0)
        for i in range(by):
            out = jnp.where(row == i, edge[8 - by + i:8 - by + i + 1, :], out)
    else:
        out = pltpu.roll(x, t_len - by, 0)
        for i in range(by):
            out = jnp.where(row == t_len - by + i, edge[i:i + 1, :], out)
    return out


def _mix_fwd(x, o, rest, mod, mod_next, conv_w, w_attn, w_conv, w_out, name):
    s_len, d = x.shape
    tm = MIX_TILE
    a_w = o.shape[1]

    def body(x_ref, o_ref, u_ref, b_ref, c_ref, ga_ref, gc_ref, mod_ref, modn_ref, cw_ref, wa_ref, wc_ref, wo_ref,
             xo_ref, z_ref, ya_ref, yc_ref, conv_ref, yb_ref, m_ref, o16_ref, h_ref, ht_ref, carry):
        @pl.when(pl.program_id(0) == 0)
        def _():
            carry[...] = jnp.zeros_like(carry)

        xc = c_ref[...].astype(F32) * u_ref[...].astype(F32)
        edge = carry[...]
        conv = (_shift_rows(xc, 2, edge, True) * cw_ref[0:1, :] + _shift_rows(xc, 1, edge, True) * cw_ref[1:2, :]
                + xc * cw_ref[2:3, :])
        carry[...] = xc[tm - 8:tm, :]
        yb = (b_ref[...].astype(F32) * conv).astype(BF16)
        o16 = o_ref[...].astype(BF16)
        ya = _dot(o16, wa_ref[...])
        yc = _dot(yb, wc_ref[...])
        merged = (_sigmoid(ga_ref[...].astype(F32)) * ya + _sigmoid(gc_ref[...].astype(F32)) * yc).astype(BF16)
        z = _dot(merged, wo_ref[...])
        xo = x_ref[...] + mod_ref[2:3, :] * z
        xo_ref[...] = xo
        hn = ((xo * _rms(xo)) * modn_ref[3:4, :]) * (1.0 + modn_ref[1:2, :]) + modn_ref[0:1, :]
        h_ref[...] = hn.astype(BF16)
        ht_ref[...] = hn.T.astype(BF16)
        z_ref[...] = z.astype(BF16)
        ya_ref[...] = ya.astype(BF16)
        yc_ref[...] = yc.astype(BF16)
        conv_ref[...] = conv.astype(BF16)
        yb_ref[...] = yb
        m_ref[...] = merged
        o16_ref[...] = o16

    tile = pl.BlockSpec((tm, d), lambda i: (i, 0))
    sect = lambda k: pl.BlockSpec((tm, d), lambda i: (i, k))
    att = pl.BlockSpec((tm, a_w), lambda i: (i, 0))
    const = lambda shape: pl.BlockSpec(shape, lambda i: (0, 0))
    f32_out = jax.ShapeDtypeStruct((s_len, d), F32)
    b16_out = jax.ShapeDtypeStruct((s_len, d), BF16)
    return pl.pallas_call(
        body, name=name, grid=(s_len // tm,),
        in_specs=[tile, att, sect(0), sect(1), sect(2), sect(3), sect(4), const((8, d)), const((8, d)), const((8, d)),
                  const((a_w, d)), const((d, d)), const((d, d))],
        out_specs=[tile] * 7 + [att, tile, pl.BlockSpec((d, tm), lambda i: (0, i))],
        out_shape=[f32_out] + [b16_out] * 6 + [jax.ShapeDtypeStruct((s_len, a_w), BF16), b16_out,
                                               jax.ShapeDtypeStruct((d, s_len), BF16)],
        scratch_shapes=[pltpu.VMEM((8, d), F32)],
        compiler_params=_params(1),
    )(x, o, rest, rest, rest, rest, rest, mod, mod_next, conv_w, w_attn, w_conv, w_out)


def _mix_bwd(dxo, ya, yc, conv, rest, mod, conv_w, w_attn, w_conv, w_out, a_w, name):
    s_len, d = dxo.shape
    tm = MIX_TILE
    n_tiles = s_len // tm

    def body(dxo_ref, ya_ref, yc_ref, conv_ref, u_ref, b_ref, c_ref, ga_ref, gc_ref, mod_ref, cw_ref,
             wa_ref, wc_ref, wo_ref, do_ref, drest_ref, dz_ref, dya_ref, dyc_ref, st_ref, carry):
        @pl.when(pl.program_id(0) == 0)
        def _():
            carry[...] = jnp.zeros_like(carry)
            st_ref[...] = jnp.zeros_like(st_ref)

        dz = (mod_ref[2:3, :] * dxo_ref[...]).astype(BF16)
        dz_ref[...] = dz
        dm = _dot_nt(dz, wo_ref[...])
        sa, sc = _sigmoid(ga_ref[...].astype(F32)), _sigmoid(gc_ref[...].astype(F32))
        dya = (dm * sa).astype(BF16)
        dyc = (dm * sc).astype(BF16)
        dya_ref[...] = dya
        dyc_ref[...] = dyc
        drest_ref[:, 3 * d:4 * d] = (dm * ya_ref[...].astype(F32) * (sa * (1.0 - sa))).astype(BF16)
        drest_ref[:, 4 * d:5 * d] = (dm * yc_ref[...].astype(F32) * (sc * (1.0 - sc))).astype(BF16)
        do_ref[...] = _dot_nt(dya, wa_ref[...])
        dyb = _dot_nt(dyc, wc_ref[...])
        drest_ref[:, d:2 * d] = (dyb * conv_ref[...].astype(F32)).astype(BF16)
        dconv = dyb * b_ref[...].astype(F32)
        edge = carry[...]
        sh1 = _shift_rows(dconv, 1, edge, False)
        sh2 = _shift_rows(dconv, 2, edge, False)
        carry[...] = dconv[0:8, :]
        dxc = dconv * cw_ref[2:3, :] + sh1 * cw_ref[1:2, :] + sh2 * cw_ref[0:1, :]
        u, c = u_ref[...].astype(F32), c_ref[...].astype(F32)
        xc = c * u
        drest_ref[:, 0:d] = (dxc * c).astype(BF16)
        drest_ref[:, 2 * d:3 * d] = (dxc * u).astype(BF16)
        st_ref[0:1, :] += jnp.sum(xc * sh2, axis=0, keepdims=True)
        st_ref[1:2, :] += jnp.sum(xc * sh1, axis=0, keepdims=True)
        st_ref[2:3, :] += jnp.sum(xc * dconv, axis=0, keepdims=True)

    rev = lambda i: n_tiles - 1 - i
    tile = pl.BlockSpec((tm, d), lambda i: (rev(i), 0))
    sect = lambda k: pl.BlockSpec((tm, d), lambda i: (rev(i), k))
    const = lambda shape: pl.BlockSpec(shape, lambda i: (0, 0))
    b16_out = jax.ShapeDtypeStruct((s_len, d), BF16)
    return pl.pallas_call(
        body, name=name, grid=(n_tiles,),
        in_specs=[tile, tile, tile, tile, sect(0), sect(1), sect(2), sect(3), sect(4), const((8, d)), const((8, d)),
                  const((a_w, d)), const((d, d)), const((d, d))],
        out_specs=[pl.BlockSpec((tm, a_w), lambda i: (rev(i), 0)), pl.BlockSpec((tm, 5 * d), lambda i: (rev(i), 0)),
                   tile, tile, tile, const((8, d))],
        out_shape=[jax.ShapeDtypeStruct((s_len, a_w), F32), jax.ShapeDtypeStruct((s_len, 5 * d), BF16),
                   b16_out, b16_out, b16_out, jax.ShapeDtypeStruct((8, d), F32)],
        scratch_shapes=[pltpu.VMEM((8, d), F32)],
        compiler_params=_params(1),
    )(dxo, ya, yc, conv, rest, rest, rest, rest, rest, mod, conv_w, w_attn, w_conv, w_out)


ADA_COLS = 128


def _ada_fwd(c_all, w_shard, b_shard, name):
    d, cols = w_shard.shape

    def body(c_ref, w_ref, b_ref, o_ref):
        cv = c_ref[...]
        o_ref[...] = jnp.dot(cv * _sigmoid(cv), w_ref[...], preferred_element_type=F32,
                             precision=lax.Precision.HIGHEST) + b_ref[...]

    return pl.pallas_call(
        body, name=name, grid=(cols // ADA_COLS,),
        in_specs=[pl.BlockSpec((8, d), lambda j: (0, 0)), pl.BlockSpec((d, ADA_COLS), lambda j: (0, j)),
                  pl.BlockSpec((1, ADA_COLS), lambda j: (0, j))],
        out_specs=pl.BlockSpec((8, ADA_COLS), lambda j: (0, j)),
        out_shape=jax.ShapeDtypeStruct((8, cols), F32),
        compiler_params=_params(1),
    )(c_all, w_shard, b_shard)


def _ada_bwd(c_all, dmod_shard, w, m, v, name):
    d, cols = w.shape

    def body(c_ref, dm_ref, w_ref, m_ref, v_ref, g_ref, d_ref, nm_ref, nv_ref):
        cv = c_ref[...]
        g = lax.dot_general(cv * _sigmoid(cv), dm_ref[...], (((0,), (0,)), ((), ())),
                            preferred_element_type=F32, precision=lax.Precision.HIGHEST)
        g_ref[...] = g
        d_ref[...], nm_ref[...], nv_ref[...] = _adamw_math(w_ref[...], g, m_ref[...], v_ref[...])

    blk = pl.BlockSpec((d, ADA_COLS), lambda j: (0, j))
    shape = jax.ShapeDtypeStruct((d, cols), F32)
    return pl.pallas_call(
        body, name=name, grid=(cols // ADA_COLS,),
        in_specs=[pl.BlockSpec((8, d), lambda j: (0, 0)), pl.BlockSpec((8, ADA_COLS), lambda j: (0, j)), blk, blk, blk],
        out_specs=[blk] * 4, out_shape=[shape] * 4,
        compiler_params=_params(1),
    )(c_all, dmod_shard, w, m, v)


def _small_update(parts, w, m, v, name):
    n = w.shape[1]

    def body(p_ref, w_ref, m_ref, v_ref, g_ref, d_ref, nm_ref, nv_ref):
        g = p_ref[0:1, :]
        for i in range(1, 8):
            g = g + p_ref[i:i + 1, :]
        g_ref[...] = g
        d_ref[...], nm_ref[...], nv_ref[...] = _adamw_math(w_ref[...], g, m_ref[...], v_ref[...])

    shape = jax.ShapeDtypeStruct((1, n), F32)
    return pl.pallas_call(body, name=name, out_shape=[shape] * 4, compiler_params=_params())(parts, w, m, v)


def _cols_to_shards(w, n):
    r, nc = w.shape
    return w.reshape(r, n, nc // n).transpose(1, 0, 2)


def kernel(x, c, w_ada, b_ada, norm_ffn1, ffn1_w_gate, ffn1_w_up, ffn1_w_down, norm_mix, w_in, q_norm, k_norm, conv_w, w_attn_branch, w_conv_branch, w_out, norm_ffn2, ffn2_w_gate, ffn2_w_up, ffn2_w_down, loss_target, m_w_ada, m_b_ada, m_norm_ffn1, m_ffn1_w_gate, m_ffn1_w_up, m_ffn1_w_down, m_norm_mix, m_w_in, m_q_norm, m_k_norm, m_conv_w, m_w_attn_branch, m_w_conv_branch, m_w_out, m_norm_ffn2, m_ffn2_w_gate, m_ffn2_w_up, m_ffn2_w_down, v_w_ada, v_b_ada, v_norm_ffn1, v_ffn1_w_gate, v_ffn1_w_up, v_ffn1_w_down, v_norm_mix, v_w_in, v_q_norm, v_k_norm, v_conv_w, v_w_attn_branch, v_w_conv_branch, v_w_out, v_norm_ffn2, v_ffn2_w_gate, v_ffn2_w_up, v_ffn2_w_down):
    ix, iy, ic = _place()
    chip = 2 * ix + iy
    me = 4 * ix + 2 * iy + ic
    xs = x[0]
    target = loss_target[0]
    s_len, d = xs.shape
    ada_cols = w_ada.shape[2]
    conv_cols = conv_w.shape[2]

    conv_rows = jnp.zeros((8, conv_cols), F32).at[0:3].set(conv_w[0])
    small_in = jnp.concatenate([jnp.broadcast_to(c, (8, d)), conv_rows], axis=1)
    small_all = _allgather8(small_in, "gather_c").reshape(8, 8, d + conv_cols)
    c_all = small_all[:, 0, :d]
    conv_full = small_all[0::2, 0:3, d:].transpose(1, 0, 2).reshape(3, N_CHIPS * conv_cols)
    conv_pad = jnp.zeros((8, N_CHIPS * conv_cols), F32).at[0:3].set(conv_full)
    b_shard = lax.dynamic_slice(b_ada, (0, chip * ada_cols), (1, ada_cols))
    mod_part = _ada_fwd(c_all, w_ada[0], b_shard, "ada_fwd")
    mod_all = _allgather8(mod_part, "gather_mod").reshape(N_CHIPS, 2, 8, ada_cols)[:, 0]
    mod_mine = lax.dynamic_slice(mod_all, (0, me, 0), (N_CHIPS, 1, ada_cols)).reshape(9, d)

    def mod_rows(i, gain):
        return jnp.zeros((8, d), F32).at[0:3].set(mod_mine[3 * i:3 * i + 3]).at[3:4].set(gain)

    mod1, mod2, mod3 = mod_rows(0, norm_ffn1), mod_rows(1, norm_mix), mod_rows(2, norm_ffn2)

    to16 = lambda w: w[0].astype(BF16)
    wg1, wu1, wd1 = _gather_weights([to16(ffn1_w_gate), to16(ffn1_w_up), to16(ffn1_w_down)], [False] * 3,
                                    "gather_ffn1", 1)
    h1, h1t = _norm_mod(xs, mod1, "norm1")
    (w_in_full,) = _gather_weights([to16(w_in)], [True], "gather_w_in", 2, after=(wd1, h1))

    g1, u1, y1 = _ffn_fwd(h1, wg1, wu1, wd1, "ffn1_fwd")
    x1, h2, h2t = _norm_mod(xs, mod2, "norm2", prev=(y1, mod1, 0.5))
    qkv, rest, qkv_hat = _in_proj(h2, w_in_full, q_norm, k_norm, "in_proj")
    w_ab, w_cb_g, w_o_g, wg2, wu2, wd2 = _gather_weights(
        [to16(w_attn_branch), to16(w_conv_branch), to16(w_out),
         to16(ffn2_w_gate), to16(ffn2_w_up), to16(ffn2_w_down)], [True] + [False] * 5,
        "gather_rest", 3, after=(h2,))
    a_w = w_ab.shape[0]
    w_cb = w_cb_g.reshape(d, d)
    w_o = w_o_g.reshape(d, d)
    o, lse = _attn_fwd(qkv_hat, "attn_fwd")
    x2, z, ya, yc, conv, yb, merged, o16, h3, h3t = _mix_fwd(x1, o, rest, mod2, mod3, conv_pad, w_ab, w_cb, w_o,
                                                             "mix_fwd")
    g3, u3, y3 = _ffn_fwd(h3, wg2, wu2, wd2, "ffn2_fwd")
    dx3, dy3, loss_part = _loss_grad(x2, y3, mod3, target, "loss")

    c_idx = jnp.reshape(ic, (1,)).astype(jnp.int32)
    chip_idx = jnp.stack([chip, ic]).astype(jnp.int32)

    def reduce_start(grads, names, tag, collective_id):
        from_sibling = _rs_pair_exchange(grads, "rs_pair_" + tag)
        pair_sums = [_pair_add(g, r, c_idx, "pair_add_" + nm) for g, r, nm in zip(grads, from_sibling, names)]
        return pair_sums, _rs_chip_exchange(pair_sums, "rs_chips_" + tag, collective_id)

    def reduce_finish(pair_sums, from_chips, names, tag, after):
        totals = [_chip_add(p, r, chip_idx, "chip_add_" + nm, after)
                  for p, r, nm in zip(pair_sums, from_chips, names)]
        return dict(zip(names, _rs_share(totals, "rs_share_" + tag)))

    names_a = ["ffn2_w_gate", "ffn2_w_up", "ffn2_w_down"]
    names_b = ["w_in", "w_attn_branch", "w_conv_branch", "w_out"]
    names_c = ["ffn1_w_gate", "ffn1_w_up", "ffn1_w_down"]

    dh3, dg3, du3, a3 = _ffn_bwd(dy3, g3, u3, wg2, wu2, wd2, "ffn2_bwd")
    sums_a, chips_a = reduce_start(list(_ffn_wgrads(h3t, dg3, du3, a3, dy3, "ffn2")), names_a, "a", 4)
    dx2, st3 = _norm_bwd(dh3, x2, mod3, dx3, y3, 0.5, "norm3_bwd", after=tuple(sums_a))

    do, drest, dz, dya, dyc, st_conv = _mix_bwd(dx2, ya, yc, conv, rest, mod2, conv_pad, w_ab, w_cb, w_o, a_w, "mix_bwd")
    dq, dk, dv, st_qk = _attn_bwd(qkv, qkv_hat, do, o, lse, q_norm, k_norm, "attn_bwd")
    tok = lambda width: (lambda ts: pl.BlockSpec((ts, width), lambda cc, s: (s, 0)))
    colblk = lambda width: (lambda ts: pl.BlockSpec((ts, width), lambda cc, s: (s, cc)))
    tok_t = lambda ts: pl.BlockSpec((d, ts), lambda cc, s: (0, s))
    whole = pl.BlockSpec((d, QKV), lambda cc, s: (0, 0))
    dw_in = [_wgrad(h2t, part, tok_t, tok(QKV), (d, QKV), whole, (d, QKV), 1, "dw_in_" + nm, True)
             for part, nm in ((dq, "q"), (dk, "k"), (dv, "v"))]
    dw_in.append(_wgrad(h2t, drest, tok_t, colblk(d), (d, 5 * d), pl.BlockSpec((d, d), lambda cc, s: (0, cc)),
                        (d, d), 5, "dw_in_rest", True))
    dw_in = _cols_to_shards(jnp.concatenate(dw_in, axis=1), N_CHIPS)
    shard_w = d // N_CHIPS
    dw_ab = _wgrad(o16, dya, tok(a_w), colblk(shard_w), (a_w, d), pl.BlockSpec((a_w, shard_w), lambda cc, s: (0, cc)),
                   (a_w, shard_w), N_CHIPS, "dw_attn_branch")
    dw_ab = _cols_to_shards(dw_ab, N_CHIPS)
    row_out = pl.BlockSpec((None, shard_w, d), lambda cc, s: (cc, 0, 0))
    dw_cb = _wgrad(yb, dyc, colblk(shard_w), tok(d), (N_CHIPS, shard_w, d), row_out, (shard_w, d), N_CHIPS, "dw_conv_branch")
    dw_o = _wgrad(merged, dz, colblk(shard_w), tok(d), (N_CHIPS, shard_w, d), row_out, (shard_w, d), N_CHIPS, "dw_out")
    shard_grads = reduce_finish(sums_a, chips_a, names_a, "a", after=(dw_in, dw_o))
    sums_b, chips_b = reduce_start([dw_in, dw_ab, dw_cb, dw_o], names_b, "b", 5)

    dh2 = _in_proj_bwd(dq, dk, dv, drest, w_in_full, "in_proj_bwd", after=tuple(sums_b))
    dx1, st2, dy1 = _norm_bwd(dh2, x1, mod2, dx2, z, 1.0, "norm2_bwd", prev=(mod1, 0.5))
    dh1, dg1, du1, a1 = _ffn_bwd(dy1, g1, u1, wg1, wu1, wd1, "ffn1_bwd")
    dx0, st1 = _norm_bwd(dh1, xs, mod1, dx1, y1, 0.5, "norm1_bwd")
    grads_c = list(_ffn_wgrads(h1t, dg1, du1, a1, dy1, "ffn1"))
    shard_grads.update(reduce_finish(sums_b, chips_b, names_b, "b", after=tuple(grads_c)))
    sums_c, chips_c = reduce_start(grads_c, names_c, "c", 6)

    dmod = jnp.concatenate([st1[0:3], st2[0:3], st3[0:3]], axis=0).reshape(1, 9 * d)
    loss_cols = jnp.zeros((1, HEAD_DIM), F32).at[0, 0].set(jnp.sum(loss_part))
    small = jnp.concatenate([dmod, st1[3:4], st2[3:4], st3[3:4], st_qk[0:1], st_qk[1:2],
                             st_conv[0:3].reshape(1, 3 * d), loss_cols], axis=1)
    small_all = _allgather8(jnp.broadcast_to(small, (8, small.shape[1])), "gather_small").reshape(8, 8, -1)[:, 0]
    loss = (0.5 / d) * jnp.sum(small_all[:, -HEAD_DIM])
    small_all = small_all[:, :-HEAD_DIM]
    dmod_all = small_all[:, :9 * d]
    dmod_shard = lax.dynamic_slice(dmod_all, (0, chip * ada_cols), (8, ada_cols))
    g_w_ada, d_w_ada, nm_w_ada, nv_w_ada = _ada_bwd(c_all, dmod_shard, w_ada[0], m_w_ada[0], v_w_ada[0], "ada_bwd")

    vec_names = ["b_ada", "norm_ffn1", "norm_mix", "norm_ffn2", "q_norm", "k_norm"]
    vec_w = [b_ada, norm_ffn1, norm_mix, norm_ffn2, q_norm, k_norm]
    vec_m = [m_b_ada, m_norm_ffn1, m_norm_mix, m_norm_ffn2, m_q_norm, m_k_norm]
    vec_v = [v_b_ada, v_norm_ffn1, v_norm_mix, v_norm_ffn2, v_q_norm, v_k_norm]
    n_vec = sum(w.shape[1] for w in vec_w)
    cat = lambda arrs: jnp.concatenate(arrs, axis=1)
    vec_out = _small_update(small_all[:, :n_vec], cat(vec_w), cat(vec_m), cat(vec_v), "small_update")
    conv_parts = small_all[:, n_vec:].reshape(8, 3, N_CHIPS * conv_cols)
    conv_parts = lax.dynamic_slice(conv_parts, (0, 0, chip * conv_cols), (8, 3, conv_cols)).reshape(8, 3 * conv_cols)
    flat3 = lambda w: w[0].reshape(1, 3 * conv_cols)
    conv_out = _small_update(conv_parts, flat3(conv_w), flat3(m_conv_w), flat3(v_conv_w), "conv_update")

    res = {"w_ada": [t[None] for t in (g_w_ada, d_w_ada, nm_w_ada, nv_w_ada)],
           "conv_w": [t.reshape(1, 3, conv_cols) for t in conv_out]}
    off = 0
    for nm, w in zip(vec_names, vec_w):
        width = w.shape[1]
        res[nm] = [t[:, off:off + width] for t in vec_out]
        off += width
    big = {"ffn1_w_gate": (ffn1_w_gate, m_ffn1_w_gate, v_ffn1_w_gate), "ffn1_w_up": (ffn1_w_up, m_ffn1_w_up, v_ffn1_w_up),
           "ffn1_w_down": (ffn1_w_down, m_ffn1_w_down, v_ffn1_w_down), "w_in": (w_in, m_w_in, v_w_in),
           "w_attn_branch": (w_attn_branch, m_w_attn_branch, v_w_attn_branch),
           "w_conv_branch": (w_conv_branch, m_w_conv_branch, v_w_conv_branch), "w_out": (w_out, m_w_out, v_w_out),
           "ffn2_w_gate": (ffn2_w_gate, m_ffn2_w_gate, v_ffn2_w_gate), "ffn2_w_up": (ffn2_w_up, m_ffn2_w_up, v_ffn2_w_up),
           "ffn2_w_down": (ffn2_w_down, m_ffn2_w_down, v_ffn2_w_down)}
    def update(nm, after=()):
        w, m, v = big[nm]
        g, delta, new_m, new_v = _adamw(w[0], shard_grads[nm], m[0], v[0], "adamw_" + nm, after)
        res[nm] = [t[None] for t in (g, delta, new_m, new_v)]
        return new_v

    last = tuple(sums_c)
    for nm in names_a + names_b:
        last = (update(nm, last),)
    shard_grads.update(reduce_finish(sums_c, chips_c, names_c, "c", after=last))
    for nm in names_c:
        update(nm)

    order = ["w_ada", "b_ada", "norm_ffn1", "ffn1_w_gate", "ffn1_w_up", "ffn1_w_down", "norm_mix", "w_in", "q_norm",
             "k_norm", "conv_w", "w_attn_branch", "w_conv_branch", "w_out", "norm_ffn2", "ffn2_w_gate", "ffn2_w_up",
             "ffn2_w_down"]
    return (loss, dx0[None], *[res[nm][0] for nm in order], *[res[nm][1] for nm in order],
            *[res[nm][2] for nm in order], *[res[nm][3] for nm in order])
```

```python
import jax
import jax.numpy as jnp
from jax import lax
from jax.experimental import pallas as pl
from jax.experimental.pallas import tpu as pltpu
from jax.experimental.pallas import tpu_sc as plsc

F32 = jnp.float32
BF16 = jnp.bfloat16
MESH = pl.DeviceIdType.MESH
ANY = pl.BlockSpec(memory_space=pl.ANY)

NORM_EPS = 1e-6
HEAD_DIM = 128
N_GROUPS = 3
HEADS = 4
DILATIONS = (1, 4, 16)
ATTN_BLOCK = 128
SLAB = ATTN_BLOCK * max(DILATIONS)
QKV = N_GROUPS * HEADS * HEAD_DIM
ATTN_SCALE = HEAD_DIM ** -0.5
NEG = -1e30
N_CHIPS = 4

ADAM_LR = 0.001
ADAM_B1 = 0.9
ADAM_B2 = 0.999
ADAM_EPS = 1e-08
ADAM_WD = 0.01
ADAM_STEP = 10

VMEM_LIMIT_BYTES = 56 * 1024 * 1024
TOKEN_TILE = 512
FFN_TILE = 1024
PROJ_TILE = 2048
WGRAD_TILE = 2048
IN_BLOCK = 512
MIX_TILE = 256


def _params(n_axes=0):
    return pltpu.CompilerParams(
        dimension_semantics=("arbitrary",) * n_axes if n_axes else None,
        vmem_limit_bytes=VMEM_LIMIT_BYTES)


def _dot(a, b):
    return jnp.dot(a, b, preferred_element_type=F32)


def _dot_nt(a, b):
    return lax.dot_general(a, b, (((1,), (1,)), ((), ())), preferred_element_type=F32)


def _dot_tn(a, b):
    return lax.dot_general(a, b, (((0,), (0,)), ((), ())), preferred_element_type=F32)


def _sigmoid(x):
    return 1.0 / (1.0 + jnp.exp(-x))


def _place():
    return lax.axis_index("x"), lax.axis_index("y"), lax.axis_index("c")


def _ordered(body, n_in, after):
    if not after:
        return body
    return lambda *refs: body(*refs[:n_in], *refs[n_in + len(after):])


def _allgather8(block, name):
    m_per, n = block.shape

    def body(x_ref, out_ref, send_sems, recv_sems, local_sem):
        x, y, c = _place()
        me, sibling = (x, y, c), (x, y, 1 - c)
        chips = [(1 - x, y), (x, 1 - y), (1 - x, 1 - y)]

        def rows(px, py, pc):
            return out_ref.at[pl.ds((4 * px + 2 * py + pc) * m_per, m_per), :]

        def copy(k, blk, to, src=None):
            return pltpu.make_async_remote_copy(
                src_ref=rows(*blk) if src is None else src, dst_ref=rows(*blk),
                send_sem=send_sems.at[k], recv_sem=recv_sems.at[k],
                device_id=to, device_id_type=MESH)

        mine = pltpu.make_async_copy(x_ref, rows(*me), local_sem)
        mine.start()
        first = [copy(0, me, sibling, src=x_ref)]
        first += [copy(1 + j, me, (*chip, c), src=x_ref) for j, chip in enumerate(chips)]
        for cp in first:
            cp.start()
        passed = [copy(4 + j, (*chip, c), sibling) for j, chip in enumerate(chips)]
        for j, chip in enumerate(chips):
            copy(1 + j, (*chip, c), me).wait_recv()
            passed[j].start()
        copy(0, sibling, me).wait_recv()
        for j, chip in enumerate(chips):
            copy(4 + j, (*chip, 1 - c), me).wait_recv()
        for cp in first + passed:
            cp.wait_send()
        mine.wait()

    return pl.pallas_call(
        body, name=name,
        out_shape=jax.ShapeDtypeStruct((8 * m_per, n), block.dtype),
        in_specs=[pl.BlockSpec(memory_space=pltpu.VMEM)],
        out_specs=pl.BlockSpec(memory_space=pltpu.VMEM),
        scratch_shapes=[pltpu.SemaphoreType.DMA((7,)), pltpu.SemaphoreType.DMA((7,)),
                        pltpu.SemaphoreType.DMA],
        compiler_params=_params(),
    )(block)


def _handshake(peers):
    barrier = pltpu.get_barrier_semaphore()
    for peer in peers:
        pl.semaphore_signal(barrier, inc=1, device_id=peer, device_id_type=MESH)
    pl.semaphore_wait(barrier, len(peers))


def _gather_weights(shards, by_cols, name, collective_id, after=()):
    n_arr = len(shards)

    def body(*refs):
        srcs, outs = refs[:n_arr], refs[n_arr + len(after):2 * n_arr + len(after)]
        send_sems, recv_sems, local_sems = refs[2 * n_arr + len(after):]
        x, y, c = _place()
        me_dev, sibling = (x, y, c), (x, y, 1 - c)
        chips = [(1 - x, y), (x, 1 - y), (1 - x, 1 - y)]
        me = 2 * x + y
        _handshake([sibling] + [(*chip, c) for chip in chips])

        def place(k, chip_idx, rows):
            if by_cols[k]:
                width = srcs[k].shape[1]
                return outs[k].at[rows, pl.ds(pl.multiple_of(chip_idx * width, 128), width)]
            return outs[k].at[chip_idx, rows]

        def copy(k, slot, chip_idx, half_sel, to, from_shard=False):
            half = srcs[k].shape[0] // 2
            rows = pl.ds(half_sel * half, half)
            dst = place(k, chip_idx, rows)
            return pltpu.make_async_remote_copy(
                src_ref=srcs[k].at[rows] if from_shard else dst, dst_ref=dst,
                send_sem=send_sems.at[6 * k + slot], recv_sem=recv_sems.at[6 * k + slot],
                device_id=to, device_id_type=MESH)

        own = [pltpu.make_async_copy(srcs[k], place(k, me, pl.ds(0, srcs[k].shape[0])), local_sems.at[k])
               for k in range(n_arr)]
        for cp in own:
            cp.start()
        sent = []
        for k in range(n_arr):
            for j, chip in enumerate(chips):
                sent.append(copy(k, j, me, c, (*chip, c), from_shard=True))
                sent[-1].start()
        for k in range(n_arr):
            for j, chip in enumerate(chips):
                chip_idx = 2 * chip[0] + chip[1]
                copy(k, j, chip_idx, c, me_dev).wait_recv()
                sent.append(copy(k, 3 + j, chip_idx, c, sibling))
                sent[-1].start()
        for k in range(n_arr):
            for j, chip in enumerate(chips):
                copy(k, 3 + j, 2 * chip[0] + chip[1], 1 - c, me_dev).wait_recv()
        for cp in sent:
            cp.wait_send()
        for cp in own:
            cp.wait()

    def gathered(k):
        r, cols = shards[k].shape
        return (r, N_CHIPS * cols) if by_cols[k] else (N_CHIPS, r, cols)

    return pl.kernel(
        body, name=name,
        out_type=[jax.ShapeDtypeStruct(gathered(k), shards[k].dtype) for k in range(n_arr)],
        mesh=plsc.ScalarSubcoreMesh(axis_name="sequencer", num_cores=1),
        scratch_types=[pltpu.SemaphoreType.DMA((6 * n_arr,)), pltpu.SemaphoreType.DMA((6 * n_arr,)),
                       pltpu.SemaphoreType.DMA((n_arr,))],
        compiler_params=pltpu.CompilerParams(collective_id=collective_id),
    )(*shards, *after)


def _rs_pair_exchange(grads, name, collective_id):
    n_arr = len(grads)

    def body(*refs):
        srcs, outs = refs[:n_arr], refs[n_arr:2 * n_arr]
        send_sems, recv_sems = refs[2 * n_arr:]
        x, y, c = _place()
        _handshake([(x, y, 1 - c)])
        cps = []
        for k in range(n_arr):
            half = srcs[k].shape[1] // 2
            cps.append(pltpu.make_async_remote_copy(
                src_ref=srcs[k].at[:, pl.ds((1 - c) * half, half)], dst_ref=outs[k],
                send_sem=send_sems.at[k], recv_sem=recv_sems.at[k],
                device_id=(x, y, 1 - c), device_id_type=MESH))
            cps[-1].start()
        for cp in cps:
            cp.wait_recv()
        for cp in cps:
            cp.wait_send()

    return pl.kernel(
        body, name=name,
        out_type=[jax.ShapeDtypeStruct((g.shape[0], g.shape[1] // 2, g.shape[2]), g.dtype) for g in grads],
        mesh=plsc.ScalarSubcoreMesh(axis_name="sequencer", num_cores=1),
        scratch_types=[pltpu.SemaphoreType.DMA((n_arr,)), pltpu.SemaphoreType.DMA((n_arr,))],
        compiler_params=pltpu.CompilerParams(collective_id=collective_id),
    )(*grads)


def _rs_chip_exchange(sums, name, collective_id):
    n_arr = len(sums)

    def body(*refs):
        srcs, outs = refs[:n_arr], refs[n_arr:2 * n_arr]
        send_sems, recv_sems = refs[2 * n_arr:]
        x, y, c = _place()
        chips = [(1 - x, y), (x, 1 - y), (1 - x, 1 - y)]
        _handshake([(*chip, c) for chip in chips])
        cps = []
        for k in range(n_arr):
            for j, chip in enumerate(chips):
                cps.append(pltpu.make_async_remote_copy(
                    src_ref=srcs[k].at[2 * chip[0] + chip[1]], dst_ref=outs[k].at[j],
                    send_sem=send_sems.at[3 * k + j], recv_sem=recv_sems.at[3 * k + j],
                    device_id=(*chip, c), device_id_type=MESH))
                cps[-1].start()
        for cp in cps:
            cp.wait_recv()
        for cp in cps:
            cp.wait_send()

    return pl.kernel(
        body, name=name,
        out_type=[jax.ShapeDtypeStruct((3,) + s.shape[1:], s.dtype) for s in sums],
        mesh=plsc.ScalarSubcoreMesh(axis_name="sequencer", num_cores=1),
        scratch_types=[pltpu.SemaphoreType.DMA((3 * n_arr,)), pltpu.SemaphoreType.DMA((3 * n_arr,))],
        compiler_params=pltpu.CompilerParams(collective_id=collective_id),
    )(*sums)


def _rs_share(totals, name):
    n_arr = len(totals)

    def body(*refs):
        outs = refs[n_arr:2 * n_arr]
        send_sems, recv_sems = refs[2 * n_arr:]
        x, y, c = _place()

        def half_rows(k, sel):
            return outs[k].at[sel]

        cps = []
        for k in range(n_arr):
            cps.append(pltpu.make_async_remote_copy(
                src_ref=half_rows(k, c), dst_ref=half_rows(k, c), send_sem=send_sems.at[k], recv_sem=recv_sems.at[k],
                device_id=(x, y, 1 - c), device_id_type=MESH))
            cps[-1].start()
        for k in range(n_arr):
            pltpu.make_async_remote_copy(
                src_ref=half_rows(k, c), dst_ref=half_rows(k, 1 - c), send_sem=send_sems.at[k],
                recv_sem=recv_sems.at[k], device_id=(x, y, 1 - c), device_id_type=MESH).wait_recv()
        for cp in cps:
            cp.wait_send()

    shared = pl.pallas_call(
        body, name=name,
        out_shape=[jax.ShapeDtypeStruct(t.shape, t.dtype) for t in totals],
        in_specs=[ANY] * n_arr, out_specs=[ANY] * n_arr,
        input_output_aliases={k: k for k in range(n_arr)},
        scratch_shapes=[pltpu.SemaphoreType.DMA((n_arr,)), pltpu.SemaphoreType.DMA((n_arr,))],
        compiler_params=_params(),
    )(*totals)
    return [t.reshape(2 * t.shape[1], t.shape[2]) for t in shared]


def _pair_add(grad, recv, c_idx, name, after=()):
    n, r, cols = grad.shape
    half = r // 2
    rows = half // 2

    def body(_, g_ref, r_ref, o_ref):
        o_ref[...] = (g_ref[...].astype(F32) + r_ref[...].astype(F32)).astype(o_ref.dtype)

    return pl.pallas_call(
        _ordered(body, 3, after), name=name,
        grid_spec=pltpu.PrefetchScalarGridSpec(
            num_scalar_prefetch=1, grid=(n, 2),
            in_specs=[pl.BlockSpec((None, None, rows, cols), lambda s, i, ci: (s, ci[0], i, 0)),
                      pl.BlockSpec((None, rows, cols), lambda s, i, ci: (s, i, 0))] + [ANY] * len(after),
            out_specs=pl.BlockSpec((None, rows, cols), lambda s, i, ci: (s, i, 0))),
        out_shape=jax.ShapeDtypeStruct((n, half, cols), BF16),
        compiler_params=_params(2),
    )(c_idx, grad.reshape(n, 2, half, cols), recv, *after)


def _chip_add(sums, recv, chip_and_core, name, after=()):
    _, half, cols = sums.shape
    rows = half // 2

    def body(_, s_ref, r0_ref, r1_ref, r2_ref, o_ref):
        o_ref[...] = ((s_ref[...].astype(F32) + r0_ref[...].astype(F32))
                      + r1_ref[...].astype(F32)) + r2_ref[...].astype(F32)

    def recv_spec(j):
        return pl.BlockSpec((None, rows, cols), lambda i, ci: (j, i, 0))

    return pl.pallas_call(
        _ordered(body, 5, after), name=name,
        grid_spec=pltpu.PrefetchScalarGridSpec(
            num_scalar_prefetch=1, grid=(2,),
            in_specs=[pl.BlockSpec((None, rows, cols), lambda i, ci: (ci[0], i, 0)),
                      recv_spec(0), recv_spec(1), recv_spec(2)] + [ANY] * len(after),
            out_specs=pl.BlockSpec((None, rows, cols), lambda i, ci: (ci[1], i, 0))),
        out_shape=jax.ShapeDtypeStruct((2, half, cols), F32),
        compiler_params=_params(1),
    )(chip_and_core, sums, recv, recv, recv, *after)


def _rms(x):
    return lax.rsqrt(jnp.mean(x * x, axis=-1, keepdims=True) + NORM_EPS)


def _norm_mod(x, mod, name, prev=None):
    s_len, d = x.shape
    tm = TOKEN_TILE

    def body(*refs):
        if prev is None:
            x_ref, mod_ref, h_ref, ht_ref = refs
            xv = x_ref[...]
        else:
            x_ref, y_ref, modp_ref, mod_ref, xo_ref, h_ref, ht_ref = refs
            xv = x_ref[...] + prev[2] * modp_ref[2:3, :] * y_ref[...]
            xo_ref[...] = xv
        n = (xv * _rms(xv)) * mod_ref[3:4, :]
        h = n * (1.0 + mod_ref[1:2, :]) + mod_ref[0:1, :]
        h_ref[...] = h.astype(BF16)
        ht_ref[...] = h.T.astype(BF16)

    tile = pl.BlockSpec((tm, d), lambda i: (i, 0))
    small = pl.BlockSpec((8, d), lambda i: (0, 0))
    h_specs = [tile, pl.BlockSpec((d, tm), lambda i: (0, i))]
    h_shapes = [jax.ShapeDtypeStruct((s_len, d), BF16), jax.ShapeDtypeStruct((d, s_len), BF16)]
    if prev is None:
        return pl.pallas_call(
            body, name=name, grid=(s_len // tm,), in_specs=[tile, small], out_specs=h_specs, out_shape=h_shapes,
            compiler_params=_params(1))(x, mod)
    return pl.pallas_call(
        body, name=name, grid=(s_len // tm,), in_specs=[tile, tile, small, small],
        out_specs=[tile] + h_specs, out_shape=[jax.ShapeDtypeStruct((s_len, d), F32)] + h_shapes,
        compiler_params=_params(1))(x, prev[0], prev[1], mod)


def _norm_bwd(dh, x, mod, dxo, y_raw, coef, name, after=(), prev=None):
    s_len, d = x.shape
    tm = TOKEN_TILE

    def body(*refs):
        if prev is None:
            dh_ref, x_ref, mod_ref, dxo_ref, y_ref, dx_ref, st_ref = refs
        else:
            dh_ref, x_ref, mod_ref, dxo_ref, y_ref, modp_ref, dx_ref, st_ref, dyp_ref = refs

        @pl.when(pl.program_id(0) == 0)
        def _():
            st_ref[...] = jnp.zeros_like(st_ref)

        xv, dhv, dxov = x_ref[...], dh_ref[...], dxo_ref[...]
        r = _rms(xv)
        xh = xv * r
        gain, scale = mod_ref[3:4, :], mod_ref[1:2, :]
        dn = dhv * (1.0 + scale)
        dxh = dn * gain
        dx = dxov + r * (dxh - xh * jnp.mean(dxh * xh, axis=-1, keepdims=True))
        dx_ref[...] = dx
        if prev is not None:
            dyp_ref[...] = (prev[1] * modp_ref[2:3, :] * dx).astype(BF16)
        st_ref[0:1, :] += jnp.sum(dhv, axis=0, keepdims=True)
        st_ref[1:2, :] += jnp.sum(dhv * (xh * gain), axis=0, keepdims=True)
        st_ref[2:3, :] += coef * jnp.sum(y_ref[...].astype(F32) * dxov, axis=0, keepdims=True)
        st_ref[3:4, :] += jnp.sum(dn * xh, axis=0, keepdims=True)

    tile = pl.BlockSpec((tm, d), lambda i: (i, 0))
    small = pl.BlockSpec((8, d), lambda i: (0, 0))
    operands = [dh, x, mod, dxo, y_raw] + ([] if prev is None else [prev[0]])
    in_specs = [tile, tile, small, tile, tile] + ([] if prev is None else [small])
    out_specs = [tile, small] + ([] if prev is None else [tile])
    out_shape = [jax.ShapeDtypeStruct((s_len, d), F32), jax.ShapeDtypeStruct((8, d), F32)]
    if prev is not None:
        out_shape.append(jax.ShapeDtypeStruct((s_len, d), BF16))
    return pl.pallas_call(
        _ordered(body, len(operands), after), name=name, grid=(s_len // tm,),
        in_specs=in_specs + [ANY] * len(after), out_specs=out_specs, out_shape=out_shape,
        compiler_params=_params(1),
    )(*operands, *after)


def _loss_grad(x, y, mod, target, name):
    s_len, d = x.shape
    tm = TOKEN_TILE

    def body(x_ref, y_ref, mod_ref, t_ref, do_ref, dy_ref, part_ref):
        @pl.when(pl.program_id(0) == 0)
        def _():
            part_ref[...] = jnp.zeros_like(part_ref)

        half_gate = 0.5 * mod_ref[2:3, :]
        err = (x_ref[...] + half_gate * y_ref[...]) - t_ref[...]
        do = err * (1.0 / d)
        do_ref[...] = do
        dy_ref[...] = (half_gate * do).astype(BF16)
        sq = err * err
        part_ref[...] += jnp.sum(sq.reshape(tm // 8, 8, d), axis=0)

    tile = pl.BlockSpec((tm, d), lambda i: (i, 0))
    small = pl.BlockSpec((8, d), lambda i: (0, 0))
    return pl.pallas_call(
        body, name=name, grid=(s_len // tm,),
        in_specs=[tile, tile, small, tile],
        out_specs=[tile, tile, small],
        out_shape=[jax.ShapeDtypeStruct((s_len, d), F32), jax.ShapeDtypeStruct((s_len, d), BF16),
                   jax.ShapeDtypeStruct((8, d), F32)],
        compiler_params=_params(1),
    )(x, y, mod, target)


def _adamw_math(w, g, m, v):
    m = ADAM_B1 * m + (1.0 - ADAM_B1) * g
    v = ADAM_B2 * v + (1.0 - ADAM_B2) * (g * g)
    m_hat = m / (1.0 - ADAM_B1 ** ADAM_STEP)
    v_hat = v / (1.0 - ADAM_B2 ** ADAM_STEP)
    delta = -ADAM_LR * (m_hat / (jnp.sqrt(v_hat) + ADAM_EPS) + ADAM_WD * w)
    return delta, m, v


def _adamw(w, g, m, v, name, after=()):
    r, cols = w.shape
    tr = r // 8 if r % 64 == 0 else r

    def body(w_ref, g_ref, m_ref, v_ref, go_ref, d_ref, nm_ref, nv_ref):
        gv = g_ref[...]
        go_ref[...] = gv
        d_ref[...], nm_ref[...], nv_ref[...] = _adamw_math(w_ref[...], gv, m_ref[...], v_ref[...])

    tile = pl.BlockSpec((tr, cols), lambda i: (i, 0))
    shape = jax.ShapeDtypeStruct((r, cols), F32)
    return pl.pallas_call(
        _ordered(body, 4, after), name=name, grid=(r // tr,),
        in_specs=[tile] * 4 + [ANY] * len(after), out_specs=[tile] * 4, out_shape=[shape] * 4,
        compiler_params=_params(1),
    )(w, g, m, v, *after)


def _in_parts(tm, n_qkv, n_rest):
    def part(lo, n_blk):
        return pl.BlockSpec((tm, IN_BLOCK), lambda i, j: (i, jnp.clip(j - lo, 0, n_blk - 1)))
    return [part(0, n_qkv), part(n_qkv, n_qkv), part(2 * n_qkv, n_qkv), part(3 * n_qkv, n_rest)]


def _pick_part(j, n_qkv, refs, fn):
    bounds = [0, n_qkv, 2 * n_qkv, 3 * n_qkv]
    for p, ref in enumerate(refs):
        inside = j >= bounds[p]
        if p + 1 < len(refs):
            inside = inside & (j < bounds[p + 1])
        pl.when(inside)(lambda ref=ref: fn(ref))


def _rows(base, count, stride):
    return pl.ds(base, count) if stride == 1 else pl.ds(base, count, stride=stride)


REORDER_STRIDE = 4


def _reorder_plan(dil, parts=1):
    inner = min(dil, REORDER_STRIDE)
    return inner, dil // inner, SLAB // parts // inner, SLAB // dil


def _to_residue_order(dst, src, dil, tmp, part=0, parts=1):
    inner, outer, big, seg = _reorder_plan(dil, parts)
    piece = seg // parts
    if outer == 1:
        for r in range(dil):
            dst[pl.ds(r * seg + part * piece, piece), :] = src[_rows(r, piece, dil), :].astype(dst.dtype)
        return
    for b in range(inner):
        tmp[pl.ds(b * big, big), :] = src[_rows(b, big, inner), :]
    for a in range(outer):
        for b in range(inner):
            dst[pl.ds((inner * a + b) * seg + part * piece, piece), :] = (
                tmp[_rows(b * big + a, piece, outer), :].astype(dst.dtype))


def _to_token_order(dst, src, dil, tmp):
    inner, outer, big, seg = _reorder_plan(dil)
    if outer == 1:
        for r in range(dil):
            dst[_rows(r, seg, dil), :] = src[pl.ds(r * seg, seg), :]
        return
    for a in range(outer):
        for b in range(inner):
            tmp[_rows(b * big + a, seg, outer), :] = src[pl.ds((inner * a + b) * seg, seg), :]
    for b in range(inner):
        dst[_rows(b, big, inner), :] = tmp[pl.ds(b * big, big), :]


def _in_proj(h, w, q_norm, k_norm, name):
    s_len, d = h.shape
    tm = PROJ_TILE
    assert tm == SLAB and IN_BLOCK == HEADS * HEAD_DIM
    steps = w.shape[1] // IN_BLOCK
    n_qkv = 3 * QKV // IN_BLOCK
    halves = 2
    rows = [pl.ds(p * (tm // halves), tm // halves) for p in range(halves)]

    def body(h_ref, w_ref, qn_ref, kn_ref, qkv_ref, rest_ref, hat_ref, tok_s, tmp_s):
        j = pl.program_id(1)
        res = [_dot(h_ref[rows[p], :], w_ref[...]) for p in range(halves)]

        def emit(sect, gi):
            dil = DILATIONS[gi]
            for p in range(halves):
                qkv_ref[rows[p], :] = res[p]
                for hh in range(HEADS):
                    cols = slice(hh * HEAD_DIM, (hh + 1) * HEAD_DIM)
                    x = res[p][:, cols]
                    if sect < 2:
                        x = (x * _rms(x)) * (qn_ref if sect == 0 else kn_ref)[...]
                    tok_s[...] = x
                    _to_residue_order(hat_ref.at[:, cols], tok_s, dil, tmp_s, p, halves)

        for sect in range(3):
            for gi in range(N_GROUPS):
                pl.when(j == sect * N_GROUPS + gi)(lambda sect=sect, gi=gi: emit(sect, gi))

        @pl.when(j >= n_qkv)
        def _():
            for p in range(halves):
                rest_ref[rows[p], :] = res[p].astype(BF16)

    qkv_blk = pl.BlockSpec((tm, IN_BLOCK), lambda i, j: (i, jnp.minimum(j, n_qkv - 1)))
    small = pl.BlockSpec((1, HEAD_DIM), lambda i, j: (0, 0))
    return pl.pallas_call(
        body, name=name, grid=(s_len // tm, steps),
        in_specs=[pl.BlockSpec((tm, d), lambda i, j: (i, 0)), pl.BlockSpec((d, IN_BLOCK), lambda i, j: (0, j)),
                  small, small],
        out_specs=[qkv_blk, pl.BlockSpec((tm, IN_BLOCK), lambda i, j: (i, jnp.maximum(j - n_qkv, 0))), qkv_blk],
        out_shape=[jax.ShapeDtypeStruct((s_len, 3 * QKV), F32),
                   jax.ShapeDtypeStruct((s_len, w.shape[1] - 3 * QKV), BF16),
                   jax.ShapeDtypeStruct((s_len, 3 * QKV), BF16)],
        scratch_shapes=[pltpu.VMEM((tm // halves, HEAD_DIM), F32)] * 2,
        compiler_params=_params(2),
    )(h, w, q_norm, k_norm)


def _in_proj_bwd(dq, dk, dv, drest, w, name, after=()):
    s_len = dq.shape[0]
    d = w.shape[0]
    tm = PROJ_TILE
    steps = w.shape[1] // IN_BLOCK
    n_qkv = QKV // IN_BLOCK

    def body(dq_ref, dk_ref, dv_ref, dr_ref, w_ref, o_ref, acc_ref):
        j = pl.program_id(1)

        @pl.when(j == 0)
        def _():
            acc_ref[...] = jnp.zeros_like(acc_ref)

        def add(a_ref):
            acc_ref[...] += _dot_nt(a_ref[...], w_ref[...])

        _pick_part(j, n_qkv, [dq_ref, dk_ref, dv_ref, dr_ref], add)

        @pl.when(j == steps - 1)
        def _():
            o_ref[...] = acc_ref[...]

    return pl.pallas_call(
        _ordered(body, 5, after), name=name, grid=(s_len // tm, steps),
        in_specs=(_in_parts(tm, n_qkv, steps - 3 * n_qkv) + [pl.BlockSpec((d, IN_BLOCK), lambda i, j: (0, j))]
                  + [ANY] * len(after)),
        out_specs=pl.BlockSpec((tm, d), lambda i, j: (i, 0)),
        out_shape=jax.ShapeDtypeStruct((s_len, d), F32),
        scratch_shapes=[pltpu.VMEM((tm, d), F32)],
        compiler_params=_params(2),
    )(dq, dk, dv, drest, w, *after)


def _wgrad(x, y, x_spec, y_spec, out_shape, out_spec, acc_shape, n_chunks, name, x_transposed=False, after=()):
    s_len = y.shape[-2]
    ts = WGRAD_TILE
    steps = s_len // ts

    def body(x_ref, y_ref, o_ref, acc_ref):
        s = pl.program_id(1)

        @pl.when(s == 0)
        def _():
            acc_ref[...] = jnp.zeros_like(acc_ref)

        acc_ref[...] += (_dot if x_transposed else _dot_tn)(x_ref[...], y_ref[...])

        @pl.when(s == steps - 1)
        def _():
            o_ref[...] = acc_ref[...].astype(o_ref.dtype)

    return pl.pallas_call(
        _ordered(body, 2, after), name=name, grid=(n_chunks, steps),
        in_specs=[x_spec(ts), y_spec(ts)] + [ANY] * len(after), out_specs=out_spec,
        out_shape=jax.ShapeDtypeStruct(out_shape, BF16),
        scratch_shapes=[pltpu.VMEM(acc_shape, F32)],
        compiler_params=_params(2),
    )(x, y, *after)


def _pieces(width, piece=256):
    return [slice(a, min(a + piece, width)) for a in range(0, width, piece)]


def _ffn_fwd(h, w_gate, w_up, w_down, name):
    s_len, d = h.shape
    n_chunks, _, fs = w_gate.shape
    tm = FFN_TILE

    def body(h_ref, wg_ref, wu_ref, wd_ref, g_ref, u_ref, y_ref):
        j = pl.program_id(1)
        hv = h_ref[...]
        pieces = _pieces(fs)
        first = lambda cols: (_dot(hv, wg_ref[:, cols]), _dot(hv, wu_ref[:, cols]))
        total = None
        ahead = first(pieces[0])
        for k, cols in enumerate(pieces):
            g, u = ahead
            if k + 1 < len(pieces):
                ahead = first(pieces[k + 1])
            g_ref[:, cols] = g.astype(BF16)
            u_ref[:, cols] = u.astype(BF16)
            act = (g * _sigmoid(g)) * u
            part = _dot(act.astype(BF16), wd_ref[cols, :])
            total = part if total is None else total + part

        @pl.when(j == 0)
        def _():
            y_ref[...] = total

        @pl.when(j > 0)
        def _():
            y_ref[...] += total

    tile = pl.BlockSpec((tm, d), lambda i, j: (i, 0))
    hid = pl.BlockSpec((None, tm, fs), lambda i, j: (j, i, 0))
    w_in_spec = pl.BlockSpec((None, d, fs), lambda i, j: (j, 0, 0))
    hid_shape = jax.ShapeDtypeStruct((n_chunks, s_len, fs), BF16)
    return pl.pallas_call(
        body, name=name, grid=(s_len // tm, n_chunks),
        in_specs=[tile, w_in_spec, w_in_spec, pl.BlockSpec((None, fs, d), lambda i, j: (j, 0, 0))],
        out_specs=[hid, hid, tile],
        out_shape=[hid_shape, hid_shape, jax.ShapeDtypeStruct((s_len, d), F32)],
        compiler_params=_params(2),
    )(h, w_gate, w_up, w_down)


def _ffn_bwd(dy, g_pre, u_pre, w_gate, w_up, w_down, name):
    s_len, d = dy.shape
    n_chunks, _, fs = w_gate.shape
    tm = FFN_TILE

    def body(dy_ref, g_ref, u_ref, wg_ref, wu_ref, wd_ref, dh_ref, dg_ref, du_ref, a_ref):
        j = pl.program_id(1)
        dyv = dy_ref[...]
        pieces = _pieces(fs)
        first = lambda cols: _dot_nt(dyv, wd_ref[cols, :])
        total = None
        ahead = first(pieces[0])
        for k, cols in enumerate(pieces):
            da = ahead
            if k + 1 < len(pieces):
                ahead = first(pieces[k + 1])
            g = g_ref[:, cols].astype(F32)
            u = u_ref[:, cols].astype(F32)
            sg = _sigmoid(g)
            silu = g * sg
            dg = (da * u * (sg * (1.0 + g * (1.0 - sg)))).astype(BF16)
            du = (da * silu).astype(BF16)
            dg_ref[:, cols] = dg
            du_ref[:, cols] = du
            a_ref[:, cols] = (silu * u).astype(BF16)
            part = _dot_nt(dg, wg_ref[:, cols]) + _dot_nt(du, wu_ref[:, cols])
            total = part if total is None else total + part

        @pl.when(j == 0)
        def _():
            dh_ref[...] = total

        @pl.when(j > 0)
        def _():
            dh_ref[...] += total

    tile = pl.BlockSpec((tm, d), lambda i, j: (i, 0))
    hid = pl.BlockSpec((None, tm, fs), lambda i, j: (j, i, 0))
    w_in_spec = pl.BlockSpec((None, d, fs), lambda i, j: (j, 0, 0))
    hid_shape = jax.ShapeDtypeStruct((n_chunks, s_len, fs), BF16)
    return pl.pallas_call(
        body, name=name, grid=(s_len // tm, n_chunks),
        in_specs=[tile, hid, hid, w_in_spec, w_in_spec, pl.BlockSpec((None, fs, d), lambda i, j: (j, 0, 0))],
        out_specs=[tile, hid, hid, hid],
        out_shape=[jax.ShapeDtypeStruct((s_len, d), F32), hid_shape, hid_shape, hid_shape],
        compiler_params=_params(2),
    )(dy, g_pre, u_pre, w_gate, w_up, w_down)


def _ffn_wgrads(ht, dg, du, act, dy, tag, after=()):
    n_chunks, s_len, fs = dg.shape
    d = ht.shape[0]
    tok = lambda ts: pl.BlockSpec((ts, d), lambda c, s: (s, 0))
    tok_t = lambda ts: pl.BlockSpec((d, ts), lambda c, s: (0, s))
    hid = lambda ts: pl.BlockSpec((None, ts, fs), lambda c, s: (c, s, 0))
    d_up = pl.BlockSpec((None, d, fs), lambda c, s: (c, 0, 0))
    d_down = pl.BlockSpec((None, fs, d), lambda c, s: (c, 0, 0))
    dwg = _wgrad(ht, dg, tok_t, hid, (n_chunks, d, fs), d_up, (d, fs), n_chunks, tag + "_dwg", True, after)
    dwu = _wgrad(ht, du, tok_t, hid, (n_chunks, d, fs), d_up, (d, fs), n_chunks, tag + "_dwu", True, after)
    dwd = _wgrad(act, dy, hid, tok, (n_chunks, fs, d), d_down, (fs, d), n_chunks, tag + "_dwd", False, after)
    return dwg, dwu, dwd


def _band_bias():
    qi = lax.broadcasted_iota(jnp.int32, (ATTN_BLOCK, 2 * ATTN_BLOCK), 0)
    kj = lax.broadcasted_iota(jnp.int32, (ATTN_BLOCK, 2 * ATTN_BLOCK), 1)
    band = (kj >= qi) & (kj <= qi + ATTN_BLOCK)
    return jnp.where(band, 0.0, NEG), jnp.where(band & (kj >= ATTN_BLOCK), 0.0, NEG)


def _qkv_specs(slab_of, sections):
    def spec(sect, back):
        return pl.BlockSpec((SLAB, HEAD_DIM),
                            lambda h, s, g: (jnp.maximum(slab_of(s) - back, 0), (sect * N_GROUPS + g) * HEADS + h))
    return [spec(sect, back) for sect, back in sections]


HAT_BLOCKS = [(0, 0), (1, 0), (2, 0), (1, 1), (2, 1)]


def _stage_keys(k_ref, v_ref, kp_ref, vp_ref, kbuf, vbuf, dil, n):
    run = SLAB // dil
    for r in range(dil):
        own, before = pl.ds(r * run, run), pl.ds(2 * r * run, run)
        kbuf[pl.ds((2 * r + 1) * run, run), :] = k_ref[own, :]
        vbuf[pl.ds((2 * r + 1) * run, run), :] = v_ref[own, :]

        @pl.when(n > 0)
        def _():
            kbuf[before, :] = kp_ref[own, :]
            vbuf[before, :] = vp_ref[own, :]

        @pl.when(n == 0)
        def _():
            kbuf[before, :] = jnp.zeros((run, HEAD_DIM), BF16)
            vbuf[before, :] = jnp.zeros((run, HEAD_DIM), BF16)


def _for_each_tile(dil, n, first_fn, rest_fn):
    run = SLAB // dil
    bias, first_bias = _band_bias()
    tiles = []
    for jj in range(run // ATTN_BLOCK):
        start = jj * ATTN_BLOCK
        tile_bias = jnp.where(n == 0, first_bias, bias) if jj == 0 else bias
        for r in range(dil):
            tiles.append((pl.ds(r * run + start, ATTN_BLOCK),
                          pl.ds((2 * r + 1) * run - ATTN_BLOCK + start, 2 * ATTN_BLOCK), tile_bias))
    ahead = first_fn(*tiles[0])
    for t, tile in enumerate(tiles):
        begun = ahead
        if t + 1 < len(tiles):
            ahead = first_fn(*tiles[t + 1])
        rest_fn(*tile, begun)


def _attn_fwd(hat, name):
    s_len = hat.shape[0]
    e = HEAD_DIM
    n_slabs = s_len // SLAB

    def body(q_ref, k_ref, v_ref, kp_ref, vp_ref, o_ref, lse_ref, kbuf, vbuf, m_s, l_s, acc_s, m_p, l_p, acc_p, tmp_s):
        n, grp = pl.program_id(1), pl.program_id(2)

        def run(gi, dil):
            _stage_keys(k_ref, v_ref, kp_ref, vp_ref, kbuf, vbuf, dil, n)

            def scores(q_rows, kv_rows, bias):
                return _dot_nt(q_ref[q_rows, :], kbuf[kv_rows, :])

            def rest(q_rows, kv_rows, bias, qk):
                s = qk * ATTN_SCALE + bias
                m = jnp.max(s, axis=-1, keepdims=True)
                p = jnp.exp(s - m)
                m_p[q_rows, :] = jnp.broadcast_to(m, (ATTN_BLOCK, e))
                l_p[q_rows, :] = jnp.broadcast_to(jnp.sum(p, axis=-1, keepdims=True), (ATTN_BLOCK, e))
                acc_p[q_rows, :] = _dot(p.astype(BF16), vbuf[kv_rows, :])

            _for_each_tile(dil, n, scores, rest)
            _to_token_order(m_s.at[gi], m_p, dil, tmp_s)
            _to_token_order(l_s.at[gi], l_p, dil, tmp_s)
            _to_token_order(acc_s.at[gi], acc_p, dil, tmp_s)

        for gi, dil in enumerate(DILATIONS):
            pl.when(grp == gi)(lambda gi=gi, dil=dil: run(gi, dil))

        @pl.when(grp == N_GROUPS - 1)
        def _():
            m_all = jnp.maximum(jnp.maximum(m_s[0], m_s[1]), m_s[2])
            den = jnp.zeros((SLAB, e), F32)
            num = jnp.zeros((SLAB, e), F32)
            for gi in range(N_GROUPS):
                w = jnp.exp(m_s[gi] - m_all)
                den += l_s[gi] * w
                num += acc_s[gi] * w
            o_ref[...] = num / den
            lse_ref[...] = m_all + jnp.log(den)

    out = pl.BlockSpec((SLAB, e), lambda h, n, g: (n, h))
    return pl.pallas_call(
        body, name=name, grid=(HEADS, n_slabs, N_GROUPS),
        in_specs=_qkv_specs(lambda n: n, HAT_BLOCKS),
        out_specs=[out, out],
        out_shape=[jax.ShapeDtypeStruct((s_len, HEADS * e), F32)] * 2,
        scratch_shapes=[pltpu.VMEM((2 * SLAB, e), BF16), pltpu.VMEM((2 * SLAB, e), BF16),
                        pltpu.VMEM((N_GROUPS, SLAB, e), F32), pltpu.VMEM((N_GROUPS, SLAB, e), F32),
                        pltpu.VMEM((N_GROUPS, SLAB, e), F32)]
        + [pltpu.VMEM((SLAB, e), F32)] * 4,
        compiler_params=_params(3),
    )(hat, hat, hat, hat, hat)


def _attn_bwd(qkv, hat, d_out, out, lse, q_norm, k_norm, name):
    s_len = qkv.shape[0]
    e = HEAD_DIM
    n_slabs = s_len // SLAB

    def body(q_ref, k_ref, v_ref, kp_ref, vp_ref, qraw_ref, kraw_ref, do_ref, o_ref, lse_ref, qn_ref, kn_ref,
             dq_ref, dk_ref, dv_ref, st_ref, kbuf, vbuf, stat_s, dqs, dkb, dvb, dk_tok, dv_tok, carry,
             do_p, stat_p, dq_p, dk_p, dv_p, tmp_s, do16_p):
        head, step, grp = pl.program_id(0), pl.program_id(1), pl.program_id(2)
        n = n_slabs - 1 - step
        dkb[...] = jnp.zeros_like(dkb)
        dvb[...] = jnp.zeros_like(dvb)
        @pl.when(grp == 0)
        def _():
            lane = lax.broadcasted_iota(jnp.int32, (SLAB, e), 1)
            stat_s[...] = jnp.where(lane < e // 2, lse_ref[...],
                                    jnp.sum(do_ref[...] * o_ref[...], axis=-1, keepdims=True))

        @pl.when((head == 0) & (step == 0) & (grp == 0))
        def _():
            st_ref[...] = jnp.zeros_like(st_ref)

        def run(gi, dil):
            seg = SLAB // dil
            _stage_keys(k_ref, v_ref, kp_ref, vp_ref, kbuf, vbuf, dil, n)

            @pl.when(step == 0)
            def _():
                carry[gi] = jnp.zeros((2, SLAB, e), F32)

            _to_residue_order(do_p, do_ref, dil, tmp_s)
            do16_p[...] = do_p[...].astype(BF16)
            _to_residue_order(stat_p, stat_s, dil, tmp_s)

            def scores(q_rows, kv_rows, bias):
                return _dot_nt(q_ref[q_rows, :], kbuf[kv_rows, :]), _dot_nt(do16_p[q_rows, :], vbuf[kv_rows, :])

            def rest(q_rows, kv_rows, bias, begun):
                qk, dp = begun
                q = q_ref[q_rows, :]
                k = kbuf[kv_rows, :]
                stat = stat_p[q_rows, :]
                p = jnp.exp(qk * ATTN_SCALE + bias - stat[:, 0:1])
                ds = (p * (dp - stat[:, e // 2:e // 2 + 1]) * ATTN_SCALE).astype(BF16)
                dq_p[q_rows, :] = _dot(ds, k)
                dkb[kv_rows, :] += _dot_tn(ds, q)
                dvb[kv_rows, :] += _dot_tn(p.astype(BF16), do16_p[q_rows, :])

            _for_each_tile(dil, n, scores, rest)
            for r in range(dil):
                own, before = pl.ds((2 * r + 1) * seg, seg), pl.ds(2 * r * seg, seg)
                kept = pl.ds(r * seg, seg)
                dk_p[kept, :] = dkb[own, :] + carry.at[gi, 0][kept, :]
                dv_p[kept, :] = dvb[own, :] + carry.at[gi, 1][kept, :]
                carry.at[gi, 0][kept, :] = dkb[before, :]
                carry.at[gi, 1][kept, :] = dvb[before, :]
            _to_token_order(dqs, dq_p, dil, tmp_s)
            _to_token_order(dk_tok, dk_p, dil, tmp_s)
            _to_token_order(dv_tok, dv_p, dil, tmp_s)

            def norm_bwd(raw, gain, d_hat):
                r = _rms(raw)
                y = raw * r
                dy = d_hat * gain
                return r * (dy - y * jnp.mean(dy * y, axis=-1, keepdims=True)), jnp.sum(d_hat * y, axis=0, keepdims=True)

            dq, dqn = norm_bwd(qraw_ref[...], qn_ref[...], dqs[...])
            dk, dkn = norm_bwd(kraw_ref[...], kn_ref[...], dk_tok[...])
            dq_ref[...] = dq.astype(BF16)
            dk_ref[...] = dk.astype(BF16)
            dv_ref[...] = dv_tok[...].astype(BF16)
            st_ref[0:1, :] += dqn
            st_ref[1:2, :] += dkn

        for gi, dil in enumerate(DILATIONS):
            pl.when(grp == gi)(lambda gi=gi, dil=dil: run(gi, dil))

    slab_of = lambda s: n_slabs - 1 - s
    small = pl.BlockSpec((1, e), lambda h, s, g: (0, 0))
    head_blk = pl.BlockSpec((SLAB, e), lambda h, s, g: (slab_of(s), h))
    grad_blk = pl.BlockSpec((SLAB, e), lambda h, s, g: (slab_of(s), g * HEADS + h))
    grad_shape = jax.ShapeDtypeStruct((s_len, QKV), BF16)
    return pl.pallas_call(
        body, name=name, grid=(HEADS, n_slabs, N_GROUPS),
        in_specs=(_qkv_specs(slab_of, HAT_BLOCKS) + _qkv_specs(slab_of, [(0, 0), (1, 0)])
                  + [head_blk, head_blk, head_blk, small, small]),
        out_specs=[grad_blk, grad_blk, grad_blk, pl.BlockSpec((8, e), lambda h, s, g: (0, 0))],
        out_shape=[grad_shape, grad_shape, grad_shape, jax.ShapeDtypeStruct((8, e), F32)],
        scratch_shapes=[pltpu.VMEM((2 * SLAB, e), BF16), pltpu.VMEM((2 * SLAB, e), BF16), pltpu.VMEM((SLAB, e), F32),
                        pltpu.VMEM((SLAB, e), F32), pltpu.VMEM((2 * SLAB, e), F32), pltpu.VMEM((2 * SLAB, e), F32),
                        pltpu.VMEM((SLAB, e), F32), pltpu.VMEM((SLAB, e), F32),
                        pltpu.VMEM((N_GROUPS, 2, SLAB, e), F32)]
        + [pltpu.VMEM((SLAB, e), F32)] * 6 + [pltpu.VMEM((SLAB, e), BF16)],
        compiler_params=_params(3),
    )(hat, hat, hat, hat, hat, qkv, qkv, d_out, out, lse, q_norm, k_norm)


def _shift_rows(x, by, edge, forward):
    t_len = x.shape[0]
    row = lax.broadcasted_iota(jnp.int32, x.shape, 0)
    if forward:
        out = pltpu.roll(x, by, 0)
        for i in range(by):
            out = jnp.where(row == i, edge[8 - by + i:8 - by + i + 1, :], out)
    else:
        out = pltpu.roll(x, t_len - by, 0)
        for i in range(by):
            out = jnp.where(row == t_len - by + i, edge[i:i + 1, :], out)
    return out


def _mix_fwd(x, o, rest, mod, mod_next, conv_w, w_attn, w_conv, w_out, name):
    s_len, d = x.shape
    tm = MIX_TILE
    a_w = o.shape[1]

    def body(x_ref, o_ref, u_ref, b_ref, c_ref, ga_ref, gc_ref, mod_ref, modn_ref, cw_ref, wa_ref, wc_ref, wo_ref,
             xo_ref, z_ref, ya_ref, yc_ref, conv_ref, yb_ref, m_ref, o16_ref, h_ref, ht_ref, carry):
        @pl.when(pl.program_id(0) == 0)
        def _():
            carry[...] = jnp.zeros_like(carry)

        xc = c_ref[...].astype(F32) * u_ref[...].astype(F32)
        edge = carry[...]
        conv = (_shift_rows(xc, 2, edge, True) * cw_ref[0:1, :] + _shift_rows(xc, 1, edge, True) * cw_ref[1:2, :]
                + xc * cw_ref[2:3, :])
        carry[...] = xc[tm - 8:tm, :]
        yb = (b_ref[...].astype(F32) * conv).astype(BF16)
        o16 = o_ref[...].astype(BF16)
        ya = _dot(o16, wa_ref[...])
        yc = _dot(yb, wc_ref[...])
        merged = (_sigmoid(ga_ref[...].astype(F32)) * ya + _sigmoid(gc_ref[...].astype(F32)) * yc).astype(BF16)
        z = _dot(merged, wo_ref[...])
        xo = x_ref[...] + mod_ref[2:3, :] * z
        xo_ref[...] = xo
        hn = ((xo * _rms(xo)) * modn_ref[3:4, :]) * (1.0 + modn_ref[1:2, :]) + modn_ref[0:1, :]
        h_ref[...] = hn.astype(BF16)
        ht_ref[...] = hn.T.astype(BF16)
        z_ref[...] = z.astype(BF16)
        ya_ref[...] = ya.astype(BF16)
        yc_ref[...] = yc.astype(BF16)
        conv_ref[...] = conv.astype(BF16)
        yb_ref[...] = yb
        m_ref[...] = merged
        o16_ref[...] = o16

    tile = pl.BlockSpec((tm, d), lambda i: (i, 0))
    sect = lambda k: pl.BlockSpec((tm, d), lambda i: (i, k))
    att = pl.BlockSpec((tm, a_w), lambda i: (i, 0))
    const = lambda shape: pl.BlockSpec(shape, lambda i: (0, 0))
    f32_out = jax.ShapeDtypeStruct((s_len, d), F32)
    b16_out = jax.ShapeDtypeStruct((s_len, d), BF16)
    return pl.pallas_call(
        body, name=name, grid=(s_len // tm,),
        in_specs=[tile, att, sect(0), sect(1), sect(2), sect(3), sect(4), const((8, d)), const((8, d)), const((8, d)),
                  const((a_w, d)), const((d, d)), const((d, d))],
        out_specs=[tile] * 7 + [att, tile, pl.BlockSpec((d, tm), lambda i: (0, i))],
        out_shape=[f32_out] + [b16_out] * 6 + [jax.ShapeDtypeStruct((s_len, a_w), BF16), b16_out,
                                               jax.ShapeDtypeStruct((d, s_len), BF16)],
        scratch_shapes=[pltpu.VMEM((8, d), F32)],
        compiler_params=_params(1),
    )(x, o, rest, rest, rest, rest, rest, mod, mod_next, conv_w, w_attn, w_conv, w_out)


def _mix_bwd(dxo, ya, yc, conv, rest, mod, conv_w, w_attn, w_conv, w_out, a_w, name, after=()):
    s_len, d = dxo.shape
    tm = MIX_TILE
    n_tiles = s_len // tm

    def body(dxo_ref, ya_ref, yc_ref, conv_ref, u_ref, b_ref, c_ref, ga_ref, gc_ref, mod_ref, cw_ref,
             wa_ref, wc_ref, wo_ref, do_ref, drest_ref, dz_ref, dya_ref, dyc_ref, st_ref, carry):
        @pl.when(pl.program_id(0) == 0)
        def _():
            carry[...] = jnp.zeros_like(carry)
            st_ref[...] = jnp.zeros_like(st_ref)

        dz = (mod_ref[2:3, :] * dxo_ref[...]).astype(BF16)
        dz_ref[...] = dz
        dm = _dot_nt(dz, wo_ref[...])
        sa, sc = _sigmoid(ga_ref[...].astype(F32)), _sigmoid(gc_ref[...].astype(F32))
        dya = (dm * sa).astype(BF16)
        dyc = (dm * sc).astype(BF16)
        dya_ref[...] = dya
        dyc_ref[...] = dyc
        drest_ref[:, 3 * d:4 * d] = (dm * ya_ref[...].astype(F32) * (sa * (1.0 - sa))).astype(BF16)
        drest_ref[:, 4 * d:5 * d] = (dm * yc_ref[...].astype(F32) * (sc * (1.0 - sc))).astype(BF16)
        do_ref[...] = _dot_nt(dya, wa_ref[...])
        dyb = _dot_nt(dyc, wc_ref[...])
        drest_ref[:, d:2 * d] = (dyb * conv_ref[...].astype(F32)).astype(BF16)
        dconv = dyb * b_ref[...].astype(F32)
        edge = carry[...]
        sh1 = _shift_rows(dconv, 1, edge, False)
        sh2 = _shift_rows(dconv, 2, edge, False)
        carry[...] = dconv[0:8, :]
        dxc = dconv * cw_ref[2:3, :] + sh1 * cw_ref[1:2, :] + sh2 * cw_ref[0:1, :]
        u, c = u_ref[...].astype(F32), c_ref[...].astype(F32)
        xc = c * u
        drest_ref[:, 0:d] = (dxc * c).astype(BF16)
        drest_ref[:, 2 * d:3 * d] = (dxc * u).astype(BF16)
        st_ref[0:1, :] += jnp.sum(xc * sh2, axis=0, keepdims=True)
        st_ref[1:2, :] += jnp.sum(xc * sh1, axis=0, keepdims=True)
        st_ref[2:3, :] += jnp.sum(xc * dconv, axis=0, keepdims=True)

    rev = lambda i: n_tiles - 1 - i
    tile = pl.BlockSpec((tm, d), lambda i: (rev(i), 0))
    sect = lambda k: pl.BlockSpec((tm, d), lambda i: (rev(i), k))
    const = lambda shape: pl.BlockSpec(shape, lambda i: (0, 0))
    b16_out = jax.ShapeDtypeStruct((s_len, d), BF16)
    return pl.pallas_call(
        _ordered(body, 14, after), name=name, grid=(n_tiles,),
        in_specs=[tile, tile, tile, tile, sect(0), sect(1), sect(2), sect(3), sect(4), const((8, d)), const((8, d)),
                  const((a_w, d)), const((d, d)), const((d, d))] + [ANY] * len(after),
        out_specs=[pl.BlockSpec((tm, a_w), lambda i: (rev(i), 0)), pl.BlockSpec((tm, 5 * d), lambda i: (rev(i), 0)),
                   tile, tile, tile, const((8, d))],
        out_shape=[jax.ShapeDtypeStruct((s_len, a_w), F32), jax.ShapeDtypeStruct((s_len, 5 * d), BF16),
                   b16_out, b16_out, b16_out, jax.ShapeDtypeStruct((8, d), F32)],
        scratch_shapes=[pltpu.VMEM((8, d), F32)],
        compiler_params=_params(1),
    )(dxo, ya, yc, conv, rest, rest, rest, rest, rest, mod, conv_w, w_attn, w_conv, w_out, *after)


ADA_COLS = 128


def _ada_fwd(c_all, w_shard, b_shard, name):
    d, cols = w_shard.shape

    def body(c_ref, w_ref, b_ref, o_ref):
        cv = c_ref[...]
        o_ref[...] = jnp.dot(cv * _sigmoid(cv), w_ref[...], preferred_element_type=F32,
                             precision=lax.Precision.HIGHEST) + b_ref[...]

    return pl.pallas_call(
        body, name=name, grid=(cols // ADA_COLS,),
        in_specs=[pl.BlockSpec((8, d), lambda j: (0, 0)), pl.BlockSpec((d, ADA_COLS), lambda j: (0, j)),
                  pl.BlockSpec((1, ADA_COLS), lambda j: (0, j))],
        out_specs=pl.BlockSpec((8, ADA_COLS), lambda j: (0, j)),
        out_shape=jax.ShapeDtypeStruct((8, cols), F32),
        compiler_params=_params(1),
    )(c_all, w_shard, b_shard)


def _ada_bwd(c_all, dmod_shard, w, m, v, name):
    d, cols = w.shape

    def body(c_ref, dm_ref, w_ref, m_ref, v_ref, g_ref, d_ref, nm_ref, nv_ref):
        cv = c_ref[...]
        g = lax.dot_general(cv * _sigmoid(cv), dm_ref[...], (((0,), (0,)), ((), ())),
                            preferred_element_type=F32, precision=lax.Precision.HIGHEST)
        g_ref[...] = g
        d_ref[...], nm_ref[...], nv_ref[...] = _adamw_math(w_ref[...], g, m_ref[...], v_ref[...])

    blk = pl.BlockSpec((d, ADA_COLS), lambda j: (0, j))
    shape = jax.ShapeDtypeStruct((d, cols), F32)
    return pl.pallas_call(
        body, name=name, grid=(cols // ADA_COLS,),
        in_specs=[pl.BlockSpec((8, d), lambda j: (0, 0)), pl.BlockSpec((8, ADA_COLS), lambda j: (0, j)), blk, blk, blk],
        out_specs=[blk] * 4, out_shape=[shape] * 4,
        compiler_params=_params(1),
    )(c_all, dmod_shard, w, m, v)


def _small_update(parts, w, m, v, name):
    n = w.shape[1]

    def body(p_ref, w_ref, m_ref, v_ref, g_ref, d_ref, nm_ref, nv_ref):
        g = p_ref[0:1, :]
        for i in range(1, 8):
            g = g + p_ref[i:i + 1, :]
        g_ref[...] = g
        d_ref[...], nm_ref[...], nv_ref[...] = _adamw_math(w_ref[...], g, m_ref[...], v_ref[...])

    shape = jax.ShapeDtypeStruct((1, n), F32)
    return pl.pallas_call(body, name=name, out_shape=[shape] * 4, compiler_params=_params())(parts, w, m, v)


def _cols_to_shards(w, n):
    r, nc = w.shape
    return w.reshape(r, n, nc // n).transpose(1, 0, 2)


def kernel(x, c, w_ada, b_ada, norm_ffn1, ffn1_w_gate, ffn1_w_up, ffn1_w_down, norm_mix, w_in, q_norm, k_norm, conv_w, w_attn_branch, w_conv_branch, w_out, norm_ffn2, ffn2_w_gate, ffn2_w_up, ffn2_w_down, loss_target, m_w_ada, m_b_ada, m_norm_ffn1, m_ffn1_w_gate, m_ffn1_w_up, m_ffn1_w_down, m_norm_mix, m_w_in, m_q_norm, m_k_norm, m_conv_w, m_w_attn_branch, m_w_conv_branch, m_w_out, m_norm_ffn2, m_ffn2_w_gate, m_ffn2_w_up, m_ffn2_w_down, v_w_ada, v_b_ada, v_norm_ffn1, v_ffn1_w_gate, v_ffn1_w_up, v_ffn1_w_down, v_norm_mix, v_w_in, v_q_norm, v_k_norm, v_conv_w, v_w_attn_branch, v_w_conv_branch, v_w_out, v_norm_ffn2, v_ffn2_w_gate, v_ffn2_w_up, v_ffn2_w_down):
    ix, iy, ic = _place()
    chip = 2 * ix + iy
    me = 4 * ix + 2 * iy + ic
    xs = x[0]
    target = loss_target[0]
    s_len, d = xs.shape
    ada_cols = w_ada.shape[2]
    conv_cols = conv_w.shape[2]

    conv_rows = jnp.zeros((8, conv_cols), F32).at[0:3].set(conv_w[0])
    small_in = jnp.concatenate([jnp.broadcast_to(c, (8, d)), conv_rows], axis=1)
    small_all = _allgather8(small_in, "gather_c").reshape(8, 8, d + conv_cols)
    c_all = small_all[:, 0, :d]
    conv_full = small_all[0::2, 0:3, d:].transpose(1, 0, 2).reshape(3, N_CHIPS * conv_cols)
    conv_pad = jnp.zeros((8, N_CHIPS * conv_cols), F32).at[0:3].set(conv_full)
    b_shard = lax.dynamic_slice(b_ada, (0, chip * ada_cols), (1, ada_cols))
    mod_part = _ada_fwd(c_all, w_ada[0], b_shard, "ada_fwd")
    mod_all = _allgather8(mod_part, "gather_mod").reshape(N_CHIPS, 2, 8, ada_cols)[:, 0]
    mod_mine = lax.dynamic_slice(mod_all, (0, me, 0), (N_CHIPS, 1, ada_cols)).reshape(9, d)

    def mod_rows(i, gain):
        return jnp.zeros((8, d), F32).at[0:3].set(mod_mine[3 * i:3 * i + 3]).at[3:4].set(gain)

    mod1, mod2, mod3 = mod_rows(0, norm_ffn1), mod_rows(1, norm_mix), mod_rows(2, norm_ffn2)

    to16 = lambda w: w[0].astype(BF16)
    wg1, wu1, wd1 = _gather_weights([to16(ffn1_w_gate), to16(ffn1_w_up), to16(ffn1_w_down)], [False] * 3,
                                    "gather_ffn1", 1)
    h1, h1t = _norm_mod(xs, mod1, "norm1")
    (w_in_full,) = _gather_weights([to16(w_in)], [True], "gather_w_in", 2, after=(wd1, h1))

    g1, u1, y1 = _ffn_fwd(h1, wg1, wu1, wd1, "ffn1_fwd")
    x1, h2, h2t = _norm_mod(xs, mod2, "norm2", prev=(y1, mod1, 0.5))
    qkv, rest, qkv_hat = _in_proj(h2, w_in_full, q_norm, k_norm, "in_proj")
    w_ab, w_cb_g, w_o_g, wg2, wu2, wd2 = _gather_weights(
        [to16(w_attn_branch), to16(w_conv_branch), to16(w_out),
         to16(ffn2_w_gate), to16(ffn2_w_up), to16(ffn2_w_down)], [True] + [False] * 5,
        "gather_rest", 3, after=(h2,))
    a_w = w_ab.shape[0]
    w_cb = w_cb_g.reshape(d, d)
    w_o = w_o_g.reshape(d, d)
    o, lse = _attn_fwd(qkv_hat, "attn_fwd")
    x2, z, ya, yc, conv, yb, merged, o16, h3, h3t = _mix_fwd(x1, o, rest, mod2, mod3, conv_pad, w_ab, w_cb, w_o,
                                                             "mix_fwd")
    g3, u3, y3 = _ffn_fwd(h3, wg2, wu2, wd2, "ffn2_fwd")
    dx3, dy3, loss_part = _loss_grad(x2, y3, mod3, target, "loss")

    c_idx = jnp.reshape(ic, (1,)).astype(jnp.int32)
    chip_idx = jnp.stack([chip, ic]).astype(jnp.int32)

    def pair_send(grads, tag, collective_id):
        return _rs_pair_exchange(grads, "rs_pair_" + tag, collective_id)

    def chip_send(grads, from_sibling, names, tag, collective_id, after):
        pair_sums = [_pair_add(g, r, c_idx, "pair_add_" + nm, after) for g, r, nm in zip(grads, from_sibling, names)]
        return pair_sums, _rs_chip_exchange(pair_sums, "rs_chips_" + tag, collective_id)

    def reduce_finish(pair_sums, from_chips, names, tag, after):
        totals = [_chip_add(p, r, chip_idx, "chip_add_" + nm, after)
                  for p, r, nm in zip(pair_sums, from_chips, names)]
        return dict(zip(names, _rs_share(totals, "rs_share_" + tag)))

    names_a = ["ffn2_w_gate", "ffn2_w_up", "ffn2_w_down"]
    names_b = ["w_in", "w_attn_branch", "w_conv_branch", "w_out"]
    names_c = ["ffn1_w_gate", "ffn1_w_up", "ffn1_w_down"]

    dh3, dg3, du3, a3 = _ffn_bwd(dy3, g3, u3, wg2, wu2, wd2, "ffn2_bwd")
    grads_a = list(_ffn_wgrads(h3t, dg3, du3, a3, dy3, "ffn2"))
    sibling_a = pair_send(grads_a, "a", 7)
    dx2, st3 = _norm_bwd(dh3, x2, mod3, dx3, y3, 0.5, "norm3_bwd")
    sums_a, chips_a = chip_send(grads_a, sibling_a, names_a, "a", 4, after=(dx2,))

    do, drest, dz, dya, dyc, st_conv = _mix_bwd(dx2, ya, yc, conv, rest, mod2, conv_pad, w_ab, w_cb, w_o, a_w,
                                                "mix_bwd", after=tuple(sums_a))
    dq, dk, dv, st_qk = _attn_bwd(qkv, qkv_hat, do, o, lse, q_norm, k_norm, "attn_bwd")
    tok = lambda width: (lambda ts: pl.BlockSpec((ts, width), lambda cc, s: (s, 0)))
    colblk = lambda width: (lambda ts: pl.BlockSpec((ts, width), lambda cc, s: (s, cc)))
    tok_t = lambda ts: pl.BlockSpec((d, ts), lambda cc, s: (0, s))
    whole = pl.BlockSpec((d, QKV), lambda cc, s: (0, 0))
    dw_in = [_wgrad(h2t, part, tok_t, tok(QKV), (d, QKV), whole, (d, QKV), 1, "dw_in_" + nm, True)
             for part, nm in ((dq, "q"), (dk, "k"), (dv, "v"))]
    dw_in.append(_wgrad(h2t, drest, tok_t, colblk(d), (d, 5 * d), pl.BlockSpec((d, d), lambda cc, s: (0, cc)),
                        (d, d), 5, "dw_in_rest", True))
    dw_in = _cols_to_shards(jnp.concatenate(dw_in, axis=1), N_CHIPS)
    shard_w = d // N_CHIPS
    dw_ab = _wgrad(o16, dya, tok(a_w), colblk(shard_w), (a_w, d), pl.BlockSpec((a_w, shard_w), lambda cc, s: (0, cc)),
                   (a_w, shard_w), N_CHIPS, "dw_attn_branch")
    dw_ab = _cols_to_shards(dw_ab, N_CHIPS)
    row_out = pl.BlockSpec((None, shard_w, d), lambda cc, s: (cc, 0, 0))
    dw_cb = _wgrad(yb, dyc, colblk(shard_w), tok(d), (N_CHIPS, shard_w, d), row_out, (shard_w, d), N_CHIPS, "dw_conv_branch")
    dw_o = _wgrad(merged, dz, colblk(shard_w), tok(d), (N_CHIPS, shard_w, d), row_out, (shard_w, d), N_CHIPS, "dw_out")
    shard_grads = reduce_finish(sums_a, chips_a, names_a, "a", after=(dw_in, dw_o))
    grads_b = [dw_in, dw_ab, dw_cb, dw_o]
    sibling_b = pair_send(grads_b, "b", 8)

    dh2 = _in_proj_bwd(dq, dk, dv, drest, w_in_full, "in_proj_bwd")
    sums_b, chips_b = chip_send(grads_b, sibling_b, names_b, "b", 5, after=(dh2,))
    dx1, st2, dy1 = _norm_bwd(dh2, x1, mod2, dx2, z, 1.0, "norm2_bwd", after=tuple(sums_b), prev=(mod1, 0.5))
    dh1, dg1, du1, a1 = _ffn_bwd(dy1, g1, u1, wg1, wu1, wd1, "ffn1_bwd")
    dx0, st1 = _norm_bwd(dh1, xs, mod1, dx1, y1, 0.5, "norm1_bwd")
    grads_c = list(_ffn_wgrads(h1t, dg1, du1, a1, dy1, "ffn1"))
    sibling_c = pair_send(grads_c, "c", 9)
    shard_grads.update(reduce_finish(sums_b, chips_b, names_b, "b", after=tuple(grads_c)))

    dmod = jnp.concatenate([st1[0:3], st2[0:3], st3[0:3]], axis=0).reshape(1, 9 * d)
    loss_cols = jnp.zeros((1, HEAD_DIM), F32).at[0, 0].set(jnp.sum(loss_part))
    small = jnp.concatenate([dmod, st1[3:4], st2[3:4], st3[3:4], st_qk[0:1], st_qk[1:2],
                             st_conv[0:3].reshape(1, 3 * d), loss_cols], axis=1)
    small_all = _allgather8(jnp.broadcast_to(small, (8, small.shape[1])), "gather_small").reshape(8, 8, -1)[:, 0]
    loss = (0.5 / d) * jnp.sum(small_all[:, -HEAD_DIM])
    small_all = small_all[:, :-HEAD_DIM]
    dmod_all = small_all[:, :9 * d]
    dmod_shard = lax.dynamic_slice(dmod_all, (0, chip * ada_cols), (8, ada_cols))
    g_w_ada, d_w_ada, nm_w_ada, nv_w_ada = _ada_bwd(c_all, dmod_shard, w_ada[0], m_w_ada[0], v_w_ada[0], "ada_bwd")

    vec_names = ["b_ada", "norm_ffn1", "norm_mix", "norm_ffn2", "q_norm", "k_norm"]
    vec_w = [b_ada, norm_ffn1, norm_mix, norm_ffn2, q_norm, k_norm]
    vec_m = [m_b_ada, m_norm_ffn1, m_norm_mix, m_norm_ffn2, m_q_norm, m_k_norm]
    vec_v = [v_b_ada, v_norm_ffn1, v_norm_mix, v_norm_ffn2, v_q_norm, v_k_norm]
    n_vec = sum(w.shape[1] for w in vec_w)
    cat = lambda arrs: jnp.concatenate(arrs, axis=1)
    vec_out = _small_update(small_all[:, :n_vec], cat(vec_w), cat(vec_m), cat(vec_v), "small_update")
    conv_parts = small_all[:, n_vec:].reshape(8, 3, N_CHIPS * conv_cols)
    conv_parts = lax.dynamic_slice(conv_parts, (0, 0, chip * conv_cols), (8, 3, conv_cols)).reshape(8, 3 * conv_cols)
    flat3 = lambda w: w[0].reshape(1, 3 * conv_cols)
    conv_out = _small_update(conv_parts, flat3(conv_w), flat3(m_conv_w), flat3(v_conv_w), "conv_update")

    res = {"w_ada": [t[None] for t in (g_w_ada, d_w_ada, nm_w_ada, nv_w_ada)],
           "conv_w": [t.reshape(1, 3, conv_cols) for t in conv_out]}
    off = 0
    for nm, w in zip(vec_names, vec_w):
        width = w.shape[1]
        res[nm] = [t[:, off:off + width] for t in vec_out]
        off += width
    big = {"ffn1_w_gate": (ffn1_w_gate, m_ffn1_w_gate, v_ffn1_w_gate), "ffn1_w_up": (ffn1_w_up, m_ffn1_w_up, v_ffn1_w_up),
           "ffn1_w_down": (ffn1_w_down, m_ffn1_w_down, v_ffn1_w_down), "w_in": (w_in, m_w_in, v_w_in),
           "w_attn_branch": (w_attn_branch, m_w_attn_branch, v_w_attn_branch),
           "w_conv_branch": (w_conv_branch, m_w_conv_branch, v_w_conv_branch), "w_out": (w_out, m_w_out, v_w_out),
           "ffn2_w_gate": (ffn2_w_gate, m_ffn2_w_gate, v_ffn2_w_gate), "ffn2_w_up": (ffn2_w_up, m_ffn2_w_up, v_ffn2_w_up),
           "ffn2_w_down": (ffn2_w_down, m_ffn2_w_down, v_ffn2_w_down)}
    def update(nm, after=()):
        w, m, v = big[nm]
        g, delta, new_m, new_v = _adamw(w[0], shard_grads[nm], m[0], v[0], "adamw_" + nm, after)
        res[nm] = [t[None] for t in (g, delta, new_m, new_v)]
        return new_v

    last = tuple(shard_grads[nm] for nm in names_b)
    for nm in names_a:
        last = (update(nm, last),)
    sums_c, chips_c = chip_send(grads_c, sibling_c, names_c, "c", 6, after=last)
    last = tuple(sums_c)
    for nm in names_b:
        last = (update(nm, last),)
    shard_grads.update(reduce_finish(sums_c, chips_c, names_c, "c", after=last))
    for nm in names_c:
        update(nm)

    order = ["w_ada", "b_ada", "norm_ffn1", "ffn1_w_gate", "ffn1_w_up", "ffn1_w_down", "norm_mix", "w_in", "q_norm",
             "k_norm", "conv_w", "w_attn_branch", "w_conv_branch", "w_out", "norm_ffn2", "ffn2_w_gate", "ffn2_w_up",
             "ffn2_w_down"]
    return (loss, dx0[None], *[res[nm][0] for nm in order], *[res[nm][1] for nm in order],
            *[res[nm][2] for nm in order], *[res[nm][3] for nm in order])
```

```python
import jax
import jax.numpy as jnp
from jax import lax
from jax.experimental import pallas as pl
from jax.experimental.pallas import tpu as pltpu
from jax.experimental.pallas import tpu_sc as plsc

F32 = jnp.float32
BF16 = jnp.bfloat16
MESH = pl.DeviceIdType.MESH
ANY = pl.BlockSpec(memory_space=pl.ANY)

NORM_EPS = 1e-6
HEAD_DIM = 128
N_GROUPS = 3
HEADS = 4
DILATIONS = (1, 4, 16)
ATTN_BLOCK = 128
SLAB = ATTN_BLOCK * max(DILATIONS)
QKV = N_GROUPS * HEADS * HEAD_DIM
ATTN_SCALE = HEAD_DIM ** -0.5
NEG = -1e30
N_CHIPS = 4

ADAM_LR = 0.001
ADAM_B1 = 0.9
ADAM_B2 = 0.999
ADAM_EPS = 1e-08
ADAM_WD = 0.01
ADAM_STEP = 10

VMEM_LIMIT_BYTES = 56 * 1024 * 1024
TOKEN_TILE = 512
FFN_TILE = 1024
PROJ_TILE = 2048
WGRAD_TILE = 2048
IN_BLOCK = 512
MIX_TILE = 256
ADAMW_TILE_BYTES = 3 * 512 * 1024


def _params(n_axes=0):
    return pltpu.CompilerParams(
        dimension_semantics=("arbitrary",) * n_axes if n_axes else None,
        vmem_limit_bytes=VMEM_LIMIT_BYTES)


def _dot(a, b):
    return jnp.dot(a, b, preferred_element_type=F32)


def _dot_nt(a, b):
    return lax.dot_general(a, b, (((1,), (1,)), ((), ())), preferred_element_type=F32)


def _dot_tn(a, b):
    return lax.dot_general(a, b, (((0,), (0,)), ((), ())), preferred_element_type=F32)


def _sigmoid(x):
    return 1.0 / (1.0 + jnp.exp(-x))


def _place():
    return lax.axis_index("x"), lax.axis_index("y"), lax.axis_index("c")


def _ordered(body, n_in, after):
    if not after:
        return body
    return lambda *refs: body(*refs[:n_in], *refs[n_in + len(after):])


def _allgather8(block, name):
    m_per, n = block.shape

    def body(x_ref, out_ref, send_sems, recv_sems, local_sem):
        x, y, c = _place()
        me, sibling = (x, y, c), (x, y, 1 - c)
        chips = [(1 - x, y), (x, 1 - y), (1 - x, 1 - y)]

        def rows(px, py, pc):
            return out_ref.at[pl.ds((4 * px + 2 * py + pc) * m_per, m_per), :]

        def copy(k, blk, to, src=None):
            return pltpu.make_async_remote_copy(
                src_ref=rows(*blk) if src is None else src, dst_ref=rows(*blk),
                send_sem=send_sems.at[k], recv_sem=recv_sems.at[k],
                device_id=to, device_id_type=MESH)

        mine = pltpu.make_async_copy(x_ref, rows(*me), local_sem)
        mine.start()
        first = [copy(0, me, sibling, src=x_ref)]
        first += [copy(1 + j, me, (*chip, c), src=x_ref) for j, chip in enumerate(chips)]
        for cp in first:
            cp.start()
        passed = [copy(4 + j, (*chip, c), sibling) for j, chip in enumerate(chips)]
        for j, chip in enumerate(chips):
            copy(1 + j, (*chip, c), me).wait_recv()
            passed[j].start()
        copy(0, sibling, me).wait_recv()
        for j, chip in enumerate(chips):
            copy(4 + j, (*chip, 1 - c), me).wait_recv()
        for cp in first + passed:
            cp.wait_send()
        mine.wait()

    return pl.pallas_call(
        body, name=name,
        out_shape=jax.ShapeDtypeStruct((8 * m_per, n), block.dtype),
        in_specs=[pl.BlockSpec(memory_space=pltpu.VMEM)],
        out_specs=pl.BlockSpec(memory_space=pltpu.VMEM),
        scratch_shapes=[pltpu.SemaphoreType.DMA((7,)), pltpu.SemaphoreType.DMA((7,)),
                        pltpu.SemaphoreType.DMA],
        compiler_params=_params(),
    )(block)


def _handshake(peers):
    barrier = pltpu.get_barrier_semaphore()
    for peer in peers:
        pl.semaphore_signal(barrier, inc=1, device_id=peer, device_id_type=MESH)
    pl.semaphore_wait(barrier, len(peers))


def _gather_weights(shards, by_cols, name, collective_id, after=()):
    n_arr = len(shards)

    def body(*refs):
        srcs, outs = refs[:n_arr], refs[n_arr + len(after):2 * n_arr + len(after)]
        send_sems, recv_sems, local_sems = refs[2 * n_arr + len(after):]
        x, y, c = _place()
        me_dev, sibling = (x, y, c), (x, y, 1 - c)
        chips = [(1 - x, y), (x, 1 - y), (1 - x, 1 - y)]
        me = 2 * x + y
        _handshake([sibling] + [(*chip, c) for chip in chips])

        def place(k, chip_idx, rows):
            if by_cols[k]:
                width = srcs[k].shape[1]
                return outs[k].at[rows, pl.ds(pl.multiple_of(chip_idx * width, 128), width)]
            return outs[k].at[chip_idx, rows]

        def copy(k, slot, chip_idx, half_sel, to, from_shard=False):
            half = srcs[k].shape[0] // 2
            rows = pl.ds(half_sel * half, half)
            dst = place(k, chip_idx, rows)
            return pltpu.make_async_remote_copy(
                src_ref=srcs[k].at[rows] if from_shard else dst, dst_ref=dst,
                send_sem=send_sems.at[6 * k + slot], recv_sem=recv_sems.at[6 * k + slot],
                device_id=to, device_id_type=MESH)

        own = [pltpu.make_async_copy(srcs[k], place(k, me, pl.ds(0, srcs[k].shape[0])), local_sems.at[k])
               for k in range(n_arr)]
        for cp in own:
            cp.start()
        sent = []
        for k in range(n_arr):
            for j, chip in enumerate(chips):
                sent.append(copy(k, j, me, c, (*chip, c), from_shard=True))
                sent[-1].start()
        for k in range(n_arr):
            for j, chip in enumerate(chips):
                chip_idx = 2 * chip[0] + chip[1]
                copy(k, j, chip_idx, c, me_dev).wait_recv()
                sent.append(copy(k, 3 + j, chip_idx, c, sibling))
                sent[-1].start()
        for k in range(n_arr):
            for j, chip in enumerate(chips):
                copy(k, 3 + j, 2 * chip[0] + chip[1], 1 - c, me_dev).wait_recv()
        for cp in sent:
            cp.wait_send()
        for cp in own:
            cp.wait()

    def gathered(k):
        r, cols = shards[k].shape
        return (r, N_CHIPS * cols) if by_cols[k] else (N_CHIPS, r, cols)

    return pl.kernel(
        body, name=name,
        out_type=[jax.ShapeDtypeStruct(gathered(k), shards[k].dtype) for k in range(n_arr)],
        mesh=plsc.ScalarSubcoreMesh(axis_name="sequencer", num_cores=1),
        scratch_types=[pltpu.SemaphoreType.DMA((6 * n_arr,)), pltpu.SemaphoreType.DMA((6 * n_arr,)),
                       pltpu.SemaphoreType.DMA((n_arr,))],
        compiler_params=pltpu.CompilerParams(collective_id=collective_id),
    )(*shards, *after)


def _rs_pair_exchange(grads, name, collective_id):
    n_arr = len(grads)

    def body(*refs):
        srcs, outs = refs[:n_arr], refs[n_arr:2 * n_arr]
        send_sems, recv_sems = refs[2 * n_arr:]
        x, y, c = _place()
        _handshake([(x, y, 1 - c)])
        cps = []
        for k in range(n_arr):
            half = srcs[k].shape[1] // 2
            cps.append(pltpu.make_async_remote_copy(
                src_ref=srcs[k].at[:, pl.ds((1 - c) * half, half)], dst_ref=outs[k],
                send_sem=send_sems.at[k], recv_sem=recv_sems.at[k],
                device_id=(x, y, 1 - c), device_id_type=MESH))
            cps[-1].start()
        for cp in cps:
            cp.wait_recv()
        for cp in cps:
            cp.wait_send()

    return pl.kernel(
        body, name=name,
        out_type=[jax.ShapeDtypeStruct((g.shape[0], g.shape[1] // 2, g.shape[2]), g.dtype) for g in grads],
        mesh=plsc.ScalarSubcoreMesh(axis_name="sequencer", num_cores=1),
        scratch_types=[pltpu.SemaphoreType.DMA((n_arr,)), pltpu.SemaphoreType.DMA((n_arr,))],
        compiler_params=pltpu.CompilerParams(collective_id=collective_id),
    )(*grads)


def _rs_chip_exchange(sums, name, collective_id):
    n_arr = len(sums)

    def body(*refs):
        srcs, outs = refs[:n_arr], refs[n_arr:2 * n_arr]
        send_sems, recv_sems = refs[2 * n_arr:]
        x, y, c = _place()
        chips = [(1 - x, y), (x, 1 - y), (1 - x, 1 - y)]
        _handshake([(*chip, c) for chip in chips])
        cps = []
        for k in range(n_arr):
            for j, chip in enumerate(chips):
                cps.append(pltpu.make_async_remote_copy(
                    src_ref=srcs[k].at[2 * chip[0] + chip[1]], dst_ref=outs[k].at[j],
                    send_sem=send_sems.at[3 * k + j], recv_sem=recv_sems.at[3 * k + j],
                    device_id=(*chip, c), device_id_type=MESH))
                cps[-1].start()
        for cp in cps:
            cp.wait_recv()
        for cp in cps:
            cp.wait_send()

    return pl.kernel(
        body, name=name,
        out_type=[jax.ShapeDtypeStruct((3,) + s.shape[1:], s.dtype) for s in sums],
        mesh=plsc.ScalarSubcoreMesh(axis_name="sequencer", num_cores=1),
        scratch_types=[pltpu.SemaphoreType.DMA((3 * n_arr,)), pltpu.SemaphoreType.DMA((3 * n_arr,))],
        compiler_params=pltpu.CompilerParams(collective_id=collective_id),
    )(*sums)


def _rs_share(totals, name):
    n_arr = len(totals)

    def body(*refs):
        outs = refs[n_arr:2 * n_arr]
        send_sems, recv_sems = refs[2 * n_arr:]
        x, y, c = _place()

        def half_rows(k, sel):
            return outs[k].at[sel]

        cps = []
        for k in range(n_arr):
            cps.append(pltpu.make_async_remote_copy(
                src_ref=half_rows(k, c), dst_ref=half_rows(k, c), send_sem=send_sems.at[k], recv_sem=recv_sems.at[k],
                device_id=(x, y, 1 - c), device_id_type=MESH))
            cps[-1].start()
        for k in range(n_arr):
            pltpu.make_async_remote_copy(
                src_ref=half_rows(k, c), dst_ref=half_rows(k, 1 - c), send_sem=send_sems.at[k],
                recv_sem=recv_sems.at[k], device_id=(x, y, 1 - c), device_id_type=MESH).wait_recv()
        for cp in cps:
            cp.wait_send()

    shared = pl.pallas_call(
        body, name=name,
        out_shape=[jax.ShapeDtypeStruct(t.shape, t.dtype) for t in totals],
        in_specs=[ANY] * n_arr, out_specs=[ANY] * n_arr,
        input_output_aliases={k: k for k in range(n_arr)},
        scratch_shapes=[pltpu.SemaphoreType.DMA((n_arr,)), pltpu.SemaphoreType.DMA((n_arr,))],
        compiler_params=_params(),
    )(*totals)
    return [t.reshape(2 * t.shape[1], t.shape[2]) for t in shared]


def _pair_add(grad, recv, c_idx, name, after=()):
    n, r, cols = grad.shape
    half = r // 2
    rows = half // 2

    def body(_, g_ref, r_ref, o_ref):
        o_ref[...] = (g_ref[...].astype(F32) + r_ref[...].astype(F32)).astype(o_ref.dtype)

    return pl.pallas_call(
        _ordered(body, 3, after), name=name,
        grid_spec=pltpu.PrefetchScalarGridSpec(
            num_scalar_prefetch=1, grid=(n, 2),
            in_specs=[pl.BlockSpec((None, None, rows, cols), lambda s, i, ci: (s, ci[0], i, 0)),
                      pl.BlockSpec((None, rows, cols), lambda s, i, ci: (s, i, 0))] + [ANY] * len(after),
            out_specs=pl.BlockSpec((None, rows, cols), lambda s, i, ci: (s, i, 0))),
        out_shape=jax.ShapeDtypeStruct((n, half, cols), BF16),
        compiler_params=_params(2),
    )(c_idx, grad.reshape(n, 2, half, cols), recv, *after)


def _chip_add(sums, recv, chip_and_core, name, after=()):
    _, half, cols = sums.shape
    rows = half // 2

    def body(_, s_ref, r0_ref, r1_ref, r2_ref, o_ref):
        o_ref[...] = ((s_ref[...].astype(F32) + r0_ref[...].astype(F32))
                      + r1_ref[...].astype(F32)) + r2_ref[...].astype(F32)

    def recv_spec(j):
        return pl.BlockSpec((None, rows, cols), lambda i, ci: (j, i, 0))

    return pl.pallas_call(
        _ordered(body, 5, after), name=name,
        grid_spec=pltpu.PrefetchScalarGridSpec(
            num_scalar_prefetch=1, grid=(2,),
            in_specs=[pl.BlockSpec((None, rows, cols), lambda i, ci: (ci[0], i, 0)),
                      recv_spec(0), recv_spec(1), recv_spec(2)] + [ANY] * len(after),
            out_specs=pl.BlockSpec((None, rows, cols), lambda i, ci: (ci[1], i, 0))),
        out_shape=jax.ShapeDtypeStruct((2, half, cols), F32),
        compiler_params=_params(1),
    )(chip_and_core, sums, recv, recv, recv, *after)


def _rms(x):
    return lax.rsqrt(jnp.mean(x * x, axis=-1, keepdims=True) + NORM_EPS)


def _norm_mod(x, mod, name, prev=None):
    s_len, d = x.shape
    tm = TOKEN_TILE

    def body(*refs):
        if prev is None:
            x_ref, mod_ref, h_ref, ht_ref = refs
            xv = x_ref[...]
        else:
            x_ref, y_ref, modp_ref, mod_ref, xo_ref, h_ref, ht_ref = refs
            xv = x_ref[...] + prev[2] * modp_ref[2:3, :] * y_ref[...]
            xo_ref[...] = xv
        n = (xv * _rms(xv)) * mod_ref[3:4, :]
        h = n * (1.0 + mod_ref[1:2, :]) + mod_ref[0:1, :]
        h_ref[...] = h.astype(BF16)
        ht_ref[...] = h.T.astype(BF16)

    tile = pl.BlockSpec((tm, d), lambda i: (i, 0))
    small = pl.BlockSpec((8, d), lambda i: (0, 0))
    h_specs = [tile, pl.BlockSpec((d, tm), lambda i: (0, i))]
    h_shapes = [jax.ShapeDtypeStruct((s_len, d), BF16), jax.ShapeDtypeStruct((d, s_len), BF16)]
    if prev is None:
        return pl.pallas_call(
            body, name=name, grid=(s_len // tm,), in_specs=[tile, small], out_specs=h_specs, out_shape=h_shapes,
            compiler_params=_params(1))(x, mod)
    return pl.pallas_call(
        body, name=name, grid=(s_len // tm,), in_specs=[tile, tile, small, small],
        out_specs=[tile] + h_specs, out_shape=[jax.ShapeDtypeStruct((s_len, d), F32)] + h_shapes,
        compiler_params=_params(1))(x, prev[0], prev[1], mod)


def _norm_bwd(dh, x, mod, dxo, y_raw, coef, name, after=(), prev=None):
    s_len, d = x.shape
    tm = TOKEN_TILE

    def body(*refs):
        if prev is None:
            dh_ref, x_ref, mod_ref, dxo_ref, y_ref, dx_ref, st_ref = refs
        else:
            dh_ref, x_ref, mod_ref, dxo_ref, y_ref, modp_ref, dx_ref, st_ref, dyp_ref = refs

        @pl.when(pl.program_id(0) == 0)
        def _():
            st_ref[...] = jnp.zeros_like(st_ref)

        xv, dhv, dxov = x_ref[...], dh_ref[...], dxo_ref[...]
        r = _rms(xv)
        xh = xv * r
        gain, scale = mod_ref[3:4, :], mod_ref[1:2, :]
        dn = dhv * (1.0 + scale)
        dxh = dn * gain
        dx = dxov + r * (dxh - xh * jnp.mean(dxh * xh, axis=-1, keepdims=True))
        dx_ref[...] = dx
        if prev is not None:
            dyp_ref[...] = (prev[1] * modp_ref[2:3, :] * dx).astype(BF16)
        st_ref[0:1, :] += jnp.sum(dhv, axis=0, keepdims=True)
        st_ref[1:2, :] += jnp.sum(dhv * (xh * gain), axis=0, keepdims=True)
        st_ref[2:3, :] += coef * jnp.sum(y_ref[...].astype(F32) * dxov, axis=0, keepdims=True)
        st_ref[3:4, :] += jnp.sum(dn * xh, axis=0, keepdims=True)

    tile = pl.BlockSpec((tm, d), lambda i: (i, 0))
    small = pl.BlockSpec((8, d), lambda i: (0, 0))
    operands = [dh, x, mod, dxo, y_raw] + ([] if prev is None else [prev[0]])
    in_specs = [tile, tile, small, tile, tile] + ([] if prev is None else [small])
    out_specs = [tile, small] + ([] if prev is None else [tile])
    out_shape = [jax.ShapeDtypeStruct((s_len, d), F32), jax.ShapeDtypeStruct((8, d), F32)]
    if prev is not None:
        out_shape.append(jax.ShapeDtypeStruct((s_len, d), BF16))
    return pl.pallas_call(
        _ordered(body, len(operands), after), name=name, grid=(s_len // tm,),
        in_specs=in_specs + [ANY] * len(after), out_specs=out_specs, out_shape=out_shape,
        compiler_params=_params(1),
    )(*operands, *after)


def _loss_grad(x, y, mod, target, name):
    s_len, d = x.shape
    tm = TOKEN_TILE

    def body(x_ref, y_ref, mod_ref, t_ref, do_ref, dy_ref, part_ref):
        @pl.when(pl.program_id(0) == 0)
        def _():
            part_ref[...] = jnp.zeros_like(part_ref)

        half_gate = 0.5 * mod_ref[2:3, :]
        err = (x_ref[...] + half_gate * y_ref[...]) - t_ref[...]
        do = err * (1.0 / d)
        do_ref[...] = do
        dy_ref[...] = (half_gate * do).astype(BF16)
        sq = err * err
        part_ref[...] += jnp.sum(sq.reshape(tm // 8, 8, d), axis=0)

    tile = pl.BlockSpec((tm, d), lambda i: (i, 0))
    small = pl.BlockSpec((8, d), lambda i: (0, 0))
    return pl.pallas_call(
        body, name=name, grid=(s_len // tm,),
        in_specs=[tile, tile, small, tile],
        out_specs=[tile, tile, small],
        out_shape=[jax.ShapeDtypeStruct((s_len, d), F32), jax.ShapeDtypeStruct((s_len, d), BF16),
                   jax.ShapeDtypeStruct((8, d), F32)],
        compiler_params=_params(1),
    )(x, y, mod, target)


def _adamw_math(w, g, m, v):
    m = ADAM_B1 * m + (1.0 - ADAM_B1) * g
    v = ADAM_B2 * v + (1.0 - ADAM_B2) * (g * g)
    m_hat = m / (1.0 - ADAM_B1 ** ADAM_STEP)
    v_hat = v / (1.0 - ADAM_B2 ** ADAM_STEP)
    delta = -ADAM_LR * (m_hat / (jnp.sqrt(v_hat) + ADAM_EPS) + ADAM_WD * w)
    return delta, m, v


def _adamw(w, g, m, v, name, after=()):
    r, cols = w.shape
    tr = max([t for t in (r // k for k in (1, 2, 4, 8, 16)) if t % 8 == 0 and r % t == 0
              and t * cols * 4 <= ADAMW_TILE_BYTES] or [r])

    def body(w_ref, g_ref, m_ref, v_ref, go_ref, d_ref, nm_ref, nv_ref):
        gv = g_ref[...]
        go_ref[...] = gv
        d_ref[...], nm_ref[...], nv_ref[...] = _adamw_math(w_ref[...], gv, m_ref[...], v_ref[...])

    tile = pl.BlockSpec((tr, cols), lambda i: (i, 0))
    shape = jax.ShapeDtypeStruct((r, cols), F32)
    return pl.pallas_call(
        _ordered(body, 4, after), name=name, grid=(r // tr,),
        in_specs=[tile] * 4 + [ANY] * len(after), out_specs=[tile] * 4, out_shape=[shape] * 4,
        compiler_params=_params(1),
    )(w, g, m, v, *after)


def _in_parts(tm, n_qkv, n_rest):
    def part(lo, n_blk):
        return pl.BlockSpec((tm, IN_BLOCK), lambda i, j: (i, jnp.clip(j - lo, 0, n_blk - 1)))
    return [part(0, n_qkv), part(n_qkv, n_qkv), part(2 * n_qkv, n_qkv), part(3 * n_qkv, n_rest)]


def _pick_part(j, n_qkv, refs, fn):
    bounds = [0, n_qkv, 2 * n_qkv, 3 * n_qkv]
    for p, ref in enumerate(refs):
        inside = j >= bounds[p]
        if p + 1 < len(refs):
            inside = inside & (j < bounds[p + 1])
        pl.when(inside)(lambda ref=ref: fn(ref))


def _rows(base, count, stride):
    return pl.ds(base, count) if stride == 1 else pl.ds(base, count, stride=stride)


REORDER_STRIDE = 4


def _reorder_plan(dil, parts=1):
    inner = min(dil, REORDER_STRIDE)
    return inner, dil // inner, SLAB // parts // inner, SLAB // dil


def _to_residue_order(dst, src, dil, tmp, part=0, parts=1):
    inner, outer, big, seg = _reorder_plan(dil, parts)
    piece = seg // parts
    if outer == 1:
        for r in range(dil):
            dst[pl.ds(r * seg + part * piece, piece), :] = src[_rows(r, piece, dil), :].astype(dst.dtype)
        return
    for b in range(inner):
        tmp[pl.ds(b * big, big), :] = src[_rows(b, big, inner), :]
    for a in range(outer):
        for b in range(inner):
            dst[pl.ds((inner * a + b) * seg + part * piece, piece), :] = (
                tmp[_rows(b * big + a, piece, outer), :].astype(dst.dtype))


def _to_token_order(dst, src, dil, tmp):
    inner, outer, big, seg = _reorder_plan(dil)
    if outer == 1:
        for r in range(dil):
            dst[_rows(r, seg, dil), :] = src[pl.ds(r * seg, seg), :]
        return
    for a in range(outer):
        for b in range(inner):
            tmp[_rows(b * big + a, seg, outer), :] = src[pl.ds((inner * a + b) * seg, seg), :]
    for b in range(inner):
        dst[_rows(b, big, inner), :] = tmp[pl.ds(b * big, big), :]


def _in_proj(h, w, q_norm, k_norm, name):
    s_len, d = h.shape
    tm = PROJ_TILE
    assert tm == SLAB and IN_BLOCK == HEADS * HEAD_DIM
    steps = w.shape[1] // IN_BLOCK
    n_qkv = 3 * QKV // IN_BLOCK
    halves = 2
    rows = [pl.ds(p * (tm // halves), tm // halves) for p in range(halves)]

    def body(h_ref, w_ref, qn_ref, kn_ref, qkv_ref, rest_ref, hat_ref, tok_s, tmp_s):
        j = pl.program_id(1)
        res = [_dot(h_ref[rows[p], :], w_ref[...]) for p in range(halves)]

        def emit(sect, gi):
            dil = DILATIONS[gi]
            for p in range(halves):
                qkv_ref[rows[p], :] = res[p]
                for hh in range(HEADS):
                    cols = slice(hh * HEAD_DIM, (hh + 1) * HEAD_DIM)
                    x = res[p][:, cols]
                    if sect < 2:
                        x = (x * _rms(x)) * (qn_ref if sect == 0 else kn_ref)[...]
                    tok_s[...] = x
                    _to_residue_order(hat_ref.at[:, cols], tok_s, dil, tmp_s, p, halves)

        for sect in range(3):
            for gi in range(N_GROUPS):
                pl.when(j == sect * N_GROUPS + gi)(lambda sect=sect, gi=gi: emit(sect, gi))

        @pl.when(j >= n_qkv)
        def _():
            for p in range(halves):
                rest_ref[rows[p], :] = res[p].astype(BF16)

    qkv_blk = pl.BlockSpec((tm, IN_BLOCK), lambda i, j: (i, jnp.minimum(j, n_qkv - 1)))
    small = pl.BlockSpec((1, HEAD_DIM), lambda i, j: (0, 0))
    return pl.pallas_call(
        body, name=name, grid=(s_len // tm, steps),
        in_specs=[pl.BlockSpec((tm, d), lambda i, j: (i, 0)), pl.BlockSpec((d, IN_BLOCK), lambda i, j: (0, j)),
                  small, small],
        out_specs=[qkv_blk, pl.BlockSpec((tm, IN_BLOCK), lambda i, j: (i, jnp.maximum(j - n_qkv, 0))), qkv_blk],
        out_shape=[jax.ShapeDtypeStruct((s_len, 3 * QKV), F32),
                   jax.ShapeDtypeStruct((s_len, w.shape[1] - 3 * QKV), BF16),
                   jax.ShapeDtypeStruct((s_len, 3 * QKV), BF16)],
        scratch_shapes=[pltpu.VMEM((tm // halves, HEAD_DIM), F32)] * 2,
        compiler_params=_params(2),
    )(h, w, q_norm, k_norm)


def _in_proj_bwd(dq, dk, dv, drest, w, name, after=()):
    s_len = dq.shape[0]
    d = w.shape[0]
    tm = PROJ_TILE
    steps = w.shape[1] // IN_BLOCK
    n_qkv = QKV // IN_BLOCK

    def body(dq_ref, dk_ref, dv_ref, dr_ref, w_ref, o_ref, acc_ref):
        j = pl.program_id(1)

        @pl.when(j == 0)
        def _():
            acc_ref[...] = jnp.zeros_like(acc_ref)

        def add(a_ref):
            acc_ref[...] += _dot_nt(a_ref[...], w_ref[...])

        _pick_part(j, n_qkv, [dq_ref, dk_ref, dv_ref, dr_ref], add)

        @pl.when(j == steps - 1)
        def _():
            o_ref[...] = acc_ref[...]

    return pl.pallas_call(
        _ordered(body, 5, after), name=name, grid=(s_len // tm, steps),
        in_specs=(_in_parts(tm, n_qkv, steps - 3 * n_qkv) + [pl.BlockSpec((d, IN_BLOCK), lambda i, j: (0, j))]
                  + [ANY] * len(after)),
        out_specs=pl.BlockSpec((tm, d), lambda i, j: (i, 0)),
        out_shape=jax.ShapeDtypeStruct((s_len, d), F32),
        scratch_shapes=[pltpu.VMEM((tm, d), F32)],
        compiler_params=_params(2),
    )(dq, dk, dv, drest, w, *after)


def _wgrad(x, y, x_spec, y_spec, out_shape, out_spec, acc_shape, n_chunks, name, x_transposed=False, after=()):
    s_len = y.shape[-2]
    ts = WGRAD_TILE
    steps = s_len // ts

    def body(x_ref, y_ref, o_ref, acc_ref):
        s = pl.program_id(1)

        @pl.when(s == 0)
        def _():
            acc_ref[...] = jnp.zeros_like(acc_ref)

        acc_ref[...] += (_dot if x_transposed else _dot_tn)(x_ref[...], y_ref[...])

        @pl.when(s == steps - 1)
        def _():
            o_ref[...] = acc_ref[...].astype(o_ref.dtype)

    return pl.pallas_call(
        _ordered(body, 2, after), name=name, grid=(n_chunks, steps),
        in_specs=[x_spec(ts), y_spec(ts)] + [ANY] * len(after), out_specs=out_spec,
        out_shape=jax.ShapeDtypeStruct(out_shape, BF16),
        scratch_shapes=[pltpu.VMEM(acc_shape, F32)],
        compiler_params=_params(2),
    )(x, y, *after)


def _pieces(width, piece=256):
    return [slice(a, min(a + piece, width)) for a in range(0, width, piece)]


def _ffn_fwd(h, w_gate, w_up, w_down, name):
    s_len, d = h.shape
    n_chunks, _, fs = w_gate.shape
    tm = FFN_TILE

    def body(h_ref, wg_ref, wu_ref, wd_ref, g_ref, u_ref, y_ref):
        j = pl.program_id(1)
        hv = h_ref[...]
        pieces = _pieces(fs)
        first = lambda cols: (_dot(hv, wg_ref[:, cols]), _dot(hv, wu_ref[:, cols]))
        total = None
        ahead = first(pieces[0])
        for k, cols in enumerate(pieces):
            g, u = ahead
            if k + 1 < len(pieces):
                ahead = first(pieces[k + 1])
            g_ref[:, cols] = g.astype(BF16)
            u_ref[:, cols] = u.astype(BF16)
            act = (g * _sigmoid(g)) * u
            part = _dot(act.astype(BF16), wd_ref[cols, :])
            total = part if total is None else total + part

        @pl.when(j == 0)
        def _():
            y_ref[...] = total

        @pl.when(j > 0)
        def _():
            y_ref[...] += total

    tile = pl.BlockSpec((tm, d), lambda i, j: (i, 0))
    hid = pl.BlockSpec((None, tm, fs), lambda i, j: (j, i, 0))
    w_in_spec = pl.BlockSpec((None, d, fs), lambda i, j: (j, 0, 0))
    hid_shape = jax.ShapeDtypeStruct((n_chunks, s_len, fs), BF16)
    return pl.pallas_call(
        body, name=name, grid=(s_len // tm, n_chunks),
        in_specs=[tile, w_in_spec, w_in_spec, pl.BlockSpec((None, fs, d), lambda i, j: (j, 0, 0))],
        out_specs=[hid, hid, tile],
        out_shape=[hid_shape, hid_shape, jax.ShapeDtypeStruct((s_len, d), F32)],
        compiler_params=_params(2),
    )(h, w_gate, w_up, w_down)


def _ffn_bwd(dy, g_pre, u_pre, w_gate, w_up, w_down, name):
    s_len, d = dy.shape
    n_chunks, _, fs = w_gate.shape
    tm = FFN_TILE

    def body(dy_ref, g_ref, u_ref, wg_ref, wu_ref, wd_ref, dh_ref, dg_ref, du_ref, a_ref):
        j = pl.program_id(1)
        dyv = dy_ref[...]
        pieces = _pieces(fs)
        first = lambda cols: _dot_nt(dyv, wd_ref[cols, :])
        total = None
        ahead = first(pieces[0])
        for k, cols in enumerate(pieces):
            da = ahead
            if k + 1 < len(pieces):
                ahead = first(pieces[k + 1])
            g = g_ref[:, cols].astype(F32)
            u = u_ref[:, cols].astype(F32)
            sg = _sigmoid(g)
            silu = g * sg
            dg = (da * u * (sg * (1.0 + g * (1.0 - sg)))).astype(BF16)
            du = (da * silu).astype(BF16)
            dg_ref[:, cols] = dg
            du_ref[:, cols] = du
            a_ref[:, cols] = (silu * u).astype(BF16)
            part = _dot_nt(dg, wg_ref[:, cols]) + _dot_nt(du, wu_ref[:, cols])
            total = part if total is None else total + part

        @pl.when(j == 0)
        def _():
            dh_ref[...] = total

        @pl.when(j > 0)
        def _():
            dh_ref[...] += total

    tile = pl.BlockSpec((tm, d), lambda i, j: (i, 0))
    hid = pl.BlockSpec((None, tm, fs), lambda i, j: (j, i, 0))
    w_in_spec = pl.BlockSpec((None, d, fs), lambda i, j: (j, 0, 0))
    hid_shape = jax.ShapeDtypeStruct((n_chunks, s_len, fs), BF16)
    return pl.pallas_call(
        body, name=name, grid=(s_len // tm, n_chunks),
        in_specs=[tile, hid, hid, w_in_spec, w_in_spec, pl.BlockSpec((None, fs, d), lambda i, j: (j, 0, 0))],
        out_specs=[tile, hid, hid, hid],
        out_shape=[jax.ShapeDtypeStruct((s_len, d), F32), hid_shape, hid_shape, hid_shape],
        compiler_params=_params(2),
    )(dy, g_pre, u_pre, w_gate, w_up, w_down)


def _ffn_wgrads(ht, dg, du, act, dy, tag, after=()):
    n_chunks, s_len, fs = dg.shape
    d = ht.shape[0]
    tok = lambda ts: pl.BlockSpec((ts, d), lambda c, s: (s, 0))
    tok_t = lambda ts: pl.BlockSpec((d, ts), lambda c, s: (0, s))
    hid = lambda ts: pl.BlockSpec((None, ts, fs), lambda c, s: (c, s, 0))
    d_up = pl.BlockSpec((None, d, fs), lambda c, s: (c, 0, 0))
    d_down = pl.BlockSpec((None, fs, d), lambda c, s: (c, 0, 0))
    dwg = _wgrad(ht, dg, tok_t, hid, (n_chunks, d, fs), d_up, (d, fs), n_chunks, tag + "_dwg", True, after)
    dwu = _wgrad(ht, du, tok_t, hid, (n_chunks, d, fs), d_up, (d, fs), n_chunks, tag + "_dwu", True, after)
    dwd = _wgrad(act, dy, hid, tok, (n_chunks, fs, d), d_down, (fs, d), n_chunks, tag + "_dwd", False, after)
    return dwg, dwu, dwd


def _band_bias():
    qi = lax.broadcasted_iota(jnp.int32, (ATTN_BLOCK, 2 * ATTN_BLOCK), 0)
    kj = lax.broadcasted_iota(jnp.int32, (ATTN_BLOCK, 2 * ATTN_BLOCK), 1)
    band = (kj >= qi) & (kj <= qi + ATTN_BLOCK)
    return jnp.where(band, 0.0, NEG), jnp.where(band & (kj >= ATTN_BLOCK), 0.0, NEG)


def _qkv_specs(slab_of, sections):
    def spec(sect, back):
        return pl.BlockSpec((SLAB, HEAD_DIM),
                            lambda h, s, g: (jnp.maximum(slab_of(s) - back, 0), (sect * N_GROUPS + g) * HEADS + h))
    return [spec(sect, back) for sect, back in sections]


HAT_BLOCKS = [(0, 0), (1, 0), (2, 0), (1, 1), (2, 1)]


def _stage_keys(k_ref, v_ref, kp_ref, vp_ref, kbuf, vbuf, dil, n):
    run = SLAB // dil
    for r in range(dil):
        own, before = pl.ds(r * run, run), pl.ds(2 * r * run, run)
        kbuf[pl.ds((2 * r + 1) * run, run), :] = k_ref[own, :]
        vbuf[pl.ds((2 * r + 1) * run, run), :] = v_ref[own, :]

        @pl.when(n > 0)
        def _():
            kbuf[before, :] = kp_ref[own, :]
            vbuf[before, :] = vp_ref[own, :]

        @pl.when(n == 0)
        def _():
            kbuf[before, :] = jnp.zeros((run, HEAD_DIM), BF16)
            vbuf[before, :] = jnp.zeros((run, HEAD_DIM), BF16)


def _for_each_tile(dil, n, first_fn, rest_fn):
    run = SLAB // dil
    bias, first_bias = _band_bias()
    tiles = []
    for jj in range(run // ATTN_BLOCK):
        start = jj * ATTN_BLOCK
        tile_bias = jnp.where(n == 0, first_bias, bias) if jj == 0 else bias
        for r in range(dil):
            tiles.append((pl.ds(r * run + start, ATTN_BLOCK),
                          pl.ds((2 * r + 1) * run - ATTN_BLOCK + start, 2 * ATTN_BLOCK), tile_bias))
    ahead = first_fn(*tiles[0])
    for t, tile in enumerate(tiles):
        begun = ahead
        if t + 1 < len(tiles):
            ahead = first_fn(*tiles[t + 1])
        rest_fn(*tile, begun)


def _attn_fwd(hat, name):
    s_len = hat.shape[0]
    e = HEAD_DIM
    n_slabs = s_len // SLAB

    def body(q_ref, k_ref, v_ref, kp_ref, vp_ref, o_ref, lse_ref, kbuf, vbuf, m_s, l_s, acc_s, m_p, l_p, acc_p, tmp_s):
        n, grp = pl.program_id(1), pl.program_id(2)

        def run(gi, dil):
            _stage_keys(k_ref, v_ref, kp_ref, vp_ref, kbuf, vbuf, dil, n)

            def scores(q_rows, kv_rows, bias):
                return _dot_nt(q_ref[q_rows, :], kbuf[kv_rows, :])

            def rest(q_rows, kv_rows, bias, qk):
                s = qk * ATTN_SCALE + bias
                m = jnp.max(s, axis=-1, keepdims=True)
                p = jnp.exp(s - m)
                m_p[q_rows, :] = jnp.broadcast_to(m, (ATTN_BLOCK, e))
                l_p[q_rows, :] = jnp.broadcast_to(jnp.sum(p, axis=-1, keepdims=True), (ATTN_BLOCK, e))
                acc_p[q_rows, :] = _dot(p.astype(BF16), vbuf[kv_rows, :])

            _for_each_tile(dil, n, scores, rest)
            _to_token_order(m_s.at[gi], m_p, dil, tmp_s)
            _to_token_order(l_s.at[gi], l_p, dil, tmp_s)
            _to_token_order(acc_s.at[gi], acc_p, dil, tmp_s)

        for gi, dil in enumerate(DILATIONS):
            pl.when(grp == gi)(lambda gi=gi, dil=dil: run(gi, dil))

        @pl.when(grp == N_GROUPS - 1)
        def _():
            m_all = jnp.maximum(jnp.maximum(m_s[0], m_s[1]), m_s[2])
            den = jnp.zeros((SLAB, e), F32)
            num = jnp.zeros((SLAB, e), F32)
            for gi in range(N_GROUPS):
                w = jnp.exp(m_s[gi] - m_all)
                den += l_s[gi] * w
                num += acc_s[gi] * w
            o_ref[...] = num / den
            lse_ref[...] = m_all + jnp.log(den)

    out = pl.BlockSpec((SLAB, e), lambda h, n, g: (n, h))
    return pl.pallas_call(
        body, name=name, grid=(HEADS, n_slabs, N_GROUPS),
        in_specs=_qkv_specs(lambda n: n, HAT_BLOCKS),
        out_specs=[out, out],
        out_shape=[jax.ShapeDtypeStruct((s_len, HEADS * e), F32)] * 2,
        scratch_shapes=[pltpu.VMEM((2 * SLAB, e), BF16), pltpu.VMEM((2 * SLAB, e), BF16),
                        pltpu.VMEM((N_GROUPS, SLAB, e), F32), pltpu.VMEM((N_GROUPS, SLAB, e), F32),
                        pltpu.VMEM((N_GROUPS, SLAB, e), F32)]
        + [pltpu.VMEM((SLAB, e), F32)] * 4,
        compiler_params=_params(3),
    )(hat, hat, hat, hat, hat)


def _attn_bwd(qkv, hat, d_out, out, lse, q_norm, k_norm, name):
    s_len = qkv.shape[0]
    e = HEAD_DIM
    n_slabs = s_len // SLAB

    def body(q_ref, k_ref, v_ref, kp_ref, vp_ref, qraw_ref, kraw_ref, do_ref, o_ref, lse_ref, qn_ref, kn_ref,
             dq_ref, dk_ref, dv_ref, st_ref, kbuf, vbuf, stat_s, dqs, dkb, dvb, dk_tok, dv_tok, carry,
             do_p, stat_p, dq_p, dk_p, dv_p, tmp_s, do16_p):
        head, step, grp = pl.program_id(0), pl.program_id(1), pl.program_id(2)
        n = n_slabs - 1 - step
        dkb[...] = jnp.zeros_like(dkb)
        dvb[...] = jnp.zeros_like(dvb)
        @pl.when(grp == 0)
        def _():
            lane = lax.broadcasted_iota(jnp.int32, (SLAB, e), 1)
            stat_s[...] = jnp.where(lane < e // 2, lse_ref[...],
                                    jnp.sum(do_ref[...] * o_ref[...], axis=-1, keepdims=True))

        @pl.when((head == 0) & (step == 0) & (grp == 0))
        def _():
            st_ref[...] = jnp.zeros_like(st_ref)

        def run(gi, dil):
            seg = SLAB // dil
            _stage_keys(k_ref, v_ref, kp_ref, vp_ref, kbuf, vbuf, dil, n)

            @pl.when(step == 0)
            def _():
                carry[gi] = jnp.zeros((2, SLAB, e), F32)

            _to_residue_order(do_p, do_ref, dil, tmp_s)
            do16_p[...] = do_p[...].astype(BF16)
            _to_residue_order(stat_p, stat_s, dil, tmp_s)

            def scores(q_rows, kv_rows, bias):
                return _dot_nt(q_ref[q_rows, :], kbuf[kv_rows, :]), _dot_nt(do16_p[q_rows, :], vbuf[kv_rows, :])

            def rest(q_rows, kv_rows, bias, begun):
                qk, dp = begun
                q = q_ref[q_rows, :]
                k = kbuf[kv_rows, :]
                stat = stat_p[q_rows, :]
                p = jnp.exp(qk * ATTN_SCALE + bias - stat[:, 0:1])
                ds = (p * (dp - stat[:, e // 2:e // 2 + 1]) * ATTN_SCALE).astype(BF16)
                dq_p[q_rows, :] = _dot(ds, k)
                dkb[kv_rows, :] += _dot_tn(ds, q)
                dvb[kv_rows, :] += _dot_tn(p.astype(BF16), do16_p[q_rows, :])

            _for_each_tile(dil, n, scores, rest)
            for r in range(dil):
                own, before = pl.ds((2 * r + 1) * seg, seg), pl.ds(2 * r * seg, seg)
                kept = pl.ds(r * seg, seg)
                dk_p[kept, :] = dkb[own, :] + carry.at[gi, 0][kept, :]
                dv_p[kept, :] = dvb[own, :] + carry.at[gi, 1][kept, :]
                carry.at[gi, 0][kept, :] = dkb[before, :]
                carry.at[gi, 1][kept, :] = dvb[before, :]
            _to_token_order(dqs, dq_p, dil, tmp_s)
            _to_token_order(dk_tok, dk_p, dil, tmp_s)
            _to_token_order(dv_tok, dv_p, dil, tmp_s)

            def norm_bwd(raw, gain, d_hat):
                r = _rms(raw)
                y = raw * r
                dy = d_hat * gain
                return r * (dy - y * jnp.mean(dy * y, axis=-1, keepdims=True)), jnp.sum(d_hat * y, axis=0, keepdims=True)

            dq, dqn = norm_bwd(qraw_ref[...], qn_ref[...], dqs[...])
            dk, dkn = norm_bwd(kraw_ref[...], kn_ref[...], dk_tok[...])
            dq_ref[...] = dq.astype(BF16)
            dk_ref[...] = dk.astype(BF16)
            dv_ref[...] = dv_tok[...].astype(BF16)
            st_ref[0:1, :] += dqn
            st_ref[1:2, :] += dkn

        for gi, dil in enumerate(DILATIONS):
            pl.when(grp == gi)(lambda gi=gi, dil=dil: run(gi, dil))

    slab_of = lambda s: n_slabs - 1 - s
    small = pl.BlockSpec((1, e), lambda h, s, g: (0, 0))
    head_blk = pl.BlockSpec((SLAB, e), lambda h, s, g: (slab_of(s), h))
    grad_blk = pl.BlockSpec((SLAB, e), lambda h, s, g: (slab_of(s), g * HEADS + h))
    grad_shape = jax.ShapeDtypeStruct((s_len, QKV), BF16)
    return pl.pallas_call(
        body, name=name, grid=(HEADS, n_slabs, N_GROUPS),
        in_specs=(_qkv_specs(slab_of, HAT_BLOCKS) + _qkv_specs(slab_of, [(0, 0), (1, 0)])
                  + [head_blk, head_blk, head_blk, small, small]),
        out_specs=[grad_blk, grad_blk, grad_blk, pl.BlockSpec((8, e), lambda h, s, g: (0, 0))],
        out_shape=[grad_shape, grad_shape, grad_shape, jax.ShapeDtypeStruct((8, e), F32)],
        scratch_shapes=[pltpu.VMEM((2 * SLAB, e), BF16), pltpu.VMEM((2 * SLAB, e), BF16), pltpu.VMEM((SLAB, e), F32),
                        pltpu.VMEM((SLAB, e), F32), pltpu.VMEM((2 * SLAB, e), F32), pltpu.VMEM((2 * SLAB, e), F32),
                        pltpu.VMEM((SLAB, e), F32), pltpu.VMEM((SLAB, e), F32),
                        pltpu.VMEM((N_GROUPS, 2, SLAB, e), F32)]
        + [pltpu.VMEM((SLAB, e), F32)] * 6 + [pltpu.VMEM((SLAB, e), BF16)],
        compiler_params=_params(3),
    )(hat, hat, hat, hat, hat, qkv, qkv, d_out, out, lse, q_norm, k_norm)


def _shift_rows(x, by, edge, forward):
    t_len = x.shape[0]
    row = lax.broadcasted_iota(jnp.int32, x.shape, 0)
    if forward:
        out = pltpu.roll(x, by, 0)
        for i in range(by):
            out = jnp.where(row == i, edge[8 - by + i:8 - by + i + 1, :], out)
    else:
        out = pltpu.roll(x, t_len - by, 0)
        for i in range(by):
            out = jnp.where(row == t_len - by + i, edge[i:i + 1, :], out)
    return out


def _mix_fwd(x, o, rest, mod, mod_next, conv_w, w_attn, w_conv, w_out, name):
    s_len, d = x.shape
    tm = MIX_TILE
    a_w = o.shape[1]

    def body(x_ref, o_ref, u_ref, b_ref, c_ref, ga_ref, gc_ref, mod_ref, modn_ref, cw_ref, wa_ref, wc_ref, wo_ref,
             xo_ref, z_ref, ya_ref, yc_ref, conv_ref, yb_ref, m_ref, o16_ref, h_ref, ht_ref, carry):
        @pl.when(pl.program_id(0) == 0)
        def _():
            carry[...] = jnp.zeros_like(carry)

        xc = c_ref[...].astype(F32) * u_ref[...].astype(F32)
        edge = carry[...]
        conv = (_shift_rows(xc, 2, edge, True) * cw_ref[0:1, :] + _shift_rows(xc, 1, edge, True) * cw_ref[1:2, :]
                + xc * cw_ref[2:3, :])
        carry[...] = xc[tm - 8:tm, :]
        yb = (b_ref[...].astype(F32) * conv).astype(BF16)
        o16 = o_ref[...].astype(BF16)
        ya = _dot(o16, wa_ref[...])
        yc = _dot(yb, wc_ref[...])
        merged = (_sigmoid(ga_ref[...].astype(F32)) * ya + _sigmoid(gc_ref[...].astype(F32)) * yc).astype(BF16)
        z = _dot(merged, wo_ref[...])
        xo = x_ref[...] + mod_ref[2:3, :] * z
        xo_ref[...] = xo
        hn = ((xo * _rms(xo)) * modn_ref[3:4, :]) * (1.0 + modn_ref[1:2, :]) + modn_ref[0:1, :]
        h_ref[...] = hn.astype(BF16)
        ht_ref[...] = hn.T.astype(BF16)
        z_ref[...] = z.astype(BF16)
        ya_ref[...] = ya.astype(BF16)
        yc_ref[...] = yc.astype(BF16)
        conv_ref[...] = conv.astype(BF16)
        yb_ref[...] = yb
        m_ref[...] = merged
        o16_ref[...] = o16

    tile = pl.BlockSpec((tm, d), lambda i: (i, 0))
    sect = lambda k: pl.BlockSpec((tm, d), lambda i: (i, k))
    att = pl.BlockSpec((tm, a_w), lambda i: (i, 0))
    const = lambda shape: pl.BlockSpec(shape, lambda i: (0, 0))
    f32_out = jax.ShapeDtypeStruct((s_len, d), F32)
    b16_out = jax.ShapeDtypeStruct((s_len, d), BF16)
    return pl.pallas_call(
        body, name=name, grid=(s_len // tm,),
        in_specs=[tile, att, sect(0), sect(1), sect(2), sect(3), sect(4), const((8, d)), const((8, d)), const((8, d)),
                  const((a_w, d)), const((d, d)), const((d, d))],
        out_specs=[tile] * 7 + [att, tile, pl.BlockSpec((d, tm), lambda i: (0, i))],
        out_shape=[f32_out] + [b16_out] * 6 + [jax.ShapeDtypeStruct((s_len, a_w), BF16), b16_out,
                                               jax.ShapeDtypeStruct((d, s_len), BF16)],
        scratch_shapes=[pltpu.VMEM((8, d), F32)],
        compiler_params=_params(1),
    )(x, o, rest, rest, rest, rest, rest, mod, mod_next, conv_w, w_attn, w_conv, w_out)


def _mix_bwd(dxo, ya, yc, conv, rest, mod, conv_w, w_attn, w_conv, w_out, a_w, name, after=()):
    s_len, d = dxo.shape
    tm = MIX_TILE
    n_tiles = s_len // tm

    def body(dxo_ref, ya_ref, yc_ref, conv_ref, u_ref, b_ref, c_ref, ga_ref, gc_ref, mod_ref, cw_ref,
             wa_ref, wc_ref, wo_ref, do_ref, drest_ref, dz_ref, dya_ref, dyc_ref, st_ref, carry):
        @pl.when(pl.program_id(0) == 0)
        def _():
            carry[...] = jnp.zeros_like(carry)
            st_ref[...] = jnp.zeros_like(st_ref)

        dz = (mod_ref[2:3, :] * dxo_ref[...]).astype(BF16)
        dz_ref[...] = dz
        dm = _dot_nt(dz, wo_ref[...])
        sa, sc = _sigmoid(ga_ref[...].astype(F32)), _sigmoid(gc_ref[...].astype(F32))
        dya = (dm * sa).astype(BF16)
        dyc = (dm * sc).astype(BF16)
        dya_ref[...] = dya
        dyc_ref[...] = dyc
        drest_ref[:, 3 * d:4 * d] = (dm * ya_ref[...].astype(F32) * (sa * (1.0 - sa))).astype(BF16)
        drest_ref[:, 4 * d:5 * d] = (dm * yc_ref[...].astype(F32) * (sc * (1.0 - sc))).astype(BF16)
        do_ref[...] = _dot_nt(dya, wa_ref[...])
        dyb = _dot_nt(dyc, wc_ref[...])
        drest_ref[:, d:2 * d] = (dyb * conv_ref[...].astype(F32)).astype(BF16)
        dconv = dyb * b_ref[...].astype(F32)
        edge = carry[...]
        sh1 = _shift_rows(dconv, 1, edge, False)
        sh2 = _shift_rows(dconv, 2, edge, False)
        carry[...] = dconv[0:8, :]
        dxc = dconv * cw_ref[2:3, :] + sh1 * cw_ref[1:2, :] + sh2 * cw_ref[0:1, :]
        u, c = u_ref[...].astype(F32), c_ref[...].astype(F32)
        xc = c * u
        drest_ref[:, 0:d] = (dxc * c).astype(BF16)
        drest_ref[:, 2 * d:3 * d] = (dxc * u).astype(BF16)
        st_ref[0:1, :] += jnp.sum(xc * sh2, axis=0, keepdims=True)
        st_ref[1:2, :] += jnp.sum(xc * sh1, axis=0, keepdims=True)
        st_ref[2:3, :] += jnp.sum(xc * dconv, axis=0, keepdims=True)

    rev = lambda i: n_tiles - 1 - i
    tile = pl.BlockSpec((tm, d), lambda i: (rev(i), 0))
    sect = lambda k: pl.BlockSpec((tm, d), lambda i: (rev(i), k))
    const = lambda shape: pl.BlockSpec(shape, lambda i: (0, 0))
    b16_out = jax.ShapeDtypeStruct((s_len, d), BF16)
    return pl.pallas_call(
        _ordered(body, 14, after), name=name, grid=(n_tiles,),
        in_specs=[tile, tile, tile, tile, sect(0), sect(1), sect(2), sect(3), sect(4), const((8, d)), const((8, d)),
                  const((a_w, d)), const((d, d)), const((d, d))] + [ANY] * len(after),
        out_specs=[pl.BlockSpec((tm, a_w), lambda i: (rev(i), 0)), pl.BlockSpec((tm, 5 * d), lambda i: (rev(i), 0)),
                   tile, tile, tile, const((8, d))],
        out_shape=[jax.ShapeDtypeStruct((s_len, a_w), F32), jax.ShapeDtypeStruct((s_len, 5 * d), BF16),
                   b16_out, b16_out, b16_out, jax.ShapeDtypeStruct((8, d), F32)],
        scratch_shapes=[pltpu.VMEM((8, d), F32)],
        compiler_params=_params(1),
    )(dxo, ya, yc, conv, rest, rest, rest, rest, rest, mod, conv_w, w_attn, w_conv, w_out, *after)


ADA_COLS = 128


def _ada_fwd(c_all, w_shard, b_shard, name):
    d, cols = w_shard.shape

    def body(c_ref, w_ref, b_ref, o_ref):
        cv = c_ref[...]
        o_ref[...] = jnp.dot(cv * _sigmoid(cv), w_ref[...], preferred_element_type=F32,
                             precision=lax.Precision.HIGHEST) + b_ref[...]

    return pl.pallas_call(
        body, name=name, grid=(cols // ADA_COLS,),
        in_specs=[pl.BlockSpec((8, d), lambda j: (0, 0)), pl.BlockSpec((d, ADA_COLS), lambda j: (0, j)),
                  pl.BlockSpec((1, ADA_COLS), lambda j: (0, j))],
        out_specs=pl.BlockSpec((8, ADA_COLS), lambda j: (0, j)),
        out_shape=jax.ShapeDtypeStruct((8, cols), F32),
        compiler_params=_params(1),
    )(c_all, w_shard, b_shard)


def _ada_bwd(c_all, dmod_shard, w, m, v, name):
    d, cols = w.shape

    def body(c_ref, dm_ref, w_ref, m_ref, v_ref, g_ref, d_ref, nm_ref, nv_ref):
        cv = c_ref[...]
        g = lax.dot_general(cv * _sigmoid(cv), dm_ref[...], (((0,), (0,)), ((), ())),
                            preferred_element_type=F32, precision=lax.Precision.HIGHEST)
        g_ref[...] = g
        d_ref[...], nm_ref[...], nv_ref[...] = _adamw_math(w_ref[...], g, m_ref[...], v_ref[...])

    blk = pl.BlockSpec((d, ADA_COLS), lambda j: (0, j))
    shape = jax.ShapeDtypeStruct((d, cols), F32)
    return pl.pallas_call(
        body, name=name, grid=(cols // ADA_COLS,),
        in_specs=[pl.BlockSpec((8, d), lambda j: (0, 0)), pl.BlockSpec((8, ADA_COLS), lambda j: (0, j)), blk, blk, blk],
        out_specs=[blk] * 4, out_shape=[shape] * 4,
        compiler_params=_params(1),
    )(c_all, dmod_shard, w, m, v)


def _small_update(parts, w, m, v, name):
    n = w.shape[1]

    def body(p_ref, w_ref, m_ref, v_ref, g_ref, d_ref, nm_ref, nv_ref):
        g = p_ref[0:1, :]
        for i in range(1, 8):
            g = g + p_ref[i:i + 1, :]
        g_ref[...] = g
        d_ref[...], nm_ref[...], nv_ref[...] = _adamw_math(w_ref[...], g, m_ref[...], v_ref[...])

    shape = jax.ShapeDtypeStruct((1, n), F32)
    return pl.pallas_call(body, name=name, out_shape=[shape] * 4, compiler_params=_params())(parts, w, m, v)


def _cols_to_shards(w, n):
    r, nc = w.shape
    return w.reshape(r, n, nc // n).transpose(1, 0, 2)


def kernel(x, c, w_ada, b_ada, norm_ffn1, ffn1_w_gate, ffn1_w_up, ffn1_w_down, norm_mix, w_in, q_norm, k_norm, conv_w, w_attn_branch, w_conv_branch, w_out, norm_ffn2, ffn2_w_gate, ffn2_w_up, ffn2_w_down, loss_target, m_w_ada, m_b_ada, m_norm_ffn1, m_ffn1_w_gate, m_ffn1_w_up, m_ffn1_w_down, m_norm_mix, m_w_in, m_q_norm, m_k_norm, m_conv_w, m_w_attn_branch, m_w_conv_branch, m_w_out, m_norm_ffn2, m_ffn2_w_gate, m_ffn2_w_up, m_ffn2_w_down, v_w_ada, v_b_ada, v_norm_ffn1, v_ffn1_w_gate, v_ffn1_w_up, v_ffn1_w_down, v_norm_mix, v_w_in, v_q_norm, v_k_norm, v_conv_w, v_w_attn_branch, v_w_conv_branch, v_w_out, v_norm_ffn2, v_ffn2_w_gate, v_ffn2_w_up, v_ffn2_w_down):
    ix, iy, ic = _place()
    chip = 2 * ix + iy
    me = 4 * ix + 2 * iy + ic
    xs = x[0]
    target = loss_target[0]
    s_len, d = xs.shape
    ada_cols = w_ada.shape[2]
    conv_cols = conv_w.shape[2]

    conv_rows = jnp.zeros((8, conv_cols), F32).at[0:3].set(conv_w[0])
    small_in = jnp.concatenate([jnp.broadcast_to(c, (8, d)), conv_rows], axis=1)
    small_all = _allgather8(small_in, "gather_c").reshape(8, 8, d + conv_cols)
    c_all = small_all[:, 0, :d]
    conv_full = small_all[0::2, 0:3, d:].transpose(1, 0, 2).reshape(3, N_CHIPS * conv_cols)
    conv_pad = jnp.zeros((8, N_CHIPS * conv_cols), F32).at[0:3].set(conv_full)
    b_shard = lax.dynamic_slice(b_ada, (0, chip * ada_cols), (1, ada_cols))
    mod_part = _ada_fwd(c_all, w_ada[0], b_shard, "ada_fwd")
    mod_all = _allgather8(mod_part, "gather_mod").reshape(N_CHIPS, 2, 8, ada_cols)[:, 0]
    mod_mine = lax.dynamic_slice(mod_all, (0, me, 0), (N_CHIPS, 1, ada_cols)).reshape(9, d)

    def mod_rows(i, gain):
        return jnp.zeros((8, d), F32).at[0:3].set(mod_mine[3 * i:3 * i + 3]).at[3:4].set(gain)

    mod1, mod2, mod3 = mod_rows(0, norm_ffn1), mod_rows(1, norm_mix), mod_rows(2, norm_ffn2)

    to16 = lambda w: w[0].astype(BF16)
    wg1, wu1, wd1 = _gather_weights([to16(ffn1_w_gate), to16(ffn1_w_up), to16(ffn1_w_down)], [False] * 3,
                                    "gather_ffn1", 1)
    h1, h1t = _norm_mod(xs, mod1, "norm1")
    (w_in_full,) = _gather_weights([to16(w_in)], [True], "gather_w_in", 2, after=(wd1, h1))

    g1, u1, y1 = _ffn_fwd(h1, wg1, wu1, wd1, "ffn1_fwd")
    x1, h2, h2t = _norm_mod(xs, mod2, "norm2", prev=(y1, mod1, 0.5))
    qkv, rest, qkv_hat = _in_proj(h2, w_in_full, q_norm, k_norm, "in_proj")
    w_ab, w_cb_g, w_o_g, wg2, wu2, wd2 = _gather_weights(
        [to16(w_attn_branch), to16(w_conv_branch), to16(w_out),
         to16(ffn2_w_gate), to16(ffn2_w_up), to16(ffn2_w_down)], [True] + [False] * 5,
        "gather_rest", 3, after=(h2,))
    a_w = w_ab.shape[0]
    w_cb = w_cb_g.reshape(d, d)
    w_o = w_o_g.reshape(d, d)
    o, lse = _attn_fwd(qkv_hat, "attn_fwd")
    x2, z, ya, yc, conv, yb, merged, o16, h3, h3t = _mix_fwd(x1, o, rest, mod2, mod3, conv_pad, w_ab, w_cb, w_o,
                                                             "mix_fwd")
    g3, u3, y3 = _ffn_fwd(h3, wg2, wu2, wd2, "ffn2_fwd")
    dx3, dy3, loss_part = _loss_grad(x2, y3, mod3, target, "loss")

    c_idx = jnp.reshape(ic, (1,)).astype(jnp.int32)
    chip_idx = jnp.stack([chip, ic]).astype(jnp.int32)

    def pair_send(grads, tag, collective_id):
        return _rs_pair_exchange(grads, "rs_pair_" + tag, collective_id)

    def chip_send(grads, from_sibling, names, tag, collective_id, after):
        pair_sums = [_pair_add(g, r, c_idx, "pair_add_" + nm, after) for g, r, nm in zip(grads, from_sibling, names)]
        return pair_sums, _rs_chip_exchange(pair_sums, "rs_chips_" + tag, collective_id)

    def reduce_finish(pair_sums, from_chips, names, tag, after):
        totals = [_chip_add(p, r, chip_idx, "chip_add_" + nm, after)
                  for p, r, nm in zip(pair_sums, from_chips, names)]
        return dict(zip(names, _rs_share(totals, "rs_share_" + tag)))

    names_a = ["ffn2_w_gate", "ffn2_w_up", "ffn2_w_down"]
    names_b = ["w_in", "w_attn_branch", "w_conv_branch", "w_out"]
    names_c = ["ffn1_w_gate", "ffn1_w_up", "ffn1_w_down"]

    dh3, dg3, du3, a3 = _ffn_bwd(dy3, g3, u3, wg2, wu2, wd2, "ffn2_bwd")
    grads_a = list(_ffn_wgrads(h3t, dg3, du3, a3, dy3, "ffn2"))
    sibling_a = pair_send(grads_a, "a", 7)
    dx2, st3 = _norm_bwd(dh3, x2, mod3, dx3, y3, 0.5, "norm3_bwd")
    sums_a, chips_a = chip_send(grads_a, sibling_a, names_a, "a", 4, after=(dx2,))

    do, drest, dz, dya, dyc, st_conv = _mix_bwd(dx2, ya, yc, conv, rest, mod2, conv_pad, w_ab, w_cb, w_o, a_w,
                                                "mix_bwd", after=tuple(sums_a))
    dq, dk, dv, st_qk = _attn_bwd(qkv, qkv_hat, do, o, lse, q_norm, k_norm, "attn_bwd")
    tok = lambda width: (lambda ts: pl.BlockSpec((ts, width), lambda cc, s: (s, 0)))
    colblk = lambda width: (lambda ts: pl.BlockSpec((ts, width), lambda cc, s: (s, cc)))
    tok_t = lambda ts: pl.BlockSpec((d, ts), lambda cc, s: (0, s))
    whole = pl.BlockSpec((d, QKV), lambda cc, s: (0, 0))
    dw_in = [_wgrad(h2t, part, tok_t, tok(QKV), (d, QKV), whole, (d, QKV), 1, "dw_in_" + nm, True)
             for part, nm in ((dq, "q"), (dk, "k"), (dv, "v"))]
    dw_in.append(_wgrad(h2t, drest, tok_t, colblk(d), (d, 5 * d), pl.BlockSpec((d, d), lambda cc, s: (0, cc)),
                        (d, d), 5, "dw_in_rest", True))
    dw_in = _cols_to_shards(jnp.concatenate(dw_in, axis=1), N_CHIPS)
    shard_w = d // N_CHIPS
    dw_ab = _wgrad(o16, dya, tok(a_w), colblk(shard_w), (a_w, d), pl.BlockSpec((a_w, shard_w), lambda cc, s: (0, cc)),
                   (a_w, shard_w), N_CHIPS, "dw_attn_branch")
    dw_ab = _cols_to_shards(dw_ab, N_CHIPS)
    row_out = pl.BlockSpec((None, shard_w, d), lambda cc, s: (cc, 0, 0))
    dw_cb = _wgrad(yb, dyc, colblk(shard_w), tok(d), (N_CHIPS, shard_w, d), row_out, (shard_w, d), N_CHIPS, "dw_conv_branch")
    dw_o = _wgrad(merged, dz, colblk(shard_w), tok(d), (N_CHIPS, shard_w, d), row_out, (shard_w, d), N_CHIPS, "dw_out")
    shard_grads = reduce_finish(sums_a, chips_a, names_a, "a", after=(dw_in, dw_o))
    grads_b = [dw_in, dw_ab, dw_cb, dw_o]
    sibling_b = pair_send(grads_b, "b", 8)

    dh2 = _in_proj_bwd(dq, dk, dv, drest, w_in_full, "in_proj_bwd")
    sums_b, chips_b = chip_send(grads_b, sibling_b, names_b, "b", 5, after=(dh2,))
    dx1, st2, dy1 = _norm_bwd(dh2, x1, mod2, dx2, z, 1.0, "norm2_bwd", after=tuple(sums_b), prev=(mod1, 0.5))
    dh1, dg1, du1, a1 = _ffn_bwd(dy1, g1, u1, wg1, wu1, wd1, "ffn1_bwd")
    dx0, st1 = _norm_bwd(dh1, xs, mod1, dx1, y1, 0.5, "norm1_bwd")
    grads_c = list(_ffn_wgrads(h1t, dg1, du1, a1, dy1, "ffn1"))
    sibling_c = pair_send(grads_c, "c", 9)
    shard_grads.update(reduce_finish(sums_b, chips_b, names_b, "b", after=tuple(grads_c)))

    dmod = jnp.concatenate([st1[0:3], st2[0:3], st3[0:3]], axis=0).reshape(1, 9 * d)
    loss_cols = jnp.zeros((1, HEAD_DIM), F32).at[0, 0].set(jnp.sum(loss_part))
    small = jnp.concatenate([dmod, st1[3:4], st2[3:4], st3[3:4], st_qk[0:1], st_qk[1:2],
                             st_conv[0:3].reshape(1, 3 * d), loss_cols], axis=1)
    small_all = _allgather8(jnp.broadcast_to(small, (8, small.shape[1])), "gather_small").reshape(8, 8, -1)[:, 0]
    loss = (0.5 / d) * jnp.sum(small_all[:, -HEAD_DIM])
    small_all = small_all[:, :-HEAD_DIM]
    dmod_all = small_all[:, :9 * d]
    dmod_shard = lax.dynamic_slice(dmod_all, (0, chip * ada_cols), (8, ada_cols))
    g_w_ada, d_w_ada, nm_w_ada, nv_w_ada = _ada_bwd(c_all, dmod_shard, w_ada[0], m_w_ada[0], v_w_ada[0], "ada_bwd")

    vec_names = ["b_ada", "norm_ffn1", "norm_mix", "norm_ffn2", "q_norm", "k_norm"]
    vec_w = [b_ada, norm_ffn1, norm_mix, norm_ffn2, q_norm, k_norm]
    vec_m = [m_b_ada, m_norm_ffn1, m_norm_mix, m_norm_ffn2, m_q_norm, m_k_norm]
    vec_v = [v_b_ada, v_norm_ffn1, v_norm_mix, v_norm_ffn2, v_q_norm, v_k_norm]
    n_vec = sum(w.shape[1] for w in vec_w)
    cat = lambda arrs: jnp.concatenate(arrs, axis=1)
    vec_out = _small_update(small_all[:, :n_vec], cat(vec_w), cat(vec_m), cat(vec_v), "small_update")
    conv_parts = small_all[:, n_vec:].reshape(8, 3, N_CHIPS * conv_cols)
    conv_parts = lax.dynamic_slice(conv_parts, (0, 0, chip * conv_cols), (8, 3, conv_cols)).reshape(8, 3 * conv_cols)
    flat3 = lambda w: w[0].reshape(1, 3 * conv_cols)
    conv_out = _small_update(conv_parts, flat3(conv_w), flat3(m_conv_w), flat3(v_conv_w), "conv_update")

    res = {"w_ada": [t[None] for t in (g_w_ada, d_w_ada, nm_w_ada, nv_w_ada)],
           "conv_w": [t.reshape(1, 3, conv_cols) for t in conv_out]}
    off = 0
    for nm, w in zip(vec_names, vec_w):
        width = w.shape[1]
        res[nm] = [t[:, off:off + width] for t in vec_out]
        off += width
    big = {"ffn1_w_gate": (ffn1_w_gate, m_ffn1_w_gate, v_ffn1_w_gate), "ffn1_w_up": (ffn1_w_up, m_ffn1_w_up, v_ffn1_w_up),
           "ffn1_w_down": (ffn1_w_down, m_ffn1_w_down, v_ffn1_w_down), "w_in": (w_in, m_w_in, v_w_in),
           "w_attn_branch": (w_attn_branch, m_w_attn_branch, v_w_attn_branch),
           "w_conv_branch": (w_conv_branch, m_w_conv_branch, v_w_conv_branch), "w_out": (w_out, m_w_out, v_w_out),
           "ffn2_w_gate": (ffn2_w_gate, m_ffn2_w_gate, v_ffn2_w_gate), "ffn2_w_up": (ffn2_w_up, m_ffn2_w_up, v_ffn2_w_up),
           "ffn2_w_down": (ffn2_w_down, m_ffn2_w_down, v_ffn2_w_down)}
    def update(nm, after=()):
        w, m, v = big[nm]
        g, delta, new_m, new_v = _adamw(w[0], shard_grads[nm], m[0], v[0], "adamw_" + nm, after)
        res[nm] = [t[None] for t in (g, delta, new_m, new_v)]
        return new_v

    last = tuple(shard_grads[nm] for nm in names_b)
    for nm in names_a:
        last = (update(nm, last),)
    sums_c, chips_c = chip_send(grads_c, sibling_c, names_c, "c", 6, after=last)
    last = tuple(sums_c)
    for nm in names_b:
        last = (update(nm, last),)
    shard_grads.update(reduce_finish(sums_c, chips_c, names_c, "c", after=last))
    for nm in names_c:
        update(nm)

    order = ["w_ada", "b_ada", "norm_ffn1", "ffn1_w_gate", "ffn1_w_up", "ffn1_w_down", "norm_mix", "w_in", "q_norm",
             "k_norm", "conv_w", "w_attn_branch", "w_conv_branch", "w_out", "norm_ffn2", "ffn2_w_gate", "ffn2_w_up",
             "ffn2_w_down"]
    return (loss, dx0[None], *[res[nm][0] for nm in order], *[res[nm][1] for nm in order],
            *[res[nm][2] for nm in order], *[res[nm][3] for nm in order])
```

```python
import jax
import jax.numpy as jnp
from jax import lax
from jax.experimental import pallas as pl
from jax.experimental.pallas import tpu as pltpu
from jax.experimental.pallas import tpu_sc as plsc

F32 = jnp.float32
BF16 = jnp.bfloat16
MESH = pl.DeviceIdType.MESH
ANY = pl.BlockSpec(memory_space=pl.ANY)

NORM_EPS = 1e-6
HEAD_DIM = 128
N_GROUPS = 3
HEADS = 4
DILATIONS = (1, 4, 16)
ATTN_BLOCK = 128
SLAB = ATTN_BLOCK * max(DILATIONS)
QKV = N_GROUPS * HEADS * HEAD_DIM
ATTN_SCALE = HEAD_DIM ** -0.5
NEG = -1e30
N_CHIPS = 4

ADAM_LR = 0.001
ADAM_B1 = 0.9
ADAM_B2 = 0.999
ADAM_EPS = 1e-08
ADAM_WD = 0.01
ADAM_STEP = 10

VMEM_LIMIT_BYTES = 56 * 1024 * 1024
TOKEN_TILE = 512
FFN_TILE = 1024
PROJ_TILE = 2048
WGRAD_TILE = 2048
IN_BLOCK = 512
MIX_TILE = 256
ADAMW_TILE_BYTES = 3 * 512 * 1024


def _params(n_axes=0):
    return pltpu.CompilerParams(
        dimension_semantics=("arbitrary",) * n_axes if n_axes else None,
        vmem_limit_bytes=VMEM_LIMIT_BYTES)


def _dot(a, b):
    return jnp.dot(a, b, preferred_element_type=F32)


def _dot_nt(a, b):
    return lax.dot_general(a, b, (((1,), (1,)), ((), ())), preferred_element_type=F32)


def _dot_tn(a, b):
    return lax.dot_general(a, b, (((0,), (0,)), ((), ())), preferred_element_type=F32)


def _sigmoid(x):
    return 1.0 / (1.0 + jnp.exp(-x))


def _place():
    return lax.axis_index("x"), lax.axis_index("y"), lax.axis_index("c")


def _ordered(body, n_in, after):
    if not after:
        return body
    return lambda *refs: body(*refs[:n_in], *refs[n_in + len(after):])


def _allgather8(block, name):
    m_per, n = block.shape

    def body(x_ref, out_ref, send_sems, recv_sems, local_sem):
        x, y, c = _place()
        me, sibling = (x, y, c), (x, y, 1 - c)
        chips = [(1 - x, y), (x, 1 - y), (1 - x, 1 - y)]

        def rows(px, py, pc):
            return out_ref.at[pl.ds((4 * px + 2 * py + pc) * m_per, m_per), :]

        def copy(k, blk, to, src=None):
            return pltpu.make_async_remote_copy(
                src_ref=rows(*blk) if src is None else src, dst_ref=rows(*blk),
                send_sem=send_sems.at[k], recv_sem=recv_sems.at[k],
                device_id=to, device_id_type=MESH)

        mine = pltpu.make_async_copy(x_ref, rows(*me), local_sem)
        mine.start()
        first = [copy(0, me, sibling, src=x_ref)]
        first += [copy(1 + j, me, (*chip, c), src=x_ref) for j, chip in enumerate(chips)]
        for cp in first:
            cp.start()
        passed = [copy(4 + j, (*chip, c), sibling) for j, chip in enumerate(chips)]
        for j, chip in enumerate(chips):
            copy(1 + j, (*chip, c), me).wait_recv()
            passed[j].start()
        copy(0, sibling, me).wait_recv()
        for j, chip in enumerate(chips):
            copy(4 + j, (*chip, 1 - c), me).wait_recv()
        for cp in first + passed:
            cp.wait_send()
        mine.wait()

    return pl.pallas_call(
        body, name=name,
        out_shape=jax.ShapeDtypeStruct((8 * m_per, n), block.dtype),
        in_specs=[pl.BlockSpec(memory_space=pltpu.VMEM)],
        out_specs=pl.BlockSpec(memory_space=pltpu.VMEM),
        scratch_shapes=[pltpu.SemaphoreType.DMA((7,)), pltpu.SemaphoreType.DMA((7,)),
                        pltpu.SemaphoreType.DMA],
        compiler_params=_params(),
    )(block)


def _handshake(peers):
    barrier = pltpu.get_barrier_semaphore()
    for peer in peers:
        pl.semaphore_signal(barrier, inc=1, device_id=peer, device_id_type=MESH)
    pl.semaphore_wait(barrier, len(peers))


def _gather_weights(shards, by_cols, name, collective_id, after=()):
    n_arr = len(shards)

    def body(*refs):
        srcs, outs = refs[:n_arr], refs[n_arr + len(after):2 * n_arr + len(after)]
        send_sems, recv_sems, local_sems = refs[2 * n_arr + len(after):]
        x, y, c = _place()
        me_dev, sibling = (x, y, c), (x, y, 1 - c)
        chips = [(1 - x, y), (x, 1 - y), (1 - x, 1 - y)]
        me = 2 * x + y
        _handshake([sibling] + [(*chip, c) for chip in chips])

        def place(k, chip_idx, rows):
            if by_cols[k]:
                width = srcs[k].shape[1]
                return outs[k].at[rows, pl.ds(pl.multiple_of(chip_idx * width, 128), width)]
            return outs[k].at[chip_idx, rows]

        def copy(k, slot, chip_idx, half_sel, to, from_shard=False):
            half = srcs[k].shape[0] // 2
            rows = pl.ds(half_sel * half, half)
            dst = place(k, chip_idx, rows)
            return pltpu.make_async_remote_copy(
                src_ref=srcs[k].at[rows] if from_shard else dst, dst_ref=dst,
                send_sem=send_sems.at[6 * k + slot], recv_sem=recv_sems.at[6 * k + slot],
                device_id=to, device_id_type=MESH)

        own = [pltpu.make_async_copy(srcs[k], place(k, me, pl.ds(0, srcs[k].shape[0])), local_sems.at[k])
               for k in range(n_arr)]
        for cp in own:
            cp.start()
        sent = []
        for k in range(n_arr):
            for j, chip in enumerate(chips):
                sent.append(copy(k, j, me, c, (*chip, c), from_shard=True))
                sent[-1].start()
        for k in range(n_arr):
            for j, chip in enumerate(chips):
                chip_idx = 2 * chip[0] + chip[1]
                copy(k, j, chip_idx, c, me_dev).wait_recv()
                sent.append(copy(k, 3 + j, chip_idx, c, sibling))
                sent[-1].start()
        for k in range(n_arr):
            for j, chip in enumerate(chips):
                copy(k, 3 + j, 2 * chip[0] + chip[1], 1 - c, me_dev).wait_recv()
        for cp in sent:
            cp.wait_send()
        for cp in own:
            cp.wait()

    def gathered(k):
        r, cols = shards[k].shape
        return (r, N_CHIPS * cols) if by_cols[k] else (N_CHIPS, r, cols)

    return pl.kernel(
        body, name=name,
        out_type=[jax.ShapeDtypeStruct(gathered(k), shards[k].dtype) for k in range(n_arr)],
        mesh=plsc.ScalarSubcoreMesh(axis_name="sequencer", num_cores=1),
        scratch_types=[pltpu.SemaphoreType.DMA((6 * n_arr,)), pltpu.SemaphoreType.DMA((6 * n_arr,)),
                       pltpu.SemaphoreType.DMA((n_arr,))],
        compiler_params=pltpu.CompilerParams(collective_id=collective_id),
    )(*shards, *after)


def _rs_pair_exchange(grads, name, collective_id):
    n_arr = len(grads)

    def body(*refs):
        srcs, outs = refs[:n_arr], refs[n_arr:2 * n_arr]
        send_sems, recv_sems = refs[2 * n_arr:]
        x, y, c = _place()
        _handshake([(x, y, 1 - c)])
        cps = []
        for k in range(n_arr):
            half = srcs[k].shape[1] // 2
            cps.append(pltpu.make_async_remote_copy(
                src_ref=srcs[k].at[:, pl.ds((1 - c) * half, half)], dst_ref=outs[k],
                send_sem=send_sems.at[k], recv_sem=recv_sems.at[k],
                device_id=(x, y, 1 - c), device_id_type=MESH))
            cps[-1].start()
        for cp in cps:
            cp.wait_recv()
        for cp in cps:
            cp.wait_send()

    return pl.kernel(
        body, name=name,
        out_type=[jax.ShapeDtypeStruct((g.shape[0], g.shape[1] // 2, g.shape[2]), g.dtype) for g in grads],
        mesh=plsc.ScalarSubcoreMesh(axis_name="sequencer", num_cores=1),
        scratch_types=[pltpu.SemaphoreType.DMA((n_arr,)), pltpu.SemaphoreType.DMA((n_arr,))],
        compiler_params=pltpu.CompilerParams(collective_id=collective_id),
    )(*grads)


def _rs_chip_exchange(sums, name, collective_id):
    n_arr = len(sums)

    def body(*refs):
        srcs, outs = refs[:n_arr], refs[n_arr:2 * n_arr]
        send_sems, recv_sems = refs[2 * n_arr:]
        x, y, c = _place()
        chips = [(1 - x, y), (x, 1 - y), (1 - x, 1 - y)]
        _handshake([(*chip, c) for chip in chips])
        cps = []
        for k in range(n_arr):
            for j, chip in enumerate(chips):
                cps.append(pltpu.make_async_remote_copy(
                    src_ref=srcs[k].at[2 * chip[0] + chip[1]], dst_ref=outs[k].at[j],
                    send_sem=send_sems.at[3 * k + j], recv_sem=recv_sems.at[3 * k + j],
                    device_id=(*chip, c), device_id_type=MESH))
                cps[-1].start()
        for cp in cps:
            cp.wait_recv()
        for cp in cps:
            cp.wait_send()

    return pl.kernel(
        body, name=name,
        out_type=[jax.ShapeDtypeStruct((3,) + s.shape[1:], s.dtype) for s in sums],
        mesh=plsc.ScalarSubcoreMesh(axis_name="sequencer", num_cores=1),
        scratch_types=[pltpu.SemaphoreType.DMA((3 * n_arr,)), pltpu.SemaphoreType.DMA((3 * n_arr,))],
        compiler_params=pltpu.CompilerParams(collective_id=collective_id),
    )(*sums)


def _rs_share(totals, name):
    n_arr = len(totals)

    def body(*refs):
        outs = refs[n_arr:2 * n_arr]
        send_sems, recv_sems = refs[2 * n_arr:]
        x, y, c = _place()

        def half_rows(k, sel):
            return outs[k].at[sel]

        cps = []
        for k in range(n_arr):
            cps.append(pltpu.make_async_remote_copy(
                src_ref=half_rows(k, c), dst_ref=half_rows(k, c), send_sem=send_sems.at[k], recv_sem=recv_sems.at[k],
                device_id=(x, y, 1 - c), device_id_type=MESH))
            cps[-1].start()
        for k in range(n_arr):
            pltpu.make_async_remote_copy(
                src_ref=half_rows(k, c), dst_ref=half_rows(k, 1 - c), send_sem=send_sems.at[k],
                recv_sem=recv_sems.at[k], device_id=(x, y, 1 - c), device_id_type=MESH).wait_recv()
        for cp in cps:
            cp.wait_send()

    shared = pl.pallas_call(
        body, name=name,
        out_shape=[jax.ShapeDtypeStruct(t.shape, t.dtype) for t in totals],
        in_specs=[ANY] * n_arr, out_specs=[ANY] * n_arr,
        input_output_aliases={k: k for k in range(n_arr)},
        scratch_shapes=[pltpu.SemaphoreType.DMA((n_arr,)), pltpu.SemaphoreType.DMA((n_arr,))],
        compiler_params=_params(),
    )(*totals)
    return [t.reshape(2 * t.shape[1], t.shape[2]) for t in shared]


def _pair_add(grad, recv, c_idx, name, after=()):
    n, r, cols = grad.shape
    half = r // 2
    rows = half // 2

    def body(_, g_ref, r_ref, o_ref):
        o_ref[...] = (g_ref[...].astype(F32) + r_ref[...].astype(F32)).astype(o_ref.dtype)

    return pl.pallas_call(
        _ordered(body, 3, after), name=name,
        grid_spec=pltpu.PrefetchScalarGridSpec(
            num_scalar_prefetch=1, grid=(n, 2),
            in_specs=[pl.BlockSpec((None, None, rows, cols), lambda s, i, ci: (s, ci[0], i, 0)),
                      pl.BlockSpec((None, rows, cols), lambda s, i, ci: (s, i, 0))] + [ANY] * len(after),
            out_specs=pl.BlockSpec((None, rows, cols), lambda s, i, ci: (s, i, 0))),
        out_shape=jax.ShapeDtypeStruct((n, half, cols), BF16),
        compiler_params=_params(2),
    )(c_idx, grad.reshape(n, 2, half, cols), recv, *after)


def _chip_add(sums, recv, chip_and_core, name, after=()):
    _, half, cols = sums.shape
    rows = half // 2

    def body(_, s_ref, r0_ref, r1_ref, r2_ref, o_ref):
        o_ref[...] = ((s_ref[...].astype(F32) + r0_ref[...].astype(F32))
                      + r1_ref[...].astype(F32)) + r2_ref[...].astype(F32)

    def recv_spec(j):
        return pl.BlockSpec((None, rows, cols), lambda i, ci: (j, i, 0))

    return pl.pallas_call(
        _ordered(body, 5, after), name=name,
        grid_spec=pltpu.PrefetchScalarGridSpec(
            num_scalar_prefetch=1, grid=(2,),
            in_specs=[pl.BlockSpec((None, rows, cols), lambda i, ci: (ci[0], i, 0)),
                      recv_spec(0), recv_spec(1), recv_spec(2)] + [ANY] * len(after),
            out_specs=pl.BlockSpec((None, rows, cols), lambda i, ci: (ci[1], i, 0))),
        out_shape=jax.ShapeDtypeStruct((2, half, cols), F32),
        compiler_params=_params(1),
    )(chip_and_core, sums, recv, recv, recv, *after)


def _rms(x):
    return lax.rsqrt(jnp.mean(x * x, axis=-1, keepdims=True) + NORM_EPS)


def _norm_mod(x, mod, name, prev=None):
    s_len, d = x.shape
    tm = TOKEN_TILE

    def body(*refs):
        if prev is None:
            x_ref, mod_ref, h_ref, ht_ref = refs
            xv = x_ref[...]
        else:
            x_ref, y_ref, modp_ref, mod_ref, xo_ref, h_ref, ht_ref = refs
            xv = x_ref[...] + prev[2] * modp_ref[2:3, :] * y_ref[...]
            xo_ref[...] = xv
        n = (xv * _rms(xv)) * mod_ref[3:4, :]
        h = n * (1.0 + mod_ref[1:2, :]) + mod_ref[0:1, :]
        h_ref[...] = h.astype(BF16)
        ht_ref[...] = h.T.astype(BF16)

    tile = pl.BlockSpec((tm, d), lambda i: (i, 0))
    small = pl.BlockSpec((8, d), lambda i: (0, 0))
    h_specs = [tile, pl.BlockSpec((d, tm), lambda i: (0, i))]
    h_shapes = [jax.ShapeDtypeStruct((s_len, d), BF16), jax.ShapeDtypeStruct((d, s_len), BF16)]
    if prev is None:
        return pl.pallas_call(
            body, name=name, grid=(s_len // tm,), in_specs=[tile, small], out_specs=h_specs, out_shape=h_shapes,
            compiler_params=_params(1))(x, mod)
    return pl.pallas_call(
        body, name=name, grid=(s_len // tm,), in_specs=[tile, tile, small, small],
        out_specs=[tile] + h_specs, out_shape=[jax.ShapeDtypeStruct((s_len, d), F32)] + h_shapes,
        compiler_params=_params(1))(x, prev[0], prev[1], mod)


def _norm_bwd(dh, x, mod, dxo, y_raw, coef, name, after=(), prev=None):
    s_len, d = x.shape
    tm = TOKEN_TILE

    def body(*refs):
        if prev is None:
            dh_ref, x_ref, mod_ref, dxo_ref, y_ref, dx_ref, st_ref = refs
        else:
            dh_ref, x_ref, mod_ref, dxo_ref, y_ref, modp_ref, dx_ref, st_ref, dyp_ref = refs

        @pl.when(pl.program_id(0) == 0)
        def _():
            st_ref[...] = jnp.zeros_like(st_ref)

        xv, dhv, dxov = x_ref[...], dh_ref[...], dxo_ref[...]
        r = _rms(xv)
        xh = xv * r
        gain, scale = mod_ref[3:4, :], mod_ref[1:2, :]
        dn = dhv * (1.0 + scale)
        dxh = dn * gain
        dx = dxov + r * (dxh - xh * jnp.mean(dxh * xh, axis=-1, keepdims=True))
        dx_ref[...] = dx
        if prev is not None:
            dyp_ref[...] = (prev[1] * modp_ref[2:3, :] * dx).astype(BF16)
        st_ref[0:1, :] += jnp.sum(dhv, axis=0, keepdims=True)
        st_ref[1:2, :] += jnp.sum(dhv * (xh * gain), axis=0, keepdims=True)
        st_ref[2:3, :] += coef * jnp.sum(y_ref[...].astype(F32) * dxov, axis=0, keepdims=True)
        st_ref[3:4, :] += jnp.sum(dn * xh, axis=0, keepdims=True)

    tile = pl.BlockSpec((tm, d), lambda i: (i, 0))
    small = pl.BlockSpec((8, d), lambda i: (0, 0))
    operands = [dh, x, mod, dxo, y_raw] + ([] if prev is None else [prev[0]])
    in_specs = [tile, tile, small, tile, tile] + ([] if prev is None else [small])
    out_specs = [tile, small] + ([] if prev is None else [tile])
    out_shape = [jax.ShapeDtypeStruct((s_len, d), F32), jax.ShapeDtypeStruct((8, d), F32)]
    if prev is not None:
        out_shape.append(jax.ShapeDtypeStruct((s_len, d), BF16))
    return pl.pallas_call(
        _ordered(body, len(operands), after), name=name, grid=(s_len // tm,),
        in_specs=in_specs + [ANY] * len(after), out_specs=out_specs, out_shape=out_shape,
        compiler_params=_params(1),
    )(*operands, *after)


def _loss_grad(x, y, mod, target, name):
    s_len, d = x.shape
    tm = TOKEN_TILE

    def body(x_ref, y_ref, mod_ref, t_ref, do_ref, dy_ref, part_ref):
        @pl.when(pl.program_id(0) == 0)
        def _():
            part_ref[...] = jnp.zeros_like(part_ref)

        half_gate = 0.5 * mod_ref[2:3, :]
        err = (x_ref[...] + half_gate * y_ref[...]) - t_ref[...]
        do = err * (1.0 / d)
        do_ref[...] = do
        dy_ref[...] = (half_gate * do).astype(BF16)
        sq = err * err
        part_ref[...] += jnp.sum(sq.reshape(tm // 8, 8, d), axis=0)

    tile = pl.BlockSpec((tm, d), lambda i: (i, 0))
    small = pl.BlockSpec((8, d), lambda i: (0, 0))
    return pl.pallas_call(
        body, name=name, grid=(s_len // tm,),
        in_specs=[tile, tile, small, tile],
        out_specs=[tile, tile, small],
        out_shape=[jax.ShapeDtypeStruct((s_len, d), F32), jax.ShapeDtypeStruct((s_len, d), BF16),
                   jax.ShapeDtypeStruct((8, d), F32)],
        compiler_params=_params(1),
    )(x, y, mod, target)


def _adamw_math(w, g, m, v):
    m = ADAM_B1 * m + (1.0 - ADAM_B1) * g
    v = ADAM_B2 * v + (1.0 - ADAM_B2) * (g * g)
    m_hat = m / (1.0 - ADAM_B1 ** ADAM_STEP)
    v_hat = v / (1.0 - ADAM_B2 ** ADAM_STEP)
    delta = -ADAM_LR * (m_hat / (jnp.sqrt(v_hat) + ADAM_EPS) + ADAM_WD * w)
    return delta, m, v


def _adamw(w, g, m, v, name, after=()):
    r, cols = w.shape
    tr = max([t for t in (r // k for k in (1, 2, 4, 8, 16)) if t % 8 == 0 and r % t == 0
              and t * cols * 4 <= ADAMW_TILE_BYTES] or [r])

    def body(w_ref, g_ref, m_ref, v_ref, go_ref, d_ref, nm_ref, nv_ref):
        gv = g_ref[...]
        go_ref[...] = gv
        d_ref[...], nm_ref[...], nv_ref[...] = _adamw_math(w_ref[...], gv, m_ref[...], v_ref[...])

    tile = pl.BlockSpec((tr, cols), lambda i: (i, 0))
    shape = jax.ShapeDtypeStruct((r, cols), F32)
    return pl.pallas_call(
        _ordered(body, 4, after), name=name, grid=(r // tr,),
        in_specs=[tile] * 4 + [ANY] * len(after), out_specs=[tile] * 4, out_shape=[shape] * 4,
        compiler_params=_params(1),
    )(w, g, m, v, *after)


def _in_parts(tm, n_qkv, n_rest):
    def part(lo, n_blk):
        return pl.BlockSpec((tm, IN_BLOCK), lambda i, j: (i, jnp.clip(j - lo, 0, n_blk - 1)))
    return [part(0, n_qkv), part(n_qkv, n_qkv), part(2 * n_qkv, n_qkv), part(3 * n_qkv, n_rest)]


def _pick_part(j, n_qkv, refs, fn):
    bounds = [0, n_qkv, 2 * n_qkv, 3 * n_qkv]
    for p, ref in enumerate(refs):
        inside = j >= bounds[p]
        if p + 1 < len(refs):
            inside = inside & (j < bounds[p + 1])
        pl.when(inside)(lambda ref=ref: fn(ref))


def _rows(base, count, stride):
    return pl.ds(base, count) if stride == 1 else pl.ds(base, count, stride=stride)


REORDER_STRIDE = 4


def _reorder_plan(dil, parts=1):
    inner = min(dil, REORDER_STRIDE)
    return inner, dil // inner, SLAB // parts // inner, SLAB // dil


def _to_residue_order(dst, src, dil, tmp, part=0, parts=1):
    inner, outer, big, seg = _reorder_plan(dil, parts)
    piece = seg // parts
    if outer == 1:
        for r in range(dil):
            dst[pl.ds(r * seg + part * piece, piece), :] = src[_rows(r, piece, dil), :].astype(dst.dtype)
        return
    for b in range(inner):
        tmp[pl.ds(b * big, big), :] = src[_rows(b, big, inner), :]
    for a in range(outer):
        for b in range(inner):
            dst[pl.ds((inner * a + b) * seg + part * piece, piece), :] = (
                tmp[_rows(b * big + a, piece, outer), :].astype(dst.dtype))


def _to_token_order(dst, src, dil, tmp):
    inner, outer, big, seg = _reorder_plan(dil)
    if outer == 1:
        for r in range(dil):
            dst[_rows(r, seg, dil), :] = src[pl.ds(r * seg, seg), :]
        return
    for a in range(outer):
        for b in range(inner):
            tmp[_rows(b * big + a, seg, outer), :] = src[pl.ds((inner * a + b) * seg, seg), :]
    for b in range(inner):
        dst[_rows(b, big, inner), :] = tmp[pl.ds(b * big, big), :]


def _in_proj(h, w, q_norm, k_norm, name):
    s_len, d = h.shape
    tm = PROJ_TILE
    assert tm == SLAB and IN_BLOCK == HEADS * HEAD_DIM
    steps = w.shape[1] // IN_BLOCK
    n_qkv = 3 * QKV // IN_BLOCK
    halves = 2
    rows = [pl.ds(p * (tm // halves), tm // halves) for p in range(halves)]

    gains = jnp.concatenate([q_norm, k_norm, jnp.ones((6, HEAD_DIM), F32)], axis=0)

    def body(h_ref, w_ref, gains_ref, qkv_ref, rest_ref, hat_ref, tok_s, tmp_s):
        j = pl.program_id(1)
        sect = j // N_GROUPS
        multiply = lambda p: _dot(h_ref[rows[p], :], w_ref[...])

        def emit(gi):
            dil = DILATIONS[gi]
            res = [multiply(p) for p in range(halves)]
            gain = gains_ref[pl.ds(sect, 1), :]
            plain = sect == 2
            for p in range(halves):
                qkv_ref[rows[p], :] = res[p]
                for hh in range(HEADS):
                    cols = slice(hh * HEAD_DIM, (hh + 1) * HEAD_DIM)
                    x = res[p][:, cols]
                    tok_s[...] = (x * jnp.where(plain, 1.0, _rms(x))) * gain
                    _to_residue_order(hat_ref.at[:, cols], tok_s, dil, tmp_s, p, halves)

        for gi in range(N_GROUPS):
            pl.when((j < n_qkv) & (j % N_GROUPS == gi))(lambda gi=gi: emit(gi))

        @pl.when(j >= n_qkv)
        def _():
            for p in range(halves):
                rest_ref[rows[p], :] = multiply(p).astype(BF16)

    qkv_blk = pl.BlockSpec((tm, IN_BLOCK), lambda i, j: (i, jnp.minimum(j, n_qkv - 1)))
    return pl.pallas_call(
        body, name=name, grid=(s_len // tm, steps),
        in_specs=[pl.BlockSpec((tm, d), lambda i, j: (i, 0)), pl.BlockSpec((d, IN_BLOCK), lambda i, j: (0, j)),
                  pl.BlockSpec((8, HEAD_DIM), lambda i, j: (0, 0))],
        out_specs=[qkv_blk, pl.BlockSpec((tm, IN_BLOCK), lambda i, j: (i, jnp.maximum(j - n_qkv, 0))), qkv_blk],
        out_shape=[jax.ShapeDtypeStruct((s_len, 3 * QKV), F32),
                   jax.ShapeDtypeStruct((s_len, w.shape[1] - 3 * QKV), BF16),
                   jax.ShapeDtypeStruct((s_len, 3 * QKV), BF16)],
        scratch_shapes=[pltpu.VMEM((tm // halves, HEAD_DIM), F32)] * 2,
        compiler_params=_params(2),
    )(h, w, gains)


def _in_proj_bwd(dq, dk, dv, drest, w, name, after=()):
    s_len = dq.shape[0]
    d = w.shape[0]
    tm = PROJ_TILE
    steps = w.shape[1] // IN_BLOCK
    n_qkv = QKV // IN_BLOCK

    def body(dq_ref, dk_ref, dv_ref, dr_ref, w_ref, o_ref, acc_ref):
        j = pl.program_id(1)

        @pl.when(j == 0)
        def _():
            acc_ref[...] = jnp.zeros_like(acc_ref)

        def add(a_ref):
            acc_ref[...] += _dot_nt(a_ref[...], w_ref[...])

        _pick_part(j, n_qkv, [dq_ref, dk_ref, dv_ref, dr_ref], add)

        @pl.when(j == steps - 1)
        def _():
            o_ref[...] = acc_ref[...]

    return pl.pallas_call(
        _ordered(body, 5, after), name=name, grid=(s_len // tm, steps),
        in_specs=(_in_parts(tm, n_qkv, steps - 3 * n_qkv) + [pl.BlockSpec((d, IN_BLOCK), lambda i, j: (0, j))]
                  + [ANY] * len(after)),
        out_specs=pl.BlockSpec((tm, d), lambda i, j: (i, 0)),
        out_shape=jax.ShapeDtypeStruct((s_len, d), F32),
        scratch_shapes=[pltpu.VMEM((tm, d), F32)],
        compiler_params=_params(2),
    )(dq, dk, dv, drest, w, *after)


def _wgrad(x, y, x_spec, y_spec, out_shape, out_spec, acc_shape, n_chunks, name, x_transposed=False, after=()):
    s_len = y.shape[-2]
    ts = WGRAD_TILE
    steps = s_len // ts

    def body(x_ref, y_ref, o_ref, acc_ref):
        s = pl.program_id(1)

        @pl.when(s == 0)
        def _():
            acc_ref[...] = jnp.zeros_like(acc_ref)

        acc_ref[...] += (_dot if x_transposed else _dot_tn)(x_ref[...], y_ref[...])

        @pl.when(s == steps - 1)
        def _():
            o_ref[...] = acc_ref[...].astype(o_ref.dtype)

    return pl.pallas_call(
        _ordered(body, 2, after), name=name, grid=(n_chunks, steps),
        in_specs=[x_spec(ts), y_spec(ts)] + [ANY] * len(after), out_specs=out_spec,
        out_shape=jax.ShapeDtypeStruct(out_shape, BF16),
        scratch_shapes=[pltpu.VMEM(acc_shape, F32)],
        compiler_params=_params(2),
    )(x, y, *after)


def _pieces(width, piece=256):
    return [slice(a, min(a + piece, width)) for a in range(0, width, piece)]


def _ffn_fwd(h, w_gate, w_up, w_down, name):
    s_len, d = h.shape
    n_chunks, _, fs = w_gate.shape
    tm = FFN_TILE

    def body(h_ref, wg_ref, wu_ref, wd_ref, g_ref, u_ref, y_ref):
        j = pl.program_id(1)
        hv = h_ref[...]
        pieces = _pieces(fs)
        first = lambda cols: (_dot(hv, wg_ref[:, cols]), _dot(hv, wu_ref[:, cols]))
        total = None
        ahead = first(pieces[0])
        for k, cols in enumerate(pieces):
            g, u = ahead
            if k + 1 < len(pieces):
                ahead = first(pieces[k + 1])
            g_ref[:, cols] = g.astype(BF16)
            u_ref[:, cols] = u.astype(BF16)
            act = (g * _sigmoid(g)) * u
            part = _dot(act.astype(BF16), wd_ref[cols, :])
            total = part if total is None else total + part

        @pl.when(j == 0)
        def _():
            y_ref[...] = total

        @pl.when(j > 0)
        def _():
            y_ref[...] += total

    tile = pl.BlockSpec((tm, d), lambda i, j: (i, 0))
    hid = pl.BlockSpec((None, tm, fs), lambda i, j: (j, i, 0))
    w_in_spec = pl.BlockSpec((None, d, fs), lambda i, j: (j, 0, 0))
    hid_shape = jax.ShapeDtypeStruct((n_chunks, s_len, fs), BF16)
    return pl.pallas_call(
        body, name=name, grid=(s_len // tm, n_chunks),
        in_specs=[tile, w_in_spec, w_in_spec, pl.BlockSpec((None, fs, d), lambda i, j: (j, 0, 0))],
        out_specs=[hid, hid, tile],
        out_shape=[hid_shape, hid_shape, jax.ShapeDtypeStruct((s_len, d), F32)],
        compiler_params=_params(2),
    )(h, w_gate, w_up, w_down)


def _ffn_bwd(dy, g_pre, u_pre, w_gate, w_up, w_down, name):
    s_len, d = dy.shape
    n_chunks, _, fs = w_gate.shape
    tm = FFN_TILE

    def body(dy_ref, g_ref, u_ref, wg_ref, wu_ref, wd_ref, dh_ref, dg_ref, du_ref, a_ref):
        j = pl.program_id(1)
        dyv = dy_ref[...]
        pieces = _pieces(fs)
        first = lambda cols: _dot_nt(dyv, wd_ref[cols, :])
        total = None
        ahead = first(pieces[0])
        for k, cols in enumerate(pieces):
            da = ahead
            if k + 1 < len(pieces):
                ahead = first(pieces[k + 1])
            g = g_ref[:, cols].astype(F32)
            u = u_ref[:, cols].astype(F32)
            sg = _sigmoid(g)
            silu = g * sg
            dg = (da * u * (sg * (1.0 + g * (1.0 - sg)))).astype(BF16)
            du = (da * silu).astype(BF16)
            dg_ref[:, cols] = dg
            du_ref[:, cols] = du
            a_ref[:, cols] = (silu * u).astype(BF16)
            part = _dot_nt(dg, wg_ref[:, cols]) + _dot_nt(du, wu_ref[:, cols])
            total = part if total is None else total + part

        @pl.when(j == 0)
        def _():
            dh_ref[...] = total

        @pl.when(j > 0)
        def _():
            dh_ref[...] += total

    tile = pl.BlockSpec((tm, d), lambda i, j: (i, 0))
    hid = pl.BlockSpec((None, tm, fs), lambda i, j: (j, i, 0))
    w_in_spec = pl.BlockSpec((None, d, fs), lambda i, j: (j, 0, 0))
    hid_shape = jax.ShapeDtypeStruct((n_chunks, s_len, fs), BF16)
    return pl.pallas_call(
        body, name=name, grid=(s_len // tm, n_chunks),
        in_specs=[tile, hid, hid, w_in_spec, w_in_spec, pl.BlockSpec((None, fs, d), lambda i, j: (j, 0, 0))],
        out_specs=[tile, hid, hid, hid],
        out_shape=[jax.ShapeDtypeStruct((s_len, d), F32), hid_shape, hid_shape, hid_shape],
        compiler_params=_params(2),
    )(dy, g_pre, u_pre, w_gate, w_up, w_down)


def _ffn_wgrads(ht, dg, du, act, dy, tag, after=()):
    n_chunks, s_len, fs = dg.shape
    d = ht.shape[0]
    tok = lambda ts: pl.BlockSpec((ts, d), lambda c, s: (s, 0))
    tok_t = lambda ts: pl.BlockSpec((d, ts), lambda c, s: (0, s))
    hid = lambda ts: pl.BlockSpec((None, ts, fs), lambda c, s: (c, s, 0))
    d_up = pl.BlockSpec((None, d, fs), lambda c, s: (c, 0, 0))
    d_down = pl.BlockSpec((None, fs, d), lambda c, s: (c, 0, 0))
    dwg = _wgrad(ht, dg, tok_t, hid, (n_chunks, d, fs), d_up, (d, fs), n_chunks, tag + "_dwg", True, after)
    dwu = _wgrad(ht, du, tok_t, hid, (n_chunks, d, fs), d_up, (d, fs), n_chunks, tag + "_dwu", True, after)
    dwd = _wgrad(act, dy, hid, tok, (n_chunks, fs, d), d_down, (fs, d), n_chunks, tag + "_dwd", False, after)
    return dwg, dwu, dwd


def _band_bias():
    qi = lax.broadcasted_iota(jnp.int32, (ATTN_BLOCK, 2 * ATTN_BLOCK), 0)
    kj = lax.broadcasted_iota(jnp.int32, (ATTN_BLOCK, 2 * ATTN_BLOCK), 1)
    band = (kj >= qi) & (kj <= qi + ATTN_BLOCK)
    return jnp.where(band, 0.0, NEG), jnp.where(band & (kj >= ATTN_BLOCK), 0.0, NEG)


def _qkv_specs(slab_of, sections):
    def spec(sect, back):
        return pl.BlockSpec((SLAB, HEAD_DIM),
                            lambda h, s, g: (jnp.maximum(slab_of(s) - back, 0), (sect * N_GROUPS + g) * HEADS + h))
    return [spec(sect, back) for sect, back in sections]


HAT_BLOCKS = [(0, 0), (1, 0), (2, 0), (1, 1), (2, 1)]


def _stage_keys(k_ref, v_ref, kp_ref, vp_ref, kbuf, vbuf, dil, n):
    run = SLAB // dil
    for r in range(dil):
        own, before = pl.ds(r * run, run), pl.ds(2 * r * run, run)
        kbuf[pl.ds((2 * r + 1) * run, run), :] = k_ref[own, :]
        vbuf[pl.ds((2 * r + 1) * run, run), :] = v_ref[own, :]

        @pl.when(n > 0)
        def _():
            kbuf[before, :] = kp_ref[own, :]
            vbuf[before, :] = vp_ref[own, :]

        @pl.when(n == 0)
        def _():
            kbuf[before, :] = jnp.zeros((run, HEAD_DIM), BF16)
            vbuf[before, :] = jnp.zeros((run, HEAD_DIM), BF16)


def _for_each_tile(dil, n, first_fn, rest_fn):
    run = SLAB // dil
    bias, first_bias = _band_bias()
    tiles = []
    for jj in range(run // ATTN_BLOCK):
        start = jj * ATTN_BLOCK
        tile_bias = jnp.where(n == 0, first_bias, bias) if jj == 0 else bias
        for r in range(dil):
            tiles.append((pl.ds(r * run + start, ATTN_BLOCK),
                          pl.ds((2 * r + 1) * run - ATTN_BLOCK + start, 2 * ATTN_BLOCK), tile_bias))
    ahead = first_fn(*tiles[0])
    for t, tile in enumerate(tiles):
        begun = ahead
        if t + 1 < len(tiles):
            ahead = first_fn(*tiles[t + 1])
        rest_fn(*tile, begun)


def _attn_fwd(hat, name):
    s_len = hat.shape[0]
    e = HEAD_DIM
    n_slabs = s_len // SLAB

    def body(q_ref, k_ref, v_ref, kp_ref, vp_ref, o_ref, lse_ref, kbuf, vbuf, m_s, l_s, acc_s, m_p, l_p, acc_p, tmp_s):
        n, grp = pl.program_id(1), pl.program_id(2)

        def run(gi, dil):
            _stage_keys(k_ref, v_ref, kp_ref, vp_ref, kbuf, vbuf, dil, n)

            def scores(q_rows, kv_rows, bias):
                return _dot_nt(q_ref[q_rows, :], kbuf[kv_rows, :])

            def rest(q_rows, kv_rows, bias, qk):
                s = qk * ATTN_SCALE + bias
                m = jnp.max(s, axis=-1, keepdims=True)
                p = jnp.exp(s - m)
                m_p[q_rows, :] = jnp.broadcast_to(m, (ATTN_BLOCK, e))
                l_p[q_rows, :] = jnp.broadcast_to(jnp.sum(p, axis=-1, keepdims=True), (ATTN_BLOCK, e))
                acc_p[q_rows, :] = _dot(p.astype(BF16), vbuf[kv_rows, :])

            _for_each_tile(dil, n, scores, rest)
            _to_token_order(m_s.at[gi], m_p, dil, tmp_s)
            _to_token_order(l_s.at[gi], l_p, dil, tmp_s)
            _to_token_order(acc_s.at[gi], acc_p, dil, tmp_s)

        for gi, dil in enumerate(DILATIONS):
            pl.when(grp == gi)(lambda gi=gi, dil=dil: run(gi, dil))

        @pl.when(grp == N_GROUPS - 1)
        def _():
            m_all = jnp.maximum(jnp.maximum(m_s[0], m_s[1]), m_s[2])
            den = jnp.zeros((SLAB, e), F32)
            num = jnp.zeros((SLAB, e), F32)
            for gi in range(N_GROUPS):
                w = jnp.exp(m_s[gi] - m_all)
                den += l_s[gi] * w
                num += acc_s[gi] * w
            o_ref[...] = num / den
            lse_ref[...] = m_all + jnp.log(den)

    out = pl.BlockSpec((SLAB, e), lambda h, n, g: (n, h))
    return pl.pallas_call(
        body, name=name, grid=(HEADS, n_slabs, N_GROUPS),
        in_specs=_qkv_specs(lambda n: n, HAT_BLOCKS),
        out_specs=[out, out],
        out_shape=[jax.ShapeDtypeStruct((s_len, HEADS * e), F32)] * 2,
        scratch_shapes=[pltpu.VMEM((2 * SLAB, e), BF16), pltpu.VMEM((2 * SLAB, e), BF16),
                        pltpu.VMEM((N_GROUPS, SLAB, e), F32), pltpu.VMEM((N_GROUPS, SLAB, e), F32),
                        pltpu.VMEM((N_GROUPS, SLAB, e), F32)]
        + [pltpu.VMEM((SLAB, e), F32)] * 4,
        compiler_params=_params(3),
    )(hat, hat, hat, hat, hat)


def _attn_bwd(qkv, hat, d_out, out, lse, q_norm, k_norm, name):
    s_len = qkv.shape[0]
    e = HEAD_DIM
    n_slabs = s_len // SLAB

    def body(q_ref, k_ref, v_ref, kp_ref, vp_ref, qraw_ref, kraw_ref, do_ref, o_ref, lse_ref, qn_ref, kn_ref,
             dq_ref, dk_ref, dv_ref, st_ref, kbuf, vbuf, stat_s, dqs, dkb, dvb, dk_tok, dv_tok, carry,
             do_p, stat_p, dq_p, dk_p, dv_p, tmp_s, do16_p):
        head, step, grp = pl.program_id(0), pl.program_id(1), pl.program_id(2)
        n = n_slabs - 1 - step
        dkb[...] = jnp.zeros_like(dkb)
        dvb[...] = jnp.zeros_like(dvb)
        @pl.when(grp == 0)
        def _():
            lane = lax.broadcasted_iota(jnp.int32, (SLAB, e), 1)
            stat_s[...] = jnp.where(lane < e // 2, lse_ref[...],
                                    jnp.sum(do_ref[...] * o_ref[...], axis=-1, keepdims=True))

        @pl.when((head == 0) & (step == 0) & (grp == 0))
        def _():
            st_ref[...] = jnp.zeros_like(st_ref)

        def run(gi, dil):
            seg = SLAB // dil
            _stage_keys(k_ref, v_ref, kp_ref, vp_ref, kbuf, vbuf, dil, n)

            @pl.when(step == 0)
            def _():
                carry[gi] = jnp.zeros((2, SLAB, e), F32)

            _to_residue_order(do_p, do_ref, dil, tmp_s)
            do16_p[...] = do_p[...].astype(BF16)
            _to_residue_order(stat_p, stat_s, dil, tmp_s)

            def scores(q_rows, kv_rows, bias):
                return _dot_nt(q_ref[q_rows, :], kbuf[kv_rows, :]), _dot_nt(do16_p[q_rows, :], vbuf[kv_rows, :])

            def rest(q_rows, kv_rows, bias, begun):
                qk, dp = begun
                q = q_ref[q_rows, :]
                k = kbuf[kv_rows, :]
                stat = stat_p[q_rows, :]
                p = jnp.exp(qk * ATTN_SCALE + bias - stat[:, 0:1])
                ds = (p * (dp - stat[:, e // 2:e // 2 + 1]) * ATTN_SCALE).astype(BF16)
                dq_p[q_rows, :] = _dot(ds, k)
                dkb[kv_rows, :] += _dot_tn(ds, q)
                dvb[kv_rows, :] += _dot_tn(p.astype(BF16), do16_p[q_rows, :])

            _for_each_tile(dil, n, scores, rest)
            for r in range(dil):
                own, before = pl.ds((2 * r + 1) * seg, seg), pl.ds(2 * r * seg, seg)
                kept = pl.ds(r * seg, seg)
                dk_p[kept, :] = dkb[own, :] + carry.at[gi, 0][kept, :]
                dv_p[kept, :] = dvb[own, :] + carry.at[gi, 1][kept, :]
                carry.at[gi, 0][kept, :] = dkb[before, :]
                carry.at[gi, 1][kept, :] = dvb[before, :]
            _to_token_order(dqs, dq_p, dil, tmp_s)
            _to_token_order(dk_tok, dk_p, dil, tmp_s)
            _to_token_order(dv_tok, dv_p, dil, tmp_s)

            def norm_bwd(raw, gain, d_hat):
                r = _rms(raw)
                y = raw * r
                dy = d_hat * gain
                return r * (dy - y * jnp.mean(dy * y, axis=-1, keepdims=True)), jnp.sum(d_hat * y, axis=0, keepdims=True)

            dq, dqn = norm_bwd(qraw_ref[...], qn_ref[...], dqs[...])
            dk, dkn = norm_bwd(kraw_ref[...], kn_ref[...], dk_tok[...])
            dq_ref[...] = dq.astype(BF16)
            dk_ref[...] = dk.astype(BF16)
            dv_ref[...] = dv_tok[...].astype(BF16)
            st_ref[0:1, :] += dqn
            st_ref[1:2, :] += dkn

        for gi, dil in enumerate(DILATIONS):
            pl.when(grp == gi)(lambda gi=gi, dil=dil: run(gi, dil))

    slab_of = lambda s: n_slabs - 1 - s
    small = pl.BlockSpec((1, e), lambda h, s, g: (0, 0))
    head_blk = pl.BlockSpec((SLAB, e), lambda h, s, g: (slab_of(s), h))
    grad_blk = pl.BlockSpec((SLAB, e), lambda h, s, g: (slab_of(s), g * HEADS + h))
    grad_shape = jax.ShapeDtypeStruct((s_len, QKV), BF16)
    return pl.pallas_call(
        body, name=name, grid=(HEADS, n_slabs, N_GROUPS),
        in_specs=(_qkv_specs(slab_of, HAT_BLOCKS) + _qkv_specs(slab_of, [(0, 0), (1, 0)])
                  + [head_blk, head_blk, head_blk, small, small]),
        out_specs=[grad_blk, grad_blk, grad_blk, pl.BlockSpec((8, e), lambda h, s, g: (0, 0))],
        out_shape=[grad_shape, grad_shape, grad_shape, jax.ShapeDtypeStruct((8, e), F32)],
        scratch_shapes=[pltpu.VMEM((2 * SLAB, e), BF16), pltpu.VMEM((2 * SLAB, e), BF16), pltpu.VMEM((SLAB, e), F32),
                        pltpu.VMEM((SLAB, e), F32), pltpu.VMEM((2 * SLAB, e), F32), pltpu.VMEM((2 * SLAB, e), F32),
                        pltpu.VMEM((SLAB, e), F32), pltpu.VMEM((SLAB, e), F32),
                        pltpu.VMEM((N_GROUPS, 2, SLAB, e), F32)]
        + [pltpu.VMEM((SLAB, e), F32)] * 6 + [pltpu.VMEM((SLAB, e), BF16)],
        compiler_params=_params(3),
    )(hat, hat, hat, hat, hat, qkv, qkv, d_out, out, lse, q_norm, k_norm)


def _shift_rows(x, by, edge, forward):
    t_len = x.shape[0]
    row = lax.broadcasted_iota(jnp.int32, x.shape, 0)
    if forward:
        out = pltpu.roll(x, by, 0)
        for i in range(by):
            out = jnp.where(row == i, edge[8 - by + i:8 - by + i + 1, :], out)
    else:
        out = pltpu.roll(x, t_len - by, 0)
        for i in range(by):
            out = jnp.where(row == t_len - by + i, edge[i:i + 1, :], out)
    return out


def _mix_fwd(x, o, rest, mod, mod_next, conv_w, w_attn, w_conv, w_out, name):
    s_len, d = x.shape
    tm = MIX_TILE
    a_w = o.shape[1]

    def body(x_ref, o_ref, u_ref, b_ref, c_ref, ga_ref, gc_ref, mod_ref, modn_ref, cw_ref, wa_ref, wc_ref, wo_ref,
             xo_ref, z_ref, ya_ref, yc_ref, conv_ref, yb_ref, m_ref, o16_ref, h_ref, ht_ref, carry):
        @pl.when(pl.program_id(0) == 0)
        def _():
            carry[...] = jnp.zeros_like(carry)

        xc = c_ref[...].astype(F32) * u_ref[...].astype(F32)
        edge = carry[...]
        conv = (_shift_rows(xc, 2, edge, True) * cw_ref[0:1, :] + _shift_rows(xc, 1, edge, True) * cw_ref[1:2, :]
                + xc * cw_ref[2:3, :])
        carry[...] = xc[tm - 8:tm, :]
        yb = (b_ref[...].astype(F32) * conv).astype(BF16)
        o16 = o_ref[...].astype(BF16)
        ya = _dot(o16, wa_ref[...])
        yc = _dot(yb, wc_ref[...])
        merged = (_sigmoid(ga_ref[...].astype(F32)) * ya + _sigmoid(gc_ref[...].astype(F32)) * yc).astype(BF16)
        z = _dot(merged, wo_ref[...])
        xo = x_ref[...] + mod_ref[2:3, :] * z
        xo_ref[...] = xo
        hn = ((xo * _rms(xo)) * modn_ref[3:4, :]) * (1.0 + modn_ref[1:2, :]) + modn_ref[0:1, :]
        h_ref[...] = hn.astype(BF16)
        ht_ref[...] = hn.T.astype(BF16)
        z_ref[...] = z.astype(BF16)
        ya_ref[...] = ya.astype(BF16)
        yc_ref[...] = yc.astype(BF16)
        conv_ref[...] = conv.astype(BF16)
        yb_ref[...] = yb
        m_ref[...] = merged
        o16_ref[...] = o16

    tile = pl.BlockSpec((tm, d), lambda i: (i, 0))
    sect = lambda k: pl.BlockSpec((tm, d), lambda i: (i, k))
    att = pl.BlockSpec((tm, a_w), lambda i: (i, 0))
    const = lambda shape: pl.BlockSpec(shape, lambda i: (0, 0))
    f32_out = jax.ShapeDtypeStruct((s_len, d), F32)
    b16_out = jax.ShapeDtypeStruct((s_len, d), BF16)
    return pl.pallas_call(
        body, name=name, grid=(s_len // tm,),
        in_specs=[tile, att, sect(0), sect(1), sect(2), sect(3), sect(4), const((8, d)), const((8, d)), const((8, d)),
                  const((a_w, d)), const((d, d)), const((d, d))],
        out_specs=[tile] * 7 + [att, tile, pl.BlockSpec((d, tm), lambda i: (0, i))],
        out_shape=[f32_out] + [b16_out] * 6 + [jax.ShapeDtypeStruct((s_len, a_w), BF16), b16_out,
                                               jax.ShapeDtypeStruct((d, s_len), BF16)],
        scratch_shapes=[pltpu.VMEM((8, d), F32)],
        compiler_params=_params(1),
    )(x, o, rest, rest, rest, rest, rest, mod, mod_next, conv_w, w_attn, w_conv, w_out)


def _mix_bwd(dxo, ya, yc, conv, rest, mod, conv_w, w_attn, w_conv, w_out, a_w, name, after=()):
    s_len, d = dxo.shape
    tm = MIX_TILE
    n_tiles = s_len // tm

    def body(dxo_ref, ya_ref, yc_ref, conv_ref, u_ref, b_ref, c_ref, ga_ref, gc_ref, mod_ref, cw_ref,
             wa_ref, wc_ref, wo_ref, do_ref, drest_ref, dz_ref, dya_ref, dyc_ref, st_ref, carry):
        @pl.when(pl.program_id(0) == 0)
        def _():
            carry[...] = jnp.zeros_like(carry)
            st_ref[...] = jnp.zeros_like(st_ref)

        dz = (mod_ref[2:3, :] * dxo_ref[...]).astype(BF16)
        dz_ref[...] = dz
        dm = _dot_nt(dz, wo_ref[...])
        sa, sc = _sigmoid(ga_ref[...].astype(F32)), _sigmoid(gc_ref[...].astype(F32))
        dya = (dm * sa).astype(BF16)
        dyc = (dm * sc).astype(BF16)
        dya_ref[...] = dya
        dyc_ref[...] = dyc
        drest_ref[:, 3 * d:4 * d] = (dm * ya_ref[...].astype(F32) * (sa * (1.0 - sa))).astype(BF16)
        drest_ref[:, 4 * d:5 * d] = (dm * yc_ref[...].astype(F32) * (sc * (1.0 - sc))).astype(BF16)
        do_ref[...] = _dot_nt(dya, wa_ref[...])
        dyb = _dot_nt(dyc, wc_ref[...])
        drest_ref[:, d:2 * d] = (dyb * conv_ref[...].astype(F32)).astype(BF16)
        dconv = dyb * b_ref[...].astype(F32)
        edge = carry[...]
        sh1 = _shift_rows(dconv, 1, edge, False)
        sh2 = _shift_rows(dconv, 2, edge, False)
        carry[...] = dconv[0:8, :]
        dxc = dconv * cw_ref[2:3, :] + sh1 * cw_ref[1:2, :] + sh2 * cw_ref[0:1, :]
        u, c = u_ref[...].astype(F32), c_ref[...].astype(F32)
        xc = c * u
        drest_ref[:, 0:d] = (dxc * c).astype(BF16)
        drest_ref[:, 2 * d:3 * d] = (dxc * u).astype(BF16)
        st_ref[0:1, :] += jnp.sum(xc * sh2, axis=0, keepdims=True)
        st_ref[1:2, :] += jnp.sum(xc * sh1, axis=0, keepdims=True)
        st_ref[2:3, :] += jnp.sum(xc * dconv, axis=0, keepdims=True)

    rev = lambda i: n_tiles - 1 - i
    tile = pl.BlockSpec((tm, d), lambda i: (rev(i), 0))
    sect = lambda k: pl.BlockSpec((tm, d), lambda i: (rev(i), k))
    const = lambda shape: pl.BlockSpec(shape, lambda i: (0, 0))
    b16_out = jax.ShapeDtypeStruct((s_len, d), BF16)
    return pl.pallas_call(
        _ordered(body, 14, after), name=name, grid=(n_tiles,),
        in_specs=[tile, tile, tile, tile, sect(0), sect(1), sect(2), sect(3), sect(4), const((8, d)), const((8, d)),
                  const((a_w, d)), const((d, d)), const((d, d))] + [ANY] * len(after),
        out_specs=[pl.BlockSpec((tm, a_w), lambda i: (rev(i), 0)), pl.BlockSpec((tm, 5 * d), lambda i: (rev(i), 0)),
                   tile, tile, tile, const((8, d))],
        out_shape=[jax.ShapeDtypeStruct((s_len, a_w), F32), jax.ShapeDtypeStruct((s_len, 5 * d), BF16),
                   b16_out, b16_out, b16_out, jax.ShapeDtypeStruct((8, d), F32)],
        scratch_shapes=[pltpu.VMEM((8, d), F32)],
        compiler_params=_params(1),
    )(dxo, ya, yc, conv, rest, rest, rest, rest, rest, mod, conv_w, w_attn, w_conv, w_out, *after)


ADA_COLS = 128


def _ada_fwd(c_all, w_shard, b_shard, name):
    d, cols = w_shard.shape

    def body(c_ref, w_ref, b_ref, o_ref):
        cv = c_ref[...]
        o_ref[...] = jnp.dot(cv * _sigmoid(cv), w_ref[...], preferred_element_type=F32,
                             precision=lax.Precision.HIGHEST) + b_ref[...]

    return pl.pallas_call(
        body, name=name, grid=(cols // ADA_COLS,),
        in_specs=[pl.BlockSpec((8, d), lambda j: (0, 0)), pl.BlockSpec((d, ADA_COLS), lambda j: (0, j)),
                  pl.BlockSpec((1, ADA_COLS), lambda j: (0, j))],
        out_specs=pl.BlockSpec((8, ADA_COLS), lambda j: (0, j)),
        out_shape=jax.ShapeDtypeStruct((8, cols), F32),
        compiler_params=_params(1),
    )(c_all, w_shard, b_shard)


def _ada_bwd(c_all, dmod_shard, w, m, v, name):
    d, cols = w.shape

    def body(c_ref, dm_ref, w_ref, m_ref, v_ref, g_ref, d_ref, nm_ref, nv_ref):
        cv = c_ref[...]
        g = lax.dot_general(cv * _sigmoid(cv), dm_ref[...], (((0,), (0,)), ((), ())),
                            preferred_element_type=F32, precision=lax.Precision.HIGHEST)
        g_ref[...] = g
        d_ref[...], nm_ref[...], nv_ref[...] = _adamw_math(w_ref[...], g, m_ref[...], v_ref[...])

    blk = pl.BlockSpec((d, ADA_COLS), lambda j: (0, j))
    shape = jax.ShapeDtypeStruct((d, cols), F32)
    return pl.pallas_call(
        body, name=name, grid=(cols // ADA_COLS,),
        in_specs=[pl.BlockSpec((8, d), lambda j: (0, 0)), pl.BlockSpec((8, ADA_COLS), lambda j: (0, j)), blk, blk, blk],
        out_specs=[blk] * 4, out_shape=[shape] * 4,
        compiler_params=_params(1),
    )(c_all, dmod_shard, w, m, v)


def _small_update(parts, w, m, v, name):
    n = w.shape[1]

    def body(p_ref, w_ref, m_ref, v_ref, g_ref, d_ref, nm_ref, nv_ref):
        g = p_ref[0:1, :]
        for i in range(1, 8):
            g = g + p_ref[i:i + 1, :]
        g_ref[...] = g
        d_ref[...], nm_ref[...], nv_ref[...] = _adamw_math(w_ref[...], g, m_ref[...], v_ref[...])

    shape = jax.ShapeDtypeStruct((1, n), F32)
    return pl.pallas_call(body, name=name, out_shape=[shape] * 4, compiler_params=_params())(parts, w, m, v)


def _cols_to_shards(w, n):
    r, nc = w.shape
    return w.reshape(r, n, nc // n).transpose(1, 0, 2)


def kernel(x, c, w_ada, b_ada, norm_ffn1, ffn1_w_gate, ffn1_w_up, ffn1_w_down, norm_mix, w_in, q_norm, k_norm, conv_w, w_attn_branch, w_conv_branch, w_out, norm_ffn2, ffn2_w_gate, ffn2_w_up, ffn2_w_down, loss_target, m_w_ada, m_b_ada, m_norm_ffn1, m_ffn1_w_gate, m_ffn1_w_up, m_ffn1_w_down, m_norm_mix, m_w_in, m_q_norm, m_k_norm, m_conv_w, m_w_attn_branch, m_w_conv_branch, m_w_out, m_norm_ffn2, m_ffn2_w_gate, m_ffn2_w_up, m_ffn2_w_down, v_w_ada, v_b_ada, v_norm_ffn1, v_ffn1_w_gate, v_ffn1_w_up, v_ffn1_w_down, v_norm_mix, v_w_in, v_q_norm, v_k_norm, v_conv_w, v_w_attn_branch, v_w_conv_branch, v_w_out, v_norm_ffn2, v_ffn2_w_gate, v_ffn2_w_up, v_ffn2_w_down):
    ix, iy, ic = _place()
    chip = 2 * ix + iy
    me = 4 * ix + 2 * iy + ic
    xs = x[0]
    target = loss_target[0]
    s_len, d = xs.shape
    ada_cols = w_ada.shape[2]
    conv_cols = conv_w.shape[2]

    conv_rows = jnp.zeros((8, conv_cols), F32).at[0:3].set(conv_w[0])
    small_in = jnp.concatenate([jnp.broadcast_to(c, (8, d)), conv_rows], axis=1)
    small_all = _allgather8(small_in, "gather_c").reshape(8, 8, d + conv_cols)
    c_all = small_all[:, 0, :d]
    conv_full = small_all[0::2, 0:3, d:].transpose(1, 0, 2).reshape(3, N_CHIPS * conv_cols)
    conv_pad = jnp.zeros((8, N_CHIPS * conv_cols), F32).at[0:3].set(conv_full)
    b_shard = lax.dynamic_slice(b_ada, (0, chip * ada_cols), (1, ada_cols))
    mod_part = _ada_fwd(c_all, w_ada[0], b_shard, "ada_fwd")
    mod_all = _allgather8(mod_part, "gather_mod").reshape(N_CHIPS, 2, 8, ada_cols)[:, 0]
    mod_mine = lax.dynamic_slice(mod_all, (0, me, 0), (N_CHIPS, 1, ada_cols)).reshape(9, d)

    def mod_rows(i, gain):
        return jnp.zeros((8, d), F32).at[0:3].set(mod_mine[3 * i:3 * i + 3]).at[3:4].set(gain)

    mod1, mod2, mod3 = mod_rows(0, norm_ffn1), mod_rows(1, norm_mix), mod_rows(2, norm_ffn2)

    to16 = lambda w: w[0].astype(BF16)
    wg1, wu1, wd1 = _gather_weights([to16(ffn1_w_gate), to16(ffn1_w_up), to16(ffn1_w_down)], [False] * 3,
                                    "gather_ffn1", 1)
    h1, h1t = _norm_mod(xs, mod1, "norm1")
    (w_in_full,) = _gather_weights([to16(w_in)], [True], "gather_w_in", 2, after=(wd1, h1))

    g1, u1, y1 = _ffn_fwd(h1, wg1, wu1, wd1, "ffn1_fwd")
    x1, h2, h2t = _norm_mod(xs, mod2, "norm2", prev=(y1, mod1, 0.5))
    qkv, rest, qkv_hat = _in_proj(h2, w_in_full, q_norm, k_norm, "in_proj")
    w_ab, w_cb_g, w_o_g, wg2, wu2, wd2 = _gather_weights(
        [to16(w_attn_branch), to16(w_conv_branch), to16(w_out),
         to16(ffn2_w_gate), to16(ffn2_w_up), to16(ffn2_w_down)], [True] + [False] * 5,
        "gather_rest", 3, after=(h2,))
    a_w = w_ab.shape[0]
    w_cb = w_cb_g.reshape(d, d)
    w_o = w_o_g.reshape(d, d)
    o, lse = _attn_fwd(qkv_hat, "attn_fwd")
    x2, z, ya, yc, conv, yb, merged, o16, h3, h3t = _mix_fwd(x1, o, rest, mod2, mod3, conv_pad, w_ab, w_cb, w_o,
                                                             "mix_fwd")
    g3, u3, y3 = _ffn_fwd(h3, wg2, wu2, wd2, "ffn2_fwd")
    dx3, dy3, loss_part = _loss_grad(x2, y3, mod3, target, "loss")

    c_idx = jnp.reshape(ic, (1,)).astype(jnp.int32)
    chip_idx = jnp.stack([chip, ic]).astype(jnp.int32)

    def pair_send(grads, tag, collective_id):
        return _rs_pair_exchange(grads, "rs_pair_" + tag, collective_id)

    def chip_send(grads, from_sibling, names, tag, collective_id, after):
        pair_sums = [_pair_add(g, r, c_idx, "pair_add_" + nm, after) for g, r, nm in zip(grads, from_sibling, names)]
        return pair_sums, _rs_chip_exchange(pair_sums, "rs_chips_" + tag, collective_id)

    def reduce_finish(pair_sums, from_chips, names, tag, after):
        totals = [_chip_add(p, r, chip_idx, "chip_add_" + nm, after)
                  for p, r, nm in zip(pair_sums, from_chips, names)]
        return dict(zip(names, _rs_share(totals, "rs_share_" + tag)))

    names_a = ["ffn2_w_gate", "ffn2_w_up", "ffn2_w_down"]
    names_b = ["w_in", "w_attn_branch", "w_conv_branch", "w_out"]
    names_c = ["ffn1_w_gate", "ffn1_w_up", "ffn1_w_down"]

    dh3, dg3, du3, a3 = _ffn_bwd(dy3, g3, u3, wg2, wu2, wd2, "ffn2_bwd")
    grads_a = list(_ffn_wgrads(h3t, dg3, du3, a3, dy3, "ffn2"))
    sibling_a = pair_send(grads_a, "a", 7)
    dx2, st3 = _norm_bwd(dh3, x2, mod3, dx3, y3, 0.5, "norm3_bwd")
    sums_a, chips_a = chip_send(grads_a, sibling_a, names_a, "a", 4, after=(dx2,))

    do, drest, dz, dya, dyc, st_conv = _mix_bwd(dx2, ya, yc, conv, rest, mod2, conv_pad, w_ab, w_cb, w_o, a_w,
                                                "mix_bwd", after=tuple(sums_a))
    dq, dk, dv, st_qk = _attn_bwd(qkv, qkv_hat, do, o, lse, q_norm, k_norm, "attn_bwd")
    tok = lambda width: (lambda ts: pl.BlockSpec((ts, width), lambda cc, s: (s, 0)))
    colblk = lambda width: (lambda ts: pl.BlockSpec((ts, width), lambda cc, s: (s, cc)))
    tok_t = lambda ts: pl.BlockSpec((d, ts), lambda cc, s: (0, s))
    whole = pl.BlockSpec((d, QKV), lambda cc, s: (0, 0))
    dw_in = [_wgrad(h2t, part, tok_t, tok(QKV), (d, QKV), whole, (d, QKV), 1, "dw_in_" + nm, True)
             for part, nm in ((dq, "q"), (dk, "k"), (dv, "v"))]
    dw_in.append(_wgrad(h2t, drest, tok_t, colblk(d), (d, 5 * d), pl.BlockSpec((d, d), lambda cc, s: (0, cc)),
                        (d, d), 5, "dw_in_rest", True))
    dw_in = _cols_to_shards(jnp.concatenate(dw_in, axis=1), N_CHIPS)
    shard_w = d // N_CHIPS
    dw_ab = _wgrad(o16, dya, tok(a_w), colblk(shard_w), (a_w, d), pl.BlockSpec((a_w, shard_w), lambda cc, s: (0, cc)),
                   (a_w, shard_w), N_CHIPS, "dw_attn_branch")
    dw_ab = _cols_to_shards(dw_ab, N_CHIPS)
    row_out = pl.BlockSpec((None, shard_w, d), lambda cc, s: (cc, 0, 0))
    dw_cb = _wgrad(yb, dyc, colblk(shard_w), tok(d), (N_CHIPS, shard_w, d), row_out, (shard_w, d), N_CHIPS, "dw_conv_branch")
    dw_o = _wgrad(merged, dz, colblk(shard_w), tok(d), (N_CHIPS, shard_w, d), row_out, (shard_w, d), N_CHIPS, "dw_out")
    shard_grads = reduce_finish(sums_a, chips_a, names_a, "a", after=(dw_in, dw_o))
    grads_b = [dw_in, dw_ab, dw_cb, dw_o]
    sibling_b = pair_send(grads_b, "b", 8)

    dh2 = _in_proj_bwd(dq, dk, dv, drest, w_in_full, "in_proj_bwd")
    sums_b, chips_b = chip_send(grads_b, sibling_b, names_b, "b", 5, after=(dh2,))
    dx1, st2, dy1 = _norm_bwd(dh2, x1, mod2, dx2, z, 1.0, "norm2_bwd", after=tuple(sums_b), prev=(mod1, 0.5))
    dh1, dg1, du1, a1 = _ffn_bwd(dy1, g1, u1, wg1, wu1, wd1, "ffn1_bwd")
    dx0, st1 = _norm_bwd(dh1, xs, mod1, dx1, y1, 0.5, "norm1_bwd")
    grads_c = list(_ffn_wgrads(h1t, dg1, du1, a1, dy1, "ffn1"))
    sibling_c = pair_send(grads_c, "c", 9)
    shard_grads.update(reduce_finish(sums_b, chips_b, names_b, "b", after=tuple(grads_c)))

    dmod = jnp.concatenate([st1[0:3], st2[0:3], st3[0:3]], axis=0).reshape(1, 9 * d)
    loss_cols = jnp.zeros((1, HEAD_DIM), F32).at[0, 0].set(jnp.sum(loss_part))
    small = jnp.concatenate([dmod, st1[3:4], st2[3:4], st3[3:4], st_qk[0:1], st_qk[1:2],
                             st_conv[0:3].reshape(1, 3 * d), loss_cols], axis=1)
    small_all = _allgather8(jnp.broadcast_to(small, (8, small.shape[1])), "gather_small").reshape(8, 8, -1)[:, 0]
    loss = (0.5 / d) * jnp.sum(small_all[:, -HEAD_DIM])
    small_all = small_all[:, :-HEAD_DIM]
    dmod_all = small_all[:, :9 * d]
    dmod_shard = lax.dynamic_slice(dmod_all, (0, chip * ada_cols), (8, ada_cols))
    g_w_ada, d_w_ada, nm_w_ada, nv_w_ada = _ada_bwd(c_all, dmod_shard, w_ada[0], m_w_ada[0], v_w_ada[0], "ada_bwd")

    vec_names = ["b_ada", "norm_ffn1", "norm_mix", "norm_ffn2", "q_norm", "k_norm"]
    vec_w = [b_ada, norm_ffn1, norm_mix, norm_ffn2, q_norm, k_norm]
    vec_m = [m_b_ada, m_norm_ffn1, m_norm_mix, m_norm_ffn2, m_q_norm, m_k_norm]
    vec_v = [v_b_ada, v_norm_ffn1, v_norm_mix, v_norm_ffn2, v_q_norm, v_k_norm]
    n_vec = sum(w.shape[1] for w in vec_w)
    cat = lambda arrs: jnp.concatenate(arrs, axis=1)
    vec_out = _small_update(small_all[:, :n_vec], cat(vec_w), cat(vec_m), cat(vec_v), "small_update")
    conv_parts = small_all[:, n_vec:].reshape(8, 3, N_CHIPS * conv_cols)
    conv_parts = lax.dynamic_slice(conv_parts, (0, 0, chip * conv_cols), (8, 3, conv_cols)).reshape(8, 3 * conv_cols)
    flat3 = lambda w: w[0].reshape(1, 3 * conv_cols)
    conv_out = _small_update(conv_parts, flat3(conv_w), flat3(m_conv_w), flat3(v_conv_w), "conv_update")

    res = {"w_ada": [t[None] for t in (g_w_ada, d_w_ada, nm_w_ada, nv_w_ada)],
           "conv_w": [t.reshape(1, 3, conv_cols) for t in conv_out]}
    off = 0
    for nm, w in zip(vec_names, vec_w):
        width = w.shape[1]
        res[nm] = [t[:, off:off + width] for t in vec_out]
        off += width
    big = {"ffn1_w_gate": (ffn1_w_gate, m_ffn1_w_gate, v_ffn1_w_gate), "ffn1_w_up": (ffn1_w_up, m_ffn1_w_up, v_ffn1_w_up),
           "ffn1_w_down": (ffn1_w_down, m_ffn1_w_down, v_ffn1_w_down), "w_in": (w_in, m_w_in, v_w_in),
           "w_attn_branch": (w_attn_branch, m_w_attn_branch, v_w_attn_branch),
           "w_conv_branch": (w_conv_branch, m_w_conv_branch, v_w_conv_branch), "w_out": (w_out, m_w_out, v_w_out),
           "ffn2_w_gate": (ffn2_w_gate, m_ffn2_w_gate, v_ffn2_w_gate), "ffn2_w_up": (ffn2_w_up, m_ffn2_w_up, v_ffn2_w_up),
           "ffn2_w_down": (ffn2_w_down, m_ffn2_w_down, v_ffn2_w_down)}
    def update(nm, after=()):
        w, m, v = big[nm]
        g, delta, new_m, new_v = _adamw(w[0], shard_grads[nm], m[0], v[0], "adamw_" + nm, after)
        res[nm] = [t[None] for t in (g, delta, new_m, new_v)]
        return new_v

    last = tuple(shard_grads[nm] for nm in names_b)
    for nm in names_a:
        last = (update(nm, last),)
    sums_c, chips_c = chip_send(grads_c, sibling_c, names_c, "c", 6, after=last)
    last = tuple(sums_c)
    for nm in names_b:
        last = (update(nm, last),)
    shard_grads.update(reduce_finish(sums_c, chips_c, names_c, "c", after=last))
    for nm in names_c:
        update(nm)

    order = ["w_ada", "b_ada", "norm_ffn1", "ffn1_w_gate", "ffn1_w_up", "ffn1_w_down", "norm_mix", "w_in", "q_norm",
             "k_norm", "conv_w", "w_attn_branch", "w_conv_branch", "w_out", "norm_ffn2", "ffn2_w_gate", "ffn2_w_up",
             "ffn2_w_down"]
    return (loss, dx0[None], *[res[nm][0] for nm in order], *[res[nm][1] for nm in order],
            *[res[nm][2] for nm in order], *[res[nm][3] for nm in order])
```

```python
import jax
import jax.numpy as jnp
from jax import lax
from jax.experimental import pallas as pl
from jax.experimental.pallas import tpu as pltpu
from jax.experimental.pallas import tpu_sc as plsc

F32 = jnp.float32
BF16 = jnp.bfloat16
MESH = pl.DeviceIdType.MESH
ANY = pl.BlockSpec(memory_space=pl.ANY)

NORM_EPS = 1e-6
HEAD_DIM = 128
N_GROUPS = 3
HEADS = 4
DILATIONS = (1, 4, 16)
ATTN_BLOCK = 128
SLAB = ATTN_BLOCK * max(DILATIONS)
QKV = N_GROUPS * HEADS * HEAD_DIM
ATTN_SCALE = HEAD_DIM ** -0.5
NEG = -1e30
N_CHIPS = 4

ADAM_LR = 0.001
ADAM_B1 = 0.9
ADAM_B2 = 0.999
ADAM_EPS = 1e-08
ADAM_WD = 0.01
ADAM_STEP = 10

VMEM_LIMIT_BYTES = 56 * 1024 * 1024
TOKEN_TILE = 512
FFN_TILE = 1024
PROJ_TILE = 2048
WGRAD_TILE = 2048
IN_BLOCK = 512
MIX_TILE = 256
ACC_PIECES = 4
ADAMW_TILE_BYTES = 3 * 512 * 1024


def _params(n_axes=0):
    return pltpu.CompilerParams(
        dimension_semantics=("arbitrary",) * n_axes if n_axes else None,
        vmem_limit_bytes=VMEM_LIMIT_BYTES)


def _dot(a, b):
    return jnp.dot(a, b, preferred_element_type=F32)


def _dot_nt(a, b):
    return lax.dot_general(a, b, (((1,), (1,)), ((), ())), preferred_element_type=F32)


def _dot_tn(a, b):
    return lax.dot_general(a, b, (((0,), (0,)), ((), ())), preferred_element_type=F32)


def _sigmoid(x):
    return 1.0 / (1.0 + jnp.exp(-x))


def _place():
    return lax.axis_index("x"), lax.axis_index("y"), lax.axis_index("c")


def _ordered(body, n_in, after):
    if not after:
        return body
    return lambda *refs: body(*refs[:n_in], *refs[n_in + len(after):])


def _allgather8(block, name):
    m_per, n = block.shape

    def body(x_ref, out_ref, send_sems, recv_sems, local_sem):
        x, y, c = _place()
        me, sibling = (x, y, c), (x, y, 1 - c)
        chips = [(1 - x, y), (x, 1 - y), (1 - x, 1 - y)]

        def rows(px, py, pc):
            return out_ref.at[pl.ds((4 * px + 2 * py + pc) * m_per, m_per), :]

        def copy(k, blk, to, src=None):
            return pltpu.make_async_remote_copy(
                src_ref=rows(*blk) if src is None else src, dst_ref=rows(*blk),
                send_sem=send_sems.at[k], recv_sem=recv_sems.at[k],
                device_id=to, device_id_type=MESH)

        mine = pltpu.make_async_copy(x_ref, rows(*me), local_sem)
        mine.start()
        first = [copy(0, me, sibling, src=x_ref)]
        first += [copy(1 + j, me, (*chip, c), src=x_ref) for j, chip in enumerate(chips)]
        for cp in first:
            cp.start()
        passed = [copy(4 + j, (*chip, c), sibling) for j, chip in enumerate(chips)]
        for j, chip in enumerate(chips):
            copy(1 + j, (*chip, c), me).wait_recv()
            passed[j].start()
        copy(0, sibling, me).wait_recv()
        for j, chip in enumerate(chips):
            copy(4 + j, (*chip, 1 - c), me).wait_recv()
        for cp in first + passed:
            cp.wait_send()
        mine.wait()

    return pl.pallas_call(
        body, name=name,
        out_shape=jax.ShapeDtypeStruct((8 * m_per, n), block.dtype),
        in_specs=[pl.BlockSpec(memory_space=pltpu.VMEM)],
        out_specs=pl.BlockSpec(memory_space=pltpu.VMEM),
        scratch_shapes=[pltpu.SemaphoreType.DMA((7,)), pltpu.SemaphoreType.DMA((7,)),
                        pltpu.SemaphoreType.DMA],
        compiler_params=_params(),
    )(block)


def _handshake(peers):
    barrier = pltpu.get_barrier_semaphore()
    for peer in peers:
        pl.semaphore_signal(barrier, inc=1, device_id=peer, device_id_type=MESH)
    pl.semaphore_wait(barrier, len(peers))


def _gather_weights(shards, by_cols, name, collective_id, after=()):
    n_arr = len(shards)

    def body(*refs):
        srcs, outs = refs[:n_arr], refs[n_arr + len(after):2 * n_arr + len(after)]
        send_sems, recv_sems, local_sems = refs[2 * n_arr + len(after):]
        x, y, c = _place()
        me_dev, sibling = (x, y, c), (x, y, 1 - c)
        chips = [(1 - x, y), (x, 1 - y), (1 - x, 1 - y)]
        me = 2 * x + y
        _handshake([sibling] + [(*chip, c) for chip in chips])

        def place(k, chip_idx, rows):
            if by_cols[k]:
                width = srcs[k].shape[1]
                return outs[k].at[rows, pl.ds(pl.multiple_of(chip_idx * width, 128), width)]
            return outs[k].at[chip_idx, rows]

        def copy(k, slot, chip_idx, half_sel, to, from_shard=False):
            half = srcs[k].shape[0] // 2
            rows = pl.ds(half_sel * half, half)
            dst = place(k, chip_idx, rows)
            return pltpu.make_async_remote_copy(
                src_ref=srcs[k].at[rows] if from_shard else dst, dst_ref=dst,
                send_sem=send_sems.at[6 * k + slot], recv_sem=recv_sems.at[6 * k + slot],
                device_id=to, device_id_type=MESH)

        own = [pltpu.make_async_copy(srcs[k], place(k, me, pl.ds(0, srcs[k].shape[0])), local_sems.at[k])
               for k in range(n_arr)]
        for cp in own:
            cp.start()
        sent = []
        for k in range(n_arr):
            for j, chip in enumerate(chips):
                sent.append(copy(k, j, me, c, (*chip, c), from_shard=True))
                sent[-1].start()
        for k in range(n_arr):
            for j, chip in enumerate(chips):
                chip_idx = 2 * chip[0] + chip[1]
                copy(k, j, chip_idx, c, me_dev).wait_recv()
                sent.append(copy(k, 3 + j, chip_idx, c, sibling))
                sent[-1].start()
        for k in range(n_arr):
            for j, chip in enumerate(chips):
                copy(k, 3 + j, 2 * chip[0] + chip[1], 1 - c, me_dev).wait_recv()
        for cp in sent:
            cp.wait_send()
        for cp in own:
            cp.wait()

    def gathered(k):
        r, cols = shards[k].shape
        return (r, N_CHIPS * cols) if by_cols[k] else (N_CHIPS, r, cols)

    return pl.kernel(
        body, name=name,
        out_type=[jax.ShapeDtypeStruct(gathered(k), shards[k].dtype) for k in range(n_arr)],
        mesh=plsc.ScalarSubcoreMesh(axis_name="sequencer", num_cores=1),
        scratch_types=[pltpu.SemaphoreType.DMA((6 * n_arr,)), pltpu.SemaphoreType.DMA((6 * n_arr,)),
                       pltpu.SemaphoreType.DMA((n_arr,))],
        compiler_params=pltpu.CompilerParams(collective_id=collective_id),
    )(*shards, *after)


def _rs_pair_exchange(grads, name, collective_id):
    n_arr = len(grads)

    def body(*refs):
        srcs, outs = refs[:n_arr], refs[n_arr:2 * n_arr]
        send_sems, recv_sems = refs[2 * n_arr:]
        x, y, c = _place()
        _handshake([(x, y, 1 - c)])
        cps = []
        for k in range(n_arr):
            half = srcs[k].shape[1] // 2
            cps.append(pltpu.make_async_remote_copy(
                src_ref=srcs[k].at[:, pl.ds((1 - c) * half, half)], dst_ref=outs[k],
                send_sem=send_sems.at[k], recv_sem=recv_sems.at[k],
                device_id=(x, y, 1 - c), device_id_type=MESH))
            cps[-1].start()
        for cp in cps:
            cp.wait_recv()
        for cp in cps:
            cp.wait_send()

    return pl.kernel(
        body, name=name,
        out_type=[jax.ShapeDtypeStruct((g.shape[0], g.shape[1] // 2, g.shape[2]), g.dtype) for g in grads],
        mesh=plsc.ScalarSubcoreMesh(axis_name="sequencer", num_cores=1),
        scratch_types=[pltpu.SemaphoreType.DMA((n_arr,)), pltpu.SemaphoreType.DMA((n_arr,))],
        compiler_params=pltpu.CompilerParams(collective_id=collective_id),
    )(*grads)


def _rs_chip_exchange(sums, name, collective_id):
    n_arr = len(sums)

    def body(*refs):
        srcs, outs = refs[:n_arr], refs[n_arr:2 * n_arr]
        send_sems, recv_sems = refs[2 * n_arr:]
        x, y, c = _place()
        chips = [(1 - x, y), (x, 1 - y), (1 - x, 1 - y)]
        _handshake([(*chip, c) for chip in chips])
        cps = []
        for k in range(n_arr):
            for j, chip in enumerate(chips):
                cps.append(pltpu.make_async_remote_copy(
                    src_ref=srcs[k].at[2 * chip[0] + chip[1]], dst_ref=outs[k].at[j],
                    send_sem=send_sems.at[3 * k + j], recv_sem=recv_sems.at[3 * k + j],
                    device_id=(*chip, c), device_id_type=MESH))
                cps[-1].start()
        for cp in cps:
            cp.wait_recv()
        for cp in cps:
            cp.wait_send()

    return pl.kernel(
        body, name=name,
        out_type=[jax.ShapeDtypeStruct((3,) + s.shape[1:], s.dtype) for s in sums],
        mesh=plsc.ScalarSubcoreMesh(axis_name="sequencer", num_cores=1),
        scratch_types=[pltpu.SemaphoreType.DMA((3 * n_arr,)), pltpu.SemaphoreType.DMA((3 * n_arr,))],
        compiler_params=pltpu.CompilerParams(collective_id=collective_id),
    )(*sums)


def _rs_share(totals, name):
    n_arr = len(totals)

    def body(*refs):
        outs = refs[n_arr:2 * n_arr]
        send_sems, recv_sems = refs[2 * n_arr:]
        x, y, c = _place()

        def half_rows(k, sel):
            return outs[k].at[sel]

        cps = []
        for k in range(n_arr):
            cps.append(pltpu.make_async_remote_copy(
                src_ref=half_rows(k, c), dst_ref=half_rows(k, c), send_sem=send_sems.at[k], recv_sem=recv_sems.at[k],
                device_id=(x, y, 1 - c), device_id_type=MESH))
            cps[-1].start()
        for k in range(n_arr):
            pltpu.make_async_remote_copy(
                src_ref=half_rows(k, c), dst_ref=half_rows(k, 1 - c), send_sem=send_sems.at[k],
                recv_sem=recv_sems.at[k], device_id=(x, y, 1 - c), device_id_type=MESH).wait_recv()
        for cp in cps:
            cp.wait_send()

    shared = pl.pallas_call(
        body, name=name,
        out_shape=[jax.ShapeDtypeStruct(t.shape, t.dtype) for t in totals],
        in_specs=[ANY] * n_arr, out_specs=[ANY] * n_arr,
        input_output_aliases={k: k for k in range(n_arr)},
        scratch_shapes=[pltpu.SemaphoreType.DMA((n_arr,)), pltpu.SemaphoreType.DMA((n_arr,))],
        compiler_params=_params(),
    )(*totals)
    return [t.reshape(2 * t.shape[1], t.shape[2]) for t in shared]


def _pair_add(grad, recv, c_idx, name, after=()):
    n, r, cols = grad.shape
    half = r // 2
    rows = half // 2

    def body(_, g_ref, r_ref, o_ref):
        o_ref[...] = (g_ref[...].astype(F32) + r_ref[...].astype(F32)).astype(o_ref.dtype)

    return pl.pallas_call(
        _ordered(body, 3, after), name=name,
        grid_spec=pltpu.PrefetchScalarGridSpec(
            num_scalar_prefetch=1, grid=(n, 2),
            in_specs=[pl.BlockSpec((None, None, rows, cols), lambda s, i, ci: (s, ci[0], i, 0)),
                      pl.BlockSpec((None, rows, cols), lambda s, i, ci: (s, i, 0))] + [ANY] * len(after),
            out_specs=pl.BlockSpec((None, rows, cols), lambda s, i, ci: (s, i, 0))),
        out_shape=jax.ShapeDtypeStruct((n, half, cols), BF16),
        compiler_params=_params(2),
    )(c_idx, grad.reshape(n, 2, half, cols), recv, *after)


def _chip_add(sums, recv, chip_and_core, name, after=()):
    _, half, cols = sums.shape
    rows = half // 2

    def body(_, s_ref, r0_ref, r1_ref, r2_ref, o_ref):
        o_ref[...] = ((s_ref[...].astype(F32) + r0_ref[...].astype(F32))
                      + r1_ref[...].astype(F32)) + r2_ref[...].astype(F32)

    def recv_spec(j):
        return pl.BlockSpec((None, rows, cols), lambda i, ci: (j, i, 0))

    return pl.pallas_call(
        _ordered(body, 5, after), name=name,
        grid_spec=pltpu.PrefetchScalarGridSpec(
            num_scalar_prefetch=1, grid=(2,),
            in_specs=[pl.BlockSpec((None, rows, cols), lambda i, ci: (ci[0], i, 0)),
                      recv_spec(0), recv_spec(1), recv_spec(2)] + [ANY] * len(after),
            out_specs=pl.BlockSpec((None, rows, cols), lambda i, ci: (ci[1], i, 0))),
        out_shape=jax.ShapeDtypeStruct((2, half, cols), F32),
        compiler_params=_params(1),
    )(chip_and_core, sums, recv, recv, recv, *after)


def _rms(x):
    return lax.rsqrt(jnp.mean(x * x, axis=-1, keepdims=True) + NORM_EPS)


def _norm_mod(x, mod, name, prev=None):
    s_len, d = x.shape
    tm = TOKEN_TILE

    def body(*refs):
        if prev is None:
            x_ref, mod_ref, h_ref, ht_ref = refs
            xv = x_ref[...]
        else:
            x_ref, y_ref, modp_ref, mod_ref, xo_ref, h_ref, ht_ref = refs
            xv = x_ref[...] + prev[2] * modp_ref[2:3, :] * y_ref[...]
            xo_ref[...] = xv
        n = (xv * _rms(xv)) * mod_ref[3:4, :]
        h = n * (1.0 + mod_ref[1:2, :]) + mod_ref[0:1, :]
        h_ref[...] = h.astype(BF16)
        ht_ref[...] = h.T.astype(BF16)

    tile = pl.BlockSpec((tm, d), lambda i: (i, 0))
    small = pl.BlockSpec((8, d), lambda i: (0, 0))
    h_specs = [tile, pl.BlockSpec((d, tm), lambda i: (0, i))]
    h_shapes = [jax.ShapeDtypeStruct((s_len, d), BF16), jax.ShapeDtypeStruct((d, s_len), BF16)]
    if prev is None:
        return pl.pallas_call(
            body, name=name, grid=(s_len // tm,), in_specs=[tile, small], out_specs=h_specs, out_shape=h_shapes,
            compiler_params=_params(1))(x, mod)
    return pl.pallas_call(
        body, name=name, grid=(s_len // tm,), in_specs=[tile, tile, small, small],
        out_specs=[tile] + h_specs, out_shape=[jax.ShapeDtypeStruct((s_len, d), F32)] + h_shapes,
        compiler_params=_params(1))(x, prev[0], prev[1], mod)


def _norm_bwd(dh, x, mod, dxo, y_raw, coef, name, after=(), prev=None):
    s_len, d = x.shape
    tm = TOKEN_TILE

    def body(*refs):
        if prev is None:
            dh_ref, x_ref, mod_ref, dxo_ref, y_ref, dx_ref, st_ref = refs
        else:
            dh_ref, x_ref, mod_ref, dxo_ref, y_ref, modp_ref, dx_ref, st_ref, dyp_ref = refs

        @pl.when(pl.program_id(0) == 0)
        def _():
            st_ref[...] = jnp.zeros_like(st_ref)

        xv, dhv, dxov = x_ref[...], dh_ref[...], dxo_ref[...]
        r = _rms(xv)
        xh = xv * r
        gain, scale = mod_ref[3:4, :], mod_ref[1:2, :]
        dn = dhv * (1.0 + scale)
        dxh = dn * gain
        dx = dxov + r * (dxh - xh * jnp.mean(dxh * xh, axis=-1, keepdims=True))
        dx_ref[...] = dx
        if prev is not None:
            dyp_ref[...] = (prev[1] * modp_ref[2:3, :] * dx).astype(BF16)
        st_ref[0:1, :] += jnp.sum(dhv, axis=0, keepdims=True)
        st_ref[1:2, :] += jnp.sum(dhv * (xh * gain), axis=0, keepdims=True)
        st_ref[2:3, :] += coef * jnp.sum(y_ref[...].astype(F32) * dxov, axis=0, keepdims=True)
        st_ref[3:4, :] += jnp.sum(dn * xh, axis=0, keepdims=True)

    tile = pl.BlockSpec((tm, d), lambda i: (i, 0))
    small = pl.BlockSpec((8, d), lambda i: (0, 0))
    operands = [dh, x, mod, dxo, y_raw] + ([] if prev is None else [prev[0]])
    in_specs = [tile, tile, small, tile, tile] + ([] if prev is None else [small])
    out_specs = [tile, small] + ([] if prev is None else [tile])
    out_shape = [jax.ShapeDtypeStruct((s_len, d), F32), jax.ShapeDtypeStruct((8, d), F32)]
    if prev is not None:
        out_shape.append(jax.ShapeDtypeStruct((s_len, d), BF16))
    return pl.pallas_call(
        _ordered(body, len(operands), after), name=name, grid=(s_len // tm,),
        in_specs=in_specs + [ANY] * len(after), out_specs=out_specs, out_shape=out_shape,
        compiler_params=_params(1),
    )(*operands, *after)


def _loss_grad(x, y, mod, target, name):
    s_len, d = x.shape
    tm = TOKEN_TILE

    def body(x_ref, y_ref, mod_ref, t_ref, do_ref, dy_ref, part_ref):
        @pl.when(pl.program_id(0) == 0)
        def _():
            part_ref[...] = jnp.zeros_like(part_ref)

        half_gate = 0.5 * mod_ref[2:3, :]
        err = (x_ref[...] + half_gate * y_ref[...]) - t_ref[...]
        do = err * (1.0 / d)
        do_ref[...] = do
        dy_ref[...] = (half_gate * do).astype(BF16)
        sq = err * err
        part_ref[...] += jnp.sum(sq.reshape(tm // 8, 8, d), axis=0)

    tile = pl.BlockSpec((tm, d), lambda i: (i, 0))
    small = pl.BlockSpec((8, d), lambda i: (0, 0))
    return pl.pallas_call(
        body, name=name, grid=(s_len // tm,),
        in_specs=[tile, tile, small, tile],
        out_specs=[tile, tile, small],
        out_shape=[jax.ShapeDtypeStruct((s_len, d), F32), jax.ShapeDtypeStruct((s_len, d), BF16),
                   jax.ShapeDtypeStruct((8, d), F32)],
        compiler_params=_params(1),
    )(x, y, mod, target)


def _adamw_math(w, g, m, v):
    m = ADAM_B1 * m + (1.0 - ADAM_B1) * g
    v = ADAM_B2 * v + (1.0 - ADAM_B2) * (g * g)
    m_hat = m / (1.0 - ADAM_B1 ** ADAM_STEP)
    v_hat = v / (1.0 - ADAM_B2 ** ADAM_STEP)
    delta = -ADAM_LR * (m_hat / (jnp.sqrt(v_hat) + ADAM_EPS) + ADAM_WD * w)
    return delta, m, v


def _adamw(w, g, m, v, name, after=()):
    r, cols = w.shape
    tr = max([t for t in (r // k for k in (1, 2, 4, 8, 16)) if t % 8 == 0 and r % t == 0
              and t * cols * 4 <= ADAMW_TILE_BYTES] or [r])

    def body(w_ref, g_ref, m_ref, v_ref, go_ref, d_ref, nm_ref, nv_ref):
        gv = g_ref[...]
        go_ref[...] = gv
        d_ref[...], nm_ref[...], nv_ref[...] = _adamw_math(w_ref[...], gv, m_ref[...], v_ref[...])

    tile = pl.BlockSpec((tr, cols), lambda i: (i, 0))
    shape = jax.ShapeDtypeStruct((r, cols), F32)
    return pl.pallas_call(
        _ordered(body, 4, after), name=name, grid=(r // tr,),
        in_specs=[tile] * 4 + [ANY] * len(after), out_specs=[tile] * 4, out_shape=[shape] * 4,
        compiler_params=_params(1),
    )(w, g, m, v, *after)


def _in_parts(tm, n_qkv, n_rest):
    def part(lo, n_blk):
        return pl.BlockSpec((tm, IN_BLOCK), lambda i, j: (i, jnp.clip(j - lo, 0, n_blk - 1)))
    return [part(0, n_qkv), part(n_qkv, n_qkv), part(2 * n_qkv, n_qkv), part(3 * n_qkv, n_rest)]


def _pick_part(j, n_qkv, refs, fn):
    bounds = [0, n_qkv, 2 * n_qkv, 3 * n_qkv]
    for p, ref in enumerate(refs):
        inside = j >= bounds[p]
        if p + 1 < len(refs):
            inside = inside & (j < bounds[p + 1])
        pl.when(inside)(lambda ref=ref: fn(ref))


def _rows(base, count, stride):
    return pl.ds(base, count) if stride == 1 else pl.ds(base, count, stride=stride)


REORDER_STRIDE = 4


def _reorder_plan(dil, parts=1):
    inner = min(dil, REORDER_STRIDE)
    return inner, dil // inner, SLAB // parts // inner, SLAB // dil


def _to_residue_order(dst, src, dil, tmp, part=0, parts=1):
    inner, outer, big, seg = _reorder_plan(dil, parts)
    piece = seg // parts
    if outer == 1:
        for r in range(dil):
            dst[pl.ds(r * seg + part * piece, piece), :] = src[_rows(r, piece, dil), :].astype(dst.dtype)
        return
    for b in range(inner):
        tmp[pl.ds(b * big, big), :] = src[_rows(b, big, inner), :]
    for a in range(outer):
        for b in range(inner):
            dst[pl.ds((inner * a + b) * seg + part * piece, piece), :] = (
                tmp[_rows(b * big + a, piece, outer), :].astype(dst.dtype))


def _to_token_order(dst, src, dil, tmp):
    inner, outer, big, seg = _reorder_plan(dil)
    if outer == 1:
        for r in range(dil):
            dst[_rows(r, seg, dil), :] = src[pl.ds(r * seg, seg), :]
        return
    for a in range(outer):
        for b in range(inner):
            tmp[_rows(b * big + a, seg, outer), :] = src[pl.ds((inner * a + b) * seg, seg), :]
    for b in range(inner):
        dst[_rows(b, big, inner), :] = tmp[pl.ds(b * big, big), :]


def _in_proj(h, w, q_norm, k_norm, name):
    s_len, d = h.shape
    tm = PROJ_TILE
    assert tm == SLAB and IN_BLOCK == HEADS * HEAD_DIM
    steps = w.shape[1] // IN_BLOCK
    n_qkv = 3 * QKV // IN_BLOCK
    halves = 4
    rows = [pl.ds(p * (tm // halves), tm // halves) for p in range(halves)]

    gains = jnp.concatenate([q_norm, k_norm, jnp.ones((6, HEAD_DIM), F32)], axis=0)

    def body(h_ref, w_ref, gains_ref, qkv_ref, rest_ref, hat_ref, tok_s, tmp_s):
        j = pl.program_id(1)
        sect = j // N_GROUPS
        multiply = lambda p: _dot(h_ref[rows[p], :], w_ref[...])

        def emit(gi):
            dil = DILATIONS[gi]
            res = [multiply(p) for p in range(halves)]
            gain = gains_ref[pl.ds(sect, 1), :]
            plain = sect == 2
            for p in range(halves):
                qkv_ref[rows[p], :] = res[p]
                for hh in range(HEADS):
                    cols = slice(hh * HEAD_DIM, (hh + 1) * HEAD_DIM)
                    x = res[p][:, cols]
                    tok_s[...] = (x * jnp.where(plain, 1.0, _rms(x))) * gain
                    _to_residue_order(hat_ref.at[:, cols], tok_s, dil, tmp_s, p, halves)

        for gi in range(N_GROUPS):
            pl.when((j < n_qkv) & (j % N_GROUPS == gi))(lambda gi=gi: emit(gi))

        @pl.when(j >= n_qkv)
        def _():
            for p in range(halves):
                rest_ref[rows[p], :] = multiply(p).astype(BF16)

    qkv_blk = pl.BlockSpec((tm, IN_BLOCK), lambda i, j: (i, jnp.minimum(j, n_qkv - 1)))
    return pl.pallas_call(
        body, name=name, grid=(s_len // tm, steps),
        in_specs=[pl.BlockSpec((tm, d), lambda i, j: (i, 0)), pl.BlockSpec((d, IN_BLOCK), lambda i, j: (0, j)),
                  pl.BlockSpec((8, HEAD_DIM), lambda i, j: (0, 0))],
        out_specs=[qkv_blk, pl.BlockSpec((tm, IN_BLOCK), lambda i, j: (i, jnp.maximum(j - n_qkv, 0))), qkv_blk],
        out_shape=[jax.ShapeDtypeStruct((s_len, 3 * QKV), F32),
                   jax.ShapeDtypeStruct((s_len, w.shape[1] - 3 * QKV), BF16),
                   jax.ShapeDtypeStruct((s_len, 3 * QKV), BF16)],
        scratch_shapes=[pltpu.VMEM((tm // halves, HEAD_DIM), F32)] * 2,
        compiler_params=_params(2),
    )(h, w, gains)


def _in_proj_bwd(dq, dk, dv, drest, w, name, after=()):
    s_len = dq.shape[0]
    d = w.shape[0]
    tm = PROJ_TILE
    steps = w.shape[1] // IN_BLOCK
    n_qkv = QKV // IN_BLOCK

    def body(dq_ref, dk_ref, dv_ref, dr_ref, w_ref, o_ref, acc_ref):
        j = pl.program_id(1)

        @pl.when(j == 0)
        def _():
            acc_ref[...] = jnp.zeros_like(acc_ref)

        def add(a_ref):
            rows = [pl.ds(p * (tm // ACC_PIECES), tm // ACC_PIECES) for p in range(ACC_PIECES)]
            products = [_dot_nt(a_ref[r, :], w_ref[...]) for r in rows]
            for r, product in zip(rows, products):
                acc_ref[r, :] += product

        _pick_part(j, n_qkv, [dq_ref, dk_ref, dv_ref, dr_ref], add)

        @pl.when(j == steps - 1)
        def _():
            o_ref[...] = acc_ref[...]

    return pl.pallas_call(
        _ordered(body, 5, after), name=name, grid=(s_len // tm, steps),
        in_specs=(_in_parts(tm, n_qkv, steps - 3 * n_qkv) + [pl.BlockSpec((d, IN_BLOCK), lambda i, j: (0, j))]
                  + [ANY] * len(after)),
        out_specs=pl.BlockSpec((tm, d), lambda i, j: (i, 0)),
        out_shape=jax.ShapeDtypeStruct((s_len, d), F32),
        scratch_shapes=[pltpu.VMEM((tm, d), F32)],
        compiler_params=_params(2),
    )(dq, dk, dv, drest, w, *after)


def _wgrad(x, y, x_spec, y_spec, out_shape, out_spec, acc_shape, n_chunks, name, x_transposed=False, after=()):
    s_len = y.shape[-2]
    ts = WGRAD_TILE
    steps = s_len // ts

    def body(x_ref, y_ref, o_ref, acc_ref):
        s = pl.program_id(1)

        @pl.when(s == 0)
        def _():
            acc_ref[...] = jnp.zeros_like(acc_ref)

        if x_transposed:
            n_rows = acc_shape[0]
            rows = [pl.ds(p * (n_rows // ACC_PIECES), n_rows // ACC_PIECES) for p in range(ACC_PIECES)]
            products = [_dot(x_ref[r, :], y_ref[...]) for r in rows]
            for r, product in zip(rows, products):
                acc_ref[r, :] += product
        else:
            cols = _pieces(acc_shape[1])
            products = [_dot_tn(x_ref[...], y_ref[:, c]) for c in cols]
            for c, product in zip(cols, products):
                acc_ref[:, c] += product

        @pl.when(s == steps - 1)
        def _():
            o_ref[...] = acc_ref[...].astype(o_ref.dtype)

    return pl.pallas_call(
        _ordered(body, 2, after), name=name, grid=(n_chunks, steps),
        in_specs=[x_spec(ts), y_spec(ts)] + [ANY] * len(after), out_specs=out_spec,
        out_shape=jax.ShapeDtypeStruct(out_shape, BF16),
        scratch_shapes=[pltpu.VMEM(acc_shape, F32)],
        compiler_params=_params(2),
    )(x, y, *after)


def _pieces(width, piece=256):
    return [slice(a, min(a + piece, width)) for a in range(0, width, piece)]


def _ffn_fwd(h, w_gate, w_up, w_down, name):
    s_len, d = h.shape
    n_chunks, _, fs = w_gate.shape
    tm = FFN_TILE

    def body(h_ref, wg_ref, wu_ref, wd_ref, g_ref, u_ref, y_ref):
        j = pl.program_id(1)
        hv = h_ref[...]
        pieces = _pieces(fs)
        first = lambda cols: (_dot(hv, wg_ref[:, cols]), _dot(hv, wu_ref[:, cols]))
        total = None
        ahead = first(pieces[0])
        for k, cols in enumerate(pieces):
            g, u = ahead
            if k + 1 < len(pieces):
                ahead = first(pieces[k + 1])
            g_ref[:, cols] = g.astype(BF16)
            u_ref[:, cols] = u.astype(BF16)
            act = (g * _sigmoid(g)) * u
            part = _dot(act.astype(BF16), wd_ref[cols, :])
            total = part if total is None else total + part

        @pl.when(j == 0)
        def _():
            y_ref[...] = total

        @pl.when(j > 0)
        def _():
            y_ref[...] += total

    tile = pl.BlockSpec((tm, d), lambda i, j: (i, 0))
    hid = pl.BlockSpec((None, tm, fs), lambda i, j: (j, i, 0))
    w_in_spec = pl.BlockSpec((None, d, fs), lambda i, j: (j, 0, 0))
    hid_shape = jax.ShapeDtypeStruct((n_chunks, s_len, fs), BF16)
    return pl.pallas_call(
        body, name=name, grid=(s_len // tm, n_chunks),
        in_specs=[tile, w_in_spec, w_in_spec, pl.BlockSpec((None, fs, d), lambda i, j: (j, 0, 0))],
        out_specs=[hid, hid, tile],
        out_shape=[hid_shape, hid_shape, jax.ShapeDtypeStruct((s_len, d), F32)],
        compiler_params=_params(2),
    )(h, w_gate, w_up, w_down)


def _ffn_bwd(dy, g_pre, u_pre, w_gate, w_up, w_down, name):
    s_len, d = dy.shape
    n_chunks, _, fs = w_gate.shape
    tm = FFN_TILE

    def body(dy_ref, g_ref, u_ref, wg_ref, wu_ref, wd_ref, dh_ref, dg_ref, du_ref, a_ref):
        j = pl.program_id(1)
        dyv = dy_ref[...]
        pieces = _pieces(fs)
        first = lambda cols: _dot_nt(dyv, wd_ref[cols, :])
        total = None
        ahead = first(pieces[0])
        for k, cols in enumerate(pieces):
            da = ahead
            if k + 1 < len(pieces):
                ahead = first(pieces[k + 1])
            g = g_ref[:, cols].astype(F32)
            u = u_ref[:, cols].astype(F32)
            sg = _sigmoid(g)
            silu = g * sg
            dg = (da * u * (sg * (1.0 + g * (1.0 - sg)))).astype(BF16)
            du = (da * silu).astype(BF16)
            dg_ref[:, cols] = dg
            du_ref[:, cols] = du
            a_ref[:, cols] = (silu * u).astype(BF16)
            part = _dot_nt(dg, wg_ref[:, cols]) + _dot_nt(du, wu_ref[:, cols])
            total = part if total is None else total + part

        @pl.when(j == 0)
        def _():
            dh_ref[...] = total

        @pl.when(j > 0)
        def _():
            dh_ref[...] += total

    tile = pl.BlockSpec((tm, d), lambda i, j: (i, 0))
    hid = pl.BlockSpec((None, tm, fs), lambda i, j: (j, i, 0))
    w_in_spec = pl.BlockSpec((None, d, fs), lambda i, j: (j, 0, 0))
    hid_shape = jax.ShapeDtypeStruct((n_chunks, s_len, fs), BF16)
    return pl.pallas_call(
        body, name=name, grid=(s_len // tm, n_chunks),
        in_specs=[tile, hid, hid, w_in_spec, w_in_spec, pl.BlockSpec((None, fs, d), lambda i, j: (j, 0, 0))],
        out_specs=[tile, hid, hid, hid],
        out_shape=[jax.ShapeDtypeStruct((s_len, d), F32), hid_shape, hid_shape, hid_shape],
        compiler_params=_params(2),
    )(dy, g_pre, u_pre, w_gate, w_up, w_down)


def _ffn_wgrads(ht, dg, du, act, dy, tag, after=()):
    n_chunks, s_len, fs = dg.shape
    d = ht.shape[0]
    tok = lambda ts: pl.BlockSpec((ts, d), lambda c, s: (s, 0))
    tok_t = lambda ts: pl.BlockSpec((d, ts), lambda c, s: (0, s))
    hid = lambda ts: pl.BlockSpec((None, ts, fs), lambda c, s: (c, s, 0))
    d_up = pl.BlockSpec((None, d, fs), lambda c, s: (c, 0, 0))
    d_down = pl.BlockSpec((None, fs, d), lambda c, s: (c, 0, 0))
    dwg = _wgrad(ht, dg, tok_t, hid, (n_chunks, d, fs), d_up, (d, fs), n_chunks, tag + "_dwg", True, after)
    dwu = _wgrad(ht, du, tok_t, hid, (n_chunks, d, fs), d_up, (d, fs), n_chunks, tag + "_dwu", True, after)
    dwd = _wgrad(act, dy, hid, tok, (n_chunks, fs, d), d_down, (fs, d), n_chunks, tag + "_dwd", False, after)
    return dwg, dwu, dwd


def _band_bias():
    qi = lax.broadcasted_iota(jnp.int32, (ATTN_BLOCK, 2 * ATTN_BLOCK), 0)
    kj = lax.broadcasted_iota(jnp.int32, (ATTN_BLOCK, 2 * ATTN_BLOCK), 1)
    band = (kj >= qi) & (kj <= qi + ATTN_BLOCK)
    return jnp.where(band, 0.0, NEG), jnp.where(band & (kj >= ATTN_BLOCK), 0.0, NEG)


def _qkv_specs(slab_of, sections):
    def spec(sect, back):
        return pl.BlockSpec((SLAB, HEAD_DIM),
                            lambda h, s, g: (jnp.maximum(slab_of(s) - back, 0), (sect * N_GROUPS + g) * HEADS + h))
    return [spec(sect, back) for sect, back in sections]


HAT_BLOCKS = [(0, 0), (1, 0), (2, 0), (1, 1), (2, 1)]


def _stage_keys(k_ref, v_ref, kp_ref, vp_ref, kbuf, vbuf, dil, n):
    run = SLAB // dil
    for r in range(dil):
        own, before = pl.ds(r * run, run), pl.ds(2 * r * run, run)
        kbuf[pl.ds((2 * r + 1) * run, run), :] = k_ref[own, :]
        vbuf[pl.ds((2 * r + 1) * run, run), :] = v_ref[own, :]

        @pl.when(n > 0)
        def _():
            kbuf[before, :] = kp_ref[own, :]
            vbuf[before, :] = vp_ref[own, :]

        @pl.when(n == 0)
        def _():
            kbuf[before, :] = jnp.zeros((run, HEAD_DIM), BF16)
            vbuf[before, :] = jnp.zeros((run, HEAD_DIM), BF16)


def _for_each_tile(dil, n, first_fn, rest_fn):
    run = SLAB // dil
    bias, first_bias = _band_bias()
    tiles = []
    for jj in range(run // ATTN_BLOCK):
        start = jj * ATTN_BLOCK
        tile_bias = jnp.where(n == 0, first_bias, bias) if jj == 0 else bias
        for r in range(dil):
            tiles.append((pl.ds(r * run + start, ATTN_BLOCK),
                          pl.ds((2 * r + 1) * run - ATTN_BLOCK + start, 2 * ATTN_BLOCK), tile_bias))
    ahead = first_fn(*tiles[0])
    for t, tile in enumerate(tiles):
        begun = ahead
        if t + 1 < len(tiles):
            ahead = first_fn(*tiles[t + 1])
        rest_fn(*tile, begun)


def _attn_fwd(hat, name):
    s_len = hat.shape[0]
    e = HEAD_DIM
    n_slabs = s_len // SLAB

    def body(q_ref, k_ref, v_ref, kp_ref, vp_ref, o_ref, lse_ref, kbuf, vbuf, m_s, l_s, acc_s, m_p, l_p, acc_p, tmp_s):
        n, grp = pl.program_id(1), pl.program_id(2)

        def run(gi, dil):
            _stage_keys(k_ref, v_ref, kp_ref, vp_ref, kbuf, vbuf, dil, n)

            def scores(q_rows, kv_rows, bias):
                return _dot_nt(q_ref[q_rows, :], kbuf[kv_rows, :])

            def rest(q_rows, kv_rows, bias, qk):
                s = qk * ATTN_SCALE + bias
                m = jnp.max(s, axis=-1, keepdims=True)
                p = jnp.exp(s - m)
                m_p[q_rows, :] = jnp.broadcast_to(m, (ATTN_BLOCK, e))
                l_p[q_rows, :] = jnp.broadcast_to(jnp.sum(p, axis=-1, keepdims=True), (ATTN_BLOCK, e))
                acc_p[q_rows, :] = _dot(p.astype(BF16), vbuf[kv_rows, :])

            _for_each_tile(dil, n, scores, rest)
            _to_token_order(m_s.at[gi], m_p, dil, tmp_s)
            _to_token_order(l_s.at[gi], l_p, dil, tmp_s)
            _to_token_order(acc_s.at[gi], acc_p, dil, tmp_s)

        for gi, dil in enumerate(DILATIONS):
            pl.when(grp == gi)(lambda gi=gi, dil=dil: run(gi, dil))

        @pl.when(grp == N_GROUPS - 1)
        def _():
            m_all = jnp.maximum(jnp.maximum(m_s[0], m_s[1]), m_s[2])
            den = jnp.zeros((SLAB, e), F32)
            num = jnp.zeros((SLAB, e), F32)
            for gi in range(N_GROUPS):
                w = jnp.exp(m_s[gi] - m_all)
                den += l_s[gi] * w
                num += acc_s[gi] * w
            o_ref[...] = num / den
            lse_ref[...] = m_all + jnp.log(den)

    out = pl.BlockSpec((SLAB, e), lambda h, n, g: (n, h))
    return pl.pallas_call(
        body, name=name, grid=(HEADS, n_slabs, N_GROUPS),
        in_specs=_qkv_specs(lambda n: n, HAT_BLOCKS),
        out_specs=[out, out],
        out_shape=[jax.ShapeDtypeStruct((s_len, HEADS * e), F32)] * 2,
        scratch_shapes=[pltpu.VMEM((2 * SLAB, e), BF16), pltpu.VMEM((2 * SLAB, e), BF16),
                        pltpu.VMEM((N_GROUPS, SLAB, e), F32), pltpu.VMEM((N_GROUPS, SLAB, e), F32),
                        pltpu.VMEM((N_GROUPS, SLAB, e), F32)]
        + [pltpu.VMEM((SLAB, e), F32)] * 4,
        compiler_params=_params(3),
    )(hat, hat, hat, hat, hat)


def _attn_bwd(qkv, hat, d_out, out, lse, q_norm, k_norm, name):
    s_len = qkv.shape[0]
    e = HEAD_DIM
    n_slabs = s_len // SLAB

    def body(q_ref, k_ref, v_ref, kp_ref, vp_ref, qraw_ref, kraw_ref, do_ref, o_ref, lse_ref, qn_ref, kn_ref,
             dq_ref, dk_ref, dv_ref, st_ref, kbuf, vbuf, stat_s, dqs, dkb, dvb, dk_tok, dv_tok, carry,
             do_p, stat_p, dq_p, dk_p, dv_p, tmp_s, do16_p):
        head, step, grp = pl.program_id(0), pl.program_id(1), pl.program_id(2)
        n = n_slabs - 1 - step
        dkb[...] = jnp.zeros_like(dkb)
        dvb[...] = jnp.zeros_like(dvb)
        @pl.when(grp == 0)
        def _():
            lane = lax.broadcasted_iota(jnp.int32, (SLAB, e), 1)
            stat_s[...] = jnp.where(lane < e // 2, lse_ref[...],
                                    jnp.sum(do_ref[...] * o_ref[...], axis=-1, keepdims=True))

        @pl.when((head == 0) & (step == 0) & (grp == 0))
        def _():
            st_ref[...] = jnp.zeros_like(st_ref)

        def run(gi, dil):
            seg = SLAB // dil
            _stage_keys(k_ref, v_ref, kp_ref, vp_ref, kbuf, vbuf, dil, n)

            @pl.when(step == 0)
            def _():
                carry[gi] = jnp.zeros((2, SLAB, e), F32)

            _to_residue_order(do_p, do_ref, dil, tmp_s)
            do16_p[...] = do_p[...].astype(BF16)
            _to_residue_order(stat_p, stat_s, dil, tmp_s)

            def scores(q_rows, kv_rows, bias):
                return _dot_nt(q_ref[q_rows, :], kbuf[kv_rows, :]), _dot_nt(do16_p[q_rows, :], vbuf[kv_rows, :])

            def rest(q_rows, kv_rows, bias, begun):
                qk, dp = begun
                q = q_ref[q_rows, :]
                k = kbuf[kv_rows, :]
                stat = stat_p[q_rows, :]
                p = jnp.exp(qk * ATTN_SCALE + bias - stat[:, 0:1])
                ds = (p * (dp - stat[:, e // 2:e // 2 + 1]) * ATTN_SCALE).astype(BF16)
                dq_p[q_rows, :] = _dot(ds, k)
                dkb[kv_rows, :] += _dot_tn(ds, q)
                dvb[kv_rows, :] += _dot_tn(p.astype(BF16), do16_p[q_rows, :])

            _for_each_tile(dil, n, scores, rest)
            for r in range(dil):
                own, before = pl.ds((2 * r + 1) * seg, seg), pl.ds(2 * r * seg, seg)
                kept = pl.ds(r * seg, seg)
                dk_p[kept, :] = dkb[own, :] + carry.at[gi, 0][kept, :]
                dv_p[kept, :] = dvb[own, :] + carry.at[gi, 1][kept, :]
                carry.at[gi, 0][kept, :] = dkb[before, :]
                carry.at[gi, 1][kept, :] = dvb[before, :]
            _to_token_order(dqs, dq_p, dil, tmp_s)
            _to_token_order(dk_tok, dk_p, dil, tmp_s)
            _to_token_order(dv_tok, dv_p, dil, tmp_s)

            def norm_bwd(raw, gain, d_hat):
                r = _rms(raw)
                y = raw * r
                dy = d_hat * gain
                return r * (dy - y * jnp.mean(dy * y, axis=-1, keepdims=True)), jnp.sum(d_hat * y, axis=0, keepdims=True)

            dq, dqn = norm_bwd(qraw_ref[...], qn_ref[...], dqs[...])
            dk, dkn = norm_bwd(kraw_ref[...], kn_ref[...], dk_tok[...])
            dq_ref[...] = dq.astype(BF16)
            dk_ref[...] = dk.astype(BF16)
            dv_ref[...] = dv_tok[...].astype(BF16)
            st_ref[0:1, :] += dqn
            st_ref[1:2, :] += dkn

        for gi, dil in enumerate(DILATIONS):
            pl.when(grp == gi)(lambda gi=gi, dil=dil: run(gi, dil))

    slab_of = lambda s: n_slabs - 1 - s
    small = pl.BlockSpec((1, e), lambda h, s, g: (0, 0))
    head_blk = pl.BlockSpec((SLAB, e), lambda h, s, g: (slab_of(s), h))
    grad_blk = pl.BlockSpec((SLAB, e), lambda h, s, g: (slab_of(s), g * HEADS + h))
    grad_shape = jax.ShapeDtypeStruct((s_len, QKV), BF16)
    return pl.pallas_call(
        body, name=name, grid=(HEADS, n_slabs, N_GROUPS),
        in_specs=(_qkv_specs(slab_of, HAT_BLOCKS) + _qkv_specs(slab_of, [(0, 0), (1, 0)])
                  + [head_blk, head_blk, head_blk, small, small]),
        out_specs=[grad_blk, grad_blk, grad_blk, pl.BlockSpec((8, e), lambda h, s, g: (0, 0))],
        out_shape=[grad_shape, grad_shape, grad_shape, jax.ShapeDtypeStruct((8, e), F32)],
        scratch_shapes=[pltpu.VMEM((2 * SLAB, e), BF16), pltpu.VMEM((2 * SLAB, e), BF16), pltpu.VMEM((SLAB, e), F32),
                        pltpu.VMEM((SLAB, e), F32), pltpu.VMEM((2 * SLAB, e), F32), pltpu.VMEM((2 * SLAB, e), F32),
                        pltpu.VMEM((SLAB, e), F32), pltpu.VMEM((SLAB, e), F32),
                        pltpu.VMEM((N_GROUPS, 2, SLAB, e), F32)]
        + [pltpu.VMEM((SLAB, e), F32)] * 6 + [pltpu.VMEM((SLAB, e), BF16)],
        compiler_params=_params(3),
    )(hat, hat, hat, hat, hat, qkv, qkv, d_out, out, lse, q_norm, k_norm)


def _shift_rows(x, by, edge, forward):
    t_len = x.shape[0]
    row = lax.broadcasted_iota(jnp.int32, x.shape, 0)
    if forward:
        out = pltpu.roll(x, by, 0)
        for i in range(by):
            out = jnp.where(row == i, edge[8 - by + i:8 - by + i + 1, :], out)
    else:
        out = pltpu.roll(x, t_len - by, 0)
        for i in range(by):
            out = jnp.where(row == t_len - by + i, edge[i:i + 1, :], out)
    return out


def _mix_fwd(x, o, rest, mod, mod_next, conv_w, w_attn, w_conv, w_out, name):
    s_len, d = x.shape
    tm = MIX_TILE
    a_w = o.shape[1]

    def body(x_ref, o_ref, u_ref, b_ref, c_ref, ga_ref, gc_ref, mod_ref, modn_ref, cw_ref, wa_ref, wc_ref, wo_ref,
             xo_ref, z_ref, ya_ref, yc_ref, conv_ref, yb_ref, m_ref, o16_ref, h_ref, ht_ref, carry):
        @pl.when(pl.program_id(0) == 0)
        def _():
            carry[...] = jnp.zeros_like(carry)

        xc = c_ref[...].astype(F32) * u_ref[...].astype(F32)
        edge = carry[...]
        conv = (_shift_rows(xc, 2, edge, True) * cw_ref[0:1, :] + _shift_rows(xc, 1, edge, True) * cw_ref[1:2, :]
                + xc * cw_ref[2:3, :])
        carry[...] = xc[tm - 8:tm, :]
        yb = (b_ref[...].astype(F32) * conv).astype(BF16)
        o16 = o_ref[...].astype(BF16)
        ya = _dot(o16, wa_ref[...])
        yc = _dot(yb, wc_ref[...])
        merged = (_sigmoid(ga_ref[...].astype(F32)) * ya + _sigmoid(gc_ref[...].astype(F32)) * yc).astype(BF16)
        z = _dot(merged, wo_ref[...])
        xo = x_ref[...] + mod_ref[2:3, :] * z
        xo_ref[...] = xo
        hn = ((xo * _rms(xo)) * modn_ref[3:4, :]) * (1.0 + modn_ref[1:2, :]) + modn_ref[0:1, :]
        h_ref[...] = hn.astype(BF16)
        ht_ref[...] = hn.T.astype(BF16)
        z_ref[...] = z.astype(BF16)
        ya_ref[...] = ya.astype(BF16)
        yc_ref[...] = yc.astype(BF16)
        conv_ref[...] = conv.astype(BF16)
        yb_ref[...] = yb
        m_ref[...] = merged
        o16_ref[...] = o16

    tile = pl.BlockSpec((tm, d), lambda i: (i, 0))
    sect = lambda k: pl.BlockSpec((tm, d), lambda i: (i, k))
    att = pl.BlockSpec((tm, a_w), lambda i: (i, 0))
    const = lambda shape: pl.BlockSpec(shape, lambda i: (0, 0))
    f32_out = jax.ShapeDtypeStruct((s_len, d), F32)
    b16_out = jax.ShapeDtypeStruct((s_len, d), BF16)
    return pl.pallas_call(
        body, name=name, grid=(s_len // tm,),
        in_specs=[tile, att, sect(0), sect(1), sect(2), sect(3), sect(4), const((8, d)), const((8, d)), const((8, d)),
                  const((a_w, d)), const((d, d)), const((d, d))],
        out_specs=[tile] * 7 + [att, tile, pl.BlockSpec((d, tm), lambda i: (0, i))],
        out_shape=[f32_out] + [b16_out] * 6 + [jax.ShapeDtypeStruct((s_len, a_w), BF16), b16_out,
                                               jax.ShapeDtypeStruct((d, s_len), BF16)],
        scratch_shapes=[pltpu.VMEM((8, d), F32)],
        compiler_params=_params(1),
    )(x, o, rest, rest, rest, rest, rest, mod, mod_next, conv_w, w_attn, w_conv, w_out)


def _mix_bwd(dxo, ya, yc, conv, rest, mod, conv_w, w_attn, w_conv, w_out, a_w, name, after=()):
    s_len, d = dxo.shape
    tm = MIX_TILE
    n_tiles = s_len // tm

    def body(dxo_ref, ya_ref, yc_ref, conv_ref, u_ref, b_ref, c_ref, ga_ref, gc_ref, mod_ref, cw_ref,
             wa_ref, wc_ref, wo_ref, do_ref, drest_ref, dz_ref, dya_ref, dyc_ref, st_ref, carry):
        @pl.when(pl.program_id(0) == 0)
        def _():
            carry[...] = jnp.zeros_like(carry)
            st_ref[...] = jnp.zeros_like(st_ref)

        dz = (mod_ref[2:3, :] * dxo_ref[...]).astype(BF16)
        dz_ref[...] = dz
        dm = _dot_nt(dz, wo_ref[...])
        sa, sc = _sigmoid(ga_ref[...].astype(F32)), _sigmoid(gc_ref[...].astype(F32))
        dya = (dm * sa).astype(BF16)
        dyc = (dm * sc).astype(BF16)
        dya_ref[...] = dya
        dyc_ref[...] = dyc
        drest_ref[:, 3 * d:4 * d] = (dm * ya_ref[...].astype(F32) * (sa * (1.0 - sa))).astype(BF16)
        drest_ref[:, 4 * d:5 * d] = (dm * yc_ref[...].astype(F32) * (sc * (1.0 - sc))).astype(BF16)
        do_ref[...] = _dot_nt(dya, wa_ref[...])
        dyb = _dot_nt(dyc, wc_ref[...])
        drest_ref[:, d:2 * d] = (dyb * conv_ref[...].astype(F32)).astype(BF16)
        dconv = dyb * b_ref[...].astype(F32)
        edge = carry[...]
        sh1 = _shift_rows(dconv, 1, edge, False)
        sh2 = _shift_rows(dconv, 2, edge, False)
        carry[...] = dconv[0:8, :]
        dxc = dconv * cw_ref[2:3, :] + sh1 * cw_ref[1:2, :] + sh2 * cw_ref[0:1, :]
        u, c = u_ref[...].astype(F32), c_ref[...].astype(F32)
        xc = c * u
        drest_ref[:, 0:d] = (dxc * c).astype(BF16)
        drest_ref[:, 2 * d:3 * d] = (dxc * u).astype(BF16)
        st_ref[0:1, :] += jnp.sum(xc * sh2, axis=0, keepdims=True)
        st_ref[1:2, :] += jnp.sum(xc * sh1, axis=0, keepdims=True)
        st_ref[2:3, :] += jnp.sum(xc * dconv, axis=0, keepdims=True)

    rev = lambda i: n_tiles - 1 - i
    tile = pl.BlockSpec((tm, d), lambda i: (rev(i), 0))
    sect = lambda k: pl.BlockSpec((tm, d), lambda i: (rev(i), k))
    const = lambda shape: pl.BlockSpec(shape, lambda i: (0, 0))
    b16_out = jax.ShapeDtypeStruct((s_len, d), BF16)
    return pl.pallas_call(
        _ordered(body, 14, after), name=name, grid=(n_tiles,),
        in_specs=[tile, tile, tile, tile, sect(0), sect(1), sect(2), sect(3), sect(4), const((8, d)), const((8, d)),
                  const((a_w, d)), const((d, d)), const((d, d))] + [ANY] * len(after),
        out_specs=[pl.BlockSpec((tm, a_w), lambda i: (rev(i), 0)), pl.BlockSpec((tm, 5 * d), lambda i: (rev(i), 0)),
                   tile, tile, tile, const((8, d))],
        out_shape=[jax.ShapeDtypeStruct((s_len, a_w), F32), jax.ShapeDtypeStruct((s_len, 5 * d), BF16),
                   b16_out, b16_out, b16_out, jax.ShapeDtypeStruct((8, d), F32)],
        scratch_shapes=[pltpu.VMEM((8, d), F32)],
        compiler_params=_params(1),
    )(dxo, ya, yc, conv, rest, rest, rest, rest, rest, mod, conv_w, w_attn, w_conv, w_out, *after)


ADA_COLS = 128


def _ada_fwd(c_all, w_shard, b_shard, name):
    d, cols = w_shard.shape

    def body(c_ref, w_ref, b_ref, o_ref):
        cv = c_ref[...]
        o_ref[...] = jnp.dot(cv * _sigmoid(cv), w_ref[...], preferred_element_type=F32,
                             precision=lax.Precision.HIGHEST) + b_ref[...]

    return pl.pallas_call(
        body, name=name, grid=(cols // ADA_COLS,),
        in_specs=[pl.BlockSpec((8, d), lambda j: (0, 0)), pl.BlockSpec((d, ADA_COLS), lambda j: (0, j)),
                  pl.BlockSpec((1, ADA_COLS), lambda j: (0, j))],
        out_specs=pl.BlockSpec((8, ADA_COLS), lambda j: (0, j)),
        out_shape=jax.ShapeDtypeStruct((8, cols), F32),
        compiler_params=_params(1),
    )(c_all, w_shard, b_shard)


def _ada_bwd(c_all, dmod_shard, w, m, v, name):
    d, cols = w.shape

    def body(c_ref, dm_ref, w_ref, m_ref, v_ref, g_ref, d_ref, nm_ref, nv_ref):
        cv = c_ref[...]
        g = lax.dot_general(cv * _sigmoid(cv), dm_ref[...], (((0,), (0,)), ((), ())),
                            preferred_element_type=F32, precision=lax.Precision.HIGHEST)
        g_ref[...] = g
        d_ref[...], nm_ref[...], nv_ref[...] = _adamw_math(w_ref[...], g, m_ref[...], v_ref[...])

    blk = pl.BlockSpec((d, ADA_COLS), lambda j: (0, j))
    shape = jax.ShapeDtypeStruct((d, cols), F32)
    return pl.pallas_call(
        body, name=name, grid=(cols // ADA_COLS,),
        in_specs=[pl.BlockSpec((8, d), lambda j: (0, 0)), pl.BlockSpec((8, ADA_COLS), lambda j: (0, j)), blk, blk, blk],
        out_specs=[blk] * 4, out_shape=[shape] * 4,
        compiler_params=_params(1),
    )(c_all, dmod_shard, w, m, v)


def _small_update(parts, w, m, v, name):
    n = w.shape[1]

    def body(p_ref, w_ref, m_ref, v_ref, g_ref, d_ref, nm_ref, nv_ref):
        g = p_ref[0:1, :]
        for i in range(1, 8):
            g = g + p_ref[i:i + 1, :]
        g_ref[...] = g
        d_ref[...], nm_ref[...], nv_ref[...] = _adamw_math(w_ref[...], g, m_ref[...], v_ref[...])

    shape = jax.ShapeDtypeStruct((1, n), F32)
    return pl.pallas_call(body, name=name, out_shape=[shape] * 4, compiler_params=_params())(parts, w, m, v)


def _cols_to_shards(w, n):
    r, nc = w.shape
    return w.reshape(r, n, nc // n).transpose(1, 0, 2)


def kernel(x, c, w_ada, b_ada, norm_ffn1, ffn1_w_gate, ffn1_w_up, ffn1_w_down, norm_mix, w_in, q_norm, k_norm, conv_w, w_attn_branch, w_conv_branch, w_out, norm_ffn2, ffn2_w_gate, ffn2_w_up, ffn2_w_down, loss_target, m_w_ada, m_b_ada, m_norm_ffn1, m_ffn1_w_gate, m_ffn1_w_up, m_ffn1_w_down, m_norm_mix, m_w_in, m_q_norm, m_k_norm, m_conv_w, m_w_attn_branch, m_w_conv_branch, m_w_out, m_norm_ffn2, m_ffn2_w_gate, m_ffn2_w_up, m_ffn2_w_down, v_w_ada, v_b_ada, v_norm_ffn1, v_ffn1_w_gate, v_ffn1_w_up, v_ffn1_w_down, v_norm_mix, v_w_in, v_q_norm, v_k_norm, v_conv_w, v_w_attn_branch, v_w_conv_branch, v_w_out, v_norm_ffn2, v_ffn2_w_gate, v_ffn2_w_up, v_ffn2_w_down):
    ix, iy, ic = _place()
    chip = 2 * ix + iy
    me = 4 * ix + 2 * iy + ic
    xs = x[0]
    target = loss_target[0]
    s_len, d = xs.shape
    ada_cols = w_ada.shape[2]
    conv_cols = conv_w.shape[2]

    conv_rows = jnp.zeros((8, conv_cols), F32).at[0:3].set(conv_w[0])
    small_in = jnp.concatenate([jnp.broadcast_to(c, (8, d)), conv_rows], axis=1)
    small_all = _allgather8(small_in, "gather_c").reshape(8, 8, d + conv_cols)
    c_all = small_all[:, 0, :d]
    conv_full = small_all[0::2, 0:3, d:].transpose(1, 0, 2).reshape(3, N_CHIPS * conv_cols)
    conv_pad = jnp.zeros((8, N_CHIPS * conv_cols), F32).at[0:3].set(conv_full)
    b_shard = lax.dynamic_slice(b_ada, (0, chip * ada_cols), (1, ada_cols))
    mod_part = _ada_fwd(c_all, w_ada[0], b_shard, "ada_fwd")
    mod_all = _allgather8(mod_part, "gather_mod").reshape(N_CHIPS, 2, 8, ada_cols)[:, 0]
    mod_mine = lax.dynamic_slice(mod_all, (0, me, 0), (N_CHIPS, 1, ada_cols)).reshape(9, d)

    def mod_rows(i, gain):
        return jnp.zeros((8, d), F32).at[0:3].set(mod_mine[3 * i:3 * i + 3]).at[3:4].set(gain)

    mod1, mod2, mod3 = mod_rows(0, norm_ffn1), mod_rows(1, norm_mix), mod_rows(2, norm_ffn2)

    to16 = lambda w: w[0].astype(BF16)
    wg1, wu1, wd1 = _gather_weights([to16(ffn1_w_gate), to16(ffn1_w_up), to16(ffn1_w_down)], [False] * 3,
                                    "gather_ffn1", 1)
    h1, h1t = _norm_mod(xs, mod1, "norm1")
    (w_in_full,) = _gather_weights([to16(w_in)], [True], "gather_w_in", 2, after=(wd1, h1))

    g1, u1, y1 = _ffn_fwd(h1, wg1, wu1, wd1, "ffn1_fwd")
    x1, h2, h2t = _norm_mod(xs, mod2, "norm2", prev=(y1, mod1, 0.5))
    qkv, rest, qkv_hat = _in_proj(h2, w_in_full, q_norm, k_norm, "in_proj")
    w_ab, w_cb_g, w_o_g, wg2, wu2, wd2 = _gather_weights(
        [to16(w_attn_branch), to16(w_conv_branch), to16(w_out),
         to16(ffn2_w_gate), to16(ffn2_w_up), to16(ffn2_w_down)], [True] + [False] * 5,
        "gather_rest", 3, after=(h2,))
    a_w = w_ab.shape[0]
    w_cb = w_cb_g.reshape(d, d)
    w_o = w_o_g.reshape(d, d)
    o, lse = _attn_fwd(qkv_hat, "attn_fwd")
    x2, z, ya, yc, conv, yb, merged, o16, h3, h3t = _mix_fwd(x1, o, rest, mod2, mod3, conv_pad, w_ab, w_cb, w_o,
                                                             "mix_fwd")
    g3, u3, y3 = _ffn_fwd(h3, wg2, wu2, wd2, "ffn2_fwd")
    dx3, dy3, loss_part = _loss_grad(x2, y3, mod3, target, "loss")

    c_idx = jnp.reshape(ic, (1,)).astype(jnp.int32)
    chip_idx = jnp.stack([chip, ic]).astype(jnp.int32)

    def pair_send(grads, tag, collective_id):
        return _rs_pair_exchange(grads, "rs_pair_" + tag, collective_id)

    def chip_send(grads, from_sibling, names, tag, collective_id, after):
        pair_sums = [_pair_add(g, r, c_idx, "pair_add_" + nm, after) for g, r, nm in zip(grads, from_sibling, names)]
        return pair_sums, _rs_chip_exchange(pair_sums, "rs_chips_" + tag, collective_id)

    def reduce_finish(pair_sums, from_chips, names, tag, after):
        totals = [_chip_add(p, r, chip_idx, "chip_add_" + nm, after)
                  for p, r, nm in zip(pair_sums, from_chips, names)]
        return dict(zip(names, _rs_share(totals, "rs_share_" + tag)))

    names_a = ["ffn2_w_gate", "ffn2_w_up", "ffn2_w_down"]
    names_b = ["w_in", "w_attn_branch", "w_conv_branch", "w_out"]
    names_c = ["ffn1_w_gate", "ffn1_w_up", "ffn1_w_down"]

    dh3, dg3, du3, a3 = _ffn_bwd(dy3, g3, u3, wg2, wu2, wd2, "ffn2_bwd")
    grads_a = list(_ffn_wgrads(h3t, dg3, du3, a3, dy3, "ffn2"))
    sibling_a = pair_send(grads_a, "a", 7)
    dx2, st3 = _norm_bwd(dh3, x2, mod3, dx3, y3, 0.5, "norm3_bwd")
    sums_a, chips_a = chip_send(grads_a, sibling_a, names_a, "a", 4, after=(dx2,))

    do, drest, dz, dya, dyc, st_conv = _mix_bwd(dx2, ya, yc, conv, rest, mod2, conv_pad, w_ab, w_cb, w_o, a_w,
                                                "mix_bwd", after=tuple(sums_a))
    dq, dk, dv, st_qk = _attn_bwd(qkv, qkv_hat, do, o, lse, q_norm, k_norm, "attn_bwd")
    tok = lambda width: (lambda ts: pl.BlockSpec((ts, width), lambda cc, s: (s, 0)))
    colblk = lambda width: (lambda ts: pl.BlockSpec((ts, width), lambda cc, s: (s, cc)))
    tok_t = lambda ts: pl.BlockSpec((d, ts), lambda cc, s: (0, s))
    whole = pl.BlockSpec((d, QKV), lambda cc, s: (0, 0))
    dw_in = [_wgrad(h2t, part, tok_t, tok(QKV), (d, QKV), whole, (d, QKV), 1, "dw_in_" + nm, True)
             for part, nm in ((dq, "q"), (dk, "k"), (dv, "v"))]
    dw_in.append(_wgrad(h2t, drest, tok_t, colblk(d), (d, 5 * d), pl.BlockSpec((d, d), lambda cc, s: (0, cc)),
                        (d, d), 5, "dw_in_rest", True))
    dw_in = _cols_to_shards(jnp.concatenate(dw_in, axis=1), N_CHIPS)
    shard_w = d // N_CHIPS
    dw_ab = _wgrad(o16, dya, tok(a_w), colblk(shard_w), (a_w, d), pl.BlockSpec((a_w, shard_w), lambda cc, s: (0, cc)),
                   (a_w, shard_w), N_CHIPS, "dw_attn_branch")
    dw_ab = _cols_to_shards(dw_ab, N_CHIPS)
    row_out = pl.BlockSpec((None, shard_w, d), lambda cc, s: (cc, 0, 0))
    dw_cb = _wgrad(yb, dyc, colblk(shard_w), tok(d), (N_CHIPS, shard_w, d), row_out, (shard_w, d), N_CHIPS, "dw_conv_branch")
    dw_o = _wgrad(merged, dz, colblk(shard_w), tok(d), (N_CHIPS, shard_w, d), row_out, (shard_w, d), N_CHIPS, "dw_out")
    shard_grads = reduce_finish(sums_a, chips_a, names_a, "a", after=(dw_in, dw_o))
    grads_b = [dw_in, dw_ab, dw_cb, dw_o]
    sibling_b = pair_send(grads_b, "b", 8)

    dh2 = _in_proj_bwd(dq, dk, dv, drest, w_in_full, "in_proj_bwd")
    sums_b, chips_b = chip_send(grads_b, sibling_b, names_b, "b", 5, after=(dh2,))
    dx1, st2, dy1 = _norm_bwd(dh2, x1, mod2, dx2, z, 1.0, "norm2_bwd", after=tuple(sums_b), prev=(mod1, 0.5))
    dh1, dg1, du1, a1 = _ffn_bwd(dy1, g1, u1, wg1, wu1, wd1, "ffn1_bwd")
    dx0, st1 = _norm_bwd(dh1, xs, mod1, dx1, y1, 0.5, "norm1_bwd")
    grads_c = list(_ffn_wgrads(h1t, dg1, du1, a1, dy1, "ffn1"))
    sibling_c = pair_send(grads_c, "c", 9)
    shard_grads.update(reduce_finish(sums_b, chips_b, names_b, "b", after=tuple(grads_c)))

    dmod = jnp.concatenate([st1[0:3], st2[0:3], st3[0:3]], axis=0).reshape(1, 9 * d)
    loss_cols = jnp.zeros((1, HEAD_DIM), F32).at[0, 0].set(jnp.sum(loss_part))
    small = jnp.concatenate([dmod, st1[3:4], st2[3:4], st3[3:4], st_qk[0:1], st_qk[1:2],
                             st_conv[0:3].reshape(1, 3 * d), loss_cols], axis=1)
    small_all = _allgather8(jnp.broadcast_to(small, (8, small.shape[1])), "gather_small").reshape(8, 8, -1)[:, 0]
    loss = (0.5 / d) * jnp.sum(small_all[:, -HEAD_DIM])
    small_all = small_all[:, :-HEAD_DIM]
    dmod_all = small_all[:, :9 * d]
    dmod_shard = lax.dynamic_slice(dmod_all, (0, chip * ada_cols), (8, ada_cols))
    g_w_ada, d_w_ada, nm_w_ada, nv_w_ada = _ada_bwd(c_all, dmod_shard, w_ada[0], m_w_ada[0], v_w_ada[0], "ada_bwd")

    vec_names = ["b_ada", "norm_ffn1", "norm_mix", "norm_ffn2", "q_norm", "k_norm"]
    vec_w = [b_ada, norm_ffn1, norm_mix, norm_ffn2, q_norm, k_norm]
    vec_m = [m_b_ada, m_norm_ffn1, m_norm_mix, m_norm_ffn2, m_q_norm, m_k_norm]
    vec_v = [v_b_ada, v_norm_ffn1, v_norm_mix, v_norm_ffn2, v_q_norm, v_k_norm]
    n_vec = sum(w.shape[1] for w in vec_w)
    cat = lambda arrs: jnp.concatenate(arrs, axis=1)
    vec_out = _small_update(small_all[:, :n_vec], cat(vec_w), cat(vec_m), cat(vec_v), "small_update")
    conv_parts = small_all[:, n_vec:].reshape(8, 3, N_CHIPS * conv_cols)
    conv_parts = lax.dynamic_slice(conv_parts, (0, 0, chip * conv_cols), (8, 3, conv_cols)).reshape(8, 3 * conv_cols)
    flat3 = lambda w: w[0].reshape(1, 3 * conv_cols)
    conv_out = _small_update(conv_parts, flat3(conv_w), flat3(m_conv_w), flat3(v_conv_w), "conv_update")

    res = {"w_ada": [t[None] for t in (g_w_ada, d_w_ada, nm_w_ada, nv_w_ada)],
           "conv_w": [t.reshape(1, 3, conv_cols) for t in conv_out]}
    off = 0
    for nm, w in zip(vec_names, vec_w):
        width = w.shape[1]
        res[nm] = [t[:, off:off + width] for t in vec_out]
        off += width
    big = {"ffn1_w_gate": (ffn1_w_gate, m_ffn1_w_gate, v_ffn1_w_gate), "ffn1_w_up": (ffn1_w_up, m_ffn1_w_up, v_ffn1_w_up),
           "ffn1_w_down": (ffn1_w_down, m_ffn1_w_down, v_ffn1_w_down), "w_in": (w_in, m_w_in, v_w_in),
           "w_attn_branch": (w_attn_branch, m_w_attn_branch, v_w_attn_branch),
           "w_conv_branch": (w_conv_branch, m_w_conv_branch, v_w_conv_branch), "w_out": (w_out, m_w_out, v_w_out),
           "ffn2_w_gate": (ffn2_w_gate, m_ffn2_w_gate, v_ffn2_w_gate), "ffn2_w_up": (ffn2_w_up, m_ffn2_w_up, v_ffn2_w_up),
           "ffn2_w_down": (ffn2_w_down, m_ffn2_w_down, v_ffn2_w_down)}
    def update(nm, after=()):
        w, m, v = big[nm]
        g, delta, new_m, new_v = _adamw(w[0], shard_grads[nm], m[0], v[0], "adamw_" + nm, after)
        res[nm] = [t[None] for t in (g, delta, new_m, new_v)]
        return new_v

    last = tuple(shard_grads[nm] for nm in names_b)
    for nm in names_a:
        last = (update(nm, last),)
    sums_c, chips_c = chip_send(grads_c, sibling_c, names_c, "c", 6, after=last)
    last = tuple(sums_c)
    for nm in names_b:
        last = (update(nm, last),)
    shard_grads.update(reduce_finish(sums_c, chips_c, names_c, "c", after=last))
    for nm in names_c:
        update(nm)

    order = ["w_ada", "b_ada", "norm_ffn1", "ffn1_w_gate", "ffn1_w_up", "ffn1_w_down", "norm_mix", "w_in", "q_norm",
             "k_norm", "conv_w", "w_attn_branch", "w_conv_branch", "w_out", "norm_ffn2", "ffn2_w_gate", "ffn2_w_up",
             "ffn2_w_down"]
    return (loss, dx0[None], *[res[nm][0] for nm in order], *[res[nm][1] for nm in order],
            *[res[nm][2] for nm in order], *[res[nm][3] for nm in order])
```

```python
import jax
import jax.numpy as jnp
from jax import lax
from jax.experimental import pallas as pl
from jax.experimental.pallas import tpu as pltpu
from jax.experimental.pallas import tpu_sc as plsc

F32 = jnp.float32
BF16 = jnp.bfloat16
MESH = pl.DeviceIdType.MESH
ANY = pl.BlockSpec(memory_space=pl.ANY)

NORM_EPS = 1e-6
HEAD_DIM = 128
N_GROUPS = 3
HEADS = 4
DILATIONS = (1, 4, 16)
ATTN_BLOCK = 128
SLAB = ATTN_BLOCK * max(DILATIONS)
QKV = N_GROUPS * HEADS * HEAD_DIM
ATTN_SCALE = HEAD_DIM ** -0.5
NEG = -1e30
N_CHIPS = 4

ADAM_LR = 0.001
ADAM_B1 = 0.9
ADAM_B2 = 0.999
ADAM_EPS = 1e-08
ADAM_WD = 0.01
ADAM_STEP = 10

VMEM_LIMIT_BYTES = 56 * 1024 * 1024
TOKEN_TILE = 512
FFN_TILE = 1024
PROJ_TILE = 2048
WGRAD_TILE = 2048
IN_BLOCK = 512
MIX_TILE = 256
ACC_PIECES = 4
ADAMW_TILE_BYTES = 3 * 512 * 1024


def _params(n_axes=0):
    return pltpu.CompilerParams(
        dimension_semantics=("arbitrary",) * n_axes if n_axes else None,
        vmem_limit_bytes=VMEM_LIMIT_BYTES)


def _dot(a, b):
    return jnp.dot(a, b, preferred_element_type=F32)


def _dot_nt(a, b):
    return lax.dot_general(a, b, (((1,), (1,)), ((), ())), preferred_element_type=F32)


def _dot_tn(a, b):
    return lax.dot_general(a, b, (((0,), (0,)), ((), ())), preferred_element_type=F32)


def _sigmoid(x):
    return 1.0 / (1.0 + jnp.exp(-x))


def _place():
    return lax.axis_index("x"), lax.axis_index("y"), lax.axis_index("c")


def _ordered(body, n_in, after):
    if not after:
        return body
    return lambda *refs: body(*refs[:n_in], *refs[n_in + len(after):])


def _allgather8(block, name):
    m_per, n = block.shape

    def body(x_ref, out_ref, send_sems, recv_sems, local_sem):
        x, y, c = _place()
        me, sibling = (x, y, c), (x, y, 1 - c)
        chips = [(1 - x, y), (x, 1 - y), (1 - x, 1 - y)]

        def rows(px, py, pc):
            return out_ref.at[pl.ds((4 * px + 2 * py + pc) * m_per, m_per), :]

        def copy(k, blk, to, src=None):
            return pltpu.make_async_remote_copy(
                src_ref=rows(*blk) if src is None else src, dst_ref=rows(*blk),
                send_sem=send_sems.at[k], recv_sem=recv_sems.at[k],
                device_id=to, device_id_type=MESH)

        mine = pltpu.make_async_copy(x_ref, rows(*me), local_sem)
        mine.start()
        first = [copy(0, me, sibling, src=x_ref)]
        first += [copy(1 + j, me, (*chip, c), src=x_ref) for j, chip in enumerate(chips)]
        for cp in first:
            cp.start()
        passed = [copy(4 + j, (*chip, c), sibling) for j, chip in enumerate(chips)]
        for j, chip in enumerate(chips):
            copy(1 + j, (*chip, c), me).wait_recv()
            passed[j].start()
        copy(0, sibling, me).wait_recv()
        for j, chip in enumerate(chips):
            copy(4 + j, (*chip, 1 - c), me).wait_recv()
        for cp in first + passed:
            cp.wait_send()
        mine.wait()

    return pl.pallas_call(
        body, name=name,
        out_shape=jax.ShapeDtypeStruct((8 * m_per, n), block.dtype),
        in_specs=[pl.BlockSpec(memory_space=pltpu.VMEM)],
        out_specs=pl.BlockSpec(memory_space=pltpu.VMEM),
        scratch_shapes=[pltpu.SemaphoreType.DMA((7,)), pltpu.SemaphoreType.DMA((7,)),
                        pltpu.SemaphoreType.DMA],
        compiler_params=_params(),
    )(block)


def _handshake(peers):
    barrier = pltpu.get_barrier_semaphore()
    for peer in peers:
        pl.semaphore_signal(barrier, inc=1, device_id=peer, device_id_type=MESH)
    pl.semaphore_wait(barrier, len(peers))


def _gather_weights(shards, by_cols, name, collective_id, after=()):
    n_arr = len(shards)

    def body(*refs):
        srcs, outs = refs[:n_arr], refs[n_arr + len(after):2 * n_arr + len(after)]
        send_sems, recv_sems, local_sems = refs[2 * n_arr + len(after):]
        x, y, c = _place()
        me_dev, sibling = (x, y, c), (x, y, 1 - c)
        chips = [(1 - x, y), (x, 1 - y), (1 - x, 1 - y)]
        me = 2 * x + y
        _handshake([sibling] + [(*chip, c) for chip in chips])

        def place(k, chip_idx, rows):
            if by_cols[k]:
                width = srcs[k].shape[1]
                return outs[k].at[rows, pl.ds(pl.multiple_of(chip_idx * width, 128), width)]
            return outs[k].at[chip_idx, rows]

        def copy(k, slot, chip_idx, half_sel, to, from_shard=False):
            half = srcs[k].shape[0] // 2
            rows = pl.ds(half_sel * half, half)
            dst = place(k, chip_idx, rows)
            return pltpu.make_async_remote_copy(
                src_ref=srcs[k].at[rows] if from_shard else dst, dst_ref=dst,
                send_sem=send_sems.at[6 * k + slot], recv_sem=recv_sems.at[6 * k + slot],
                device_id=to, device_id_type=MESH)

        own = [pltpu.make_async_copy(srcs[k], place(k, me, pl.ds(0, srcs[k].shape[0])), local_sems.at[k])
               for k in range(n_arr)]
        for cp in own:
            cp.start()
        sent = []
        for k in range(n_arr):
            for j, chip in enumerate(chips):
                sent.append(copy(k, j, me, c, (*chip, c), from_shard=True))
                sent[-1].start()
        for k in range(n_arr):
            for j, chip in enumerate(chips):
                chip_idx = 2 * chip[0] + chip[1]
                copy(k, j, chip_idx, c, me_dev).wait_recv()
                sent.append(copy(k, 3 + j, chip_idx, c, sibling))
                sent[-1].start()
        for k in range(n_arr):
            for j, chip in enumerate(chips):
                copy(k, 3 + j, 2 * chip[0] + chip[1], 1 - c, me_dev).wait_recv()
        for cp in sent:
            cp.wait_send()
        for cp in own:
            cp.wait()

    def gathered(k):
        r, cols = shards[k].shape
        return (r, N_CHIPS * cols) if by_cols[k] else (N_CHIPS, r, cols)

    return pl.kernel(
        body, name=name,
        out_type=[jax.ShapeDtypeStruct(gathered(k), shards[k].dtype) for k in range(n_arr)],
        mesh=plsc.ScalarSubcoreMesh(axis_name="sequencer", num_cores=1),
        scratch_types=[pltpu.SemaphoreType.DMA((6 * n_arr,)), pltpu.SemaphoreType.DMA((6 * n_arr,)),
                       pltpu.SemaphoreType.DMA((n_arr,))],
        compiler_params=pltpu.CompilerParams(collective_id=collective_id),
    )(*shards, *after)


def _rs_pair_exchange(grads, name, collective_id):
    n_arr = len(grads)

    def body(*refs):
        srcs, outs = refs[:n_arr], refs[n_arr:2 * n_arr]
        send_sems, recv_sems = refs[2 * n_arr:]
        x, y, c = _place()
        _handshake([(x, y, 1 - c)])
        cps = []
        for k in range(n_arr):
            half = srcs[k].shape[1] // 2
            cps.append(pltpu.make_async_remote_copy(
                src_ref=srcs[k].at[:, pl.ds((1 - c) * half, half)], dst_ref=outs[k],
                send_sem=send_sems.at[k], recv_sem=recv_sems.at[k],
                device_id=(x, y, 1 - c), device_id_type=MESH))
            cps[-1].start()
        for cp in cps:
            cp.wait_recv()
        for cp in cps:
            cp.wait_send()

    return pl.kernel(
        body, name=name,
        out_type=[jax.ShapeDtypeStruct((g.shape[0], g.shape[1] // 2, g.shape[2]), g.dtype) for g in grads],
        mesh=plsc.ScalarSubcoreMesh(axis_name="sequencer", num_cores=1),
        scratch_types=[pltpu.SemaphoreType.DMA((n_arr,)), pltpu.SemaphoreType.DMA((n_arr,))],
        compiler_params=pltpu.CompilerParams(collective_id=collective_id),
    )(*grads)


def _rs_chip_exchange(sums, name, collective_id):
    n_arr = len(sums)

    def body(*refs):
        srcs, outs = refs[:n_arr], refs[n_arr:2 * n_arr]
        send_sems, recv_sems = refs[2 * n_arr:]
        x, y, c = _place()
        chips = [(1 - x, y), (x, 1 - y), (1 - x, 1 - y)]
        _handshake([(*chip, c) for chip in chips])
        cps = []
        for k in range(n_arr):
            for j, chip in enumerate(chips):
                cps.append(pltpu.make_async_remote_copy(
                    src_ref=srcs[k].at[2 * chip[0] + chip[1]], dst_ref=outs[k].at[j],
                    send_sem=send_sems.at[3 * k + j], recv_sem=recv_sems.at[3 * k + j],
                    device_id=(*chip, c), device_id_type=MESH))
                cps[-1].start()
        for cp in cps:
            cp.wait_recv()
        for cp in cps:
            cp.wait_send()

    return pl.kernel(
        body, name=name,
        out_type=[jax.ShapeDtypeStruct((3,) + s.shape[1:], s.dtype) for s in sums],
        mesh=plsc.ScalarSubcoreMesh(axis_name="sequencer", num_cores=1),
        scratch_types=[pltpu.SemaphoreType.DMA((3 * n_arr,)), pltpu.SemaphoreType.DMA((3 * n_arr,))],
        compiler_params=pltpu.CompilerParams(collective_id=collective_id),
    )(*sums)


def _rs_share(totals, name):
    n_arr = len(totals)

    def body(*refs):
        outs = refs[n_arr:2 * n_arr]
        send_sems, recv_sems = refs[2 * n_arr:]
        x, y, c = _place()

        def half_rows(k, sel):
            return outs[k].at[sel]

        cps = []
        for k in range(n_arr):
            cps.append(pltpu.make_async_remote_copy(
                src_ref=half_rows(k, c), dst_ref=half_rows(k, c), send_sem=send_sems.at[k], recv_sem=recv_sems.at[k],
                device_id=(x, y, 1 - c), device_id_type=MESH))
            cps[-1].start()
        for k in range(n_arr):
            pltpu.make_async_remote_copy(
                src_ref=half_rows(k, c), dst_ref=half_rows(k, 1 - c), send_sem=send_sems.at[k],
                recv_sem=recv_sems.at[k], device_id=(x, y, 1 - c), device_id_type=MESH).wait_recv()
        for cp in cps:
            cp.wait_send()

    shared = pl.pallas_call(
        body, name=name,
        out_shape=[jax.ShapeDtypeStruct(t.shape, t.dtype) for t in totals],
        in_specs=[ANY] * n_arr, out_specs=[ANY] * n_arr,
        input_output_aliases={k: k for k in range(n_arr)},
        scratch_shapes=[pltpu.SemaphoreType.DMA((n_arr,)), pltpu.SemaphoreType.DMA((n_arr,))],
        compiler_params=_params(),
    )(*totals)
    return [t.reshape(2 * t.shape[1], t.shape[2]) for t in shared]


def _pair_add(grad, recv, c_idx, name, after=()):
    n, r, cols = grad.shape
    half = r // 2
    rows = half // 2

    def body(_, g_ref, r_ref, o_ref):
        o_ref[...] = (g_ref[...].astype(F32) + r_ref[...].astype(F32)).astype(o_ref.dtype)

    return pl.pallas_call(
        _ordered(body, 3, after), name=name,
        grid_spec=pltpu.PrefetchScalarGridSpec(
            num_scalar_prefetch=1, grid=(n, 2),
            in_specs=[pl.BlockSpec((None, None, rows, cols), lambda s, i, ci: (s, ci[0], i, 0)),
                      pl.BlockSpec((None, rows, cols), lambda s, i, ci: (s, i, 0))] + [ANY] * len(after),
            out_specs=pl.BlockSpec((None, rows, cols), lambda s, i, ci: (s, i, 0))),
        out_shape=jax.ShapeDtypeStruct((n, half, cols), BF16),
        compiler_params=_params(2),
    )(c_idx, grad.reshape(n, 2, half, cols), recv, *after)


def _chip_add(sums, recv, chip_and_core, name, after=()):
    _, half, cols = sums.shape
    rows = half // 2

    def body(_, s_ref, r0_ref, r1_ref, r2_ref, o_ref):
        o_ref[...] = ((s_ref[...].astype(F32) + r0_ref[...].astype(F32))
                      + r1_ref[...].astype(F32)) + r2_ref[...].astype(F32)

    def recv_spec(j):
        return pl.BlockSpec((None, rows, cols), lambda i, ci: (j, i, 0))

    return pl.pallas_call(
        _ordered(body, 5, after), name=name,
        grid_spec=pltpu.PrefetchScalarGridSpec(
            num_scalar_prefetch=1, grid=(2,),
            in_specs=[pl.BlockSpec((None, rows, cols), lambda i, ci: (ci[0], i, 0)),
                      recv_spec(0), recv_spec(1), recv_spec(2)] + [ANY] * len(after),
            out_specs=pl.BlockSpec((None, rows, cols), lambda i, ci: (ci[1], i, 0))),
        out_shape=jax.ShapeDtypeStruct((2, half, cols), F32),
        compiler_params=_params(1),
    )(chip_and_core, sums, recv, recv, recv, *after)


def _rms(x):
    return lax.rsqrt(jnp.mean(x * x, axis=-1, keepdims=True) + NORM_EPS)


def _norm_mod(x, mod, name, prev=None):
    s_len, d = x.shape
    tm = TOKEN_TILE

    def body(*refs):
        if prev is None:
            x_ref, mod_ref, h_ref, ht_ref = refs
            xv = x_ref[...]
        else:
            x_ref, y_ref, modp_ref, mod_ref, xo_ref, h_ref, ht_ref = refs
            xv = x_ref[...] + prev[2] * modp_ref[2:3, :] * y_ref[...]
            xo_ref[...] = xv
        n = (xv * _rms(xv)) * mod_ref[3:4, :]
        h = n * (1.0 + mod_ref[1:2, :]) + mod_ref[0:1, :]
        h_ref[...] = h.astype(BF16)
        ht_ref[...] = h.T.astype(BF16)

    tile = pl.BlockSpec((tm, d), lambda i: (i, 0))
    small = pl.BlockSpec((8, d), lambda i: (0, 0))
    h_specs = [tile, pl.BlockSpec((d, tm), lambda i: (0, i))]
    h_shapes = [jax.ShapeDtypeStruct((s_len, d), BF16), jax.ShapeDtypeStruct((d, s_len), BF16)]
    if prev is None:
        return pl.pallas_call(
            body, name=name, grid=(s_len // tm,), in_specs=[tile, small], out_specs=h_specs, out_shape=h_shapes,
            compiler_params=_params(1))(x, mod)
    return pl.pallas_call(
        body, name=name, grid=(s_len // tm,), in_specs=[tile, tile, small, small],
        out_specs=[tile] + h_specs, out_shape=[jax.ShapeDtypeStruct((s_len, d), F32)] + h_shapes,
        compiler_params=_params(1))(x, prev[0], prev[1], mod)


def _norm_bwd(dh, x, mod, dxo, y_raw, coef, name, after=(), prev=None):
    s_len, d = x.shape
    tm = TOKEN_TILE

    def body(*refs):
        if prev is None:
            dh_ref, x_ref, mod_ref, dxo_ref, y_ref, dx_ref, st_ref = refs
        else:
            dh_ref, x_ref, mod_ref, dxo_ref, y_ref, modp_ref, dx_ref, st_ref, dyp_ref = refs

        @pl.when(pl.program_id(0) == 0)
        def _():
            st_ref[...] = jnp.zeros_like(st_ref)

        xv, dhv, dxov = x_ref[...], dh_ref[...], dxo_ref[...]
        r = _rms(xv)
        xh = xv * r
        gain, scale = mod_ref[3:4, :], mod_ref[1:2, :]
        dn = dhv * (1.0 + scale)
        dxh = dn * gain
        dx = dxov + r * (dxh - xh * jnp.mean(dxh * xh, axis=-1, keepdims=True))
        dx_ref[...] = dx
        if prev is not None:
            dyp_ref[...] = (prev[1] * modp_ref[2:3, :] * dx).astype(BF16)
        st_ref[0:1, :] += jnp.sum(dhv, axis=0, keepdims=True)
        st_ref[1:2, :] += jnp.sum(dhv * (xh * gain), axis=0, keepdims=True)
        st_ref[2:3, :] += coef * jnp.sum(y_ref[...].astype(F32) * dxov, axis=0, keepdims=True)
        st_ref[3:4, :] += jnp.sum(dn * xh, axis=0, keepdims=True)

    tile = pl.BlockSpec((tm, d), lambda i: (i, 0))
    small = pl.BlockSpec((8, d), lambda i: (0, 0))
    operands = [dh, x, mod, dxo, y_raw] + ([] if prev is None else [prev[0]])
    in_specs = [tile, tile, small, tile, tile] + ([] if prev is None else [small])
    out_specs = [tile, small] + ([] if prev is None else [tile])
    out_shape = [jax.ShapeDtypeStruct((s_len, d), F32), jax.ShapeDtypeStruct((8, d), F32)]
    if prev is not None:
        out_shape.append(jax.ShapeDtypeStruct((s_len, d), BF16))
    return pl.pallas_call(
        _ordered(body, len(operands), after), name=name, grid=(s_len // tm,),
        in_specs=in_specs + [ANY] * len(after), out_specs=out_specs, out_shape=out_shape,
        compiler_params=_params(1),
    )(*operands, *after)


def _loss_grad(x, y, mod, target, name):
    s_len, d = x.shape
    tm = TOKEN_TILE

    def body(x_ref, y_ref, mod_ref, t_ref, do_ref, dy_ref, part_ref):
        @pl.when(pl.program_id(0) == 0)
        def _():
            part_ref[...] = jnp.zeros_like(part_ref)

        half_gate = 0.5 * mod_ref[2:3, :]
        err = (x_ref[...] + half_gate * y_ref[...]) - t_ref[...]
        do = err * (1.0 / d)
        do_ref[...] = do
        dy_ref[...] = (half_gate * do).astype(BF16)
        sq = err * err
        part_ref[...] += jnp.sum(sq.reshape(tm // 8, 8, d), axis=0)

    tile = pl.BlockSpec((tm, d), lambda i: (i, 0))
    small = pl.BlockSpec((8, d), lambda i: (0, 0))
    return pl.pallas_call(
        body, name=name, grid=(s_len // tm,),
        in_specs=[tile, tile, small, tile],
        out_specs=[tile, tile, small],
        out_shape=[jax.ShapeDtypeStruct((s_len, d), F32), jax.ShapeDtypeStruct((s_len, d), BF16),
                   jax.ShapeDtypeStruct((8, d), F32)],
        compiler_params=_params(1),
    )(x, y, mod, target)


def _adamw_math(w, g, m, v):
    m = ADAM_B1 * m + (1.0 - ADAM_B1) * g
    v = ADAM_B2 * v + (1.0 - ADAM_B2) * (g * g)
    m_hat = m / (1.0 - ADAM_B1 ** ADAM_STEP)
    v_hat = v / (1.0 - ADAM_B2 ** ADAM_STEP)
    delta = -ADAM_LR * (m_hat / (jnp.sqrt(v_hat) + ADAM_EPS) + ADAM_WD * w)
    return delta, m, v


def _adamw(w, g, m, v, name, after=()):
    r, cols = w.shape
    tr = max([t for t in (r // k for k in (1, 2, 4, 8, 16)) if t % 8 == 0 and r % t == 0
              and t * cols * 4 <= ADAMW_TILE_BYTES] or [r])

    def body(w_ref, g_ref, m_ref, v_ref, go_ref, d_ref, nm_ref, nv_ref):
        gv = g_ref[...]
        go_ref[...] = gv
        d_ref[...], nm_ref[...], nv_ref[...] = _adamw_math(w_ref[...], gv, m_ref[...], v_ref[...])

    tile = pl.BlockSpec((tr, cols), lambda i: (i, 0))
    shape = jax.ShapeDtypeStruct((r, cols), F32)
    return pl.pallas_call(
        _ordered(body, 4, after), name=name, grid=(r // tr,),
        in_specs=[tile] * 4 + [ANY] * len(after), out_specs=[tile] * 4, out_shape=[shape] * 4,
        compiler_params=_params(1),
    )(w, g, m, v, *after)


def _in_parts(tm, n_qkv, n_rest):
    def part(lo, n_blk):
        return pl.BlockSpec((tm, IN_BLOCK), lambda i, j: (i, jnp.clip(j - lo, 0, n_blk - 1)))
    return [part(0, n_qkv), part(n_qkv, n_qkv), part(2 * n_qkv, n_qkv), part(3 * n_qkv, n_rest)]


def _pick_part(j, n_qkv, refs, fn):
    bounds = [0, n_qkv, 2 * n_qkv, 3 * n_qkv]
    for p, ref in enumerate(refs):
        inside = j >= bounds[p]
        if p + 1 < len(refs):
            inside = inside & (j < bounds[p + 1])
        pl.when(inside)(lambda ref=ref: fn(ref))


def _rows(base, count, stride):
    return pl.ds(base, count) if stride == 1 else pl.ds(base, count, stride=stride)


REORDER_STRIDE = 4


def _reorder_plan(dil, parts=1):
    inner = min(dil, REORDER_STRIDE)
    return inner, dil // inner, SLAB // parts // inner, SLAB // dil


def _to_residue_order(dst, src, dil, tmp, part=0, parts=1):
    inner, outer, big, seg = _reorder_plan(dil, parts)
    piece = seg // parts
    if outer == 1:
        for r in range(dil):
            dst[pl.ds(r * seg + part * piece, piece), :] = src[_rows(r, piece, dil), :].astype(dst.dtype)
        return
    for b in range(inner):
        tmp[pl.ds(b * big, big), :] = src[_rows(b, big, inner), :]
    for a in range(outer):
        for b in range(inner):
            dst[pl.ds((inner * a + b) * seg + part * piece, piece), :] = (
                tmp[_rows(b * big + a, piece, outer), :].astype(dst.dtype))


def _to_token_order(dst, src, dil, tmp):
    inner, outer, big, seg = _reorder_plan(dil)
    if outer == 1:
        for r in range(dil):
            dst[_rows(r, seg, dil), :] = src[pl.ds(r * seg, seg), :]
        return
    for a in range(outer):
        for b in range(inner):
            tmp[_rows(b * big + a, seg, outer), :] = src[pl.ds((inner * a + b) * seg, seg), :]
    for b in range(inner):
        dst[_rows(b, big, inner), :] = tmp[pl.ds(b * big, big), :]


def _in_proj(h, w, q_norm, k_norm, name):
    s_len, d = h.shape
    tm = PROJ_TILE
    assert tm == SLAB and IN_BLOCK == HEADS * HEAD_DIM
    steps = w.shape[1] // IN_BLOCK
    n_qkv = 3 * QKV // IN_BLOCK
    parts = 4
    rows = [pl.ds(p * (tm // parts), tm // parts) for p in range(parts)]

    gains = jnp.concatenate([q_norm, k_norm, jnp.ones((6, HEAD_DIM), F32)], axis=0)

    def body(h_ref, w_ref, gains_ref, qkv_ref, rest_ref, hat_ref, tok_s, tmp_s):
        j = pl.program_id(1)
        sect = j // N_GROUPS
        multiply = lambda p: _dot(h_ref[rows[p], :], w_ref[...])

        def emit(gi):
            dil = DILATIONS[gi]
            res = [multiply(p) for p in range(parts)]
            gain = gains_ref[pl.ds(sect, 1), :]
            plain = sect == 2
            for p in range(parts):
                qkv_ref[rows[p], :] = res[p]
                for hh in range(HEADS):
                    cols = slice(hh * HEAD_DIM, (hh + 1) * HEAD_DIM)
                    x = res[p][:, cols]
                    tok_s[...] = (x * jnp.where(plain, 1.0, _rms(x))) * gain
                    _to_residue_order(hat_ref.at[:, cols], tok_s, dil, tmp_s, p, parts)

        for gi in range(N_GROUPS):
            pl.when((j < n_qkv) & (j % N_GROUPS == gi))(lambda gi=gi: emit(gi))

        @pl.when(j >= n_qkv)
        def _():
            for p in range(parts):
                rest_ref[rows[p], :] = multiply(p).astype(BF16)

    qkv_blk = pl.BlockSpec((tm, IN_BLOCK), lambda i, j: (i, jnp.minimum(j, n_qkv - 1)))
    return pl.pallas_call(
        body, name=name, grid=(s_len // tm, steps),
        in_specs=[pl.BlockSpec((tm, d), lambda i, j: (i, 0)), pl.BlockSpec((d, IN_BLOCK), lambda i, j: (0, j)),
                  pl.BlockSpec((8, HEAD_DIM), lambda i, j: (0, 0))],
        out_specs=[qkv_blk, pl.BlockSpec((tm, IN_BLOCK), lambda i, j: (i, jnp.maximum(j - n_qkv, 0))), qkv_blk],
        out_shape=[jax.ShapeDtypeStruct((s_len, 3 * QKV), F32),
                   jax.ShapeDtypeStruct((s_len, w.shape[1] - 3 * QKV), BF16),
                   jax.ShapeDtypeStruct((s_len, 3 * QKV), BF16)],
        scratch_shapes=[pltpu.VMEM((tm // parts, HEAD_DIM), F32)] * 2,
        compiler_params=_params(2),
    )(h, w, gains)


def _in_proj_bwd(dq, dk, dv, drest, w, name, after=()):
    s_len = dq.shape[0]
    d = w.shape[0]
    tm = PROJ_TILE
    steps = w.shape[1] // IN_BLOCK
    n_qkv = QKV // IN_BLOCK

    def body(dq_ref, dk_ref, dv_ref, dr_ref, w_ref, o_ref, acc_ref):
        j = pl.program_id(1)

        @pl.when(j == 0)
        def _():
            acc_ref[...] = jnp.zeros_like(acc_ref)

        def add(a_ref):
            rows = [pl.ds(p * (tm // ACC_PIECES), tm // ACC_PIECES) for p in range(ACC_PIECES)]
            products = [_dot_nt(a_ref[r, :], w_ref[...]) for r in rows]
            for r, product in zip(rows, products):
                acc_ref[r, :] += product

        _pick_part(j, n_qkv, [dq_ref, dk_ref, dv_ref, dr_ref], add)

        @pl.when(j == steps - 1)
        def _():
            o_ref[...] = acc_ref[...]

    return pl.pallas_call(
        _ordered(body, 5, after), name=name, grid=(s_len // tm, steps),
        in_specs=(_in_parts(tm, n_qkv, steps - 3 * n_qkv) + [pl.BlockSpec((d, IN_BLOCK), lambda i, j: (0, j))]
                  + [ANY] * len(after)),
        out_specs=pl.BlockSpec((tm, d), lambda i, j: (i, 0)),
        out_shape=jax.ShapeDtypeStruct((s_len, d), F32),
        scratch_shapes=[pltpu.VMEM((tm, d), F32)],
        compiler_params=_params(2),
    )(dq, dk, dv, drest, w, *after)


def _wgrad(x, y, x_spec, y_spec, out_shape, out_spec, acc_shape, n_chunks, name, x_transposed=False, after=()):
    s_len = y.shape[-2]
    ts = WGRAD_TILE
    steps = s_len // ts

    def body(x_ref, y_ref, o_ref, acc_ref):
        s = pl.program_id(1)

        @pl.when(s == 0)
        def _():
            acc_ref[...] = jnp.zeros_like(acc_ref)

        if x_transposed:
            n_rows = acc_shape[0]
            rows = [pl.ds(p * (n_rows // ACC_PIECES), n_rows // ACC_PIECES) for p in range(ACC_PIECES)]
            products = [_dot(x_ref[r, :], y_ref[...]) for r in rows]
            for r, product in zip(rows, products):
                acc_ref[r, :] += product
        else:
            cols = _pieces(acc_shape[1])
            products = [_dot_tn(x_ref[...], y_ref[:, c]) for c in cols]
            for c, product in zip(cols, products):
                acc_ref[:, c] += product

        @pl.when(s == steps - 1)
        def _():
            o_ref[...] = acc_ref[...].astype(o_ref.dtype)

    return pl.pallas_call(
        _ordered(body, 2, after), name=name, grid=(n_chunks, steps),
        in_specs=[x_spec(ts), y_spec(ts)] + [ANY] * len(after), out_specs=out_spec,
        out_shape=jax.ShapeDtypeStruct(out_shape, BF16),
        scratch_shapes=[pltpu.VMEM(acc_shape, F32)],
        compiler_params=_params(2),
    )(x, y, *after)


def _pieces(width, piece=256):
    return [slice(a, min(a + piece, width)) for a in range(0, width, piece)]


def _ffn_fwd(h, w_gate, w_up, w_down, name):
    s_len, d = h.shape
    n_chunks, _, fs = w_gate.shape
    tm = FFN_TILE

    def body(h_ref, wg_ref, wu_ref, wd_ref, g_ref, u_ref, y_ref):
        j = pl.program_id(1)
        hv = h_ref[...]
        pieces = _pieces(fs)
        first = lambda cols: (_dot(hv, wg_ref[:, cols]), _dot(hv, wu_ref[:, cols]))
        total = None
        ahead = first(pieces[0])
        for k, cols in enumerate(pieces):
            g, u = ahead
            if k + 1 < len(pieces):
                ahead = first(pieces[k + 1])
            g_ref[:, cols] = g.astype(BF16)
            u_ref[:, cols] = u.astype(BF16)
            act = (g * _sigmoid(g)) * u
            part = _dot(act.astype(BF16), wd_ref[cols, :])
            total = part if total is None else total + part

        @pl.when(j == 0)
        def _():
            y_ref[...] = total

        @pl.when(j > 0)
        def _():
            y_ref[...] += total

    tile = pl.BlockSpec((tm, d), lambda i, j: (i, 0))
    hid = pl.BlockSpec((None, tm, fs), lambda i, j: (j, i, 0))
    w_in_spec = pl.BlockSpec((None, d, fs), lambda i, j: (j, 0, 0))
    hid_shape = jax.ShapeDtypeStruct((n_chunks, s_len, fs), BF16)
    return pl.pallas_call(
        body, name=name, grid=(s_len // tm, n_chunks),
        in_specs=[tile, w_in_spec, w_in_spec, pl.BlockSpec((None, fs, d), lambda i, j: (j, 0, 0))],
        out_specs=[hid, hid, tile],
        out_shape=[hid_shape, hid_shape, jax.ShapeDtypeStruct((s_len, d), F32)],
        compiler_params=_params(2),
    )(h, w_gate, w_up, w_down)


def _ffn_bwd(dy, g_pre, u_pre, w_gate, w_up, w_down, name):
    s_len, d = dy.shape
    n_chunks, _, fs = w_gate.shape
    tm = FFN_TILE

    def body(dy_ref, g_ref, u_ref, wg_ref, wu_ref, wd_ref, dh_ref, dg_ref, du_ref, a_ref):
        j = pl.program_id(1)
        dyv = dy_ref[...]
        pieces = _pieces(fs)
        first = lambda cols: _dot_nt(dyv, wd_ref[cols, :])
        total = None
        ahead = first(pieces[0])
        for k, cols in enumerate(pieces):
            da = ahead
            if k + 1 < len(pieces):
                ahead = first(pieces[k + 1])
            g = g_ref[:, cols].astype(F32)
            u = u_ref[:, cols].astype(F32)
            sg = _sigmoid(g)
            silu = g * sg
            dg = (da * u * (sg * (1.0 + g * (1.0 - sg)))).astype(BF16)
            du = (da * silu).astype(BF16)
            dg_ref[:, cols] = dg
            du_ref[:, cols] = du
            a_ref[:, cols] = (silu * u).astype(BF16)
            part = _dot_nt(dg, wg_ref[:, cols]) + _dot_nt(du, wu_ref[:, cols])
            total = part if total is None else total + part

        @pl.when(j == 0)
        def _():
            dh_ref[...] = total

        @pl.when(j > 0)
        def _():
            dh_ref[...] += total

    tile = pl.BlockSpec((tm, d), lambda i, j: (i, 0))
    hid = pl.BlockSpec((None, tm, fs), lambda i, j: (j, i, 0))
    w_in_spec = pl.BlockSpec((None, d, fs), lambda i, j: (j, 0, 0))
    hid_shape = jax.ShapeDtypeStruct((n_chunks, s_len, fs), BF16)
    return pl.pallas_call(
        body, name=name, grid=(s_len // tm, n_chunks),
        in_specs=[tile, hid, hid, w_in_spec, w_in_spec, pl.BlockSpec((None, fs, d), lambda i, j: (j, 0, 0))],
        out_specs=[tile, hid, hid, hid],
        out_shape=[jax.ShapeDtypeStruct((s_len, d), F32), hid_shape, hid_shape, hid_shape],
        compiler_params=_params(2),
    )(dy, g_pre, u_pre, w_gate, w_up, w_down)


def _ffn_wgrads(ht, dg, du, act, dy, tag, after=()):
    n_chunks, s_len, fs = dg.shape
    d = ht.shape[0]
    tok = lambda ts: pl.BlockSpec((ts, d), lambda c, s: (s, 0))
    tok_t = lambda ts: pl.BlockSpec((d, ts), lambda c, s: (0, s))
    hid = lambda ts: pl.BlockSpec((None, ts, fs), lambda c, s: (c, s, 0))
    d_up = pl.BlockSpec((None, d, fs), lambda c, s: (c, 0, 0))
    d_down = pl.BlockSpec((None, fs, d), lambda c, s: (c, 0, 0))
    dwg = _wgrad(ht, dg, tok_t, hid, (n_chunks, d, fs), d_up, (d, fs), n_chunks, tag + "_dwg", True, after)
    dwu = _wgrad(ht, du, tok_t, hid, (n_chunks, d, fs), d_up, (d, fs), n_chunks, tag + "_dwu", True, after)
    dwd = _wgrad(act, dy, hid, tok, (n_chunks, fs, d), d_down, (fs, d), n_chunks, tag + "_dwd", False, after)
    return dwg, dwu, dwd


def _band_bias():
    qi = lax.broadcasted_iota(jnp.int32, (ATTN_BLOCK, 2 * ATTN_BLOCK), 0)
    kj = lax.broadcasted_iota(jnp.int32, (ATTN_BLOCK, 2 * ATTN_BLOCK), 1)
    band = (kj >= qi) & (kj <= qi + ATTN_BLOCK)
    return jnp.where(band, 0.0, NEG), jnp.where(band & (kj >= ATTN_BLOCK), 0.0, NEG)


def _qkv_specs(slab_of, sections):
    def spec(sect, back):
        return pl.BlockSpec((SLAB, HEAD_DIM),
                            lambda h, s, g: (jnp.maximum(slab_of(s) - back, 0), (sect * N_GROUPS + g) * HEADS + h))
    return [spec(sect, back) for sect, back in sections]


HAT_BLOCKS = [(0, 0), (1, 0), (2, 0), (1, 1), (2, 1)]


def _stage_keys(k_ref, v_ref, kp_ref, vp_ref, kbuf, vbuf, dil, n):
    run = SLAB // dil
    for r in range(dil):
        own, before = pl.ds(r * run, run), pl.ds(2 * r * run, run)
        kbuf[pl.ds((2 * r + 1) * run, run), :] = k_ref[own, :]
        vbuf[pl.ds((2 * r + 1) * run, run), :] = v_ref[own, :]

        @pl.when(n > 0)
        def _():
            kbuf[before, :] = kp_ref[own, :]
            vbuf[before, :] = vp_ref[own, :]

        @pl.when(n == 0)
        def _():
            kbuf[before, :] = jnp.zeros((run, HEAD_DIM), BF16)
            vbuf[before, :] = jnp.zeros((run, HEAD_DIM), BF16)


def _for_each_tile(dil, n, first_fn, rest_fn):
    run = SLAB // dil
    bias, first_bias = _band_bias()
    tiles = []
    for jj in range(run // ATTN_BLOCK):
        start = jj * ATTN_BLOCK
        tile_bias = jnp.where(n == 0, first_bias, bias) if jj == 0 else bias
        for r in range(dil):
            tiles.append((pl.ds(r * run + start, ATTN_BLOCK),
                          pl.ds((2 * r + 1) * run - ATTN_BLOCK + start, 2 * ATTN_BLOCK), tile_bias))
    ahead = first_fn(*tiles[0])
    for t, tile in enumerate(tiles):
        begun = ahead
        if t + 1 < len(tiles):
            ahead = first_fn(*tiles[t + 1])
        rest_fn(*tile, begun)


def _attn_fwd(hat, name):
    s_len = hat.shape[0]
    e = HEAD_DIM
    n_slabs = s_len // SLAB

    def body(q_ref, k_ref, v_ref, kp_ref, vp_ref, o_ref, lse_ref, kbuf, vbuf, m_s, l_s, acc_s, m_p, l_p, acc_p, tmp_s):
        n, grp = pl.program_id(1), pl.program_id(2)

        def run(gi, dil):
            _stage_keys(k_ref, v_ref, kp_ref, vp_ref, kbuf, vbuf, dil, n)

            def scores(q_rows, kv_rows, bias):
                return _dot_nt(q_ref[q_rows, :], kbuf[kv_rows, :])

            def rest(q_rows, kv_rows, bias, qk):
                s = qk * ATTN_SCALE + bias
                m = jnp.max(s, axis=-1, keepdims=True)
                p = jnp.exp(s - m)
                m_p[q_rows, :] = jnp.broadcast_to(m, (ATTN_BLOCK, e))
                l_p[q_rows, :] = jnp.broadcast_to(jnp.sum(p, axis=-1, keepdims=True), (ATTN_BLOCK, e))
                acc_p[q_rows, :] = _dot(p.astype(BF16), vbuf[kv_rows, :])

            _for_each_tile(dil, n, scores, rest)
            _to_token_order(m_s.at[gi], m_p, dil, tmp_s)
            _to_token_order(l_s.at[gi], l_p, dil, tmp_s)
            _to_token_order(acc_s.at[gi], acc_p, dil, tmp_s)

        for gi, dil in enumerate(DILATIONS):
            pl.when(grp == gi)(lambda gi=gi, dil=dil: run(gi, dil))

        @pl.when(grp == N_GROUPS - 1)
        def _():
            m_all = jnp.maximum(jnp.maximum(m_s[0], m_s[1]), m_s[2])
            den = jnp.zeros((SLAB, e), F32)
            num = jnp.zeros((SLAB, e), F32)
            for gi in range(N_GROUPS):
                w = jnp.exp(m_s[gi] - m_all)
                den += l_s[gi] * w
                num += acc_s[gi] * w
            o_ref[...] = (num / den).astype(BF16)
            lse_ref[...] = m_all + jnp.log(den)

    out = pl.BlockSpec((SLAB, e), lambda h, n, g: (n, h))
    return pl.pallas_call(
        body, name=name, grid=(HEADS, n_slabs, N_GROUPS),
        in_specs=_qkv_specs(lambda n: n, HAT_BLOCKS),
        out_specs=[out, out],
        out_shape=[jax.ShapeDtypeStruct((s_len, HEADS * e), BF16), jax.ShapeDtypeStruct((s_len, HEADS * e), F32)],
        scratch_shapes=[pltpu.VMEM((2 * SLAB, e), BF16), pltpu.VMEM((2 * SLAB, e), BF16),
                        pltpu.VMEM((N_GROUPS, SLAB, e), F32), pltpu.VMEM((N_GROUPS, SLAB, e), F32),
                        pltpu.VMEM((N_GROUPS, SLAB, e), F32)]
        + [pltpu.VMEM((SLAB, e), F32)] * 4,
        compiler_params=_params(3),
    )(hat, hat, hat, hat, hat)


def _attn_bwd(qkv, hat, d_out, out, lse, q_norm, k_norm, name):
    s_len = qkv.shape[0]
    e = HEAD_DIM
    n_slabs = s_len // SLAB

    def body(q_ref, k_ref, v_ref, kp_ref, vp_ref, qraw_ref, kraw_ref, do_ref, o_ref, lse_ref, qn_ref, kn_ref,
             dq_ref, dk_ref, dv_ref, st_ref, kbuf, vbuf, stat_s, dqs, dkb, dvb, dk_tok, dv_tok, carry,
             do_p, stat_p, dq_p, dk_p, dv_p, tmp_s, do16_p):
        head, step, grp = pl.program_id(0), pl.program_id(1), pl.program_id(2)
        n = n_slabs - 1 - step
        dkb[...] = jnp.zeros_like(dkb)
        dvb[...] = jnp.zeros_like(dvb)
        @pl.when(grp == 0)
        def _():
            lane = lax.broadcasted_iota(jnp.int32, (SLAB, e), 1)
            stat_s[...] = jnp.where(lane < e // 2, lse_ref[...],
                                    jnp.sum(do_ref[...] * o_ref[...].astype(F32), axis=-1, keepdims=True))

        @pl.when((head == 0) & (step == 0) & (grp == 0))
        def _():
            st_ref[...] = jnp.zeros_like(st_ref)

        def run(gi, dil):
            seg = SLAB // dil
            _stage_keys(k_ref, v_ref, kp_ref, vp_ref, kbuf, vbuf, dil, n)

            @pl.when(step == 0)
            def _():
                carry[gi] = jnp.zeros((2, SLAB, e), F32)

            _to_residue_order(do_p, do_ref, dil, tmp_s)
            do16_p[...] = do_p[...].astype(BF16)
            _to_residue_order(stat_p, stat_s, dil, tmp_s)

            def scores(q_rows, kv_rows, bias):
                return _dot_nt(q_ref[q_rows, :], kbuf[kv_rows, :]), _dot_nt(do16_p[q_rows, :], vbuf[kv_rows, :])

            def rest(q_rows, kv_rows, bias, begun):
                qk, dp = begun
                q = q_ref[q_rows, :]
                k = kbuf[kv_rows, :]
                stat = stat_p[q_rows, :]
                p = jnp.exp(qk * ATTN_SCALE + bias - stat[:, 0:1])
                ds = (p * (dp - stat[:, e // 2:e // 2 + 1]) * ATTN_SCALE).astype(BF16)
                dq_p[q_rows, :] = _dot(ds, k)
                dkb[kv_rows, :] += _dot_tn(ds, q)
                dvb[kv_rows, :] += _dot_tn(p.astype(BF16), do16_p[q_rows, :])

            _for_each_tile(dil, n, scores, rest)
            for r in range(dil):
                own, before = pl.ds((2 * r + 1) * seg, seg), pl.ds(2 * r * seg, seg)
                kept = pl.ds(r * seg, seg)
                dk_p[kept, :] = dkb[own, :] + carry.at[gi, 0][kept, :]
                dv_p[kept, :] = dvb[own, :] + carry.at[gi, 1][kept, :]
                carry.at[gi, 0][kept, :] = dkb[before, :]
                carry.at[gi, 1][kept, :] = dvb[before, :]
            _to_token_order(dqs, dq_p, dil, tmp_s)
            _to_token_order(dk_tok, dk_p, dil, tmp_s)
            _to_token_order(dv_tok, dv_p, dil, tmp_s)

            def norm_bwd(raw, gain, d_hat):
                r = _rms(raw)
                y = raw * r
                dy = d_hat * gain
                return r * (dy - y * jnp.mean(dy * y, axis=-1, keepdims=True)), jnp.sum(d_hat * y, axis=0, keepdims=True)

            dq, dqn = norm_bwd(qraw_ref[...], qn_ref[...], dqs[...])
            dk, dkn = norm_bwd(kraw_ref[...], kn_ref[...], dk_tok[...])
            dq_ref[...] = dq.astype(BF16)
            dk_ref[...] = dk.astype(BF16)
            dv_ref[...] = dv_tok[...].astype(BF16)
            st_ref[0:1, :] += dqn
            st_ref[1:2, :] += dkn

        for gi, dil in enumerate(DILATIONS):
            pl.when(grp == gi)(lambda gi=gi, dil=dil: run(gi, dil))

    slab_of = lambda s: n_slabs - 1 - s
    small = pl.BlockSpec((1, e), lambda h, s, g: (0, 0))
    head_blk = pl.BlockSpec((SLAB, e), lambda h, s, g: (slab_of(s), h))
    grad_blk = pl.BlockSpec((SLAB, e), lambda h, s, g: (slab_of(s), g * HEADS + h))
    grad_shape = jax.ShapeDtypeStruct((s_len, QKV), BF16)
    return pl.pallas_call(
        body, name=name, grid=(HEADS, n_slabs, N_GROUPS),
        in_specs=(_qkv_specs(slab_of, HAT_BLOCKS) + _qkv_specs(slab_of, [(0, 0), (1, 0)])
                  + [head_blk, head_blk, head_blk, small, small]),
        out_specs=[grad_blk, grad_blk, grad_blk, pl.BlockSpec((8, e), lambda h, s, g: (0, 0))],
        out_shape=[grad_shape, grad_shape, grad_shape, jax.ShapeDtypeStruct((8, e), F32)],
        scratch_shapes=[pltpu.VMEM((2 * SLAB, e), BF16), pltpu.VMEM((2 * SLAB, e), BF16), pltpu.VMEM((SLAB, e), F32),
                        pltpu.VMEM((SLAB, e), F32), pltpu.VMEM((2 * SLAB, e), F32), pltpu.VMEM((2 * SLAB, e), F32),
                        pltpu.VMEM((SLAB, e), F32), pltpu.VMEM((SLAB, e), F32),
                        pltpu.VMEM((N_GROUPS, 2, SLAB, e), F32)]
        + [pltpu.VMEM((SLAB, e), F32)] * 6 + [pltpu.VMEM((SLAB, e), BF16)],
        compiler_params=_params(3),
    )(hat, hat, hat, hat, hat, qkv, qkv, d_out, out, lse, q_norm, k_norm)


def _shift_rows(x, by, edge, forward):
    t_len = x.shape[0]
    row = lax.broadcasted_iota(jnp.int32, x.shape, 0)
    if forward:
        out = pltpu.roll(x, by, 0)
        for i in range(by):
            out = jnp.where(row == i, edge[8 - by + i:8 - by + i + 1, :], out)
    else:
        out = pltpu.roll(x, t_len - by, 0)
        for i in range(by):
            out = jnp.where(row == t_len - by + i, edge[i:i + 1, :], out)
    return out


def _mix_fwd(x, o, rest, mod, mod_next, conv_w, w_attn, w_conv, w_out, name):
    s_len, d = x.shape
    tm = MIX_TILE
    a_w = o.shape[1]

    def body(x_ref, o_ref, u_ref, b_ref, c_ref, ga_ref, gc_ref, mod_ref, modn_ref, cw_ref, wa_ref, wc_ref, wo_ref,
             xo_ref, z_ref, ya_ref, yc_ref, conv_ref, yb_ref, m_ref, h_ref, ht_ref, carry):
        @pl.when(pl.program_id(0) == 0)
        def _():
            carry[...] = jnp.zeros_like(carry)

        xc = c_ref[...].astype(F32) * u_ref[...].astype(F32)
        edge = carry[...]
        conv = (_shift_rows(xc, 2, edge, True) * cw_ref[0:1, :] + _shift_rows(xc, 1, edge, True) * cw_ref[1:2, :]
                + xc * cw_ref[2:3, :])
        carry[...] = xc[tm - 8:tm, :]
        yb = (b_ref[...].astype(F32) * conv).astype(BF16)
        ya = _dot(o_ref[...], wa_ref[...])
        yc = _dot(yb, wc_ref[...])
        merged = (_sigmoid(ga_ref[...].astype(F32)) * ya + _sigmoid(gc_ref[...].astype(F32)) * yc).astype(BF16)
        z = _dot(merged, wo_ref[...])
        xo = x_ref[...] + mod_ref[2:3, :] * z
        xo_ref[...] = xo
        hn = ((xo * _rms(xo)) * modn_ref[3:4, :]) * (1.0 + modn_ref[1:2, :]) + modn_ref[0:1, :]
        h_ref[...] = hn.astype(BF16)
        ht_ref[...] = hn.T.astype(BF16)
        z_ref[...] = z.astype(BF16)
        ya_ref[...] = ya.astype(BF16)
        yc_ref[...] = yc.astype(BF16)
        conv_ref[...] = conv.astype(BF16)
        yb_ref[...] = yb
        m_ref[...] = merged

    tile = pl.BlockSpec((tm, d), lambda i: (i, 0))
    sect = lambda k: pl.BlockSpec((tm, d), lambda i: (i, k))
    att = pl.BlockSpec((tm, a_w), lambda i: (i, 0))
    const = lambda shape: pl.BlockSpec(shape, lambda i: (0, 0))
    f32_out = jax.ShapeDtypeStruct((s_len, d), F32)
    b16_out = jax.ShapeDtypeStruct((s_len, d), BF16)
    return pl.pallas_call(
        body, name=name, grid=(s_len // tm,),
        in_specs=[tile, att, sect(0), sect(1), sect(2), sect(3), sect(4), const((8, d)), const((8, d)), const((8, d)),
                  const((a_w, d)), const((d, d)), const((d, d))],
        out_specs=[tile] * 7 + [tile, pl.BlockSpec((d, tm), lambda i: (0, i))],
        out_shape=[f32_out] + [b16_out] * 6 + [b16_out, jax.ShapeDtypeStruct((d, s_len), BF16)],
        scratch_shapes=[pltpu.VMEM((8, d), F32)],
        compiler_params=_params(1),
    )(x, o, rest, rest, rest, rest, rest, mod, mod_next, conv_w, w_attn, w_conv, w_out)


def _mix_bwd(dxo, ya, yc, conv, rest, mod, conv_w, w_attn, w_conv, w_out, a_w, name, after=()):
    s_len, d = dxo.shape
    tm = MIX_TILE
    n_tiles = s_len // tm

    def body(dxo_ref, ya_ref, yc_ref, conv_ref, u_ref, b_ref, c_ref, ga_ref, gc_ref, mod_ref, cw_ref,
             wa_ref, wc_ref, wo_ref, do_ref, drest_ref, dz_ref, dya_ref, dyc_ref, st_ref, carry):
        @pl.when(pl.program_id(0) == 0)
        def _():
            carry[...] = jnp.zeros_like(carry)
            st_ref[...] = jnp.zeros_like(st_ref)

        dz = (mod_ref[2:3, :] * dxo_ref[...]).astype(BF16)
        dz_ref[...] = dz
        dm = _dot_nt(dz, wo_ref[...])
        sa, sc = _sigmoid(ga_ref[...].astype(F32)), _sigmoid(gc_ref[...].astype(F32))
        dya = (dm * sa).astype(BF16)
        dyc = (dm * sc).astype(BF16)
        dya_ref[...] = dya
        dyc_ref[...] = dyc
        drest_ref[:, 3 * d:4 * d] = (dm * ya_ref[...].astype(F32) * (sa * (1.0 - sa))).astype(BF16)
        drest_ref[:, 4 * d:5 * d] = (dm * yc_ref[...].astype(F32) * (sc * (1.0 - sc))).astype(BF16)
        do_ref[...] = _dot_nt(dya, wa_ref[...])
        dyb = _dot_nt(dyc, wc_ref[...])
        drest_ref[:, d:2 * d] = (dyb * conv_ref[...].astype(F32)).astype(BF16)
        dconv = dyb * b_ref[...].astype(F32)
        edge = carry[...]
        sh1 = _shift_rows(dconv, 1, edge, False)
        sh2 = _shift_rows(dconv, 2, edge, False)
        carry[...] = dconv[0:8, :]
        dxc = dconv * cw_ref[2:3, :] + sh1 * cw_ref[1:2, :] + sh2 * cw_ref[0:1, :]
        u, c = u_ref[...].astype(F32), c_ref[...].astype(F32)
        xc = c * u
        drest_ref[:, 0:d] = (dxc * c).astype(BF16)
        drest_ref[:, 2 * d:3 * d] = (dxc * u).astype(BF16)
        st_ref[0:1, :] += jnp.sum(xc * sh2, axis=0, keepdims=True)
        st_ref[1:2, :] += jnp.sum(xc * sh1, axis=0, keepdims=True)
        st_ref[2:3, :] += jnp.sum(xc * dconv, axis=0, keepdims=True)

    rev = lambda i: n_tiles - 1 - i
    tile = pl.BlockSpec((tm, d), lambda i: (rev(i), 0))
    sect = lambda k: pl.BlockSpec((tm, d), lambda i: (rev(i), k))
    const = lambda shape: pl.BlockSpec(shape, lambda i: (0, 0))
    b16_out = jax.ShapeDtypeStruct((s_len, d), BF16)
    return pl.pallas_call(
        _ordered(body, 14, after), name=name, grid=(n_tiles,),
        in_specs=[tile, tile, tile, tile, sect(0), sect(1), sect(2), sect(3), sect(4), const((8, d)), const((8, d)),
                  const((a_w, d)), const((d, d)), const((d, d))] + [ANY] * len(after),
        out_specs=[pl.BlockSpec((tm, a_w), lambda i: (rev(i), 0)), pl.BlockSpec((tm, 5 * d), lambda i: (rev(i), 0)),
                   tile, tile, tile, const((8, d))],
        out_shape=[jax.ShapeDtypeStruct((s_len, a_w), F32), jax.ShapeDtypeStruct((s_len, 5 * d), BF16),
                   b16_out, b16_out, b16_out, jax.ShapeDtypeStruct((8, d), F32)],
        scratch_shapes=[pltpu.VMEM((8, d), F32)],
        compiler_params=_params(1),
    )(dxo, ya, yc, conv, rest, rest, rest, rest, rest, mod, conv_w, w_attn, w_conv, w_out, *after)


ADA_COLS = 128


def _ada_fwd(c_all, w_shard, b_shard, name):
    d, cols = w_shard.shape

    def body(c_ref, w_ref, b_ref, o_ref):
        cv = c_ref[...]
        o_ref[...] = jnp.dot(cv * _sigmoid(cv), w_ref[...], preferred_element_type=F32,
                             precision=lax.Precision.HIGHEST) + b_ref[...]

    return pl.pallas_call(
        body, name=name, grid=(cols // ADA_COLS,),
        in_specs=[pl.BlockSpec((8, d), lambda j: (0, 0)), pl.BlockSpec((d, ADA_COLS), lambda j: (0, j)),
                  pl.BlockSpec((1, ADA_COLS), lambda j: (0, j))],
        out_specs=pl.BlockSpec((8, ADA_COLS), lambda j: (0, j)),
        out_shape=jax.ShapeDtypeStruct((8, cols), F32),
        compiler_params=_params(1),
    )(c_all, w_shard, b_shard)


def _ada_bwd(c_all, dmod_shard, w, m, v, name):
    d, cols = w.shape

    def body(c_ref, dm_ref, w_ref, m_ref, v_ref, g_ref, d_ref, nm_ref, nv_ref):
        cv = c_ref[...]
        g = lax.dot_general(cv * _sigmoid(cv), dm_ref[...], (((0,), (0,)), ((), ())),
                            preferred_element_type=F32, precision=lax.Precision.HIGHEST)
        g_ref[...] = g
        d_ref[...], nm_ref[...], nv_ref[...] = _adamw_math(w_ref[...], g, m_ref[...], v_ref[...])

    blk = pl.BlockSpec((d, ADA_COLS), lambda j: (0, j))
    shape = jax.ShapeDtypeStruct((d, cols), F32)
    return pl.pallas_call(
        body, name=name, grid=(cols // ADA_COLS,),
        in_specs=[pl.BlockSpec((8, d), lambda j: (0, 0)), pl.BlockSpec((8, ADA_COLS), lambda j: (0, j)), blk, blk, blk],
        out_specs=[blk] * 4, out_shape=[shape] * 4,
        compiler_params=_params(1),
    )(c_all, dmod_shard, w, m, v)


def _small_update(parts, w, m, v, name):
    n = w.shape[1]

    def body(p_ref, w_ref, m_ref, v_ref, g_ref, d_ref, nm_ref, nv_ref):
        g = p_ref[0:1, :]
        for i in range(1, 8):
            g = g + p_ref[i:i + 1, :]
        g_ref[...] = g
        d_ref[...], nm_ref[...], nv_ref[...] = _adamw_math(w_ref[...], g, m_ref[...], v_ref[...])

    shape = jax.ShapeDtypeStruct((1, n), F32)
    return pl.pallas_call(body, name=name, out_shape=[shape] * 4, compiler_params=_params())(parts, w, m, v)


def _cols_to_shards(w, n):
    r, nc = w.shape
    return w.reshape(r, n, nc // n).transpose(1, 0, 2)


def kernel(x, c, w_ada, b_ada, norm_ffn1, ffn1_w_gate, ffn1_w_up, ffn1_w_down, norm_mix, w_in, q_norm, k_norm, conv_w, w_attn_branch, w_conv_branch, w_out, norm_ffn2, ffn2_w_gate, ffn2_w_up, ffn2_w_down, loss_target, m_w_ada, m_b_ada, m_norm_ffn1, m_ffn1_w_gate, m_ffn1_w_up, m_ffn1_w_down, m_norm_mix, m_w_in, m_q_norm, m_k_norm, m_conv_w, m_w_attn_branch, m_w_conv_branch, m_w_out, m_norm_ffn2, m_ffn2_w_gate, m_ffn2_w_up, m_ffn2_w_down, v_w_ada, v_b_ada, v_norm_ffn1, v_ffn1_w_gate, v_ffn1_w_up, v_ffn1_w_down, v_norm_mix, v_w_in, v_q_norm, v_k_norm, v_conv_w, v_w_attn_branch, v_w_conv_branch, v_w_out, v_norm_ffn2, v_ffn2_w_gate, v_ffn2_w_up, v_ffn2_w_down):
    ix, iy, ic = _place()
    chip = 2 * ix + iy
    me = 4 * ix + 2 * iy + ic
    xs = x[0]
    target = loss_target[0]
    s_len, d = xs.shape
    ada_cols = w_ada.shape[2]
    conv_cols = conv_w.shape[2]

    conv_rows = jnp.zeros((8, conv_cols), F32).at[0:3].set(conv_w[0])
    small_in = jnp.concatenate([jnp.broadcast_to(c, (8, d)), conv_rows], axis=1)
    small_all = _allgather8(small_in, "gather_c").reshape(8, 8, d + conv_cols)
    c_all = small_all[:, 0, :d]
    conv_full = small_all[0::2, 0:3, d:].transpose(1, 0, 2).reshape(3, N_CHIPS * conv_cols)
    conv_pad = jnp.zeros((8, N_CHIPS * conv_cols), F32).at[0:3].set(conv_full)
    b_shard = lax.dynamic_slice(b_ada, (0, chip * ada_cols), (1, ada_cols))
    mod_part = _ada_fwd(c_all, w_ada[0], b_shard, "ada_fwd")
    mod_all = _allgather8(mod_part, "gather_mod").reshape(N_CHIPS, 2, 8, ada_cols)[:, 0]
    mod_mine = lax.dynamic_slice(mod_all, (0, me, 0), (N_CHIPS, 1, ada_cols)).reshape(9, d)

    def mod_rows(i, gain):
        return jnp.zeros((8, d), F32).at[0:3].set(mod_mine[3 * i:3 * i + 3]).at[3:4].set(gain)

    mod1, mod2, mod3 = mod_rows(0, norm_ffn1), mod_rows(1, norm_mix), mod_rows(2, norm_ffn2)

    to16 = lambda w: w[0].astype(BF16)
    wg1, wu1, wd1 = _gather_weights([to16(ffn1_w_gate), to16(ffn1_w_up), to16(ffn1_w_down)], [False] * 3,
                                    "gather_ffn1", 1)
    h1, h1t = _norm_mod(xs, mod1, "norm1")
    (w_in_full,) = _gather_weights([to16(w_in)], [True], "gather_w_in", 2, after=(wd1, h1))

    g1, u1, y1 = _ffn_fwd(h1, wg1, wu1, wd1, "ffn1_fwd")
    x1, h2, h2t = _norm_mod(xs, mod2, "norm2", prev=(y1, mod1, 0.5))
    qkv, rest, qkv_hat = _in_proj(h2, w_in_full, q_norm, k_norm, "in_proj")
    w_ab, w_cb_g, w_o_g, wg2, wu2, wd2 = _gather_weights(
        [to16(w_attn_branch), to16(w_conv_branch), to16(w_out),
         to16(ffn2_w_gate), to16(ffn2_w_up), to16(ffn2_w_down)], [True] + [False] * 5,
        "gather_rest", 3, after=(h2,))
    a_w = w_ab.shape[0]
    w_cb = w_cb_g.reshape(d, d)
    w_o = w_o_g.reshape(d, d)
    o, lse = _attn_fwd(qkv_hat, "attn_fwd")
    x2, z, ya, yc, conv, yb, merged, h3, h3t = _mix_fwd(x1, o, rest, mod2, mod3, conv_pad, w_ab, w_cb, w_o, "mix_fwd")
    g3, u3, y3 = _ffn_fwd(h3, wg2, wu2, wd2, "ffn2_fwd")
    dx3, dy3, loss_part = _loss_grad(x2, y3, mod3, target, "loss")

    c_idx = jnp.reshape(ic, (1,)).astype(jnp.int32)
    chip_idx = jnp.stack([chip, ic]).astype(jnp.int32)

    def pair_send(grads, tag, collective_id):
        return _rs_pair_exchange(grads, "rs_pair_" + tag, collective_id)

    def chip_send(grads, from_sibling, names, tag, collective_id, after):
        pair_sums = [_pair_add(g, r, c_idx, "pair_add_" + nm, after) for g, r, nm in zip(grads, from_sibling, names)]
        return pair_sums, _rs_chip_exchange(pair_sums, "rs_chips_" + tag, collective_id)

    def reduce_finish(pair_sums, from_chips, names, tag, after):
        totals = [_chip_add(p, r, chip_idx, "chip_add_" + nm, after)
                  for p, r, nm in zip(pair_sums, from_chips, names)]
        return dict(zip(names, _rs_share(totals, "rs_share_" + tag)))

    names_a = ["ffn2_w_gate", "ffn2_w_up", "ffn2_w_down"]
    names_b = ["w_in", "w_attn_branch", "w_conv_branch", "w_out"]
    names_c = ["ffn1_w_gate", "ffn1_w_up", "ffn1_w_down"]

    dh3, dg3, du3, a3 = _ffn_bwd(dy3, g3, u3, wg2, wu2, wd2, "ffn2_bwd")
    grads_a = list(_ffn_wgrads(h3t, dg3, du3, a3, dy3, "ffn2"))
    sibling_a = pair_send(grads_a, "a", 7)
    dx2, st3 = _norm_bwd(dh3, x2, mod3, dx3, y3, 0.5, "norm3_bwd")
    sums_a, chips_a = chip_send(grads_a, sibling_a, names_a, "a", 4, after=(dx2,))

    do, drest, dz, dya, dyc, st_conv = _mix_bwd(dx2, ya, yc, conv, rest, mod2, conv_pad, w_ab, w_cb, w_o, a_w,
                                                "mix_bwd", after=tuple(sums_a))
    dq, dk, dv, st_qk = _attn_bwd(qkv, qkv_hat, do, o, lse, q_norm, k_norm, "attn_bwd")
    tok = lambda width: (lambda ts: pl.BlockSpec((ts, width), lambda cc, s: (s, 0)))
    colblk = lambda width: (lambda ts: pl.BlockSpec((ts, width), lambda cc, s: (s, cc)))
    tok_t = lambda ts: pl.BlockSpec((d, ts), lambda cc, s: (0, s))
    whole = pl.BlockSpec((d, QKV), lambda cc, s: (0, 0))
    dw_in = [_wgrad(h2t, part, tok_t, tok(QKV), (d, QKV), whole, (d, QKV), 1, "dw_in_" + nm, True)
             for part, nm in ((dq, "q"), (dk, "k"), (dv, "v"))]
    dw_in.append(_wgrad(h2t, drest, tok_t, colblk(d), (d, 5 * d), pl.BlockSpec((d, d), lambda cc, s: (0, cc)),
                        (d, d), 5, "dw_in_rest", True))
    dw_in = _cols_to_shards(jnp.concatenate(dw_in, axis=1), N_CHIPS)
    shard_w = d // N_CHIPS
    dw_ab = _wgrad(o, dya, tok(a_w), colblk(shard_w), (a_w, d), pl.BlockSpec((a_w, shard_w), lambda cc, s: (0, cc)),
                   (a_w, shard_w), N_CHIPS, "dw_attn_branch")
    dw_ab = _cols_to_shards(dw_ab, N_CHIPS)
    row_out = pl.BlockSpec((None, shard_w, d), lambda cc, s: (cc, 0, 0))
    dw_cb = _wgrad(yb, dyc, colblk(shard_w), tok(d), (N_CHIPS, shard_w, d), row_out, (shard_w, d), N_CHIPS, "dw_conv_branch")
    dw_o = _wgrad(merged, dz, colblk(shard_w), tok(d), (N_CHIPS, shard_w, d), row_out, (shard_w, d), N_CHIPS, "dw_out")
    shard_grads = reduce_finish(sums_a, chips_a, names_a, "a", after=(dw_in, dw_o))
    grads_b = [dw_in, dw_ab, dw_cb, dw_o]
    sibling_b = pair_send(grads_b, "b", 8)

    dh2 = _in_proj_bwd(dq, dk, dv, drest, w_in_full, "in_proj_bwd")
    sums_b, chips_b = chip_send(grads_b, sibling_b, names_b, "b", 5, after=(dh2,))
    dx1, st2, dy1 = _norm_bwd(dh2, x1, mod2, dx2, z, 1.0, "norm2_bwd", after=tuple(sums_b), prev=(mod1, 0.5))
    dh1, dg1, du1, a1 = _ffn_bwd(dy1, g1, u1, wg1, wu1, wd1, "ffn1_bwd")
    dx0, st1 = _norm_bwd(dh1, xs, mod1, dx1, y1, 0.5, "norm1_bwd")
    grads_c = list(_ffn_wgrads(h1t, dg1, du1, a1, dy1, "ffn1"))
    sibling_c = pair_send(grads_c, "c", 9)
    shard_grads.update(reduce_finish(sums_b, chips_b, names_b, "b", after=tuple(grads_c)))

    dmod = jnp.concatenate([st1[0:3], st2[0:3], st3[0:3]], axis=0).reshape(1, 9 * d)
    loss_cols = jnp.zeros((1, HEAD_DIM), F32).at[0, 0].set(jnp.sum(loss_part))
    small = jnp.concatenate([dmod, st1[3:4], st2[3:4], st3[3:4], st_qk[0:1], st_qk[1:2],
                             st_conv[0:3].reshape(1, 3 * d), loss_cols], axis=1)
    small_all = _allgather8(jnp.broadcast_to(small, (8, small.shape[1])), "gather_small").reshape(8, 8, -1)[:, 0]
    loss = (0.5 / d) * jnp.sum(small_all[:, -HEAD_DIM])
    small_all = small_all[:, :-HEAD_DIM]
    dmod_all = small_all[:, :9 * d]
    dmod_shard = lax.dynamic_slice(dmod_all, (0, chip * ada_cols), (8, ada_cols))
    g_w_ada, d_w_ada, nm_w_ada, nv_w_ada = _ada_bwd(c_all, dmod_shard, w_ada[0], m_w_ada[0], v_w_ada[0], "ada_bwd")

    vec_names = ["b_ada", "norm_ffn1", "norm_mix", "norm_ffn2", "q_norm", "k_norm"]
    vec_w = [b_ada, norm_ffn1, norm_mix, norm_ffn2, q_norm, k_norm]
    vec_m = [m_b_ada, m_norm_ffn1, m_norm_mix, m_norm_ffn2, m_q_norm, m_k_norm]
    vec_v = [v_b_ada, v_norm_ffn1, v_norm_mix, v_norm_ffn2, v_q_norm, v_k_norm]
    n_vec = sum(w.shape[1] for w in vec_w)
    cat = lambda arrs: jnp.concatenate(arrs, axis=1)
    vec_out = _small_update(small_all[:, :n_vec], cat(vec_w), cat(vec_m), cat(vec_v), "small_update")
    conv_parts = small_all[:, n_vec:].reshape(8, 3, N_CHIPS * conv_cols)
    conv_parts = lax.dynamic_slice(conv_parts, (0, 0, chip * conv_cols), (8, 3, conv_cols)).reshape(8, 3 * conv_cols)
    flat3 = lambda w: w[0].reshape(1, 3 * conv_cols)
    conv_out = _small_update(conv_parts, flat3(conv_w), flat3(m_conv_w), flat3(v_conv_w), "conv_update")

    res = {"w_ada": [t[None] for t in (g_w_ada, d_w_ada, nm_w_ada, nv_w_ada)],
           "conv_w": [t.reshape(1, 3, conv_cols) for t in conv_out]}
    off = 0
    for nm, w in zip(vec_names, vec_w):
        width = w.shape[1]
        res[nm] = [t[:, off:off + width] for t in vec_out]
        off += width
    big = {"ffn1_w_gate": (ffn1_w_gate, m_ffn1_w_gate, v_ffn1_w_gate), "ffn1_w_up": (ffn1_w_up, m_ffn1_w_up, v_ffn1_w_up),
           "ffn1_w_down": (ffn1_w_down, m_ffn1_w_down, v_ffn1_w_down), "w_in": (w_in, m_w_in, v_w_in),
           "w_attn_branch": (w_attn_branch, m_w_attn_branch, v_w_attn_branch),
           "w_conv_branch": (w_conv_branch, m_w_conv_branch, v_w_conv_branch), "w_out": (w_out, m_w_out, v_w_out),
           "ffn2_w_gate": (ffn2_w_gate, m_ffn2_w_gate, v_ffn2_w_gate), "ffn2_w_up": (ffn2_w_up, m_ffn2_w_up, v_ffn2_w_up),
           "ffn2_w_down": (ffn2_w_down, m_ffn2_w_down, v_ffn2_w_down)}
    def update(nm, after=()):
        w, m, v = big[nm]
        g, delta, new_m, new_v = _adamw(w[0], shard_grads[nm], m[0], v[0], "adamw_" + nm, after)
        res[nm] = [t[None] for t in (g, delta, new_m, new_v)]
        return new_v

    last = tuple(shard_grads[nm] for nm in names_b)
    for nm in names_a:
        last = (update(nm, last),)
    sums_c, chips_c = chip_send(grads_c, sibling_c, names_c, "c", 6, after=last)
    last = tuple(sums_c)
    for nm in names_b:
        last = (update(nm, last),)
    shard_grads.update(reduce_finish(sums_c, chips_c, names_c, "c", after=last))
    for nm in names_c:
        update(nm)

    order = ["w_ada", "b_ada", "norm_ffn1", "ffn1_w_gate", "ffn1_w_up", "ffn1_w_down", "norm_mix", "w_in", "q_norm",
             "k_norm", "conv_w", "w_attn_branch", "w_conv_branch", "w_out", "norm_ffn2", "ffn2_w_gate", "ffn2_w_up",
             "ffn2_w_down"]
    return (loss, dx0[None], *[res[nm][0] for nm in order], *[res[nm][1] for nm in order],
            *[res[nm][2] for nm in order], *[res[nm][3] for nm in order])
```

```python
import jax
import jax.numpy as jnp
from jax import lax
from jax.experimental import pallas as pl
from jax.experimental.pallas import tpu as pltpu
from jax.experimental.pallas import tpu_sc as plsc

F32 = jnp.float32
BF16 = jnp.bfloat16
MESH = pl.DeviceIdType.MESH
ANY = pl.BlockSpec(memory_space=pl.ANY)

NORM_EPS = 1e-6
HEAD_DIM = 128
N_GROUPS = 3
HEADS = 4
DILATIONS = (1, 4, 16)
ATTN_BLOCK = 128
SLAB = ATTN_BLOCK * max(DILATIONS)
QKV = N_GROUPS * HEADS * HEAD_DIM
ATTN_SCALE = HEAD_DIM ** -0.5
NEG = -1e30
N_CHIPS = 4

ADAM_LR = 0.001
ADAM_B1 = 0.9
ADAM_B2 = 0.999
ADAM_EPS = 1e-08
ADAM_WD = 0.01
ADAM_STEP = 10

VMEM_LIMIT_BYTES = 56 * 1024 * 1024
TOKEN_TILE = 512
FFN_TILE = 1024
PROJ_TILE = 2048
WGRAD_TILE = 2048
IN_BLOCK = 512
MIX_TILE = 512
ACC_PIECES = 4
ADAMW_TILE_BYTES = 3 * 512 * 1024


def _params(n_axes=0):
    return pltpu.CompilerParams(
        dimension_semantics=("arbitrary",) * n_axes if n_axes else None,
        vmem_limit_bytes=VMEM_LIMIT_BYTES)


def _dot(a, b):
    return jnp.dot(a, b, preferred_element_type=F32)


def _dot_nt(a, b):
    return lax.dot_general(a, b, (((1,), (1,)), ((), ())), preferred_element_type=F32)


def _dot_tn(a, b):
    return lax.dot_general(a, b, (((0,), (0,)), ((), ())), preferred_element_type=F32)


def _sigmoid(x):
    return 1.0 / (1.0 + jnp.exp(-x))


def _place():
    return lax.axis_index("x"), lax.axis_index("y"), lax.axis_index("c")


def _ordered(body, n_in, after):
    if not after:
        return body
    return lambda *refs: body(*refs[:n_in], *refs[n_in + len(after):])


def _allgather8(block, name):
    m_per, n = block.shape

    def body(x_ref, out_ref, send_sems, recv_sems, local_sem):
        x, y, c = _place()
        me, sibling = (x, y, c), (x, y, 1 - c)
        chips = [(1 - x, y), (x, 1 - y), (1 - x, 1 - y)]

        def rows(px, py, pc):
            return out_ref.at[pl.ds((4 * px + 2 * py + pc) * m_per, m_per), :]

        def copy(k, blk, to, src=None):
            return pltpu.make_async_remote_copy(
                src_ref=rows(*blk) if src is None else src, dst_ref=rows(*blk),
                send_sem=send_sems.at[k], recv_sem=recv_sems.at[k],
                device_id=to, device_id_type=MESH)

        mine = pltpu.make_async_copy(x_ref, rows(*me), local_sem)
        mine.start()
        first = [copy(0, me, sibling, src=x_ref)]
        first += [copy(1 + j, me, (*chip, c), src=x_ref) for j, chip in enumerate(chips)]
        for cp in first:
            cp.start()
        passed = [copy(4 + j, (*chip, c), sibling) for j, chip in enumerate(chips)]
        for j, chip in enumerate(chips):
            copy(1 + j, (*chip, c), me).wait_recv()
            passed[j].start()
        copy(0, sibling, me).wait_recv()
        for j, chip in enumerate(chips):
            copy(4 + j, (*chip, 1 - c), me).wait_recv()
        for cp in first + passed:
            cp.wait_send()
        mine.wait()

    return pl.pallas_call(
        body, name=name,
        out_shape=jax.ShapeDtypeStruct((8 * m_per, n), block.dtype),
        in_specs=[pl.BlockSpec(memory_space=pltpu.VMEM)],
        out_specs=pl.BlockSpec(memory_space=pltpu.VMEM),
        scratch_shapes=[pltpu.SemaphoreType.DMA((7,)), pltpu.SemaphoreType.DMA((7,)),
                        pltpu.SemaphoreType.DMA],
        compiler_params=_params(),
    )(block)


def _handshake(peers):
    barrier = pltpu.get_barrier_semaphore()
    for peer in peers:
        pl.semaphore_signal(barrier, inc=1, device_id=peer, device_id_type=MESH)
    pl.semaphore_wait(barrier, len(peers))


def _gather_weights(shards, by_cols, name, collective_id, after=()):
    n_arr = len(shards)

    def body(*refs):
        srcs, outs = refs[:n_arr], refs[n_arr + len(after):2 * n_arr + len(after)]
        send_sems, recv_sems, local_sems = refs[2 * n_arr + len(after):]
        x, y, c = _place()
        me_dev, sibling = (x, y, c), (x, y, 1 - c)
        chips = [(1 - x, y), (x, 1 - y), (1 - x, 1 - y)]
        me = 2 * x + y
        _handshake([sibling] + [(*chip, c) for chip in chips])

        def place(k, chip_idx, rows):
            if by_cols[k]:
                width = srcs[k].shape[1]
                return outs[k].at[rows, pl.ds(pl.multiple_of(chip_idx * width, 128), width)]
            return outs[k].at[chip_idx, rows]

        def copy(k, slot, chip_idx, half_sel, to, from_shard=False):
            half = srcs[k].shape[0] // 2
            rows = pl.ds(half_sel * half, half)
            dst = place(k, chip_idx, rows)
            return pltpu.make_async_remote_copy(
                src_ref=srcs[k].at[rows] if from_shard else dst, dst_ref=dst,
                send_sem=send_sems.at[6 * k + slot], recv_sem=recv_sems.at[6 * k + slot],
                device_id=to, device_id_type=MESH)

        own = [pltpu.make_async_copy(srcs[k], place(k, me, pl.ds(0, srcs[k].shape[0])), local_sems.at[k])
               for k in range(n_arr)]
        for cp in own:
            cp.start()
        sent = []
        for k in range(n_arr):
            for j, chip in enumerate(chips):
                sent.append(copy(k, j, me, c, (*chip, c), from_shard=True))
                sent[-1].start()
        for k in range(n_arr):
            for j, chip in enumerate(chips):
                chip_idx = 2 * chip[0] + chip[1]
                copy(k, j, chip_idx, c, me_dev).wait_recv()
                sent.append(copy(k, 3 + j, chip_idx, c, sibling))
                sent[-1].start()
        for k in range(n_arr):
            for j, chip in enumerate(chips):
                copy(k, 3 + j, 2 * chip[0] + chip[1], 1 - c, me_dev).wait_recv()
        for cp in sent:
            cp.wait_send()
        for cp in own:
            cp.wait()

    def gathered(k):
        r, cols = shards[k].shape
        return (r, N_CHIPS * cols) if by_cols[k] else (N_CHIPS, r, cols)

    return pl.kernel(
        body, name=name,
        out_type=[jax.ShapeDtypeStruct(gathered(k), shards[k].dtype) for k in range(n_arr)],
        mesh=plsc.ScalarSubcoreMesh(axis_name="sequencer", num_cores=1),
        scratch_types=[pltpu.SemaphoreType.DMA((6 * n_arr,)), pltpu.SemaphoreType.DMA((6 * n_arr,)),
                       pltpu.SemaphoreType.DMA((n_arr,))],
        compiler_params=pltpu.CompilerParams(collective_id=collective_id),
    )(*shards, *after)


def _rs_pair_exchange(grads, name, collective_id):
    n_arr = len(grads)

    def body(*refs):
        srcs, outs = refs[:n_arr], refs[n_arr:2 * n_arr]
        send_sems, recv_sems = refs[2 * n_arr:]
        x, y, c = _place()
        _handshake([(x, y, 1 - c)])
        cps = []
        for k in range(n_arr):
            half = srcs[k].shape[1] // 2
            cps.append(pltpu.make_async_remote_copy(
                src_ref=srcs[k].at[:, pl.ds((1 - c) * half, half)], dst_ref=outs[k],
                send_sem=send_sems.at[k], recv_sem=recv_sems.at[k],
                device_id=(x, y, 1 - c), device_id_type=MESH))
            cps[-1].start()
        for cp in cps:
            cp.wait_recv()
        for cp in cps:
            cp.wait_send()

    return pl.kernel(
        body, name=name,
        out_type=[jax.ShapeDtypeStruct((g.shape[0], g.shape[1] // 2, g.shape[2]), g.dtype) for g in grads],
        mesh=plsc.ScalarSubcoreMesh(axis_name="sequencer", num_cores=1),
        scratch_types=[pltpu.SemaphoreType.DMA((n_arr,)), pltpu.SemaphoreType.DMA((n_arr,))],
        compiler_params=pltpu.CompilerParams(collective_id=collective_id),
    )(*grads)


def _rs_chip_exchange(sums, name, collective_id):
    n_arr = len(sums)

    def body(*refs):
        srcs, outs = refs[:n_arr], refs[n_arr:2 * n_arr]
        send_sems, recv_sems = refs[2 * n_arr:]
        x, y, c = _place()
        chips = [(1 - x, y), (x, 1 - y), (1 - x, 1 - y)]
        _handshake([(*chip, c) for chip in chips])
        cps = []
        for k in range(n_arr):
            for j, chip in enumerate(chips):
                cps.append(pltpu.make_async_remote_copy(
                    src_ref=srcs[k].at[2 * chip[0] + chip[1]], dst_ref=outs[k].at[j],
                    send_sem=send_sems.at[3 * k + j], recv_sem=recv_sems.at[3 * k + j],
                    device_id=(*chip, c), device_id_type=MESH))
                cps[-1].start()
        for cp in cps:
            cp.wait_recv()
        for cp in cps:
            cp.wait_send()

    return pl.kernel(
        body, name=name,
        out_type=[jax.ShapeDtypeStruct((3,) + s.shape[1:], s.dtype) for s in sums],
        mesh=plsc.ScalarSubcoreMesh(axis_name="sequencer", num_cores=1),
        scratch_types=[pltpu.SemaphoreType.DMA((3 * n_arr,)), pltpu.SemaphoreType.DMA((3 * n_arr,))],
        compiler_params=pltpu.CompilerParams(collective_id=collective_id),
    )(*sums)


def _rs_share(totals, name):
    n_arr = len(totals)

    def body(*refs):
        outs = refs[n_arr:2 * n_arr]
        send_sems, recv_sems = refs[2 * n_arr:]
        x, y, c = _place()

        def half_rows(k, sel):
            return outs[k].at[sel]

        cps = []
        for k in range(n_arr):
            cps.append(pltpu.make_async_remote_copy(
                src_ref=half_rows(k, c), dst_ref=half_rows(k, c), send_sem=send_sems.at[k], recv_sem=recv_sems.at[k],
                device_id=(x, y, 1 - c), device_id_type=MESH))
            cps[-1].start()
        for k in range(n_arr):
            pltpu.make_async_remote_copy(
                src_ref=half_rows(k, c), dst_ref=half_rows(k, 1 - c), send_sem=send_sems.at[k],
                recv_sem=recv_sems.at[k], device_id=(x, y, 1 - c), device_id_type=MESH).wait_recv()
        for cp in cps:
            cp.wait_send()

    shared = pl.pallas_call(
        body, name=name,
        out_shape=[jax.ShapeDtypeStruct(t.shape, t.dtype) for t in totals],
        in_specs=[ANY] * n_arr, out_specs=[ANY] * n_arr,
        input_output_aliases={k: k for k in range(n_arr)},
        scratch_shapes=[pltpu.SemaphoreType.DMA((n_arr,)), pltpu.SemaphoreType.DMA((n_arr,))],
        compiler_params=_params(),
    )(*totals)
    return [t.reshape(2 * t.shape[1], t.shape[2]) for t in shared]


def _pair_add(grad, recv, c_idx, name, after=()):
    n, r, cols = grad.shape
    half = r // 2
    rows = half // 2

    def body(_, g_ref, r_ref, o_ref):
        o_ref[...] = (g_ref[...].astype(F32) + r_ref[...].astype(F32)).astype(o_ref.dtype)

    return pl.pallas_call(
        _ordered(body, 3, after), name=name,
        grid_spec=pltpu.PrefetchScalarGridSpec(
            num_scalar_prefetch=1, grid=(n, 2),
            in_specs=[pl.BlockSpec((None, None, rows, cols), lambda s, i, ci: (s, ci[0], i, 0)),
                      pl.BlockSpec((None, rows, cols), lambda s, i, ci: (s, i, 0))] + [ANY] * len(after),
            out_specs=pl.BlockSpec((None, rows, cols), lambda s, i, ci: (s, i, 0))),
        out_shape=jax.ShapeDtypeStruct((n, half, cols), BF16),
        compiler_params=_params(2),
    )(c_idx, grad.reshape(n, 2, half, cols), recv, *after)


def _chip_add(sums, recv, chip_and_core, name, after=()):
    _, half, cols = sums.shape
    rows = half // 2

    def body(_, s_ref, r0_ref, r1_ref, r2_ref, o_ref):
        o_ref[...] = ((s_ref[...].astype(F32) + r0_ref[...].astype(F32))
                      + r1_ref[...].astype(F32)) + r2_ref[...].astype(F32)

    def recv_spec(j):
        return pl.BlockSpec((None, rows, cols), lambda i, ci: (j, i, 0))

    return pl.pallas_call(
        _ordered(body, 5, after), name=name,
        grid_spec=pltpu.PrefetchScalarGridSpec(
            num_scalar_prefetch=1, grid=(2,),
            in_specs=[pl.BlockSpec((None, rows, cols), lambda i, ci: (ci[0], i, 0)),
                      recv_spec(0), recv_spec(1), recv_spec(2)] + [ANY] * len(after),
            out_specs=pl.BlockSpec((None, rows, cols), lambda i, ci: (ci[1], i, 0))),
        out_shape=jax.ShapeDtypeStruct((2, half, cols), F32),
        compiler_params=_params(1),
    )(chip_and_core, sums, recv, recv, recv, *after)


def _rms(x):
    return lax.rsqrt(jnp.mean(x * x, axis=-1, keepdims=True) + NORM_EPS)


def _norm_mod(x, mod, name, prev=None):
    s_len, d = x.shape
    tm = TOKEN_TILE

    def body(*refs):
        if prev is None:
            x_ref, mod_ref, h_ref, ht_ref = refs
            xv = x_ref[...]
        else:
            x_ref, y_ref, modp_ref, mod_ref, xo_ref, h_ref, ht_ref = refs
            xv = x_ref[...] + prev[2] * modp_ref[2:3, :] * y_ref[...]
            xo_ref[...] = xv
        n = (xv * _rms(xv)) * mod_ref[3:4, :]
        h = n * (1.0 + mod_ref[1:2, :]) + mod_ref[0:1, :]
        h_ref[...] = h.astype(BF16)
        ht_ref[...] = h.T.astype(BF16)

    tile = pl.BlockSpec((tm, d), lambda i: (i, 0))
    small = pl.BlockSpec((8, d), lambda i: (0, 0))
    h_specs = [tile, pl.BlockSpec((d, tm), lambda i: (0, i))]
    h_shapes = [jax.ShapeDtypeStruct((s_len, d), BF16), jax.ShapeDtypeStruct((d, s_len), BF16)]
    if prev is None:
        return pl.pallas_call(
            body, name=name, grid=(s_len // tm,), in_specs=[tile, small], out_specs=h_specs, out_shape=h_shapes,
            compiler_params=_params(1))(x, mod)
    return pl.pallas_call(
        body, name=name, grid=(s_len // tm,), in_specs=[tile, tile, small, small],
        out_specs=[tile] + h_specs, out_shape=[jax.ShapeDtypeStruct((s_len, d), F32)] + h_shapes,
        compiler_params=_params(1))(x, prev[0], prev[1], mod)


def _norm_bwd(dh, x, mod, dxo, y_raw, coef, name, after=(), prev=None):
    s_len, d = x.shape
    tm = TOKEN_TILE

    def body(*refs):
        if prev is None:
            dh_ref, x_ref, mod_ref, dxo_ref, y_ref, dx_ref, st_ref = refs
        else:
            dh_ref, x_ref, mod_ref, dxo_ref, y_ref, modp_ref, dx_ref, st_ref, dyp_ref = refs

        @pl.when(pl.program_id(0) == 0)
        def _():
            st_ref[...] = jnp.zeros_like(st_ref)

        xv, dhv, dxov = x_ref[...], dh_ref[...], dxo_ref[...]
        r = _rms(xv)
        xh = xv * r
        gain, scale = mod_ref[3:4, :], mod_ref[1:2, :]
        dn = dhv * (1.0 + scale)
        dxh = dn * gain
        dx = dxov + r * (dxh - xh * jnp.mean(dxh * xh, axis=-1, keepdims=True))
        dx_ref[...] = dx
        if prev is not None:
            dyp_ref[...] = (prev[1] * modp_ref[2:3, :] * dx).astype(BF16)
        st_ref[0:1, :] += jnp.sum(dhv, axis=0, keepdims=True)
        st_ref[1:2, :] += jnp.sum(dhv * (xh * gain), axis=0, keepdims=True)
        st_ref[2:3, :] += coef * jnp.sum(y_ref[...].astype(F32) * dxov, axis=0, keepdims=True)
        st_ref[3:4, :] += jnp.sum(dn * xh, axis=0, keepdims=True)

    tile = pl.BlockSpec((tm, d), lambda i: (i, 0))
    small = pl.BlockSpec((8, d), lambda i: (0, 0))
    operands = [dh, x, mod, dxo, y_raw] + ([] if prev is None else [prev[0]])
    in_specs = [tile, tile, small, tile, tile] + ([] if prev is None else [small])
    out_specs = [tile, small] + ([] if prev is None else [tile])
    out_shape = [jax.ShapeDtypeStruct((s_len, d), F32), jax.ShapeDtypeStruct((8, d), F32)]
    if prev is not None:
        out_shape.append(jax.ShapeDtypeStruct((s_len, d), BF16))
    return pl.pallas_call(
        _ordered(body, len(operands), after), name=name, grid=(s_len // tm,),
        in_specs=in_specs + [ANY] * len(after), out_specs=out_specs, out_shape=out_shape,
        compiler_params=_params(1),
    )(*operands, *after)


def _loss_grad(x, y, mod, target, name):
    s_len, d = x.shape
    tm = TOKEN_TILE

    def body(x_ref, y_ref, mod_ref, t_ref, do_ref, dy_ref, part_ref):
        @pl.when(pl.program_id(0) == 0)
        def _():
            part_ref[...] = jnp.zeros_like(part_ref)

        half_gate = 0.5 * mod_ref[2:3, :]
        err = (x_ref[...] + half_gate * y_ref[...]) - t_ref[...]
        do = err * (1.0 / d)
        do_ref[...] = do
        dy_ref[...] = (half_gate * do).astype(BF16)
        sq = err * err
        part_ref[...] += jnp.sum(sq.reshape(tm // 8, 8, d), axis=0)

    tile = pl.BlockSpec((tm, d), lambda i: (i, 0))
    small = pl.BlockSpec((8, d), lambda i: (0, 0))
    return pl.pallas_call(
        body, name=name, grid=(s_len // tm,),
        in_specs=[tile, tile, small, tile],
        out_specs=[tile, tile, small],
        out_shape=[jax.ShapeDtypeStruct((s_len, d), F32), jax.ShapeDtypeStruct((s_len, d), BF16),
                   jax.ShapeDtypeStruct((8, d), F32)],
        compiler_params=_params(1),
    )(x, y, mod, target)


def _adamw_math(w, g, m, v):
    m = ADAM_B1 * m + (1.0 - ADAM_B1) * g
    v = ADAM_B2 * v + (1.0 - ADAM_B2) * (g * g)
    m_hat = m / (1.0 - ADAM_B1 ** ADAM_STEP)
    v_hat = v / (1.0 - ADAM_B2 ** ADAM_STEP)
    delta = -ADAM_LR * (m_hat / (jnp.sqrt(v_hat) + ADAM_EPS) + ADAM_WD * w)
    return delta, m, v


def _adamw(w, g, m, v, name, after=()):
    r, cols = w.shape
    tr = max([t for t in (r // k for k in (1, 2, 4, 8, 16)) if t % 8 == 0 and r % t == 0
              and t * cols * 4 <= ADAMW_TILE_BYTES] or [r])

    def body(w_ref, g_ref, m_ref, v_ref, go_ref, d_ref, nm_ref, nv_ref):
        gv = g_ref[...]
        go_ref[...] = gv
        d_ref[...], nm_ref[...], nv_ref[...] = _adamw_math(w_ref[...], gv, m_ref[...], v_ref[...])

    tile = pl.BlockSpec((tr, cols), lambda i: (i, 0))
    shape = jax.ShapeDtypeStruct((r, cols), F32)
    return pl.pallas_call(
        _ordered(body, 4, after), name=name, grid=(r // tr,),
        in_specs=[tile] * 4 + [ANY] * len(after), out_specs=[tile] * 4, out_shape=[shape] * 4,
        compiler_params=_params(1),
    )(w, g, m, v, *after)


def _in_parts(tm, n_qkv, n_rest):
    def part(lo, n_blk):
        return pl.BlockSpec((tm, IN_BLOCK), lambda i, j: (i, jnp.clip(j - lo, 0, n_blk - 1)))
    return [part(0, n_qkv), part(n_qkv, n_qkv), part(2 * n_qkv, n_qkv), part(3 * n_qkv, n_rest)]


def _pick_part(j, n_qkv, refs, fn):
    bounds = [0, n_qkv, 2 * n_qkv, 3 * n_qkv]
    for p, ref in enumerate(refs):
        inside = j >= bounds[p]
        if p + 1 < len(refs):
            inside = inside & (j < bounds[p + 1])
        pl.when(inside)(lambda ref=ref: fn(ref))


def _rows(base, count, stride):
    return pl.ds(base, count) if stride == 1 else pl.ds(base, count, stride=stride)


REORDER_STRIDE = 4


def _reorder_plan(dil, parts=1):
    inner = min(dil, REORDER_STRIDE)
    return inner, dil // inner, SLAB // parts // inner, SLAB // dil


def _to_residue_order(dst, src, dil, tmp, part=0, parts=1):
    inner, outer, big, seg = _reorder_plan(dil, parts)
    piece = seg // parts
    if outer == 1:
        for r in range(dil):
            dst[pl.ds(r * seg + part * piece, piece), :] = src[_rows(r, piece, dil), :].astype(dst.dtype)
        return
    for b in range(inner):
        tmp[pl.ds(b * big, big), :] = src[_rows(b, big, inner), :]
    for a in range(outer):
        for b in range(inner):
            dst[pl.ds((inner * a + b) * seg + part * piece, piece), :] = (
                tmp[_rows(b * big + a, piece, outer), :].astype(dst.dtype))


def _to_token_order(dst, src, dil, tmp):
    inner, outer, big, seg = _reorder_plan(dil)
    if outer == 1:
        for r in range(dil):
            dst[_rows(r, seg, dil), :] = src[pl.ds(r * seg, seg), :]
        return
    for a in range(outer):
        for b in range(inner):
            tmp[_rows(b * big + a, seg, outer), :] = src[pl.ds((inner * a + b) * seg, seg), :]
    for b in range(inner):
        dst[_rows(b, big, inner), :] = tmp[pl.ds(b * big, big), :]


def _in_proj(h, w, q_norm, k_norm, name):
    s_len, d = h.shape
    tm = PROJ_TILE
    assert tm == SLAB and IN_BLOCK == HEADS * HEAD_DIM
    steps = w.shape[1] // IN_BLOCK
    n_qkv = 3 * QKV // IN_BLOCK
    parts = 4
    rows = [pl.ds(p * (tm // parts), tm // parts) for p in range(parts)]

    gains = jnp.concatenate([q_norm, k_norm, jnp.ones((6, HEAD_DIM), F32)], axis=0)

    def body(h_ref, w_ref, gains_ref, qkv_ref, rest_ref, hat_ref, tok_s, tmp_s):
        j = pl.program_id(1)
        sect = j // N_GROUPS
        multiply = lambda p: _dot(h_ref[rows[p], :], w_ref[...])

        def emit(gi):
            dil = DILATIONS[gi]
            res = [multiply(p) for p in range(parts)]
            gain = gains_ref[pl.ds(sect, 1), :]
            plain = sect == 2
            for p in range(parts):
                qkv_ref[rows[p], :] = res[p]
                for hh in range(HEADS):
                    cols = slice(hh * HEAD_DIM, (hh + 1) * HEAD_DIM)
                    x = res[p][:, cols]
                    tok_s[...] = (x * jnp.where(plain, 1.0, _rms(x))) * gain
                    _to_residue_order(hat_ref.at[:, cols], tok_s, dil, tmp_s, p, parts)

        for gi in range(N_GROUPS):
            pl.when((j < n_qkv) & (j % N_GROUPS == gi))(lambda gi=gi: emit(gi))

        @pl.when(j >= n_qkv)
        def _():
            for p in range(parts):
                rest_ref[rows[p], :] = multiply(p).astype(BF16)

    qkv_blk = pl.BlockSpec((tm, IN_BLOCK), lambda i, j: (i, jnp.minimum(j, n_qkv - 1)))
    return pl.pallas_call(
        body, name=name, grid=(s_len // tm, steps),
        in_specs=[pl.BlockSpec((tm, d), lambda i, j: (i, 0)), pl.BlockSpec((d, IN_BLOCK), lambda i, j: (0, j)),
                  pl.BlockSpec((8, HEAD_DIM), lambda i, j: (0, 0))],
        out_specs=[qkv_blk, pl.BlockSpec((tm, IN_BLOCK), lambda i, j: (i, jnp.maximum(j - n_qkv, 0))), qkv_blk],
        out_shape=[jax.ShapeDtypeStruct((s_len, 3 * QKV), F32),
                   jax.ShapeDtypeStruct((s_len, w.shape[1] - 3 * QKV), BF16),
                   jax.ShapeDtypeStruct((s_len, 3 * QKV), BF16)],
        scratch_shapes=[pltpu.VMEM((tm // parts, HEAD_DIM), F32)] * 2,
        compiler_params=_params(2),
    )(h, w, gains)


def _in_proj_bwd(dq, dk, dv, drest, w, name, after=()):
    s_len = dq.shape[0]
    d = w.shape[0]
    tm = PROJ_TILE
    steps = w.shape[1] // IN_BLOCK
    n_qkv = QKV // IN_BLOCK

    def body(dq_ref, dk_ref, dv_ref, dr_ref, w_ref, o_ref, acc_ref):
        j = pl.program_id(1)

        @pl.when(j == 0)
        def _():
            acc_ref[...] = jnp.zeros_like(acc_ref)

        def add(a_ref):
            rows = [pl.ds(p * (tm // ACC_PIECES), tm // ACC_PIECES) for p in range(ACC_PIECES)]
            products = [_dot_nt(a_ref[r, :], w_ref[...]) for r in rows]
            for r, product in zip(rows, products):
                acc_ref[r, :] += product

        _pick_part(j, n_qkv, [dq_ref, dk_ref, dv_ref, dr_ref], add)

        @pl.when(j == steps - 1)
        def _():
            o_ref[...] = acc_ref[...]

    return pl.pallas_call(
        _ordered(body, 5, after), name=name, grid=(s_len // tm, steps),
        in_specs=(_in_parts(tm, n_qkv, steps - 3 * n_qkv) + [pl.BlockSpec((d, IN_BLOCK), lambda i, j: (0, j))]
                  + [ANY] * len(after)),
        out_specs=pl.BlockSpec((tm, d), lambda i, j: (i, 0)),
        out_shape=jax.ShapeDtypeStruct((s_len, d), F32),
        scratch_shapes=[pltpu.VMEM((tm, d), F32)],
        compiler_params=_params(2),
    )(dq, dk, dv, drest, w, *after)


def _wgrad(x, y, x_spec, y_spec, out_shape, out_spec, acc_shape, n_chunks, name, x_transposed=False, after=()):
    s_len = y.shape[-2]
    ts = WGRAD_TILE
    steps = s_len // ts

    def body(x_ref, y_ref, o_ref, acc_ref):
        s = pl.program_id(1)

        @pl.when(s == 0)
        def _():
            acc_ref[...] = jnp.zeros_like(acc_ref)

        if x_transposed:
            n_rows = acc_shape[0]
            rows = [pl.ds(p * (n_rows // ACC_PIECES), n_rows // ACC_PIECES) for p in range(ACC_PIECES)]
            products = [_dot(x_ref[r, :], y_ref[...]) for r in rows]
            for r, product in zip(rows, products):
                acc_ref[r, :] += product
        else:
            cols = _pieces(acc_shape[1])
            products = [_dot_tn(x_ref[...], y_ref[:, c]) for c in cols]
            for c, product in zip(cols, products):
                acc_ref[:, c] += product

        @pl.when(s == steps - 1)
        def _():
            o_ref[...] = acc_ref[...].astype(o_ref.dtype)

    return pl.pallas_call(
        _ordered(body, 2, after), name=name, grid=(n_chunks, steps),
        in_specs=[x_spec(ts), y_spec(ts)] + [ANY] * len(after), out_specs=out_spec,
        out_shape=jax.ShapeDtypeStruct(out_shape, BF16),
        scratch_shapes=[pltpu.VMEM(acc_shape, F32)],
        compiler_params=_params(2),
    )(x, y, *after)


def _pieces(width, piece=256):
    return [slice(a, min(a + piece, width)) for a in range(0, width, piece)]


def _ffn_fwd(h, w_gate, w_up, w_down, name):
    s_len, d = h.shape
    n_chunks, _, fs = w_gate.shape
    tm = FFN_TILE

    def body(h_ref, wg_ref, wu_ref, wd_ref, g_ref, u_ref, y_ref):
        j = pl.program_id(1)
        hv = h_ref[...]
        pieces = _pieces(fs)
        first = lambda cols: (_dot(hv, wg_ref[:, cols]), _dot(hv, wu_ref[:, cols]))
        total = None
        ahead = first(pieces[0])
        for k, cols in enumerate(pieces):
            g, u = ahead
            if k + 1 < len(pieces):
                ahead = first(pieces[k + 1])
            g_ref[:, cols] = g.astype(BF16)
            u_ref[:, cols] = u.astype(BF16)
            act = (g * _sigmoid(g)) * u
            part = _dot(act.astype(BF16), wd_ref[cols, :])
            total = part if total is None else total + part

        @pl.when(j == 0)
        def _():
            y_ref[...] = total

        @pl.when(j > 0)
        def _():
            y_ref[...] += total

    tile = pl.BlockSpec((tm, d), lambda i, j: (i, 0))
    hid = pl.BlockSpec((None, tm, fs), lambda i, j: (j, i, 0))
    w_in_spec = pl.BlockSpec((None, d, fs), lambda i, j: (j, 0, 0))
    hid_shape = jax.ShapeDtypeStruct((n_chunks, s_len, fs), BF16)
    return pl.pallas_call(
        body, name=name, grid=(s_len // tm, n_chunks),
        in_specs=[tile, w_in_spec, w_in_spec, pl.BlockSpec((None, fs, d), lambda i, j: (j, 0, 0))],
        out_specs=[hid, hid, tile],
        out_shape=[hid_shape, hid_shape, jax.ShapeDtypeStruct((s_len, d), F32)],
        compiler_params=_params(2),
    )(h, w_gate, w_up, w_down)


def _ffn_bwd(dy, g_pre, u_pre, w_gate, w_up, w_down, name):
    s_len, d = dy.shape
    n_chunks, _, fs = w_gate.shape
    tm = FFN_TILE

    def body(dy_ref, g_ref, u_ref, wg_ref, wu_ref, wd_ref, dh_ref, dg_ref, du_ref, a_ref):
        j = pl.program_id(1)
        dyv = dy_ref[...]
        pieces = _pieces(fs)
        first = lambda cols: _dot_nt(dyv, wd_ref[cols, :])
        total = None
        ahead = first(pieces[0])
        for k, cols in enumerate(pieces):
            da = ahead
            if k + 1 < len(pieces):
                ahead = first(pieces[k + 1])
            g = g_ref[:, cols].astype(F32)
            u = u_ref[:, cols].astype(F32)
            sg = _sigmoid(g)
            silu = g * sg
            dg = (da * u * (sg * (1.0 + g * (1.0 - sg)))).astype(BF16)
            du = (da * silu).astype(BF16)
            dg_ref[:, cols] = dg
            du_ref[:, cols] = du
            a_ref[:, cols] = (silu * u).astype(BF16)
            part = _dot_nt(dg, wg_ref[:, cols]) + _dot_nt(du, wu_ref[:, cols])
            total = part if total is None else total + part

        @pl.when(j == 0)
        def _():
            dh_ref[...] = total

        @pl.when(j > 0)
        def _():
            dh_ref[...] += total

    tile = pl.BlockSpec((tm, d), lambda i, j: (i, 0))
    hid = pl.BlockSpec((None, tm, fs), lambda i, j: (j, i, 0))
    w_in_spec = pl.BlockSpec((None, d, fs), lambda i, j: (j, 0, 0))
    hid_shape = jax.ShapeDtypeStruct((n_chunks, s_len, fs), BF16)
    return pl.pallas_call(
        body, name=name, grid=(s_len // tm, n_chunks),
        in_specs=[tile, hid, hid, w_in_spec, w_in_spec, pl.BlockSpec((None, fs, d), lambda i, j: (j, 0, 0))],
        out_specs=[tile, hid, hid, hid],
        out_shape=[jax.ShapeDtypeStruct((s_len, d), F32), hid_shape, hid_shape, hid_shape],
        compiler_params=_params(2),
    )(dy, g_pre, u_pre, w_gate, w_up, w_down)


def _ffn_wgrads(ht, dg, du, act, dy, tag, after=()):
    n_chunks, s_len, fs = dg.shape
    d = ht.shape[0]
    tok = lambda ts: pl.BlockSpec((ts, d), lambda c, s: (s, 0))
    tok_t = lambda ts: pl.BlockSpec((d, ts), lambda c, s: (0, s))
    hid = lambda ts: pl.BlockSpec((None, ts, fs), lambda c, s: (c, s, 0))
    d_up = pl.BlockSpec((None, d, fs), lambda c, s: (c, 0, 0))
    d_down = pl.BlockSpec((None, fs, d), lambda c, s: (c, 0, 0))
    dwg = _wgrad(ht, dg, tok_t, hid, (n_chunks, d, fs), d_up, (d, fs), n_chunks, tag + "_dwg", True, after)
    dwu = _wgrad(ht, du, tok_t, hid, (n_chunks, d, fs), d_up, (d, fs), n_chunks, tag + "_dwu", True, after)
    dwd = _wgrad(act, dy, hid, tok, (n_chunks, fs, d), d_down, (fs, d), n_chunks, tag + "_dwd", False, after)
    return dwg, dwu, dwd


def _band_bias():
    qi = lax.broadcasted_iota(jnp.int32, (ATTN_BLOCK, 2 * ATTN_BLOCK), 0)
    kj = lax.broadcasted_iota(jnp.int32, (ATTN_BLOCK, 2 * ATTN_BLOCK), 1)
    band = (kj >= qi) & (kj <= qi + ATTN_BLOCK)
    return jnp.where(band, 0.0, NEG), jnp.where(band & (kj >= ATTN_BLOCK), 0.0, NEG)


def _qkv_specs(slab_of, sections):
    def spec(sect, back):
        return pl.BlockSpec((SLAB, HEAD_DIM),
                            lambda h, s, g: (jnp.maximum(slab_of(s) - back, 0), (sect * N_GROUPS + g) * HEADS + h))
    return [spec(sect, back) for sect, back in sections]


HAT_BLOCKS = [(0, 0), (1, 0), (2, 0), (1, 1), (2, 1)]


def _stage_keys(k_ref, v_ref, kp_ref, vp_ref, kbuf, vbuf, dil, n):
    run = SLAB // dil
    for r in range(dil):
        own, before = pl.ds(r * run, run), pl.ds(2 * r * run, run)
        kbuf[pl.ds((2 * r + 1) * run, run), :] = k_ref[own, :]
        vbuf[pl.ds((2 * r + 1) * run, run), :] = v_ref[own, :]

        @pl.when(n > 0)
        def _():
            kbuf[before, :] = kp_ref[own, :]
            vbuf[before, :] = vp_ref[own, :]

        @pl.when(n == 0)
        def _():
            kbuf[before, :] = jnp.zeros((run, HEAD_DIM), BF16)
            vbuf[before, :] = jnp.zeros((run, HEAD_DIM), BF16)


def _for_each_tile(dil, n, first_fn, rest_fn):
    run = SLAB // dil
    bias, first_bias = _band_bias()
    tiles = []
    for jj in range(run // ATTN_BLOCK):
        start = jj * ATTN_BLOCK
        tile_bias = jnp.where(n == 0, first_bias, bias) if jj == 0 else bias
        for r in range(dil):
            tiles.append((pl.ds(r * run + start, ATTN_BLOCK),
                          pl.ds((2 * r + 1) * run - ATTN_BLOCK + start, 2 * ATTN_BLOCK), tile_bias))
    ahead = first_fn(*tiles[0])
    for t, tile in enumerate(tiles):
        begun = ahead
        if t + 1 < len(tiles):
            ahead = first_fn(*tiles[t + 1])
        rest_fn(*tile, begun)


def _attn_fwd(hat, name):
    s_len = hat.shape[0]
    e = HEAD_DIM
    n_slabs = s_len // SLAB

    def body(q_ref, k_ref, v_ref, kp_ref, vp_ref, o_ref, lse_ref, kbuf, vbuf, m_s, l_s, acc_s, m_p, l_p, acc_p, tmp_s):
        n, grp = pl.program_id(1), pl.program_id(2)

        def run(gi, dil):
            _stage_keys(k_ref, v_ref, kp_ref, vp_ref, kbuf, vbuf, dil, n)

            def scores(q_rows, kv_rows, bias):
                return _dot_nt(q_ref[q_rows, :], kbuf[kv_rows, :])

            def rest(q_rows, kv_rows, bias, qk):
                s = qk * ATTN_SCALE + bias
                m = jnp.max(s, axis=-1, keepdims=True)
                p = jnp.exp(s - m)
                m_p[q_rows, :] = jnp.broadcast_to(m, (ATTN_BLOCK, e))
                l_p[q_rows, :] = jnp.broadcast_to(jnp.sum(p, axis=-1, keepdims=True), (ATTN_BLOCK, e))
                acc_p[q_rows, :] = _dot(p.astype(BF16), vbuf[kv_rows, :])

            _for_each_tile(dil, n, scores, rest)
            _to_token_order(m_s.at[gi], m_p, dil, tmp_s)
            _to_token_order(l_s.at[gi], l_p, dil, tmp_s)
            _to_token_order(acc_s.at[gi], acc_p, dil, tmp_s)

        for gi, dil in enumerate(DILATIONS):
            pl.when(grp == gi)(lambda gi=gi, dil=dil: run(gi, dil))

        @pl.when(grp == N_GROUPS - 1)
        def _():
            m_all = jnp.maximum(jnp.maximum(m_s[0], m_s[1]), m_s[2])
            den = jnp.zeros((SLAB, e), F32)
            num = jnp.zeros((SLAB, e), F32)
            for gi in range(N_GROUPS):
                w = jnp.exp(m_s[gi] - m_all)
                den += l_s[gi] * w
                num += acc_s[gi] * w
            o_ref[...] = (num / den).astype(BF16)
            lse_ref[...] = m_all + jnp.log(den)

    out = pl.BlockSpec((SLAB, e), lambda h, n, g: (n, h))
    return pl.pallas_call(
        body, name=name, grid=(HEADS, n_slabs, N_GROUPS),
        in_specs=_qkv_specs(lambda n: n, HAT_BLOCKS),
        out_specs=[out, out],
        out_shape=[jax.ShapeDtypeStruct((s_len, HEADS * e), BF16), jax.ShapeDtypeStruct((s_len, HEADS * e), F32)],
        scratch_shapes=[pltpu.VMEM((2 * SLAB, e), BF16), pltpu.VMEM((2 * SLAB, e), BF16),
                        pltpu.VMEM((N_GROUPS, SLAB, e), F32), pltpu.VMEM((N_GROUPS, SLAB, e), F32),
                        pltpu.VMEM((N_GROUPS, SLAB, e), F32)]
        + [pltpu.VMEM((SLAB, e), F32)] * 4,
        compiler_params=_params(3),
    )(hat, hat, hat, hat, hat)


def _attn_bwd(qkv, hat, d_out, out, lse, q_norm, k_norm, name):
    s_len = qkv.shape[0]
    e = HEAD_DIM
    n_slabs = s_len // SLAB

    def body(q_ref, k_ref, v_ref, kp_ref, vp_ref, qraw_ref, kraw_ref, do_ref, o_ref, lse_ref, qn_ref, kn_ref,
             dq_ref, dk_ref, dv_ref, st_ref, kbuf, vbuf, stat_s, dqs, dkb, dvb, dk_tok, dv_tok, carry,
             do_p, stat_p, dq_p, dk_p, dv_p, tmp_s, do16_p):
        head, step, grp = pl.program_id(0), pl.program_id(1), pl.program_id(2)
        n = n_slabs - 1 - step
        dkb[...] = jnp.zeros_like(dkb)
        dvb[...] = jnp.zeros_like(dvb)
        @pl.when(grp == 0)
        def _():
            lane = lax.broadcasted_iota(jnp.int32, (SLAB, e), 1)
            stat_s[...] = jnp.where(lane < e // 2, lse_ref[...],
                                    jnp.sum(do_ref[...] * o_ref[...].astype(F32), axis=-1, keepdims=True))

        @pl.when((head == 0) & (step == 0) & (grp == 0))
        def _():
            st_ref[...] = jnp.zeros_like(st_ref)

        def run(gi, dil):
            seg = SLAB // dil
            _stage_keys(k_ref, v_ref, kp_ref, vp_ref, kbuf, vbuf, dil, n)

            @pl.when(step == 0)
            def _():
                carry[gi] = jnp.zeros((2, SLAB, e), F32)

            _to_residue_order(do_p, do_ref, dil, tmp_s)
            do16_p[...] = do_p[...].astype(BF16)
            _to_residue_order(stat_p, stat_s, dil, tmp_s)

            def scores(q_rows, kv_rows, bias):
                return _dot_nt(q_ref[q_rows, :], kbuf[kv_rows, :]), _dot_nt(do16_p[q_rows, :], vbuf[kv_rows, :])

            def rest(q_rows, kv_rows, bias, begun):
                qk, dp = begun
                q = q_ref[q_rows, :]
                k = kbuf[kv_rows, :]
                stat = stat_p[q_rows, :]
                p = jnp.exp(qk * ATTN_SCALE + bias - stat[:, 0:1])
                ds = (p * (dp - stat[:, e // 2:e // 2 + 1]) * ATTN_SCALE).astype(BF16)
                dq_p[q_rows, :] = _dot(ds, k)
                dkb[kv_rows, :] += _dot_tn(ds, q)
                dvb[kv_rows, :] += _dot_tn(p.astype(BF16), do16_p[q_rows, :])

            _for_each_tile(dil, n, scores, rest)
            for r in range(dil):
                own, before = pl.ds((2 * r + 1) * seg, seg), pl.ds(2 * r * seg, seg)
                kept = pl.ds(r * seg, seg)
                dk_p[kept, :] = dkb[own, :] + carry.at[gi, 0][kept, :]
                dv_p[kept, :] = dvb[own, :] + carry.at[gi, 1][kept, :]
                carry.at[gi, 0][kept, :] = dkb[before, :]
                carry.at[gi, 1][kept, :] = dvb[before, :]
            _to_token_order(dqs, dq_p, dil, tmp_s)
            _to_token_order(dk_tok, dk_p, dil, tmp_s)
            _to_token_order(dv_tok, dv_p, dil, tmp_s)

            def norm_bwd(raw, gain, d_hat):
                r = _rms(raw)
                y = raw * r
                dy = d_hat * gain
                return r * (dy - y * jnp.mean(dy * y, axis=-1, keepdims=True)), jnp.sum(d_hat * y, axis=0, keepdims=True)

            dq, dqn = norm_bwd(qraw_ref[...], qn_ref[...], dqs[...])
            dk, dkn = norm_bwd(kraw_ref[...], kn_ref[...], dk_tok[...])
            dq_ref[...] = dq.astype(BF16)
            dk_ref[...] = dk.astype(BF16)
            dv_ref[...] = dv_tok[...].astype(BF16)
            st_ref[0:1, :] += dqn
            st_ref[1:2, :] += dkn

        for gi, dil in enumerate(DILATIONS):
            pl.when(grp == gi)(lambda gi=gi, dil=dil: run(gi, dil))

    slab_of = lambda s: n_slabs - 1 - s
    small = pl.BlockSpec((1, e), lambda h, s, g: (0, 0))
    head_blk = pl.BlockSpec((SLAB, e), lambda h, s, g: (slab_of(s), h))
    grad_blk = pl.BlockSpec((SLAB, e), lambda h, s, g: (slab_of(s), g * HEADS + h))
    grad_shape = jax.ShapeDtypeStruct((s_len, QKV), BF16)
    return pl.pallas_call(
        body, name=name, grid=(HEADS, n_slabs, N_GROUPS),
        in_specs=(_qkv_specs(slab_of, HAT_BLOCKS) + _qkv_specs(slab_of, [(0, 0), (1, 0)])
                  + [head_blk, head_blk, head_blk, small, small]),
        out_specs=[grad_blk, grad_blk, grad_blk, pl.BlockSpec((8, e), lambda h, s, g: (0, 0))],
        out_shape=[grad_shape, grad_shape, grad_shape, jax.ShapeDtypeStruct((8, e), F32)],
        scratch_shapes=[pltpu.VMEM((2 * SLAB, e), BF16), pltpu.VMEM((2 * SLAB, e), BF16), pltpu.VMEM((SLAB, e), F32),
                        pltpu.VMEM((SLAB, e), F32), pltpu.VMEM((2 * SLAB, e), F32), pltpu.VMEM((2 * SLAB, e), F32),
                        pltpu.VMEM((SLAB, e), F32), pltpu.VMEM((SLAB, e), F32),
                        pltpu.VMEM((N_GROUPS, 2, SLAB, e), F32)]
        + [pltpu.VMEM((SLAB, e), F32)] * 6 + [pltpu.VMEM((SLAB, e), BF16)],
        compiler_params=_params(3),
    )(hat, hat, hat, hat, hat, qkv, qkv, d_out, out, lse, q_norm, k_norm)


def _shift_rows(x, by, edge, forward):
    t_len = x.shape[0]
    row = lax.broadcasted_iota(jnp.int32, x.shape, 0)
    if forward:
        out = pltpu.roll(x, by, 0)
        for i in range(by):
            out = jnp.where(row == i, edge[8 - by + i:8 - by + i + 1, :], out)
    else:
        out = pltpu.roll(x, t_len - by, 0)
        for i in range(by):
            out = jnp.where(row == t_len - by + i, edge[i:i + 1, :], out)
    return out


def _mix_fwd(x, o, rest, mod, mod_next, conv_w, w_attn, w_conv, w_out, name):
    s_len, d = x.shape
    tm = MIX_TILE
    a_w = o.shape[1]

    def body(x_ref, o_ref, u_ref, b_ref, c_ref, ga_ref, gc_ref, mod_ref, modn_ref, cw_ref, wa_ref, wc_ref, wo_ref,
             xo_ref, z_ref, ya_ref, yc_ref, conv_ref, yb_ref, m_ref, h_ref, ht_ref, carry):
        @pl.when(pl.program_id(0) == 0)
        def _():
            carry[...] = jnp.zeros_like(carry)

        xc = c_ref[...].astype(F32) * u_ref[...].astype(F32)
        edge = carry[...]
        conv = (_shift_rows(xc, 2, edge, True) * cw_ref[0:1, :] + _shift_rows(xc, 1, edge, True) * cw_ref[1:2, :]
                + xc * cw_ref[2:3, :])
        carry[...] = xc[tm - 8:tm, :]
        yb = (b_ref[...].astype(F32) * conv).astype(BF16)
        ya = _dot(o_ref[...], wa_ref[...])
        yc = _dot(yb, wc_ref[...])
        merged = (_sigmoid(ga_ref[...].astype(F32)) * ya + _sigmoid(gc_ref[...].astype(F32)) * yc).astype(BF16)
        z = _dot(merged, wo_ref[...])
        xo = x_ref[...] + mod_ref[2:3, :] * z
        xo_ref[...] = xo
        hn = ((xo * _rms(xo)) * modn_ref[3:4, :]) * (1.0 + modn_ref[1:2, :]) + modn_ref[0:1, :]
        h_ref[...] = hn.astype(BF16)
        ht_ref[...] = hn.T.astype(BF16)
        z_ref[...] = z.astype(BF16)
        ya_ref[...] = ya.astype(BF16)
        yc_ref[...] = yc.astype(BF16)
        conv_ref[...] = conv.astype(BF16)
        yb_ref[...] = yb
        m_ref[...] = merged

    tile = pl.BlockSpec((tm, d), lambda i: (i, 0))
    sect = lambda k: pl.BlockSpec((tm, d), lambda i: (i, k))
    att = pl.BlockSpec((tm, a_w), lambda i: (i, 0))
    const = lambda shape: pl.BlockSpec(shape, lambda i: (0, 0))
    f32_out = jax.ShapeDtypeStruct((s_len, d), F32)
    b16_out = jax.ShapeDtypeStruct((s_len, d), BF16)
    return pl.pallas_call(
        body, name=name, grid=(s_len // tm,),
        in_specs=[tile, att, sect(0), sect(1), sect(2), sect(3), sect(4), const((8, d)), const((8, d)), const((8, d)),
                  const((a_w, d)), const((d, d)), const((d, d))],
        out_specs=[tile] * 7 + [tile, pl.BlockSpec((d, tm), lambda i: (0, i))],
        out_shape=[f32_out] + [b16_out] * 6 + [b16_out, jax.ShapeDtypeStruct((d, s_len), BF16)],
        scratch_shapes=[pltpu.VMEM((8, d), F32)],
        compiler_params=_params(1),
    )(x, o, rest, rest, rest, rest, rest, mod, mod_next, conv_w, w_attn, w_conv, w_out)


def _mix_bwd(dxo, ya, yc, conv, rest, mod, conv_w, w_attn, w_conv, w_out, a_w, name, after=()):
    s_len, d = dxo.shape
    tm = MIX_TILE
    n_tiles = s_len // tm

    def body(dxo_ref, ya_ref, yc_ref, conv_ref, u_ref, b_ref, c_ref, ga_ref, gc_ref, mod_ref, cw_ref,
             wa_ref, wc_ref, wo_ref, do_ref, drest_ref, dz_ref, dya_ref, dyc_ref, st_ref, carry):
        @pl.when(pl.program_id(0) == 0)
        def _():
            carry[...] = jnp.zeros_like(carry)
            st_ref[...] = jnp.zeros_like(st_ref)

        dz = (mod_ref[2:3, :] * dxo_ref[...]).astype(BF16)
        dz_ref[...] = dz
        dm = _dot_nt(dz, wo_ref[...])
        sa, sc = _sigmoid(ga_ref[...].astype(F32)), _sigmoid(gc_ref[...].astype(F32))
        dya = (dm * sa).astype(BF16)
        dyc = (dm * sc).astype(BF16)
        dya_ref[...] = dya
        dyc_ref[...] = dyc
        drest_ref[:, 3 * d:4 * d] = (dm * ya_ref[...].astype(F32) * (sa * (1.0 - sa))).astype(BF16)
        drest_ref[:, 4 * d:5 * d] = (dm * yc_ref[...].astype(F32) * (sc * (1.0 - sc))).astype(BF16)
        do_ref[...] = _dot_nt(dya, wa_ref[...])
        dyb = _dot_nt(dyc, wc_ref[...])
        drest_ref[:, d:2 * d] = (dyb * conv_ref[...].astype(F32)).astype(BF16)
        dconv = dyb * b_ref[...].astype(F32)
        edge = carry[...]
        sh1 = _shift_rows(dconv, 1, edge, False)
        sh2 = _shift_rows(dconv, 2, edge, False)
        carry[...] = dconv[0:8, :]
        dxc = dconv * cw_ref[2:3, :] + sh1 * cw_ref[1:2, :] + sh2 * cw_ref[0:1, :]
        u, c = u_ref[...].astype(F32), c_ref[...].astype(F32)
        xc = c * u
        drest_ref[:, 0:d] = (dxc * c).astype(BF16)
        drest_ref[:, 2 * d:3 * d] = (dxc * u).astype(BF16)
        st_ref[0:1, :] += jnp.sum(xc * sh2, axis=0, keepdims=True)
        st_ref[1:2, :] += jnp.sum(xc * sh1, axis=0, keepdims=True)
        st_ref[2:3, :] += jnp.sum(xc * dconv, axis=0, keepdims=True)

    rev = lambda i: n_tiles - 1 - i
    tile = pl.BlockSpec((tm, d), lambda i: (rev(i), 0))
    sect = lambda k: pl.BlockSpec((tm, d), lambda i: (rev(i), k))
    const = lambda shape: pl.BlockSpec(shape, lambda i: (0, 0))
    b16_out = jax.ShapeDtypeStruct((s_len, d), BF16)
    return pl.pallas_call(
        _ordered(body, 14, after), name=name, grid=(n_tiles,),
        in_specs=[tile, tile, tile, tile, sect(0), sect(1), sect(2), sect(3), sect(4), const((8, d)), const((8, d)),
                  const((a_w, d)), const((d, d)), const((d, d))] + [ANY] * len(after),
        out_specs=[pl.BlockSpec((tm, a_w), lambda i: (rev(i), 0)), pl.BlockSpec((tm, 5 * d), lambda i: (rev(i), 0)),
                   tile, tile, tile, const((8, d))],
        out_shape=[jax.ShapeDtypeStruct((s_len, a_w), F32), jax.ShapeDtypeStruct((s_len, 5 * d), BF16),
                   b16_out, b16_out, b16_out, jax.ShapeDtypeStruct((8, d), F32)],
        scratch_shapes=[pltpu.VMEM((8, d), F32)],
        compiler_params=_params(1),
    )(dxo, ya, yc, conv, rest, rest, rest, rest, rest, mod, conv_w, w_attn, w_conv, w_out, *after)


ADA_COLS = 128


def _ada_fwd(c_all, w_shard, b_shard, name):
    d, cols = w_shard.shape

    def body(c_ref, w_ref, b_ref, o_ref):
        cv = c_ref[...]
        o_ref[...] = jnp.dot(cv * _sigmoid(cv), w_ref[...], preferred_element_type=F32,
                             precision=lax.Precision.HIGHEST) + b_ref[...]

    return pl.pallas_call(
        body, name=name, grid=(cols // ADA_COLS,),
        in_specs=[pl.BlockSpec((8, d), lambda j: (0, 0)), pl.BlockSpec((d, ADA_COLS), lambda j: (0, j)),
                  pl.BlockSpec((1, ADA_COLS), lambda j: (0, j))],
        out_specs=pl.BlockSpec((8, ADA_COLS), lambda j: (0, j)),
        out_shape=jax.ShapeDtypeStruct((8, cols), F32),
        compiler_params=_params(1),
    )(c_all, w_shard, b_shard)


def _ada_bwd(c_all, dmod_shard, w, m, v, name):
    d, cols = w.shape

    def body(c_ref, dm_ref, w_ref, m_ref, v_ref, g_ref, d_ref, nm_ref, nv_ref):
        cv = c_ref[...]
        g = lax.dot_general(cv * _sigmoid(cv), dm_ref[...], (((0,), (0,)), ((), ())),
                            preferred_element_type=F32, precision=lax.Precision.HIGHEST)
        g_ref[...] = g
        d_ref[...], nm_ref[...], nv_ref[...] = _adamw_math(w_ref[...], g, m_ref[...], v_ref[...])

    blk = pl.BlockSpec((d, ADA_COLS), lambda j: (0, j))
    shape = jax.ShapeDtypeStruct((d, cols), F32)
    return pl.pallas_call(
        body, name=name, grid=(cols // ADA_COLS,),
        in_specs=[pl.BlockSpec((8, d), lambda j: (0, 0)), pl.BlockSpec((8, ADA_COLS), lambda j: (0, j)), blk, blk, blk],
        out_specs=[blk] * 4, out_shape=[shape] * 4,
        compiler_params=_params(1),
    )(c_all, dmod_shard, w, m, v)


def _small_update(parts, w, m, v, name):
    n = w.shape[1]

    def body(p_ref, w_ref, m_ref, v_ref, g_ref, d_ref, nm_ref, nv_ref):
        g = p_ref[0:1, :]
        for i in range(1, 8):
            g = g + p_ref[i:i + 1, :]
        g_ref[...] = g
        d_ref[...], nm_ref[...], nv_ref[...] = _adamw_math(w_ref[...], g, m_ref[...], v_ref[...])

    shape = jax.ShapeDtypeStruct((1, n), F32)
    return pl.pallas_call(body, name=name, out_shape=[shape] * 4, compiler_params=_params())(parts, w, m, v)


def _cols_to_shards(w, n):
    r, nc = w.shape
    return w.reshape(r, n, nc // n).transpose(1, 0, 2)


def kernel(x, c, w_ada, b_ada, norm_ffn1, ffn1_w_gate, ffn1_w_up, ffn1_w_down, norm_mix, w_in, q_norm, k_norm, conv_w, w_attn_branch, w_conv_branch, w_out, norm_ffn2, ffn2_w_gate, ffn2_w_up, ffn2_w_down, loss_target, m_w_ada, m_b_ada, m_norm_ffn1, m_ffn1_w_gate, m_ffn1_w_up, m_ffn1_w_down, m_norm_mix, m_w_in, m_q_norm, m_k_norm, m_conv_w, m_w_attn_branch, m_w_conv_branch, m_w_out, m_norm_ffn2, m_ffn2_w_gate, m_ffn2_w_up, m_ffn2_w_down, v_w_ada, v_b_ada, v_norm_ffn1, v_ffn1_w_gate, v_ffn1_w_up, v_ffn1_w_down, v_norm_mix, v_w_in, v_q_norm, v_k_norm, v_conv_w, v_w_attn_branch, v_w_conv_branch, v_w_out, v_norm_ffn2, v_ffn2_w_gate, v_ffn2_w_up, v_ffn2_w_down):
    ix, iy, ic = _place()
    chip = 2 * ix + iy
    me = 4 * ix + 2 * iy + ic
    xs = x[0]
    target = loss_target[0]
    s_len, d = xs.shape
    ada_cols = w_ada.shape[2]
    conv_cols = conv_w.shape[2]

    conv_rows = jnp.zeros((8, conv_cols), F32).at[0:3].set(conv_w[0])
    small_in = jnp.concatenate([jnp.broadcast_to(c, (8, d)), conv_rows], axis=1)
    small_all = _allgather8(small_in, "gather_c").reshape(8, 8, d + conv_cols)
    c_all = small_all[:, 0, :d]
    conv_full = small_all[0::2, 0:3, d:].transpose(1, 0, 2).reshape(3, N_CHIPS * conv_cols)
    conv_pad = jnp.zeros((8, N_CHIPS * conv_cols), F32).at[0:3].set(conv_full)
    b_shard = lax.dynamic_slice(b_ada, (0, chip * ada_cols), (1, ada_cols))
    mod_part = _ada_fwd(c_all, w_ada[0], b_shard, "ada_fwd")
    mod_all = _allgather8(mod_part, "gather_mod").reshape(N_CHIPS, 2, 8, ada_cols)[:, 0]
    mod_mine = lax.dynamic_slice(mod_all, (0, me, 0), (N_CHIPS, 1, ada_cols)).reshape(9, d)

    def mod_rows(i, gain):
        return jnp.zeros((8, d), F32).at[0:3].set(mod_mine[3 * i:3 * i + 3]).at[3:4].set(gain)

    mod1, mod2, mod3 = mod_rows(0, norm_ffn1), mod_rows(1, norm_mix), mod_rows(2, norm_ffn2)

    to16 = lambda w: w[0].astype(BF16)
    wg1, wu1, wd1 = _gather_weights([to16(ffn1_w_gate), to16(ffn1_w_up), to16(ffn1_w_down)], [False] * 3,
                                    "gather_ffn1", 1)
    h1, h1t = _norm_mod(xs, mod1, "norm1")
    (w_in_full,) = _gather_weights([to16(w_in)], [True], "gather_w_in", 2, after=(wd1, h1))

    g1, u1, y1 = _ffn_fwd(h1, wg1, wu1, wd1, "ffn1_fwd")
    x1, h2, h2t = _norm_mod(xs, mod2, "norm2", prev=(y1, mod1, 0.5))
    qkv, rest, qkv_hat = _in_proj(h2, w_in_full, q_norm, k_norm, "in_proj")
    w_ab, w_cb_g, w_o_g, wg2, wu2, wd2 = _gather_weights(
        [to16(w_attn_branch), to16(w_conv_branch), to16(w_out),
         to16(ffn2_w_gate), to16(ffn2_w_up), to16(ffn2_w_down)], [True] + [False] * 5,
        "gather_rest", 3, after=(h2,))
    a_w = w_ab.shape[0]
    w_cb = w_cb_g.reshape(d, d)
    w_o = w_o_g.reshape(d, d)
    o, lse = _attn_fwd(qkv_hat, "attn_fwd")
    x2, z, ya, yc, conv, yb, merged, h3, h3t = _mix_fwd(x1, o, rest, mod2, mod3, conv_pad, w_ab, w_cb, w_o, "mix_fwd")
    g3, u3, y3 = _ffn_fwd(h3, wg2, wu2, wd2, "ffn2_fwd")
    dx3, dy3, loss_part = _loss_grad(x2, y3, mod3, target, "loss")

    c_idx = jnp.reshape(ic, (1,)).astype(jnp.int32)
    chip_idx = jnp.stack([chip, ic]).astype(jnp.int32)

    def pair_send(grads, tag, collective_id):
        return _rs_pair_exchange(grads, "rs_pair_" + tag, collective_id)

    def chip_send(grads, from_sibling, names, tag, collective_id, after):
        pair_sums = [_pair_add(g, r, c_idx, "pair_add_" + nm, after) for g, r, nm in zip(grads, from_sibling, names)]
        return pair_sums, _rs_chip_exchange(pair_sums, "rs_chips_" + tag, collective_id)

    def reduce_finish(pair_sums, from_chips, names, tag, after):
        totals = [_chip_add(p, r, chip_idx, "chip_add_" + nm, after)
                  for p, r, nm in zip(pair_sums, from_chips, names)]
        return dict(zip(names, _rs_share(totals, "rs_share_" + tag)))

    names_a = ["ffn2_w_gate", "ffn2_w_up", "ffn2_w_down"]
    names_b = ["w_in", "w_attn_branch", "w_conv_branch", "w_out"]
    names_c = ["ffn1_w_gate", "ffn1_w_up", "ffn1_w_down"]

    dh3, dg3, du3, a3 = _ffn_bwd(dy3, g3, u3, wg2, wu2, wd2, "ffn2_bwd")
    grads_a = list(_ffn_wgrads(h3t, dg3, du3, a3, dy3, "ffn2"))
    sibling_a = pair_send(grads_a, "a", 7)
    dx2, st3 = _norm_bwd(dh3, x2, mod3, dx3, y3, 0.5, "norm3_bwd")
    sums_a, chips_a = chip_send(grads_a, sibling_a, names_a, "a", 4, after=(dx2,))

    do, drest, dz, dya, dyc, st_conv = _mix_bwd(dx2, ya, yc, conv, rest, mod2, conv_pad, w_ab, w_cb, w_o, a_w,
                                                "mix_bwd", after=tuple(sums_a))
    dq, dk, dv, st_qk = _attn_bwd(qkv, qkv_hat, do, o, lse, q_norm, k_norm, "attn_bwd")
    tok = lambda width: (lambda ts: pl.BlockSpec((ts, width), lambda cc, s: (s, 0)))
    colblk = lambda width: (lambda ts: pl.BlockSpec((ts, width), lambda cc, s: (s, cc)))
    tok_t = lambda ts: pl.BlockSpec((d, ts), lambda cc, s: (0, s))
    whole = pl.BlockSpec((d, QKV), lambda cc, s: (0, 0))
    dw_in = [_wgrad(h2t, part, tok_t, tok(QKV), (d, QKV), whole, (d, QKV), 1, "dw_in_" + nm, True)
             for part, nm in ((dq, "q"), (dk, "k"), (dv, "v"))]
    dw_in.append(_wgrad(h2t, drest, tok_t, colblk(d), (d, 5 * d), pl.BlockSpec((d, d), lambda cc, s: (0, cc)),
                        (d, d), 5, "dw_in_rest", True))
    dw_in = _cols_to_shards(jnp.concatenate(dw_in, axis=1), N_CHIPS)
    shard_w = d // N_CHIPS
    dw_ab = _wgrad(o, dya, tok(a_w), colblk(shard_w), (a_w, d), pl.BlockSpec((a_w, shard_w), lambda cc, s: (0, cc)),
                   (a_w, shard_w), N_CHIPS, "dw_attn_branch")
    dw_ab = _cols_to_shards(dw_ab, N_CHIPS)
    row_out = pl.BlockSpec((None, shard_w, d), lambda cc, s: (cc, 0, 0))
    dw_cb = _wgrad(yb, dyc, colblk(shard_w), tok(d), (N_CHIPS, shard_w, d), row_out, (shard_w, d), N_CHIPS, "dw_conv_branch")
    dw_o = _wgrad(merged, dz, colblk(shard_w), tok(d), (N_CHIPS, shard_w, d), row_out, (shard_w, d), N_CHIPS, "dw_out")
    shard_grads = reduce_finish(sums_a, chips_a, names_a, "a", after=(dw_in, dw_o))
    grads_b = [dw_in, dw_ab, dw_cb, dw_o]
    sibling_b = pair_send(grads_b, "b", 8)

    dh2 = _in_proj_bwd(dq, dk, dv, drest, w_in_full, "in_proj_bwd")
    sums_b, chips_b = chip_send(grads_b, sibling_b, names_b, "b", 5, after=(dh2,))
    dx1, st2, dy1 = _norm_bwd(dh2, x1, mod2, dx2, z, 1.0, "norm2_bwd", after=tuple(sums_b), prev=(mod1, 0.5))
    dh1, dg1, du1, a1 = _ffn_bwd(dy1, g1, u1, wg1, wu1, wd1, "ffn1_bwd")
    dx0, st1 = _norm_bwd(dh1, xs, mod1, dx1, y1, 0.5, "norm1_bwd")
    grads_c = list(_ffn_wgrads(h1t, dg1, du1, a1, dy1, "ffn1"))
    sibling_c = pair_send(grads_c, "c", 9)
    shard_grads.update(reduce_finish(sums_b, chips_b, names_b, "b", after=tuple(grads_c)))

    dmod = jnp.concatenate([st1[0:3], st2[0:3], st3[0:3]], axis=0).reshape(1, 9 * d)
    loss_cols = jnp.zeros((1, HEAD_DIM), F32).at[0, 0].set(jnp.sum(loss_part))
    small = jnp.concatenate([dmod, st1[3:4], st2[3:4], st3[3:4], st_qk[0:1], st_qk[1:2],
                             st_conv[0:3].reshape(1, 3 * d), loss_cols], axis=1)
    small_all = _allgather8(jnp.broadcast_to(small, (8, small.shape[1])), "gather_small").reshape(8, 8, -1)[:, 0]
    loss = (0.5 / d) * jnp.sum(small_all[:, -HEAD_DIM])
    small_all = small_all[:, :-HEAD_DIM]
    dmod_all = small_all[:, :9 * d]
    dmod_shard = lax.dynamic_slice(dmod_all, (0, chip * ada_cols), (8, ada_cols))
    g_w_ada, d_w_ada, nm_w_ada, nv_w_ada = _ada_bwd(c_all, dmod_shard, w_ada[0], m_w_ada[0], v_w_ada[0], "ada_bwd")

    vec_names = ["b_ada", "norm_ffn1", "norm_mix", "norm_ffn2", "q_norm", "k_norm"]
    vec_w = [b_ada, norm_ffn1, norm_mix, norm_ffn2, q_norm, k_norm]
    vec_m = [m_b_ada, m_norm_ffn1, m_norm_mix, m_norm_ffn2, m_q_norm, m_k_norm]
    vec_v = [v_b_ada, v_norm_ffn1, v_norm_mix, v_norm_ffn2, v_q_norm, v_k_norm]
    n_vec = sum(w.shape[1] for w in vec_w)
    cat = lambda arrs: jnp.concatenate(arrs, axis=1)
    vec_out = _small_update(small_all[:, :n_vec], cat(vec_w), cat(vec_m), cat(vec_v), "small_update")
    conv_parts = small_all[:, n_vec:].reshape(8, 3, N_CHIPS * conv_cols)
    conv_parts = lax.dynamic_slice(conv_parts, (0, 0, chip * conv_cols), (8, 3, conv_cols)).reshape(8, 3 * conv_cols)
    flat3 = lambda w: w[0].reshape(1, 3 * conv_cols)
    conv_out = _small_update(conv_parts, flat3(conv_w), flat3(m_conv_w), flat3(v_conv_w), "conv_update")

    res = {"w_ada": [t[None] for t in (g_w_ada, d_w_ada, nm_w_ada, nv_w_ada)],
           "conv_w": [t.reshape(1, 3, conv_cols) for t in conv_out]}
    off = 0
    for nm, w in zip(vec_names, vec_w):
        width = w.shape[1]
        res[nm] = [t[:, off:off + width] for t in vec_out]
        off += width
    big = {"ffn1_w_gate": (ffn1_w_gate, m_ffn1_w_gate, v_ffn1_w_gate), "ffn1_w_up": (ffn1_w_up, m_ffn1_w_up, v_ffn1_w_up),
           "ffn1_w_down": (ffn1_w_down, m_ffn1_w_down, v_ffn1_w_down), "w_in": (w_in, m_w_in, v_w_in),
           "w_attn_branch": (w_attn_branch, m_w_attn_branch, v_w_attn_branch),
           "w_conv_branch": (w_conv_branch, m_w_conv_branch, v_w_conv_branch), "w_out": (w_out, m_w_out, v_w_out),
           "ffn2_w_gate": (ffn2_w_gate, m_ffn2_w_gate, v_ffn2_w_gate), "ffn2_w_up": (ffn2_w_up, m_ffn2_w_up, v_ffn2_w_up),
           "ffn2_w_down": (ffn2_w_down, m_ffn2_w_down, v_ffn2_w_down)}
    def update(nm, after=()):
        w, m, v = big[nm]
        g, delta, new_m, new_v = _adamw(w[0], shard_grads[nm], m[0], v[0], "adamw_" + nm, after)
        res[nm] = [t[None] for t in (g, delta, new_m, new_v)]
        return new_v

    last = tuple(shard_grads[nm] for nm in names_b)
    for nm in names_a:
        last = (update(nm, last),)
    sums_c, chips_c = chip_send(grads_c, sibling_c, names_c, "c", 6, after=last)
    last = tuple(sums_c)
    for nm in names_b:
        last = (update(nm, last),)
    shard_grads.update(reduce_finish(sums_c, chips_c, names_c, "c", after=last))
    for nm in names_c:
        update(nm)

    order = ["w_ada", "b_ada", "norm_ffn1", "ffn1_w_gate", "ffn1_w_up", "ffn1_w_down", "norm_mix", "w_in", "q_norm",
             "k_norm", "conv_w", "w_attn_branch", "w_conv_branch", "w_out", "norm_ffn2", "ffn2_w_gate", "ffn2_w_up",
             "ffn2_w_down"]
    return (loss, dx0[None], *[res[nm][0] for nm in order], *[res[nm][1] for nm in order],
            *[res[nm][2] for nm in order], *[res[nm][3] for nm in order])
```

```python
import jax
import jax.numpy as jnp
from jax import lax
from jax.experimental import pallas as pl
from jax.experimental.pallas import tpu as pltpu
from jax.experimental.pallas import tpu_sc as plsc

F32 = jnp.float32
BF16 = jnp.bfloat16
MESH = pl.DeviceIdType.MESH
ANY = pl.BlockSpec(memory_space=pl.ANY)

NORM_EPS = 1e-6
HEAD_DIM = 128
N_GROUPS = 3
HEADS = 4
DILATIONS = (1, 4, 16)
ATTN_BLOCK = 128
SLAB = ATTN_BLOCK * max(DILATIONS)
QKV = N_GROUPS * HEADS * HEAD_DIM
ATTN_SCALE = HEAD_DIM ** -0.5
NEG = -1e30
N_CHIPS = 4

ADAM_LR = 0.001
ADAM_B1 = 0.9
ADAM_B2 = 0.999
ADAM_EPS = 1e-08
ADAM_WD = 0.01
ADAM_STEP = 10

VMEM_LIMIT_BYTES = 56 * 1024 * 1024
TOKEN_TILE = 512
FFN_TILE = 1024
PROJ_TILE = 2048
WGRAD_TILE = 2048
IN_BLOCK = 512
MIX_TILE = 512
ACC_PIECES = 4
ADAMW_TILE_BYTES = 3 * 512 * 1024


def _params(n_axes=0):
    return pltpu.CompilerParams(
        dimension_semantics=("arbitrary",) * n_axes if n_axes else None,
        vmem_limit_bytes=VMEM_LIMIT_BYTES)


def _dot(a, b):
    return jnp.dot(a, b, preferred_element_type=F32)


def _dot_nt(a, b):
    return lax.dot_general(a, b, (((1,), (1,)), ((), ())), preferred_element_type=F32)


def _dot_tn(a, b):
    return lax.dot_general(a, b, (((0,), (0,)), ((), ())), preferred_element_type=F32)


def _sigmoid(x):
    return 0.5 * jnp.tanh(0.5 * x) + 0.5


def _place():
    return lax.axis_index("x"), lax.axis_index("y"), lax.axis_index("c")


def _ordered(body, n_in, after):
    if not after:
        return body
    return lambda *refs: body(*refs[:n_in], *refs[n_in + len(after):])


def _allgather8(block, name):
    m_per, n = block.shape

    def body(x_ref, out_ref, send_sems, recv_sems, local_sem):
        x, y, c = _place()
        me, sibling = (x, y, c), (x, y, 1 - c)
        chips = [(1 - x, y), (x, 1 - y), (1 - x, 1 - y)]

        def rows(px, py, pc):
            return out_ref.at[pl.ds((4 * px + 2 * py + pc) * m_per, m_per), :]

        def copy(k, blk, to, src=None):
            return pltpu.make_async_remote_copy(
                src_ref=rows(*blk) if src is None else src, dst_ref=rows(*blk),
                send_sem=send_sems.at[k], recv_sem=recv_sems.at[k],
                device_id=to, device_id_type=MESH)

        mine = pltpu.make_async_copy(x_ref, rows(*me), local_sem)
        mine.start()
        first = [copy(0, me, sibling, src=x_ref)]
        first += [copy(1 + j, me, (*chip, c), src=x_ref) for j, chip in enumerate(chips)]
        for cp in first:
            cp.start()
        passed = [copy(4 + j, (*chip, c), sibling) for j, chip in enumerate(chips)]
        for j, chip in enumerate(chips):
            copy(1 + j, (*chip, c), me).wait_recv()
            passed[j].start()
        copy(0, sibling, me).wait_recv()
        for j, chip in enumerate(chips):
            copy(4 + j, (*chip, 1 - c), me).wait_recv()
        for cp in first + passed:
            cp.wait_send()
        mine.wait()

    return pl.pallas_call(
        body, name=name,
        out_shape=jax.ShapeDtypeStruct((8 * m_per, n), block.dtype),
        in_specs=[pl.BlockSpec(memory_space=pltpu.VMEM)],
        out_specs=pl.BlockSpec(memory_space=pltpu.VMEM),
        scratch_shapes=[pltpu.SemaphoreType.DMA((7,)), pltpu.SemaphoreType.DMA((7,)),
                        pltpu.SemaphoreType.DMA],
        compiler_params=_params(),
    )(block)


def _handshake(peers):
    barrier = pltpu.get_barrier_semaphore()
    for peer in peers:
        pl.semaphore_signal(barrier, inc=1, device_id=peer, device_id_type=MESH)
    pl.semaphore_wait(barrier, len(peers))


def _gather_weights(shards, by_cols, name, collective_id, after=()):
    n_arr = len(shards)

    def body(*refs):
        srcs, outs = refs[:n_arr], refs[n_arr + len(after):2 * n_arr + len(after)]
        send_sems, recv_sems, local_sems = refs[2 * n_arr + len(after):]
        x, y, c = _place()
        me_dev, sibling = (x, y, c), (x, y, 1 - c)
        chips = [(1 - x, y), (x, 1 - y), (1 - x, 1 - y)]
        me = 2 * x + y
        _handshake([sibling] + [(*chip, c) for chip in chips])

        def place(k, chip_idx, rows):
            if by_cols[k]:
                width = srcs[k].shape[1]
                return outs[k].at[rows, pl.ds(pl.multiple_of(chip_idx * width, 128), width)]
            return outs[k].at[chip_idx, rows]

        def copy(k, slot, chip_idx, half_sel, to, from_shard=False):
            half = srcs[k].shape[0] // 2
            rows = pl.ds(half_sel * half, half)
            dst = place(k, chip_idx, rows)
            return pltpu.make_async_remote_copy(
                src_ref=srcs[k].at[rows] if from_shard else dst, dst_ref=dst,
                send_sem=send_sems.at[6 * k + slot], recv_sem=recv_sems.at[6 * k + slot],
                device_id=to, device_id_type=MESH)

        own = [pltpu.make_async_copy(srcs[k], place(k, me, pl.ds(0, srcs[k].shape[0])), local_sems.at[k])
               for k in range(n_arr)]
        for cp in own:
            cp.start()
        sent = []
        for k in range(n_arr):
            for j, chip in enumerate(chips):
                sent.append(copy(k, j, me, c, (*chip, c), from_shard=True))
                sent[-1].start()
        for k in range(n_arr):
            for j, chip in enumerate(chips):
                chip_idx = 2 * chip[0] + chip[1]
                copy(k, j, chip_idx, c, me_dev).wait_recv()
                sent.append(copy(k, 3 + j, chip_idx, c, sibling))
                sent[-1].start()
        for k in range(n_arr):
            for j, chip in enumerate(chips):
                copy(k, 3 + j, 2 * chip[0] + chip[1], 1 - c, me_dev).wait_recv()
        for cp in sent:
            cp.wait_send()
        for cp in own:
            cp.wait()

    def gathered(k):
        r, cols = shards[k].shape
        return (r, N_CHIPS * cols) if by_cols[k] else (N_CHIPS, r, cols)

    return pl.kernel(
        body, name=name,
        out_type=[jax.ShapeDtypeStruct(gathered(k), shards[k].dtype) for k in range(n_arr)],
        mesh=plsc.ScalarSubcoreMesh(axis_name="sequencer", num_cores=1),
        scratch_types=[pltpu.SemaphoreType.DMA((6 * n_arr,)), pltpu.SemaphoreType.DMA((6 * n_arr,)),
                       pltpu.SemaphoreType.DMA((n_arr,))],
        compiler_params=pltpu.CompilerParams(collective_id=collective_id),
    )(*shards, *after)


def _rs_pair_exchange(grads, name, collective_id):
    n_arr = len(grads)

    def body(*refs):
        srcs, outs = refs[:n_arr], refs[n_arr:2 * n_arr]
        send_sems, recv_sems = refs[2 * n_arr:]
        x, y, c = _place()
        _handshake([(x, y, 1 - c)])
        cps = []
        for k in range(n_arr):
            half = srcs[k].shape[1] // 2
            cps.append(pltpu.make_async_remote_copy(
                src_ref=srcs[k].at[:, pl.ds((1 - c) * half, half)], dst_ref=outs[k],
                send_sem=send_sems.at[k], recv_sem=recv_sems.at[k],
                device_id=(x, y, 1 - c), device_id_type=MESH))
            cps[-1].start()
        for cp in cps:
            cp.wait_recv()
        for cp in cps:
            cp.wait_send()

    return pl.kernel(
        body, name=name,
        out_type=[jax.ShapeDtypeStruct((g.shape[0], g.shape[1] // 2, g.shape[2]), g.dtype) for g in grads],
        mesh=plsc.ScalarSubcoreMesh(axis_name="sequencer", num_cores=1),
        scratch_types=[pltpu.SemaphoreType.DMA((n_arr,)), pltpu.SemaphoreType.DMA((n_arr,))],
        compiler_params=pltpu.CompilerParams(collective_id=collective_id),
    )(*grads)


def _rs_chip_exchange(sums, name, collective_id):
    n_arr = len(sums)

    def body(*refs):
        srcs, outs = refs[:n_arr], refs[n_arr:2 * n_arr]
        send_sems, recv_sems = refs[2 * n_arr:]
        x, y, c = _place()
        chips = [(1 - x, y), (x, 1 - y), (1 - x, 1 - y)]
        _handshake([(*chip, c) for chip in chips])
        cps = []
        for k in range(n_arr):
            for j, chip in enumerate(chips):
                cps.append(pltpu.make_async_remote_copy(
                    src_ref=srcs[k].at[2 * chip[0] + chip[1]], dst_ref=outs[k].at[j],
                    send_sem=send_sems.at[3 * k + j], recv_sem=recv_sems.at[3 * k + j],
                    device_id=(*chip, c), device_id_type=MESH))
                cps[-1].start()
        for cp in cps:
            cp.wait_recv()
        for cp in cps:
            cp.wait_send()

    return pl.kernel(
        body, name=name,
        out_type=[jax.ShapeDtypeStruct((3,) + s.shape[1:], s.dtype) for s in sums],
        mesh=plsc.ScalarSubcoreMesh(axis_name="sequencer", num_cores=1),
        scratch_types=[pltpu.SemaphoreType.DMA((3 * n_arr,)), pltpu.SemaphoreType.DMA((3 * n_arr,))],
        compiler_params=pltpu.CompilerParams(collective_id=collective_id),
    )(*sums)


def _rs_share(totals, name):
    n_arr = len(totals)

    def body(*refs):
        outs = refs[n_arr:2 * n_arr]
        send_sems, recv_sems = refs[2 * n_arr:]
        x, y, c = _place()

        def half_rows(k, sel):
            return outs[k].at[sel]

        cps = []
        for k in range(n_arr):
            cps.append(pltpu.make_async_remote_copy(
                src_ref=half_rows(k, c), dst_ref=half_rows(k, c), send_sem=send_sems.at[k], recv_sem=recv_sems.at[k],
                device_id=(x, y, 1 - c), device_id_type=MESH))
            cps[-1].start()
        for k in range(n_arr):
            pltpu.make_async_remote_copy(
                src_ref=half_rows(k, c), dst_ref=half_rows(k, 1 - c), send_sem=send_sems.at[k],
                recv_sem=recv_sems.at[k], device_id=(x, y, 1 - c), device_id_type=MESH).wait_recv()
        for cp in cps:
            cp.wait_send()

    shared = pl.pallas_call(
        body, name=name,
        out_shape=[jax.ShapeDtypeStruct(t.shape, t.dtype) for t in totals],
        in_specs=[ANY] * n_arr, out_specs=[ANY] * n_arr,
        input_output_aliases={k: k for k in range(n_arr)},
        scratch_shapes=[pltpu.SemaphoreType.DMA((n_arr,)), pltpu.SemaphoreType.DMA((n_arr,))],
        compiler_params=_params(),
    )(*totals)
    return [t.reshape(2 * t.shape[1], t.shape[2]) for t in shared]


def _pair_add(grad, recv, c_idx, name, after=()):
    n, r, cols = grad.shape
    half = r // 2
    rows = half // 2

    def body(_, g_ref, r_ref, o_ref):
        o_ref[...] = (g_ref[...].astype(F32) + r_ref[...].astype(F32)).astype(o_ref.dtype)

    return pl.pallas_call(
        _ordered(body, 3, after), name=name,
        grid_spec=pltpu.PrefetchScalarGridSpec(
            num_scalar_prefetch=1, grid=(n, 2),
            in_specs=[pl.BlockSpec((None, None, rows, cols), lambda s, i, ci: (s, ci[0], i, 0)),
                      pl.BlockSpec((None, rows, cols), lambda s, i, ci: (s, i, 0))] + [ANY] * len(after),
            out_specs=pl.BlockSpec((None, rows, cols), lambda s, i, ci: (s, i, 0))),
        out_shape=jax.ShapeDtypeStruct((n, half, cols), BF16),
        compiler_params=_params(2),
    )(c_idx, grad.reshape(n, 2, half, cols), recv, *after)


def _chip_add(sums, recv, chip_and_core, name, after=()):
    _, half, cols = sums.shape
    rows = half // 2

    def body(_, s_ref, r0_ref, r1_ref, r2_ref, o_ref):
        o_ref[...] = ((s_ref[...].astype(F32) + r0_ref[...].astype(F32))
                      + r1_ref[...].astype(F32)) + r2_ref[...].astype(F32)

    def recv_spec(j):
        return pl.BlockSpec((None, rows, cols), lambda i, ci: (j, i, 0))

    return pl.pallas_call(
        _ordered(body, 5, after), name=name,
        grid_spec=pltpu.PrefetchScalarGridSpec(
            num_scalar_prefetch=1, grid=(2,),
            in_specs=[pl.BlockSpec((None, rows, cols), lambda i, ci: (ci[0], i, 0)),
                      recv_spec(0), recv_spec(1), recv_spec(2)] + [ANY] * len(after),
            out_specs=pl.BlockSpec((None, rows, cols), lambda i, ci: (ci[1], i, 0))),
        out_shape=jax.ShapeDtypeStruct((2, half, cols), F32),
        compiler_params=_params(1),
    )(chip_and_core, sums, recv, recv, recv, *after)


def _rms(x):
    return lax.rsqrt(jnp.mean(x * x, axis=-1, keepdims=True) + NORM_EPS)


def _norm_mod(x, mod, name, prev=None):
    s_len, d = x.shape
    tm = TOKEN_TILE

    def body(*refs):
        if prev is None:
            x_ref, mod_ref, h_ref, ht_ref = refs
            xv = x_ref[...]
        else:
            x_ref, y_ref, modp_ref, mod_ref, xo_ref, h_ref, ht_ref = refs
            xv = x_ref[...] + prev[2] * modp_ref[2:3, :] * y_ref[...]
            xo_ref[...] = xv
        n = (xv * _rms(xv)) * mod_ref[3:4, :]
        h = n * (1.0 + mod_ref[1:2, :]) + mod_ref[0:1, :]
        h_ref[...] = h.astype(BF16)
        ht_ref[...] = h.T.astype(BF16)

    tile = pl.BlockSpec((tm, d), lambda i: (i, 0))
    small = pl.BlockSpec((8, d), lambda i: (0, 0))
    h_specs = [tile, pl.BlockSpec((d, tm), lambda i: (0, i))]
    h_shapes = [jax.ShapeDtypeStruct((s_len, d), BF16), jax.ShapeDtypeStruct((d, s_len), BF16)]
    if prev is None:
        return pl.pallas_call(
            body, name=name, grid=(s_len // tm,), in_specs=[tile, small], out_specs=h_specs, out_shape=h_shapes,
            compiler_params=_params(1))(x, mod)
    return pl.pallas_call(
        body, name=name, grid=(s_len // tm,), in_specs=[tile, tile, small, small],
        out_specs=[tile] + h_specs, out_shape=[jax.ShapeDtypeStruct((s_len, d), F32)] + h_shapes,
        compiler_params=_params(1))(x, prev[0], prev[1], mod)


def _norm_bwd(dh, x, mod, dxo, y_raw, coef, name, after=(), prev=None):
    s_len, d = x.shape
    tm = TOKEN_TILE

    def body(*refs):
        if prev is None:
            dh_ref, x_ref, mod_ref, dxo_ref, y_ref, dx_ref, st_ref = refs
        else:
            dh_ref, x_ref, mod_ref, dxo_ref, y_ref, modp_ref, dx_ref, st_ref, dyp_ref = refs

        @pl.when(pl.program_id(0) == 0)
        def _():
            st_ref[...] = jnp.zeros_like(st_ref)

        xv, dhv, dxov = x_ref[...], dh_ref[...], dxo_ref[...]
        r = _rms(xv)
        xh = xv * r
        gain, scale = mod_ref[3:4, :], mod_ref[1:2, :]
        dn = dhv * (1.0 + scale)
        dxh = dn * gain
        dx = dxov + r * (dxh - xh * jnp.mean(dxh * xh, axis=-1, keepdims=True))
        dx_ref[...] = dx
        if prev is not None:
            dyp_ref[...] = (prev[1] * modp_ref[2:3, :] * dx).astype(BF16)
        st_ref[0:1, :] += jnp.sum(dhv, axis=0, keepdims=True)
        st_ref[1:2, :] += jnp.sum(dhv * (xh * gain), axis=0, keepdims=True)
        st_ref[2:3, :] += coef * jnp.sum(y_ref[...].astype(F32) * dxov, axis=0, keepdims=True)
        st_ref[3:4, :] += jnp.sum(dn * xh, axis=0, keepdims=True)

    tile = pl.BlockSpec((tm, d), lambda i: (i, 0))
    small = pl.BlockSpec((8, d), lambda i: (0, 0))
    operands = [dh, x, mod, dxo, y_raw] + ([] if prev is None else [prev[0]])
    in_specs = [tile, tile, small, tile, tile] + ([] if prev is None else [small])
    out_specs = [tile, small] + ([] if prev is None else [tile])
    out_shape = [jax.ShapeDtypeStruct((s_len, d), F32), jax.ShapeDtypeStruct((8, d), F32)]
    if prev is not None:
        out_shape.append(jax.ShapeDtypeStruct((s_len, d), BF16))
    return pl.pallas_call(
        _ordered(body, len(operands), after), name=name, grid=(s_len // tm,),
        in_specs=in_specs + [ANY] * len(after), out_specs=out_specs, out_shape=out_shape,
        compiler_params=_params(1),
    )(*operands, *after)


def _loss_grad(x, y, mod, target, name):
    s_len, d = x.shape
    tm = TOKEN_TILE

    def body(x_ref, y_ref, mod_ref, t_ref, do_ref, dy_ref, part_ref):
        @pl.when(pl.program_id(0) == 0)
        def _():
            part_ref[...] = jnp.zeros_like(part_ref)

        half_gate = 0.5 * mod_ref[2:3, :]
        err = (x_ref[...] + half_gate * y_ref[...]) - t_ref[...]
        do = err * (1.0 / d)
        do_ref[...] = do
        dy_ref[...] = (half_gate * do).astype(BF16)
        sq = err * err
        part_ref[...] += jnp.sum(sq.reshape(tm // 8, 8, d), axis=0)

    tile = pl.BlockSpec((tm, d), lambda i: (i, 0))
    small = pl.BlockSpec((8, d), lambda i: (0, 0))
    return pl.pallas_call(
        body, name=name, grid=(s_len // tm,),
        in_specs=[tile, tile, small, tile],
        out_specs=[tile, tile, small],
        out_shape=[jax.ShapeDtypeStruct((s_len, d), F32), jax.ShapeDtypeStruct((s_len, d), BF16),
                   jax.ShapeDtypeStruct((8, d), F32)],
        compiler_params=_params(1),
    )(x, y, mod, target)


def _adamw_math(w, g, m, v):
    m = ADAM_B1 * m + (1.0 - ADAM_B1) * g
    v = ADAM_B2 * v + (1.0 - ADAM_B2) * (g * g)
    m_hat = m / (1.0 - ADAM_B1 ** ADAM_STEP)
    v_hat = v / (1.0 - ADAM_B2 ** ADAM_STEP)
    delta = -ADAM_LR * (m_hat / (jnp.sqrt(v_hat) + ADAM_EPS) + ADAM_WD * w)
    return delta, m, v


def _adamw(w, g, m, v, name, after=()):
    r, cols = w.shape
    tr = max([t for t in (r // k for k in (1, 2, 4, 8, 16)) if t % 8 == 0 and r % t == 0
              and t * cols * 4 <= ADAMW_TILE_BYTES] or [r])

    def body(w_ref, g_ref, m_ref, v_ref, go_ref, d_ref, nm_ref, nv_ref):
        gv = g_ref[...]
        go_ref[...] = gv
        d_ref[...], nm_ref[...], nv_ref[...] = _adamw_math(w_ref[...], gv, m_ref[...], v_ref[...])

    tile = pl.BlockSpec((tr, cols), lambda i: (i, 0))
    shape = jax.ShapeDtypeStruct((r, cols), F32)
    return pl.pallas_call(
        _ordered(body, 4, after), name=name, grid=(r // tr,),
        in_specs=[tile] * 4 + [ANY] * len(after), out_specs=[tile] * 4, out_shape=[shape] * 4,
        compiler_params=_params(1),
    )(w, g, m, v, *after)


def _in_parts(tm, n_qkv, n_rest):
    def part(lo, n_blk):
        return pl.BlockSpec((tm, IN_BLOCK), lambda i, j: (i, jnp.clip(j - lo, 0, n_blk - 1)))
    return [part(0, n_qkv), part(n_qkv, n_qkv), part(2 * n_qkv, n_qkv), part(3 * n_qkv, n_rest)]


def _pick_part(j, n_qkv, refs, fn):
    bounds = [0, n_qkv, 2 * n_qkv, 3 * n_qkv]
    for p, ref in enumerate(refs):
        inside = j >= bounds[p]
        if p + 1 < len(refs):
            inside = inside & (j < bounds[p + 1])
        pl.when(inside)(lambda ref=ref: fn(ref))


def _rows(base, count, stride):
    return pl.ds(base, count) if stride == 1 else pl.ds(base, count, stride=stride)


REORDER_STRIDE = 4


def _reorder_plan(dil, parts=1):
    inner = min(dil, REORDER_STRIDE)
    return inner, dil // inner, SLAB // parts // inner, SLAB // dil


def _to_residue_order(dst, src, dil, tmp, part=0, parts=1):
    inner, outer, big, seg = _reorder_plan(dil, parts)
    piece = seg // parts
    if outer == 1:
        for r in range(dil):
            dst[pl.ds(r * seg + part * piece, piece), :] = src[_rows(r, piece, dil), :].astype(dst.dtype)
        return
    for b in range(inner):
        tmp[pl.ds(b * big, big), :] = src[_rows(b, big, inner), :]
    for a in range(outer):
        for b in range(inner):
            dst[pl.ds((inner * a + b) * seg + part * piece, piece), :] = (
                tmp[_rows(b * big + a, piece, outer), :].astype(dst.dtype))


def _to_token_order(dst, src, dil, tmp):
    inner, outer, big, seg = _reorder_plan(dil)
    if outer == 1:
        for r in range(dil):
            dst[_rows(r, seg, dil), :] = src[pl.ds(r * seg, seg), :]
        return
    for a in range(outer):
        for b in range(inner):
            tmp[_rows(b * big + a, seg, outer), :] = src[pl.ds((inner * a + b) * seg, seg), :]
    for b in range(inner):
        dst[_rows(b, big, inner), :] = tmp[pl.ds(b * big, big), :]


def _in_proj(h, w, q_norm, k_norm, name):
    s_len, d = h.shape
    tm = PROJ_TILE
    assert tm == SLAB and IN_BLOCK == HEADS * HEAD_DIM
    steps = w.shape[1] // IN_BLOCK
    n_qkv = 3 * QKV // IN_BLOCK
    parts = 4
    rows = [pl.ds(p * (tm // parts), tm // parts) for p in range(parts)]

    gains = jnp.concatenate([q_norm, k_norm, jnp.ones((6, HEAD_DIM), F32)], axis=0)

    def body(h_ref, w_ref, gains_ref, qkv_ref, rest_ref, hat_ref, tok_s, tmp_s):
        j = pl.program_id(1)
        sect = j // N_GROUPS
        multiply = lambda p: _dot(h_ref[rows[p], :], w_ref[...])

        def emit(gi):
            dil = DILATIONS[gi]
            res = [multiply(p) for p in range(parts)]
            gain = gains_ref[pl.ds(sect, 1), :]
            plain = sect == 2
            for p in range(parts):
                qkv_ref[rows[p], :] = res[p]
                for hh in range(HEADS):
                    cols = slice(hh * HEAD_DIM, (hh + 1) * HEAD_DIM)
                    x = res[p][:, cols]
                    tok_s[...] = (x * jnp.where(plain, 1.0, _rms(x))) * gain
                    _to_residue_order(hat_ref.at[:, cols], tok_s, dil, tmp_s, p, parts)

        for gi in range(N_GROUPS):
            pl.when((j < n_qkv) & (j % N_GROUPS == gi))(lambda gi=gi: emit(gi))

        @pl.when(j >= n_qkv)
        def _():
            for p in range(parts):
                rest_ref[rows[p], :] = multiply(p).astype(BF16)

    qkv_blk = pl.BlockSpec((tm, IN_BLOCK), lambda i, j: (i, jnp.minimum(j, n_qkv - 1)))
    return pl.pallas_call(
        body, name=name, grid=(s_len // tm, steps),
        in_specs=[pl.BlockSpec((tm, d), lambda i, j: (i, 0)), pl.BlockSpec((d, IN_BLOCK), lambda i, j: (0, j)),
                  pl.BlockSpec((8, HEAD_DIM), lambda i, j: (0, 0))],
        out_specs=[qkv_blk, pl.BlockSpec((tm, IN_BLOCK), lambda i, j: (i, jnp.maximum(j - n_qkv, 0))), qkv_blk],
        out_shape=[jax.ShapeDtypeStruct((s_len, 3 * QKV), F32),
                   jax.ShapeDtypeStruct((s_len, w.shape[1] - 3 * QKV), BF16),
                   jax.ShapeDtypeStruct((s_len, 3 * QKV), BF16)],
        scratch_shapes=[pltpu.VMEM((tm // parts, HEAD_DIM), F32)] * 2,
        compiler_params=_params(2),
    )(h, w, gains)


def _in_proj_bwd(dq, dk, dv, drest, w, name, after=()):
    s_len = dq.shape[0]
    d = w.shape[0]
    tm = PROJ_TILE
    steps = w.shape[1] // IN_BLOCK
    n_qkv = QKV // IN_BLOCK

    def body(dq_ref, dk_ref, dv_ref, dr_ref, w_ref, o_ref, acc_ref):
        j = pl.program_id(1)

        @pl.when(j == 0)
        def _():
            acc_ref[...] = jnp.zeros_like(acc_ref)

        def add(a_ref):
            rows = [pl.ds(p * (tm // ACC_PIECES), tm // ACC_PIECES) for p in range(ACC_PIECES)]
            products = [_dot_nt(a_ref[r, :], w_ref[...]) for r in rows]
            for r, product in zip(rows, products):
                acc_ref[r, :] += product

        _pick_part(j, n_qkv, [dq_ref, dk_ref, dv_ref, dr_ref], add)

        @pl.when(j == steps - 1)
        def _():
            o_ref[...] = acc_ref[...]

    return pl.pallas_call(
        _ordered(body, 5, after), name=name, grid=(s_len // tm, steps),
        in_specs=(_in_parts(tm, n_qkv, steps - 3 * n_qkv) + [pl.BlockSpec((d, IN_BLOCK), lambda i, j: (0, j))]
                  + [ANY] * len(after)),
        out_specs=pl.BlockSpec((tm, d), lambda i, j: (i, 0)),
        out_shape=jax.ShapeDtypeStruct((s_len, d), F32),
        scratch_shapes=[pltpu.VMEM((tm, d), F32)],
        compiler_params=_params(2),
    )(dq, dk, dv, drest, w, *after)


def _wgrad(x, y, x_spec, y_spec, out_shape, out_spec, acc_shape, n_chunks, name, x_transposed=False, after=()):
    s_len = y.shape[-2]
    ts = WGRAD_TILE
    steps = s_len // ts

    def body(x_ref, y_ref, o_ref, acc_ref):
        s = pl.program_id(1)

        @pl.when(s == 0)
        def _():
            acc_ref[...] = jnp.zeros_like(acc_ref)

        if x_transposed:
            n_rows = acc_shape[0]
            rows = [pl.ds(p * (n_rows // ACC_PIECES), n_rows // ACC_PIECES) for p in range(ACC_PIECES)]
            products = [_dot(x_ref[r, :], y_ref[...]) for r in rows]
            for r, product in zip(rows, products):
                acc_ref[r, :] += product
        else:
            cols = _pieces(acc_shape[1])
            products = [_dot_tn(x_ref[...], y_ref[:, c]) for c in cols]
            for c, product in zip(cols, products):
                acc_ref[:, c] += product

        @pl.when(s == steps - 1)
        def _():
            o_ref[...] = acc_ref[...].astype(o_ref.dtype)

    return pl.pallas_call(
        _ordered(body, 2, after), name=name, grid=(n_chunks, steps),
        in_specs=[x_spec(ts), y_spec(ts)] + [ANY] * len(after), out_specs=out_spec,
        out_shape=jax.ShapeDtypeStruct(out_shape, BF16),
        scratch_shapes=[pltpu.VMEM(acc_shape, F32)],
        compiler_params=_params(2),
    )(x, y, *after)


def _pieces(width, piece=256):
    return [slice(a, min(a + piece, width)) for a in range(0, width, piece)]


def _ffn_fwd(h, w_gate, w_up, w_down, name):
    s_len, d = h.shape
    n_chunks, _, fs = w_gate.shape
    tm = FFN_TILE

    def body(h_ref, wg_ref, wu_ref, wd_ref, g_ref, u_ref, y_ref):
        j = pl.program_id(1)
        hv = h_ref[...]
        pieces = _pieces(fs)
        first = lambda cols: (_dot(hv, wg_ref[:, cols]), _dot(hv, wu_ref[:, cols]))
        total = None
        ahead = first(pieces[0])
        for k, cols in enumerate(pieces):
            g, u = ahead
            if k + 1 < len(pieces):
                ahead = first(pieces[k + 1])
            g_ref[:, cols] = g.astype(BF16)
            u_ref[:, cols] = u.astype(BF16)
            act = (g * _sigmoid(g)) * u
            part = _dot(act.astype(BF16), wd_ref[cols, :])
            total = part if total is None else total + part

        @pl.when(j == 0)
        def _():
            y_ref[...] = total

        @pl.when(j > 0)
        def _():
            y_ref[...] += total

    tile = pl.BlockSpec((tm, d), lambda i, j: (i, 0))
    hid = pl.BlockSpec((None, tm, fs), lambda i, j: (j, i, 0))
    w_in_spec = pl.BlockSpec((None, d, fs), lambda i, j: (j, 0, 0))
    hid_shape = jax.ShapeDtypeStruct((n_chunks, s_len, fs), BF16)
    return pl.pallas_call(
        body, name=name, grid=(s_len // tm, n_chunks),
        in_specs=[tile, w_in_spec, w_in_spec, pl.BlockSpec((None, fs, d), lambda i, j: (j, 0, 0))],
        out_specs=[hid, hid, tile],
        out_shape=[hid_shape, hid_shape, jax.ShapeDtypeStruct((s_len, d), F32)],
        compiler_params=_params(2),
    )(h, w_gate, w_up, w_down)


def _ffn_bwd(dy, g_pre, u_pre, w_gate, w_up, w_down, name):
    s_len, d = dy.shape
    n_chunks, _, fs = w_gate.shape
    tm = FFN_TILE

    def body(dy_ref, g_ref, u_ref, wg_ref, wu_ref, wd_ref, dh_ref, dg_ref, du_ref, a_ref):
        j = pl.program_id(1)
        dyv = dy_ref[...]
        pieces = _pieces(fs)
        first = lambda cols: _dot_nt(dyv, wd_ref[cols, :])
        total = None
        ahead = first(pieces[0])
        for k, cols in enumerate(pieces):
            da = ahead
            if k + 1 < len(pieces):
                ahead = first(pieces[k + 1])
            g = g_ref[:, cols].astype(F32)
            u = u_ref[:, cols].astype(F32)
            sg = _sigmoid(g)
            silu = g * sg
            dg = (da * u * (sg * (1.0 + g * (1.0 - sg)))).astype(BF16)
            du = (da * silu).astype(BF16)
            dg_ref[:, cols] = dg
            du_ref[:, cols] = du
            a_ref[:, cols] = (silu * u).astype(BF16)
            part = _dot_nt(dg, wg_ref[:, cols]) + _dot_nt(du, wu_ref[:, cols])
            total = part if total is None else total + part

        @pl.when(j == 0)
        def _():
            dh_ref[...] = total

        @pl.when(j > 0)
        def _():
            dh_ref[...] += total

    tile = pl.BlockSpec((tm, d), lambda i, j: (i, 0))
    hid = pl.BlockSpec((None, tm, fs), lambda i, j: (j, i, 0))
    w_in_spec = pl.BlockSpec((None, d, fs), lambda i, j: (j, 0, 0))
    hid_shape = jax.ShapeDtypeStruct((n_chunks, s_len, fs), BF16)
    return pl.pallas_call(
        body, name=name, grid=(s_len // tm, n_chunks),
        in_specs=[tile, hid, hid, w_in_spec, w_in_spec, pl.BlockSpec((None, fs, d), lambda i, j: (j, 0, 0))],
        out_specs=[tile, hid, hid, hid],
        out_shape=[jax.ShapeDtypeStruct((s_len, d), F32), hid_shape, hid_shape, hid_shape],
        compiler_params=_params(2),
    )(dy, g_pre, u_pre, w_gate, w_up, w_down)


def _ffn_wgrads(ht, dg, du, act, dy, tag, after=()):
    n_chunks, s_len, fs = dg.shape
    d = ht.shape[0]
    tok = lambda ts: pl.BlockSpec((ts, d), lambda c, s: (s, 0))
    tok_t = lambda ts: pl.BlockSpec((d, ts), lambda c, s: (0, s))
    hid = lambda ts: pl.BlockSpec((None, ts, fs), lambda c, s: (c, s, 0))
    d_up = pl.BlockSpec((None, d, fs), lambda c, s: (c, 0, 0))
    d_down = pl.BlockSpec((None, fs, d), lambda c, s: (c, 0, 0))
    dwg = _wgrad(ht, dg, tok_t, hid, (n_chunks, d, fs), d_up, (d, fs), n_chunks, tag + "_dwg", True, after)
    dwu = _wgrad(ht, du, tok_t, hid, (n_chunks, d, fs), d_up, (d, fs), n_chunks, tag + "_dwu", True, after)
    dwd = _wgrad(act, dy, hid, tok, (n_chunks, fs, d), d_down, (fs, d), n_chunks, tag + "_dwd", False, after)
    return dwg, dwu, dwd


def _band_bias():
    qi = lax.broadcasted_iota(jnp.int32, (ATTN_BLOCK, 2 * ATTN_BLOCK), 0)
    kj = lax.broadcasted_iota(jnp.int32, (ATTN_BLOCK, 2 * ATTN_BLOCK), 1)
    band = (kj >= qi) & (kj <= qi + ATTN_BLOCK)
    return jnp.where(band, 0.0, NEG), jnp.where(band & (kj >= ATTN_BLOCK), 0.0, NEG)


def _qkv_specs(slab_of, sections):
    def spec(sect, back):
        return pl.BlockSpec((SLAB, HEAD_DIM),
                            lambda h, s, g: (jnp.maximum(slab_of(s) - back, 0), (sect * N_GROUPS + g) * HEADS + h))
    return [spec(sect, back) for sect, back in sections]


HAT_BLOCKS = [(0, 0), (1, 0), (2, 0), (1, 1), (2, 1)]


def _stage_keys(k_ref, v_ref, kp_ref, vp_ref, kbuf, vbuf, dil, n):
    run = SLAB // dil
    for r in range(dil):
        own, before = pl.ds(r * run, run), pl.ds(2 * r * run, run)
        kbuf[pl.ds((2 * r + 1) * run, run), :] = k_ref[own, :]
        vbuf[pl.ds((2 * r + 1) * run, run), :] = v_ref[own, :]

        @pl.when(n > 0)
        def _():
            kbuf[before, :] = kp_ref[own, :]
            vbuf[before, :] = vp_ref[own, :]

        @pl.when(n == 0)
        def _():
            kbuf[before, :] = jnp.zeros((run, HEAD_DIM), BF16)
            vbuf[before, :] = jnp.zeros((run, HEAD_DIM), BF16)


def _for_each_tile(dil, n, first_fn, rest_fn):
    run = SLAB // dil
    bias, first_bias = _band_bias()
    tiles = []
    for jj in range(run // ATTN_BLOCK):
        start = jj * ATTN_BLOCK
        tile_bias = jnp.where(n == 0, first_bias, bias) if jj == 0 else bias
        for r in range(dil):
            tiles.append((pl.ds(r * run + start, ATTN_BLOCK),
                          pl.ds((2 * r + 1) * run - ATTN_BLOCK + start, 2 * ATTN_BLOCK), tile_bias))
    ahead = first_fn(*tiles[0])
    for t, tile in enumerate(tiles):
        begun = ahead
        if t + 1 < len(tiles):
            ahead = first_fn(*tiles[t + 1])
        rest_fn(*tile, begun)


def _attn_fwd(hat, name):
    s_len = hat.shape[0]
    e = HEAD_DIM
    n_slabs = s_len // SLAB

    def body(q_ref, k_ref, v_ref, kp_ref, vp_ref, o_ref, lse_ref, kbuf, vbuf, m_s, l_s, acc_s, m_p, l_p, acc_p, tmp_s):
        n, grp = pl.program_id(1), pl.program_id(2)

        def run(gi, dil):
            _stage_keys(k_ref, v_ref, kp_ref, vp_ref, kbuf, vbuf, dil, n)

            def scores(q_rows, kv_rows, bias):
                return _dot_nt(q_ref[q_rows, :], kbuf[kv_rows, :])

            def rest(q_rows, kv_rows, bias, qk):
                s = qk * ATTN_SCALE + bias
                m = jnp.max(s, axis=-1, keepdims=True)
                p = jnp.exp(s - m)
                m_p[q_rows, :] = jnp.broadcast_to(m, (ATTN_BLOCK, e))
                l_p[q_rows, :] = jnp.broadcast_to(jnp.sum(p, axis=-1, keepdims=True), (ATTN_BLOCK, e))
                acc_p[q_rows, :] = _dot(p.astype(BF16), vbuf[kv_rows, :])

            _for_each_tile(dil, n, scores, rest)
            _to_token_order(m_s.at[gi], m_p, dil, tmp_s)
            _to_token_order(l_s.at[gi], l_p, dil, tmp_s)
            _to_token_order(acc_s.at[gi], acc_p, dil, tmp_s)

        for gi, dil in enumerate(DILATIONS):
            pl.when(grp == gi)(lambda gi=gi, dil=dil: run(gi, dil))

        @pl.when(grp == N_GROUPS - 1)
        def _():
            m_all = jnp.maximum(jnp.maximum(m_s[0], m_s[1]), m_s[2])
            den = jnp.zeros((SLAB, e), F32)
            num = jnp.zeros((SLAB, e), F32)
            for gi in range(N_GROUPS):
                w = jnp.exp(m_s[gi] - m_all)
                den += l_s[gi] * w
                num += acc_s[gi] * w
            o_ref[...] = (num / den).astype(BF16)
            lse_ref[...] = m_all + jnp.log(den)

    out = pl.BlockSpec((SLAB, e), lambda h, n, g: (n, h))
    return pl.pallas_call(
        body, name=name, grid=(HEADS, n_slabs, N_GROUPS),
        in_specs=_qkv_specs(lambda n: n, HAT_BLOCKS),
        out_specs=[out, out],
        out_shape=[jax.ShapeDtypeStruct((s_len, HEADS * e), BF16), jax.ShapeDtypeStruct((s_len, HEADS * e), F32)],
        scratch_shapes=[pltpu.VMEM((2 * SLAB, e), BF16), pltpu.VMEM((2 * SLAB, e), BF16),
                        pltpu.VMEM((N_GROUPS, SLAB, e), F32), pltpu.VMEM((N_GROUPS, SLAB, e), F32),
                        pltpu.VMEM((N_GROUPS, SLAB, e), F32)]
        + [pltpu.VMEM((SLAB, e), F32)] * 4,
        compiler_params=_params(3),
    )(hat, hat, hat, hat, hat)


def _attn_bwd(qkv, hat, d_out, out, lse, q_norm, k_norm, name):
    s_len = qkv.shape[0]
    e = HEAD_DIM
    n_slabs = s_len // SLAB

    def body(q_ref, k_ref, v_ref, kp_ref, vp_ref, qraw_ref, kraw_ref, do_ref, o_ref, lse_ref, qn_ref, kn_ref,
             dq_ref, dk_ref, dv_ref, st_ref, kbuf, vbuf, stat_s, dqs, dkb, dvb, dk_tok, dv_tok, carry,
             do_p, stat_p, dq_p, dk_p, dv_p, tmp_s, do16_p):
        head, step, grp = pl.program_id(0), pl.program_id(1), pl.program_id(2)
        n = n_slabs - 1 - step
        dkb[...] = jnp.zeros_like(dkb)
        dvb[...] = jnp.zeros_like(dvb)
        @pl.when(grp == 0)
        def _():
            lane = lax.broadcasted_iota(jnp.int32, (SLAB, e), 1)
            stat_s[...] = jnp.where(lane < e // 2, lse_ref[...],
                                    jnp.sum(do_ref[...] * o_ref[...].astype(F32), axis=-1, keepdims=True))

        @pl.when((head == 0) & (step == 0) & (grp == 0))
        def _():
            st_ref[...] = jnp.zeros_like(st_ref)

        def run(gi, dil):
            seg = SLAB // dil
            _stage_keys(k_ref, v_ref, kp_ref, vp_ref, kbuf, vbuf, dil, n)

            @pl.when(step == 0)
            def _():
                carry[gi] = jnp.zeros((2, SLAB, e), F32)

            _to_residue_order(do_p, do_ref, dil, tmp_s)
            do16_p[...] = do_p[...].astype(BF16)
            _to_residue_order(stat_p, stat_s, dil, tmp_s)

            def scores(q_rows, kv_rows, bias):
                return _dot_nt(q_ref[q_rows, :], kbuf[kv_rows, :]), _dot_nt(do16_p[q_rows, :], vbuf[kv_rows, :])

            def rest(q_rows, kv_rows, bias, begun):
                qk, dp = begun
                q = q_ref[q_rows, :]
                k = kbuf[kv_rows, :]
                stat = stat_p[q_rows, :]
                p = jnp.exp(qk * ATTN_SCALE + bias - stat[:, 0:1])
                ds = (p * (dp - stat[:, e // 2:e // 2 + 1]) * ATTN_SCALE).astype(BF16)
                dq_p[q_rows, :] = _dot(ds, k)
                dkb[kv_rows, :] += _dot_tn(ds, q)
                dvb[kv_rows, :] += _dot_tn(p.astype(BF16), do16_p[q_rows, :])

            _for_each_tile(dil, n, scores, rest)
            for r in range(dil):
                own, before = pl.ds((2 * r + 1) * seg, seg), pl.ds(2 * r * seg, seg)
                kept = pl.ds(r * seg, seg)
                dk_p[kept, :] = dkb[own, :] + carry.at[gi, 0][kept, :]
                dv_p[kept, :] = dvb[own, :] + carry.at[gi, 1][kept, :]
                carry.at[gi, 0][kept, :] = dkb[before, :]
                carry.at[gi, 1][kept, :] = dvb[before, :]
            _to_token_order(dqs, dq_p, dil, tmp_s)
            _to_token_order(dk_tok, dk_p, dil, tmp_s)
            _to_token_order(dv_tok, dv_p, dil, tmp_s)

            def norm_bwd(raw, gain, d_hat):
                r = _rms(raw)
                y = raw * r
                dy = d_hat * gain
                return r * (dy - y * jnp.mean(dy * y, axis=-1, keepdims=True)), jnp.sum(d_hat * y, axis=0, keepdims=True)

            dq, dqn = norm_bwd(qraw_ref[...], qn_ref[...], dqs[...])
            dk, dkn = norm_bwd(kraw_ref[...], kn_ref[...], dk_tok[...])
            dq_ref[...] = dq.astype(BF16)
            dk_ref[...] = dk.astype(BF16)
            dv_ref[...] = dv_tok[...].astype(BF16)
            st_ref[0:1, :] += dqn
            st_ref[1:2, :] += dkn

        for gi, dil in enumerate(DILATIONS):
            pl.when(grp == gi)(lambda gi=gi, dil=dil: run(gi, dil))

    slab_of = lambda s: n_slabs - 1 - s
    small = pl.BlockSpec((1, e), lambda h, s, g: (0, 0))
    head_blk = pl.BlockSpec((SLAB, e), lambda h, s, g: (slab_of(s), h))
    grad_blk = pl.BlockSpec((SLAB, e), lambda h, s, g: (slab_of(s), g * HEADS + h))
    grad_shape = jax.ShapeDtypeStruct((s_len, QKV), BF16)
    return pl.pallas_call(
        body, name=name, grid=(HEADS, n_slabs, N_GROUPS),
        in_specs=(_qkv_specs(slab_of, HAT_BLOCKS) + _qkv_specs(slab_of, [(0, 0), (1, 0)])
                  + [head_blk, head_blk, head_blk, small, small]),
        out_specs=[grad_blk, grad_blk, grad_blk, pl.BlockSpec((8, e), lambda h, s, g: (0, 0))],
        out_shape=[grad_shape, grad_shape, grad_shape, jax.ShapeDtypeStruct((8, e), F32)],
        scratch_shapes=[pltpu.VMEM((2 * SLAB, e), BF16), pltpu.VMEM((2 * SLAB, e), BF16), pltpu.VMEM((SLAB, e), F32),
                        pltpu.VMEM((SLAB, e), F32), pltpu.VMEM((2 * SLAB, e), F32), pltpu.VMEM((2 * SLAB, e), F32),
                        pltpu.VMEM((SLAB, e), F32), pltpu.VMEM((SLAB, e), F32),
                        pltpu.VMEM((N_GROUPS, 2, SLAB, e), F32)]
        + [pltpu.VMEM((SLAB, e), F32)] * 6 + [pltpu.VMEM((SLAB, e), BF16)],
        compiler_params=_params(3),
    )(hat, hat, hat, hat, hat, qkv, qkv, d_out, out, lse, q_norm, k_norm)


def _shift_rows(x, by, edge, forward):
    t_len = x.shape[0]
    row = lax.broadcasted_iota(jnp.int32, x.shape, 0)
    if forward:
        out = pltpu.roll(x, by, 0)
        for i in range(by):
            out = jnp.where(row == i, edge[8 - by + i:8 - by + i + 1, :], out)
    else:
        out = pltpu.roll(x, t_len - by, 0)
        for i in range(by):
            out = jnp.where(row == t_len - by + i, edge[i:i + 1, :], out)
    return out


def _mix_fwd(x, o, rest, mod, mod_next, conv_w, w_attn, w_conv, w_out, name):
    s_len, d = x.shape
    tm = MIX_TILE
    a_w = o.shape[1]

    def body(x_ref, o_ref, u_ref, b_ref, c_ref, ga_ref, gc_ref, mod_ref, modn_ref, cw_ref, wa_ref, wc_ref, wo_ref,
             xo_ref, z_ref, ya_ref, yc_ref, conv_ref, yb_ref, m_ref, h_ref, ht_ref, carry):
        @pl.when(pl.program_id(0) == 0)
        def _():
            carry[...] = jnp.zeros_like(carry)

        xc = c_ref[...].astype(F32) * u_ref[...].astype(F32)
        edge = carry[...]
        conv = (_shift_rows(xc, 2, edge, True) * cw_ref[0:1, :] + _shift_rows(xc, 1, edge, True) * cw_ref[1:2, :]
                + xc * cw_ref[2:3, :])
        carry[...] = xc[tm - 8:tm, :]
        yb = (b_ref[...].astype(F32) * conv).astype(BF16)
        ya = _dot(o_ref[...], wa_ref[...])
        yc = _dot(yb, wc_ref[...])
        merged = (_sigmoid(ga_ref[...].astype(F32)) * ya + _sigmoid(gc_ref[...].astype(F32)) * yc).astype(BF16)
        z = _dot(merged, wo_ref[...])
        xo = x_ref[...] + mod_ref[2:3, :] * z
        xo_ref[...] = xo
        hn = ((xo * _rms(xo)) * modn_ref[3:4, :]) * (1.0 + modn_ref[1:2, :]) + modn_ref[0:1, :]
        h_ref[...] = hn.astype(BF16)
        ht_ref[...] = hn.T.astype(BF16)
        z_ref[...] = z.astype(BF16)
        ya_ref[...] = ya.astype(BF16)
        yc_ref[...] = yc.astype(BF16)
        conv_ref[...] = conv.astype(BF16)
        yb_ref[...] = yb
        m_ref[...] = merged

    tile = pl.BlockSpec((tm, d), lambda i: (i, 0))
    sect = lambda k: pl.BlockSpec((tm, d), lambda i: (i, k))
    att = pl.BlockSpec((tm, a_w), lambda i: (i, 0))
    const = lambda shape: pl.BlockSpec(shape, lambda i: (0, 0))
    f32_out = jax.ShapeDtypeStruct((s_len, d), F32)
    b16_out = jax.ShapeDtypeStruct((s_len, d), BF16)
    return pl.pallas_call(
        body, name=name, grid=(s_len // tm,),
        in_specs=[tile, att, sect(0), sect(1), sect(2), sect(3), sect(4), const((8, d)), const((8, d)), const((8, d)),
                  const((a_w, d)), const((d, d)), const((d, d))],
        out_specs=[tile] * 7 + [tile, pl.BlockSpec((d, tm), lambda i: (0, i))],
        out_shape=[f32_out] + [b16_out] * 6 + [b16_out, jax.ShapeDtypeStruct((d, s_len), BF16)],
        scratch_shapes=[pltpu.VMEM((8, d), F32)],
        compiler_params=_params(1),
    )(x, o, rest, rest, rest, rest, rest, mod, mod_next, conv_w, w_attn, w_conv, w_out)


def _mix_bwd(dxo, ya, yc, conv, rest, mod, conv_w, w_attn, w_conv, w_out, a_w, name, after=()):
    s_len, d = dxo.shape
    tm = MIX_TILE
    n_tiles = s_len // tm

    def body(dxo_ref, ya_ref, yc_ref, conv_ref, u_ref, b_ref, c_ref, ga_ref, gc_ref, mod_ref, cw_ref,
             wa_ref, wc_ref, wo_ref, do_ref, drest_ref, dz_ref, dya_ref, dyc_ref, st_ref, carry):
        @pl.when(pl.program_id(0) == 0)
        def _():
            carry[...] = jnp.zeros_like(carry)
            st_ref[...] = jnp.zeros_like(st_ref)

        dz = (mod_ref[2:3, :] * dxo_ref[...]).astype(BF16)
        dz_ref[...] = dz
        dm = _dot_nt(dz, wo_ref[...])
        sa, sc = _sigmoid(ga_ref[...].astype(F32)), _sigmoid(gc_ref[...].astype(F32))
        dya = (dm * sa).astype(BF16)
        dyc = (dm * sc).astype(BF16)
        dya_ref[...] = dya
        dyc_ref[...] = dyc
        drest_ref[:, 3 * d:4 * d] = (dm * ya_ref[...].astype(F32) * (sa * (1.0 - sa))).astype(BF16)
        drest_ref[:, 4 * d:5 * d] = (dm * yc_ref[...].astype(F32) * (sc * (1.0 - sc))).astype(BF16)
        do_ref[...] = _dot_nt(dya, wa_ref[...])
        dyb = _dot_nt(dyc, wc_ref[...])
        drest_ref[:, d:2 * d] = (dyb * conv_ref[...].astype(F32)).astype(BF16)
        dconv = dyb * b_ref[...].astype(F32)
        edge = carry[...]
        sh1 = _shift_rows(dconv, 1, edge, False)
        sh2 = _shift_rows(dconv, 2, edge, False)
        carry[...] = dconv[0:8, :]
        dxc = dconv * cw_ref[2:3, :] + sh1 * cw_ref[1:2, :] + sh2 * cw_ref[0:1, :]
        u, c = u_ref[...].astype(F32), c_ref[...].astype(F32)
        xc = c * u
        drest_ref[:, 0:d] = (dxc * c).astype(BF16)
        drest_ref[:, 2 * d:3 * d] = (dxc * u).astype(BF16)
        st_ref[0:1, :] += jnp.sum(xc * sh2, axis=0, keepdims=True)
        st_ref[1:2, :] += jnp.sum(xc * sh1, axis=0, keepdims=True)
        st_ref[2:3, :] += jnp.sum(xc * dconv, axis=0, keepdims=True)

    rev = lambda i: n_tiles - 1 - i
    tile = pl.BlockSpec((tm, d), lambda i: (rev(i), 0))
    sect = lambda k: pl.BlockSpec((tm, d), lambda i: (rev(i), k))
    const = lambda shape: pl.BlockSpec(shape, lambda i: (0, 0))
    b16_out = jax.ShapeDtypeStruct((s_len, d), BF16)
    return pl.pallas_call(
        _ordered(body, 14, after), name=name, grid=(n_tiles,),
        in_specs=[tile, tile, tile, tile, sect(0), sect(1), sect(2), sect(3), sect(4), const((8, d)), const((8, d)),
                  const((a_w, d)), const((d, d)), const((d, d))] + [ANY] * len(after),
        out_specs=[pl.BlockSpec((tm, a_w), lambda i: (rev(i), 0)), pl.BlockSpec((tm, 5 * d), lambda i: (rev(i), 0)),
                   tile, tile, tile, const((8, d))],
        out_shape=[jax.ShapeDtypeStruct((s_len, a_w), F32), jax.ShapeDtypeStruct((s_len, 5 * d), BF16),
                   b16_out, b16_out, b16_out, jax.ShapeDtypeStruct((8, d), F32)],
        scratch_shapes=[pltpu.VMEM((8, d), F32)],
        compiler_params=_params(1),
    )(dxo, ya, yc, conv, rest, rest, rest, rest, rest, mod, conv_w, w_attn, w_conv, w_out, *after)


ADA_COLS = 128


def _ada_fwd(c_all, w_shard, b_shard, name):
    d, cols = w_shard.shape

    def body(c_ref, w_ref, b_ref, o_ref):
        cv = c_ref[...]
        o_ref[...] = jnp.dot(cv * _sigmoid(cv), w_ref[...], preferred_element_type=F32,
                             precision=lax.Precision.HIGHEST) + b_ref[...]

    return pl.pallas_call(
        body, name=name, grid=(cols // ADA_COLS,),
        in_specs=[pl.BlockSpec((8, d), lambda j: (0, 0)), pl.BlockSpec((d, ADA_COLS), lambda j: (0, j)),
                  pl.BlockSpec((1, ADA_COLS), lambda j: (0, j))],
        out_specs=pl.BlockSpec((8, ADA_COLS), lambda j: (0, j)),
        out_shape=jax.ShapeDtypeStruct((8, cols), F32),
        compiler_params=_params(1),
    )(c_all, w_shard, b_shard)


def _ada_bwd(c_all, dmod_shard, w, m, v, name):
    d, cols = w.shape

    def body(c_ref, dm_ref, w_ref, m_ref, v_ref, g_ref, d_ref, nm_ref, nv_ref):
        cv = c_ref[...]
        g = lax.dot_general(cv * _sigmoid(cv), dm_ref[...], (((0,), (0,)), ((), ())),
                            preferred_element_type=F32, precision=lax.Precision.HIGHEST)
        g_ref[...] = g
        d_ref[...], nm_ref[...], nv_ref[...] = _adamw_math(w_ref[...], g, m_ref[...], v_ref[...])

    blk = pl.BlockSpec((d, ADA_COLS), lambda j: (0, j))
    shape = jax.ShapeDtypeStruct((d, cols), F32)
    return pl.pallas_call(
        body, name=name, grid=(cols // ADA_COLS,),
        in_specs=[pl.BlockSpec((8, d), lambda j: (0, 0)), pl.BlockSpec((8, ADA_COLS), lambda j: (0, j)), blk, blk, blk],
        out_specs=[blk] * 4, out_shape=[shape] * 4,
        compiler_params=_params(1),
    )(c_all, dmod_shard, w, m, v)


def _small_update(parts, w, m, v, name):
    n = w.shape[1]

    def body(p_ref, w_ref, m_ref, v_ref, g_ref, d_ref, nm_ref, nv_ref):
        g = p_ref[0:1, :]
        for i in range(1, 8):
            g = g + p_ref[i:i + 1, :]
        g_ref[...] = g
        d_ref[...], nm_ref[...], nv_ref[...] = _adamw_math(w_ref[...], g, m_ref[...], v_ref[...])

    shape = jax.ShapeDtypeStruct((1, n), F32)
    return pl.pallas_call(body, name=name, out_shape=[shape] * 4, compiler_params=_params())(parts, w, m, v)


def _cols_to_shards(w, n):
    r, nc = w.shape
    return w.reshape(r, n, nc // n).transpose(1, 0, 2)


def kernel(x, c, w_ada, b_ada, norm_ffn1, ffn1_w_gate, ffn1_w_up, ffn1_w_down, norm_mix, w_in, q_norm, k_norm, conv_w, w_attn_branch, w_conv_branch, w_out, norm_ffn2, ffn2_w_gate, ffn2_w_up, ffn2_w_down, loss_target, m_w_ada, m_b_ada, m_norm_ffn1, m_ffn1_w_gate, m_ffn1_w_up, m_ffn1_w_down, m_norm_mix, m_w_in, m_q_norm, m_k_norm, m_conv_w, m_w_attn_branch, m_w_conv_branch, m_w_out, m_norm_ffn2, m_ffn2_w_gate, m_ffn2_w_up, m_ffn2_w_down, v_w_ada, v_b_ada, v_norm_ffn1, v_ffn1_w_gate, v_ffn1_w_up, v_ffn1_w_down, v_norm_mix, v_w_in, v_q_norm, v_k_norm, v_conv_w, v_w_attn_branch, v_w_conv_branch, v_w_out, v_norm_ffn2, v_ffn2_w_gate, v_ffn2_w_up, v_ffn2_w_down):
    ix, iy, ic = _place()
    chip = 2 * ix + iy
    me = 4 * ix + 2 * iy + ic
    xs = x[0]
    target = loss_target[0]
    s_len, d = xs.shape
    ada_cols = w_ada.shape[2]
    conv_cols = conv_w.shape[2]

    conv_rows = jnp.zeros((8, conv_cols), F32).at[0:3].set(conv_w[0])
    small_in = jnp.concatenate([jnp.broadcast_to(c, (8, d)), conv_rows], axis=1)
    small_all = _allgather8(small_in, "gather_c").reshape(8, 8, d + conv_cols)
    c_all = small_all[:, 0, :d]
    conv_full = small_all[0::2, 0:3, d:].transpose(1, 0, 2).reshape(3, N_CHIPS * conv_cols)
    conv_pad = jnp.zeros((8, N_CHIPS * conv_cols), F32).at[0:3].set(conv_full)
    b_shard = lax.dynamic_slice(b_ada, (0, chip * ada_cols), (1, ada_cols))
    mod_part = _ada_fwd(c_all, w_ada[0], b_shard, "ada_fwd")
    mod_all = _allgather8(mod_part, "gather_mod").reshape(N_CHIPS, 2, 8, ada_cols)[:, 0]
    mod_mine = lax.dynamic_slice(mod_all, (0, me, 0), (N_CHIPS, 1, ada_cols)).reshape(9, d)

    def mod_rows(i, gain):
        return jnp.zeros((8, d), F32).at[0:3].set(mod_mine[3 * i:3 * i + 3]).at[3:4].set(gain)

    mod1, mod2, mod3 = mod_rows(0, norm_ffn1), mod_rows(1, norm_mix), mod_rows(2, norm_ffn2)

    to16 = lambda w: w[0].astype(BF16)
    wg1, wu1, wd1 = _gather_weights([to16(ffn1_w_gate), to16(ffn1_w_up), to16(ffn1_w_down)], [False] * 3,
                                    "gather_ffn1", 1)
    h1, h1t = _norm_mod(xs, mod1, "norm1")
    (w_in_full,) = _gather_weights([to16(w_in)], [True], "gather_w_in", 2, after=(wd1, h1))

    g1, u1, y1 = _ffn_fwd(h1, wg1, wu1, wd1, "ffn1_fwd")
    x1, h2, h2t = _norm_mod(xs, mod2, "norm2", prev=(y1, mod1, 0.5))
    qkv, rest, qkv_hat = _in_proj(h2, w_in_full, q_norm, k_norm, "in_proj")
    w_ab, w_cb_g, w_o_g, wg2, wu2, wd2 = _gather_weights(
        [to16(w_attn_branch), to16(w_conv_branch), to16(w_out),
         to16(ffn2_w_gate), to16(ffn2_w_up), to16(ffn2_w_down)], [True] + [False] * 5,
        "gather_rest", 3, after=(h2,))
    a_w = w_ab.shape[0]
    w_cb = w_cb_g.reshape(d, d)
    w_o = w_o_g.reshape(d, d)
    o, lse = _attn_fwd(qkv_hat, "attn_fwd")
    x2, z, ya, yc, conv, yb, merged, h3, h3t = _mix_fwd(x1, o, rest, mod2, mod3, conv_pad, w_ab, w_cb, w_o, "mix_fwd")
    g3, u3, y3 = _ffn_fwd(h3, wg2, wu2, wd2, "ffn2_fwd")
    dx3, dy3, loss_part = _loss_grad(x2, y3, mod3, target, "loss")

    c_idx = jnp.reshape(ic, (1,)).astype(jnp.int32)
    chip_idx = jnp.stack([chip, ic]).astype(jnp.int32)

    def pair_send(grads, tag, collective_id):
        return _rs_pair_exchange(grads, "rs_pair_" + tag, collective_id)

    def chip_send(grads, from_sibling, names, tag, collective_id, after):
        pair_sums = [_pair_add(g, r, c_idx, "pair_add_" + nm, after) for g, r, nm in zip(grads, from_sibling, names)]
        return pair_sums, _rs_chip_exchange(pair_sums, "rs_chips_" + tag, collective_id)

    def reduce_finish(pair_sums, from_chips, names, tag, after):
        totals = [_chip_add(p, r, chip_idx, "chip_add_" + nm, after)
                  for p, r, nm in zip(pair_sums, from_chips, names)]
        return dict(zip(names, _rs_share(totals, "rs_share_" + tag)))

    names_a = ["ffn2_w_gate", "ffn2_w_up", "ffn2_w_down"]
    names_b = ["w_in", "w_attn_branch", "w_conv_branch", "w_out"]
    names_c = ["ffn1_w_gate", "ffn1_w_up", "ffn1_w_down"]

    dh3, dg3, du3, a3 = _ffn_bwd(dy3, g3, u3, wg2, wu2, wd2, "ffn2_bwd")
    grads_a = list(_ffn_wgrads(h3t, dg3, du3, a3, dy3, "ffn2"))
    sibling_a = pair_send(grads_a, "a", 7)
    dx2, st3 = _norm_bwd(dh3, x2, mod3, dx3, y3, 0.5, "norm3_bwd")
    sums_a, chips_a = chip_send(grads_a, sibling_a, names_a, "a", 4, after=(dx2,))

    do, drest, dz, dya, dyc, st_conv = _mix_bwd(dx2, ya, yc, conv, rest, mod2, conv_pad, w_ab, w_cb, w_o, a_w,
                                                "mix_bwd", after=tuple(sums_a))
    dq, dk, dv, st_qk = _attn_bwd(qkv, qkv_hat, do, o, lse, q_norm, k_norm, "attn_bwd")
    tok = lambda width: (lambda ts: pl.BlockSpec((ts, width), lambda cc, s: (s, 0)))
    colblk = lambda width: (lambda ts: pl.BlockSpec((ts, width), lambda cc, s: (s, cc)))
    tok_t = lambda ts: pl.BlockSpec((d, ts), lambda cc, s: (0, s))
    whole = pl.BlockSpec((d, QKV), lambda cc, s: (0, 0))
    dw_in = [_wgrad(h2t, part, tok_t, tok(QKV), (d, QKV), whole, (d, QKV), 1, "dw_in_" + nm, True)
             for part, nm in ((dq, "q"), (dk, "k"), (dv, "v"))]
    dw_in.append(_wgrad(h2t, drest, tok_t, colblk(d), (d, 5 * d), pl.BlockSpec((d, d), lambda cc, s: (0, cc)),
                        (d, d), 5, "dw_in_rest", True))
    dw_in = _cols_to_shards(jnp.concatenate(dw_in, axis=1), N_CHIPS)
    shard_w = d // N_CHIPS
    dw_ab = _wgrad(o, dya, tok(a_w), colblk(shard_w), (a_w, d), pl.BlockSpec((a_w, shard_w), lambda cc, s: (0, cc)),
                   (a_w, shard_w), N_CHIPS, "dw_attn_branch")
    dw_ab = _cols_to_shards(dw_ab, N_CHIPS)
    row_out = pl.BlockSpec((None, shard_w, d), lambda cc, s: (cc, 0, 0))
    dw_cb = _wgrad(yb, dyc, colblk(shard_w), tok(d), (N_CHIPS, shard_w, d), row_out, (shard_w, d), N_CHIPS, "dw_conv_branch")
    dw_o = _wgrad(merged, dz, colblk(shard_w), tok(d), (N_CHIPS, shard_w, d), row_out, (shard_w, d), N_CHIPS, "dw_out")
    shard_grads = reduce_finish(sums_a, chips_a, names_a, "a", after=(dw_in, dw_o))
    grads_b = [dw_in, dw_ab, dw_cb, dw_o]
    sibling_b = pair_send(grads_b, "b", 8)

    dh2 = _in_proj_bwd(dq, dk, dv, drest, w_in_full, "in_proj_bwd")
    sums_b, chips_b = chip_send(grads_b, sibling_b, names_b, "b", 5, after=(dh2,))
    dx1, st2, dy1 = _norm_bwd(dh2, x1, mod2, dx2, z, 1.0, "norm2_bwd", after=tuple(sums_b), prev=(mod1, 0.5))
    dh1, dg1, du1, a1 = _ffn_bwd(dy1, g1, u1, wg1, wu1, wd1, "ffn1_bwd")
    dx0, st1 = _norm_bwd(dh1, xs, mod1, dx1, y1, 0.5, "norm1_bwd")
    grads_c = list(_ffn_wgrads(h1t, dg1, du1, a1, dy1, "ffn1"))
    sibling_c = pair_send(grads_c, "c", 9)
    shard_grads.update(reduce_finish(sums_b, chips_b, names_b, "b", after=tuple(grads_c)))

    dmod = jnp.concatenate([st1[0:3], st2[0:3], st3[0:3]], axis=0).reshape(1, 9 * d)
    loss_cols = jnp.zeros((1, HEAD_DIM), F32).at[0, 0].set(jnp.sum(loss_part))
    small = jnp.concatenate([dmod, st1[3:4], st2[3:4], st3[3:4], st_qk[0:1], st_qk[1:2],
                             st_conv[0:3].reshape(1, 3 * d), loss_cols], axis=1)
    small_all = _allgather8(jnp.broadcast_to(small, (8, small.shape[1])), "gather_small").reshape(8, 8, -1)[:, 0]
    loss = (0.5 / d) * jnp.sum(small_all[:, -HEAD_DIM])
    small_all = small_all[:, :-HEAD_DIM]
    dmod_all = small_all[:, :9 * d]
    dmod_shard = lax.dynamic_slice(dmod_all, (0, chip * ada_cols), (8, ada_cols))
    g_w_ada, d_w_ada, nm_w_ada, nv_w_ada = _ada_bwd(c_all, dmod_shard, w_ada[0], m_w_ada[0], v_w_ada[0], "ada_bwd")

    vec_names = ["b_ada", "norm_ffn1", "norm_mix", "norm_ffn2", "q_norm", "k_norm"]
    vec_w = [b_ada, norm_ffn1, norm_mix, norm_ffn2, q_norm, k_norm]
    vec_m = [m_b_ada, m_norm_ffn1, m_norm_mix, m_norm_ffn2, m_q_norm, m_k_norm]
    vec_v = [v_b_ada, v_norm_ffn1, v_norm_mix, v_norm_ffn2, v_q_norm, v_k_norm]
    n_vec = sum(w.shape[1] for w in vec_w)
    cat = lambda arrs: jnp.concatenate(arrs, axis=1)
    vec_out = _small_update(small_all[:, :n_vec], cat(vec_w), cat(vec_m), cat(vec_v), "small_update")
    conv_parts = small_all[:, n_vec:].reshape(8, 3, N_CHIPS * conv_cols)
    conv_parts = lax.dynamic_slice(conv_parts, (0, 0, chip * conv_cols), (8, 3, conv_cols)).reshape(8, 3 * conv_cols)
    flat3 = lambda w: w[0].reshape(1, 3 * conv_cols)
    conv_out = _small_update(conv_parts, flat3(conv_w), flat3(m_conv_w), flat3(v_conv_w), "conv_update")

    res = {"w_ada": [t[None] for t in (g_w_ada, d_w_ada, nm_w_ada, nv_w_ada)],
           "conv_w": [t.reshape(1, 3, conv_cols) for t in conv_out]}
    off = 0
    for nm, w in zip(vec_names, vec_w):
        width = w.shape[1]
        res[nm] = [t[:, off:off + width] for t in vec_out]
        off += width
    big = {"ffn1_w_gate": (ffn1_w_gate, m_ffn1_w_gate, v_ffn1_w_gate), "ffn1_w_up": (ffn1_w_up, m_ffn1_w_up, v_ffn1_w_up),
           "ffn1_w_down": (ffn1_w_down, m_ffn1_w_down, v_ffn1_w_down), "w_in": (w_in, m_w_in, v_w_in),
           "w_attn_branch": (w_attn_branch, m_w_attn_branch, v_w_attn_branch),
           "w_conv_branch": (w_conv_branch, m_w_conv_branch, v_w_conv_branch), "w_out": (w_out, m_w_out, v_w_out),
           "ffn2_w_gate": (ffn2_w_gate, m_ffn2_w_gate, v_ffn2_w_gate), "ffn2_w_up": (ffn2_w_up, m_ffn2_w_up, v_ffn2_w_up),
           "ffn2_w_down": (ffn2_w_down, m_ffn2_w_down, v_ffn2_w_down)}
    def update(nm, after=()):
        w, m, v = big[nm]
        g, delta, new_m, new_v = _adamw(w[0], shard_grads[nm], m[0], v[0], "adamw_" + nm, after)
        res[nm] = [t[None] for t in (g, delta, new_m, new_v)]
        return new_v

    last = tuple(shard_grads[nm] for nm in names_b)
    for nm in names_a:
        last = (update(nm, last),)
    sums_c, chips_c = chip_send(grads_c, sibling_c, names_c, "c", 6, after=last)
    last = tuple(sums_c)
    for nm in names_b:
        last = (update(nm, last),)
    shard_grads.update(reduce_finish(sums_c, chips_c, names_c, "c", after=last))
    for nm in names_c:
        update(nm)

    order = ["w_ada", "b_ada", "norm_ffn1", "ffn1_w_gate", "ffn1_w_up", "ffn1_w_down", "norm_mix", "w_in", "q_norm",
             "k_norm", "conv_w", "w_attn_branch", "w_conv_branch", "w_out", "norm_ffn2", "ffn2_w_gate", "ffn2_w_up",
             "ffn2_w_down"]
    return (loss, dx0[None], *[res[nm][0] for nm in order], *[res[nm][1] for nm in order],
            *[res[nm][2] for nm in order], *[res[nm][3] for nm in order])
```

```python
import jax
import jax.numpy as jnp
from jax import lax
from jax.experimental import pallas as pl
from jax.experimental.pallas import tpu as pltpu
from jax.experimental.pallas import tpu_sc as plsc

F32 = jnp.float32
BF16 = jnp.bfloat16
MESH = pl.DeviceIdType.MESH
ANY = pl.BlockSpec(memory_space=pl.ANY)

NORM_EPS = 1e-6
LANE_TILE = 128
HEAD_DIM = 128
N_GROUPS = 3
HEADS = 4
DILATIONS = (1, 4, 16)
ATTN_BLOCK = 128
SLAB = ATTN_BLOCK * max(DILATIONS)
QKV = N_GROUPS * HEADS * HEAD_DIM
ATTN_SCALE = HEAD_DIM ** -0.5
NEG = -1e30
N_CHIPS = 4

ADAM_LR = 0.001
ADAM_B1 = 0.9
ADAM_B2 = 0.999
ADAM_EPS = 1e-08
ADAM_WD = 0.01
ADAM_STEP = 10

VMEM_LIMIT_BYTES = 56 * 1024 * 1024
TOKEN_TILE = 512
FFN_TILE = 1024
PROJ_TILE = 2048
WGRAD_TILE = 2048
IN_BLOCK = 512
MIX_TILE = 512
ACC_PIECES = 4
ADAMW_TILE_BYTES = 3 * 512 * 1024


def _params(n_axes=0):
    return pltpu.CompilerParams(
        dimension_semantics=("arbitrary",) * n_axes if n_axes else None,
        vmem_limit_bytes=VMEM_LIMIT_BYTES)


def _dot(a, b):
    return jnp.dot(a, b, preferred_element_type=F32)


def _dot_nt(a, b):
    return lax.dot_general(a, b, (((1,), (1,)), ((), ())), preferred_element_type=F32)


def _dot_tn(a, b):
    return lax.dot_general(a, b, (((0,), (0,)), ((), ())), preferred_element_type=F32)


def _sigmoid(x):
    return 0.5 * jnp.tanh(0.5 * x) + 0.5


def _place():
    return lax.axis_index("x"), lax.axis_index("y"), lax.axis_index("c")


def _ordered(body, n_in, after):
    if not after:
        return body
    return lambda *refs: body(*refs[:n_in], *refs[n_in + len(after):])


def _allgather8(block, name):
    m_per, n = block.shape

    def body(x_ref, out_ref, send_sems, recv_sems, local_sem):
        x, y, c = _place()
        me, sibling = (x, y, c), (x, y, 1 - c)
        chips = [(1 - x, y), (x, 1 - y), (1 - x, 1 - y)]

        def rows(px, py, pc):
            return out_ref.at[pl.ds((4 * px + 2 * py + pc) * m_per, m_per), :]

        def copy(k, blk, to, src=None):
            return pltpu.make_async_remote_copy(
                src_ref=rows(*blk) if src is None else src, dst_ref=rows(*blk),
                send_sem=send_sems.at[k], recv_sem=recv_sems.at[k],
                device_id=to, device_id_type=MESH)

        mine = pltpu.make_async_copy(x_ref, rows(*me), local_sem)
        mine.start()
        first = [copy(0, me, sibling, src=x_ref)]
        first += [copy(1 + j, me, (*chip, c), src=x_ref) for j, chip in enumerate(chips)]
        for cp in first:
            cp.start()
        passed = [copy(4 + j, (*chip, c), sibling) for j, chip in enumerate(chips)]
        for j, chip in enumerate(chips):
            copy(1 + j, (*chip, c), me).wait_recv()
            passed[j].start()
        copy(0, sibling, me).wait_recv()
        for j, chip in enumerate(chips):
            copy(4 + j, (*chip, 1 - c), me).wait_recv()
        for cp in first + passed:
            cp.wait_send()
        mine.wait()

    return pl.pallas_call(
        body, name=name,
        out_shape=jax.ShapeDtypeStruct((8 * m_per, n), block.dtype),
        in_specs=[pl.BlockSpec(memory_space=pltpu.VMEM)],
        out_specs=pl.BlockSpec(memory_space=pltpu.VMEM),
        scratch_shapes=[pltpu.SemaphoreType.DMA((7,)), pltpu.SemaphoreType.DMA((7,)),
                        pltpu.SemaphoreType.DMA],
        compiler_params=_params(),
    )(block)


def _handshake(peers):
    barrier = pltpu.get_barrier_semaphore()
    for peer in peers:
        pl.semaphore_signal(barrier, inc=1, device_id=peer, device_id_type=MESH)
    pl.semaphore_wait(barrier, len(peers))


def _gather_weights(shards, by_cols, name, collective_id, after=()):
    n_arr = len(shards)

    def body(*refs):
        srcs, outs = refs[:n_arr], refs[n_arr + len(after):2 * n_arr + len(after)]
        send_sems, recv_sems, local_sems = refs[2 * n_arr + len(after):]
        x, y, c = _place()
        me_dev, sibling = (x, y, c), (x, y, 1 - c)
        chips = [(1 - x, y), (x, 1 - y), (1 - x, 1 - y)]
        me = 2 * x + y
        _handshake([sibling] + [(*chip, c) for chip in chips])

        def place(k, chip_idx, rows):
            if by_cols[k]:
                width = srcs[k].shape[1]
                return outs[k].at[rows, pl.ds(pl.multiple_of(chip_idx * width, 128), width)]
            return outs[k].at[chip_idx, rows]

        def copy(k, slot, chip_idx, half_sel, to, from_shard=False):
            half = srcs[k].shape[0] // 2
            rows = pl.ds(half_sel * half, half)
            dst = place(k, chip_idx, rows)
            return pltpu.make_async_remote_copy(
                src_ref=srcs[k].at[rows] if from_shard else dst, dst_ref=dst,
                send_sem=send_sems.at[6 * k + slot], recv_sem=recv_sems.at[6 * k + slot],
                device_id=to, device_id_type=MESH)

        own = [pltpu.make_async_copy(srcs[k], place(k, me, pl.ds(0, srcs[k].shape[0])), local_sems.at[k])
               for k in range(n_arr)]
        for cp in own:
            cp.start()
        sent = []
        for k in range(n_arr):
            for j, chip in enumerate(chips):
                sent.append(copy(k, j, me, c, (*chip, c), from_shard=True))
                sent[-1].start()
        for k in range(n_arr):
            for j, chip in enumerate(chips):
                chip_idx = 2 * chip[0] + chip[1]
                copy(k, j, chip_idx, c, me_dev).wait_recv()
                sent.append(copy(k, 3 + j, chip_idx, c, sibling))
                sent[-1].start()
        for k in range(n_arr):
            for j, chip in enumerate(chips):
                copy(k, 3 + j, 2 * chip[0] + chip[1], 1 - c, me_dev).wait_recv()
        for cp in sent:
            cp.wait_send()
        for cp in own:
            cp.wait()

    def gathered(k):
        r, cols = shards[k].shape
        return (r, N_CHIPS * cols) if by_cols[k] else (N_CHIPS, r, cols)

    return pl.kernel(
        body, name=name,
        out_type=[jax.ShapeDtypeStruct(gathered(k), shards[k].dtype) for k in range(n_arr)],
        mesh=plsc.ScalarSubcoreMesh(axis_name="sequencer", num_cores=1),
        scratch_types=[pltpu.SemaphoreType.DMA((6 * n_arr,)), pltpu.SemaphoreType.DMA((6 * n_arr,)),
                       pltpu.SemaphoreType.DMA((n_arr,))],
        compiler_params=pltpu.CompilerParams(collective_id=collective_id),
    )(*shards, *after)


def _rs_pair_exchange(grads, name, collective_id):
    n_arr = len(grads)

    def body(*refs):
        srcs, outs = refs[:n_arr], refs[n_arr:2 * n_arr]
        send_sems, recv_sems = refs[2 * n_arr:]
        x, y, c = _place()
        _handshake([(x, y, 1 - c)])
        cps = []
        for k in range(n_arr):
            half = srcs[k].shape[1] // 2
            cps.append(pltpu.make_async_remote_copy(
                src_ref=srcs[k].at[:, pl.ds((1 - c) * half, half)], dst_ref=outs[k],
                send_sem=send_sems.at[k], recv_sem=recv_sems.at[k],
                device_id=(x, y, 1 - c), device_id_type=MESH))
            cps[-1].start()
        for cp in cps:
            cp.wait_recv()
        for cp in cps:
            cp.wait_send()

    return pl.kernel(
        body, name=name,
        out_type=[jax.ShapeDtypeStruct((g.shape[0], g.shape[1] // 2, g.shape[2]), g.dtype) for g in grads],
        mesh=plsc.ScalarSubcoreMesh(axis_name="sequencer", num_cores=1),
        scratch_types=[pltpu.SemaphoreType.DMA((n_arr,)), pltpu.SemaphoreType.DMA((n_arr,))],
        compiler_params=pltpu.CompilerParams(collective_id=collective_id),
    )(*grads)


def _rs_chip_exchange(sums, name, collective_id):
    n_arr = len(sums)

    def body(*refs):
        srcs, outs = refs[:n_arr], refs[n_arr:2 * n_arr]
        send_sems, recv_sems = refs[2 * n_arr:]
        x, y, c = _place()
        chips = [(1 - x, y), (x, 1 - y), (1 - x, 1 - y)]
        _handshake([(*chip, c) for chip in chips])
        cps = []
        for k in range(n_arr):
            for j, chip in enumerate(chips):
                cps.append(pltpu.make_async_remote_copy(
                    src_ref=srcs[k].at[2 * chip[0] + chip[1]], dst_ref=outs[k].at[j],
                    send_sem=send_sems.at[3 * k + j], recv_sem=recv_sems.at[3 * k + j],
                    device_id=(*chip, c), device_id_type=MESH))
                cps[-1].start()
        for cp in cps:
            cp.wait_recv()
        for cp in cps:
            cp.wait_send()

    return pl.kernel(
        body, name=name,
        out_type=[jax.ShapeDtypeStruct((3,) + s.shape[1:], s.dtype) for s in sums],
        mesh=plsc.ScalarSubcoreMesh(axis_name="sequencer", num_cores=1),
        scratch_types=[pltpu.SemaphoreType.DMA((3 * n_arr,)), pltpu.SemaphoreType.DMA((3 * n_arr,))],
        compiler_params=pltpu.CompilerParams(collective_id=collective_id),
    )(*sums)


def _rs_share(totals, name):
    n_arr = len(totals)

    def body(*refs):
        outs = refs[n_arr:2 * n_arr]
        send_sems, recv_sems = refs[2 * n_arr:]
        x, y, c = _place()

        def half_rows(k, sel):
            return outs[k].at[sel]

        cps = []
        for k in range(n_arr):
            cps.append(pltpu.make_async_remote_copy(
                src_ref=half_rows(k, c), dst_ref=half_rows(k, c), send_sem=send_sems.at[k], recv_sem=recv_sems.at[k],
                device_id=(x, y, 1 - c), device_id_type=MESH))
            cps[-1].start()
        for k in range(n_arr):
            pltpu.make_async_remote_copy(
                src_ref=half_rows(k, c), dst_ref=half_rows(k, 1 - c), send_sem=send_sems.at[k],
                recv_sem=recv_sems.at[k], device_id=(x, y, 1 - c), device_id_type=MESH).wait_recv()
        for cp in cps:
            cp.wait_send()

    shared = pl.pallas_call(
        body, name=name,
        out_shape=[jax.ShapeDtypeStruct(t.shape, t.dtype) for t in totals],
        in_specs=[ANY] * n_arr, out_specs=[ANY] * n_arr,
        input_output_aliases={k: k for k in range(n_arr)},
        scratch_shapes=[pltpu.SemaphoreType.DMA((n_arr,)), pltpu.SemaphoreType.DMA((n_arr,))],
        compiler_params=_params(),
    )(*totals)
    return [t.reshape(2 * t.shape[1], t.shape[2]) for t in shared]


def _pair_add(grad, recv, c_idx, name, after=()):
    n, r, cols = grad.shape
    half = r // 2
    rows = half // 2

    def body(_, g_ref, r_ref, o_ref):
        o_ref[...] = (g_ref[...].astype(F32) + r_ref[...].astype(F32)).astype(o_ref.dtype)

    return pl.pallas_call(
        _ordered(body, 3, after), name=name,
        grid_spec=pltpu.PrefetchScalarGridSpec(
            num_scalar_prefetch=1, grid=(n, 2),
            in_specs=[pl.BlockSpec((None, None, rows, cols), lambda s, i, ci: (s, ci[0], i, 0)),
                      pl.BlockSpec((None, rows, cols), lambda s, i, ci: (s, i, 0))] + [ANY] * len(after),
            out_specs=pl.BlockSpec((None, rows, cols), lambda s, i, ci: (s, i, 0))),
        out_shape=jax.ShapeDtypeStruct((n, half, cols), BF16),
        compiler_params=_params(2),
    )(c_idx, grad.reshape(n, 2, half, cols), recv, *after)


def _chip_add(sums, recv, chip_and_core, name, after=()):
    _, half, cols = sums.shape
    rows = half // 2

    def body(_, s_ref, r0_ref, r1_ref, r2_ref, o_ref):
        o_ref[...] = ((s_ref[...].astype(F32) + r0_ref[...].astype(F32))
                      + r1_ref[...].astype(F32)) + r2_ref[...].astype(F32)

    def recv_spec(j):
        return pl.BlockSpec((None, rows, cols), lambda i, ci: (j, i, 0))

    return pl.pallas_call(
        _ordered(body, 5, after), name=name,
        grid_spec=pltpu.PrefetchScalarGridSpec(
            num_scalar_prefetch=1, grid=(2,),
            in_specs=[pl.BlockSpec((None, rows, cols), lambda i, ci: (ci[0], i, 0)),
                      recv_spec(0), recv_spec(1), recv_spec(2)] + [ANY] * len(after),
            out_specs=pl.BlockSpec((None, rows, cols), lambda i, ci: (ci[1], i, 0))),
        out_shape=jax.ShapeDtypeStruct((2, half, cols), F32),
        compiler_params=_params(1),
    )(chip_and_core, sums, recv, recv, recv, *after)


def _rms(x):
    return lax.rsqrt(jnp.mean(x * x, axis=-1, keepdims=True) + NORM_EPS)


def _norm_mod(x, mod, name, prev=None):
    s_len, d = x.shape
    tm = TOKEN_TILE

    def body(*refs):
        if prev is None:
            x_ref, mod_ref, h_ref, ht_ref = refs
            xv = x_ref[...]
        else:
            x_ref, y_ref, modp_ref, mod_ref, xo_ref, h_ref, ht_ref = refs
            xv = x_ref[...] + prev[2] * modp_ref[2:3, :] * y_ref[...]
            xo_ref[...] = xv
        n = (xv * _rms(xv)) * mod_ref[3:4, :]
        h = n * (1.0 + mod_ref[1:2, :]) + mod_ref[0:1, :]
        h_ref[...] = h.astype(BF16)
        ht_ref[...] = h.T.astype(BF16)

    tile = pl.BlockSpec((tm, d), lambda i: (i, 0))
    small = pl.BlockSpec((8, d), lambda i: (0, 0))
    h_specs = [tile, pl.BlockSpec((d, tm), lambda i: (0, i))]
    h_shapes = [jax.ShapeDtypeStruct((s_len, d), BF16), jax.ShapeDtypeStruct((d, s_len), BF16)]
    if prev is None:
        return pl.pallas_call(
            body, name=name, grid=(s_len // tm,), in_specs=[tile, small], out_specs=h_specs, out_shape=h_shapes,
            compiler_params=_params(1))(x, mod)
    return pl.pallas_call(
        body, name=name, grid=(s_len // tm,), in_specs=[tile, tile, small, small],
        out_specs=[tile] + h_specs, out_shape=[jax.ShapeDtypeStruct((s_len, d), F32)] + h_shapes,
        compiler_params=_params(1))(x, prev[0], prev[1], mod)


def _norm_bwd(dh, x, mod, dxo, y_raw, coef, name, after=(), prev=None):
    s_len, d = x.shape
    tm = TOKEN_TILE

    def body(*refs):
        if prev is None:
            dh_ref, x_ref, mod_ref, dxo_ref, y_ref, dx_ref, st_ref = refs
        else:
            dh_ref, x_ref, mod_ref, dxo_ref, y_ref, modp_ref, dx_ref, st_ref, dyp_ref = refs

        @pl.when(pl.program_id(0) == 0)
        def _():
            st_ref[...] = jnp.zeros_like(st_ref)

        xv, dhv, dxov = x_ref[...], dh_ref[...], dxo_ref[...]
        r = _rms(xv)
        xh = xv * r
        gain, scale = mod_ref[3:4, :], mod_ref[1:2, :]
        dn = dhv * (1.0 + scale)
        dxh = dn * gain
        dx = dxov + r * (dxh - xh * jnp.mean(dxh * xh, axis=-1, keepdims=True))
        dx_ref[...] = dx
        if prev is not None:
            dyp_ref[...] = (prev[1] * modp_ref[2:3, :] * dx).astype(BF16)
        st_ref[0:1, :] += jnp.sum(dhv, axis=0, keepdims=True)
        st_ref[1:2, :] += jnp.sum(dhv * (xh * gain), axis=0, keepdims=True)
        st_ref[2:3, :] += coef * jnp.sum(y_ref[...].astype(F32) * dxov, axis=0, keepdims=True)
        st_ref[3:4, :] += jnp.sum(dn * xh, axis=0, keepdims=True)

    tile = pl.BlockSpec((tm, d), lambda i: (i, 0))
    small = pl.BlockSpec((8, d), lambda i: (0, 0))
    operands = [dh, x, mod, dxo, y_raw] + ([] if prev is None else [prev[0]])
    in_specs = [tile, tile, small, tile, tile] + ([] if prev is None else [small])
    out_specs = [tile, small] + ([] if prev is None else [tile])
    out_shape = [jax.ShapeDtypeStruct((s_len, d), F32), jax.ShapeDtypeStruct((8, d), F32)]
    if prev is not None:
        out_shape.append(jax.ShapeDtypeStruct((s_len, d), BF16))
    return pl.pallas_call(
        _ordered(body, len(operands), after), name=name, grid=(s_len // tm,),
        in_specs=in_specs + [ANY] * len(after), out_specs=out_specs, out_shape=out_shape,
        compiler_params=_params(1),
    )(*operands, *after)


def _loss_grad(x, y, mod, target, name):
    s_len, d = x.shape
    tm = TOKEN_TILE

    def body(x_ref, y_ref, mod_ref, t_ref, do_ref, dy_ref, part_ref):
        @pl.when(pl.program_id(0) == 0)
        def _():
            part_ref[...] = jnp.zeros_like(part_ref)

        half_gate = 0.5 * mod_ref[2:3, :]
        err = (x_ref[...] + half_gate * y_ref[...]) - t_ref[...]
        do = err * (1.0 / d)
        do_ref[...] = do
        dy_ref[...] = (half_gate * do).astype(BF16)
        sq = err * err
        part_ref[...] += jnp.sum(sq.reshape(tm // 8, 8, d), axis=0)

    tile = pl.BlockSpec((tm, d), lambda i: (i, 0))
    small = pl.BlockSpec((8, d), lambda i: (0, 0))
    return pl.pallas_call(
        body, name=name, grid=(s_len // tm,),
        in_specs=[tile, tile, small, tile],
        out_specs=[tile, tile, small],
        out_shape=[jax.ShapeDtypeStruct((s_len, d), F32), jax.ShapeDtypeStruct((s_len, d), BF16),
                   jax.ShapeDtypeStruct((8, d), F32)],
        compiler_params=_params(1),
    )(x, y, mod, target)


def _adamw_math(w, g, m, v):
    m = ADAM_B1 * m + (1.0 - ADAM_B1) * g
    v = ADAM_B2 * v + (1.0 - ADAM_B2) * (g * g)
    m_hat = m / (1.0 - ADAM_B1 ** ADAM_STEP)
    v_hat = v / (1.0 - ADAM_B2 ** ADAM_STEP)
    delta = -ADAM_LR * (m_hat / (jnp.sqrt(v_hat) + ADAM_EPS) + ADAM_WD * w)
    return delta, m, v


def _adamw(w, g, m, v, name, after=()):
    r, cols = w.shape
    tr = max([t for t in (r // k for k in (1, 2, 4, 8, 16)) if t % 8 == 0 and r % t == 0
              and t * cols * 4 <= ADAMW_TILE_BYTES] or [r])

    def body(w_ref, g_ref, m_ref, v_ref, go_ref, d_ref, nm_ref, nv_ref):
        gv = g_ref[...]
        go_ref[...] = gv
        d_ref[...], nm_ref[...], nv_ref[...] = _adamw_math(w_ref[...], gv, m_ref[...], v_ref[...])

    tile = pl.BlockSpec((tr, cols), lambda i: (i, 0))
    shape = jax.ShapeDtypeStruct((r, cols), F32)
    return pl.pallas_call(
        _ordered(body, 4, after), name=name, grid=(r // tr,),
        in_specs=[tile] * 4 + [ANY] * len(after), out_specs=[tile] * 4, out_shape=[shape] * 4,
        compiler_params=_params(1),
    )(w, g, m, v, *after)


def _in_parts(tm, n_qkv, n_rest):
    def part(lo, n_blk):
        return pl.BlockSpec((tm, IN_BLOCK), lambda i, j: (i, jnp.clip(j - lo, 0, n_blk - 1)))
    return [part(0, n_qkv), part(n_qkv, n_qkv), part(2 * n_qkv, n_qkv), part(3 * n_qkv, n_rest)]


def _pick_part(j, n_qkv, refs, fn):
    bounds = [0, n_qkv, 2 * n_qkv, 3 * n_qkv]
    for p, ref in enumerate(refs):
        inside = j >= bounds[p]
        if p + 1 < len(refs):
            inside = inside & (j < bounds[p + 1])
        pl.when(inside)(lambda ref=ref: fn(ref))


def _rows(base, count, stride):
    return pl.ds(base, count) if stride == 1 else pl.ds(base, count, stride=stride)


REORDER_STRIDE = 4


def _reorder_plan(dil, parts=1):
    inner = min(dil, REORDER_STRIDE)
    return inner, dil // inner, SLAB // parts // inner, SLAB // dil


def _to_residue_order(dst, src, dil, tmp, part=0, parts=1):
    inner, outer, big, seg = _reorder_plan(dil, parts)
    piece = seg // parts
    if outer == 1:
        for r in range(dil):
            dst[pl.ds(r * seg + part * piece, piece), :] = src[_rows(r, piece, dil), :].astype(dst.dtype)
        return
    for b in range(inner):
        tmp[pl.ds(b * big, big), :] = src[_rows(b, big, inner), :]
    for a in range(outer):
        for b in range(inner):
            dst[pl.ds((inner * a + b) * seg + part * piece, piece), :] = (
                tmp[_rows(b * big + a, piece, outer), :].astype(dst.dtype))


def _to_token_order(dst, src, dil, tmp):
    inner, outer, big, seg = _reorder_plan(dil)
    if outer == 1:
        for r in range(dil):
            dst[_rows(r, seg, dil), :] = src[pl.ds(r * seg, seg), :]
        return
    for a in range(outer):
        for b in range(inner):
            tmp[_rows(b * big + a, seg, outer), :] = src[pl.ds((inner * a + b) * seg, seg), :]
    for b in range(inner):
        dst[_rows(b, big, inner), :] = tmp[pl.ds(b * big, big), :]


def _in_proj(h, w, q_norm, k_norm, name):
    s_len, d = h.shape
    tm = PROJ_TILE
    assert tm == SLAB and IN_BLOCK == HEADS * HEAD_DIM
    steps = w.shape[1] // IN_BLOCK
    n_qkv = 3 * QKV // IN_BLOCK
    parts = 4
    rows = [pl.ds(p * (tm // parts), tm // parts) for p in range(parts)]

    gains = jnp.concatenate([q_norm, k_norm, jnp.ones((6, HEAD_DIM), F32)], axis=0)

    def body(h_ref, w_ref, gains_ref, qkv_ref, rest_ref, hat_ref, tok_s, tmp_s):
        j = pl.program_id(1)
        sect = j // N_GROUPS
        multiply = lambda p: _dot(h_ref[rows[p], :], w_ref[...])

        def emit(gi):
            dil = DILATIONS[gi]
            res = [multiply(p) for p in range(parts)]
            gain = gains_ref[pl.ds(sect, 1), :]
            plain = sect == 2
            for p in range(parts):
                qkv_ref[rows[p], :] = res[p]
                for hh in range(HEADS):
                    cols = slice(hh * HEAD_DIM, (hh + 1) * HEAD_DIM)
                    x = res[p][:, cols]
                    tok_s[...] = (x * jnp.where(plain, 1.0, _rms(x))) * gain
                    _to_residue_order(hat_ref.at[:, cols], tok_s, dil, tmp_s, p, parts)

        for gi in range(N_GROUPS):
            pl.when((j < n_qkv) & (j % N_GROUPS == gi))(lambda gi=gi: emit(gi))

        @pl.when(j >= n_qkv)
        def _():
            for p in range(parts):
                rest_ref[rows[p], :] = multiply(p).astype(BF16)

    qkv_blk = pl.BlockSpec((tm, IN_BLOCK), lambda i, j: (i, jnp.minimum(j, n_qkv - 1)))
    return pl.pallas_call(
        body, name=name, grid=(s_len // tm, steps),
        in_specs=[pl.BlockSpec((tm, d), lambda i, j: (i, 0)), pl.BlockSpec((d, IN_BLOCK), lambda i, j: (0, j)),
                  pl.BlockSpec((8, HEAD_DIM), lambda i, j: (0, 0))],
        out_specs=[qkv_blk, pl.BlockSpec((tm, IN_BLOCK), lambda i, j: (i, jnp.maximum(j - n_qkv, 0))), qkv_blk],
        out_shape=[jax.ShapeDtypeStruct((s_len, 3 * QKV), F32),
                   jax.ShapeDtypeStruct((s_len, w.shape[1] - 3 * QKV), BF16),
                   jax.ShapeDtypeStruct((s_len, 3 * QKV), BF16)],
        scratch_shapes=[pltpu.VMEM((tm // parts, HEAD_DIM), F32)] * 2,
        compiler_params=_params(2),
    )(h, w, gains)


def _in_proj_bwd(dq, dk, dv, drest, w, name, after=()):
    s_len = dq.shape[0]
    d = w.shape[0]
    tm = PROJ_TILE
    steps = w.shape[1] // IN_BLOCK
    n_qkv = QKV // IN_BLOCK

    def body(dq_ref, dk_ref, dv_ref, dr_ref, w_ref, o_ref, acc_ref):
        j = pl.program_id(1)

        @pl.when(j == 0)
        def _():
            acc_ref[...] = jnp.zeros_like(acc_ref)

        def add(a_ref):
            rows = [pl.ds(p * (tm // ACC_PIECES), tm // ACC_PIECES) for p in range(ACC_PIECES)]
            products = [_dot_nt(a_ref[r, :], w_ref[...]) for r in rows]
            for r, product in zip(rows, products):
                acc_ref[r, :] += product

        _pick_part(j, n_qkv, [dq_ref, dk_ref, dv_ref, dr_ref], add)

        @pl.when(j == steps - 1)
        def _():
            o_ref[...] = acc_ref[...]

    return pl.pallas_call(
        _ordered(body, 5, after), name=name, grid=(s_len // tm, steps),
        in_specs=(_in_parts(tm, n_qkv, steps - 3 * n_qkv) + [pl.BlockSpec((d, IN_BLOCK), lambda i, j: (0, j))]
                  + [ANY] * len(after)),
        out_specs=pl.BlockSpec((tm, d), lambda i, j: (i, 0)),
        out_shape=jax.ShapeDtypeStruct((s_len, d), F32),
        scratch_shapes=[pltpu.VMEM((tm, d), F32)],
        compiler_params=_params(2),
    )(dq, dk, dv, drest, w, *after)


def _wgrad(x, y, x_spec, y_spec, out_shape, out_spec, acc_shape, n_chunks, name, x_transposed=False, after=()):
    s_len = y.shape[-2]
    ts = WGRAD_TILE
    steps = s_len // ts

    def body(x_ref, y_ref, o_ref, acc_ref):
        s = pl.program_id(1)

        @pl.when(s == 0)
        def _():
            acc_ref[...] = jnp.zeros_like(acc_ref)

        if x_transposed:
            n_rows = acc_shape[0]
            rows = [pl.ds(p * (n_rows // ACC_PIECES), n_rows // ACC_PIECES) for p in range(ACC_PIECES)]
            products = [_dot(x_ref[r, :], y_ref[...]) for r in rows]
            for r, product in zip(rows, products):
                acc_ref[r, :] += product
        else:
            cols = _pieces(acc_shape[1])
            products = [_dot_tn(x_ref[...], y_ref[:, c]) for c in cols]
            for c, product in zip(cols, products):
                acc_ref[:, c] += product

        @pl.when(s == steps - 1)
        def _():
            o_ref[...] = acc_ref[...].astype(o_ref.dtype)

    return pl.pallas_call(
        _ordered(body, 2, after), name=name, grid=(n_chunks, steps),
        in_specs=[x_spec(ts), y_spec(ts)] + [ANY] * len(after), out_specs=out_spec,
        out_shape=jax.ShapeDtypeStruct(out_shape, BF16),
        scratch_shapes=[pltpu.VMEM(acc_shape, F32)],
        compiler_params=_params(2),
    )(x, y, *after)


def _pieces(width, piece=256):
    return [slice(a, min(a + piece, width)) for a in range(0, width, piece)]


def _ffn_fwd(h, w_gate, w_up, w_down, name):
    s_len, d = h.shape
    n_chunks, _, fs = w_gate.shape
    tm = FFN_TILE

    def body(h_ref, wg_ref, wu_ref, wd_ref, g_ref, u_ref, y_ref):
        j = pl.program_id(1)
        hv = h_ref[...]
        pieces = _pieces(fs)
        first = lambda cols: (_dot(hv, wg_ref[:, cols]), _dot(hv, wu_ref[:, cols]))
        total = None
        ahead = first(pieces[0])
        for k, cols in enumerate(pieces):
            g, u = ahead
            if k + 1 < len(pieces):
                ahead = first(pieces[k + 1])
            g_ref[:, cols] = g.astype(BF16)
            u_ref[:, cols] = u.astype(BF16)
            act = (g * _sigmoid(g)) * u
            part = _dot(act.astype(BF16), wd_ref[cols, :])
            total = part if total is None else total + part

        @pl.when(j == 0)
        def _():
            y_ref[...] = total

        @pl.when(j > 0)
        def _():
            y_ref[...] += total

    tile = pl.BlockSpec((tm, d), lambda i, j: (i, 0))
    hid = pl.BlockSpec((None, tm, fs), lambda i, j: (j, i, 0))
    w_in_spec = pl.BlockSpec((None, d, fs), lambda i, j: (j, 0, 0))
    hid_shape = jax.ShapeDtypeStruct((n_chunks, s_len, fs), BF16)
    return pl.pallas_call(
        body, name=name, grid=(s_len // tm, n_chunks),
        in_specs=[tile, w_in_spec, w_in_spec, pl.BlockSpec((None, fs, d), lambda i, j: (j, 0, 0))],
        out_specs=[hid, hid, tile],
        out_shape=[hid_shape, hid_shape, jax.ShapeDtypeStruct((s_len, d), F32)],
        compiler_params=_params(2),
    )(h, w_gate, w_up, w_down)


def _ffn_bwd(dy, g_pre, u_pre, w_gate, w_up, w_down, name):
    s_len, d = dy.shape
    n_chunks, _, fs = w_gate.shape
    tm = FFN_TILE

    def body(dy_ref, g_ref, u_ref, wg_ref, wu_ref, wd_ref, dh_ref, dg_ref, du_ref, a_ref):
        j = pl.program_id(1)
        dyv = dy_ref[...]
        pieces = _pieces(fs)
        first = lambda cols: _dot_nt(dyv, wd_ref[cols, :])
        total = None
        ahead = first(pieces[0])
        for k, cols in enumerate(pieces):
            da = ahead
            if k + 1 < len(pieces):
                ahead = first(pieces[k + 1])
            g = g_ref[:, cols].astype(F32)
            u = u_ref[:, cols].astype(F32)
            sg = _sigmoid(g)
            silu = g * sg
            dg = (da * u * (sg * (1.0 + g * (1.0 - sg)))).astype(BF16)
            du = (da * silu).astype(BF16)
            dg_ref[:, cols] = dg
            du_ref[:, cols] = du
            a_ref[:, cols] = (silu * u).astype(BF16)
            part = _dot_nt(dg, wg_ref[:, cols]) + _dot_nt(du, wu_ref[:, cols])
            total = part if total is None else total + part

        @pl.when(j == 0)
        def _():
            dh_ref[...] = total

        @pl.when(j > 0)
        def _():
            dh_ref[...] += total

    tile = pl.BlockSpec((tm, d), lambda i, j: (i, 0))
    hid = pl.BlockSpec((None, tm, fs), lambda i, j: (j, i, 0))
    w_in_spec = pl.BlockSpec((None, d, fs), lambda i, j: (j, 0, 0))
    hid_shape = jax.ShapeDtypeStruct((n_chunks, s_len, fs), BF16)
    return pl.pallas_call(
        body, name=name, grid=(s_len // tm, n_chunks),
        in_specs=[tile, hid, hid, w_in_spec, w_in_spec, pl.BlockSpec((None, fs, d), lambda i, j: (j, 0, 0))],
        out_specs=[tile, hid, hid, hid],
        out_shape=[jax.ShapeDtypeStruct((s_len, d), F32), hid_shape, hid_shape, hid_shape],
        compiler_params=_params(2),
    )(dy, g_pre, u_pre, w_gate, w_up, w_down)


def _ffn_wgrads(ht, dg, du, act, dy, tag, after=()):
    n_chunks, s_len, fs = dg.shape
    d = ht.shape[0]
    tok = lambda ts: pl.BlockSpec((ts, d), lambda c, s: (s, 0))
    tok_t = lambda ts: pl.BlockSpec((d, ts), lambda c, s: (0, s))
    hid = lambda ts: pl.BlockSpec((None, ts, fs), lambda c, s: (c, s, 0))
    d_up = pl.BlockSpec((None, d, fs), lambda c, s: (c, 0, 0))
    d_down = pl.BlockSpec((None, fs, d), lambda c, s: (c, 0, 0))
    dwg = _wgrad(ht, dg, tok_t, hid, (n_chunks, d, fs), d_up, (d, fs), n_chunks, tag + "_dwg", True, after)
    dwu = _wgrad(ht, du, tok_t, hid, (n_chunks, d, fs), d_up, (d, fs), n_chunks, tag + "_dwu", True, after)
    dwd = _wgrad(act, dy, hid, tok, (n_chunks, fs, d), d_down, (fs, d), n_chunks, tag + "_dwd", False, after)
    return dwg, dwu, dwd


def _band_bias():
    qi = lax.broadcasted_iota(jnp.int32, (ATTN_BLOCK, 2 * ATTN_BLOCK), 0)
    kj = lax.broadcasted_iota(jnp.int32, (ATTN_BLOCK, 2 * ATTN_BLOCK), 1)
    band = (kj >= qi) & (kj <= qi + ATTN_BLOCK)
    return jnp.where(band, 0.0, NEG), jnp.where(band & (kj >= ATTN_BLOCK), 0.0, NEG)


def _qkv_specs(slab_of, sections):
    def spec(sect, back):
        return pl.BlockSpec((SLAB, HEAD_DIM),
                            lambda h, s, g: (jnp.maximum(slab_of(s) - back, 0), (sect * N_GROUPS + g) * HEADS + h))
    return [spec(sect, back) for sect, back in sections]


HAT_BLOCKS = [(0, 0), (1, 0), (2, 0), (1, 1), (2, 1)]


def _stage_keys(k_ref, v_ref, kp_ref, vp_ref, kbuf, vbuf, dil, n):
    run = SLAB // dil
    for r in range(dil):
        own, before = pl.ds(r * run, run), pl.ds(2 * r * run, run)
        kbuf[pl.ds((2 * r + 1) * run, run), :] = k_ref[own, :]
        vbuf[pl.ds((2 * r + 1) * run, run), :] = v_ref[own, :]

        @pl.when(n > 0)
        def _():
            kbuf[before, :] = kp_ref[own, :]
            vbuf[before, :] = vp_ref[own, :]

        @pl.when(n == 0)
        def _():
            kbuf[before, :] = jnp.zeros((run, HEAD_DIM), BF16)
            vbuf[before, :] = jnp.zeros((run, HEAD_DIM), BF16)


def _for_each_tile(dil, n, first_fn, rest_fn):
    run = SLAB // dil
    bias, first_bias = _band_bias()
    tiles = []
    for jj in range(run // ATTN_BLOCK):
        start = jj * ATTN_BLOCK
        tile_bias = jnp.where(n == 0, first_bias, bias) if jj == 0 else bias
        for r in range(dil):
            tiles.append((pl.ds(r * run + start, ATTN_BLOCK),
                          pl.ds((2 * r + 1) * run - ATTN_BLOCK + start, 2 * ATTN_BLOCK), tile_bias))
    ahead = first_fn(*tiles[0])
    for t, tile in enumerate(tiles):
        begun = ahead
        if t + 1 < len(tiles):
            ahead = first_fn(*tiles[t + 1])
        rest_fn(*tile, begun)


def _attn_fwd(hat, name):
    s_len = hat.shape[0]
    e = HEAD_DIM
    n_slabs = s_len // SLAB

    def body(q_ref, k_ref, v_ref, kp_ref, vp_ref, o_ref, lse_ref, kbuf, vbuf, m_s, l_s, acc_s, m_p, l_p, acc_p, tmp_s):
        n, grp = pl.program_id(1), pl.program_id(2)

        def run(gi, dil):
            _stage_keys(k_ref, v_ref, kp_ref, vp_ref, kbuf, vbuf, dil, n)

            def scores(q_rows, kv_rows, bias):
                return _dot_nt(q_ref[q_rows, :], kbuf[kv_rows, :])

            def rest(q_rows, kv_rows, bias, qk):
                s = qk * ATTN_SCALE + bias
                m = jnp.max(s, axis=-1, keepdims=True)
                p = jnp.exp(s - m)
                m_p[q_rows, :] = jnp.broadcast_to(m, (ATTN_BLOCK, e))
                l_p[q_rows, :] = jnp.broadcast_to(jnp.sum(p, axis=-1, keepdims=True), (ATTN_BLOCK, e))
                acc_p[q_rows, :] = _dot(p.astype(BF16), vbuf[kv_rows, :])

            _for_each_tile(dil, n, scores, rest)
            _to_token_order(m_s.at[gi], m_p, dil, tmp_s)
            _to_token_order(l_s.at[gi], l_p, dil, tmp_s)
            _to_token_order(acc_s.at[gi], acc_p, dil, tmp_s)

        for gi, dil in enumerate(DILATIONS):
            pl.when(grp == gi)(lambda gi=gi, dil=dil: run(gi, dil))

        @pl.when(grp == N_GROUPS - 1)
        def _():
            m_all = jnp.maximum(jnp.maximum(m_s[0], m_s[1]), m_s[2])
            den = jnp.zeros((SLAB, e), F32)
            num = jnp.zeros((SLAB, e), F32)
            for gi in range(N_GROUPS):
                w = jnp.exp(m_s[gi] - m_all)
                den += l_s[gi] * w
                num += acc_s[gi] * w
            o_ref[...] = (num / den).astype(BF16)
            lse_ref[...] = m_all + jnp.log(den)

    out = pl.BlockSpec((SLAB, e), lambda h, n, g: (n, h))
    return pl.pallas_call(
        body, name=name, grid=(HEADS, n_slabs, N_GROUPS),
        in_specs=_qkv_specs(lambda n: n, HAT_BLOCKS),
        out_specs=[out, out],
        out_shape=[jax.ShapeDtypeStruct((s_len, HEADS * e), BF16), jax.ShapeDtypeStruct((s_len, HEADS * e), F32)],
        scratch_shapes=[pltpu.VMEM((2 * SLAB, e), BF16), pltpu.VMEM((2 * SLAB, e), BF16),
                        pltpu.VMEM((N_GROUPS, SLAB, e), F32), pltpu.VMEM((N_GROUPS, SLAB, e), F32),
                        pltpu.VMEM((N_GROUPS, SLAB, e), F32)]
        + [pltpu.VMEM((SLAB, e), F32)] * 4,
        compiler_params=_params(3),
    )(hat, hat, hat, hat, hat)


def _attn_bwd(qkv, hat, d_out, out, lse, q_norm, k_norm, name):
    s_len = qkv.shape[0]
    e = HEAD_DIM
    n_slabs = s_len // SLAB

    def body(q_ref, k_ref, v_ref, kp_ref, vp_ref, qraw_ref, kraw_ref, do_ref, o_ref, lse_ref, qn_ref, kn_ref,
             dq_ref, dk_ref, dv_ref, st_ref, kbuf, vbuf, stat_s, dqs, dkb, dvb, dk_tok, dv_tok, carry,
             do_p, stat_p, dq_p, dk_p, dv_p, tmp_s, do16_p):
        head, step, grp = pl.program_id(0), pl.program_id(1), pl.program_id(2)
        n = n_slabs - 1 - step
        dkb[...] = jnp.zeros_like(dkb)
        dvb[...] = jnp.zeros_like(dvb)
        @pl.when(grp == 0)
        def _():
            lane = lax.broadcasted_iota(jnp.int32, (SLAB, e), 1)
            stat_s[...] = jnp.where(lane < e // 2, lse_ref[...],
                                    jnp.sum(do_ref[...] * o_ref[...].astype(F32), axis=-1, keepdims=True))

        @pl.when((head == 0) & (step == 0) & (grp == 0))
        def _():
            st_ref[...] = jnp.zeros_like(st_ref)

        def run(gi, dil):
            seg = SLAB // dil
            _stage_keys(k_ref, v_ref, kp_ref, vp_ref, kbuf, vbuf, dil, n)

            @pl.when(step == 0)
            def _():
                carry[gi] = jnp.zeros((2, SLAB, e), F32)

            _to_residue_order(do_p, do_ref, dil, tmp_s)
            do16_p[...] = do_p[...].astype(BF16)
            _to_residue_order(stat_p, stat_s, dil, tmp_s)

            def scores(q_rows, kv_rows, bias):
                return _dot_nt(q_ref[q_rows, :], kbuf[kv_rows, :]), _dot_nt(do16_p[q_rows, :], vbuf[kv_rows, :])

            def rest(q_rows, kv_rows, bias, begun):
                qk, dp = begun
                q = q_ref[q_rows, :]
                k = kbuf[kv_rows, :]
                stat = stat_p[q_rows, :]
                p = jnp.exp(qk * ATTN_SCALE + bias - stat[:, 0:1])
                ds = (p * (dp - stat[:, e // 2:e // 2 + 1]) * ATTN_SCALE).astype(BF16)
                dq_p[q_rows, :] = _dot(ds, k)
                dkb[kv_rows, :] += _dot_tn(ds, q)
                dvb[kv_rows, :] += _dot_tn(p.astype(BF16), do16_p[q_rows, :])

            _for_each_tile(dil, n, scores, rest)
            for r in range(dil):
                own, before = pl.ds((2 * r + 1) * seg, seg), pl.ds(2 * r * seg, seg)
                kept = pl.ds(r * seg, seg)
                dk_p[kept, :] = dkb[own, :] + carry.at[gi, 0][kept, :]
                dv_p[kept, :] = dvb[own, :] + carry.at[gi, 1][kept, :]
                carry.at[gi, 0][kept, :] = dkb[before, :]
                carry.at[gi, 1][kept, :] = dvb[before, :]
            _to_token_order(dqs, dq_p, dil, tmp_s)
            _to_token_order(dk_tok, dk_p, dil, tmp_s)
            _to_token_order(dv_tok, dv_p, dil, tmp_s)

            def norm_bwd(raw, gain, d_hat):
                r = _rms(raw)
                y = raw * r
                dy = d_hat * gain
                return r * (dy - y * jnp.mean(dy * y, axis=-1, keepdims=True)), jnp.sum(d_hat * y, axis=0, keepdims=True)

            dq, dqn = norm_bwd(qraw_ref[...], qn_ref[...], dqs[...])
            dk, dkn = norm_bwd(kraw_ref[...], kn_ref[...], dk_tok[...])
            dq_ref[...] = dq.astype(BF16)
            dk_ref[...] = dk.astype(BF16)
            dv_ref[...] = dv_tok[...].astype(BF16)
            st_ref[0:1, :] += dqn
            st_ref[1:2, :] += dkn

        for gi, dil in enumerate(DILATIONS):
            pl.when(grp == gi)(lambda gi=gi, dil=dil: run(gi, dil))

    slab_of = lambda s: n_slabs - 1 - s
    small = pl.BlockSpec((1, e), lambda h, s, g: (0, 0))
    head_blk = pl.BlockSpec((SLAB, e), lambda h, s, g: (slab_of(s), h))
    grad_blk = pl.BlockSpec((SLAB, e), lambda h, s, g: (slab_of(s), g * HEADS + h))
    grad_shape = jax.ShapeDtypeStruct((s_len, QKV), BF16)
    return pl.pallas_call(
        body, name=name, grid=(HEADS, n_slabs, N_GROUPS),
        in_specs=(_qkv_specs(slab_of, HAT_BLOCKS) + _qkv_specs(slab_of, [(0, 0), (1, 0)])
                  + [head_blk, head_blk, head_blk, small, small]),
        out_specs=[grad_blk, grad_blk, grad_blk, pl.BlockSpec((8, e), lambda h, s, g: (0, 0))],
        out_shape=[grad_shape, grad_shape, grad_shape, jax.ShapeDtypeStruct((8, e), F32)],
        scratch_shapes=[pltpu.VMEM((2 * SLAB, e), BF16), pltpu.VMEM((2 * SLAB, e), BF16), pltpu.VMEM((SLAB, e), F32),
                        pltpu.VMEM((SLAB, e), F32), pltpu.VMEM((2 * SLAB, e), F32), pltpu.VMEM((2 * SLAB, e), F32),
                        pltpu.VMEM((SLAB, e), F32), pltpu.VMEM((SLAB, e), F32),
                        pltpu.VMEM((N_GROUPS, 2, SLAB, e), F32)]
        + [pltpu.VMEM((SLAB, e), F32)] * 6 + [pltpu.VMEM((SLAB, e), BF16)],
        compiler_params=_params(3),
    )(hat, hat, hat, hat, hat, qkv, qkv, d_out, out, lse, q_norm, k_norm)


def _shift_rows(x, by, edge, forward):
    t_len = x.shape[0]
    row = lax.broadcasted_iota(jnp.int32, x.shape, 0)
    if forward:
        out = pltpu.roll(x, by, 0)
        for i in range(by):
            out = jnp.where(row == i, edge[8 - by + i:8 - by + i + 1, :], out)
    else:
        out = pltpu.roll(x, t_len - by, 0)
        for i in range(by):
            out = jnp.where(row == t_len - by + i, edge[i:i + 1, :], out)
    return out


def _mix_fwd(x, o, rest, mod, mod_next, conv_w, w_attn, w_conv, w_out, name):
    s_len, d = x.shape
    tm = MIX_TILE
    a_w = o.shape[1]

    def body(x_ref, o_ref, u_ref, b_ref, c_ref, ga_ref, gc_ref, mod_ref, modn_ref, cw_ref, wa_ref, wc_ref, wo_ref,
             xo_ref, z_ref, ya_ref, yc_ref, conv_ref, yb_ref, m_ref, h_ref, ht_ref, carry):
        @pl.when(pl.program_id(0) == 0)
        def _():
            carry[...] = jnp.zeros_like(carry)

        xc = c_ref[...].astype(F32) * u_ref[...].astype(F32)
        edge = carry[...]
        conv = (_shift_rows(xc, 2, edge, True) * cw_ref[0:1, :] + _shift_rows(xc, 1, edge, True) * cw_ref[1:2, :]
                + xc * cw_ref[2:3, :])
        carry[...] = xc[tm - 8:tm, :]
        yb = (b_ref[...].astype(F32) * conv).astype(BF16)
        ya = _dot(o_ref[...], wa_ref[...])
        yc = _dot(yb, wc_ref[...])
        merged = (_sigmoid(ga_ref[...].astype(F32)) * ya + _sigmoid(gc_ref[...].astype(F32)) * yc).astype(BF16)
        z = _dot(merged, wo_ref[...])
        xo = x_ref[...] + mod_ref[2:3, :] * z
        xo_ref[...] = xo
        hn = ((xo * _rms(xo)) * modn_ref[3:4, :]) * (1.0 + modn_ref[1:2, :]) + modn_ref[0:1, :]
        h_ref[...] = hn.astype(BF16)
        ht_ref[...] = hn.T.astype(BF16)
        z_ref[...] = z.astype(BF16)
        ya_ref[...] = ya.astype(BF16)
        yc_ref[...] = yc.astype(BF16)
        conv_ref[...] = conv.astype(BF16)
        yb_ref[...] = yb
        m_ref[...] = merged

    tile = pl.BlockSpec((tm, d), lambda i: (i, 0))
    sect = lambda k: pl.BlockSpec((tm, d), lambda i: (i, k))
    att = pl.BlockSpec((tm, a_w), lambda i: (i, 0))
    const = lambda shape: pl.BlockSpec(shape, lambda i: (0, 0))
    f32_out = jax.ShapeDtypeStruct((s_len, d), F32)
    b16_out = jax.ShapeDtypeStruct((s_len, d), BF16)
    return pl.pallas_call(
        body, name=name, grid=(s_len // tm,),
        in_specs=[tile, att, sect(0), sect(1), sect(2), sect(3), sect(4), const((8, d)), const((8, d)), const((8, d)),
                  const((a_w, d)), const((d, d)), const((d, d))],
        out_specs=[tile] * 7 + [tile, pl.BlockSpec((d, tm), lambda i: (0, i))],
        out_shape=[f32_out] + [b16_out] * 6 + [b16_out, jax.ShapeDtypeStruct((d, s_len), BF16)],
        scratch_shapes=[pltpu.VMEM((8, d), F32)],
        compiler_params=_params(1),
    )(x, o, rest, rest, rest, rest, rest, mod, mod_next, conv_w, w_attn, w_conv, w_out)


def _mix_bwd(dxo, ya, yc, conv, rest, mod, conv_w, w_attn, w_conv, w_out, a_w, name, after=()):
    s_len, d = dxo.shape
    tm = MIX_TILE
    n_tiles = s_len // tm

    def body(dxo_ref, ya_ref, yc_ref, conv_ref, u_ref, b_ref, c_ref, ga_ref, gc_ref, mod_ref, cw_ref,
             wa_ref, wc_ref, wo_ref, do_ref, drest_ref, dz_ref, dya_ref, dyc_ref, st_ref, carry):
        @pl.when(pl.program_id(0) == 0)
        def _():
            carry[...] = jnp.zeros_like(carry)
            st_ref[...] = jnp.zeros_like(st_ref)

        dz = (mod_ref[2:3, :] * dxo_ref[...]).astype(BF16)
        dz_ref[...] = dz
        dm = _dot_nt(dz, wo_ref[...])
        sa, sc = _sigmoid(ga_ref[...].astype(F32)), _sigmoid(gc_ref[...].astype(F32))
        dya = (dm * sa).astype(BF16)
        dyc = (dm * sc).astype(BF16)
        dya_ref[...] = dya
        dyc_ref[...] = dyc
        drest_ref[:, 3 * d:4 * d] = (dm * ya_ref[...].astype(F32) * (sa * (1.0 - sa))).astype(BF16)
        drest_ref[:, 4 * d:5 * d] = (dm * yc_ref[...].astype(F32) * (sc * (1.0 - sc))).astype(BF16)
        do_ref[...] = _dot_nt(dya, wa_ref[...])
        dyb = _dot_nt(dyc, wc_ref[...])
        drest_ref[:, d:2 * d] = (dyb * conv_ref[...].astype(F32)).astype(BF16)
        dconv = dyb * b_ref[...].astype(F32)
        edge = carry[...]
        sh1 = _shift_rows(dconv, 1, edge, False)
        sh2 = _shift_rows(dconv, 2, edge, False)
        carry[...] = dconv[0:8, :]
        dxc = dconv * cw_ref[2:3, :] + sh1 * cw_ref[1:2, :] + sh2 * cw_ref[0:1, :]
        u, c = u_ref[...].astype(F32), c_ref[...].astype(F32)
        xc = c * u
        drest_ref[:, 0:d] = (dxc * c).astype(BF16)
        drest_ref[:, 2 * d:3 * d] = (dxc * u).astype(BF16)
        st_ref[0:1, :] += jnp.sum(xc * sh2, axis=0, keepdims=True)
        st_ref[1:2, :] += jnp.sum(xc * sh1, axis=0, keepdims=True)
        st_ref[2:3, :] += jnp.sum(xc * dconv, axis=0, keepdims=True)

    rev = lambda i: n_tiles - 1 - i
    tile = pl.BlockSpec((tm, d), lambda i: (rev(i), 0))
    sect = lambda k: pl.BlockSpec((tm, d), lambda i: (rev(i), k))
    const = lambda shape: pl.BlockSpec(shape, lambda i: (0, 0))
    b16_out = jax.ShapeDtypeStruct((s_len, d), BF16)
    return pl.pallas_call(
        _ordered(body, 14, after), name=name, grid=(n_tiles,),
        in_specs=[tile, tile, tile, tile, sect(0), sect(1), sect(2), sect(3), sect(4), const((8, d)), const((8, d)),
                  const((a_w, d)), const((d, d)), const((d, d))] + [ANY] * len(after),
        out_specs=[pl.BlockSpec((tm, a_w), lambda i: (rev(i), 0)), pl.BlockSpec((tm, 5 * d), lambda i: (rev(i), 0)),
                   tile, tile, tile, const((8, d))],
        out_shape=[jax.ShapeDtypeStruct((s_len, a_w), F32), jax.ShapeDtypeStruct((s_len, 5 * d), BF16),
                   b16_out, b16_out, b16_out, jax.ShapeDtypeStruct((8, d), F32)],
        scratch_shapes=[pltpu.VMEM((8, d), F32)],
        compiler_params=_params(1),
    )(dxo, ya, yc, conv, rest, rest, rest, rest, rest, mod, conv_w, w_attn, w_conv, w_out, *after)


ADA_COLS = 128


def _ada_fwd(c_all, w_shard, b_shard, name):
    d, cols = w_shard.shape

    def body(c_ref, w_ref, b_ref, o_ref):
        cv = c_ref[...]
        o_ref[...] = jnp.dot(cv * _sigmoid(cv), w_ref[...], preferred_element_type=F32,
                             precision=lax.Precision.HIGHEST) + b_ref[...]

    return pl.pallas_call(
        body, name=name, grid=(cols // ADA_COLS,),
        in_specs=[pl.BlockSpec((8, d), lambda j: (0, 0)), pl.BlockSpec((d, ADA_COLS), lambda j: (0, j)),
                  pl.BlockSpec((1, ADA_COLS), lambda j: (0, j))],
        out_specs=pl.BlockSpec((8, ADA_COLS), lambda j: (0, j)),
        out_shape=jax.ShapeDtypeStruct((8, cols), F32),
        compiler_params=_params(1),
    )(c_all, w_shard, b_shard)


def _ada_bwd(c_all, dmod_shard, w, m, v, name):
    d, cols = w.shape

    def body(c_ref, dm_ref, w_ref, m_ref, v_ref, g_ref, d_ref, nm_ref, nv_ref):
        cv = c_ref[...]
        g = lax.dot_general(cv * _sigmoid(cv), dm_ref[...], (((0,), (0,)), ((), ())),
                            preferred_element_type=F32, precision=lax.Precision.HIGHEST)
        g_ref[...] = g
        d_ref[...], nm_ref[...], nv_ref[...] = _adamw_math(w_ref[...], g, m_ref[...], v_ref[...])

    blk = pl.BlockSpec((d, ADA_COLS), lambda j: (0, j))
    shape = jax.ShapeDtypeStruct((d, cols), F32)
    return pl.pallas_call(
        body, name=name, grid=(cols // ADA_COLS,),
        in_specs=[pl.BlockSpec((8, d), lambda j: (0, 0)), pl.BlockSpec((8, ADA_COLS), lambda j: (0, j)), blk, blk, blk],
        out_specs=[blk] * 4, out_shape=[shape] * 4,
        compiler_params=_params(1),
    )(c_all, dmod_shard, w, m, v)


def _small_update(parts, w, m, v, name):
    n = w.shape[1]

    def body(p_ref, w_ref, m_ref, v_ref, g_ref, d_ref, nm_ref, nv_ref):
        g = p_ref[0:1, :]
        for i in range(1, 8):
            g = g + p_ref[i:i + 1, :]
        g_ref[...] = g
        d_ref[...], nm_ref[...], nv_ref[...] = _adamw_math(w_ref[...], g, m_ref[...], v_ref[...])

    shape = jax.ShapeDtypeStruct((1, n), F32)
    return pl.pallas_call(body, name=name, out_shape=[shape] * 4, compiler_params=_params())(parts, w, m, v)


def _cols_to_shards(w, n):
    r, nc = w.shape
    return w.reshape(r, n, nc // n).transpose(1, 0, 2)


def kernel(x, c, w_ada, b_ada, norm_ffn1, ffn1_w_gate, ffn1_w_up, ffn1_w_down, norm_mix, w_in, q_norm, k_norm, conv_w, w_attn_branch, w_conv_branch, w_out, norm_ffn2, ffn2_w_gate, ffn2_w_up, ffn2_w_down, loss_target, m_w_ada, m_b_ada, m_norm_ffn1, m_ffn1_w_gate, m_ffn1_w_up, m_ffn1_w_down, m_norm_mix, m_w_in, m_q_norm, m_k_norm, m_conv_w, m_w_attn_branch, m_w_conv_branch, m_w_out, m_norm_ffn2, m_ffn2_w_gate, m_ffn2_w_up, m_ffn2_w_down, v_w_ada, v_b_ada, v_norm_ffn1, v_ffn1_w_gate, v_ffn1_w_up, v_ffn1_w_down, v_norm_mix, v_w_in, v_q_norm, v_k_norm, v_conv_w, v_w_attn_branch, v_w_conv_branch, v_w_out, v_norm_ffn2, v_ffn2_w_gate, v_ffn2_w_up, v_ffn2_w_down):
    ix, iy, ic = _place()
    chip = 2 * ix + iy
    me = 4 * ix + 2 * iy + ic
    xs = x[0]
    target = loss_target[0]
    s_len, d = xs.shape
    ada_cols = w_ada.shape[2]
    conv_cols = conv_w.shape[2]

    conv_rows = jnp.zeros((8, conv_cols), F32).at[0:3].set(conv_w[0])
    small_in = jnp.concatenate([jnp.broadcast_to(c, (8, d)), conv_rows], axis=1)
    small_all = _allgather8(small_in, "gather_c").reshape(8, 8, d + conv_cols)
    c_all = small_all[:, 0, :d]
    conv_full = small_all[0::2, 0:3, d:].transpose(1, 0, 2).reshape(3, N_CHIPS * conv_cols)
    conv_pad = jnp.zeros((8, N_CHIPS * conv_cols), F32).at[0:3].set(conv_full)
    b_shard = lax.dynamic_slice(b_ada, (0, chip * ada_cols), (1, ada_cols))
    mod_part = _ada_fwd(c_all, w_ada[0], b_shard, "ada_fwd")
    mod_all = _allgather8(mod_part, "gather_mod").reshape(N_CHIPS, 2, 8, ada_cols)[:, 0]
    mod_mine = lax.dynamic_slice(mod_all, (0, me, 0), (N_CHIPS, 1, ada_cols)).reshape(9, d)

    def mod_rows(i, gain):
        return jnp.zeros((8, d), F32).at[0:3].set(mod_mine[3 * i:3 * i + 3]).at[3:4].set(gain)

    mod1, mod2, mod3 = mod_rows(0, norm_ffn1), mod_rows(1, norm_mix), mod_rows(2, norm_ffn2)

    to16 = lambda w: w[0].astype(BF16)
    wg1, wu1, wd1 = _gather_weights([to16(ffn1_w_gate), to16(ffn1_w_up), to16(ffn1_w_down)], [False] * 3,
                                    "gather_ffn1", 1)
    h1, h1t = _norm_mod(xs, mod1, "norm1")
    (w_in_full,) = _gather_weights([to16(w_in)], [True], "gather_w_in", 2, after=(wd1, h1))

    g1, u1, y1 = _ffn_fwd(h1, wg1, wu1, wd1, "ffn1_fwd")
    x1, h2, h2t = _norm_mod(xs, mod2, "norm2", prev=(y1, mod1, 0.5))
    qkv, rest, qkv_hat = _in_proj(h2, w_in_full, q_norm, k_norm, "in_proj")
    w_ab, w_cb_g, w_o_g, wg2, wu2, wd2 = _gather_weights(
        [to16(w_attn_branch), to16(w_conv_branch), to16(w_out),
         to16(ffn2_w_gate), to16(ffn2_w_up), to16(ffn2_w_down)], [True] + [False] * 5,
        "gather_rest", 3, after=(h2,))
    a_w = w_ab.shape[0]
    w_cb = w_cb_g.reshape(d, d)
    w_o = w_o_g.reshape(d, d)
    o, lse = _attn_fwd(qkv_hat, "attn_fwd")
    x2, z, ya, yc, conv, yb, merged, h3, h3t = _mix_fwd(x1, o, rest, mod2, mod3, conv_pad, w_ab, w_cb, w_o, "mix_fwd")
    g3, u3, y3 = _ffn_fwd(h3, wg2, wu2, wd2, "ffn2_fwd")
    dx3, dy3, loss_part = _loss_grad(x2, y3, mod3, target, "loss")

    c_idx = jnp.reshape(ic, (1,)).astype(jnp.int32)
    chip_idx = jnp.stack([chip, ic]).astype(jnp.int32)

    def pair_send(grads, tag, collective_id):
        return _rs_pair_exchange(grads, "rs_pair_" + tag, collective_id)

    def chip_send(grads, from_sibling, names, tag, collective_id, after):
        pair_sums = [_pair_add(g, r, c_idx, "pair_add_" + nm, after) for g, r, nm in zip(grads, from_sibling, names)]
        return pair_sums, _rs_chip_exchange(pair_sums, "rs_chips_" + tag, collective_id)

    def reduce_finish(pair_sums, from_chips, names, tag, after):
        totals = [_chip_add(p, r, chip_idx, "chip_add_" + nm, after)
                  for p, r, nm in zip(pair_sums, from_chips, names)]
        return dict(zip(names, _rs_share(totals, "rs_share_" + tag)))

    names_a = ["ffn2_w_gate", "ffn2_w_up", "ffn2_w_down"]
    names_b = ["w_in", "w_attn_branch", "w_conv_branch", "w_out"]
    names_c = ["ffn1_w_gate", "ffn1_w_up", "ffn1_w_down"]

    dh3, dg3, du3, a3 = _ffn_bwd(dy3, g3, u3, wg2, wu2, wd2, "ffn2_bwd")
    grads_a = list(_ffn_wgrads(h3t, dg3, du3, a3, dy3, "ffn2"))
    sibling_a = pair_send(grads_a, "a", 7)
    dx2, st3 = _norm_bwd(dh3, x2, mod3, dx3, y3, 0.5, "norm3_bwd")
    sums_a, chips_a = chip_send(grads_a, sibling_a, names_a, "a", 4, after=(dx2,))

    do, drest, dz, dya, dyc, st_conv = _mix_bwd(dx2, ya, yc, conv, rest, mod2, conv_pad, w_ab, w_cb, w_o, a_w,
                                                "mix_bwd", after=tuple(sums_a))
    dq, dk, dv, st_qk = _attn_bwd(qkv, qkv_hat, do, o, lse, q_norm, k_norm, "attn_bwd")
    tok = lambda width: (lambda ts: pl.BlockSpec((ts, width), lambda cc, s: (s, 0)))
    colblk = lambda width: (lambda ts: pl.BlockSpec((ts, width), lambda cc, s: (s, cc)))
    tok_t = lambda ts: pl.BlockSpec((d, ts), lambda cc, s: (0, s))
    whole = pl.BlockSpec((d, QKV), lambda cc, s: (0, 0))
    dw_in = [_wgrad(h2t, part, tok_t, tok(QKV), (d, QKV), whole, (d, QKV), 1, "dw_in_" + nm, True)
             for part, nm in ((dq, "q"), (dk, "k"), (dv, "v"))]
    dw_in.append(_wgrad(h2t, drest, tok_t, colblk(d), (d, 5 * d), pl.BlockSpec((d, d), lambda cc, s: (0, cc)),
                        (d, d), 5, "dw_in_rest", True))
    dw_in = _cols_to_shards(jnp.concatenate(dw_in, axis=1), N_CHIPS)
    shard_w = d // N_CHIPS
    dw_ab = _wgrad(o, dya, tok(a_w), colblk(shard_w), (a_w, d), pl.BlockSpec((a_w, shard_w), lambda cc, s: (0, cc)),
                   (a_w, shard_w), N_CHIPS, "dw_attn_branch")
    dw_ab = _cols_to_shards(dw_ab, N_CHIPS)
    row_out = pl.BlockSpec((None, shard_w, d), lambda cc, s: (cc, 0, 0))
    dw_cb = _wgrad(yb, dyc, colblk(shard_w), tok(d), (N_CHIPS, shard_w, d), row_out, (shard_w, d), N_CHIPS, "dw_conv_branch")
    dw_o = _wgrad(merged, dz, colblk(shard_w), tok(d), (N_CHIPS, shard_w, d), row_out, (shard_w, d), N_CHIPS, "dw_out")
    shard_grads = reduce_finish(sums_a, chips_a, names_a, "a", after=(dw_in, dw_o))
    grads_b = [dw_in, dw_ab, dw_cb, dw_o]
    sibling_b = pair_send(grads_b, "b", 8)

    dh2 = _in_proj_bwd(dq, dk, dv, drest, w_in_full, "in_proj_bwd")
    sums_b, chips_b = chip_send(grads_b, sibling_b, names_b, "b", 5, after=(dh2,))
    dx1, st2, dy1 = _norm_bwd(dh2, x1, mod2, dx2, z, 1.0, "norm2_bwd", after=tuple(sums_b), prev=(mod1, 0.5))
    dh1, dg1, du1, a1 = _ffn_bwd(dy1, g1, u1, wg1, wu1, wd1, "ffn1_bwd")
    dx0, st1 = _norm_bwd(dh1, xs, mod1, dx1, y1, 0.5, "norm1_bwd")
    grads_c = list(_ffn_wgrads(h1t, dg1, du1, a1, dy1, "ffn1"))
    sibling_c = pair_send(grads_c, "c", 9)
    shard_grads.update(reduce_finish(sums_b, chips_b, names_b, "b", after=tuple(grads_c)))

    dmod = jnp.concatenate([st1[0:3], st2[0:3], st3[0:3]], axis=0).reshape(1, 9 * d)
    loss_cols = jnp.zeros((1, HEAD_DIM), F32).at[0, 0].set(jnp.sum(loss_part))
    small = jnp.concatenate([dmod, st1[3:4], st2[3:4], st3[3:4], st_qk[0:1], st_qk[1:2],
                             st_conv[0:3].reshape(1, 3 * d), loss_cols], axis=1)
    small_all = _allgather8(jnp.broadcast_to(small, (8, small.shape[1])), "gather_small").reshape(8, 8, -1)[:, 0]
    loss = (0.5 / d) * jnp.sum(small_all[:, -HEAD_DIM])
    small_all = small_all[:, :-HEAD_DIM]
    dmod_all = small_all[:, :9 * d]
    dmod_shard = lax.dynamic_slice(dmod_all, (0, chip * ada_cols), (8, ada_cols))
    g_w_ada, d_w_ada, nm_w_ada, nv_w_ada = _ada_bwd(c_all, dmod_shard, w_ada[0], m_w_ada[0], v_w_ada[0], "ada_bwd")

    vec_names = ["b_ada", "norm_ffn1", "norm_mix", "norm_ffn2", "q_norm", "k_norm"]
    vec_w = [b_ada, norm_ffn1, norm_mix, norm_ffn2, q_norm, k_norm]
    vec_m = [m_b_ada, m_norm_ffn1, m_norm_mix, m_norm_ffn2, m_q_norm, m_k_norm]
    vec_v = [v_b_ada, v_norm_ffn1, v_norm_mix, v_norm_ffn2, v_q_norm, v_k_norm]
    n_vec = sum(w.shape[1] for w in vec_w)
    cat = lambda arrs: jnp.concatenate(arrs, axis=1)
    vec_out = _small_update(small_all[:, :n_vec], cat(vec_w), cat(vec_m), cat(vec_v), "small_update")
    conv_parts = small_all[:, n_vec:].reshape(8, 3, N_CHIPS * conv_cols)
    conv_parts = lax.dynamic_slice(conv_parts, (0, 0, chip * conv_cols), (8, 3, conv_cols)).reshape(8, 3 * conv_cols)
    flat3 = lambda w: w[0].reshape(1, 3 * conv_cols)
    conv_out = _small_update(conv_parts, flat3(conv_w), flat3(m_conv_w), flat3(v_conv_w), "conv_update")

    res = {"w_ada": [t[None] for t in (g_w_ada, d_w_ada, nm_w_ada, nv_w_ada)],
           "conv_w": [t.reshape(1, 3, conv_cols) for t in conv_out]}
    off = 0
    for nm, w in zip(vec_names, vec_w):
        width = w.shape[1]
        res[nm] = [t[:, off:off + width] for t in vec_out]
        off += width
    big = {"ffn1_w_gate": (ffn1_w_gate, m_ffn1_w_gate, v_ffn1_w_gate), "ffn1_w_up": (ffn1_w_up, m_ffn1_w_up, v_ffn1_w_up),
           "ffn1_w_down": (ffn1_w_down, m_ffn1_w_down, v_ffn1_w_down), "w_in": (w_in, m_w_in, v_w_in),
           "w_attn_branch": (w_attn_branch, m_w_attn_branch, v_w_attn_branch),
           "w_conv_branch": (w_conv_branch, m_w_conv_branch, v_w_conv_branch), "w_out": (w_out, m_w_out, v_w_out),
           "ffn2_w_gate": (ffn2_w_gate, m_ffn2_w_gate, v_ffn2_w_gate), "ffn2_w_up": (ffn2_w_up, m_ffn2_w_up, v_ffn2_w_up),
           "ffn2_w_down": (ffn2_w_down, m_ffn2_w_down, v_ffn2_w_down)}
    def update(nm, after=()):
        w, m, v = big[nm]
        flip = w.shape[2] % LANE_TILE != 0 and w.shape[1] % LANE_TILE == 0
        turn = (lambda a: a.T) if flip else (lambda a: a)
        outs = _adamw(turn(w[0]), turn(shard_grads[nm]), turn(m[0]), turn(v[0]), "adamw_" + nm, after)
        res[nm] = [turn(t)[None] for t in outs]
        return outs[3]

    last = tuple(shard_grads[nm] for nm in names_b)
    for nm in names_a:
        last = (update(nm, last),)
    sums_c, chips_c = chip_send(grads_c, sibling_c, names_c, "c", 6, after=last)
    last = tuple(sums_c)
    for nm in names_b:
        last = (update(nm, last),)
    shard_grads.update(reduce_finish(sums_c, chips_c, names_c, "c", after=last))
    for nm in names_c:
        update(nm)

    order = ["w_ada", "b_ada", "norm_ffn1", "ffn1_w_gate", "ffn1_w_up", "ffn1_w_down", "norm_mix", "w_in", "q_norm",
             "k_norm", "conv_w", "w_attn_branch", "w_conv_branch", "w_out", "norm_ffn2", "ffn2_w_gate", "ffn2_w_up",
             "ffn2_w_down"]
    return (loss, dx0[None], *[res[nm][0] for nm in order], *[res[nm][1] for nm in order],
            *[res[nm][2] for nm in order], *[res[nm][3] for nm in order])
```

```python
import jax
import jax.numpy as jnp
from jax import lax
from jax.experimental import pallas as pl
from jax.experimental.pallas import tpu as pltpu
from jax.experimental.pallas import tpu_sc as plsc

F32 = jnp.float32
BF16 = jnp.bfloat16
MESH = pl.DeviceIdType.MESH
ANY = pl.BlockSpec(memory_space=pl.ANY)

NORM_EPS = 1e-6
LANE_TILE = 128
HEAD_DIM = 128
N_GROUPS = 3
HEADS = 4
DILATIONS = (1, 4, 16)
ATTN_BLOCK = 128
SLAB = ATTN_BLOCK * max(DILATIONS)
QKV = N_GROUPS * HEADS * HEAD_DIM
ATTN_SCALE = HEAD_DIM ** -0.5
NEG = -1e30
N_CHIPS = 4

ADAM_LR = 0.001
ADAM_B1 = 0.9
ADAM_B2 = 0.999
ADAM_EPS = 1e-08
ADAM_WD = 0.01
ADAM_STEP = 10

VMEM_LIMIT_BYTES = 56 * 1024 * 1024
TOKEN_TILE = 512
FFN_TILE = 1024
PROJ_TILE = 2048
WGRAD_TILE = 2048
IN_BLOCK = 512
MIX_TILE = 512
ACC_PIECES = 4
ADAMW_TILE_BYTES = 3 * 512 * 1024


def _params(n_axes=0):
    return pltpu.CompilerParams(
        dimension_semantics=("arbitrary",) * n_axes if n_axes else None,
        vmem_limit_bytes=VMEM_LIMIT_BYTES)


def _dot(a, b):
    return jnp.dot(a, b, preferred_element_type=F32)


def _dot_nt(a, b):
    return lax.dot_general(a, b, (((1,), (1,)), ((), ())), preferred_element_type=F32)


def _dot_tn(a, b):
    return lax.dot_general(a, b, (((0,), (0,)), ((), ())), preferred_element_type=F32)


def _sigmoid(x):
    return 0.5 * jnp.tanh(0.5 * x) + 0.5


def _place():
    return lax.axis_index("x"), lax.axis_index("y"), lax.axis_index("c")


def _ordered(body, n_in, after):
    if not after:
        return body
    return lambda *refs: body(*refs[:n_in], *refs[n_in + len(after):])


def _allgather8(block, name):
    m_per, n = block.shape

    def body(x_ref, out_ref, send_sems, recv_sems, local_sem):
        x, y, c = _place()
        me, sibling = (x, y, c), (x, y, 1 - c)
        chips = [(1 - x, y), (x, 1 - y), (1 - x, 1 - y)]

        def rows(px, py, pc):
            return out_ref.at[pl.ds((4 * px + 2 * py + pc) * m_per, m_per), :]

        def copy(k, blk, to, src=None):
            return pltpu.make_async_remote_copy(
                src_ref=rows(*blk) if src is None else src, dst_ref=rows(*blk),
                send_sem=send_sems.at[k], recv_sem=recv_sems.at[k],
                device_id=to, device_id_type=MESH)

        mine = pltpu.make_async_copy(x_ref, rows(*me), local_sem)
        mine.start()
        first = [copy(0, me, sibling, src=x_ref)]
        first += [copy(1 + j, me, (*chip, c), src=x_ref) for j, chip in enumerate(chips)]
        for cp in first:
            cp.start()
        passed = [copy(4 + j, (*chip, c), sibling) for j, chip in enumerate(chips)]
        for j, chip in enumerate(chips):
            copy(1 + j, (*chip, c), me).wait_recv()
            passed[j].start()
        copy(0, sibling, me).wait_recv()
        for j, chip in enumerate(chips):
            copy(4 + j, (*chip, 1 - c), me).wait_recv()
        for cp in first + passed:
            cp.wait_send()
        mine.wait()

    return pl.pallas_call(
        body, name=name,
        out_shape=jax.ShapeDtypeStruct((8 * m_per, n), block.dtype),
        in_specs=[pl.BlockSpec(memory_space=pltpu.VMEM)],
        out_specs=pl.BlockSpec(memory_space=pltpu.VMEM),
        scratch_shapes=[pltpu.SemaphoreType.DMA((7,)), pltpu.SemaphoreType.DMA((7,)),
                        pltpu.SemaphoreType.DMA],
        compiler_params=_params(),
    )(block)


def _handshake(peers):
    barrier = pltpu.get_barrier_semaphore()
    for peer in peers:
        pl.semaphore_signal(barrier, inc=1, device_id=peer, device_id_type=MESH)
    pl.semaphore_wait(barrier, len(peers))


def _gather_weights(shards, by_cols, name, collective_id, after=()):
    n_arr = len(shards)

    def body(*refs):
        srcs, outs = refs[:n_arr], refs[n_arr + len(after):2 * n_arr + len(after)]
        send_sems, recv_sems, local_sems = refs[2 * n_arr + len(after):]
        x, y, c = _place()
        me_dev, sibling = (x, y, c), (x, y, 1 - c)
        chips = [(1 - x, y), (x, 1 - y), (1 - x, 1 - y)]
        me = 2 * x + y
        _handshake([sibling] + [(*chip, c) for chip in chips])

        def place(k, chip_idx, rows):
            if by_cols[k]:
                width = srcs[k].shape[1]
                return outs[k].at[rows, pl.ds(pl.multiple_of(chip_idx * width, 128), width)]
            return outs[k].at[chip_idx, rows]

        def copy(k, slot, chip_idx, half_sel, to, from_shard=False):
            half = srcs[k].shape[0] // 2
            rows = pl.ds(half_sel * half, half)
            dst = place(k, chip_idx, rows)
            return pltpu.make_async_remote_copy(
                src_ref=srcs[k].at[rows] if from_shard else dst, dst_ref=dst,
                send_sem=send_sems.at[6 * k + slot], recv_sem=recv_sems.at[6 * k + slot],
                device_id=to, device_id_type=MESH)

        own = [pltpu.make_async_copy(srcs[k], place(k, me, pl.ds(0, srcs[k].shape[0])), local_sems.at[k])
               for k in range(n_arr)]
        for cp in own:
            cp.start()
        sent = []
        for k in range(n_arr):
            for j, chip in enumerate(chips):
                sent.append(copy(k, j, me, c, (*chip, c), from_shard=True))
                sent[-1].start()
        for k in range(n_arr):
            for j, chip in enumerate(chips):
                chip_idx = 2 * chip[0] + chip[1]
                copy(k, j, chip_idx, c, me_dev).wait_recv()
                sent.append(copy(k, 3 + j, chip_idx, c, sibling))
                sent[-1].start()
        for k in range(n_arr):
            for j, chip in enumerate(chips):
                copy(k, 3 + j, 2 * chip[0] + chip[1], 1 - c, me_dev).wait_recv()
        for cp in sent:
            cp.wait_send()
        for cp in own:
            cp.wait()

    def gathered(k):
        r, cols = shards[k].shape
        return (r, N_CHIPS * cols) if by_cols[k] else (N_CHIPS, r, cols)

    return pl.kernel(
        body, name=name,
        out_type=[jax.ShapeDtypeStruct(gathered(k), shards[k].dtype) for k in range(n_arr)],
        mesh=plsc.ScalarSubcoreMesh(axis_name="sequencer", num_cores=1),
        scratch_types=[pltpu.SemaphoreType.DMA((6 * n_arr,)), pltpu.SemaphoreType.DMA((6 * n_arr,)),
                       pltpu.SemaphoreType.DMA((n_arr,))],
        compiler_params=pltpu.CompilerParams(collective_id=collective_id),
    )(*shards, *after)


def _rs_pair_exchange(grads, name, collective_id):
    n_arr = len(grads)

    def body(*refs):
        srcs, outs = refs[:n_arr], refs[n_arr:2 * n_arr]
        send_sems, recv_sems = refs[2 * n_arr:]
        x, y, c = _place()
        _handshake([(x, y, 1 - c)])
        cps = []
        for k in range(n_arr):
            half = srcs[k].shape[1] // 2
            cps.append(pltpu.make_async_remote_copy(
                src_ref=srcs[k].at[:, pl.ds((1 - c) * half, half)], dst_ref=outs[k],
                send_sem=send_sems.at[k], recv_sem=recv_sems.at[k],
                device_id=(x, y, 1 - c), device_id_type=MESH))
            cps[-1].start()
        for cp in cps:
            cp.wait_recv()
        for cp in cps:
            cp.wait_send()

    return pl.kernel(
        body, name=name,
        out_type=[jax.ShapeDtypeStruct((g.shape[0], g.shape[1] // 2, g.shape[2]), g.dtype) for g in grads],
        mesh=plsc.ScalarSubcoreMesh(axis_name="sequencer", num_cores=1),
        scratch_types=[pltpu.SemaphoreType.DMA((n_arr,)), pltpu.SemaphoreType.DMA((n_arr,))],
        compiler_params=pltpu.CompilerParams(collective_id=collective_id),
    )(*grads)


def _rs_chip_exchange(sums, name, collective_id):
    n_arr = len(sums)

    def body(*refs):
        srcs, outs = refs[:n_arr], refs[n_arr:2 * n_arr]
        send_sems, recv_sems = refs[2 * n_arr:]
        x, y, c = _place()
        chips = [(1 - x, y), (x, 1 - y), (1 - x, 1 - y)]
        _handshake([(*chip, c) for chip in chips])
        cps = []
        for k in range(n_arr):
            for j, chip in enumerate(chips):
                cps.append(pltpu.make_async_remote_copy(
                    src_ref=srcs[k].at[2 * chip[0] + chip[1]], dst_ref=outs[k].at[j],
                    send_sem=send_sems.at[3 * k + j], recv_sem=recv_sems.at[3 * k + j],
                    device_id=(*chip, c), device_id_type=MESH))
                cps[-1].start()
        for cp in cps:
            cp.wait_recv()
        for cp in cps:
            cp.wait_send()

    return pl.kernel(
        body, name=name,
        out_type=[jax.ShapeDtypeStruct((3,) + s.shape[1:], s.dtype) for s in sums],
        mesh=plsc.ScalarSubcoreMesh(axis_name="sequencer", num_cores=1),
        scratch_types=[pltpu.SemaphoreType.DMA((3 * n_arr,)), pltpu.SemaphoreType.DMA((3 * n_arr,))],
        compiler_params=pltpu.CompilerParams(collective_id=collective_id),
    )(*sums)


def _rs_share(totals, name):
    n_arr = len(totals)

    def body(*refs):
        outs = refs[n_arr:2 * n_arr]
        send_sems, recv_sems = refs[2 * n_arr:]
        x, y, c = _place()

        def half_rows(k, sel):
            return outs[k].at[sel]

        cps = []
        for k in range(n_arr):
            cps.append(pltpu.make_async_remote_copy(
                src_ref=half_rows(k, c), dst_ref=half_rows(k, c), send_sem=send_sems.at[k], recv_sem=recv_sems.at[k],
                device_id=(x, y, 1 - c), device_id_type=MESH))
            cps[-1].start()
        for k in range(n_arr):
            pltpu.make_async_remote_copy(
                src_ref=half_rows(k, c), dst_ref=half_rows(k, 1 - c), send_sem=send_sems.at[k],
                recv_sem=recv_sems.at[k], device_id=(x, y, 1 - c), device_id_type=MESH).wait_recv()
        for cp in cps:
            cp.wait_send()

    shared = pl.pallas_call(
        body, name=name,
        out_shape=[jax.ShapeDtypeStruct(t.shape, t.dtype) for t in totals],
        in_specs=[ANY] * n_arr, out_specs=[ANY] * n_arr,
        input_output_aliases={k: k for k in range(n_arr)},
        scratch_shapes=[pltpu.SemaphoreType.DMA((n_arr,)), pltpu.SemaphoreType.DMA((n_arr,))],
        compiler_params=_params(),
    )(*totals)
    return [t.reshape(2 * t.shape[1], t.shape[2]) for t in shared]


def _pair_add(grad, recv, c_idx, name, after=()):
    n, r, cols = grad.shape
    half = r // 2
    rows = half // 2

    def body(_, g_ref, r_ref, o_ref):
        o_ref[...] = (g_ref[...].astype(F32) + r_ref[...].astype(F32)).astype(o_ref.dtype)

    return pl.pallas_call(
        _ordered(body, 3, after), name=name,
        grid_spec=pltpu.PrefetchScalarGridSpec(
            num_scalar_prefetch=1, grid=(n, 2),
            in_specs=[pl.BlockSpec((None, None, rows, cols), lambda s, i, ci: (s, ci[0], i, 0)),
                      pl.BlockSpec((None, rows, cols), lambda s, i, ci: (s, i, 0))] + [ANY] * len(after),
            out_specs=pl.BlockSpec((None, rows, cols), lambda s, i, ci: (s, i, 0))),
        out_shape=jax.ShapeDtypeStruct((n, half, cols), BF16),
        compiler_params=_params(2),
    )(c_idx, grad.reshape(n, 2, half, cols), recv, *after)


def _chip_add(sums, recv, chip_and_core, name, after=()):
    _, half, cols = sums.shape
    rows = half // 2

    def body(_, s_ref, r0_ref, r1_ref, r2_ref, o_ref):
        o_ref[...] = ((s_ref[...].astype(F32) + r0_ref[...].astype(F32))
                      + r1_ref[...].astype(F32)) + r2_ref[...].astype(F32)

    def recv_spec(j):
        return pl.BlockSpec((None, rows, cols), lambda i, ci: (j, i, 0))

    return pl.pallas_call(
        _ordered(body, 5, after), name=name,
        grid_spec=pltpu.PrefetchScalarGridSpec(
            num_scalar_prefetch=1, grid=(2,),
            in_specs=[pl.BlockSpec((None, rows, cols), lambda i, ci: (ci[0], i, 0)),
                      recv_spec(0), recv_spec(1), recv_spec(2)] + [ANY] * len(after),
            out_specs=pl.BlockSpec((None, rows, cols), lambda i, ci: (ci[1], i, 0))),
        out_shape=jax.ShapeDtypeStruct((2, half, cols), F32),
        compiler_params=_params(1),
    )(chip_and_core, sums, recv, recv, recv, *after)


def _rms(x):
    return lax.rsqrt(jnp.mean(x * x, axis=-1, keepdims=True) + NORM_EPS)


def _norm_mod(x, mod, name, prev=None):
    s_len, d = x.shape
    tm = TOKEN_TILE

    def body(*refs):
        if prev is None:
            x_ref, mod_ref, h_ref, ht_ref = refs
            xv = x_ref[...]
        else:
            x_ref, y_ref, modp_ref, mod_ref, xo_ref, h_ref, ht_ref = refs
            xv = x_ref[...] + prev[2] * modp_ref[2:3, :] * y_ref[...]
            xo_ref[...] = xv
        n = (xv * _rms(xv)) * mod_ref[3:4, :]
        h = n * (1.0 + mod_ref[1:2, :]) + mod_ref[0:1, :]
        h_ref[...] = h.astype(BF16)
        ht_ref[...] = h.T.astype(BF16)

    tile = pl.BlockSpec((tm, d), lambda i: (i, 0))
    small = pl.BlockSpec((8, d), lambda i: (0, 0))
    h_specs = [tile, pl.BlockSpec((d, tm), lambda i: (0, i))]
    h_shapes = [jax.ShapeDtypeStruct((s_len, d), BF16), jax.ShapeDtypeStruct((d, s_len), BF16)]
    if prev is None:
        return pl.pallas_call(
            body, name=name, grid=(s_len // tm,), in_specs=[tile, small], out_specs=h_specs, out_shape=h_shapes,
            compiler_params=_params(1))(x, mod)
    return pl.pallas_call(
        body, name=name, grid=(s_len // tm,), in_specs=[tile, tile, small, small],
        out_specs=[tile] + h_specs, out_shape=[jax.ShapeDtypeStruct((s_len, d), F32)] + h_shapes,
        compiler_params=_params(1))(x, prev[0], prev[1], mod)


def _norm_bwd(dh, x, mod, dxo, y_raw, coef, name, after=(), prev=None):
    s_len, d = x.shape
    tm = TOKEN_TILE

    def body(*refs):
        if prev is None:
            dh_ref, x_ref, mod_ref, dxo_ref, y_ref, dx_ref, st_ref = refs
        else:
            dh_ref, x_ref, mod_ref, dxo_ref, y_ref, modp_ref, dx_ref, st_ref, dyp_ref = refs

        @pl.when(pl.program_id(0) == 0)
        def _():
            st_ref[...] = jnp.zeros_like(st_ref)

        xv, dhv, dxov = x_ref[...], dh_ref[...], dxo_ref[...]
        r = _rms(xv)
        xh = xv * r
        gain, scale = mod_ref[3:4, :], mod_ref[1:2, :]
        dn = dhv * (1.0 + scale)
        dxh = dn * gain
        dx = dxov + r * (dxh - xh * jnp.mean(dxh * xh, axis=-1, keepdims=True))
        dx_ref[...] = dx
        if prev is not None:
            dyp_ref[...] = (prev[1] * modp_ref[2:3, :] * dx).astype(BF16)
        st_ref[0:1, :] += jnp.sum(dhv, axis=0, keepdims=True)
        st_ref[1:2, :] += jnp.sum(dhv * (xh * gain), axis=0, keepdims=True)
        st_ref[2:3, :] += coef * jnp.sum(y_ref[...].astype(F32) * dxov, axis=0, keepdims=True)
        st_ref[3:4, :] += jnp.sum(dn * xh, axis=0, keepdims=True)

    tile = pl.BlockSpec((tm, d), lambda i: (i, 0))
    small = pl.BlockSpec((8, d), lambda i: (0, 0))
    operands = [dh, x, mod, dxo, y_raw] + ([] if prev is None else [prev[0]])
    in_specs = [tile, tile, small, tile, tile] + ([] if prev is None else [small])
    out_specs = [tile, small] + ([] if prev is None else [tile])
    out_shape = [jax.ShapeDtypeStruct((s_len, d), F32), jax.ShapeDtypeStruct((8, d), F32)]
    if prev is not None:
        out_shape.append(jax.ShapeDtypeStruct((s_len, d), BF16))
    return pl.pallas_call(
        _ordered(body, len(operands), after), name=name, grid=(s_len // tm,),
        in_specs=in_specs + [ANY] * len(after), out_specs=out_specs, out_shape=out_shape,
        compiler_params=_params(1),
    )(*operands, *after)


def _loss_grad(x, y, mod, target, name):
    s_len, d = x.shape
    tm = TOKEN_TILE

    def body(x_ref, y_ref, mod_ref, t_ref, do_ref, dy_ref, part_ref):
        @pl.when(pl.program_id(0) == 0)
        def _():
            part_ref[...] = jnp.zeros_like(part_ref)

        half_gate = 0.5 * mod_ref[2:3, :]
        err = (x_ref[...] + half_gate * y_ref[...]) - t_ref[...]
        do = err * (1.0 / d)
        do_ref[...] = do
        dy_ref[...] = (half_gate * do).astype(BF16)
        sq = err * err
        part_ref[...] += jnp.sum(sq.reshape(tm // 8, 8, d), axis=0)

    tile = pl.BlockSpec((tm, d), lambda i: (i, 0))
    small = pl.BlockSpec((8, d), lambda i: (0, 0))
    return pl.pallas_call(
        body, name=name, grid=(s_len // tm,),
        in_specs=[tile, tile, small, tile],
        out_specs=[tile, tile, small],
        out_shape=[jax.ShapeDtypeStruct((s_len, d), F32), jax.ShapeDtypeStruct((s_len, d), BF16),
                   jax.ShapeDtypeStruct((8, d), F32)],
        compiler_params=_params(1),
    )(x, y, mod, target)


def _adamw_math(w, g, m, v):
    m = ADAM_B1 * m + (1.0 - ADAM_B1) * g
    v = ADAM_B2 * v + (1.0 - ADAM_B2) * (g * g)
    m_hat = m / (1.0 - ADAM_B1 ** ADAM_STEP)
    v_hat = v / (1.0 - ADAM_B2 ** ADAM_STEP)
    delta = -ADAM_LR * (m_hat / (jnp.sqrt(v_hat) + ADAM_EPS) + ADAM_WD * w)
    return delta, m, v


def _adamw(w, g, m, v, name, after=()):
    r, cols = w.shape
    tr = max([t for t in (r // k for k in (1, 2, 4, 8, 16)) if t % 8 == 0 and r % t == 0
              and t * cols * 4 <= ADAMW_TILE_BYTES] or [r])

    def body(w_ref, g_ref, m_ref, v_ref, go_ref, d_ref, nm_ref, nv_ref):
        gv = g_ref[...]
        go_ref[...] = gv
        d_ref[...], nm_ref[...], nv_ref[...] = _adamw_math(w_ref[...], gv, m_ref[...], v_ref[...])

    tile = pl.BlockSpec((tr, cols), lambda i: (i, 0))
    shape = jax.ShapeDtypeStruct((r, cols), F32)
    return pl.pallas_call(
        _ordered(body, 4, after), name=name, grid=(r // tr,),
        in_specs=[tile] * 4 + [ANY] * len(after), out_specs=[tile] * 4, out_shape=[shape] * 4,
        compiler_params=_params(1),
    )(w, g, m, v, *after)


def _in_parts(tm, n_qkv, n_rest):
    def part(lo, n_blk):
        return pl.BlockSpec((tm, IN_BLOCK), lambda i, j: (i, jnp.clip(j - lo, 0, n_blk - 1)))
    return [part(0, n_qkv), part(n_qkv, n_qkv), part(2 * n_qkv, n_qkv), part(3 * n_qkv, n_rest)]


def _pick_part(j, n_qkv, refs, fn):
    bounds = [0, n_qkv, 2 * n_qkv, 3 * n_qkv]
    for p, ref in enumerate(refs):
        inside = j >= bounds[p]
        if p + 1 < len(refs):
            inside = inside & (j < bounds[p + 1])
        pl.when(inside)(lambda ref=ref: fn(ref))


def _rows(base, count, stride):
    return pl.ds(base, count) if stride == 1 else pl.ds(base, count, stride=stride)


REORDER_STRIDE = 4


def _reorder_plan(dil, parts=1):
    inner = min(dil, REORDER_STRIDE)
    return inner, dil // inner, SLAB // parts // inner, SLAB // dil


def _to_residue_order(dst, src, dil, tmp, part=0, parts=1):
    inner, outer, big, seg = _reorder_plan(dil, parts)
    piece = seg // parts
    if outer == 1:
        for r in range(dil):
            dst[pl.ds(r * seg + part * piece, piece), :] = src[_rows(r, piece, dil), :].astype(dst.dtype)
        return
    for b in range(inner):
        tmp[pl.ds(b * big, big), :] = src[_rows(b, big, inner), :]
    for a in range(outer):
        for b in range(inner):
            dst[pl.ds((inner * a + b) * seg + part * piece, piece), :] = (
                tmp[_rows(b * big + a, piece, outer), :].astype(dst.dtype))


def _to_token_order(dst, src, dil, tmp):
    inner, outer, big, seg = _reorder_plan(dil)
    if outer == 1:
        for r in range(dil):
            dst[_rows(r, seg, dil), :] = src[pl.ds(r * seg, seg), :]
        return
    for a in range(outer):
        for b in range(inner):
            tmp[_rows(b * big + a, seg, outer), :] = src[pl.ds((inner * a + b) * seg, seg), :]
    for b in range(inner):
        dst[_rows(b, big, inner), :] = tmp[pl.ds(b * big, big), :]


def _in_proj(h, w, q_norm, k_norm, name):
    s_len, d = h.shape
    tm = PROJ_TILE
    assert tm == SLAB and IN_BLOCK == HEADS * HEAD_DIM
    steps = w.shape[1] // IN_BLOCK
    n_qkv = 3 * QKV // IN_BLOCK
    parts = 4
    rows = [pl.ds(p * (tm // parts), tm // parts) for p in range(parts)]

    gains = jnp.concatenate([q_norm, k_norm, jnp.ones((6, HEAD_DIM), F32)], axis=0)

    def body(h_ref, w_ref, gains_ref, qkv_ref, rest_ref, hat_ref, tok_s, tmp_s):
        j = pl.program_id(1)
        sect = j // N_GROUPS
        multiply = lambda p: _dot(h_ref[rows[p], :], w_ref[...])

        def emit(gi):
            dil = DILATIONS[gi]
            res = [multiply(p) for p in range(parts)]
            gain = gains_ref[pl.ds(sect, 1), :]
            plain = sect == 2
            for p in range(parts):
                qkv_ref[rows[p], :] = res[p]
                for hh in range(HEADS):
                    cols = slice(hh * HEAD_DIM, (hh + 1) * HEAD_DIM)
                    x = res[p][:, cols]
                    tok_s[...] = (x * jnp.where(plain, 1.0, _rms(x))) * gain
                    _to_residue_order(hat_ref.at[:, cols], tok_s, dil, tmp_s, p, parts)

        for gi in range(N_GROUPS):
            pl.when((j < n_qkv) & (j % N_GROUPS == gi))(lambda gi=gi: emit(gi))

        @pl.when(j >= n_qkv)
        def _():
            for p in range(parts):
                rest_ref[rows[p], :] = multiply(p).astype(BF16)

    qkv_blk = pl.BlockSpec((tm, IN_BLOCK), lambda i, j: (i, jnp.minimum(j, n_qkv - 1)))
    return pl.pallas_call(
        body, name=name, grid=(s_len // tm, steps),
        in_specs=[pl.BlockSpec((tm, d), lambda i, j: (i, 0)), pl.BlockSpec((d, IN_BLOCK), lambda i, j: (0, j)),
                  pl.BlockSpec((8, HEAD_DIM), lambda i, j: (0, 0))],
        out_specs=[qkv_blk, pl.BlockSpec((tm, IN_BLOCK), lambda i, j: (i, jnp.maximum(j - n_qkv, 0))), qkv_blk],
        out_shape=[jax.ShapeDtypeStruct((s_len, 3 * QKV), F32),
                   jax.ShapeDtypeStruct((s_len, w.shape[1] - 3 * QKV), BF16),
                   jax.ShapeDtypeStruct((s_len, 3 * QKV), BF16)],
        scratch_shapes=[pltpu.VMEM((tm // parts, HEAD_DIM), F32)] * 2,
        compiler_params=_params(2),
    )(h, w, gains)


def _in_proj_bwd(dq, dk, dv, drest, w, name, after=()):
    s_len = dq.shape[0]
    d = w.shape[0]
    tm = PROJ_TILE
    steps = w.shape[1] // IN_BLOCK
    n_qkv = QKV // IN_BLOCK

    def body(dq_ref, dk_ref, dv_ref, dr_ref, w_ref, o_ref, acc_ref):
        j = pl.program_id(1)

        @pl.when(j == 0)
        def _():
            acc_ref[...] = jnp.zeros_like(acc_ref)

        def add(a_ref):
            rows = [pl.ds(p * (tm // ACC_PIECES), tm // ACC_PIECES) for p in range(ACC_PIECES)]
            products = [_dot_nt(a_ref[r, :], w_ref[...]) for r in rows]
            for r, product in zip(rows, products):
                acc_ref[r, :] += product

        _pick_part(j, n_qkv, [dq_ref, dk_ref, dv_ref, dr_ref], add)

        @pl.when(j == steps - 1)
        def _():
            o_ref[...] = acc_ref[...]

    return pl.pallas_call(
        _ordered(body, 5, after), name=name, grid=(s_len // tm, steps),
        in_specs=(_in_parts(tm, n_qkv, steps - 3 * n_qkv) + [pl.BlockSpec((d, IN_BLOCK), lambda i, j: (0, j))]
                  + [ANY] * len(after)),
        out_specs=pl.BlockSpec((tm, d), lambda i, j: (i, 0)),
        out_shape=jax.ShapeDtypeStruct((s_len, d), F32),
        scratch_shapes=[pltpu.VMEM((tm, d), F32)],
        compiler_params=_params(2),
    )(dq, dk, dv, drest, w, *after)


def _wgrad(x, y, x_spec, y_spec, out_shape, out_spec, acc_shape, n_chunks, name, x_transposed=False, after=()):
    s_len = y.shape[-2]
    ts = WGRAD_TILE
    steps = s_len // ts

    def body(x_ref, y_ref, o_ref, acc_ref):
        s = pl.program_id(1)

        @pl.when(s == 0)
        def _():
            acc_ref[...] = jnp.zeros_like(acc_ref)

        if x_transposed:
            n_rows = acc_shape[0]
            rows = [pl.ds(p * (n_rows // ACC_PIECES), n_rows // ACC_PIECES) for p in range(ACC_PIECES)]
            products = [_dot(x_ref[r, :], y_ref[...]) for r in rows]
            for r, product in zip(rows, products):
                acc_ref[r, :] += product
        else:
            cols = _pieces(acc_shape[1])
            products = [_dot_tn(x_ref[...], y_ref[:, c]) for c in cols]
            for c, product in zip(cols, products):
                acc_ref[:, c] += product

        @pl.when(s == steps - 1)
        def _():
            o_ref[...] = acc_ref[...].astype(o_ref.dtype)

    return pl.pallas_call(
        _ordered(body, 2, after), name=name, grid=(n_chunks, steps),
        in_specs=[x_spec(ts), y_spec(ts)] + [ANY] * len(after), out_specs=out_spec,
        out_shape=jax.ShapeDtypeStruct(out_shape, BF16),
        scratch_shapes=[pltpu.VMEM(acc_shape, F32)],
        compiler_params=_params(2),
    )(x, y, *after)


def _pieces(width, piece=256):
    return [slice(a, min(a + piece, width)) for a in range(0, width, piece)]


def _ffn_fwd(h, w_gate, w_up, w_down, name):
    s_len, d = h.shape
    n_chunks, _, fs = w_gate.shape
    tm = FFN_TILE

    def body(h_ref, wg_ref, wu_ref, wd_ref, g_ref, u_ref, y_ref):
        j = pl.program_id(1)
        hv = h_ref[...]
        pieces = _pieces(fs)
        first = lambda cols: (_dot(hv, wg_ref[:, cols]), _dot(hv, wu_ref[:, cols]))
        total = None
        ahead = first(pieces[0])
        for k, cols in enumerate(pieces):
            g, u = ahead
            if k + 1 < len(pieces):
                ahead = first(pieces[k + 1])
            g_ref[:, cols] = g.astype(BF16)
            u_ref[:, cols] = u.astype(BF16)
            act = (g * _sigmoid(g)) * u
            part = _dot(act.astype(BF16), wd_ref[cols, :])
            total = part if total is None else total + part

        @pl.when(j == 0)
        def _():
            y_ref[...] = total

        @pl.when(j > 0)
        def _():
            y_ref[...] += total

    tile = pl.BlockSpec((tm, d), lambda i, j: (i, 0))
    hid = pl.BlockSpec((None, tm, fs), lambda i, j: (j, i, 0))
    w_in_spec = pl.BlockSpec((None, d, fs), lambda i, j: (j, 0, 0))
    hid_shape = jax.ShapeDtypeStruct((n_chunks, s_len, fs), BF16)
    return pl.pallas_call(
        body, name=name, grid=(s_len // tm, n_chunks),
        in_specs=[tile, w_in_spec, w_in_spec, pl.BlockSpec((None, fs, d), lambda i, j: (j, 0, 0))],
        out_specs=[hid, hid, tile],
        out_shape=[hid_shape, hid_shape, jax.ShapeDtypeStruct((s_len, d), F32)],
        compiler_params=_params(2),
    )(h, w_gate, w_up, w_down)


def _ffn_bwd(dy, g_pre, u_pre, w_gate, w_up, w_down, name):
    s_len, d = dy.shape
    n_chunks, _, fs = w_gate.shape
    tm = FFN_TILE

    def body(dy_ref, g_ref, u_ref, wg_ref, wu_ref, wd_ref, dh_ref, dg_ref, du_ref, a_ref):
        j = pl.program_id(1)
        dyv = dy_ref[...]
        pieces = _pieces(fs)
        first = lambda cols: _dot_nt(dyv, wd_ref[cols, :])
        total = None
        ahead = first(pieces[0])
        for k, cols in enumerate(pieces):
            da = ahead
            if k + 1 < len(pieces):
                ahead = first(pieces[k + 1])
            g = g_ref[:, cols].astype(F32)
            u = u_ref[:, cols].astype(F32)
            sg = _sigmoid(g)
            silu = g * sg
            dg = (da * u * (sg * (1.0 + g * (1.0 - sg)))).astype(BF16)
            du = (da * silu).astype(BF16)
            dg_ref[:, cols] = dg
            du_ref[:, cols] = du
            a_ref[:, cols] = (silu * u).astype(BF16)
            part = _dot_nt(dg, wg_ref[:, cols]) + _dot_nt(du, wu_ref[:, cols])
            total = part if total is None else total + part

        @pl.when(j == 0)
        def _():
            dh_ref[...] = total

        @pl.when(j > 0)
        def _():
            dh_ref[...] += total

    tile = pl.BlockSpec((tm, d), lambda i, j: (i, 0))
    hid = pl.BlockSpec((None, tm, fs), lambda i, j: (j, i, 0))
    w_in_spec = pl.BlockSpec((None, d, fs), lambda i, j: (j, 0, 0))
    hid_shape = jax.ShapeDtypeStruct((n_chunks, s_len, fs), BF16)
    return pl.pallas_call(
        body, name=name, grid=(s_len // tm, n_chunks),
        in_specs=[tile, hid, hid, w_in_spec, w_in_spec, pl.BlockSpec((None, fs, d), lambda i, j: (j, 0, 0))],
        out_specs=[tile, hid, hid, hid],
        out_shape=[jax.ShapeDtypeStruct((s_len, d), F32), hid_shape, hid_shape, hid_shape],
        compiler_params=_params(2),
    )(dy, g_pre, u_pre, w_gate, w_up, w_down)


def _ffn_wgrads(ht, dg, du, act, dy, tag, after=()):
    n_chunks, s_len, fs = dg.shape
    d = ht.shape[0]
    tok = lambda ts: pl.BlockSpec((ts, d), lambda c, s: (s, 0))
    tok_t = lambda ts: pl.BlockSpec((d, ts), lambda c, s: (0, s))
    hid = lambda ts: pl.BlockSpec((None, ts, fs), lambda c, s: (c, s, 0))
    d_up = pl.BlockSpec((None, d, fs), lambda c, s: (c, 0, 0))
    d_down = pl.BlockSpec((None, fs, d), lambda c, s: (c, 0, 0))
    dwg = _wgrad(ht, dg, tok_t, hid, (n_chunks, d, fs), d_up, (d, fs), n_chunks, tag + "_dwg", True, after)
    dwu = _wgrad(ht, du, tok_t, hid, (n_chunks, d, fs), d_up, (d, fs), n_chunks, tag + "_dwu", True, after)
    dwd = _wgrad(act, dy, hid, tok, (n_chunks, fs, d), d_down, (fs, d), n_chunks, tag + "_dwd", False, after)
    return dwg, dwu, dwd


def _band_bias():
    qi = lax.broadcasted_iota(jnp.int32, (ATTN_BLOCK, 2 * ATTN_BLOCK), 0)
    kj = lax.broadcasted_iota(jnp.int32, (ATTN_BLOCK, 2 * ATTN_BLOCK), 1)
    band = (kj >= qi) & (kj <= qi + ATTN_BLOCK)
    return jnp.where(band, 0.0, NEG), jnp.where(band & (kj >= ATTN_BLOCK), 0.0, NEG)


def _qkv_specs(slab_of, sections):
    def spec(sect, back):
        return pl.BlockSpec((SLAB, HEAD_DIM),
                            lambda h, s, g: (jnp.maximum(slab_of(s) - back, 0), (sect * N_GROUPS + g) * HEADS + h))
    return [spec(sect, back) for sect, back in sections]


HAT_BLOCKS = [(0, 0), (1, 0), (2, 0), (1, 1), (2, 1)]


def _stage_keys(k_ref, v_ref, kp_ref, vp_ref, kbuf, vbuf, dil, n):
    run = SLAB // dil
    for r in range(dil):
        own, before = pl.ds(r * run, run), pl.ds(2 * r * run, run)
        kbuf[pl.ds((2 * r + 1) * run, run), :] = k_ref[own, :]
        vbuf[pl.ds((2 * r + 1) * run, run), :] = v_ref[own, :]

        @pl.when(n > 0)
        def _():
            kbuf[before, :] = kp_ref[own, :]
            vbuf[before, :] = vp_ref[own, :]

        @pl.when(n == 0)
        def _():
            kbuf[before, :] = jnp.zeros((run, HEAD_DIM), BF16)
            vbuf[before, :] = jnp.zeros((run, HEAD_DIM), BF16)


def _for_each_tile(dil, n, first_fn, rest_fn):
    run = SLAB // dil
    bias, first_bias = _band_bias()
    tiles = []
    for jj in range(run // ATTN_BLOCK):
        start = jj * ATTN_BLOCK
        tile_bias = jnp.where(n == 0, first_bias, bias) if jj == 0 else bias
        for r in range(dil):
            tiles.append((pl.ds(r * run + start, ATTN_BLOCK),
                          pl.ds((2 * r + 1) * run - ATTN_BLOCK + start, 2 * ATTN_BLOCK), tile_bias))
    ahead = first_fn(*tiles[0])
    for t, tile in enumerate(tiles):
        begun = ahead
        if t + 1 < len(tiles):
            ahead = first_fn(*tiles[t + 1])
        rest_fn(*tile, begun)


def _attn_fwd(hat, name):
    s_len = hat.shape[0]
    e = HEAD_DIM
    n_slabs = s_len // SLAB

    def body(q_ref, k_ref, v_ref, kp_ref, vp_ref, o_ref, lse_ref, kbuf, vbuf, m_s, l_s, acc_s, m_p, l_p, acc_p, tmp_s):
        n, grp = pl.program_id(1), pl.program_id(2)

        def run(gi, dil):
            _stage_keys(k_ref, v_ref, kp_ref, vp_ref, kbuf, vbuf, dil, n)

            def scores(q_rows, kv_rows, bias):
                return _dot_nt(q_ref[q_rows, :], kbuf[kv_rows, :])

            def rest(q_rows, kv_rows, bias, qk):
                s = qk * ATTN_SCALE + bias
                m = jnp.max(s, axis=-1, keepdims=True)
                p = jnp.exp(s - m)
                m_p[q_rows, :] = jnp.broadcast_to(m, (ATTN_BLOCK, e))
                l_p[q_rows, :] = jnp.broadcast_to(jnp.sum(p, axis=-1, keepdims=True), (ATTN_BLOCK, e))
                acc_p[q_rows, :] = _dot(p.astype(BF16), vbuf[kv_rows, :])

            _for_each_tile(dil, n, scores, rest)
            _to_token_order(m_s.at[gi], m_p, dil, tmp_s)
            _to_token_order(l_s.at[gi], l_p, dil, tmp_s)
            _to_token_order(acc_s.at[gi], acc_p, dil, tmp_s)

        for gi, dil in enumerate(DILATIONS):
            pl.when(grp == gi)(lambda gi=gi, dil=dil: run(gi, dil))

        @pl.when(grp == N_GROUPS - 1)
        def _():
            m_all = jnp.maximum(jnp.maximum(m_s[0], m_s[1]), m_s[2])
            den = jnp.zeros((SLAB, e), F32)
            num = jnp.zeros((SLAB, e), F32)
            for gi in range(N_GROUPS):
                w = jnp.exp(m_s[gi] - m_all)
                den += l_s[gi] * w
                num += acc_s[gi] * w
            o_ref[...] = (num / den).astype(BF16)
            lse_ref[...] = m_all + jnp.log(den)

    out = pl.BlockSpec((SLAB, e), lambda h, n, g: (n, h))
    return pl.pallas_call(
        body, name=name, grid=(HEADS, n_slabs, N_GROUPS),
        in_specs=_qkv_specs(lambda n: n, HAT_BLOCKS),
        out_specs=[out, out],
        out_shape=[jax.ShapeDtypeStruct((s_len, HEADS * e), BF16), jax.ShapeDtypeStruct((s_len, HEADS * e), F32)],
        scratch_shapes=[pltpu.VMEM((2 * SLAB, e), BF16), pltpu.VMEM((2 * SLAB, e), BF16),
                        pltpu.VMEM((N_GROUPS, SLAB, e), F32), pltpu.VMEM((N_GROUPS, SLAB, e), F32),
                        pltpu.VMEM((N_GROUPS, SLAB, e), F32)]
        + [pltpu.VMEM((SLAB, e), F32)] * 4,
        compiler_params=_params(3),
    )(hat, hat, hat, hat, hat)


def _attn_bwd(qkv, hat, d_out, out, lse, q_norm, k_norm, name):
    s_len = qkv.shape[0]
    e = HEAD_DIM
    n_slabs = s_len // SLAB

    def body(q_ref, k_ref, v_ref, kp_ref, vp_ref, qraw_ref, kraw_ref, do_ref, o_ref, lse_ref, qn_ref, kn_ref,
             dq_ref, dk_ref, dv_ref, st_ref, kbuf, vbuf, stat_s, dqs, dkb, dvb, dk_tok, dv_tok, carry,
             do_p, stat_p, dq_p, dk_p, dv_p, tmp_s, do16_p):
        head, step, grp = pl.program_id(0), pl.program_id(1), pl.program_id(2)
        n = n_slabs - 1 - step
        dkb[...] = jnp.zeros_like(dkb)
        dvb[...] = jnp.zeros_like(dvb)
        @pl.when(grp == 0)
        def _():
            lane = lax.broadcasted_iota(jnp.int32, (SLAB, e), 1)
            stat_s[...] = jnp.where(lane < e // 2, lse_ref[...],
                                    jnp.sum(do_ref[...] * o_ref[...].astype(F32), axis=-1, keepdims=True))

        @pl.when((head == 0) & (step == 0) & (grp == 0))
        def _():
            st_ref[...] = jnp.zeros_like(st_ref)

        def run(gi, dil):
            seg = SLAB // dil
            _stage_keys(k_ref, v_ref, kp_ref, vp_ref, kbuf, vbuf, dil, n)

            @pl.when(step == 0)
            def _():
                carry[gi] = jnp.zeros((2, SLAB, e), F32)

            _to_residue_order(do_p, do_ref, dil, tmp_s)
            do16_p[...] = do_p[...].astype(BF16)
            _to_residue_order(stat_p, stat_s, dil, tmp_s)

            def scores(q_rows, kv_rows, bias):
                return _dot_nt(q_ref[q_rows, :], kbuf[kv_rows, :]), _dot_nt(do16_p[q_rows, :], vbuf[kv_rows, :])

            def rest(q_rows, kv_rows, bias, begun):
                qk, dp = begun
                q = q_ref[q_rows, :]
                k = kbuf[kv_rows, :]
                stat = stat_p[q_rows, :]
                p = jnp.exp(qk * ATTN_SCALE + bias - stat[:, 0:1])
                ds = (p * (dp - stat[:, e // 2:e // 2 + 1]) * ATTN_SCALE).astype(BF16)
                dq_p[q_rows, :] = _dot(ds, k)
                dkb[kv_rows, :] += _dot_tn(ds, q)
                dvb[kv_rows, :] += _dot_tn(p.astype(BF16), do16_p[q_rows, :])

            _for_each_tile(dil, n, scores, rest)
            for r in range(dil):
                own, before = pl.ds((2 * r + 1) * seg, seg), pl.ds(2 * r * seg, seg)
                kept = pl.ds(r * seg, seg)
                dk_p[kept, :] = dkb[own, :] + carry.at[gi, 0][kept, :]
                dv_p[kept, :] = dvb[own, :] + carry.at[gi, 1][kept, :]
                carry.at[gi, 0][kept, :] = dkb[before, :]
                carry.at[gi, 1][kept, :] = dvb[before, :]
            _to_token_order(dqs, dq_p, dil, tmp_s)
            _to_token_order(dk_tok, dk_p, dil, tmp_s)
            _to_token_order(dv_tok, dv_p, dil, tmp_s)

            def norm_bwd(raw, gain, d_hat):
                r = _rms(raw)
                y = raw * r
                dy = d_hat * gain
                return r * (dy - y * jnp.mean(dy * y, axis=-1, keepdims=True)), jnp.sum(d_hat * y, axis=0, keepdims=True)

            dq, dqn = norm_bwd(qraw_ref[...], qn_ref[...], dqs[...])
            dk, dkn = norm_bwd(kraw_ref[...], kn_ref[...], dk_tok[...])
            dq_ref[...] = dq.astype(BF16)
            dk_ref[...] = dk.astype(BF16)
            dv_ref[...] = dv_tok[...].astype(BF16)
            st_ref[0:1, :] += dqn
            st_ref[1:2, :] += dkn

        for gi, dil in enumerate(DILATIONS):
            pl.when(grp == gi)(lambda gi=gi, dil=dil: run(gi, dil))

    slab_of = lambda s: n_slabs - 1 - s
    small = pl.BlockSpec((1, e), lambda h, s, g: (0, 0))
    head_blk = pl.BlockSpec((SLAB, e), lambda h, s, g: (slab_of(s), h))
    grad_blk = pl.BlockSpec((SLAB, e), lambda h, s, g: (slab_of(s), g * HEADS + h))
    grad_shape = jax.ShapeDtypeStruct((s_len, QKV), BF16)
    return pl.pallas_call(
        body, name=name, grid=(HEADS, n_slabs, N_GROUPS),
        in_specs=(_qkv_specs(slab_of, HAT_BLOCKS) + _qkv_specs(slab_of, [(0, 0), (1, 0)])
                  + [head_blk, head_blk, head_blk, small, small]),
        out_specs=[grad_blk, grad_blk, grad_blk, pl.BlockSpec((8, e), lambda h, s, g: (0, 0))],
        out_shape=[grad_shape, grad_shape, grad_shape, jax.ShapeDtypeStruct((8, e), F32)],
        scratch_shapes=[pltpu.VMEM((2 * SLAB, e), BF16), pltpu.VMEM((2 * SLAB, e), BF16), pltpu.VMEM((SLAB, e), F32),
                        pltpu.VMEM((SLAB, e), F32), pltpu.VMEM((2 * SLAB, e), F32), pltpu.VMEM((2 * SLAB, e), F32),
                        pltpu.VMEM((SLAB, e), F32), pltpu.VMEM((SLAB, e), F32),
                        pltpu.VMEM((N_GROUPS, 2, SLAB, e), F32)]
        + [pltpu.VMEM((SLAB, e), F32)] * 6 + [pltpu.VMEM((SLAB, e), BF16)],
        compiler_params=_params(3),
    )(hat, hat, hat, hat, hat, qkv, qkv, d_out, out, lse, q_norm, k_norm)


def _shift_rows(x, by, edge, forward):
    t_len = x.shape[0]
    row = lax.broadcasted_iota(jnp.int32, x.shape, 0)
    if forward:
        out = pltpu.roll(x, by, 0)
        for i in range(by):
            out = jnp.where(row == i, edge[8 - by + i:8 - by + i + 1, :], out)
    else:
        out = pltpu.roll(x, t_len - by, 0)
        for i in range(by):
            out = jnp.where(row == t_len - by + i, edge[i:i + 1, :], out)
    return out


def _mix_fwd(x, o, rest, mod, mod_next, conv_w, w_attn, w_conv, w_out, name):
    s_len, d = x.shape
    tm = MIX_TILE
    a_w = o.shape[1]

    def body(x_ref, o_ref, u_ref, b_ref, c_ref, ga_ref, gc_ref, mod_ref, modn_ref, cw_ref, wa_ref, wc_ref, wo_ref,
             xo_ref, z_ref, ya_ref, yc_ref, conv_ref, yb_ref, m_ref, h_ref, ht_ref, carry):
        @pl.when(pl.program_id(0) == 0)
        def _():
            carry[...] = jnp.zeros_like(carry)

        xc = c_ref[...].astype(F32) * u_ref[...].astype(F32)
        edge = carry[...]
        conv = (_shift_rows(xc, 2, edge, True) * cw_ref[0:1, :] + _shift_rows(xc, 1, edge, True) * cw_ref[1:2, :]
                + xc * cw_ref[2:3, :])
        carry[...] = xc[tm - 8:tm, :]
        yb = (b_ref[...].astype(F32) * conv).astype(BF16)
        ya = _dot(o_ref[...], wa_ref[...])
        yc = _dot(yb, wc_ref[...])
        merged = (_sigmoid(ga_ref[...].astype(F32)) * ya + _sigmoid(gc_ref[...].astype(F32)) * yc).astype(BF16)
        z = _dot(merged, wo_ref[...])
        xo = x_ref[...] + mod_ref[2:3, :] * z
        xo_ref[...] = xo
        hn = ((xo * _rms(xo)) * modn_ref[3:4, :]) * (1.0 + modn_ref[1:2, :]) + modn_ref[0:1, :]
        h_ref[...] = hn.astype(BF16)
        ht_ref[...] = hn.T.astype(BF16)
        z_ref[...] = z.astype(BF16)
        ya_ref[...] = ya.astype(BF16)
        yc_ref[...] = yc.astype(BF16)
        conv_ref[...] = conv.astype(BF16)
        yb_ref[...] = yb
        m_ref[...] = merged

    tile = pl.BlockSpec((tm, d), lambda i: (i, 0))
    sect = lambda k: pl.BlockSpec((tm, d), lambda i: (i, k))
    att = pl.BlockSpec((tm, a_w), lambda i: (i, 0))
    const = lambda shape: pl.BlockSpec(shape, lambda i: (0, 0))
    f32_out = jax.ShapeDtypeStruct((s_len, d), F32)
    b16_out = jax.ShapeDtypeStruct((s_len, d), BF16)
    return pl.pallas_call(
        body, name=name, grid=(s_len // tm,),
        in_specs=[tile, att, sect(0), sect(1), sect(2), sect(3), sect(4), const((8, d)), const((8, d)), const((8, d)),
                  const((a_w, d)), const((d, d)), const((d, d))],
        out_specs=[tile] * 7 + [tile, pl.BlockSpec((d, tm), lambda i: (0, i))],
        out_shape=[f32_out] + [b16_out] * 6 + [b16_out, jax.ShapeDtypeStruct((d, s_len), BF16)],
        scratch_shapes=[pltpu.VMEM((8, d), F32)],
        compiler_params=_params(1),
    )(x, o, rest, rest, rest, rest, rest, mod, mod_next, conv_w, w_attn, w_conv, w_out)


def _mix_bwd(dxo, ya, yc, conv, rest, mod, conv_w, w_attn, w_conv, w_out, a_w, name, after=()):
    s_len, d = dxo.shape
    tm = MIX_TILE
    n_tiles = s_len // tm

    def body(dxo_ref, ya_ref, yc_ref, conv_ref, u_ref, b_ref, c_ref, ga_ref, gc_ref, mod_ref, cw_ref,
             wa_ref, wc_ref, wo_ref, do_ref, drest_ref, dz_ref, dya_ref, dyc_ref, st_ref, carry):
        @pl.when(pl.program_id(0) == 0)
        def _():
            carry[...] = jnp.zeros_like(carry)
            st_ref[...] = jnp.zeros_like(st_ref)

        dz = (mod_ref[2:3, :] * dxo_ref[...]).astype(BF16)
        dz_ref[...] = dz
        dm = _dot_nt(dz, wo_ref[...])
        sa, sc = _sigmoid(ga_ref[...].astype(F32)), _sigmoid(gc_ref[...].astype(F32))
        dya = (dm * sa).astype(BF16)
        dyc = (dm * sc).astype(BF16)
        dya_ref[...] = dya
        dyc_ref[...] = dyc
        drest_ref[:, 3 * d:4 * d] = (dm * ya_ref[...].astype(F32) * (sa * (1.0 - sa))).astype(BF16)
        drest_ref[:, 4 * d:5 * d] = (dm * yc_ref[...].astype(F32) * (sc * (1.0 - sc))).astype(BF16)
        do_ref[...] = _dot_nt(dya, wa_ref[...])
        dyb = _dot_nt(dyc, wc_ref[...])
        drest_ref[:, d:2 * d] = (dyb * conv_ref[...].astype(F32)).astype(BF16)
        dconv = dyb * b_ref[...].astype(F32)
        edge = carry[...]
        sh1 = _shift_rows(dconv, 1, edge, False)
        sh2 = _shift_rows(dconv, 2, edge, False)
        carry[...] = dconv[0:8, :]
        dxc = dconv * cw_ref[2:3, :] + sh1 * cw_ref[1:2, :] + sh2 * cw_ref[0:1, :]
        u, c = u_ref[...].astype(F32), c_ref[...].astype(F32)
        xc = c * u
        drest_ref[:, 0:d] = (dxc * c).astype(BF16)
        drest_ref[:, 2 * d:3 * d] = (dxc * u).astype(BF16)
        st_ref[0:1, :] += jnp.sum(xc * sh2, axis=0, keepdims=True)
        st_ref[1:2, :] += jnp.sum(xc * sh1, axis=0, keepdims=True)
        st_ref[2:3, :] += jnp.sum(xc * dconv, axis=0, keepdims=True)

    rev = lambda i: n_tiles - 1 - i
    tile = pl.BlockSpec((tm, d), lambda i: (rev(i), 0))
    sect = lambda k: pl.BlockSpec((tm, d), lambda i: (rev(i), k))
    const = lambda shape: pl.BlockSpec(shape, lambda i: (0, 0))
    b16_out = jax.ShapeDtypeStruct((s_len, d), BF16)
    return pl.pallas_call(
        _ordered(body, 14, after), name=name, grid=(n_tiles,),
        in_specs=[tile, tile, tile, tile, sect(0), sect(1), sect(2), sect(3), sect(4), const((8, d)), const((8, d)),
                  const((a_w, d)), const((d, d)), const((d, d))] + [ANY] * len(after),
        out_specs=[pl.BlockSpec((tm, a_w), lambda i: (rev(i), 0)), pl.BlockSpec((tm, 5 * d), lambda i: (rev(i), 0)),
                   tile, tile, tile, const((8, d))],
        out_shape=[jax.ShapeDtypeStruct((s_len, a_w), F32), jax.ShapeDtypeStruct((s_len, 5 * d), BF16),
                   b16_out, b16_out, b16_out, jax.ShapeDtypeStruct((8, d), F32)],
        scratch_shapes=[pltpu.VMEM((8, d), F32)],
        compiler_params=_params(1),
    )(dxo, ya, yc, conv, rest, rest, rest, rest, rest, mod, conv_w, w_attn, w_conv, w_out, *after)


ADA_COLS = 128


def _ada_fwd(c_all, w_shard, b_shard, name):
    d, cols = w_shard.shape

    def body(c_ref, w_ref, b_ref, o_ref):
        cv = c_ref[...]
        o_ref[...] = jnp.dot(cv * _sigmoid(cv), w_ref[...], preferred_element_type=F32,
                             precision=lax.Precision.HIGHEST) + b_ref[...]

    return pl.pallas_call(
        body, name=name, grid=(cols // ADA_COLS,),
        in_specs=[pl.BlockSpec((8, d), lambda j: (0, 0)), pl.BlockSpec((d, ADA_COLS), lambda j: (0, j)),
                  pl.BlockSpec((1, ADA_COLS), lambda j: (0, j))],
        out_specs=pl.BlockSpec((8, ADA_COLS), lambda j: (0, j)),
        out_shape=jax.ShapeDtypeStruct((8, cols), F32),
        compiler_params=_params(1),
    )(c_all, w_shard, b_shard)


def _ada_bwd(c_all, dmod_shard, w, m, v, name):
    d, cols = w.shape

    def body(c_ref, dm_ref, w_ref, m_ref, v_ref, g_ref, d_ref, nm_ref, nv_ref):
        cv = c_ref[...]
        g = lax.dot_general(cv * _sigmoid(cv), dm_ref[...], (((0,), (0,)), ((), ())),
                            preferred_element_type=F32, precision=lax.Precision.HIGHEST)
        g_ref[...] = g
        d_ref[...], nm_ref[...], nv_ref[...] = _adamw_math(w_ref[...], g, m_ref[...], v_ref[...])

    blk = pl.BlockSpec((d, ADA_COLS), lambda j: (0, j))
    shape = jax.ShapeDtypeStruct((d, cols), F32)
    return pl.pallas_call(
        body, name=name, grid=(cols // ADA_COLS,),
        in_specs=[pl.BlockSpec((8, d), lambda j: (0, 0)), pl.BlockSpec((8, ADA_COLS), lambda j: (0, j)), blk, blk, blk],
        out_specs=[blk] * 4, out_shape=[shape] * 4,
        compiler_params=_params(1),
    )(c_all, dmod_shard, w, m, v)


def _small_update(parts, w, m, v, name):
    n = w.shape[1]

    def body(p_ref, w_ref, m_ref, v_ref, g_ref, d_ref, nm_ref, nv_ref):
        g = p_ref[0:1, :]
        for i in range(1, 8):
            g = g + p_ref[i:i + 1, :]
        g_ref[...] = g
        d_ref[...], nm_ref[...], nv_ref[...] = _adamw_math(w_ref[...], g, m_ref[...], v_ref[...])

    shape = jax.ShapeDtypeStruct((1, n), F32)
    return pl.pallas_call(body, name=name, out_shape=[shape] * 4, compiler_params=_params())(parts, w, m, v)


def _cols_to_shards(w, n):
    r, nc = w.shape
    return w.reshape(r, n, nc // n).transpose(1, 0, 2)


def kernel(x, c, w_ada, b_ada, norm_ffn1, ffn1_w_gate, ffn1_w_up, ffn1_w_down, norm_mix, w_in, q_norm, k_norm, conv_w, w_attn_branch, w_conv_branch, w_out, norm_ffn2, ffn2_w_gate, ffn2_w_up, ffn2_w_down, loss_target, m_w_ada, m_b_ada, m_norm_ffn1, m_ffn1_w_gate, m_ffn1_w_up, m_ffn1_w_down, m_norm_mix, m_w_in, m_q_norm, m_k_norm, m_conv_w, m_w_attn_branch, m_w_conv_branch, m_w_out, m_norm_ffn2, m_ffn2_w_gate, m_ffn2_w_up, m_ffn2_w_down, v_w_ada, v_b_ada, v_norm_ffn1, v_ffn1_w_gate, v_ffn1_w_up, v_ffn1_w_down, v_norm_mix, v_w_in, v_q_norm, v_k_norm, v_conv_w, v_w_attn_branch, v_w_conv_branch, v_w_out, v_norm_ffn2, v_ffn2_w_gate, v_ffn2_w_up, v_ffn2_w_down):
    ix, iy, ic = _place()
    chip = 2 * ix + iy
    me = 4 * ix + 2 * iy + ic
    xs = x[0]
    target = loss_target[0]
    s_len, d = xs.shape
    ada_cols = w_ada.shape[2]
    conv_cols = conv_w.shape[2]

    conv_rows = jnp.zeros((8, conv_cols), F32).at[0:3].set(conv_w[0])
    small_in = jnp.concatenate([jnp.broadcast_to(c, (8, d)), conv_rows], axis=1)
    small_all = _allgather8(small_in, "gather_c").reshape(8, 8, d + conv_cols)
    c_all = small_all[:, 0, :d]
    conv_full = small_all[0::2, 0:3, d:].transpose(1, 0, 2).reshape(3, N_CHIPS * conv_cols)
    conv_pad = jnp.zeros((8, N_CHIPS * conv_cols), F32).at[0:3].set(conv_full)
    b_shard = lax.dynamic_slice(b_ada, (0, chip * ada_cols), (1, ada_cols))
    mod_part = _ada_fwd(c_all, w_ada[0], b_shard, "ada_fwd")
    mod_all = _allgather8(mod_part, "gather_mod").reshape(N_CHIPS, 2, 8, ada_cols)[:, 0]
    mod_mine = lax.dynamic_slice(mod_all, (0, me, 0), (N_CHIPS, 1, ada_cols)).reshape(9, d)

    def mod_rows(i, gain):
        return jnp.zeros((8, d), F32).at[0:3].set(mod_mine[3 * i:3 * i + 3]).at[3:4].set(gain)

    mod1, mod2, mod3 = mod_rows(0, norm_ffn1), mod_rows(1, norm_mix), mod_rows(2, norm_ffn2)

    to16 = lambda w: w[0].astype(BF16)
    wg1, wu1, wd1 = _gather_weights([to16(ffn1_w_gate), to16(ffn1_w_up), to16(ffn1_w_down)], [False] * 3,
                                    "gather_ffn1", 1)
    h1, h1t = _norm_mod(xs, mod1, "norm1")
    (w_in_full,) = _gather_weights([to16(w_in)], [True], "gather_w_in", 2, after=(wd1, h1))

    g1, u1, y1 = _ffn_fwd(h1, wg1, wu1, wd1, "ffn1_fwd")
    x1, h2, h2t = _norm_mod(xs, mod2, "norm2", prev=(y1, mod1, 0.5))
    qkv, rest, qkv_hat = _in_proj(h2, w_in_full, q_norm, k_norm, "in_proj")
    w_ab, w_cb_g, w_o_g, wg2, wu2, wd2 = _gather_weights(
        [to16(w_attn_branch), to16(w_conv_branch), to16(w_out),
         to16(ffn2_w_gate), to16(ffn2_w_up), to16(ffn2_w_down)], [True] + [False] * 5,
        "gather_rest", 3, after=(h2,))
    a_w = w_ab.shape[0]
    w_cb = w_cb_g.reshape(d, d)
    w_o = w_o_g.reshape(d, d)
    o, lse = _attn_fwd(qkv_hat, "attn_fwd")
    x2, z, ya, yc, conv, yb, merged, h3, h3t = _mix_fwd(x1, o, rest, mod2, mod3, conv_pad, w_ab, w_cb, w_o, "mix_fwd")
    g3, u3, y3 = _ffn_fwd(h3, wg2, wu2, wd2, "ffn2_fwd")
    dx3, dy3, loss_part = _loss_grad(x2, y3, mod3, target, "loss")

    c_idx = jnp.reshape(ic, (1,)).astype(jnp.int32)
    chip_idx = jnp.stack([chip, ic]).astype(jnp.int32)

    def pair_send(grads, tag, collective_id):
        return _rs_pair_exchange(grads, "rs_pair_" + tag, collective_id)

    def chip_send(grads, from_sibling, names, tag, collective_id, after):
        pair_sums = [_pair_add(g, r, c_idx, "pair_add_" + nm, after) for g, r, nm in zip(grads, from_sibling, names)]
        return pair_sums, _rs_chip_exchange(pair_sums, "rs_chips_" + tag, collective_id)

    def reduce_finish(pair_sums, from_chips, names, tag, after):
        totals = [_chip_add(p, r, chip_idx, "chip_add_" + nm, after)
                  for p, r, nm in zip(pair_sums, from_chips, names)]
        return dict(zip(names, _rs_share(totals, "rs_share_" + tag)))

    names_a = ["ffn2_w_gate", "ffn2_w_up", "ffn2_w_down"]
    names_b = ["w_in", "w_attn_branch", "w_conv_branch", "w_out"]
    names_c = ["ffn1_w_gate", "ffn1_w_up", "ffn1_w_down"]

    dh3, dg3, du3, a3 = _ffn_bwd(dy3, g3, u3, wg2, wu2, wd2, "ffn2_bwd")
    grads_a = list(_ffn_wgrads(h3t, dg3, du3, a3, dy3, "ffn2"))
    sibling_a = pair_send(grads_a, "a", 7)
    dx2, st3 = _norm_bwd(dh3, x2, mod3, dx3, y3, 0.5, "norm3_bwd")
    sums_a, chips_a = chip_send(grads_a, sibling_a, names_a, "a", 4, after=(dx2,))

    do, drest, dz, dya, dyc, st_conv = _mix_bwd(dx2, ya, yc, conv, rest, mod2, conv_pad, w_ab, w_cb, w_o, a_w,
                                                "mix_bwd", after=tuple(sums_a))
    dq, dk, dv, st_qk = _attn_bwd(qkv, qkv_hat, do, o, lse, q_norm, k_norm, "attn_bwd")
    tok = lambda width: (lambda ts: pl.BlockSpec((ts, width), lambda cc, s: (s, 0)))
    colblk = lambda width: (lambda ts: pl.BlockSpec((ts, width), lambda cc, s: (s, cc)))
    tok_t = lambda ts: pl.BlockSpec((d, ts), lambda cc, s: (0, s))
    whole = pl.BlockSpec((d, QKV), lambda cc, s: (0, 0))
    dw_in = [_wgrad(h2t, part, tok_t, tok(QKV), (d, QKV), whole, (d, QKV), 1, "dw_in_" + nm, True)
             for part, nm in ((dq, "q"), (dk, "k"), (dv, "v"))]
    dw_in.append(_wgrad(h2t, drest, tok_t, colblk(d), (d, 5 * d), pl.BlockSpec((d, d), lambda cc, s: (0, cc)),
                        (d, d), 5, "dw_in_rest", True))
    dw_in = _cols_to_shards(jnp.concatenate(dw_in, axis=1), N_CHIPS)
    shard_w = d // N_CHIPS
    dw_ab = _wgrad(o, dya, tok(a_w), colblk(shard_w), (a_w, d), pl.BlockSpec((a_w, shard_w), lambda cc, s: (0, cc)),
                   (a_w, shard_w), N_CHIPS, "dw_attn_branch")
    dw_ab = _cols_to_shards(dw_ab, N_CHIPS)
    row_out = pl.BlockSpec((None, shard_w, d), lambda cc, s: (cc, 0, 0))
    dw_cb = _wgrad(yb, dyc, colblk(shard_w), tok(d), (N_CHIPS, shard_w, d), row_out, (shard_w, d), N_CHIPS, "dw_conv_branch")
    dw_o = _wgrad(merged, dz, colblk(shard_w), tok(d), (N_CHIPS, shard_w, d), row_out, (shard_w, d), N_CHIPS, "dw_out")
    shard_grads = reduce_finish(sums_a, chips_a, names_a, "a", after=(dw_in, dw_o))
    grads_b = [dw_in, dw_ab, dw_cb, dw_o]
    sibling_b = pair_send(grads_b, "b", 8)

    dh2 = _in_proj_bwd(dq, dk, dv, drest, w_in_full, "in_proj_bwd")
    sums_b, chips_b = chip_send(grads_b, sibling_b, names_b, "b", 5, after=(dh2,))
    dx1, st2, dy1 = _norm_bwd(dh2, x1, mod2, dx2, z, 1.0, "norm2_bwd", after=tuple(sums_b), prev=(mod1, 0.5))
    dh1, dg1, du1, a1 = _ffn_bwd(dy1, g1, u1, wg1, wu1, wd1, "ffn1_bwd")
    dx0, st1 = _norm_bwd(dh1, xs, mod1, dx1, y1, 0.5, "norm1_bwd")
    grads_c = list(_ffn_wgrads(h1t, dg1, du1, a1, dy1, "ffn1"))
    sibling_c = pair_send(grads_c, "c", 9)
    shard_grads.update(reduce_finish(sums_b, chips_b, names_b, "b", after=tuple(grads_c)))

    dmod = jnp.concatenate([st1[0:3], st2[0:3], st3[0:3]], axis=0).reshape(1, 9 * d)
    loss_cols = jnp.zeros((1, HEAD_DIM), F32).at[0, 0].set(jnp.sum(loss_part))
    small = jnp.concatenate([dmod, st1[3:4], st2[3:4], st3[3:4], st_qk[0:1], st_qk[1:2],
                             st_conv[0:3].reshape(1, 3 * d), loss_cols], axis=1)
    n_small = small.shape[1]
    fold = -(-n_small // (8 * LANE_TILE)) * LANE_TILE
    folded = jnp.pad(small, ((0, 0), (0, 8 * fold - n_small))).reshape(8, fold)
    small_all = _allgather8(folded, "gather_small").reshape(8, 8 * fold)[:, :n_small]
    loss = (0.5 / d) * jnp.sum(small_all[:, -HEAD_DIM])
    small_all = small_all[:, :-HEAD_DIM]
    dmod_all = small_all[:, :9 * d]
    dmod_shard = lax.dynamic_slice(dmod_all, (0, chip * ada_cols), (8, ada_cols))
    g_w_ada, d_w_ada, nm_w_ada, nv_w_ada = _ada_bwd(c_all, dmod_shard, w_ada[0], m_w_ada[0], v_w_ada[0], "ada_bwd")

    vec_names = ["b_ada", "norm_ffn1", "norm_mix", "norm_ffn2", "q_norm", "k_norm"]
    vec_w = [b_ada, norm_ffn1, norm_mix, norm_ffn2, q_norm, k_norm]
    vec_m = [m_b_ada, m_norm_ffn1, m_norm_mix, m_norm_ffn2, m_q_norm, m_k_norm]
    vec_v = [v_b_ada, v_norm_ffn1, v_norm_mix, v_norm_ffn2, v_q_norm, v_k_norm]
    n_vec = sum(w.shape[1] for w in vec_w)
    cat = lambda arrs: jnp.concatenate(arrs, axis=1)
    vec_out = _small_update(small_all[:, :n_vec], cat(vec_w), cat(vec_m), cat(vec_v), "small_update")
    conv_parts = small_all[:, n_vec:].reshape(8, 3, N_CHIPS * conv_cols)
    conv_parts = lax.dynamic_slice(conv_parts, (0, 0, chip * conv_cols), (8, 3, conv_cols)).reshape(8, 3 * conv_cols)
    flat3 = lambda w: w[0].reshape(1, 3 * conv_cols)
    conv_out = _small_update(conv_parts, flat3(conv_w), flat3(m_conv_w), flat3(v_conv_w), "conv_update")

    res = {"w_ada": [t[None] for t in (g_w_ada, d_w_ada, nm_w_ada, nv_w_ada)],
           "conv_w": [t.reshape(1, 3, conv_cols) for t in conv_out]}
    off = 0
    for nm, w in zip(vec_names, vec_w):
        width = w.shape[1]
        res[nm] = [t[:, off:off + width] for t in vec_out]
        off += width
    big = {"ffn1_w_gate": (ffn1_w_gate, m_ffn1_w_gate, v_ffn1_w_gate), "ffn1_w_up": (ffn1_w_up, m_ffn1_w_up, v_ffn1_w_up),
           "ffn1_w_down": (ffn1_w_down, m_ffn1_w_down, v_ffn1_w_down), "w_in": (w_in, m_w_in, v_w_in),
           "w_attn_branch": (w_attn_branch, m_w_attn_branch, v_w_attn_branch),
           "w_conv_branch": (w_conv_branch, m_w_conv_branch, v_w_conv_branch), "w_out": (w_out, m_w_out, v_w_out),
           "ffn2_w_gate": (ffn2_w_gate, m_ffn2_w_gate, v_ffn2_w_gate), "ffn2_w_up": (ffn2_w_up, m_ffn2_w_up, v_ffn2_w_up),
           "ffn2_w_down": (ffn2_w_down, m_ffn2_w_down, v_ffn2_w_down)}
    def update(nm, after=()):
        w, m, v = big[nm]
        flip = w.shape[2] % LANE_TILE != 0 and w.shape[1] % LANE_TILE == 0
        turn = (lambda a: a.T) if flip else (lambda a: a)
        outs = _adamw(turn(w[0]), turn(shard_grads[nm]), turn(m[0]), turn(v[0]), "adamw_" + nm, after)
        res[nm] = [turn(t)[None] for t in outs]
        return outs[3]

    last = tuple(shard_grads[nm] for nm in names_b)
    for nm in names_a:
        last = (update(nm, last),)
    sums_c, chips_c = chip_send(grads_c, sibling_c, names_c, "c", 6, after=last)
    last = tuple(sums_c)
    for nm in names_b:
        last = (update(nm, last),)
    shard_grads.update(reduce_finish(sums_c, chips_c, names_c, "c", after=last))
    for nm in names_c:
        update(nm)

    order = ["w_ada", "b_ada", "norm_ffn1", "ffn1_w_gate", "ffn1_w_up", "ffn1_w_down", "norm_mix", "w_in", "q_norm",
             "k_norm", "conv_w", "w_attn_branch", "w_conv_branch", "w_out", "norm_ffn2", "ffn2_w_gate", "ffn2_w_up",
             "ffn2_w_down"]
    return (loss, dx0[None], *[res[nm][0] for nm in order], *[res[nm][1] for nm in order],
            *[res[nm][2] for nm in order], *[res[nm][3] for nm in order])
```

```python
import jax
import jax.numpy as jnp
from jax import lax
from jax.experimental import pallas as pl
from jax.experimental.pallas import tpu as pltpu
from jax.experimental.pallas import tpu_sc as plsc

F32 = jnp.float32
BF16 = jnp.bfloat16
MESH = pl.DeviceIdType.MESH
ANY = pl.BlockSpec(memory_space=pl.ANY)

NORM_EPS = 1e-6
LANE_TILE = 128
HEAD_DIM = 128
N_GROUPS = 3
HEADS = 4
DILATIONS = (1, 4, 16)
ATTN_BLOCK = 128
SLAB = ATTN_BLOCK * max(DILATIONS)
QKV = N_GROUPS * HEADS * HEAD_DIM
ATTN_SCALE = HEAD_DIM ** -0.5
NEG = -1e30
N_CHIPS = 4

ADAM_LR = 0.001
ADAM_B1 = 0.9
ADAM_B2 = 0.999
ADAM_EPS = 1e-08
ADAM_WD = 0.01
ADAM_STEP = 10

VMEM_LIMIT_BYTES = 56 * 1024 * 1024
TOKEN_TILE = 512
FFN_TILE = 1024
PROJ_TILE = 2048
WGRAD_TILE = 2048
IN_BLOCK = 512
MIX_TILE = 512
ACC_PIECES = 4
ADAMW_TILE_BYTES = 3 * 512 * 1024


def _params(n_axes=0):
    return pltpu.CompilerParams(
        dimension_semantics=("arbitrary",) * n_axes if n_axes else None,
        vmem_limit_bytes=VMEM_LIMIT_BYTES)


def _dot(a, b):
    return jnp.dot(a, b, preferred_element_type=F32)


def _dot_nt(a, b):
    return lax.dot_general(a, b, (((1,), (1,)), ((), ())), preferred_element_type=F32)


def _dot_tn(a, b):
    return lax.dot_general(a, b, (((0,), (0,)), ((), ())), preferred_element_type=F32)


def _sigmoid(x):
    return 0.5 * jnp.tanh(0.5 * x) + 0.5


def _place():
    return lax.axis_index("x"), lax.axis_index("y"), lax.axis_index("c")


def _ordered(body, n_in, after):
    if not after:
        return body
    return lambda *refs: body(*refs[:n_in], *refs[n_in + len(after):])


def _allgather8(block, name):
    m_per, n = block.shape

    def body(x_ref, out_ref, send_sems, recv_sems, local_sem):
        x, y, c = _place()
        me, sibling = (x, y, c), (x, y, 1 - c)
        chips = [(1 - x, y), (x, 1 - y), (1 - x, 1 - y)]

        def rows(px, py, pc):
            return out_ref.at[pl.ds((4 * px + 2 * py + pc) * m_per, m_per), :]

        def copy(k, blk, to, src=None):
            return pltpu.make_async_remote_copy(
                src_ref=rows(*blk) if src is None else src, dst_ref=rows(*blk),
                send_sem=send_sems.at[k], recv_sem=recv_sems.at[k],
                device_id=to, device_id_type=MESH)

        mine = pltpu.make_async_copy(x_ref, rows(*me), local_sem)
        mine.start()
        first = [copy(0, me, sibling, src=x_ref)]
        first += [copy(1 + j, me, (*chip, c), src=x_ref) for j, chip in enumerate(chips)]
        for cp in first:
            cp.start()
        passed = [copy(4 + j, (*chip, c), sibling) for j, chip in enumerate(chips)]
        for j, chip in enumerate(chips):
            copy(1 + j, (*chip, c), me).wait_recv()
            passed[j].start()
        copy(0, sibling, me).wait_recv()
        for j, chip in enumerate(chips):
            copy(4 + j, (*chip, 1 - c), me).wait_recv()
        for cp in first + passed:
            cp.wait_send()
        mine.wait()

    return pl.pallas_call(
        body, name=name,
        out_shape=jax.ShapeDtypeStruct((8 * m_per, n), block.dtype),
        in_specs=[pl.BlockSpec(memory_space=pltpu.VMEM)],
        out_specs=pl.BlockSpec(memory_space=pltpu.VMEM),
        scratch_shapes=[pltpu.SemaphoreType.DMA((7,)), pltpu.SemaphoreType.DMA((7,)),
                        pltpu.SemaphoreType.DMA],
        compiler_params=_params(),
    )(block)


def _handshake(peers):
    barrier = pltpu.get_barrier_semaphore()
    for peer in peers:
        pl.semaphore_signal(barrier, inc=1, device_id=peer, device_id_type=MESH)
    pl.semaphore_wait(barrier, len(peers))


def _gather_weights(shards, by_cols, name, collective_id, after=()):
    n_arr = len(shards)

    def body(*refs):
        srcs, outs = refs[:n_arr], refs[n_arr + len(after):2 * n_arr + len(after)]
        send_sems, recv_sems, local_sems = refs[2 * n_arr + len(after):]
        x, y, c = _place()
        me_dev, sibling = (x, y, c), (x, y, 1 - c)
        chips = [(1 - x, y), (x, 1 - y), (1 - x, 1 - y)]
        me = 2 * x + y
        _handshake([sibling] + [(*chip, c) for chip in chips])

        def place(k, chip_idx, rows):
            if by_cols[k]:
                width = srcs[k].shape[1]
                return outs[k].at[rows, pl.ds(pl.multiple_of(chip_idx * width, 128), width)]
            return outs[k].at[chip_idx, rows]

        def copy(k, slot, chip_idx, half_sel, to, from_shard=False):
            half = srcs[k].shape[0] // 2
            rows = pl.ds(half_sel * half, half)
            dst = place(k, chip_idx, rows)
            return pltpu.make_async_remote_copy(
                src_ref=srcs[k].at[rows] if from_shard else dst, dst_ref=dst,
                send_sem=send_sems.at[6 * k + slot], recv_sem=recv_sems.at[6 * k + slot],
                device_id=to, device_id_type=MESH)

        own = [pltpu.make_async_copy(srcs[k], place(k, me, pl.ds(0, srcs[k].shape[0])), local_sems.at[k])
               for k in range(n_arr)]
        for cp in own:
            cp.start()
        sent = []
        for k in range(n_arr):
            for j, chip in enumerate(chips):
                sent.append(copy(k, j, me, c, (*chip, c), from_shard=True))
                sent[-1].start()
        for k in range(n_arr):
            for j, chip in enumerate(chips):
                chip_idx = 2 * chip[0] + chip[1]
                copy(k, j, chip_idx, c, me_dev).wait_recv()
                sent.append(copy(k, 3 + j, chip_idx, c, sibling))
                sent[-1].start()
        for k in range(n_arr):
            for j, chip in enumerate(chips):
                copy(k, 3 + j, 2 * chip[0] + chip[1], 1 - c, me_dev).wait_recv()
        for cp in sent:
            cp.wait_send()
        for cp in own:
            cp.wait()

    def gathered(k):
        r, cols = shards[k].shape
        return (r, N_CHIPS * cols) if by_cols[k] else (N_CHIPS, r, cols)

    return pl.kernel(
        body, name=name,
        out_type=[jax.ShapeDtypeStruct(gathered(k), shards[k].dtype) for k in range(n_arr)],
        mesh=plsc.ScalarSubcoreMesh(axis_name="sequencer", num_cores=1),
        scratch_types=[pltpu.SemaphoreType.DMA((6 * n_arr,)), pltpu.SemaphoreType.DMA((6 * n_arr,)),
                       pltpu.SemaphoreType.DMA((n_arr,))],
        compiler_params=pltpu.CompilerParams(collective_id=collective_id),
    )(*shards, *after)


def _rs_pair_exchange(grads, name, collective_id):
    n_arr = len(grads)

    def body(*refs):
        srcs, outs = refs[:n_arr], refs[n_arr:2 * n_arr]
        send_sems, recv_sems = refs[2 * n_arr:]
        x, y, c = _place()
        _handshake([(x, y, 1 - c)])
        cps = []
        for k in range(n_arr):
            half = srcs[k].shape[1] // 2
            cps.append(pltpu.make_async_remote_copy(
                src_ref=srcs[k].at[:, pl.ds((1 - c) * half, half)], dst_ref=outs[k],
                send_sem=send_sems.at[k], recv_sem=recv_sems.at[k],
                device_id=(x, y, 1 - c), device_id_type=MESH))
            cps[-1].start()
        for cp in cps:
            cp.wait_recv()
        for cp in cps:
            cp.wait_send()

    return pl.kernel(
        body, name=name,
        out_type=[jax.ShapeDtypeStruct((g.shape[0], g.shape[1] // 2, g.shape[2]), g.dtype) for g in grads],
        mesh=plsc.ScalarSubcoreMesh(axis_name="sequencer", num_cores=1),
        scratch_types=[pltpu.SemaphoreType.DMA((n_arr,)), pltpu.SemaphoreType.DMA((n_arr,))],
        compiler_params=pltpu.CompilerParams(collective_id=collective_id),
    )(*grads)


def _rs_chip_exchange(sums, name, collective_id):
    n_arr = len(sums)

    def body(*refs):
        srcs, outs = refs[:n_arr], refs[n_arr:2 * n_arr]
        send_sems, recv_sems = refs[2 * n_arr:]
        x, y, c = _place()
        chips = [(1 - x, y), (x, 1 - y), (1 - x, 1 - y)]
        _handshake([(*chip, c) for chip in chips])
        cps = []
        for k in range(n_arr):
            for j, chip in enumerate(chips):
                cps.append(pltpu.make_async_remote_copy(
                    src_ref=srcs[k].at[2 * chip[0] + chip[1]], dst_ref=outs[k].at[j],
                    send_sem=send_sems.at[3 * k + j], recv_sem=recv_sems.at[3 * k + j],
                    device_id=(*chip, c), device_id_type=MESH))
                cps[-1].start()
        for cp in cps:
            cp.wait_recv()
        for cp in cps:
            cp.wait_send()

    return pl.kernel(
        body, name=name,
        out_type=[jax.ShapeDtypeStruct((3,) + s.shape[1:], s.dtype) for s in sums],
        mesh=plsc.ScalarSubcoreMesh(axis_name="sequencer", num_cores=1),
        scratch_types=[pltpu.SemaphoreType.DMA((3 * n_arr,)), pltpu.SemaphoreType.DMA((3 * n_arr,))],
        compiler_params=pltpu.CompilerParams(collective_id=collective_id),
    )(*sums)


def _rs_share(totals, name):
    n_arr = len(totals)

    def body(*refs):
        outs = refs[n_arr:2 * n_arr]
        send_sems, recv_sems = refs[2 * n_arr:]
        x, y, c = _place()

        def half_rows(k, sel):
            return outs[k].at[sel]

        cps = []
        for k in range(n_arr):
            cps.append(pltpu.make_async_remote_copy(
                src_ref=half_rows(k, c), dst_ref=half_rows(k, c), send_sem=send_sems.at[k], recv_sem=recv_sems.at[k],
                device_id=(x, y, 1 - c), device_id_type=MESH))
            cps[-1].start()
        for k in range(n_arr):
            pltpu.make_async_remote_copy(
                src_ref=half_rows(k, c), dst_ref=half_rows(k, 1 - c), send_sem=send_sems.at[k],
                recv_sem=recv_sems.at[k], device_id=(x, y, 1 - c), device_id_type=MESH).wait_recv()
        for cp in cps:
            cp.wait_send()

    shared = pl.pallas_call(
        body, name=name,
        out_shape=[jax.ShapeDtypeStruct(t.shape, t.dtype) for t in totals],
        in_specs=[ANY] * n_arr, out_specs=[ANY] * n_arr,
        input_output_aliases={k: k for k in range(n_arr)},
        scratch_shapes=[pltpu.SemaphoreType.DMA((n_arr,)), pltpu.SemaphoreType.DMA((n_arr,))],
        compiler_params=_params(),
    )(*totals)
    return [t.reshape(2 * t.shape[1], t.shape[2]) for t in shared]


def _pair_add(grad, recv, c_idx, name, after=()):
    n, r, cols = grad.shape
    half = r // 2
    rows = half // 2

    def body(_, g_ref, r_ref, o_ref):
        o_ref[...] = (g_ref[...].astype(F32) + r_ref[...].astype(F32)).astype(o_ref.dtype)

    return pl.pallas_call(
        _ordered(body, 3, after), name=name,
        grid_spec=pltpu.PrefetchScalarGridSpec(
            num_scalar_prefetch=1, grid=(n, 2),
            in_specs=[pl.BlockSpec((None, None, rows, cols), lambda s, i, ci: (s, ci[0], i, 0)),
                      pl.BlockSpec((None, rows, cols), lambda s, i, ci: (s, i, 0))] + [ANY] * len(after),
            out_specs=pl.BlockSpec((None, rows, cols), lambda s, i, ci: (s, i, 0))),
        out_shape=jax.ShapeDtypeStruct((n, half, cols), BF16),
        compiler_params=_params(2),
    )(c_idx, grad.reshape(n, 2, half, cols), recv, *after)


def _chip_add(sums, recv, chip_and_core, name, after=()):
    _, half, cols = sums.shape
    rows = half // 2

    def body(_, s_ref, r0_ref, r1_ref, r2_ref, o_ref):
        o_ref[...] = ((s_ref[...].astype(F32) + r0_ref[...].astype(F32))
                      + r1_ref[...].astype(F32)) + r2_ref[...].astype(F32)

    def recv_spec(j):
        return pl.BlockSpec((None, rows, cols), lambda i, ci: (j, i, 0))

    return pl.pallas_call(
        _ordered(body, 5, after), name=name,
        grid_spec=pltpu.PrefetchScalarGridSpec(
            num_scalar_prefetch=1, grid=(2,),
            in_specs=[pl.BlockSpec((None, rows, cols), lambda i, ci: (ci[0], i, 0)),
                      recv_spec(0), recv_spec(1), recv_spec(2)] + [ANY] * len(after),
            out_specs=pl.BlockSpec((None, rows, cols), lambda i, ci: (ci[1], i, 0))),
        out_shape=jax.ShapeDtypeStruct((2, half, cols), F32),
        compiler_params=_params(1),
    )(chip_and_core, sums, recv, recv, recv, *after)


def _rms(x):
    return lax.rsqrt(jnp.mean(x * x, axis=-1, keepdims=True) + NORM_EPS)


def _norm_mod(x, mod, name, prev=None):
    s_len, d = x.shape
    tm = TOKEN_TILE

    def body(*refs):
        if prev is None:
            x_ref, mod_ref, h_ref, ht_ref = refs
            xv = x_ref[...]
        else:
            x_ref, y_ref, modp_ref, mod_ref, xo_ref, h_ref, ht_ref = refs
            xv = x_ref[...] + prev[2] * modp_ref[2:3, :] * y_ref[...]
            xo_ref[...] = xv
        n = (xv * _rms(xv)) * mod_ref[3:4, :]
        h = n * (1.0 + mod_ref[1:2, :]) + mod_ref[0:1, :]
        h_ref[...] = h.astype(BF16)
        ht_ref[...] = h.T.astype(BF16)

    tile = pl.BlockSpec((tm, d), lambda i: (i, 0))
    small = pl.BlockSpec((8, d), lambda i: (0, 0))
    h_specs = [tile, pl.BlockSpec((d, tm), lambda i: (0, i))]
    h_shapes = [jax.ShapeDtypeStruct((s_len, d), BF16), jax.ShapeDtypeStruct((d, s_len), BF16)]
    if prev is None:
        return pl.pallas_call(
            body, name=name, grid=(s_len // tm,), in_specs=[tile, small], out_specs=h_specs, out_shape=h_shapes,
            compiler_params=_params(1))(x, mod)
    return pl.pallas_call(
        body, name=name, grid=(s_len // tm,), in_specs=[tile, tile, small, small],
        out_specs=[tile] + h_specs, out_shape=[jax.ShapeDtypeStruct((s_len, d), F32)] + h_shapes,
        compiler_params=_params(1))(x, prev[0], prev[1], mod)


def _norm_bwd(dh, x, mod, dxo, y_raw, coef, name, after=(), prev=None):
    s_len, d = x.shape
    tm = TOKEN_TILE

    def body(*refs):
        if prev is None:
            dh_ref, x_ref, mod_ref, dxo_ref, y_ref, dx_ref, st_ref = refs
        else:
            dh_ref, x_ref, mod_ref, dxo_ref, y_ref, modp_ref, dx_ref, st_ref, dyp_ref = refs

        @pl.when(pl.program_id(0) == 0)
        def _():
            st_ref[...] = jnp.zeros_like(st_ref)

        xv, dhv, dxov = x_ref[...], dh_ref[...], dxo_ref[...]
        r = _rms(xv)
        xh = xv * r
        gain, scale = mod_ref[3:4, :], mod_ref[1:2, :]
        dn = dhv * (1.0 + scale)
        dxh = dn * gain
        dx = dxov + r * (dxh - xh * jnp.mean(dxh * xh, axis=-1, keepdims=True))
        dx_ref[...] = dx
        if prev is not None:
            dyp_ref[...] = (prev[1] * modp_ref[2:3, :] * dx).astype(BF16)
        st_ref[0:1, :] += jnp.sum(dhv, axis=0, keepdims=True)
        st_ref[1:2, :] += jnp.sum(dhv * (xh * gain), axis=0, keepdims=True)
        st_ref[2:3, :] += coef * jnp.sum(y_ref[...].astype(F32) * dxov, axis=0, keepdims=True)
        st_ref[3:4, :] += jnp.sum(dn * xh, axis=0, keepdims=True)

    tile = pl.BlockSpec((tm, d), lambda i: (i, 0))
    small = pl.BlockSpec((8, d), lambda i: (0, 0))
    operands = [dh, x, mod, dxo, y_raw] + ([] if prev is None else [prev[0]])
    in_specs = [tile, tile, small, tile, tile] + ([] if prev is None else [small])
    out_specs = [tile, small] + ([] if prev is None else [tile])
    out_shape = [jax.ShapeDtypeStruct((s_len, d), F32), jax.ShapeDtypeStruct((8, d), F32)]
    if prev is not None:
        out_shape.append(jax.ShapeDtypeStruct((s_len, d), BF16))
    return pl.pallas_call(
        _ordered(body, len(operands), after), name=name, grid=(s_len // tm,),
        in_specs=in_specs + [ANY] * len(after), out_specs=out_specs, out_shape=out_shape,
        compiler_params=_params(1),
    )(*operands, *after)


def _loss_grad(x, y, mod, target, name):
    s_len, d = x.shape
    tm = TOKEN_TILE

    def body(x_ref, y_ref, mod_ref, t_ref, do_ref, dy_ref, part_ref):
        @pl.when(pl.program_id(0) == 0)
        def _():
            part_ref[...] = jnp.zeros_like(part_ref)

        half_gate = 0.5 * mod_ref[2:3, :]
        err = (x_ref[...] + half_gate * y_ref[...]) - t_ref[...]
        do = err * (1.0 / d)
        do_ref[...] = do
        dy_ref[...] = (half_gate * do).astype(BF16)
        sq = err * err
        part_ref[...] += jnp.sum(sq.reshape(tm // 8, 8, d), axis=0)

    tile = pl.BlockSpec((tm, d), lambda i: (i, 0))
    small = pl.BlockSpec((8, d), lambda i: (0, 0))
    return pl.pallas_call(
        body, name=name, grid=(s_len // tm,),
        in_specs=[tile, tile, small, tile],
        out_specs=[tile, tile, small],
        out_shape=[jax.ShapeDtypeStruct((s_len, d), F32), jax.ShapeDtypeStruct((s_len, d), BF16),
                   jax.ShapeDtypeStruct((8, d), F32)],
        compiler_params=_params(1),
    )(x, y, mod, target)


def _adamw_math(w, g, m, v):
    m = ADAM_B1 * m + (1.0 - ADAM_B1) * g
    v = ADAM_B2 * v + (1.0 - ADAM_B2) * (g * g)
    m_hat = m / (1.0 - ADAM_B1 ** ADAM_STEP)
    v_hat = v / (1.0 - ADAM_B2 ** ADAM_STEP)
    delta = -ADAM_LR * (m_hat / (jnp.sqrt(v_hat) + ADAM_EPS) + ADAM_WD * w)
    return delta, m, v


def _adamw(w, g, m, v, name, after=()):
    r, cols = w.shape
    tr = max([t for t in (r // k for k in (1, 2, 4, 8, 16)) if t % 8 == 0 and r % t == 0
              and t * cols * 4 <= ADAMW_TILE_BYTES] or [r])

    def body(w_ref, g_ref, m_ref, v_ref, go_ref, d_ref, nm_ref, nv_ref):
        gv = g_ref[...]
        go_ref[...] = gv
        d_ref[...], nm_ref[...], nv_ref[...] = _adamw_math(w_ref[...], gv, m_ref[...], v_ref[...])

    tile = pl.BlockSpec((tr, cols), lambda i: (i, 0))
    shape = jax.ShapeDtypeStruct((r, cols), F32)
    return pl.pallas_call(
        _ordered(body, 4, after), name=name, grid=(r // tr,),
        in_specs=[tile] * 4 + [ANY] * len(after), out_specs=[tile] * 4, out_shape=[shape] * 4,
        compiler_params=_params(1),
    )(w, g, m, v, *after)


def _in_parts(tm, n_qkv, n_rest):
    def part(lo, n_blk):
        return pl.BlockSpec((tm, IN_BLOCK), lambda i, j: (i, jnp.clip(j - lo, 0, n_blk - 1)))
    return [part(0, n_qkv), part(n_qkv, n_qkv), part(2 * n_qkv, n_qkv), part(3 * n_qkv, n_rest)]


def _pick_part(j, n_qkv, refs, fn):
    bounds = [0, n_qkv, 2 * n_qkv, 3 * n_qkv]
    for p, ref in enumerate(refs):
        inside = j >= bounds[p]
        if p + 1 < len(refs):
            inside = inside & (j < bounds[p + 1])
        pl.when(inside)(lambda ref=ref: fn(ref))


def _rows(base, count, stride):
    return pl.ds(base, count) if stride == 1 else pl.ds(base, count, stride=stride)


REORDER_STRIDE = 4


def _reorder_plan(dil, parts=1):
    inner = min(dil, REORDER_STRIDE)
    return inner, dil // inner, SLAB // parts // inner, SLAB // dil


def _to_residue_order(dst, src, dil, tmp, part=0, parts=1):
    inner, outer, big, seg = _reorder_plan(dil, parts)
    piece = seg // parts
    if outer == 1:
        for r in range(dil):
            dst[pl.ds(r * seg + part * piece, piece), :] = src[_rows(r, piece, dil), :].astype(dst.dtype)
        return
    for b in range(inner):
        tmp[pl.ds(b * big, big), :] = src[_rows(b, big, inner), :]
    for a in range(outer):
        for b in range(inner):
            dst[pl.ds((inner * a + b) * seg + part * piece, piece), :] = (
                tmp[_rows(b * big + a, piece, outer), :].astype(dst.dtype))


def _to_token_order(dst, src, dil, tmp):
    inner, outer, big, seg = _reorder_plan(dil)
    if outer == 1:
        for r in range(dil):
            dst[_rows(r, seg, dil), :] = src[pl.ds(r * seg, seg), :]
        return
    for a in range(outer):
        for b in range(inner):
            tmp[_rows(b * big + a, seg, outer), :] = src[pl.ds((inner * a + b) * seg, seg), :]
    for b in range(inner):
        dst[_rows(b, big, inner), :] = tmp[pl.ds(b * big, big), :]


def _in_proj(h, w, q_norm, k_norm, name):
    s_len, d = h.shape
    tm = PROJ_TILE
    assert tm == SLAB and IN_BLOCK == HEADS * HEAD_DIM
    steps = w.shape[1] // IN_BLOCK
    n_qkv = 3 * QKV // IN_BLOCK
    parts = 4
    rows = [pl.ds(p * (tm // parts), tm // parts) for p in range(parts)]

    gains = jnp.concatenate([q_norm, k_norm, jnp.ones((6, HEAD_DIM), F32)], axis=0)

    def body(h_ref, w_ref, gains_ref, qkv_ref, rest_ref, hat_ref, tok_s, tmp_s):
        j = pl.program_id(1)
        sect = j // N_GROUPS
        multiply = lambda p: _dot(h_ref[rows[p], :], w_ref[...])

        def emit(gi):
            dil = DILATIONS[gi]
            res = [multiply(p) for p in range(parts)]
            gain = gains_ref[pl.ds(sect, 1), :]
            plain = sect == 2
            for p in range(parts):
                qkv_ref[rows[p], :] = res[p]
                for hh in range(HEADS):
                    cols = slice(hh * HEAD_DIM, (hh + 1) * HEAD_DIM)
                    x = res[p][:, cols]
                    tok_s[...] = (x * jnp.where(plain, 1.0, _rms(x))) * gain
                    _to_residue_order(hat_ref.at[:, cols], tok_s, dil, tmp_s, p, parts)

        for gi in range(N_GROUPS):
            pl.when((j < n_qkv) & (j % N_GROUPS == gi))(lambda gi=gi: emit(gi))

        @pl.when(j >= n_qkv)
        def _():
            for p in range(parts):
                rest_ref[rows[p], :] = multiply(p).astype(BF16)

    qkv_blk = pl.BlockSpec((tm, IN_BLOCK), lambda i, j: (i, jnp.minimum(j, n_qkv - 1)))
    return pl.pallas_call(
        body, name=name, grid=(s_len // tm, steps),
        in_specs=[pl.BlockSpec((tm, d), lambda i, j: (i, 0)), pl.BlockSpec((d, IN_BLOCK), lambda i, j: (0, j)),
                  pl.BlockSpec((8, HEAD_DIM), lambda i, j: (0, 0))],
        out_specs=[qkv_blk, pl.BlockSpec((tm, IN_BLOCK), lambda i, j: (i, jnp.maximum(j - n_qkv, 0))), qkv_blk],
        out_shape=[jax.ShapeDtypeStruct((s_len, 3 * QKV), F32),
                   jax.ShapeDtypeStruct((s_len, w.shape[1] - 3 * QKV), BF16),
                   jax.ShapeDtypeStruct((s_len, 3 * QKV), BF16)],
        scratch_shapes=[pltpu.VMEM((tm // parts, HEAD_DIM), F32)] * 2,
        compiler_params=_params(2),
    )(h, w, gains)


def _in_proj_bwd(dq, dk, dv, drest, w, name, after=()):
    s_len = dq.shape[0]
    d = w.shape[0]
    tm = TOKEN_TILE

    def body(dq_ref, dk_ref, dv_ref, dr_ref, w_ref, o_ref):
        total, lo = None, 0
        for a_ref in (dq_ref, dk_ref, dv_ref, dr_ref):
            width = a_ref.shape[1]
            part = _dot_nt(a_ref[...], w_ref[:, lo:lo + width])
            total = part if total is None else total + part
            lo += width
        o_ref[...] = total

    tile = lambda width: pl.BlockSpec((tm, width), lambda i: (i, 0))
    whole_w = pl.BlockSpec(w.shape, lambda i: (0, 0), pipeline_mode=pl.Buffered(1))
    return pl.pallas_call(
        _ordered(body, 5, after), name=name, grid=(s_len // tm,),
        in_specs=[tile(QKV)] * 3 + [tile(drest.shape[1]), whole_w] + [ANY] * len(after),
        out_specs=tile(d),
        out_shape=jax.ShapeDtypeStruct((s_len, d), F32),
        compiler_params=_params(1),
    )(dq, dk, dv, drest, w, *after)


def _wgrad(x, y, x_spec, y_spec, out_shape, out_spec, acc_shape, n_chunks, name, x_transposed=False, after=()):
    s_len = y.shape[-2]
    ts = WGRAD_TILE
    steps = s_len // ts

    def body(x_ref, y_ref, o_ref, acc_ref):
        s = pl.program_id(1)

        @pl.when(s == 0)
        def _():
            acc_ref[...] = jnp.zeros_like(acc_ref)

        if x_transposed:
            n_rows = acc_shape[0]
            rows = [pl.ds(p * (n_rows // ACC_PIECES), n_rows // ACC_PIECES) for p in range(ACC_PIECES)]
            products = [_dot(x_ref[r, :], y_ref[...]) for r in rows]
            for r, product in zip(rows, products):
                acc_ref[r, :] += product
        else:
            cols = _pieces(acc_shape[1])
            products = [_dot_tn(x_ref[...], y_ref[:, c]) for c in cols]
            for c, product in zip(cols, products):
                acc_ref[:, c] += product

        @pl.when(s == steps - 1)
        def _():
            o_ref[...] = acc_ref[...].astype(o_ref.dtype)

    return pl.pallas_call(
        _ordered(body, 2, after), name=name, grid=(n_chunks, steps),
        in_specs=[x_spec(ts), y_spec(ts)] + [ANY] * len(after), out_specs=out_spec,
        out_shape=jax.ShapeDtypeStruct(out_shape, BF16),
        scratch_shapes=[pltpu.VMEM(acc_shape, F32)],
        compiler_params=_params(2),
    )(x, y, *after)


def _pieces(width, piece=256):
    return [slice(a, min(a + piece, width)) for a in range(0, width, piece)]


def _ffn_fwd(h, w_gate, w_up, w_down, name):
    s_len, d = h.shape
    n_chunks, _, fs = w_gate.shape
    tm = FFN_TILE

    def body(h_ref, wg_ref, wu_ref, wd_ref, g_ref, u_ref, y_ref):
        j = pl.program_id(1)
        hv = h_ref[...]
        pieces = _pieces(fs)
        first = lambda cols: (_dot(hv, wg_ref[:, cols]), _dot(hv, wu_ref[:, cols]))
        total = None
        ahead = first(pieces[0])
        for k, cols in enumerate(pieces):
            g, u = ahead
            if k + 1 < len(pieces):
                ahead = first(pieces[k + 1])
            g_ref[:, cols] = g.astype(BF16)
            u_ref[:, cols] = u.astype(BF16)
            act = (g * _sigmoid(g)) * u
            part = _dot(act.astype(BF16), wd_ref[cols, :])
            total = part if total is None else total + part

        @pl.when(j == 0)
        def _():
            y_ref[...] = total

        @pl.when(j > 0)
        def _():
            y_ref[...] += total

    tile = pl.BlockSpec((tm, d), lambda i, j: (i, 0))
    hid = pl.BlockSpec((None, tm, fs), lambda i, j: (j, i, 0))
    w_in_spec = pl.BlockSpec((None, d, fs), lambda i, j: (j, 0, 0))
    hid_shape = jax.ShapeDtypeStruct((n_chunks, s_len, fs), BF16)
    return pl.pallas_call(
        body, name=name, grid=(s_len // tm, n_chunks),
        in_specs=[tile, w_in_spec, w_in_spec, pl.BlockSpec((None, fs, d), lambda i, j: (j, 0, 0))],
        out_specs=[hid, hid, tile],
        out_shape=[hid_shape, hid_shape, jax.ShapeDtypeStruct((s_len, d), F32)],
        compiler_params=_params(2),
    )(h, w_gate, w_up, w_down)


def _ffn_bwd(dy, g_pre, u_pre, w_gate, w_up, w_down, name):
    s_len, d = dy.shape
    n_chunks, _, fs = w_gate.shape
    tm = FFN_TILE

    def body(dy_ref, g_ref, u_ref, wg_ref, wu_ref, wd_ref, dh_ref, dg_ref, du_ref, a_ref):
        j = pl.program_id(1)
        dyv = dy_ref[...]
        pieces = _pieces(fs)
        first = lambda cols: _dot_nt(dyv, wd_ref[cols, :])
        total = None
        ahead = first(pieces[0])
        for k, cols in enumerate(pieces):
            da = ahead
            if k + 1 < len(pieces):
                ahead = first(pieces[k + 1])
            g = g_ref[:, cols].astype(F32)
            u = u_ref[:, cols].astype(F32)
            sg = _sigmoid(g)
            silu = g * sg
            dg = (da * u * (sg * (1.0 + g * (1.0 - sg)))).astype(BF16)
            du = (da * silu).astype(BF16)
            dg_ref[:, cols] = dg
            du_ref[:, cols] = du
            a_ref[:, cols] = (silu * u).astype(BF16)
            part = _dot_nt(dg, wg_ref[:, cols]) + _dot_nt(du, wu_ref[:, cols])
            total = part if total is None else total + part

        @pl.when(j == 0)
        def _():
            dh_ref[...] = total

        @pl.when(j > 0)
        def _():
            dh_ref[...] += total

    tile = pl.BlockSpec((tm, d), lambda i, j: (i, 0))
    hid = pl.BlockSpec((None, tm, fs), lambda i, j: (j, i, 0))
    w_in_spec = pl.BlockSpec((None, d, fs), lambda i, j: (j, 0, 0))
    hid_shape = jax.ShapeDtypeStruct((n_chunks, s_len, fs), BF16)
    return pl.pallas_call(
        body, name=name, grid=(s_len // tm, n_chunks),
        in_specs=[tile, hid, hid, w_in_spec, w_in_spec, pl.BlockSpec((None, fs, d), lambda i, j: (j, 0, 0))],
        out_specs=[tile, hid, hid, hid],
        out_shape=[jax.ShapeDtypeStruct((s_len, d), F32), hid_shape, hid_shape, hid_shape],
        compiler_params=_params(2),
    )(dy, g_pre, u_pre, w_gate, w_up, w_down)


def _ffn_wgrads(ht, dg, du, act, dy, tag, after=()):
    n_chunks, s_len, fs = dg.shape
    d = ht.shape[0]
    tok = lambda ts: pl.BlockSpec((ts, d), lambda c, s: (s, 0))
    tok_t = lambda ts: pl.BlockSpec((d, ts), lambda c, s: (0, s))
    hid = lambda ts: pl.BlockSpec((None, ts, fs), lambda c, s: (c, s, 0))
    d_up = pl.BlockSpec((None, d, fs), lambda c, s: (c, 0, 0))
    d_down = pl.BlockSpec((None, fs, d), lambda c, s: (c, 0, 0))
    dwg = _wgrad(ht, dg, tok_t, hid, (n_chunks, d, fs), d_up, (d, fs), n_chunks, tag + "_dwg", True, after)
    dwu = _wgrad(ht, du, tok_t, hid, (n_chunks, d, fs), d_up, (d, fs), n_chunks, tag + "_dwu", True, after)
    dwd = _wgrad(act, dy, hid, tok, (n_chunks, fs, d), d_down, (fs, d), n_chunks, tag + "_dwd", False, after)
    return dwg, dwu, dwd


def _band_bias():
    qi = lax.broadcasted_iota(jnp.int32, (ATTN_BLOCK, 2 * ATTN_BLOCK), 0)
    kj = lax.broadcasted_iota(jnp.int32, (ATTN_BLOCK, 2 * ATTN_BLOCK), 1)
    band = (kj >= qi) & (kj <= qi + ATTN_BLOCK)
    return jnp.where(band, 0.0, NEG), jnp.where(band & (kj >= ATTN_BLOCK), 0.0, NEG)


def _qkv_specs(slab_of, sections):
    def spec(sect, back):
        return pl.BlockSpec((SLAB, HEAD_DIM),
                            lambda h, s, g: (jnp.maximum(slab_of(s) - back, 0), (sect * N_GROUPS + g) * HEADS + h))
    return [spec(sect, back) for sect, back in sections]


HAT_BLOCKS = [(0, 0), (1, 0), (2, 0), (1, 1), (2, 1)]


def _stage_keys(k_ref, v_ref, kp_ref, vp_ref, kbuf, vbuf, dil, n):
    run = SLAB // dil
    for r in range(dil):
        own, before = pl.ds(r * run, run), pl.ds(2 * r * run, run)
        kbuf[pl.ds((2 * r + 1) * run, run), :] = k_ref[own, :]
        vbuf[pl.ds((2 * r + 1) * run, run), :] = v_ref[own, :]

        @pl.when(n > 0)
        def _():
            kbuf[before, :] = kp_ref[own, :]
            vbuf[before, :] = vp_ref[own, :]

        @pl.when(n == 0)
        def _():
            kbuf[before, :] = jnp.zeros((run, HEAD_DIM), BF16)
            vbuf[before, :] = jnp.zeros((run, HEAD_DIM), BF16)


def _for_each_tile(dil, n, first_fn, rest_fn):
    run = SLAB // dil
    bias, first_bias = _band_bias()
    tiles = []
    for jj in range(run // ATTN_BLOCK):
        start = jj * ATTN_BLOCK
        tile_bias = jnp.where(n == 0, first_bias, bias) if jj == 0 else bias
        for r in range(dil):
            tiles.append((pl.ds(r * run + start, ATTN_BLOCK),
                          pl.ds((2 * r + 1) * run - ATTN_BLOCK + start, 2 * ATTN_BLOCK), tile_bias))
    ahead = first_fn(*tiles[0])
    for t, tile in enumerate(tiles):
        begun = ahead
        if t + 1 < len(tiles):
            ahead = first_fn(*tiles[t + 1])
        rest_fn(*tile, begun)


def _attn_fwd(hat, name):
    s_len = hat.shape[0]
    e = HEAD_DIM
    n_slabs = s_len // SLAB

    def body(q_ref, k_ref, v_ref, kp_ref, vp_ref, o_ref, lse_ref, kbuf, vbuf, m_s, l_s, acc_s, m_p, l_p, acc_p, tmp_s):
        n, grp = pl.program_id(1), pl.program_id(2)

        def run(gi, dil):
            _stage_keys(k_ref, v_ref, kp_ref, vp_ref, kbuf, vbuf, dil, n)

            def scores(q_rows, kv_rows, bias):
                return _dot_nt(q_ref[q_rows, :], kbuf[kv_rows, :])

            def rest(q_rows, kv_rows, bias, qk):
                s = qk * ATTN_SCALE + bias
                m = jnp.max(s, axis=-1, keepdims=True)
                p = jnp.exp(s - m)
                m_p[q_rows, :] = jnp.broadcast_to(m, (ATTN_BLOCK, e))
                l_p[q_rows, :] = jnp.broadcast_to(jnp.sum(p, axis=-1, keepdims=True), (ATTN_BLOCK, e))
                acc_p[q_rows, :] = _dot(p.astype(BF16), vbuf[kv_rows, :])

            _for_each_tile(dil, n, scores, rest)
            _to_token_order(m_s.at[gi], m_p, dil, tmp_s)
            _to_token_order(l_s.at[gi], l_p, dil, tmp_s)
            _to_token_order(acc_s.at[gi], acc_p, dil, tmp_s)

        for gi, dil in enumerate(DILATIONS):
            pl.when(grp == gi)(lambda gi=gi, dil=dil: run(gi, dil))

        @pl.when(grp == N_GROUPS - 1)
        def _():
            m_all = jnp.maximum(jnp.maximum(m_s[0], m_s[1]), m_s[2])
            den = jnp.zeros((SLAB, e), F32)
            num = jnp.zeros((SLAB, e), F32)
            for gi in range(N_GROUPS):
                w = jnp.exp(m_s[gi] - m_all)
                den += l_s[gi] * w
                num += acc_s[gi] * w
            o_ref[...] = (num / den).astype(BF16)
            lse_ref[...] = m_all + jnp.log(den)

    out = pl.BlockSpec((SLAB, e), lambda h, n, g: (n, h))
    return pl.pallas_call(
        body, name=name, grid=(HEADS, n_slabs, N_GROUPS),
        in_specs=_qkv_specs(lambda n: n, HAT_BLOCKS),
        out_specs=[out, out],
        out_shape=[jax.ShapeDtypeStruct((s_len, HEADS * e), BF16), jax.ShapeDtypeStruct((s_len, HEADS * e), F32)],
        scratch_shapes=[pltpu.VMEM((2 * SLAB, e), BF16), pltpu.VMEM((2 * SLAB, e), BF16),
                        pltpu.VMEM((N_GROUPS, SLAB, e), F32), pltpu.VMEM((N_GROUPS, SLAB, e), F32),
                        pltpu.VMEM((N_GROUPS, SLAB, e), F32)]
        + [pltpu.VMEM((SLAB, e), F32)] * 4,
        compiler_params=_params(3),
    )(hat, hat, hat, hat, hat)


def _attn_bwd(qkv, hat, d_out, out, lse, q_norm, k_norm, name):
    s_len = qkv.shape[0]
    e = HEAD_DIM
    n_slabs = s_len // SLAB

    def body(q_ref, k_ref, v_ref, kp_ref, vp_ref, qraw_ref, kraw_ref, do_ref, o_ref, lse_ref, qn_ref, kn_ref,
             dq_ref, dk_ref, dv_ref, st_ref, kbuf, vbuf, stat_s, dqs, dkb, dvb, dk_tok, dv_tok, carry,
             do_p, stat_p, dq_p, dk_p, dv_p, tmp_s, do16_p):
        head, step, grp = pl.program_id(0), pl.program_id(1), pl.program_id(2)
        n = n_slabs - 1 - step
        dkb[...] = jnp.zeros_like(dkb)
        dvb[...] = jnp.zeros_like(dvb)
        @pl.when(grp == 0)
        def _():
            lane = lax.broadcasted_iota(jnp.int32, (SLAB, e), 1)
            stat_s[...] = jnp.where(lane < e // 2, lse_ref[...],
                                    jnp.sum(do_ref[...] * o_ref[...].astype(F32), axis=-1, keepdims=True))

        @pl.when((head == 0) & (step == 0) & (grp == 0))
        def _():
            st_ref[...] = jnp.zeros_like(st_ref)

        def run(gi, dil):
            seg = SLAB // dil
            _stage_keys(k_ref, v_ref, kp_ref, vp_ref, kbuf, vbuf, dil, n)

            @pl.when(step == 0)
            def _():
                carry[gi] = jnp.zeros((2, SLAB, e), F32)

            _to_residue_order(do_p, do_ref, dil, tmp_s)
            do16_p[...] = do_p[...].astype(BF16)
            _to_residue_order(stat_p, stat_s, dil, tmp_s)

            def scores(q_rows, kv_rows, bias):
                return _dot_nt(q_ref[q_rows, :], kbuf[kv_rows, :]), _dot_nt(do16_p[q_rows, :], vbuf[kv_rows, :])

            def rest(q_rows, kv_rows, bias, begun):
                qk, dp = begun
                q = q_ref[q_rows, :]
                k = kbuf[kv_rows, :]
                stat = stat_p[q_rows, :]
                p = jnp.exp(qk * ATTN_SCALE + bias - stat[:, 0:1])
                ds = (p * (dp - stat[:, e // 2:e // 2 + 1]) * ATTN_SCALE).astype(BF16)
                dq_p[q_rows, :] = _dot(ds, k)
                dkb[kv_rows, :] += _dot_tn(ds, q)
                dvb[kv_rows, :] += _dot_tn(p.astype(BF16), do16_p[q_rows, :])

            _for_each_tile(dil, n, scores, rest)
            for r in range(dil):
                own, before = pl.ds((2 * r + 1) * seg, seg), pl.ds(2 * r * seg, seg)
                kept = pl.ds(r * seg, seg)
                dk_p[kept, :] = dkb[own, :] + carry.at[gi, 0][kept, :]
                dv_p[kept, :] = dvb[own, :] + carry.at[gi, 1][kept, :]
                carry.at[gi, 0][kept, :] = dkb[before, :]
                carry.at[gi, 1][kept, :] = dvb[before, :]
            _to_token_order(dqs, dq_p, dil, tmp_s)
            _to_token_order(dk_tok, dk_p, dil, tmp_s)
            _to_token_order(dv_tok, dv_p, dil, tmp_s)

            def norm_bwd(raw, gain, d_hat):
                r = _rms(raw)
                y = raw * r
                dy = d_hat * gain
                return r * (dy - y * jnp.mean(dy * y, axis=-1, keepdims=True)), jnp.sum(d_hat * y, axis=0, keepdims=True)

            dq, dqn = norm_bwd(qraw_ref[...], qn_ref[...], dqs[...])
            dk, dkn = norm_bwd(kraw_ref[...], kn_ref[...], dk_tok[...])
            dq_ref[...] = dq.astype(BF16)
            dk_ref[...] = dk.astype(BF16)
            dv_ref[...] = dv_tok[...].astype(BF16)
            st_ref[0:1, :] += dqn
            st_ref[1:2, :] += dkn

        for gi, dil in enumerate(DILATIONS):
            pl.when(grp == gi)(lambda gi=gi, dil=dil: run(gi, dil))

    slab_of = lambda s: n_slabs - 1 - s
    small = pl.BlockSpec((1, e), lambda h, s, g: (0, 0))
    head_blk = pl.BlockSpec((SLAB, e), lambda h, s, g: (slab_of(s), h))
    grad_blk = pl.BlockSpec((SLAB, e), lambda h, s, g: (slab_of(s), g * HEADS + h))
    grad_shape = jax.ShapeDtypeStruct((s_len, QKV), BF16)
    return pl.pallas_call(
        body, name=name, grid=(HEADS, n_slabs, N_GROUPS),
        in_specs=(_qkv_specs(slab_of, HAT_BLOCKS) + _qkv_specs(slab_of, [(0, 0), (1, 0)])
                  + [head_blk, head_blk, head_blk, small, small]),
        out_specs=[grad_blk, grad_blk, grad_blk, pl.BlockSpec((8, e), lambda h, s, g: (0, 0))],
        out_shape=[grad_shape, grad_shape, grad_shape, jax.ShapeDtypeStruct((8, e), F32)],
        scratch_shapes=[pltpu.VMEM((2 * SLAB, e), BF16), pltpu.VMEM((2 * SLAB, e), BF16), pltpu.VMEM((SLAB, e), F32),
                        pltpu.VMEM((SLAB, e), F32), pltpu.VMEM((2 * SLAB, e), F32), pltpu.VMEM((2 * SLAB, e), F32),
                        pltpu.VMEM((SLAB, e), F32), pltpu.VMEM((SLAB, e), F32),
                        pltpu.VMEM((N_GROUPS, 2, SLAB, e), F32)]
        + [pltpu.VMEM((SLAB, e), F32)] * 6 + [pltpu.VMEM((SLAB, e), BF16)],
        compiler_params=_params(3),
    )(hat, hat, hat, hat, hat, qkv, qkv, d_out, out, lse, q_norm, k_norm)


def _shift_rows(x, by, edge, forward):
    t_len = x.shape[0]
    row = lax.broadcasted_iota(jnp.int32, x.shape, 0)
    if forward:
        out = pltpu.roll(x, by, 0)
        for i in range(by):
            out = jnp.where(row == i, edge[8 - by + i:8 - by + i + 1, :], out)
    else:
        out = pltpu.roll(x, t_len - by, 0)
        for i in range(by):
            out = jnp.where(row == t_len - by + i, edge[i:i + 1, :], out)
    return out


def _mix_fwd(x, o, rest, mod, mod_next, conv_w, w_attn, w_conv, w_out, name):
    s_len, d = x.shape
    tm = MIX_TILE
    a_w = o.shape[1]

    def body(x_ref, o_ref, u_ref, b_ref, c_ref, ga_ref, gc_ref, mod_ref, modn_ref, cw_ref, wa_ref, wc_ref, wo_ref,
             xo_ref, z_ref, ya_ref, yc_ref, conv_ref, yb_ref, m_ref, h_ref, ht_ref, carry):
        @pl.when(pl.program_id(0) == 0)
        def _():
            carry[...] = jnp.zeros_like(carry)

        xc = c_ref[...].astype(F32) * u_ref[...].astype(F32)
        edge = carry[...]
        conv = (_shift_rows(xc, 2, edge, True) * cw_ref[0:1, :] + _shift_rows(xc, 1, edge, True) * cw_ref[1:2, :]
                + xc * cw_ref[2:3, :])
        carry[...] = xc[tm - 8:tm, :]
        yb = (b_ref[...].astype(F32) * conv).astype(BF16)
        ya = _dot(o_ref[...], wa_ref[...])
        yc = _dot(yb, wc_ref[...])
        merged = (_sigmoid(ga_ref[...].astype(F32)) * ya + _sigmoid(gc_ref[...].astype(F32)) * yc).astype(BF16)
        z = _dot(merged, wo_ref[...])
        xo = x_ref[...] + mod_ref[2:3, :] * z
        xo_ref[...] = xo
        hn = ((xo * _rms(xo)) * modn_ref[3:4, :]) * (1.0 + modn_ref[1:2, :]) + modn_ref[0:1, :]
        h_ref[...] = hn.astype(BF16)
        ht_ref[...] = hn.T.astype(BF16)
        z_ref[...] = z.astype(BF16)
        ya_ref[...] = ya.astype(BF16)
        yc_ref[...] = yc.astype(BF16)
        conv_ref[...] = conv.astype(BF16)
        yb_ref[...] = yb
        m_ref[...] = merged

    tile = pl.BlockSpec((tm, d), lambda i: (i, 0))
    sect = lambda k: pl.BlockSpec((tm, d), lambda i: (i, k))
    att = pl.BlockSpec((tm, a_w), lambda i: (i, 0))
    const = lambda shape: pl.BlockSpec(shape, lambda i: (0, 0))
    f32_out = jax.ShapeDtypeStruct((s_len, d), F32)
    b16_out = jax.ShapeDtypeStruct((s_len, d), BF16)
    return pl.pallas_call(
        body, name=name, grid=(s_len // tm,),
        in_specs=[tile, att, sect(0), sect(1), sect(2), sect(3), sect(4), const((8, d)), const((8, d)), const((8, d)),
                  const((a_w, d)), const((d, d)), const((d, d))],
        out_specs=[tile] * 7 + [tile, pl.BlockSpec((d, tm), lambda i: (0, i))],
        out_shape=[f32_out] + [b16_out] * 6 + [b16_out, jax.ShapeDtypeStruct((d, s_len), BF16)],
        scratch_shapes=[pltpu.VMEM((8, d), F32)],
        compiler_params=_params(1),
    )(x, o, rest, rest, rest, rest, rest, mod, mod_next, conv_w, w_attn, w_conv, w_out)


def _mix_bwd(dxo, ya, yc, conv, rest, mod, conv_w, w_attn, w_conv, w_out, a_w, name, after=()):
    s_len, d = dxo.shape
    tm = MIX_TILE
    n_tiles = s_len // tm

    def body(dxo_ref, ya_ref, yc_ref, conv_ref, u_ref, b_ref, c_ref, ga_ref, gc_ref, mod_ref, cw_ref,
             wa_ref, wc_ref, wo_ref, do_ref, drest_ref, dz_ref, dya_ref, dyc_ref, st_ref, carry):
        @pl.when(pl.program_id(0) == 0)
        def _():
            carry[...] = jnp.zeros_like(carry)
            st_ref[...] = jnp.zeros_like(st_ref)

        dz = (mod_ref[2:3, :] * dxo_ref[...]).astype(BF16)
        dz_ref[...] = dz
        dm = _dot_nt(dz, wo_ref[...])
        sa, sc = _sigmoid(ga_ref[...].astype(F32)), _sigmoid(gc_ref[...].astype(F32))
        dya = (dm * sa).astype(BF16)
        dyc = (dm * sc).astype(BF16)
        dya_ref[...] = dya
        dyc_ref[...] = dyc
        drest_ref[:, 3 * d:4 * d] = (dm * ya_ref[...].astype(F32) * (sa * (1.0 - sa))).astype(BF16)
        drest_ref[:, 4 * d:5 * d] = (dm * yc_ref[...].astype(F32) * (sc * (1.0 - sc))).astype(BF16)
        do_ref[...] = _dot_nt(dya, wa_ref[...])
        dyb = _dot_nt(dyc, wc_ref[...])
        drest_ref[:, d:2 * d] = (dyb * conv_ref[...].astype(F32)).astype(BF16)
        dconv = dyb * b_ref[...].astype(F32)
        edge = carry[...]
        sh1 = _shift_rows(dconv, 1, edge, False)
        sh2 = _shift_rows(dconv, 2, edge, False)
        carry[...] = dconv[0:8, :]
        dxc = dconv * cw_ref[2:3, :] + sh1 * cw_ref[1:2, :] + sh2 * cw_ref[0:1, :]
        u, c = u_ref[...].astype(F32), c_ref[...].astype(F32)
        xc = c * u
        drest_ref[:, 0:d] = (dxc * c).astype(BF16)
        drest_ref[:, 2 * d:3 * d] = (dxc * u).astype(BF16)
        st_ref[0:1, :] += jnp.sum(xc * sh2, axis=0, keepdims=True)
        st_ref[1:2, :] += jnp.sum(xc * sh1, axis=0, keepdims=True)
        st_ref[2:3, :] += jnp.sum(xc * dconv, axis=0, keepdims=True)

    rev = lambda i: n_tiles - 1 - i
    tile = pl.BlockSpec((tm, d), lambda i: (rev(i), 0))
    sect = lambda k: pl.BlockSpec((tm, d), lambda i: (rev(i), k))
    const = lambda shape: pl.BlockSpec(shape, lambda i: (0, 0))
    b16_out = jax.ShapeDtypeStruct((s_len, d), BF16)
    return pl.pallas_call(
        _ordered(body, 14, after), name=name, grid=(n_tiles,),
        in_specs=[tile, tile, tile, tile, sect(0), sect(1), sect(2), sect(3), sect(4), const((8, d)), const((8, d)),
                  const((a_w, d)), const((d, d)), const((d, d))] + [ANY] * len(after),
        out_specs=[pl.BlockSpec((tm, a_w), lambda i: (rev(i), 0)), pl.BlockSpec((tm, 5 * d), lambda i: (rev(i), 0)),
                   tile, tile, tile, const((8, d))],
        out_shape=[jax.ShapeDtypeStruct((s_len, a_w), F32), jax.ShapeDtypeStruct((s_len, 5 * d), BF16),
                   b16_out, b16_out, b16_out, jax.ShapeDtypeStruct((8, d), F32)],
        scratch_shapes=[pltpu.VMEM((8, d), F32)],
        compiler_params=_params(1),
    )(dxo, ya, yc, conv, rest, rest, rest, rest, rest, mod, conv_w, w_attn, w_conv, w_out, *after)


ADA_COLS = 128


def _ada_fwd(c_all, w_shard, b_shard, name):
    d, cols = w_shard.shape

    def body(c_ref, w_ref, b_ref, o_ref):
        cv = c_ref[...]
        o_ref[...] = jnp.dot(cv * _sigmoid(cv), w_ref[...], preferred_element_type=F32,
                             precision=lax.Precision.HIGHEST) + b_ref[...]

    return pl.pallas_call(
        body, name=name, grid=(cols // ADA_COLS,),
        in_specs=[pl.BlockSpec((8, d), lambda j: (0, 0)), pl.BlockSpec((d, ADA_COLS), lambda j: (0, j)),
                  pl.BlockSpec((1, ADA_COLS), lambda j: (0, j))],
        out_specs=pl.BlockSpec((8, ADA_COLS), lambda j: (0, j)),
        out_shape=jax.ShapeDtypeStruct((8, cols), F32),
        compiler_params=_params(1),
    )(c_all, w_shard, b_shard)


def _ada_bwd(c_all, dmod_shard, w, m, v, name):
    d, cols = w.shape

    def body(c_ref, dm_ref, w_ref, m_ref, v_ref, g_ref, d_ref, nm_ref, nv_ref):
        cv = c_ref[...]
        g = lax.dot_general(cv * _sigmoid(cv), dm_ref[...], (((0,), (0,)), ((), ())),
                            preferred_element_type=F32, precision=lax.Precision.HIGHEST)
        g_ref[...] = g
        d_ref[...], nm_ref[...], nv_ref[...] = _adamw_math(w_ref[...], g, m_ref[...], v_ref[...])

    blk = pl.BlockSpec((d, ADA_COLS), lambda j: (0, j))
    shape = jax.ShapeDtypeStruct((d, cols), F32)
    return pl.pallas_call(
        body, name=name, grid=(cols // ADA_COLS,),
        in_specs=[pl.BlockSpec((8, d), lambda j: (0, 0)), pl.BlockSpec((8, ADA_COLS), lambda j: (0, j)), blk, blk, blk],
        out_specs=[blk] * 4, out_shape=[shape] * 4,
        compiler_params=_params(1),
    )(c_all, dmod_shard, w, m, v)


def _small_update(parts, w, m, v, name):
    n = w.shape[1]

    def body(p_ref, w_ref, m_ref, v_ref, g_ref, d_ref, nm_ref, nv_ref):
        g = p_ref[0:1, :]
        for i in range(1, 8):
            g = g + p_ref[i:i + 1, :]
        g_ref[...] = g
        d_ref[...], nm_ref[...], nv_ref[...] = _adamw_math(w_ref[...], g, m_ref[...], v_ref[...])

    shape = jax.ShapeDtypeStruct((1, n), F32)
    return pl.pallas_call(body, name=name, out_shape=[shape] * 4, compiler_params=_params())(parts, w, m, v)


def _cols_to_shards(w, n):
    r, nc = w.shape
    return w.reshape(r, n, nc // n).transpose(1, 0, 2)


def kernel(x, c, w_ada, b_ada, norm_ffn1, ffn1_w_gate, ffn1_w_up, ffn1_w_down, norm_mix, w_in, q_norm, k_norm, conv_w, w_attn_branch, w_conv_branch, w_out, norm_ffn2, ffn2_w_gate, ffn2_w_up, ffn2_w_down, loss_target, m_w_ada, m_b_ada, m_norm_ffn1, m_ffn1_w_gate, m_ffn1_w_up, m_ffn1_w_down, m_norm_mix, m_w_in, m_q_norm, m_k_norm, m_conv_w, m_w_attn_branch, m_w_conv_branch, m_w_out, m_norm_ffn2, m_ffn2_w_gate, m_ffn2_w_up, m_ffn2_w_down, v_w_ada, v_b_ada, v_norm_ffn1, v_ffn1_w_gate, v_ffn1_w_up, v_ffn1_w_down, v_norm_mix, v_w_in, v_q_norm, v_k_norm, v_conv_w, v_w_attn_branch, v_w_conv_branch, v_w_out, v_norm_ffn2, v_ffn2_w_gate, v_ffn2_w_up, v_ffn2_w_down):
    ix, iy, ic = _place()
    chip = 2 * ix + iy
    me = 4 * ix + 2 * iy + ic
    xs = x[0]
    target = loss_target[0]
    s_len, d = xs.shape
    ada_cols = w_ada.shape[2]
    conv_cols = conv_w.shape[2]

    conv_rows = jnp.zeros((8, conv_cols), F32).at[0:3].set(conv_w[0])
    small_in = jnp.concatenate([jnp.broadcast_to(c, (8, d)), conv_rows], axis=1)
    small_all = _allgather8(small_in, "gather_c").reshape(8, 8, d + conv_cols)
    c_all = small_all[:, 0, :d]
    conv_full = small_all[0::2, 0:3, d:].transpose(1, 0, 2).reshape(3, N_CHIPS * conv_cols)
    conv_pad = jnp.zeros((8, N_CHIPS * conv_cols), F32).at[0:3].set(conv_full)
    b_shard = lax.dynamic_slice(b_ada, (0, chip * ada_cols), (1, ada_cols))
    mod_part = _ada_fwd(c_all, w_ada[0], b_shard, "ada_fwd")
    mod_all = _allgather8(mod_part, "gather_mod").reshape(N_CHIPS, 2, 8, ada_cols)[:, 0]
    mod_mine = lax.dynamic_slice(mod_all, (0, me, 0), (N_CHIPS, 1, ada_cols)).reshape(9, d)

    def mod_rows(i, gain):
        return jnp.zeros((8, d), F32).at[0:3].set(mod_mine[3 * i:3 * i + 3]).at[3:4].set(gain)

    mod1, mod2, mod3 = mod_rows(0, norm_ffn1), mod_rows(1, norm_mix), mod_rows(2, norm_ffn2)

    to16 = lambda w: w[0].astype(BF16)
    wg1, wu1, wd1 = _gather_weights([to16(ffn1_w_gate), to16(ffn1_w_up), to16(ffn1_w_down)], [False] * 3,
                                    "gather_ffn1", 1)
    h1, h1t = _norm_mod(xs, mod1, "norm1")
    (w_in_full,) = _gather_weights([to16(w_in)], [True], "gather_w_in", 2, after=(wd1, h1))

    g1, u1, y1 = _ffn_fwd(h1, wg1, wu1, wd1, "ffn1_fwd")
    x1, h2, h2t = _norm_mod(xs, mod2, "norm2", prev=(y1, mod1, 0.5))
    qkv, rest, qkv_hat = _in_proj(h2, w_in_full, q_norm, k_norm, "in_proj")
    w_ab, w_cb_g, w_o_g, wg2, wu2, wd2 = _gather_weights(
        [to16(w_attn_branch), to16(w_conv_branch), to16(w_out),
         to16(ffn2_w_gate), to16(ffn2_w_up), to16(ffn2_w_down)], [True] + [False] * 5,
        "gather_rest", 3, after=(h2,))
    a_w = w_ab.shape[0]
    w_cb = w_cb_g.reshape(d, d)
    w_o = w_o_g.reshape(d, d)
    o, lse = _attn_fwd(qkv_hat, "attn_fwd")
    x2, z, ya, yc, conv, yb, merged, h3, h3t = _mix_fwd(x1, o, rest, mod2, mod3, conv_pad, w_ab, w_cb, w_o, "mix_fwd")
    g3, u3, y3 = _ffn_fwd(h3, wg2, wu2, wd2, "ffn2_fwd")
    dx3, dy3, loss_part = _loss_grad(x2, y3, mod3, target, "loss")

    c_idx = jnp.reshape(ic, (1,)).astype(jnp.int32)
    chip_idx = jnp.stack([chip, ic]).astype(jnp.int32)

    def pair_send(grads, tag, collective_id):
        return _rs_pair_exchange(grads, "rs_pair_" + tag, collective_id)

    def chip_send(grads, from_sibling, names, tag, collective_id, after):
        pair_sums = [_pair_add(g, r, c_idx, "pair_add_" + nm, after) for g, r, nm in zip(grads, from_sibling, names)]
        return pair_sums, _rs_chip_exchange(pair_sums, "rs_chips_" + tag, collective_id)

    def reduce_finish(pair_sums, from_chips, names, tag, after):
        totals = [_chip_add(p, r, chip_idx, "chip_add_" + nm, after)
                  for p, r, nm in zip(pair_sums, from_chips, names)]
        return dict(zip(names, _rs_share(totals, "rs_share_" + tag)))

    names_a = ["ffn2_w_gate", "ffn2_w_up", "ffn2_w_down"]
    names_b = ["w_in", "w_attn_branch", "w_conv_branch", "w_out"]
    names_c = ["ffn1_w_gate", "ffn1_w_up", "ffn1_w_down"]

    dh3, dg3, du3, a3 = _ffn_bwd(dy3, g3, u3, wg2, wu2, wd2, "ffn2_bwd")
    grads_a = list(_ffn_wgrads(h3t, dg3, du3, a3, dy3, "ffn2"))
    sibling_a = pair_send(grads_a, "a", 7)
    dx2, st3 = _norm_bwd(dh3, x2, mod3, dx3, y3, 0.5, "norm3_bwd")
    sums_a, chips_a = chip_send(grads_a, sibling_a, names_a, "a", 4, after=(dx2,))

    do, drest, dz, dya, dyc, st_conv = _mix_bwd(dx2, ya, yc, conv, rest, mod2, conv_pad, w_ab, w_cb, w_o, a_w,
                                                "mix_bwd", after=tuple(sums_a))
    dq, dk, dv, st_qk = _attn_bwd(qkv, qkv_hat, do, o, lse, q_norm, k_norm, "attn_bwd")
    tok = lambda width: (lambda ts: pl.BlockSpec((ts, width), lambda cc, s: (s, 0)))
    colblk = lambda width: (lambda ts: pl.BlockSpec((ts, width), lambda cc, s: (s, cc)))
    tok_t = lambda ts: pl.BlockSpec((d, ts), lambda cc, s: (0, s))
    whole = pl.BlockSpec((d, QKV), lambda cc, s: (0, 0))
    dw_in = [_wgrad(h2t, part, tok_t, tok(QKV), (d, QKV), whole, (d, QKV), 1, "dw_in_" + nm, True)
             for part, nm in ((dq, "q"), (dk, "k"), (dv, "v"))]
    dw_in.append(_wgrad(h2t, drest, tok_t, colblk(d), (d, 5 * d), pl.BlockSpec((d, d), lambda cc, s: (0, cc)),
                        (d, d), 5, "dw_in_rest", True))
    dw_in = _cols_to_shards(jnp.concatenate(dw_in, axis=1), N_CHIPS)
    shard_w = d // N_CHIPS
    dw_ab = _wgrad(o, dya, tok(a_w), colblk(shard_w), (a_w, d), pl.BlockSpec((a_w, shard_w), lambda cc, s: (0, cc)),
                   (a_w, shard_w), N_CHIPS, "dw_attn_branch")
    dw_ab = _cols_to_shards(dw_ab, N_CHIPS)
    row_out = pl.BlockSpec((None, shard_w, d), lambda cc, s: (cc, 0, 0))
    dw_cb = _wgrad(yb, dyc, colblk(shard_w), tok(d), (N_CHIPS, shard_w, d), row_out, (shard_w, d), N_CHIPS, "dw_conv_branch")
    dw_o = _wgrad(merged, dz, colblk(shard_w), tok(d), (N_CHIPS, shard_w, d), row_out, (shard_w, d), N_CHIPS, "dw_out")
    shard_grads = reduce_finish(sums_a, chips_a, names_a, "a", after=(dw_in, dw_o))
    grads_b = [dw_in, dw_ab, dw_cb, dw_o]
    sibling_b = pair_send(grads_b, "b", 8)

    dh2 = _in_proj_bwd(dq, dk, dv, drest, w_in_full, "in_proj_bwd")
    sums_b, chips_b = chip_send(grads_b, sibling_b, names_b, "b", 5, after=(dh2,))
    dx1, st2, dy1 = _norm_bwd(dh2, x1, mod2, dx2, z, 1.0, "norm2_bwd", after=tuple(sums_b), prev=(mod1, 0.5))
    dh1, dg1, du1, a1 = _ffn_bwd(dy1, g1, u1, wg1, wu1, wd1, "ffn1_bwd")
    dx0, st1 = _norm_bwd(dh1, xs, mod1, dx1, y1, 0.5, "norm1_bwd")
    grads_c = list(_ffn_wgrads(h1t, dg1, du1, a1, dy1, "ffn1"))
    sibling_c = pair_send(grads_c, "c", 9)
    shard_grads.update(reduce_finish(sums_b, chips_b, names_b, "b", after=tuple(grads_c)))

    dmod = jnp.concatenate([st1[0:3], st2[0:3], st3[0:3]], axis=0).reshape(1, 9 * d)
    loss_cols = jnp.zeros((1, HEAD_DIM), F32).at[0, 0].set(jnp.sum(loss_part))
    small = jnp.concatenate([dmod, st1[3:4], st2[3:4], st3[3:4], st_qk[0:1], st_qk[1:2],
                             st_conv[0:3].reshape(1, 3 * d), loss_cols], axis=1)
    n_small = small.shape[1]
    fold = -(-n_small // (8 * LANE_TILE)) * LANE_TILE
    folded = jnp.pad(small, ((0, 0), (0, 8 * fold - n_small))).reshape(8, fold)
    small_all = _allgather8(folded, "gather_small").reshape(8, 8 * fold)[:, :n_small]
    loss = (0.5 / d) * jnp.sum(small_all[:, -HEAD_DIM])
    small_all = small_all[:, :-HEAD_DIM]
    dmod_all = small_all[:, :9 * d]
    dmod_shard = lax.dynamic_slice(dmod_all, (0, chip * ada_cols), (8, ada_cols))
    g_w_ada, d_w_ada, nm_w_ada, nv_w_ada = _ada_bwd(c_all, dmod_shard, w_ada[0], m_w_ada[0], v_w_ada[0], "ada_bwd")

    vec_names = ["b_ada", "norm_ffn1", "norm_mix", "norm_ffn2", "q_norm", "k_norm"]
    vec_w = [b_ada, norm_ffn1, norm_mix, norm_ffn2, q_norm, k_norm]
    vec_m = [m_b_ada, m_norm_ffn1, m_norm_mix, m_norm_ffn2, m_q_norm, m_k_norm]
    vec_v = [v_b_ada, v_norm_ffn1, v_norm_mix, v_norm_ffn2, v_q_norm, v_k_norm]
    n_vec = sum(w.shape[1] for w in vec_w)
    cat = lambda arrs: jnp.concatenate(arrs, axis=1)
    vec_out = _small_update(small_all[:, :n_vec], cat(vec_w), cat(vec_m), cat(vec_v), "small_update")
    conv_parts = small_all[:, n_vec:].reshape(8, 3, N_CHIPS * conv_cols)
    conv_parts = lax.dynamic_slice(conv_parts, (0, 0, chip * conv_cols), (8, 3, conv_cols)).reshape(8, 3 * conv_cols)
    flat3 = lambda w: w[0].reshape(1, 3 * conv_cols)
    conv_out = _small_update(conv_parts, flat3(conv_w), flat3(m_conv_w), flat3(v_conv_w), "conv_update")

    res = {"w_ada": [t[None] for t in (g_w_ada, d_w_ada, nm_w_ada, nv_w_ada)],
           "conv_w": [t.reshape(1, 3, conv_cols) for t in conv_out]}
    off = 0
    for nm, w in zip(vec_names, vec_w):
        width = w.shape[1]
        res[nm] = [t[:, off:off + width] for t in vec_out]
        off += width
    big = {"ffn1_w_gate": (ffn1_w_gate, m_ffn1_w_gate, v_ffn1_w_gate), "ffn1_w_up": (ffn1_w_up, m_ffn1_w_up, v_ffn1_w_up),
           "ffn1_w_down": (ffn1_w_down, m_ffn1_w_down, v_ffn1_w_down), "w_in": (w_in, m_w_in, v_w_in),
           "w_attn_branch": (w_attn_branch, m_w_attn_branch, v_w_attn_branch),
           "w_conv_branch": (w_conv_branch, m_w_conv_branch, v_w_conv_branch), "w_out": (w_out, m_w_out, v_w_out),
           "ffn2_w_gate": (ffn2_w_gate, m_ffn2_w_gate, v_ffn2_w_gate), "ffn2_w_up": (ffn2_w_up, m_ffn2_w_up, v_ffn2_w_up),
           "ffn2_w_down": (ffn2_w_down, m_ffn2_w_down, v_ffn2_w_down)}
    def update(nm, after=()):
        w, m, v = big[nm]
        flip = w.shape[2] % LANE_TILE != 0 and w.shape[1] % LANE_TILE == 0
        turn = (lambda a: a.T) if flip else (lambda a: a)
        outs = _adamw(turn(w[0]), turn(shard_grads[nm]), turn(m[0]), turn(v[0]), "adamw_" + nm, after)
        res[nm] = [turn(t)[None] for t in outs]
        return outs[3]

    last = tuple(shard_grads[nm] for nm in names_b)
    for nm in names_a:
        last = (update(nm, last),)
    sums_c, chips_c = chip_send(grads_c, sibling_c, names_c, "c", 6, after=last)
    last = tuple(sums_c)
    for nm in names_b:
        last = (update(nm, last),)
    shard_grads.update(reduce_finish(sums_c, chips_c, names_c, "c", after=last))
    for nm in names_c:
        update(nm)

    order = ["w_ada", "b_ada", "norm_ffn1", "ffn1_w_gate", "ffn1_w_up", "ffn1_w_down", "norm_mix", "w_in", "q_norm",
             "k_norm", "conv_w", "w_attn_branch", "w_conv_branch", "w_out", "norm_ffn2", "ffn2_w_gate", "ffn2_w_up",
             "ffn2_w_down"]
    return (loss, dx0[None], *[res[nm][0] for nm in order], *[res[nm][1] for nm in order],
            *[res[nm][2] for nm in order], *[res[nm][3] for nm in order])
```

```python
import jax
import jax.numpy as jnp
from jax import lax
from jax.experimental import pallas as pl
from jax.experimental.pallas import tpu as pltpu
from jax.experimental.pallas import tpu_sc as plsc

F32 = jnp.float32
BF16 = jnp.bfloat16
MESH = pl.DeviceIdType.MESH
ANY = pl.BlockSpec(memory_space=pl.ANY)

NORM_EPS = 1e-6
LANE_TILE = 128
HEAD_DIM = 128
N_GROUPS = 3
HEADS = 4
DILATIONS = (1, 4, 16)
ATTN_BLOCK = 128
SLAB = ATTN_BLOCK * max(DILATIONS)
QKV = N_GROUPS * HEADS * HEAD_DIM
ATTN_SCALE = HEAD_DIM ** -0.5
NEG = -1e30
N_CHIPS = 4

ADAM_LR = 0.001
ADAM_B1 = 0.9
ADAM_B2 = 0.999
ADAM_EPS = 1e-08
ADAM_WD = 0.01
ADAM_STEP = 10

VMEM_LIMIT_BYTES = 56 * 1024 * 1024
TOKEN_TILE = 512
FFN_TILE = 1024
PROJ_TILE = 2048
WGRAD_TILE = 2048
IN_BLOCK = 512
MIX_TILE = 512
ACC_PIECES = 4
ADAMW_TILE_BYTES = 3 * 512 * 1024


def _params(n_axes=0):
    return pltpu.CompilerParams(
        dimension_semantics=("arbitrary",) * n_axes if n_axes else None,
        vmem_limit_bytes=VMEM_LIMIT_BYTES)


def _dot(a, b):
    return jnp.dot(a, b, preferred_element_type=F32)


def _dot_nt(a, b):
    return lax.dot_general(a, b, (((1,), (1,)), ((), ())), preferred_element_type=F32)


def _dot_tn(a, b):
    return lax.dot_general(a, b, (((0,), (0,)), ((), ())), preferred_element_type=F32)


def _sigmoid(x):
    return 0.5 * jnp.tanh(0.5 * x) + 0.5


def _place():
    return lax.axis_index("x"), lax.axis_index("y"), lax.axis_index("c")


def _ordered(body, n_in, after):
    if not after:
        return body
    return lambda *refs: body(*refs[:n_in], *refs[n_in + len(after):])


def _allgather8(block, name):
    m_per, n = block.shape

    def body(x_ref, out_ref, send_sems, recv_sems, local_sem):
        x, y, c = _place()
        me, sibling = (x, y, c), (x, y, 1 - c)
        chips = [(1 - x, y), (x, 1 - y), (1 - x, 1 - y)]

        def rows(px, py, pc):
            return out_ref.at[pl.ds((4 * px + 2 * py + pc) * m_per, m_per), :]

        def copy(k, blk, to, src=None):
            return pltpu.make_async_remote_copy(
                src_ref=rows(*blk) if src is None else src, dst_ref=rows(*blk),
                send_sem=send_sems.at[k], recv_sem=recv_sems.at[k],
                device_id=to, device_id_type=MESH)

        mine = pltpu.make_async_copy(x_ref, rows(*me), local_sem)
        mine.start()
        first = [copy(0, me, sibling, src=x_ref)]
        first += [copy(1 + j, me, (*chip, c), src=x_ref) for j, chip in enumerate(chips)]
        for cp in first:
            cp.start()
        passed = [copy(4 + j, (*chip, c), sibling) for j, chip in enumerate(chips)]
        for j, chip in enumerate(chips):
            copy(1 + j, (*chip, c), me).wait_recv()
            passed[j].start()
        copy(0, sibling, me).wait_recv()
        for j, chip in enumerate(chips):
            copy(4 + j, (*chip, 1 - c), me).wait_recv()
        for cp in first + passed:
            cp.wait_send()
        mine.wait()

    return pl.pallas_call(
        body, name=name,
        out_shape=jax.ShapeDtypeStruct((8 * m_per, n), block.dtype),
        in_specs=[pl.BlockSpec(memory_space=pltpu.VMEM)],
        out_specs=pl.BlockSpec(memory_space=pltpu.VMEM),
        scratch_shapes=[pltpu.SemaphoreType.DMA((7,)), pltpu.SemaphoreType.DMA((7,)),
                        pltpu.SemaphoreType.DMA],
        compiler_params=_params(),
    )(block)


def _handshake(peers):
    barrier = pltpu.get_barrier_semaphore()
    for peer in peers:
        pl.semaphore_signal(barrier, inc=1, device_id=peer, device_id_type=MESH)
    pl.semaphore_wait(barrier, len(peers))


def _gather_weights(shards, by_cols, name, collective_id, after=()):
    n_arr = len(shards)

    def body(*refs):
        srcs, outs = refs[:n_arr], refs[n_arr + len(after):2 * n_arr + len(after)]
        send_sems, recv_sems, local_sems = refs[2 * n_arr + len(after):]
        x, y, c = _place()
        me_dev, sibling = (x, y, c), (x, y, 1 - c)
        chips = [(1 - x, y), (x, 1 - y), (1 - x, 1 - y)]
        me = 2 * x + y
        _handshake([sibling] + [(*chip, c) for chip in chips])

        def place(k, chip_idx, rows):
            if by_cols[k]:
                width = srcs[k].shape[1]
                return outs[k].at[rows, pl.ds(pl.multiple_of(chip_idx * width, 128), width)]
            return outs[k].at[chip_idx, rows]

        def copy(k, slot, chip_idx, half_sel, to, from_shard=False):
            half = srcs[k].shape[0] // 2
            rows = pl.ds(half_sel * half, half)
            dst = place(k, chip_idx, rows)
            return pltpu.make_async_remote_copy(
                src_ref=srcs[k].at[rows] if from_shard else dst, dst_ref=dst,
                send_sem=send_sems.at[6 * k + slot], recv_sem=recv_sems.at[6 * k + slot],
                device_id=to, device_id_type=MESH)

        own = [pltpu.make_async_copy(srcs[k], place(k, me, pl.ds(0, srcs[k].shape[0])), local_sems.at[k])
               for k in range(n_arr)]
        for cp in own:
            cp.start()
        sent = []
        for k in range(n_arr):
            for j, chip in enumerate(chips):
                sent.append(copy(k, j, me, c, (*chip, c), from_shard=True))
                sent[-1].start()
        for k in range(n_arr):
            for j, chip in enumerate(chips):
                chip_idx = 2 * chip[0] + chip[1]
                copy(k, j, chip_idx, c, me_dev).wait_recv()
                sent.append(copy(k, 3 + j, chip_idx, c, sibling))
                sent[-1].start()
        for k in range(n_arr):
            for j, chip in enumerate(chips):
                copy(k, 3 + j, 2 * chip[0] + chip[1], 1 - c, me_dev).wait_recv()
        for cp in sent:
            cp.wait_send()
        for cp in own:
            cp.wait()

    def gathered(k):
        r, cols = shards[k].shape
        return (r, N_CHIPS * cols) if by_cols[k] else (N_CHIPS, r, cols)

    return pl.kernel(
        body, name=name,
        out_type=[jax.ShapeDtypeStruct(gathered(k), shards[k].dtype) for k in range(n_arr)],
        mesh=plsc.ScalarSubcoreMesh(axis_name="sequencer", num_cores=1),
        scratch_types=[pltpu.SemaphoreType.DMA((6 * n_arr,)), pltpu.SemaphoreType.DMA((6 * n_arr,)),
                       pltpu.SemaphoreType.DMA((n_arr,))],
        compiler_params=pltpu.CompilerParams(collective_id=collective_id),
    )(*shards, *after)


def _rs_pair_exchange(grads, name, collective_id):
    n_arr = len(grads)

    def body(*refs):
        srcs, outs = refs[:n_arr], refs[n_arr:2 * n_arr]
        send_sems, recv_sems = refs[2 * n_arr:]
        x, y, c = _place()
        _handshake([(x, y, 1 - c)])
        cps = []
        for k in range(n_arr):
            half = srcs[k].shape[1] // 2
            cps.append(pltpu.make_async_remote_copy(
                src_ref=srcs[k].at[:, pl.ds((1 - c) * half, half)], dst_ref=outs[k],
                send_sem=send_sems.at[k], recv_sem=recv_sems.at[k],
                device_id=(x, y, 1 - c), device_id_type=MESH))
            cps[-1].start()
        for cp in cps:
            cp.wait_recv()
        for cp in cps:
            cp.wait_send()

    return pl.kernel(
        body, name=name,
        out_type=[jax.ShapeDtypeStruct((g.shape[0], g.shape[1] // 2, g.shape[2]), g.dtype) for g in grads],
        mesh=plsc.ScalarSubcoreMesh(axis_name="sequencer", num_cores=1),
        scratch_types=[pltpu.SemaphoreType.DMA((n_arr,)), pltpu.SemaphoreType.DMA((n_arr,))],
        compiler_params=pltpu.CompilerParams(collective_id=collective_id),
    )(*grads)


def _rs_chip_exchange(sums, name, collective_id):
    n_arr = len(sums)

    def body(*refs):
        srcs, outs = refs[:n_arr], refs[n_arr:2 * n_arr]
        send_sems, recv_sems = refs[2 * n_arr:]
        x, y, c = _place()
        chips = [(1 - x, y), (x, 1 - y), (1 - x, 1 - y)]
        _handshake([(*chip, c) for chip in chips])
        cps = []
        for k in range(n_arr):
            for j, chip in enumerate(chips):
                cps.append(pltpu.make_async_remote_copy(
                    src_ref=srcs[k].at[2 * chip[0] + chip[1]], dst_ref=outs[k].at[j],
                    send_sem=send_sems.at[3 * k + j], recv_sem=recv_sems.at[3 * k + j],
                    device_id=(*chip, c), device_id_type=MESH))
                cps[-1].start()
        for cp in cps:
            cp.wait_recv()
        for cp in cps:
            cp.wait_send()

    return pl.kernel(
        body, name=name,
        out_type=[jax.ShapeDtypeStruct((3,) + s.shape[1:], s.dtype) for s in sums],
        mesh=plsc.ScalarSubcoreMesh(axis_name="sequencer", num_cores=1),
        scratch_types=[pltpu.SemaphoreType.DMA((3 * n_arr,)), pltpu.SemaphoreType.DMA((3 * n_arr,))],
        compiler_params=pltpu.CompilerParams(collective_id=collective_id),
    )(*sums)


def _rs_share(totals, name):
    n_arr = len(totals)

    def body(*refs):
        outs = refs[n_arr:2 * n_arr]
        send_sems, recv_sems = refs[2 * n_arr:]
        x, y, c = _place()

        def half_rows(k, sel):
            return outs[k].at[sel]

        cps = []
        for k in range(n_arr):
            cps.append(pltpu.make_async_remote_copy(
                src_ref=half_rows(k, c), dst_ref=half_rows(k, c), send_sem=send_sems.at[k], recv_sem=recv_sems.at[k],
                device_id=(x, y, 1 - c), device_id_type=MESH))
            cps[-1].start()
        for k in range(n_arr):
            pltpu.make_async_remote_copy(
                src_ref=half_rows(k, c), dst_ref=half_rows(k, 1 - c), send_sem=send_sems.at[k],
                recv_sem=recv_sems.at[k], device_id=(x, y, 1 - c), device_id_type=MESH).wait_recv()
        for cp in cps:
            cp.wait_send()

    shared = pl.pallas_call(
        body, name=name,
        out_shape=[jax.ShapeDtypeStruct(t.shape, t.dtype) for t in totals],
        in_specs=[ANY] * n_arr, out_specs=[ANY] * n_arr,
        input_output_aliases={k: k for k in range(n_arr)},
        scratch_shapes=[pltpu.SemaphoreType.DMA((n_arr,)), pltpu.SemaphoreType.DMA((n_arr,))],
        compiler_params=_params(),
    )(*totals)
    return [t.reshape(2 * t.shape[1], t.shape[2]) for t in shared]


def _pair_add(grad, recv, c_idx, name, after=()):
    n, r, cols = grad.shape
    half = r // 2
    rows = half // 2

    def body(_, g_ref, r_ref, o_ref):
        o_ref[...] = (g_ref[...].astype(F32) + r_ref[...].astype(F32)).astype(o_ref.dtype)

    return pl.pallas_call(
        _ordered(body, 3, after), name=name,
        grid_spec=pltpu.PrefetchScalarGridSpec(
            num_scalar_prefetch=1, grid=(n, 2),
            in_specs=[pl.BlockSpec((None, None, rows, cols), lambda s, i, ci: (s, ci[0], i, 0)),
                      pl.BlockSpec((None, rows, cols), lambda s, i, ci: (s, i, 0))] + [ANY] * len(after),
            out_specs=pl.BlockSpec((None, rows, cols), lambda s, i, ci: (s, i, 0))),
        out_shape=jax.ShapeDtypeStruct((n, half, cols), BF16),
        compiler_params=_params(2),
    )(c_idx, grad.reshape(n, 2, half, cols), recv, *after)


def _chip_add(sums, recv, chip_and_core, name, after=()):
    _, half, cols = sums.shape
    rows = half // 2

    def body(_, s_ref, r0_ref, r1_ref, r2_ref, o_ref):
        o_ref[...] = ((s_ref[...].astype(F32) + r0_ref[...].astype(F32))
                      + r1_ref[...].astype(F32)) + r2_ref[...].astype(F32)

    def recv_spec(j):
        return pl.BlockSpec((None, rows, cols), lambda i, ci: (j, i, 0))

    return pl.pallas_call(
        _ordered(body, 5, after), name=name,
        grid_spec=pltpu.PrefetchScalarGridSpec(
            num_scalar_prefetch=1, grid=(2,),
            in_specs=[pl.BlockSpec((None, rows, cols), lambda i, ci: (ci[0], i, 0)),
                      recv_spec(0), recv_spec(1), recv_spec(2)] + [ANY] * len(after),
            out_specs=pl.BlockSpec((None, rows, cols), lambda i, ci: (ci[1], i, 0))),
        out_shape=jax.ShapeDtypeStruct((2, half, cols), F32),
        compiler_params=_params(1),
    )(chip_and_core, sums, recv, recv, recv, *after)


def _rms(x):
    return lax.rsqrt(jnp.mean(x * x, axis=-1, keepdims=True) + NORM_EPS)


def _norm_mod(x, mod, name, prev=None):
    s_len, d = x.shape
    tm = TOKEN_TILE

    def body(*refs):
        if prev is None:
            x_ref, mod_ref, h_ref, ht_ref = refs
            xv = x_ref[...]
        else:
            x_ref, y_ref, modp_ref, mod_ref, xo_ref, h_ref, ht_ref = refs
            xv = x_ref[...] + prev[2] * modp_ref[2:3, :] * y_ref[...]
            xo_ref[...] = xv
        n = (xv * _rms(xv)) * mod_ref[3:4, :]
        h = n * (1.0 + mod_ref[1:2, :]) + mod_ref[0:1, :]
        h_ref[...] = h.astype(BF16)
        ht_ref[...] = h.T.astype(BF16)

    tile = pl.BlockSpec((tm, d), lambda i: (i, 0))
    small = pl.BlockSpec((8, d), lambda i: (0, 0))
    h_specs = [tile, pl.BlockSpec((d, tm), lambda i: (0, i))]
    h_shapes = [jax.ShapeDtypeStruct((s_len, d), BF16), jax.ShapeDtypeStruct((d, s_len), BF16)]
    if prev is None:
        return pl.pallas_call(
            body, name=name, grid=(s_len // tm,), in_specs=[tile, small], out_specs=h_specs, out_shape=h_shapes,
            compiler_params=_params(1))(x, mod)
    return pl.pallas_call(
        body, name=name, grid=(s_len // tm,), in_specs=[tile, tile, small, small],
        out_specs=[tile] + h_specs, out_shape=[jax.ShapeDtypeStruct((s_len, d), F32)] + h_shapes,
        compiler_params=_params(1))(x, prev[0], prev[1], mod)


def _norm_bwd(dh, x, mod, dxo, y_raw, coef, name, after=(), prev=None, dxo_is_target=False):
    s_len, d = x.shape
    tm = TOKEN_TILE

    def body(*refs):
        if prev is None:
            dh_ref, x_ref, mod_ref, dxo_ref, y_ref, dx_ref, st_ref = refs
        else:
            dh_ref, x_ref, mod_ref, dxo_ref, y_ref, modp_ref, dx_ref, st_ref, dyp_ref = refs

        @pl.when(pl.program_id(0) == 0)
        def _():
            st_ref[...] = jnp.zeros_like(st_ref)

        xv, dhv, dxov = x_ref[...], dh_ref[...], dxo_ref[...]
        if dxo_is_target:
            dxov = ((xv + (coef * mod_ref[2:3, :]) * y_ref[...].astype(F32)) - dxov) * (1.0 / d)
        r = _rms(xv)
        xh = xv * r
        gain, scale = mod_ref[3:4, :], mod_ref[1:2, :]
        dn = dhv * (1.0 + scale)
        dxh = dn * gain
        dx = dxov + r * (dxh - xh * jnp.mean(dxh * xh, axis=-1, keepdims=True))
        dx_ref[...] = dx
        if prev is not None:
            dyp_ref[...] = (prev[1] * modp_ref[2:3, :] * dx).astype(BF16)
        st_ref[0:1, :] += jnp.sum(dhv, axis=0, keepdims=True)
        st_ref[1:2, :] += jnp.sum(dhv * (xh * gain), axis=0, keepdims=True)
        st_ref[2:3, :] += coef * jnp.sum(y_ref[...].astype(F32) * dxov, axis=0, keepdims=True)
        st_ref[3:4, :] += jnp.sum(dn * xh, axis=0, keepdims=True)

    tile = pl.BlockSpec((tm, d), lambda i: (i, 0))
    small = pl.BlockSpec((8, d), lambda i: (0, 0))
    operands = [dh, x, mod, dxo, y_raw] + ([] if prev is None else [prev[0]])
    in_specs = [tile, tile, small, tile, tile] + ([] if prev is None else [small])
    out_specs = [tile, small] + ([] if prev is None else [tile])
    out_shape = [jax.ShapeDtypeStruct((s_len, d), F32), jax.ShapeDtypeStruct((8, d), F32)]
    if prev is not None:
        out_shape.append(jax.ShapeDtypeStruct((s_len, d), BF16))
    return pl.pallas_call(
        _ordered(body, len(operands), after), name=name, grid=(s_len // tm,),
        in_specs=in_specs + [ANY] * len(after), out_specs=out_specs, out_shape=out_shape,
        compiler_params=_params(1),
    )(*operands, *after)


def _loss_grad(x, y, mod, target, name):
    s_len, d = x.shape
    tm = TOKEN_TILE

    def body(x_ref, y_ref, mod_ref, t_ref, dy_ref, part_ref):
        @pl.when(pl.program_id(0) == 0)
        def _():
            part_ref[...] = jnp.zeros_like(part_ref)

        half_gate = 0.5 * mod_ref[2:3, :]
        err = (x_ref[...] + half_gate * y_ref[...]) - t_ref[...]
        do = err * (1.0 / d)
        dy_ref[...] = (half_gate * do).astype(BF16)
        sq = err * err
        part_ref[...] += jnp.sum(sq.reshape(tm // 8, 8, d), axis=0)

    tile = pl.BlockSpec((tm, d), lambda i: (i, 0))
    small = pl.BlockSpec((8, d), lambda i: (0, 0))
    return pl.pallas_call(
        body, name=name, grid=(s_len // tm,),
        in_specs=[tile, tile, small, tile],
        out_specs=[tile, small],
        out_shape=[jax.ShapeDtypeStruct((s_len, d), BF16), jax.ShapeDtypeStruct((8, d), F32)],
        compiler_params=_params(1),
    )(x, y, mod, target)


def _adamw_math(w, g, m, v):
    m = ADAM_B1 * m + (1.0 - ADAM_B1) * g
    v = ADAM_B2 * v + (1.0 - ADAM_B2) * (g * g)
    m_hat = m / (1.0 - ADAM_B1 ** ADAM_STEP)
    v_hat = v / (1.0 - ADAM_B2 ** ADAM_STEP)
    delta = -ADAM_LR * (m_hat / (jnp.sqrt(v_hat) + ADAM_EPS) + ADAM_WD * w)
    return delta, m, v


def _adamw(w, g, m, v, name, after=()):
    r, cols = w.shape
    tr = max([t for t in (r // k for k in (1, 2, 4, 8, 16)) if t % 8 == 0 and r % t == 0
              and t * cols * 4 <= ADAMW_TILE_BYTES] or [r])

    def body(w_ref, g_ref, m_ref, v_ref, go_ref, d_ref, nm_ref, nv_ref):
        gv = g_ref[...]
        go_ref[...] = gv
        d_ref[...], nm_ref[...], nv_ref[...] = _adamw_math(w_ref[...], gv, m_ref[...], v_ref[...])

    tile = pl.BlockSpec((tr, cols), lambda i: (i, 0))
    shape = jax.ShapeDtypeStruct((r, cols), F32)
    return pl.pallas_call(
        _ordered(body, 4, after), name=name, grid=(r // tr,),
        in_specs=[tile] * 4 + [ANY] * len(after), out_specs=[tile] * 4, out_shape=[shape] * 4,
        compiler_params=_params(1),
    )(w, g, m, v, *after)


def _in_parts(tm, n_qkv, n_rest):
    def part(lo, n_blk):
        return pl.BlockSpec((tm, IN_BLOCK), lambda i, j: (i, jnp.clip(j - lo, 0, n_blk - 1)))
    return [part(0, n_qkv), part(n_qkv, n_qkv), part(2 * n_qkv, n_qkv), part(3 * n_qkv, n_rest)]


def _pick_part(j, n_qkv, refs, fn):
    bounds = [0, n_qkv, 2 * n_qkv, 3 * n_qkv]
    for p, ref in enumerate(refs):
        inside = j >= bounds[p]
        if p + 1 < len(refs):
            inside = inside & (j < bounds[p + 1])
        pl.when(inside)(lambda ref=ref: fn(ref))


def _rows(base, count, stride):
    return pl.ds(base, count) if stride == 1 else pl.ds(base, count, stride=stride)


REORDER_STRIDE = 4


def _reorder_plan(dil, parts=1):
    inner = min(dil, REORDER_STRIDE)
    return inner, dil // inner, SLAB // parts // inner, SLAB // dil


def _to_residue_order(dst, src, dil, tmp, part=0, parts=1):
    inner, outer, big, seg = _reorder_plan(dil, parts)
    piece = seg // parts
    if outer == 1:
        for r in range(dil):
            dst[pl.ds(r * seg + part * piece, piece), :] = src[_rows(r, piece, dil), :].astype(dst.dtype)
        return
    for b in range(inner):
        tmp[pl.ds(b * big, big), :] = src[_rows(b, big, inner), :]
    for a in range(outer):
        for b in range(inner):
            dst[pl.ds((inner * a + b) * seg + part * piece, piece), :] = (
                tmp[_rows(b * big + a, piece, outer), :].astype(dst.dtype))


def _to_token_order(dst, src, dil, tmp):
    inner, outer, big, seg = _reorder_plan(dil)
    if outer == 1:
        for r in range(dil):
            dst[_rows(r, seg, dil), :] = src[pl.ds(r * seg, seg), :]
        return
    for a in range(outer):
        for b in range(inner):
            tmp[_rows(b * big + a, seg, outer), :] = src[pl.ds((inner * a + b) * seg, seg), :]
    for b in range(inner):
        dst[_rows(b, big, inner), :] = tmp[pl.ds(b * big, big), :]


def _in_proj(h, w, q_norm, k_norm, name):
    s_len, d = h.shape
    tm = PROJ_TILE
    assert tm == SLAB and IN_BLOCK == HEADS * HEAD_DIM
    steps = w.shape[1] // IN_BLOCK
    n_qkv = 3 * QKV // IN_BLOCK
    parts = 4
    rows = [pl.ds(p * (tm // parts), tm // parts) for p in range(parts)]

    gains = jnp.concatenate([q_norm, k_norm, jnp.ones((6, HEAD_DIM), F32)], axis=0)

    def body(h_ref, w_ref, gains_ref, qkv_ref, rest_ref, hat_ref, tok_s, tmp_s):
        j = pl.program_id(1)
        sect = j // N_GROUPS
        multiply = lambda p: _dot(h_ref[rows[p], :], w_ref[...])

        def emit(gi):
            dil = DILATIONS[gi]
            res = [multiply(p) for p in range(parts)]
            gain = gains_ref[pl.ds(sect, 1), :]
            plain = sect == 2
            for p in range(parts):
                qkv_ref[rows[p], :] = res[p]
                for hh in range(HEADS):
                    cols = slice(hh * HEAD_DIM, (hh + 1) * HEAD_DIM)
                    x = res[p][:, cols]
                    tok_s[...] = (x * jnp.where(plain, 1.0, _rms(x))) * gain
                    _to_residue_order(hat_ref.at[:, cols], tok_s, dil, tmp_s, p, parts)

        for gi in range(N_GROUPS):
            pl.when((j < n_qkv) & (j % N_GROUPS == gi))(lambda gi=gi: emit(gi))

        @pl.when(j >= n_qkv)
        def _():
            for p in range(parts):
                rest_ref[rows[p], :] = multiply(p).astype(BF16)

    qkv_blk = pl.BlockSpec((tm, IN_BLOCK), lambda i, j: (i, jnp.minimum(j, n_qkv - 1)))
    return pl.pallas_call(
        body, name=name, grid=(s_len // tm, steps),
        in_specs=[pl.BlockSpec((tm, d), lambda i, j: (i, 0)), pl.BlockSpec((d, IN_BLOCK), lambda i, j: (0, j)),
                  pl.BlockSpec((8, HEAD_DIM), lambda i, j: (0, 0))],
        out_specs=[qkv_blk, pl.BlockSpec((tm, IN_BLOCK), lambda i, j: (i, jnp.maximum(j - n_qkv, 0))), qkv_blk],
        out_shape=[jax.ShapeDtypeStruct((s_len, 3 * QKV), F32),
                   jax.ShapeDtypeStruct((s_len, w.shape[1] - 3 * QKV), BF16),
                   jax.ShapeDtypeStruct((s_len, 3 * QKV), BF16)],
        scratch_shapes=[pltpu.VMEM((tm // parts, HEAD_DIM), F32)] * 2,
        compiler_params=_params(2),
    )(h, w, gains)


def _in_proj_bwd(dq, dk, dv, drest, w, name, after=()):
    s_len = dq.shape[0]
    d = w.shape[0]
    tm = TOKEN_TILE

    def body(dq_ref, dk_ref, dv_ref, dr_ref, w_ref, o_ref):
        total, lo = None, 0
        for a_ref in (dq_ref, dk_ref, dv_ref, dr_ref):
            width = a_ref.shape[1]
            part = _dot_nt(a_ref[...], w_ref[:, lo:lo + width])
            total = part if total is None else total + part
            lo += width
        o_ref[...] = total

    tile = lambda width: pl.BlockSpec((tm, width), lambda i: (i, 0))
    whole_w = pl.BlockSpec(w.shape, lambda i: (0, 0), pipeline_mode=pl.Buffered(1))
    return pl.pallas_call(
        _ordered(body, 5, after), name=name, grid=(s_len // tm,),
        in_specs=[tile(QKV)] * 3 + [tile(drest.shape[1]), whole_w] + [ANY] * len(after),
        out_specs=tile(d),
        out_shape=jax.ShapeDtypeStruct((s_len, d), F32),
        compiler_params=_params(1),
    )(dq, dk, dv, drest, w, *after)


def _wgrad(x, y, x_spec, y_spec, out_shape, out_spec, acc_shape, n_chunks, name, x_transposed=False, after=()):
    s_len = y.shape[-2]
    ts = WGRAD_TILE
    steps = s_len // ts

    def body(x_ref, y_ref, o_ref, acc_ref):
        s = pl.program_id(1)

        @pl.when(s == 0)
        def _():
            acc_ref[...] = jnp.zeros_like(acc_ref)

        if x_transposed:
            n_rows = acc_shape[0]
            rows = [pl.ds(p * (n_rows // ACC_PIECES), n_rows // ACC_PIECES) for p in range(ACC_PIECES)]
            products = [_dot(x_ref[r, :], y_ref[...]) for r in rows]
            for r, product in zip(rows, products):
                acc_ref[r, :] += product
        else:
            cols = _pieces(acc_shape[1])
            products = [_dot_tn(x_ref[...], y_ref[:, c]) for c in cols]
            for c, product in zip(cols, products):
                acc_ref[:, c] += product

        @pl.when(s == steps - 1)
        def _():
            o_ref[...] = acc_ref[...].astype(o_ref.dtype)

    return pl.pallas_call(
        _ordered(body, 2, after), name=name, grid=(n_chunks, steps),
        in_specs=[x_spec(ts), y_spec(ts)] + [ANY] * len(after), out_specs=out_spec,
        out_shape=jax.ShapeDtypeStruct(out_shape, BF16),
        scratch_shapes=[pltpu.VMEM(acc_shape, F32)],
        compiler_params=_params(2),
    )(x, y, *after)


def _pieces(width, piece=256):
    return [slice(a, min(a + piece, width)) for a in range(0, width, piece)]


def _ffn_fwd(h, w_gate, w_up, w_down, name):
    s_len, d = h.shape
    n_chunks, _, fs = w_gate.shape
    tm = FFN_TILE

    def body(h_ref, wg_ref, wu_ref, wd_ref, g_ref, u_ref, y_ref):
        j = pl.program_id(1)
        hv = h_ref[...]
        pieces = _pieces(fs)
        first = lambda cols: (_dot(hv, wg_ref[:, cols]), _dot(hv, wu_ref[:, cols]))
        total = None
        ahead = first(pieces[0])
        for k, cols in enumerate(pieces):
            g, u = ahead
            if k + 1 < len(pieces):
                ahead = first(pieces[k + 1])
            g_ref[:, cols] = g.astype(BF16)
            u_ref[:, cols] = u.astype(BF16)
            act = (g * _sigmoid(g)) * u
            part = _dot(act.astype(BF16), wd_ref[cols, :])
            total = part if total is None else total + part

        @pl.when(j == 0)
        def _():
            y_ref[...] = total

        @pl.when(j > 0)
        def _():
            y_ref[...] += total

    tile = pl.BlockSpec((tm, d), lambda i, j: (i, 0))
    hid = pl.BlockSpec((None, tm, fs), lambda i, j: (j, i, 0))
    w_in_spec = pl.BlockSpec((None, d, fs), lambda i, j: (j, 0, 0))
    hid_shape = jax.ShapeDtypeStruct((n_chunks, s_len, fs), BF16)
    return pl.pallas_call(
        body, name=name, grid=(s_len // tm, n_chunks),
        in_specs=[tile, w_in_spec, w_in_spec, pl.BlockSpec((None, fs, d), lambda i, j: (j, 0, 0))],
        out_specs=[hid, hid, tile],
        out_shape=[hid_shape, hid_shape, jax.ShapeDtypeStruct((s_len, d), F32)],
        compiler_params=_params(2),
    )(h, w_gate, w_up, w_down)


def _ffn_bwd(dy, g_pre, u_pre, w_gate, w_up, w_down, name):
    s_len, d = dy.shape
    n_chunks, _, fs = w_gate.shape
    tm = FFN_TILE

    def body(dy_ref, g_ref, u_ref, wg_ref, wu_ref, wd_ref, dh_ref, dg_ref, du_ref, a_ref):
        j = pl.program_id(1)
        dyv = dy_ref[...]
        pieces = _pieces(fs)
        first = lambda cols: _dot_nt(dyv, wd_ref[cols, :])
        total = None
        ahead = first(pieces[0])
        for k, cols in enumerate(pieces):
            da = ahead
            if k + 1 < len(pieces):
                ahead = first(pieces[k + 1])
            g = g_ref[:, cols].astype(F32)
            u = u_ref[:, cols].astype(F32)
            sg = _sigmoid(g)
            silu = g * sg
            dg = (da * u * (sg * (1.0 + g * (1.0 - sg)))).astype(BF16)
            du = (da * silu).astype(BF16)
            dg_ref[:, cols] = dg
            du_ref[:, cols] = du
            a_ref[:, cols] = (silu * u).astype(BF16)
            part = _dot_nt(dg, wg_ref[:, cols]) + _dot_nt(du, wu_ref[:, cols])
            total = part if total is None else total + part

        @pl.when(j == 0)
        def _():
            dh_ref[...] = total

        @pl.when(j > 0)
        def _():
            dh_ref[...] += total

    tile = pl.BlockSpec((tm, d), lambda i, j: (i, 0))
    hid = pl.BlockSpec((None, tm, fs), lambda i, j: (j, i, 0))
    w_in_spec = pl.BlockSpec((None, d, fs), lambda i, j: (j, 0, 0))
    hid_shape = jax.ShapeDtypeStruct((n_chunks, s_len, fs), BF16)
    return pl.pallas_call(
        body, name=name, grid=(s_len // tm, n_chunks),
        in_specs=[tile, hid, hid, w_in_spec, w_in_spec, pl.BlockSpec((None, fs, d), lambda i, j: (j, 0, 0))],
        out_specs=[tile, hid, hid, hid],
        out_shape=[jax.ShapeDtypeStruct((s_len, d), F32), hid_shape, hid_shape, hid_shape],
        compiler_params=_params(2),
    )(dy, g_pre, u_pre, w_gate, w_up, w_down)


def _ffn_wgrads(ht, dg, du, act, dy, tag, after=()):
    n_chunks, s_len, fs = dg.shape
    d = ht.shape[0]
    tok = lambda ts: pl.BlockSpec((ts, d), lambda c, s: (s, 0))
    tok_t = lambda ts: pl.BlockSpec((d, ts), lambda c, s: (0, s))
    hid = lambda ts: pl.BlockSpec((None, ts, fs), lambda c, s: (c, s, 0))
    d_up = pl.BlockSpec((None, d, fs), lambda c, s: (c, 0, 0))
    d_down = pl.BlockSpec((None, fs, d), lambda c, s: (c, 0, 0))
    dwg = _wgrad(ht, dg, tok_t, hid, (n_chunks, d, fs), d_up, (d, fs), n_chunks, tag + "_dwg", True, after)
    dwu = _wgrad(ht, du, tok_t, hid, (n_chunks, d, fs), d_up, (d, fs), n_chunks, tag + "_dwu", True, after)
    dwd = _wgrad(act, dy, hid, tok, (n_chunks, fs, d), d_down, (fs, d), n_chunks, tag + "_dwd", False, after)
    return dwg, dwu, dwd


def _band_bias():
    qi = lax.broadcasted_iota(jnp.int32, (ATTN_BLOCK, 2 * ATTN_BLOCK), 0)
    kj = lax.broadcasted_iota(jnp.int32, (ATTN_BLOCK, 2 * ATTN_BLOCK), 1)
    band = (kj >= qi) & (kj <= qi + ATTN_BLOCK)
    return jnp.where(band, 0.0, NEG), jnp.where(band & (kj >= ATTN_BLOCK), 0.0, NEG)


def _qkv_specs(slab_of, sections):
    def spec(sect, back):
        return pl.BlockSpec((SLAB, HEAD_DIM),
                            lambda h, s, g: (jnp.maximum(slab_of(s) - back, 0), (sect * N_GROUPS + g) * HEADS + h))
    return [spec(sect, back) for sect, back in sections]


HAT_BLOCKS = [(0, 0), (1, 0), (2, 0), (1, 1), (2, 1)]


def _stage_keys(k_ref, v_ref, kp_ref, vp_ref, kbuf, vbuf, dil, n):
    run = SLAB // dil
    for r in range(dil):
        own, before = pl.ds(r * run, run), pl.ds(2 * r * run, run)
        kbuf[pl.ds((2 * r + 1) * run, run), :] = k_ref[own, :]
        vbuf[pl.ds((2 * r + 1) * run, run), :] = v_ref[own, :]

        @pl.when(n > 0)
        def _():
            kbuf[before, :] = kp_ref[own, :]
            vbuf[before, :] = vp_ref[own, :]

        @pl.when(n == 0)
        def _():
            kbuf[before, :] = jnp.zeros((run, HEAD_DIM), BF16)
            vbuf[before, :] = jnp.zeros((run, HEAD_DIM), BF16)


def _for_each_tile(dil, n, first_fn, rest_fn):
    run = SLAB // dil
    bias, first_bias = _band_bias()
    tiles = []
    for jj in range(run // ATTN_BLOCK):
        start = jj * ATTN_BLOCK
        tile_bias = jnp.where(n == 0, first_bias, bias) if jj == 0 else bias
        for r in range(dil):
            tiles.append((pl.ds(r * run + start, ATTN_BLOCK),
                          pl.ds((2 * r + 1) * run - ATTN_BLOCK + start, 2 * ATTN_BLOCK), tile_bias))
    ahead = first_fn(*tiles[0])
    for t, tile in enumerate(tiles):
        begun = ahead
        if t + 1 < len(tiles):
            ahead = first_fn(*tiles[t + 1])
        rest_fn(*tile, begun)


def _attn_fwd(hat, name):
    s_len = hat.shape[0]
    e = HEAD_DIM
    n_slabs = s_len // SLAB

    def body(q_ref, k_ref, v_ref, kp_ref, vp_ref, o_ref, lse_ref, kbuf, vbuf, m_s, l_s, acc_s, m_p, l_p, acc_p, tmp_s):
        n, grp = pl.program_id(1), pl.program_id(2)

        def run(gi, dil):
            _stage_keys(k_ref, v_ref, kp_ref, vp_ref, kbuf, vbuf, dil, n)

            def scores(q_rows, kv_rows, bias):
                return _dot_nt(q_ref[q_rows, :], kbuf[kv_rows, :])

            def rest(q_rows, kv_rows, bias, qk):
                s = qk * ATTN_SCALE + bias
                m = jnp.max(s, axis=-1, keepdims=True)
                p = jnp.exp(s - m)
                m_p[q_rows, :] = jnp.broadcast_to(m, (ATTN_BLOCK, e))
                l_p[q_rows, :] = jnp.broadcast_to(jnp.sum(p, axis=-1, keepdims=True), (ATTN_BLOCK, e))
                acc_p[q_rows, :] = _dot(p.astype(BF16), vbuf[kv_rows, :])

            _for_each_tile(dil, n, scores, rest)
            _to_token_order(m_s.at[gi], m_p, dil, tmp_s)
            _to_token_order(l_s.at[gi], l_p, dil, tmp_s)
            _to_token_order(acc_s.at[gi], acc_p, dil, tmp_s)

        for gi, dil in enumerate(DILATIONS):
            pl.when(grp == gi)(lambda gi=gi, dil=dil: run(gi, dil))

        @pl.when(grp == N_GROUPS - 1)
        def _():
            m_all = jnp.maximum(jnp.maximum(m_s[0], m_s[1]), m_s[2])
            den = jnp.zeros((SLAB, e), F32)
            num = jnp.zeros((SLAB, e), F32)
            for gi in range(N_GROUPS):
                w = jnp.exp(m_s[gi] - m_all)
                den += l_s[gi] * w
                num += acc_s[gi] * w
            o_ref[...] = (num / den).astype(BF16)
            lse_ref[...] = m_all + jnp.log(den)

    out = pl.BlockSpec((SLAB, e), lambda h, n, g: (n, h))
    return pl.pallas_call(
        body, name=name, grid=(HEADS, n_slabs, N_GROUPS),
        in_specs=_qkv_specs(lambda n: n, HAT_BLOCKS),
        out_specs=[out, out],
        out_shape=[jax.ShapeDtypeStruct((s_len, HEADS * e), BF16), jax.ShapeDtypeStruct((s_len, HEADS * e), F32)],
        scratch_shapes=[pltpu.VMEM((2 * SLAB, e), BF16), pltpu.VMEM((2 * SLAB, e), BF16),
                        pltpu.VMEM((N_GROUPS, SLAB, e), F32), pltpu.VMEM((N_GROUPS, SLAB, e), F32),
                        pltpu.VMEM((N_GROUPS, SLAB, e), F32)]
        + [pltpu.VMEM((SLAB, e), F32)] * 4,
        compiler_params=_params(3),
    )(hat, hat, hat, hat, hat)


def _attn_bwd(qkv, hat, d_out, out, lse, q_norm, k_norm, name):
    s_len = qkv.shape[0]
    e = HEAD_DIM
    n_slabs = s_len // SLAB

    def body(q_ref, k_ref, v_ref, kp_ref, vp_ref, qraw_ref, kraw_ref, do_ref, o_ref, lse_ref, qn_ref, kn_ref,
             dq_ref, dk_ref, dv_ref, st_ref, kbuf, vbuf, stat_s, dqs, dkb, dvb, dk_tok, dv_tok, carry,
             do_p, stat_p, dq_p, dk_p, dv_p, tmp_s, do16_p):
        head, step, grp = pl.program_id(0), pl.program_id(1), pl.program_id(2)
        n = n_slabs - 1 - step
        dkb[...] = jnp.zeros_like(dkb)
        dvb[...] = jnp.zeros_like(dvb)
        @pl.when(grp == 0)
        def _():
            lane = lax.broadcasted_iota(jnp.int32, (SLAB, e), 1)
            stat_s[...] = jnp.where(lane < e // 2, lse_ref[...],
                                    jnp.sum(do_ref[...] * o_ref[...].astype(F32), axis=-1, keepdims=True))

        @pl.when((head == 0) & (step == 0) & (grp == 0))
        def _():
            st_ref[...] = jnp.zeros_like(st_ref)

        def run(gi, dil):
            seg = SLAB // dil
            _stage_keys(k_ref, v_ref, kp_ref, vp_ref, kbuf, vbuf, dil, n)

            @pl.when(step == 0)
            def _():
                carry[gi] = jnp.zeros((2, SLAB, e), F32)

            _to_residue_order(do_p, do_ref, dil, tmp_s)
            do16_p[...] = do_p[...].astype(BF16)
            _to_residue_order(stat_p, stat_s, dil, tmp_s)

            def scores(q_rows, kv_rows, bias):
                return _dot_nt(q_ref[q_rows, :], kbuf[kv_rows, :]), _dot_nt(do16_p[q_rows, :], vbuf[kv_rows, :])

            def rest(q_rows, kv_rows, bias, begun):
                qk, dp = begun
                q = q_ref[q_rows, :]
                k = kbuf[kv_rows, :]
                stat = stat_p[q_rows, :]
                p = jnp.exp(qk * ATTN_SCALE + bias - stat[:, 0:1])
                ds = (p * (dp - stat[:, e // 2:e // 2 + 1]) * ATTN_SCALE).astype(BF16)
                dq_p[q_rows, :] = _dot(ds, k)
                dkb[kv_rows, :] += _dot_tn(ds, q)
                dvb[kv_rows, :] += _dot_tn(p.astype(BF16), do16_p[q_rows, :])

            _for_each_tile(dil, n, scores, rest)
            for r in range(dil):
                own, before = pl.ds((2 * r + 1) * seg, seg), pl.ds(2 * r * seg, seg)
                kept = pl.ds(r * seg, seg)
                dk_p[kept, :] = dkb[own, :] + carry.at[gi, 0][kept, :]
                dv_p[kept, :] = dvb[own, :] + carry.at[gi, 1][kept, :]
                carry.at[gi, 0][kept, :] = dkb[before, :]
                carry.at[gi, 1][kept, :] = dvb[before, :]
            _to_token_order(dqs, dq_p, dil, tmp_s)
            _to_token_order(dk_tok, dk_p, dil, tmp_s)
            _to_token_order(dv_tok, dv_p, dil, tmp_s)

            def norm_bwd(raw, gain, d_hat):
                r = _rms(raw)
                y = raw * r
                dy = d_hat * gain
                return r * (dy - y * jnp.mean(dy * y, axis=-1, keepdims=True)), jnp.sum(d_hat * y, axis=0, keepdims=True)

            dq, dqn = norm_bwd(qraw_ref[...], qn_ref[...], dqs[...])
            dk, dkn = norm_bwd(kraw_ref[...], kn_ref[...], dk_tok[...])
            dq_ref[...] = dq.astype(BF16)
            dk_ref[...] = dk.astype(BF16)
            dv_ref[...] = dv_tok[...].astype(BF16)
            st_ref[0:1, :] += dqn
            st_ref[1:2, :] += dkn

        for gi, dil in enumerate(DILATIONS):
            pl.when(grp == gi)(lambda gi=gi, dil=dil: run(gi, dil))

    slab_of = lambda s: n_slabs - 1 - s
    small = pl.BlockSpec((1, e), lambda h, s, g: (0, 0))
    head_blk = pl.BlockSpec((SLAB, e), lambda h, s, g: (slab_of(s), h))
    grad_blk = pl.BlockSpec((SLAB, e), lambda h, s, g: (slab_of(s), g * HEADS + h))
    grad_shape = jax.ShapeDtypeStruct((s_len, QKV), BF16)
    return pl.pallas_call(
        body, name=name, grid=(HEADS, n_slabs, N_GROUPS),
        in_specs=(_qkv_specs(slab_of, HAT_BLOCKS) + _qkv_specs(slab_of, [(0, 0), (1, 0)])
                  + [head_blk, head_blk, head_blk, small, small]),
        out_specs=[grad_blk, grad_blk, grad_blk, pl.BlockSpec((8, e), lambda h, s, g: (0, 0))],
        out_shape=[grad_shape, grad_shape, grad_shape, jax.ShapeDtypeStruct((8, e), F32)],
        scratch_shapes=[pltpu.VMEM((2 * SLAB, e), BF16), pltpu.VMEM((2 * SLAB, e), BF16), pltpu.VMEM((SLAB, e), F32),
                        pltpu.VMEM((SLAB, e), F32), pltpu.VMEM((2 * SLAB, e), F32), pltpu.VMEM((2 * SLAB, e), F32),
                        pltpu.VMEM((SLAB, e), F32), pltpu.VMEM((SLAB, e), F32),
                        pltpu.VMEM((N_GROUPS, 2, SLAB, e), F32)]
        + [pltpu.VMEM((SLAB, e), F32)] * 6 + [pltpu.VMEM((SLAB, e), BF16)],
        compiler_params=_params(3),
    )(hat, hat, hat, hat, hat, qkv, qkv, d_out, out, lse, q_norm, k_norm)


def _shift_rows(x, by, edge, forward):
    t_len = x.shape[0]
    row = lax.broadcasted_iota(jnp.int32, x.shape, 0)
    if forward:
        out = pltpu.roll(x, by, 0)
        for i in range(by):
            out = jnp.where(row == i, edge[8 - by + i:8 - by + i + 1, :], out)
    else:
        out = pltpu.roll(x, t_len - by, 0)
        for i in range(by):
            out = jnp.where(row == t_len - by + i, edge[i:i + 1, :], out)
    return out


def _mix_fwd(x, o, rest, mod, mod_next, conv_w, w_attn, w_conv, w_out, name):
    s_len, d = x.shape
    tm = MIX_TILE
    a_w = o.shape[1]

    def body(x_ref, o_ref, u_ref, b_ref, c_ref, ga_ref, gc_ref, mod_ref, modn_ref, cw_ref, wa_ref, wc_ref, wo_ref,
             xo_ref, z_ref, ya_ref, yc_ref, conv_ref, yb_ref, m_ref, h_ref, ht_ref, carry):
        @pl.when(pl.program_id(0) == 0)
        def _():
            carry[...] = jnp.zeros_like(carry)

        xc = c_ref[...].astype(F32) * u_ref[...].astype(F32)
        edge = carry[...]
        conv = (_shift_rows(xc, 2, edge, True) * cw_ref[0:1, :] + _shift_rows(xc, 1, edge, True) * cw_ref[1:2, :]
                + xc * cw_ref[2:3, :])
        carry[...] = xc[tm - 8:tm, :]
        yb = (b_ref[...].astype(F32) * conv).astype(BF16)
        ya = _dot(o_ref[...], wa_ref[...])
        yc = _dot(yb, wc_ref[...])
        merged = (_sigmoid(ga_ref[...].astype(F32)) * ya + _sigmoid(gc_ref[...].astype(F32)) * yc).astype(BF16)
        z = _dot(merged, wo_ref[...])
        xo = x_ref[...] + mod_ref[2:3, :] * z
        xo_ref[...] = xo
        hn = ((xo * _rms(xo)) * modn_ref[3:4, :]) * (1.0 + modn_ref[1:2, :]) + modn_ref[0:1, :]
        h_ref[...] = hn.astype(BF16)
        ht_ref[...] = hn.T.astype(BF16)
        z_ref[...] = z.astype(BF16)
        ya_ref[...] = ya.astype(BF16)
        yc_ref[...] = yc.astype(BF16)
        conv_ref[...] = conv.astype(BF16)
        yb_ref[...] = yb
        m_ref[...] = merged

    tile = pl.BlockSpec((tm, d), lambda i: (i, 0))
    sect = lambda k: pl.BlockSpec((tm, d), lambda i: (i, k))
    att = pl.BlockSpec((tm, a_w), lambda i: (i, 0))
    const = lambda shape: pl.BlockSpec(shape, lambda i: (0, 0))
    f32_out = jax.ShapeDtypeStruct((s_len, d), F32)
    b16_out = jax.ShapeDtypeStruct((s_len, d), BF16)
    return pl.pallas_call(
        body, name=name, grid=(s_len // tm,),
        in_specs=[tile, att, sect(0), sect(1), sect(2), sect(3), sect(4), const((8, d)), const((8, d)), const((8, d)),
                  const((a_w, d)), const((d, d)), const((d, d))],
        out_specs=[tile] * 7 + [tile, pl.BlockSpec((d, tm), lambda i: (0, i))],
        out_shape=[f32_out] + [b16_out] * 6 + [b16_out, jax.ShapeDtypeStruct((d, s_len), BF16)],
        scratch_shapes=[pltpu.VMEM((8, d), F32)],
        compiler_params=_params(1),
    )(x, o, rest, rest, rest, rest, rest, mod, mod_next, conv_w, w_attn, w_conv, w_out)


def _mix_bwd(dxo, ya, yc, conv, rest, mod, conv_w, w_attn, w_conv, w_out, a_w, name, after=()):
    s_len, d = dxo.shape
    tm = MIX_TILE
    n_tiles = s_len // tm

    def body(dxo_ref, ya_ref, yc_ref, conv_ref, u_ref, b_ref, c_ref, ga_ref, gc_ref, mod_ref, cw_ref,
             wa_ref, wc_ref, wo_ref, do_ref, drest_ref, dz_ref, dya_ref, dyc_ref, st_ref, carry):
        @pl.when(pl.program_id(0) == 0)
        def _():
            carry[...] = jnp.zeros_like(carry)
            st_ref[...] = jnp.zeros_like(st_ref)

        dz = (mod_ref[2:3, :] * dxo_ref[...]).astype(BF16)
        dz_ref[...] = dz
        dm = _dot_nt(dz, wo_ref[...])
        sa, sc = _sigmoid(ga_ref[...].astype(F32)), _sigmoid(gc_ref[...].astype(F32))
        dya = (dm * sa).astype(BF16)
        dyc = (dm * sc).astype(BF16)
        dya_ref[...] = dya
        dyc_ref[...] = dyc
        drest_ref[:, 3 * d:4 * d] = (dm * ya_ref[...].astype(F32) * (sa * (1.0 - sa))).astype(BF16)
        drest_ref[:, 4 * d:5 * d] = (dm * yc_ref[...].astype(F32) * (sc * (1.0 - sc))).astype(BF16)
        do_ref[...] = _dot_nt(dya, wa_ref[...])
        dyb = _dot_nt(dyc, wc_ref[...])
        drest_ref[:, d:2 * d] = (dyb * conv_ref[...].astype(F32)).astype(BF16)
        dconv = dyb * b_ref[...].astype(F32)
        edge = carry[...]
        sh1 = _shift_rows(dconv, 1, edge, False)
        sh2 = _shift_rows(dconv, 2, edge, False)
        carry[...] = dconv[0:8, :]
        dxc = dconv * cw_ref[2:3, :] + sh1 * cw_ref[1:2, :] + sh2 * cw_ref[0:1, :]
        u, c = u_ref[...].astype(F32), c_ref[...].astype(F32)
        xc = c * u
        drest_ref[:, 0:d] = (dxc * c).astype(BF16)
        drest_ref[:, 2 * d:3 * d] = (dxc * u).astype(BF16)
        st_ref[0:1, :] += jnp.sum(xc * sh2, axis=0, keepdims=True)
        st_ref[1:2, :] += jnp.sum(xc * sh1, axis=0, keepdims=True)
        st_ref[2:3, :] += jnp.sum(xc * dconv, axis=0, keepdims=True)

    rev = lambda i: n_tiles - 1 - i
    tile = pl.BlockSpec((tm, d), lambda i: (rev(i), 0))
    sect = lambda k: pl.BlockSpec((tm, d), lambda i: (rev(i), k))
    const = lambda shape: pl.BlockSpec(shape, lambda i: (0, 0))
    b16_out = jax.ShapeDtypeStruct((s_len, d), BF16)
    return pl.pallas_call(
        _ordered(body, 14, after), name=name, grid=(n_tiles,),
        in_specs=[tile, tile, tile, tile, sect(0), sect(1), sect(2), sect(3), sect(4), const((8, d)), const((8, d)),
                  const((a_w, d)), const((d, d)), const((d, d))] + [ANY] * len(after),
        out_specs=[pl.BlockSpec((tm, a_w), lambda i: (rev(i), 0)), pl.BlockSpec((tm, 5 * d), lambda i: (rev(i), 0)),
                   tile, tile, tile, const((8, d))],
        out_shape=[jax.ShapeDtypeStruct((s_len, a_w), F32), jax.ShapeDtypeStruct((s_len, 5 * d), BF16),
                   b16_out, b16_out, b16_out, jax.ShapeDtypeStruct((8, d), F32)],
        scratch_shapes=[pltpu.VMEM((8, d), F32)],
        compiler_params=_params(1),
    )(dxo, ya, yc, conv, rest, rest, rest, rest, rest, mod, conv_w, w_attn, w_conv, w_out, *after)


ADA_COLS = 128


def _ada_fwd(c_all, w_shard, b_shard, name):
    d, cols = w_shard.shape

    def body(c_ref, w_ref, b_ref, o_ref):
        cv = c_ref[...]
        o_ref[...] = jnp.dot(cv * _sigmoid(cv), w_ref[...], preferred_element_type=F32,
                             precision=lax.Precision.HIGHEST) + b_ref[...]

    return pl.pallas_call(
        body, name=name, grid=(cols // ADA_COLS,),
        in_specs=[pl.BlockSpec((8, d), lambda j: (0, 0)), pl.BlockSpec((d, ADA_COLS), lambda j: (0, j)),
                  pl.BlockSpec((1, ADA_COLS), lambda j: (0, j))],
        out_specs=pl.BlockSpec((8, ADA_COLS), lambda j: (0, j)),
        out_shape=jax.ShapeDtypeStruct((8, cols), F32),
        compiler_params=_params(1),
    )(c_all, w_shard, b_shard)


def _ada_bwd(c_all, dmod_shard, w, m, v, name):
    d, cols = w.shape

    def body(c_ref, dm_ref, w_ref, m_ref, v_ref, g_ref, d_ref, nm_ref, nv_ref):
        cv = c_ref[...]
        g = lax.dot_general(cv * _sigmoid(cv), dm_ref[...], (((0,), (0,)), ((), ())),
                            preferred_element_type=F32, precision=lax.Precision.HIGHEST)
        g_ref[...] = g
        d_ref[...], nm_ref[...], nv_ref[...] = _adamw_math(w_ref[...], g, m_ref[...], v_ref[...])

    blk = pl.BlockSpec((d, ADA_COLS), lambda j: (0, j))
    shape = jax.ShapeDtypeStruct((d, cols), F32)
    return pl.pallas_call(
        body, name=name, grid=(cols // ADA_COLS,),
        in_specs=[pl.BlockSpec((8, d), lambda j: (0, 0)), pl.BlockSpec((8, ADA_COLS), lambda j: (0, j)), blk, blk, blk],
        out_specs=[blk] * 4, out_shape=[shape] * 4,
        compiler_params=_params(1),
    )(c_all, dmod_shard, w, m, v)


def _small_update(parts, w, m, v, name):
    n = w.shape[1]

    def body(p_ref, w_ref, m_ref, v_ref, g_ref, d_ref, nm_ref, nv_ref):
        g = p_ref[0:1, :]
        for i in range(1, 8):
            g = g + p_ref[i:i + 1, :]
        g_ref[...] = g
        d_ref[...], nm_ref[...], nv_ref[...] = _adamw_math(w_ref[...], g, m_ref[...], v_ref[...])

    shape = jax.ShapeDtypeStruct((1, n), F32)
    return pl.pallas_call(body, name=name, out_shape=[shape] * 4, compiler_params=_params())(parts, w, m, v)


def _cols_to_shards(w, n):
    r, nc = w.shape
    return w.reshape(r, n, nc // n).transpose(1, 0, 2)


def kernel(x, c, w_ada, b_ada, norm_ffn1, ffn1_w_gate, ffn1_w_up, ffn1_w_down, norm_mix, w_in, q_norm, k_norm, conv_w, w_attn_branch, w_conv_branch, w_out, norm_ffn2, ffn2_w_gate, ffn2_w_up, ffn2_w_down, loss_target, m_w_ada, m_b_ada, m_norm_ffn1, m_ffn1_w_gate, m_ffn1_w_up, m_ffn1_w_down, m_norm_mix, m_w_in, m_q_norm, m_k_norm, m_conv_w, m_w_attn_branch, m_w_conv_branch, m_w_out, m_norm_ffn2, m_ffn2_w_gate, m_ffn2_w_up, m_ffn2_w_down, v_w_ada, v_b_ada, v_norm_ffn1, v_ffn1_w_gate, v_ffn1_w_up, v_ffn1_w_down, v_norm_mix, v_w_in, v_q_norm, v_k_norm, v_conv_w, v_w_attn_branch, v_w_conv_branch, v_w_out, v_norm_ffn2, v_ffn2_w_gate, v_ffn2_w_up, v_ffn2_w_down):
    ix, iy, ic = _place()
    chip = 2 * ix + iy
    me = 4 * ix + 2 * iy + ic
    xs = x[0]
    target = loss_target[0]
    s_len, d = xs.shape
    ada_cols = w_ada.shape[2]
    conv_cols = conv_w.shape[2]

    conv_rows = jnp.zeros((8, conv_cols), F32).at[0:3].set(conv_w[0])
    small_in = jnp.concatenate([jnp.broadcast_to(c, (8, d)), conv_rows], axis=1)
    small_all = _allgather8(small_in, "gather_c").reshape(8, 8, d + conv_cols)
    c_all = small_all[:, 0, :d]
    conv_full = small_all[0::2, 0:3, d:].transpose(1, 0, 2).reshape(3, N_CHIPS * conv_cols)
    conv_pad = jnp.zeros((8, N_CHIPS * conv_cols), F32).at[0:3].set(conv_full)
    b_shard = lax.dynamic_slice(b_ada, (0, chip * ada_cols), (1, ada_cols))
    mod_part = _ada_fwd(c_all, w_ada[0], b_shard, "ada_fwd")
    mod_all = _allgather8(mod_part, "gather_mod").reshape(N_CHIPS, 2, 8, ada_cols)[:, 0]
    mod_mine = lax.dynamic_slice(mod_all, (0, me, 0), (N_CHIPS, 1, ada_cols)).reshape(9, d)

    def mod_rows(i, gain):
        return jnp.zeros((8, d), F32).at[0:3].set(mod_mine[3 * i:3 * i + 3]).at[3:4].set(gain)

    mod1, mod2, mod3 = mod_rows(0, norm_ffn1), mod_rows(1, norm_mix), mod_rows(2, norm_ffn2)

    to16 = lambda w: w[0].astype(BF16)
    wg1, wu1, wd1 = _gather_weights([to16(ffn1_w_gate), to16(ffn1_w_up), to16(ffn1_w_down)], [False] * 3,
                                    "gather_ffn1", 1)
    h1, h1t = _norm_mod(xs, mod1, "norm1")
    (w_in_full,) = _gather_weights([to16(w_in)], [True], "gather_w_in", 2, after=(wd1, h1))

    g1, u1, y1 = _ffn_fwd(h1, wg1, wu1, wd1, "ffn1_fwd")
    x1, h2, h2t = _norm_mod(xs, mod2, "norm2", prev=(y1, mod1, 0.5))
    qkv, rest, qkv_hat = _in_proj(h2, w_in_full, q_norm, k_norm, "in_proj")
    w_ab, w_cb_g, w_o_g, wg2, wu2, wd2 = _gather_weights(
        [to16(w_attn_branch), to16(w_conv_branch), to16(w_out),
         to16(ffn2_w_gate), to16(ffn2_w_up), to16(ffn2_w_down)], [True] + [False] * 5,
        "gather_rest", 3, after=(h2,))
    a_w = w_ab.shape[0]
    w_cb = w_cb_g.reshape(d, d)
    w_o = w_o_g.reshape(d, d)
    o, lse = _attn_fwd(qkv_hat, "attn_fwd")
    x2, z, ya, yc, conv, yb, merged, h3, h3t = _mix_fwd(x1, o, rest, mod2, mod3, conv_pad, w_ab, w_cb, w_o, "mix_fwd")
    g3, u3, y3 = _ffn_fwd(h3, wg2, wu2, wd2, "ffn2_fwd")
    dy3, loss_part = _loss_grad(x2, y3, mod3, target, "loss")

    c_idx = jnp.reshape(ic, (1,)).astype(jnp.int32)
    chip_idx = jnp.stack([chip, ic]).astype(jnp.int32)

    def pair_send(grads, tag, collective_id):
        return _rs_pair_exchange(grads, "rs_pair_" + tag, collective_id)

    def chip_send(grads, from_sibling, names, tag, collective_id, after):
        pair_sums = [_pair_add(g, r, c_idx, "pair_add_" + nm, after) for g, r, nm in zip(grads, from_sibling, names)]
        return pair_sums, _rs_chip_exchange(pair_sums, "rs_chips_" + tag, collective_id)

    def reduce_finish(pair_sums, from_chips, names, tag, after):
        totals = [_chip_add(p, r, chip_idx, "chip_add_" + nm, after)
                  for p, r, nm in zip(pair_sums, from_chips, names)]
        return dict(zip(names, _rs_share(totals, "rs_share_" + tag)))

    names_a = ["ffn2_w_gate", "ffn2_w_up", "ffn2_w_down"]
    names_b = ["w_in", "w_attn_branch", "w_conv_branch", "w_out"]
    names_c = ["ffn1_w_gate", "ffn1_w_up", "ffn1_w_down"]

    dh3, dg3, du3, a3 = _ffn_bwd(dy3, g3, u3, wg2, wu2, wd2, "ffn2_bwd")
    grads_a = list(_ffn_wgrads(h3t, dg3, du3, a3, dy3, "ffn2"))
    sibling_a = pair_send(grads_a, "a", 7)
    dx2, st3 = _norm_bwd(dh3, x2, mod3, target, y3, 0.5, "norm3_bwd", dxo_is_target=True)
    sums_a, chips_a = chip_send(grads_a, sibling_a, names_a, "a", 4, after=(dx2,))

    do, drest, dz, dya, dyc, st_conv = _mix_bwd(dx2, ya, yc, conv, rest, mod2, conv_pad, w_ab, w_cb, w_o, a_w,
                                                "mix_bwd", after=tuple(sums_a))
    dq, dk, dv, st_qk = _attn_bwd(qkv, qkv_hat, do, o, lse, q_norm, k_norm, "attn_bwd")
    tok = lambda width: (lambda ts: pl.BlockSpec((ts, width), lambda cc, s: (s, 0)))
    colblk = lambda width: (lambda ts: pl.BlockSpec((ts, width), lambda cc, s: (s, cc)))
    tok_t = lambda ts: pl.BlockSpec((d, ts), lambda cc, s: (0, s))
    whole = pl.BlockSpec((d, QKV), lambda cc, s: (0, 0))
    dw_in = [_wgrad(h2t, part, tok_t, tok(QKV), (d, QKV), whole, (d, QKV), 1, "dw_in_" + nm, True)
             for part, nm in ((dq, "q"), (dk, "k"), (dv, "v"))]
    dw_in.append(_wgrad(h2t, drest, tok_t, colblk(d), (d, 5 * d), pl.BlockSpec((d, d), lambda cc, s: (0, cc)),
                        (d, d), 5, "dw_in_rest", True))
    dw_in = _cols_to_shards(jnp.concatenate(dw_in, axis=1), N_CHIPS)
    shard_w = d // N_CHIPS
    dw_ab = _wgrad(o, dya, tok(a_w), colblk(shard_w), (a_w, d), pl.BlockSpec((a_w, shard_w), lambda cc, s: (0, cc)),
                   (a_w, shard_w), N_CHIPS, "dw_attn_branch")
    dw_ab = _cols_to_shards(dw_ab, N_CHIPS)
    row_out = pl.BlockSpec((None, shard_w, d), lambda cc, s: (cc, 0, 0))
    dw_cb = _wgrad(yb, dyc, colblk(shard_w), tok(d), (N_CHIPS, shard_w, d), row_out, (shard_w, d), N_CHIPS, "dw_conv_branch")
    dw_o = _wgrad(merged, dz, colblk(shard_w), tok(d), (N_CHIPS, shard_w, d), row_out, (shard_w, d), N_CHIPS, "dw_out")
    shard_grads = reduce_finish(sums_a, chips_a, names_a, "a", after=(dw_in, dw_o))
    grads_b = [dw_in, dw_ab, dw_cb, dw_o]
    sibling_b = pair_send(grads_b, "b", 8)

    dh2 = _in_proj_bwd(dq, dk, dv, drest, w_in_full, "in_proj_bwd")
    sums_b, chips_b = chip_send(grads_b, sibling_b, names_b, "b", 5, after=(dh2,))
    dx1, st2, dy1 = _norm_bwd(dh2, x1, mod2, dx2, z, 1.0, "norm2_bwd", after=tuple(sums_b), prev=(mod1, 0.5))
    dh1, dg1, du1, a1 = _ffn_bwd(dy1, g1, u1, wg1, wu1, wd1, "ffn1_bwd")
    dx0, st1 = _norm_bwd(dh1, xs, mod1, dx1, y1, 0.5, "norm1_bwd")
    grads_c = list(_ffn_wgrads(h1t, dg1, du1, a1, dy1, "ffn1"))
    sibling_c = pair_send(grads_c, "c", 9)
    shard_grads.update(reduce_finish(sums_b, chips_b, names_b, "b", after=tuple(grads_c)))

    dmod = jnp.concatenate([st1[0:3], st2[0:3], st3[0:3]], axis=0).reshape(1, 9 * d)
    loss_cols = jnp.zeros((1, HEAD_DIM), F32).at[0, 0].set(jnp.sum(loss_part))
    small = jnp.concatenate([dmod, st1[3:4], st2[3:4], st3[3:4], st_qk[0:1], st_qk[1:2],
                             st_conv[0:3].reshape(1, 3 * d), loss_cols], axis=1)
    n_small = small.shape[1]
    fold = -(-n_small // (8 * LANE_TILE)) * LANE_TILE
    folded = jnp.pad(small, ((0, 0), (0, 8 * fold - n_small))).reshape(8, fold)
    small_all = _allgather8(folded, "gather_small").reshape(8, 8 * fold)[:, :n_small]
    loss = (0.5 / d) * jnp.sum(small_all[:, -HEAD_DIM])
    small_all = small_all[:, :-HEAD_DIM]
    dmod_all = small_all[:, :9 * d]
    dmod_shard = lax.dynamic_slice(dmod_all, (0, chip * ada_cols), (8, ada_cols))
    g_w_ada, d_w_ada, nm_w_ada, nv_w_ada = _ada_bwd(c_all, dmod_shard, w_ada[0], m_w_ada[0], v_w_ada[0], "ada_bwd")

    vec_names = ["b_ada", "norm_ffn1", "norm_mix", "norm_ffn2", "q_norm", "k_norm"]
    vec_w = [b_ada, norm_ffn1, norm_mix, norm_ffn2, q_norm, k_norm]
    vec_m = [m_b_ada, m_norm_ffn1, m_norm_mix, m_norm_ffn2, m_q_norm, m_k_norm]
    vec_v = [v_b_ada, v_norm_ffn1, v_norm_mix, v_norm_ffn2, v_q_norm, v_k_norm]
    n_vec = sum(w.shape[1] for w in vec_w)
    cat = lambda arrs: jnp.concatenate(arrs, axis=1)
    vec_out = _small_update(small_all[:, :n_vec], cat(vec_w), cat(vec_m), cat(vec_v), "small_update")
    conv_parts = small_all[:, n_vec:].reshape(8, 3, N_CHIPS * conv_cols)
    conv_parts = lax.dynamic_slice(conv_parts, (0, 0, chip * conv_cols), (8, 3, conv_cols)).reshape(8, 3 * conv_cols)
    flat3 = lambda w: w[0].reshape(1, 3 * conv_cols)
    conv_out = _small_update(conv_parts, flat3(conv_w), flat3(m_conv_w), flat3(v_conv_w), "conv_update")

    res = {"w_ada": [t[None] for t in (g_w_ada, d_w_ada, nm_w_ada, nv_w_ada)],
           "conv_w": [t.reshape(1, 3, conv_cols) for t in conv_out]}
    off = 0
    for nm, w in zip(vec_names, vec_w):
        width = w.shape[1]
        res[nm] = [t[:, off:off + width] for t in vec_out]
        off += width
    big = {"ffn1_w_gate": (ffn1_w_gate, m_ffn1_w_gate, v_ffn1_w_gate), "ffn1_w_up": (ffn1_w_up, m_ffn1_w_up, v_ffn1_w_up),
           "ffn1_w_down": (ffn1_w_down, m_ffn1_w_down, v_ffn1_w_down), "w_in": (w_in, m_w_in, v_w_in),
           "w_attn_branch": (w_attn_branch, m_w_attn_branch, v_w_attn_branch),
           "w_conv_branch": (w_conv_branch, m_w_conv_branch, v_w_conv_branch), "w_out": (w_out, m_w_out, v_w_out),
           "ffn2_w_gate": (ffn2_w_gate, m_ffn2_w_gate, v_ffn2_w_gate), "ffn2_w_up": (ffn2_w_up, m_ffn2_w_up, v_ffn2_w_up),
           "ffn2_w_down": (ffn2_w_down, m_ffn2_w_down, v_ffn2_w_down)}
    def update(nm, after=()):
        w, m, v = big[nm]
        flip = w.shape[2] % LANE_TILE != 0 and w.shape[1] % LANE_TILE == 0
        turn = (lambda a: a.T) if flip else (lambda a: a)
        outs = _adamw(turn(w[0]), turn(shard_grads[nm]), turn(m[0]), turn(v[0]), "adamw_" + nm, after)
        res[nm] = [turn(t)[None] for t in outs]
        return outs[3]

    last = tuple(shard_grads[nm] for nm in names_b)
    for nm in names_a:
        last = (update(nm, last),)
    sums_c, chips_c = chip_send(grads_c, sibling_c, names_c, "c", 6, after=last)
    last = tuple(sums_c)
    for nm in names_b:
        last = (update(nm, last),)
    shard_grads.update(reduce_finish(sums_c, chips_c, names_c, "c", after=last))
    for nm in names_c:
        update(nm)

    order = ["w_ada", "b_ada", "norm_ffn1", "ffn1_w_gate", "ffn1_w_up", "ffn1_w_down", "norm_mix", "w_in", "q_norm",
             "k_norm", "conv_w", "w_attn_branch", "w_conv_branch", "w_out", "norm_ffn2", "ffn2_w_gate", "ffn2_w_up",
             "ffn2_w_down"]
    return (loss, dx0[None], *[res[nm][0] for nm in order], *[res[nm][1] for nm in order],
            *[res[nm][2] for nm in order], *[res[nm][3] for nm in order])
```
